```python
import jax, jax.numpy as jnp
from jax import lax
import numpy as np

D_MODEL = 1024
BATCH = 8
SEQ = 2048
DEPTH = 1

CHUNK = 64
GLA_HEADS = 4
GLA_DK = 64
GLA_DV = 128
GLA_QK_W = GLA_HEADS * GLA_DK
GLA_V_W = GLA_HEADS * GLA_DV
GLA_LOWRANK = 16
GLA_GATE_NORM = 16.0
ML_HEADS = 4
ML_DH = 128
ML_W = ML_HEADS * ML_DH
ML_QKV_BLOCK = 4
CONV_K = 4
D_FF = 4 * D_MODEL
EPS = 1e-6
IN_SPLITS = (GLA_QK_W, GLA_QK_W, GLA_V_W, GLA_V_W, GLA_LOWRANK, ML_W, ML_W, D_MODEL, D_MODEL)
D_IN = sum(IN_SPLITS)

kernel_name = "hybrid_gla_mlstm_sandwich_block"


def _rmsnorm(x, g):
    xf = x.astype(jnp.float32)
    y = xf * lax.rsqrt(jnp.mean(xf * xf, axis=-1, keepdims=True) + EPS)
    return (y * g.astype(jnp.float32)).astype(x.dtype)


def _headnorm(x, g):
    xh = x.reshape(*x.shape[:-1], ML_HEADS, ML_DH)
    mu = jnp.mean(xh, axis=-1, keepdims=True)
    var = jnp.mean(jnp.square(xh - mu), axis=-1, keepdims=True)
    y = ((xh - mu) * lax.rsqrt(var + EPS)).reshape(x.shape)
    return y * g.astype(jnp.float32)


def _to_chunks(x, n_heads):
    b, t, w = x.shape
    return x.astype(jnp.float32).reshape(b, t // CHUNK, CHUNK, n_heads, w // n_heads).transpose(0, 3, 1, 2, 4)


def _from_chunks(x):
    b, h, nc, c, d = x.shape
    return x.transpose(0, 2, 3, 1, 4).reshape(b, nc * c, h * d)


def _causal_conv(x, w, bias):
    y = lax.conv_general_dilated(x, w[:, None, :].astype(x.dtype), window_strides=(1,),
                                 padding=[(CONV_K - 1, 0)],
                                 dimension_numbers=('NWC', 'WIO', 'NWC'),
                                 feature_group_count=x.shape[-1])
    return y + bias


def _blockdiag(x, w):
    b, t, n = x.shape
    xb = x.reshape(b, t, n // ML_QKV_BLOCK, ML_QKV_BLOCK)
    return jnp.einsum('btgj,gji->btgi', xb, w).reshape(b, t, n)


def _gla(q, k, v, g, a_low, w_a_up, b_a_up, g_norm):
    b = q.shape[0]
    log_a = jax.nn.log_sigmoid((a_low @ w_a_up + b_a_up).astype(jnp.float32)) / GLA_GATE_NORM
    qc = _to_chunks(q, GLA_HEADS) * GLA_DK ** -0.5
    kc = _to_chunks(k, GLA_HEADS)
    vc = _to_chunks(v, GLA_HEADS)
    cum = jnp.cumsum(_to_chunks(log_a, GLA_HEADS), axis=3)
    e_pos = jnp.exp(cum)
    e_neg = jnp.exp(-cum)
    a_fwd = jnp.einsum('bhntk,bhnsk->bhnts', qc * e_pos, kc * e_neg)
    a_bwd = jnp.einsum('bhntk,bhnsk->bhnts', qc * e_neg, kc * e_pos)
    idx = jnp.arange(CHUNK)
    scores = jnp.where(idx[:, None] >= idx[None, :], a_fwd, a_bwd)
    o = jnp.einsum('bhnts,bhnsv->bhntv', scores, vc)
    cum_last = cum[..., -1:, :]
    s_chunk = jnp.einsum('bhnsk,bhnsv->bhnkv', kc * jnp.exp(cum_last - cum), vc)
    decay = jnp.exp(cum_last[..., 0, :])

    def step(s, inp):
        d, sc = inp
        return d[..., None] * s + sc, s

    s0 = jnp.zeros((b, GLA_HEADS, GLA_DK, GLA_DV), jnp.float32)
    _, s_prev = lax.scan(step, s0, (jnp.moveaxis(decay, 2, 0), jnp.moveaxis(s_chunk, 2, 0)))
    o = o + jnp.einsum('bhntk,bhnkv->bhntv', qc * e_pos, jnp.moveaxis(s_prev, 0, 2))
    o = _rmsnorm(o, g_norm)
    return (_from_chunks(o) * jax.nn.silu(g.astype(jnp.float32))).astype(q.dtype)


def _mlstm(x_m, o_pre, conv_w, conv_b, w_q, w_k, w_v, w_if, b_if, skip, g_norm):
    b, t, _ = x_m.shape
    nc = t // CHUNK
    xc = jax.nn.silu(_causal_conv(x_m, conv_w, conv_b))
    q = _blockdiag(xc, w_q)
    k = _blockdiag(xc, w_k)
    v = _blockdiag(x_m, w_v)
    gates = (jnp.concatenate([q, k, v], axis=-1) @ w_if + b_if).astype(jnp.float32)
    gates = gates.reshape(b, nc, CHUNK, 2 * ML_HEADS).transpose(0, 3, 1, 2)
    log_i = gates[:, :ML_HEADS]
    log_f = jax.nn.log_sigmoid(gates[:, ML_HEADS:])
    f_cum = jnp.cumsum(log_f, axis=-1)
    qc = _to_chunks(q, ML_HEADS)
    kc = _to_chunks(k, ML_HEADS) * ML_DH ** -0.5
    vc = _to_chunks(v, ML_HEADS)
    f_last = f_cum[..., -1]
    a = f_last[..., None] - f_cum + log_i
    m_loc = jnp.max(a, axis=-1)
    kw = kc * jnp.exp(a - m_loc[..., None])[..., None]
    c_chunk = jnp.einsum('bhnsk,bhnsv->bhnkv', kw, vc)
    n_chunk = jnp.sum(kw, axis=3)

    def step(carry, inp):
        c, n, m = carry
        fl, ml, cc, ncnk = inp
        m_new = jnp.maximum(fl + m, ml)
        sp = jnp.exp(fl + m - m_new)
        sl = jnp.exp(ml - m_new)
        c_new = sp[..., None, None] * c + sl[..., None, None] * cc
        n_new = sp[..., None] * n + sl[..., None] * ncnk
        return (c_new, n_new, m_new), (c, n, m)

    init = (jnp.zeros((b, ML_HEADS, ML_DH, ML_DH), jnp.float32),
            jnp.zeros((b, ML_HEADS, ML_DH), jnp.float32),
            jnp.zeros((b, ML_HEADS), jnp.float32))
    xs = (jnp.moveaxis(f_last, 2, 0), jnp.moveaxis(m_loc, 2, 0),
          jnp.moveaxis(c_chunk, 2, 0), jnp.moveaxis(n_chunk, 2, 0))
    _, (c_prev, n_prev, m_prev) = lax.scan(step, init, xs)
    c_prev = jnp.moveaxis(c_prev, 0, 2)
    n_prev = jnp.moveaxis(n_prev, 0, 2)
    m_prev = jnp.moveaxis(m_prev, 0, 2)
    log_d = log_i[..., None, :] - jnp.abs(f_cum[..., :, None] - f_cum[..., None, :])
    g_inter = f_cum + m_prev[..., None]
    m_t = jnp.maximum(g_inter, jnp.max(log_d, axis=-1))
    s = jnp.einsum('bhntk,bhnsk->bhnts', qc, kc) * jnp.exp(log_d - m_t[..., None])
    sc = jnp.exp(g_inter - m_t)
    num = jnp.einsum('bhnts,bhnsv->bhntv', s, vc) + sc[..., None] * jnp.einsum('bhntk,bhnkv->bhntv', qc, c_prev)
    den = jnp.sum(s, axis=-1) + sc * jnp.einsum('bhntk,bhnk->bhnt', qc, n_prev)
    den = jnp.maximum(jnp.abs(den), jnp.exp(-m_t))
    h_cell = _from_chunks(num / den[..., None]) * jax.nn.sigmoid(o_pre.astype(jnp.float32))
    h = _headnorm(h_cell, g_norm) + skip.astype(jnp.float32) * xc.astype(jnp.float32)
    return h.astype(x_m.dtype)


def _fwd_setup_inputs(seed: int = 0) -> dict:
    key = jax.random.key(seed)
    ks = jax.random.split(key, 24)
    L = DEPTH

    def nrm(k, shape, scale):
        return jax.random.normal(k, shape, jnp.float32) * scale

    def gain(k, n):
        return 1.0 + nrm(k, (L, n), 0.02)

    bi = nrm(ks[12], (L, 2 * ML_HEADS), 0.1)
    b_if = jnp.concatenate([bi[:, :ML_HEADS], jnp.linspace(3.0, 6.0, ML_HEADS)[None, :] + bi[:, ML_HEADS:]], axis=-1)
    return {
        "x": nrm(ks[0], (BATCH, SEQ, D_MODEL), 1.0),
        "g_pre_mix": gain(ks[1], D_MODEL),
        "w_in": nrm(ks[2], (L, D_MODEL, D_IN), D_MODEL ** -0.5),
        "w_a_up": nrm(ks[3], (L, GLA_LOWRANK, GLA_QK_W), GLA_LOWRANK ** -0.5),
        "b_a_up": nrm(ks[4], (L, GLA_QK_W), 0.1),
        "g_gla_norm": gain(ks[5], GLA_DV),
        "conv_w": nrm(ks[6], (L, CONV_K, ML_W), CONV_K ** -0.5),
        "conv_b": nrm(ks[7], (L, ML_W), 0.02),
        "w_q_ml": nrm(ks[8], (L, ML_W // ML_QKV_BLOCK, ML_QKV_BLOCK, ML_QKV_BLOCK), ML_QKV_BLOCK ** -0.5),
        "w_k_ml": nrm(ks[9], (L, ML_W // ML_QKV_BLOCK, ML_QKV_BLOCK, ML_QKV_BLOCK), ML_QKV_BLOCK ** -0.5),
        "w_v_ml": nrm(ks[10], (L, ML_W // ML_QKV_BLOCK, ML_QKV_BLOCK, ML_QKV_BLOCK), ML_QKV_BLOCK ** -0.5),
        "w_if": nrm(ks[11], (L, 3 * ML_W, 2 * ML_HEADS), (3 * ML_W) ** -0.5),
        "b_if": b_if,
        "ml_skip": gain(ks[13], ML_W),
        "g_ml_norm": gain(ks[14], ML_W),
        "w_pa": nrm(ks[15], (L, GLA_V_W, D_MODEL), GLA_V_W ** -0.5),
        "w_pb": nrm(ks[16], (L, ML_W, D_MODEL), ML_W ** -0.5),
        "w_o": nrm(ks[17], (L, D_MODEL, D_MODEL), D_MODEL ** -0.5),
        "g_post_mix": gain(ks[18], D_MODEL),
        "g_pre_mlp": gain(ks[19], D_MODEL),
        "w_up": nrm(ks[20], (L, D_MODEL, D_FF), D_MODEL ** -0.5),
        "w_down": nrm(ks[21], (L, D_FF, D_MODEL), D_FF ** -0.5),
        "g_post_mlp": gain(ks[22], D_MODEL),
    }


def _fwd_reference(x, g_pre_mix, w_in, w_a_up, b_a_up, g_gla_norm, conv_w, conv_b, w_q_ml, w_k_ml, w_v_ml,
              w_if, b_if, ml_skip, g_ml_norm, w_pa, w_pb, w_o, g_post_mix, g_pre_mlp, w_up, w_down,
              g_post_mlp):
    split_at = [int(s) for s in np.cumsum(IN_SPLITS)[:-1]]
    for l in range(DEPTH):
        h = _rmsnorm(x, g_pre_mix[l])
        proj = h @ w_in[l]
        q_a, k_a, v_a, g_a, a_low, x_m, o_pre, gate_a, gate_b = jnp.split(proj, split_at, axis=-1)
        y_a = _gla(q_a, k_a, v_a, g_a, a_low, w_a_up[l], b_a_up[l], g_gla_norm[l]) @ w_pa[l]
        y_b = _mlstm(x_m, o_pre, conv_w[l], conv_b[l], w_q_ml[l], w_k_ml[l], w_v_ml[l],
                     w_if[l], b_if[l], ml_skip[l], g_ml_norm[l]) @ w_pb[l]
        merged = jax.nn.sigmoid(gate_a) * y_a + jax.nn.sigmoid(gate_b) * y_b
        x = x + _rmsnorm(merged @ w_o[l], g_post_mix[l])
        h2 = _rmsnorm(x, g_pre_mlp[l])
        u = jnp.square(jax.nn.relu(h2 @ w_up[l]))
        x = x + _rmsnorm(u @ w_down[l], g_post_mlp[l])
    return x


import jax as _jax
import jax.numpy as _jnp

TWIN_FORMAT = 'train_step'
FWD_PARAMS = ['x', 'g_pre_mix', 'w_in', 'w_a_up', 'b_a_up', 'g_gla_norm', 'conv_w', 'conv_b', 'w_q_ml', 'w_k_ml', 'w_v_ml', 'w_if', 'b_if', 'ml_skip', 'g_ml_norm', 'w_pa', 'w_pb', 'w_o', 'g_post_mix', 'g_pre_mlp', 'w_up', 'w_down', 'g_post_mlp']
TWIN_WEIGHTS = ['g_pre_mix', 'w_in', 'w_a_up', 'b_a_up', 'g_gla_norm', 'conv_w', 'conv_b', 'w_q_ml', 'w_k_ml', 'w_v_ml', 'w_if', 'b_if', 'ml_skip', 'g_ml_norm', 'w_pa', 'w_pb', 'w_o', 'g_post_mix', 'g_pre_mlp', 'w_up', 'w_down', 'g_post_mlp']
TWIN_DIFF_INPUT = 'x'
TWIN_INPUTS = ['x', 'g_pre_mix', 'w_in', 'w_a_up', 'b_a_up', 'g_gla_norm', 'conv_w', 'conv_b', 'w_q_ml', 'w_k_ml', 'w_v_ml', 'w_if', 'b_if', 'ml_skip', 'g_ml_norm', 'w_pa', 'w_pb', 'w_o', 'g_post_mix', 'g_pre_mlp', 'w_up', 'w_down', 'g_post_mlp', 'loss_target', 'm_g_pre_mix', 'm_w_in', 'm_w_a_up', 'm_b_a_up', 'm_g_gla_norm', 'm_conv_w', 'm_conv_b', 'm_w_q_ml', 'm_w_k_ml', 'm_w_v_ml', 'm_w_if', 'm_b_if', 'm_ml_skip', 'm_g_ml_norm', 'm_w_pa', 'm_w_pb', 'm_w_o', 'm_g_post_mix', 'm_g_pre_mlp', 'm_w_up', 'm_w_down', 'm_g_post_mlp', 'v_g_pre_mix', 'v_w_in', 'v_w_a_up', 'v_b_a_up', 'v_g_gla_norm', 'v_conv_w', 'v_conv_b', 'v_w_q_ml', 'v_w_k_ml', 'v_w_v_ml', 'v_w_if', 'v_b_if', 'v_ml_skip', 'v_g_ml_norm', 'v_w_pa', 'v_w_pb', 'v_w_o', 'v_g_post_mix', 'v_g_pre_mlp', 'v_w_up', 'v_w_down', 'v_g_post_mlp']
TWIN_OUTPUTS = ['loss', 'grad_x', 'grad_g_pre_mix', 'grad_w_in', 'grad_w_a_up', 'grad_b_a_up', 'grad_g_gla_norm', 'grad_conv_w', 'grad_conv_b', 'grad_w_q_ml', 'grad_w_k_ml', 'grad_w_v_ml', 'grad_w_if', 'grad_b_if', 'grad_ml_skip', 'grad_g_ml_norm', 'grad_w_pa', 'grad_w_pb', 'grad_w_o', 'grad_g_post_mix', 'grad_g_pre_mlp', 'grad_w_up', 'grad_w_down', 'grad_g_post_mlp', 'delta_g_pre_mix', 'delta_w_in', 'delta_w_a_up', 'delta_b_a_up', 'delta_g_gla_norm', 'delta_conv_w', 'delta_conv_b', 'delta_w_q_ml', 'delta_w_k_ml', 'delta_w_v_ml', 'delta_w_if', 'delta_b_if', 'delta_ml_skip', 'delta_g_ml_norm', 'delta_w_pa', 'delta_w_pb', 'delta_w_o', 'delta_g_post_mix', 'delta_g_pre_mlp', 'delta_w_up', 'delta_w_down', 'delta_g_post_mlp', 'new_m_g_pre_mix', 'new_m_w_in', 'new_m_w_a_up', 'new_m_b_a_up', 'new_m_g_gla_norm', 'new_m_conv_w', 'new_m_conv_b', 'new_m_w_q_ml', 'new_m_w_k_ml', 'new_m_w_v_ml', 'new_m_w_if', 'new_m_b_if', 'new_m_ml_skip', 'new_m_g_ml_norm', 'new_m_w_pa', 'new_m_w_pb', 'new_m_w_o', 'new_m_g_post_mix', 'new_m_g_pre_mlp', 'new_m_w_up', 'new_m_w_down', 'new_m_g_post_mlp', 'new_v_g_pre_mix', 'new_v_w_in', 'new_v_w_a_up', 'new_v_b_a_up', 'new_v_g_gla_norm', 'new_v_conv_w', 'new_v_conv_b', 'new_v_w_q_ml', 'new_v_w_k_ml', 'new_v_w_v_ml', 'new_v_w_if', 'new_v_b_if', 'new_v_ml_skip', 'new_v_g_ml_norm', 'new_v_w_pa', 'new_v_w_pb', 'new_v_w_o', 'new_v_g_post_mix', 'new_v_g_pre_mlp', 'new_v_w_up', 'new_v_w_down', 'new_v_g_post_mlp']
TWIN_LEAF_KINDS = {'loss': 'loss', 'grad_x': 'grad_x', 'grad_g_pre_mix': 'grad_w', 'grad_w_in': 'grad_w', 'grad_w_a_up': 'grad_w', 'grad_b_a_up': 'grad_w', 'grad_g_gla_norm': 'grad_w', 'grad_conv_w': 'grad_w', 'grad_conv_b': 'grad_w', 'grad_w_q_ml': 'grad_w', 'grad_w_k_ml': 'grad_w', 'grad_w_v_ml': 'grad_w', 'grad_w_if': 'grad_w', 'grad_b_if': 'grad_w', 'grad_ml_skip': 'grad_w', 'grad_g_ml_norm': 'grad_w', 'grad_w_pa': 'grad_w', 'grad_w_pb': 'grad_w', 'grad_w_o': 'grad_w', 'grad_g_post_mix': 'grad_w', 'grad_g_pre_mlp': 'grad_w', 'grad_w_up': 'grad_w', 'grad_w_down': 'grad_w', 'grad_g_post_mlp': 'grad_w', 'delta_g_pre_mix': 'delta_w', 'delta_w_in': 'delta_w', 'delta_w_a_up': 'delta_w', 'delta_b_a_up': 'delta_w', 'delta_g_gla_norm': 'delta_w', 'delta_conv_w': 'delta_w', 'delta_conv_b': 'delta_w', 'delta_w_q_ml': 'delta_w', 'delta_w_k_ml': 'delta_w', 'delta_w_v_ml': 'delta_w', 'delta_w_if': 'delta_w', 'delta_b_if': 'delta_w', 'delta_ml_skip': 'delta_w', 'delta_g_ml_norm': 'delta_w', 'delta_w_pa': 'delta_w', 'delta_w_pb': 'delta_w', 'delta_w_o': 'delta_w', 'delta_g_post_mix': 'delta_w', 'delta_g_pre_mlp': 'delta_w', 'delta_w_up': 'delta_w', 'delta_w_down': 'delta_w', 'delta_g_post_mlp': 'delta_w', 'new_m_g_pre_mix': 'new_m', 'new_m_w_in': 'new_m', 'new_m_w_a_up': 'new_m', 'new_m_b_a_up': 'new_m', 'new_m_g_gla_norm': 'new_m', 'new_m_conv_w': 'new_m', 'new_m_conv_b': 'new_m', 'new_m_w_q_ml': 'new_m', 'new_m_w_k_ml': 'new_m', 'new_m_w_v_ml': 'new_m', 'new_m_w_if': 'new_m', 'new_m_b_if': 'new_m', 'new_m_ml_skip': 'new_m', 'new_m_g_ml_norm': 'new_m', 'new_m_w_pa': 'new_m', 'new_m_w_pb': 'new_m', 'new_m_w_o': 'new_m', 'new_m_g_post_mix': 'new_m', 'new_m_g_pre_mlp': 'new_m', 'new_m_w_up': 'new_m', 'new_m_w_down': 'new_m', 'new_m_g_post_mlp': 'new_m', 'new_v_g_pre_mix': 'new_v', 'new_v_w_in': 'new_v', 'new_v_w_a_up': 'new_v', 'new_v_b_a_up': 'new_v', 'new_v_g_gla_norm': 'new_v', 'new_v_conv_w': 'new_v', 'new_v_conv_b': 'new_v', 'new_v_w_q_ml': 'new_v', 'new_v_w_k_ml': 'new_v', 'new_v_w_v_ml': 'new_v', 'new_v_w_if': 'new_v', 'new_v_b_if': 'new_v', 'new_v_ml_skip': 'new_v', 'new_v_g_ml_norm': 'new_v', 'new_v_w_pa': 'new_v', 'new_v_w_pb': 'new_v', 'new_v_w_o': 'new_v', 'new_v_g_post_mix': 'new_v', 'new_v_g_pre_mlp': 'new_v', 'new_v_w_up': 'new_v', 'new_v_w_down': 'new_v', 'new_v_g_post_mlp': 'new_v'}


def _forward(args):
    return _fwd_reference(*[args[k] for k in FWD_PARAMS])


def _output_shape():
    out = _jax.eval_shape(lambda: _forward(_fwd_setup_inputs(0)))
    return out.shape, out.dtype

N_MICROBATCH = 1
ADAM_LR = 0.001
ADAM_B1 = 0.9
ADAM_B2 = 0.999
ADAM_EPS = 1e-08
ADAM_WD = 0.01
ADAM_STEP = 10
PER_EXAMPLE_BATCH_AXIS = {'x': 0, 'loss_target': 0}
SHARED_INPUTS = []
_WEIGHT_DTYPES = {'g_pre_mix': _jnp.float32, 'w_in': _jnp.float32, 'w_a_up': _jnp.float32, 'b_a_up': _jnp.float32, 'g_gla_norm': _jnp.float32, 'conv_w': _jnp.float32, 'conv_b': _jnp.float32, 'w_q_ml': _jnp.float32, 'w_k_ml': _jnp.float32, 'w_v_ml': _jnp.float32, 'w_if': _jnp.float32, 'b_if': _jnp.float32, 'ml_skip': _jnp.float32, 'g_ml_norm': _jnp.float32, 'w_pa': _jnp.float32, 'w_pb': _jnp.float32, 'w_o': _jnp.float32, 'g_post_mix': _jnp.float32, 'g_pre_mlp': _jnp.float32, 'w_up': _jnp.float32, 'w_down': _jnp.float32, 'g_post_mlp': _jnp.float32}
MOMENT_SCALE = {'g_pre_mix': 6.443981e-01, 'w_in': 2.957974e-01, 'w_a_up': 2.706971e-02, 'b_a_up': 1.195863e-01, 'g_gla_norm': 3.797250e-01, 'conv_w': 1.381663e+00, 'conv_b': 3.602771e+00, 'w_q_ml': 8.028217e-01, 'w_k_ml': 8.348701e-01, 'w_v_ml': 7.909540e-01, 'w_if': 2.505301e+00, 'b_if': 2.845735e+00, 'ml_skip': 1.449109e+00, 'g_ml_norm': 7.717447e-01, 'w_pa': 1.382815e-01, 'w_pb': 1.206441e+00, 'w_o': 1.295852e+00, 'g_post_mix': 1.614633e+01, 'g_pre_mlp': 6.171582e-01, 'w_up': 2.996120e-01, 'w_down': 1.303421e+00, 'g_post_mlp': 1.654943e+01}


def _to_microbatches(a, axis):
    t = _jnp.moveaxis(a, axis, 0)
    t = t.reshape((N_MICROBATCH, t.shape[0] // N_MICROBATCH) + t.shape[1:])
    return _jnp.moveaxis(t, 1, axis + 1)


def setup_inputs(seed: int = 0) -> dict:
    inp = _fwd_setup_inputs(seed)
    key = _jax.random.fold_in(_jax.random.key(seed), 7919)
    shape, _ = _output_shape()
    out = dict(inp)
    out["loss_target"] = _jax.random.normal(_jax.random.fold_in(key, 0), shape, _jnp.float32)
    for i, name in enumerate(TWIN_WEIGHTS):
        w = inp[name].astype(_jnp.float32)
        if MOMENT_SCALE is None:
            s = _jnp.sqrt(_jnp.mean(_jnp.square(w)) + 1e-30)
        else:
            s = MOMENT_SCALE[name]
        km, kv = _jax.random.split(_jax.random.fold_in(key, i + 1))
        out[name] = w
        out["m_" + name] = s * _jax.random.normal(km, w.shape, _jnp.float32)
        out["v_" + name] = (s * s) * _jax.random.uniform(kv, w.shape, _jnp.float32, 0.5, 1.5)
    if N_MICROBATCH > 1:
        for name, axis in PER_EXAMPLE_BATCH_AXIS.items():
            out[name] = _to_microbatches(out[name], axis)
    return {'x': out['x'], 'g_pre_mix': out['g_pre_mix'], 'w_in': out['w_in'], 'w_a_up': out['w_a_up'], 'b_a_up': out['b_a_up'], 'g_gla_norm': out['g_gla_norm'], 'conv_w': out['conv_w'], 'conv_b': out['conv_b'], 'w_q_ml': out['w_q_ml'], 'w_k_ml': out['w_k_ml'], 'w_v_ml': out['w_v_ml'], 'w_if': out['w_if'], 'b_if': out['b_if'], 'ml_skip': out['ml_skip'], 'g_ml_norm': out['g_ml_norm'], 'w_pa': out['w_pa'], 'w_pb': out['w_pb'], 'w_o': out['w_o'], 'g_post_mix': out['g_post_mix'], 'g_pre_mlp': out['g_pre_mlp'], 'w_up': out['w_up'], 'w_down': out['w_down'], 'g_post_mlp': out['g_post_mlp'], 'loss_target': out['loss_target'], 'm_g_pre_mix': out['m_g_pre_mix'], 'm_w_in': out['m_w_in'], 'm_w_a_up': out['m_w_a_up'], 'm_b_a_up': out['m_b_a_up'], 'm_g_gla_norm': out['m_g_gla_norm'], 'm_conv_w': out['m_conv_w'], 'm_conv_b': out['m_conv_b'], 'm_w_q_ml': out['m_w_q_ml'], 'm_w_k_ml': out['m_w_k_ml'], 'm_w_v_ml': out['m_w_v_ml'], 'm_w_if': out['m_w_if'], 'm_b_if': out['m_b_if'], 'm_ml_skip': out['m_ml_skip'], 'm_g_ml_norm': out['m_g_ml_norm'], 'm_w_pa': out['m_w_pa'], 'm_w_pb': out['m_w_pb'], 'm_w_o': out['m_w_o'], 'm_g_post_mix': out['m_g_post_mix'], 'm_g_pre_mlp': out['m_g_pre_mlp'], 'm_w_up': out['m_w_up'], 'm_w_down': out['m_w_down'], 'm_g_post_mlp': out['m_g_post_mlp'], 'v_g_pre_mix': out['v_g_pre_mix'], 'v_w_in': out['v_w_in'], 'v_w_a_up': out['v_w_a_up'], 'v_b_a_up': out['v_b_a_up'], 'v_g_gla_norm': out['v_g_gla_norm'], 'v_conv_w': out['v_conv_w'], 'v_conv_b': out['v_conv_b'], 'v_w_q_ml': out['v_w_q_ml'], 'v_w_k_ml': out['v_w_k_ml'], 'v_w_v_ml': out['v_w_v_ml'], 'v_w_if': out['v_w_if'], 'v_b_if': out['v_b_if'], 'v_ml_skip': out['v_ml_skip'], 'v_g_ml_norm': out['v_g_ml_norm'], 'v_w_pa': out['v_w_pa'], 'v_w_pb': out['v_w_pb'], 'v_w_o': out['v_w_o'], 'v_g_post_mix': out['v_g_post_mix'], 'v_g_pre_mlp': out['v_g_pre_mlp'], 'v_w_up': out['v_w_up'], 'v_w_down': out['v_w_down'], 'v_g_post_mlp': out['v_g_post_mlp']}


def _loss(weights, diff, rest, loss_target):
    with _jax.named_scope("forward"):
        args = {**rest, TWIN_DIFF_INPUT: diff, **{k: w.astype(_WEIGHT_DTYPES[k]) for k, w in weights.items()}}
        y = _forward(args)
    with _jax.named_scope("loss_head"):
        err = _jnp.square(y.astype(_jnp.float32) - loss_target)
        return 0.5 * _jnp.sum(_jnp.mean(err, axis=-1)) if err.ndim else 0.5 * err


def _adamw(w, g, m, v):
    m = ADAM_B1 * m + (1.0 - ADAM_B1) * g
    v = ADAM_B2 * v + (1.0 - ADAM_B2) * _jnp.square(g)
    m_hat = m / (1.0 - ADAM_B1 ** ADAM_STEP)
    v_hat = v / (1.0 - ADAM_B2 ** ADAM_STEP)
    delta = -ADAM_LR * (m_hat / (_jnp.sqrt(v_hat) + ADAM_EPS) + ADAM_WD * w)
    return delta, m, v


def reference(x, g_pre_mix, w_in, w_a_up, b_a_up, g_gla_norm, conv_w, conv_b, w_q_ml, w_k_ml, w_v_ml, w_if, b_if, ml_skip, g_ml_norm, w_pa, w_pb, w_o, g_post_mix, g_pre_mlp, w_up, w_down, g_post_mlp, loss_target, m_g_pre_mix, m_w_in, m_w_a_up, m_b_a_up, m_g_gla_norm, m_conv_w, m_conv_b, m_w_q_ml, m_w_k_ml, m_w_v_ml, m_w_if, m_b_if, m_ml_skip, m_g_ml_norm, m_w_pa, m_w_pb, m_w_o, m_g_post_mix, m_g_pre_mlp, m_w_up, m_w_down, m_g_post_mlp, v_g_pre_mix, v_w_in, v_w_a_up, v_b_a_up, v_g_gla_norm, v_conv_w, v_conv_b, v_w_q_ml, v_w_k_ml, v_w_v_ml, v_w_if, v_b_if, v_ml_skip, v_g_ml_norm, v_w_pa, v_w_pb, v_w_o, v_g_post_mix, v_g_pre_mlp, v_w_up, v_w_down, v_g_post_mlp):
    given = dict(x=x, g_pre_mix=g_pre_mix, w_in=w_in, w_a_up=w_a_up, b_a_up=b_a_up, g_gla_norm=g_gla_norm, conv_w=conv_w, conv_b=conv_b, w_q_ml=w_q_ml, w_k_ml=w_k_ml, w_v_ml=w_v_ml, w_if=w_if, b_if=b_if, ml_skip=ml_skip, g_ml_norm=g_ml_norm, w_pa=w_pa, w_pb=w_pb, w_o=w_o, g_post_mix=g_post_mix, g_pre_mlp=g_pre_mlp, w_up=w_up, w_down=w_down, g_post_mlp=g_post_mlp, loss_target=loss_target, m_g_pre_mix=m_g_pre_mix, m_w_in=m_w_in, m_w_a_up=m_w_a_up, m_b_a_up=m_b_a_up, m_g_gla_norm=m_g_gla_norm, m_conv_w=m_conv_w, m_conv_b=m_conv_b, m_w_q_ml=m_w_q_ml, m_w_k_ml=m_w_k_ml, m_w_v_ml=m_w_v_ml, m_w_if=m_w_if, m_b_if=m_b_if, m_ml_skip=m_ml_skip, m_g_ml_norm=m_g_ml_norm, m_w_pa=m_w_pa, m_w_pb=m_w_pb, m_w_o=m_w_o, m_g_post_mix=m_g_post_mix, m_g_pre_mlp=m_g_pre_mlp, m_w_up=m_w_up, m_w_down=m_w_down, m_g_post_mlp=m_g_post_mlp, v_g_pre_mix=v_g_pre_mix, v_w_in=v_w_in, v_w_a_up=v_w_a_up, v_b_a_up=v_b_a_up, v_g_gla_norm=v_g_gla_norm, v_conv_w=v_conv_w, v_conv_b=v_conv_b, v_w_q_ml=v_w_q_ml, v_w_k_ml=v_w_k_ml, v_w_v_ml=v_w_v_ml, v_w_if=v_w_if, v_b_if=v_b_if, v_ml_skip=v_ml_skip, v_g_ml_norm=v_g_ml_norm, v_w_pa=v_w_pa, v_w_pb=v_w_pb, v_w_o=v_w_o, v_g_post_mix=v_g_post_mix, v_g_pre_mlp=v_g_pre_mlp, v_w_up=v_w_up, v_w_down=v_w_down, v_g_post_mlp=v_g_post_mlp)
    weights = {n: given[n] for n in TWIN_WEIGHTS}
    shared = {n: given[n] for n in SHARED_INPUTS}
    per_example = {n: given[n] for n in ['x']}
    grad_fn = _jax.value_and_grad(_loss, argnums=(0, 1))

    def one_microbatch(ex, loss_target):
        ex = dict(ex)
        diff = ex.pop(TWIN_DIFF_INPUT)
        return grad_fn(weights, diff, {**shared, **ex}, loss_target)

    if N_MICROBATCH == 1:
        loss, (grad_w, grad_x) = one_microbatch(per_example, given["loss_target"])
    else:
        def body(carry, xs):
            loss_sum, grad_sum = carry
            l_k, (gw_k, gx_k) = one_microbatch(xs[0], xs[1])
            with _jax.named_scope("update"):
                return (loss_sum + l_k, _jax.tree.map(_jnp.add, grad_sum, gw_k)), gx_k

        init = (_jnp.zeros((), _jnp.float32), _jax.tree.map(_jnp.zeros_like, weights))
        (loss, grad_w), grad_x = _jax.lax.scan(body, init, (per_example, given["loss_target"]))
    with _jax.named_scope("update"):
        delta_w, new_m, new_v = {}, {}, {}
        for n in TWIN_WEIGHTS:
            delta_w[n], new_m[n], new_v[n] = _adamw(weights[n], grad_w[n], given["m_" + n], given["v_" + n])
    return (loss, grad_x, *[grad_w[n] for n in TWIN_WEIGHTS], *[delta_w[n] for n in TWIN_WEIGHTS],
            *[new_m[n] for n in TWIN_WEIGHTS], *[new_v[n] for n in TWIN_WEIGHTS])
```

```python
import functools

import jax
import jax.numpy as jnp
from jax import lax
from jax.experimental import pallas as pl
from jax.experimental.pallas import tpu as pltpu

F32 = jnp.float32
BF16 = jnp.bfloat16

SEQ = 2048
D_MODEL = 1024
CHUNK = 64
N_CHUNK = SEQ // CHUNK
HEADS = 4
GLA_DK = 64
GLA_DV = 128
ML_DH = 128
D_FF = 4096
EPS = 1e-6
N_CHIP = 4
N_DEV = 8
TOK_TILE = 256
N_TOK_TILE = SEQ // TOK_TILE

PM_W = 2688
PM_XM = 1536
PM_OP = 2048
PM_AL = 2560
GAB_W = 2048

ADAM_LR = 0.001
ADAM_B1 = 0.9
ADAM_B2 = 0.999
ADAM_EPS = 1e-08
ADAM_WD = 0.01
ADAM_STEP = 10

VMEM_LIMIT = 56 * 1024 * 1024


def _params(sem=None):
    return pltpu.CompilerParams(dimension_semantics=sem, vmem_limit_bytes=VMEM_LIMIT)


def _dot(a, b, ca, cb):
    return lax.dot_general(a.astype(BF16), b.astype(BF16), (((ca,), (cb,)), ((), ())), preferred_element_type=F32)


def _pmm_nn(a, b):
    return _dot(a, b, 1, 0)


def _pmm_nt(a, b):
    return _dot(a, b, 1, 1)


def _pmm_tn(a, b):
    return _dot(a, b, 0, 0)


def _pcmm(c, x):
    return lax.dot_general(c, x, (((1,), (0,)), ((), ())), precision=lax.Precision.HIGHEST, preferred_element_type=F32)


@jax.custom_vjp
def _mm_nn(a, b):
    return _dot(a, b, 1, 0)


@jax.custom_vjp
def _mm_nt(a, b):
    return _dot(a, b, 1, 1)


@jax.custom_vjp
def _mm_tn(a, b):
    return _dot(a, b, 0, 0)


_mm_nn.defvjp(lambda a, b: (_dot(a, b, 1, 0), (a, b)), lambda r, g: (_mm_nt(g, r[1]), _mm_tn(r[0], g)))
_mm_nt.defvjp(lambda a, b: (_dot(a, b, 1, 1), (a, b)), lambda r, g: (_mm_nn(g, r[1]), _mm_tn(g, r[0])))
_mm_tn.defvjp(lambda a, b: (_dot(a, b, 0, 0), (a, b)), lambda r, g: (_mm_nt(r[1], g), _mm_nn(r[0], g)))


@jax.custom_vjp
def _cmm(c, x):
    return _pcmm(c, x)


_cmm.defvjp(
    lambda c, x: (_pcmm(c, x), c),
    lambda c, g: (jnp.zeros_like(c), lax.dot_general(c, g, (((0,), (0,)), ((), ())), precision=lax.Precision.HIGHEST,
                                                      preferred_element_type=F32)),
)

_PLAIN_OPS = (_pmm_nn, _pmm_nt, _pmm_tn, _pcmm)
_VJP_OPS = (_mm_nn, _mm_nt, _mm_tn, _cmm)


def _sigmoid(x):
    return 0.5 * (jnp.tanh(0.5 * x) + 1.0)


def _log_sigmoid(x):
    return jnp.minimum(x, 0.0) - jnp.log(1.0 + jnp.exp(-jnp.abs(x)))


def _mean(x):
    return jnp.mean(x, axis=-1, keepdims=True)


def _mixer_chunk(ops, p, st, pm, xprev8):
    mm_nn, mm_nt, mm_tn, cmm = ops
    row = lax.broadcasted_iota(jnp.int32, (CHUNK, CHUNK), 0)
    col = lax.broadcasted_iota(jnp.int32, (CHUNK, CHUNK), 1)
    causal = row >= col
    tri = causal.astype(F32)
    q = pm[:, 0:256]
    k = pm[:, 256:512]
    v = pm[:, 512:1024]
    g = pm[:, 1024:1536]
    xm = pm[:, PM_XM:PM_XM + 512]
    opre = pm[:, PM_OP:PM_OP + 512]
    alow = pm[:, PM_AL:PM_AL + 128]

    la = _log_sigmoid(mm_nn(alow, p["wau"]) + p["bau"]) * (1.0 / 16.0)
    cum = cmm(tri, la)
    cum_last = cum[CHUNK - 1:CHUNK, :]
    e_pos = jnp.exp(cum)
    e_neg = jnp.exp(-cum)
    qs = q * (GLA_DK ** -0.5)
    qp = qs * e_pos
    qn = qs * e_neg
    kp = k * e_pos
    kn = k * e_neg
    kl = k * jnp.exp(cum_last - cum)
    dec = jnp.exp(cum_last)
    outs = []
    s_new = []
    for h in range(HEADS):
        s6 = slice(h * GLA_DK, (h + 1) * GLA_DK)
        s12 = slice(h * GLA_DV, (h + 1) * GLA_DV)
        scores = jnp.where(causal, mm_nt(qp[:, s6], kn[:, s6]), mm_nt(qn[:, s6], kp[:, s6]))
        o = mm_nn(scores, v[:, s12]) + mm_nt(qp[:, s6], st["S"][h])
        s_new.append(st["S"][h] * dec[:, s6] + mm_tn(v[:, s12], kl[:, s6]))
        o = o * lax.rsqrt(_mean(o * o) + EPS) * p["ggla"]
        gh = g[:, s12]
        outs.append(o * (gh * _sigmoid(gh)))

    xx = jnp.concatenate([xprev8, xm], axis=0)
    pre = p["cb"]
    for j in range(4):
        pre = pre + p["cw"][j:j + 1, :] * xx[5 + j:5 + j + CHUNK, :]
    xc = pre * _sigmoid(pre)
    qm, km, vm = [], [], []
    for h in range(HEADS):
        s12 = slice(h * ML_DH, (h + 1) * ML_DH)
        qm.append(mm_nn(xc[:, s12], p["wq"][h]))
        km.append(mm_nn(xc[:, s12], p["wk"][h]))
        vm.append(mm_nn(xm[:, s12], p["wv"][h]))
    qcat = jnp.concatenate(qm, axis=1)
    kcat = jnp.concatenate(km, axis=1)
    vcat = jnp.concatenate(vm, axis=1)
    gates = (mm_nn(qcat, p["wif"][0:512]) + mm_nn(kcat, p["wif"][512:1024]) + mm_nn(vcat, p["wif"][1024:1536])
             + p["bif"])
    lf = _log_sigmoid(gates)
    fc = cmm(tri, lf)
    gates_t = gates.T
    fc_t = fc.T
    c_new, n_new, m_new = [], [], []
    for h in range(HEADS):
        s12 = slice(h * ML_DH, (h + 1) * ML_DH)
        li_c = gates[:, h:h + 1]
        fc_c = fc[:, 4 + h:5 + h]
        li_r = gates_t[h:h + 1, :]
        fc_r = fc_t[4 + h:5 + h, :]
        m_prev = st["m"][h][:, 0:1]
        log_d = li_r - jnp.abs(fc_c - fc_r)
        g_int = fc_c + m_prev
        m_t = jnp.maximum(g_int, jnp.max(log_d, axis=1, keepdims=True))
        ks = km[h] * (ML_DH ** -0.5)
        s = mm_nt(qm[h], ks) * jnp.exp(log_d - m_t)
        scl = jnp.exp(g_int - m_t)
        num = mm_nn(s, vm[h]) + scl * mm_nn(qm[h], st["C"][h])
        den = jnp.sum(s, axis=1, keepdims=True) + scl * jnp.sum(qm[h] * st["n"][h], axis=1, keepdims=True)
        den = jnp.maximum(jnp.abs(den), jnp.exp(-m_t))
        hc = num / den * _sigmoid(opre[:, s12])
        d0 = hc - _mean(hc)
        y = d0 * lax.rsqrt(_mean(d0 * d0) + EPS)
        outs.append(y * p["gml"][:, s12] + p["skip"][:, s12] * xc[:, s12])
        f_last = fc[CHUNK - 1:CHUNK, 4 + h:5 + h]
        a = f_last - fc_c + li_c
        m_loc = jnp.max(a, axis=0, keepdims=True)
        kw = ks * jnp.exp(a - m_loc)
        m_nx = jnp.maximum(f_last + m_prev, m_loc)
        sp = jnp.exp(f_last + m_prev - m_nx)
        sl = jnp.exp(m_loc - m_nx)
        c_new.append(sp * st["C"][h] + sl * mm_tn(kw, vm[h]))
        n_new.append(sp * st["n"][h] + sl * jnp.sum(kw, axis=0, keepdims=True))
        m_new.append(jnp.broadcast_to(m_nx, (1, ML_DH)))
    ab = jnp.concatenate(outs, axis=1)
    new = {"S": jnp.stack(s_new), "C": jnp.stack(c_new), "n": jnp.stack(n_new), "m": jnp.stack(m_new)}
    return ab, new


_P_NAMES = ("wau", "bau", "ggla", "cw", "cb", "wq", "wk", "wv", "wif", "bif", "skip", "gml")
_P_SHAPES = {
    "wau": (128, 256), "bau": (1, 256), "ggla": (1, 128), "cw": (4, 512), "cb": (1, 512),
    "wq": (HEADS, 128, 128), "wk": (HEADS, 128, 128), "wv": (HEADS, 128, 128),
    "wif": (1536, 128), "bif": (1, 128), "skip": (1, 512), "gml": (1, 512),
}
_S_NAMES = ("S", "C", "n", "m")
_S_SHAPES = {"S": (HEADS, GLA_DV, GLA_DK), "C": (HEADS, ML_DH, ML_DH), "n": (HEADS, 1, ML_DH), "m": (HEADS, 1, ML_DH)}


def _const_spec(shape):
    zeros = (0,) * len(shape)
    return pl.BlockSpec(shape, lambda i: zeros)


def _mixer_fwd(pm, p):
    n_p = len(_P_NAMES)

    def body(*refs):
        pm_ref, xprev_ref = refs[0], refs[1]
        p_refs = refs[2:2 + n_p]
        ab_ref = refs[2 + n_p]
        so_refs = refs[3 + n_p:7 + n_p]
        sc_refs = refs[7 + n_p:11 + n_p]
        n = pl.program_id(0)

        @pl.when(n == 0)
        def _():
            for r in sc_refs:
                r[...] = jnp.zeros_like(r)

        st = {name: r[...] for name, r in zip(_S_NAMES, sc_refs)}
        for name, r in zip(_S_NAMES, so_refs):
            r[0] = st[name]
        pv = {name: r[...] for name, r in zip(_P_NAMES, p_refs)}
        xprev8 = jnp.where(n > 0, xprev_ref[CHUNK - 8:CHUNK, :], 0.0)
        ab, new = _mixer_chunk(_PLAIN_OPS, pv, st, pm_ref[...], xprev8)
        ab_ref[...] = ab.astype(BF16)
        for name, r in zip(_S_NAMES, sc_refs):
            r[...] = new[name]

    in_specs = [pl.BlockSpec((CHUNK, PM_W), lambda i: (i, 0)),
                pl.BlockSpec((CHUNK, 512), lambda i: (jnp.maximum(i - 1, 0), PM_XM // 512))]
    in_specs += [_const_spec(_P_SHAPES[nm]) for nm in _P_NAMES]
    out_specs = [pl.BlockSpec((CHUNK, 1024), lambda i: (i, 0))]
    out_shape = [jax.ShapeDtypeStruct((SEQ, 1024), BF16)]
    for nm in _S_NAMES:
        shp = _S_SHAPES[nm]
        out_specs.append(pl.BlockSpec((1,) + shp, lambda i: (i, 0, 0, 0)))
        out_shape.append(jax.ShapeDtypeStruct((N_CHUNK,) + shp, F32))
    return pl.pallas_call(
        body, grid=(N_CHUNK,), in_specs=in_specs, out_specs=out_specs, out_shape=out_shape,
        scratch_shapes=[pltpu.VMEM(_S_SHAPES[nm], F32) for nm in _S_NAMES],
        compiler_params=_params(("arbitrary",)), name="mixer_fwd",
    )(pm, pm, *[p[nm] for nm in _P_NAMES])


def _mixer_bwd(pm, dab, states, p):
    n_p = len(_P_NAMES)

    def body(*refs):
        pm_ref, xprev_ref, dab_ref = refs[0], refs[1], refs[2]
        si_refs = refs[3:7]
        p_refs = refs[7:7 + n_p]
        dpm_ref = refs[7 + n_p]
        dp_refs = refs[8 + n_p:8 + 2 * n_p]
        ds_refs = refs[8 + 2 * n_p:12 + 2 * n_p]
        carry_ref = refs[12 + 2 * n_p]
        i = pl.program_id(0)
        n = N_CHUNK - 1 - i

        @pl.when(i == 0)
        def _():
            for r in ds_refs:
                r[...] = jnp.zeros_like(r)
            for r in dp_refs:
                r[...] = jnp.zeros_like(r)
            carry_ref[...] = jnp.zeros_like(carry_ref)

        st = {name: r[0] for name, r in zip(_S_NAMES, si_refs)}
        pv = {name: r[...] for name, r in zip(_P_NAMES, p_refs)}
        xprev8 = jnp.where(n > 0, xprev_ref[CHUNK - 8:CHUNK, :], 0.0)
        _, vjp = jax.vjp(functools.partial(_mixer_chunk, _VJP_OPS), pv, st, pm_ref[...], xprev8)
        dst = {name: r[...] for name, r in zip(_S_NAMES, ds_refs)}
        dp, dst_prev, dpm, dxprev8 = vjp((dab_ref[...], dst))
        reach = jnp.concatenate([jnp.zeros((CHUNK - 8, 512), F32), carry_ref[...]], axis=0)
        dpm_ref[:, 0:PM_XM] = dpm[:, 0:PM_XM].astype(BF16)
        dpm_ref[:, PM_XM:PM_XM + 512] = (dpm[:, PM_XM:PM_XM + 512] + reach).astype(BF16)
        dpm_ref[:, PM_XM + 512:PM_W] = dpm[:, PM_XM + 512:PM_W].astype(BF16)
        carry_ref[...] = dxprev8
        for name, r in zip(_S_NAMES, ds_refs):
            r[...] = dst_prev[name]
        for name, r in zip(_P_NAMES, dp_refs):
            r[...] += dp[name]

    rev = lambda i: (N_CHUNK - 1 - i, 0)
    in_specs = [pl.BlockSpec((CHUNK, PM_W), rev),
                pl.BlockSpec((CHUNK, 512), lambda i: (jnp.maximum(N_CHUNK - 2 - i, 0), PM_XM // 512)),
                pl.BlockSpec((CHUNK, 1024), rev)]
    for nm in _S_NAMES:
        in_specs.append(pl.BlockSpec((1,) + _S_SHAPES[nm], lambda i: (N_CHUNK - 1 - i, 0, 0, 0)))
    in_specs += [_const_spec(_P_SHAPES[nm]) for nm in _P_NAMES]
    out_specs = [pl.BlockSpec((CHUNK, PM_W), rev)] + [_const_spec(_P_SHAPES[nm]) for nm in _P_NAMES]
    out_shape = [jax.ShapeDtypeStruct((SEQ, PM_W), BF16)] + [jax.ShapeDtypeStruct(_P_SHAPES[nm], F32) for nm in _P_NAMES]
    res = pl.pallas_call(
        body, grid=(N_CHUNK,), in_specs=in_specs, out_specs=out_specs, out_shape=out_shape,
        scratch_shapes=[pltpu.VMEM(_S_SHAPES[nm], F32) for nm in _S_NAMES] + [pltpu.VMEM((8, 512), F32)],
        compiler_params=_params(("arbitrary",)), name="mixer_bwd",
    )(pm, pm, dab, *states, *[p[nm] for nm in _P_NAMES])
    return res[0], dict(zip(_P_NAMES, res[1:]))


def _tok(width):
    return pl.BlockSpec((TOK_TILE, width), lambda i: (i, 0))


def _once(shape):
    zeros = (0,) * len(shape)
    return pl.BlockSpec(shape, lambda i: zeros, pipeline_mode=pl.Buffered(1))


def _rms_fwd(x):
    r = lax.rsqrt(_mean(x * x) + EPS)
    return x * r, r


def _rms_bwd(dy, xn, r, g):
    gd = dy * g
    return r * (gd - xn * _mean(xn * gd))


def _in_proj(x, g_pre, w_mix, w_ab):
    def body(x_ref, g_ref, wm_ref, wa_ref, pm_ref, gab_ref, h_ref):
        xn, _ = _rms_fwd(x_ref[...])
        h = (xn * g_ref[...]).astype(BF16)
        h_ref[...] = h
        pm_ref[...] = jnp.dot(h, wm_ref[...], preferred_element_type=F32)
        gab_ref[...] = jnp.dot(h, wa_ref[...], preferred_element_type=F32)

    return pl.pallas_call(
        body, grid=(N_TOK_TILE,),
        in_specs=[_tok(D_MODEL), _once((1, D_MODEL)), _once((D_MODEL, PM_W)), _once((D_MODEL, GAB_W))],
        out_specs=[_tok(PM_W), _tok(GAB_W), _tok(D_MODEL)],
        out_shape=[jax.ShapeDtypeStruct((SEQ, PM_W), F32), jax.ShapeDtypeStruct((SEQ, GAB_W), F32),
                   jax.ShapeDtypeStruct((SEQ, D_MODEL), BF16)],
        compiler_params=_params(("arbitrary",)), name="in_proj",
    )(x, g_pre, w_mix, w_ab)


def _merge_fwd(ab, gab, x, w_pa, w_pb, w_o, g_post):
    def body(ab_ref, gab_ref, x_ref, wpa_ref, wpb_ref, wo_ref, g_ref, x1_ref, mix_ref, mg_ref):
        ya = jnp.dot(ab_ref[:, 0:512], wpa_ref[...], preferred_element_type=F32)
        yb = jnp.dot(ab_ref[:, 512:1024], wpb_ref[...], preferred_element_type=F32)
        merged = (_sigmoid(gab_ref[:, 0:1024]) * ya + _sigmoid(gab_ref[:, 1024:2048]) * yb).astype(BF16)
        mg_ref[...] = merged
        mix = jnp.dot(merged, wo_ref[...], preferred_element_type=F32)
        mix_ref[...] = mix
        mn, _ = _rms_fwd(mix)
        x1_ref[...] = x_ref[...] + mn * g_ref[...]

    return pl.pallas_call(
        body, grid=(N_TOK_TILE,),
        in_specs=[_tok(1024), _tok(GAB_W), _tok(D_MODEL), _once((512, D_MODEL)), _once((512, D_MODEL)),
                  _once((D_MODEL, D_MODEL)), _once((1, D_MODEL))],
        out_specs=[_tok(D_MODEL), _tok(D_MODEL), _tok(D_MODEL)],
        out_shape=[jax.ShapeDtypeStruct((SEQ, D_MODEL), F32), jax.ShapeDtypeStruct((SEQ, D_MODEL), F32),
                   jax.ShapeDtypeStruct((SEQ, D_MODEL), BF16)],
        compiler_params=_params(("arbitrary",)), name="merge_fwd",
    )(ab, gab, x, w_pa, w_pb, w_o, g_post)


def _mlp(x1, target, g_pre, g_post, w_up, w_down):
    def body(x1_ref, t_ref, gpre_ref, gpost_ref, wup_ref, wdn_ref,
             dx1_ref, u_ref, dd_ref, h2_ref, dpre_ref, dgpost_ref, dgpre_ref, loss_ref):
        @pl.when(pl.program_id(0) == 0)
        def _():
            dgpost_ref[...] = jnp.zeros_like(dgpost_ref)
            dgpre_ref[...] = jnp.zeros_like(dgpre_ref)
            loss_ref[...] = jnp.zeros_like(loss_ref)

        x1 = x1_ref[...]
        gpre = gpre_ref[...]
        gpost = gpost_ref[...]
        xn2, r2 = _rms_fwd(x1)
        h2 = (xn2 * gpre).astype(BF16)
        h2_ref[...] = h2
        pre = jnp.dot(h2, wup_ref[...], preferred_element_type=F32)
        rl = jnp.maximum(pre, 0.0)
        u = (rl * rl).astype(BF16)
        u_ref[...] = u
        d = jnp.dot(u, wdn_ref[...], preferred_element_type=F32)
        dn, r3 = _rms_fwd(d)
        diff = x1 + dn * gpost - t_ref[...]
        loss_ref[...] += jnp.sum(diff * diff, keepdims=True).reshape(1, 1) * (0.5 / D_MODEL)
        dy = diff * (1.0 / D_MODEL)
        dgpost_ref[...] += jnp.sum(dy * dn, axis=0, keepdims=True)
        dd = _rms_bwd(dy, dn, r3, gpost).astype(BF16)
        dd_ref[...] = dd
        du = lax.dot_general(dd, wdn_ref[...], (((1,), (1,)), ((), ())), preferred_element_type=F32)
        dpre = (du * (2.0 * rl)).astype(BF16)
        dpre_ref[...] = dpre
        dh2 = lax.dot_general(dpre, wup_ref[...], (((1,), (1,)), ((), ())), preferred_element_type=F32)
        dgpre_ref[...] += jnp.sum(dh2 * xn2, axis=0, keepdims=True)
        dx1_ref[...] = dy + _rms_bwd(dh2, xn2, r2, gpre)

    acc = pl.BlockSpec((1, D_MODEL), lambda i: (0, 0))
    return pl.pallas_call(
        body, grid=(N_TOK_TILE,),
        in_specs=[_tok(D_MODEL), _tok(D_MODEL), _once((1, D_MODEL)), _once((1, D_MODEL)),
                  _once((D_MODEL, D_FF)), _once((D_FF, D_MODEL))],
        out_specs=[_tok(D_MODEL), _tok(D_FF), _tok(D_MODEL), _tok(D_MODEL), _tok(D_FF), acc, acc,
                   pl.BlockSpec((1, 128), lambda i: (0, 0))],
        out_shape=[jax.ShapeDtypeStruct((SEQ, D_MODEL), F32), jax.ShapeDtypeStruct((SEQ, D_FF), BF16),
                   jax.ShapeDtypeStruct((SEQ, D_MODEL), BF16), jax.ShapeDtypeStruct((SEQ, D_MODEL), BF16),
                   jax.ShapeDtypeStruct((SEQ, D_FF), BF16), jax.ShapeDtypeStruct((1, D_MODEL), F32),
                   jax.ShapeDtypeStruct((1, D_MODEL), F32), jax.ShapeDtypeStruct((1, 128), F32)],
        compiler_params=_params(("arbitrary",)), name="mlp_fwd_bwd",
    )(x1, target, g_pre, g_post, w_up, w_down)


def _merge_bwd(dx1, mix, ab, gab, w_pa, w_pb, w_o, g_post):
    def body(dx1_ref, mix_ref, ab_ref, gab_ref, wpa_ref, wpb_ref, wo_ref, g_ref,
             dmix_ref, dya_ref, dyb_ref, dgab_ref, dab_ref, dg_ref):
        @pl.when(pl.program_id(0) == 0)
        def _():
            dg_ref[...] = jnp.zeros_like(dg_ref)

        dx1 = dx1_ref[...]
        mn, r = _rms_fwd(mix_ref[...])
        dg_ref[...] += jnp.sum(dx1 * mn, axis=0, keepdims=True)
        dmix = _rms_bwd(dx1, mn, r, g_ref[...]).astype(BF16)
        dmix_ref[...] = dmix
        dmerged = lax.dot_general(dmix, wo_ref[...], (((1,), (1,)), ((), ())), preferred_element_type=F32)
        ya = jnp.dot(ab_ref[:, 0:512], wpa_ref[...], preferred_element_type=F32)
        yb = jnp.dot(ab_ref[:, 512:1024], wpb_ref[...], preferred_element_type=F32)
        sa = _sigmoid(gab_ref[:, 0:1024])
        sb = _sigmoid(gab_ref[:, 1024:2048])
        dya = (dmerged * sa).astype(BF16)
        dyb = (dmerged * sb).astype(BF16)
        dya_ref[...] = dya
        dyb_ref[...] = dyb
        dgab_ref[:, 0:1024] = (dmerged * ya * sa * (1.0 - sa)).astype(BF16)
        dgab_ref[:, 1024:2048] = (dmerged * yb * sb * (1.0 - sb)).astype(BF16)
        dab_ref[:, 0:512] = lax.dot_general(dya, wpa_ref[...], (((1,), (1,)), ((), ())), preferred_element_type=F32)
        dab_ref[:, 512:1024] = lax.dot_general(dyb, wpb_ref[...], (((1,), (1,)), ((), ())), preferred_element_type=F32)

    return pl.pallas_call(
        body, grid=(N_TOK_TILE,),
        in_specs=[_tok(D_MODEL), _tok(D_MODEL), _tok(1024), _tok(GAB_W), _once((512, D_MODEL)), _once((512, D_MODEL)),
                  _once((D_MODEL, D_MODEL)), _once((1, D_MODEL))],
        out_specs=[_tok(D_MODEL), _tok(D_MODEL), _tok(D_MODEL), _tok(GAB_W), _tok(1024),
                   pl.BlockSpec((1, D_MODEL), lambda i: (0, 0))],
        out_shape=[jax.ShapeDtypeStruct((SEQ, D_MODEL), BF16), jax.ShapeDtypeStruct((SEQ, D_MODEL), BF16),
                   jax.ShapeDtypeStruct((SEQ, D_MODEL), BF16), jax.ShapeDtypeStruct((SEQ, GAB_W), BF16),
                   jax.ShapeDtypeStruct((SEQ, 1024), F32), jax.ShapeDtypeStruct((1, D_MODEL), F32)],
        compiler_params=_params(("arbitrary",)), name="merge_bwd",
    )(dx1, mix, ab, gab, w_pa, w_pb, w_o, g_post)


def _in_proj_bwd(dpm, dgab, x, dx1, g_pre, w_mix, w_ab):
    def body(dpm_ref, dgab_ref, x_ref, dx1_ref, g_ref, wm_ref, wa_ref, dx_ref, dg_ref):
        @pl.when(pl.program_id(0) == 0)
        def _():
            dg_ref[...] = jnp.zeros_like(dg_ref)

        dh = lax.dot_general(dpm_ref[...], wm_ref[...], (((1,), (1,)), ((), ())), preferred_element_type=F32)
        dh = dh + lax.dot_general(dgab_ref[...], wa_ref[...], (((1,), (1,)), ((), ())), preferred_element_type=F32)
        xn, r = _rms_fwd(x_ref[...])
        dg_ref[...] += jnp.sum(dh * xn, axis=0, keepdims=True)
        dx_ref[...] = dx1_ref[...] + _rms_bwd(dh, xn, r, g_ref[...])

    return pl.pallas_call(
        body, grid=(N_TOK_TILE,),
        in_specs=[_tok(PM_W), _tok(GAB_W), _tok(D_MODEL), _tok(D_MODEL), _once((1, D_MODEL)),
                  _once((D_MODEL, PM_W)), _once((D_MODEL, GAB_W))],
        out_specs=[_tok(D_MODEL), pl.BlockSpec((1, D_MODEL), lambda i: (0, 0))],
        out_shape=[jax.ShapeDtypeStruct((SEQ, D_MODEL), F32), jax.ShapeDtypeStruct((1, D_MODEL), F32)],
        compiler_params=_params(("arbitrary",)), name="in_proj_bwd",
    )(dpm, dgab, x, dx1, g_pre, w_mix, w_ab)


def _tn_matmul(a, b, name, tm=512, tn=None):
    m, n = a.shape[1], b.shape[1]
    tm = min(tm, m)
    if tn is None:
        tn = 896 if n == PM_W else min(n, 1024)

    def body(a_ref, b_ref, o_ref):
        o_ref[...] = lax.dot_general(a_ref[...], b_ref[...], (((0,), (0,)), ((), ())),
                                     preferred_element_type=F32).astype(BF16)

    return pl.pallas_call(
        body, grid=(m // tm, n // tn),
        in_specs=[pl.BlockSpec((SEQ, tm), lambda i, j: (0, i)), pl.BlockSpec((SEQ, tn), lambda i, j: (0, j))],
        out_specs=pl.BlockSpec((tm, tn), lambda i, j: (i, j)),
        out_shape=jax.ShapeDtypeStruct((m, n), BF16),
        compiler_params=_params(("arbitrary", "arbitrary")), name=name,
    )(a, b)


MESH = pl.DeviceIdType.MESH
ANY = pl.BlockSpec(memory_space=pl.ANY)


def _gather_shards(shards):
    n = len(shards)

    def body(*refs):
        in_refs, out_refs = refs[:n], refs[n:2 * n]
        send_sems, recv_sems, local_sems = refs[2 * n:]
        x, y, c = lax.axis_index("x"), lax.axis_index("y"), lax.axis_index("c")
        me = 2 * x + y
        peers = [(1 - x, y), (x, 1 - y), (1 - x, 1 - y)]
        copies = []
        for k in range(n):
            loc = pltpu.make_async_copy(in_refs[k], out_refs[k].at[me], local_sems.at[k])
            loc.start()
            copies.append(loc)
        sends = []
        for k in range(n):
            for j, (px, py) in enumerate(peers):
                cp = pltpu.make_async_remote_copy(
                    src_ref=in_refs[k], dst_ref=out_refs[k].at[me], send_sem=send_sems.at[3 * k + j],
                    recv_sem=recv_sems.at[3 * k + j], device_id=(px, py, c), device_id_type=MESH)
                cp.start()
                sends.append(cp)
        for k in range(n):
            for j, (px, py) in enumerate(peers):
                pltpu.make_async_remote_copy(
                    src_ref=in_refs[k], dst_ref=out_refs[k].at[2 * px + py], send_sem=send_sems.at[3 * k + j],
                    recv_sem=recv_sems.at[3 * k + j], device_id=(px, py, c), device_id_type=MESH).wait_recv()
        for cp in sends:
            cp.wait_send()
        for cp in copies:
            cp.wait()

    return pl.pallas_call(
        body, in_specs=[ANY] * n, out_specs=[ANY] * n,
        out_shape=[jax.ShapeDtypeStruct((N_CHIP,) + s.shape, s.dtype) for s in shards],
        scratch_shapes=[pltpu.SemaphoreType.DMA((3 * n,)), pltpu.SemaphoreType.DMA((3 * n,)), pltpu.SemaphoreType.DMA((n,))],
        name="gather_weights",
    )(*shards)


def _exchange_grads(grads, small):
    n = len(grads)

    def body(*refs):
        g_refs, small_ref = refs[:n], refs[n]
        o_refs, osmall_ref = refs[n + 1:2 * n + 1], refs[2 * n + 1]
        send_sems, recv_sems, local_sems = refs[2 * n + 2:]
        x, y, c = lax.axis_index("x"), lax.axis_index("y"), lax.axis_index("c")
        me = 4 * x + 2 * y + c
        my_chip = 2 * x + y
        others = [(x, y, 1 - c), (1 - x, y, c), (1 - x, y, 1 - c), (x, 1 - y, c), (x, 1 - y, 1 - c),
                  (1 - x, 1 - y, c), (1 - x, 1 - y, 1 - c)]
        local = []
        for k in range(n):
            cp = pltpu.make_async_copy(g_refs[k].at[my_chip], o_refs[k].at[me], local_sems.at[k])
            cp.start()
            local.append(cp)
        cp = pltpu.make_async_copy(small_ref, osmall_ref.at[me], local_sems.at[n])
        cp.start()
        local.append(cp)
        sends = []
        for r, (px, py, pc) in enumerate(others):
            for k in range(n + 1):
                src = small_ref if k == n else g_refs[k].at[2 * px + py]
                dst = osmall_ref.at[me] if k == n else o_refs[k].at[me]
                cp = pltpu.make_async_remote_copy(
                    src_ref=src, dst_ref=dst, send_sem=send_sems.at[7 * k + r], recv_sem=recv_sems.at[7 * k + r],
                    device_id=(px, py, pc), device_id_type=MESH)
                cp.start()
                sends.append(cp)
        for r, (px, py, pc) in enumerate(others):
            frm = 4 * px + 2 * py + pc
            for k in range(n + 1):
                src = small_ref if k == n else g_refs[k].at[my_chip]
                dst = osmall_ref.at[frm] if k == n else o_refs[k].at[frm]
                pltpu.make_async_remote_copy(
                    src_ref=src, dst_ref=dst, send_sem=send_sems.at[7 * k + r], recv_sem=recv_sems.at[7 * k + r],
                    device_id=(px, py, pc), device_id_type=MESH).wait_recv()
        for cp in sends:
            cp.wait_send()
        for cp in local:
            cp.wait()

    out_shape = [jax.ShapeDtypeStruct((N_DEV,) + g.shape[1:], g.dtype) for g in grads]
    out_shape.append(jax.ShapeDtypeStruct((N_DEV,) + small.shape, small.dtype))
    res = pl.pallas_call(
        body, in_specs=[ANY] * (n + 1), out_specs=[ANY] * (n + 1), out_shape=out_shape,
        scratch_shapes=[pltpu.SemaphoreType.DMA((7 * (n + 1),)), pltpu.SemaphoreType.DMA((7 * (n + 1),)),
                        pltpu.SemaphoreType.DMA((n + 1,))],
        name="exchange_grads",
    )(*grads, small)
    return res[:n], res[n]


def _adamw_math(w, g, m, v):
    m = ADAM_B1 * m + (1.0 - ADAM_B1) * g
    v = ADAM_B2 * v + (1.0 - ADAM_B2) * (g * g)
    m_hat = m / (1.0 - ADAM_B1 ** ADAM_STEP)
    v_hat = v / (1.0 - ADAM_B2 ** ADAM_STEP)
    delta = -ADAM_LR * (m_hat / (jnp.sqrt(v_hat) + ADAM_EPS) + ADAM_WD * w)
    return delta, m, v


def _adamw_big(parts, w, m, v, name):
    r, c = w.shape
    tr = 128

    def body(p_ref, w_ref, m_ref, v_ref, g_ref, d_ref, nm_ref, nv_ref):
        g = p_ref[0].astype(F32)
        for d in range(1, N_DEV):
            g = g + p_ref[d].astype(F32)
        g_ref[...] = g
        d_ref[...], nm_ref[...], nv_ref[...] = _adamw_math(w_ref[...], g, m_ref[...], v_ref[...])

    blk = pl.BlockSpec((tr, c), lambda i: (i, 0))
    return pl.pallas_call(
        body, grid=(r // tr,),
        in_specs=[pl.BlockSpec((N_DEV, tr, c), lambda i: (0, i, 0)), blk, blk, blk],
        out_specs=[blk, blk, blk, blk], out_shape=[jax.ShapeDtypeStruct((r, c), F32)] * 4,
        compiler_params=_params(("arbitrary",)), name=name,
    )(parts, w, m, v)


def _sum_small(parts):
    def body(p_ref, o_ref):
        g = p_ref[0]
        for d in range(1, N_DEV):
            g = g + p_ref[d]
        o_ref[...] = g

    return pl.pallas_call(body, out_shape=jax.ShapeDtypeStruct(parts.shape[1:], F32), name="sum_small")(parts)


def _adamw_small(w, g, m, v):
    def body(w_ref, g_ref, m_ref, v_ref, d_ref, nm_ref, nv_ref):
        d_ref[...], nm_ref[...], nv_ref[...] = _adamw_math(w_ref[...], g_ref[...], m_ref[...], v_ref[...])

    return pl.pallas_call(body, out_shape=[jax.ShapeDtypeStruct(w.shape, F32)] * 3, name="adamw_small")(w, g, m, v)


def _pack(arrs):
    flat = jnp.concatenate([a.reshape(-1) for a in arrs])
    rows = -(-flat.shape[0] // 1024) * 8
    return jnp.pad(flat, (0, rows * 128 - flat.shape[0])).reshape(rows, 128)


def _unpack(buf, shapes):
    flat = buf.reshape(-1)
    out, off = [], 0
    for s in shapes:
        size = 1
        for d in s:
            size *= d
        out.append(flat[off:off + size].reshape(s))
        off += size
    return out


def _blockdiag_dense(w):
    wr = w.reshape(HEADS, 32, 4, 4)
    eye = jnp.eye(32, dtype=w.dtype)
    return (wr[:, :, :, None, :] * eye[None, :, None, :, None]).reshape(HEADS, 128, 128)


def _blockdiag_extract(d):
    dr = d.reshape(HEADS, 32, 4, 32, 4)
    eye = jnp.eye(32, dtype=d.dtype)
    return jnp.sum(dr * eye[None, :, None, :, None], axis=3).reshape(128, 4, 4)


def _cols(a4):
    return jnp.transpose(a4, (1, 0, 2)).reshape(a4.shape[1], -1)


def _col_shards(a):
    r = a.shape[0]
    return jnp.transpose(a.reshape(r, N_CHIP, -1), (1, 0, 2))


def _local_step(x, target, w, sp):
    sp = {n: (a.reshape(1, -1) if a.ndim == 1 else a) for n, a in sp.items()}
    wau = jnp.zeros((128, 256), F32).at[0:16].set(sp["w_a_up"])
    wif = jnp.zeros((1536, 128), F32).at[:, 0:8].set(sp["w_if"])
    bif = jnp.zeros((1, 128), F32).at[:, 0:8].set(sp["b_if"])
    p = {"wau": wau, "bau": sp["b_a_up"], "ggla": sp["g_gla_norm"], "cw": sp["conv_w"], "cb": sp["conv_b"],
         "wq": _blockdiag_dense(sp["w_q_ml"]), "wk": _blockdiag_dense(sp["w_k_ml"]), "wv": _blockdiag_dense(sp["w_v_ml"]),
         "wif": wif, "bif": bif, "skip": sp["ml_skip"], "gml": sp["g_ml_norm"]}
    w_in = w["w_in"]
    w_mix = jnp.concatenate([w_in[:, 0:1536], w_in[:, 1552:2576], w_in[:, 1536:1552], jnp.zeros((D_MODEL, 112), BF16)], axis=1)
    w_ab = w_in[:, 2576:4624]

    pm, gab, h = _in_proj(x, sp["g_pre_mix"], w_mix, w_ab)
    ab, *states = _mixer_fwd(pm, p)
    x1, mix, merged = _merge_fwd(ab, gab, x, w["w_pa"], w["w_pb"], w["w_o"], sp["g_post_mix"])
    dx1, u, dd, h2, dpre, dg_post_mlp, dg_pre_mlp, loss = _mlp(x1, target, sp["g_pre_mlp"], sp["g_post_mlp"],
                                                                w["w_up"], w["w_down"])
    dmix, dya, dyb, dgab, dab, dg_post_mix = _merge_bwd(dx1, mix, ab, gab, w["w_pa"], w["w_pb"], w["w_o"], sp["g_post_mix"])
    dpm, dp = _mixer_bwd(pm, dab, states, p)
    dx, dg_pre_mix = _in_proj_bwd(dpm, dgab, x, dx1, sp["g_pre_mix"], w_mix, w_ab)

    d_mix = _tn_matmul(h, dpm, "dw_in_mix")
    d_ab = _tn_matmul(h, dgab, "dw_in_gates")
    big = {
        "w_in": jnp.concatenate([d_mix[:, 0:1536], d_mix[:, 2560:2576], d_mix[:, 1536:2560], d_ab], axis=1),
        "w_pa": _tn_matmul(ab[:, 0:512], dya, "dw_pa"),
        "w_pb": _tn_matmul(ab[:, 512:1024], dyb, "dw_pb"),
        "w_o": _tn_matmul(merged, dmix, "dw_o"),
        "w_up": _tn_matmul(h2, dpre, "dw_up"),
        "w_down": _tn_matmul(u, dd, "dw_down"),
    }
    small = {
        "g_pre_mix": dg_pre_mix, "b_a_up": dp["bau"], "g_gla_norm": dp["ggla"], "conv_b": dp["cb"],
        "w_q_ml": _blockdiag_extract(dp["wq"]), "w_k_ml": _blockdiag_extract(dp["wk"]), "w_v_ml": _blockdiag_extract(dp["wv"]),
        "b_if": dp["bif"][:, 0:8], "ml_skip": dp["skip"], "g_ml_norm": dp["gml"], "g_post_mix": dg_post_mix,
        "g_pre_mlp": dg_pre_mlp, "g_post_mlp": dg_post_mlp, "w_a_up": dp["wau"][0:16], "conv_w": dp["cw"],
        "w_if": dp["wif"][:, 0:8], "loss": loss[:, 0:1],
    }
    return dx, big, small


_BIG = ("w_in", "w_pa", "w_pb", "w_o", "w_up", "w_down")
_BIG_ROW_SHARDED = ("w_o", "w_down")
_SMALL_REPL = ("g_pre_mix", "b_a_up", "g_gla_norm", "conv_b", "w_q_ml", "w_k_ml", "w_v_ml", "b_if", "ml_skip",
               "g_ml_norm", "g_post_mix", "g_pre_mlp", "g_post_mlp")
_SMALL_SHARDED = ("w_a_up", "conv_w", "w_if")
_SMALL_ORDER = _SMALL_REPL + _SMALL_SHARDED + ("loss",)
_WEIGHTS = ("g_pre_mix", "w_in", "w_a_up", "b_a_up", "g_gla_norm", "conv_w", "conv_b", "w_q_ml", "w_k_ml", "w_v_ml",
            "w_if", "b_if", "ml_skip", "g_ml_norm", "w_pa", "w_pb", "w_o", "g_post_mix", "g_pre_mlp", "w_up", "w_down",
            "g_post_mlp")


def kernel(x, g_pre_mix, w_in, w_a_up, b_a_up, g_gla_norm, conv_w, conv_b, w_q_ml, w_k_ml, w_v_ml, w_if, b_if, ml_skip, g_ml_norm, w_pa, w_pb, w_o, g_post_mix, g_pre_mlp, w_up, w_down, g_post_mlp, loss_target, m_g_pre_mix, m_w_in, m_w_a_up, m_b_a_up, m_g_gla_norm, m_conv_w, m_conv_b, m_w_q_ml, m_w_k_ml, m_w_v_ml, m_w_if, m_b_if, m_ml_skip, m_g_ml_norm, m_w_pa, m_w_pb, m_w_o, m_g_post_mix, m_g_pre_mlp, m_w_up, m_w_down, m_g_post_mlp, v_g_pre_mix, v_w_in, v_w_a_up, v_b_a_up, v_g_gla_norm, v_conv_w, v_conv_b, v_w_q_ml, v_w_k_ml, v_w_v_ml, v_w_if, v_b_if, v_ml_skip, v_g_ml_norm, v_w_pa, v_w_pb, v_w_o, v_g_post_mix, v_g_pre_mlp, v_w_up, v_w_down, v_g_post_mlp):
    args = dict(locals())
    wts = {n: args[n][0] for n in _WEIGHTS}
    mom = {n: args["m_" + n][0] for n in _WEIGHTS}
    var = {n: args["v_" + n][0] for n in _WEIGHTS}
    chip = 2 * lax.axis_index("x") + lax.axis_index("y")

    shards = [wts[n].astype(BF16) for n in _BIG] + [wts[n] for n in _SMALL_SHARDED]
    gathered = dict(zip(_BIG + _SMALL_SHARDED, _gather_shards(shards)))
    full = {}
    for n in _BIG:
        a4 = gathered[n]
        full[n] = a4.reshape(-1, a4.shape[2]) if n in _BIG_ROW_SHARDED else _cols(a4)
    sp = {n: wts[n] for n in _SMALL_REPL}
    sp["w_a_up"] = _cols(gathered["w_a_up"])
    sp["conv_w"] = _cols(gathered["conv_w"])
    sp["w_if"] = gathered["w_if"].reshape(1536, 8)

    dx, big, small = _local_step(x[0], loss_target[0], full, sp)

    pieces = []
    for n in _BIG:
        g = big[n]
        pieces.append(g.reshape(N_CHIP, -1, g.shape[1]) if n in _BIG_ROW_SHARDED else _col_shards(g))
    small_shapes = [small[n].shape for n in _SMALL_ORDER]
    parts, small_parts = _exchange_grads(pieces, _pack([small[n] for n in _SMALL_ORDER]))

    grads, delta, new_m, new_v = {}, {}, {}, {}
    for n, p8 in zip(_BIG, parts):
        grads[n], delta[n], new_m[n], new_v[n] = _adamw_big(p8, wts[n], mom[n], var[n], "adamw_" + n)
    summed = dict(zip(_SMALL_ORDER, _unpack(_sum_small(small_parts), small_shapes)))
    loss = summed["loss"].reshape(())
    for n in _SMALL_REPL:
        grads[n] = summed[n].reshape(wts[n].shape)
    grads["w_a_up"] = lax.dynamic_slice_in_dim(summed["w_a_up"], chip * 64, 64, axis=1)
    grads["conv_w"] = lax.dynamic_slice_in_dim(summed["conv_w"], chip * 128, 128, axis=1)
    grads["w_if"] = lax.dynamic_slice_in_dim(summed["w_if"], chip * 384, 384, axis=0)
    small_names = _SMALL_REPL + _SMALL_SHARDED
    shard_shapes = [wts[n].shape for n in small_names]
    upd = _adamw_small(_pack([wts[n] for n in small_names]), _pack([grads[n] for n in small_names]),
                       _pack([mom[n] for n in small_names]), _pack([var[n] for n in small_names]))
    for dst, buf in zip((delta, new_m, new_v), upd):
        for n, a in zip(small_names, _unpack(buf, shard_shapes)):
            dst[n] = a

    outs = [loss, dx[None]]
    for group in (grads, delta, new_m, new_v):
        outs += [group[n][None] for n in _WEIGHTS]
    return tuple(outs)
```

```python
import functools

import jax
import jax.numpy as jnp
from jax import lax
from jax.experimental import pallas as pl
from jax.experimental.pallas import tpu as pltpu

F32 = jnp.float32
BF16 = jnp.bfloat16

SEQ = 2048
D_MODEL = 1024
CHUNK = 64
N_CHUNK = SEQ // CHUNK
HEADS = 4
GLA_DK = 64
GLA_DV = 128
ML_DH = 128
D_FF = 4096
EPS = 1e-6
N_CHIP = 4
N_DEV = 8
TOK_TILE = 256
N_TOK_TILE = SEQ // TOK_TILE

PM_W = 2688
PM_XM = 1536
PM_OP = 2048
PM_AL = 2560
GAB_W = 2048

ADAM_LR = 0.001
ADAM_B1 = 0.9
ADAM_B2 = 0.999
ADAM_EPS = 1e-08
ADAM_WD = 0.01
ADAM_STEP = 10

VMEM_LIMIT = 56 * 1024 * 1024


def _params(sem=None):
    return pltpu.CompilerParams(dimension_semantics=sem, vmem_limit_bytes=VMEM_LIMIT)


def _dot(a, b, ca, cb):
    return lax.dot_general(a.astype(BF16), b.astype(BF16), (((ca,), (cb,)), ((), ())), preferred_element_type=F32)


def _pmm_nn(a, b):
    return _dot(a, b, 1, 0)


def _pmm_nt(a, b):
    return _dot(a, b, 1, 1)


def _pmm_tn(a, b):
    return _dot(a, b, 0, 0)


def _pcmm(c, x):
    return lax.dot_general(c, x, (((1,), (0,)), ((), ())), precision=lax.Precision.HIGHEST, preferred_element_type=F32)


@jax.custom_vjp
def _mm_nn(a, b):
    return _dot(a, b, 1, 0)


@jax.custom_vjp
def _mm_nt(a, b):
    return _dot(a, b, 1, 1)


@jax.custom_vjp
def _mm_tn(a, b):
    return _dot(a, b, 0, 0)


_mm_nn.defvjp(lambda a, b: (_dot(a, b, 1, 0), (a, b)), lambda r, g: (_mm_nt(g, r[1]), _mm_tn(r[0], g)))
_mm_nt.defvjp(lambda a, b: (_dot(a, b, 1, 1), (a, b)), lambda r, g: (_mm_nn(g, r[1]), _mm_tn(g, r[0])))
_mm_tn.defvjp(lambda a, b: (_dot(a, b, 0, 0), (a, b)), lambda r, g: (_mm_nt(r[1], g), _mm_nn(r[0], g)))


@jax.custom_vjp
def _cmm(c, x):
    return _pcmm(c, x)


_cmm.defvjp(
    lambda c, x: (_pcmm(c, x), c),
    lambda c, g: (jnp.zeros_like(c), lax.dot_general(c, g, (((0,), (0,)), ((), ())), precision=lax.Precision.HIGHEST,
                                                      preferred_element_type=F32)),
)

_PLAIN_OPS = (_pmm_nn, _pmm_nt, _pmm_tn, _pcmm)
_VJP_OPS = (_mm_nn, _mm_nt, _mm_tn, _cmm)


def _sigmoid(x):
    return 0.5 * (jnp.tanh(0.5 * x) + 1.0)


def _log_sigmoid(x):
    return jnp.minimum(x, 0.0) - jnp.log(1.0 + jnp.exp(-jnp.abs(x)))


def _mean(x):
    return jnp.mean(x, axis=-1, keepdims=True)


def _mixer_chunk(ops, p, st, pm, xprev8):
    mm_nn, mm_nt, mm_tn, cmm = ops
    row = lax.broadcasted_iota(jnp.int32, (CHUNK, CHUNK), 0)
    col = lax.broadcasted_iota(jnp.int32, (CHUNK, CHUNK), 1)
    causal = row >= col
    tri = causal.astype(F32)
    q = pm[:, 0:256]
    k = pm[:, 256:512]
    v = pm[:, 512:1024]
    g = pm[:, 1024:1536]
    xm = pm[:, PM_XM:PM_XM + 512]
    opre = pm[:, PM_OP:PM_OP + 512]
    alow = pm[:, PM_AL:PM_AL + 128]

    la = _log_sigmoid(mm_nn(alow, p["wau"]) + p["bau"]) * (1.0 / 16.0)
    cum = cmm(tri, la)
    cum_last = cum[CHUNK - 1:CHUNK, :]
    e_pos = jnp.exp(cum)
    e_neg = jnp.exp(-cum)
    qs = q * (GLA_DK ** -0.5)
    qp = qs * e_pos
    qn = qs * e_neg
    kp = k * e_pos
    kn = k * e_neg
    kl = k * jnp.exp(cum_last - cum)
    dec = jnp.exp(cum_last)
    outs = []
    s_new = []
    for h in range(HEADS):
        s6 = slice(h * GLA_DK, (h + 1) * GLA_DK)
        s12 = slice(h * GLA_DV, (h + 1) * GLA_DV)
        scores = jnp.where(causal, mm_nt(qp[:, s6], kn[:, s6]), mm_nt(qn[:, s6], kp[:, s6]))
        o = mm_nn(scores, v[:, s12]) + mm_nt(qp[:, s6], st["S"][h])
        s_new.append(st["S"][h] * dec[:, s6] + mm_tn(v[:, s12], kl[:, s6]))
        o = o * lax.rsqrt(_mean(o * o) + EPS) * p["ggla"]
        gh = g[:, s12]
        outs.append(o * (gh * _sigmoid(gh)))

    xx = jnp.concatenate([xprev8, xm], axis=0)
    pre = p["cb"]
    for j in range(4):
        pre = pre + p["cw"][j:j + 1, :] * xx[5 + j:5 + j + CHUNK, :]
    xc = pre * _sigmoid(pre)
    qm, km, vm = [], [], []
    for h in range(HEADS):
        s12 = slice(h * ML_DH, (h + 1) * ML_DH)
        qm.append(mm_nn(xc[:, s12], p["wq"][h]))
        km.append(mm_nn(xc[:, s12], p["wk"][h]))
        vm.append(mm_nn(xm[:, s12], p["wv"][h]))
    qcat = jnp.concatenate(qm, axis=1)
    kcat = jnp.concatenate(km, axis=1)
    vcat = jnp.concatenate(vm, axis=1)
    gates = (mm_nn(qcat, p["wif"][0:512]) + mm_nn(kcat, p["wif"][512:1024]) + mm_nn(vcat, p["wif"][1024:1536])
             + p["bif"])
    lf = _log_sigmoid(gates)
    fc = cmm(tri, lf)
    gates_t = gates.T
    fc_t = fc.T
    c_new, n_new, m_new = [], [], []
    for h in range(HEADS):
        s12 = slice(h * ML_DH, (h + 1) * ML_DH)
        li_c = gates[:, h:h + 1]
        fc_c = fc[:, 4 + h:5 + h]
        li_r = gates_t[h:h + 1, :]
        fc_r = fc_t[4 + h:5 + h, :]
        m_prev = st["m"][h][:, 0:1]
        log_d = li_r - jnp.abs(fc_c - fc_r)
        g_int = fc_c + m_prev
        m_t = jnp.maximum(g_int, jnp.max(log_d, axis=1, keepdims=True))
        ks = km[h] * (ML_DH ** -0.5)
        s = mm_nt(qm[h], ks) * jnp.exp(log_d - m_t)
        scl = jnp.exp(g_int - m_t)
        num = mm_nn(s, vm[h]) + scl * mm_nn(qm[h], st["C"][h])
        den = jnp.sum(s, axis=1, keepdims=True) + scl * jnp.sum(qm[h] * st["n"][h], axis=1, keepdims=True)
        den = jnp.maximum(jnp.abs(den), jnp.exp(-m_t))
        hc = num / den * _sigmoid(opre[:, s12])
        d0 = hc - _mean(hc)
        y = d0 * lax.rsqrt(_mean(d0 * d0) + EPS)
        outs.append(y * p["gml"][:, s12] + p["skip"][:, s12] * xc[:, s12])
        f_last = fc[CHUNK - 1:CHUNK, 4 + h:5 + h]
        a = f_last - fc_c + li_c
        m_loc = jnp.max(a, axis=0, keepdims=True)
        kw = ks * jnp.exp(a - m_loc)
        m_nx = jnp.maximum(f_last + m_prev, m_loc)
        sp = jnp.exp(f_last + m_prev - m_nx)
        sl = jnp.exp(m_loc - m_nx)
        c_new.append(sp * st["C"][h] + sl * mm_tn(kw, vm[h]))
        n_new.append(sp * st["n"][h] + sl * jnp.sum(kw, axis=0, keepdims=True))
        m_new.append(jnp.broadcast_to(m_nx, (1, ML_DH)))
    ab = jnp.concatenate(outs, axis=1)
    new = {"S": s_new, "C": c_new, "n": n_new, "m": m_new}
    return ab, new


_P_NAMES = ("wau", "bau", "ggla", "cw", "cb", "wq", "wk", "wv", "wif", "bif", "skip", "gml")
_P_SHAPES = {
    "wau": (128, 256), "bau": (1, 256), "ggla": (1, 128), "cw": (4, 512), "cb": (1, 512),
    "wq": (HEADS, 128, 128), "wk": (HEADS, 128, 128), "wv": (HEADS, 128, 128),
    "wif": (1536, 128), "bif": (1, 128), "skip": (1, 512), "gml": (1, 512),
}
_S_NAMES = ("S", "C", "n", "m")
_S_SHAPES = {"S": (HEADS, GLA_DV, GLA_DK), "C": (HEADS, ML_DH, ML_DH), "n": (HEADS, 1, ML_DH), "m": (HEADS, 1, ML_DH)}


_P_PER_HEAD = ("wq", "wk", "wv")


def _per_head(ref):
    return [ref[h] for h in range(HEADS)]


def _load_param(name, ref):
    return _per_head(ref) if name in _P_PER_HEAD else ref[...]


def _const_spec(shape):
    zeros = (0,) * len(shape)
    return pl.BlockSpec(shape, lambda i: zeros)


def _mixer_fwd(pm, p):
    n_p = len(_P_NAMES)

    def body(*refs):
        pm_ref, xprev_ref = refs[0], refs[1]
        p_refs = refs[2:2 + n_p]
        ab_ref = refs[2 + n_p]
        so_refs = refs[3 + n_p:7 + n_p]
        sc_refs = refs[7 + n_p:11 + n_p]
        n = pl.program_id(0)

        @pl.when(n == 0)
        def _():
            for r in sc_refs:
                r[...] = jnp.zeros_like(r)

        st = {name: _per_head(r) for name, r in zip(_S_NAMES, sc_refs)}
        for name, r in zip(_S_NAMES, so_refs):
            for h in range(HEADS):
                r[0, h] = st[name][h]
        pv = {name: _load_param(name, r) for name, r in zip(_P_NAMES, p_refs)}
        xprev8 = jnp.where(n > 0, xprev_ref[CHUNK - 8:CHUNK, :], 0.0)
        ab, new = _mixer_chunk(_PLAIN_OPS, pv, st, pm_ref[...], xprev8)
        ab_ref[...] = ab.astype(BF16)
        for name, r in zip(_S_NAMES, sc_refs):
            for h in range(HEADS):
                r[h] = new[name][h]

    in_specs = [pl.BlockSpec((CHUNK, PM_W), lambda i: (i, 0)),
                pl.BlockSpec((CHUNK, 512), lambda i: (jnp.maximum(i - 1, 0), PM_XM // 512))]
    in_specs += [_const_spec(_P_SHAPES[nm]) for nm in _P_NAMES]
    out_specs = [pl.BlockSpec((CHUNK, 1024), lambda i: (i, 0))]
    out_shape = [jax.ShapeDtypeStruct((SEQ, 1024), BF16)]
    for nm in _S_NAMES:
        shp = _S_SHAPES[nm]
        out_specs.append(pl.BlockSpec((1,) + shp, lambda i: (i, 0, 0, 0)))
        out_shape.append(jax.ShapeDtypeStruct((N_CHUNK,) + shp, F32))
    return pl.pallas_call(
        body, grid=(N_CHUNK,), in_specs=in_specs, out_specs=out_specs, out_shape=out_shape,
        scratch_shapes=[pltpu.VMEM(_S_SHAPES[nm], F32) for nm in _S_NAMES],
        compiler_params=_params(("arbitrary",)), name="mixer_fwd",
    )(pm, pm, *[p[nm] for nm in _P_NAMES])


def _mixer_bwd(pm, dab, states, p):
    n_p = len(_P_NAMES)

    def body(*refs):
        pm_ref, xprev_ref, dab_ref = refs[0], refs[1], refs[2]
        si_refs = refs[3:7]
        p_refs = refs[7:7 + n_p]
        dpm_ref = refs[7 + n_p]
        dp_refs = refs[8 + n_p:8 + 2 * n_p]
        ds_refs = refs[8 + 2 * n_p:12 + 2 * n_p]
        carry_ref = refs[12 + 2 * n_p]
        i = pl.program_id(0)
        n = N_CHUNK - 1 - i

        @pl.when(i == 0)
        def _():
            for r in ds_refs:
                r[...] = jnp.zeros_like(r)
            for r in dp_refs:
                r[...] = jnp.zeros_like(r)
            carry_ref[...] = jnp.zeros_like(carry_ref)

        st = {name: [r[0, h] for h in range(HEADS)] for name, r in zip(_S_NAMES, si_refs)}
        pv = {name: _load_param(name, r) for name, r in zip(_P_NAMES, p_refs)}
        xprev8 = jnp.where(n > 0, xprev_ref[CHUNK - 8:CHUNK, :], 0.0)
        _, vjp = jax.vjp(functools.partial(_mixer_chunk, _VJP_OPS), pv, st, pm_ref[...], xprev8)
        dst = {name: _per_head(r) for name, r in zip(_S_NAMES, ds_refs)}
        dp, dst_prev, dpm, dxprev8 = vjp((dab_ref[...], dst))
        reach = jnp.concatenate([jnp.zeros((CHUNK - 8, 512), F32), carry_ref[...]], axis=0)
        dpm_ref[:, 0:PM_XM] = dpm[:, 0:PM_XM].astype(BF16)
        dpm_ref[:, PM_XM:PM_XM + 512] = (dpm[:, PM_XM:PM_XM + 512] + reach).astype(BF16)
        dpm_ref[:, PM_XM + 512:PM_W] = dpm[:, PM_XM + 512:PM_W].astype(BF16)
        carry_ref[...] = dxprev8
        for name, r in zip(_S_NAMES, ds_refs):
            for h in range(HEADS):
                r[h] = dst_prev[name][h]
        for name, r in zip(_P_NAMES, dp_refs):
            if name in _P_PER_HEAD:
                for h in range(HEADS):
                    r[h] += dp[name][h]
            else:
                r[...] += dp[name]

    rev = lambda i: (N_CHUNK - 1 - i, 0)
    in_specs = [pl.BlockSpec((CHUNK, PM_W), rev),
                pl.BlockSpec((CHUNK, 512), lambda i: (jnp.maximum(N_CHUNK - 2 - i, 0), PM_XM // 512)),
                pl.BlockSpec((CHUNK, 1024), rev)]
    for nm in _S_NAMES:
        in_specs.append(pl.BlockSpec((1,) + _S_SHAPES[nm], lambda i: (N_CHUNK - 1 - i, 0, 0, 0)))
    in_specs += [_const_spec(_P_SHAPES[nm]) for nm in _P_NAMES]
    out_specs = [pl.BlockSpec((CHUNK, PM_W), rev)] + [_const_spec(_P_SHAPES[nm]) for nm in _P_NAMES]
    out_shape = [jax.ShapeDtypeStruct((SEQ, PM_W), BF16)] + [jax.ShapeDtypeStruct(_P_SHAPES[nm], F32) for nm in _P_NAMES]
    res = pl.pallas_call(
        body, grid=(N_CHUNK,), in_specs=in_specs, out_specs=out_specs, out_shape=out_shape,
        scratch_shapes=[pltpu.VMEM(_S_SHAPES[nm], F32) for nm in _S_NAMES] + [pltpu.VMEM((8, 512), F32)],
        compiler_params=_params(("arbitrary",)), name="mixer_bwd",
    )(pm, pm, dab, *states, *[p[nm] for nm in _P_NAMES])
    return res[0], dict(zip(_P_NAMES, res[1:]))


def _tok(width):
    return pl.BlockSpec((TOK_TILE, width), lambda i: (i, 0))


def _once(shape):
    zeros = (0,) * len(shape)
    return pl.BlockSpec(shape, lambda i: zeros, pipeline_mode=pl.Buffered(1))


def _rms_fwd(x):
    r = lax.rsqrt(_mean(x * x) + EPS)
    return x * r, r


def _rms_bwd(dy, xn, r, g):
    gd = dy * g
    return r * (gd - xn * _mean(xn * gd))


def _in_proj(x, g_pre, w_mix, w_ab):
    def body(x_ref, g_ref, wm_ref, wa_ref, pm_ref, gab_ref, h_ref):
        xn, _ = _rms_fwd(x_ref[...])
        h = (xn * g_ref[...]).astype(BF16)
        h_ref[...] = h
        pm_ref[...] = jnp.dot(h, wm_ref[...], preferred_element_type=F32)
        gab_ref[...] = jnp.dot(h, wa_ref[...], preferred_element_type=F32)

    return pl.pallas_call(
        body, grid=(N_TOK_TILE,),
        in_specs=[_tok(D_MODEL), _once((1, D_MODEL)), _once((D_MODEL, PM_W)), _once((D_MODEL, GAB_W))],
        out_specs=[_tok(PM_W), _tok(GAB_W), _tok(D_MODEL)],
        out_shape=[jax.ShapeDtypeStruct((SEQ, PM_W), F32), jax.ShapeDtypeStruct((SEQ, GAB_W), F32),
                   jax.ShapeDtypeStruct((SEQ, D_MODEL), BF16)],
        compiler_params=_params(("arbitrary",)), name="in_proj",
    )(x, g_pre, w_mix, w_ab)


def _merge_fwd(ab, gab, x, w_pa, w_pb, w_o, g_post):
    def body(ab_ref, gab_ref, x_ref, wpa_ref, wpb_ref, wo_ref, g_ref, x1_ref, mix_ref, mg_ref):
        ya = jnp.dot(ab_ref[:, 0:512], wpa_ref[...], preferred_element_type=F32)
        yb = jnp.dot(ab_ref[:, 512:1024], wpb_ref[...], preferred_element_type=F32)
        merged = (_sigmoid(gab_ref[:, 0:1024]) * ya + _sigmoid(gab_ref[:, 1024:2048]) * yb).astype(BF16)
        mg_ref[...] = merged
        mix = jnp.dot(merged, wo_ref[...], preferred_element_type=F32)
        mix_ref[...] = mix
        mn, _ = _rms_fwd(mix)
        x1_ref[...] = x_ref[...] + mn * g_ref[...]

    return pl.pallas_call(
        body, grid=(N_TOK_TILE,),
        in_specs=[_tok(1024), _tok(GAB_W), _tok(D_MODEL), _once((512, D_MODEL)), _once((512, D_MODEL)),
                  _once((D_MODEL, D_MODEL)), _once((1, D_MODEL))],
        out_specs=[_tok(D_MODEL), _tok(D_MODEL), _tok(D_MODEL)],
        out_shape=[jax.ShapeDtypeStruct((SEQ, D_MODEL), F32), jax.ShapeDtypeStruct((SEQ, D_MODEL), F32),
                   jax.ShapeDtypeStruct((SEQ, D_MODEL), BF16)],
        compiler_params=_params(("arbitrary",)), name="merge_fwd",
    )(ab, gab, x, w_pa, w_pb, w_o, g_post)


def _mlp(x1, target, g_pre, g_post, w_up, w_down):
    def body(x1_ref, t_ref, gpre_ref, gpost_ref, wup_ref, wdn_ref,
             dx1_ref, u_ref, dd_ref, h2_ref, dpre_ref, dgpost_ref, dgpre_ref, loss_ref):
        @pl.when(pl.program_id(0) == 0)
        def _():
            dgpost_ref[...] = jnp.zeros_like(dgpost_ref)
            dgpre_ref[...] = jnp.zeros_like(dgpre_ref)
            loss_ref[...] = jnp.zeros_like(loss_ref)

        x1 = x1_ref[...]
        gpre = gpre_ref[...]
        gpost = gpost_ref[...]
        xn2, r2 = _rms_fwd(x1)
        h2 = (xn2 * gpre).astype(BF16)
        h2_ref[...] = h2
        pre = jnp.dot(h2, wup_ref[...], preferred_element_type=F32)
        rl = jnp.maximum(pre, 0.0)
        u = (rl * rl).astype(BF16)
        u_ref[...] = u
        d = jnp.dot(u, wdn_ref[...], preferred_element_type=F32)
        dn, r3 = _rms_fwd(d)
        diff = x1 + dn * gpost - t_ref[...]
        loss_ref[...] += jnp.sum(diff * diff, keepdims=True).reshape(1, 1) * (0.5 / D_MODEL)
        dy = diff * (1.0 / D_MODEL)
        dgpost_ref[...] += jnp.sum(dy * dn, axis=0, keepdims=True)
        dd = _rms_bwd(dy, dn, r3, gpost).astype(BF16)
        dd_ref[...] = dd
        du = lax.dot_general(dd, wdn_ref[...], (((1,), (1,)), ((), ())), preferred_element_type=F32)
        dpre = (du * (2.0 * rl)).astype(BF16)
        dpre_ref[...] = dpre
        dh2 = lax.dot_general(dpre, wup_ref[...], (((1,), (1,)), ((), ())), preferred_element_type=F32)
        dgpre_ref[...] += jnp.sum(dh2 * xn2, axis=0, keepdims=True)
        dx1_ref[...] = dy + _rms_bwd(dh2, xn2, r2, gpre)

    acc = pl.BlockSpec((1, D_MODEL), lambda i: (0, 0))
    return pl.pallas_call(
        body, grid=(N_TOK_TILE,),
        in_specs=[_tok(D_MODEL), _tok(D_MODEL), _once((1, D_MODEL)), _once((1, D_MODEL)),
                  _once((D_MODEL, D_FF)), _once((D_FF, D_MODEL))],
        out_specs=[_tok(D_MODEL), _tok(D_FF), _tok(D_MODEL), _tok(D_MODEL), _tok(D_FF), acc, acc,
                   pl.BlockSpec((1, 128), lambda i: (0, 0))],
        out_shape=[jax.ShapeDtypeStruct((SEQ, D_MODEL), F32), jax.ShapeDtypeStruct((SEQ, D_FF), BF16),
                   jax.ShapeDtypeStruct((SEQ, D_MODEL), BF16), jax.ShapeDtypeStruct((SEQ, D_MODEL), BF16),
                   jax.ShapeDtypeStruct((SEQ, D_FF), BF16), jax.ShapeDtypeStruct((1, D_MODEL), F32),
                   jax.ShapeDtypeStruct((1, D_MODEL), F32), jax.ShapeDtypeStruct((1, 128), F32)],
        compiler_params=_params(("arbitrary",)), name="mlp_fwd_bwd",
    )(x1, target, g_pre, g_post, w_up, w_down)


def _merge_bwd(dx1, mix, ab, gab, w_pa, w_pb, w_o, g_post):
    def body(dx1_ref, mix_ref, ab_ref, gab_ref, wpa_ref, wpb_ref, wo_ref, g_ref,
             dmix_ref, dya_ref, dyb_ref, dgab_ref, dab_ref, dg_ref):
        @pl.when(pl.program_id(0) == 0)
        def _():
            dg_ref[...] = jnp.zeros_like(dg_ref)

        dx1 = dx1_ref[...]
        mn, r = _rms_fwd(mix_ref[...])
        dg_ref[...] += jnp.sum(dx1 * mn, axis=0, keepdims=True)
        dmix = _rms_bwd(dx1, mn, r, g_ref[...]).astype(BF16)
        dmix_ref[...] = dmix
        dmerged = lax.dot_general(dmix, wo_ref[...], (((1,), (1,)), ((), ())), preferred_element_type=F32)
        ya = jnp.dot(ab_ref[:, 0:512], wpa_ref[...], preferred_element_type=F32)
        yb = jnp.dot(ab_ref[:, 512:1024], wpb_ref[...], preferred_element_type=F32)
        sa = _sigmoid(gab_ref[:, 0:1024])
        sb = _sigmoid(gab_ref[:, 1024:2048])
        dya = (dmerged * sa).astype(BF16)
        dyb = (dmerged * sb).astype(BF16)
        dya_ref[...] = dya
        dyb_ref[...] = dyb
        dgab_ref[:, 0:1024] = (dmerged * ya * sa * (1.0 - sa)).astype(BF16)
        dgab_ref[:, 1024:2048] = (dmerged * yb * sb * (1.0 - sb)).astype(BF16)
        dab_ref[:, 0:512] = lax.dot_general(dya, wpa_ref[...], (((1,), (1,)), ((), ())), preferred_element_type=F32)
        dab_ref[:, 512:1024] = lax.dot_general(dyb, wpb_ref[...], (((1,), (1,)), ((), ())), preferred_element_type=F32)

    return pl.pallas_call(
        body, grid=(N_TOK_TILE,),
        in_specs=[_tok(D_MODEL), _tok(D_MODEL), _tok(1024), _tok(GAB_W), _once((512, D_MODEL)), _once((512, D_MODEL)),
                  _once((D_MODEL, D_MODEL)), _once((1, D_MODEL))],
        out_specs=[_tok(D_MODEL), _tok(D_MODEL), _tok(D_MODEL), _tok(GAB_W), _tok(1024),
                   pl.BlockSpec((1, D_MODEL), lambda i: (0, 0))],
        out_shape=[jax.ShapeDtypeStruct((SEQ, D_MODEL), BF16), jax.ShapeDtypeStruct((SEQ, D_MODEL), BF16),
                   jax.ShapeDtypeStruct((SEQ, D_MODEL), BF16), jax.ShapeDtypeStruct((SEQ, GAB_W), BF16),
                   jax.ShapeDtypeStruct((SEQ, 1024), F32), jax.ShapeDtypeStruct((1, D_MODEL), F32)],
        compiler_params=_params(("arbitrary",)), name="merge_bwd",
    )(dx1, mix, ab, gab, w_pa, w_pb, w_o, g_post)


def _in_proj_bwd(dpm, dgab, x, dx1, g_pre, w_mix, w_ab):
    def body(dpm_ref, dgab_ref, x_ref, dx1_ref, g_ref, wm_ref, wa_ref, dx_ref, dg_ref):
        @pl.when(pl.program_id(0) == 0)
        def _():
            dg_ref[...] = jnp.zeros_like(dg_ref)

        dh = lax.dot_general(dpm_ref[...], wm_ref[...], (((1,), (1,)), ((), ())), preferred_element_type=F32)
        dh = dh + lax.dot_general(dgab_ref[...], wa_ref[...], (((1,), (1,)), ((), ())), preferred_element_type=F32)
        xn, r = _rms_fwd(x_ref[...])
        dg_ref[...] += jnp.sum(dh * xn, axis=0, keepdims=True)
        dx_ref[...] = dx1_ref[...] + _rms_bwd(dh, xn, r, g_ref[...])

    return pl.pallas_call(
        body, grid=(N_TOK_TILE,),
        in_specs=[_tok(PM_W), _tok(GAB_W), _tok(D_MODEL), _tok(D_MODEL), _once((1, D_MODEL)),
                  _once((D_MODEL, PM_W)), _once((D_MODEL, GAB_W))],
        out_specs=[_tok(D_MODEL), pl.BlockSpec((1, D_MODEL), lambda i: (0, 0))],
        out_shape=[jax.ShapeDtypeStruct((SEQ, D_MODEL), F32), jax.ShapeDtypeStruct((1, D_MODEL), F32)],
        compiler_params=_params(("arbitrary",)), name="in_proj_bwd",
    )(dpm, dgab, x, dx1, g_pre, w_mix, w_ab)


def _tn_matmul(a, b, name, tm=512, tn=None):
    m, n = a.shape[1], b.shape[1]
    tm = min(tm, m)
    if tn is None:
        tn = 896 if n == PM_W else min(n, 1024)

    def body(a_ref, b_ref, o_ref):
        o_ref[...] = lax.dot_general(a_ref[...], b_ref[...], (((0,), (0,)), ((), ())),
                                     preferred_element_type=F32).astype(BF16)

    return pl.pallas_call(
        body, grid=(m // tm, n // tn),
        in_specs=[pl.BlockSpec((SEQ, tm), lambda i, j: (0, i)), pl.BlockSpec((SEQ, tn), lambda i, j: (0, j))],
        out_specs=pl.BlockSpec((tm, tn), lambda i, j: (i, j)),
        out_shape=jax.ShapeDtypeStruct((m, n), BF16),
        compiler_params=_params(("arbitrary", "arbitrary")), name=name,
    )(a, b)


MESH = pl.DeviceIdType.MESH
ANY = pl.BlockSpec(memory_space=pl.ANY)


def _rows_half(ref, e, rows):
    h = rows // 2
    return ref.at[pl.ds(pl.multiple_of(e * h, 16), h)]


def _remote(src, dst, send_sems, recv_sems, k, to):
    return pltpu.make_async_remote_copy(src_ref=src, dst_ref=dst, send_sem=send_sems.at[k], recv_sem=recv_sems.at[k],
                                        device_id=to, device_id_type=MESH)


def _gather_shards(big, small):
    nb, n = len(big), len(big) + len(small)

    def body(*refs):
        ins, outs = refs[:n], refs[n:2 * n]
        send_sems, recv_sems, local_sems = refs[2 * n:]
        x, y, c = lax.axis_index("x"), lax.axis_index("y"), lax.axis_index("c")
        me = 2 * x + y
        sibling = (x, y, 1 - c)
        peers = [(1 - x, y), (x, 1 - y), (1 - x, 1 - y)]
        rows = [s.shape[0] for s in big]
        waits = []
        for k in range(n):
            cp = pltpu.make_async_copy(ins[k], outs[k].at[me], local_sems.at[k])
            cp.start()
            waits.append(cp.wait)
        for k in range(n):
            for j, (px, py) in enumerate(peers):
                if k < nb:
                    cp = _remote(_rows_half(ins[k], c, rows[k]), _rows_half(outs[k].at[me], c, rows[k]),
                                 send_sems, recv_sems, 6 * k + j, (px, py, c))
                else:
                    cp = _remote(ins[k], outs[k].at[me], send_sems, recv_sems, 6 * k + j, (px, py, c))
                cp.start()
                waits.append(cp.wait_send)
        for j, (px, py) in enumerate(peers):
            pj = 2 * px + py
            for k in range(nb):
                landed = _rows_half(outs[k].at[pj], c, rows[k])
                _remote(landed, landed, send_sems, recv_sems, 6 * k + j, (px, py, c)).wait_recv()
                cp = _remote(landed, landed, send_sems, recv_sems, 6 * k + 3 + j, sibling)
                cp.start()
                waits.append(cp.wait_send)
        for j, (px, py) in enumerate(peers):
            pj = 2 * px + py
            for k in range(n):
                if k < nb:
                    passed = _rows_half(outs[k].at[pj], 1 - c, rows[k])
                    _remote(passed, passed, send_sems, recv_sems, 6 * k + 3 + j, sibling).wait_recv()
                else:
                    _remote(ins[k], outs[k].at[pj], send_sems, recv_sems, 6 * k + j, (px, py, c)).wait_recv()
        for w in waits:
            w()

    shards = list(big) + list(small)
    return pl.pallas_call(
        body, in_specs=[ANY] * n, out_specs=[ANY] * n,
        out_shape=[jax.ShapeDtypeStruct((N_CHIP,) + s.shape, s.dtype) for s in shards],
        scratch_shapes=[pltpu.SemaphoreType.DMA((6 * n,)), pltpu.SemaphoreType.DMA((6 * n,)), pltpu.SemaphoreType.DMA((n,))],
        name="gather_weights",
    )(*shards)


def _piece_halves(ref, e, rows):
    h = rows // 2
    return ref.at[pl.ds(0, N_CHIP), pl.ds(pl.multiple_of(e * h, 16), h)]


def _swap_halves(grads):
    n = len(grads)

    def body(*refs):
        g_refs, own_refs, got_refs = refs[:n], refs[n:2 * n], refs[2 * n:3 * n]
        send_sems, recv_sems, local_sems = refs[3 * n:]
        x, y, c = lax.axis_index("x"), lax.axis_index("y"), lax.axis_index("c")
        waits = []
        for k in range(n):
            rows = grads[k].shape[1]
            cp = pltpu.make_async_copy(_piece_halves(g_refs[k], c, rows), own_refs[k], local_sems.at[k])
            cp.start()
            waits.append(cp.wait)
            cp = _remote(_piece_halves(g_refs[k], 1 - c, rows), got_refs[k], send_sems, recv_sems, k, (x, y, 1 - c))
            cp.start()
            waits.append(cp.wait)
        for w in waits:
            w()

    half = [jax.ShapeDtypeStruct((N_CHIP, g.shape[1] // 2, g.shape[2]), g.dtype) for g in grads]
    res = pl.pallas_call(
        body, in_specs=[ANY] * n, out_specs=[ANY] * (2 * n), out_shape=half + half,
        scratch_shapes=[pltpu.SemaphoreType.DMA((n,)), pltpu.SemaphoreType.DMA((n,)), pltpu.SemaphoreType.DMA((n,))],
        name="swap_halves",
    )(*grads)
    return res[:n], res[n:]


def _send_partials(partials, small):
    n = len(partials)

    def body(*refs):
        q_refs, small_ref = refs[:n], refs[n]
        o_refs, osmall_ref = refs[n + 1:2 * n + 1], refs[2 * n + 1]
        send_sems, recv_sems, local_sems = refs[2 * n + 2:]
        x, y, c = lax.axis_index("x"), lax.axis_index("y"), lax.axis_index("c")
        me = 2 * x + y
        me8 = 4 * x + 2 * y + c
        peers = [(1 - x, y), (x, 1 - y), (1 - x, 1 - y)]
        others = [(x, y, 1 - c), (1 - x, y, c), (1 - x, y, 1 - c), (x, 1 - y, c), (x, 1 - y, 1 - c),
                  (1 - x, 1 - y, c), (1 - x, 1 - y, 1 - c)]
        waits = []
        for k in range(n):
            cp = pltpu.make_async_copy(q_refs[k].at[me], o_refs[k].at[me], local_sems.at[k])
            cp.start()
            waits.append(cp.wait)
        cp = pltpu.make_async_copy(small_ref, osmall_ref.at[me8], local_sems.at[n])
        cp.start()
        waits.append(cp.wait)
        for k in range(n):
            for j, (px, py) in enumerate(peers):
                cp = _remote(q_refs[k].at[2 * px + py], o_refs[k].at[me], send_sems, recv_sems, 3 * k + j, (px, py, c))
                cp.start()
                waits.append(cp.wait_send)
        for r, (px, py, pc) in enumerate(others):
            cp = _remote(small_ref, osmall_ref.at[me8], send_sems, recv_sems, 3 * n + r, (px, py, pc))
            cp.start()
            waits.append(cp.wait_send)
        for k in range(n):
            for j, (px, py) in enumerate(peers):
                _remote(q_refs[k].at[me], o_refs[k].at[2 * px + py], send_sems, recv_sems, 3 * k + j, (px, py, c)).wait_recv()
        for r, (px, py, pc) in enumerate(others):
            _remote(small_ref, osmall_ref.at[4 * px + 2 * py + pc], send_sems, recv_sems, 3 * n + r, (px, py, pc)).wait_recv()
        for w in waits:
            w()

    out_shape = [jax.ShapeDtypeStruct(q.shape, q.dtype) for q in partials]
    out_shape.append(jax.ShapeDtypeStruct((N_DEV,) + small.shape, small.dtype))
    res = pl.pallas_call(
        body, in_specs=[ANY] * (n + 1), out_specs=[ANY] * (n + 1), out_shape=out_shape,
        scratch_shapes=[pltpu.SemaphoreType.DMA((3 * n + 7,)), pltpu.SemaphoreType.DMA((3 * n + 7,)),
                        pltpu.SemaphoreType.DMA((n + 1,))],
        name="send_partials",
    )(*partials, small)
    return res[:n], res[n]


def _swap_sums(sums):
    n = len(sums)

    def body(*refs):
        s_refs, o_refs = refs[:n], refs[n:2 * n]
        send_sems, recv_sems, local_sems = refs[2 * n:]
        x, y, c = lax.axis_index("x"), lax.axis_index("y"), lax.axis_index("c")
        waits = []
        for k in range(n):
            rows = 2 * sums[k].shape[0]
            mine = _rows_half(o_refs[k], c, rows)
            cp = pltpu.make_async_copy(s_refs[k], mine, local_sems.at[k])
            cp.start()
            waits.append(cp.wait)
            cp = _remote(s_refs[k], mine, send_sems, recv_sems, k, (x, y, 1 - c))
            cp.start()
            waits.append(cp.wait_send)
        for k in range(n):
            theirs = _rows_half(o_refs[k], 1 - c, 2 * sums[k].shape[0])
            _remote(s_refs[k], theirs, send_sems, recv_sems, k, (x, y, 1 - c)).wait_recv()
        for w in waits:
            w()

    return pl.pallas_call(
        body, in_specs=[ANY] * n, out_specs=[ANY] * n,
        out_shape=[jax.ShapeDtypeStruct((2 * s.shape[0], s.shape[1]), s.dtype) for s in sums],
        scratch_shapes=[pltpu.SemaphoreType.DMA((n,)), pltpu.SemaphoreType.DMA((n,)), pltpu.SemaphoreType.DMA((n,))],
        name="swap_sums",
    )(*sums)


def _row_tile(rows, cols, itemsize, budget=2 * 1024 * 1024):
    t = rows
    while t % 32 == 0 and t * cols * itemsize > budget:
        t //= 2
    return t


def _add_halves(a, b, name):
    _, h, c = a.shape
    t = _row_tile(h, c, 2)

    def body(a_ref, b_ref, o_ref):
        o_ref[...] = (a_ref[...].astype(F32) + b_ref[...].astype(F32)).astype(BF16)

    blk = pl.BlockSpec((1, t, c), lambda j, i: (j, i, 0))
    return pl.pallas_call(body, grid=(N_CHIP, h // t), in_specs=[blk, blk], out_specs=blk,
                          out_shape=jax.ShapeDtypeStruct(a.shape, BF16),
                          compiler_params=_params(("arbitrary", "arbitrary")), name=name)(a, b)


def _sum_chips(parts, name):
    _, h, c = parts.shape
    t = _row_tile(h, c, 4)

    def body(p_ref, o_ref):
        g = p_ref[0].astype(F32)
        for s in range(1, N_CHIP):
            g = g + p_ref[s].astype(F32)
        o_ref[...] = g

    return pl.pallas_call(body, grid=(h // t,), in_specs=[pl.BlockSpec((N_CHIP, t, c), lambda i: (0, i, 0))],
                          out_specs=pl.BlockSpec((t, c), lambda i: (i, 0)), out_shape=jax.ShapeDtypeStruct((h, c), F32),
                          compiler_params=_params(("arbitrary",)), name=name)(parts)


def _adamw_math(w, g, m, v):
    m = ADAM_B1 * m + (1.0 - ADAM_B1) * g
    v = ADAM_B2 * v + (1.0 - ADAM_B2) * (g * g)
    m_hat = m / (1.0 - ADAM_B1 ** ADAM_STEP)
    v_hat = v / (1.0 - ADAM_B2 ** ADAM_STEP)
    delta = -ADAM_LR * (m_hat / (jnp.sqrt(v_hat) + ADAM_EPS) + ADAM_WD * w)
    return delta, m, v


def _adamw_big(g, w, m, v, name):
    r, c = w.shape
    tr = _row_tile(r, c, 4, budget=1024 * 1024)

    def body(g_ref, w_ref, m_ref, v_ref, d_ref, nm_ref, nv_ref):
        d_ref[...], nm_ref[...], nv_ref[...] = _adamw_math(w_ref[...], g_ref[...], m_ref[...], v_ref[...])

    blk = pl.BlockSpec((tr, c), lambda i: (i, 0))
    return pl.pallas_call(
        body, grid=(r // tr,), in_specs=[blk, blk, blk, blk],
        out_specs=[blk, blk, blk], out_shape=[jax.ShapeDtypeStruct((r, c), F32)] * 3,
        compiler_params=_params(("arbitrary",)), name=name,
    )(g, w, m, v)


def _sum_small(parts):
    def body(p_ref, o_ref):
        g = p_ref[0]
        for d in range(1, N_DEV):
            g = g + p_ref[d]
        o_ref[...] = g

    return pl.pallas_call(body, out_shape=jax.ShapeDtypeStruct(parts.shape[1:], F32), name="sum_small")(parts)


def _adamw_small(w, g, m, v):
    def body(w_ref, g_ref, m_ref, v_ref, d_ref, nm_ref, nv_ref):
        d_ref[...], nm_ref[...], nv_ref[...] = _adamw_math(w_ref[...], g_ref[...], m_ref[...], v_ref[...])

    return pl.pallas_call(body, out_shape=[jax.ShapeDtypeStruct(w.shape, F32)] * 3, name="adamw_small")(w, g, m, v)


def _pack(arrs):
    flat = jnp.concatenate([a.reshape(-1) for a in arrs])
    rows = -(-flat.shape[0] // 1024) * 8
    return jnp.pad(flat, (0, rows * 128 - flat.shape[0])).reshape(rows, 128)


def _unpack(buf, shapes):
    flat = buf.reshape(-1)
    out, off = [], 0
    for s in shapes:
        size = 1
        for d in s:
            size *= d
        out.append(flat[off:off + size].reshape(s))
        off += size
    return out


def _blockdiag_dense(w):
    wr = w.reshape(HEADS, 32, 4, 4)
    eye = jnp.eye(32, dtype=w.dtype)
    return (wr[:, :, :, None, :] * eye[None, :, None, :, None]).reshape(HEADS, 128, 128)


def _blockdiag_extract(d):
    dr = d.reshape(HEADS, 32, 4, 32, 4)
    eye = jnp.eye(32, dtype=d.dtype)
    return jnp.sum(dr * eye[None, :, None, :, None], axis=3).reshape(128, 4, 4)


def _cols(a4):
    return jnp.transpose(a4, (1, 0, 2)).reshape(a4.shape[1], -1)


def _col_shards(a):
    r = a.shape[0]
    return jnp.transpose(a.reshape(r, N_CHIP, -1), (1, 0, 2))


def _local_step(x, target, w, sp):
    sp = {n: (a.reshape(1, -1) if a.ndim == 1 else a) for n, a in sp.items()}
    wau = jnp.zeros((128, 256), F32).at[0:16].set(sp["w_a_up"])
    wif = jnp.zeros((1536, 128), F32).at[:, 0:8].set(sp["w_if"])
    bif = jnp.zeros((1, 128), F32).at[:, 0:8].set(sp["b_if"])
    p = {"wau": wau, "bau": sp["b_a_up"], "ggla": sp["g_gla_norm"], "cw": sp["conv_w"], "cb": sp["conv_b"],
         "wq": _blockdiag_dense(sp["w_q_ml"]), "wk": _blockdiag_dense(sp["w_k_ml"]), "wv": _blockdiag_dense(sp["w_v_ml"]),
         "wif": wif, "bif": bif, "skip": sp["ml_skip"], "gml": sp["g_ml_norm"]}
    w_in = w["w_in"]
    w_mix = jnp.concatenate([w_in[:, 0:1536], w_in[:, 1552:2576], w_in[:, 1536:1552], jnp.zeros((D_MODEL, 112), BF16)], axis=1)
    w_ab = w_in[:, 2576:4624]

    pm, gab, h = _in_proj(x, sp["g_pre_mix"], w_mix, w_ab)
    ab, *states = _mixer_fwd(pm, p)
    x1, mix, merged = _merge_fwd(ab, gab, x, w["w_pa"], w["w_pb"], w["w_o"], sp["g_post_mix"])
    dx1, u, dd, h2, dpre, dg_post_mlp, dg_pre_mlp, loss = _mlp(x1, target, sp["g_pre_mlp"], sp["g_post_mlp"],
                                                                w["w_up"], w["w_down"])
    dmix, dya, dyb, dgab, dab, dg_post_mix = _merge_bwd(dx1, mix, ab, gab, w["w_pa"], w["w_pb"], w["w_o"], sp["g_post_mix"])
    dpm, dp = _mixer_bwd(pm, dab, states, p)
    dx, dg_pre_mix = _in_proj_bwd(dpm, dgab, x, dx1, sp["g_pre_mix"], w_mix, w_ab)

    d_mix = _tn_matmul(h, dpm, "dw_in_mix")
    d_ab = _tn_matmul(h, dgab, "dw_in_gates")
    big = {
        "w_in": jnp.concatenate([d_mix[:, 0:1536], d_mix[:, 2560:2576], d_mix[:, 1536:2560], d_ab], axis=1),
        "w_pa": _tn_matmul(ab[:, 0:512], dya, "dw_pa"),
        "w_pb": _tn_matmul(ab[:, 512:1024], dyb, "dw_pb"),
        "w_o": _tn_matmul(merged, dmix, "dw_o"),
        "w_up": _tn_matmul(h2, dpre, "dw_up"),
        "w_down": _tn_matmul(u, dd, "dw_down"),
    }
    small = {
        "g_pre_mix": dg_pre_mix, "b_a_up": dp["bau"], "g_gla_norm": dp["ggla"], "conv_b": dp["cb"],
        "w_q_ml": _blockdiag_extract(dp["wq"]), "w_k_ml": _blockdiag_extract(dp["wk"]), "w_v_ml": _blockdiag_extract(dp["wv"]),
        "b_if": dp["bif"][:, 0:8], "ml_skip": dp["skip"], "g_ml_norm": dp["gml"], "g_post_mix": dg_post_mix,
        "g_pre_mlp": dg_pre_mlp, "g_post_mlp": dg_post_mlp, "w_a_up": dp["wau"][0:16], "conv_w": dp["cw"],
        "w_if": dp["wif"][:, 0:8], "loss": loss[:, 0:1],
    }
    return dx, big, small


_BIG = ("w_in", "w_pa", "w_pb", "w_o", "w_up", "w_down")
_BIG_ROW_SHARDED = ("w_o", "w_down")
_SMALL_REPL = ("g_pre_mix", "b_a_up", "g_gla_norm", "conv_b", "w_q_ml", "w_k_ml", "w_v_ml", "b_if", "ml_skip",
               "g_ml_norm", "g_post_mix", "g_pre_mlp", "g_post_mlp")
_SMALL_SHARDED = ("w_a_up", "conv_w", "w_if")
_SMALL_ORDER = _SMALL_REPL + _SMALL_SHARDED + ("loss",)
_WEIGHTS = ("g_pre_mix", "w_in", "w_a_up", "b_a_up", "g_gla_norm", "conv_w", "conv_b", "w_q_ml", "w_k_ml", "w_v_ml",
            "w_if", "b_if", "ml_skip", "g_ml_norm", "w_pa", "w_pb", "w_o", "g_post_mix", "g_pre_mlp", "w_up", "w_down",
            "g_post_mlp")


def kernel(x, g_pre_mix, w_in, w_a_up, b_a_up, g_gla_norm, conv_w, conv_b, w_q_ml, w_k_ml, w_v_ml, w_if, b_if, ml_skip, g_ml_norm, w_pa, w_pb, w_o, g_post_mix, g_pre_mlp, w_up, w_down, g_post_mlp, loss_target, m_g_pre_mix, m_w_in, m_w_a_up, m_b_a_up, m_g_gla_norm, m_conv_w, m_conv_b, m_w_q_ml, m_w_k_ml, m_w_v_ml, m_w_if, m_b_if, m_ml_skip, m_g_ml_norm, m_w_pa, m_w_pb, m_w_o, m_g_post_mix, m_g_pre_mlp, m_w_up, m_w_down, m_g_post_mlp, v_g_pre_mix, v_w_in, v_w_a_up, v_b_a_up, v_g_gla_norm, v_conv_w, v_conv_b, v_w_q_ml, v_w_k_ml, v_w_v_ml, v_w_if, v_b_if, v_ml_skip, v_g_ml_norm, v_w_pa, v_w_pb, v_w_o, v_g_post_mix, v_g_pre_mlp, v_w_up, v_w_down, v_g_post_mlp):
    args = dict(locals())
    wts = {n: args[n][0] for n in _WEIGHTS}
    mom = {n: args["m_" + n][0] for n in _WEIGHTS}
    var = {n: args["v_" + n][0] for n in _WEIGHTS}
    chip = 2 * lax.axis_index("x") + lax.axis_index("y")

    gathered = dict(zip(_BIG + _SMALL_SHARDED,
                        _gather_shards([wts[n].astype(BF16) for n in _BIG], [wts[n] for n in _SMALL_SHARDED])))
    full = {}
    for n in _BIG:
        a4 = gathered[n]
        full[n] = a4.reshape(-1, a4.shape[2]) if n in _BIG_ROW_SHARDED else _cols(a4)
    sp = {n: wts[n] for n in _SMALL_REPL}
    sp["w_a_up"] = _cols(gathered["w_a_up"])
    sp["conv_w"] = _cols(gathered["conv_w"])
    sp["w_if"] = gathered["w_if"].reshape(1536, 8)

    dx, big, small = _local_step(x[0], loss_target[0], full, sp)

    pieces = []
    for n in _BIG:
        g = big[n]
        pieces.append(g.reshape(N_CHIP, -1, g.shape[1]) if n in _BIG_ROW_SHARDED else _col_shards(g))
    small_shapes = [small[n].shape for n in _SMALL_ORDER]
    own, got = _swap_halves(pieces)
    partials = [_add_halves(a, b, "presum_" + n) for n, a, b in zip(_BIG, own, got)]
    parts, small_parts = _send_partials(partials, _pack([small[n] for n in _SMALL_ORDER]))
    sums = _swap_sums([_sum_chips(p4, "sum_" + n) for n, p4 in zip(_BIG, parts)])

    grads, delta, new_m, new_v = {}, {}, {}, {}
    for n, g in zip(_BIG, sums):
        grads[n] = g
        delta[n], new_m[n], new_v[n] = _adamw_big(g, wts[n], mom[n], var[n], "adamw_" + n)
    summed = dict(zip(_SMALL_ORDER, _unpack(_sum_small(small_parts), small_shapes)))
    loss = summed["loss"].reshape(())
    for n in _SMALL_REPL:
        grads[n] = summed[n].reshape(wts[n].shape)
    grads["w_a_up"] = lax.dynamic_slice_in_dim(summed["w_a_up"], chip * 64, 64, axis=1)
    grads["conv_w"] = lax.dynamic_slice_in_dim(summed["conv_w"], chip * 128, 128, axis=1)
    grads["w_if"] = lax.dynamic_slice_in_dim(summed["w_if"], chip * 384, 384, axis=0)
    small_names = _SMALL_REPL + _SMALL_SHARDED
    shard_shapes = [wts[n].shape for n in small_names]
    upd = _adamw_small(_pack([wts[n] for n in small_names]), _pack([grads[n] for n in small_names]),
                       _pack([mom[n] for n in small_names]), _pack([var[n] for n in small_names]))
    for dst, buf in zip((delta, new_m, new_v), upd):
        for n, a in zip(small_names, _unpack(buf, shard_shapes)):
            dst[n] = a

    outs = [loss, dx[None]]
    for group in (grads, delta, new_m, new_v):
        outs += [group[n][None] for n in _WEIGHTS]
    return tuple(outs)
```

```python
import functools

import jax
import jax.numpy as jnp
from jax import lax
from jax.experimental import pallas as pl
from jax.experimental.pallas import tpu as pltpu

F32 = jnp.float32
BF16 = jnp.bfloat16

SEQ = 2048
D_MODEL = 1024
CHUNK = 64
N_CHUNK = SEQ // CHUNK
HEADS = 4
GLA_DK = 64
GLA_DV = 128
ML_DH = 128
D_FF = 4096
EPS = 1e-6
N_CHIP = 4
N_DEV = 8
TOK_TILE = 256
N_TOK_TILE = SEQ // TOK_TILE

PM_W = 2688
PM_XM = 1536
PM_OP = 2048
PM_AL = 2560
GAB_W = 2048

ADAM_LR = 0.001
ADAM_B1 = 0.9
ADAM_B2 = 0.999
ADAM_EPS = 1e-08
ADAM_WD = 0.01
ADAM_STEP = 10

VMEM_LIMIT = 56 * 1024 * 1024


def _params(sem=None):
    return pltpu.CompilerParams(dimension_semantics=sem, vmem_limit_bytes=VMEM_LIMIT)


def _dot(a, b, ca, cb):
    return lax.dot_general(a.astype(BF16), b.astype(BF16), (((ca,), (cb,)), ((), ())), preferred_element_type=F32)


def _pmm_nn(a, b):
    return _dot(a, b, 1, 0)


def _pmm_nt(a, b):
    return _dot(a, b, 1, 1)


def _pmm_tn(a, b):
    return _dot(a, b, 0, 0)


def _pcmm(c, x):
    return lax.dot_general(c, x, (((1,), (0,)), ((), ())), precision=lax.Precision.HIGHEST, preferred_element_type=F32)


@jax.custom_vjp
def _mm_nn(a, b):
    return _dot(a, b, 1, 0)


@jax.custom_vjp
def _mm_nt(a, b):
    return _dot(a, b, 1, 1)


@jax.custom_vjp
def _mm_tn(a, b):
    return _dot(a, b, 0, 0)


_mm_nn.defvjp(lambda a, b: (_dot(a, b, 1, 0), (a, b)), lambda r, g: (_mm_nt(g, r[1]), _mm_tn(r[0], g)))
_mm_nt.defvjp(lambda a, b: (_dot(a, b, 1, 1), (a, b)), lambda r, g: (_mm_nn(g, r[1]), _mm_tn(g, r[0])))
_mm_tn.defvjp(lambda a, b: (_dot(a, b, 0, 0), (a, b)), lambda r, g: (_mm_nt(r[1], g), _mm_nn(r[0], g)))


@jax.custom_vjp
def _cmm(c, x):
    return _pcmm(c, x)


_cmm.defvjp(
    lambda c, x: (_pcmm(c, x), c),
    lambda c, g: (jnp.zeros_like(c), lax.dot_general(c, g, (((0,), (0,)), ((), ())), precision=lax.Precision.HIGHEST,
                                                      preferred_element_type=F32)),
)

_PLAIN_OPS = (_pmm_nn, _pmm_nt, _pmm_tn, _pcmm)
_VJP_OPS = (_mm_nn, _mm_nt, _mm_tn, _cmm)


def _sigmoid(x):
    return 0.5 * (jnp.tanh(0.5 * x) + 1.0)


def _log_sigmoid(x):
    return jnp.minimum(x, 0.0) - jnp.log(1.0 + jnp.exp(-jnp.abs(x)))


def _mean(x):
    return jnp.mean(x, axis=-1, keepdims=True)


def _mixer_chunk(ops, p, st, pm, xprev8):
    mm_nn, mm_nt, mm_tn, cmm = ops
    row = lax.broadcasted_iota(jnp.int32, (CHUNK, CHUNK), 0)
    col = lax.broadcasted_iota(jnp.int32, (CHUNK, CHUNK), 1)
    causal = row >= col
    tri = causal.astype(F32)
    q = pm[:, 0:256]
    k = pm[:, 256:512]
    v = pm[:, 512:1024]
    g = pm[:, 1024:1536]
    xm = pm[:, PM_XM:PM_XM + 512]
    opre = pm[:, PM_OP:PM_OP + 512]
    alow = pm[:, PM_AL:PM_AL + 128]

    la = _log_sigmoid(mm_nn(alow, p["wau"]) + p["bau"]) * (1.0 / 16.0)
    cum = cmm(tri, la)
    cum_last = cum[CHUNK - 1:CHUNK, :]
    e_pos = jnp.exp(cum)
    e_neg = jnp.exp(-cum)
    qs = q * (GLA_DK ** -0.5)
    qp = qs * e_pos
    qn = qs * e_neg
    kp = k * e_pos
    kn = k * e_neg
    kl = k * jnp.exp(cum_last - cum)
    dec = jnp.exp(cum_last)
    outs = []
    s_new = []
    for h in range(HEADS):
        s6 = slice(h * GLA_DK, (h + 1) * GLA_DK)
        s12 = slice(h * GLA_DV, (h + 1) * GLA_DV)
        scores = jnp.where(causal, mm_nt(qp[:, s6], kn[:, s6]), mm_nt(qn[:, s6], kp[:, s6]))
        o = mm_nn(scores, v[:, s12]) + mm_nt(qp[:, s6], st["S"][h])
        s_new.append(st["S"][h] * dec[:, s6] + mm_tn(v[:, s12], kl[:, s6]))
        o = o * lax.rsqrt(_mean(o * o) + EPS) * p["ggla"]
        gh = g[:, s12]
        outs.append(o * (gh * _sigmoid(gh)))

    xx = jnp.concatenate([xprev8, xm], axis=0)
    pre = p["cb"]
    for j in range(4):
        pre = pre + p["cw"][j:j + 1, :] * xx[5 + j:5 + j + CHUNK, :]
    xc = pre * _sigmoid(pre)
    qm, km, vm = [], [], []
    for h in range(HEADS):
        s12 = slice(h * ML_DH, (h + 1) * ML_DH)
        qm.append(mm_nn(xc[:, s12], p["wq"][h]))
        km.append(mm_nn(xc[:, s12], p["wk"][h]))
        vm.append(mm_nn(xm[:, s12], p["wv"][h]))
    qcat = jnp.concatenate(qm, axis=1)
    kcat = jnp.concatenate(km, axis=1)
    vcat = jnp.concatenate(vm, axis=1)
    gates = (mm_nn(qcat, p["wif"][0:512]) + mm_nn(kcat, p["wif"][512:1024]) + mm_nn(vcat, p["wif"][1024:1536])
             + p["bif"])
    lf = _log_sigmoid(gates)
    fc = cmm(tri, lf)
    gates_t = gates.T
    fc_t = fc.T
    c_new, n_new, m_new = [], [], []
    for h in range(HEADS):
        s12 = slice(h * ML_DH, (h + 1) * ML_DH)
        li_c = gates[:, h:h + 1]
        fc_c = fc[:, 4 + h:5 + h]
        li_r = gates_t[h:h + 1, :]
        fc_r = fc_t[4 + h:5 + h, :]
        m_prev = st["m"][h][:, 0:1]
        log_d = li_r - jnp.abs(fc_c - fc_r)
        g_int = fc_c + m_prev
        m_t = jnp.maximum(g_int, jnp.max(log_d, axis=1, keepdims=True))
        ks = km[h] * (ML_DH ** -0.5)
        s = mm_nt(qm[h], ks) * jnp.exp(log_d - m_t)
        scl = jnp.exp(g_int - m_t)
        num = mm_nn(s, vm[h]) + scl * mm_nn(qm[h], st["C"][h])
        den = jnp.sum(s, axis=1, keepdims=True) + scl * jnp.sum(qm[h] * st["n"][h], axis=1, keepdims=True)
        den = jnp.maximum(jnp.abs(den), jnp.exp(-m_t))
        hc = num / den * _sigmoid(opre[:, s12])
        d0 = hc - _mean(hc)
        y = d0 * lax.rsqrt(_mean(d0 * d0) + EPS)
        outs.append(y * p["gml"][:, s12] + p["skip"][:, s12] * xc[:, s12])
        f_last = fc[CHUNK - 1:CHUNK, 4 + h:5 + h]
        a = f_last - fc_c + li_c
        m_loc = jnp.max(a, axis=0, keepdims=True)
        kw = ks * jnp.exp(a - m_loc)
        m_nx = jnp.maximum(f_last + m_prev, m_loc)
        sp = jnp.exp(f_last + m_prev - m_nx)
        sl = jnp.exp(m_loc - m_nx)
        c_new.append(sp * st["C"][h] + sl * mm_tn(kw, vm[h]))
        n_new.append(sp * st["n"][h] + sl * jnp.sum(kw, axis=0, keepdims=True))
        m_new.append(jnp.broadcast_to(m_nx, (1, ML_DH)))
    ab = jnp.concatenate(outs, axis=1)
    new = {"S": s_new, "C": c_new, "n": n_new, "m": m_new}
    return ab, new


_P_NAMES = ("wau", "bau", "ggla", "cw", "cb", "wq", "wk", "wv", "wif", "bif", "skip", "gml")
_P_SHAPES = {
    "wau": (128, 256), "bau": (1, 256), "ggla": (1, 128), "cw": (4, 512), "cb": (1, 512),
    "wq": (HEADS, 128, 128), "wk": (HEADS, 128, 128), "wv": (HEADS, 128, 128),
    "wif": (1536, 128), "bif": (1, 128), "skip": (1, 512), "gml": (1, 512),
}
_S_NAMES = ("S", "C", "n", "m")
_S_SHAPES = {"S": (HEADS, GLA_DV, GLA_DK), "C": (HEADS, ML_DH, ML_DH), "n": (HEADS, 1, ML_DH), "m": (HEADS, 1, ML_DH)}


_P_PER_HEAD = ("wq", "wk", "wv")


def _per_head(ref):
    return [ref[h] for h in range(HEADS)]


def _load_param(name, ref):
    return _per_head(ref) if name in _P_PER_HEAD else ref[...]


def _const_spec(shape):
    zeros = (0,) * len(shape)
    return pl.BlockSpec(shape, lambda i: zeros)


def _mixer_fwd(pm, p):
    n_p = len(_P_NAMES)

    def body(*refs):
        pm_ref, xprev_ref = refs[0], refs[1]
        p_refs = refs[2:2 + n_p]
        ab_ref = refs[2 + n_p]
        so_refs = refs[3 + n_p:7 + n_p]
        sc_refs = refs[7 + n_p:11 + n_p]
        n = pl.program_id(0)

        @pl.when(n == 0)
        def _():
            for r in sc_refs:
                r[...] = jnp.zeros_like(r)

        st = {name: _per_head(r) for name, r in zip(_S_NAMES, sc_refs)}
        for name, r in zip(_S_NAMES, so_refs):
            for h in range(HEADS):
                r[0, h] = st[name][h]
        pv = {name: _load_param(name, r) for name, r in zip(_P_NAMES, p_refs)}
        xprev8 = jnp.where(n > 0, xprev_ref[CHUNK - 8:CHUNK, :], 0.0)
        ab, new = _mixer_chunk(_PLAIN_OPS, pv, st, pm_ref[...], xprev8)
        ab_ref[...] = ab.astype(BF16)
        for name, r in zip(_S_NAMES, sc_refs):
            for h in range(HEADS):
                r[h] = new[name][h]

    in_specs = [pl.BlockSpec((CHUNK, PM_W), lambda i: (i, 0)),
                pl.BlockSpec((CHUNK, 512), lambda i: (jnp.maximum(i - 1, 0), PM_XM // 512))]
    in_specs += [_const_spec(_P_SHAPES[nm]) for nm in _P_NAMES]
    out_specs = [pl.BlockSpec((CHUNK, 1024), lambda i: (i, 0))]
    out_shape = [jax.ShapeDtypeStruct((SEQ, 1024), BF16)]
    for nm in _S_NAMES:
        shp = _S_SHAPES[nm]
        out_specs.append(pl.BlockSpec((1,) + shp, lambda i: (i, 0, 0, 0)))
        out_shape.append(jax.ShapeDtypeStruct((N_CHUNK,) + shp, F32))
    return pl.pallas_call(
        body, grid=(N_CHUNK,), in_specs=in_specs, out_specs=out_specs, out_shape=out_shape,
        scratch_shapes=[pltpu.VMEM(_S_SHAPES[nm], F32) for nm in _S_NAMES],
        compiler_params=_params(("arbitrary",)), name="mixer_fwd",
    )(pm, pm, *[p[nm] for nm in _P_NAMES])


def _mixer_bwd(pm, dab, states, p):
    n_p = len(_P_NAMES)

    def body(*refs):
        pm_ref, xprev_ref, dab_ref = refs[0], refs[1], refs[2]
        si_refs = refs[3:7]
        p_refs = refs[7:7 + n_p]
        dpm_ref = refs[7 + n_p]
        dp_refs = refs[8 + n_p:8 + 2 * n_p]
        ds_refs = refs[8 + 2 * n_p:12 + 2 * n_p]
        carry_ref = refs[12 + 2 * n_p]
        i = pl.program_id(0)
        n = N_CHUNK - 1 - i

        @pl.when(i == 0)
        def _():
            for r in ds_refs:
                r[...] = jnp.zeros_like(r)
            for r in dp_refs:
                r[...] = jnp.zeros_like(r)
            carry_ref[...] = jnp.zeros_like(carry_ref)

        st = {name: [r[0, h] for h in range(HEADS)] for name, r in zip(_S_NAMES, si_refs)}
        pv = {name: _load_param(name, r) for name, r in zip(_P_NAMES, p_refs)}
        xprev8 = jnp.where(n > 0, xprev_ref[CHUNK - 8:CHUNK, :], 0.0)
        _, vjp = jax.vjp(functools.partial(_mixer_chunk, _VJP_OPS), pv, st, pm_ref[...], xprev8)
        dst = {name: _per_head(r) for name, r in zip(_S_NAMES, ds_refs)}
        dp, dst_prev, dpm, dxprev8 = vjp((dab_ref[...], dst))
        reach = jnp.concatenate([jnp.zeros((CHUNK - 8, 512), F32), carry_ref[...]], axis=0)
        dpm_ref[:, 0:PM_XM] = dpm[:, 0:PM_XM].astype(BF16)
        dpm_ref[:, PM_XM:PM_XM + 512] = (dpm[:, PM_XM:PM_XM + 512] + reach).astype(BF16)
        dpm_ref[:, PM_XM + 512:PM_W] = dpm[:, PM_XM + 512:PM_W].astype(BF16)
        carry_ref[...] = dxprev8
        for name, r in zip(_S_NAMES, ds_refs):
            for h in range(HEADS):
                r[h] = dst_prev[name][h]
        for name, r in zip(_P_NAMES, dp_refs):
            if name in _P_PER_HEAD:
                for h in range(HEADS):
                    r[h] += dp[name][h]
            else:
                r[...] += dp[name]

    rev = lambda i: (N_CHUNK - 1 - i, 0)
    in_specs = [pl.BlockSpec((CHUNK, PM_W), rev),
                pl.BlockSpec((CHUNK, 512), lambda i: (jnp.maximum(N_CHUNK - 2 - i, 0), PM_XM // 512)),
                pl.BlockSpec((CHUNK, 1024), rev)]
    for nm in _S_NAMES:
        in_specs.append(pl.BlockSpec((1,) + _S_SHAPES[nm], lambda i: (N_CHUNK - 1 - i, 0, 0, 0)))
    in_specs += [_const_spec(_P_SHAPES[nm]) for nm in _P_NAMES]
    out_specs = [pl.BlockSpec((CHUNK, PM_W), rev)] + [_const_spec(_P_SHAPES[nm]) for nm in _P_NAMES]
    out_shape = [jax.ShapeDtypeStruct((SEQ, PM_W), BF16)] + [jax.ShapeDtypeStruct(_P_SHAPES[nm], F32) for nm in _P_NAMES]
    res = pl.pallas_call(
        body, grid=(N_CHUNK,), in_specs=in_specs, out_specs=out_specs, out_shape=out_shape,
        scratch_shapes=[pltpu.VMEM(_S_SHAPES[nm], F32) for nm in _S_NAMES] + [pltpu.VMEM((8, 512), F32)],
        compiler_params=_params(("arbitrary",)), name="mixer_bwd",
    )(pm, pm, dab, *states, *[p[nm] for nm in _P_NAMES])
    return res[0], dict(zip(_P_NAMES, res[1:]))


def _tok(width):
    return pl.BlockSpec((TOK_TILE, width), lambda i: (i, 0))


def _once(shape):
    zeros = (0,) * len(shape)
    return pl.BlockSpec(shape, lambda i: zeros, pipeline_mode=pl.Buffered(1))


def _rms_fwd(x):
    r = lax.rsqrt(_mean(x * x) + EPS)
    return x * r, r


def _rms_bwd(dy, xn, r, g):
    gd = dy * g
    return r * (gd - xn * _mean(xn * gd))


def _in_proj(x, g_pre, w_mix, w_ab):
    def body(x_ref, g_ref, wm_ref, wa_ref, pm_ref, gab_ref, h_ref):
        xn, _ = _rms_fwd(x_ref[...])
        h = (xn * g_ref[...]).astype(BF16)
        h_ref[...] = h
        pm_ref[...] = jnp.dot(h, wm_ref[...], preferred_element_type=F32)
        gab_ref[...] = jnp.dot(h, wa_ref[...], preferred_element_type=F32)

    return pl.pallas_call(
        body, grid=(N_TOK_TILE,),
        in_specs=[_tok(D_MODEL), _once((1, D_MODEL)), _once((D_MODEL, PM_W)), _once((D_MODEL, GAB_W))],
        out_specs=[_tok(PM_W), _tok(GAB_W), _tok(D_MODEL)],
        out_shape=[jax.ShapeDtypeStruct((SEQ, PM_W), F32), jax.ShapeDtypeStruct((SEQ, GAB_W), F32),
                   jax.ShapeDtypeStruct((SEQ, D_MODEL), BF16)],
        compiler_params=_params(("arbitrary",)), name="in_proj",
    )(x, g_pre, w_mix, w_ab)


def _merge_fwd(ab, gab, x, w_pa, w_pb, w_o, g_post):
    def body(ab_ref, gab_ref, x_ref, wpa_ref, wpb_ref, wo_ref, g_ref, x1_ref, mix_ref, mg_ref):
        ya = jnp.dot(ab_ref[:, 0:512], wpa_ref[...], preferred_element_type=F32)
        yb = jnp.dot(ab_ref[:, 512:1024], wpb_ref[...], preferred_element_type=F32)
        merged = (_sigmoid(gab_ref[:, 0:1024]) * ya + _sigmoid(gab_ref[:, 1024:2048]) * yb).astype(BF16)
        mg_ref[...] = merged
        mix = jnp.dot(merged, wo_ref[...], preferred_element_type=F32)
        mix_ref[...] = mix
        mn, _ = _rms_fwd(mix)
        x1_ref[...] = x_ref[...] + mn * g_ref[...]

    return pl.pallas_call(
        body, grid=(N_TOK_TILE,),
        in_specs=[_tok(1024), _tok(GAB_W), _tok(D_MODEL), _once((512, D_MODEL)), _once((512, D_MODEL)),
                  _once((D_MODEL, D_MODEL)), _once((1, D_MODEL))],
        out_specs=[_tok(D_MODEL), _tok(D_MODEL), _tok(D_MODEL)],
        out_shape=[jax.ShapeDtypeStruct((SEQ, D_MODEL), F32), jax.ShapeDtypeStruct((SEQ, D_MODEL), F32),
                   jax.ShapeDtypeStruct((SEQ, D_MODEL), BF16)],
        compiler_params=_params(("arbitrary",)), name="merge_fwd",
    )(ab, gab, x, w_pa, w_pb, w_o, g_post)


def _mlp(x1, target, g_pre, g_post, w_up, w_down):
    def body(x1_ref, t_ref, gpre_ref, gpost_ref, wup_ref, wdn_ref,
             dx1_ref, u_ref, dd_ref, h2_ref, dpre_ref, dgpost_ref, dgpre_ref, loss_ref):
        @pl.when(pl.program_id(0) == 0)
        def _():
            dgpost_ref[...] = jnp.zeros_like(dgpost_ref)
            dgpre_ref[...] = jnp.zeros_like(dgpre_ref)
            loss_ref[...] = jnp.zeros_like(loss_ref)

        x1 = x1_ref[...]
        gpre = gpre_ref[...]
        gpost = gpost_ref[...]
        xn2, r2 = _rms_fwd(x1)
        h2 = (xn2 * gpre).astype(BF16)
        h2_ref[...] = h2
        pre = jnp.dot(h2, wup_ref[...], preferred_element_type=F32)
        rl = jnp.maximum(pre, 0.0)
        u = (rl * rl).astype(BF16)
        u_ref[...] = u
        d = jnp.dot(u, wdn_ref[...], preferred_element_type=F32)
        dn, r3 = _rms_fwd(d)
        diff = x1 + dn * gpost - t_ref[...]
        loss_ref[...] += jnp.sum(diff * diff, keepdims=True).reshape(1, 1) * (0.5 / D_MODEL)
        dy = diff * (1.0 / D_MODEL)
        dgpost_ref[...] += jnp.sum(dy * dn, axis=0, keepdims=True)
        dd = _rms_bwd(dy, dn, r3, gpost).astype(BF16)
        dd_ref[...] = dd
        du = lax.dot_general(dd, wdn_ref[...], (((1,), (1,)), ((), ())), preferred_element_type=F32)
        dpre = (du * (2.0 * rl)).astype(BF16)
        dpre_ref[...] = dpre
        dh2 = lax.dot_general(dpre, wup_ref[...], (((1,), (1,)), ((), ())), preferred_element_type=F32)
        dgpre_ref[...] += jnp.sum(dh2 * xn2, axis=0, keepdims=True)
        dx1_ref[...] = dy + _rms_bwd(dh2, xn2, r2, gpre)

    acc = pl.BlockSpec((1, D_MODEL), lambda i: (0, 0))
    return pl.pallas_call(
        body, grid=(N_TOK_TILE,),
        in_specs=[_tok(D_MODEL), _tok(D_MODEL), _once((1, D_MODEL)), _once((1, D_MODEL)),
                  _once((D_MODEL, D_FF)), _once((D_FF, D_MODEL))],
        out_specs=[_tok(D_MODEL), _tok(D_FF), _tok(D_MODEL), _tok(D_MODEL), _tok(D_FF), acc, acc,
                   pl.BlockSpec((1, 128), lambda i: (0, 0))],
        out_shape=[jax.ShapeDtypeStruct((SEQ, D_MODEL), F32), jax.ShapeDtypeStruct((SEQ, D_FF), BF16),
                   jax.ShapeDtypeStruct((SEQ, D_MODEL), BF16), jax.ShapeDtypeStruct((SEQ, D_MODEL), BF16),
                   jax.ShapeDtypeStruct((SEQ, D_FF), BF16), jax.ShapeDtypeStruct((1, D_MODEL), F32),
                   jax.ShapeDtypeStruct((1, D_MODEL), F32), jax.ShapeDtypeStruct((1, 128), F32)],
        compiler_params=_params(("arbitrary",)), name="mlp_fwd_bwd",
    )(x1, target, g_pre, g_post, w_up, w_down)


def _merge_bwd(dx1, mix, ab, gab, w_pa, w_pb, w_o, g_post):
    def body(dx1_ref, mix_ref, ab_ref, gab_ref, wpa_ref, wpb_ref, wo_ref, g_ref,
             dmix_ref, dya_ref, dyb_ref, dgab_ref, dab_ref, dg_ref):
        @pl.when(pl.program_id(0) == 0)
        def _():
            dg_ref[...] = jnp.zeros_like(dg_ref)

        dx1 = dx1_ref[...]
        mn, r = _rms_fwd(mix_ref[...])
        dg_ref[...] += jnp.sum(dx1 * mn, axis=0, keepdims=True)
        dmix = _rms_bwd(dx1, mn, r, g_ref[...]).astype(BF16)
        dmix_ref[...] = dmix
        dmerged = lax.dot_general(dmix, wo_ref[...], (((1,), (1,)), ((), ())), preferred_element_type=F32)
        ya = jnp.dot(ab_ref[:, 0:512], wpa_ref[...], preferred_element_type=F32)
        yb = jnp.dot(ab_ref[:, 512:1024], wpb_ref[...], preferred_element_type=F32)
        sa = _sigmoid(gab_ref[:, 0:1024])
        sb = _sigmoid(gab_ref[:, 1024:2048])
        dya = (dmerged * sa).astype(BF16)
        dyb = (dmerged * sb).astype(BF16)
        dya_ref[...] = dya
        dyb_ref[...] = dyb
        dgab_ref[:, 0:1024] = (dmerged * ya * sa * (1.0 - sa)).astype(BF16)
        dgab_ref[:, 1024:2048] = (dmerged * yb * sb * (1.0 - sb)).astype(BF16)
        dab_ref[:, 0:512] = lax.dot_general(dya, wpa_ref[...], (((1,), (1,)), ((), ())), preferred_element_type=F32)
        dab_ref[:, 512:1024] = lax.dot_general(dyb, wpb_ref[...], (((1,), (1,)), ((), ())), preferred_element_type=F32)

    return pl.pallas_call(
        body, grid=(N_TOK_TILE,),
        in_specs=[_tok(D_MODEL), _tok(D_MODEL), _tok(1024), _tok(GAB_W), _once((512, D_MODEL)), _once((512, D_MODEL)),
                  _once((D_MODEL, D_MODEL)), _once((1, D_MODEL))],
        out_specs=[_tok(D_MODEL), _tok(D_MODEL), _tok(D_MODEL), _tok(GAB_W), _tok(1024),
                   pl.BlockSpec((1, D_MODEL), lambda i: (0, 0))],
        out_shape=[jax.ShapeDtypeStruct((SEQ, D_MODEL), BF16), jax.ShapeDtypeStruct((SEQ, D_MODEL), BF16),
                   jax.ShapeDtypeStruct((SEQ, D_MODEL), BF16), jax.ShapeDtypeStruct((SEQ, GAB_W), BF16),
                   jax.ShapeDtypeStruct((SEQ, 1024), F32), jax.ShapeDtypeStruct((1, D_MODEL), F32)],
        compiler_params=_params(("arbitrary",)), name="merge_bwd",
    )(dx1, mix, ab, gab, w_pa, w_pb, w_o, g_post)


def _in_proj_bwd(dpm, dgab, x, dx1, g_pre, w_mix, w_ab):
    def body(dpm_ref, dgab_ref, x_ref, dx1_ref, g_ref, wm_ref, wa_ref, dx_ref, dg_ref):
        @pl.when(pl.program_id(0) == 0)
        def _():
            dg_ref[...] = jnp.zeros_like(dg_ref)

        dh = lax.dot_general(dpm_ref[...], wm_ref[...], (((1,), (1,)), ((), ())), preferred_element_type=F32)
        dh = dh + lax.dot_general(dgab_ref[...], wa_ref[...], (((1,), (1,)), ((), ())), preferred_element_type=F32)
        xn, r = _rms_fwd(x_ref[...])
        dg_ref[...] += jnp.sum(dh * xn, axis=0, keepdims=True)
        dx_ref[...] = dx1_ref[...] + _rms_bwd(dh, xn, r, g_ref[...])

    return pl.pallas_call(
        body, grid=(N_TOK_TILE,),
        in_specs=[_tok(PM_W), _tok(GAB_W), _tok(D_MODEL), _tok(D_MODEL), _once((1, D_MODEL)),
                  _once((D_MODEL, PM_W)), _once((D_MODEL, GAB_W))],
        out_specs=[_tok(D_MODEL), pl.BlockSpec((1, D_MODEL), lambda i: (0, 0))],
        out_shape=[jax.ShapeDtypeStruct((SEQ, D_MODEL), F32), jax.ShapeDtypeStruct((1, D_MODEL), F32)],
        compiler_params=_params(("arbitrary",)), name="in_proj_bwd",
    )(dpm, dgab, x, dx1, g_pre, w_mix, w_ab)


def _tn_matmul(a, b, name, tm=512, tn=None):
    m, n = a.shape[1], b.shape[1]
    tm = min(tm, m)
    if tn is None:
        tn = 896 if n == PM_W else min(n, 1024)

    def body(a_ref, b_ref, o_ref):
        o_ref[...] = lax.dot_general(a_ref[...], b_ref[...], (((0,), (0,)), ((), ())),
                                     preferred_element_type=F32).astype(BF16)

    return pl.pallas_call(
        body, grid=(m // tm, n // tn),
        in_specs=[pl.BlockSpec((SEQ, tm), lambda i, j: (0, i)), pl.BlockSpec((SEQ, tn), lambda i, j: (0, j))],
        out_specs=pl.BlockSpec((tm, tn), lambda i, j: (i, j)),
        out_shape=jax.ShapeDtypeStruct((m, n), BF16),
        compiler_params=_params(("arbitrary", "arbitrary")), name=name,
    )(a, b)


MESH = pl.DeviceIdType.MESH
ANY = pl.BlockSpec(memory_space=pl.ANY)
VMEM_WHOLE = pl.BlockSpec(memory_space=pltpu.VMEM)


def _rows_half(ref, e, rows):
    h = rows // 2
    return ref.at[pl.ds(pl.multiple_of(e * h, 16), h)]


def _remote(src, dst, send_sems, recv_sems, k, to):
    return pltpu.make_async_remote_copy(src_ref=src, dst_ref=dst, send_sem=send_sems.at[k], recv_sem=recv_sems.at[k],
                                        device_id=to, device_id_type=MESH)


def _gather_shards(big, small):
    nb, n = len(big), len(big) + len(small)

    def body(*refs):
        ins, outs = refs[:n], refs[n:2 * n]
        send_sems, recv_sems, local_sems = refs[2 * n:]
        x, y, c = lax.axis_index("x"), lax.axis_index("y"), lax.axis_index("c")
        me = 2 * x + y
        sibling = (x, y, 1 - c)
        peers = [(1 - x, y), (x, 1 - y), (1 - x, 1 - y)]
        rows = [s.shape[0] for s in big]
        waits = []
        for k in range(n):
            cp = pltpu.make_async_copy(ins[k], outs[k].at[me], local_sems.at[k])
            cp.start()
            waits.append(cp.wait)
        for k in range(n):
            for j, (px, py) in enumerate(peers):
                if k < nb:
                    cp = _remote(_rows_half(ins[k], c, rows[k]), _rows_half(outs[k].at[me], c, rows[k]),
                                 send_sems, recv_sems, 6 * k + j, (px, py, c))
                else:
                    cp = _remote(ins[k], outs[k].at[me], send_sems, recv_sems, 6 * k + j, (px, py, c))
                cp.start()
                waits.append(cp.wait_send)
        for j, (px, py) in enumerate(peers):
            pj = 2 * px + py
            for k in range(nb):
                landed = _rows_half(outs[k].at[pj], c, rows[k])
                _remote(landed, landed, send_sems, recv_sems, 6 * k + j, (px, py, c)).wait_recv()
                cp = _remote(landed, landed, send_sems, recv_sems, 6 * k + 3 + j, sibling)
                cp.start()
                waits.append(cp.wait_send)
        for j, (px, py) in enumerate(peers):
            pj = 2 * px + py
            for k in range(n):
                if k < nb:
                    passed = _rows_half(outs[k].at[pj], 1 - c, rows[k])
                    _remote(passed, passed, send_sems, recv_sems, 6 * k + 3 + j, sibling).wait_recv()
                else:
                    _remote(ins[k], outs[k].at[pj], send_sems, recv_sems, 6 * k + j, (px, py, c)).wait_recv()
        for w in waits:
            w()

    shards = list(big) + list(small)
    return pl.pallas_call(
        body, in_specs=[ANY] * n, out_specs=[ANY] * n,
        out_shape=[jax.ShapeDtypeStruct((N_CHIP,) + s.shape, s.dtype) for s in shards],
        scratch_shapes=[pltpu.SemaphoreType.DMA((6 * n,)), pltpu.SemaphoreType.DMA((6 * n,)), pltpu.SemaphoreType.DMA((n,))],
        name="gather_weights",
    )(*shards)


def _piece_halves(ref, e, rows):
    h = rows // 2
    return ref.at[pl.ds(0, N_CHIP), pl.ds(pl.multiple_of(e * h, 16), h)]


def _swap_halves(grads):
    n = len(grads)

    def body(*refs):
        g_refs, got_refs, stage_refs = refs[:n], refs[n:2 * n], refs[2 * n:3 * n]
        send_sems, recv_sems, local_sems = refs[3 * n:]
        x, y, c = lax.axis_index("x"), lax.axis_index("y"), lax.axis_index("c")
        staged = []
        for k in range(n):
            cp = pltpu.make_async_copy(_piece_halves(g_refs[k], 1 - c, grads[k].shape[1]), stage_refs[k], local_sems.at[k])
            cp.start()
            staged.append(cp)
        sends = []
        for k in range(n):
            staged[k].wait()
            cp = _remote(stage_refs[k], got_refs[k], send_sems, recv_sems, k, (x, y, 1 - c))
            cp.start()
            sends.append(cp)
        for cp in sends:
            cp.wait()

    half = [jax.ShapeDtypeStruct((N_CHIP, g.shape[1] // 2, g.shape[2]), g.dtype) for g in grads]
    return pl.pallas_call(
        body, in_specs=[ANY] * n, out_specs=[VMEM_WHOLE] * n, out_shape=half,
        scratch_shapes=[pltpu.VMEM(h.shape, h.dtype) for h in half]
        + [pltpu.SemaphoreType.DMA((n,)), pltpu.SemaphoreType.DMA((n,)), pltpu.SemaphoreType.DMA((n,))],
        compiler_params=_params(), name="swap_halves",
    )(*grads)


def _send_partials(partials, small):
    n = len(partials)

    def body(*refs):
        q_refs, small_ref = refs[:n], refs[n]
        o_refs, osmall_ref = refs[n + 1:2 * n + 1], refs[2 * n + 1]
        send_sems, recv_sems, local_sems = refs[2 * n + 2:]
        x, y, c = lax.axis_index("x"), lax.axis_index("y"), lax.axis_index("c")
        me = 2 * x + y
        me8 = 4 * x + 2 * y + c
        peers = [(1 - x, y), (x, 1 - y), (1 - x, 1 - y)]
        others = [(x, y, 1 - c), (1 - x, y, c), (1 - x, y, 1 - c), (x, 1 - y, c), (x, 1 - y, 1 - c),
                  (1 - x, 1 - y, c), (1 - x, 1 - y, 1 - c)]
        waits = []
        for k in range(n):
            cp = pltpu.make_async_copy(q_refs[k].at[me], o_refs[k].at[me], local_sems.at[k])
            cp.start()
            waits.append(cp.wait)
        cp = pltpu.make_async_copy(small_ref, osmall_ref.at[me8], local_sems.at[n])
        cp.start()
        waits.append(cp.wait)
        for k in range(n):
            for j, (px, py) in enumerate(peers):
                cp = _remote(q_refs[k].at[2 * px + py], o_refs[k].at[me], send_sems, recv_sems, 3 * k + j, (px, py, c))
                cp.start()
                waits.append(cp.wait_send)
        for r, (px, py, pc) in enumerate(others):
            cp = _remote(small_ref, osmall_ref.at[me8], send_sems, recv_sems, 3 * n + r, (px, py, pc))
            cp.start()
            waits.append(cp.wait_send)
        for k in range(n):
            for j, (px, py) in enumerate(peers):
                _remote(q_refs[k].at[me], o_refs[k].at[2 * px + py], send_sems, recv_sems, 3 * k + j, (px, py, c)).wait_recv()
        for r, (px, py, pc) in enumerate(others):
            _remote(small_ref, osmall_ref.at[4 * px + 2 * py + pc], send_sems, recv_sems, 3 * n + r, (px, py, pc)).wait_recv()
        for w in waits:
            w()

    out_shape = [jax.ShapeDtypeStruct(q.shape, q.dtype) for q in partials]
    out_shape.append(jax.ShapeDtypeStruct((N_DEV,) + small.shape, small.dtype))
    res = pl.pallas_call(
        body, in_specs=[ANY] * (n + 1), out_specs=[ANY] * (n + 1), out_shape=out_shape,
        scratch_shapes=[pltpu.SemaphoreType.DMA((3 * n + 7,)), pltpu.SemaphoreType.DMA((3 * n + 7,)),
                        pltpu.SemaphoreType.DMA((n + 1,))],
        name="send_partials",
    )(*partials, small)
    return res[:n], res[n]


def _swap_sums(sums):
    n = len(sums)

    def body(*refs):
        s_refs, o_refs = refs[:n], refs[n:2 * n]
        send_sems, recv_sems = refs[2 * n:]
        x, y, c = lax.axis_index("x"), lax.axis_index("y"), lax.axis_index("c")
        sends = []
        for k in range(n):
            mine = _rows_half(o_refs[k], c, 2 * sums[k].shape[0])
            cp = _remote(s_refs[k], mine, send_sems, recv_sems, k, (x, y, 1 - c))
            cp.start()
            sends.append(cp)
        for k in range(n):
            h = sums[k].shape[0]
            o_refs[k][pl.ds(pl.multiple_of(c * h, 16), h), :] = s_refs[k][...]
        for k in range(n):
            theirs = _rows_half(o_refs[k], 1 - c, 2 * sums[k].shape[0])
            _remote(s_refs[k], theirs, send_sems, recv_sems, k, (x, y, 1 - c)).wait_recv()
        for cp in sends:
            cp.wait_send()

    return pl.pallas_call(
        body, in_specs=[VMEM_WHOLE] * n, out_specs=[VMEM_WHOLE] * n,
        out_shape=[jax.ShapeDtypeStruct((2 * s.shape[0], s.shape[1]), s.dtype) for s in sums],
        scratch_shapes=[pltpu.SemaphoreType.DMA((n,)), pltpu.SemaphoreType.DMA((n,))],
        compiler_params=_params(), name="swap_sums",
    )(*sums)


def _row_tile(rows, cols, itemsize, budget=2 * 1024 * 1024):
    t = rows
    while t % 32 == 0 and t * cols * itemsize > budget:
        t //= 2
    return t


def _add_halves(core, g, b, name):
    _, h, c = b.shape
    t = _row_tile(h, c, 2)
    steps = h // t

    def body(core_ref, a_ref, b_ref, o_ref):
        o_ref[...] = (a_ref[...].astype(F32) + b_ref[...].astype(F32)).astype(BF16)

    blk = pl.BlockSpec((1, t, c), lambda j, i, core_ref: (j, i, 0))
    grid_spec = pltpu.PrefetchScalarGridSpec(
        num_scalar_prefetch=1, grid=(N_CHIP, steps),
        in_specs=[pl.BlockSpec((1, t, c), lambda j, i, core_ref: (j, core_ref[0] * steps + i, 0)), blk], out_specs=blk)
    return pl.pallas_call(body, grid_spec=grid_spec, out_shape=jax.ShapeDtypeStruct(b.shape, BF16),
                          compiler_params=_params(("arbitrary", "arbitrary")), name=name)(core, g, b)


def _sum_chips(parts, name):
    _, h, c = parts.shape
    t = _row_tile(h, c, 4)

    def body(p_ref, o_ref):
        g = p_ref[0].astype(F32)
        for s in range(1, N_CHIP):
            g = g + p_ref[s].astype(F32)
        o_ref[...] = g

    return pl.pallas_call(body, grid=(h // t,), in_specs=[pl.BlockSpec((N_CHIP, t, c), lambda i: (0, i, 0))],
                          out_specs=pl.BlockSpec((t, c), lambda i: (i, 0)), out_shape=jax.ShapeDtypeStruct((h, c), F32),
                          compiler_params=_params(("arbitrary",)), name=name)(parts)


def _adamw_math(w, g, m, v):
    m = ADAM_B1 * m + (1.0 - ADAM_B1) * g
    v = ADAM_B2 * v + (1.0 - ADAM_B2) * (g * g)
    m_hat = m / (1.0 - ADAM_B1 ** ADAM_STEP)
    v_hat = v / (1.0 - ADAM_B2 ** ADAM_STEP)
    delta = -ADAM_LR * (m_hat / (jnp.sqrt(v_hat) + ADAM_EPS) + ADAM_WD * w)
    return delta, m, v


def _adamw_big(g, w, m, v, name):
    r, c = w.shape
    tr = _row_tile(r, c, 4, budget=1024 * 1024)

    def body(g_ref, w_ref, m_ref, v_ref, d_ref, nm_ref, nv_ref):
        d_ref[...], nm_ref[...], nv_ref[...] = _adamw_math(w_ref[...], g_ref[...], m_ref[...], v_ref[...])

    blk = pl.BlockSpec((tr, c), lambda i: (i, 0))
    return pl.pallas_call(
        body, grid=(r // tr,), in_specs=[blk, blk, blk, blk],
        out_specs=[blk, blk, blk], out_shape=[jax.ShapeDtypeStruct((r, c), F32)] * 3,
        compiler_params=_params(("arbitrary",)), name=name,
    )(g, w, m, v)


def _sum_small(parts):
    def body(p_ref, o_ref):
        g = p_ref[0]
        for d in range(1, N_DEV):
            g = g + p_ref[d]
        o_ref[...] = g

    return pl.pallas_call(body, out_shape=jax.ShapeDtypeStruct(parts.shape[1:], F32), name="sum_small")(parts)


def _adamw_small(w, g, m, v):
    def body(w_ref, g_ref, m_ref, v_ref, d_ref, nm_ref, nv_ref):
        d_ref[...], nm_ref[...], nv_ref[...] = _adamw_math(w_ref[...], g_ref[...], m_ref[...], v_ref[...])

    return pl.pallas_call(body, out_shape=[jax.ShapeDtypeStruct(w.shape, F32)] * 3, name="adamw_small")(w, g, m, v)


def _pack(arrs):
    flat = jnp.concatenate([a.reshape(-1) for a in arrs])
    rows = -(-flat.shape[0] // 1024) * 8
    return jnp.pad(flat, (0, rows * 128 - flat.shape[0])).reshape(rows, 128)


def _unpack(buf, shapes):
    flat = buf.reshape(-1)
    out, off = [], 0
    for s in shapes:
        size = 1
        for d in s:
            size *= d
        out.append(flat[off:off + size].reshape(s))
        off += size
    return out


def _blockdiag_dense(w):
    wr = w.reshape(HEADS, 32, 4, 4)
    eye = jnp.eye(32, dtype=w.dtype)
    return (wr[:, :, :, None, :] * eye[None, :, None, :, None]).reshape(HEADS, 128, 128)


def _blockdiag_extract(d):
    dr = d.reshape(HEADS, 32, 4, 32, 4)
    eye = jnp.eye(32, dtype=d.dtype)
    return jnp.sum(dr * eye[None, :, None, :, None], axis=3).reshape(128, 4, 4)


def _cols(a4):
    return jnp.transpose(a4, (1, 0, 2)).reshape(a4.shape[1], -1)


def _col_shards(a):
    r = a.shape[0]
    return jnp.transpose(a.reshape(r, N_CHIP, -1), (1, 0, 2))


def _local_step(x, target, w, sp):
    sp = {n: (a.reshape(1, -1) if a.ndim == 1 else a) for n, a in sp.items()}
    wau = jnp.zeros((128, 256), F32).at[0:16].set(sp["w_a_up"])
    wif = jnp.zeros((1536, 128), F32).at[:, 0:8].set(sp["w_if"])
    bif = jnp.zeros((1, 128), F32).at[:, 0:8].set(sp["b_if"])
    p = {"wau": wau, "bau": sp["b_a_up"], "ggla": sp["g_gla_norm"], "cw": sp["conv_w"], "cb": sp["conv_b"],
         "wq": _blockdiag_dense(sp["w_q_ml"]), "wk": _blockdiag_dense(sp["w_k_ml"]), "wv": _blockdiag_dense(sp["w_v_ml"]),
         "wif": wif, "bif": bif, "skip": sp["ml_skip"], "gml": sp["g_ml_norm"]}
    w_in = w["w_in"]
    w_mix = jnp.concatenate([w_in[:, 0:1536], w_in[:, 1552:2576], w_in[:, 1536:1552], jnp.zeros((D_MODEL, 112), BF16)], axis=1)
    w_ab = w_in[:, 2576:4624]

    pm, gab, h = _in_proj(x, sp["g_pre_mix"], w_mix, w_ab)
    ab, *states = _mixer_fwd(pm, p)
    x1, mix, merged = _merge_fwd(ab, gab, x, w["w_pa"], w["w_pb"], w["w_o"], sp["g_post_mix"])
    dx1, u, dd, h2, dpre, dg_post_mlp, dg_pre_mlp, loss = _mlp(x1, target, sp["g_pre_mlp"], sp["g_post_mlp"],
                                                                w["w_up"], w["w_down"])
    dmix, dya, dyb, dgab, dab, dg_post_mix = _merge_bwd(dx1, mix, ab, gab, w["w_pa"], w["w_pb"], w["w_o"], sp["g_post_mix"])
    dpm, dp = _mixer_bwd(pm, dab, states, p)
    dx, dg_pre_mix = _in_proj_bwd(dpm, dgab, x, dx1, sp["g_pre_mix"], w_mix, w_ab)

    d_mix = _tn_matmul(h, dpm, "dw_in_mix")
    d_ab = _tn_matmul(h, dgab, "dw_in_gates")
    big = {
        "w_in": jnp.concatenate([d_mix[:, 0:1536], d_mix[:, 2560:2576], d_mix[:, 1536:2560], d_ab], axis=1),
        "w_pa": _tn_matmul(ab[:, 0:512], dya, "dw_pa"),
        "w_pb": _tn_matmul(ab[:, 512:1024], dyb, "dw_pb"),
        "w_o": _tn_matmul(merged, dmix, "dw_o"),
        "w_up": _tn_matmul(h2, dpre, "dw_up"),
        "w_down": _tn_matmul(u, dd, "dw_down"),
    }
    small = {
        "g_pre_mix": dg_pre_mix, "b_a_up": dp["bau"], "g_gla_norm": dp["ggla"], "conv_b": dp["cb"],
        "w_q_ml": _blockdiag_extract(dp["wq"]), "w_k_ml": _blockdiag_extract(dp["wk"]), "w_v_ml": _blockdiag_extract(dp["wv"]),
        "b_if": dp["bif"][:, 0:8], "ml_skip": dp["skip"], "g_ml_norm": dp["gml"], "g_post_mix": dg_post_mix,
        "g_pre_mlp": dg_pre_mlp, "g_post_mlp": dg_post_mlp, "w_a_up": dp["wau"][0:16], "conv_w": dp["cw"],
        "w_if": dp["wif"][:, 0:8], "loss": loss[:, 0:1],
    }
    return dx, big, small


_BIG = ("w_in", "w_pa", "w_pb", "w_o", "w_up", "w_down")
_BIG_ROW_SHARDED = ("w_o", "w_down")
_SMALL_REPL = ("g_pre_mix", "b_a_up", "g_gla_norm", "conv_b", "w_q_ml", "w_k_ml", "w_v_ml", "b_if", "ml_skip",
               "g_ml_norm", "g_post_mix", "g_pre_mlp", "g_post_mlp")
_SMALL_SHARDED = ("w_a_up", "conv_w", "w_if")
_SMALL_ORDER = _SMALL_REPL + _SMALL_SHARDED + ("loss",)
_WEIGHTS = ("g_pre_mix", "w_in", "w_a_up", "b_a_up", "g_gla_norm", "conv_w", "conv_b", "w_q_ml", "w_k_ml", "w_v_ml",
            "w_if", "b_if", "ml_skip", "g_ml_norm", "w_pa", "w_pb", "w_o", "g_post_mix", "g_pre_mlp", "w_up", "w_down",
            "g_post_mlp")


def kernel(x, g_pre_mix, w_in, w_a_up, b_a_up, g_gla_norm, conv_w, conv_b, w_q_ml, w_k_ml, w_v_ml, w_if, b_if, ml_skip, g_ml_norm, w_pa, w_pb, w_o, g_post_mix, g_pre_mlp, w_up, w_down, g_post_mlp, loss_target, m_g_pre_mix, m_w_in, m_w_a_up, m_b_a_up, m_g_gla_norm, m_conv_w, m_conv_b, m_w_q_ml, m_w_k_ml, m_w_v_ml, m_w_if, m_b_if, m_ml_skip, m_g_ml_norm, m_w_pa, m_w_pb, m_w_o, m_g_post_mix, m_g_pre_mlp, m_w_up, m_w_down, m_g_post_mlp, v_g_pre_mix, v_w_in, v_w_a_up, v_b_a_up, v_g_gla_norm, v_conv_w, v_conv_b, v_w_q_ml, v_w_k_ml, v_w_v_ml, v_w_if, v_b_if, v_ml_skip, v_g_ml_norm, v_w_pa, v_w_pb, v_w_o, v_g_post_mix, v_g_pre_mlp, v_w_up, v_w_down, v_g_post_mlp):
    args = dict(locals())
    wts = {n: args[n][0] for n in _WEIGHTS}
    mom = {n: args["m_" + n][0] for n in _WEIGHTS}
    var = {n: args["v_" + n][0] for n in _WEIGHTS}
    chip = 2 * lax.axis_index("x") + lax.axis_index("y")

    gathered = dict(zip(_BIG + _SMALL_SHARDED,
                        _gather_shards([wts[n].astype(BF16) for n in _BIG], [wts[n] for n in _SMALL_SHARDED])))
    full = {}
    for n in _BIG:
        a4 = gathered[n]
        full[n] = a4.reshape(-1, a4.shape[2]) if n in _BIG_ROW_SHARDED else _cols(a4)
    sp = {n: wts[n] for n in _SMALL_REPL}
    sp["w_a_up"] = _cols(gathered["w_a_up"])
    sp["conv_w"] = _cols(gathered["conv_w"])
    sp["w_if"] = gathered["w_if"].reshape(1536, 8)

    dx, big, small = _local_step(x[0], loss_target[0], full, sp)

    pieces = []
    for n in _BIG:
        g = big[n]
        pieces.append(g.reshape(N_CHIP, -1, g.shape[1]) if n in _BIG_ROW_SHARDED else _col_shards(g))
    small_shapes = [small[n].shape for n in _SMALL_ORDER]
    core = lax.axis_index("c").astype(jnp.int32).reshape(1)
    partials = [_add_halves(core, g, b, "presum_" + n) for n, g, b in zip(_BIG, pieces, _swap_halves(pieces))]
    parts, small_parts = _send_partials(partials, _pack([small[n] for n in _SMALL_ORDER]))
    sums = _swap_sums([_sum_chips(p4, "sum_" + n) for n, p4 in zip(_BIG, parts)])

    grads, delta, new_m, new_v = {}, {}, {}, {}
    for n, g in zip(_BIG, sums):
        grads[n] = g
        delta[n], new_m[n], new_v[n] = _adamw_big(g, wts[n], mom[n], var[n], "adamw_" + n)
    summed = dict(zip(_SMALL_ORDER, _unpack(_sum_small(small_parts), small_shapes)))
    loss = summed["loss"].reshape(())
    for n in _SMALL_REPL:
        grads[n] = summed[n].reshape(wts[n].shape)
    grads["w_a_up"] = lax.dynamic_slice_in_dim(summed["w_a_up"], chip * 64, 64, axis=1)
    grads["conv_w"] = lax.dynamic_slice_in_dim(summed["conv_w"], chip * 128, 128, axis=1)
    grads["w_if"] = lax.dynamic_slice_in_dim(summed["w_if"], chip * 384, 384, axis=0)
    small_names = _SMALL_REPL + _SMALL_SHARDED
    shard_shapes = [wts[n].shape for n in small_names]
    upd = _adamw_small(_pack([wts[n] for n in small_names]), _pack([grads[n] for n in small_names]),
                       _pack([mom[n] for n in small_names]), _pack([var[n] for n in small_names]))
    for dst, buf in zip((delta, new_m, new_v), upd):
        for n, a in zip(small_names, _unpack(buf, shard_shapes)):
            dst[n] = a

    outs = [loss, dx[None]]
    for group in (grads, delta, new_m, new_v):
        outs += [group[n][None] for n in _WEIGHTS]
    return tuple(outs)
```

```python
import functools

import jax
import jax.numpy as jnp
from jax import lax
from jax.experimental import pallas as pl
from jax.experimental.pallas import tpu as pltpu

F32 = jnp.float32
BF16 = jnp.bfloat16

SEQ = 2048
D_MODEL = 1024
CHUNK = 64
N_CHUNK = SEQ // CHUNK
HEADS = 4
GLA_DK = 64
GLA_DV = 128
ML_DH = 128
D_FF = 4096
EPS = 1e-6
N_CHIP = 4
N_DEV = 8
TOK_TILE = 256
N_TOK_TILE = SEQ // TOK_TILE

PM_W = 2688
PM_XM = 1536
PM_OP = 2048
PM_AL = 2560
GAB_W = 2048
D_IN = 4624
IN_SHARD = D_IN // N_CHIP
IN_ALOW = 1536
IN_XM = 1552
IN_GATES = 2576

ADAM_LR = 0.001
ADAM_B1 = 0.9
ADAM_B2 = 0.999
ADAM_EPS = 1e-08
ADAM_WD = 0.01
ADAM_STEP = 10

VMEM_LIMIT = 56 * 1024 * 1024


def _params(sem=None):
    return pltpu.CompilerParams(dimension_semantics=sem, vmem_limit_bytes=VMEM_LIMIT)


def _dot(a, b, ca, cb):
    return lax.dot_general(a.astype(BF16), b.astype(BF16), (((ca,), (cb,)), ((), ())), preferred_element_type=F32)


def _pmm_nn(a, b):
    return _dot(a, b, 1, 0)


def _pmm_nt(a, b):
    return _dot(a, b, 1, 1)


def _pmm_tn(a, b):
    return _dot(a, b, 0, 0)


def _pcmm(c, x):
    return lax.dot_general(c, x, (((1,), (0,)), ((), ())), precision=lax.Precision.HIGHEST, preferred_element_type=F32)


@jax.custom_vjp
def _mm_nn(a, b):
    return _dot(a, b, 1, 0)


@jax.custom_vjp
def _mm_nt(a, b):
    return _dot(a, b, 1, 1)


@jax.custom_vjp
def _mm_tn(a, b):
    return _dot(a, b, 0, 0)


_mm_nn.defvjp(lambda a, b: (_dot(a, b, 1, 0), (a, b)), lambda r, g: (_mm_nt(g, r[1]), _mm_tn(r[0], g)))
_mm_nt.defvjp(lambda a, b: (_dot(a, b, 1, 1), (a, b)), lambda r, g: (_mm_nn(g, r[1]), _mm_tn(g, r[0])))
_mm_tn.defvjp(lambda a, b: (_dot(a, b, 0, 0), (a, b)), lambda r, g: (_mm_nt(r[1], g), _mm_nn(r[0], g)))


@jax.custom_vjp
def _cmm(c, x):
    return _pcmm(c, x)


_cmm.defvjp(
    lambda c, x: (_pcmm(c, x), c),
    lambda c, g: (jnp.zeros_like(c), lax.dot_general(c, g, (((0,), (0,)), ((), ())), precision=lax.Precision.HIGHEST,
                                                      preferred_element_type=F32)),
)

_PLAIN_OPS = (_pmm_nn, _pmm_nt, _pmm_tn, _pcmm)
_VJP_OPS = (_mm_nn, _mm_nt, _mm_tn, _cmm)


def _sigmoid(x):
    return 0.5 * (jnp.tanh(0.5 * x) + 1.0)


def _log_sigmoid(x):
    return jnp.minimum(x, 0.0) - jnp.log(1.0 + jnp.exp(-jnp.abs(x)))


def _mean(x):
    return jnp.mean(x, axis=-1, keepdims=True)


def _nt(a, b):
    return lax.dot_general(a, b, (((1,), (1,)), ((), ())), preferred_element_type=F32)


def _tn(a, b):
    return lax.dot_general(a, b, (((0,), (0,)), ((), ())), preferred_element_type=F32)


def _mixer_chunk(ops, p, st, pm, xprev8):
    mm_nn, mm_nt, mm_tn, cmm = ops
    row = lax.broadcasted_iota(jnp.int32, (CHUNK, CHUNK), 0)
    col = lax.broadcasted_iota(jnp.int32, (CHUNK, CHUNK), 1)
    causal = row >= col
    tri = causal.astype(F32)
    q = pm[:, 0:256]
    k = pm[:, 256:512]
    v = pm[:, 512:1024]
    g = pm[:, 1024:1536]
    xm = pm[:, PM_XM:PM_XM + 512]
    opre = pm[:, PM_OP:PM_OP + 512]
    alow = pm[:, PM_AL:PM_AL + 128]

    la = _log_sigmoid(mm_nn(alow, p["wau"]) + p["bau"]) * (1.0 / 16.0)
    cum = cmm(tri, la)
    cum_last = cum[CHUNK - 1:CHUNK, :]
    e_pos = jnp.exp(cum)
    e_neg = jnp.exp(-cum)
    qs = q * (GLA_DK ** -0.5)
    qp = qs * e_pos
    qn = qs * e_neg
    kp = k * e_pos
    kn = k * e_neg
    kl = k * jnp.exp(cum_last - cum)
    dec = jnp.exp(cum_last)
    outs = []
    s_new = []
    for h in range(HEADS):
        s6 = slice(h * GLA_DK, (h + 1) * GLA_DK)
        s12 = slice(h * GLA_DV, (h + 1) * GLA_DV)
        scores = jnp.where(causal, mm_nt(qp[:, s6], kn[:, s6]), mm_nt(qn[:, s6], kp[:, s6]))
        o = mm_nn(scores, v[:, s12]) + mm_nt(qp[:, s6], st["S"][h])
        s_new.append(st["S"][h] * dec[:, s6] + mm_tn(v[:, s12], kl[:, s6]))
        o = o * lax.rsqrt(_mean(o * o) + EPS) * p["ggla"]
        gh = g[:, s12]
        outs.append(o * (gh * _sigmoid(gh)))

    xx = jnp.concatenate([xprev8, xm], axis=0)
    pre = p["cb"]
    for j in range(4):
        pre = pre + p["cw"][j:j + 1, :] * xx[5 + j:5 + j + CHUNK, :]
    xc = pre * _sigmoid(pre)
    qm, km, vm = [], [], []
    for h in range(HEADS):
        s12 = slice(h * ML_DH, (h + 1) * ML_DH)
        qm.append(mm_nn(xc[:, s12], p["wq"][h]))
        km.append(mm_nn(xc[:, s12], p["wk"][h]))
        vm.append(mm_nn(xm[:, s12], p["wv"][h]))
    qcat = jnp.concatenate(qm, axis=1)
    kcat = jnp.concatenate(km, axis=1)
    vcat = jnp.concatenate(vm, axis=1)
    gates = (mm_nn(qcat, p["wif"][0:512]) + mm_nn(kcat, p["wif"][512:1024]) + mm_nn(vcat, p["wif"][1024:1536])
             + p["bif"])
    lf = _log_sigmoid(gates)
    fc = cmm(tri, lf)
    gates_t = gates.T
    fc_t = fc.T
    c_new, n_new, m_new = [], [], []
    for h in range(HEADS):
        s12 = slice(h * ML_DH, (h + 1) * ML_DH)
        li_c = gates[:, h:h + 1]
        fc_c = fc[:, 4 + h:5 + h]
        li_r = gates_t[h:h + 1, :]
        fc_r = fc_t[4 + h:5 + h, :]
        m_prev = st["m"][h][:, 0:1]
        log_d = li_r - jnp.abs(fc_c - fc_r)
        g_int = fc_c + m_prev
        m_t = jnp.maximum(g_int, jnp.max(log_d, axis=1, keepdims=True))
        ks = km[h] * (ML_DH ** -0.5)
        s = mm_nt(qm[h], ks) * jnp.exp(log_d - m_t)
        scl = jnp.exp(g_int - m_t)
        num = mm_nn(s, vm[h]) + scl * mm_nn(qm[h], st["C"][h])
        den = jnp.sum(s, axis=1, keepdims=True) + scl * jnp.sum(qm[h] * st["n"][h], axis=1, keepdims=True)
        den = jnp.maximum(jnp.abs(den), jnp.exp(-m_t))
        hc = num / den * _sigmoid(opre[:, s12])
        d0 = hc - _mean(hc)
        y = d0 * lax.rsqrt(_mean(d0 * d0) + EPS)
        outs.append(y * p["gml"][:, s12] + p["skip"][:, s12] * xc[:, s12])
        f_last = fc[CHUNK - 1:CHUNK, 4 + h:5 + h]
        a = f_last - fc_c + li_c
        m_loc = jnp.max(a, axis=0, keepdims=True)
        kw = ks * jnp.exp(a - m_loc)
        m_nx = jnp.maximum(f_last + m_prev, m_loc)
        sp = jnp.exp(f_last + m_prev - m_nx)
        sl = jnp.exp(m_loc - m_nx)
        c_new.append(sp * st["C"][h] + sl * mm_tn(kw, vm[h]))
        n_new.append(sp * st["n"][h] + sl * jnp.sum(kw, axis=0, keepdims=True))
        m_new.append(jnp.broadcast_to(m_nx, (1, ML_DH)))
    ab = jnp.concatenate(outs, axis=1)
    new = {"S": s_new, "C": c_new, "n": n_new, "m": m_new}
    return ab, new


_P_NAMES = ("wau", "bau", "ggla", "cw", "cb", "wq", "wk", "wv", "wif", "bif", "skip", "gml")
_P_SHAPES = {
    "wau": (128, 256), "bau": (1, 256), "ggla": (1, 128), "cw": (4, 512), "cb": (1, 512),
    "wq": (512, 128), "wk": (512, 128), "wv": (512, 128),
    "wif": (1536, 128), "bif": (1, 128), "skip": (1, 512), "gml": (1, 512),
}
_P_BLOCKDIAG = ("wq", "wk", "wv")
_S_NAMES = ("S", "C", "n", "m")
_S_SHAPES = {"S": (HEADS, GLA_DV, GLA_DK), "C": (HEADS, ML_DH, ML_DH), "n": (HEADS, 1, ML_DH), "m": (HEADS, 1, ML_DH)}


def _per_head(ref):
    return [ref[h] for h in range(HEADS)]


def _block_mask():
    r = lax.broadcasted_iota(jnp.int32, (128, 128), 0)
    c = lax.broadcasted_iota(jnp.int32, (128, 128), 1)
    same_block = (r >> 2) == (c >> 2)
    spread = jnp.logical_and(r < 4, (c & 3) == r)
    return same_block.astype(F32), spread.astype(F32)


def _expand_blockdiag(w_ref, dense_ref):
    same_block, spread = _block_mask()
    for h in range(HEADS):
        tiled = _pmm_nn(w_ref[h * 128:(h + 1) * 128, :], spread)
        dense_ref[h] = tiled * same_block


def _collect_blockdiag(ddense_ref, dw_ref):
    same_block, spread = _block_mask()
    for h in range(HEADS):
        dw_ref[h * 128:(h + 1) * 128, :] = lax.dot_general(
            ddense_ref[h] * same_block, spread, (((1,), (1,)), ((), ())), precision=lax.Precision.HIGHEST,
            preferred_element_type=F32)


def _const_spec(shape):
    zeros = (0,) * len(shape)
    return pl.BlockSpec(shape, lambda i: zeros)


def _mixer_fwd(pm, p):
    n_p = len(_P_NAMES)

    def body(*refs):
        pm_ref, xprev_ref = refs[0], refs[1]
        p_refs = dict(zip(_P_NAMES, refs[2:2 + n_p]))
        ab_ref = refs[2 + n_p]
        so_refs = refs[3 + n_p:7 + n_p]
        sc_refs = refs[7 + n_p:11 + n_p]
        dense = dict(zip(_P_BLOCKDIAG, refs[11 + n_p:14 + n_p]))
        n = pl.program_id(0)

        @pl.when(n == 0)
        def _():
            for r in sc_refs:
                r[...] = jnp.zeros_like(r)
            for nm in _P_BLOCKDIAG:
                _expand_blockdiag(p_refs[nm], dense[nm])

        st = {name: _per_head(r) for name, r in zip(_S_NAMES, sc_refs)}
        for name, r in zip(_S_NAMES, so_refs):
            for h in range(HEADS):
                r[0, h] = st[name][h]
        pv = {nm: (_per_head(dense[nm]) if nm in _P_BLOCKDIAG else p_refs[nm][...]) for nm in _P_NAMES}
        xprev8 = jnp.where(n > 0, xprev_ref[CHUNK - 8:CHUNK, :], 0.0)
        ab, new = _mixer_chunk(_PLAIN_OPS, pv, st, pm_ref[...], xprev8)
        ab_ref[...] = ab.astype(BF16)
        for name, r in zip(_S_NAMES, sc_refs):
            for h in range(HEADS):
                r[h] = new[name][h]

    in_specs = [pl.BlockSpec((CHUNK, PM_W), lambda i: (i, 0)),
                pl.BlockSpec((CHUNK, 512), lambda i: (jnp.maximum(i - 1, 0), PM_XM // 512))]
    in_specs += [_const_spec(_P_SHAPES[nm]) for nm in _P_NAMES]
    out_specs = [pl.BlockSpec((CHUNK, 1024), lambda i: (i, 0))]
    out_shape = [jax.ShapeDtypeStruct((SEQ, 1024), BF16)]
    for nm in _S_NAMES:
        shp = _S_SHAPES[nm]
        out_specs.append(pl.BlockSpec((1,) + shp, lambda i: (i, 0, 0, 0)))
        out_shape.append(jax.ShapeDtypeStruct((N_CHUNK,) + shp, F32))
    return pl.pallas_call(
        body, grid=(N_CHUNK,), in_specs=in_specs, out_specs=out_specs, out_shape=out_shape,
        scratch_shapes=[pltpu.VMEM(_S_SHAPES[nm], F32) for nm in _S_NAMES]
        + [pltpu.VMEM((HEADS, 128, 128), F32) for _ in _P_BLOCKDIAG],
        compiler_params=_params(("arbitrary",)), name="mixer_fwd",
    )(pm, pm, *[p[nm] for nm in _P_NAMES])


def _mixer_bwd(pm, dab, states, p):
    n_p = len(_P_NAMES)

    def body(*refs):
        pm_ref, xprev_ref, dab_ref = refs[0], refs[1], refs[2]
        si_refs = refs[3:7]
        p_refs = dict(zip(_P_NAMES, refs[7:7 + n_p]))
        dpm_ref = refs[7 + n_p]
        dp_refs = dict(zip(_P_NAMES, refs[8 + n_p:8 + 2 * n_p]))
        ds_refs = refs[8 + 2 * n_p:12 + 2 * n_p]
        carry_ref = refs[12 + 2 * n_p]
        dense = dict(zip(_P_BLOCKDIAG, refs[13 + 2 * n_p:16 + 2 * n_p]))
        ddense = dict(zip(_P_BLOCKDIAG, refs[16 + 2 * n_p:19 + 2 * n_p]))
        i = pl.program_id(0)
        n = N_CHUNK - 1 - i

        @pl.when(i == 0)
        def _():
            for r in ds_refs:
                r[...] = jnp.zeros_like(r)
            for nm in _P_NAMES:
                if nm in _P_BLOCKDIAG:
                    ddense[nm][...] = jnp.zeros_like(ddense[nm])
                    _expand_blockdiag(p_refs[nm], dense[nm])
                else:
                    dp_refs[nm][...] = jnp.zeros_like(dp_refs[nm])
            carry_ref[...] = jnp.zeros_like(carry_ref)

        st = {name: [r[0, h] for h in range(HEADS)] for name, r in zip(_S_NAMES, si_refs)}
        pv = {nm: (_per_head(dense[nm]) if nm in _P_BLOCKDIAG else p_refs[nm][...]) for nm in _P_NAMES}
        xprev8 = jnp.where(n > 0, xprev_ref[CHUNK - 8:CHUNK, :], 0.0)
        _, vjp = jax.vjp(functools.partial(_mixer_chunk, _VJP_OPS), pv, st, pm_ref[...], xprev8)
        dst = {name: _per_head(r) for name, r in zip(_S_NAMES, ds_refs)}
        dp, dst_prev, dpm, dxprev8 = vjp((dab_ref[...], dst))
        reach = jnp.concatenate([jnp.zeros((CHUNK - 8, 512), F32), carry_ref[...]], axis=0)
        dpm_ref[:, 0:PM_XM] = dpm[:, 0:PM_XM].astype(BF16)
        dpm_ref[:, PM_XM:PM_XM + 512] = (dpm[:, PM_XM:PM_XM + 512] + reach).astype(BF16)
        dpm_ref[:, PM_XM + 512:PM_W] = dpm[:, PM_XM + 512:PM_W].astype(BF16)
        carry_ref[...] = dxprev8
        for name, r in zip(_S_NAMES, ds_refs):
            for h in range(HEADS):
                r[h] = dst_prev[name][h]
        for nm in _P_NAMES:
            if nm in _P_BLOCKDIAG:
                for h in range(HEADS):
                    ddense[nm][h] += dp[nm][h]
            else:
                dp_refs[nm][...] += dp[nm]

        @pl.when(i == N_CHUNK - 1)
        def _():
            for nm in _P_BLOCKDIAG:
                _collect_blockdiag(ddense[nm], dp_refs[nm])

    rev = lambda i: (N_CHUNK - 1 - i, 0)
    in_specs = [pl.BlockSpec((CHUNK, PM_W), rev),
                pl.BlockSpec((CHUNK, 512), lambda i: (jnp.maximum(N_CHUNK - 2 - i, 0), PM_XM // 512)),
                pl.BlockSpec((CHUNK, 1024), rev)]
    for nm in _S_NAMES:
        in_specs.append(pl.BlockSpec((1,) + _S_SHAPES[nm], lambda i: (N_CHUNK - 1 - i, 0, 0, 0)))
    in_specs += [_const_spec(_P_SHAPES[nm]) for nm in _P_NAMES]
    out_specs = [pl.BlockSpec((CHUNK, PM_W), rev)] + [_const_spec(_P_SHAPES[nm]) for nm in _P_NAMES]
    out_shape = [jax.ShapeDtypeStruct((SEQ, PM_W), BF16)] + [jax.ShapeDtypeStruct(_P_SHAPES[nm], F32) for nm in _P_NAMES]
    res = pl.pallas_call(
        body, grid=(N_CHUNK,), in_specs=in_specs, out_specs=out_specs, out_shape=out_shape,
        scratch_shapes=[pltpu.VMEM(_S_SHAPES[nm], F32) for nm in _S_NAMES] + [pltpu.VMEM((8, 512), F32)]
        + [pltpu.VMEM((HEADS, 128, 128), F32) for _ in range(2 * len(_P_BLOCKDIAG))],
        compiler_params=_params(("arbitrary",)), name="mixer_bwd",
    )(pm, pm, dab, *states, *[p[nm] for nm in _P_NAMES])
    return res[0], dict(zip(_P_NAMES, res[1:]))


def _tok(width):
    return pl.BlockSpec((TOK_TILE, width), lambda i: (i, 0))


def _once(shape):
    zeros = (0,) * len(shape)
    return pl.BlockSpec(shape, lambda i: zeros, pipeline_mode=pl.Buffered(1))


def _rms_fwd(x):
    r = lax.rsqrt(_mean(x * x) + EPS)
    return x * r, r


def _rms_bwd(dy, xn, r, g):
    gd = dy * g
    return r * (gd - xn * _mean(xn * gd))


def _in_proj(x, g_pre, wt_in):
    def body(x_ref, g_ref, wt_ref, pm_ref, gab_ref, h_ref):
        xn, _ = _rms_fwd(x_ref[...])
        h = (xn * g_ref[...]).astype(BF16)
        h_ref[...] = h
        pm_ref[:, 0:PM_XM] = _nt(h, wt_ref[0:IN_ALOW, :])
        pm_ref[:, PM_XM:PM_AL] = _nt(h, wt_ref[IN_XM:IN_GATES, :])
        pm_ref[:, PM_AL:PM_W] = _nt(h, wt_ref[IN_ALOW:IN_ALOW + 128, :])
        gab_ref[...] = _nt(h, wt_ref[IN_GATES:D_IN, :])

    return pl.pallas_call(
        body, grid=(N_TOK_TILE,),
        in_specs=[_tok(D_MODEL), _once((1, D_MODEL)), _once((D_IN, D_MODEL))],
        out_specs=[_tok(PM_W), _tok(GAB_W), _tok(D_MODEL)],
        out_shape=[jax.ShapeDtypeStruct((SEQ, PM_W), F32), jax.ShapeDtypeStruct((SEQ, GAB_W), F32),
                   jax.ShapeDtypeStruct((SEQ, D_MODEL), BF16)],
        compiler_params=_params(("arbitrary",)), name="in_proj",
    )(x, g_pre, wt_in)


def _merge_fwd(ab, gab, x, w_pa4, w_pb4, w_o, g_post):
    def body(ab_ref, gab_ref, x_ref, wpa_ref, wpb_ref, wo_ref, g_ref, x1_ref, mix_ref, mg_ref):
        a = ab_ref[:, 0:512]
        b = ab_ref[:, 512:1024]
        for j in range(N_CHIP):
            blk = slice(j * 256, (j + 1) * 256)
            ya = jnp.dot(a, wpa_ref[j], preferred_element_type=F32)
            yb = jnp.dot(b, wpb_ref[j], preferred_element_type=F32)
            sa = _sigmoid(gab_ref[:, j * 256:(j + 1) * 256])
            sb = _sigmoid(gab_ref[:, 1024 + j * 256:1024 + (j + 1) * 256])
            mg_ref[:, blk] = (sa * ya + sb * yb).astype(BF16)
        mix = jnp.dot(mg_ref[...], wo_ref[...], preferred_element_type=F32)
        mix_ref[...] = mix
        mn, _ = _rms_fwd(mix)
        x1_ref[...] = x_ref[...] + mn * g_ref[...]

    return pl.pallas_call(
        body, grid=(N_TOK_TILE,),
        in_specs=[_tok(1024), _tok(GAB_W), _tok(D_MODEL), _once((N_CHIP, 512, 256)), _once((N_CHIP, 512, 256)),
                  _once((D_MODEL, D_MODEL)), _once((1, D_MODEL))],
        out_specs=[_tok(D_MODEL), _tok(D_MODEL), _tok(D_MODEL)],
        out_shape=[jax.ShapeDtypeStruct((SEQ, D_MODEL), F32), jax.ShapeDtypeStruct((SEQ, D_MODEL), F32),
                   jax.ShapeDtypeStruct((SEQ, D_MODEL), BF16)],
        compiler_params=_params(("arbitrary",)), name="merge_fwd",
    )(ab, gab, x, w_pa4, w_pb4, w_o, g_post)


def _mlp(x1, target, g_pre, g_post, w_up4, w_down):
    def body(x1_ref, t_ref, gpre_ref, gpost_ref, wup_ref, wdn_ref,
             dx1_ref, u_ref, dd_ref, h2_ref, dpre_ref, dgpost_ref, dgpre_ref, loss_ref):
        @pl.when(pl.program_id(0) == 0)
        def _():
            dgpost_ref[...] = jnp.zeros_like(dgpost_ref)
            dgpre_ref[...] = jnp.zeros_like(dgpre_ref)
            loss_ref[...] = jnp.zeros_like(loss_ref)

        x1 = x1_ref[...]
        gpre = gpre_ref[...]
        gpost = gpost_ref[...]
        xn2, r2 = _rms_fwd(x1)
        h2 = (xn2 * gpre).astype(BF16)
        h2_ref[...] = h2
        rl = []
        d = jnp.zeros((TOK_TILE, D_MODEL), F32)
        for j in range(N_CHIP):
            blk = slice(j * 1024, (j + 1) * 1024)
            r = jnp.maximum(jnp.dot(h2, wup_ref[j], preferred_element_type=F32), 0.0)
            rl.append(r)
            u = (r * r).astype(BF16)
            u_ref[:, blk] = u
            d = d + jnp.dot(u, wdn_ref[blk, :], preferred_element_type=F32)
        dn, r3 = _rms_fwd(d)
        diff = x1 + dn * gpost - t_ref[...]
        loss_ref[...] += jnp.sum(diff * diff, keepdims=True) * (0.5 / D_MODEL)
        dy = diff * (1.0 / D_MODEL)
        dgpost_ref[...] += jnp.sum(dy * dn, axis=0, keepdims=True)
        dd = _rms_bwd(dy, dn, r3, gpost).astype(BF16)
        dd_ref[...] = dd
        dh2 = jnp.zeros((TOK_TILE, D_MODEL), F32)
        for j in range(N_CHIP):
            blk = slice(j * 1024, (j + 1) * 1024)
            dpre = (_nt(dd, wdn_ref[blk, :]) * (2.0 * rl[j])).astype(BF16)
            dpre_ref[:, blk] = dpre
            dh2 = dh2 + _nt(dpre, wup_ref[j])
        dgpre_ref[...] += jnp.sum(dh2 * xn2, axis=0, keepdims=True)
        dx1_ref[...] = dy + _rms_bwd(dh2, xn2, r2, gpre)

    acc = pl.BlockSpec((1, D_MODEL), lambda i: (0, 0))
    return pl.pallas_call(
        body, grid=(N_TOK_TILE,),
        in_specs=[_tok(D_MODEL), _tok(D_MODEL), _once((1, D_MODEL)), _once((1, D_MODEL)),
                  _once((N_CHIP, D_MODEL, 1024)), _once((D_FF, D_MODEL))],
        out_specs=[_tok(D_MODEL), _tok(D_FF), _tok(D_MODEL), _tok(D_MODEL), _tok(D_FF), acc, acc,
                   pl.BlockSpec((1, 128), lambda i: (0, 0))],
        out_shape=[jax.ShapeDtypeStruct((SEQ, D_MODEL), F32), jax.ShapeDtypeStruct((SEQ, D_FF), BF16),
                   jax.ShapeDtypeStruct((SEQ, D_MODEL), BF16), jax.ShapeDtypeStruct((SEQ, D_MODEL), BF16),
                   jax.ShapeDtypeStruct((SEQ, D_FF), BF16), jax.ShapeDtypeStruct((1, D_MODEL), F32),
                   jax.ShapeDtypeStruct((1, D_MODEL), F32), jax.ShapeDtypeStruct((1, 128), F32)],
        compiler_params=_params(("arbitrary",)), name="mlp_fwd_bwd",
    )(x1, target, g_pre, g_post, w_up4, w_down)


def _merge_bwd(dx1, mix, ab, gab, w_pa4, w_pb4, w_o, g_post):
    def body(dx1_ref, mix_ref, ab_ref, gab_ref, wpa_ref, wpb_ref, wo_ref, g_ref,
             dmix_ref, dya_ref, dyb_ref, dgab_ref, dab_ref, dg_ref):
        @pl.when(pl.program_id(0) == 0)
        def _():
            dg_ref[...] = jnp.zeros_like(dg_ref)

        dx1 = dx1_ref[...]
        mn, r = _rms_fwd(mix_ref[...])
        dg_ref[...] += jnp.sum(dx1 * mn, axis=0, keepdims=True)
        dmix = _rms_bwd(dx1, mn, r, g_ref[...]).astype(BF16)
        dmix_ref[...] = dmix
        dmerged = _nt(dmix, wo_ref[...])
        a = ab_ref[:, 0:512]
        b = ab_ref[:, 512:1024]
        da = jnp.zeros((TOK_TILE, 512), F32)
        db = jnp.zeros((TOK_TILE, 512), F32)
        for j in range(N_CHIP):
            blk = slice(j * 256, (j + 1) * 256)
            blk_b = slice(1024 + j * 256, 1024 + (j + 1) * 256)
            dm = dmerged[:, blk]
            ya = jnp.dot(a, wpa_ref[j], preferred_element_type=F32)
            yb = jnp.dot(b, wpb_ref[j], preferred_element_type=F32)
            sa = _sigmoid(gab_ref[:, blk])
            sb = _sigmoid(gab_ref[:, blk_b])
            dya = (dm * sa).astype(BF16)
            dyb = (dm * sb).astype(BF16)
            dya_ref[:, blk] = dya
            dyb_ref[:, blk] = dyb
            dgab_ref[:, blk] = (dm * ya * sa * (1.0 - sa)).astype(BF16)
            dgab_ref[:, blk_b] = (dm * yb * sb * (1.0 - sb)).astype(BF16)
            da = da + _nt(dya, wpa_ref[j])
            db = db + _nt(dyb, wpb_ref[j])
        dab_ref[:, 0:512] = da
        dab_ref[:, 512:1024] = db

    return pl.pallas_call(
        body, grid=(N_TOK_TILE,),
        in_specs=[_tok(D_MODEL), _tok(D_MODEL), _tok(1024), _tok(GAB_W), _once((N_CHIP, 512, 256)),
                  _once((N_CHIP, 512, 256)), _once((D_MODEL, D_MODEL)), _once((1, D_MODEL))],
        out_specs=[_tok(D_MODEL), _tok(D_MODEL), _tok(D_MODEL), _tok(GAB_W), _tok(1024),
                   pl.BlockSpec((1, D_MODEL), lambda i: (0, 0))],
        out_shape=[jax.ShapeDtypeStruct((SEQ, D_MODEL), BF16), jax.ShapeDtypeStruct((SEQ, D_MODEL), BF16),
                   jax.ShapeDtypeStruct((SEQ, D_MODEL), BF16), jax.ShapeDtypeStruct((SEQ, GAB_W), BF16),
                   jax.ShapeDtypeStruct((SEQ, 1024), F32), jax.ShapeDtypeStruct((1, D_MODEL), F32)],
        compiler_params=_params(("arbitrary",)), name="merge_bwd",
    )(dx1, mix, ab, gab, w_pa4, w_pb4, w_o, g_post)


def _in_proj_bwd(dpm, dgab, x, dx1, g_pre, wt_in):
    def body(dpm_ref, dgab_ref, x_ref, dx1_ref, g_ref, wt_ref, dx_ref, dg_ref):
        @pl.when(pl.program_id(0) == 0)
        def _():
            dg_ref[...] = jnp.zeros_like(dg_ref)

        dh = jnp.dot(dpm_ref[:, 0:PM_XM], wt_ref[0:IN_ALOW, :], preferred_element_type=F32)
        dh = dh + jnp.dot(dpm_ref[:, PM_XM:PM_AL], wt_ref[IN_XM:IN_GATES, :], preferred_element_type=F32)
        dh = dh + jnp.dot(dpm_ref[:, PM_AL:PM_W], wt_ref[IN_ALOW:IN_ALOW + 128, :], preferred_element_type=F32)
        dh = dh + jnp.dot(dgab_ref[...], wt_ref[IN_GATES:D_IN, :], preferred_element_type=F32)
        xn, r = _rms_fwd(x_ref[...])
        dg_ref[...] += jnp.sum(dh * xn, axis=0, keepdims=True)
        dx_ref[...] = dx1_ref[...] + _rms_bwd(dh, xn, r, g_ref[...])

    return pl.pallas_call(
        body, grid=(N_TOK_TILE,),
        in_specs=[_tok(PM_W), _tok(GAB_W), _tok(D_MODEL), _tok(D_MODEL), _once((1, D_MODEL)), _once((D_IN, D_MODEL))],
        out_specs=[_tok(D_MODEL), pl.BlockSpec((1, D_MODEL), lambda i: (0, 0))],
        out_shape=[jax.ShapeDtypeStruct((SEQ, D_MODEL), F32), jax.ShapeDtypeStruct((1, D_MODEL), F32)],
        compiler_params=_params(("arbitrary",)), name="in_proj_bwd",
    )(dpm, dgab, x, dx1, g_pre, wt_in)


def _dw_in(dpm, dgab, h):
    n_pm = PM_AL // 512
    n_blk = n_pm + GAB_W // 512

    def body(dpm_ref, dgab_ref, dal_ref, h_ref, o_ref):
        i = pl.program_id(0)
        off = pl.multiple_of(i * 512 + 16 * (i >= 3).astype(jnp.int32), 16)

        @pl.when(i < n_pm)
        def _():
            o_ref[pl.ds(off, 512), :] = _tn(dpm_ref[...], h_ref[...]).astype(BF16)

        @pl.when(i >= n_pm)
        def _():
            o_ref[pl.ds(off, 512), :] = _tn(dgab_ref[...], h_ref[...]).astype(BF16)

        @pl.when(i == 0)
        def _():
            o_ref[IN_ALOW:IN_XM, :] = _tn(dal_ref[...], h_ref[...])[0:IN_XM - IN_ALOW].astype(BF16)

    return pl.pallas_call(
        body, grid=(n_blk,),
        in_specs=[pl.BlockSpec((SEQ, 512), lambda i: (0, jnp.minimum(i, n_pm - 1))),
                  pl.BlockSpec((SEQ, 512), lambda i: (0, jnp.maximum(i - n_pm, 0))),
                  pl.BlockSpec((SEQ, 128), lambda i: (0, PM_AL // 128)),
                  _once((SEQ, D_MODEL))],
        out_specs=pl.BlockSpec((D_IN, D_MODEL), lambda i: (0, 0)),
        out_shape=jax.ShapeDtypeStruct((D_IN, D_MODEL), BF16),
        compiler_params=_params(("arbitrary",)), name="dw_in",
    )(dpm, dgab, dpm, h)


def _tn_matmul(a, b, name, shards=1, tm=512):
    m, n = a.shape[1], b.shape[1]
    tm = min(tm, m)
    tn = n // shards if shards > 1 else min(n, 1024)

    def body(a_ref, b_ref, o_ref):
        o_ref[...] = _tn(a_ref[...], b_ref[...]).astype(BF16)

    if shards > 1:
        out_spec = pl.BlockSpec((None, tm, tn), lambda i, j: (j, i, 0))
        out_shape = jax.ShapeDtypeStruct((shards, m, tn), BF16)
    else:
        out_spec = pl.BlockSpec((tm, tn), lambda i, j: (i, j))
        out_shape = jax.ShapeDtypeStruct((m, n), BF16)
    return pl.pallas_call(
        body, grid=(m // tm, n // tn),
        in_specs=[pl.BlockSpec((SEQ, tm), lambda i, j: (0, i)), pl.BlockSpec((SEQ, tn), lambda i, j: (0, j))],
        out_specs=out_spec, out_shape=out_shape,
        compiler_params=_params(("arbitrary", "arbitrary")), name=name,
    )(a, b)


MESH = pl.DeviceIdType.MESH
ANY = pl.BlockSpec(memory_space=pl.ANY)
VMEM_WHOLE = pl.BlockSpec(memory_space=pltpu.VMEM)

_BIG = ("w_in", "w_pa", "w_pb", "w_o", "w_up", "w_down")
_BIG_SHARD = {"w_in": (IN_SHARD, D_MODEL), "w_pa": (512, 256), "w_pb": (512, 256), "w_o": (256, D_MODEL),
              "w_up": (D_MODEL, 1024), "w_down": (1024, D_MODEL)}
_BIG_SPLIT = {"w_in": 1, "w_pa": 0, "w_pb": 0, "w_o": 0, "w_up": 0, "w_down": 0}


def _half(ref, e, name, lead=0):
    axis = _BIG_SPLIT[name]
    size = _BIG_SHARD[name][axis] // 2
    start = pl.multiple_of(e * size, 128 if axis == 1 else 16)
    idx = [pl.ds(0, ref.shape[a]) for a in range(lead)]
    idx += [pl.ds(start, size), pl.ds(0, _BIG_SHARD[name][1])] if axis == 0 else [pl.ds(0, _BIG_SHARD[name][0]), pl.ds(start, size)]
    return ref.at[tuple(idx)]


def _half_shape(name):
    r, c = _BIG_SHARD[name]
    return (r // 2, c) if _BIG_SPLIT[name] == 0 else (r, c // 2)


def _remote(src, dst, send_sems, recv_sems, k, to):
    return pltpu.make_async_remote_copy(src_ref=src, dst_ref=dst, send_sem=send_sems.at[k], recv_sem=recv_sems.at[k],
                                        device_id=to, device_id_type=MESH)


def _gather_shards(big, small):
    nb, n = len(big), len(big) + len(small)

    def body(*refs):
        ins, outs = refs[:n], refs[n:2 * n]
        send_sems, recv_sems = refs[2 * n:]
        x, y, c = lax.axis_index("x"), lax.axis_index("y"), lax.axis_index("c")
        me = 2 * x + y
        sibling = (x, y, 1 - c)
        peers = [(1 - x, y), (x, 1 - y), (1 - x, 1 - y)]
        sends = []
        for k in range(n):
            for j, (px, py) in enumerate(peers):
                if k < nb:
                    cp = _remote(_half(ins[k], c, _BIG[k]), _half(outs[k].at[me], c, _BIG[k]),
                                 send_sems, recv_sems, 6 * k + j, (px, py, c))
                else:
                    cp = _remote(ins[k], outs[k].at[me], send_sems, recv_sems, 6 * k + j, (px, py, c))
                cp.start()
                sends.append(cp)
        for k in range(n):
            outs[k][me] = ins[k][...]
        for j, (px, py) in enumerate(peers):
            pj = 2 * px + py
            for k in range(nb):
                landed = _half(outs[k].at[pj], c, _BIG[k])
                _remote(landed, landed, send_sems, recv_sems, 6 * k + j, (px, py, c)).wait_recv()
                cp = _remote(landed, landed, send_sems, recv_sems, 6 * k + 3 + j, sibling)
                cp.start()
                sends.append(cp)
        for j, (px, py) in enumerate(peers):
            pj = 2 * px + py
            for k in range(n):
                if k < nb:
                    passed = _half(outs[k].at[pj], 1 - c, _BIG[k])
                    _remote(passed, passed, send_sems, recv_sems, 6 * k + 3 + j, sibling).wait_recv()
                else:
                    _remote(ins[k], outs[k].at[pj], send_sems, recv_sems, 6 * k + j, (px, py, c)).wait_recv()
        for cp in sends:
            cp.wait_send()

    shards = list(big) + list(small)
    return pl.pallas_call(
        body, in_specs=[VMEM_WHOLE] * n, out_specs=[VMEM_WHOLE] * n,
        out_shape=[jax.ShapeDtypeStruct((N_CHIP,) + s.shape, s.dtype) for s in shards],
        scratch_shapes=[pltpu.SemaphoreType.DMA((6 * n,)), pltpu.SemaphoreType.DMA((6 * n,))],
        compiler_params=_params(), name="gather_weights",
    )(*shards)


def _swap_halves(grads):
    n = len(grads)

    def body(*refs):
        g_refs, got_refs, stage_refs = refs[:n], refs[n:2 * n], refs[2 * n:3 * n]
        send_sems, recv_sems, local_sems = refs[3 * n:]
        x, y, c = lax.axis_index("x"), lax.axis_index("y"), lax.axis_index("c")
        staged = []
        for k in range(n):
            cp = pltpu.make_async_copy(_half(g_refs[k], 1 - c, _BIG[k], lead=1), stage_refs[k], local_sems.at[k])
            cp.start()
            staged.append(cp)
        sends = []
        for k in range(n):
            staged[k].wait()
            cp = _remote(stage_refs[k], got_refs[k], send_sems, recv_sems, k, (x, y, 1 - c))
            cp.start()
            sends.append(cp)
        for cp in sends:
            cp.wait()

    half = [jax.ShapeDtypeStruct((N_CHIP,) + _half_shape(nm), BF16) for nm in _BIG]
    return pl.pallas_call(
        body, in_specs=[ANY] * n, out_specs=[VMEM_WHOLE] * n, out_shape=half,
        scratch_shapes=[pltpu.VMEM(h.shape, h.dtype) for h in half]
        + [pltpu.SemaphoreType.DMA((n,)), pltpu.SemaphoreType.DMA((n,)), pltpu.SemaphoreType.DMA((n,))],
        compiler_params=_params(), name="swap_halves",
    )(*grads)


def _send_partials(partials, small):
    n = len(partials)

    def body(*refs):
        q_refs, small_ref = refs[:n], refs[n]
        o_refs, osmall_ref = refs[n + 1:2 * n + 1], refs[2 * n + 1]
        send_sems, recv_sems, local_sems = refs[2 * n + 2:]
        x, y, c = lax.axis_index("x"), lax.axis_index("y"), lax.axis_index("c")
        me = 2 * x + y
        me8 = 4 * x + 2 * y + c
        peers = [(1 - x, y), (x, 1 - y), (1 - x, 1 - y)]
        others = [(x, y, 1 - c), (1 - x, y, c), (1 - x, y, 1 - c), (x, 1 - y, c), (x, 1 - y, 1 - c),
                  (1 - x, 1 - y, c), (1 - x, 1 - y, 1 - c)]
        waits = []
        for k in range(n):
            cp = pltpu.make_async_copy(q_refs[k].at[me], o_refs[k].at[me], local_sems.at[k])
            cp.start()
            waits.append(cp.wait)
        cp = pltpu.make_async_copy(small_ref, osmall_ref.at[me8], local_sems.at[n])
        cp.start()
        waits.append(cp.wait)
        for k in range(n):
            for j, (px, py) in enumerate(peers):
                cp = _remote(q_refs[k].at[2 * px + py], o_refs[k].at[me], send_sems, recv_sems, 3 * k + j, (px, py, c))
                cp.start()
                waits.append(cp.wait_send)
        for r, (px, py, pc) in enumerate(others):
            cp = _remote(small_ref, osmall_ref.at[me8], send_sems, recv_sems, 3 * n + r, (px, py, pc))
            cp.start()
            waits.append(cp.wait_send)
        for k in range(n):
            for j, (px, py) in enumerate(peers):
                _remote(q_refs[k].at[me], o_refs[k].at[2 * px + py], send_sems, recv_sems, 3 * k + j, (px, py, c)).wait_recv()
        for r, (px, py, pc) in enumerate(others):
            _remote(small_ref, osmall_ref.at[4 * px + 2 * py + pc], send_sems, recv_sems, 3 * n + r, (px, py, pc)).wait_recv()
        for w in waits:
            w()

    out_shape = [jax.ShapeDtypeStruct(q.shape, q.dtype) for q in partials]
    out_shape.append(jax.ShapeDtypeStruct((N_DEV,) + small.shape, small.dtype))
    res = pl.pallas_call(
        body, in_specs=[ANY] * (n + 1), out_specs=[ANY] * (n + 1), out_shape=out_shape,
        scratch_shapes=[pltpu.SemaphoreType.DMA((3 * n + 7,)), pltpu.SemaphoreType.DMA((3 * n + 7,)),
                        pltpu.SemaphoreType.DMA((n + 1,))],
        name="send_partials",
    )(*partials, small)
    return res[:n], res[n]


def _swap_sums(sums):
    n = len(sums)

    def body(*refs):
        s_refs, o_refs = refs[:n], refs[n:2 * n]
        send_sems, recv_sems = refs[2 * n:]
        x, y, c = lax.axis_index("x"), lax.axis_index("y"), lax.axis_index("c")
        sends = []
        for k in range(n):
            cp = _remote(s_refs[k], _half(o_refs[k], c, _BIG[k]), send_sems, recv_sems, k, (x, y, 1 - c))
            cp.start()
            sends.append(cp)
        for e in range(2):
            @pl.when(c == e)
            def _():
                for k in range(n):
                    r, cols = _half_shape(_BIG[k])
                    if _BIG_SPLIT[_BIG[k]] == 0:
                        o_refs[k][e * r:(e + 1) * r, :] = s_refs[k][...]
                    else:
                        o_refs[k][:, e * cols:(e + 1) * cols] = s_refs[k][...]
        for k in range(n):
            theirs = _half(o_refs[k], 1 - c, _BIG[k])
            _remote(s_refs[k], theirs, send_sems, recv_sems, k, (x, y, 1 - c)).wait_recv()
        for cp in sends:
            cp.wait_send()

    return pl.pallas_call(
        body, in_specs=[VMEM_WHOLE] * n, out_specs=[VMEM_WHOLE] * n,
        out_shape=[jax.ShapeDtypeStruct(_BIG_SHARD[nm], F32) for nm in _BIG],
        scratch_shapes=[pltpu.SemaphoreType.DMA((n,)), pltpu.SemaphoreType.DMA((n,))],
        compiler_params=_params(), name="swap_sums",
    )(*sums)


def _tile(rows, cols, itemsize, budget):
    t = cols if rows % 16 else rows
    other = rows if rows % 16 else cols
    step = 256 if rows % 16 else 32
    while t % step == 0 and t * other * itemsize > budget:
        t //= 2
    return (rows, t) if rows % 16 else (t, cols)


def _add_halves(core, name, g, b):
    _, hr, hc = b.shape
    tr, tc = _tile(hr, hc, 2, 2 * 1024 * 1024)
    nr, nc = hr // tr, hc // tc
    by_rows = _BIG_SPLIT[name] == 0

    def body(core_ref, a_ref, b_ref, o_ref):
        o_ref[...] = (a_ref[...].astype(F32) + b_ref[...].astype(F32)).astype(BF16)

    blk = pl.BlockSpec((1, tr, tc), lambda j, i, l, core_ref: (j, i, l))
    if by_rows:
        g_blk = pl.BlockSpec((1, tr, tc), lambda j, i, l, core_ref: (j, core_ref[0] * nr + i, l))
    else:
        g_blk = pl.BlockSpec((1, tr, tc), lambda j, i, l, core_ref: (j, i, core_ref[0] * nc + l))
    grid_spec = pltpu.PrefetchScalarGridSpec(num_scalar_prefetch=1, grid=(N_CHIP, nr, nc), in_specs=[g_blk, blk],
                                             out_specs=blk)
    return pl.pallas_call(body, grid_spec=grid_spec, out_shape=jax.ShapeDtypeStruct(b.shape, BF16),
                          compiler_params=_params(("arbitrary",) * 3), name="presum_" + name)(core, g, b)


def _sum_chips(name, parts):
    _, hr, hc = parts.shape
    tr, tc = _tile(hr, hc, 4, 1024 * 1024)

    def body(p_ref, o_ref):
        g = p_ref[0].astype(F32)
        for s in range(1, N_CHIP):
            g = g + p_ref[s].astype(F32)
        o_ref[...] = g

    return pl.pallas_call(body, grid=(hr // tr, hc // tc),
                          in_specs=[pl.BlockSpec((N_CHIP, tr, tc), lambda i, l: (0, i, l))],
                          out_specs=pl.BlockSpec((tr, tc), lambda i, l: (i, l)),
                          out_shape=jax.ShapeDtypeStruct((hr, hc), F32),
                          compiler_params=_params(("arbitrary", "arbitrary")), name="sum_" + name)(parts)


def _adamw_math(w, g, m, v):
    m = ADAM_B1 * m + (1.0 - ADAM_B1) * g
    v = ADAM_B2 * v + (1.0 - ADAM_B2) * (g * g)
    m_hat = m / (1.0 - ADAM_B1 ** ADAM_STEP)
    v_hat = v / (1.0 - ADAM_B2 ** ADAM_STEP)
    delta = -ADAM_LR * (m_hat / (jnp.sqrt(v_hat) + ADAM_EPS) + ADAM_WD * w)
    return delta, m, v


def _adamw_big(g, w, m, v, name):
    r, c = w.shape
    tr, tc = _tile(r, c, 4, 1024 * 1024)

    def body(g_ref, w_ref, m_ref, v_ref, d_ref, nm_ref, nv_ref):
        d_ref[...], nm_ref[...], nv_ref[...] = _adamw_math(w_ref[...], g_ref[...], m_ref[...], v_ref[...])

    blk = pl.BlockSpec((tr, tc), lambda i, l: (i, l))
    return pl.pallas_call(
        body, grid=(r // tr, c // tc), in_specs=[blk, blk, blk, blk],
        out_specs=[blk, blk, blk], out_shape=[jax.ShapeDtypeStruct((r, c), F32)] * 3,
        compiler_params=_params(("arbitrary", "arbitrary")), name=name,
    )(g, w, m, v)


def _sum_small(parts):
    def body(p_ref, o_ref):
        g = p_ref[0]
        for d in range(1, N_DEV):
            g = g + p_ref[d]
        o_ref[...] = g

    return pl.pallas_call(body, out_shape=jax.ShapeDtypeStruct(parts.shape[1:], F32), name="sum_small")(parts)


def _adamw_small(w, g, m, v):
    def body(w_ref, g_ref, m_ref, v_ref, d_ref, nm_ref, nv_ref):
        d_ref[...], nm_ref[...], nv_ref[...] = _adamw_math(w_ref[...], g_ref[...], m_ref[...], v_ref[...])

    return pl.pallas_call(body, out_shape=[jax.ShapeDtypeStruct(w.shape, F32)] * 3, name="adamw_small")(w, g, m, v)


def _pack(arrs):
    flat = jnp.concatenate([a.reshape(-1) for a in arrs])
    rows = -(-flat.shape[0] // 1024) * 8
    return jnp.pad(flat, (0, rows * 128 - flat.shape[0])).reshape(rows, 128)


def _unpack(buf, shapes):
    flat = buf.reshape(-1)
    out, off = [], 0
    for s in shapes:
        size = 1
        for d in s:
            size *= d
        out.append(flat[off:off + size].reshape(s))
        off += size
    return out


def _block_rows(w):
    return jnp.pad(w.reshape(512, 4), ((0, 0), (0, 124)))


def _cols(a4):
    return jnp.transpose(a4, (1, 0, 2)).reshape(a4.shape[1], -1)


def _local_step(x, target, w, sp):
    sp = {n: (a.reshape(1, -1) if a.ndim == 1 else a) for n, a in sp.items()}
    wau = jnp.zeros((128, 256), F32).at[0:16].set(sp["w_a_up"])
    wif = jnp.zeros((1536, 128), F32).at[:, 0:8].set(sp["w_if"])
    bif = jnp.zeros((1, 128), F32).at[:, 0:8].set(sp["b_if"])
    p = {"wau": wau, "bau": sp["b_a_up"], "ggla": sp["g_gla_norm"], "cw": sp["conv_w"], "cb": sp["conv_b"],
         "wq": _block_rows(sp["w_q_ml"]), "wk": _block_rows(sp["w_k_ml"]), "wv": _block_rows(sp["w_v_ml"]),
         "wif": wif, "bif": bif, "skip": sp["ml_skip"], "gml": sp["g_ml_norm"]}

    pm, gab, h = _in_proj(x, sp["g_pre_mix"], w["w_in"])
    ab, *states = _mixer_fwd(pm, p)
    x1, mix, merged = _merge_fwd(ab, gab, x, w["w_pa"], w["w_pb"], w["w_o"], sp["g_post_mix"])
    dx1, u, dd, h2, dpre, dg_post_mlp, dg_pre_mlp, loss = _mlp(x1, target, sp["g_pre_mlp"], sp["g_post_mlp"],
                                                                w["w_up"], w["w_down"])
    dmix, dya, dyb, dgab, dab, dg_post_mix = _merge_bwd(dx1, mix, ab, gab, w["w_pa"], w["w_pb"], w["w_o"], sp["g_post_mix"])
    dpm, dp = _mixer_bwd(pm, dab, states, p)
    dx, dg_pre_mix = _in_proj_bwd(dpm, dgab, x, dx1, sp["g_pre_mix"], w["w_in"])

    big = {
        "w_in": _dw_in(dpm, dgab, h),
        "w_pa": _tn_matmul(ab[:, 0:512], dya, "dw_pa", shards=N_CHIP),
        "w_pb": _tn_matmul(ab[:, 512:1024], dyb, "dw_pb", shards=N_CHIP),
        "w_o": _tn_matmul(merged, dmix, "dw_o"),
        "w_up": _tn_matmul(h2, dpre, "dw_up", shards=N_CHIP),
        "w_down": _tn_matmul(u, dd, "dw_down"),
    }
    small = {
        "g_pre_mix": dg_pre_mix, "b_a_up": dp["bau"], "g_gla_norm": dp["ggla"], "conv_b": dp["cb"],
        "w_q_ml": dp["wq"][:, 0:4].reshape(128, 4, 4), "w_k_ml": dp["wk"][:, 0:4].reshape(128, 4, 4),
        "w_v_ml": dp["wv"][:, 0:4].reshape(128, 4, 4),
        "b_if": dp["bif"][:, 0:8], "ml_skip": dp["skip"], "g_ml_norm": dp["gml"], "g_post_mix": dg_post_mix,
        "g_pre_mlp": dg_pre_mlp, "g_post_mlp": dg_post_mlp, "w_a_up": dp["wau"][0:16], "conv_w": dp["cw"],
        "w_if": dp["wif"][:, 0:8], "loss": loss[:, 0:1],
    }
    return dx, big, small


_SMALL_REPL = ("g_pre_mix", "b_a_up", "g_gla_norm", "conv_b", "w_q_ml", "w_k_ml", "w_v_ml", "b_if", "ml_skip",
               "g_ml_norm", "g_post_mix", "g_pre_mlp", "g_post_mlp")
_SMALL_SHARDED = ("w_a_up", "conv_w", "w_if")
_SMALL_ORDER = _SMALL_REPL + _SMALL_SHARDED + ("loss",)
_WEIGHTS = ("g_pre_mix", "w_in", "w_a_up", "b_a_up", "g_gla_norm", "conv_w", "conv_b", "w_q_ml", "w_k_ml", "w_v_ml",
            "w_if", "b_if", "ml_skip", "g_ml_norm", "w_pa", "w_pb", "w_o", "g_post_mix", "g_pre_mlp", "w_up", "w_down",
            "g_post_mlp")


def _as_shard(name, a):
    return a[0].T if name == "w_in" else a[0]


def _from_shard(name, a):
    return (a.T if name == "w_in" else a)[None]


def kernel(x, g_pre_mix, w_in, w_a_up, b_a_up, g_gla_norm, conv_w, conv_b, w_q_ml, w_k_ml, w_v_ml, w_if, b_if, ml_skip, g_ml_norm, w_pa, w_pb, w_o, g_post_mix, g_pre_mlp, w_up, w_down, g_post_mlp, loss_target, m_g_pre_mix, m_w_in, m_w_a_up, m_b_a_up, m_g_gla_norm, m_conv_w, m_conv_b, m_w_q_ml, m_w_k_ml, m_w_v_ml, m_w_if, m_b_if, m_ml_skip, m_g_ml_norm, m_w_pa, m_w_pb, m_w_o, m_g_post_mix, m_g_pre_mlp, m_w_up, m_w_down, m_g_post_mlp, v_g_pre_mix, v_w_in, v_w_a_up, v_b_a_up, v_g_gla_norm, v_conv_w, v_conv_b, v_w_q_ml, v_w_k_ml, v_w_v_ml, v_w_if, v_b_if, v_ml_skip, v_g_ml_norm, v_w_pa, v_w_pb, v_w_o, v_g_post_mix, v_g_pre_mlp, v_w_up, v_w_down, v_g_post_mlp):
    args = dict(locals())
    wts = {n: _as_shard(n, args[n]) for n in _WEIGHTS}
    mom = {n: _as_shard(n, args["m_" + n]) for n in _WEIGHTS}
    var = {n: _as_shard(n, args["v_" + n]) for n in _WEIGHTS}
    chip = 2 * lax.axis_index("x") + lax.axis_index("y")
    core = lax.axis_index("c").astype(jnp.int32).reshape(1)

    gathered = dict(zip(_BIG + _SMALL_SHARDED,
                        _gather_shards([wts[n].astype(BF16) for n in _BIG], [wts[n] for n in _SMALL_SHARDED])))
    full = {n: gathered[n] for n in ("w_pa", "w_pb", "w_up")}
    full["w_in"] = gathered["w_in"].reshape(D_IN, D_MODEL)
    full["w_o"] = gathered["w_o"].reshape(D_MODEL, D_MODEL)
    full["w_down"] = gathered["w_down"].reshape(D_FF, D_MODEL)
    sp = {n: wts[n] for n in _SMALL_REPL}
    sp["w_a_up"] = _cols(gathered["w_a_up"])
    sp["conv_w"] = _cols(gathered["conv_w"])
    sp["w_if"] = gathered["w_if"].reshape(1536, 8)

    dx, big, small = _local_step(x[0], loss_target[0], full, sp)

    pieces = [big[n].reshape((N_CHIP,) + _BIG_SHARD[n]) for n in _BIG]
    small_shapes = [small[n].shape for n in _SMALL_ORDER]
    partials = [_add_halves(core, n, g, b) for n, g, b in zip(_BIG, pieces, _swap_halves(pieces))]
    parts, small_parts = _send_partials(partials, _pack([small[n] for n in _SMALL_ORDER]))
    sums = _swap_sums([_sum_chips(n, p4) for n, p4 in zip(_BIG, parts)])

    grads, delta, new_m, new_v = {}, {}, {}, {}
    for n, g in zip(_BIG, sums):
        grads[n] = g
        delta[n], new_m[n], new_v[n] = _adamw_big(g, wts[n], mom[n], var[n], "adamw_" + n)
    summed = dict(zip(_SMALL_ORDER, _unpack(_sum_small(small_parts), small_shapes)))
    loss = summed["loss"].reshape(())
    for n in _SMALL_REPL:
        grads[n] = summed[n].reshape(wts[n].shape)
    grads["w_a_up"] = lax.dynamic_slice_in_dim(summed["w_a_up"], chip * 64, 64, axis=1)
    grads["conv_w"] = lax.dynamic_slice_in_dim(summed["conv_w"], chip * 128, 128, axis=1)
    grads["w_if"] = lax.dynamic_slice_in_dim(summed["w_if"], chip * 384, 384, axis=0)
    small_names = _SMALL_REPL + _SMALL_SHARDED
    shard_shapes = [wts[n].shape for n in small_names]
    upd = _adamw_small(_pack([wts[n] for n in small_names]), _pack([grads[n] for n in small_names]),
                       _pack([mom[n] for n in small_names]), _pack([var[n] for n in small_names]))
    for dst, buf in zip((delta, new_m, new_v), upd):
        for n, a in zip(small_names, _unpack(buf, shard_shapes)):
            dst[n] = a

    outs = [loss, dx[None]]
    for group in (grads, delta, new_m, new_v):
        outs += [_from_shard(n, group[n]) for n in _WEIGHTS]
    return tuple(outs)
```

```python
import functools

import jax
import jax.numpy as jnp
from jax import lax
from jax.experimental import pallas as pl
from jax.experimental.pallas import tpu as pltpu

F32 = jnp.float32
BF16 = jnp.bfloat16

SEQ = 2048
D_MODEL = 1024
CHUNK = 64
N_CHUNK = SEQ // CHUNK
HEADS = 4
GLA_DK = 64
GLA_DV = 128
ML_DH = 128
D_FF = 4096
EPS = 1e-6
N_CHIP = 4
N_DEV = 8
TOK_TILE = 256
N_TOK_TILE = SEQ // TOK_TILE
RIDE_PASS_ON = N_CHUNK - 3

PM_W = 2688
PM_XM = 1536
PM_OP = 2048
PM_AL = 2560
GAB_W = 2048
D_IN = 4624
IN_SHARD = D_IN // N_CHIP
IN_ALOW = 1536
IN_XM = 1552
IN_GATES = 2576

ADAM_LR = 0.001
ADAM_B1 = 0.9
ADAM_B2 = 0.999
ADAM_EPS = 1e-08
ADAM_WD = 0.01
ADAM_STEP = 10

VMEM_LIMIT = 56 * 1024 * 1024


def _params(sem=None):
    return pltpu.CompilerParams(dimension_semantics=sem, vmem_limit_bytes=VMEM_LIMIT)


def _dot(a, b, ca, cb):
    return lax.dot_general(a.astype(BF16), b.astype(BF16), (((ca,), (cb,)), ((), ())), preferred_element_type=F32)


def _pmm_nn(a, b):
    return _dot(a, b, 1, 0)


def _pmm_nt(a, b):
    return _dot(a, b, 1, 1)


def _pmm_tn(a, b):
    return _dot(a, b, 0, 0)


def _pcmm(c, x):
    return lax.dot_general(c, x, (((1,), (0,)), ((), ())), precision=lax.Precision.HIGHEST, preferred_element_type=F32)


@jax.custom_vjp
def _mm_nn(a, b):
    return _dot(a, b, 1, 0)


@jax.custom_vjp
def _mm_nt(a, b):
    return _dot(a, b, 1, 1)


@jax.custom_vjp
def _mm_tn(a, b):
    return _dot(a, b, 0, 0)


_mm_nn.defvjp(lambda a, b: (_dot(a, b, 1, 0), (a, b)), lambda r, g: (_mm_nt(g, r[1]), _mm_tn(r[0], g)))
_mm_nt.defvjp(lambda a, b: (_dot(a, b, 1, 1), (a, b)), lambda r, g: (_mm_nn(g, r[1]), _mm_tn(g, r[0])))
_mm_tn.defvjp(lambda a, b: (_dot(a, b, 0, 0), (a, b)), lambda r, g: (_mm_nt(r[1], g), _mm_nn(r[0], g)))


@jax.custom_vjp
def _cmm(c, x):
    return _pcmm(c, x)


_cmm.defvjp(
    lambda c, x: (_pcmm(c, x), c),
    lambda c, g: (jnp.zeros_like(c), lax.dot_general(c, g, (((0,), (0,)), ((), ())), precision=lax.Precision.HIGHEST,
                                                      preferred_element_type=F32)),
)

_PLAIN_OPS = (_pmm_nn, _pmm_nt, _pmm_tn, _pcmm)
_VJP_OPS = (_mm_nn, _mm_nt, _mm_tn, _cmm)


def _sigmoid(x):
    return 0.5 * (jnp.tanh(0.5 * x) + 1.0)


def _log_sigmoid(x):
    return jnp.minimum(x, 0.0) - jnp.log(1.0 + jnp.exp(-jnp.abs(x)))


def _mean(x):
    return jnp.mean(x, axis=-1, keepdims=True)


def _nt(a, b):
    return lax.dot_general(a, b, (((1,), (1,)), ((), ())), preferred_element_type=F32)


def _tn(a, b):
    return lax.dot_general(a, b, (((0,), (0,)), ((), ())), preferred_element_type=F32)


def _mixer_chunk(ops, p, st, pm, xprev8):
    mm_nn, mm_nt, mm_tn, cmm = ops
    row = lax.broadcasted_iota(jnp.int32, (CHUNK, CHUNK), 0)
    col = lax.broadcasted_iota(jnp.int32, (CHUNK, CHUNK), 1)
    causal = row >= col
    tri = causal.astype(F32)
    q = pm[:, 0:256]
    k = pm[:, 256:512]
    v = pm[:, 512:1024]
    g = pm[:, 1024:1536]
    xm = pm[:, PM_XM:PM_XM + 512]
    opre = pm[:, PM_OP:PM_OP + 512]
    alow = pm[:, PM_AL:PM_AL + 128]

    la = _log_sigmoid(mm_nn(alow, p["wau"]) + p["bau"]) * (1.0 / 16.0)
    cum = cmm(tri, la)
    cum_last = cum[CHUNK - 1:CHUNK, :]
    e_pos = jnp.exp(cum)
    e_neg = jnp.exp(-cum)
    qs = q * (GLA_DK ** -0.5)
    qp = qs * e_pos
    qn = qs * e_neg
    kp = k * e_pos
    kn = k * e_neg
    kl = k * jnp.exp(cum_last - cum)
    dec = jnp.exp(cum_last)
    outs = []
    s_new = []
    for h in range(HEADS):
        s6 = slice(h * GLA_DK, (h + 1) * GLA_DK)
        s12 = slice(h * GLA_DV, (h + 1) * GLA_DV)
        scores = jnp.where(causal, mm_nt(qp[:, s6], kn[:, s6]), mm_nt(qn[:, s6], kp[:, s6]))
        o = mm_nn(scores, v[:, s12]) + mm_nt(qp[:, s6], st["S"][h])
        s_new.append(st["S"][h] * dec[:, s6] + mm_tn(v[:, s12], kl[:, s6]))
        o = o * lax.rsqrt(_mean(o * o) + EPS) * p["ggla"]
        gh = g[:, s12]
        outs.append(o * (gh * _sigmoid(gh)))

    xx = jnp.concatenate([xprev8, xm], axis=0)
    pre = p["cb"]
    for j in range(4):
        pre = pre + p["cw"][j:j + 1, :] * xx[5 + j:5 + j + CHUNK, :]
    xc = pre * _sigmoid(pre)
    qm, km, vm = [], [], []
    for h in range(HEADS):
        s12 = slice(h * ML_DH, (h + 1) * ML_DH)
        qm.append(mm_nn(xc[:, s12], p["wq"][h]))
        km.append(mm_nn(xc[:, s12], p["wk"][h]))
        vm.append(mm_nn(xm[:, s12], p["wv"][h]))
    qcat = jnp.concatenate(qm, axis=1)
    kcat = jnp.concatenate(km, axis=1)
    vcat = jnp.concatenate(vm, axis=1)
    gates = (mm_nn(qcat, p["wif"][0:512]) + mm_nn(kcat, p["wif"][512:1024]) + mm_nn(vcat, p["wif"][1024:1536])
             + p["bif"])
    lf = _log_sigmoid(gates)
    fc = cmm(tri, lf)
    gates_t = gates.T
    fc_t = fc.T
    c_new, n_new, m_new = [], [], []
    for h in range(HEADS):
        s12 = slice(h * ML_DH, (h + 1) * ML_DH)
        li_c = gates[:, h:h + 1]
        fc_c = fc[:, 4 + h:5 + h]
        li_r = gates_t[h:h + 1, :]
        fc_r = fc_t[4 + h:5 + h, :]
        m_prev = st["m"][h][:, 0:1]
        log_d = li_r - jnp.abs(fc_c - fc_r)
        g_int = fc_c + m_prev
        m_t = jnp.maximum(g_int, jnp.max(log_d, axis=1, keepdims=True))
        ks = km[h] * (ML_DH ** -0.5)
        s = mm_nt(qm[h], ks) * jnp.exp(log_d - m_t)
        scl = jnp.exp(g_int - m_t)
        num = mm_nn(s, vm[h]) + scl * mm_nn(qm[h], st["C"][h])
        den = jnp.sum(s, axis=1, keepdims=True) + scl * jnp.sum(qm[h] * st["n"][h], axis=1, keepdims=True)
        den = jnp.maximum(jnp.abs(den), jnp.exp(-m_t))
        hc = num / den * _sigmoid(opre[:, s12])
        d0 = hc - _mean(hc)
        y = d0 * lax.rsqrt(_mean(d0 * d0) + EPS)
        outs.append(y * p["gml"][:, s12] + p["skip"][:, s12] * xc[:, s12])
        f_last = fc[CHUNK - 1:CHUNK, 4 + h:5 + h]
        a = f_last - fc_c + li_c
        m_loc = jnp.max(a, axis=0, keepdims=True)
        kw = ks * jnp.exp(a - m_loc)
        m_nx = jnp.maximum(f_last + m_prev, m_loc)
        sp = jnp.exp(f_last + m_prev - m_nx)
        sl = jnp.exp(m_loc - m_nx)
        c_new.append(sp * st["C"][h] + sl * mm_tn(kw, vm[h]))
        n_new.append(sp * st["n"][h] + sl * jnp.sum(kw, axis=0, keepdims=True))
        m_new.append(jnp.broadcast_to(m_nx, (1, ML_DH)))
    ab = jnp.concatenate(outs, axis=1)
    new = {"S": s_new, "C": c_new, "n": n_new, "m": m_new}
    return ab, new


_P_NAMES = ("wau", "bau", "ggla", "cw", "cb", "wq", "wk", "wv", "wif", "bif", "skip", "gml")
_P_SHAPES = {
    "wau": (128, 256), "bau": (1, 256), "ggla": (1, 128), "cw": (4, 512), "cb": (1, 512),
    "wq": (512, 128), "wk": (512, 128), "wv": (512, 128),
    "wif": (1536, 128), "bif": (1, 128), "skip": (1, 512), "gml": (1, 512),
}
_P_BLOCKDIAG = ("wq", "wk", "wv")
_S_NAMES = ("S", "C", "n", "m")
_S_SHAPES = {"S": (HEADS, GLA_DV, GLA_DK), "C": (HEADS, ML_DH, ML_DH), "n": (HEADS, 1, ML_DH), "m": (HEADS, 1, ML_DH)}


def _per_head(ref):
    return [ref[h] for h in range(HEADS)]


def _block_mask():
    r = lax.broadcasted_iota(jnp.int32, (128, 128), 0)
    c = lax.broadcasted_iota(jnp.int32, (128, 128), 1)
    same_block = (r >> 2) == (c >> 2)
    spread = jnp.logical_and(r < 4, (c & 3) == r)
    return same_block.astype(F32), spread.astype(F32)


def _expand_blockdiag(w_ref, dense_ref):
    same_block, spread = _block_mask()
    for h in range(HEADS):
        tiled = _pmm_nn(w_ref[h * 128:(h + 1) * 128, :], spread)
        dense_ref[h] = tiled * same_block


def _collect_blockdiag(ddense_ref, dw_ref):
    same_block, spread = _block_mask()
    for h in range(HEADS):
        dw_ref[h * 128:(h + 1) * 128, :] = lax.dot_general(
            ddense_ref[h] * same_block, spread, (((1,), (1,)), ((), ())), precision=lax.Precision.HIGHEST,
            preferred_element_type=F32)


def _const_spec(shape):
    zeros = (0,) * len(shape)
    return pl.BlockSpec(shape, lambda i: zeros)


def _split(refs, *counts):
    out, at = [], 0
    for c in counts:
        out.append(refs[at:at + c])
        at += c
    assert at == len(refs)
    return out


def _ride(rider, phases, cond, ins, outs, sems):
    if rider is None:
        return
    lands, (send_sems, recv_sems, flush_sems) = sems[:-3], sems[-3:]

    @pl.when(cond)
    def _():
        for phase in phases:
            getattr(rider, phase)(ins, lands, send_sems, recv_sems)
        if "last" in phases:
            flush = [pltpu.make_async_copy(lands[k], outs[k], flush_sems.at[k]) for k in range(len(outs))]
            for cp in flush:
                cp.start()
            for cp in flush:
                cp.wait()


def _rider_specs(rider, rider_ins):
    if rider is None:
        return [], [], [], []
    scratch = [pltpu.VMEM(s.shape, s.dtype) for s in rider.out_shape]
    scratch += [pltpu.SemaphoreType.DMA((rider.n_sems,)), pltpu.SemaphoreType.DMA((rider.n_sems,)),
                pltpu.SemaphoreType.DMA((len(rider.out_shape),))]
    return [VMEM_WHOLE] * len(rider_ins), [ANY] * len(rider.out_shape), list(rider.out_shape), scratch


def _mixer_fwd(pm, p, rider=None, rider_ins=()):
    n_p = len(_P_NAMES)
    r_in, r_out_specs, r_out_shape, r_sems = _rider_specs(rider, rider_ins)

    def body(*refs):
        (pm_ref, xprev_ref), p_list, ride_in, (ab_ref,), so_refs, ride_out, sc_refs, dense_list, sems = _split(
            refs, 2, n_p, len(r_in), 1, 4, len(r_out_specs), 4, 3, len(r_sems))
        p_refs = dict(zip(_P_NAMES, p_list))
        dense = dict(zip(_P_BLOCKDIAG, dense_list))
        n = pl.program_id(0)
        _ride(rider, ("first",), n == 0, ride_in, ride_out, sems)

        @pl.when(n == 0)
        def _():
            for r in sc_refs:
                r[...] = jnp.zeros_like(r)
            for nm in _P_BLOCKDIAG:
                _expand_blockdiag(p_refs[nm], dense[nm])

        st = {name: _per_head(r) for name, r in zip(_S_NAMES, sc_refs)}
        for name, r in zip(_S_NAMES, so_refs):
            for h in range(HEADS):
                r[0, h] = st[name][h]
        pv = {nm: (_per_head(dense[nm]) if nm in _P_BLOCKDIAG else p_refs[nm][...]) for nm in _P_NAMES}
        xprev8 = jnp.where(n > 0, xprev_ref[CHUNK - 8:CHUNK, :], 0.0)
        ab, new = _mixer_chunk(_PLAIN_OPS, pv, st, pm_ref[...], xprev8)
        ab_ref[...] = ab.astype(BF16)
        for name, r in zip(_S_NAMES, sc_refs):
            for h in range(HEADS):
                r[h] = new[name][h]
        _ride(rider, ("middle",), n == RIDE_PASS_ON, ride_in, ride_out, sems)
        _ride(rider, ("last",), n == N_CHUNK - 1, ride_in, ride_out, sems)

    in_specs = [pl.BlockSpec((CHUNK, PM_W), lambda i: (i, 0)),
                pl.BlockSpec((CHUNK, 512), lambda i: (jnp.maximum(i - 1, 0), PM_XM // 512))]
    in_specs += [_const_spec(_P_SHAPES[nm]) for nm in _P_NAMES] + r_in
    out_specs = [pl.BlockSpec((CHUNK, 1024), lambda i: (i, 0))]
    out_shape = [jax.ShapeDtypeStruct((SEQ, 1024), BF16)]
    for nm in _S_NAMES:
        shp = _S_SHAPES[nm]
        out_specs.append(pl.BlockSpec((1,) + shp, lambda i: (i, 0, 0, 0)))
        out_shape.append(jax.ShapeDtypeStruct((N_CHUNK,) + shp, F32))
    return pl.pallas_call(
        body, grid=(N_CHUNK,), in_specs=in_specs, out_specs=out_specs + r_out_specs, out_shape=out_shape + r_out_shape,
        scratch_shapes=[pltpu.VMEM(_S_SHAPES[nm], F32) for nm in _S_NAMES]
        + [pltpu.VMEM((HEADS, 128, 128), F32) for _ in _P_BLOCKDIAG] + r_sems,
        compiler_params=_params(("arbitrary",)), name="mixer_fwd",
    )(pm, pm, *[p[nm] for nm in _P_NAMES], *rider_ins)


def _mixer_bwd(pm, dab, states, p, rider=None, rider_ins=()):
    n_p = len(_P_NAMES)
    r_in, r_out_specs, r_out_shape, r_sems = _rider_specs(rider, rider_ins)

    def body(*refs):
        ((pm_ref, xprev_ref, dab_ref), si_refs, p_list, ride_in, (dpm_ref,), dp_list, ride_out, ds_refs, (carry_ref,),
         dense_list, ddense_list, sems) = _split(refs, 3, 4, n_p, len(r_in), 1, n_p, len(r_out_specs), 4, 1, 3, 3, len(r_sems))
        p_refs = dict(zip(_P_NAMES, p_list))
        dp_refs = dict(zip(_P_NAMES, dp_list))
        dense = dict(zip(_P_BLOCKDIAG, dense_list))
        ddense = dict(zip(_P_BLOCKDIAG, ddense_list))
        i = pl.program_id(0)
        n = N_CHUNK - 1 - i
        _ride(rider, ("first",), i == 0, ride_in, ride_out, sems)

        @pl.when(i == 0)
        def _():
            for r in ds_refs:
                r[...] = jnp.zeros_like(r)
            for nm in _P_NAMES:
                if nm in _P_BLOCKDIAG:
                    ddense[nm][...] = jnp.zeros_like(ddense[nm])
                    _expand_blockdiag(p_refs[nm], dense[nm])
                else:
                    dp_refs[nm][...] = jnp.zeros_like(dp_refs[nm])
            carry_ref[...] = jnp.zeros_like(carry_ref)

        st = {name: [r[0, h] for h in range(HEADS)] for name, r in zip(_S_NAMES, si_refs)}
        pv = {nm: (_per_head(dense[nm]) if nm in _P_BLOCKDIAG else p_refs[nm][...]) for nm in _P_NAMES}
        xprev8 = jnp.where(n > 0, xprev_ref[CHUNK - 8:CHUNK, :], 0.0)
        _, vjp = jax.vjp(functools.partial(_mixer_chunk, _VJP_OPS), pv, st, pm_ref[...], xprev8)
        dst = {name: _per_head(r) for name, r in zip(_S_NAMES, ds_refs)}
        dp, dst_prev, dpm, dxprev8 = vjp((dab_ref[...], dst))
        reach = jnp.concatenate([jnp.zeros((CHUNK - 8, 512), F32), carry_ref[...]], axis=0)
        dpm_ref[:, 0:PM_XM] = dpm[:, 0:PM_XM].astype(BF16)
        dpm_ref[:, PM_XM:PM_XM + 512] = (dpm[:, PM_XM:PM_XM + 512] + reach).astype(BF16)
        dpm_ref[:, PM_XM + 512:PM_W] = dpm[:, PM_XM + 512:PM_W].astype(BF16)
        carry_ref[...] = dxprev8
        for name, r in zip(_S_NAMES, ds_refs):
            for h in range(HEADS):
                r[h] = dst_prev[name][h]
        for nm in _P_NAMES:
            if nm in _P_BLOCKDIAG:
                for h in range(HEADS):
                    ddense[nm][h] += dp[nm][h]
            else:
                dp_refs[nm][...] += dp[nm]

        @pl.when(i == N_CHUNK - 1)
        def _():
            for nm in _P_BLOCKDIAG:
                _collect_blockdiag(ddense[nm], dp_refs[nm])

        _ride(rider, ("middle",), i == RIDE_PASS_ON, ride_in, ride_out, sems)
        _ride(rider, ("last",), i == N_CHUNK - 1, ride_in, ride_out, sems)

    rev = lambda i: (N_CHUNK - 1 - i, 0)
    in_specs = [pl.BlockSpec((CHUNK, PM_W), rev),
                pl.BlockSpec((CHUNK, 512), lambda i: (jnp.maximum(N_CHUNK - 2 - i, 0), PM_XM // 512)),
                pl.BlockSpec((CHUNK, 1024), rev)]
    for nm in _S_NAMES:
        in_specs.append(pl.BlockSpec((1,) + _S_SHAPES[nm], lambda i: (N_CHUNK - 1 - i, 0, 0, 0)))
    in_specs += [_const_spec(_P_SHAPES[nm]) for nm in _P_NAMES] + r_in
    out_specs = [pl.BlockSpec((CHUNK, PM_W), rev)] + [_const_spec(_P_SHAPES[nm]) for nm in _P_NAMES]
    out_shape = [jax.ShapeDtypeStruct((SEQ, PM_W), BF16)] + [jax.ShapeDtypeStruct(_P_SHAPES[nm], F32) for nm in _P_NAMES]
    res = pl.pallas_call(
        body, grid=(N_CHUNK,), in_specs=in_specs, out_specs=out_specs + r_out_specs, out_shape=out_shape + r_out_shape,
        scratch_shapes=[pltpu.VMEM(_S_SHAPES[nm], F32) for nm in _S_NAMES] + [pltpu.VMEM((8, 512), F32)]
        + [pltpu.VMEM((HEADS, 128, 128), F32) for _ in range(2 * len(_P_BLOCKDIAG))] + r_sems,
        compiler_params=_params(("arbitrary",)), name="mixer_bwd",
    )(pm, pm, dab, *states, *[p[nm] for nm in _P_NAMES], *rider_ins)
    return res[0], dict(zip(_P_NAMES, res[1:1 + n_p])), res[1 + n_p:]


def _tok(width):
    return pl.BlockSpec((TOK_TILE, width), lambda i: (i, 0))


def _once(shape):
    zeros = (0,) * len(shape)
    return pl.BlockSpec(shape, lambda i: zeros, pipeline_mode=pl.Buffered(1))


def _rms_fwd(x):
    r = lax.rsqrt(_mean(x * x) + EPS)
    return x * r, r


def _rms_bwd(dy, xn, r, g):
    gd = dy * g
    return r * (gd - xn * _mean(xn * gd))


def _in_proj(x, g_pre, wt_in):
    def body(x_ref, g_ref, wt_ref, pm_ref, gab_ref, h_ref):
        xn, _ = _rms_fwd(x_ref[...])
        h = (xn * g_ref[...]).astype(BF16)
        h_ref[...] = h
        pm_ref[:, 0:PM_XM] = _nt(h, wt_ref[0:IN_ALOW, :])
        pm_ref[:, PM_XM:PM_AL] = _nt(h, wt_ref[IN_XM:IN_GATES, :])
        pm_ref[:, PM_AL:PM_W] = _nt(h, wt_ref[IN_ALOW:IN_ALOW + 128, :])
        gab_ref[...] = _nt(h, wt_ref[IN_GATES:D_IN, :])

    return pl.pallas_call(
        body, grid=(N_TOK_TILE,),
        in_specs=[_tok(D_MODEL), _once((1, D_MODEL)), _once((D_IN, D_MODEL))],
        out_specs=[_tok(PM_W), _tok(GAB_W), _tok(D_MODEL)],
        out_shape=[jax.ShapeDtypeStruct((SEQ, PM_W), F32), jax.ShapeDtypeStruct((SEQ, GAB_W), F32),
                   jax.ShapeDtypeStruct((SEQ, D_MODEL), BF16)],
        compiler_params=_params(("arbitrary",)), name="in_proj",
    )(x, g_pre, wt_in)


def _merge_fwd(ab, gab, x, w_pa4, w_pb4, w_o, g_post):
    def body(ab_ref, gab_ref, x_ref, wpa_ref, wpb_ref, wo_ref, g_ref, x1_ref, mix_ref, mg_ref):
        a = ab_ref[:, 0:512]
        b = ab_ref[:, 512:1024]
        for j in range(N_CHIP):
            blk = slice(j * 256, (j + 1) * 256)
            ya = jnp.dot(a, wpa_ref[j], preferred_element_type=F32)
            yb = jnp.dot(b, wpb_ref[j], preferred_element_type=F32)
            sa = _sigmoid(gab_ref[:, j * 256:(j + 1) * 256])
            sb = _sigmoid(gab_ref[:, 1024 + j * 256:1024 + (j + 1) * 256])
            mg_ref[:, blk] = (sa * ya + sb * yb).astype(BF16)
        mix = jnp.dot(mg_ref[...], wo_ref[...], preferred_element_type=F32)
        mix_ref[...] = mix
        mn, _ = _rms_fwd(mix)
        x1_ref[...] = x_ref[...] + mn * g_ref[...]

    return pl.pallas_call(
        body, grid=(N_TOK_TILE,),
        in_specs=[_tok(1024), _tok(GAB_W), _tok(D_MODEL), _once((N_CHIP, 512, 256)), _once((N_CHIP, 512, 256)),
                  _once((D_MODEL, D_MODEL)), _once((1, D_MODEL))],
        out_specs=[_tok(D_MODEL), _tok(D_MODEL), _tok(D_MODEL)],
        out_shape=[jax.ShapeDtypeStruct((SEQ, D_MODEL), F32), jax.ShapeDtypeStruct((SEQ, D_MODEL), F32),
                   jax.ShapeDtypeStruct((SEQ, D_MODEL), BF16)],
        compiler_params=_params(("arbitrary",)), name="merge_fwd",
    )(ab, gab, x, w_pa4, w_pb4, w_o, g_post)


def _mlp(x1, target, g_pre, g_post, w_up4, w_down):
    def body(x1_ref, t_ref, gpre_ref, gpost_ref, wup_ref, wdn_ref,
             dx1_ref, u_ref, dd_ref, h2_ref, dpre_ref, dgpost_ref, dgpre_ref, loss_ref):
        @pl.when(pl.program_id(0) == 0)
        def _():
            dgpost_ref[...] = jnp.zeros_like(dgpost_ref)
            dgpre_ref[...] = jnp.zeros_like(dgpre_ref)
            loss_ref[...] = jnp.zeros_like(loss_ref)

        x1 = x1_ref[...]
        gpre = gpre_ref[...]
        gpost = gpost_ref[...]
        xn2, r2 = _rms_fwd(x1)
        h2 = (xn2 * gpre).astype(BF16)
        h2_ref[...] = h2
        rl = []
        d = jnp.zeros((TOK_TILE, D_MODEL), F32)
        for j in range(N_CHIP):
            blk = slice(j * 1024, (j + 1) * 1024)
            r = jnp.maximum(jnp.dot(h2, wup_ref[j], preferred_element_type=F32), 0.0)
            rl.append(r)
            u = (r * r).astype(BF16)
            u_ref[:, blk] = u
            d = d + jnp.dot(u, wdn_ref[blk, :], preferred_element_type=F32)
        dn, r3 = _rms_fwd(d)
        diff = x1 + dn * gpost - t_ref[...]
        loss_ref[...] += jnp.sum(diff * diff, keepdims=True) * (0.5 / D_MODEL)
        dy = diff * (1.0 / D_MODEL)
        dgpost_ref[...] += jnp.sum(dy * dn, axis=0, keepdims=True)
        dd = _rms_bwd(dy, dn, r3, gpost).astype(BF16)
        dd_ref[...] = dd
        dh2 = jnp.zeros((TOK_TILE, D_MODEL), F32)
        for j in range(N_CHIP):
            blk = slice(j * 1024, (j + 1) * 1024)
            dpre = (_nt(dd, wdn_ref[blk, :]) * (2.0 * rl[j])).astype(BF16)
            dpre_ref[:, blk] = dpre
            dh2 = dh2 + _nt(dpre, wup_ref[j])
        dgpre_ref[...] += jnp.sum(dh2 * xn2, axis=0, keepdims=True)
        dx1_ref[...] = dy + _rms_bwd(dh2, xn2, r2, gpre)

    acc = pl.BlockSpec((1, D_MODEL), lambda i: (0, 0))
    return pl.pallas_call(
        body, grid=(N_TOK_TILE,),
        in_specs=[_tok(D_MODEL), _tok(D_MODEL), _once((1, D_MODEL)), _once((1, D_MODEL)),
                  _once((N_CHIP, D_MODEL, 1024)), _once((D_FF, D_MODEL))],
        out_specs=[_tok(D_MODEL), _tok(D_FF), _tok(D_MODEL), _tok(D_MODEL), _tok(D_FF), acc, acc,
                   pl.BlockSpec((1, 128), lambda i: (0, 0))],
        out_shape=[jax.ShapeDtypeStruct((SEQ, D_MODEL), F32), jax.ShapeDtypeStruct((SEQ, D_FF), BF16),
                   jax.ShapeDtypeStruct((SEQ, D_MODEL), BF16), jax.ShapeDtypeStruct((SEQ, D_MODEL), BF16),
                   jax.ShapeDtypeStruct((SEQ, D_FF), BF16), jax.ShapeDtypeStruct((1, D_MODEL), F32),
                   jax.ShapeDtypeStruct((1, D_MODEL), F32), jax.ShapeDtypeStruct((1, 128), F32)],
        compiler_params=_params(("arbitrary",)), name="mlp_fwd_bwd",
    )(x1, target, g_pre, g_post, w_up4, w_down)


def _merge_bwd(dx1, mix, ab, gab, w_pa4, w_pb4, w_o, g_post):
    def body(dx1_ref, mix_ref, ab_ref, gab_ref, wpa_ref, wpb_ref, wo_ref, g_ref,
             dmix_ref, dya_ref, dyb_ref, dgab_ref, dab_ref, dg_ref):
        @pl.when(pl.program_id(0) == 0)
        def _():
            dg_ref[...] = jnp.zeros_like(dg_ref)

        dx1 = dx1_ref[...]
        mn, r = _rms_fwd(mix_ref[...])
        dg_ref[...] += jnp.sum(dx1 * mn, axis=0, keepdims=True)
        dmix = _rms_bwd(dx1, mn, r, g_ref[...]).astype(BF16)
        dmix_ref[...] = dmix
        dmerged = _nt(dmix, wo_ref[...])
        a = ab_ref[:, 0:512]
        b = ab_ref[:, 512:1024]
        da = jnp.zeros((TOK_TILE, 512), F32)
        db = jnp.zeros((TOK_TILE, 512), F32)
        for j in range(N_CHIP):
            blk = slice(j * 256, (j + 1) * 256)
            blk_b = slice(1024 + j * 256, 1024 + (j + 1) * 256)
            dm = dmerged[:, blk]
            ya = jnp.dot(a, wpa_ref[j], preferred_element_type=F32)
            yb = jnp.dot(b, wpb_ref[j], preferred_element_type=F32)
            sa = _sigmoid(gab_ref[:, blk])
            sb = _sigmoid(gab_ref[:, blk_b])
            dya = (dm * sa).astype(BF16)
            dyb = (dm * sb).astype(BF16)
            dya_ref[:, blk] = dya
            dyb_ref[:, blk] = dyb
            dgab_ref[:, blk] = (dm * ya * sa * (1.0 - sa)).astype(BF16)
            dgab_ref[:, blk_b] = (dm * yb * sb * (1.0 - sb)).astype(BF16)
            da = da + _nt(dya, wpa_ref[j])
            db = db + _nt(dyb, wpb_ref[j])
        dab_ref[:, 0:512] = da
        dab_ref[:, 512:1024] = db

    return pl.pallas_call(
        body, grid=(N_TOK_TILE,),
        in_specs=[_tok(D_MODEL), _tok(D_MODEL), _tok(1024), _tok(GAB_W), _once((N_CHIP, 512, 256)),
                  _once((N_CHIP, 512, 256)), _once((D_MODEL, D_MODEL)), _once((1, D_MODEL))],
        out_specs=[_tok(D_MODEL), _tok(D_MODEL), _tok(D_MODEL), _tok(GAB_W), _tok(1024),
                   pl.BlockSpec((1, D_MODEL), lambda i: (0, 0))],
        out_shape=[jax.ShapeDtypeStruct((SEQ, D_MODEL), BF16), jax.ShapeDtypeStruct((SEQ, D_MODEL), BF16),
                   jax.ShapeDtypeStruct((SEQ, D_MODEL), BF16), jax.ShapeDtypeStruct((SEQ, GAB_W), BF16),
                   jax.ShapeDtypeStruct((SEQ, 1024), F32), jax.ShapeDtypeStruct((1, D_MODEL), F32)],
        compiler_params=_params(("arbitrary",)), name="merge_bwd",
    )(dx1, mix, ab, gab, w_pa4, w_pb4, w_o, g_post)


def _in_proj_bwd(dpm, dgab, x, dx1, g_pre, wt_in):
    def body(dpm_ref, dgab_ref, x_ref, dx1_ref, g_ref, wt_ref, dx_ref, dg_ref):
        @pl.when(pl.program_id(0) == 0)
        def _():
            dg_ref[...] = jnp.zeros_like(dg_ref)

        dh = jnp.dot(dpm_ref[:, 0:PM_XM], wt_ref[0:IN_ALOW, :], preferred_element_type=F32)
        dh = dh + jnp.dot(dpm_ref[:, PM_XM:PM_AL], wt_ref[IN_XM:IN_GATES, :], preferred_element_type=F32)
        dh = dh + jnp.dot(dpm_ref[:, PM_AL:PM_W], wt_ref[IN_ALOW:IN_ALOW + 128, :], preferred_element_type=F32)
        dh = dh + jnp.dot(dgab_ref[...], wt_ref[IN_GATES:D_IN, :], preferred_element_type=F32)
        xn, r = _rms_fwd(x_ref[...])
        dg_ref[...] += jnp.sum(dh * xn, axis=0, keepdims=True)
        dx_ref[...] = dx1_ref[...] + _rms_bwd(dh, xn, r, g_ref[...])

    return pl.pallas_call(
        body, grid=(N_TOK_TILE,),
        in_specs=[_tok(PM_W), _tok(GAB_W), _tok(D_MODEL), _tok(D_MODEL), _once((1, D_MODEL)), _once((D_IN, D_MODEL))],
        out_specs=[_tok(D_MODEL), pl.BlockSpec((1, D_MODEL), lambda i: (0, 0))],
        out_shape=[jax.ShapeDtypeStruct((SEQ, D_MODEL), F32), jax.ShapeDtypeStruct((1, D_MODEL), F32)],
        compiler_params=_params(("arbitrary",)), name="in_proj_bwd",
    )(dpm, dgab, x, dx1, g_pre, wt_in)


def _dw_in(dpm, dgab, h):
    n_pm = PM_AL // 512
    n_blk = n_pm + GAB_W // 512

    def body(dpm_ref, dgab_ref, dal_ref, h_ref, o_ref):
        i = pl.program_id(0)
        off = pl.multiple_of(i * 512 + 16 * (i >= 3).astype(jnp.int32), 16)

        @pl.when(i < n_pm)
        def _():
            o_ref[pl.ds(off, 512), :] = _tn(dpm_ref[...], h_ref[...]).astype(BF16)

        @pl.when(i >= n_pm)
        def _():
            o_ref[pl.ds(off, 512), :] = _tn(dgab_ref[...], h_ref[...]).astype(BF16)

        @pl.when(i == 0)
        def _():
            o_ref[IN_ALOW:IN_XM, :] = _tn(dal_ref[...], h_ref[...])[0:IN_XM - IN_ALOW].astype(BF16)

    return pl.pallas_call(
        body, grid=(n_blk,),
        in_specs=[pl.BlockSpec((SEQ, 512), lambda i: (0, jnp.minimum(i, n_pm - 1))),
                  pl.BlockSpec((SEQ, 512), lambda i: (0, jnp.maximum(i - n_pm, 0))),
                  pl.BlockSpec((SEQ, 128), lambda i: (0, PM_AL // 128)),
                  _once((SEQ, D_MODEL))],
        out_specs=pl.BlockSpec((D_IN, D_MODEL), lambda i: (0, 0)),
        out_shape=jax.ShapeDtypeStruct((D_IN, D_MODEL), BF16),
        compiler_params=_params(("arbitrary",)), name="dw_in",
    )(dpm, dgab, dpm, h)


def _tn_matmul(a, b, name, shards=1, tm=512):
    m, n = a.shape[1], b.shape[1]
    tm = min(tm, m)
    tn = n // shards if shards > 1 else min(n, 1024)

    def body(a_ref, b_ref, o_ref):
        o_ref[...] = _tn(a_ref[...], b_ref[...]).astype(BF16)

    if shards > 1:
        out_spec = pl.BlockSpec((None, tm, tn), lambda i, j: (j, i, 0))
        out_shape = jax.ShapeDtypeStruct((shards, m, tn), BF16)
    else:
        out_spec = pl.BlockSpec((tm, tn), lambda i, j: (i, j))
        out_shape = jax.ShapeDtypeStruct((m, n), BF16)
    return pl.pallas_call(
        body, grid=(m // tm, n // tn),
        in_specs=[pl.BlockSpec((SEQ, tm), lambda i, j: (0, i)), pl.BlockSpec((SEQ, tn), lambda i, j: (0, j))],
        out_specs=out_spec, out_shape=out_shape,
        compiler_params=_params(("arbitrary", "arbitrary")), name=name,
    )(a, b)


MESH = pl.DeviceIdType.MESH
ANY = pl.BlockSpec(memory_space=pl.ANY)
VMEM_WHOLE = pl.BlockSpec(memory_space=pltpu.VMEM)

_BIG = ("w_in", "w_pa", "w_pb", "w_o", "w_up", "w_down")
_BIG_SHARD = {"w_in": (IN_SHARD, D_MODEL), "w_pa": (512, 256), "w_pb": (512, 256), "w_o": (256, D_MODEL),
              "w_up": (D_MODEL, 1024), "w_down": (1024, D_MODEL)}
_BIG_SPLIT = {"w_in": 1, "w_pa": 0, "w_pb": 0, "w_o": 0, "w_up": 0, "w_down": 0}


def _half(ref, e, name, lead=0):
    axis = _BIG_SPLIT[name]
    size = _BIG_SHARD[name][axis] // 2
    start = pl.multiple_of(e * size, 128 if axis == 1 else 16)
    idx = [pl.ds(0, ref.shape[a]) for a in range(lead)]
    idx += [pl.ds(start, size), pl.ds(0, _BIG_SHARD[name][1])] if axis == 0 else [pl.ds(0, _BIG_SHARD[name][0]), pl.ds(start, size)]
    return ref.at[tuple(idx)]


def _half_shape(name):
    r, c = _BIG_SHARD[name]
    return (r // 2, c) if _BIG_SPLIT[name] == 0 else (r, c // 2)


def _remote(src, dst, send_sems, recv_sems, k, to):
    return pltpu.make_async_remote_copy(src_ref=src, dst_ref=dst, send_sem=send_sems.at[k], recv_sem=recv_sems.at[k],
                                        device_id=to, device_id_type=MESH)


def _mesh_place():
    x, y, c = lax.axis_index("x"), lax.axis_index("y"), lax.axis_index("c")
    return x, y, c, [(1 - x, y), (x, 1 - y), (1 - x, 1 - y)]


class _Gather:
    def __init__(self, names, small=()):
        self.names = tuple(names)
        self.nb = len(self.names)
        self.n = self.nb + len(small)
        self.n_sems = 6 * self.n
        self.out_shape = [jax.ShapeDtypeStruct((N_CHIP,) + _BIG_SHARD[nm], BF16) for nm in self.names]
        self.out_shape += [jax.ShapeDtypeStruct((N_CHIP,) + s.shape, s.dtype) for s in small]

    def _ici(self, ins, outs, ss, rs, k, j, peer, slot, c):
        if k < self.nb:
            return _remote(_half(ins[k], c, self.names[k]), _half(outs[k].at[slot], c, self.names[k]), ss, rs, 6 * k + j,
                           (*peer, c))
        return _remote(ins[k], outs[k].at[slot], ss, rs, 6 * k + j, (*peer, c))

    def _passed(self, outs, ss, rs, k, j, slot, e, sibling):
        part = _half(outs[k].at[slot], e, self.names[k])
        return _remote(part, part, ss, rs, 6 * k + 3 + j, sibling)

    def first(self, ins, outs, ss, rs):
        x, y, c, peers = _mesh_place()
        me = 2 * x + y
        for k in range(self.n):
            for j, peer in enumerate(peers):
                self._ici(ins, outs, ss, rs, k, j, peer, me, c).start()
        for k in range(self.n):
            outs[k][me] = ins[k][...]

    def middle(self, ins, outs, ss, rs):
        x, y, c, peers = _mesh_place()
        for j, (px, py) in enumerate(peers):
            for k in range(self.nb):
                self._ici(ins, outs, ss, rs, k, j, (px, py), 2 * px + py, c).wait_recv()
                self._passed(outs, ss, rs, k, j, 2 * px + py, c, (x, y, 1 - c)).start()

    def last(self, ins, outs, ss, rs):
        x, y, c, peers = _mesh_place()
        for j, (px, py) in enumerate(peers):
            for k in range(self.n):
                if k < self.nb:
                    self._passed(outs, ss, rs, k, j, 2 * px + py, 1 - c, (x, y, 1 - c)).wait_recv()
                    self._passed(outs, ss, rs, k, j, 2 * px + py, c, (x, y, 1 - c)).wait_send()
                else:
                    self._ici(ins, outs, ss, rs, k, j, (px, py), 2 * px + py, c).wait_recv()
                self._ici(ins, outs, ss, rs, k, j, (px, py), 2 * x + y, c).wait_send()


def _run_alone(rider, ins, name):
    def body(*refs):
        r_in, r_out, sems = _split(refs, len(ins), len(rider.out_shape), 2)
        rider.first(r_in, r_out, *sems)
        rider.middle(r_in, r_out, *sems)
        rider.last(r_in, r_out, *sems)

    return pl.pallas_call(
        body, in_specs=[VMEM_WHOLE] * len(ins), out_specs=[VMEM_WHOLE] * len(rider.out_shape), out_shape=rider.out_shape,
        scratch_shapes=[pltpu.SemaphoreType.DMA((rider.n_sems,)), pltpu.SemaphoreType.DMA((rider.n_sems,))],
        compiler_params=_params(), name=name,
    )(*ins)


def _presum(names, grads, name):
    n = len(grads)

    def body(*refs):
        g_refs, got_refs, stage_refs, (send_sems, recv_sems, local_sems) = _split(refs, n, n, n, 3)
        x, y, c = lax.axis_index("x"), lax.axis_index("y"), lax.axis_index("c")

        def stage(e):
            cps = [pltpu.make_async_copy(_half(g_refs[k], e, names[k], lead=1), stage_refs[k], local_sems.at[k])
                   for k in range(n)]
            for cp in cps:
                cp.start()
            return cps

        staged = stage(1 - c)
        sends = []
        for k in range(n):
            staged[k].wait()
            cp = _remote(stage_refs[k], got_refs[k], send_sems, recv_sems, k, (x, y, 1 - c))
            cp.start()
            sends.append(cp)
        for cp in sends:
            cp.wait_send()
        staged = stage(c)
        for k in range(n):
            sends[k].wait_recv()
            staged[k].wait()

            @pl.loop(0, N_CHIP)
            def _(j):
                got_refs[k][j] = (got_refs[k][j].astype(F32) + stage_refs[k][j].astype(F32)).astype(BF16)

    half = [jax.ShapeDtypeStruct((N_CHIP,) + _half_shape(nm), BF16) for nm in names]
    return pl.pallas_call(
        body, in_specs=[ANY] * n, out_specs=[VMEM_WHOLE] * n, out_shape=half,
        scratch_shapes=[pltpu.VMEM(h.shape, h.dtype) for h in half]
        + [pltpu.SemaphoreType.DMA((n,)), pltpu.SemaphoreType.DMA((n,)), pltpu.SemaphoreType.DMA((n,))],
        compiler_params=_params(), name=name,
    )(*grads)


class _SendPartials:
    def __init__(self, names, small_shape=None):
        self.n = len(names)
        self.small = small_shape is not None
        self.n_sems = 3 * self.n + 7
        self.out_shape = [jax.ShapeDtypeStruct((N_CHIP,) + _half_shape(nm), BF16) for nm in names]
        if self.small:
            self.out_shape.append(jax.ShapeDtypeStruct((N_DEV,) + small_shape, F32))

    def _piece(self, ins, outs, ss, rs, k, j, peer, src_slot, dst_slot, c):
        return _remote(ins[k].at[src_slot], outs[k].at[dst_slot], ss, rs, 3 * k + j, (*peer, c))

    def _small(self, ins, outs, ss, rs, r, other, slot):
        return _remote(ins[self.n], outs[self.n].at[slot], ss, rs, 3 * self.n + r, other)

    @staticmethod
    def _others(x, y, c):
        return [(x, y, 1 - c), (1 - x, y, c), (1 - x, y, 1 - c), (x, 1 - y, c), (x, 1 - y, 1 - c),
                (1 - x, 1 - y, c), (1 - x, 1 - y, 1 - c)]

    def first(self, ins, outs, ss, rs):
        x, y, c, peers = _mesh_place()
        me = 2 * x + y
        for k in range(self.n):
            for j, (px, py) in enumerate(peers):
                self._piece(ins, outs, ss, rs, k, j, (px, py), 2 * px + py, me, c).start()
        if self.small:
            for r, other in enumerate(self._others(x, y, c)):
                self._small(ins, outs, ss, rs, r, other, 4 * x + 2 * y + c).start()
            outs[self.n][4 * x + 2 * y + c] = ins[self.n][...]
        for k in range(self.n):
            outs[k][me] = ins[k][me]

    def middle(self, ins, outs, ss, rs):
        pass

    def last(self, ins, outs, ss, rs):
        x, y, c, peers = _mesh_place()
        me = 2 * x + y
        for k in range(self.n):
            for j, (px, py) in enumerate(peers):
                self._piece(ins, outs, ss, rs, k, j, (px, py), me, 2 * px + py, c).wait_recv()
                self._piece(ins, outs, ss, rs, k, j, (px, py), 2 * px + py, me, c).wait_send()
        if self.small:
            for r, (px, py, pc) in enumerate(self._others(x, y, c)):
                self._small(ins, outs, ss, rs, r, (px, py, pc), 4 * px + 2 * py + pc).wait_recv()
                self._small(ins, outs, ss, rs, r, (px, py, pc), 4 * x + 2 * y + c).wait_send()


def _sum_swap(names, parts):
    n = len(parts)

    def body(*refs):
        p_refs, o_refs, (send_sems, recv_sems) = _split(refs, n, n, 2)
        x, y, c = lax.axis_index("x"), lax.axis_index("y"), lax.axis_index("c")
        for e in range(2):
            @pl.when(c == e)
            def _():
                for k in range(n):
                    g = p_refs[k][0].astype(F32)
                    for s in range(1, N_CHIP):
                        g = g + p_refs[k][s].astype(F32)
                    r, cols = _half_shape(names[k])
                    if _BIG_SPLIT[names[k]] == 0:
                        o_refs[k][e * r:(e + 1) * r, :] = g
                    else:
                        o_refs[k][:, e * cols:(e + 1) * cols] = g
        sends = []
        for k in range(n):
            mine = _half(o_refs[k], c, names[k])
            cp = _remote(mine, mine, send_sems, recv_sems, k, (x, y, 1 - c))
            cp.start()
            sends.append(cp)
        for k in range(n):
            theirs = _half(o_refs[k], 1 - c, names[k])
            _remote(theirs, theirs, send_sems, recv_sems, k, (x, y, 1 - c)).wait_recv()
        for cp in sends:
            cp.wait_send()

    return pl.pallas_call(
        body, in_specs=[VMEM_WHOLE] * n, out_specs=[VMEM_WHOLE] * n,
        out_shape=[jax.ShapeDtypeStruct(_BIG_SHARD[nm], F32) for nm in names],
        scratch_shapes=[pltpu.SemaphoreType.DMA((n,)), pltpu.SemaphoreType.DMA((n,))],
        compiler_params=_params(), name="sum_swap",
    )(*parts)


def _tile(rows, cols, itemsize, budget):
    t = cols if rows % 16 else rows
    other = rows if rows % 16 else cols
    step = 256 if rows % 16 else 32
    while t % step == 0 and t * other * itemsize > budget:
        t //= 2
    return (rows, t) if rows % 16 else (t, cols)


def _adamw_math(w, g, m, v):
    m = ADAM_B1 * m + (1.0 - ADAM_B1) * g
    v = ADAM_B2 * v + (1.0 - ADAM_B2) * (g * g)
    m_hat = m / (1.0 - ADAM_B1 ** ADAM_STEP)
    v_hat = v / (1.0 - ADAM_B2 ** ADAM_STEP)
    delta = -ADAM_LR * (m_hat / (jnp.sqrt(v_hat) + ADAM_EPS) + ADAM_WD * w)
    return delta, m, v


def _adamw_big(g, w, m, v, name):
    r, c = w.shape
    tr, tc = _tile(r, c, 4, 1024 * 1024)

    def body(g_ref, w_ref, m_ref, v_ref, d_ref, nm_ref, nv_ref):
        d_ref[...], nm_ref[...], nv_ref[...] = _adamw_math(w_ref[...], g_ref[...], m_ref[...], v_ref[...])

    blk = pl.BlockSpec((tr, tc), lambda i, l: (i, l))
    return pl.pallas_call(
        body, grid=(r // tr, c // tc), in_specs=[blk, blk, blk, blk],
        out_specs=[blk, blk, blk], out_shape=[jax.ShapeDtypeStruct((r, c), F32)] * 3,
        compiler_params=_params(("arbitrary", "arbitrary")), name=name,
    )(g, w, m, v)


def _sum_small(parts):
    def body(p_ref, o_ref):
        g = p_ref[0]
        for d in range(1, N_DEV):
            g = g + p_ref[d]
        o_ref[...] = g

    return pl.pallas_call(body, out_shape=jax.ShapeDtypeStruct(parts.shape[1:], F32), name="sum_small")(parts)


def _adamw_small(w, g, m, v):
    def body(w_ref, g_ref, m_ref, v_ref, d_ref, nm_ref, nv_ref):
        d_ref[...], nm_ref[...], nv_ref[...] = _adamw_math(w_ref[...], g_ref[...], m_ref[...], v_ref[...])

    return pl.pallas_call(body, out_shape=[jax.ShapeDtypeStruct(w.shape, F32)] * 3, name="adamw_small")(w, g, m, v)


def _pack(arrs):
    flat = jnp.concatenate([a.reshape(-1) for a in arrs])
    rows = -(-flat.shape[0] // 1024) * 8
    return jnp.pad(flat, (0, rows * 128 - flat.shape[0])).reshape(rows, 128)


def _unpack(buf, shapes):
    flat = buf.reshape(-1)
    out, off = [], 0
    for s in shapes:
        size = 1
        for d in s:
            size *= d
        out.append(flat[off:off + size].reshape(s))
        off += size
    return out


def _block_rows(w):
    return jnp.pad(w.reshape(512, 4), ((0, 0), (0, 124)))


def _cols(a4):
    return jnp.transpose(a4, (1, 0, 2)).reshape(a4.shape[1], -1)


_LATE = ("w_pa", "w_pb", "w_o", "w_up", "w_down")


def _full_weights(gathered):
    joined = {"w_in": (D_IN, D_MODEL), "w_o": (D_MODEL, D_MODEL), "w_down": (D_FF, D_MODEL)}
    return {n: (a.reshape(joined[n]) if n in joined else a) for n, a in gathered.items()}


def _local_step(x, target, w, sp, late_shards=None):
    sp = {n: (a.reshape(1, -1) if a.ndim == 1 else a) for n, a in sp.items()}
    wau = jnp.zeros((128, 256), F32).at[0:16].set(sp["w_a_up"])
    wif = jnp.zeros((1536, 128), F32).at[:, 0:8].set(sp["w_if"])
    bif = jnp.zeros((1, 128), F32).at[:, 0:8].set(sp["b_if"])
    p = {"wau": wau, "bau": sp["b_a_up"], "ggla": sp["g_gla_norm"], "cw": sp["conv_w"], "cb": sp["conv_b"],
         "wq": _block_rows(sp["w_q_ml"]), "wk": _block_rows(sp["w_k_ml"]), "wv": _block_rows(sp["w_v_ml"]),
         "wif": wif, "bif": bif, "skip": sp["ml_skip"], "gml": sp["g_ml_norm"]}

    pm, gab, h = _in_proj(x, sp["g_pre_mix"], w["w_in"])
    if late_shards is None:
        ab, *states = _mixer_fwd(pm, p)
    else:
        ab, *rest = _mixer_fwd(pm, p, _Gather(_LATE), late_shards)
        states = rest[:4]
        w = dict(w, **_full_weights(dict(zip(_LATE, rest[4:]))))
    x1, mix, merged = _merge_fwd(ab, gab, x, w["w_pa"], w["w_pb"], w["w_o"], sp["g_post_mix"])
    dx1, u, dd, h2, dpre, dg_post_mlp, dg_pre_mlp, loss = _mlp(x1, target, sp["g_pre_mlp"], sp["g_post_mlp"],
                                                                w["w_up"], w["w_down"])
    dmix, dya, dyb, dgab, dab, dg_post_mix = _merge_bwd(dx1, mix, ab, gab, w["w_pa"], w["w_pb"], w["w_o"], sp["g_post_mix"])
    big = {
        "w_pa": _tn_matmul(ab[:, 0:512], dya, "dw_pa", shards=N_CHIP),
        "w_pb": _tn_matmul(ab[:, 512:1024], dyb, "dw_pb", shards=N_CHIP),
        "w_o": _tn_matmul(merged, dmix, "dw_o"),
        "w_up": _tn_matmul(h2, dpre, "dw_up", shards=N_CHIP),
        "w_down": _tn_matmul(u, dd, "dw_down"),
    }
    if late_shards is None:
        dpm, dp, _ = _mixer_bwd(pm, dab, states, p)
    else:
        partial = _presum(_LATE, [big[n].reshape((N_CHIP,) + _BIG_SHARD[n]) for n in _LATE], "presum_late")
        dpm, dp, parts = _mixer_bwd(pm, dab, states, p, _SendPartials(_LATE), partial)
        big = dict(zip(_LATE, parts))
    dx, dg_pre_mix = _in_proj_bwd(dpm, dgab, x, dx1, sp["g_pre_mix"], w["w_in"])
    big["w_in"] = _dw_in(dpm, dgab, h)
    small = {
        "g_pre_mix": dg_pre_mix, "b_a_up": dp["bau"], "g_gla_norm": dp["ggla"], "conv_b": dp["cb"],
        "w_q_ml": dp["wq"][:, 0:4].reshape(128, 4, 4), "w_k_ml": dp["wk"][:, 0:4].reshape(128, 4, 4),
        "w_v_ml": dp["wv"][:, 0:4].reshape(128, 4, 4),
        "b_if": dp["bif"][:, 0:8], "ml_skip": dp["skip"], "g_ml_norm": dp["gml"], "g_post_mix": dg_post_mix,
        "g_pre_mlp": dg_pre_mlp, "g_post_mlp": dg_post_mlp, "w_a_up": dp["wau"][0:16], "conv_w": dp["cw"],
        "w_if": dp["wif"][:, 0:8], "loss": loss[:, 0:1],
    }
    return dx, big, small


_SMALL_REPL = ("g_pre_mix", "b_a_up", "g_gla_norm", "conv_b", "w_q_ml", "w_k_ml", "w_v_ml", "b_if", "ml_skip",
               "g_ml_norm", "g_post_mix", "g_pre_mlp", "g_post_mlp")
_SMALL_SHARDED = ("w_a_up", "conv_w", "w_if")
_SMALL_ORDER = _SMALL_REPL + _SMALL_SHARDED + ("loss",)
_WEIGHTS = ("g_pre_mix", "w_in", "w_a_up", "b_a_up", "g_gla_norm", "conv_w", "conv_b", "w_q_ml", "w_k_ml", "w_v_ml",
            "w_if", "b_if", "ml_skip", "g_ml_norm", "w_pa", "w_pb", "w_o", "g_post_mix", "g_pre_mlp", "w_up", "w_down",
            "g_post_mlp")


def _as_shard(name, a):
    return a[0].T if name == "w_in" else a[0]


def _from_shard(name, a):
    return (a.T if name == "w_in" else a)[None]


def kernel(x, g_pre_mix, w_in, w_a_up, b_a_up, g_gla_norm, conv_w, conv_b, w_q_ml, w_k_ml, w_v_ml, w_if, b_if, ml_skip, g_ml_norm, w_pa, w_pb, w_o, g_post_mix, g_pre_mlp, w_up, w_down, g_post_mlp, loss_target, m_g_pre_mix, m_w_in, m_w_a_up, m_b_a_up, m_g_gla_norm, m_conv_w, m_conv_b, m_w_q_ml, m_w_k_ml, m_w_v_ml, m_w_if, m_b_if, m_ml_skip, m_g_ml_norm, m_w_pa, m_w_pb, m_w_o, m_g_post_mix, m_g_pre_mlp, m_w_up, m_w_down, m_g_post_mlp, v_g_pre_mix, v_w_in, v_w_a_up, v_b_a_up, v_g_gla_norm, v_conv_w, v_conv_b, v_w_q_ml, v_w_k_ml, v_w_v_ml, v_w_if, v_b_if, v_ml_skip, v_g_ml_norm, v_w_pa, v_w_pb, v_w_o, v_g_post_mix, v_g_pre_mlp, v_w_up, v_w_down, v_g_post_mlp):
    args = dict(locals())
    wts = {n: _as_shard(n, args[n]) for n in _WEIGHTS}
    mom = {n: _as_shard(n, args["m_" + n]) for n in _WEIGHTS}
    var = {n: _as_shard(n, args["v_" + n]) for n in _WEIGHTS}
    chip = 2 * lax.axis_index("x") + lax.axis_index("y")

    first = ("w_in",) + _SMALL_SHARDED
    gathered = dict(zip(first, _run_alone(_Gather(("w_in",), [wts[n] for n in _SMALL_SHARDED]),
                                          [wts[n].astype(BF16) if n == "w_in" else wts[n] for n in first],
                                          "gather_first")))
    sp = {n: wts[n] for n in _SMALL_REPL}
    sp["w_a_up"] = _cols(gathered["w_a_up"])
    sp["conv_w"] = _cols(gathered["conv_w"])
    sp["w_if"] = gathered["w_if"].reshape(1536, 8)

    dx, big, small = _local_step(x[0], loss_target[0], _full_weights({"w_in": gathered["w_in"]}), sp,
                                 late_shards=[wts[n].astype(BF16) for n in _LATE])

    small_shapes = [small[n].shape for n in _SMALL_ORDER]
    packed = _pack([small[n] for n in _SMALL_ORDER])
    partial = _presum(("w_in",), [big["w_in"].reshape((N_CHIP,) + _BIG_SHARD["w_in"])], "presum_w_in")
    parts_in, small_parts = _run_alone(_SendPartials(("w_in",), packed.shape), [*partial, packed], "send_partials")
    big["w_in"] = parts_in
    sums = _sum_swap(_BIG, [big[n] for n in _BIG])

    grads, delta, new_m, new_v = {}, {}, {}, {}
    for n, g in zip(_BIG, sums):
        grads[n] = g
        delta[n], new_m[n], new_v[n] = _adamw_big(g, wts[n], mom[n], var[n], "adamw_" + n)
    summed = dict(zip(_SMALL_ORDER, _unpack(_sum_small(small_parts), small_shapes)))
    loss = summed["loss"].reshape(())
    for n in _SMALL_REPL:
        grads[n] = summed[n].reshape(wts[n].shape)
    grads["w_a_up"] = lax.dynamic_slice_in_dim(summed["w_a_up"], chip * 64, 64, axis=1)
    grads["conv_w"] = lax.dynamic_slice_in_dim(summed["conv_w"], chip * 128, 128, axis=1)
    grads["w_if"] = lax.dynamic_slice_in_dim(summed["w_if"], chip * 384, 384, axis=0)
    small_names = _SMALL_REPL + _SMALL_SHARDED
    shard_shapes = [wts[n].shape for n in small_names]
    upd = _adamw_small(_pack([wts[n] for n in small_names]), _pack([grads[n] for n in small_names]),
                       _pack([mom[n] for n in small_names]), _pack([var[n] for n in small_names]))
    for dst, buf in zip((delta, new_m, new_v), upd):
        for n, a in zip(small_names, _unpack(buf, shard_shapes)):
            dst[n] = a

    outs = [loss, dx[None]]
    for group in (grads, delta, new_m, new_v):
        outs += [_from_shard(n, group[n]) for n in _WEIGHTS]
    return tuple(outs)
```

```python
import functools

import jax
import jax.numpy as jnp
from jax import lax
from jax.experimental import pallas as pl
from jax.experimental.pallas import tpu as pltpu

F32 = jnp.float32
BF16 = jnp.bfloat16

SEQ = 2048
D_MODEL = 1024
CHUNK = 64
N_CHUNK = SEQ // CHUNK
HEADS = 4
GLA_DK = 64
GLA_DV = 128
ML_DH = 128
D_FF = 4096
EPS = 1e-6
N_CHIP = 4
N_DEV = 8
TOK_TILE = 256
N_TOK_TILE = SEQ // TOK_TILE
SWEEP = 2
N_SWEEP = N_CHUNK // SWEEP

PM_W = 2688
PM_XM = 1536
PM_OP = 2048
PM_AL = 2560
GAB_W = 2048
D_IN = 4624
IN_SHARD = D_IN // N_CHIP
IN_ALOW = 1536
IN_XM = 1552
IN_GATES = 2576

ADAM_LR = 0.001
ADAM_B1 = 0.9
ADAM_B2 = 0.999
ADAM_EPS = 1e-08
ADAM_WD = 0.01
ADAM_STEP = 10

VMEM_LIMIT = 56 * 1024 * 1024


def _params(sem=None):
    return pltpu.CompilerParams(dimension_semantics=sem, vmem_limit_bytes=VMEM_LIMIT)


def _dot(a, b, ca, cb):
    return lax.dot_general(a.astype(BF16), b.astype(BF16), (((ca,), (cb,)), ((), ())), preferred_element_type=F32)


def _pmm_nn(a, b):
    return _dot(a, b, 1, 0)


def _pmm_nt(a, b):
    return _dot(a, b, 1, 1)


def _pmm_tn(a, b):
    return _dot(a, b, 0, 0)


def _pcmm(c, x):
    return lax.dot_general(c, x, (((1,), (0,)), ((), ())), precision=lax.Precision.HIGHEST, preferred_element_type=F32)


@jax.custom_vjp
def _mm_nn(a, b):
    return _dot(a, b, 1, 0)


@jax.custom_vjp
def _mm_nt(a, b):
    return _dot(a, b, 1, 1)


@jax.custom_vjp
def _mm_tn(a, b):
    return _dot(a, b, 0, 0)


_mm_nn.defvjp(lambda a, b: (_dot(a, b, 1, 0), (a, b)), lambda r, g: (_mm_nt(g, r[1]), _mm_tn(r[0], g)))
_mm_nt.defvjp(lambda a, b: (_dot(a, b, 1, 1), (a, b)), lambda r, g: (_mm_nn(g, r[1]), _mm_tn(g, r[0])))
_mm_tn.defvjp(lambda a, b: (_dot(a, b, 0, 0), (a, b)), lambda r, g: (_mm_nt(r[1], g), _mm_nn(r[0], g)))


@jax.custom_vjp
def _cmm(c, x):
    return _pcmm(c, x)


_cmm.defvjp(
    lambda c, x: (_pcmm(c, x), c),
    lambda c, g: (jnp.zeros_like(c), lax.dot_general(c, g, (((0,), (0,)), ((), ())), precision=lax.Precision.HIGHEST,
                                                      preferred_element_type=F32)),
)

_PLAIN_OPS = (_pmm_nn, _pmm_nt, _pmm_tn, _pcmm)
_VJP_OPS = (_mm_nn, _mm_nt, _mm_tn, _cmm)


def _sigmoid(x):
    return 0.5 * (jnp.tanh(0.5 * x) + 1.0)


def _log_sigmoid(x):
    return jnp.minimum(x, 0.0) - jnp.log(1.0 + jnp.exp(-jnp.abs(x)))


def _mean(x):
    return jnp.mean(x, axis=-1, keepdims=True)


def _nt(a, b):
    return lax.dot_general(a, b, (((1,), (1,)), ((), ())), preferred_element_type=F32)


def _tn(a, b):
    return lax.dot_general(a, b, (((0,), (0,)), ((), ())), preferred_element_type=F32)


def _mixer_chunk(ops, p, st, pm, xprev8):
    mm_nn, mm_nt, mm_tn, cmm = ops
    row = lax.broadcasted_iota(jnp.int32, (CHUNK, CHUNK), 0)
    col = lax.broadcasted_iota(jnp.int32, (CHUNK, CHUNK), 1)
    causal = row >= col
    tri = causal.astype(F32)
    q = pm[:, 0:256]
    k = pm[:, 256:512]
    v = pm[:, 512:1024]
    g = pm[:, 1024:1536]
    xm = pm[:, PM_XM:PM_XM + 512]
    opre = pm[:, PM_OP:PM_OP + 512]
    alow = pm[:, PM_AL:PM_AL + 128]

    la = _log_sigmoid(mm_nn(alow, p["wau"]) + p["bau"]) * (1.0 / 16.0)
    cum = cmm(tri, la)
    cum_last = cum[CHUNK - 1:CHUNK, :]
    e_pos = jnp.exp(cum)
    e_neg = jnp.exp(-cum)
    qs = q * (GLA_DK ** -0.5)
    qp = qs * e_pos
    qn = qs * e_neg
    kp = k * e_pos
    kn = k * e_neg
    kl = k * jnp.exp(cum_last - cum)
    dec = jnp.exp(cum_last)
    outs = []
    s_new = []
    for h in range(HEADS):
        s6 = slice(h * GLA_DK, (h + 1) * GLA_DK)
        s12 = slice(h * GLA_DV, (h + 1) * GLA_DV)
        scores = jnp.where(causal, mm_nt(qp[:, s6], kn[:, s6]), mm_nt(qn[:, s6], kp[:, s6]))
        o = mm_nn(scores, v[:, s12]) + mm_nt(qp[:, s6], st["S"][h])
        s_new.append(st["S"][h] * dec[:, s6] + mm_tn(v[:, s12], kl[:, s6]))
        o = o * lax.rsqrt(_mean(o * o) + EPS) * p["ggla"]
        gh = g[:, s12]
        outs.append(o * (gh * _sigmoid(gh)))

    xx = jnp.concatenate([xprev8, xm], axis=0)
    pre = p["cb"]
    for j in range(4):
        pre = pre + p["cw"][j:j + 1, :] * xx[5 + j:5 + j + CHUNK, :]
    xc = pre * _sigmoid(pre)
    qm, km, vm = [], [], []
    for h in range(HEADS):
        s12 = slice(h * ML_DH, (h + 1) * ML_DH)
        qm.append(mm_nn(xc[:, s12], p["wq"][h]))
        km.append(mm_nn(xc[:, s12], p["wk"][h]))
        vm.append(mm_nn(xm[:, s12], p["wv"][h]))
    qcat = jnp.concatenate(qm, axis=1)
    kcat = jnp.concatenate(km, axis=1)
    vcat = jnp.concatenate(vm, axis=1)
    gates = (mm_nn(qcat, p["wif"][0:512]) + mm_nn(kcat, p["wif"][512:1024]) + mm_nn(vcat, p["wif"][1024:1536])
             + p["bif"])
    lf = _log_sigmoid(gates)
    fc = cmm(tri, lf)
    gates_t = gates.T
    fc_t = fc.T
    c_new, n_new, m_new = [], [], []
    for h in range(HEADS):
        s12 = slice(h * ML_DH, (h + 1) * ML_DH)
        li_c = gates[:, h:h + 1]
        fc_c = fc[:, 4 + h:5 + h]
        li_r = gates_t[h:h + 1, :]
        fc_r = fc_t[4 + h:5 + h, :]
        m_prev = st["m"][h][:, 0:1]
        log_d = li_r - jnp.abs(fc_c - fc_r)
        g_int = fc_c + m_prev
        m_t = jnp.maximum(g_int, jnp.max(log_d, axis=1, keepdims=True))
        ks = km[h] * (ML_DH ** -0.5)
        s = mm_nt(qm[h], ks) * jnp.exp(log_d - m_t)
        scl = jnp.exp(g_int - m_t)
        num = mm_nn(s, vm[h]) + scl * mm_nn(qm[h], st["C"][h])
        den = jnp.sum(s, axis=1, keepdims=True) + scl * jnp.sum(qm[h] * st["n"][h], axis=1, keepdims=True)
        den = jnp.maximum(jnp.abs(den), jnp.exp(-m_t))
        hc = num / den * _sigmoid(opre[:, s12])
        d0 = hc - _mean(hc)
        y = d0 * lax.rsqrt(_mean(d0 * d0) + EPS)
        outs.append(y * p["gml"][:, s12] + p["skip"][:, s12] * xc[:, s12])
        f_last = fc[CHUNK - 1:CHUNK, 4 + h:5 + h]
        a = f_last - fc_c + li_c
        m_loc = jnp.max(a, axis=0, keepdims=True)
        kw = ks * jnp.exp(a - m_loc)
        m_nx = jnp.maximum(f_last + m_prev, m_loc)
        sp = jnp.exp(f_last + m_prev - m_nx)
        sl = jnp.exp(m_loc - m_nx)
        c_new.append(sp * st["C"][h] + sl * mm_tn(kw, vm[h]))
        n_new.append(sp * st["n"][h] + sl * jnp.sum(kw, axis=0, keepdims=True))
        m_new.append(jnp.broadcast_to(m_nx, (1, ML_DH)))
    ab = jnp.concatenate(outs, axis=1)
    new = {"S": s_new, "C": c_new, "n": n_new, "m": m_new}
    return ab, new


_P_NAMES = ("wau", "bau", "ggla", "cw", "cb", "wq", "wk", "wv", "wif", "bif", "skip", "gml")
_P_SHAPES = {
    "wau": (128, 256), "bau": (1, 256), "ggla": (1, 128), "cw": (4, 512), "cb": (1, 512),
    "wq": (512, 128), "wk": (512, 128), "wv": (512, 128),
    "wif": (1536, 128), "bif": (1, 128), "skip": (1, 512), "gml": (1, 512),
}
_P_BLOCKDIAG = ("wq", "wk", "wv")
_S_NAMES = ("S", "C", "n", "m")
_S_SHAPES = {"S": (HEADS, GLA_DV, GLA_DK), "C": (HEADS, ML_DH, ML_DH), "n": (HEADS, 1, ML_DH), "m": (HEADS, 1, ML_DH)}


def _per_head(ref):
    return [ref[h] for h in range(HEADS)]


def _block_mask():
    r = lax.broadcasted_iota(jnp.int32, (128, 128), 0)
    c = lax.broadcasted_iota(jnp.int32, (128, 128), 1)
    same_block = (r >> 2) == (c >> 2)
    spread = jnp.logical_and(r < 4, (c & 3) == r)
    return same_block.astype(F32), spread.astype(F32)


def _expand_blockdiag(w_ref, dense_ref):
    same_block, spread = _block_mask()
    for h in range(HEADS):
        tiled = _pmm_nn(w_ref[h * 128:(h + 1) * 128, :], spread)
        dense_ref[h] = tiled * same_block


def _collect_blockdiag(ddense_ref, dw_ref):
    same_block, spread = _block_mask()
    for h in range(HEADS):
        dw_ref[h * 128:(h + 1) * 128, :] = lax.dot_general(
            ddense_ref[h] * same_block, spread, (((1,), (1,)), ((), ())), precision=lax.Precision.HIGHEST,
            preferred_element_type=F32)


def _const_spec(shape):
    zeros = (0,) * len(shape)
    return pl.BlockSpec(shape, lambda i: zeros)


def _split(refs, *counts):
    out, at = [], 0
    for c in counts:
        out.append(refs[at:at + c])
        at += c
    assert at == len(refs)
    return out


def _ride(rider, phases, cond, ins, outs, sems):
    if rider is None:
        return
    lands, (send_sems, recv_sems, flush_sems) = sems[:-3], sems[-3:]

    @pl.when(cond)
    def _():
        for phase in phases:
            getattr(rider, phase)(ins, lands, send_sems, recv_sems)
        if "last" in phases:
            flush = [pltpu.make_async_copy(lands[k], outs[k], flush_sems.at[k]) for k in range(len(outs))]
            for cp in flush:
                cp.start()
            for cp in flush:
                cp.wait()


def _rider_specs(rider, rider_ins):
    if rider is None:
        return [], [], [], []
    scratch = [pltpu.VMEM(s.shape, s.dtype) for s in rider.out_shape]
    scratch += [pltpu.SemaphoreType.DMA((rider.n_sems,)), pltpu.SemaphoreType.DMA((rider.n_sems,)),
                pltpu.SemaphoreType.DMA((len(rider.out_shape),))]
    return [VMEM_WHOLE] * len(rider_ins), [ANY] * len(rider.out_shape), list(rider.out_shape), scratch


def _mixer_fwd(pm, p, rider=None, rider_ins=()):
    n_p = len(_P_NAMES)
    r_in, r_out_specs, r_out_shape, r_sems = _rider_specs(rider, rider_ins)

    def body(*refs):
        (pm_ref, xprev_ref), p_list, ride_in, (ab_ref,), so_refs, ride_out, sc_refs, dense_list, sems = _split(
            refs, 2, n_p, len(r_in), 1, 4, len(r_out_specs), 4, 3, len(r_sems))
        p_refs = dict(zip(_P_NAMES, p_list))
        dense = dict(zip(_P_BLOCKDIAG, dense_list))
        n = pl.program_id(0)
        _ride(rider, ("first",), n == 0, ride_in, ride_out, sems)

        @pl.when(n == 0)
        def _():
            for r in sc_refs:
                r[...] = jnp.zeros_like(r)
            for nm in _P_BLOCKDIAG:
                _expand_blockdiag(p_refs[nm], dense[nm])

        st = {name: _per_head(r) for name, r in zip(_S_NAMES, sc_refs)}
        pv = {nm: (_per_head(dense[nm]) if nm in _P_BLOCKDIAG else p_refs[nm][...]) for nm in _P_NAMES}
        for g in range(SWEEP):
            rows = slice(g * CHUNK, (g + 1) * CHUNK)
            for name, r in zip(_S_NAMES, so_refs):
                for h in range(HEADS):
                    r[g, h] = st[name][h]
            if g == 0:
                xprev8 = jnp.where(n > 0, xprev_ref[CHUNK - 8:CHUNK, :], 0.0)
            else:
                xprev8 = pm_ref[g * CHUNK - 8:g * CHUNK, PM_XM:PM_XM + 512]
            ab, st = _mixer_chunk(_PLAIN_OPS, pv, st, pm_ref[rows, :], xprev8)
            ab_ref[rows, :] = ab.astype(BF16)
        for name, r in zip(_S_NAMES, sc_refs):
            for h in range(HEADS):
                r[h] = st[name][h]
        _ride(rider, ("middle",), n == N_SWEEP - 2, ride_in, ride_out, sems)
        _ride(rider, ("last",), n == N_SWEEP - 1, ride_in, ride_out, sems)

    in_specs = [pl.BlockSpec((SWEEP * CHUNK, PM_W), lambda i: (i, 0)),
                pl.BlockSpec((CHUNK, 512), lambda i: (jnp.maximum(SWEEP * i - 1, 0), PM_XM // 512))]
    in_specs += [_const_spec(_P_SHAPES[nm]) for nm in _P_NAMES] + r_in
    out_specs = [pl.BlockSpec((SWEEP * CHUNK, 1024), lambda i: (i, 0))]
    out_shape = [jax.ShapeDtypeStruct((SEQ, 1024), BF16)]
    for nm in _S_NAMES:
        shp = _S_SHAPES[nm]
        out_specs.append(pl.BlockSpec((SWEEP,) + shp, lambda i: (i, 0, 0, 0)))
        out_shape.append(jax.ShapeDtypeStruct((N_CHUNK,) + shp, F32))
    return pl.pallas_call(
        body, grid=(N_SWEEP,), in_specs=in_specs, out_specs=out_specs + r_out_specs, out_shape=out_shape + r_out_shape,
        scratch_shapes=[pltpu.VMEM(_S_SHAPES[nm], F32) for nm in _S_NAMES]
        + [pltpu.VMEM((HEADS, 128, 128), F32) for _ in _P_BLOCKDIAG] + r_sems,
        compiler_params=_params(("arbitrary",)), name="mixer_fwd",
    )(pm, pm, *[p[nm] for nm in _P_NAMES], *rider_ins)


def _mixer_bwd(pm, dab, states, p, rider=None, rider_ins=()):
    n_p = len(_P_NAMES)
    r_in, r_out_specs, r_out_shape, r_sems = _rider_specs(rider, rider_ins)

    def body(*refs):
        ((pm_ref, xprev_ref, dab_ref), si_refs, p_list, ride_in, (dpm_ref,), dp_list, ride_out, ds_refs, (carry_ref,),
         dense_list, ddense_list, sems) = _split(refs, 3, 4, n_p, len(r_in), 1, n_p, len(r_out_specs), 4, 1, 3, 3, len(r_sems))
        p_refs = dict(zip(_P_NAMES, p_list))
        dp_refs = dict(zip(_P_NAMES, dp_list))
        dense = dict(zip(_P_BLOCKDIAG, dense_list))
        ddense = dict(zip(_P_BLOCKDIAG, ddense_list))
        i = pl.program_id(0)
        blk = N_SWEEP - 1 - i
        _ride(rider, ("first",), i == 0, ride_in, ride_out, sems)

        @pl.when(i == 0)
        def _():
            for r in ds_refs:
                r[...] = jnp.zeros_like(r)
            for nm in _P_NAMES:
                if nm in _P_BLOCKDIAG:
                    ddense[nm][...] = jnp.zeros_like(ddense[nm])
                    _expand_blockdiag(p_refs[nm], dense[nm])
                else:
                    dp_refs[nm][...] = jnp.zeros_like(dp_refs[nm])
            carry_ref[...] = jnp.zeros_like(carry_ref)

        pv = {nm: (_per_head(dense[nm]) if nm in _P_BLOCKDIAG else p_refs[nm][...]) for nm in _P_NAMES}
        dst = {name: _per_head(r) for name, r in zip(_S_NAMES, ds_refs)}
        carry = carry_ref[...]
        dp_sum = None
        for g in reversed(range(SWEEP)):
            rows = slice(g * CHUNK, (g + 1) * CHUNK)
            st = {name: [r[g, h] for h in range(HEADS)] for name, r in zip(_S_NAMES, si_refs)}
            if g == 0:
                xprev8 = jnp.where(blk > 0, xprev_ref[CHUNK - 8:CHUNK, :], 0.0)
            else:
                xprev8 = pm_ref[g * CHUNK - 8:g * CHUNK, PM_XM:PM_XM + 512]
            _, vjp = jax.vjp(functools.partial(_mixer_chunk, _VJP_OPS), pv, st, pm_ref[rows, :], xprev8)
            dp, dst, dpm, dxprev8 = vjp((dab_ref[rows, :], dst))
            reach = jnp.concatenate([jnp.zeros((CHUNK - 8, 512), F32), carry], axis=0)
            dpm_ref[rows, 0:PM_XM] = dpm[:, 0:PM_XM].astype(BF16)
            dpm_ref[rows, PM_XM:PM_XM + 512] = (dpm[:, PM_XM:PM_XM + 512] + reach).astype(BF16)
            dpm_ref[rows, PM_XM + 512:PM_W] = dpm[:, PM_XM + 512:PM_W].astype(BF16)
            carry = dxprev8
            dp_sum = dp if dp_sum is None else jax.tree.map(jnp.add, dp_sum, dp)
        carry_ref[...] = carry
        for name, r in zip(_S_NAMES, ds_refs):
            for h in range(HEADS):
                r[h] = dst[name][h]
        for nm in _P_NAMES:
            if nm in _P_BLOCKDIAG:
                for h in range(HEADS):
                    ddense[nm][h] += dp_sum[nm][h]
            else:
                dp_refs[nm][...] += dp_sum[nm]

        @pl.when(i == N_SWEEP - 1)
        def _():
            for nm in _P_BLOCKDIAG:
                _collect_blockdiag(ddense[nm], dp_refs[nm])

        _ride(rider, ("middle",), i == N_SWEEP - 2, ride_in, ride_out, sems)
        _ride(rider, ("last",), i == N_SWEEP - 1, ride_in, ride_out, sems)

    rev = lambda i: (N_SWEEP - 1 - i, 0)
    in_specs = [pl.BlockSpec((SWEEP * CHUNK, PM_W), rev),
                pl.BlockSpec((CHUNK, 512), lambda i: (jnp.maximum(SWEEP * (N_SWEEP - 1 - i) - 1, 0), PM_XM // 512)),
                pl.BlockSpec((SWEEP * CHUNK, 1024), rev)]
    for nm in _S_NAMES:
        in_specs.append(pl.BlockSpec((SWEEP,) + _S_SHAPES[nm], lambda i: (N_SWEEP - 1 - i, 0, 0, 0)))
    in_specs += [_const_spec(_P_SHAPES[nm]) for nm in _P_NAMES] + r_in
    out_specs = [pl.BlockSpec((SWEEP * CHUNK, PM_W), rev)] + [_const_spec(_P_SHAPES[nm]) for nm in _P_NAMES]
    out_shape = [jax.ShapeDtypeStruct((SEQ, PM_W), BF16)] + [jax.ShapeDtypeStruct(_P_SHAPES[nm], F32) for nm in _P_NAMES]
    res = pl.pallas_call(
        body, grid=(N_SWEEP,), in_specs=in_specs, out_specs=out_specs + r_out_specs, out_shape=out_shape + r_out_shape,
        scratch_shapes=[pltpu.VMEM(_S_SHAPES[nm], F32) for nm in _S_NAMES] + [pltpu.VMEM((8, 512), F32)]
        + [pltpu.VMEM((HEADS, 128, 128), F32) for _ in range(2 * len(_P_BLOCKDIAG))] + r_sems,
        compiler_params=_params(("arbitrary",)), name="mixer_bwd",
    )(pm, pm, dab, *states, *[p[nm] for nm in _P_NAMES], *rider_ins)
    return res[0], dict(zip(_P_NAMES, res[1:1 + n_p])), res[1 + n_p:]


def _tok(width):
    return pl.BlockSpec((TOK_TILE, width), lambda i: (i, 0))


def _once(shape):
    zeros = (0,) * len(shape)
    return pl.BlockSpec(shape, lambda i: zeros, pipeline_mode=pl.Buffered(1))


def _rms_fwd(x):
    r = lax.rsqrt(_mean(x * x) + EPS)
    return x * r, r


def _rms_bwd(dy, xn, r, g):
    gd = dy * g
    return r * (gd - xn * _mean(xn * gd))


def _in_proj(x, g_pre, wt_in):
    def body(x_ref, g_ref, wt_ref, pm_ref, gab_ref, h_ref):
        xn, _ = _rms_fwd(x_ref[...])
        h = (xn * g_ref[...]).astype(BF16)
        h_ref[...] = h
        pm_ref[:, 0:PM_XM] = _nt(h, wt_ref[0:IN_ALOW, :])
        pm_ref[:, PM_XM:PM_AL] = _nt(h, wt_ref[IN_XM:IN_GATES, :])
        pm_ref[:, PM_AL:PM_W] = _nt(h, wt_ref[IN_ALOW:IN_ALOW + 128, :])
        gab_ref[...] = _nt(h, wt_ref[IN_GATES:D_IN, :])

    return pl.pallas_call(
        body, grid=(N_TOK_TILE,),
        in_specs=[_tok(D_MODEL), _once((1, D_MODEL)), _once((D_IN, D_MODEL))],
        out_specs=[_tok(PM_W), _tok(GAB_W), _tok(D_MODEL)],
        out_shape=[jax.ShapeDtypeStruct((SEQ, PM_W), F32), jax.ShapeDtypeStruct((SEQ, GAB_W), F32),
                   jax.ShapeDtypeStruct((SEQ, D_MODEL), BF16)],
        compiler_params=_params(("arbitrary",)), name="in_proj",
    )(x, g_pre, wt_in)


def _merge_fwd(ab, gab, x, w_pa4, w_pb4, w_o, g_post):
    def body(ab_ref, gab_ref, x_ref, wpa_ref, wpb_ref, wo_ref, g_ref, x1_ref, mix_ref, mg_ref):
        a = ab_ref[:, 0:512]
        b = ab_ref[:, 512:1024]
        for j in range(N_CHIP):
            blk = slice(j * 256, (j + 1) * 256)
            ya = jnp.dot(a, wpa_ref[j], preferred_element_type=F32)
            yb = jnp.dot(b, wpb_ref[j], preferred_element_type=F32)
            sa = _sigmoid(gab_ref[:, j * 256:(j + 1) * 256])
            sb = _sigmoid(gab_ref[:, 1024 + j * 256:1024 + (j + 1) * 256])
            mg_ref[:, blk] = (sa * ya + sb * yb).astype(BF16)
        mix = jnp.dot(mg_ref[...], wo_ref[...], preferred_element_type=F32)
        mix_ref[...] = mix
        mn, _ = _rms_fwd(mix)
        x1_ref[...] = x_ref[...] + mn * g_ref[...]

    return pl.pallas_call(
        body, grid=(N_TOK_TILE,),
        in_specs=[_tok(1024), _tok(GAB_W), _tok(D_MODEL), _once((N_CHIP, 512, 256)), _once((N_CHIP, 512, 256)),
                  _once((D_MODEL, D_MODEL)), _once((1, D_MODEL))],
        out_specs=[_tok(D_MODEL), _tok(D_MODEL), _tok(D_MODEL)],
        out_shape=[jax.ShapeDtypeStruct((SEQ, D_MODEL), F32), jax.ShapeDtypeStruct((SEQ, D_MODEL), F32),
                   jax.ShapeDtypeStruct((SEQ, D_MODEL), BF16)],
        compiler_params=_params(("arbitrary",)), name="merge_fwd",
    )(ab, gab, x, w_pa4, w_pb4, w_o, g_post)


def _mlp(x1, target, g_pre, g_post, w_up4, w_down):
    def body(x1_ref, t_ref, gpre_ref, gpost_ref, wup_ref, wdn_ref,
             dx1_ref, u_ref, dd_ref, h2_ref, dpre_ref, dgpost_ref, dgpre_ref, loss_ref):
        @pl.when(pl.program_id(0) == 0)
        def _():
            dgpost_ref[...] = jnp.zeros_like(dgpost_ref)
            dgpre_ref[...] = jnp.zeros_like(dgpre_ref)
            loss_ref[...] = jnp.zeros_like(loss_ref)

        x1 = x1_ref[...]
        gpre = gpre_ref[...]
        gpost = gpost_ref[...]
        xn2, r2 = _rms_fwd(x1)
        h2 = (xn2 * gpre).astype(BF16)
        h2_ref[...] = h2
        rl = []
        d = jnp.zeros((TOK_TILE, D_MODEL), F32)
        for j in range(N_CHIP):
            blk = slice(j * 1024, (j + 1) * 1024)
            r = jnp.maximum(jnp.dot(h2, wup_ref[j], preferred_element_type=F32), 0.0)
            rl.append(r)
            u = (r * r).astype(BF16)
            u_ref[:, blk] = u
            d = d + jnp.dot(u, wdn_ref[blk, :], preferred_element_type=F32)
        dn, r3 = _rms_fwd(d)
        diff = x1 + dn * gpost - t_ref[...]
        loss_ref[...] += jnp.sum(diff * diff, keepdims=True) * (0.5 / D_MODEL)
        dy = diff * (1.0 / D_MODEL)
        dgpost_ref[...] += jnp.sum(dy * dn, axis=0, keepdims=True)
        dd = _rms_bwd(dy, dn, r3, gpost).astype(BF16)
        dd_ref[...] = dd
        dh2 = jnp.zeros((TOK_TILE, D_MODEL), F32)
        for j in range(N_CHIP):
            blk = slice(j * 1024, (j + 1) * 1024)
            dpre = (_nt(dd, wdn_ref[blk, :]) * (2.0 * rl[j])).astype(BF16)
            dpre_ref[:, blk] = dpre
            dh2 = dh2 + _nt(dpre, wup_ref[j])
        dgpre_ref[...] += jnp.sum(dh2 * xn2, axis=0, keepdims=True)
        dx1_ref[...] = dy + _rms_bwd(dh2, xn2, r2, gpre)

    acc = pl.BlockSpec((1, D_MODEL), lambda i: (0, 0))
    return pl.pallas_call(
        body, grid=(N_TOK_TILE,),
        in_specs=[_tok(D_MODEL), _tok(D_MODEL), _once((1, D_MODEL)), _once((1, D_MODEL)),
                  _once((N_CHIP, D_MODEL, 1024)), _once((D_FF, D_MODEL))],
        out_specs=[_tok(D_MODEL), _tok(D_FF), _tok(D_MODEL), _tok(D_MODEL), _tok(D_FF), acc, acc,
                   pl.BlockSpec((1, 128), lambda i: (0, 0))],
        out_shape=[jax.ShapeDtypeStruct((SEQ, D_MODEL), F32), jax.ShapeDtypeStruct((SEQ, D_FF), BF16),
                   jax.ShapeDtypeStruct((SEQ, D_MODEL), BF16), jax.ShapeDtypeStruct((SEQ, D_MODEL), BF16),
                   jax.ShapeDtypeStruct((SEQ, D_FF), BF16), jax.ShapeDtypeStruct((1, D_MODEL), F32),
                   jax.ShapeDtypeStruct((1, D_MODEL), F32), jax.ShapeDtypeStruct((1, 128), F32)],
        compiler_params=_params(("arbitrary",)), name="mlp_fwd_bwd",
    )(x1, target, g_pre, g_post, w_up4, w_down)


def _merge_bwd(dx1, mix, ab, gab, w_pa4, w_pb4, w_o, g_post):
    def body(dx1_ref, mix_ref, ab_ref, gab_ref, wpa_ref, wpb_ref, wo_ref, g_ref,
             dmix_ref, dya_ref, dyb_ref, dgab_ref, dab_ref, dg_ref):
        @pl.when(pl.program_id(0) == 0)
        def _():
            dg_ref[...] = jnp.zeros_like(dg_ref)

        dx1 = dx1_ref[...]
        mn, r = _rms_fwd(mix_ref[...])
        dg_ref[...] += jnp.sum(dx1 * mn, axis=0, keepdims=True)
        dmix = _rms_bwd(dx1, mn, r, g_ref[...]).astype(BF16)
        dmix_ref[...] = dmix
        dmerged = _nt(dmix, wo_ref[...])
        a = ab_ref[:, 0:512]
        b = ab_ref[:, 512:1024]
        da = jnp.zeros((TOK_TILE, 512), F32)
        db = jnp.zeros((TOK_TILE, 512), F32)
        for j in range(N_CHIP):
            blk = slice(j * 256, (j + 1) * 256)
            blk_b = slice(1024 + j * 256, 1024 + (j + 1) * 256)
            dm = dmerged[:, blk]
            ya = jnp.dot(a, wpa_ref[j], preferred_element_type=F32)
            yb = jnp.dot(b, wpb_ref[j], preferred_element_type=F32)
            sa = _sigmoid(gab_ref[:, blk])
            sb = _sigmoid(gab_ref[:, blk_b])
            dya = (dm * sa).astype(BF16)
            dyb = (dm * sb).astype(BF16)
            dya_ref[:, blk] = dya
            dyb_ref[:, blk] = dyb
            dgab_ref[:, blk] = (dm * ya * sa * (1.0 - sa)).astype(BF16)
            dgab_ref[:, blk_b] = (dm * yb * sb * (1.0 - sb)).astype(BF16)
            da = da + _nt(dya, wpa_ref[j])
            db = db + _nt(dyb, wpb_ref[j])
        dab_ref[:, 0:512] = da
        dab_ref[:, 512:1024] = db

    return pl.pallas_call(
        body, grid=(N_TOK_TILE,),
        in_specs=[_tok(D_MODEL), _tok(D_MODEL), _tok(1024), _tok(GAB_W), _once((N_CHIP, 512, 256)),
                  _once((N_CHIP, 512, 256)), _once((D_MODEL, D_MODEL)), _once((1, D_MODEL))],
        out_specs=[_tok(D_MODEL), _tok(D_MODEL), _tok(D_MODEL), _tok(GAB_W), _tok(1024),
                   pl.BlockSpec((1, D_MODEL), lambda i: (0, 0))],
        out_shape=[jax.ShapeDtypeStruct((SEQ, D_MODEL), BF16), jax.ShapeDtypeStruct((SEQ, D_MODEL), BF16),
                   jax.ShapeDtypeStruct((SEQ, D_MODEL), BF16), jax.ShapeDtypeStruct((SEQ, GAB_W), BF16),
                   jax.ShapeDtypeStruct((SEQ, 1024), F32), jax.ShapeDtypeStruct((1, D_MODEL), F32)],
        compiler_params=_params(("arbitrary",)), name="merge_bwd",
    )(dx1, mix, ab, gab, w_pa4, w_pb4, w_o, g_post)


def _in_proj_bwd(dpm, dgab, x, dx1, g_pre, wt_in):
    def body(dpm_ref, dgab_ref, x_ref, dx1_ref, g_ref, wt_ref, dx_ref, dg_ref):
        @pl.when(pl.program_id(0) == 0)
        def _():
            dg_ref[...] = jnp.zeros_like(dg_ref)

        dh = jnp.dot(dpm_ref[:, 0:PM_XM], wt_ref[0:IN_ALOW, :], preferred_element_type=F32)
        dh = dh + jnp.dot(dpm_ref[:, PM_XM:PM_AL], wt_ref[IN_XM:IN_GATES, :], preferred_element_type=F32)
        dh = dh + jnp.dot(dpm_ref[:, PM_AL:PM_W], wt_ref[IN_ALOW:IN_ALOW + 128, :], preferred_element_type=F32)
        dh = dh + jnp.dot(dgab_ref[...], wt_ref[IN_GATES:D_IN, :], preferred_element_type=F32)
        xn, r = _rms_fwd(x_ref[...])
        dg_ref[...] += jnp.sum(dh * xn, axis=0, keepdims=True)
        dx_ref[...] = dx1_ref[...] + _rms_bwd(dh, xn, r, g_ref[...])

    return pl.pallas_call(
        body, grid=(N_TOK_TILE,),
        in_specs=[_tok(PM_W), _tok(GAB_W), _tok(D_MODEL), _tok(D_MODEL), _once((1, D_MODEL)), _once((D_IN, D_MODEL))],
        out_specs=[_tok(D_MODEL), pl.BlockSpec((1, D_MODEL), lambda i: (0, 0))],
        out_shape=[jax.ShapeDtypeStruct((SEQ, D_MODEL), F32), jax.ShapeDtypeStruct((1, D_MODEL), F32)],
        compiler_params=_params(("arbitrary",)), name="in_proj_bwd",
    )(dpm, dgab, x, dx1, g_pre, wt_in)


def _dw_in(dpm, dgab, h):
    n_pm = PM_AL // 512
    n_blk = n_pm + GAB_W // 512

    def body(dpm_ref, dgab_ref, dal_ref, h_ref, o_ref):
        i = pl.program_id(0)
        off = pl.multiple_of(i * 512 + 16 * (i >= 3).astype(jnp.int32), 16)

        @pl.when(i < n_pm)
        def _():
            o_ref[pl.ds(off, 512), :] = _tn(dpm_ref[...], h_ref[...]).astype(BF16)

        @pl.when(i >= n_pm)
        def _():
            o_ref[pl.ds(off, 512), :] = _tn(dgab_ref[...], h_ref[...]).astype(BF16)

        @pl.when(i == 0)
        def _():
            o_ref[IN_ALOW:IN_XM, :] = _tn(dal_ref[...], h_ref[...])[0:IN_XM - IN_ALOW].astype(BF16)

    return pl.pallas_call(
        body, grid=(n_blk,),
        in_specs=[pl.BlockSpec((SEQ, 512), lambda i: (0, jnp.minimum(i, n_pm - 1))),
                  pl.BlockSpec((SEQ, 512), lambda i: (0, jnp.maximum(i - n_pm, 0))),
                  pl.BlockSpec((SEQ, 128), lambda i: (0, PM_AL // 128)),
                  _once((SEQ, D_MODEL))],
        out_specs=pl.BlockSpec((D_IN, D_MODEL), lambda i: (0, 0)),
        out_shape=jax.ShapeDtypeStruct((D_IN, D_MODEL), BF16),
        compiler_params=_params(("arbitrary",)), name="dw_in",
    )(dpm, dgab, dpm, h)


def _tn_matmul(a, b, name, shards=1, tm=512):
    m, n = a.shape[1], b.shape[1]
    tm = min(tm, m)
    tn = n // shards if shards > 1 else min(n, 1024)

    def body(a_ref, b_ref, o_ref):
        o_ref[...] = _tn(a_ref[...], b_ref[...]).astype(BF16)

    if shards > 1:
        out_spec = pl.BlockSpec((None, tm, tn), lambda i, j: (j, i, 0))
        out_shape = jax.ShapeDtypeStruct((shards, m, tn), BF16)
    else:
        out_spec = pl.BlockSpec((tm, tn), lambda i, j: (i, j))
        out_shape = jax.ShapeDtypeStruct((m, n), BF16)
    return pl.pallas_call(
        body, grid=(m // tm, n // tn),
        in_specs=[pl.BlockSpec((SEQ, tm), lambda i, j: (0, i)), pl.BlockSpec((SEQ, tn), lambda i, j: (0, j))],
        out_specs=out_spec, out_shape=out_shape,
        compiler_params=_params(("arbitrary", "arbitrary")), name=name,
    )(a, b)


MESH = pl.DeviceIdType.MESH
ANY = pl.BlockSpec(memory_space=pl.ANY)
VMEM_WHOLE = pl.BlockSpec(memory_space=pltpu.VMEM)

_BIG = ("w_in", "w_pa", "w_pb", "w_o", "w_up", "w_down")
_BIG_SHARD = {"w_in": (IN_SHARD, D_MODEL), "w_pa": (512, 256), "w_pb": (512, 256), "w_o": (256, D_MODEL),
              "w_up": (D_MODEL, 1024), "w_down": (1024, D_MODEL)}
_BIG_SPLIT = {"w_in": 1, "w_pa": 0, "w_pb": 0, "w_o": 0, "w_up": 0, "w_down": 0}


def _half(ref, e, name, lead=0):
    axis = _BIG_SPLIT[name]
    size = _BIG_SHARD[name][axis] // 2
    start = pl.multiple_of(e * size, 128 if axis == 1 else 16)
    idx = [pl.ds(0, ref.shape[a]) for a in range(lead)]
    idx += [pl.ds(start, size), pl.ds(0, _BIG_SHARD[name][1])] if axis == 0 else [pl.ds(0, _BIG_SHARD[name][0]), pl.ds(start, size)]
    return ref.at[tuple(idx)]


def _half_shape(name):
    r, c = _BIG_SHARD[name]
    return (r // 2, c) if _BIG_SPLIT[name] == 0 else (r, c // 2)


def _remote(src, dst, send_sems, recv_sems, k, to):
    return pltpu.make_async_remote_copy(src_ref=src, dst_ref=dst, send_sem=send_sems.at[k], recv_sem=recv_sems.at[k],
                                        device_id=to, device_id_type=MESH)


def _mesh_place():
    x, y, c = lax.axis_index("x"), lax.axis_index("y"), lax.axis_index("c")
    return x, y, c, [(1 - x, y), (x, 1 - y), (1 - x, 1 - y)]


class _Gather:
    def __init__(self, names, small=()):
        self.names = tuple(names)
        self.nb = len(self.names)
        self.n = self.nb + len(small)
        self.n_sems = 6 * self.n
        self.out_shape = [jax.ShapeDtypeStruct((N_CHIP,) + _BIG_SHARD[nm], BF16) for nm in self.names]
        self.out_shape += [jax.ShapeDtypeStruct((N_CHIP,) + s.shape, s.dtype) for s in small]

    def _ici(self, ins, outs, ss, rs, k, j, peer, slot, c):
        if k < self.nb:
            return _remote(_half(ins[k], c, self.names[k]), _half(outs[k].at[slot], c, self.names[k]), ss, rs, 6 * k + j,
                           (*peer, c))
        return _remote(ins[k], outs[k].at[slot], ss, rs, 6 * k + j, (*peer, c))

    def _passed(self, outs, ss, rs, k, j, slot, e, sibling):
        part = _half(outs[k].at[slot], e, self.names[k])
        return _remote(part, part, ss, rs, 6 * k + 3 + j, sibling)

    def first(self, ins, outs, ss, rs):
        x, y, c, peers = _mesh_place()
        me = 2 * x + y
        for k in range(self.n):
            for j, peer in enumerate(peers):
                self._ici(ins, outs, ss, rs, k, j, peer, me, c).start()
        for k in range(self.n):
            outs[k][me] = ins[k][...]

    def middle(self, ins, outs, ss, rs):
        x, y, c, peers = _mesh_place()
        for j, (px, py) in enumerate(peers):
            for k in range(self.nb):
                self._ici(ins, outs, ss, rs, k, j, (px, py), 2 * px + py, c).wait_recv()
                self._passed(outs, ss, rs, k, j, 2 * px + py, c, (x, y, 1 - c)).start()

    def last(self, ins, outs, ss, rs):
        x, y, c, peers = _mesh_place()
        for j, (px, py) in enumerate(peers):
            for k in range(self.n):
                if k < self.nb:
                    self._passed(outs, ss, rs, k, j, 2 * px + py, 1 - c, (x, y, 1 - c)).wait_recv()
                    self._passed(outs, ss, rs, k, j, 2 * px + py, c, (x, y, 1 - c)).wait_send()
                else:
                    self._ici(ins, outs, ss, rs, k, j, (px, py), 2 * px + py, c).wait_recv()
                self._ici(ins, outs, ss, rs, k, j, (px, py), 2 * x + y, c).wait_send()


def _run_alone(rider, ins, name):
    def body(*refs):
        r_in, r_out, sems = _split(refs, len(ins), len(rider.out_shape), 2)
        rider.first(r_in, r_out, *sems)
        rider.middle(r_in, r_out, *sems)
        rider.last(r_in, r_out, *sems)

    return pl.pallas_call(
        body, in_specs=[VMEM_WHOLE] * len(ins), out_specs=[VMEM_WHOLE] * len(rider.out_shape), out_shape=rider.out_shape,
        scratch_shapes=[pltpu.SemaphoreType.DMA((rider.n_sems,)), pltpu.SemaphoreType.DMA((rider.n_sems,))],
        compiler_params=_params(), name=name,
    )(*ins)


def _presum(names, grads, name):
    n = len(grads)

    def body(*refs):
        g_refs, got_refs, stage_refs, (send_sems, recv_sems, local_sems) = _split(refs, n, n, n, 3)
        x, y, c = lax.axis_index("x"), lax.axis_index("y"), lax.axis_index("c")

        def stage(e):
            cps = [pltpu.make_async_copy(_half(g_refs[k], e, names[k], lead=1), stage_refs[k], local_sems.at[k])
                   for k in range(n)]
            for cp in cps:
                cp.start()
            return cps

        staged = stage(1 - c)
        sends = []
        for k in range(n):
            staged[k].wait()
            cp = _remote(stage_refs[k], got_refs[k], send_sems, recv_sems, k, (x, y, 1 - c))
            cp.start()
            sends.append(cp)
        for cp in sends:
            cp.wait_send()
        staged = stage(c)
        for k in range(n):
            sends[k].wait_recv()
            staged[k].wait()

            @pl.loop(0, N_CHIP)
            def _(j):
                got_refs[k][j] = (got_refs[k][j].astype(F32) + stage_refs[k][j].astype(F32)).astype(BF16)

    half = [jax.ShapeDtypeStruct((N_CHIP,) + _half_shape(nm), BF16) for nm in names]
    return pl.pallas_call(
        body, in_specs=[ANY] * n, out_specs=[VMEM_WHOLE] * n, out_shape=half,
        scratch_shapes=[pltpu.VMEM(h.shape, h.dtype) for h in half]
        + [pltpu.SemaphoreType.DMA((n,)), pltpu.SemaphoreType.DMA((n,)), pltpu.SemaphoreType.DMA((n,))],
        compiler_params=_params(), name=name,
    )(*grads)


class _SendPartials:
    def __init__(self, names, small_shape=None):
        self.n = len(names)
        self.small = small_shape is not None
        self.n_sems = 3 * self.n + 7
        self.out_shape = [jax.ShapeDtypeStruct((N_CHIP,) + _half_shape(nm), BF16) for nm in names]
        if self.small:
            self.out_shape.append(jax.ShapeDtypeStruct((N_DEV,) + small_shape, F32))

    def _piece(self, ins, outs, ss, rs, k, j, peer, src_slot, dst_slot, c):
        return _remote(ins[k].at[src_slot], outs[k].at[dst_slot], ss, rs, 3 * k + j, (*peer, c))

    def _small(self, ins, outs, ss, rs, r, other, slot):
        return _remote(ins[self.n], outs[self.n].at[slot], ss, rs, 3 * self.n + r, other)

    @staticmethod
    def _others(x, y, c):
        return [(x, y, 1 - c), (1 - x, y, c), (1 - x, y, 1 - c), (x, 1 - y, c), (x, 1 - y, 1 - c),
                (1 - x, 1 - y, c), (1 - x, 1 - y, 1 - c)]

    def first(self, ins, outs, ss, rs):
        x, y, c, peers = _mesh_place()
        me = 2 * x + y
        for k in range(self.n):
            for j, (px, py) in enumerate(peers):
                self._piece(ins, outs, ss, rs, k, j, (px, py), 2 * px + py, me, c).start()
        if self.small:
            for r, other in enumerate(self._others(x, y, c)):
                self._small(ins, outs, ss, rs, r, other, 4 * x + 2 * y + c).start()
            outs[self.n][4 * x + 2 * y + c] = ins[self.n][...]
        for k in range(self.n):
            outs[k][me] = ins[k][me]

    def middle(self, ins, outs, ss, rs):
        pass

    def last(self, ins, outs, ss, rs):
        x, y, c, peers = _mesh_place()
        me = 2 * x + y
        for k in range(self.n):
            for j, (px, py) in enumerate(peers):
                self._piece(ins, outs, ss, rs, k, j, (px, py), me, 2 * px + py, c).wait_recv()
                self._piece(ins, outs, ss, rs, k, j, (px, py), 2 * px + py, me, c).wait_send()
        if self.small:
            for r, (px, py, pc) in enumerate(self._others(x, y, c)):
                self._small(ins, outs, ss, rs, r, (px, py, pc), 4 * px + 2 * py + pc).wait_recv()
                self._small(ins, outs, ss, rs, r, (px, py, pc), 4 * x + 2 * y + c).wait_send()


def _sum_swap(names, parts):
    n = len(parts)

    def body(*refs):
        p_refs, o_refs, (send_sems, recv_sems) = _split(refs, n, n, 2)
        x, y, c = lax.axis_index("x"), lax.axis_index("y"), lax.axis_index("c")
        for e in range(2):
            @pl.when(c == e)
            def _():
                for k in range(n):
                    g = p_refs[k][0].astype(F32)
                    for s in range(1, N_CHIP):
                        g = g + p_refs[k][s].astype(F32)
                    r, cols = _half_shape(names[k])
                    if _BIG_SPLIT[names[k]] == 0:
                        o_refs[k][e * r:(e + 1) * r, :] = g
                    else:
                        o_refs[k][:, e * cols:(e + 1) * cols] = g
        sends = []
        for k in range(n):
            mine = _half(o_refs[k], c, names[k])
            cp = _remote(mine, mine, send_sems, recv_sems, k, (x, y, 1 - c))
            cp.start()
            sends.append(cp)
        for k in range(n):
            theirs = _half(o_refs[k], 1 - c, names[k])
            _remote(theirs, theirs, send_sems, recv_sems, k, (x, y, 1 - c)).wait_recv()
        for cp in sends:
            cp.wait_send()

    return pl.pallas_call(
        body, in_specs=[VMEM_WHOLE] * n, out_specs=[VMEM_WHOLE] * n,
        out_shape=[jax.ShapeDtypeStruct(_BIG_SHARD[nm], F32) for nm in names],
        scratch_shapes=[pltpu.SemaphoreType.DMA((n,)), pltpu.SemaphoreType.DMA((n,))],
        compiler_params=_params(), name="sum_swap",
    )(*parts)


def _tile(rows, cols, itemsize, budget):
    t = cols if rows % 16 else rows
    other = rows if rows % 16 else cols
    step = 256 if rows % 16 else 32
    while t % step == 0 and t * other * itemsize > budget:
        t //= 2
    return (rows, t) if rows % 16 else (t, cols)


def _adamw_math(w, g, m, v):
    m = ADAM_B1 * m + (1.0 - ADAM_B1) * g
    v = ADAM_B2 * v + (1.0 - ADAM_B2) * (g * g)
    m_hat = m / (1.0 - ADAM_B1 ** ADAM_STEP)
    v_hat = v / (1.0 - ADAM_B2 ** ADAM_STEP)
    delta = -ADAM_LR * (m_hat / (jnp.sqrt(v_hat) + ADAM_EPS) + ADAM_WD * w)
    return delta, m, v


def _adamw_big(g, w, m, v, name):
    r, c = w.shape
    tr, tc = _tile(r, c, 4, 1024 * 1024)

    def body(g_ref, w_ref, m_ref, v_ref, d_ref, nm_ref, nv_ref):
        d_ref[...], nm_ref[...], nv_ref[...] = _adamw_math(w_ref[...], g_ref[...], m_ref[...], v_ref[...])

    blk = pl.BlockSpec((tr, tc), lambda i, l: (i, l))
    return pl.pallas_call(
        body, grid=(r // tr, c // tc), in_specs=[blk, blk, blk, blk],
        out_specs=[blk, blk, blk], out_shape=[jax.ShapeDtypeStruct((r, c), F32)] * 3,
        compiler_params=_params(("arbitrary", "arbitrary")), name=name,
    )(g, w, m, v)


def _sum_small(parts):
    def body(p_ref, o_ref):
        g = p_ref[0]
        for d in range(1, N_DEV):
            g = g + p_ref[d]
        o_ref[...] = g

    return pl.pallas_call(body, out_shape=jax.ShapeDtypeStruct(parts.shape[1:], F32), name="sum_small")(parts)


def _adamw_small(ws, gs, ms, vs):
    n = len(ws)

    def body(*refs):
        w_refs, g_refs, m_refs, v_refs, d_refs, nm_refs, nv_refs = _split(refs, *([n] * 7))
        for k in range(n):
            d_refs[k][...], nm_refs[k][...], nv_refs[k][...] = _adamw_math(w_refs[k][...], g_refs[k][...], m_refs[k][...],
                                                                             v_refs[k][...])

    shapes = [jax.ShapeDtypeStruct(w.shape, F32) for w in ws]
    res = pl.pallas_call(body, out_shape=shapes * 3, name="adamw_small")(*ws, *gs, *ms, *vs)
    return res[:n], res[n:2 * n], res[2 * n:]


def _pack(arrs):
    flat = jnp.concatenate([a.reshape(-1) for a in arrs])
    rows = -(-flat.shape[0] // 1024) * 8
    return jnp.pad(flat, (0, rows * 128 - flat.shape[0])).reshape(rows, 128)


def _unpack(buf, shapes):
    flat = buf.reshape(-1)
    out, off = [], 0
    for s in shapes:
        size = 1
        for d in s:
            size *= d
        out.append(flat[off:off + size].reshape(s))
        off += size
    return out


def _block_rows(w):
    return jnp.pad(w.reshape(512, 4), ((0, 0), (0, 124)))


def _cols(a4):
    return jnp.transpose(a4, (1, 0, 2)).reshape(a4.shape[1], -1)


_LATE = ("w_pa", "w_pb", "w_o", "w_up", "w_down")


def _full_weights(gathered):
    joined = {"w_in": (D_IN, D_MODEL), "w_o": (D_MODEL, D_MODEL), "w_down": (D_FF, D_MODEL)}
    return {n: (a.reshape(joined[n]) if n in joined else a) for n, a in gathered.items()}


def _local_step(x, target, w, sp, late_shards=None):
    sp = {n: (a.reshape(1, -1) if a.ndim == 1 else a) for n, a in sp.items()}
    wau = jnp.zeros((128, 256), F32).at[0:16].set(sp["w_a_up"])
    wif = jnp.zeros((1536, 128), F32).at[:, 0:8].set(sp["w_if"])
    bif = jnp.zeros((1, 128), F32).at[:, 0:8].set(sp["b_if"])
    p = {"wau": wau, "bau": sp["b_a_up"], "ggla": sp["g_gla_norm"], "cw": sp["conv_w"], "cb": sp["conv_b"],
         "wq": _block_rows(sp["w_q_ml"]), "wk": _block_rows(sp["w_k_ml"]), "wv": _block_rows(sp["w_v_ml"]),
         "wif": wif, "bif": bif, "skip": sp["ml_skip"], "gml": sp["g_ml_norm"]}

    pm, gab, h = _in_proj(x, sp["g_pre_mix"], w["w_in"])
    if late_shards is None:
        ab, *states = _mixer_fwd(pm, p)
    else:
        ab, *rest = _mixer_fwd(pm, p, _Gather(_LATE), late_shards)
        states = rest[:4]
        w = dict(w, **_full_weights(dict(zip(_LATE, rest[4:]))))
    x1, mix, merged = _merge_fwd(ab, gab, x, w["w_pa"], w["w_pb"], w["w_o"], sp["g_post_mix"])
    dx1, u, dd, h2, dpre, dg_post_mlp, dg_pre_mlp, loss = _mlp(x1, target, sp["g_pre_mlp"], sp["g_post_mlp"],
                                                                w["w_up"], w["w_down"])
    dmix, dya, dyb, dgab, dab, dg_post_mix = _merge_bwd(dx1, mix, ab, gab, w["w_pa"], w["w_pb"], w["w_o"], sp["g_post_mix"])
    big = {
        "w_pa": _tn_matmul(ab[:, 0:512], dya, "dw_pa", shards=N_CHIP),
        "w_pb": _tn_matmul(ab[:, 512:1024], dyb, "dw_pb", shards=N_CHIP),
        "w_o": _tn_matmul(merged, dmix, "dw_o"),
        "w_up": _tn_matmul(h2, dpre, "dw_up", shards=N_CHIP),
        "w_down": _tn_matmul(u, dd, "dw_down"),
    }
    if late_shards is None:
        dpm, dp, _ = _mixer_bwd(pm, dab, states, p)
    else:
        partial = _presum(_LATE, [big[n].reshape((N_CHIP,) + _BIG_SHARD[n]) for n in _LATE], "presum_late")
        dpm, dp, parts = _mixer_bwd(pm, dab, states, p, _SendPartials(_LATE), partial)
        big = dict(zip(_LATE, parts))
    dx, dg_pre_mix = _in_proj_bwd(dpm, dgab, x, dx1, sp["g_pre_mix"], w["w_in"])
    big["w_in"] = _dw_in(dpm, dgab, h)
    small = {
        "g_pre_mix": dg_pre_mix, "b_a_up": dp["bau"], "g_gla_norm": dp["ggla"], "conv_b": dp["cb"],
        "w_q_ml": dp["wq"][:, 0:4].reshape(128, 4, 4), "w_k_ml": dp["wk"][:, 0:4].reshape(128, 4, 4),
        "w_v_ml": dp["wv"][:, 0:4].reshape(128, 4, 4),
        "b_if": dp["bif"][:, 0:8], "ml_skip": dp["skip"], "g_ml_norm": dp["gml"], "g_post_mix": dg_post_mix,
        "g_pre_mlp": dg_pre_mlp, "g_post_mlp": dg_post_mlp, "w_a_up": dp["wau"][0:16], "conv_w": dp["cw"],
        "w_if": dp["wif"][:, 0:8], "loss": loss[:, 0:1],
    }
    return dx, big, small


_SMALL_REPL = ("g_pre_mix", "b_a_up", "g_gla_norm", "conv_b", "w_q_ml", "w_k_ml", "w_v_ml", "b_if", "ml_skip",
               "g_ml_norm", "g_post_mix", "g_pre_mlp", "g_post_mlp")
_SMALL_SHARDED = ("w_a_up", "conv_w", "w_if")
_SMALL_ORDER = _SMALL_REPL + _SMALL_SHARDED + ("loss",)
_WEIGHTS = ("g_pre_mix", "w_in", "w_a_up", "b_a_up", "g_gla_norm", "conv_w", "conv_b", "w_q_ml", "w_k_ml", "w_v_ml",
            "w_if", "b_if", "ml_skip", "g_ml_norm", "w_pa", "w_pb", "w_o", "g_post_mix", "g_pre_mlp", "w_up", "w_down",
            "g_post_mlp")


def _as_shard(name, a):
    return a[0].T if name == "w_in" else a[0]


def _from_shard(name, a):
    return (a.T if name == "w_in" else a)[None]


def kernel(x, g_pre_mix, w_in, w_a_up, b_a_up, g_gla_norm, conv_w, conv_b, w_q_ml, w_k_ml, w_v_ml, w_if, b_if, ml_skip, g_ml_norm, w_pa, w_pb, w_o, g_post_mix, g_pre_mlp, w_up, w_down, g_post_mlp, loss_target, m_g_pre_mix, m_w_in, m_w_a_up, m_b_a_up, m_g_gla_norm, m_conv_w, m_conv_b, m_w_q_ml, m_w_k_ml, m_w_v_ml, m_w_if, m_b_if, m_ml_skip, m_g_ml_norm, m_w_pa, m_w_pb, m_w_o, m_g_post_mix, m_g_pre_mlp, m_w_up, m_w_down, m_g_post_mlp, v_g_pre_mix, v_w_in, v_w_a_up, v_b_a_up, v_g_gla_norm, v_conv_w, v_conv_b, v_w_q_ml, v_w_k_ml, v_w_v_ml, v_w_if, v_b_if, v_ml_skip, v_g_ml_norm, v_w_pa, v_w_pb, v_w_o, v_g_post_mix, v_g_pre_mlp, v_w_up, v_w_down, v_g_post_mlp):
    args = dict(locals())
    wts = {n: _as_shard(n, args[n]) for n in _WEIGHTS}
    mom = {n: _as_shard(n, args["m_" + n]) for n in _WEIGHTS}
    var = {n: _as_shard(n, args["v_" + n]) for n in _WEIGHTS}
    chip = 2 * lax.axis_index("x") + lax.axis_index("y")

    first = ("w_in",) + _SMALL_SHARDED
    gathered = dict(zip(first, _run_alone(_Gather(("w_in",), [wts[n] for n in _SMALL_SHARDED]),
                                          [wts[n].astype(BF16) if n == "w_in" else wts[n] for n in first],
                                          "gather_first")))
    sp = {n: wts[n] for n in _SMALL_REPL}
    sp["w_a_up"] = _cols(gathered["w_a_up"])
    sp["conv_w"] = _cols(gathered["conv_w"])
    sp["w_if"] = gathered["w_if"].reshape(1536, 8)

    dx, big, small = _local_step(x[0], loss_target[0], _full_weights({"w_in": gathered["w_in"]}), sp,
                                 late_shards=[wts[n].astype(BF16) for n in _LATE])

    small_shapes = [small[n].shape for n in _SMALL_ORDER]
    packed = _pack([small[n] for n in _SMALL_ORDER])
    partial = _presum(("w_in",), [big["w_in"].reshape((N_CHIP,) + _BIG_SHARD["w_in"])], "presum_w_in")
    parts_in, small_parts = _run_alone(_SendPartials(("w_in",), packed.shape), [*partial, packed], "send_partials")
    big["w_in"] = parts_in
    sums = _sum_swap(_BIG, [big[n] for n in _BIG])

    grads, delta, new_m, new_v = {}, {}, {}, {}
    for n, g in zip(_BIG, sums):
        d, nm, nv = _adamw_big(g, wts[n], mom[n], var[n], "adamw_" + n)
        grads[n], delta[n], new_m[n], new_v[n] = (_from_shard(n, a) for a in (g, d, nm, nv))
    summed = dict(zip(_SMALL_ORDER, _unpack(_sum_small(small_parts), small_shapes)))
    loss = summed["loss"].reshape(())
    for n in _SMALL_REPL:
        grads[n] = summed[n].reshape(args[n].shape)
    grads["w_a_up"] = lax.dynamic_slice_in_dim(summed["w_a_up"], chip * 64, 64, axis=1)[None]
    grads["conv_w"] = lax.dynamic_slice_in_dim(summed["conv_w"], chip * 128, 128, axis=1)[None]
    grads["w_if"] = lax.dynamic_slice_in_dim(summed["w_if"], chip * 384, 384, axis=0)[None]
    small_names = _SMALL_REPL + _SMALL_SHARDED
    upd = _adamw_small([args[n] for n in small_names], [grads[n] for n in small_names],
                       [args["m_" + n] for n in small_names], [args["v_" + n] for n in small_names])
    for dst, arrs in zip((delta, new_m, new_v), upd):
        dst.update(zip(small_names, arrs))

    outs = [loss, dx[None]]
    for group in (grads, delta, new_m, new_v):
        outs += [group[n] for n in _WEIGHTS]
    return tuple(outs)
```

```python
import functools

import jax
import jax.numpy as jnp
from jax import lax
from jax.experimental import pallas as pl
from jax.experimental.pallas import tpu as pltpu

F32 = jnp.float32
BF16 = jnp.bfloat16

SEQ = 2048
D_MODEL = 1024
CHUNK = 64
N_CHUNK = SEQ // CHUNK
HEADS = 4
GLA_DK = 64
GLA_DV = 128
ML_DH = 128
D_FF = 4096
EPS = 1e-6
N_CHIP = 4
N_DEV = 8
TOK_TILE = 256
N_TOK_TILE = SEQ // TOK_TILE
SWEEP = 2
N_SWEEP = N_CHUNK // SWEEP

PM_W = 2688
PM_XM = 1536
PM_OP = 2048
PM_AL = 2560
GAB_W = 2048
D_IN = 4624
IN_SHARD = D_IN // N_CHIP
IN_ALOW = 1536
IN_XM = 1552
IN_GATES = 2576

ADAM_LR = 0.001
ADAM_B1 = 0.9
ADAM_B2 = 0.999
ADAM_EPS = 1e-08
ADAM_WD = 0.01
ADAM_STEP = 10

VMEM_LIMIT = 56 * 1024 * 1024


def _params(sem=None):
    return pltpu.CompilerParams(dimension_semantics=sem, vmem_limit_bytes=VMEM_LIMIT)


def _dot(a, b, ca, cb):
    return lax.dot_general(a.astype(BF16), b.astype(BF16), (((ca,), (cb,)), ((), ())), preferred_element_type=F32)


def _pmm_nn(a, b):
    return _dot(a, b, 1, 0)


def _pmm_nt(a, b):
    return _dot(a, b, 1, 1)


def _pmm_tn(a, b):
    return _dot(a, b, 0, 0)


def _pcmm(c, x):
    return lax.dot_general(c, x, (((1,), (0,)), ((), ())), precision=lax.Precision.HIGHEST, preferred_element_type=F32)


@jax.custom_vjp
def _mm_nn(a, b):
    return _dot(a, b, 1, 0)


@jax.custom_vjp
def _mm_nt(a, b):
    return _dot(a, b, 1, 1)


@jax.custom_vjp
def _mm_tn(a, b):
    return _dot(a, b, 0, 0)


_mm_nn.defvjp(lambda a, b: (_dot(a, b, 1, 0), (a, b)), lambda r, g: (_mm_nt(g, r[1]), _mm_tn(r[0], g)))
_mm_nt.defvjp(lambda a, b: (_dot(a, b, 1, 1), (a, b)), lambda r, g: (_mm_nn(g, r[1]), _mm_tn(g, r[0])))
_mm_tn.defvjp(lambda a, b: (_dot(a, b, 0, 0), (a, b)), lambda r, g: (_mm_nt(r[1], g), _mm_nn(r[0], g)))


@jax.custom_vjp
def _cmm(c, x):
    return _pcmm(c, x)


_cmm.defvjp(
    lambda c, x: (_pcmm(c, x), c),
    lambda c, g: (jnp.zeros_like(c), lax.dot_general(c, g, (((0,), (0,)), ((), ())), precision=lax.Precision.HIGHEST,
                                                      preferred_element_type=F32)),
)

_PLAIN_OPS = (_pmm_nn, _pmm_nt, _pmm_tn, _pcmm)
_VJP_OPS = (_mm_nn, _mm_nt, _mm_tn, _cmm)


def _sigmoid(x):
    return 0.5 * (jnp.tanh(0.5 * x) + 1.0)


def _log_sigmoid(x):
    return jnp.minimum(x, 0.0) - jnp.log(1.0 + jnp.exp(-jnp.abs(x)))


def _mean(x):
    return jnp.mean(x, axis=-1, keepdims=True)


def _nt(a, b):
    return lax.dot_general(a, b, (((1,), (1,)), ((), ())), preferred_element_type=F32)


def _tn(a, b):
    return lax.dot_general(a, b, (((0,), (0,)), ((), ())), preferred_element_type=F32)


def _mixer_chunk(ops, p, st, pm, xprev8):
    mm_nn, mm_nt, mm_tn, cmm = ops
    row = lax.broadcasted_iota(jnp.int32, (CHUNK, CHUNK), 0)
    col = lax.broadcasted_iota(jnp.int32, (CHUNK, CHUNK), 1)
    causal = row >= col
    tri = causal.astype(F32)
    q = pm[:, 0:256]
    k = pm[:, 256:512]
    v = pm[:, 512:1024]
    g = pm[:, 1024:1536]
    xm = pm[:, PM_XM:PM_XM + 512]
    opre = pm[:, PM_OP:PM_OP + 512]
    alow = pm[:, PM_AL:PM_AL + 128]

    la = _log_sigmoid(mm_nn(alow, p["wau"]) + p["bau"]) * (1.0 / 16.0)
    cum = cmm(tri, la)
    cum_last = cum[CHUNK - 1:CHUNK, :]
    e_pos = jnp.exp(cum)
    e_neg = jnp.exp(-cum)
    qs = q * (GLA_DK ** -0.5)
    qp = qs * e_pos
    qn = qs * e_neg
    kp = k * e_pos
    kn = k * e_neg
    kl = k * jnp.exp(cum_last - cum)
    dec = jnp.exp(cum_last)
    outs = []
    s_new = []
    for h in range(HEADS):
        s6 = slice(h * GLA_DK, (h + 1) * GLA_DK)
        s12 = slice(h * GLA_DV, (h + 1) * GLA_DV)
        scores = jnp.where(causal, mm_nt(qp[:, s6], kn[:, s6]), mm_nt(qn[:, s6], kp[:, s6]))
        o = mm_nn(scores, v[:, s12]) + mm_nt(qp[:, s6], st["S"][h])
        s_new.append(st["S"][h] * dec[:, s6] + mm_tn(v[:, s12], kl[:, s6]))
        o = o * lax.rsqrt(_mean(o * o) + EPS) * p["ggla"]
        gh = g[:, s12]
        outs.append(o * (gh * _sigmoid(gh)))

    xx = jnp.concatenate([xprev8, xm], axis=0)
    pre = p["cb"]
    for j in range(4):
        pre = pre + p["cw"][j:j + 1, :] * xx[5 + j:5 + j + CHUNK, :]
    xc = pre * _sigmoid(pre)
    qm, km, vm = [], [], []
    for h in range(HEADS):
        s12 = slice(h * ML_DH, (h + 1) * ML_DH)
        qm.append(mm_nn(xc[:, s12], p["wq"][h]))
        km.append(mm_nn(xc[:, s12], p["wk"][h]))
        vm.append(mm_nn(xm[:, s12], p["wv"][h]))
    qcat = jnp.concatenate(qm, axis=1)
    kcat = jnp.concatenate(km, axis=1)
    vcat = jnp.concatenate(vm, axis=1)
    gates = (mm_nn(qcat, p["wif"][0:512]) + mm_nn(kcat, p["wif"][512:1024]) + mm_nn(vcat, p["wif"][1024:1536])
             + p["bif"])
    lf = _log_sigmoid(gates)
    fc = cmm(tri, lf)
    gates_t = gates.T
    fc_t = fc.T
    c_new, n_new, m_new = [], [], []
    for h in range(HEADS):
        s12 = slice(h * ML_DH, (h + 1) * ML_DH)
        li_c = gates[:, h:h + 1]
        fc_c = fc[:, 4 + h:5 + h]
        li_r = gates_t[h:h + 1, :]
        fc_r = fc_t[4 + h:5 + h, :]
        m_prev = st["m"][h][:, 0:1]
        log_d = li_r - jnp.abs(fc_c - fc_r)
        g_int = fc_c + m_prev
        m_t = jnp.maximum(g_int, jnp.max(log_d, axis=1, keepdims=True))
        ks = km[h] * (ML_DH ** -0.5)
        s = mm_nt(qm[h], ks) * jnp.exp(log_d - m_t)
        scl = jnp.exp(g_int - m_t)
        num = mm_nn(s, vm[h]) + scl * mm_nn(qm[h], st["C"][h])
        den = jnp.sum(s, axis=1, keepdims=True) + scl * jnp.sum(qm[h] * st["n"][h], axis=1, keepdims=True)
        den = jnp.maximum(jnp.abs(den), jnp.exp(-m_t))
        hc = num / den * _sigmoid(opre[:, s12])
        d0 = hc - _mean(hc)
        y = d0 * lax.rsqrt(_mean(d0 * d0) + EPS)
        outs.append(y * p["gml"][:, s12] + p["skip"][:, s12] * xc[:, s12])
        f_last = fc[CHUNK - 1:CHUNK, 4 + h:5 + h]
        a = f_last - fc_c + li_c
        m_loc = jnp.max(a, axis=0, keepdims=True)
        kw = ks * jnp.exp(a - m_loc)
        m_nx = jnp.maximum(f_last + m_prev, m_loc)
        sp = jnp.exp(f_last + m_prev - m_nx)
        sl = jnp.exp(m_loc - m_nx)
        c_new.append(sp * st["C"][h] + sl * mm_tn(kw, vm[h]))
        n_new.append(sp * st["n"][h] + sl * jnp.sum(kw, axis=0, keepdims=True))
        m_new.append(jnp.broadcast_to(m_nx, (1, ML_DH)))
    ab = jnp.concatenate(outs, axis=1)
    new = {"S": s_new, "C": c_new, "n": n_new, "m": m_new}
    return ab, new


_P_NAMES = ("wau", "bau", "ggla", "cw", "cb", "wq", "wk", "wv", "wif", "bif", "skip", "gml")
_P_SHAPES = {
    "wau": (128, 256), "bau": (1, 256), "ggla": (1, 128), "cw": (4, 512), "cb": (1, 512),
    "wq": (512, 128), "wk": (512, 128), "wv": (512, 128),
    "wif": (1536, 128), "bif": (1, 128), "skip": (1, 512), "gml": (1, 512),
}
_P_BLOCKDIAG = ("wq", "wk", "wv")
_S_NAMES = ("S", "C", "n", "m")
_S_SHAPES = {"S": (HEADS, GLA_DV, GLA_DK), "C": (HEADS, ML_DH, ML_DH), "n": (HEADS, 1, ML_DH), "m": (HEADS, 1, ML_DH)}


def _per_head(ref):
    return [ref[h] for h in range(HEADS)]


def _block_mask():
    r = lax.broadcasted_iota(jnp.int32, (128, 128), 0)
    c = lax.broadcasted_iota(jnp.int32, (128, 128), 1)
    same_block = (r >> 2) == (c >> 2)
    spread = jnp.logical_and(r < 4, (c & 3) == r)
    return same_block.astype(F32), spread.astype(F32)


def _expand_blockdiag(w_ref, dense_ref):
    same_block, spread = _block_mask()
    for h in range(HEADS):
        tiled = _pmm_nn(w_ref[h * 128:(h + 1) * 128, :], spread)
        dense_ref[h] = tiled * same_block


def _collect_blockdiag(ddense_ref, dw_ref):
    same_block, spread = _block_mask()
    for h in range(HEADS):
        dw_ref[h * 128:(h + 1) * 128, :] = lax.dot_general(
            ddense_ref[h] * same_block, spread, (((1,), (1,)), ((), ())), precision=lax.Precision.HIGHEST,
            preferred_element_type=F32)


def _const_spec(shape):
    zeros = (0,) * len(shape)
    return pl.BlockSpec(shape, lambda i: zeros)


def _split(refs, *counts):
    out, at = [], 0
    for c in counts:
        out.append(refs[at:at + c])
        at += c
    assert at == len(refs)
    return out


def _ride(rider, phases, cond, ins, outs, sems):
    if rider is None:
        return
    lands, (send_sems, recv_sems, flush_sems) = sems[:-3], sems[-3:]

    @pl.when(cond)
    def _():
        for phase in phases:
            getattr(rider, phase)(ins, lands, send_sems, recv_sems)
        if "last" in phases:
            flush = [pltpu.make_async_copy(lands[k], outs[k], flush_sems.at[k]) for k in range(len(outs))]
            for cp in flush:
                cp.start()
            for cp in flush:
                cp.wait()


def _rider_specs(rider, rider_ins):
    if rider is None:
        return [], [], [], []
    scratch = [pltpu.VMEM(s.shape, s.dtype) for s in rider.out_shape]
    scratch += [pltpu.SemaphoreType.DMA((rider.n_sems,)), pltpu.SemaphoreType.DMA((rider.n_sems,)),
                pltpu.SemaphoreType.DMA((len(rider.out_shape),))]
    return [VMEM_WHOLE] * len(rider_ins), [ANY] * len(rider.out_shape), list(rider.out_shape), scratch


def _mixer_fwd(pm, p, rider=None, rider_ins=()):
    n_p = len(_P_NAMES)
    r_in, r_out_specs, r_out_shape, r_sems = _rider_specs(rider, rider_ins)

    def body(*refs):
        (pm_ref, xprev_ref), p_list, ride_in, (ab_ref,), so_refs, ride_out, sc_refs, dense_list, sems = _split(
            refs, 2, n_p, len(r_in), 1, 4, len(r_out_specs), 4, 3, len(r_sems))
        p_refs = dict(zip(_P_NAMES, p_list))
        dense = dict(zip(_P_BLOCKDIAG, dense_list))
        n = pl.program_id(0)
        _ride(rider, ("first",), n == 0, ride_in, ride_out, sems)

        @pl.when(n == 0)
        def _():
            for r in sc_refs:
                r[...] = jnp.zeros_like(r)
            for nm in _P_BLOCKDIAG:
                _expand_blockdiag(p_refs[nm], dense[nm])

        st = {name: _per_head(r) for name, r in zip(_S_NAMES, sc_refs)}
        pv = {nm: (_per_head(dense[nm]) if nm in _P_BLOCKDIAG else p_refs[nm][...]) for nm in _P_NAMES}
        for g in range(SWEEP):
            rows = slice(g * CHUNK, (g + 1) * CHUNK)
            for name, r in zip(_S_NAMES, so_refs):
                for h in range(HEADS):
                    r[g, h] = st[name][h]
            if g == 0:
                xprev8 = jnp.where(n > 0, xprev_ref[CHUNK - 8:CHUNK, :], 0.0)
            else:
                xprev8 = pm_ref[g * CHUNK - 8:g * CHUNK, PM_XM:PM_XM + 512]
            ab, st = _mixer_chunk(_PLAIN_OPS, pv, st, pm_ref[rows, :], xprev8)
            ab_ref[rows, :] = ab.astype(BF16)
        for name, r in zip(_S_NAMES, sc_refs):
            for h in range(HEADS):
                r[h] = st[name][h]
        _ride(rider, ("middle",), n == N_SWEEP - 2, ride_in, ride_out, sems)
        _ride(rider, ("last",), n == N_SWEEP - 1, ride_in, ride_out, sems)

    in_specs = [pl.BlockSpec((SWEEP * CHUNK, PM_W), lambda i: (i, 0)),
                pl.BlockSpec((CHUNK, 512), lambda i: (jnp.maximum(SWEEP * i - 1, 0), PM_XM // 512))]
    in_specs += [_const_spec(_P_SHAPES[nm]) for nm in _P_NAMES] + r_in
    out_specs = [pl.BlockSpec((SWEEP * CHUNK, 1024), lambda i: (i, 0))]
    out_shape = [jax.ShapeDtypeStruct((SEQ, 1024), BF16)]
    for nm in _S_NAMES:
        shp = _S_SHAPES[nm]
        out_specs.append(pl.BlockSpec((SWEEP,) + shp, lambda i: (i, 0, 0, 0)))
        out_shape.append(jax.ShapeDtypeStruct((N_CHUNK,) + shp, F32))
    return pl.pallas_call(
        body, grid=(N_SWEEP,), in_specs=in_specs, out_specs=out_specs + r_out_specs, out_shape=out_shape + r_out_shape,
        scratch_shapes=[pltpu.VMEM(_S_SHAPES[nm], F32) for nm in _S_NAMES]
        + [pltpu.VMEM((HEADS, 128, 128), F32) for _ in _P_BLOCKDIAG] + r_sems,
        compiler_params=_params(("arbitrary",)), name="mixer_fwd",
    )(pm, pm, *[p[nm] for nm in _P_NAMES], *rider_ins)


def _mixer_bwd(pm, dab, states, p, rider=None, rider_ins=()):
    n_p = len(_P_NAMES)
    r_in, r_out_specs, r_out_shape, r_sems = _rider_specs(rider, rider_ins)

    def body(*refs):
        ((pm_ref, xprev_ref, dab_ref), si_refs, p_list, ride_in, (dpm_ref,), dp_list, ride_out, ds_refs, (carry_ref,),
         dense_list, ddense_list, sems) = _split(refs, 3, 4, n_p, len(r_in), 1, n_p, len(r_out_specs), 4, 1, 3, 3, len(r_sems))
        p_refs = dict(zip(_P_NAMES, p_list))
        dp_refs = dict(zip(_P_NAMES, dp_list))
        dense = dict(zip(_P_BLOCKDIAG, dense_list))
        ddense = dict(zip(_P_BLOCKDIAG, ddense_list))
        i = pl.program_id(0)
        blk = N_SWEEP - 1 - i
        _ride(rider, ("first",), i == 0, ride_in, ride_out, sems)

        @pl.when(i == 0)
        def _():
            for r in ds_refs:
                r[...] = jnp.zeros_like(r)
            for nm in _P_NAMES:
                if nm in _P_BLOCKDIAG:
                    ddense[nm][...] = jnp.zeros_like(ddense[nm])
                    _expand_blockdiag(p_refs[nm], dense[nm])
                else:
                    dp_refs[nm][...] = jnp.zeros_like(dp_refs[nm])
            carry_ref[...] = jnp.zeros_like(carry_ref)

        pv = {nm: (_per_head(dense[nm]) if nm in _P_BLOCKDIAG else p_refs[nm][...]) for nm in _P_NAMES}
        dst = {name: _per_head(r) for name, r in zip(_S_NAMES, ds_refs)}
        carry = carry_ref[...]
        dp_sum = None
        for g in reversed(range(SWEEP)):
            rows = slice(g * CHUNK, (g + 1) * CHUNK)
            st = {name: [r[g, h] for h in range(HEADS)] for name, r in zip(_S_NAMES, si_refs)}
            if g == 0:
                xprev8 = jnp.where(blk > 0, xprev_ref[CHUNK - 8:CHUNK, :], 0.0)
            else:
                xprev8 = pm_ref[g * CHUNK - 8:g * CHUNK, PM_XM:PM_XM + 512]
            _, vjp = jax.vjp(functools.partial(_mixer_chunk, _VJP_OPS), pv, st, pm_ref[rows, :], xprev8)
            dp, dst, dpm, dxprev8 = vjp((dab_ref[rows, :], dst))
            reach = jnp.concatenate([jnp.zeros((CHUNK - 8, 512), F32), carry], axis=0)
            dpm_ref[rows, 0:PM_XM] = dpm[:, 0:PM_XM].astype(BF16)
            dpm_ref[rows, PM_XM:PM_XM + 512] = (dpm[:, PM_XM:PM_XM + 512] + reach).astype(BF16)
            dpm_ref[rows, PM_XM + 512:PM_W] = dpm[:, PM_XM + 512:PM_W].astype(BF16)
            carry = dxprev8
            dp_sum = dp if dp_sum is None else jax.tree.map(jnp.add, dp_sum, dp)
        carry_ref[...] = carry
        for name, r in zip(_S_NAMES, ds_refs):
            for h in range(HEADS):
                r[h] = dst[name][h]
        for nm in _P_NAMES:
            if nm in _P_BLOCKDIAG:
                for h in range(HEADS):
                    ddense[nm][h] += dp_sum[nm][h]
            else:
                dp_refs[nm][...] += dp_sum[nm]

        @pl.when(i == N_SWEEP - 1)
        def _():
            for nm in _P_BLOCKDIAG:
                _collect_blockdiag(ddense[nm], dp_refs[nm])

        _ride(rider, ("middle",), i == N_SWEEP - 2, ride_in, ride_out, sems)
        _ride(rider, ("last",), i == N_SWEEP - 1, ride_in, ride_out, sems)

    rev = lambda i: (N_SWEEP - 1 - i, 0)
    in_specs = [pl.BlockSpec((SWEEP * CHUNK, PM_W), rev),
                pl.BlockSpec((CHUNK, 512), lambda i: (jnp.maximum(SWEEP * (N_SWEEP - 1 - i) - 1, 0), PM_XM // 512)),
                pl.BlockSpec((SWEEP * CHUNK, 1024), rev)]
    for nm in _S_NAMES:
        in_specs.append(pl.BlockSpec((SWEEP,) + _S_SHAPES[nm], lambda i: (N_SWEEP - 1 - i, 0, 0, 0)))
    in_specs += [_const_spec(_P_SHAPES[nm]) for nm in _P_NAMES] + r_in
    out_specs = [pl.BlockSpec((SWEEP * CHUNK, PM_W), rev)] + [_const_spec(_P_SHAPES[nm]) for nm in _P_NAMES]
    out_shape = [jax.ShapeDtypeStruct((SEQ, PM_W), BF16)] + [jax.ShapeDtypeStruct(_P_SHAPES[nm], F32) for nm in _P_NAMES]
    res = pl.pallas_call(
        body, grid=(N_SWEEP,), in_specs=in_specs, out_specs=out_specs + r_out_specs, out_shape=out_shape + r_out_shape,
        scratch_shapes=[pltpu.VMEM(_S_SHAPES[nm], F32) for nm in _S_NAMES] + [pltpu.VMEM((8, 512), F32)]
        + [pltpu.VMEM((HEADS, 128, 128), F32) for _ in range(2 * len(_P_BLOCKDIAG))] + r_sems,
        compiler_params=_params(("arbitrary",)), name="mixer_bwd",
    )(pm, pm, dab, *states, *[p[nm] for nm in _P_NAMES], *rider_ins)
    return res[0], dict(zip(_P_NAMES, res[1:1 + n_p])), res[1 + n_p:]


def _tok(width):
    return pl.BlockSpec((TOK_TILE, width), lambda i: (i, 0))


def _once(shape):
    zeros = (0,) * len(shape)
    return pl.BlockSpec(shape, lambda i: zeros, pipeline_mode=pl.Buffered(1))


def _rms_fwd(x):
    r = lax.rsqrt(_mean(x * x) + EPS)
    return x * r, r


def _rms_bwd(dy, xn, r, g):
    gd = dy * g
    return r * (gd - xn * _mean(xn * gd))


def _in_proj(x, g_pre, wt_in):
    def body(x_ref, g_ref, wt_ref, pm_ref, gab_ref, h_ref):
        xn, _ = _rms_fwd(x_ref[...])
        h = (xn * g_ref[...]).astype(BF16)
        h_ref[...] = h
        pm_ref[:, 0:PM_XM] = _nt(h, wt_ref[0:IN_ALOW, :])
        pm_ref[:, PM_XM:PM_AL] = _nt(h, wt_ref[IN_XM:IN_GATES, :])
        pm_ref[:, PM_AL:PM_W] = _nt(h, wt_ref[IN_ALOW:IN_ALOW + 128, :])
        gab_ref[...] = _nt(h, wt_ref[IN_GATES:D_IN, :])

    return pl.pallas_call(
        body, grid=(N_TOK_TILE,),
        in_specs=[_tok(D_MODEL), _once((1, D_MODEL)), _once((D_IN, D_MODEL))],
        out_specs=[_tok(PM_W), _tok(GAB_W), _tok(D_MODEL)],
        out_shape=[jax.ShapeDtypeStruct((SEQ, PM_W), F32), jax.ShapeDtypeStruct((SEQ, GAB_W), F32),
                   jax.ShapeDtypeStruct((SEQ, D_MODEL), BF16)],
        compiler_params=_params(("arbitrary",)), name="in_proj",
    )(x, g_pre, wt_in)


def _merge_fwd(ab, gab, x, w_pa4, w_pb4, w_o, g_post):
    def body(ab_ref, gab_ref, x_ref, wpa_ref, wpb_ref, wo_ref, g_ref, x1_ref, mix_ref, mg_ref):
        a = ab_ref[:, 0:512]
        b = ab_ref[:, 512:1024]
        for j in range(N_CHIP):
            blk = slice(j * 256, (j + 1) * 256)
            ya = jnp.dot(a, wpa_ref[j], preferred_element_type=F32)
            yb = jnp.dot(b, wpb_ref[j], preferred_element_type=F32)
            sa = _sigmoid(gab_ref[:, j * 256:(j + 1) * 256])
            sb = _sigmoid(gab_ref[:, 1024 + j * 256:1024 + (j + 1) * 256])
            mg_ref[:, blk] = (sa * ya + sb * yb).astype(BF16)
        mix = jnp.dot(mg_ref[...], wo_ref[...], preferred_element_type=F32)
        mix_ref[...] = mix
        mn, _ = _rms_fwd(mix)
        x1_ref[...] = x_ref[...] + mn * g_ref[...]

    return pl.pallas_call(
        body, grid=(N_TOK_TILE,),
        in_specs=[_tok(1024), _tok(GAB_W), _tok(D_MODEL), _once((N_CHIP, 512, 256)), _once((N_CHIP, 512, 256)),
                  _once((D_MODEL, D_MODEL)), _once((1, D_MODEL))],
        out_specs=[_tok(D_MODEL), _tok(D_MODEL), _tok(D_MODEL)],
        out_shape=[jax.ShapeDtypeStruct((SEQ, D_MODEL), F32), jax.ShapeDtypeStruct((SEQ, D_MODEL), F32),
                   jax.ShapeDtypeStruct((SEQ, D_MODEL), BF16)],
        compiler_params=_params(("arbitrary",)), name="merge_fwd",
    )(ab, gab, x, w_pa4, w_pb4, w_o, g_post)


def _mlp(x1, target, g_pre, g_post, w_up4, w_down):
    def body(x1_ref, t_ref, gpre_ref, gpost_ref, wup_ref, wdn_ref,
             dx1_ref, u_ref, dd_ref, h2_ref, dpre_ref, dgpost_ref, dgpre_ref, loss_ref):
        @pl.when(pl.program_id(0) == 0)
        def _():
            dgpost_ref[...] = jnp.zeros_like(dgpost_ref)
            dgpre_ref[...] = jnp.zeros_like(dgpre_ref)
            loss_ref[...] = jnp.zeros_like(loss_ref)

        x1 = x1_ref[...]
        gpre = gpre_ref[...]
        gpost = gpost_ref[...]
        xn2, r2 = _rms_fwd(x1)
        h2 = (xn2 * gpre).astype(BF16)
        h2_ref[...] = h2
        rl = []
        d = jnp.zeros((TOK_TILE, D_MODEL), F32)
        for j in range(N_CHIP):
            blk = slice(j * 1024, (j + 1) * 1024)
            r = jnp.maximum(jnp.dot(h2, wup_ref[j], preferred_element_type=F32), 0.0)
            rl.append(r)
            u = (r * r).astype(BF16)
            u_ref[:, blk] = u
            d = d + jnp.dot(u, wdn_ref[blk, :], preferred_element_type=F32)
        dn, r3 = _rms_fwd(d)
        diff = x1 + dn * gpost - t_ref[...]
        loss_ref[...] += jnp.sum(diff * diff, keepdims=True) * (0.5 / D_MODEL)
        dy = diff * (1.0 / D_MODEL)
        dgpost_ref[...] += jnp.sum(dy * dn, axis=0, keepdims=True)
        dd = _rms_bwd(dy, dn, r3, gpost).astype(BF16)
        dd_ref[...] = dd
        dh2 = jnp.zeros((TOK_TILE, D_MODEL), F32)
        for j in range(N_CHIP):
            blk = slice(j * 1024, (j + 1) * 1024)
            dpre = (_nt(dd, wdn_ref[blk, :]) * (2.0 * rl[j])).astype(BF16)
            dpre_ref[:, blk] = dpre
            dh2 = dh2 + _nt(dpre, wup_ref[j])
        dgpre_ref[...] += jnp.sum(dh2 * xn2, axis=0, keepdims=True)
        dx1_ref[...] = dy + _rms_bwd(dh2, xn2, r2, gpre)

    acc = pl.BlockSpec((1, D_MODEL), lambda i: (0, 0))
    return pl.pallas_call(
        body, grid=(N_TOK_TILE,),
        in_specs=[_tok(D_MODEL), _tok(D_MODEL), _once((1, D_MODEL)), _once((1, D_MODEL)),
                  _once((N_CHIP, D_MODEL, 1024)), _once((D_FF, D_MODEL))],
        out_specs=[_tok(D_MODEL), _tok(D_FF), _tok(D_MODEL), _tok(D_MODEL), _tok(D_FF), acc, acc,
                   pl.BlockSpec((1, 128), lambda i: (0, 0))],
        out_shape=[jax.ShapeDtypeStruct((SEQ, D_MODEL), F32), jax.ShapeDtypeStruct((SEQ, D_FF), BF16),
                   jax.ShapeDtypeStruct((SEQ, D_MODEL), BF16), jax.ShapeDtypeStruct((SEQ, D_MODEL), BF16),
                   jax.ShapeDtypeStruct((SEQ, D_FF), BF16), jax.ShapeDtypeStruct((1, D_MODEL), F32),
                   jax.ShapeDtypeStruct((1, D_MODEL), F32), jax.ShapeDtypeStruct((1, 128), F32)],
        compiler_params=_params(("arbitrary",)), name="mlp_fwd_bwd",
    )(x1, target, g_pre, g_post, w_up4, w_down)


def _merge_bwd(dx1, mix, ab, gab, w_pa4, w_pb4, w_o, g_post):
    def body(dx1_ref, mix_ref, ab_ref, gab_ref, wpa_ref, wpb_ref, wo_ref, g_ref,
             dmix_ref, dya_ref, dyb_ref, dgab_ref, dab_ref, dg_ref):
        @pl.when(pl.program_id(0) == 0)
        def _():
            dg_ref[...] = jnp.zeros_like(dg_ref)

        dx1 = dx1_ref[...]
        mn, r = _rms_fwd(mix_ref[...])
        dg_ref[...] += jnp.sum(dx1 * mn, axis=0, keepdims=True)
        dmix = _rms_bwd(dx1, mn, r, g_ref[...]).astype(BF16)
        dmix_ref[...] = dmix
        dmerged = _nt(dmix, wo_ref[...])
        a = ab_ref[:, 0:512]
        b = ab_ref[:, 512:1024]
        da = jnp.zeros((TOK_TILE, 512), F32)
        db = jnp.zeros((TOK_TILE, 512), F32)
        for j in range(N_CHIP):
            blk = slice(j * 256, (j + 1) * 256)
            blk_b = slice(1024 + j * 256, 1024 + (j + 1) * 256)
            dm = dmerged[:, blk]
            ya = jnp.dot(a, wpa_ref[j], preferred_element_type=F32)
            yb = jnp.dot(b, wpb_ref[j], preferred_element_type=F32)
            sa = _sigmoid(gab_ref[:, blk])
            sb = _sigmoid(gab_ref[:, blk_b])
            dya = (dm * sa).astype(BF16)
            dyb = (dm * sb).astype(BF16)
            dya_ref[:, blk] = dya
            dyb_ref[:, blk] = dyb
            dgab_ref[:, blk] = (dm * ya * sa * (1.0 - sa)).astype(BF16)
            dgab_ref[:, blk_b] = (dm * yb * sb * (1.0 - sb)).astype(BF16)
            da = da + _nt(dya, wpa_ref[j])
            db = db + _nt(dyb, wpb_ref[j])
        dab_ref[:, 0:512] = da
        dab_ref[:, 512:1024] = db

    return pl.pallas_call(
        body, grid=(N_TOK_TILE,),
        in_specs=[_tok(D_MODEL), _tok(D_MODEL), _tok(1024), _tok(GAB_W), _once((N_CHIP, 512, 256)),
                  _once((N_CHIP, 512, 256)), _once((D_MODEL, D_MODEL)), _once((1, D_MODEL))],
        out_specs=[_tok(D_MODEL), _tok(D_MODEL), _tok(D_MODEL), _tok(GAB_W), _tok(1024),
                   pl.BlockSpec((1, D_MODEL), lambda i: (0, 0))],
        out_shape=[jax.ShapeDtypeStruct((SEQ, D_MODEL), BF16), jax.ShapeDtypeStruct((SEQ, D_MODEL), BF16),
                   jax.ShapeDtypeStruct((SEQ, D_MODEL), BF16), jax.ShapeDtypeStruct((SEQ, GAB_W), BF16),
                   jax.ShapeDtypeStruct((SEQ, 1024), F32), jax.ShapeDtypeStruct((1, D_MODEL), F32)],
        compiler_params=_params(("arbitrary",)), name="merge_bwd",
    )(dx1, mix, ab, gab, w_pa4, w_pb4, w_o, g_post)


def _in_proj_bwd(dpm, dgab, x, dx1, g_pre, wt_in):
    def body(dpm_ref, dgab_ref, x_ref, dx1_ref, g_ref, wt_ref, dx_ref, dg_ref):
        @pl.when(pl.program_id(0) == 0)
        def _():
            dg_ref[...] = jnp.zeros_like(dg_ref)

        dh = jnp.dot(dpm_ref[:, 0:PM_XM], wt_ref[0:IN_ALOW, :], preferred_element_type=F32)
        dh = dh + jnp.dot(dpm_ref[:, PM_XM:PM_AL], wt_ref[IN_XM:IN_GATES, :], preferred_element_type=F32)
        dh = dh + jnp.dot(dpm_ref[:, PM_AL:PM_W], wt_ref[IN_ALOW:IN_ALOW + 128, :], preferred_element_type=F32)
        dh = dh + jnp.dot(dgab_ref[...], wt_ref[IN_GATES:D_IN, :], preferred_element_type=F32)
        xn, r = _rms_fwd(x_ref[...])
        dg_ref[...] += jnp.sum(dh * xn, axis=0, keepdims=True)
        dx_ref[...] = dx1_ref[...] + _rms_bwd(dh, xn, r, g_ref[...])

    return pl.pallas_call(
        body, grid=(N_TOK_TILE,),
        in_specs=[_tok(PM_W), _tok(GAB_W), _tok(D_MODEL), _tok(D_MODEL), _once((1, D_MODEL)), _once((D_IN, D_MODEL))],
        out_specs=[_tok(D_MODEL), pl.BlockSpec((1, D_MODEL), lambda i: (0, 0))],
        out_shape=[jax.ShapeDtypeStruct((SEQ, D_MODEL), F32), jax.ShapeDtypeStruct((1, D_MODEL), F32)],
        compiler_params=_params(("arbitrary",)), name="in_proj_bwd",
    )(dpm, dgab, x, dx1, g_pre, wt_in)


def _dw_in(dpm, dgab, h):
    n_pm = PM_AL // 512
    n_blk = n_pm + GAB_W // 512

    def body(dpm_ref, dgab_ref, dal_ref, h_ref, o_ref):
        i = pl.program_id(0)
        off = pl.multiple_of(i * 512 + 16 * (i >= 3).astype(jnp.int32), 16)

        @pl.when(i < n_pm)
        def _():
            o_ref[pl.ds(off, 512), :] = _tn(dpm_ref[...], h_ref[...]).astype(BF16)

        @pl.when(i >= n_pm)
        def _():
            o_ref[pl.ds(off, 512), :] = _tn(dgab_ref[...], h_ref[...]).astype(BF16)

        @pl.when(i == 0)
        def _():
            o_ref[IN_ALOW:IN_XM, :] = _tn(dal_ref[...], h_ref[...])[0:IN_XM - IN_ALOW].astype(BF16)

    return pl.pallas_call(
        body, grid=(n_blk,),
        in_specs=[pl.BlockSpec((SEQ, 512), lambda i: (0, jnp.minimum(i, n_pm - 1))),
                  pl.BlockSpec((SEQ, 512), lambda i: (0, jnp.maximum(i - n_pm, 0))),
                  pl.BlockSpec((SEQ, 128), lambda i: (0, PM_AL // 128)),
                  _once((SEQ, D_MODEL))],
        out_specs=pl.BlockSpec((D_IN, D_MODEL), lambda i: (0, 0)),
        out_shape=jax.ShapeDtypeStruct((D_IN, D_MODEL), BF16),
        compiler_params=_params(("arbitrary",)), name="dw_in",
    )(dpm, dgab, dpm, h)


def _tn_matmul(a, b, name, shards=1, tm=512):
    m, n = a.shape[1], b.shape[1]
    tm = min(tm, m)
    tn = n // shards if shards > 1 else min(n, 1024)

    def body(a_ref, b_ref, o_ref):
        o_ref[...] = _tn(a_ref[...], b_ref[...]).astype(BF16)

    if shards > 1:
        out_spec = pl.BlockSpec((None, tm, tn), lambda i, j: (j, i, 0))
        out_shape = jax.ShapeDtypeStruct((shards, m, tn), BF16)
    else:
        out_spec = pl.BlockSpec((tm, tn), lambda i, j: (i, j))
        out_shape = jax.ShapeDtypeStruct((m, n), BF16)
    return pl.pallas_call(
        body, grid=(m // tm, n // tn),
        in_specs=[pl.BlockSpec((SEQ, tm), lambda i, j: (0, i)), pl.BlockSpec((SEQ, tn), lambda i, j: (0, j))],
        out_specs=out_spec, out_shape=out_shape,
        compiler_params=_params(("arbitrary", "arbitrary")), name=name,
    )(a, b)


MESH = pl.DeviceIdType.MESH
ANY = pl.BlockSpec(memory_space=pl.ANY)
VMEM_WHOLE = pl.BlockSpec(memory_space=pltpu.VMEM)

_BIG = ("w_in", "w_pa", "w_pb", "w_o", "w_up", "w_down")
_BIG_SHARD = {"w_in": (IN_SHARD, D_MODEL), "w_pa": (512, 256), "w_pb": (512, 256), "w_o": (256, D_MODEL),
              "w_up": (D_MODEL, 1024), "w_down": (1024, D_MODEL)}
_BIG_SPLIT = {"w_in": 1, "w_pa": 0, "w_pb": 0, "w_o": 0, "w_up": 0, "w_down": 0}


def _half(ref, e, name, lead=0):
    axis = _BIG_SPLIT[name]
    size = _BIG_SHARD[name][axis] // 2
    start = pl.multiple_of(e * size, 128 if axis == 1 else 16)
    idx = [pl.ds(0, ref.shape[a]) for a in range(lead)]
    idx += [pl.ds(start, size), pl.ds(0, _BIG_SHARD[name][1])] if axis == 0 else [pl.ds(0, _BIG_SHARD[name][0]), pl.ds(start, size)]
    return ref.at[tuple(idx)]


def _half_shape(name):
    r, c = _BIG_SHARD[name]
    return (r // 2, c) if _BIG_SPLIT[name] == 0 else (r, c // 2)


def _remote(src, dst, send_sems, recv_sems, k, to):
    return pltpu.make_async_remote_copy(src_ref=src, dst_ref=dst, send_sem=send_sems.at[k], recv_sem=recv_sems.at[k],
                                        device_id=to, device_id_type=MESH)


def _mesh_place():
    x, y, c = lax.axis_index("x"), lax.axis_index("y"), lax.axis_index("c")
    return x, y, c, [(1 - x, y), (x, 1 - y), (1 - x, 1 - y)]


class _Gather:
    def __init__(self, names, small=()):
        self.names = tuple(names)
        self.nb = len(self.names)
        self.n = self.nb + len(small)
        self.n_sems = 6 * self.n
        self.out_shape = [jax.ShapeDtypeStruct((N_CHIP,) + _BIG_SHARD[nm], BF16) for nm in self.names]
        self.out_shape += [jax.ShapeDtypeStruct((N_CHIP,) + s.shape, s.dtype) for s in small]

    def _ici(self, ins, outs, ss, rs, k, j, peer, slot, c):
        if k < self.nb:
            return _remote(_half(ins[k], c, self.names[k]), _half(outs[k].at[slot], c, self.names[k]), ss, rs, 6 * k + j,
                           (*peer, c))
        return _remote(ins[k], outs[k].at[slot], ss, rs, 6 * k + j, (*peer, c))

    def _passed(self, outs, ss, rs, k, j, slot, e, sibling):
        part = _half(outs[k].at[slot], e, self.names[k])
        return _remote(part, part, ss, rs, 6 * k + 3 + j, sibling)

    def first(self, ins, outs, ss, rs):
        x, y, c, peers = _mesh_place()
        me = 2 * x + y
        for k in range(self.n):
            for j, peer in enumerate(peers):
                self._ici(ins, outs, ss, rs, k, j, peer, me, c).start()
        for k in range(self.n):
            outs[k][me] = ins[k][...]

    def middle(self, ins, outs, ss, rs):
        x, y, c, peers = _mesh_place()
        for j, (px, py) in enumerate(peers):
            for k in range(self.nb):
                self._ici(ins, outs, ss, rs, k, j, (px, py), 2 * px + py, c).wait_recv()
                self._passed(outs, ss, rs, k, j, 2 * px + py, c, (x, y, 1 - c)).start()

    def last(self, ins, outs, ss, rs):
        x, y, c, peers = _mesh_place()
        for j, (px, py) in enumerate(peers):
            for k in range(self.n):
                if k < self.nb:
                    self._passed(outs, ss, rs, k, j, 2 * px + py, 1 - c, (x, y, 1 - c)).wait_recv()
                    self._passed(outs, ss, rs, k, j, 2 * px + py, c, (x, y, 1 - c)).wait_send()
                else:
                    self._ici(ins, outs, ss, rs, k, j, (px, py), 2 * px + py, c).wait_recv()
                self._ici(ins, outs, ss, rs, k, j, (px, py), 2 * x + y, c).wait_send()


def _run_alone(rider, ins, name):
    def body(*refs):
        r_in, r_out, sems = _split(refs, len(ins), len(rider.out_shape), 2)
        rider.first(r_in, r_out, *sems)
        rider.middle(r_in, r_out, *sems)
        rider.last(r_in, r_out, *sems)

    return pl.pallas_call(
        body, in_specs=[VMEM_WHOLE] * len(ins), out_specs=[VMEM_WHOLE] * len(rider.out_shape), out_shape=rider.out_shape,
        scratch_shapes=[pltpu.SemaphoreType.DMA((rider.n_sems,)), pltpu.SemaphoreType.DMA((rider.n_sems,))],
        compiler_params=_params(), name=name,
    )(*ins)


def _presum(names, grads, name):
    n = len(grads)

    def body(*refs):
        g_refs, got_refs, stage_refs, (send_sems, recv_sems, local_sems) = _split(refs, n, n, n, 3)
        x, y, c = lax.axis_index("x"), lax.axis_index("y"), lax.axis_index("c")

        def stage(e):
            cps = [pltpu.make_async_copy(_half(g_refs[k], e, names[k], lead=1), stage_refs[k], local_sems.at[k])
                   for k in range(n)]
            for cp in cps:
                cp.start()
            return cps

        staged = stage(1 - c)
        sends = []
        for k in range(n):
            staged[k].wait()
            cp = _remote(stage_refs[k], got_refs[k], send_sems, recv_sems, k, (x, y, 1 - c))
            cp.start()
            sends.append(cp)
        for cp in sends:
            cp.wait_send()
        staged = stage(c)
        for k in range(n):
            sends[k].wait_recv()
            staged[k].wait()

            @pl.loop(0, N_CHIP)
            def _(j):
                got_refs[k][j] = (got_refs[k][j].astype(F32) + stage_refs[k][j].astype(F32)).astype(BF16)

    half = [jax.ShapeDtypeStruct((N_CHIP,) + _half_shape(nm), BF16) for nm in names]
    return pl.pallas_call(
        body, in_specs=[ANY] * n, out_specs=[VMEM_WHOLE] * n, out_shape=half,
        scratch_shapes=[pltpu.VMEM(h.shape, h.dtype) for h in half]
        + [pltpu.SemaphoreType.DMA((n,)), pltpu.SemaphoreType.DMA((n,)), pltpu.SemaphoreType.DMA((n,))],
        compiler_params=_params(), name=name,
    )(*grads)


class _SendPartials:
    def __init__(self, names, small_shape=None):
        self.n = len(names)
        self.small = small_shape is not None
        self.n_sems = 3 * self.n + 7
        self.out_shape = [jax.ShapeDtypeStruct((N_CHIP,) + _half_shape(nm), BF16) for nm in names]
        if self.small:
            self.out_shape.append(jax.ShapeDtypeStruct((N_DEV,) + small_shape, F32))

    def _piece(self, ins, outs, ss, rs, k, j, peer, src_slot, dst_slot, c):
        return _remote(ins[k].at[src_slot], outs[k].at[dst_slot], ss, rs, 3 * k + j, (*peer, c))

    def _small(self, ins, outs, ss, rs, r, other, slot):
        return _remote(ins[self.n], outs[self.n].at[slot], ss, rs, 3 * self.n + r, other)

    @staticmethod
    def _others(x, y, c):
        return [(x, y, 1 - c), (1 - x, y, c), (1 - x, y, 1 - c), (x, 1 - y, c), (x, 1 - y, 1 - c),
                (1 - x, 1 - y, c), (1 - x, 1 - y, 1 - c)]

    def first(self, ins, outs, ss, rs):
        x, y, c, peers = _mesh_place()
        me = 2 * x + y
        for k in range(self.n):
            for j, (px, py) in enumerate(peers):
                self._piece(ins, outs, ss, rs, k, j, (px, py), 2 * px + py, me, c).start()
        if self.small:
            for r, other in enumerate(self._others(x, y, c)):
                self._small(ins, outs, ss, rs, r, other, 4 * x + 2 * y + c).start()
            outs[self.n][4 * x + 2 * y + c] = ins[self.n][...]
        for k in range(self.n):
            outs[k][me] = ins[k][me]

    def middle(self, ins, outs, ss, rs):
        pass

    def last(self, ins, outs, ss, rs):
        x, y, c, peers = _mesh_place()
        me = 2 * x + y
        for k in range(self.n):
            for j, (px, py) in enumerate(peers):
                self._piece(ins, outs, ss, rs, k, j, (px, py), me, 2 * px + py, c).wait_recv()
                self._piece(ins, outs, ss, rs, k, j, (px, py), 2 * px + py, me, c).wait_send()
        if self.small:
            for r, (px, py, pc) in enumerate(self._others(x, y, c)):
                self._small(ins, outs, ss, rs, r, (px, py, pc), 4 * px + 2 * py + pc).wait_recv()
                self._small(ins, outs, ss, rs, r, (px, py, pc), 4 * x + 2 * y + c).wait_send()


def _sum_swap(names, parts):
    n = len(parts)

    def body(*refs):
        p_refs, o_refs, (send_sems, recv_sems) = _split(refs, n, n, 2)
        x, y, c = lax.axis_index("x"), lax.axis_index("y"), lax.axis_index("c")
        for e in range(2):
            @pl.when(c == e)
            def _():
                for k in range(n):
                    g = p_refs[k][0].astype(F32)
                    for s in range(1, N_CHIP):
                        g = g + p_refs[k][s].astype(F32)
                    r, cols = _half_shape(names[k])
                    if _BIG_SPLIT[names[k]] == 0:
                        o_refs[k][e * r:(e + 1) * r, :] = g
                    else:
                        o_refs[k][:, e * cols:(e + 1) * cols] = g
        sends = []
        for k in range(n):
            mine = _half(o_refs[k], c, names[k])
            cp = _remote(mine, mine, send_sems, recv_sems, k, (x, y, 1 - c))
            cp.start()
            sends.append(cp)
        for k in range(n):
            theirs = _half(o_refs[k], 1 - c, names[k])
            _remote(theirs, theirs, send_sems, recv_sems, k, (x, y, 1 - c)).wait_recv()
        for cp in sends:
            cp.wait_send()

    return pl.pallas_call(
        body, in_specs=[VMEM_WHOLE] * n, out_specs=[VMEM_WHOLE] * n,
        out_shape=[jax.ShapeDtypeStruct(_BIG_SHARD[nm], F32) for nm in names],
        scratch_shapes=[pltpu.SemaphoreType.DMA((n,)), pltpu.SemaphoreType.DMA((n,))],
        compiler_params=_params(), name="sum_swap",
    )(*parts)


def _tile(rows, cols, itemsize, budget):
    t = cols if rows % 16 else rows
    other = rows if rows % 16 else cols
    step = 256 if rows % 16 else 32
    while t % step == 0 and t * other * itemsize > budget:
        t //= 2
    return (rows, t) if rows % 16 else (t, cols)


def _adamw_math(w, g, m, v):
    m = ADAM_B1 * m + (1.0 - ADAM_B1) * g
    v = ADAM_B2 * v + (1.0 - ADAM_B2) * (g * g)
    m_hat = m / (1.0 - ADAM_B1 ** ADAM_STEP)
    v_hat = v / (1.0 - ADAM_B2 ** ADAM_STEP)
    delta = -ADAM_LR * (m_hat / (jnp.sqrt(v_hat) + ADAM_EPS) + ADAM_WD * w)
    return delta, m, v


def _adamw_big(g, w, m, v, name):
    r, c = w.shape
    tr, tc = _tile(r, c, 4, 1024 * 1024)

    def body(g_ref, w_ref, m_ref, v_ref, d_ref, nm_ref, nv_ref):
        d_ref[...], nm_ref[...], nv_ref[...] = _adamw_math(w_ref[...], g_ref[...], m_ref[...], v_ref[...])

    blk = pl.BlockSpec((tr, tc), lambda i, l: (i, l))
    return pl.pallas_call(
        body, grid=(r // tr, c // tc), in_specs=[blk, blk, blk, blk],
        out_specs=[blk, blk, blk], out_shape=[jax.ShapeDtypeStruct((r, c), F32)] * 3,
        compiler_params=_params(("arbitrary", "arbitrary")), name=name,
    )(g, w, m, v)


def _adamw_rows(g, w, m, v, name):
    r, _, c = w.shape
    tc = 128

    def body(g_ref, w_ref, m_ref, v_ref, g3_ref, d_ref, nm_ref, nv_ref):
        g = g_ref[...]
        g3_ref[:, 0, :] = g
        d_ref[:, 0, :], nm_ref[:, 0, :], nv_ref[:, 0, :] = _adamw_math(w_ref[:, 0, :], g, m_ref[:, 0, :], v_ref[:, 0, :])

    rows = pl.BlockSpec((r, 1, tc), lambda l: (0, 0, l))
    return pl.pallas_call(
        body, grid=(c // tc,), in_specs=[pl.BlockSpec((r, tc), lambda l: (0, l)), rows, rows, rows],
        out_specs=[rows] * 4, out_shape=[jax.ShapeDtypeStruct((r, 1, c), F32)] * 4,
        compiler_params=_params(("arbitrary",)), name=name,
    )(g, w, m, v)


def _sum_small(parts):
    def body(p_ref, o_ref):
        g = p_ref[0]
        for d in range(1, N_DEV):
            g = g + p_ref[d]
        o_ref[...] = g

    return pl.pallas_call(body, out_shape=jax.ShapeDtypeStruct(parts.shape[1:], F32), name="sum_small")(parts)


def _adamw_small(ws, gs, ms, vs):
    n = len(ws)

    def body(*refs):
        w_refs, g_refs, m_refs, v_refs, d_refs, nm_refs, nv_refs = _split(refs, *([n] * 7))
        for k in range(n):
            d_refs[k][...], nm_refs[k][...], nv_refs[k][...] = _adamw_math(w_refs[k][...], g_refs[k][...], m_refs[k][...],
                                                                             v_refs[k][...])

    shapes = [jax.ShapeDtypeStruct(w.shape, F32) for w in ws]
    res = pl.pallas_call(body, out_shape=shapes * 3, name="adamw_small")(*ws, *gs, *ms, *vs)
    return res[:n], res[n:2 * n], res[2 * n:]


def _pack(arrs):
    flat = jnp.concatenate([a.reshape(-1) for a in arrs])
    rows = -(-flat.shape[0] // 1024) * 8
    return jnp.pad(flat, (0, rows * 128 - flat.shape[0])).reshape(rows, 128)


def _unpack(buf, shapes):
    flat = buf.reshape(-1)
    out, off = [], 0
    for s in shapes:
        size = 1
        for d in s:
            size *= d
        out.append(flat[off:off + size].reshape(s))
        off += size
    return out


def _block_rows(w):
    return jnp.pad(w.reshape(512, 4), ((0, 0), (0, 124)))


def _cols(a4):
    return jnp.transpose(a4, (1, 0, 2)).reshape(a4.shape[1], -1)


_LATE = ("w_pa", "w_pb", "w_o", "w_up", "w_down")


def _full_weights(gathered):
    joined = {"w_in": (D_IN, D_MODEL), "w_o": (D_MODEL, D_MODEL), "w_down": (D_FF, D_MODEL)}
    return {n: (a.reshape(joined[n]) if n in joined else a) for n, a in gathered.items()}


def _local_step(x, target, w, sp, late_shards=None):
    sp = {n: (a.reshape(1, -1) if a.ndim == 1 else a) for n, a in sp.items()}
    wau = jnp.zeros((128, 256), F32).at[0:16].set(sp["w_a_up"])
    wif = jnp.zeros((1536, 128), F32).at[:, 0:8].set(sp["w_if"])
    bif = jnp.zeros((1, 128), F32).at[:, 0:8].set(sp["b_if"])
    p = {"wau": wau, "bau": sp["b_a_up"], "ggla": sp["g_gla_norm"], "cw": sp["conv_w"], "cb": sp["conv_b"],
         "wq": _block_rows(sp["w_q_ml"]), "wk": _block_rows(sp["w_k_ml"]), "wv": _block_rows(sp["w_v_ml"]),
         "wif": wif, "bif": bif, "skip": sp["ml_skip"], "gml": sp["g_ml_norm"]}

    pm, gab, h = _in_proj(x, sp["g_pre_mix"], w["w_in"])
    if late_shards is None:
        ab, *states = _mixer_fwd(pm, p)
    else:
        ab, *rest = _mixer_fwd(pm, p, _Gather(_LATE), late_shards)
        states = rest[:4]
        w = dict(w, **_full_weights(dict(zip(_LATE, rest[4:]))))
    x1, mix, merged = _merge_fwd(ab, gab, x, w["w_pa"], w["w_pb"], w["w_o"], sp["g_post_mix"])
    dx1, u, dd, h2, dpre, dg_post_mlp, dg_pre_mlp, loss = _mlp(x1, target, sp["g_pre_mlp"], sp["g_post_mlp"],
                                                                w["w_up"], w["w_down"])
    dmix, dya, dyb, dgab, dab, dg_post_mix = _merge_bwd(dx1, mix, ab, gab, w["w_pa"], w["w_pb"], w["w_o"], sp["g_post_mix"])
    big = {
        "w_pa": _tn_matmul(ab[:, 0:512], dya, "dw_pa", shards=N_CHIP),
        "w_pb": _tn_matmul(ab[:, 512:1024], dyb, "dw_pb", shards=N_CHIP),
        "w_o": _tn_matmul(merged, dmix, "dw_o"),
        "w_up": _tn_matmul(h2, dpre, "dw_up", shards=N_CHIP),
        "w_down": _tn_matmul(u, dd, "dw_down"),
    }
    if late_shards is None:
        dpm, dp, _ = _mixer_bwd(pm, dab, states, p)
    else:
        partial = _presum(_LATE, [big[n].reshape((N_CHIP,) + _BIG_SHARD[n]) for n in _LATE], "presum_late")
        dpm, dp, parts = _mixer_bwd(pm, dab, states, p, _SendPartials(_LATE), partial)
        big = dict(zip(_LATE, parts))
    dx, dg_pre_mix = _in_proj_bwd(dpm, dgab, x, dx1, sp["g_pre_mix"], w["w_in"])
    big["w_in"] = _dw_in(dpm, dgab, h)
    small = {
        "g_pre_mix": dg_pre_mix, "b_a_up": dp["bau"], "g_gla_norm": dp["ggla"], "conv_b": dp["cb"],
        "w_q_ml": dp["wq"][:, 0:4].reshape(128, 4, 4), "w_k_ml": dp["wk"][:, 0:4].reshape(128, 4, 4),
        "w_v_ml": dp["wv"][:, 0:4].reshape(128, 4, 4),
        "b_if": dp["bif"][:, 0:8], "ml_skip": dp["skip"], "g_ml_norm": dp["gml"], "g_post_mix": dg_post_mix,
        "g_pre_mlp": dg_pre_mlp, "g_post_mlp": dg_post_mlp, "w_a_up": dp["wau"][0:16], "conv_w": dp["cw"],
        "w_if": dp["wif"][:, 0:8], "loss": loss[:, 0:1],
    }
    return dx, big, small


_SMALL_REPL = ("g_pre_mix", "b_a_up", "g_gla_norm", "conv_b", "w_q_ml", "w_k_ml", "w_v_ml", "b_if", "ml_skip",
               "g_ml_norm", "g_post_mix", "g_pre_mlp", "g_post_mlp")
_SMALL_SHARDED = ("w_a_up", "conv_w", "w_if")
_SMALL_ORDER = _SMALL_REPL + _SMALL_SHARDED + ("loss",)
_WEIGHTS = ("g_pre_mix", "w_in", "w_a_up", "b_a_up", "g_gla_norm", "conv_w", "conv_b", "w_q_ml", "w_k_ml", "w_v_ml",
            "w_if", "b_if", "ml_skip", "g_ml_norm", "w_pa", "w_pb", "w_o", "g_post_mix", "g_pre_mlp", "w_up", "w_down",
            "g_post_mlp")


def _as_shard(name, a):
    return jnp.transpose(a, (2, 0, 1)) if name == "w_in" else a[0]


def _from_shard(name, a):
    return jnp.transpose(a, (1, 2, 0)) if name == "w_in" else a[None]


def kernel(x, g_pre_mix, w_in, w_a_up, b_a_up, g_gla_norm, conv_w, conv_b, w_q_ml, w_k_ml, w_v_ml, w_if, b_if, ml_skip, g_ml_norm, w_pa, w_pb, w_o, g_post_mix, g_pre_mlp, w_up, w_down, g_post_mlp, loss_target, m_g_pre_mix, m_w_in, m_w_a_up, m_b_a_up, m_g_gla_norm, m_conv_w, m_conv_b, m_w_q_ml, m_w_k_ml, m_w_v_ml, m_w_if, m_b_if, m_ml_skip, m_g_ml_norm, m_w_pa, m_w_pb, m_w_o, m_g_post_mix, m_g_pre_mlp, m_w_up, m_w_down, m_g_post_mlp, v_g_pre_mix, v_w_in, v_w_a_up, v_b_a_up, v_g_gla_norm, v_conv_w, v_conv_b, v_w_q_ml, v_w_k_ml, v_w_v_ml, v_w_if, v_b_if, v_ml_skip, v_g_ml_norm, v_w_pa, v_w_pb, v_w_o, v_g_post_mix, v_g_pre_mlp, v_w_up, v_w_down, v_g_post_mlp):
    args = dict(locals())
    wts = {n: _as_shard(n, args[n]) for n in _WEIGHTS}
    mom = {n: _as_shard(n, args["m_" + n]) for n in _WEIGHTS}
    var = {n: _as_shard(n, args["v_" + n]) for n in _WEIGHTS}
    chip = 2 * lax.axis_index("x") + lax.axis_index("y")

    first = ("w_in",) + _SMALL_SHARDED
    gathered = dict(zip(first, _run_alone(_Gather(("w_in",), [wts[n] for n in _SMALL_SHARDED]),
                                          [wts[n][:, 0, :].astype(BF16) if n == "w_in" else wts[n] for n in first],
                                          "gather_first")))
    sp = {n: wts[n] for n in _SMALL_REPL}
    sp["w_a_up"] = _cols(gathered["w_a_up"])
    sp["conv_w"] = _cols(gathered["conv_w"])
    sp["w_if"] = gathered["w_if"].reshape(1536, 8)

    dx, big, small = _local_step(x[0], loss_target[0], _full_weights({"w_in": gathered["w_in"]}), sp,
                                 late_shards=[wts[n].astype(BF16) for n in _LATE])

    small_shapes = [small[n].shape for n in _SMALL_ORDER]
    packed = _pack([small[n] for n in _SMALL_ORDER])
    partial = _presum(("w_in",), [big["w_in"].reshape((N_CHIP,) + _BIG_SHARD["w_in"])], "presum_w_in")
    parts_in, small_parts = _run_alone(_SendPartials(("w_in",), packed.shape), [*partial, packed], "send_partials")
    big["w_in"] = parts_in
    sums = _sum_swap(_BIG, [big[n] for n in _BIG])

    grads, delta, new_m, new_v = {}, {}, {}, {}
    for n, g in zip(_BIG, sums):
        if n == "w_in":
            g, d, nm, nv = _adamw_rows(g, wts[n], mom[n], var[n], "adamw_" + n)
        else:
            d, nm, nv = _adamw_big(g, wts[n], mom[n], var[n], "adamw_" + n)
        grads[n], delta[n], new_m[n], new_v[n] = (_from_shard(n, a) for a in (g, d, nm, nv))
    summed = dict(zip(_SMALL_ORDER, _unpack(_sum_small(small_parts), small_shapes)))
    loss = summed["loss"].reshape(())
    for n in _SMALL_REPL:
        grads[n] = summed[n].reshape(args[n].shape)
    grads["w_a_up"] = lax.dynamic_slice_in_dim(summed["w_a_up"], chip * 64, 64, axis=1)[None]
    grads["conv_w"] = lax.dynamic_slice_in_dim(summed["conv_w"], chip * 128, 128, axis=1)[None]
    grads["w_if"] = lax.dynamic_slice_in_dim(summed["w_if"], chip * 384, 384, axis=0)[None]
    small_names = _SMALL_REPL + _SMALL_SHARDED
    upd = _adamw_small([args[n] for n in small_names], [grads[n] for n in small_names],
                       [args["m_" + n] for n in small_names], [args["v_" + n] for n in small_names])
    for dst, arrs in zip((delta, new_m, new_v), upd):
        dst.update(zip(small_names, arrs))

    outs = [loss, dx[None]]
    for group in (grads, delta, new_m, new_v):
        outs += [group[n] for n in _WEIGHTS]
    return tuple(outs)
```

```python
import functools

import jax
import jax.numpy as jnp
from jax import lax
from jax.experimental import pallas as pl
from jax.experimental.pallas import tpu as pltpu

F32 = jnp.float32
BF16 = jnp.bfloat16

SEQ = 2048
D_MODEL = 1024
CHUNK = 64
N_CHUNK = SEQ // CHUNK
HEADS = 4
GLA_DK = 64
GLA_DV = 128
ML_DH = 128
D_FF = 4096
EPS = 1e-6
N_CHIP = 4
N_DEV = 8
TOK_TILE = 256
N_TOK_TILE = SEQ // TOK_TILE
SWEEP = 2
N_SWEEP = N_CHUNK // SWEEP

PM_W = 2688
PM_XM = 1536
PM_OP = 2048
PM_AL = 2560
GAB_W = 2048
D_IN = 4624
IN_SHARD = D_IN // N_CHIP
IN_ALOW = 1536
IN_XM = 1552
IN_GATES = 2576

ADAM_LR = 0.001
ADAM_B1 = 0.9
ADAM_B2 = 0.999
ADAM_EPS = 1e-08
ADAM_WD = 0.01
ADAM_STEP = 10

VMEM_LIMIT = 56 * 1024 * 1024


def _params(sem=None):
    return pltpu.CompilerParams(dimension_semantics=sem, vmem_limit_bytes=VMEM_LIMIT)


def _dot(a, b, ca, cb):
    return lax.dot_general(a.astype(BF16), b.astype(BF16), (((ca,), (cb,)), ((), ())), preferred_element_type=F32)


def _pmm_nn(a, b):
    return _dot(a, b, 1, 0)


def _pmm_nt(a, b):
    return _dot(a, b, 1, 1)


def _pmm_tn(a, b):
    return _dot(a, b, 0, 0)


def _pcmm(c, x):
    return lax.dot_general(c, x, (((1,), (0,)), ((), ())), precision=lax.Precision.HIGHEST, preferred_element_type=F32)


@jax.custom_vjp
def _mm_nn(a, b):
    return _dot(a, b, 1, 0)


@jax.custom_vjp
def _mm_nt(a, b):
    return _dot(a, b, 1, 1)


@jax.custom_vjp
def _mm_tn(a, b):
    return _dot(a, b, 0, 0)


_mm_nn.defvjp(lambda a, b: (_dot(a, b, 1, 0), (a, b)), lambda r, g: (_mm_nt(g, r[1]), _mm_tn(r[0], g)))
_mm_nt.defvjp(lambda a, b: (_dot(a, b, 1, 1), (a, b)), lambda r, g: (_mm_nn(g, r[1]), _mm_tn(g, r[0])))
_mm_tn.defvjp(lambda a, b: (_dot(a, b, 0, 0), (a, b)), lambda r, g: (_mm_nt(r[1], g), _mm_nn(r[0], g)))


@jax.custom_vjp
def _cmm(c, x):
    return _pcmm(c, x)


_cmm.defvjp(
    lambda c, x: (_pcmm(c, x), c),
    lambda c, g: (jnp.zeros_like(c), lax.dot_general(c, g, (((0,), (0,)), ((), ())), precision=lax.Precision.HIGHEST,
                                                      preferred_element_type=F32)),
)

_PLAIN_OPS = (_pmm_nn, _pmm_nt, _pmm_tn, _pcmm)
_VJP_OPS = (_mm_nn, _mm_nt, _mm_tn, _cmm)


def _sigmoid(x):
    return 0.5 * (jnp.tanh(0.5 * x) + 1.0)


def _log_sigmoid(x):
    return jnp.minimum(x, 0.0) - jnp.log(1.0 + jnp.exp(-jnp.abs(x)))


def _mean(x):
    return jnp.mean(x, axis=-1, keepdims=True)


def _nt(a, b):
    return lax.dot_general(a, b, (((1,), (1,)), ((), ())), preferred_element_type=F32)


def _tn(a, b):
    return lax.dot_general(a, b, (((0,), (0,)), ((), ())), preferred_element_type=F32)


def _mixer_chunk(ops, p, st, pm, xprev8):
    mm_nn, mm_nt, mm_tn, cmm = ops
    row = lax.broadcasted_iota(jnp.int32, (CHUNK, CHUNK), 0)
    col = lax.broadcasted_iota(jnp.int32, (CHUNK, CHUNK), 1)
    causal = row >= col
    tri = causal.astype(F32)
    q = pm[:, 0:256]
    k = pm[:, 256:512]
    v = pm[:, 512:1024]
    g = pm[:, 1024:1536]
    xm = pm[:, PM_XM:PM_XM + 512]
    opre = pm[:, PM_OP:PM_OP + 512]
    alow = pm[:, PM_AL:PM_AL + 128]

    la = _log_sigmoid(mm_nn(alow, p["wau"]) + p["bau"]) * (1.0 / 16.0)
    cum = cmm(tri, la)
    cum_last = cum[CHUNK - 1:CHUNK, :]
    e_pos = jnp.exp(cum)
    e_neg = jnp.exp(-cum)
    qs = q * (GLA_DK ** -0.5)
    qp = qs * e_pos
    qn = qs * e_neg
    kp = k * e_pos
    kn = k * e_neg
    kl = k * jnp.exp(cum_last - cum)
    dec = jnp.exp(cum_last)
    hs = range(HEADS)
    s6 = [slice(h * GLA_DK, (h + 1) * GLA_DK) for h in hs]
    s12 = [slice(h * 128, (h + 1) * 128) for h in hs]
    a_fwd = [mm_nt(qp[:, s6[h]], kn[:, s6[h]]) for h in hs]
    a_bwd = [mm_nt(qn[:, s6[h]], kp[:, s6[h]]) for h in hs]
    o_inter = [mm_nt(qp[:, s6[h]], st["S"][h]) for h in hs]
    s_chunk = [mm_tn(v[:, s12[h]], kl[:, s6[h]]) for h in hs]
    scores = [jnp.where(causal, a_fwd[h], a_bwd[h]) for h in hs]
    o = [mm_nn(scores[h], v[:, s12[h]]) + o_inter[h] for h in hs]
    s_new = [st["S"][h] * dec[:, s6[h]] + s_chunk[h] for h in hs]
    o = [o[h] * lax.rsqrt(_mean(o[h] * o[h]) + EPS) * p["ggla"] for h in hs]
    outs = [o[h] * (g[:, s12[h]] * _sigmoid(g[:, s12[h]])) for h in hs]

    xx = jnp.concatenate([xprev8, xm], axis=0)
    pre = p["cb"]
    for j in range(4):
        pre = pre + p["cw"][j:j + 1, :] * xx[5 + j:5 + j + CHUNK, :]
    xc = pre * _sigmoid(pre)
    qm = [mm_nn(xc[:, s12[h]], p["wq"][h]) for h in hs]
    km = [mm_nn(xc[:, s12[h]], p["wk"][h]) for h in hs]
    vm = [mm_nn(xm[:, s12[h]], p["wv"][h]) for h in hs]
    qcat = jnp.concatenate(qm, axis=1)
    kcat = jnp.concatenate(km, axis=1)
    vcat = jnp.concatenate(vm, axis=1)
    gates = (mm_nn(qcat, p["wif"][0:512]) + mm_nn(kcat, p["wif"][512:1024]) + mm_nn(vcat, p["wif"][1024:1536])
             + p["bif"])
    lf = _log_sigmoid(gates)
    fc = cmm(tri, lf)
    gates_t = gates.T
    fc_t = fc.T
    ks = [km[h] * (ML_DH ** -0.5) for h in hs]
    qk = [mm_nt(qm[h], ks[h]) for h in hs]
    q_c = [mm_nn(qm[h], st["C"][h]) for h in hs]
    li_c = [gates[:, h:h + 1] for h in hs]
    fc_c = [fc[:, 4 + h:5 + h] for h in hs]
    f_last = [fc[CHUNK - 1:CHUNK, 4 + h:5 + h] for h in hs]
    m_prev = [st["m"][h][:, 0:1] for h in hs]
    a = [f_last[h] - fc_c[h] + li_c[h] for h in hs]
    m_loc = [jnp.max(a[h], axis=0, keepdims=True) for h in hs]
    kw = [ks[h] * jnp.exp(a[h] - m_loc[h]) for h in hs]
    c_chunk = [mm_tn(kw[h], vm[h]) for h in hs]
    log_d = [gates_t[h:h + 1, :] - jnp.abs(fc_c[h] - fc_t[4 + h:5 + h, :]) for h in hs]
    g_int = [fc_c[h] + m_prev[h] for h in hs]
    m_t = [jnp.maximum(g_int[h], jnp.max(log_d[h], axis=1, keepdims=True)) for h in hs]
    s = [qk[h] * jnp.exp(log_d[h] - m_t[h]) for h in hs]
    scl = [jnp.exp(g_int[h] - m_t[h]) for h in hs]
    num = [mm_nn(s[h], vm[h]) + scl[h] * q_c[h] for h in hs]
    den = [jnp.sum(s[h], axis=1, keepdims=True) + scl[h] * jnp.sum(qm[h] * st["n"][h], axis=1, keepdims=True) for h in hs]
    den = [jnp.maximum(jnp.abs(den[h]), jnp.exp(-m_t[h])) for h in hs]
    hc = [num[h] / den[h] * _sigmoid(opre[:, s12[h]]) for h in hs]
    d0 = [hc[h] - _mean(hc[h]) for h in hs]
    y = [d0[h] * lax.rsqrt(_mean(d0[h] * d0[h]) + EPS) for h in hs]
    outs += [y[h] * p["gml"][:, s12[h]] + p["skip"][:, s12[h]] * xc[:, s12[h]] for h in hs]
    m_nx = [jnp.maximum(f_last[h] + m_prev[h], m_loc[h]) for h in hs]
    sp = [jnp.exp(f_last[h] + m_prev[h] - m_nx[h]) for h in hs]
    sl = [jnp.exp(m_loc[h] - m_nx[h]) for h in hs]
    c_new = [sp[h] * st["C"][h] + sl[h] * c_chunk[h] for h in hs]
    n_new = [sp[h] * st["n"][h] + sl[h] * jnp.sum(kw[h], axis=0, keepdims=True) for h in hs]
    m_new = [jnp.broadcast_to(m_nx[h], (1, ML_DH)) for h in hs]
    ab = jnp.concatenate(outs, axis=1)
    new = {"S": s_new, "C": c_new, "n": n_new, "m": m_new}
    return ab, new


_P_NAMES = ("wau", "bau", "ggla", "cw", "cb", "wq", "wk", "wv", "wif", "bif", "skip", "gml")
_P_SHAPES = {
    "wau": (128, 256), "bau": (1, 256), "ggla": (1, 128), "cw": (4, 512), "cb": (1, 512),
    "wq": (512, 128), "wk": (512, 128), "wv": (512, 128),
    "wif": (1536, 128), "bif": (1, 128), "skip": (1, 512), "gml": (1, 512),
}
_P_BLOCKDIAG = ("wq", "wk", "wv")
_S_NAMES = ("S", "C", "n", "m")
_S_SHAPES = {"S": (HEADS, GLA_DV, GLA_DK), "C": (HEADS, ML_DH, ML_DH), "n": (HEADS, 1, ML_DH), "m": (HEADS, 1, ML_DH)}


def _per_head(ref):
    return [ref[h] for h in range(HEADS)]


def _block_mask():
    r = lax.broadcasted_iota(jnp.int32, (128, 128), 0)
    c = lax.broadcasted_iota(jnp.int32, (128, 128), 1)
    same_block = (r >> 2) == (c >> 2)
    spread = jnp.logical_and(r < 4, (c & 3) == r)
    return same_block.astype(F32), spread.astype(F32)


def _expand_blockdiag(w_ref, dense_ref):
    same_block, spread = _block_mask()
    for h in range(HEADS):
        tiled = _pmm_nn(w_ref[h * 128:(h + 1) * 128, :], spread)
        dense_ref[h] = tiled * same_block


def _collect_blockdiag(ddense_ref, dw_ref):
    same_block, spread = _block_mask()
    for h in range(HEADS):
        dw_ref[h * 128:(h + 1) * 128, :] = lax.dot_general(
            ddense_ref[h] * same_block, spread, (((1,), (1,)), ((), ())), precision=lax.Precision.HIGHEST,
            preferred_element_type=F32)


def _const_spec(shape):
    zeros = (0,) * len(shape)
    return pl.BlockSpec(shape, lambda i: zeros)


def _split(refs, *counts):
    out, at = [], 0
    for c in counts:
        out.append(refs[at:at + c])
        at += c
    assert at == len(refs)
    return out


def _ride(rider, phases, cond, ins, outs, sems):
    if rider is None:
        return
    lands, (send_sems, recv_sems, flush_sems) = sems[:-3], sems[-3:]

    @pl.when(cond)
    def _():
        for phase in phases:
            getattr(rider, phase)(ins, lands, send_sems, recv_sems)
        if "last" in phases:
            flush = [pltpu.make_async_copy(lands[k], outs[k], flush_sems.at[k]) for k in range(len(outs))]
            for cp in flush:
                cp.start()
            for cp in flush:
                cp.wait()


def _rider_specs(rider, rider_ins):
    if rider is None:
        return [], [], [], []
    scratch = [pltpu.VMEM(s.shape, s.dtype) for s in rider.out_shape]
    scratch += [pltpu.SemaphoreType.DMA((rider.n_sems,)), pltpu.SemaphoreType.DMA((rider.n_sems,)),
                pltpu.SemaphoreType.DMA((len(rider.out_shape),))]
    return [VMEM_WHOLE] * len(rider_ins), [ANY] * len(rider.out_shape), list(rider.out_shape), scratch


def _mixer_fwd(pm, p, rider=None, rider_ins=()):
    n_p = len(_P_NAMES)
    r_in, r_out_specs, r_out_shape, r_sems = _rider_specs(rider, rider_ins)

    def body(*refs):
        (pm_ref, xprev_ref), p_list, ride_in, (ab_ref,), so_refs, ride_out, sc_refs, dense_list, sems = _split(
            refs, 2, n_p, len(r_in), 1, 4, len(r_out_specs), 4, 3, len(r_sems))
        p_refs = dict(zip(_P_NAMES, p_list))
        dense = dict(zip(_P_BLOCKDIAG, dense_list))
        n = pl.program_id(0)
        _ride(rider, ("first",), n == 0, ride_in, ride_out, sems)

        @pl.when(n == 0)
        def _():
            for r in sc_refs:
                r[...] = jnp.zeros_like(r)
            for nm in _P_BLOCKDIAG:
                _expand_blockdiag(p_refs[nm], dense[nm])

        st = {name: _per_head(r) for name, r in zip(_S_NAMES, sc_refs)}
        pv = {nm: (_per_head(dense[nm]) if nm in _P_BLOCKDIAG else p_refs[nm][...]) for nm in _P_NAMES}
        for g in range(SWEEP):
            rows = slice(g * CHUNK, (g + 1) * CHUNK)
            for name, r in zip(_S_NAMES, so_refs):
                for h in range(HEADS):
                    r[g, h] = st[name][h]
            if g == 0:
                xprev8 = jnp.where(n > 0, xprev_ref[CHUNK - 8:CHUNK, :], 0.0)
            else:
                xprev8 = pm_ref[g * CHUNK - 8:g * CHUNK, PM_XM:PM_XM + 512]
            ab, st = _mixer_chunk(_PLAIN_OPS, pv, st, pm_ref[rows, :], xprev8)
            ab_ref[rows, :] = ab.astype(BF16)
        for name, r in zip(_S_NAMES, sc_refs):
            for h in range(HEADS):
                r[h] = st[name][h]
        _ride(rider, ("middle",), n == N_SWEEP - 2, ride_in, ride_out, sems)
        _ride(rider, ("last",), n == N_SWEEP - 1, ride_in, ride_out, sems)

    in_specs = [pl.BlockSpec((SWEEP * CHUNK, PM_W), lambda i: (i, 0)),
                pl.BlockSpec((CHUNK, 512), lambda i: (jnp.maximum(SWEEP * i - 1, 0), PM_XM // 512))]
    in_specs += [_const_spec(_P_SHAPES[nm]) for nm in _P_NAMES] + r_in
    out_specs = [pl.BlockSpec((SWEEP * CHUNK, 1024), lambda i: (i, 0))]
    out_shape = [jax.ShapeDtypeStruct((SEQ, 1024), BF16)]
    for nm in _S_NAMES:
        shp = _S_SHAPES[nm]
        out_specs.append(pl.BlockSpec((SWEEP,) + shp, lambda i: (i, 0, 0, 0)))
        out_shape.append(jax.ShapeDtypeStruct((N_CHUNK,) + shp, F32))
    return pl.pallas_call(
        body, grid=(N_SWEEP,), in_specs=in_specs, out_specs=out_specs + r_out_specs, out_shape=out_shape + r_out_shape,
        scratch_shapes=[pltpu.VMEM(_S_SHAPES[nm], F32) for nm in _S_NAMES]
        + [pltpu.VMEM((HEADS, 128, 128), F32) for _ in _P_BLOCKDIAG] + r_sems,
        compiler_params=_params(("arbitrary",)), name="mixer_fwd",
    )(pm, pm, *[p[nm] for nm in _P_NAMES], *rider_ins)


def _mixer_bwd(pm, dab, states, p, rider=None, rider_ins=()):
    n_p = len(_P_NAMES)
    r_in, r_out_specs, r_out_shape, r_sems = _rider_specs(rider, rider_ins)

    def body(*refs):
        ((pm_ref, xprev_ref, dab_ref), si_refs, p_list, ride_in, (dpm_ref,), dp_list, ride_out, ds_refs, (carry_ref,),
         dense_list, ddense_list, sems) = _split(refs, 3, 4, n_p, len(r_in), 1, n_p, len(r_out_specs), 4, 1, 3, 3, len(r_sems))
        p_refs = dict(zip(_P_NAMES, p_list))
        dp_refs = dict(zip(_P_NAMES, dp_list))
        dense = dict(zip(_P_BLOCKDIAG, dense_list))
        ddense = dict(zip(_P_BLOCKDIAG, ddense_list))
        i = pl.program_id(0)
        blk = N_SWEEP - 1 - i
        _ride(rider, ("first",), i == 0, ride_in, ride_out, sems)

        @pl.when(i == 0)
        def _():
            for r in ds_refs:
                r[...] = jnp.zeros_like(r)
            for nm in _P_NAMES:
                if nm in _P_BLOCKDIAG:
                    ddense[nm][...] = jnp.zeros_like(ddense[nm])
                    _expand_blockdiag(p_refs[nm], dense[nm])
                else:
                    dp_refs[nm][...] = jnp.zeros_like(dp_refs[nm])
            carry_ref[...] = jnp.zeros_like(carry_ref)

        pv = {nm: (_per_head(dense[nm]) if nm in _P_BLOCKDIAG else p_refs[nm][...]) for nm in _P_NAMES}
        dst = {name: _per_head(r) for name, r in zip(_S_NAMES, ds_refs)}
        carry = carry_ref[...]
        dp_sum = None
        for g in reversed(range(SWEEP)):
            rows = slice(g * CHUNK, (g + 1) * CHUNK)
            st = {name: [r[g, h] for h in range(HEADS)] for name, r in zip(_S_NAMES, si_refs)}
            if g == 0:
                xprev8 = jnp.where(blk > 0, xprev_ref[CHUNK - 8:CHUNK, :], 0.0)
            else:
                xprev8 = pm_ref[g * CHUNK - 8:g * CHUNK, PM_XM:PM_XM + 512]
            _, vjp = jax.vjp(functools.partial(_mixer_chunk, _VJP_OPS), pv, st, pm_ref[rows, :], xprev8)
            dp, dst, dpm, dxprev8 = vjp((dab_ref[rows, :], dst))
            reach = jnp.concatenate([jnp.zeros((CHUNK - 8, 512), F32), carry], axis=0)
            dpm_ref[rows, 0:PM_XM] = dpm[:, 0:PM_XM].astype(BF16)
            dpm_ref[rows, PM_XM:PM_XM + 512] = (dpm[:, PM_XM:PM_XM + 512] + reach).astype(BF16)
            dpm_ref[rows, PM_XM + 512:PM_W] = dpm[:, PM_XM + 512:PM_W].astype(BF16)
            carry = dxprev8
            dp_sum = dp if dp_sum is None else jax.tree.map(jnp.add, dp_sum, dp)
        carry_ref[...] = carry
        for name, r in zip(_S_NAMES, ds_refs):
            for h in range(HEADS):
                r[h] = dst[name][h]
        for nm in _P_NAMES:
            if nm in _P_BLOCKDIAG:
                for h in range(HEADS):
                    ddense[nm][h] += dp_sum[nm][h]
            else:
                dp_refs[nm][...] += dp_sum[nm]

        @pl.when(i == N_SWEEP - 1)
        def _():
            for nm in _P_BLOCKDIAG:
                _collect_blockdiag(ddense[nm], dp_refs[nm])

        _ride(rider, ("middle",), i == N_SWEEP - 2, ride_in, ride_out, sems)
        _ride(rider, ("last",), i == N_SWEEP - 1, ride_in, ride_out, sems)

    rev = lambda i: (N_SWEEP - 1 - i, 0)
    in_specs = [pl.BlockSpec((SWEEP * CHUNK, PM_W), rev),
                pl.BlockSpec((CHUNK, 512), lambda i: (jnp.maximum(SWEEP * (N_SWEEP - 1 - i) - 1, 0), PM_XM // 512)),
                pl.BlockSpec((SWEEP * CHUNK, 1024), rev)]
    for nm in _S_NAMES:
        in_specs.append(pl.BlockSpec((SWEEP,) + _S_SHAPES[nm], lambda i: (N_SWEEP - 1 - i, 0, 0, 0)))
    in_specs += [_const_spec(_P_SHAPES[nm]) for nm in _P_NAMES] + r_in
    out_specs = [pl.BlockSpec((SWEEP * CHUNK, PM_W), rev)] + [_const_spec(_P_SHAPES[nm]) for nm in _P_NAMES]
    out_shape = [jax.ShapeDtypeStruct((SEQ, PM_W), BF16)] + [jax.ShapeDtypeStruct(_P_SHAPES[nm], F32) for nm in _P_NAMES]
    res = pl.pallas_call(
        body, grid=(N_SWEEP,), in_specs=in_specs, out_specs=out_specs + r_out_specs, out_shape=out_shape + r_out_shape,
        scratch_shapes=[pltpu.VMEM(_S_SHAPES[nm], F32) for nm in _S_NAMES] + [pltpu.VMEM((8, 512), F32)]
        + [pltpu.VMEM((HEADS, 128, 128), F32) for _ in range(2 * len(_P_BLOCKDIAG))] + r_sems,
        compiler_params=_params(("arbitrary",)), name="mixer_bwd",
    )(pm, pm, dab, *states, *[p[nm] for nm in _P_NAMES], *rider_ins)
    return res[0], dict(zip(_P_NAMES, res[1:1 + n_p])), res[1 + n_p:]


def _tok(width):
    return pl.BlockSpec((TOK_TILE, width), lambda i: (i, 0))


def _once(shape):
    zeros = (0,) * len(shape)
    return pl.BlockSpec(shape, lambda i: zeros, pipeline_mode=pl.Buffered(1))


def _rms_fwd(x):
    r = lax.rsqrt(_mean(x * x) + EPS)
    return x * r, r


def _rms_bwd(dy, xn, r, g):
    gd = dy * g
    return r * (gd - xn * _mean(xn * gd))


def _in_proj(x, g_pre, wt_in):
    def body(x_ref, g_ref, wt_ref, pm_ref, gab_ref, h_ref):
        xn, _ = _rms_fwd(x_ref[...])
        h = (xn * g_ref[...]).astype(BF16)
        h_ref[...] = h
        pm_ref[:, 0:PM_XM] = _nt(h, wt_ref[0:IN_ALOW, :])
        pm_ref[:, PM_XM:PM_AL] = _nt(h, wt_ref[IN_XM:IN_GATES, :])
        pm_ref[:, PM_AL:PM_W] = _nt(h, wt_ref[IN_ALOW:IN_ALOW + 128, :])
        gab_ref[...] = _nt(h, wt_ref[IN_GATES:D_IN, :])

    return pl.pallas_call(
        body, grid=(N_TOK_TILE,),
        in_specs=[_tok(D_MODEL), _once((1, D_MODEL)), _once((D_IN, D_MODEL))],
        out_specs=[_tok(PM_W), _tok(GAB_W), _tok(D_MODEL)],
        out_shape=[jax.ShapeDtypeStruct((SEQ, PM_W), F32), jax.ShapeDtypeStruct((SEQ, GAB_W), F32),
                   jax.ShapeDtypeStruct((SEQ, D_MODEL), BF16)],
        compiler_params=_params(("arbitrary",)), name="in_proj",
    )(x, g_pre, wt_in)


def _merge_fwd(ab, gab, x, w_pa4, w_pb4, w_o, g_post):
    def body(ab_ref, gab_ref, x_ref, wpa_ref, wpb_ref, wo_ref, g_ref, x1_ref, mix_ref, mg_ref):
        a = ab_ref[:, 0:512]
        b = ab_ref[:, 512:1024]
        for j in range(N_CHIP):
            blk = slice(j * 256, (j + 1) * 256)
            ya = jnp.dot(a, wpa_ref[j], preferred_element_type=F32)
            yb = jnp.dot(b, wpb_ref[j], preferred_element_type=F32)
            sa = _sigmoid(gab_ref[:, j * 256:(j + 1) * 256])
            sb = _sigmoid(gab_ref[:, 1024 + j * 256:1024 + (j + 1) * 256])
            mg_ref[:, blk] = (sa * ya + sb * yb).astype(BF16)
        mix = jnp.dot(mg_ref[...], wo_ref[...], preferred_element_type=F32)
        mix_ref[...] = mix
        mn, _ = _rms_fwd(mix)
        x1_ref[...] = x_ref[...] + mn * g_ref[...]

    return pl.pallas_call(
        body, grid=(N_TOK_TILE,),
        in_specs=[_tok(1024), _tok(GAB_W), _tok(D_MODEL), _once((N_CHIP, 512, 256)), _once((N_CHIP, 512, 256)),
                  _once((D_MODEL, D_MODEL)), _once((1, D_MODEL))],
        out_specs=[_tok(D_MODEL), _tok(D_MODEL), _tok(D_MODEL)],
        out_shape=[jax.ShapeDtypeStruct((SEQ, D_MODEL), F32), jax.ShapeDtypeStruct((SEQ, D_MODEL), F32),
                   jax.ShapeDtypeStruct((SEQ, D_MODEL), BF16)],
        compiler_params=_params(("arbitrary",)), name="merge_fwd",
    )(ab, gab, x, w_pa4, w_pb4, w_o, g_post)


def _mlp(x1, target, g_pre, g_post, w_up4, w_down):
    def body(x1_ref, t_ref, gpre_ref, gpost_ref, wup_ref, wdn_ref,
             dx1_ref, u_ref, dd_ref, h2_ref, dpre_ref, dgpost_ref, dgpre_ref, loss_ref):
        @pl.when(pl.program_id(0) == 0)
        def _():
            dgpost_ref[...] = jnp.zeros_like(dgpost_ref)
            dgpre_ref[...] = jnp.zeros_like(dgpre_ref)
            loss_ref[...] = jnp.zeros_like(loss_ref)

        x1 = x1_ref[...]
        gpre = gpre_ref[...]
        gpost = gpost_ref[...]
        xn2, r2 = _rms_fwd(x1)
        h2 = (xn2 * gpre).astype(BF16)
        h2_ref[...] = h2
        rl = []
        d = jnp.zeros((TOK_TILE, D_MODEL), F32)
        for j in range(N_CHIP):
            blk = slice(j * 1024, (j + 1) * 1024)
            r = jnp.maximum(jnp.dot(h2, wup_ref[j], preferred_element_type=F32), 0.0)
            rl.append(r)
            u = (r * r).astype(BF16)
            u_ref[:, blk] = u
            d = d + jnp.dot(u, wdn_ref[blk, :], preferred_element_type=F32)
        dn, r3 = _rms_fwd(d)
        diff = x1 + dn * gpost - t_ref[...]
        loss_ref[...] += jnp.sum(diff * diff, keepdims=True) * (0.5 / D_MODEL)
        dy = diff * (1.0 / D_MODEL)
        dgpost_ref[...] += jnp.sum(dy * dn, axis=0, keepdims=True)
        dd = _rms_bwd(dy, dn, r3, gpost).astype(BF16)
        dd_ref[...] = dd
        dh2 = jnp.zeros((TOK_TILE, D_MODEL), F32)
        for j in range(N_CHIP):
            blk = slice(j * 1024, (j + 1) * 1024)
            dpre = (_nt(dd, wdn_ref[blk, :]) * (2.0 * rl[j])).astype(BF16)
            dpre_ref[:, blk] = dpre
            dh2 = dh2 + _nt(dpre, wup_ref[j])
        dgpre_ref[...] += jnp.sum(dh2 * xn2, axis=0, keepdims=True)
        dx1_ref[...] = dy + _rms_bwd(dh2, xn2, r2, gpre)

    acc = pl.BlockSpec((1, D_MODEL), lambda i: (0, 0))
    return pl.pallas_call(
        body, grid=(N_TOK_TILE,),
        in_specs=[_tok(D_MODEL), _tok(D_MODEL), _once((1, D_MODEL)), _once((1, D_MODEL)),
                  _once((N_CHIP, D_MODEL, 1024)), _once((D_FF, D_MODEL))],
        out_specs=[_tok(D_MODEL), _tok(D_FF), _tok(D_MODEL), _tok(D_MODEL), _tok(D_FF), acc, acc,
                   pl.BlockSpec((1, 128), lambda i: (0, 0))],
        out_shape=[jax.ShapeDtypeStruct((SEQ, D_MODEL), F32), jax.ShapeDtypeStruct((SEQ, D_FF), BF16),
                   jax.ShapeDtypeStruct((SEQ, D_MODEL), BF16), jax.ShapeDtypeStruct((SEQ, D_MODEL), BF16),
                   jax.ShapeDtypeStruct((SEQ, D_FF), BF16), jax.ShapeDtypeStruct((1, D_MODEL), F32),
                   jax.ShapeDtypeStruct((1, D_MODEL), F32), jax.ShapeDtypeStruct((1, 128), F32)],
        compiler_params=_params(("arbitrary",)), name="mlp_fwd_bwd",
    )(x1, target, g_pre, g_post, w_up4, w_down)


def _merge_bwd(dx1, mix, ab, gab, w_pa4, w_pb4, w_o, g_post):
    def body(dx1_ref, mix_ref, ab_ref, gab_ref, wpa_ref, wpb_ref, wo_ref, g_ref,
             dmix_ref, dya_ref, dyb_ref, dgab_ref, dab_ref, dg_ref):
        @pl.when(pl.program_id(0) == 0)
        def _():
            dg_ref[...] = jnp.zeros_like(dg_ref)

        dx1 = dx1_ref[...]
        mn, r = _rms_fwd(mix_ref[...])
        dg_ref[...] += jnp.sum(dx1 * mn, axis=0, keepdims=True)
        dmix = _rms_bwd(dx1, mn, r, g_ref[...]).astype(BF16)
        dmix_ref[...] = dmix
        dmerged = _nt(dmix, wo_ref[...])
        a = ab_ref[:, 0:512]
        b = ab_ref[:, 512:1024]
        da = jnp.zeros((TOK_TILE, 512), F32)
        db = jnp.zeros((TOK_TILE, 512), F32)
        for j in range(N_CHIP):
            blk = slice(j * 256, (j + 1) * 256)
            blk_b = slice(1024 + j * 256, 1024 + (j + 1) * 256)
            dm = dmerged[:, blk]
            ya = jnp.dot(a, wpa_ref[j], preferred_element_type=F32)
            yb = jnp.dot(b, wpb_ref[j], preferred_element_type=F32)
            sa = _sigmoid(gab_ref[:, blk])
            sb = _sigmoid(gab_ref[:, blk_b])
            dya = (dm * sa).astype(BF16)
            dyb = (dm * sb).astype(BF16)
            dya_ref[:, blk] = dya
            dyb_ref[:, blk] = dyb
            dgab_ref[:, blk] = (dm * ya * sa * (1.0 - sa)).astype(BF16)
            dgab_ref[:, blk_b] = (dm * yb * sb * (1.0 - sb)).astype(BF16)
            da = da + _nt(dya, wpa_ref[j])
            db = db + _nt(dyb, wpb_ref[j])
        dab_ref[:, 0:512] = da
        dab_ref[:, 512:1024] = db

    return pl.pallas_call(
        body, grid=(N_TOK_TILE,),
        in_specs=[_tok(D_MODEL), _tok(D_MODEL), _tok(1024), _tok(GAB_W), _once((N_CHIP, 512, 256)),
                  _once((N_CHIP, 512, 256)), _once((D_MODEL, D_MODEL)), _once((1, D_MODEL))],
        out_specs=[_tok(D_MODEL), _tok(D_MODEL), _tok(D_MODEL), _tok(GAB_W), _tok(1024),
                   pl.BlockSpec((1, D_MODEL), lambda i: (0, 0))],
        out_shape=[jax.ShapeDtypeStruct((SEQ, D_MODEL), BF16), jax.ShapeDtypeStruct((SEQ, D_MODEL), BF16),
                   jax.ShapeDtypeStruct((SEQ, D_MODEL), BF16), jax.ShapeDtypeStruct((SEQ, GAB_W), BF16),
                   jax.ShapeDtypeStruct((SEQ, 1024), F32), jax.ShapeDtypeStruct((1, D_MODEL), F32)],
        compiler_params=_params(("arbitrary",)), name="merge_bwd",
    )(dx1, mix, ab, gab, w_pa4, w_pb4, w_o, g_post)


def _in_proj_bwd(dpm, dgab, x, dx1, g_pre, wt_in):
    def body(dpm_ref, dgab_ref, x_ref, dx1_ref, g_ref, wt_ref, dx_ref, dg_ref):
        @pl.when(pl.program_id(0) == 0)
        def _():
            dg_ref[...] = jnp.zeros_like(dg_ref)

        dh = jnp.dot(dpm_ref[:, 0:PM_XM], wt_ref[0:IN_ALOW, :], preferred_element_type=F32)
        dh = dh + jnp.dot(dpm_ref[:, PM_XM:PM_AL], wt_ref[IN_XM:IN_GATES, :], preferred_element_type=F32)
        dh = dh + jnp.dot(dpm_ref[:, PM_AL:PM_W], wt_ref[IN_ALOW:IN_ALOW + 128, :], preferred_element_type=F32)
        dh = dh + jnp.dot(dgab_ref[...], wt_ref[IN_GATES:D_IN, :], preferred_element_type=F32)
        xn, r = _rms_fwd(x_ref[...])
        dg_ref[...] += jnp.sum(dh * xn, axis=0, keepdims=True)
        dx_ref[...] = dx1_ref[...] + _rms_bwd(dh, xn, r, g_ref[...])

    return pl.pallas_call(
        body, grid=(N_TOK_TILE,),
        in_specs=[_tok(PM_W), _tok(GAB_W), _tok(D_MODEL), _tok(D_MODEL), _once((1, D_MODEL)), _once((D_IN, D_MODEL))],
        out_specs=[_tok(D_MODEL), pl.BlockSpec((1, D_MODEL), lambda i: (0, 0))],
        out_shape=[jax.ShapeDtypeStruct((SEQ, D_MODEL), F32), jax.ShapeDtypeStruct((1, D_MODEL), F32)],
        compiler_params=_params(("arbitrary",)), name="in_proj_bwd",
    )(dpm, dgab, x, dx1, g_pre, wt_in)


def _dw_in(dpm, dgab, h):
    n_pm = PM_AL // 512
    n_blk = n_pm + GAB_W // 512

    def body(dpm_ref, dgab_ref, dal_ref, h_ref, o_ref):
        i = pl.program_id(0)
        off = pl.multiple_of(i * 512 + 16 * (i >= 3).astype(jnp.int32), 16)

        @pl.when(i < n_pm)
        def _():
            o_ref[pl.ds(off, 512), :] = _tn(dpm_ref[...], h_ref[...]).astype(BF16)

        @pl.when(i >= n_pm)
        def _():
            o_ref[pl.ds(off, 512), :] = _tn(dgab_ref[...], h_ref[...]).astype(BF16)

        @pl.when(i == 0)
        def _():
            o_ref[IN_ALOW:IN_XM, :] = _tn(dal_ref[...], h_ref[...])[0:IN_XM - IN_ALOW].astype(BF16)

    return pl.pallas_call(
        body, grid=(n_blk,),
        in_specs=[pl.BlockSpec((SEQ, 512), lambda i: (0, jnp.minimum(i, n_pm - 1))),
                  pl.BlockSpec((SEQ, 512), lambda i: (0, jnp.maximum(i - n_pm, 0))),
                  pl.BlockSpec((SEQ, 128), lambda i: (0, PM_AL // 128)),
                  _once((SEQ, D_MODEL))],
        out_specs=pl.BlockSpec((D_IN, D_MODEL), lambda i: (0, 0)),
        out_shape=jax.ShapeDtypeStruct((D_IN, D_MODEL), BF16),
        compiler_params=_params(("arbitrary",)), name="dw_in",
    )(dpm, dgab, dpm, h)


def _tn_matmul(a, b, name, shards=1, tm=512):
    m, n = a.shape[1], b.shape[1]
    tm = min(tm, m)
    tn = n // shards if shards > 1 else min(n, 1024)

    def body(a_ref, b_ref, o_ref):
        o_ref[...] = _tn(a_ref[...], b_ref[...]).astype(BF16)

    if shards > 1:
        out_spec = pl.BlockSpec((None, tm, tn), lambda i, j: (j, i, 0))
        out_shape = jax.ShapeDtypeStruct((shards, m, tn), BF16)
    else:
        out_spec = pl.BlockSpec((tm, tn), lambda i, j: (i, j))
        out_shape = jax.ShapeDtypeStruct((m, n), BF16)
    return pl.pallas_call(
        body, grid=(m // tm, n // tn),
        in_specs=[pl.BlockSpec((SEQ, tm), lambda i, j: (0, i)), pl.BlockSpec((SEQ, tn), lambda i, j: (0, j))],
        out_specs=out_spec, out_shape=out_shape,
        compiler_params=_params(("arbitrary", "arbitrary")), name=name,
    )(a, b)


MESH = pl.DeviceIdType.MESH
ANY = pl.BlockSpec(memory_space=pl.ANY)
VMEM_WHOLE = pl.BlockSpec(memory_space=pltpu.VMEM)

_BIG = ("w_in", "w_pa", "w_pb", "w_o", "w_up", "w_down")
_BIG_SHARD = {"w_in": (IN_SHARD, D_MODEL), "w_pa": (512, 256), "w_pb": (512, 256), "w_o": (256, D_MODEL),
              "w_up": (D_MODEL, 1024), "w_down": (1024, D_MODEL)}
_BIG_SPLIT = {"w_in": 1, "w_pa": 0, "w_pb": 0, "w_o": 0, "w_up": 0, "w_down": 0}


def _half(ref, e, name, lead=0):
    axis = _BIG_SPLIT[name]
    size = _BIG_SHARD[name][axis] // 2
    start = pl.multiple_of(e * size, 128 if axis == 1 else 16)
    idx = [pl.ds(0, ref.shape[a]) for a in range(lead)]
    idx += [pl.ds(start, size), pl.ds(0, _BIG_SHARD[name][1])] if axis == 0 else [pl.ds(0, _BIG_SHARD[name][0]), pl.ds(start, size)]
    return ref.at[tuple(idx)]


def _half_shape(name):
    r, c = _BIG_SHARD[name]
    return (r // 2, c) if _BIG_SPLIT[name] == 0 else (r, c // 2)


def _remote(src, dst, send_sems, recv_sems, k, to):
    return pltpu.make_async_remote_copy(src_ref=src, dst_ref=dst, send_sem=send_sems.at[k], recv_sem=recv_sems.at[k],
                                        device_id=to, device_id_type=MESH)


def _mesh_place():
    x, y, c = lax.axis_index("x"), lax.axis_index("y"), lax.axis_index("c")
    return x, y, c, [(1 - x, y), (x, 1 - y), (1 - x, 1 - y)]


class _Gather:
    def __init__(self, names, small=()):
        self.names = tuple(names)
        self.nb = len(self.names)
        self.n = self.nb + len(small)
        self.n_sems = 6 * self.n
        self.out_shape = [jax.ShapeDtypeStruct((N_CHIP,) + _BIG_SHARD[nm], BF16) for nm in self.names]
        self.out_shape += [jax.ShapeDtypeStruct((N_CHIP,) + s.shape, s.dtype) for s in small]

    def _ici(self, ins, outs, ss, rs, k, j, peer, slot, c):
        if k < self.nb:
            return _remote(_half(ins[k], c, self.names[k]), _half(outs[k].at[slot], c, self.names[k]), ss, rs, 6 * k + j,
                           (*peer, c))
        return _remote(ins[k], outs[k].at[slot], ss, rs, 6 * k + j, (*peer, c))

    def _passed(self, outs, ss, rs, k, j, slot, e, sibling):
        part = _half(outs[k].at[slot], e, self.names[k])
        return _remote(part, part, ss, rs, 6 * k + 3 + j, sibling)

    def first(self, ins, outs, ss, rs):
        x, y, c, peers = _mesh_place()
        me = 2 * x + y
        for k in range(self.n):
            for j, peer in enumerate(peers):
                self._ici(ins, outs, ss, rs, k, j, peer, me, c).start()
        for k in range(self.n):
            outs[k][me] = ins[k][...]

    def middle(self, ins, outs, ss, rs):
        x, y, c, peers = _mesh_place()
        for j, (px, py) in enumerate(peers):
            for k in range(self.nb):
                self._ici(ins, outs, ss, rs, k, j, (px, py), 2 * px + py, c).wait_recv()
                self._passed(outs, ss, rs, k, j, 2 * px + py, c, (x, y, 1 - c)).start()

    def last(self, ins, outs, ss, rs):
        x, y, c, peers = _mesh_place()
        for j, (px, py) in enumerate(peers):
            for k in range(self.n):
                if k < self.nb:
                    self._passed(outs, ss, rs, k, j, 2 * px + py, 1 - c, (x, y, 1 - c)).wait_recv()
                    self._passed(outs, ss, rs, k, j, 2 * px + py, c, (x, y, 1 - c)).wait_send()
                else:
                    self._ici(ins, outs, ss, rs, k, j, (px, py), 2 * px + py, c).wait_recv()
                self._ici(ins, outs, ss, rs, k, j, (px, py), 2 * x + y, c).wait_send()


def _run_alone(rider, ins, name):
    def body(*refs):
        r_in, r_out, sems = _split(refs, len(ins), len(rider.out_shape), 2)
        rider.first(r_in, r_out, *sems)
        rider.middle(r_in, r_out, *sems)
        rider.last(r_in, r_out, *sems)

    return pl.pallas_call(
        body, in_specs=[VMEM_WHOLE] * len(ins), out_specs=[VMEM_WHOLE] * len(rider.out_shape), out_shape=rider.out_shape,
        scratch_shapes=[pltpu.SemaphoreType.DMA((rider.n_sems,)), pltpu.SemaphoreType.DMA((rider.n_sems,))],
        compiler_params=_params(), name=name,
    )(*ins)


def _presum(names, grads, name):
    n = len(grads)

    def body(*refs):
        g_refs, got_refs, stage_refs, (send_sems, recv_sems, local_sems) = _split(refs, n, n, n, 3)
        x, y, c = lax.axis_index("x"), lax.axis_index("y"), lax.axis_index("c")

        def stage(e):
            cps = [pltpu.make_async_copy(_half(g_refs[k], e, names[k], lead=1), stage_refs[k], local_sems.at[k])
                   for k in range(n)]
            for cp in cps:
                cp.start()
            return cps

        staged = stage(1 - c)
        sends = []
        for k in range(n):
            staged[k].wait()
            cp = _remote(stage_refs[k], got_refs[k], send_sems, recv_sems, k, (x, y, 1 - c))
            cp.start()
            sends.append(cp)
        for cp in sends:
            cp.wait_send()
        staged = stage(c)
        for k in range(n):
            sends[k].wait_recv()
            staged[k].wait()

            @pl.loop(0, N_CHIP)
            def _(j):
                got_refs[k][j] = (got_refs[k][j].astype(F32) + stage_refs[k][j].astype(F32)).astype(BF16)

    half = [jax.ShapeDtypeStruct((N_CHIP,) + _half_shape(nm), BF16) for nm in names]
    return pl.pallas_call(
        body, in_specs=[ANY] * n, out_specs=[VMEM_WHOLE] * n, out_shape=half,
        scratch_shapes=[pltpu.VMEM(h.shape, h.dtype) for h in half]
        + [pltpu.SemaphoreType.DMA((n,)), pltpu.SemaphoreType.DMA((n,)), pltpu.SemaphoreType.DMA((n,))],
        compiler_params=_params(), name=name,
    )(*grads)


class _SendPartials:
    def __init__(self, names, small_shape=None):
        self.n = len(names)
        self.small = small_shape is not None
        self.n_sems = 3 * self.n + 7
        self.out_shape = [jax.ShapeDtypeStruct((N_CHIP,) + _half_shape(nm), BF16) for nm in names]
        if self.small:
            self.out_shape.append(jax.ShapeDtypeStruct((N_DEV,) + small_shape, F32))

    def _piece(self, ins, outs, ss, rs, k, j, peer, src_slot, dst_slot, c):
        return _remote(ins[k].at[src_slot], outs[k].at[dst_slot], ss, rs, 3 * k + j, (*peer, c))

    def _small(self, ins, outs, ss, rs, r, other, slot):
        return _remote(ins[self.n], outs[self.n].at[slot], ss, rs, 3 * self.n + r, other)

    @staticmethod
    def _others(x, y, c):
        return [(x, y, 1 - c), (1 - x, y, c), (1 - x, y, 1 - c), (x, 1 - y, c), (x, 1 - y, 1 - c),
                (1 - x, 1 - y, c), (1 - x, 1 - y, 1 - c)]

    def first(self, ins, outs, ss, rs):
        x, y, c, peers = _mesh_place()
        me = 2 * x + y
        for k in range(self.n):
            for j, (px, py) in enumerate(peers):
                self._piece(ins, outs, ss, rs, k, j, (px, py), 2 * px + py, me, c).start()
        if self.small:
            for r, other in enumerate(self._others(x, y, c)):
                self._small(ins, outs, ss, rs, r, other, 4 * x + 2 * y + c).start()
            outs[self.n][4 * x + 2 * y + c] = ins[self.n][...]
        for k in range(self.n):
            outs[k][me] = ins[k][me]

    def middle(self, ins, outs, ss, rs):
        pass

    def last(self, ins, outs, ss, rs):
        x, y, c, peers = _mesh_place()
        me = 2 * x + y
        for k in range(self.n):
            for j, (px, py) in enumerate(peers):
                self._piece(ins, outs, ss, rs, k, j, (px, py), me, 2 * px + py, c).wait_recv()
                self._piece(ins, outs, ss, rs, k, j, (px, py), 2 * px + py, me, c).wait_send()
        if self.small:
            for r, (px, py, pc) in enumerate(self._others(x, y, c)):
                self._small(ins, outs, ss, rs, r, (px, py, pc), 4 * px + 2 * py + pc).wait_recv()
                self._small(ins, outs, ss, rs, r, (px, py, pc), 4 * x + 2 * y + c).wait_send()


def _sum_swap(names, parts):
    n = len(parts)

    def body(*refs):
        p_refs, o_refs, (send_sems, recv_sems) = _split(refs, n, n, 2)
        x, y, c = lax.axis_index("x"), lax.axis_index("y"), lax.axis_index("c")
        for e in range(2):
            @pl.when(c == e)
            def _():
                for k in range(n):
                    g = p_refs[k][0].astype(F32)
                    for s in range(1, N_CHIP):
                        g = g + p_refs[k][s].astype(F32)
                    r, cols = _half_shape(names[k])
                    if _BIG_SPLIT[names[k]] == 0:
                        o_refs[k][e * r:(e + 1) * r, :] = g
                    else:
                        o_refs[k][:, e * cols:(e + 1) * cols] = g
        sends = []
        for k in range(n):
            mine = _half(o_refs[k], c, names[k])
            cp = _remote(mine, mine, send_sems, recv_sems, k, (x, y, 1 - c))
            cp.start()
            sends.append(cp)
        for k in range(n):
            theirs = _half(o_refs[k], 1 - c, names[k])
            _remote(theirs, theirs, send_sems, recv_sems, k, (x, y, 1 - c)).wait_recv()
        for cp in sends:
            cp.wait_send()

    return pl.pallas_call(
        body, in_specs=[VMEM_WHOLE] * n, out_specs=[VMEM_WHOLE] * n,
        out_shape=[jax.ShapeDtypeStruct(_BIG_SHARD[nm], F32) for nm in names],
        scratch_shapes=[pltpu.SemaphoreType.DMA((n,)), pltpu.SemaphoreType.DMA((n,))],
        compiler_params=_params(), name="sum_swap",
    )(*parts)


def _tile(rows, cols, itemsize, budget):
    t = cols if rows % 16 else rows
    other = rows if rows % 16 else cols
    step = 256 if rows % 16 else 32
    while t % step == 0 and t * other * itemsize > budget:
        t //= 2
    return (rows, t) if rows % 16 else (t, cols)


def _adamw_math(w, g, m, v):
    m = ADAM_B1 * m + (1.0 - ADAM_B1) * g
    v = ADAM_B2 * v + (1.0 - ADAM_B2) * (g * g)
    m_hat = m / (1.0 - ADAM_B1 ** ADAM_STEP)
    v_hat = v / (1.0 - ADAM_B2 ** ADAM_STEP)
    delta = -ADAM_LR * (m_hat / (jnp.sqrt(v_hat) + ADAM_EPS) + ADAM_WD * w)
    return delta, m, v


def _adamw_big(g, w, m, v, name):
    r, c = w.shape
    tr, tc = _tile(r, c, 4, 1024 * 1024)

    def body(g_ref, w_ref, m_ref, v_ref, d_ref, nm_ref, nv_ref):
        d_ref[...], nm_ref[...], nv_ref[...] = _adamw_math(w_ref[...], g_ref[...], m_ref[...], v_ref[...])

    blk = pl.BlockSpec((tr, tc), lambda i, l: (i, l))
    return pl.pallas_call(
        body, grid=(r // tr, c // tc), in_specs=[blk, blk, blk, blk],
        out_specs=[blk, blk, blk], out_shape=[jax.ShapeDtypeStruct((r, c), F32)] * 3,
        compiler_params=_params(("arbitrary", "arbitrary")), name=name,
    )(g, w, m, v)


def _adamw_rows(g, w, m, v, name):
    r, _, c = w.shape
    tc = 128

    def body(g_ref, w_ref, m_ref, v_ref, g3_ref, d_ref, nm_ref, nv_ref):
        g = g_ref[...]
        g3_ref[:, 0, :] = g
        d_ref[:, 0, :], nm_ref[:, 0, :], nv_ref[:, 0, :] = _adamw_math(w_ref[:, 0, :], g, m_ref[:, 0, :], v_ref[:, 0, :])

    rows = pl.BlockSpec((r, 1, tc), lambda l: (0, 0, l))
    return pl.pallas_call(
        body, grid=(c // tc,), in_specs=[pl.BlockSpec((r, tc), lambda l: (0, l)), rows, rows, rows],
        out_specs=[rows] * 4, out_shape=[jax.ShapeDtypeStruct((r, 1, c), F32)] * 4,
        compiler_params=_params(("arbitrary",)), name=name,
    )(g, w, m, v)


def _sum_small(parts):
    def body(p_ref, o_ref):
        g = p_ref[0]
        for d in range(1, N_DEV):
            g = g + p_ref[d]
        o_ref[...] = g

    return pl.pallas_call(body, out_shape=jax.ShapeDtypeStruct(parts.shape[1:], F32), name="sum_small")(parts)


def _adamw_small(ws, gs, ms, vs):
    n = len(ws)

    def body(*refs):
        w_refs, g_refs, m_refs, v_refs, d_refs, nm_refs, nv_refs = _split(refs, *([n] * 7))
        for k in range(n):
            d_refs[k][...], nm_refs[k][...], nv_refs[k][...] = _adamw_math(w_refs[k][...], g_refs[k][...], m_refs[k][...],
                                                                             v_refs[k][...])

    shapes = [jax.ShapeDtypeStruct(w.shape, F32) for w in ws]
    res = pl.pallas_call(body, out_shape=shapes * 3, name="adamw_small")(*ws, *gs, *ms, *vs)
    return res[:n], res[n:2 * n], res[2 * n:]


def _pack(arrs):
    flat = jnp.concatenate([a.reshape(-1) for a in arrs])
    rows = -(-flat.shape[0] // 1024) * 8
    return jnp.pad(flat, (0, rows * 128 - flat.shape[0])).reshape(rows, 128)


def _unpack(buf, shapes):
    flat = buf.reshape(-1)
    out, off = [], 0
    for s in shapes:
        size = 1
        for d in s:
            size *= d
        out.append(flat[off:off + size].reshape(s))
        off += size
    return out


def _block_rows(w):
    return jnp.pad(w.reshape(512, 4), ((0, 0), (0, 124)))


def _cols(a4):
    return jnp.transpose(a4, (1, 0, 2)).reshape(a4.shape[1], -1)


_LATE = ("w_pa", "w_pb", "w_o", "w_up", "w_down")


def _full_weights(gathered):
    joined = {"w_in": (D_IN, D_MODEL), "w_o": (D_MODEL, D_MODEL), "w_down": (D_FF, D_MODEL)}
    return {n: (a.reshape(joined[n]) if n in joined else a) for n, a in gathered.items()}


def _local_step(x, target, w, sp, late_shards=None):
    sp = {n: (a.reshape(1, -1) if a.ndim == 1 else a) for n, a in sp.items()}
    wau = jnp.zeros((128, 256), F32).at[0:16].set(sp["w_a_up"])
    wif = jnp.zeros((1536, 128), F32).at[:, 0:8].set(sp["w_if"])
    bif = jnp.zeros((1, 128), F32).at[:, 0:8].set(sp["b_if"])
    p = {"wau": wau, "bau": sp["b_a_up"], "ggla": sp["g_gla_norm"], "cw": sp["conv_w"], "cb": sp["conv_b"],
         "wq": _block_rows(sp["w_q_ml"]), "wk": _block_rows(sp["w_k_ml"]), "wv": _block_rows(sp["w_v_ml"]),
         "wif": wif, "bif": bif, "skip": sp["ml_skip"], "gml": sp["g_ml_norm"]}

    pm, gab, h = _in_proj(x, sp["g_pre_mix"], w["w_in"])
    if late_shards is None:
        ab, *states = _mixer_fwd(pm, p)
    else:
        ab, *rest = _mixer_fwd(pm, p, _Gather(_LATE), late_shards)
        states = rest[:4]
        w = dict(w, **_full_weights(dict(zip(_LATE, rest[4:]))))
    x1, mix, merged = _merge_fwd(ab, gab, x, w["w_pa"], w["w_pb"], w["w_o"], sp["g_post_mix"])
    dx1, u, dd, h2, dpre, dg_post_mlp, dg_pre_mlp, loss = _mlp(x1, target, sp["g_pre_mlp"], sp["g_post_mlp"],
                                                                w["w_up"], w["w_down"])
    dmix, dya, dyb, dgab, dab, dg_post_mix = _merge_bwd(dx1, mix, ab, gab, w["w_pa"], w["w_pb"], w["w_o"], sp["g_post_mix"])
    big = {
        "w_pa": _tn_matmul(ab[:, 0:512], dya, "dw_pa", shards=N_CHIP),
        "w_pb": _tn_matmul(ab[:, 512:1024], dyb, "dw_pb", shards=N_CHIP),
        "w_o": _tn_matmul(merged, dmix, "dw_o"),
        "w_up": _tn_matmul(h2, dpre, "dw_up", shards=N_CHIP),
        "w_down": _tn_matmul(u, dd, "dw_down"),
    }
    if late_shards is None:
        dpm, dp, _ = _mixer_bwd(pm, dab, states, p)
    else:
        partial = _presum(_LATE, [big[n].reshape((N_CHIP,) + _BIG_SHARD[n]) for n in _LATE], "presum_late")
        dpm, dp, parts = _mixer_bwd(pm, dab, states, p, _SendPartials(_LATE), partial)
        big = dict(zip(_LATE, parts))
    dx, dg_pre_mix = _in_proj_bwd(dpm, dgab, x, dx1, sp["g_pre_mix"], w["w_in"])
    big["w_in"] = _dw_in(dpm, dgab, h)
    small = {
        "g_pre_mix": dg_pre_mix, "b_a_up": dp["bau"], "g_gla_norm": dp["ggla"], "conv_b": dp["cb"],
        "w_q_ml": dp["wq"][:, 0:4].reshape(128, 4, 4), "w_k_ml": dp["wk"][:, 0:4].reshape(128, 4, 4),
        "w_v_ml": dp["wv"][:, 0:4].reshape(128, 4, 4),
        "b_if": dp["bif"][:, 0:8], "ml_skip": dp["skip"], "g_ml_norm": dp["gml"], "g_post_mix": dg_post_mix,
        "g_pre_mlp": dg_pre_mlp, "g_post_mlp": dg_post_mlp, "w_a_up": dp["wau"][0:16], "conv_w": dp["cw"],
        "w_if": dp["wif"][:, 0:8], "loss": loss[:, 0:1],
    }
    return dx, big, small


_SMALL_REPL = ("g_pre_mix", "b_a_up", "g_gla_norm", "conv_b", "w_q_ml", "w_k_ml", "w_v_ml", "b_if", "ml_skip",
               "g_ml_norm", "g_post_mix", "g_pre_mlp", "g_post_mlp")
_SMALL_SHARDED = ("w_a_up", "conv_w", "w_if")
_SMALL_ORDER = _SMALL_REPL + _SMALL_SHARDED + ("loss",)
_WEIGHTS = ("g_pre_mix", "w_in", "w_a_up", "b_a_up", "g_gla_norm", "conv_w", "conv_b", "w_q_ml", "w_k_ml", "w_v_ml",
            "w_if", "b_if", "ml_skip", "g_ml_norm", "w_pa", "w_pb", "w_o", "g_post_mix", "g_pre_mlp", "w_up", "w_down",
            "g_post_mlp")


def _as_shard(name, a):
    return jnp.transpose(a, (2, 0, 1)) if name == "w_in" else a[0]


def _from_shard(name, a):
    return jnp.transpose(a, (1, 2, 0)) if name == "w_in" else a[None]


def kernel(x, g_pre_mix, w_in, w_a_up, b_a_up, g_gla_norm, conv_w, conv_b, w_q_ml, w_k_ml, w_v_ml, w_if, b_if, ml_skip, g_ml_norm, w_pa, w_pb, w_o, g_post_mix, g_pre_mlp, w_up, w_down, g_post_mlp, loss_target, m_g_pre_mix, m_w_in, m_w_a_up, m_b_a_up, m_g_gla_norm, m_conv_w, m_conv_b, m_w_q_ml, m_w_k_ml, m_w_v_ml, m_w_if, m_b_if, m_ml_skip, m_g_ml_norm, m_w_pa, m_w_pb, m_w_o, m_g_post_mix, m_g_pre_mlp, m_w_up, m_w_down, m_g_post_mlp, v_g_pre_mix, v_w_in, v_w_a_up, v_b_a_up, v_g_gla_norm, v_conv_w, v_conv_b, v_w_q_ml, v_w_k_ml, v_w_v_ml, v_w_if, v_b_if, v_ml_skip, v_g_ml_norm, v_w_pa, v_w_pb, v_w_o, v_g_post_mix, v_g_pre_mlp, v_w_up, v_w_down, v_g_post_mlp):
    args = dict(locals())
    wts = {n: _as_shard(n, args[n]) for n in _WEIGHTS}
    mom = {n: _as_shard(n, args["m_" + n]) for n in _WEIGHTS}
    var = {n: _as_shard(n, args["v_" + n]) for n in _WEIGHTS}
    chip = 2 * lax.axis_index("x") + lax.axis_index("y")

    first = ("w_in",) + _SMALL_SHARDED
    gathered = dict(zip(first, _run_alone(_Gather(("w_in",), [wts[n] for n in _SMALL_SHARDED]),
                                          [wts[n][:, 0, :].astype(BF16) if n == "w_in" else wts[n] for n in first],
                                          "gather_first")))
    sp = {n: wts[n] for n in _SMALL_REPL}
    sp["w_a_up"] = _cols(gathered["w_a_up"])
    sp["conv_w"] = _cols(gathered["conv_w"])
    sp["w_if"] = gathered["w_if"].reshape(1536, 8)

    dx, big, small = _local_step(x[0], loss_target[0], _full_weights({"w_in": gathered["w_in"]}), sp,
                                 late_shards=[wts[n].astype(BF16) for n in _LATE])

    small_shapes = [small[n].shape for n in _SMALL_ORDER]
    packed = _pack([small[n] for n in _SMALL_ORDER])
    partial = _presum(("w_in",), [big["w_in"].reshape((N_CHIP,) + _BIG_SHARD["w_in"])], "presum_w_in")
    parts_in, small_parts = _run_alone(_SendPartials(("w_in",), packed.shape), [*partial, packed], "send_partials")
    big["w_in"] = parts_in
    sums = _sum_swap(_BIG, [big[n] for n in _BIG])

    grads, delta, new_m, new_v = {}, {}, {}, {}
    for n, g in zip(_BIG, sums):
        if n == "w_in":
            g, d, nm, nv = _adamw_rows(g, wts[n], mom[n], var[n], "adamw_" + n)
        else:
            d, nm, nv = _adamw_big(g, wts[n], mom[n], var[n], "adamw_" + n)
        grads[n], delta[n], new_m[n], new_v[n] = (_from_shard(n, a) for a in (g, d, nm, nv))
    summed = dict(zip(_SMALL_ORDER, _unpack(_sum_small(small_parts), small_shapes)))
    loss = summed["loss"].reshape(())
    for n in _SMALL_REPL:
        grads[n] = summed[n].reshape(args[n].shape)
    grads["w_a_up"] = lax.dynamic_slice_in_dim(summed["w_a_up"], chip * 64, 64, axis=1)[None]
    grads["conv_w"] = lax.dynamic_slice_in_dim(summed["conv_w"], chip * 128, 128, axis=1)[None]
    grads["w_if"] = lax.dynamic_slice_in_dim(summed["w_if"], chip * 384, 384, axis=0)[None]
    small_names = _SMALL_REPL + _SMALL_SHARDED
    upd = _adamw_small([args[n] for n in small_names], [grads[n] for n in small_names],
                       [args["m_" + n] for n in small_names], [args["v_" + n] for n in small_names])
    for dst, arrs in zip((delta, new_m, new_v), upd):
        dst.update(zip(small_names, arrs))

    outs = [loss, dx[None]]
    for group in (grads, delta, new_m, new_v):
        outs += [group[n] for n in _WEIGHTS]
    return tuple(outs)
```

```python
import functools

import jax
import jax.numpy as jnp
from jax import lax
from jax.experimental import pallas as pl
from jax.experimental.pallas import tpu as pltpu

F32 = jnp.float32
BF16 = jnp.bfloat16

SEQ = 2048
D_MODEL = 1024
CHUNK = 64
N_CHUNK = SEQ // CHUNK
HEADS = 4
GLA_DK = 64
GLA_DV = 128
ML_DH = 128
D_FF = 4096
EPS = 1e-6
N_CHIP = 4
N_DEV = 8
TOK_TILE = 256
N_TOK_TILE = SEQ // TOK_TILE
SWEEP = 2
assert CHUNK == 64
N_SWEEP = N_CHUNK // SWEEP

PM_W = 2688
PM_XM = 1536
PM_OP = 2048
PM_AL = 2560
GAB_W = 2048
D_IN = 4624
IN_SHARD = D_IN // N_CHIP
IN_ALOW = 1536
IN_XM = 1552
IN_GATES = 2576

ADAM_LR = 0.001
ADAM_B1 = 0.9
ADAM_B2 = 0.999
ADAM_EPS = 1e-08
ADAM_WD = 0.01
ADAM_STEP = 10

VMEM_LIMIT = 56 * 1024 * 1024


def _params(sem=None):
    return pltpu.CompilerParams(dimension_semantics=sem, vmem_limit_bytes=VMEM_LIMIT)


def _dot(a, b, ca, cb):
    return lax.dot_general(a.astype(BF16), b.astype(BF16), (((ca,), (cb,)), ((), ())), preferred_element_type=F32)


def _pmm_nn(a, b):
    return _dot(a, b, 1, 0)


def _pmm_nt(a, b):
    return _dot(a, b, 1, 1)


def _pmm_tn(a, b):
    return _dot(a, b, 0, 0)


def _pcmm(c, x):
    return lax.dot_general(c, x, (((1,), (0,)), ((), ())), precision=lax.Precision.HIGHEST, preferred_element_type=F32)


@jax.custom_vjp
def _mm_nn(a, b):
    return _dot(a, b, 1, 0)


@jax.custom_vjp
def _mm_nt(a, b):
    return _dot(a, b, 1, 1)


@jax.custom_vjp
def _mm_tn(a, b):
    return _dot(a, b, 0, 0)


_mm_nn.defvjp(lambda a, b: (_dot(a, b, 1, 0), (a, b)), lambda r, g: (_mm_nt(g, r[1]), _mm_tn(r[0], g)))
_mm_nt.defvjp(lambda a, b: (_dot(a, b, 1, 1), (a, b)), lambda r, g: (_mm_nn(g, r[1]), _mm_tn(g, r[0])))
_mm_tn.defvjp(lambda a, b: (_dot(a, b, 0, 0), (a, b)), lambda r, g: (_mm_nt(r[1], g), _mm_nn(r[0], g)))


@jax.custom_vjp
def _cmm(c, x):
    return _pcmm(c, x)


_cmm.defvjp(
    lambda c, x: (_pcmm(c, x), c),
    lambda c, g: (jnp.zeros_like(c), lax.dot_general(c, g, (((0,), (0,)), ((), ())), precision=lax.Precision.HIGHEST,
                                                      preferred_element_type=F32)),
)

_PLAIN_OPS = (_pmm_nn, _pmm_nt, _pmm_tn, _pcmm)
_VJP_OPS = (_mm_nn, _mm_nt, _mm_tn, _cmm)


def _sigmoid(x):
    return 0.5 * (jnp.tanh(0.5 * x) + 1.0)


def _log_sigmoid(x):
    return jnp.minimum(x, 0.0) - jnp.log(1.0 + jnp.exp(-jnp.abs(x)))


def _mean(x):
    return jnp.mean(x, axis=-1, keepdims=True)


def _nt(a, b):
    return lax.dot_general(a, b, (((1,), (1,)), ((), ())), preferred_element_type=F32)


def _tn(a, b):
    return lax.dot_general(a, b, (((0,), (0,)), ((), ())), preferred_element_type=F32)


def _mixer_chunk(ops, p, st, pm, xprev8):
    mm_nn, mm_nt, mm_tn, cmm = ops
    n_rows = pm.shape[0]
    n_ch = n_rows // CHUNK
    row = lax.broadcasted_iota(jnp.int32, (n_rows, n_rows), 0)
    col = lax.broadcasted_iota(jnp.int32, (n_rows, n_rows), 1)
    tri = jnp.logical_and((row >> 6) == (col >> 6), row >= col).astype(F32)
    causal = tri[0:CHUNK, 0:CHUNK] > 0.0
    q = pm[:, 0:256]
    k = pm[:, 256:512]
    v = pm[:, 512:1024]
    g = pm[:, 1024:1536]
    xm = pm[:, PM_XM:PM_XM + 512]
    opre = pm[:, PM_OP:PM_OP + 512]
    alow = pm[:, PM_AL:PM_AL + 128]
    hs = range(HEADS)
    cs = range(n_ch)
    pairs = [(i, h) for i in cs for h in hs]
    rs = [slice(i * CHUNK, (i + 1) * CHUNK) for i in cs]
    last = [slice((i + 1) * CHUNK - 1, (i + 1) * CHUNK) for i in cs]
    s6 = [slice(h * GLA_DK, (h + 1) * GLA_DK) for h in hs]
    s12 = [slice(h * 128, (h + 1) * 128) for h in hs]

    la = _log_sigmoid(mm_nn(alow, p["wau"]) + p["bau"]) * (1.0 / 16.0)
    cum = cmm(tri, la)
    cum_last = [cum[last[i], :] for i in cs]
    to_end = jnp.concatenate([cum_last[i] - cum[rs[i], :] for i in cs], axis=0)
    e_pos = jnp.exp(cum)
    e_neg = jnp.exp(-cum)
    qs = q * (GLA_DK ** -0.5)
    qp = qs * e_pos
    qn = qs * e_neg
    kp = k * e_pos
    kn = k * e_neg
    kl = k * jnp.exp(to_end)
    dec = [jnp.exp(cum_last[i]) for i in cs]
    a_fwd = {(i, h): mm_nt(qp[rs[i], s6[h]], kn[rs[i], s6[h]]) for i, h in pairs}
    a_bwd = {(i, h): mm_nt(qn[rs[i], s6[h]], kp[rs[i], s6[h]]) for i, h in pairs}
    s_chunk = {(i, h): mm_tn(v[rs[i], s12[h]], kl[rs[i], s6[h]]) for i, h in pairs}
    mem = {(0, h): st["S"][h] for h in hs}
    for i, h in pairs:
        mem[(i + 1, h)] = mem[(i, h)] * dec[i][:, s6[h]] + s_chunk[(i, h)]
    s_new = [mem[(n_ch, h)] for h in hs]
    o_inter = {(i, h): mm_nt(qp[rs[i], s6[h]], mem[(i, h)]) for i, h in pairs}
    scores = {ih: jnp.where(causal, a_fwd[ih], a_bwd[ih]) for ih in pairs}
    o = {(i, h): mm_nn(scores[(i, h)], v[rs[i], s12[h]]) + o_inter[(i, h)] for i, h in pairs}
    o = {ih: o[ih] * lax.rsqrt(_mean(o[ih] * o[ih]) + EPS) * p["ggla"] for ih in pairs}
    gate = g * _sigmoid(g)
    out_a = {(i, h): o[(i, h)] * gate[rs[i], s12[h]] for i, h in pairs}

    xx = jnp.concatenate([xprev8, xm], axis=0)
    pre = p["cb"]
    for j in range(4):
        pre = pre + p["cw"][j:j + 1, :] * xx[5 + j:5 + j + n_rows, :]
    xc = pre * _sigmoid(pre)
    qm = [mm_nn(xc[:, s12[h]], p["wq"][h]) for h in hs]
    km = [mm_nn(xc[:, s12[h]], p["wk"][h]) for h in hs]
    vm = [mm_nn(xm[:, s12[h]], p["wv"][h]) for h in hs]
    qcat = jnp.concatenate(qm, axis=1)
    kcat = jnp.concatenate(km, axis=1)
    vcat = jnp.concatenate(vm, axis=1)
    gates = (mm_nn(qcat, p["wif"][0:512]) + mm_nn(kcat, p["wif"][512:1024]) + mm_nn(vcat, p["wif"][1024:1536])
             + p["bif"])
    lf = _log_sigmoid(gates)
    fc = cmm(tri, lf)
    gates_t = gates.T
    fc_t = fc.T
    ks = [km[h] * (ML_DH ** -0.5) for h in hs]
    qk = {(i, h): mm_nt(qm[h][rs[i]], ks[h][rs[i]]) for i, h in pairs}
    li_c = {(i, h): gates[rs[i], h:h + 1] for i, h in pairs}
    fc_c = {(i, h): fc[rs[i], 4 + h:5 + h] for i, h in pairs}
    f_last = {(i, h): fc[last[i], 4 + h:5 + h] for i, h in pairs}
    a = {ih: f_last[ih] - fc_c[ih] + li_c[ih] for ih in pairs}
    m_loc = {ih: jnp.max(a[ih], axis=0, keepdims=True) for ih in pairs}
    kw = {(i, h): ks[h][rs[i]] * jnp.exp(a[(i, h)] - m_loc[(i, h)]) for i, h in pairs}
    c_chunk = {(i, h): mm_tn(kw[(i, h)], vm[h][rs[i]]) for i, h in pairs}
    c_in = {(0, h): st["C"][h] for h in hs}
    n_in = {(0, h): st["n"][h] for h in hs}
    m_in = {(0, h): st["m"][h][:, 0:1] for h in hs}
    for i, h in pairs:
        m_nx = jnp.maximum(f_last[(i, h)] + m_in[(i, h)], m_loc[(i, h)])
        sp = jnp.exp(f_last[(i, h)] + m_in[(i, h)] - m_nx)
        sl = jnp.exp(m_loc[(i, h)] - m_nx)
        c_in[(i + 1, h)] = sp * c_in[(i, h)] + sl * c_chunk[(i, h)]
        n_in[(i + 1, h)] = sp * n_in[(i, h)] + sl * jnp.sum(kw[(i, h)], axis=0, keepdims=True)
        m_in[(i + 1, h)] = m_nx
    q_c = {(i, h): mm_nn(qm[h][rs[i]], c_in[(i, h)]) for i, h in pairs}
    log_d = {(i, h): gates_t[h:h + 1, rs[i]] - jnp.abs(fc_c[(i, h)] - fc_t[4 + h:5 + h, rs[i]]) for i, h in pairs}
    g_int = {ih: fc_c[ih] + m_in[ih] for ih in pairs}
    m_t = {ih: jnp.maximum(g_int[ih], jnp.max(log_d[ih], axis=1, keepdims=True)) for ih in pairs}
    s = {ih: qk[ih] * jnp.exp(log_d[ih] - m_t[ih]) for ih in pairs}
    scl = {ih: jnp.exp(g_int[ih] - m_t[ih]) for ih in pairs}
    num = {(i, h): mm_nn(s[(i, h)], vm[h][rs[i]]) + scl[(i, h)] * q_c[(i, h)] for i, h in pairs}
    den = {(i, h): jnp.sum(s[(i, h)], axis=1, keepdims=True)
           + scl[(i, h)] * jnp.sum(qm[h][rs[i]] * n_in[(i, h)], axis=1, keepdims=True) for i, h in pairs}
    den = {ih: jnp.maximum(jnp.abs(den[ih]), jnp.exp(-m_t[ih])) for ih in pairs}
    open_gate = _sigmoid(opre)
    hc = {(i, h): num[(i, h)] / den[(i, h)] * open_gate[rs[i], s12[h]] for i, h in pairs}
    d0 = {ih: hc[ih] - _mean(hc[ih]) for ih in pairs}
    y = {ih: d0[ih] * lax.rsqrt(_mean(d0[ih] * d0[ih]) + EPS) for ih in pairs}
    skipped = p["skip"] * xc
    out_b = {(i, h): y[(i, h)] * p["gml"][:, s12[h]] + skipped[rs[i], s12[h]] for i, h in pairs}
    ab = jnp.concatenate([jnp.concatenate([out_a[(i, h)] for h in hs] + [out_b[(i, h)] for h in hs], axis=1) for i in cs],
                         axis=0)
    new = {"S": s_new, "C": [c_in[(n_ch, h)] for h in hs], "n": [n_in[(n_ch, h)] for h in hs],
           "m": [jnp.broadcast_to(m_in[(n_ch, h)], (1, ML_DH)) for h in hs]}
    return ab, new


_P_NAMES = ("wau", "bau", "ggla", "cw", "cb", "wq", "wk", "wv", "wif", "bif", "skip", "gml")
_P_SHAPES = {
    "wau": (128, 256), "bau": (1, 256), "ggla": (1, 128), "cw": (4, 512), "cb": (1, 512),
    "wq": (512, 128), "wk": (512, 128), "wv": (512, 128),
    "wif": (1536, 128), "bif": (1, 128), "skip": (1, 512), "gml": (1, 512),
}
_P_BLOCKDIAG = ("wq", "wk", "wv")
_S_NAMES = ("S", "C", "n", "m")
_S_SHAPES = {"S": (HEADS, GLA_DV, GLA_DK), "C": (HEADS, ML_DH, ML_DH), "n": (HEADS, 1, ML_DH), "m": (HEADS, 1, ML_DH)}


def _per_head(ref):
    return [ref[h] for h in range(HEADS)]


def _block_mask():
    r = lax.broadcasted_iota(jnp.int32, (128, 128), 0)
    c = lax.broadcasted_iota(jnp.int32, (128, 128), 1)
    same_block = (r >> 2) == (c >> 2)
    spread = jnp.logical_and(r < 4, (c & 3) == r)
    return same_block.astype(F32), spread.astype(F32)


def _expand_blockdiag(w_ref, dense_ref):
    same_block, spread = _block_mask()
    for h in range(HEADS):
        tiled = _pmm_nn(w_ref[h * 128:(h + 1) * 128, :], spread)
        dense_ref[h] = tiled * same_block


def _collect_blockdiag(ddense_ref, dw_ref):
    same_block, spread = _block_mask()
    for h in range(HEADS):
        dw_ref[h * 128:(h + 1) * 128, :] = lax.dot_general(
            ddense_ref[h] * same_block, spread, (((1,), (1,)), ((), ())), precision=lax.Precision.HIGHEST,
            preferred_element_type=F32)


def _const_spec(shape):
    zeros = (0,) * len(shape)
    return pl.BlockSpec(shape, lambda i: zeros)


def _split(refs, *counts):
    out, at = [], 0
    for c in counts:
        out.append(refs[at:at + c])
        at += c
    assert at == len(refs)
    return out


def _ride(rider, phases, cond, ins, outs, sems):
    if rider is None:
        return
    lands, (send_sems, recv_sems, flush_sems) = sems[:-3], sems[-3:]

    @pl.when(cond)
    def _():
        for phase in phases:
            getattr(rider, phase)(ins, lands, send_sems, recv_sems)
        if "last" in phases:
            flush = [pltpu.make_async_copy(lands[k], outs[k], flush_sems.at[k]) for k in range(len(outs))]
            for cp in flush:
                cp.start()
            for cp in flush:
                cp.wait()


def _rider_specs(rider, rider_ins):
    if rider is None:
        return [], [], [], []
    scratch = [pltpu.VMEM(s.shape, s.dtype) for s in rider.out_shape]
    scratch += [pltpu.SemaphoreType.DMA((rider.n_sems,)), pltpu.SemaphoreType.DMA((rider.n_sems,)),
                pltpu.SemaphoreType.DMA((len(rider.out_shape),))]
    return [VMEM_WHOLE] * len(rider_ins), [ANY] * len(rider.out_shape), list(rider.out_shape), scratch


def _mixer_fwd(pm, p, rider=None, rider_ins=()):
    n_p = len(_P_NAMES)
    r_in, r_out_specs, r_out_shape, r_sems = _rider_specs(rider, rider_ins)

    def body(*refs):
        (pm_ref, xprev_ref), p_list, ride_in, (ab_ref,), so_refs, ride_out, sc_refs, dense_list, sems = _split(
            refs, 2, n_p, len(r_in), 1, 4, len(r_out_specs), 4, 3, len(r_sems))
        p_refs = dict(zip(_P_NAMES, p_list))
        dense = dict(zip(_P_BLOCKDIAG, dense_list))
        n = pl.program_id(0)
        _ride(rider, ("first",), n == 0, ride_in, ride_out, sems)

        @pl.when(n == 0)
        def _():
            for r in sc_refs:
                r[...] = jnp.zeros_like(r)
            for nm in _P_BLOCKDIAG:
                _expand_blockdiag(p_refs[nm], dense[nm])

        st = {name: _per_head(r) for name, r in zip(_S_NAMES, sc_refs)}
        pv = {nm: (_per_head(dense[nm]) if nm in _P_BLOCKDIAG else p_refs[nm][...]) for nm in _P_NAMES}
        for name, r in zip(_S_NAMES, so_refs):
            for h in range(HEADS):
                r[0, h] = st[name][h]
        xprev8 = jnp.where(n > 0, xprev_ref[CHUNK - 8:CHUNK, :], 0.0)
        ab, st = _mixer_chunk(_PLAIN_OPS, pv, st, pm_ref[...], xprev8)
        ab_ref[...] = ab.astype(BF16)
        for name, r in zip(_S_NAMES, sc_refs):
            for h in range(HEADS):
                r[h] = st[name][h]
        _ride(rider, ("middle",), n == N_SWEEP - 2, ride_in, ride_out, sems)
        _ride(rider, ("last",), n == N_SWEEP - 1, ride_in, ride_out, sems)

    in_specs = [pl.BlockSpec((SWEEP * CHUNK, PM_W), lambda i: (i, 0)),
                pl.BlockSpec((CHUNK, 512), lambda i: (jnp.maximum(SWEEP * i - 1, 0), PM_XM // 512))]
    in_specs += [_const_spec(_P_SHAPES[nm]) for nm in _P_NAMES] + r_in
    out_specs = [pl.BlockSpec((SWEEP * CHUNK, 1024), lambda i: (i, 0))]
    out_shape = [jax.ShapeDtypeStruct((SEQ, 1024), BF16)]
    for nm in _S_NAMES:
        shp = _S_SHAPES[nm]
        out_specs.append(pl.BlockSpec((1,) + shp, lambda i: (i, 0, 0, 0)))
        out_shape.append(jax.ShapeDtypeStruct((N_SWEEP,) + shp, F32))
    return pl.pallas_call(
        body, grid=(N_SWEEP,), in_specs=in_specs, out_specs=out_specs + r_out_specs, out_shape=out_shape + r_out_shape,
        scratch_shapes=[pltpu.VMEM(_S_SHAPES[nm], F32) for nm in _S_NAMES]
        + [pltpu.VMEM((HEADS, 128, 128), F32) for _ in _P_BLOCKDIAG] + r_sems,
        compiler_params=_params(("arbitrary",)), name="mixer_fwd",
    )(pm, pm, *[p[nm] for nm in _P_NAMES], *rider_ins)


def _mixer_bwd(pm, dab, states, p, rider=None, rider_ins=()):
    n_p = len(_P_NAMES)
    r_in, r_out_specs, r_out_shape, r_sems = _rider_specs(rider, rider_ins)

    def body(*refs):
        ((pm_ref, xprev_ref, dab_ref), si_refs, p_list, ride_in, (dpm_ref,), dp_list, ride_out, ds_refs, (carry_ref,),
         dense_list, ddense_list, sems) = _split(refs, 3, 4, n_p, len(r_in), 1, n_p, len(r_out_specs), 4, 1, 3, 3, len(r_sems))
        p_refs = dict(zip(_P_NAMES, p_list))
        dp_refs = dict(zip(_P_NAMES, dp_list))
        dense = dict(zip(_P_BLOCKDIAG, dense_list))
        ddense = dict(zip(_P_BLOCKDIAG, ddense_list))
        i = pl.program_id(0)
        blk = N_SWEEP - 1 - i
        _ride(rider, ("first",), i == 0, ride_in, ride_out, sems)

        @pl.when(i == 0)
        def _():
            for r in ds_refs:
                r[...] = jnp.zeros_like(r)
            for nm in _P_NAMES:
                if nm in _P_BLOCKDIAG:
                    ddense[nm][...] = jnp.zeros_like(ddense[nm])
                    _expand_blockdiag(p_refs[nm], dense[nm])
                else:
                    dp_refs[nm][...] = jnp.zeros_like(dp_refs[nm])
            carry_ref[...] = jnp.zeros_like(carry_ref)

        pv = {nm: (_per_head(dense[nm]) if nm in _P_BLOCKDIAG else p_refs[nm][...]) for nm in _P_NAMES}
        dst = {name: _per_head(r) for name, r in zip(_S_NAMES, ds_refs)}
        st = {name: [r[0, h] for h in range(HEADS)] for name, r in zip(_S_NAMES, si_refs)}
        xprev8 = jnp.where(blk > 0, xprev_ref[CHUNK - 8:CHUNK, :], 0.0)
        _, vjp = jax.vjp(functools.partial(_mixer_chunk, _VJP_OPS), pv, st, pm_ref[...], xprev8)
        dp_sum, dst, dpm, dxprev8 = vjp((dab_ref[...], dst))
        reach = jnp.concatenate([jnp.zeros((SWEEP * CHUNK - 8, 512), F32), carry_ref[...]], axis=0)
        dpm_ref[:, 0:PM_XM] = dpm[:, 0:PM_XM].astype(BF16)
        dpm_ref[:, PM_XM:PM_XM + 512] = (dpm[:, PM_XM:PM_XM + 512] + reach).astype(BF16)
        dpm_ref[:, PM_XM + 512:PM_W] = dpm[:, PM_XM + 512:PM_W].astype(BF16)
        carry_ref[...] = dxprev8
        for name, r in zip(_S_NAMES, ds_refs):
            for h in range(HEADS):
                r[h] = dst[name][h]
        for nm in _P_NAMES:
            if nm in _P_BLOCKDIAG:
                for h in range(HEADS):
                    ddense[nm][h] += dp_sum[nm][h]
            else:
                dp_refs[nm][...] += dp_sum[nm]

        @pl.when(i == N_SWEEP - 1)
        def _():
            for nm in _P_BLOCKDIAG:
                _collect_blockdiag(ddense[nm], dp_refs[nm])

        _ride(rider, ("middle",), i == N_SWEEP - 2, ride_in, ride_out, sems)
        _ride(rider, ("last",), i == N_SWEEP - 1, ride_in, ride_out, sems)

    rev = lambda i: (N_SWEEP - 1 - i, 0)
    in_specs = [pl.BlockSpec((SWEEP * CHUNK, PM_W), rev),
                pl.BlockSpec((CHUNK, 512), lambda i: (jnp.maximum(SWEEP * (N_SWEEP - 1 - i) - 1, 0), PM_XM // 512)),
                pl.BlockSpec((SWEEP * CHUNK, 1024), rev)]
    for nm in _S_NAMES:
        in_specs.append(pl.BlockSpec((1,) + _S_SHAPES[nm], lambda i: (N_SWEEP - 1 - i, 0, 0, 0)))
    in_specs += [_const_spec(_P_SHAPES[nm]) for nm in _P_NAMES] + r_in
    out_specs = [pl.BlockSpec((SWEEP * CHUNK, PM_W), rev)] + [_const_spec(_P_SHAPES[nm]) for nm in _P_NAMES]
    out_shape = [jax.ShapeDtypeStruct((SEQ, PM_W), BF16)] + [jax.ShapeDtypeStruct(_P_SHAPES[nm], F32) for nm in _P_NAMES]
    res = pl.pallas_call(
        body, grid=(N_SWEEP,), in_specs=in_specs, out_specs=out_specs + r_out_specs, out_shape=out_shape + r_out_shape,
        scratch_shapes=[pltpu.VMEM(_S_SHAPES[nm], F32) for nm in _S_NAMES] + [pltpu.VMEM((8, 512), F32)]
        + [pltpu.VMEM((HEADS, 128, 128), F32) for _ in range(2 * len(_P_BLOCKDIAG))] + r_sems,
        compiler_params=_params(("arbitrary",)), name="mixer_bwd",
    )(pm, pm, dab, *states, *[p[nm] for nm in _P_NAMES], *rider_ins)
    return res[0], dict(zip(_P_NAMES, res[1:1 + n_p])), res[1 + n_p:]


def _tok(width):
    return pl.BlockSpec((TOK_TILE, width), lambda i: (i, 0))


def _once(shape):
    zeros = (0,) * len(shape)
    return pl.BlockSpec(shape, lambda i: zeros, pipeline_mode=pl.Buffered(1))


def _rms_fwd(x):
    r = lax.rsqrt(_mean(x * x) + EPS)
    return x * r, r


def _rms_bwd(dy, xn, r, g):
    gd = dy * g
    return r * (gd - xn * _mean(xn * gd))


def _in_proj(x, g_pre, wt_in):
    def body(x_ref, g_ref, wt_ref, pm_ref, gab_ref, h_ref):
        xn, _ = _rms_fwd(x_ref[...])
        h = (xn * g_ref[...]).astype(BF16)
        h_ref[...] = h
        pm_ref[:, 0:PM_XM] = _nt(h, wt_ref[0:IN_ALOW, :])
        pm_ref[:, PM_XM:PM_AL] = _nt(h, wt_ref[IN_XM:IN_GATES, :])
        pm_ref[:, PM_AL:PM_W] = _nt(h, wt_ref[IN_ALOW:IN_ALOW + 128, :])
        gab_ref[...] = _nt(h, wt_ref[IN_GATES:D_IN, :])

    return pl.pallas_call(
        body, grid=(N_TOK_TILE,),
        in_specs=[_tok(D_MODEL), _once((1, D_MODEL)), _once((D_IN, D_MODEL))],
        out_specs=[_tok(PM_W), _tok(GAB_W), _tok(D_MODEL)],
        out_shape=[jax.ShapeDtypeStruct((SEQ, PM_W), F32), jax.ShapeDtypeStruct((SEQ, GAB_W), F32),
                   jax.ShapeDtypeStruct((SEQ, D_MODEL), BF16)],
        compiler_params=_params(("arbitrary",)), name="in_proj",
    )(x, g_pre, wt_in)


def _merge_fwd(ab, gab, x, w_pa4, w_pb4, w_o, g_post):
    def body(ab_ref, gab_ref, x_ref, wpa_ref, wpb_ref, wo_ref, g_ref, x1_ref, mix_ref, mg_ref):
        a = ab_ref[:, 0:512]
        b = ab_ref[:, 512:1024]
        for j in range(N_CHIP):
            blk = slice(j * 256, (j + 1) * 256)
            ya = jnp.dot(a, wpa_ref[j], preferred_element_type=F32)
            yb = jnp.dot(b, wpb_ref[j], preferred_element_type=F32)
            sa = _sigmoid(gab_ref[:, j * 256:(j + 1) * 256])
            sb = _sigmoid(gab_ref[:, 1024 + j * 256:1024 + (j + 1) * 256])
            mg_ref[:, blk] = (sa * ya + sb * yb).astype(BF16)
        mix = jnp.dot(mg_ref[...], wo_ref[...], preferred_element_type=F32)
        mix_ref[...] = mix
        mn, _ = _rms_fwd(mix)
        x1_ref[...] = x_ref[...] + mn * g_ref[...]

    return pl.pallas_call(
        body, grid=(N_TOK_TILE,),
        in_specs=[_tok(1024), _tok(GAB_W), _tok(D_MODEL), _once((N_CHIP, 512, 256)), _once((N_CHIP, 512, 256)),
                  _once((D_MODEL, D_MODEL)), _once((1, D_MODEL))],
        out_specs=[_tok(D_MODEL), _tok(D_MODEL), _tok(D_MODEL)],
        out_shape=[jax.ShapeDtypeStruct((SEQ, D_MODEL), F32), jax.ShapeDtypeStruct((SEQ, D_MODEL), F32),
                   jax.ShapeDtypeStruct((SEQ, D_MODEL), BF16)],
        compiler_params=_params(("arbitrary",)), name="merge_fwd",
    )(ab, gab, x, w_pa4, w_pb4, w_o, g_post)


def _mlp(x1, target, g_pre, g_post, w_up4, w_down):
    def body(x1_ref, t_ref, gpre_ref, gpost_ref, wup_ref, wdn_ref,
             dx1_ref, u_ref, dd_ref, h2_ref, dpre_ref, dgpost_ref, dgpre_ref, loss_ref):
        @pl.when(pl.program_id(0) == 0)
        def _():
            dgpost_ref[...] = jnp.zeros_like(dgpost_ref)
            dgpre_ref[...] = jnp.zeros_like(dgpre_ref)
            loss_ref[...] = jnp.zeros_like(loss_ref)

        x1 = x1_ref[...]
        gpre = gpre_ref[...]
        gpost = gpost_ref[...]
        xn2, r2 = _rms_fwd(x1)
        h2 = (xn2 * gpre).astype(BF16)
        h2_ref[...] = h2
        rl = []
        d = jnp.zeros((TOK_TILE, D_MODEL), F32)
        for j in range(N_CHIP):
            blk = slice(j * 1024, (j + 1) * 1024)
            r = jnp.maximum(jnp.dot(h2, wup_ref[j], preferred_element_type=F32), 0.0)
            rl.append(r)
            u = (r * r).astype(BF16)
            u_ref[:, blk] = u
            d = d + jnp.dot(u, wdn_ref[blk, :], preferred_element_type=F32)
        dn, r3 = _rms_fwd(d)
        diff = x1 + dn * gpost - t_ref[...]
        loss_ref[...] += jnp.sum(diff * diff, keepdims=True) * (0.5 / D_MODEL)
        dy = diff * (1.0 / D_MODEL)
        dgpost_ref[...] += jnp.sum(dy * dn, axis=0, keepdims=True)
        dd = _rms_bwd(dy, dn, r3, gpost).astype(BF16)
        dd_ref[...] = dd
        dh2 = jnp.zeros((TOK_TILE, D_MODEL), F32)
        for j in range(N_CHIP):
            blk = slice(j * 1024, (j + 1) * 1024)
            dpre = (_nt(dd, wdn_ref[blk, :]) * (2.0 * rl[j])).astype(BF16)
            dpre_ref[:, blk] = dpre
            dh2 = dh2 + _nt(dpre, wup_ref[j])
        dgpre_ref[...] += jnp.sum(dh2 * xn2, axis=0, keepdims=True)
        dx1_ref[...] = dy + _rms_bwd(dh2, xn2, r2, gpre)

    acc = pl.BlockSpec((1, D_MODEL), lambda i: (0, 0))
    return pl.pallas_call(
        body, grid=(N_TOK_TILE,),
        in_specs=[_tok(D_MODEL), _tok(D_MODEL), _once((1, D_MODEL)), _once((1, D_MODEL)),
                  _once((N_CHIP, D_MODEL, 1024)), _once((D_FF, D_MODEL))],
        out_specs=[_tok(D_MODEL), _tok(D_FF), _tok(D_MODEL), _tok(D_MODEL), _tok(D_FF), acc, acc,
                   pl.BlockSpec((1, 128), lambda i: (0, 0))],
        out_shape=[jax.ShapeDtypeStruct((SEQ, D_MODEL), F32), jax.ShapeDtypeStruct((SEQ, D_FF), BF16),
                   jax.ShapeDtypeStruct((SEQ, D_MODEL), BF16), jax.ShapeDtypeStruct((SEQ, D_MODEL), BF16),
                   jax.ShapeDtypeStruct((SEQ, D_FF), BF16), jax.ShapeDtypeStruct((1, D_MODEL), F32),
                   jax.ShapeDtypeStruct((1, D_MODEL), F32), jax.ShapeDtypeStruct((1, 128), F32)],
        compiler_params=_params(("arbitrary",)), name="mlp_fwd_bwd",
    )(x1, target, g_pre, g_post, w_up4, w_down)


def _merge_bwd(dx1, mix, ab, gab, w_pa4, w_pb4, w_o, g_post):
    def body(dx1_ref, mix_ref, ab_ref, gab_ref, wpa_ref, wpb_ref, wo_ref, g_ref,
             dmix_ref, dya_ref, dyb_ref, dgab_ref, dab_ref, dg_ref):
        @pl.when(pl.program_id(0) == 0)
        def _():
            dg_ref[...] = jnp.zeros_like(dg_ref)

        dx1 = dx1_ref[...]
        mn, r = _rms_fwd(mix_ref[...])
        dg_ref[...] += jnp.sum(dx1 * mn, axis=0, keepdims=True)
        dmix = _rms_bwd(dx1, mn, r, g_ref[...]).astype(BF16)
        dmix_ref[...] = dmix
        dmerged = _nt(dmix, wo_ref[...])
        a = ab_ref[:, 0:512]
        b = ab_ref[:, 512:1024]
        da = jnp.zeros((TOK_TILE, 512), F32)
        db = jnp.zeros((TOK_TILE, 512), F32)
        for j in range(N_CHIP):
            blk = slice(j * 256, (j + 1) * 256)
            blk_b = slice(1024 + j * 256, 1024 + (j + 1) * 256)
            dm = dmerged[:, blk]
            ya = jnp.dot(a, wpa_ref[j], preferred_element_type=F32)
            yb = jnp.dot(b, wpb_ref[j], preferred_element_type=F32)
            sa = _sigmoid(gab_ref[:, blk])
            sb = _sigmoid(gab_ref[:, blk_b])
            dya = (dm * sa).astype(BF16)
            dyb = (dm * sb).astype(BF16)
            dya_ref[:, blk] = dya
            dyb_ref[:, blk] = dyb
            dgab_ref[:, blk] = (dm * ya * sa * (1.0 - sa)).astype(BF16)
            dgab_ref[:, blk_b] = (dm * yb * sb * (1.0 - sb)).astype(BF16)
            da = da + _nt(dya, wpa_ref[j])
            db = db + _nt(dyb, wpb_ref[j])
        dab_ref[:, 0:512] = da
        dab_ref[:, 512:1024] = db

    return pl.pallas_call(
        body, grid=(N_TOK_TILE,),
        in_specs=[_tok(D_MODEL), _tok(D_MODEL), _tok(1024), _tok(GAB_W), _once((N_CHIP, 512, 256)),
                  _once((N_CHIP, 512, 256)), _once((D_MODEL, D_MODEL)), _once((1, D_MODEL))],
        out_specs=[_tok(D_MODEL), _tok(D_MODEL), _tok(D_MODEL), _tok(GAB_W), _tok(1024),
                   pl.BlockSpec((1, D_MODEL), lambda i: (0, 0))],
        out_shape=[jax.ShapeDtypeStruct((SEQ, D_MODEL), BF16), jax.ShapeDtypeStruct((SEQ, D_MODEL), BF16),
                   jax.ShapeDtypeStruct((SEQ, D_MODEL), BF16), jax.ShapeDtypeStruct((SEQ, GAB_W), BF16),
                   jax.ShapeDtypeStruct((SEQ, 1024), F32), jax.ShapeDtypeStruct((1, D_MODEL), F32)],
        compiler_params=_params(("arbitrary",)), name="merge_bwd",
    )(dx1, mix, ab, gab, w_pa4, w_pb4, w_o, g_post)


def _in_proj_bwd(dpm, dgab, x, dx1, g_pre, wt_in):
    def body(dpm_ref, dgab_ref, x_ref, dx1_ref, g_ref, wt_ref, dx_ref, dg_ref):
        @pl.when(pl.program_id(0) == 0)
        def _():
            dg_ref[...] = jnp.zeros_like(dg_ref)

        dh = jnp.dot(dpm_ref[:, 0:PM_XM], wt_ref[0:IN_ALOW, :], preferred_element_type=F32)
        dh = dh + jnp.dot(dpm_ref[:, PM_XM:PM_AL], wt_ref[IN_XM:IN_GATES, :], preferred_element_type=F32)
        dh = dh + jnp.dot(dpm_ref[:, PM_AL:PM_W], wt_ref[IN_ALOW:IN_ALOW + 128, :], preferred_element_type=F32)
        dh = dh + jnp.dot(dgab_ref[...], wt_ref[IN_GATES:D_IN, :], preferred_element_type=F32)
        xn, r = _rms_fwd(x_ref[...])
        dg_ref[...] += jnp.sum(dh * xn, axis=0, keepdims=True)
        dx_ref[...] = dx1_ref[...] + _rms_bwd(dh, xn, r, g_ref[...])

    return pl.pallas_call(
        body, grid=(N_TOK_TILE,),
        in_specs=[_tok(PM_W), _tok(GAB_W), _tok(D_MODEL), _tok(D_MODEL), _once((1, D_MODEL)), _once((D_IN, D_MODEL))],
        out_specs=[_tok(D_MODEL), pl.BlockSpec((1, D_MODEL), lambda i: (0, 0))],
        out_shape=[jax.ShapeDtypeStruct((SEQ, D_MODEL), F32), jax.ShapeDtypeStruct((1, D_MODEL), F32)],
        compiler_params=_params(("arbitrary",)), name="in_proj_bwd",
    )(dpm, dgab, x, dx1, g_pre, wt_in)


def _dw_in(dpm, dgab, h):
    n_pm = PM_AL // 512
    n_blk = n_pm + GAB_W // 512

    def body(dpm_ref, dgab_ref, dal_ref, h_ref, o_ref):
        i = pl.program_id(0)
        off = pl.multiple_of(i * 512 + 16 * (i >= 3).astype(jnp.int32), 16)

        @pl.when(i < n_pm)
        def _():
            o_ref[pl.ds(off, 512), :] = _tn(dpm_ref[...], h_ref[...]).astype(BF16)

        @pl.when(i >= n_pm)
        def _():
            o_ref[pl.ds(off, 512), :] = _tn(dgab_ref[...], h_ref[...]).astype(BF16)

        @pl.when(i == 0)
        def _():
            o_ref[IN_ALOW:IN_XM, :] = _tn(dal_ref[...], h_ref[...])[0:IN_XM - IN_ALOW].astype(BF16)

    return pl.pallas_call(
        body, grid=(n_blk,),
        in_specs=[pl.BlockSpec((SEQ, 512), lambda i: (0, jnp.minimum(i, n_pm - 1))),
                  pl.BlockSpec((SEQ, 512), lambda i: (0, jnp.maximum(i - n_pm, 0))),
                  pl.BlockSpec((SEQ, 128), lambda i: (0, PM_AL // 128)),
                  _once((SEQ, D_MODEL))],
        out_specs=pl.BlockSpec((D_IN, D_MODEL), lambda i: (0, 0)),
        out_shape=jax.ShapeDtypeStruct((D_IN, D_MODEL), BF16),
        compiler_params=_params(("arbitrary",)), name="dw_in",
    )(dpm, dgab, dpm, h)


def _tn_matmul(a, b, name, shards=1, tm=512):
    m, n = a.shape[1], b.shape[1]
    tm = min(tm, m)
    tn = n // shards if shards > 1 else min(n, 1024)

    def body(a_ref, b_ref, o_ref):
        o_ref[...] = _tn(a_ref[...], b_ref[...]).astype(BF16)

    if shards > 1:
        out_spec = pl.BlockSpec((None, tm, tn), lambda i, j: (j, i, 0))
        out_shape = jax.ShapeDtypeStruct((shards, m, tn), BF16)
    else:
        out_spec = pl.BlockSpec((tm, tn), lambda i, j: (i, j))
        out_shape = jax.ShapeDtypeStruct((m, n), BF16)
    return pl.pallas_call(
        body, grid=(m // tm, n // tn),
        in_specs=[pl.BlockSpec((SEQ, tm), lambda i, j: (0, i)), pl.BlockSpec((SEQ, tn), lambda i, j: (0, j))],
        out_specs=out_spec, out_shape=out_shape,
        compiler_params=_params(("arbitrary", "arbitrary")), name=name,
    )(a, b)


MESH = pl.DeviceIdType.MESH
ANY = pl.BlockSpec(memory_space=pl.ANY)
VMEM_WHOLE = pl.BlockSpec(memory_space=pltpu.VMEM)

_BIG = ("w_in", "w_pa", "w_pb", "w_o", "w_up", "w_down")
_BIG_SHARD = {"w_in": (IN_SHARD, D_MODEL), "w_pa": (512, 256), "w_pb": (512, 256), "w_o": (256, D_MODEL),
              "w_up": (D_MODEL, 1024), "w_down": (1024, D_MODEL)}
_BIG_SPLIT = {"w_in": 1, "w_pa": 0, "w_pb": 0, "w_o": 0, "w_up": 0, "w_down": 0}


def _half(ref, e, name, lead=0):
    axis = _BIG_SPLIT[name]
    size = _BIG_SHARD[name][axis] // 2
    start = pl.multiple_of(e * size, 128 if axis == 1 else 16)
    idx = [pl.ds(0, ref.shape[a]) for a in range(lead)]
    idx += [pl.ds(start, size), pl.ds(0, _BIG_SHARD[name][1])] if axis == 0 else [pl.ds(0, _BIG_SHARD[name][0]), pl.ds(start, size)]
    return ref.at[tuple(idx)]


def _half_shape(name):
    r, c = _BIG_SHARD[name]
    return (r // 2, c) if _BIG_SPLIT[name] == 0 else (r, c // 2)


def _remote(src, dst, send_sems, recv_sems, k, to):
    return pltpu.make_async_remote_copy(src_ref=src, dst_ref=dst, send_sem=send_sems.at[k], recv_sem=recv_sems.at[k],
                                        device_id=to, device_id_type=MESH)


def _mesh_place():
    x, y, c = lax.axis_index("x"), lax.axis_index("y"), lax.axis_index("c")
    return x, y, c, [(1 - x, y), (x, 1 - y), (1 - x, 1 - y)]


class _Gather:
    def __init__(self, names, small=()):
        self.names = tuple(names)
        self.nb = len(self.names)
        self.n = self.nb + len(small)
        self.n_sems = 6 * self.n
        self.out_shape = [jax.ShapeDtypeStruct((N_CHIP,) + _BIG_SHARD[nm], BF16) for nm in self.names]
        self.out_shape += [jax.ShapeDtypeStruct((N_CHIP,) + s.shape, s.dtype) for s in small]

    def _ici(self, ins, outs, ss, rs, k, j, peer, slot, c):
        if k < self.nb:
            return _remote(_half(ins[k], c, self.names[k]), _half(outs[k].at[slot], c, self.names[k]), ss, rs, 6 * k + j,
                           (*peer, c))
        return _remote(ins[k], outs[k].at[slot], ss, rs, 6 * k + j, (*peer, c))

    def _passed(self, outs, ss, rs, k, j, slot, e, sibling):
        part = _half(outs[k].at[slot], e, self.names[k])
        return _remote(part, part, ss, rs, 6 * k + 3 + j, sibling)

    def first(self, ins, outs, ss, rs):
        x, y, c, peers = _mesh_place()
        me = 2 * x + y
        for k in range(self.n):
            for j, peer in enumerate(peers):
                self._ici(ins, outs, ss, rs, k, j, peer, me, c).start()
        for k in range(self.n):
            outs[k][me] = ins[k][...]

    def middle(self, ins, outs, ss, rs):
        x, y, c, peers = _mesh_place()
        for j, (px, py) in enumerate(peers):
            for k in range(self.nb):
                self._ici(ins, outs, ss, rs, k, j, (px, py), 2 * px + py, c).wait_recv()
                self._passed(outs, ss, rs, k, j, 2 * px + py, c, (x, y, 1 - c)).start()

    def last(self, ins, outs, ss, rs):
        x, y, c, peers = _mesh_place()
        for j, (px, py) in enumerate(peers):
            for k in range(self.n):
                if k < self.nb:
                    self._passed(outs, ss, rs, k, j, 2 * px + py, 1 - c, (x, y, 1 - c)).wait_recv()
                    self._passed(outs, ss, rs, k, j, 2 * px + py, c, (x, y, 1 - c)).wait_send()
                else:
                    self._ici(ins, outs, ss, rs, k, j, (px, py), 2 * px + py, c).wait_recv()
                self._ici(ins, outs, ss, rs, k, j, (px, py), 2 * x + y, c).wait_send()


def _run_alone(rider, ins, name):
    def body(*refs):
        r_in, r_out, sems = _split(refs, len(ins), len(rider.out_shape), 2)
        rider.first(r_in, r_out, *sems)
        rider.middle(r_in, r_out, *sems)
        rider.last(r_in, r_out, *sems)

    return pl.pallas_call(
        body, in_specs=[VMEM_WHOLE] * len(ins), out_specs=[VMEM_WHOLE] * len(rider.out_shape), out_shape=rider.out_shape,
        scratch_shapes=[pltpu.SemaphoreType.DMA((rider.n_sems,)), pltpu.SemaphoreType.DMA((rider.n_sems,))],
        compiler_params=_params(), name=name,
    )(*ins)


def _presum(names, grads, name):
    n = len(grads)

    def body(*refs):
        g_refs, got_refs, stage_refs, (send_sems, recv_sems, local_sems) = _split(refs, n, n, n, 3)
        x, y, c = lax.axis_index("x"), lax.axis_index("y"), lax.axis_index("c")

        def stage(e):
            cps = [pltpu.make_async_copy(_half(g_refs[k], e, names[k], lead=1), stage_refs[k], local_sems.at[k])
                   for k in range(n)]
            for cp in cps:
                cp.start()
            return cps

        staged = stage(1 - c)
        sends = []
        for k in range(n):
            staged[k].wait()
            cp = _remote(stage_refs[k], got_refs[k], send_sems, recv_sems, k, (x, y, 1 - c))
            cp.start()
            sends.append(cp)
        for cp in sends:
            cp.wait_send()
        staged = stage(c)
        for k in range(n):
            sends[k].wait_recv()
            staged[k].wait()

            @pl.loop(0, N_CHIP)
            def _(j):
                got_refs[k][j] = (got_refs[k][j].astype(F32) + stage_refs[k][j].astype(F32)).astype(BF16)

    half = [jax.ShapeDtypeStruct((N_CHIP,) + _half_shape(nm), BF16) for nm in names]
    return pl.pallas_call(
        body, in_specs=[ANY] * n, out_specs=[VMEM_WHOLE] * n, out_shape=half,
        scratch_shapes=[pltpu.VMEM(h.shape, h.dtype) for h in half]
        + [pltpu.SemaphoreType.DMA((n,)), pltpu.SemaphoreType.DMA((n,)), pltpu.SemaphoreType.DMA((n,))],
        compiler_params=_params(), name=name,
    )(*grads)


class _SendPartials:
    def __init__(self, names, small_shape=None):
        self.n = len(names)
        self.small = small_shape is not None
        self.n_sems = 3 * self.n + 7
        self.out_shape = [jax.ShapeDtypeStruct((N_CHIP,) + _half_shape(nm), BF16) for nm in names]
        if self.small:
            self.out_shape.append(jax.ShapeDtypeStruct((N_DEV,) + small_shape, F32))

    def _piece(self, ins, outs, ss, rs, k, j, peer, src_slot, dst_slot, c):
        return _remote(ins[k].at[src_slot], outs[k].at[dst_slot], ss, rs, 3 * k + j, (*peer, c))

    def _small(self, ins, outs, ss, rs, r, other, slot):
        return _remote(ins[self.n], outs[self.n].at[slot], ss, rs, 3 * self.n + r, other)

    @staticmethod
    def _others(x, y, c):
        return [(x, y, 1 - c), (1 - x, y, c), (1 - x, y, 1 - c), (x, 1 - y, c), (x, 1 - y, 1 - c),
                (1 - x, 1 - y, c), (1 - x, 1 - y, 1 - c)]

    def first(self, ins, outs, ss, rs):
        x, y, c, peers = _mesh_place()
        me = 2 * x + y
        for k in range(self.n):
            for j, (px, py) in enumerate(peers):
                self._piece(ins, outs, ss, rs, k, j, (px, py), 2 * px + py, me, c).start()
        if self.small:
            for r, other in enumerate(self._others(x, y, c)):
                self._small(ins, outs, ss, rs, r, other, 4 * x + 2 * y + c).start()
            outs[self.n][4 * x + 2 * y + c] = ins[self.n][...]
        for k in range(self.n):
            outs[k][me] = ins[k][me]

    def middle(self, ins, outs, ss, rs):
        pass

    def last(self, ins, outs, ss, rs):
        x, y, c, peers = _mesh_place()
        me = 2 * x + y
        for k in range(self.n):
            for j, (px, py) in enumerate(peers):
                self._piece(ins, outs, ss, rs, k, j, (px, py), me, 2 * px + py, c).wait_recv()
                self._piece(ins, outs, ss, rs, k, j, (px, py), 2 * px + py, me, c).wait_send()
        if self.small:
            for r, (px, py, pc) in enumerate(self._others(x, y, c)):
                self._small(ins, outs, ss, rs, r, (px, py, pc), 4 * px + 2 * py + pc).wait_recv()
                self._small(ins, outs, ss, rs, r, (px, py, pc), 4 * x + 2 * y + c).wait_send()


def _sum_swap(names, parts):
    n = len(parts)

    def body(*refs):
        p_refs, o_refs, (send_sems, recv_sems) = _split(refs, n, n, 2)
        x, y, c = lax.axis_index("x"), lax.axis_index("y"), lax.axis_index("c")
        for e in range(2):
            @pl.when(c == e)
            def _():
                for k in range(n):
                    g = p_refs[k][0].astype(F32)
                    for s in range(1, N_CHIP):
                        g = g + p_refs[k][s].astype(F32)
                    r, cols = _half_shape(names[k])
                    if _BIG_SPLIT[names[k]] == 0:
                        o_refs[k][e * r:(e + 1) * r, :] = g
                    else:
                        o_refs[k][:, e * cols:(e + 1) * cols] = g
        sends = []
        for k in range(n):
            mine = _half(o_refs[k], c, names[k])
            cp = _remote(mine, mine, send_sems, recv_sems, k, (x, y, 1 - c))
            cp.start()
            sends.append(cp)
        for k in range(n):
            theirs = _half(o_refs[k], 1 - c, names[k])
            _remote(theirs, theirs, send_sems, recv_sems, k, (x, y, 1 - c)).wait_recv()
        for cp in sends:
            cp.wait_send()

    return pl.pallas_call(
        body, in_specs=[VMEM_WHOLE] * n, out_specs=[VMEM_WHOLE] * n,
        out_shape=[jax.ShapeDtypeStruct(_BIG_SHARD[nm], F32) for nm in names],
        scratch_shapes=[pltpu.SemaphoreType.DMA((n,)), pltpu.SemaphoreType.DMA((n,))],
        compiler_params=_params(), name="sum_swap",
    )(*parts)


def _tile(rows, cols, itemsize, budget):
    t = cols if rows % 16 else rows
    other = rows if rows % 16 else cols
    step = 256 if rows % 16 else 32
    while t % step == 0 and t * other * itemsize > budget:
        t //= 2
    return (rows, t) if rows % 16 else (t, cols)


def _adamw_math(w, g, m, v):
    m = ADAM_B1 * m + (1.0 - ADAM_B1) * g
    v = ADAM_B2 * v + (1.0 - ADAM_B2) * (g * g)
    m_hat = m / (1.0 - ADAM_B1 ** ADAM_STEP)
    v_hat = v / (1.0 - ADAM_B2 ** ADAM_STEP)
    delta = -ADAM_LR * (m_hat / (jnp.sqrt(v_hat) + ADAM_EPS) + ADAM_WD * w)
    return delta, m, v


def _adamw_big(g, w, m, v, name):
    r, c = w.shape
    tr, tc = _tile(r, c, 4, 1024 * 1024)

    def body(g_ref, w_ref, m_ref, v_ref, d_ref, nm_ref, nv_ref):
        d_ref[...], nm_ref[...], nv_ref[...] = _adamw_math(w_ref[...], g_ref[...], m_ref[...], v_ref[...])

    blk = pl.BlockSpec((tr, tc), lambda i, l: (i, l))
    return pl.pallas_call(
        body, grid=(r // tr, c // tc), in_specs=[blk, blk, blk, blk],
        out_specs=[blk, blk, blk], out_shape=[jax.ShapeDtypeStruct((r, c), F32)] * 3,
        compiler_params=_params(("arbitrary", "arbitrary")), name=name,
    )(g, w, m, v)


def _adamw_rows(g, w, m, v, name):
    r, _, c = w.shape
    tc = 128

    def body(g_ref, w_ref, m_ref, v_ref, g3_ref, d_ref, nm_ref, nv_ref):
        g = g_ref[...]
        g3_ref[:, 0, :] = g
        d_ref[:, 0, :], nm_ref[:, 0, :], nv_ref[:, 0, :] = _adamw_math(w_ref[:, 0, :], g, m_ref[:, 0, :], v_ref[:, 0, :])

    rows = pl.BlockSpec((r, 1, tc), lambda l: (0, 0, l))
    return pl.pallas_call(
        body, grid=(c // tc,), in_specs=[pl.BlockSpec((r, tc), lambda l: (0, l)), rows, rows, rows],
        out_specs=[rows] * 4, out_shape=[jax.ShapeDtypeStruct((r, 1, c), F32)] * 4,
        compiler_params=_params(("arbitrary",)), name=name,
    )(g, w, m, v)


def _sum_small(parts):
    def body(p_ref, o_ref):
        g = p_ref[0]
        for d in range(1, N_DEV):
            g = g + p_ref[d]
        o_ref[...] = g

    return pl.pallas_call(body, out_shape=jax.ShapeDtypeStruct(parts.shape[1:], F32), name="sum_small")(parts)


def _adamw_small(ws, gs, ms, vs):
    n = len(ws)

    def body(*refs):
        w_refs, g_refs, m_refs, v_refs, d_refs, nm_refs, nv_refs = _split(refs, *([n] * 7))
        for k in range(n):
            d_refs[k][...], nm_refs[k][...], nv_refs[k][...] = _adamw_math(w_refs[k][...], g_refs[k][...], m_refs[k][...],
                                                                             v_refs[k][...])

    shapes = [jax.ShapeDtypeStruct(w.shape, F32) for w in ws]
    res = pl.pallas_call(body, out_shape=shapes * 3, name="adamw_small")(*ws, *gs, *ms, *vs)
    return res[:n], res[n:2 * n], res[2 * n:]


def _pack(arrs):
    flat = jnp.concatenate([a.reshape(-1) for a in arrs])
    rows = -(-flat.shape[0] // 1024) * 8
    return jnp.pad(flat, (0, rows * 128 - flat.shape[0])).reshape(rows, 128)


def _unpack(buf, shapes):
    flat = buf.reshape(-1)
    out, off = [], 0
    for s in shapes:
        size = 1
        for d in s:
            size *= d
        out.append(flat[off:off + size].reshape(s))
        off += size
    return out


def _block_rows(w):
    return jnp.pad(w.reshape(512, 4), ((0, 0), (0, 124)))


def _cols(a4):
    return jnp.transpose(a4, (1, 0, 2)).reshape(a4.shape[1], -1)


_LATE = ("w_pa", "w_pb", "w_o", "w_up", "w_down")


def _full_weights(gathered):
    joined = {"w_in": (D_IN, D_MODEL), "w_o": (D_MODEL, D_MODEL), "w_down": (D_FF, D_MODEL)}
    return {n: (a.reshape(joined[n]) if n in joined else a) for n, a in gathered.items()}


def _local_step(x, target, w, sp, late_shards=None):
    sp = {n: (a.reshape(1, -1) if a.ndim == 1 else a) for n, a in sp.items()}
    wau = jnp.zeros((128, 256), F32).at[0:16].set(sp["w_a_up"])
    wif = jnp.zeros((1536, 128), F32).at[:, 0:8].set(sp["w_if"])
    bif = jnp.zeros((1, 128), F32).at[:, 0:8].set(sp["b_if"])
    p = {"wau": wau, "bau": sp["b_a_up"], "ggla": sp["g_gla_norm"], "cw": sp["conv_w"], "cb": sp["conv_b"],
         "wq": _block_rows(sp["w_q_ml"]), "wk": _block_rows(sp["w_k_ml"]), "wv": _block_rows(sp["w_v_ml"]),
         "wif": wif, "bif": bif, "skip": sp["ml_skip"], "gml": sp["g_ml_norm"]}

    pm, gab, h = _in_proj(x, sp["g_pre_mix"], w["w_in"])
    if late_shards is None:
        ab, *states = _mixer_fwd(pm, p)
    else:
        ab, *rest = _mixer_fwd(pm, p, _Gather(_LATE), late_shards)
        states = rest[:4]
        w = dict(w, **_full_weights(dict(zip(_LATE, rest[4:]))))
    x1, mix, merged = _merge_fwd(ab, gab, x, w["w_pa"], w["w_pb"], w["w_o"], sp["g_post_mix"])
    dx1, u, dd, h2, dpre, dg_post_mlp, dg_pre_mlp, loss = _mlp(x1, target, sp["g_pre_mlp"], sp["g_post_mlp"],
                                                                w["w_up"], w["w_down"])
    dmix, dya, dyb, dgab, dab, dg_post_mix = _merge_bwd(dx1, mix, ab, gab, w["w_pa"], w["w_pb"], w["w_o"], sp["g_post_mix"])
    big = {
        "w_pa": _tn_matmul(ab[:, 0:512], dya, "dw_pa", shards=N_CHIP),
        "w_pb": _tn_matmul(ab[:, 512:1024], dyb, "dw_pb", shards=N_CHIP),
        "w_o": _tn_matmul(merged, dmix, "dw_o"),
        "w_up": _tn_matmul(h2, dpre, "dw_up", shards=N_CHIP),
        "w_down": _tn_matmul(u, dd, "dw_down"),
    }
    if late_shards is None:
        dpm, dp, _ = _mixer_bwd(pm, dab, states, p)
    else:
        partial = _presum(_LATE, [big[n].reshape((N_CHIP,) + _BIG_SHARD[n]) for n in _LATE], "presum_late")
        dpm, dp, parts = _mixer_bwd(pm, dab, states, p, _SendPartials(_LATE), partial)
        big = dict(zip(_LATE, parts))
    dx, dg_pre_mix = _in_proj_bwd(dpm, dgab, x, dx1, sp["g_pre_mix"], w["w_in"])
    big["w_in"] = _dw_in(dpm, dgab, h)
    small = {
        "g_pre_mix": dg_pre_mix, "b_a_up": dp["bau"], "g_gla_norm": dp["ggla"], "conv_b": dp["cb"],
        "w_q_ml": dp["wq"][:, 0:4].reshape(128, 4, 4), "w_k_ml": dp["wk"][:, 0:4].reshape(128, 4, 4),
        "w_v_ml": dp["wv"][:, 0:4].reshape(128, 4, 4),
        "b_if": dp["bif"][:, 0:8], "ml_skip": dp["skip"], "g_ml_norm": dp["gml"], "g_post_mix": dg_post_mix,
        "g_pre_mlp": dg_pre_mlp, "g_post_mlp": dg_post_mlp, "w_a_up": dp["wau"][0:16], "conv_w": dp["cw"],
        "w_if": dp["wif"][:, 0:8], "loss": loss[:, 0:1],
    }
    return dx, big, small


_SMALL_REPL = ("g_pre_mix", "b_a_up", "g_gla_norm", "conv_b", "w_q_ml", "w_k_ml", "w_v_ml", "b_if", "ml_skip",
               "g_ml_norm", "g_post_mix", "g_pre_mlp", "g_post_mlp")
_SMALL_SHARDED = ("w_a_up", "conv_w", "w_if")
_SMALL_ORDER = _SMALL_REPL + _SMALL_SHARDED + ("loss",)
_WEIGHTS = ("g_pre_mix", "w_in", "w_a_up", "b_a_up", "g_gla_norm", "conv_w", "conv_b", "w_q_ml", "w_k_ml", "w_v_ml",
            "w_if", "b_if", "ml_skip", "g_ml_norm", "w_pa", "w_pb", "w_o", "g_post_mix", "g_pre_mlp", "w_up", "w_down",
            "g_post_mlp")


def _as_shard(name, a):
    return jnp.transpose(a, (2, 0, 1)) if name == "w_in" else a[0]


def _from_shard(name, a):
    return jnp.transpose(a, (1, 2, 0)) if name == "w_in" else a[None]


def kernel(x, g_pre_mix, w_in, w_a_up, b_a_up, g_gla_norm, conv_w, conv_b, w_q_ml, w_k_ml, w_v_ml, w_if, b_if, ml_skip, g_ml_norm, w_pa, w_pb, w_o, g_post_mix, g_pre_mlp, w_up, w_down, g_post_mlp, loss_target, m_g_pre_mix, m_w_in, m_w_a_up, m_b_a_up, m_g_gla_norm, m_conv_w, m_conv_b, m_w_q_ml, m_w_k_ml, m_w_v_ml, m_w_if, m_b_if, m_ml_skip, m_g_ml_norm, m_w_pa, m_w_pb, m_w_o, m_g_post_mix, m_g_pre_mlp, m_w_up, m_w_down, m_g_post_mlp, v_g_pre_mix, v_w_in, v_w_a_up, v_b_a_up, v_g_gla_norm, v_conv_w, v_conv_b, v_w_q_ml, v_w_k_ml, v_w_v_ml, v_w_if, v_b_if, v_ml_skip, v_g_ml_norm, v_w_pa, v_w_pb, v_w_o, v_g_post_mix, v_g_pre_mlp, v_w_up, v_w_down, v_g_post_mlp):
    args = dict(locals())
    wts = {n: _as_shard(n, args[n]) for n in _WEIGHTS}
    mom = {n: _as_shard(n, args["m_" + n]) for n in _WEIGHTS}
    var = {n: _as_shard(n, args["v_" + n]) for n in _WEIGHTS}
    chip = 2 * lax.axis_index("x") + lax.axis_index("y")

    first = ("w_in",) + _SMALL_SHARDED
    gathered = dict(zip(first, _run_alone(_Gather(("w_in",), [wts[n] for n in _SMALL_SHARDED]),
                                          [wts[n][:, 0, :].astype(BF16) if n == "w_in" else wts[n] for n in first],
                                          "gather_first")))
    sp = {n: wts[n] for n in _SMALL_REPL}
    sp["w_a_up"] = _cols(gathered["w_a_up"])
    sp["conv_w"] = _cols(gathered["conv_w"])
    sp["w_if"] = gathered["w_if"].reshape(1536, 8)

    dx, big, small = _local_step(x[0], loss_target[0], _full_weights({"w_in": gathered["w_in"]}), sp,
                                 late_shards=[wts[n].astype(BF16) for n in _LATE])

    small_shapes = [small[n].shape for n in _SMALL_ORDER]
    packed = _pack([small[n] for n in _SMALL_ORDER])
    partial = _presum(("w_in",), [big["w_in"].reshape((N_CHIP,) + _BIG_SHARD["w_in"])], "presum_w_in")
    parts_in, small_parts = _run_alone(_SendPartials(("w_in",), packed.shape), [*partial, packed], "send_partials")
    big["w_in"] = parts_in
    sums = _sum_swap(_BIG, [big[n] for n in _BIG])

    grads, delta, new_m, new_v = {}, {}, {}, {}
    for n, g in zip(_BIG, sums):
        if n == "w_in":
            g, d, nm, nv = _adamw_rows(g, wts[n], mom[n], var[n], "adamw_" + n)
        else:
            d, nm, nv = _adamw_big(g, wts[n], mom[n], var[n], "adamw_" + n)
        grads[n], delta[n], new_m[n], new_v[n] = (_from_shard(n, a) for a in (g, d, nm, nv))
    summed = dict(zip(_SMALL_ORDER, _unpack(_sum_small(small_parts), small_shapes)))
    loss = summed["loss"].reshape(())
    for n in _SMALL_REPL:
        grads[n] = summed[n].reshape(args[n].shape)
    grads["w_a_up"] = lax.dynamic_slice_in_dim(summed["w_a_up"], chip * 64, 64, axis=1)[None]
    grads["conv_w"] = lax.dynamic_slice_in_dim(summed["conv_w"], chip * 128, 128, axis=1)[None]
    grads["w_if"] = lax.dynamic_slice_in_dim(summed["w_if"], chip * 384, 384, axis=0)[None]
    small_names = _SMALL_REPL + _SMALL_SHARDED
    upd = _adamw_small([args[n] for n in small_names], [grads[n] for n in small_names],
                       [args["m_" + n] for n in small_names], [args["v_" + n] for n in small_names])
    for dst, arrs in zip((delta, new_m, new_v), upd):
        dst.update(zip(small_names, arrs))

    outs = [loss, dx[None]]
    for group in (grads, delta, new_m, new_v):
        outs += [group[n] for n in _WEIGHTS]
    return tuple(outs)
```

```python
import functools

import jax
import jax.numpy as jnp
from jax import lax
from jax.experimental import pallas as pl
from jax.experimental.pallas import tpu as pltpu

F32 = jnp.float32
BF16 = jnp.bfloat16

SEQ = 2048
D_MODEL = 1024
CHUNK = 64
N_CHUNK = SEQ // CHUNK
HEADS = 4
GLA_DK = 64
GLA_DV = 128
ML_DH = 128
D_FF = 4096
EPS = 1e-6
N_CHIP = 4
N_DEV = 8
TOK_TILE = 256
N_TOK_TILE = SEQ // TOK_TILE
SWEEP = 2
assert CHUNK == 64
N_SWEEP = N_CHUNK // SWEEP

PM_W = 2688
PM_XM = 1536
PM_OP = 2048
PM_AL = 2560
GAB_W = 2048
D_IN = 4624
IN_SHARD = D_IN // N_CHIP
IN_ALOW = 1536
IN_XM = 1552
IN_GATES = 2576

ADAM_LR = 0.001
ADAM_B1 = 0.9
ADAM_B2 = 0.999
ADAM_EPS = 1e-08
ADAM_WD = 0.01
ADAM_STEP = 10

VMEM_LIMIT = 56 * 1024 * 1024


def _params(sem=None):
    return pltpu.CompilerParams(dimension_semantics=sem, vmem_limit_bytes=VMEM_LIMIT)


def _dot(a, b, ca, cb):
    return lax.dot_general(a.astype(BF16), b.astype(BF16), (((ca,), (cb,)), ((), ())), preferred_element_type=F32)


def _pmm_nn(a, b):
    return _dot(a, b, 1, 0)


def _pmm_nt(a, b):
    return _dot(a, b, 1, 1)


def _pmm_tn(a, b):
    return _dot(a, b, 0, 0)


def _pcmm(c, x):
    return lax.dot_general(c, x, (((1,), (0,)), ((), ())), precision=lax.Precision.HIGHEST, preferred_element_type=F32)


@jax.custom_vjp
def _mm_nn(a, b):
    return _dot(a, b, 1, 0)


@jax.custom_vjp
def _mm_nt(a, b):
    return _dot(a, b, 1, 1)


@jax.custom_vjp
def _mm_tn(a, b):
    return _dot(a, b, 0, 0)


_mm_nn.defvjp(lambda a, b: (_dot(a, b, 1, 0), (a, b)), lambda r, g: (_mm_nt(g, r[1]), _mm_tn(r[0], g)))
_mm_nt.defvjp(lambda a, b: (_dot(a, b, 1, 1), (a, b)), lambda r, g: (_mm_nn(g, r[1]), _mm_tn(g, r[0])))
_mm_tn.defvjp(lambda a, b: (_dot(a, b, 0, 0), (a, b)), lambda r, g: (_mm_nt(r[1], g), _mm_nn(r[0], g)))


@jax.custom_vjp
def _cmm(c, x):
    return _pcmm(c, x)


_cmm.defvjp(
    lambda c, x: (_pcmm(c, x), c),
    lambda c, g: (jnp.zeros_like(c), lax.dot_general(c, g, (((0,), (0,)), ((), ())), precision=lax.Precision.HIGHEST,
                                                      preferred_element_type=F32)),
)

_PLAIN_OPS = (_pmm_nn, _pmm_nt, _pmm_tn, _pcmm)
_VJP_OPS = (_mm_nn, _mm_nt, _mm_tn, _cmm)


def _sigmoid(x):
    return 0.5 * (jnp.tanh(0.5 * x) + 1.0)


def _log_sigmoid(x):
    return jnp.minimum(x, 0.0) - jnp.log(1.0 + jnp.exp(-jnp.abs(x)))


def _mean(x):
    return jnp.mean(x, axis=-1, keepdims=True)


def _nt(a, b):
    return lax.dot_general(a, b, (((1,), (1,)), ((), ())), preferred_element_type=F32)


def _tn(a, b):
    return lax.dot_general(a, b, (((0,), (0,)), ((), ())), preferred_element_type=F32)


def _mixer_chunk(ops, p, st, pm, xprev8):
    mm_nn, mm_nt, mm_tn, cmm = ops
    n_rows = pm.shape[0]
    n_ch = n_rows // CHUNK
    row = lax.broadcasted_iota(jnp.int32, (n_rows, n_rows), 0)
    col = lax.broadcasted_iota(jnp.int32, (n_rows, n_rows), 1)
    tri = jnp.logical_and((row >> 6) == (col >> 6), row >= col).astype(F32)
    causal = tri[0:CHUNK, 0:CHUNK] > 0.0
    q = pm[:, 0:256]
    k = pm[:, 256:512]
    v = pm[:, 512:1024]
    g = pm[:, 1024:1536]
    xm = pm[:, PM_XM:PM_XM + 512]
    opre = pm[:, PM_OP:PM_OP + 512]
    alow = pm[:, PM_AL:PM_AL + 128]
    hs = range(HEADS)
    cs = range(n_ch)
    pairs = [(i, h) for i in cs for h in hs]
    rs = [slice(i * CHUNK, (i + 1) * CHUNK) for i in cs]
    last = [slice((i + 1) * CHUNK - 1, (i + 1) * CHUNK) for i in cs]
    s6 = [slice(h * GLA_DK, (h + 1) * GLA_DK) for h in hs]
    s12 = [slice(h * 128, (h + 1) * 128) for h in hs]

    la = _log_sigmoid(mm_nn(alow, p["wau"]) + p["bau"]) * (1.0 / 16.0)
    cum = cmm(tri, la)
    cum_last = [cum[last[i], :] for i in cs]
    to_end = jnp.concatenate([cum_last[i] - cum[rs[i], :] for i in cs], axis=0)
    e_pos = jnp.exp(cum)
    e_neg = jnp.exp(-cum)
    qs = q * (GLA_DK ** -0.5)
    qp = qs * e_pos
    qn = qs * e_neg
    kp = k * e_pos
    kn = k * e_neg
    kl = k * jnp.exp(to_end)
    dec = [jnp.exp(cum_last[i]) for i in cs]
    a_fwd = {(i, h): mm_nt(qp[rs[i], s6[h]], kn[rs[i], s6[h]]) for i, h in pairs}
    a_bwd = {(i, h): mm_nt(qn[rs[i], s6[h]], kp[rs[i], s6[h]]) for i, h in pairs}
    s_chunk = {(i, h): mm_tn(v[rs[i], s12[h]], kl[rs[i], s6[h]]) for i, h in pairs}
    mem = {(0, h): st["S"][h] for h in hs}
    for i, h in pairs:
        mem[(i + 1, h)] = mem[(i, h)] * dec[i][:, s6[h]] + s_chunk[(i, h)]
    s_new = [mem[(n_ch, h)] for h in hs]
    o_inter = {(i, h): mm_nt(qp[rs[i], s6[h]], mem[(i, h)]) for i, h in pairs}
    scores = {ih: jnp.where(causal, a_fwd[ih], a_bwd[ih]) for ih in pairs}
    o = {(i, h): mm_nn(scores[(i, h)], v[rs[i], s12[h]]) + o_inter[(i, h)] for i, h in pairs}
    o = {ih: o[ih] * lax.rsqrt(_mean(o[ih] * o[ih]) + EPS) * p["ggla"] for ih in pairs}
    gate = g * _sigmoid(g)
    out_a = {(i, h): o[(i, h)] * gate[rs[i], s12[h]] for i, h in pairs}

    xx = jnp.concatenate([xprev8, xm], axis=0)
    pre = p["cb"]
    for j in range(4):
        pre = pre + p["cw"][j:j + 1, :] * xx[5 + j:5 + j + n_rows, :]
    xc = pre * _sigmoid(pre)
    qm = [mm_nn(xc[:, s12[h]], p["wq"][h]) for h in hs]
    km = [mm_nn(xc[:, s12[h]], p["wk"][h]) for h in hs]
    vm = [mm_nn(xm[:, s12[h]], p["wv"][h]) for h in hs]
    qcat = jnp.concatenate(qm, axis=1)
    kcat = jnp.concatenate(km, axis=1)
    vcat = jnp.concatenate(vm, axis=1)
    gates = (mm_nn(qcat, p["wif"][0:512]) + mm_nn(kcat, p["wif"][512:1024]) + mm_nn(vcat, p["wif"][1024:1536])
             + p["bif"])
    lf = _log_sigmoid(gates)
    fc = cmm(tri, lf)
    gates_t = gates.T
    fc_t = fc.T
    ks = [km[h] * (ML_DH ** -0.5) for h in hs]
    qk = {(i, h): mm_nt(qm[h][rs[i]], ks[h][rs[i]]) for i, h in pairs}
    li_c = {(i, h): gates[rs[i], h:h + 1] for i, h in pairs}
    fc_c = {(i, h): fc[rs[i], 4 + h:5 + h] for i, h in pairs}
    f_last = {(i, h): fc[last[i], 4 + h:5 + h] for i, h in pairs}
    a = {ih: f_last[ih] - fc_c[ih] + li_c[ih] for ih in pairs}
    m_loc = {ih: jnp.max(a[ih], axis=0, keepdims=True) for ih in pairs}
    kw = {(i, h): ks[h][rs[i]] * jnp.exp(a[(i, h)] - m_loc[(i, h)]) for i, h in pairs}
    c_chunk = {(i, h): mm_tn(kw[(i, h)], vm[h][rs[i]]) for i, h in pairs}
    c_in = {(0, h): st["C"][h] for h in hs}
    n_in = {(0, h): st["n"][h] for h in hs}
    m_in = {(0, h): st["m"][h][:, 0:1] for h in hs}
    for i, h in pairs:
        m_nx = jnp.maximum(f_last[(i, h)] + m_in[(i, h)], m_loc[(i, h)])
        sp = jnp.exp(f_last[(i, h)] + m_in[(i, h)] - m_nx)
        sl = jnp.exp(m_loc[(i, h)] - m_nx)
        c_in[(i + 1, h)] = sp * c_in[(i, h)] + sl * c_chunk[(i, h)]
        n_in[(i + 1, h)] = sp * n_in[(i, h)] + sl * jnp.sum(kw[(i, h)], axis=0, keepdims=True)
        m_in[(i + 1, h)] = m_nx
    q_c = {(i, h): mm_nn(qm[h][rs[i]], c_in[(i, h)]) for i, h in pairs}
    log_d = {(i, h): gates_t[h:h + 1, rs[i]] - jnp.abs(fc_c[(i, h)] - fc_t[4 + h:5 + h, rs[i]]) for i, h in pairs}
    g_int = {ih: fc_c[ih] + m_in[ih] for ih in pairs}
    m_t = {ih: jnp.maximum(g_int[ih], jnp.max(log_d[ih], axis=1, keepdims=True)) for ih in pairs}
    s = {ih: qk[ih] * jnp.exp(log_d[ih] - m_t[ih]) for ih in pairs}
    scl = {ih: jnp.exp(g_int[ih] - m_t[ih]) for ih in pairs}
    num = {(i, h): mm_nn(s[(i, h)], vm[h][rs[i]]) + scl[(i, h)] * q_c[(i, h)] for i, h in pairs}
    den = {(i, h): jnp.sum(s[(i, h)], axis=1, keepdims=True)
           + scl[(i, h)] * jnp.sum(qm[h][rs[i]] * n_in[(i, h)], axis=1, keepdims=True) for i, h in pairs}
    den = {ih: jnp.maximum(jnp.abs(den[ih]), jnp.exp(-m_t[ih])) for ih in pairs}
    open_gate = _sigmoid(opre)
    hc = {(i, h): num[(i, h)] / den[(i, h)] * open_gate[rs[i], s12[h]] for i, h in pairs}
    d0 = {ih: hc[ih] - _mean(hc[ih]) for ih in pairs}
    y = {ih: d0[ih] * lax.rsqrt(_mean(d0[ih] * d0[ih]) + EPS) for ih in pairs}
    skipped = p["skip"] * xc
    out_b = {(i, h): y[(i, h)] * p["gml"][:, s12[h]] + skipped[rs[i], s12[h]] for i, h in pairs}
    ab = jnp.concatenate([jnp.concatenate([out_a[(i, h)] for h in hs] + [out_b[(i, h)] for h in hs], axis=1) for i in cs],
                         axis=0)
    new = {"S": s_new, "C": [c_in[(n_ch, h)] for h in hs], "n": [n_in[(n_ch, h)] for h in hs],
           "m": [jnp.broadcast_to(m_in[(n_ch, h)], (1, ML_DH)) for h in hs]}
    return ab, new


_P_NAMES = ("wau", "bau", "ggla", "cw", "cb", "wq", "wk", "wv", "wif", "bif", "skip", "gml")
_P_SHAPES = {
    "wau": (128, 256), "bau": (1, 256), "ggla": (1, 128), "cw": (4, 512), "cb": (1, 512),
    "wq": (512, 128), "wk": (512, 128), "wv": (512, 128),
    "wif": (1536, 128), "bif": (1, 128), "skip": (1, 512), "gml": (1, 512),
}
_P_BLOCKDIAG = ("wq", "wk", "wv")
_S_NAMES = ("S", "C", "n", "m")
_S_SHAPES = {"S": (HEADS, GLA_DV, GLA_DK), "C": (HEADS, ML_DH, ML_DH), "n": (HEADS, 1, ML_DH), "m": (HEADS, 1, ML_DH)}


def _per_head(ref):
    return [ref[h] for h in range(HEADS)]


def _block_mask():
    r = lax.broadcasted_iota(jnp.int32, (128, 128), 0)
    c = lax.broadcasted_iota(jnp.int32, (128, 128), 1)
    same_block = (r >> 2) == (c >> 2)
    spread = jnp.logical_and(r < 4, (c & 3) == r)
    return same_block.astype(F32), spread.astype(F32)


def _expand_blockdiag(w_ref, dense_ref):
    same_block, spread = _block_mask()
    for h in range(HEADS):
        tiled = _pmm_nn(w_ref[h * 128:(h + 1) * 128, :], spread)
        dense_ref[h] = tiled * same_block


def _collect_blockdiag(ddense_ref, dw_ref):
    same_block, spread = _block_mask()
    for h in range(HEADS):
        dw_ref[h * 128:(h + 1) * 128, :] = lax.dot_general(
            ddense_ref[h] * same_block, spread, (((1,), (1,)), ((), ())), precision=lax.Precision.HIGHEST,
            preferred_element_type=F32)


def _const_spec(shape):
    zeros = (0,) * len(shape)
    return pl.BlockSpec(shape, lambda i: zeros)


def _split(refs, *counts):
    out, at = [], 0
    for c in counts:
        out.append(refs[at:at + c])
        at += c
    assert at == len(refs)
    return out


def _ride(rider, phases, cond, ins, outs, sems):
    if rider is None:
        return
    lands, (send_sems, recv_sems, flush_sems) = sems[:-3], sems[-3:]

    @pl.when(cond)
    def _():
        for phase in phases:
            getattr(rider, phase)(ins, lands, send_sems, recv_sems)
        if "last" in phases:
            flush = [pltpu.make_async_copy(lands[k], outs[k], flush_sems.at[k]) for k in range(len(outs))]
            for cp in flush:
                cp.start()
            for cp in flush:
                cp.wait()


def _rider_specs(rider, rider_ins):
    if rider is None:
        return [], [], [], []
    scratch = [pltpu.VMEM(s.shape, s.dtype) for s in rider.out_shape]
    scratch += [pltpu.SemaphoreType.DMA((rider.n_sems,)), pltpu.SemaphoreType.DMA((rider.n_sems,)),
                pltpu.SemaphoreType.DMA((len(rider.out_shape),))]
    return [VMEM_WHOLE] * len(rider_ins), [ANY] * len(rider.out_shape), list(rider.out_shape), scratch


def _mixer_fwd(pm, p, rider=None, rider_ins=()):
    n_p = len(_P_NAMES)
    r_in, r_out_specs, r_out_shape, r_sems = _rider_specs(rider, rider_ins)

    def body(*refs):
        (pm_ref, xprev_ref), p_list, ride_in, (ab_ref,), so_refs, ride_out, sc_refs, dense_list, sems = _split(
            refs, 2, n_p, len(r_in), 1, 4, len(r_out_specs), 4, 3, len(r_sems))
        p_refs = dict(zip(_P_NAMES, p_list))
        dense = dict(zip(_P_BLOCKDIAG, dense_list))
        n = pl.program_id(0)
        _ride(rider, ("first",), n == 0, ride_in, ride_out, sems)

        @pl.when(n == 0)
        def _():
            for r in sc_refs:
                r[...] = jnp.zeros_like(r)
            for nm in _P_BLOCKDIAG:
                _expand_blockdiag(p_refs[nm], dense[nm])

        st = {name: _per_head(r) for name, r in zip(_S_NAMES, sc_refs)}
        pv = {nm: (_per_head(dense[nm]) if nm in _P_BLOCKDIAG else p_refs[nm][...]) for nm in _P_NAMES}
        for name, r in zip(_S_NAMES, so_refs):
            for h in range(HEADS):
                r[0, h] = st[name][h]
        xprev8 = jnp.where(n > 0, xprev_ref[CHUNK - 8:CHUNK, :], 0.0)
        ab, st = _mixer_chunk(_PLAIN_OPS, pv, st, pm_ref[...], xprev8)
        ab_ref[...] = ab.astype(BF16)
        for name, r in zip(_S_NAMES, sc_refs):
            for h in range(HEADS):
                r[h] = st[name][h]
        _ride(rider, ("middle",), n == N_SWEEP - 2, ride_in, ride_out, sems)
        _ride(rider, ("last",), n == N_SWEEP - 1, ride_in, ride_out, sems)

    in_specs = [pl.BlockSpec((SWEEP * CHUNK, PM_W), lambda i: (i, 0)),
                pl.BlockSpec((CHUNK, 512), lambda i: (jnp.maximum(SWEEP * i - 1, 0), PM_XM // 512))]
    in_specs += [_const_spec(_P_SHAPES[nm]) for nm in _P_NAMES] + r_in
    out_specs = [pl.BlockSpec((SWEEP * CHUNK, 1024), lambda i: (i, 0))]
    out_shape = [jax.ShapeDtypeStruct((SEQ, 1024), BF16)]
    for nm in _S_NAMES:
        shp = _S_SHAPES[nm]
        out_specs.append(pl.BlockSpec((1,) + shp, lambda i: (i, 0, 0, 0)))
        out_shape.append(jax.ShapeDtypeStruct((N_SWEEP,) + shp, F32))
    return pl.pallas_call(
        body, grid=(N_SWEEP,), in_specs=in_specs, out_specs=out_specs + r_out_specs, out_shape=out_shape + r_out_shape,
        scratch_shapes=[pltpu.VMEM(_S_SHAPES[nm], F32) for nm in _S_NAMES]
        + [pltpu.VMEM((HEADS, 128, 128), F32) for _ in _P_BLOCKDIAG] + r_sems,
        compiler_params=_params(("arbitrary",)), name="mixer_fwd",
    )(pm, pm, *[p[nm] for nm in _P_NAMES], *rider_ins)


def _mixer_bwd(pm, dab, states, p, rider=None, rider_ins=()):
    n_p = len(_P_NAMES)
    r_in, r_out_specs, r_out_shape, r_sems = _rider_specs(rider, rider_ins)

    def body(*refs):
        ((pm_ref, xprev_ref, dab_ref), si_refs, p_list, ride_in, (dpm_ref,), dp_list, ride_out, ds_refs, (carry_ref,),
         dense_list, ddense_list, sems) = _split(refs, 3, 4, n_p, len(r_in), 1, n_p, len(r_out_specs), 4, 1, 3, 3, len(r_sems))
        p_refs = dict(zip(_P_NAMES, p_list))
        dp_refs = dict(zip(_P_NAMES, dp_list))
        dense = dict(zip(_P_BLOCKDIAG, dense_list))
        ddense = dict(zip(_P_BLOCKDIAG, ddense_list))
        i = pl.program_id(0)
        blk = N_SWEEP - 1 - i
        _ride(rider, ("first",), i == 0, ride_in, ride_out, sems)

        @pl.when(i == 0)
        def _():
            for r in ds_refs:
                r[...] = jnp.zeros_like(r)
            for nm in _P_NAMES:
                if nm in _P_BLOCKDIAG:
                    ddense[nm][...] = jnp.zeros_like(ddense[nm])
                    _expand_blockdiag(p_refs[nm], dense[nm])
                else:
                    dp_refs[nm][...] = jnp.zeros_like(dp_refs[nm])
            carry_ref[...] = jnp.zeros_like(carry_ref)

        pv = {nm: (_per_head(dense[nm]) if nm in _P_BLOCKDIAG else p_refs[nm][...]) for nm in _P_NAMES}
        dst = {name: _per_head(r) for name, r in zip(_S_NAMES, ds_refs)}
        st = {name: [r[0, h] for h in range(HEADS)] for name, r in zip(_S_NAMES, si_refs)}
        xprev8 = jnp.where(blk > 0, xprev_ref[CHUNK - 8:CHUNK, :], 0.0)
        _, vjp = jax.vjp(functools.partial(_mixer_chunk, _VJP_OPS), pv, st, pm_ref[...], xprev8)
        dp_sum, dst, dpm, dxprev8 = vjp((dab_ref[...], dst))
        reach = jnp.concatenate([jnp.zeros((SWEEP * CHUNK - 8, 512), F32), carry_ref[...]], axis=0)
        dpm_ref[:, 0:PM_XM] = dpm[:, 0:PM_XM].astype(BF16)
        dpm_ref[:, PM_XM:PM_XM + 512] = (dpm[:, PM_XM:PM_XM + 512] + reach).astype(BF16)
        dpm_ref[:, PM_XM + 512:PM_W] = dpm[:, PM_XM + 512:PM_W].astype(BF16)
        carry_ref[...] = dxprev8
        for name, r in zip(_S_NAMES, ds_refs):
            for h in range(HEADS):
                r[h] = dst[name][h]
        for nm in _P_NAMES:
            if nm in _P_BLOCKDIAG:
                for h in range(HEADS):
                    ddense[nm][h] += dp_sum[nm][h]
            else:
                dp_refs[nm][...] += dp_sum[nm]

        @pl.when(i == N_SWEEP - 1)
        def _():
            for nm in _P_BLOCKDIAG:
                _collect_blockdiag(ddense[nm], dp_refs[nm])

        _ride(rider, ("middle",), i == N_SWEEP - 2, ride_in, ride_out, sems)
        _ride(rider, ("last",), i == N_SWEEP - 1, ride_in, ride_out, sems)

    rev = lambda i: (N_SWEEP - 1 - i, 0)
    in_specs = [pl.BlockSpec((SWEEP * CHUNK, PM_W), rev),
                pl.BlockSpec((CHUNK, 512), lambda i: (jnp.maximum(SWEEP * (N_SWEEP - 1 - i) - 1, 0), PM_XM // 512)),
                pl.BlockSpec((SWEEP * CHUNK, 1024), rev)]
    for nm in _S_NAMES:
        in_specs.append(pl.BlockSpec((1,) + _S_SHAPES[nm], lambda i: (N_SWEEP - 1 - i, 0, 0, 0)))
    in_specs += [_const_spec(_P_SHAPES[nm]) for nm in _P_NAMES] + r_in
    out_specs = [pl.BlockSpec((SWEEP * CHUNK, PM_W), rev)] + [_const_spec(_P_SHAPES[nm]) for nm in _P_NAMES]
    out_shape = [jax.ShapeDtypeStruct((SEQ, PM_W), BF16)] + [jax.ShapeDtypeStruct(_P_SHAPES[nm], F32) for nm in _P_NAMES]
    res = pl.pallas_call(
        body, grid=(N_SWEEP,), in_specs=in_specs, out_specs=out_specs + r_out_specs, out_shape=out_shape + r_out_shape,
        scratch_shapes=[pltpu.VMEM(_S_SHAPES[nm], F32) for nm in _S_NAMES] + [pltpu.VMEM((8, 512), F32)]
        + [pltpu.VMEM((HEADS, 128, 128), F32) for _ in range(2 * len(_P_BLOCKDIAG))] + r_sems,
        compiler_params=_params(("arbitrary",)), name="mixer_bwd",
    )(pm, pm, dab, *states, *[p[nm] for nm in _P_NAMES], *rider_ins)
    return res[0], dict(zip(_P_NAMES, res[1:1 + n_p])), res[1 + n_p:]


def _tok(width):
    return pl.BlockSpec((TOK_TILE, width), lambda i: (i, 0))


def _once(shape):
    zeros = (0,) * len(shape)
    return pl.BlockSpec(shape, lambda i: zeros, pipeline_mode=pl.Buffered(1))


def _rms_fwd(x):
    r = lax.rsqrt(_mean(x * x) + EPS)
    return x * r, r


def _rms_bwd(dy, xn, r, g):
    gd = dy * g
    return r * (gd - xn * _mean(xn * gd))


def _tiled_call(body, in_specs, out_specs, out_shape, args, name, rider=None, rider_ins=()):
    r_in, r_out_specs, r_out_shape, r_scratch = _rider_specs(rider, rider_ins)
    n_in, n_out = len(in_specs), len(out_specs)

    def hosted(*refs):
        ins, ride_in, outs, ride_out, scratch = _split(refs, n_in, len(r_in), n_out, len(r_out_specs), len(r_scratch))
        i = pl.program_id(0)
        _ride(rider, ("first",), i == 0, ride_in, ride_out, scratch)
        body(*ins, *outs)
        _ride(rider, ("middle",), i == N_TOK_TILE - 2, ride_in, ride_out, scratch)
        _ride(rider, ("last",), i == N_TOK_TILE - 1, ride_in, ride_out, scratch)

    res = pl.pallas_call(
        hosted, grid=(N_TOK_TILE,), in_specs=list(in_specs) + r_in, out_specs=list(out_specs) + r_out_specs,
        out_shape=list(out_shape) + r_out_shape, scratch_shapes=r_scratch,
        compiler_params=_params(("arbitrary",)), name=name,
    )(*args, *rider_ins)
    return res[:n_out], res[n_out:]


def _in_proj(x, g_pre, wt_in, rider=None, rider_ins=()):
    def body(x_ref, g_ref, wt_ref, pm_ref, gab_ref, h_ref):
        xn, _ = _rms_fwd(x_ref[...])
        h = (xn * g_ref[...]).astype(BF16)
        h_ref[...] = h
        pm_ref[:, 0:PM_XM] = _nt(h, wt_ref[0:IN_ALOW, :])
        pm_ref[:, PM_XM:PM_AL] = _nt(h, wt_ref[IN_XM:IN_GATES, :])
        pm_ref[:, PM_AL:PM_W] = _nt(h, wt_ref[IN_ALOW:IN_ALOW + 128, :])
        gab_ref[...] = _nt(h, wt_ref[IN_GATES:D_IN, :])

    return _tiled_call(
        body, [_tok(D_MODEL), _once((1, D_MODEL)), _once((D_IN, D_MODEL))], [_tok(PM_W), _tok(GAB_W), _tok(D_MODEL)],
        [jax.ShapeDtypeStruct((SEQ, PM_W), F32), jax.ShapeDtypeStruct((SEQ, GAB_W), F32),
         jax.ShapeDtypeStruct((SEQ, D_MODEL), BF16)], (x, g_pre, wt_in), "in_proj", rider, rider_ins)


def _merge_fwd(ab, gab, x, w_pa4, w_pb4, w_o, g_post, rider=None, rider_ins=()):
    def body(ab_ref, gab_ref, x_ref, wpa_ref, wpb_ref, wo_ref, g_ref, x1_ref, mix_ref, mg_ref):
        a = ab_ref[:, 0:512]
        b = ab_ref[:, 512:1024]
        for j in range(N_CHIP):
            blk = slice(j * 256, (j + 1) * 256)
            ya = jnp.dot(a, wpa_ref[j], preferred_element_type=F32)
            yb = jnp.dot(b, wpb_ref[j], preferred_element_type=F32)
            sa = _sigmoid(gab_ref[:, j * 256:(j + 1) * 256])
            sb = _sigmoid(gab_ref[:, 1024 + j * 256:1024 + (j + 1) * 256])
            mg_ref[:, blk] = (sa * ya + sb * yb).astype(BF16)
        mix = jnp.dot(mg_ref[...], wo_ref[...], preferred_element_type=F32)
        mix_ref[...] = mix
        mn, _ = _rms_fwd(mix)
        x1_ref[...] = x_ref[...] + mn * g_ref[...]

    return _tiled_call(
        body, [_tok(1024), _tok(GAB_W), _tok(D_MODEL), _once((N_CHIP, 512, 256)), _once((N_CHIP, 512, 256)),
               _once((D_MODEL, D_MODEL)), _once((1, D_MODEL))], [_tok(D_MODEL), _tok(D_MODEL), _tok(D_MODEL)],
        [jax.ShapeDtypeStruct((SEQ, D_MODEL), F32), jax.ShapeDtypeStruct((SEQ, D_MODEL), F32),
         jax.ShapeDtypeStruct((SEQ, D_MODEL), BF16)], (ab, gab, x, w_pa4, w_pb4, w_o, g_post), "merge_fwd", rider, rider_ins)


def _mlp(x1, target, g_pre, g_post, w_up4, w_down):
    def body(x1_ref, t_ref, gpre_ref, gpost_ref, wup_ref, wdn_ref,
             dx1_ref, u_ref, dd_ref, h2_ref, dpre_ref, dgpost_ref, dgpre_ref, loss_ref):
        @pl.when(pl.program_id(0) == 0)
        def _():
            dgpost_ref[...] = jnp.zeros_like(dgpost_ref)
            dgpre_ref[...] = jnp.zeros_like(dgpre_ref)
            loss_ref[...] = jnp.zeros_like(loss_ref)

        x1 = x1_ref[...]
        gpre = gpre_ref[...]
        gpost = gpost_ref[...]
        xn2, r2 = _rms_fwd(x1)
        h2 = (xn2 * gpre).astype(BF16)
        h2_ref[...] = h2
        rl = []
        d = jnp.zeros((TOK_TILE, D_MODEL), F32)
        for j in range(N_CHIP):
            blk = slice(j * 1024, (j + 1) * 1024)
            r = jnp.maximum(jnp.dot(h2, wup_ref[j], preferred_element_type=F32), 0.0)
            rl.append(r)
            u = (r * r).astype(BF16)
            u_ref[:, blk] = u
            d = d + jnp.dot(u, wdn_ref[blk, :], preferred_element_type=F32)
        dn, r3 = _rms_fwd(d)
        diff = x1 + dn * gpost - t_ref[...]
        loss_ref[...] += jnp.sum(diff * diff, keepdims=True) * (0.5 / D_MODEL)
        dy = diff * (1.0 / D_MODEL)
        dgpost_ref[...] += jnp.sum(dy * dn, axis=0, keepdims=True)
        dd = _rms_bwd(dy, dn, r3, gpost).astype(BF16)
        dd_ref[...] = dd
        dh2 = jnp.zeros((TOK_TILE, D_MODEL), F32)
        for j in range(N_CHIP):
            blk = slice(j * 1024, (j + 1) * 1024)
            dpre = (_nt(dd, wdn_ref[blk, :]) * (2.0 * rl[j])).astype(BF16)
            dpre_ref[:, blk] = dpre
            dh2 = dh2 + _nt(dpre, wup_ref[j])
        dgpre_ref[...] += jnp.sum(dh2 * xn2, axis=0, keepdims=True)
        dx1_ref[...] = dy + _rms_bwd(dh2, xn2, r2, gpre)

    acc = pl.BlockSpec((1, D_MODEL), lambda i: (0, 0))
    return pl.pallas_call(
        body, grid=(N_TOK_TILE,),
        in_specs=[_tok(D_MODEL), _tok(D_MODEL), _once((1, D_MODEL)), _once((1, D_MODEL)),
                  _once((N_CHIP, D_MODEL, 1024)), _once((D_FF, D_MODEL))],
        out_specs=[_tok(D_MODEL), _tok(D_FF), _tok(D_MODEL), _tok(D_MODEL), _tok(D_FF), acc, acc,
                   pl.BlockSpec((1, 128), lambda i: (0, 0))],
        out_shape=[jax.ShapeDtypeStruct((SEQ, D_MODEL), F32), jax.ShapeDtypeStruct((SEQ, D_FF), BF16),
                   jax.ShapeDtypeStruct((SEQ, D_MODEL), BF16), jax.ShapeDtypeStruct((SEQ, D_MODEL), BF16),
                   jax.ShapeDtypeStruct((SEQ, D_FF), BF16), jax.ShapeDtypeStruct((1, D_MODEL), F32),
                   jax.ShapeDtypeStruct((1, D_MODEL), F32), jax.ShapeDtypeStruct((1, 128), F32)],
        compiler_params=_params(("arbitrary",)), name="mlp_fwd_bwd",
    )(x1, target, g_pre, g_post, w_up4, w_down)


def _merge_bwd(dx1, mix, ab, gab, w_pa4, w_pb4, w_o, g_post):
    def body(dx1_ref, mix_ref, ab_ref, gab_ref, wpa_ref, wpb_ref, wo_ref, g_ref,
             dmix_ref, dya_ref, dyb_ref, dgab_ref, dab_ref, dg_ref):
        @pl.when(pl.program_id(0) == 0)
        def _():
            dg_ref[...] = jnp.zeros_like(dg_ref)

        dx1 = dx1_ref[...]
        mn, r = _rms_fwd(mix_ref[...])
        dg_ref[...] += jnp.sum(dx1 * mn, axis=0, keepdims=True)
        dmix = _rms_bwd(dx1, mn, r, g_ref[...]).astype(BF16)
        dmix_ref[...] = dmix
        dmerged = _nt(dmix, wo_ref[...])
        a = ab_ref[:, 0:512]
        b = ab_ref[:, 512:1024]
        da = jnp.zeros((TOK_TILE, 512), F32)
        db = jnp.zeros((TOK_TILE, 512), F32)
        for j in range(N_CHIP):
            blk = slice(j * 256, (j + 1) * 256)
            blk_b = slice(1024 + j * 256, 1024 + (j + 1) * 256)
            dm = dmerged[:, blk]
            ya = jnp.dot(a, wpa_ref[j], preferred_element_type=F32)
            yb = jnp.dot(b, wpb_ref[j], preferred_element_type=F32)
            sa = _sigmoid(gab_ref[:, blk])
            sb = _sigmoid(gab_ref[:, blk_b])
            dya = (dm * sa).astype(BF16)
            dyb = (dm * sb).astype(BF16)
            dya_ref[:, blk] = dya
            dyb_ref[:, blk] = dyb
            dgab_ref[:, blk] = (dm * ya * sa * (1.0 - sa)).astype(BF16)
            dgab_ref[:, blk_b] = (dm * yb * sb * (1.0 - sb)).astype(BF16)
            da = da + _nt(dya, wpa_ref[j])
            db = db + _nt(dyb, wpb_ref[j])
        dab_ref[:, 0:512] = da
        dab_ref[:, 512:1024] = db

    return pl.pallas_call(
        body, grid=(N_TOK_TILE,),
        in_specs=[_tok(D_MODEL), _tok(D_MODEL), _tok(1024), _tok(GAB_W), _once((N_CHIP, 512, 256)),
                  _once((N_CHIP, 512, 256)), _once((D_MODEL, D_MODEL)), _once((1, D_MODEL))],
        out_specs=[_tok(D_MODEL), _tok(D_MODEL), _tok(D_MODEL), _tok(GAB_W), _tok(1024),
                   pl.BlockSpec((1, D_MODEL), lambda i: (0, 0))],
        out_shape=[jax.ShapeDtypeStruct((SEQ, D_MODEL), BF16), jax.ShapeDtypeStruct((SEQ, D_MODEL), BF16),
                   jax.ShapeDtypeStruct((SEQ, D_MODEL), BF16), jax.ShapeDtypeStruct((SEQ, GAB_W), BF16),
                   jax.ShapeDtypeStruct((SEQ, 1024), F32), jax.ShapeDtypeStruct((1, D_MODEL), F32)],
        compiler_params=_params(("arbitrary",)), name="merge_bwd",
    )(dx1, mix, ab, gab, w_pa4, w_pb4, w_o, g_post)


def _in_proj_bwd(dpm, dgab, x, dx1, g_pre, wt_in):
    def body(dpm_ref, dgab_ref, x_ref, dx1_ref, g_ref, wt_ref, dx_ref, dg_ref):
        @pl.when(pl.program_id(0) == 0)
        def _():
            dg_ref[...] = jnp.zeros_like(dg_ref)

        dh = jnp.dot(dpm_ref[:, 0:PM_XM], wt_ref[0:IN_ALOW, :], preferred_element_type=F32)
        dh = dh + jnp.dot(dpm_ref[:, PM_XM:PM_AL], wt_ref[IN_XM:IN_GATES, :], preferred_element_type=F32)
        dh = dh + jnp.dot(dpm_ref[:, PM_AL:PM_W], wt_ref[IN_ALOW:IN_ALOW + 128, :], preferred_element_type=F32)
        dh = dh + jnp.dot(dgab_ref[...], wt_ref[IN_GATES:D_IN, :], preferred_element_type=F32)
        xn, r = _rms_fwd(x_ref[...])
        dg_ref[...] += jnp.sum(dh * xn, axis=0, keepdims=True)
        dx_ref[...] = dx1_ref[...] + _rms_bwd(dh, xn, r, g_ref[...])

    return pl.pallas_call(
        body, grid=(N_TOK_TILE,),
        in_specs=[_tok(PM_W), _tok(GAB_W), _tok(D_MODEL), _tok(D_MODEL), _once((1, D_MODEL)), _once((D_IN, D_MODEL))],
        out_specs=[_tok(D_MODEL), pl.BlockSpec((1, D_MODEL), lambda i: (0, 0))],
        out_shape=[jax.ShapeDtypeStruct((SEQ, D_MODEL), F32), jax.ShapeDtypeStruct((1, D_MODEL), F32)],
        compiler_params=_params(("arbitrary",)), name="in_proj_bwd",
    )(dpm, dgab, x, dx1, g_pre, wt_in)


def _dw_in(dpm, dgab, h):
    n_pm = PM_AL // 512
    n_blk = n_pm + GAB_W // 512

    def body(dpm_ref, dgab_ref, dal_ref, h_ref, o_ref):
        i = pl.program_id(0)
        off = pl.multiple_of(i * 512 + 16 * (i >= 3).astype(jnp.int32), 16)

        @pl.when(i < n_pm)
        def _():
            o_ref[pl.ds(off, 512), :] = _tn(dpm_ref[...], h_ref[...]).astype(BF16)

        @pl.when(i >= n_pm)
        def _():
            o_ref[pl.ds(off, 512), :] = _tn(dgab_ref[...], h_ref[...]).astype(BF16)

        @pl.when(i == 0)
        def _():
            o_ref[IN_ALOW:IN_XM, :] = _tn(dal_ref[...], h_ref[...])[0:IN_XM - IN_ALOW].astype(BF16)

    return pl.pallas_call(
        body, grid=(n_blk,),
        in_specs=[pl.BlockSpec((SEQ, 512), lambda i: (0, jnp.minimum(i, n_pm - 1))),
                  pl.BlockSpec((SEQ, 512), lambda i: (0, jnp.maximum(i - n_pm, 0))),
                  pl.BlockSpec((SEQ, 128), lambda i: (0, PM_AL // 128)),
                  _once((SEQ, D_MODEL))],
        out_specs=pl.BlockSpec((D_IN, D_MODEL), lambda i: (0, 0)),
        out_shape=jax.ShapeDtypeStruct((D_IN, D_MODEL), BF16),
        compiler_params=_params(("arbitrary",)), name="dw_in",
    )(dpm, dgab, dpm, h)


def _tn_matmul(a, b, name, shards=1, tm=512):
    m, n = a.shape[1], b.shape[1]
    tm = min(tm, m)
    tn = n // shards if shards > 1 else min(n, 1024)

    def body(a_ref, b_ref, o_ref):
        o_ref[...] = _tn(a_ref[...], b_ref[...]).astype(BF16)

    if shards > 1:
        out_spec = pl.BlockSpec((None, tm, tn), lambda i, j: (j, i, 0))
        out_shape = jax.ShapeDtypeStruct((shards, m, tn), BF16)
    else:
        out_spec = pl.BlockSpec((tm, tn), lambda i, j: (i, j))
        out_shape = jax.ShapeDtypeStruct((m, n), BF16)
    return pl.pallas_call(
        body, grid=(m // tm, n // tn),
        in_specs=[pl.BlockSpec((SEQ, tm), lambda i, j: (0, i)), pl.BlockSpec((SEQ, tn), lambda i, j: (0, j))],
        out_specs=out_spec, out_shape=out_shape,
        compiler_params=_params(("arbitrary", "arbitrary")), name=name,
    )(a, b)


MESH = pl.DeviceIdType.MESH
ANY = pl.BlockSpec(memory_space=pl.ANY)
VMEM_WHOLE = pl.BlockSpec(memory_space=pltpu.VMEM)

_BIG = ("w_in", "w_pa", "w_pb", "w_o", "w_up", "w_down")
_BIG_SHARD = {"w_in": (IN_SHARD, D_MODEL), "w_pa": (512, 256), "w_pb": (512, 256), "w_o": (256, D_MODEL),
              "w_up": (D_MODEL, 1024), "w_down": (1024, D_MODEL)}
_BIG_SPLIT = {"w_in": 1, "w_pa": 0, "w_pb": 0, "w_o": 0, "w_up": 0, "w_down": 0}


def _half(ref, e, name, lead=0):
    axis = _BIG_SPLIT[name]
    size = _BIG_SHARD[name][axis] // 2
    start = pl.multiple_of(e * size, 128 if axis == 1 else 16)
    idx = [pl.ds(0, ref.shape[a]) for a in range(lead)]
    idx += [pl.ds(start, size), pl.ds(0, _BIG_SHARD[name][1])] if axis == 0 else [pl.ds(0, _BIG_SHARD[name][0]), pl.ds(start, size)]
    return ref.at[tuple(idx)]


def _half_shape(name):
    r, c = _BIG_SHARD[name]
    return (r // 2, c) if _BIG_SPLIT[name] == 0 else (r, c // 2)


def _remote(src, dst, send_sems, recv_sems, k, to):
    return pltpu.make_async_remote_copy(src_ref=src, dst_ref=dst, send_sem=send_sems.at[k], recv_sem=recv_sems.at[k],
                                        device_id=to, device_id_type=MESH)


def _mesh_place():
    x, y, c = lax.axis_index("x"), lax.axis_index("y"), lax.axis_index("c")
    return x, y, c, [(1 - x, y), (x, 1 - y), (1 - x, 1 - y)]


class _Gather:
    def __init__(self, names, small=()):
        self.names = tuple(names)
        self.nb = len(self.names)
        self.n = self.nb + len(small)
        self.n_sems = 6 * self.n
        self.out_shape = [jax.ShapeDtypeStruct((N_CHIP,) + _BIG_SHARD[nm], BF16) for nm in self.names]
        self.out_shape += [jax.ShapeDtypeStruct((N_CHIP,) + s.shape, s.dtype) for s in small]

    def _ici(self, ins, outs, ss, rs, k, j, peer, slot, c):
        if k < self.nb:
            return _remote(_half(ins[k], c, self.names[k]), _half(outs[k].at[slot], c, self.names[k]), ss, rs, 6 * k + j,
                           (*peer, c))
        return _remote(ins[k], outs[k].at[slot], ss, rs, 6 * k + j, (*peer, c))

    def _passed(self, outs, ss, rs, k, j, slot, e, sibling):
        part = _half(outs[k].at[slot], e, self.names[k])
        return _remote(part, part, ss, rs, 6 * k + 3 + j, sibling)

    def first(self, ins, outs, ss, rs):
        x, y, c, peers = _mesh_place()
        me = 2 * x + y
        for k in range(self.n):
            for j, peer in enumerate(peers):
                self._ici(ins, outs, ss, rs, k, j, peer, me, c).start()
        for k in range(self.n):
            outs[k][me] = ins[k][...]

    def middle(self, ins, outs, ss, rs):
        x, y, c, peers = _mesh_place()
        for j, (px, py) in enumerate(peers):
            for k in range(self.nb):
                self._ici(ins, outs, ss, rs, k, j, (px, py), 2 * px + py, c).wait_recv()
                self._passed(outs, ss, rs, k, j, 2 * px + py, c, (x, y, 1 - c)).start()

    def last(self, ins, outs, ss, rs):
        x, y, c, peers = _mesh_place()
        for j, (px, py) in enumerate(peers):
            for k in range(self.n):
                if k < self.nb:
                    self._passed(outs, ss, rs, k, j, 2 * px + py, 1 - c, (x, y, 1 - c)).wait_recv()
                    self._passed(outs, ss, rs, k, j, 2 * px + py, c, (x, y, 1 - c)).wait_send()
                else:
                    self._ici(ins, outs, ss, rs, k, j, (px, py), 2 * px + py, c).wait_recv()
                self._ici(ins, outs, ss, rs, k, j, (px, py), 2 * x + y, c).wait_send()


def _run_alone(rider, ins, name):
    def body(*refs):
        r_in, r_out, sems = _split(refs, len(ins), len(rider.out_shape), 2)
        rider.first(r_in, r_out, *sems)
        rider.middle(r_in, r_out, *sems)
        rider.last(r_in, r_out, *sems)

    return pl.pallas_call(
        body, in_specs=[VMEM_WHOLE] * len(ins), out_specs=[VMEM_WHOLE] * len(rider.out_shape), out_shape=rider.out_shape,
        scratch_shapes=[pltpu.SemaphoreType.DMA((rider.n_sems,)), pltpu.SemaphoreType.DMA((rider.n_sems,))],
        compiler_params=_params(), name=name,
    )(*ins)


def _presum(names, grads, name):
    n = len(grads)

    def body(*refs):
        g_refs, got_refs, stage_refs, (send_sems, recv_sems, local_sems) = _split(refs, n, n, n, 3)
        x, y, c = lax.axis_index("x"), lax.axis_index("y"), lax.axis_index("c")

        def stage(e):
            cps = [pltpu.make_async_copy(_half(g_refs[k], e, names[k], lead=1), stage_refs[k], local_sems.at[k])
                   for k in range(n)]
            for cp in cps:
                cp.start()
            return cps

        staged = stage(1 - c)
        sends = []
        for k in range(n):
            staged[k].wait()
            cp = _remote(stage_refs[k], got_refs[k], send_sems, recv_sems, k, (x, y, 1 - c))
            cp.start()
            sends.append(cp)
        for cp in sends:
            cp.wait_send()
        staged = stage(c)
        for k in range(n):
            sends[k].wait_recv()
            staged[k].wait()

            @pl.loop(0, N_CHIP)
            def _(j):
                got_refs[k][j] = (got_refs[k][j].astype(F32) + stage_refs[k][j].astype(F32)).astype(BF16)

    half = [jax.ShapeDtypeStruct((N_CHIP,) + _half_shape(nm), BF16) for nm in names]
    return pl.pallas_call(
        body, in_specs=[ANY] * n, out_specs=[VMEM_WHOLE] * n, out_shape=half,
        scratch_shapes=[pltpu.VMEM(h.shape, h.dtype) for h in half]
        + [pltpu.SemaphoreType.DMA((n,)), pltpu.SemaphoreType.DMA((n,)), pltpu.SemaphoreType.DMA((n,))],
        compiler_params=_params(), name=name,
    )(*grads)


class _SendPartials:
    def __init__(self, names, small_shape=None):
        self.n = len(names)
        self.small = small_shape is not None
        self.n_sems = 3 * self.n + 7
        self.out_shape = [jax.ShapeDtypeStruct((N_CHIP,) + _half_shape(nm), BF16) for nm in names]
        if self.small:
            self.out_shape.append(jax.ShapeDtypeStruct((N_DEV,) + small_shape, F32))

    def _piece(self, ins, outs, ss, rs, k, j, peer, src_slot, dst_slot, c):
        return _remote(ins[k].at[src_slot], outs[k].at[dst_slot], ss, rs, 3 * k + j, (*peer, c))

    def _small(self, ins, outs, ss, rs, r, other, slot):
        return _remote(ins[self.n], outs[self.n].at[slot], ss, rs, 3 * self.n + r, other)

    @staticmethod
    def _others(x, y, c):
        return [(x, y, 1 - c), (1 - x, y, c), (1 - x, y, 1 - c), (x, 1 - y, c), (x, 1 - y, 1 - c),
                (1 - x, 1 - y, c), (1 - x, 1 - y, 1 - c)]

    def first(self, ins, outs, ss, rs):
        x, y, c, peers = _mesh_place()
        me = 2 * x + y
        for k in range(self.n):
            for j, (px, py) in enumerate(peers):
                self._piece(ins, outs, ss, rs, k, j, (px, py), 2 * px + py, me, c).start()
        if self.small:
            for r, other in enumerate(self._others(x, y, c)):
                self._small(ins, outs, ss, rs, r, other, 4 * x + 2 * y + c).start()
            outs[self.n][4 * x + 2 * y + c] = ins[self.n][...]
        for k in range(self.n):
            outs[k][me] = ins[k][me]

    def middle(self, ins, outs, ss, rs):
        pass

    def last(self, ins, outs, ss, rs):
        x, y, c, peers = _mesh_place()
        me = 2 * x + y
        for k in range(self.n):
            for j, (px, py) in enumerate(peers):
                self._piece(ins, outs, ss, rs, k, j, (px, py), me, 2 * px + py, c).wait_recv()
                self._piece(ins, outs, ss, rs, k, j, (px, py), 2 * px + py, me, c).wait_send()
        if self.small:
            for r, (px, py, pc) in enumerate(self._others(x, y, c)):
                self._small(ins, outs, ss, rs, r, (px, py, pc), 4 * px + 2 * py + pc).wait_recv()
                self._small(ins, outs, ss, rs, r, (px, py, pc), 4 * x + 2 * y + c).wait_send()


def _sum_swap(names, parts):
    n = len(parts)

    def body(*refs):
        p_refs, o_refs, (send_sems, recv_sems) = _split(refs, n, n, 2)
        x, y, c = lax.axis_index("x"), lax.axis_index("y"), lax.axis_index("c")
        for e in range(2):
            @pl.when(c == e)
            def _():
                for k in range(n):
                    g = p_refs[k][0].astype(F32)
                    for s in range(1, N_CHIP):
                        g = g + p_refs[k][s].astype(F32)
                    r, cols = _half_shape(names[k])
                    if _BIG_SPLIT[names[k]] == 0:
                        o_refs[k][e * r:(e + 1) * r, :] = g
                    else:
                        o_refs[k][:, e * cols:(e + 1) * cols] = g
        sends = []
        for k in range(n):
            mine = _half(o_refs[k], c, names[k])
            cp = _remote(mine, mine, send_sems, recv_sems, k, (x, y, 1 - c))
            cp.start()
            sends.append(cp)
        for k in range(n):
            theirs = _half(o_refs[k], 1 - c, names[k])
            _remote(theirs, theirs, send_sems, recv_sems, k, (x, y, 1 - c)).wait_recv()
        for cp in sends:
            cp.wait_send()

    return pl.pallas_call(
        body, in_specs=[VMEM_WHOLE] * n, out_specs=[VMEM_WHOLE] * n,
        out_shape=[jax.ShapeDtypeStruct(_BIG_SHARD[nm], F32) for nm in names],
        scratch_shapes=[pltpu.SemaphoreType.DMA((n,)), pltpu.SemaphoreType.DMA((n,))],
        compiler_params=_params(), name="sum_swap",
    )(*parts)


def _tile(rows, cols, itemsize, budget):
    t = cols if rows % 16 else rows
    other = rows if rows % 16 else cols
    step = 256 if rows % 16 else 32
    while t % step == 0 and t * other * itemsize > budget:
        t //= 2
    return (rows, t) if rows % 16 else (t, cols)


def _adamw_math(w, g, m, v):
    m = ADAM_B1 * m + (1.0 - ADAM_B1) * g
    v = ADAM_B2 * v + (1.0 - ADAM_B2) * (g * g)
    m_hat = m / (1.0 - ADAM_B1 ** ADAM_STEP)
    v_hat = v / (1.0 - ADAM_B2 ** ADAM_STEP)
    delta = -ADAM_LR * (m_hat / (jnp.sqrt(v_hat) + ADAM_EPS) + ADAM_WD * w)
    return delta, m, v


def _adamw_big(g, w, m, v, name):
    r, c = w.shape
    tr, tc = _tile(r, c, 4, 1024 * 1024)

    def body(g_ref, w_ref, m_ref, v_ref, d_ref, nm_ref, nv_ref):
        d_ref[...], nm_ref[...], nv_ref[...] = _adamw_math(w_ref[...], g_ref[...], m_ref[...], v_ref[...])

    blk = pl.BlockSpec((tr, tc), lambda i, l: (i, l))
    return pl.pallas_call(
        body, grid=(r // tr, c // tc), in_specs=[blk, blk, blk, blk],
        out_specs=[blk, blk, blk], out_shape=[jax.ShapeDtypeStruct((r, c), F32)] * 3,
        compiler_params=_params(("arbitrary", "arbitrary")), name=name,
    )(g, w, m, v)


def _adamw_rows(g, w, m, v, name):
    r, _, c = w.shape
    tc = 128

    def body(g_ref, w_ref, m_ref, v_ref, g3_ref, d_ref, nm_ref, nv_ref):
        g = g_ref[...]
        g3_ref[:, 0, :] = g
        d_ref[:, 0, :], nm_ref[:, 0, :], nv_ref[:, 0, :] = _adamw_math(w_ref[:, 0, :], g, m_ref[:, 0, :], v_ref[:, 0, :])

    rows = pl.BlockSpec((r, 1, tc), lambda l: (0, 0, l))
    return pl.pallas_call(
        body, grid=(c // tc,), in_specs=[pl.BlockSpec((r, tc), lambda l: (0, l)), rows, rows, rows],
        out_specs=[rows] * 4, out_shape=[jax.ShapeDtypeStruct((r, 1, c), F32)] * 4,
        compiler_params=_params(("arbitrary",)), name=name,
    )(g, w, m, v)


def _sum_small(parts):
    def body(p_ref, o_ref):
        g = p_ref[0]
        for d in range(1, N_DEV):
            g = g + p_ref[d]
        o_ref[...] = g

    return pl.pallas_call(body, out_shape=jax.ShapeDtypeStruct(parts.shape[1:], F32), name="sum_small")(parts)


def _adamw_small(ws, gs, ms, vs):
    n = len(ws)

    def body(*refs):
        w_refs, g_refs, m_refs, v_refs, d_refs, nm_refs, nv_refs = _split(refs, *([n] * 7))
        for k in range(n):
            d_refs[k][...], nm_refs[k][...], nv_refs[k][...] = _adamw_math(w_refs[k][...], g_refs[k][...], m_refs[k][...],
                                                                             v_refs[k][...])

    shapes = [jax.ShapeDtypeStruct(w.shape, F32) for w in ws]
    res = pl.pallas_call(body, out_shape=shapes * 3, name="adamw_small")(*ws, *gs, *ms, *vs)
    return res[:n], res[n:2 * n], res[2 * n:]


def _pack(arrs):
    flat = jnp.concatenate([a.reshape(-1) for a in arrs])
    rows = -(-flat.shape[0] // 1024) * 8
    return jnp.pad(flat, (0, rows * 128 - flat.shape[0])).reshape(rows, 128)


def _unpack(buf, shapes):
    flat = buf.reshape(-1)
    out, off = [], 0
    for s in shapes:
        size = 1
        for d in s:
            size *= d
        out.append(flat[off:off + size].reshape(s))
        off += size
    return out


def _block_rows(w):
    return jnp.pad(w.reshape(512, 4), ((0, 0), (0, 124)))


def _cols(a4):
    return jnp.transpose(a4, (1, 0, 2)).reshape(a4.shape[1], -1)


_LATE = ("w_pa", "w_pb", "w_o", "w_up", "w_down")
_RIDE_IN_PROJ = ("w_pa", "w_pb", "w_o")
_RIDE_MIXER = ("w_up",)
_RIDE_MERGE = ("w_down",)


def _full_weights(gathered):
    joined = {"w_in": (D_IN, D_MODEL), "w_o": (D_MODEL, D_MODEL), "w_down": (D_FF, D_MODEL)}
    return {n: (a.reshape(joined[n]) if n in joined else a) for n, a in gathered.items()}


def _local_step(x, target, w, sp, late_shards=None):
    sp = {n: (a.reshape(1, -1) if a.ndim == 1 else a) for n, a in sp.items()}
    wau = jnp.zeros((128, 256), F32).at[0:16].set(sp["w_a_up"])
    wif = jnp.zeros((1536, 128), F32).at[:, 0:8].set(sp["w_if"])
    bif = jnp.zeros((1, 128), F32).at[:, 0:8].set(sp["b_if"])
    p = {"wau": wau, "bau": sp["b_a_up"], "ggla": sp["g_gla_norm"], "cw": sp["conv_w"], "cb": sp["conv_b"],
         "wq": _block_rows(sp["w_q_ml"]), "wk": _block_rows(sp["w_k_ml"]), "wv": _block_rows(sp["w_v_ml"]),
         "wif": wif, "bif": bif, "skip": sp["ml_skip"], "gml": sp["g_ml_norm"]}

    if late_shards is None:
        (pm, gab, h), _ = _in_proj(x, sp["g_pre_mix"], w["w_in"])
        ab, *states = _mixer_fwd(pm, p)
        (x1, mix, merged), _ = _merge_fwd(ab, gab, x, w["w_pa"], w["w_pb"], w["w_o"], sp["g_post_mix"])
    else:
        shard = dict(zip(_LATE, late_shards))
        (pm, gab, h), got = _in_proj(x, sp["g_pre_mix"], w["w_in"], _Gather(_RIDE_IN_PROJ),
                                     [shard[n] for n in _RIDE_IN_PROJ])
        w = dict(w, **_full_weights(dict(zip(_RIDE_IN_PROJ, got))))
        ab, *rest = _mixer_fwd(pm, p, _Gather(_RIDE_MIXER), [shard[n] for n in _RIDE_MIXER])
        states = rest[:4]
        w.update(_full_weights(dict(zip(_RIDE_MIXER, rest[4:]))))
        (x1, mix, merged), got = _merge_fwd(ab, gab, x, w["w_pa"], w["w_pb"], w["w_o"], sp["g_post_mix"],
                                            _Gather(_RIDE_MERGE), [shard[n] for n in _RIDE_MERGE])
        w.update(_full_weights(dict(zip(_RIDE_MERGE, got))))
    dx1, u, dd, h2, dpre, dg_post_mlp, dg_pre_mlp, loss = _mlp(x1, target, sp["g_pre_mlp"], sp["g_post_mlp"],
                                                                w["w_up"], w["w_down"])
    dmix, dya, dyb, dgab, dab, dg_post_mix = _merge_bwd(dx1, mix, ab, gab, w["w_pa"], w["w_pb"], w["w_o"], sp["g_post_mix"])
    big = {
        "w_pa": _tn_matmul(ab[:, 0:512], dya, "dw_pa", shards=N_CHIP),
        "w_pb": _tn_matmul(ab[:, 512:1024], dyb, "dw_pb", shards=N_CHIP),
        "w_o": _tn_matmul(merged, dmix, "dw_o"),
        "w_up": _tn_matmul(h2, dpre, "dw_up", shards=N_CHIP),
        "w_down": _tn_matmul(u, dd, "dw_down"),
    }
    if late_shards is None:
        dpm, dp, _ = _mixer_bwd(pm, dab, states, p)
    else:
        partial = _presum(_LATE, [big[n].reshape((N_CHIP,) + _BIG_SHARD[n]) for n in _LATE], "presum_late")
        dpm, dp, parts = _mixer_bwd(pm, dab, states, p, _SendPartials(_LATE), partial)
        big = dict(zip(_LATE, parts))
    dx, dg_pre_mix = _in_proj_bwd(dpm, dgab, x, dx1, sp["g_pre_mix"], w["w_in"])
    big["w_in"] = _dw_in(dpm, dgab, h)
    small = {
        "g_pre_mix": dg_pre_mix, "b_a_up": dp["bau"], "g_gla_norm": dp["ggla"], "conv_b": dp["cb"],
        "w_q_ml": dp["wq"][:, 0:4].reshape(128, 4, 4), "w_k_ml": dp["wk"][:, 0:4].reshape(128, 4, 4),
        "w_v_ml": dp["wv"][:, 0:4].reshape(128, 4, 4),
        "b_if": dp["bif"][:, 0:8], "ml_skip": dp["skip"], "g_ml_norm": dp["gml"], "g_post_mix": dg_post_mix,
        "g_pre_mlp": dg_pre_mlp, "g_post_mlp": dg_post_mlp, "w_a_up": dp["wau"][0:16], "conv_w": dp["cw"],
        "w_if": dp["wif"][:, 0:8], "loss": loss[:, 0:1],
    }
    return dx, big, small


_SMALL_REPL = ("g_pre_mix", "b_a_up", "g_gla_norm", "conv_b", "w_q_ml", "w_k_ml", "w_v_ml", "b_if", "ml_skip",
               "g_ml_norm", "g_post_mix", "g_pre_mlp", "g_post_mlp")
_SMALL_SHARDED = ("w_a_up", "conv_w", "w_if")
_SMALL_ORDER = _SMALL_REPL + _SMALL_SHARDED + ("loss",)
_WEIGHTS = ("g_pre_mix", "w_in", "w_a_up", "b_a_up", "g_gla_norm", "conv_w", "conv_b", "w_q_ml", "w_k_ml", "w_v_ml",
            "w_if", "b_if", "ml_skip", "g_ml_norm", "w_pa", "w_pb", "w_o", "g_post_mix", "g_pre_mlp", "w_up", "w_down",
            "g_post_mlp")


def _as_shard(name, a):
    return jnp.transpose(a, (2, 0, 1)) if name == "w_in" else a[0]


def _from_shard(name, a):
    return jnp.transpose(a, (1, 2, 0)) if name == "w_in" else a[None]


def kernel(x, g_pre_mix, w_in, w_a_up, b_a_up, g_gla_norm, conv_w, conv_b, w_q_ml, w_k_ml, w_v_ml, w_if, b_if, ml_skip, g_ml_norm, w_pa, w_pb, w_o, g_post_mix, g_pre_mlp, w_up, w_down, g_post_mlp, loss_target, m_g_pre_mix, m_w_in, m_w_a_up, m_b_a_up, m_g_gla_norm, m_conv_w, m_conv_b, m_w_q_ml, m_w_k_ml, m_w_v_ml, m_w_if, m_b_if, m_ml_skip, m_g_ml_norm, m_w_pa, m_w_pb, m_w_o, m_g_post_mix, m_g_pre_mlp, m_w_up, m_w_down, m_g_post_mlp, v_g_pre_mix, v_w_in, v_w_a_up, v_b_a_up, v_g_gla_norm, v_conv_w, v_conv_b, v_w_q_ml, v_w_k_ml, v_w_v_ml, v_w_if, v_b_if, v_ml_skip, v_g_ml_norm, v_w_pa, v_w_pb, v_w_o, v_g_post_mix, v_g_pre_mlp, v_w_up, v_w_down, v_g_post_mlp):
    args = dict(locals())
    wts = {n: _as_shard(n, args[n]) for n in _WEIGHTS}
    mom = {n: _as_shard(n, args["m_" + n]) for n in _WEIGHTS}
    var = {n: _as_shard(n, args["v_" + n]) for n in _WEIGHTS}
    chip = 2 * lax.axis_index("x") + lax.axis_index("y")

    first = ("w_in",) + _SMALL_SHARDED
    gathered = dict(zip(first, _run_alone(_Gather(("w_in",), [wts[n] for n in _SMALL_SHARDED]),
                                          [wts[n][:, 0, :].astype(BF16) if n == "w_in" else wts[n] for n in first],
                                          "gather_first")))
    sp = {n: wts[n] for n in _SMALL_REPL}
    sp["w_a_up"] = _cols(gathered["w_a_up"])
    sp["conv_w"] = _cols(gathered["conv_w"])
    sp["w_if"] = gathered["w_if"].reshape(1536, 8)

    dx, big, small = _local_step(x[0], loss_target[0], _full_weights({"w_in": gathered["w_in"]}), sp,
                                 late_shards=[wts[n].astype(BF16) for n in _LATE])

    small_shapes = [small[n].shape for n in _SMALL_ORDER]
    packed = _pack([small[n] for n in _SMALL_ORDER])
    partial = _presum(("w_in",), [big["w_in"].reshape((N_CHIP,) + _BIG_SHARD["w_in"])], "presum_w_in")
    parts_in, small_parts = _run_alone(_SendPartials(("w_in",), packed.shape), [*partial, packed], "send_partials")
    big["w_in"] = parts_in
    sums = _sum_swap(_BIG, [big[n] for n in _BIG])

    grads, delta, new_m, new_v = {}, {}, {}, {}
    for n, g in zip(_BIG, sums):
        if n == "w_in":
            g, d, nm, nv = _adamw_rows(g, wts[n], mom[n], var[n], "adamw_" + n)
        else:
            d, nm, nv = _adamw_big(g, wts[n], mom[n], var[n], "adamw_" + n)
        grads[n], delta[n], new_m[n], new_v[n] = (_from_shard(n, a) for a in (g, d, nm, nv))
    summed = dict(zip(_SMALL_ORDER, _unpack(_sum_small(small_parts), small_shapes)))
    loss = summed["loss"].reshape(())
    for n in _SMALL_REPL:
        grads[n] = summed[n].reshape(args[n].shape)
    grads["w_a_up"] = lax.dynamic_slice_in_dim(summed["w_a_up"], chip * 64, 64, axis=1)[None]
    grads["conv_w"] = lax.dynamic_slice_in_dim(summed["conv_w"], chip * 128, 128, axis=1)[None]
    grads["w_if"] = lax.dynamic_slice_in_dim(summed["w_if"], chip * 384, 384, axis=0)[None]
    small_names = _SMALL_REPL + _SMALL_SHARDED
    upd = _adamw_small([args[n] for n in small_names], [grads[n] for n in small_names],
                       [args["m_" + n] for n in small_names], [args["v_" + n] for n in small_names])
    for dst, arrs in zip((delta, new_m, new_v), upd):
        dst.update(zip(small_names, arrs))

    outs = [loss, dx[None]]
    for group in (grads, delta, new_m, new_v):
        outs += [group[n] for n in _WEIGHTS]
    return tuple(outs)
```

```python
import functools

import jax
import jax.numpy as jnp
from jax import lax
from jax.experimental import pallas as pl
from jax.experimental.pallas import tpu as pltpu

F32 = jnp.float32
BF16 = jnp.bfloat16

SEQ = 2048
D_MODEL = 1024
CHUNK = 64
N_CHUNK = SEQ // CHUNK
HEADS = 4
GLA_DK = 64
GLA_DV = 128
ML_DH = 128
D_FF = 4096
EPS = 1e-6
N_CHIP = 4
N_DEV = 8
TOK_TILE = 256
N_TOK_TILE = SEQ // TOK_TILE
SWEEP = 2
assert CHUNK == 64
N_SWEEP = N_CHUNK // SWEEP

PM_W = 2688
PM_XM = 1536
PM_OP = 2048
PM_AL = 2560
GAB_W = 2048
D_IN = 4624
IN_SHARD = D_IN // N_CHIP
IN_ALOW = 1536
IN_XM = 1552
IN_GATES = 2576

ADAM_LR = 0.001
ADAM_B1 = 0.9
ADAM_B2 = 0.999
ADAM_EPS = 1e-08
ADAM_WD = 0.01
ADAM_STEP = 10

VMEM_LIMIT = 56 * 1024 * 1024


def _params(sem=None):
    return pltpu.CompilerParams(dimension_semantics=sem, vmem_limit_bytes=VMEM_LIMIT)


def _dot(a, b, ca, cb):
    return lax.dot_general(a.astype(BF16), b.astype(BF16), (((ca,), (cb,)), ((), ())), preferred_element_type=F32)


def _pmm_nn(a, b):
    return _dot(a, b, 1, 0)


def _pmm_nt(a, b):
    return _dot(a, b, 1, 1)


def _pmm_tn(a, b):
    return _dot(a, b, 0, 0)


def _pcmm(c, x):
    return lax.dot_general(c, x, (((1,), (0,)), ((), ())), precision=lax.Precision.HIGHEST, preferred_element_type=F32)


@jax.custom_vjp
def _mm_nn(a, b):
    return _dot(a, b, 1, 0)


@jax.custom_vjp
def _mm_nt(a, b):
    return _dot(a, b, 1, 1)


@jax.custom_vjp
def _mm_tn(a, b):
    return _dot(a, b, 0, 0)


_mm_nn.defvjp(lambda a, b: (_dot(a, b, 1, 0), (a, b)), lambda r, g: (_mm_nt(g, r[1]), _mm_tn(r[0], g)))
_mm_nt.defvjp(lambda a, b: (_dot(a, b, 1, 1), (a, b)), lambda r, g: (_mm_nn(g, r[1]), _mm_tn(g, r[0])))
_mm_tn.defvjp(lambda a, b: (_dot(a, b, 0, 0), (a, b)), lambda r, g: (_mm_nt(r[1], g), _mm_nn(r[0], g)))


@jax.custom_vjp
def _cmm(c, x):
    return _pcmm(c, x)


_cmm.defvjp(
    lambda c, x: (_pcmm(c, x), c),
    lambda c, g: (jnp.zeros_like(c), lax.dot_general(c, g, (((0,), (0,)), ((), ())), precision=lax.Precision.HIGHEST,
                                                      preferred_element_type=F32)),
)

_PLAIN_OPS = (_pmm_nn, _pmm_nt, _pmm_tn, _pcmm)
_VJP_OPS = (_mm_nn, _mm_nt, _mm_tn, _cmm)


def _sigmoid(x):
    return 0.5 * (jnp.tanh(0.5 * x) + 1.0)


def _log_sigmoid(x):
    return jnp.minimum(x, 0.0) - jnp.log(1.0 + jnp.exp(-jnp.abs(x)))


def _mean(x):
    return jnp.mean(x, axis=-1, keepdims=True)


def _nt(a, b):
    return lax.dot_general(a, b, (((1,), (1,)), ((), ())), preferred_element_type=F32)


def _tn(a, b):
    return lax.dot_general(a, b, (((0,), (0,)), ((), ())), preferred_element_type=F32)


def _mixer_chunk(ops, p, st, pm, xprev8):
    mm_nn, mm_nt, mm_tn, cmm = ops
    n_rows = pm.shape[0]
    n_ch = n_rows // CHUNK
    row = lax.broadcasted_iota(jnp.int32, (n_rows, n_rows), 0)
    col = lax.broadcasted_iota(jnp.int32, (n_rows, n_rows), 1)
    tri = jnp.logical_and((row >> 6) == (col >> 6), row >= col).astype(F32)
    causal = tri[0:CHUNK, 0:CHUNK] > 0.0
    q = pm[:, 0:256]
    k = pm[:, 256:512]
    v = pm[:, 512:1024]
    g = pm[:, 1024:1536]
    xm = pm[:, PM_XM:PM_XM + 512]
    opre = pm[:, PM_OP:PM_OP + 512]
    alow = pm[:, PM_AL:PM_AL + 128]
    hs = range(HEADS)
    cs = range(n_ch)
    pairs = [(i, h) for i in cs for h in hs]
    rs = [slice(i * CHUNK, (i + 1) * CHUNK) for i in cs]
    last = [slice((i + 1) * CHUNK - 1, (i + 1) * CHUNK) for i in cs]
    s6 = [slice(h * GLA_DK, (h + 1) * GLA_DK) for h in hs]
    s12 = [slice(h * 128, (h + 1) * 128) for h in hs]

    la = _log_sigmoid(mm_nn(alow, p["wau"]) + p["bau"]) * (1.0 / 16.0)
    cum = cmm(tri, la)
    cum_last = [cum[last[i], :] for i in cs]
    to_end = jnp.concatenate([cum_last[i] - cum[rs[i], :] for i in cs], axis=0)
    e_pos = jnp.exp(cum)
    e_neg = jnp.exp(-cum)
    qs = q * (GLA_DK ** -0.5)
    qp = qs * e_pos
    qn = qs * e_neg
    kp = k * e_pos
    kn = k * e_neg
    kl = k * jnp.exp(to_end)
    dec = [jnp.exp(cum_last[i]) for i in cs]
    a_fwd = {(i, h): mm_nt(qp[rs[i], s6[h]], kn[rs[i], s6[h]]) for i, h in pairs}
    a_bwd = {(i, h): mm_nt(qn[rs[i], s6[h]], kp[rs[i], s6[h]]) for i, h in pairs}
    s_chunk = {(i, h): mm_tn(v[rs[i], s12[h]], kl[rs[i], s6[h]]) for i, h in pairs}
    mem = {(0, h): st["S"][h] for h in hs}
    for i, h in pairs:
        mem[(i + 1, h)] = mem[(i, h)] * dec[i][:, s6[h]] + s_chunk[(i, h)]
    s_new = [mem[(n_ch, h)] for h in hs]
    o_inter = {(i, h): mm_nt(qp[rs[i], s6[h]], mem[(i, h)]) for i, h in pairs}
    scores = {ih: jnp.where(causal, a_fwd[ih], a_bwd[ih]) for ih in pairs}
    o = {(i, h): mm_nn(scores[(i, h)], v[rs[i], s12[h]]) + o_inter[(i, h)] for i, h in pairs}
    o = {ih: o[ih] * lax.rsqrt(_mean(o[ih] * o[ih]) + EPS) * p["ggla"] for ih in pairs}
    gate = g * _sigmoid(g)
    out_a = {(i, h): o[(i, h)] * gate[rs[i], s12[h]] for i, h in pairs}

    xx = jnp.concatenate([xprev8, xm], axis=0)
    pre = p["cb"]
    for j in range(4):
        pre = pre + p["cw"][j:j + 1, :] * xx[5 + j:5 + j + n_rows, :]
    xc = pre * _sigmoid(pre)
    qm = [mm_nn(xc[:, s12[h]], p["wq"][h]) for h in hs]
    km = [mm_nn(xc[:, s12[h]], p["wk"][h]) for h in hs]
    vm = [mm_nn(xm[:, s12[h]], p["wv"][h]) for h in hs]
    qcat = jnp.concatenate(qm, axis=1)
    kcat = jnp.concatenate(km, axis=1)
    vcat = jnp.concatenate(vm, axis=1)
    gates = (mm_nn(qcat, p["wif"][0:512]) + mm_nn(kcat, p["wif"][512:1024]) + mm_nn(vcat, p["wif"][1024:1536])
             + p["bif"])
    lf = _log_sigmoid(gates)
    fc = cmm(tri, lf)
    gates_t = gates.T
    fc_t = fc.T
    ks = [km[h] * (ML_DH ** -0.5) for h in hs]
    qk = {(i, h): mm_nt(qm[h][rs[i]], ks[h][rs[i]]) for i, h in pairs}
    li_c = {(i, h): gates[rs[i], h:h + 1] for i, h in pairs}
    fc_c = {(i, h): fc[rs[i], 4 + h:5 + h] for i, h in pairs}
    f_last = {(i, h): fc[last[i], 4 + h:5 + h] for i, h in pairs}
    a = {ih: f_last[ih] - fc_c[ih] + li_c[ih] for ih in pairs}
    m_loc = {ih: jnp.max(a[ih], axis=0, keepdims=True) for ih in pairs}
    kw = {(i, h): ks[h][rs[i]] * jnp.exp(a[(i, h)] - m_loc[(i, h)]) for i, h in pairs}
    c_chunk = {(i, h): mm_tn(kw[(i, h)], vm[h][rs[i]]) for i, h in pairs}
    c_in = {(0, h): st["C"][h] for h in hs}
    n_in = {(0, h): st["n"][h] for h in hs}
    m_in = {(0, h): st["m"][h][:, 0:1] for h in hs}
    for i, h in pairs:
        m_nx = jnp.maximum(f_last[(i, h)] + m_in[(i, h)], m_loc[(i, h)])
        sp = jnp.exp(f_last[(i, h)] + m_in[(i, h)] - m_nx)
        sl = jnp.exp(m_loc[(i, h)] - m_nx)
        c_in[(i + 1, h)] = sp * c_in[(i, h)] + sl * c_chunk[(i, h)]
        n_in[(i + 1, h)] = sp * n_in[(i, h)] + sl * jnp.sum(kw[(i, h)], axis=0, keepdims=True)
        m_in[(i + 1, h)] = m_nx
    q_c = {(i, h): mm_nn(qm[h][rs[i]], c_in[(i, h)]) for i, h in pairs}
    log_d = {(i, h): gates_t[h:h + 1, rs[i]] - jnp.abs(fc_c[(i, h)] - fc_t[4 + h:5 + h, rs[i]]) for i, h in pairs}
    g_int = {ih: fc_c[ih] + m_in[ih] for ih in pairs}
    m_t = {ih: jnp.maximum(g_int[ih], jnp.max(log_d[ih], axis=1, keepdims=True)) for ih in pairs}
    s = {ih: qk[ih] * jnp.exp(log_d[ih] - m_t[ih]) for ih in pairs}
    scl = {ih: jnp.exp(g_int[ih] - m_t[ih]) for ih in pairs}
    num = {(i, h): mm_nn(s[(i, h)], vm[h][rs[i]]) + scl[(i, h)] * q_c[(i, h)] for i, h in pairs}
    den = {(i, h): jnp.sum(s[(i, h)], axis=1, keepdims=True)
           + scl[(i, h)] * jnp.sum(qm[h][rs[i]] * n_in[(i, h)], axis=1, keepdims=True) for i, h in pairs}
    den = {ih: jnp.maximum(jnp.abs(den[ih]), jnp.exp(-m_t[ih])) for ih in pairs}
    open_gate = _sigmoid(opre)
    hc = {(i, h): num[(i, h)] / den[(i, h)] * open_gate[rs[i], s12[h]] for i, h in pairs}
    d0 = {ih: hc[ih] - _mean(hc[ih]) for ih in pairs}
    y = {ih: d0[ih] * lax.rsqrt(_mean(d0[ih] * d0[ih]) + EPS) for ih in pairs}
    skipped = p["skip"] * xc
    out_b = {(i, h): y[(i, h)] * p["gml"][:, s12[h]] + skipped[rs[i], s12[h]] for i, h in pairs}
    ab = jnp.concatenate([jnp.concatenate([out_a[(i, h)] for h in hs] + [out_b[(i, h)] for h in hs], axis=1) for i in cs],
                         axis=0)
    new = {"S": s_new, "C": [c_in[(n_ch, h)] for h in hs], "n": [n_in[(n_ch, h)] for h in hs],
           "m": [jnp.broadcast_to(m_in[(n_ch, h)], (1, ML_DH)) for h in hs]}
    return ab, new


_P_NAMES = ("wau", "bau", "ggla", "cw", "cb", "wq", "wk", "wv", "wif", "bif", "skip", "gml")
_P_SHAPES = {
    "wau": (128, 256), "bau": (1, 256), "ggla": (1, 128), "cw": (4, 512), "cb": (1, 512),
    "wq": (512, 128), "wk": (512, 128), "wv": (512, 128),
    "wif": (1536, 128), "bif": (1, 128), "skip": (1, 512), "gml": (1, 512),
}
_P_BLOCKDIAG = ("wq", "wk", "wv")
_S_NAMES = ("S", "C", "n", "m")
_S_SHAPES = {"S": (HEADS, GLA_DV, GLA_DK), "C": (HEADS, ML_DH, ML_DH), "n": (HEADS, 1, ML_DH), "m": (HEADS, 1, ML_DH)}


def _per_head(ref):
    return [ref[h] for h in range(HEADS)]


def _block_mask():
    r = lax.broadcasted_iota(jnp.int32, (128, 128), 0)
    c = lax.broadcasted_iota(jnp.int32, (128, 128), 1)
    same_block = (r >> 2) == (c >> 2)
    spread = jnp.logical_and(r < 4, (c & 3) == r)
    return same_block.astype(F32), spread.astype(F32)


def _expand_blockdiag(w_ref, dense_ref):
    same_block, spread = _block_mask()
    for h in range(HEADS):
        tiled = _pmm_nn(w_ref[h * 128:(h + 1) * 128, :], spread)
        dense_ref[h] = tiled * same_block


def _collect_blockdiag(ddense_ref, dw_ref):
    same_block, spread = _block_mask()
    for h in range(HEADS):
        dw_ref[h * 128:(h + 1) * 128, :] = lax.dot_general(
            ddense_ref[h] * same_block, spread, (((1,), (1,)), ((), ())), precision=lax.Precision.HIGHEST,
            preferred_element_type=F32)


def _const_spec(shape):
    zeros = (0,) * len(shape)
    return pl.BlockSpec(shape, lambda i: zeros)


def _split(refs, *counts):
    out, at = [], 0
    for c in counts:
        out.append(refs[at:at + c])
        at += c
    assert at == len(refs)
    return out


def _ride(rider, phases, cond, ins, outs, sems):
    if rider is None:
        return
    lands, (send_sems, recv_sems, flush_sems) = sems[:-3], sems[-3:]

    @pl.when(cond)
    def _():
        for phase in phases:
            getattr(rider, phase)(ins, lands, send_sems, recv_sems)
        if "last" in phases:
            flush = [pltpu.make_async_copy(lands[k], outs[k], flush_sems.at[k]) for k in range(len(outs))]
            for cp in flush:
                cp.start()
            for cp in flush:
                cp.wait()


def _rider_specs(rider, rider_ins):
    if rider is None:
        return [], [], [], []
    scratch = [pltpu.VMEM(s.shape, s.dtype) for s in rider.out_shape]
    scratch += [pltpu.SemaphoreType.DMA((rider.n_sems,)), pltpu.SemaphoreType.DMA((rider.n_sems,)),
                pltpu.SemaphoreType.DMA((len(rider.out_shape),))]
    return [VMEM_WHOLE] * len(rider_ins), [ANY] * len(rider.out_shape), list(rider.out_shape), scratch


def _mixer_fwd(pm, p, rider=None, rider_ins=()):
    n_p = len(_P_NAMES)
    r_in, r_out_specs, r_out_shape, r_sems = _rider_specs(rider, rider_ins)

    def body(*refs):
        (pm_ref, xprev_ref), p_list, ride_in, (ab_ref,), so_refs, ride_out, sc_refs, dense_list, sems = _split(
            refs, 2, n_p, len(r_in), 1, 4, len(r_out_specs), 4, 3, len(r_sems))
        p_refs = dict(zip(_P_NAMES, p_list))
        dense = dict(zip(_P_BLOCKDIAG, dense_list))
        n = pl.program_id(0)
        _ride(rider, ("first",), n == 0, ride_in, ride_out, sems)

        @pl.when(n == 0)
        def _():
            for r in sc_refs:
                r[...] = jnp.zeros_like(r)
            for nm in _P_BLOCKDIAG:
                _expand_blockdiag(p_refs[nm], dense[nm])

        st = {name: _per_head(r) for name, r in zip(_S_NAMES, sc_refs)}
        pv = {nm: (_per_head(dense[nm]) if nm in _P_BLOCKDIAG else p_refs[nm][...]) for nm in _P_NAMES}
        for name, r in zip(_S_NAMES, so_refs):
            for h in range(HEADS):
                r[0, h] = st[name][h]
        xprev8 = jnp.where(n > 0, xprev_ref[CHUNK - 8:CHUNK, :], 0.0)
        ab, st = _mixer_chunk(_PLAIN_OPS, pv, st, pm_ref[...], xprev8)
        ab_ref[...] = ab.astype(BF16)
        for name, r in zip(_S_NAMES, sc_refs):
            for h in range(HEADS):
                r[h] = st[name][h]
        _ride(rider, ("middle",), n == N_SWEEP - 2, ride_in, ride_out, sems)
        _ride(rider, ("last",), n == N_SWEEP - 1, ride_in, ride_out, sems)

    in_specs = [pl.BlockSpec((SWEEP * CHUNK, PM_W), lambda i: (i, 0)),
                pl.BlockSpec((CHUNK, 512), lambda i: (jnp.maximum(SWEEP * i - 1, 0), PM_XM // 512))]
    in_specs += [_const_spec(_P_SHAPES[nm]) for nm in _P_NAMES] + r_in
    out_specs = [pl.BlockSpec((SWEEP * CHUNK, 1024), lambda i: (i, 0))]
    out_shape = [jax.ShapeDtypeStruct((SEQ, 1024), BF16)]
    for nm in _S_NAMES:
        shp = _S_SHAPES[nm]
        out_specs.append(pl.BlockSpec((1,) + shp, lambda i: (i, 0, 0, 0)))
        out_shape.append(jax.ShapeDtypeStruct((N_SWEEP,) + shp, F32))
    return pl.pallas_call(
        body, grid=(N_SWEEP,), in_specs=in_specs, out_specs=out_specs + r_out_specs, out_shape=out_shape + r_out_shape,
        scratch_shapes=[pltpu.VMEM(_S_SHAPES[nm], F32) for nm in _S_NAMES]
        + [pltpu.VMEM((HEADS, 128, 128), F32) for _ in _P_BLOCKDIAG] + r_sems,
        compiler_params=_params(("arbitrary",)), name="mixer_fwd",
    )(pm, pm, *[p[nm] for nm in _P_NAMES], *rider_ins)


def _mixer_bwd(pm, dab, states, p, rider=None, rider_ins=()):
    n_p = len(_P_NAMES)
    r_in, r_out_specs, r_out_shape, r_sems = _rider_specs(rider, rider_ins)

    def body(*refs):
        ((pm_ref, xprev_ref, dab_ref), si_refs, p_list, ride_in, (dpm_ref,), dp_list, ride_out, ds_refs, (carry_ref,),
         dense_list, ddense_list, sems) = _split(refs, 3, 4, n_p, len(r_in), 1, n_p, len(r_out_specs), 4, 1, 3, 3, len(r_sems))
        p_refs = dict(zip(_P_NAMES, p_list))
        dp_refs = dict(zip(_P_NAMES, dp_list))
        dense = dict(zip(_P_BLOCKDIAG, dense_list))
        ddense = dict(zip(_P_BLOCKDIAG, ddense_list))
        i = pl.program_id(0)
        blk = N_SWEEP - 1 - i
        _ride(rider, ("first",), i == 0, ride_in, ride_out, sems)

        @pl.when(i == 0)
        def _():
            for r in ds_refs:
                r[...] = jnp.zeros_like(r)
            for nm in _P_NAMES:
                if nm in _P_BLOCKDIAG:
                    ddense[nm][...] = jnp.zeros_like(ddense[nm])
                    _expand_blockdiag(p_refs[nm], dense[nm])
                else:
                    dp_refs[nm][...] = jnp.zeros_like(dp_refs[nm])
            carry_ref[...] = jnp.zeros_like(carry_ref)

        pv = {nm: (_per_head(dense[nm]) if nm in _P_BLOCKDIAG else p_refs[nm][...]) for nm in _P_NAMES}
        dst = {name: _per_head(r) for name, r in zip(_S_NAMES, ds_refs)}
        st = {name: [r[0, h] for h in range(HEADS)] for name, r in zip(_S_NAMES, si_refs)}
        xprev8 = jnp.where(blk > 0, xprev_ref[CHUNK - 8:CHUNK, :], 0.0)
        _, vjp = jax.vjp(functools.partial(_mixer_chunk, _VJP_OPS), pv, st, pm_ref[...], xprev8)
        dp_sum, dst, dpm, dxprev8 = vjp((dab_ref[...], dst))
        reach = jnp.concatenate([jnp.zeros((SWEEP * CHUNK - 8, 512), F32), carry_ref[...]], axis=0)
        dpm_ref[:, 0:PM_XM] = dpm[:, 0:PM_XM].astype(BF16)
        dpm_ref[:, PM_XM:PM_XM + 512] = (dpm[:, PM_XM:PM_XM + 512] + reach).astype(BF16)
        dpm_ref[:, PM_XM + 512:PM_W] = dpm[:, PM_XM + 512:PM_W].astype(BF16)
        carry_ref[...] = dxprev8
        for name, r in zip(_S_NAMES, ds_refs):
            for h in range(HEADS):
                r[h] = dst[name][h]
        for nm in _P_NAMES:
            if nm in _P_BLOCKDIAG:
                for h in range(HEADS):
                    ddense[nm][h] += dp_sum[nm][h]
            else:
                dp_refs[nm][...] += dp_sum[nm]

        @pl.when(i == N_SWEEP - 1)
        def _():
            for nm in _P_BLOCKDIAG:
                _collect_blockdiag(ddense[nm], dp_refs[nm])

        _ride(rider, ("middle",), i == N_SWEEP - 2, ride_in, ride_out, sems)
        _ride(rider, ("last",), i == N_SWEEP - 1, ride_in, ride_out, sems)

    rev = lambda i: (N_SWEEP - 1 - i, 0)
    in_specs = [pl.BlockSpec((SWEEP * CHUNK, PM_W), rev),
                pl.BlockSpec((CHUNK, 512), lambda i: (jnp.maximum(SWEEP * (N_SWEEP - 1 - i) - 1, 0), PM_XM // 512)),
                pl.BlockSpec((SWEEP * CHUNK, 1024), rev)]
    for nm in _S_NAMES:
        in_specs.append(pl.BlockSpec((1,) + _S_SHAPES[nm], lambda i: (N_SWEEP - 1 - i, 0, 0, 0)))
    in_specs += [_const_spec(_P_SHAPES[nm]) for nm in _P_NAMES] + r_in
    out_specs = [pl.BlockSpec((SWEEP * CHUNK, PM_W), rev)] + [_const_spec(_P_SHAPES[nm]) for nm in _P_NAMES]
    out_shape = [jax.ShapeDtypeStruct((SEQ, PM_W), BF16)] + [jax.ShapeDtypeStruct(_P_SHAPES[nm], F32) for nm in _P_NAMES]
    res = pl.pallas_call(
        body, grid=(N_SWEEP,), in_specs=in_specs, out_specs=out_specs + r_out_specs, out_shape=out_shape + r_out_shape,
        scratch_shapes=[pltpu.VMEM(_S_SHAPES[nm], F32) for nm in _S_NAMES] + [pltpu.VMEM((8, 512), F32)]
        + [pltpu.VMEM((HEADS, 128, 128), F32) for _ in range(2 * len(_P_BLOCKDIAG))] + r_sems,
        compiler_params=_params(("arbitrary",)), name="mixer_bwd",
    )(pm, pm, dab, *states, *[p[nm] for nm in _P_NAMES], *rider_ins)
    return res[0], dict(zip(_P_NAMES, res[1:1 + n_p])), res[1 + n_p:]


def _tok(width):
    return pl.BlockSpec((TOK_TILE, width), lambda i: (i, 0))


def _once(shape):
    zeros = (0,) * len(shape)
    return pl.BlockSpec(shape, lambda i: zeros, pipeline_mode=pl.Buffered(1))


def _rms_fwd(x):
    r = lax.rsqrt(_mean(x * x) + EPS)
    return x * r, r


def _rms_bwd(dy, xn, r, g):
    gd = dy * g
    return r * (gd - xn * _mean(xn * gd))


def _tiled_call(body, in_specs, out_specs, out_shape, args, name, rider=None, rider_ins=()):
    r_in, r_out_specs, r_out_shape, r_scratch = _rider_specs(rider, rider_ins)
    n_in, n_out = len(in_specs), len(out_specs)

    def hosted(*refs):
        ins, ride_in, outs, ride_out, scratch = _split(refs, n_in, len(r_in), n_out, len(r_out_specs), len(r_scratch))
        i = pl.program_id(0)
        _ride(rider, ("first",), i == 0, ride_in, ride_out, scratch)
        body(*ins, *outs)
        _ride(rider, ("middle",), i == N_TOK_TILE - 2, ride_in, ride_out, scratch)
        _ride(rider, ("last",), i == N_TOK_TILE - 1, ride_in, ride_out, scratch)

    res = pl.pallas_call(
        hosted, grid=(N_TOK_TILE,), in_specs=list(in_specs) + r_in, out_specs=list(out_specs) + r_out_specs,
        out_shape=list(out_shape) + r_out_shape, scratch_shapes=r_scratch,
        compiler_params=_params(("arbitrary",)), name=name,
    )(*args, *rider_ins)
    return res[:n_out], res[n_out:]


def _in_proj(x, g_pre, wt_in, rider=None, rider_ins=()):
    def body(x_ref, g_ref, wt_ref, pm_ref, gab_ref, h_ref):
        xn, _ = _rms_fwd(x_ref[...])
        h = (xn * g_ref[...]).astype(BF16)
        h_ref[...] = h
        pm_ref[:, 0:PM_XM] = _nt(h, wt_ref[0:IN_ALOW, :])
        pm_ref[:, PM_XM:PM_AL] = _nt(h, wt_ref[IN_XM:IN_GATES, :])
        pm_ref[:, PM_AL:PM_W] = _nt(h, wt_ref[IN_ALOW:IN_ALOW + 128, :])
        gab_ref[...] = _nt(h, wt_ref[IN_GATES:D_IN, :])

    return _tiled_call(
        body, [_tok(D_MODEL), _once((1, D_MODEL)), _once((D_IN, D_MODEL))], [_tok(PM_W), _tok(GAB_W), _tok(D_MODEL)],
        [jax.ShapeDtypeStruct((SEQ, PM_W), F32), jax.ShapeDtypeStruct((SEQ, GAB_W), F32),
         jax.ShapeDtypeStruct((SEQ, D_MODEL), BF16)], (x, g_pre, wt_in), "in_proj", rider, rider_ins)


def _merge_fwd(ab, gab, x, w_pa4, w_pb4, w_o, g_post, rider=None, rider_ins=()):
    def body(ab_ref, gab_ref, x_ref, wpa_ref, wpb_ref, wo_ref, g_ref, x1_ref, mix_ref, mg_ref):
        a = ab_ref[:, 0:512]
        b = ab_ref[:, 512:1024]
        for j in range(N_CHIP):
            blk = slice(j * 256, (j + 1) * 256)
            ya = jnp.dot(a, wpa_ref[j], preferred_element_type=F32)
            yb = jnp.dot(b, wpb_ref[j], preferred_element_type=F32)
            sa = _sigmoid(gab_ref[:, j * 256:(j + 1) * 256])
            sb = _sigmoid(gab_ref[:, 1024 + j * 256:1024 + (j + 1) * 256])
            mg_ref[:, blk] = (sa * ya + sb * yb).astype(BF16)
        mix = jnp.dot(mg_ref[...], wo_ref[...], preferred_element_type=F32)
        mix_ref[...] = mix
        mn, _ = _rms_fwd(mix)
        x1_ref[...] = x_ref[...] + mn * g_ref[...]

    return _tiled_call(
        body, [_tok(1024), _tok(GAB_W), _tok(D_MODEL), _once((N_CHIP, 512, 256)), _once((N_CHIP, 512, 256)),
               _once((D_MODEL, D_MODEL)), _once((1, D_MODEL))], [_tok(D_MODEL), _tok(D_MODEL), _tok(D_MODEL)],
        [jax.ShapeDtypeStruct((SEQ, D_MODEL), F32), jax.ShapeDtypeStruct((SEQ, D_MODEL), F32),
         jax.ShapeDtypeStruct((SEQ, D_MODEL), BF16)], (ab, gab, x, w_pa4, w_pb4, w_o, g_post), "merge_fwd", rider, rider_ins)


def _mlp(x1, target, g_pre, g_post, w_up4, w_down):
    def body(x1_ref, t_ref, gpre_ref, gpost_ref, wup_ref, wdn_ref,
             dx1_ref, u_ref, dd_ref, h2_ref, dpre_ref, dgpost_ref, dgpre_ref, loss_ref):
        @pl.when(pl.program_id(0) == 0)
        def _():
            dgpost_ref[...] = jnp.zeros_like(dgpost_ref)
            dgpre_ref[...] = jnp.zeros_like(dgpre_ref)
            loss_ref[...] = jnp.zeros_like(loss_ref)

        x1 = x1_ref[...]
        gpre = gpre_ref[...]
        gpost = gpost_ref[...]
        xn2, r2 = _rms_fwd(x1)
        h2 = (xn2 * gpre).astype(BF16)
        h2_ref[...] = h2
        rl = []
        d = jnp.zeros((TOK_TILE, D_MODEL), F32)
        for j in range(N_CHIP):
            blk = slice(j * 1024, (j + 1) * 1024)
            r = jnp.maximum(jnp.dot(h2, wup_ref[j], preferred_element_type=F32), 0.0)
            rl.append(r)
            u = (r * r).astype(BF16)
            u_ref[:, blk] = u
            d = d + jnp.dot(u, wdn_ref[blk, :], preferred_element_type=F32)
        dn, r3 = _rms_fwd(d)
        diff = x1 + dn * gpost - t_ref[...]
        loss_ref[...] += jnp.sum(diff * diff, keepdims=True) * (0.5 / D_MODEL)
        dy = diff * (1.0 / D_MODEL)
        dgpost_ref[...] += jnp.sum(dy * dn, axis=0, keepdims=True)
        dd = _rms_bwd(dy, dn, r3, gpost).astype(BF16)
        dd_ref[...] = dd
        dh2 = jnp.zeros((TOK_TILE, D_MODEL), F32)
        for j in range(N_CHIP):
            blk = slice(j * 1024, (j + 1) * 1024)
            dpre = (_nt(dd, wdn_ref[blk, :]) * (2.0 * rl[j])).astype(BF16)
            dpre_ref[:, blk] = dpre
            dh2 = dh2 + _nt(dpre, wup_ref[j])
        dgpre_ref[...] += jnp.sum(dh2 * xn2, axis=0, keepdims=True)
        dx1_ref[...] = dy + _rms_bwd(dh2, xn2, r2, gpre)

    acc = pl.BlockSpec((1, D_MODEL), lambda i: (0, 0))
    return pl.pallas_call(
        body, grid=(N_TOK_TILE,),
        in_specs=[_tok(D_MODEL), _tok(D_MODEL), _once((1, D_MODEL)), _once((1, D_MODEL)),
                  _once((N_CHIP, D_MODEL, 1024)), _once((D_FF, D_MODEL))],
        out_specs=[_tok(D_MODEL), _tok(D_FF), _tok(D_MODEL), _tok(D_MODEL), _tok(D_FF), acc, acc,
                   pl.BlockSpec((1, 128), lambda i: (0, 0))],
        out_shape=[jax.ShapeDtypeStruct((SEQ, D_MODEL), F32), jax.ShapeDtypeStruct((SEQ, D_FF), BF16),
                   jax.ShapeDtypeStruct((SEQ, D_MODEL), BF16), jax.ShapeDtypeStruct((SEQ, D_MODEL), BF16),
                   jax.ShapeDtypeStruct((SEQ, D_FF), BF16), jax.ShapeDtypeStruct((1, D_MODEL), F32),
                   jax.ShapeDtypeStruct((1, D_MODEL), F32), jax.ShapeDtypeStruct((1, 128), F32)],
        compiler_params=_params(("arbitrary",)), name="mlp_fwd_bwd",
    )(x1, target, g_pre, g_post, w_up4, w_down)


def _merge_bwd(dx1, mix, ab, gab, w_pa4, w_pb4, w_o, g_post):
    def body(dx1_ref, mix_ref, ab_ref, gab_ref, wpa_ref, wpb_ref, wo_ref, g_ref,
             dmix_ref, dya_ref, dyb_ref, dgab_ref, dab_ref, dg_ref):
        @pl.when(pl.program_id(0) == 0)
        def _():
            dg_ref[...] = jnp.zeros_like(dg_ref)

        dx1 = dx1_ref[...]
        mn, r = _rms_fwd(mix_ref[...])
        dg_ref[...] += jnp.sum(dx1 * mn, axis=0, keepdims=True)
        dmix = _rms_bwd(dx1, mn, r, g_ref[...]).astype(BF16)
        dmix_ref[...] = dmix
        dmerged = _nt(dmix, wo_ref[...])
        a = ab_ref[:, 0:512]
        b = ab_ref[:, 512:1024]
        da = jnp.zeros((TOK_TILE, 512), F32)
        db = jnp.zeros((TOK_TILE, 512), F32)
        for j in range(N_CHIP):
            blk = slice(j * 256, (j + 1) * 256)
            blk_b = slice(1024 + j * 256, 1024 + (j + 1) * 256)
            dm = dmerged[:, blk]
            ya = jnp.dot(a, wpa_ref[j], preferred_element_type=F32)
            yb = jnp.dot(b, wpb_ref[j], preferred_element_type=F32)
            sa = _sigmoid(gab_ref[:, blk])
            sb = _sigmoid(gab_ref[:, blk_b])
            dya = (dm * sa).astype(BF16)
            dyb = (dm * sb).astype(BF16)
            dya_ref[:, blk] = dya
            dyb_ref[:, blk] = dyb
            dgab_ref[:, blk] = (dm * ya * sa * (1.0 - sa)).astype(BF16)
            dgab_ref[:, blk_b] = (dm * yb * sb * (1.0 - sb)).astype(BF16)
            da = da + _nt(dya, wpa_ref[j])
            db = db + _nt(dyb, wpb_ref[j])
        dab_ref[:, 0:512] = da
        dab_ref[:, 512:1024] = db

    return pl.pallas_call(
        body, grid=(N_TOK_TILE,),
        in_specs=[_tok(D_MODEL), _tok(D_MODEL), _tok(1024), _tok(GAB_W), _once((N_CHIP, 512, 256)),
                  _once((N_CHIP, 512, 256)), _once((D_MODEL, D_MODEL)), _once((1, D_MODEL))],
        out_specs=[_tok(D_MODEL), _tok(D_MODEL), _tok(D_MODEL), _tok(GAB_W), _tok(1024),
                   pl.BlockSpec((1, D_MODEL), lambda i: (0, 0))],
        out_shape=[jax.ShapeDtypeStruct((SEQ, D_MODEL), BF16), jax.ShapeDtypeStruct((SEQ, D_MODEL), BF16),
                   jax.ShapeDtypeStruct((SEQ, D_MODEL), BF16), jax.ShapeDtypeStruct((SEQ, GAB_W), BF16),
                   jax.ShapeDtypeStruct((SEQ, 1024), F32), jax.ShapeDtypeStruct((1, D_MODEL), F32)],
        compiler_params=_params(("arbitrary",)), name="merge_bwd",
    )(dx1, mix, ab, gab, w_pa4, w_pb4, w_o, g_post)


def _in_proj_bwd(dpm, dgab, x, dx1, g_pre, wt_in, rider=None, rider_ins=()):
    def body(dpm_ref, dgab_ref, x_ref, dx1_ref, g_ref, wt_ref, dx_ref, dg_ref):
        @pl.when(pl.program_id(0) == 0)
        def _():
            dg_ref[...] = jnp.zeros_like(dg_ref)

        dh = jnp.dot(dpm_ref[:, 0:PM_XM], wt_ref[0:IN_ALOW, :], preferred_element_type=F32)
        dh = dh + jnp.dot(dpm_ref[:, PM_XM:PM_AL], wt_ref[IN_XM:IN_GATES, :], preferred_element_type=F32)
        dh = dh + jnp.dot(dpm_ref[:, PM_AL:PM_W], wt_ref[IN_ALOW:IN_ALOW + 128, :], preferred_element_type=F32)
        dh = dh + jnp.dot(dgab_ref[...], wt_ref[IN_GATES:D_IN, :], preferred_element_type=F32)
        xn, r = _rms_fwd(x_ref[...])
        dg_ref[...] += jnp.sum(dh * xn, axis=0, keepdims=True)
        dx_ref[...] = dx1_ref[...] + _rms_bwd(dh, xn, r, g_ref[...])

    return _tiled_call(
        body, [_tok(PM_W), _tok(GAB_W), _tok(D_MODEL), _tok(D_MODEL), _once((1, D_MODEL)), _once((D_IN, D_MODEL))],
        [_tok(D_MODEL), pl.BlockSpec((1, D_MODEL), lambda i: (0, 0))],
        [jax.ShapeDtypeStruct((SEQ, D_MODEL), F32), jax.ShapeDtypeStruct((1, D_MODEL), F32)],
        (dpm, dgab, x, dx1, g_pre, wt_in), "in_proj_bwd", rider, rider_ins)


def _dw_in(dpm, dgab, h):
    n_pm = PM_AL // 512
    n_blk = n_pm + GAB_W // 512

    def body(dpm_ref, dgab_ref, dal_ref, h_ref, o_ref):
        i = pl.program_id(0)
        off = pl.multiple_of(i * 512 + 16 * (i >= 3).astype(jnp.int32), 16)

        @pl.when(i < n_pm)
        def _():
            o_ref[pl.ds(off, 512), :] = _tn(dpm_ref[...], h_ref[...]).astype(BF16)

        @pl.when(i >= n_pm)
        def _():
            o_ref[pl.ds(off, 512), :] = _tn(dgab_ref[...], h_ref[...]).astype(BF16)

        @pl.when(i == 0)
        def _():
            o_ref[IN_ALOW:IN_XM, :] = _tn(dal_ref[...], h_ref[...])[0:IN_XM - IN_ALOW].astype(BF16)

    return pl.pallas_call(
        body, grid=(n_blk,),
        in_specs=[pl.BlockSpec((SEQ, 512), lambda i: (0, jnp.minimum(i, n_pm - 1))),
                  pl.BlockSpec((SEQ, 512), lambda i: (0, jnp.maximum(i - n_pm, 0))),
                  pl.BlockSpec((SEQ, 128), lambda i: (0, PM_AL // 128)),
                  _once((SEQ, D_MODEL))],
        out_specs=pl.BlockSpec((D_IN, D_MODEL), lambda i: (0, 0)),
        out_shape=jax.ShapeDtypeStruct((D_IN, D_MODEL), BF16),
        compiler_params=_params(("arbitrary",)), name="dw_in",
    )(dpm, dgab, dpm, h)


def _tn_matmul(a, b, name, shards=1, tm=512):
    m, n = a.shape[1], b.shape[1]
    tm = min(tm, m)
    tn = n // shards if shards > 1 else min(n, 1024)

    def body(a_ref, b_ref, o_ref):
        o_ref[...] = _tn(a_ref[...], b_ref[...]).astype(BF16)

    if shards > 1:
        out_spec = pl.BlockSpec((None, tm, tn), lambda i, j: (j, i, 0))
        out_shape = jax.ShapeDtypeStruct((shards, m, tn), BF16)
    else:
        out_spec = pl.BlockSpec((tm, tn), lambda i, j: (i, j))
        out_shape = jax.ShapeDtypeStruct((m, n), BF16)
    return pl.pallas_call(
        body, grid=(m // tm, n // tn),
        in_specs=[pl.BlockSpec((SEQ, tm), lambda i, j: (0, i)), pl.BlockSpec((SEQ, tn), lambda i, j: (0, j))],
        out_specs=out_spec, out_shape=out_shape,
        compiler_params=_params(("arbitrary", "arbitrary")), name=name,
    )(a, b)


MESH = pl.DeviceIdType.MESH
ANY = pl.BlockSpec(memory_space=pl.ANY)
VMEM_WHOLE = pl.BlockSpec(memory_space=pltpu.VMEM)

_BIG = ("w_in", "w_pa", "w_pb", "w_o", "w_up", "w_down")
_BIG_SHARD = {"w_in": (IN_SHARD, D_MODEL), "w_pa": (512, 256), "w_pb": (512, 256), "w_o": (256, D_MODEL),
              "w_up": (D_MODEL, 1024), "w_down": (1024, D_MODEL)}
_BIG_SPLIT = {"w_in": 1, "w_pa": 0, "w_pb": 0, "w_o": 0, "w_up": 0, "w_down": 0}


def _half(ref, e, name, lead=0):
    axis = _BIG_SPLIT[name]
    size = _BIG_SHARD[name][axis] // 2
    start = pl.multiple_of(e * size, 128 if axis == 1 else 16)
    idx = [pl.ds(0, ref.shape[a]) for a in range(lead)]
    idx += [pl.ds(start, size), pl.ds(0, _BIG_SHARD[name][1])] if axis == 0 else [pl.ds(0, _BIG_SHARD[name][0]), pl.ds(start, size)]
    return ref.at[tuple(idx)]


def _half_shape(name):
    r, c = _BIG_SHARD[name]
    return (r // 2, c) if _BIG_SPLIT[name] == 0 else (r, c // 2)


def _remote(src, dst, send_sems, recv_sems, k, to):
    return pltpu.make_async_remote_copy(src_ref=src, dst_ref=dst, send_sem=send_sems.at[k], recv_sem=recv_sems.at[k],
                                        device_id=to, device_id_type=MESH)


def _mesh_place():
    x, y, c = lax.axis_index("x"), lax.axis_index("y"), lax.axis_index("c")
    return x, y, c, [(1 - x, y), (x, 1 - y), (1 - x, 1 - y)]


class _Gather:
    def __init__(self, names, small=()):
        self.names = tuple(names)
        self.nb = len(self.names)
        self.n = self.nb + len(small)
        self.n_sems = 6 * self.n
        self.out_shape = [jax.ShapeDtypeStruct((N_CHIP,) + _BIG_SHARD[nm], BF16) for nm in self.names]
        self.out_shape += [jax.ShapeDtypeStruct((N_CHIP,) + s.shape, s.dtype) for s in small]

    def _ici(self, ins, outs, ss, rs, k, j, peer, slot, c):
        if k < self.nb:
            return _remote(_half(ins[k], c, self.names[k]), _half(outs[k].at[slot], c, self.names[k]), ss, rs, 6 * k + j,
                           (*peer, c))
        return _remote(ins[k], outs[k].at[slot], ss, rs, 6 * k + j, (*peer, c))

    def _passed(self, outs, ss, rs, k, j, slot, e, sibling):
        part = _half(outs[k].at[slot], e, self.names[k])
        return _remote(part, part, ss, rs, 6 * k + 3 + j, sibling)

    def first(self, ins, outs, ss, rs):
        x, y, c, peers = _mesh_place()
        me = 2 * x + y
        for k in range(self.n):
            for j, peer in enumerate(peers):
                self._ici(ins, outs, ss, rs, k, j, peer, me, c).start()
        for k in range(self.n):
            outs[k][me] = ins[k][...]

    def middle(self, ins, outs, ss, rs):
        x, y, c, peers = _mesh_place()
        for j, (px, py) in enumerate(peers):
            for k in range(self.nb):
                self._ici(ins, outs, ss, rs, k, j, (px, py), 2 * px + py, c).wait_recv()
                self._passed(outs, ss, rs, k, j, 2 * px + py, c, (x, y, 1 - c)).start()

    def last(self, ins, outs, ss, rs):
        x, y, c, peers = _mesh_place()
        for j, (px, py) in enumerate(peers):
            for k in range(self.n):
                if k < self.nb:
                    self._passed(outs, ss, rs, k, j, 2 * px + py, 1 - c, (x, y, 1 - c)).wait_recv()
                    self._passed(outs, ss, rs, k, j, 2 * px + py, c, (x, y, 1 - c)).wait_send()
                else:
                    self._ici(ins, outs, ss, rs, k, j, (px, py), 2 * px + py, c).wait_recv()
                self._ici(ins, outs, ss, rs, k, j, (px, py), 2 * x + y, c).wait_send()


def _run_alone(rider, ins, name):
    def body(*refs):
        r_in, r_out, sems = _split(refs, len(ins), len(rider.out_shape), 2)
        rider.first(r_in, r_out, *sems)
        rider.middle(r_in, r_out, *sems)
        rider.last(r_in, r_out, *sems)

    return pl.pallas_call(
        body, in_specs=[VMEM_WHOLE] * len(ins), out_specs=[VMEM_WHOLE] * len(rider.out_shape), out_shape=rider.out_shape,
        scratch_shapes=[pltpu.SemaphoreType.DMA((rider.n_sems,)), pltpu.SemaphoreType.DMA((rider.n_sems,))],
        compiler_params=_params(), name=name,
    )(*ins)


def _presum(names, grads, name):
    n = len(grads)

    def body(*refs):
        g_refs, got_refs, stage_refs, (send_sems, recv_sems, local_sems) = _split(refs, n, n, n, 3)
        x, y, c = lax.axis_index("x"), lax.axis_index("y"), lax.axis_index("c")

        def stage(e):
            cps = [pltpu.make_async_copy(_half(g_refs[k], e, names[k], lead=1), stage_refs[k], local_sems.at[k])
                   for k in range(n)]
            for cp in cps:
                cp.start()
            return cps

        staged = stage(1 - c)
        sends = []
        for k in range(n):
            staged[k].wait()
            cp = _remote(stage_refs[k], got_refs[k], send_sems, recv_sems, k, (x, y, 1 - c))
            cp.start()
            sends.append(cp)
        for cp in sends:
            cp.wait_send()
        staged = stage(c)
        for k in range(n):
            sends[k].wait_recv()
            staged[k].wait()

            @pl.loop(0, N_CHIP)
            def _(j):
                got_refs[k][j] = (got_refs[k][j].astype(F32) + stage_refs[k][j].astype(F32)).astype(BF16)

    half = [jax.ShapeDtypeStruct((N_CHIP,) + _half_shape(nm), BF16) for nm in names]
    return pl.pallas_call(
        body, in_specs=[ANY] * n, out_specs=[VMEM_WHOLE] * n, out_shape=half,
        scratch_shapes=[pltpu.VMEM(h.shape, h.dtype) for h in half]
        + [pltpu.SemaphoreType.DMA((n,)), pltpu.SemaphoreType.DMA((n,)), pltpu.SemaphoreType.DMA((n,))],
        compiler_params=_params(), name=name,
    )(*grads)


class _SendPartials:
    def __init__(self, names, small_shape=None):
        self.n = len(names)
        self.small = small_shape is not None
        self.n_sems = 3 * self.n + 7
        self.out_shape = [jax.ShapeDtypeStruct((N_CHIP,) + _half_shape(nm), BF16) for nm in names]
        if self.small:
            self.out_shape.append(jax.ShapeDtypeStruct((N_DEV,) + small_shape, F32))

    def _piece(self, ins, outs, ss, rs, k, j, peer, src_slot, dst_slot, c):
        return _remote(ins[k].at[src_slot], outs[k].at[dst_slot], ss, rs, 3 * k + j, (*peer, c))

    def _small(self, ins, outs, ss, rs, r, other, slot):
        return _remote(ins[self.n], outs[self.n].at[slot], ss, rs, 3 * self.n + r, other)

    @staticmethod
    def _others(x, y, c):
        return [(x, y, 1 - c), (1 - x, y, c), (1 - x, y, 1 - c), (x, 1 - y, c), (x, 1 - y, 1 - c),
                (1 - x, 1 - y, c), (1 - x, 1 - y, 1 - c)]

    def first(self, ins, outs, ss, rs):
        x, y, c, peers = _mesh_place()
        me = 2 * x + y
        for k in range(self.n):
            for j, (px, py) in enumerate(peers):
                self._piece(ins, outs, ss, rs, k, j, (px, py), 2 * px + py, me, c).start()
        if self.small:
            for r, other in enumerate(self._others(x, y, c)):
                self._small(ins, outs, ss, rs, r, other, 4 * x + 2 * y + c).start()
            outs[self.n][4 * x + 2 * y + c] = ins[self.n][...]
        for k in range(self.n):
            outs[k][me] = ins[k][me]

    def middle(self, ins, outs, ss, rs):
        pass

    def last(self, ins, outs, ss, rs):
        x, y, c, peers = _mesh_place()
        me = 2 * x + y
        for k in range(self.n):
            for j, (px, py) in enumerate(peers):
                self._piece(ins, outs, ss, rs, k, j, (px, py), me, 2 * px + py, c).wait_recv()
                self._piece(ins, outs, ss, rs, k, j, (px, py), 2 * px + py, me, c).wait_send()
        if self.small:
            for r, (px, py, pc) in enumerate(self._others(x, y, c)):
                self._small(ins, outs, ss, rs, r, (px, py, pc), 4 * px + 2 * py + pc).wait_recv()
                self._small(ins, outs, ss, rs, r, (px, py, pc), 4 * x + 2 * y + c).wait_send()


def _sum_swap(names, parts):
    n = len(parts)

    def body(*refs):
        p_refs, o_refs, (send_sems, recv_sems) = _split(refs, n, n, 2)
        x, y, c = lax.axis_index("x"), lax.axis_index("y"), lax.axis_index("c")
        for e in range(2):
            @pl.when(c == e)
            def _():
                for k in range(n):
                    g = p_refs[k][0].astype(F32)
                    for s in range(1, N_CHIP):
                        g = g + p_refs[k][s].astype(F32)
                    r, cols = _half_shape(names[k])
                    if _BIG_SPLIT[names[k]] == 0:
                        o_refs[k][e * r:(e + 1) * r, :] = g
                    else:
                        o_refs[k][:, e * cols:(e + 1) * cols] = g
        sends = []
        for k in range(n):
            mine = _half(o_refs[k], c, names[k])
            cp = _remote(mine, mine, send_sems, recv_sems, k, (x, y, 1 - c))
            cp.start()
            sends.append(cp)
        for k in range(n):
            theirs = _half(o_refs[k], 1 - c, names[k])
            _remote(theirs, theirs, send_sems, recv_sems, k, (x, y, 1 - c)).wait_recv()
        for cp in sends:
            cp.wait_send()

    return pl.pallas_call(
        body, in_specs=[VMEM_WHOLE] * n, out_specs=[VMEM_WHOLE] * n,
        out_shape=[jax.ShapeDtypeStruct(_BIG_SHARD[nm], F32) for nm in names],
        scratch_shapes=[pltpu.SemaphoreType.DMA((n,)), pltpu.SemaphoreType.DMA((n,))],
        compiler_params=_params(), name="sum_swap",
    )(*parts)


def _tile(rows, cols, itemsize, budget):
    t = cols if rows % 16 else rows
    other = rows if rows % 16 else cols
    step = 256 if rows % 16 else 32
    while t % step == 0 and t * other * itemsize > budget:
        t //= 2
    return (rows, t) if rows % 16 else (t, cols)


def _adamw_math(w, g, m, v):
    m = ADAM_B1 * m + (1.0 - ADAM_B1) * g
    v = ADAM_B2 * v + (1.0 - ADAM_B2) * (g * g)
    m_hat = m / (1.0 - ADAM_B1 ** ADAM_STEP)
    v_hat = v / (1.0 - ADAM_B2 ** ADAM_STEP)
    delta = -ADAM_LR * (m_hat / (jnp.sqrt(v_hat) + ADAM_EPS) + ADAM_WD * w)
    return delta, m, v


def _adamw_big(g, w, m, v, name):
    r, c = w.shape
    tr, tc = _tile(r, c, 4, 1024 * 1024)

    def body(g_ref, w_ref, m_ref, v_ref, d_ref, nm_ref, nv_ref):
        d_ref[...], nm_ref[...], nv_ref[...] = _adamw_math(w_ref[...], g_ref[...], m_ref[...], v_ref[...])

    blk = pl.BlockSpec((tr, tc), lambda i, l: (i, l))
    return pl.pallas_call(
        body, grid=(r // tr, c // tc), in_specs=[blk, blk, blk, blk],
        out_specs=[blk, blk, blk], out_shape=[jax.ShapeDtypeStruct((r, c), F32)] * 3,
        compiler_params=_params(("arbitrary", "arbitrary")), name=name,
    )(g, w, m, v)


def _adamw_rows(g, w, m, v, name):
    r, _, c = w.shape
    tc = 128

    def body(g_ref, w_ref, m_ref, v_ref, g3_ref, d_ref, nm_ref, nv_ref):
        g = g_ref[...]
        g3_ref[:, 0, :] = g
        d_ref[:, 0, :], nm_ref[:, 0, :], nv_ref[:, 0, :] = _adamw_math(w_ref[:, 0, :], g, m_ref[:, 0, :], v_ref[:, 0, :])

    rows = pl.BlockSpec((r, 1, tc), lambda l: (0, 0, l))
    return pl.pallas_call(
        body, grid=(c // tc,), in_specs=[pl.BlockSpec((r, tc), lambda l: (0, l)), rows, rows, rows],
        out_specs=[rows] * 4, out_shape=[jax.ShapeDtypeStruct((r, 1, c), F32)] * 4,
        compiler_params=_params(("arbitrary",)), name=name,
    )(g, w, m, v)


def _sum_small(parts):
    def body(p_ref, o_ref):
        g = p_ref[0]
        for d in range(1, N_DEV):
            g = g + p_ref[d]
        o_ref[...] = g

    return pl.pallas_call(body, out_shape=jax.ShapeDtypeStruct(parts.shape[1:], F32), name="sum_small")(parts)


def _adamw_small(ws, gs, ms, vs):
    n = len(ws)

    def body(*refs):
        w_refs, g_refs, m_refs, v_refs, d_refs, nm_refs, nv_refs = _split(refs, *([n] * 7))
        for k in range(n):
            d_refs[k][...], nm_refs[k][...], nv_refs[k][...] = _adamw_math(w_refs[k][...], g_refs[k][...], m_refs[k][...],
                                                                             v_refs[k][...])

    shapes = [jax.ShapeDtypeStruct(w.shape, F32) for w in ws]
    res = pl.pallas_call(body, out_shape=shapes * 3, name="adamw_small")(*ws, *gs, *ms, *vs)
    return res[:n], res[n:2 * n], res[2 * n:]


def _pack(arrs):
    flat = jnp.concatenate([a.reshape(-1) for a in arrs])
    rows = -(-flat.shape[0] // 1024) * 8
    return jnp.pad(flat, (0, rows * 128 - flat.shape[0])).reshape(rows, 128)


def _unpack(buf, shapes):
    flat = buf.reshape(-1)
    out, off = [], 0
    for s in shapes:
        size = 1
        for d in s:
            size *= d
        out.append(flat[off:off + size].reshape(s))
        off += size
    return out


def _block_rows(w):
    return jnp.pad(w.reshape(512, 4), ((0, 0), (0, 124)))


def _cols(a4):
    return jnp.transpose(a4, (1, 0, 2)).reshape(a4.shape[1], -1)


_LATE = ("w_pa", "w_pb", "w_o", "w_up", "w_down")
_RIDE_IN_PROJ = ("w_pa", "w_pb", "w_o")
_RIDE_MIXER = ("w_up",)
_RIDE_MERGE = ("w_down",)


def _full_weights(gathered):
    joined = {"w_in": (D_IN, D_MODEL), "w_o": (D_MODEL, D_MODEL), "w_down": (D_FF, D_MODEL)}
    return {n: (a.reshape(joined[n]) if n in joined else a) for n, a in gathered.items()}


def _local_step(x, target, w, sp, late_shards=None):
    sp = {n: (a.reshape(1, -1) if a.ndim == 1 else a) for n, a in sp.items()}
    wau = jnp.zeros((128, 256), F32).at[0:16].set(sp["w_a_up"])
    wif = jnp.zeros((1536, 128), F32).at[:, 0:8].set(sp["w_if"])
    bif = jnp.zeros((1, 128), F32).at[:, 0:8].set(sp["b_if"])
    p = {"wau": wau, "bau": sp["b_a_up"], "ggla": sp["g_gla_norm"], "cw": sp["conv_w"], "cb": sp["conv_b"],
         "wq": _block_rows(sp["w_q_ml"]), "wk": _block_rows(sp["w_k_ml"]), "wv": _block_rows(sp["w_v_ml"]),
         "wif": wif, "bif": bif, "skip": sp["ml_skip"], "gml": sp["g_ml_norm"]}

    if late_shards is None:
        (pm, gab, h), _ = _in_proj(x, sp["g_pre_mix"], w["w_in"])
        ab, *states = _mixer_fwd(pm, p)
        (x1, mix, merged), _ = _merge_fwd(ab, gab, x, w["w_pa"], w["w_pb"], w["w_o"], sp["g_post_mix"])
    else:
        shard = dict(zip(_LATE, late_shards))
        (pm, gab, h), got = _in_proj(x, sp["g_pre_mix"], w["w_in"], _Gather(_RIDE_IN_PROJ),
                                     [shard[n] for n in _RIDE_IN_PROJ])
        w = dict(w, **_full_weights(dict(zip(_RIDE_IN_PROJ, got))))
        ab, *rest = _mixer_fwd(pm, p, _Gather(_RIDE_MIXER), [shard[n] for n in _RIDE_MIXER])
        states = rest[:4]
        w.update(_full_weights(dict(zip(_RIDE_MIXER, rest[4:]))))
        (x1, mix, merged), got = _merge_fwd(ab, gab, x, w["w_pa"], w["w_pb"], w["w_o"], sp["g_post_mix"],
                                            _Gather(_RIDE_MERGE), [shard[n] for n in _RIDE_MERGE])
        w.update(_full_weights(dict(zip(_RIDE_MERGE, got))))
    dx1, u, dd, h2, dpre, dg_post_mlp, dg_pre_mlp, loss = _mlp(x1, target, sp["g_pre_mlp"], sp["g_post_mlp"],
                                                                w["w_up"], w["w_down"])
    dmix, dya, dyb, dgab, dab, dg_post_mix = _merge_bwd(dx1, mix, ab, gab, w["w_pa"], w["w_pb"], w["w_o"], sp["g_post_mix"])
    big = {
        "w_pa": _tn_matmul(ab[:, 0:512], dya, "dw_pa", shards=N_CHIP),
        "w_pb": _tn_matmul(ab[:, 512:1024], dyb, "dw_pb", shards=N_CHIP),
        "w_o": _tn_matmul(merged, dmix, "dw_o"),
        "w_up": _tn_matmul(h2, dpre, "dw_up", shards=N_CHIP),
        "w_down": _tn_matmul(u, dd, "dw_down"),
    }
    if late_shards is None:
        dpm, dp, _ = _mixer_bwd(pm, dab, states, p)
    else:
        partial = _presum(_LATE, [big[n].reshape((N_CHIP,) + _BIG_SHARD[n]) for n in _LATE], "presum_late")
        dpm, dp, parts = _mixer_bwd(pm, dab, states, p, _SendPartials(_LATE), partial)
        big = dict(zip(_LATE, parts))
    big["w_in"] = _dw_in(dpm, dgab, h)
    if late_shards is None:
        (dx, dg_pre_mix), _ = _in_proj_bwd(dpm, dgab, x, dx1, sp["g_pre_mix"], w["w_in"])
    else:
        partial = _presum(("w_in",), [big["w_in"].reshape((N_CHIP,) + _BIG_SHARD["w_in"])], "presum_w_in")
        (dx, dg_pre_mix), parts = _in_proj_bwd(dpm, dgab, x, dx1, sp["g_pre_mix"], w["w_in"], _SendPartials(("w_in",)),
                                               partial)
        big["w_in"] = parts[0]
    small = {
        "g_pre_mix": dg_pre_mix, "b_a_up": dp["bau"], "g_gla_norm": dp["ggla"], "conv_b": dp["cb"],
        "w_q_ml": dp["wq"][:, 0:4].reshape(128, 4, 4), "w_k_ml": dp["wk"][:, 0:4].reshape(128, 4, 4),
        "w_v_ml": dp["wv"][:, 0:4].reshape(128, 4, 4),
        "b_if": dp["bif"][:, 0:8], "ml_skip": dp["skip"], "g_ml_norm": dp["gml"], "g_post_mix": dg_post_mix,
        "g_pre_mlp": dg_pre_mlp, "g_post_mlp": dg_post_mlp, "w_a_up": dp["wau"][0:16], "conv_w": dp["cw"],
        "w_if": dp["wif"][:, 0:8], "loss": loss[:, 0:1],
    }
    return dx, big, small


_SMALL_REPL = ("g_pre_mix", "b_a_up", "g_gla_norm", "conv_b", "w_q_ml", "w_k_ml", "w_v_ml", "b_if", "ml_skip",
               "g_ml_norm", "g_post_mix", "g_pre_mlp", "g_post_mlp")
_SMALL_SHARDED = ("w_a_up", "conv_w", "w_if")
_SMALL_ORDER = _SMALL_REPL + _SMALL_SHARDED + ("loss",)
_WEIGHTS = ("g_pre_mix", "w_in", "w_a_up", "b_a_up", "g_gla_norm", "conv_w", "conv_b", "w_q_ml", "w_k_ml", "w_v_ml",
            "w_if", "b_if", "ml_skip", "g_ml_norm", "w_pa", "w_pb", "w_o", "g_post_mix", "g_pre_mlp", "w_up", "w_down",
            "g_post_mlp")


def _as_shard(name, a):
    return jnp.transpose(a, (2, 0, 1)) if name == "w_in" else a[0]


def _from_shard(name, a):
    return jnp.transpose(a, (1, 2, 0)) if name == "w_in" else a[None]


def kernel(x, g_pre_mix, w_in, w_a_up, b_a_up, g_gla_norm, conv_w, conv_b, w_q_ml, w_k_ml, w_v_ml, w_if, b_if, ml_skip, g_ml_norm, w_pa, w_pb, w_o, g_post_mix, g_pre_mlp, w_up, w_down, g_post_mlp, loss_target, m_g_pre_mix, m_w_in, m_w_a_up, m_b_a_up, m_g_gla_norm, m_conv_w, m_conv_b, m_w_q_ml, m_w_k_ml, m_w_v_ml, m_w_if, m_b_if, m_ml_skip, m_g_ml_norm, m_w_pa, m_w_pb, m_w_o, m_g_post_mix, m_g_pre_mlp, m_w_up, m_w_down, m_g_post_mlp, v_g_pre_mix, v_w_in, v_w_a_up, v_b_a_up, v_g_gla_norm, v_conv_w, v_conv_b, v_w_q_ml, v_w_k_ml, v_w_v_ml, v_w_if, v_b_if, v_ml_skip, v_g_ml_norm, v_w_pa, v_w_pb, v_w_o, v_g_post_mix, v_g_pre_mlp, v_w_up, v_w_down, v_g_post_mlp):
    args = dict(locals())
    wts = {n: _as_shard(n, args[n]) for n in _WEIGHTS}
    mom = {n: _as_shard(n, args["m_" + n]) for n in _WEIGHTS}
    var = {n: _as_shard(n, args["v_" + n]) for n in _WEIGHTS}
    chip = 2 * lax.axis_index("x") + lax.axis_index("y")

    first = ("w_in",) + _SMALL_SHARDED
    gathered = dict(zip(first, _run_alone(_Gather(("w_in",), [wts[n] for n in _SMALL_SHARDED]),
                                          [wts[n][:, 0, :].astype(BF16) if n == "w_in" else wts[n] for n in first],
                                          "gather_first")))
    sp = {n: wts[n] for n in _SMALL_REPL}
    sp["w_a_up"] = _cols(gathered["w_a_up"])
    sp["conv_w"] = _cols(gathered["conv_w"])
    sp["w_if"] = gathered["w_if"].reshape(1536, 8)

    dx, big, small = _local_step(x[0], loss_target[0], _full_weights({"w_in": gathered["w_in"]}), sp,
                                 late_shards=[wts[n].astype(BF16) for n in _LATE])

    small_shapes = [small[n].shape for n in _SMALL_ORDER]
    packed = _pack([small[n] for n in _SMALL_ORDER])
    (small_parts,) = _run_alone(_SendPartials((), packed.shape), [packed], "exchange_small")
    sums = _sum_swap(_BIG, [big[n] for n in _BIG])

    grads, delta, new_m, new_v = {}, {}, {}, {}
    for n, g in zip(_BIG, sums):
        if n == "w_in":
            g, d, nm, nv = _adamw_rows(g, wts[n], mom[n], var[n], "adamw_" + n)
        else:
            d, nm, nv = _adamw_big(g, wts[n], mom[n], var[n], "adamw_" + n)
        grads[n], delta[n], new_m[n], new_v[n] = (_from_shard(n, a) for a in (g, d, nm, nv))
    summed = dict(zip(_SMALL_ORDER, _unpack(_sum_small(small_parts), small_shapes)))
    loss = summed["loss"].reshape(())
    for n in _SMALL_REPL:
        grads[n] = summed[n].reshape(args[n].shape)
    grads["w_a_up"] = lax.dynamic_slice_in_dim(summed["w_a_up"], chip * 64, 64, axis=1)[None]
    grads["conv_w"] = lax.dynamic_slice_in_dim(summed["conv_w"], chip * 128, 128, axis=1)[None]
    grads["w_if"] = lax.dynamic_slice_in_dim(summed["w_if"], chip * 384, 384, axis=0)[None]
    small_names = _SMALL_REPL + _SMALL_SHARDED
    upd = _adamw_small([args[n] for n in small_names], [grads[n] for n in small_names],
                       [args["m_" + n] for n in small_names], [args["v_" + n] for n in small_names])
    for dst, arrs in zip((delta, new_m, new_v), upd):
        dst.update(zip(small_names, arrs))

    outs = [loss, dx[None]]
    for group in (grads, delta, new_m, new_v):
        outs += [group[n] for n in _WEIGHTS]
    return tuple(outs)
```

```python
import functools

import jax
import jax.numpy as jnp
from jax import lax
from jax.experimental import pallas as pl
from jax.experimental.pallas import tpu as pltpu

F32 = jnp.float32
BF16 = jnp.bfloat16

SEQ = 2048
D_MODEL = 1024
CHUNK = 64
N_CHUNK = SEQ // CHUNK
HEADS = 4
GLA_DK = 64
GLA_DV = 128
ML_DH = 128
D_FF = 4096
EPS = 1e-6
N_CHIP = 4
N_DEV = 8
TOK_TILE = 256
N_TOK_TILE = SEQ // TOK_TILE
SWEEP = 2
assert CHUNK == 64
N_SWEEP = N_CHUNK // SWEEP

PM_W = 2688
PM_XM = 1536
PM_OP = 2048
PM_AL = 2560
GAB_W = 2048
D_IN = 4624
IN_SHARD = D_IN // N_CHIP
IN_ALOW = 1536
IN_XM = 1552
IN_GATES = 2576

ADAM_LR = 0.001
ADAM_B1 = 0.9
ADAM_B2 = 0.999
ADAM_EPS = 1e-08
ADAM_WD = 0.01
ADAM_STEP = 10

VMEM_LIMIT = 56 * 1024 * 1024


def _params(sem=None):
    return pltpu.CompilerParams(dimension_semantics=sem, vmem_limit_bytes=VMEM_LIMIT)


def _dot(a, b, ca, cb):
    return lax.dot_general(a.astype(BF16), b.astype(BF16), (((ca,), (cb,)), ((), ())), preferred_element_type=F32)


def _pmm_nn(a, b):
    return _dot(a, b, 1, 0)


def _pmm_nt(a, b):
    return _dot(a, b, 1, 1)


def _pmm_tn(a, b):
    return _dot(a, b, 0, 0)


def _pcmm(c, x):
    return lax.dot_general(c, x, (((1,), (0,)), ((), ())), precision=lax.Precision.HIGHEST, preferred_element_type=F32)


@jax.custom_vjp
def _mm_nn(a, b):
    return _dot(a, b, 1, 0)


@jax.custom_vjp
def _mm_nt(a, b):
    return _dot(a, b, 1, 1)


@jax.custom_vjp
def _mm_tn(a, b):
    return _dot(a, b, 0, 0)


_mm_nn.defvjp(lambda a, b: (_dot(a, b, 1, 0), (a, b)), lambda r, g: (_mm_nt(g, r[1]), _mm_tn(r[0], g)))
_mm_nt.defvjp(lambda a, b: (_dot(a, b, 1, 1), (a, b)), lambda r, g: (_mm_nn(g, r[1]), _mm_tn(g, r[0])))
_mm_tn.defvjp(lambda a, b: (_dot(a, b, 0, 0), (a, b)), lambda r, g: (_mm_nt(r[1], g), _mm_nn(r[0], g)))


@jax.custom_vjp
def _cmm(c, x):
    return _pcmm(c, x)


_cmm.defvjp(
    lambda c, x: (_pcmm(c, x), c),
    lambda c, g: (jnp.zeros_like(c), lax.dot_general(c, g, (((0,), (0,)), ((), ())), precision=lax.Precision.HIGHEST,
                                                      preferred_element_type=F32)),
)

_PLAIN_OPS = (_pmm_nn, _pmm_nt, _pmm_tn, _pcmm)
_VJP_OPS = (_mm_nn, _mm_nt, _mm_tn, _cmm)


def _sigmoid(x):
    return 0.5 * (jnp.tanh(0.5 * x) + 1.0)


def _log_sigmoid(x):
    return jnp.minimum(x, 0.0) - jnp.log(1.0 + jnp.exp(-jnp.abs(x)))


def _mean(x):
    return jnp.mean(x, axis=-1, keepdims=True)


def _nt(a, b):
    return lax.dot_general(a, b, (((1,), (1,)), ((), ())), preferred_element_type=F32)


def _tn(a, b):
    return lax.dot_general(a, b, (((0,), (0,)), ((), ())), preferred_element_type=F32)


def _mixer_chunk(ops, p, st, pm, xprev8):
    mm_nn, mm_nt, mm_tn, cmm = ops
    n_rows = pm.shape[0]
    n_ch = n_rows // CHUNK
    row = lax.broadcasted_iota(jnp.int32, (n_rows, n_rows), 0)
    col = lax.broadcasted_iota(jnp.int32, (n_rows, n_rows), 1)
    tri = jnp.logical_and((row >> 6) == (col >> 6), row >= col).astype(F32)
    causal = tri[0:CHUNK, 0:CHUNK] > 0.0
    q = pm[:, 0:256]
    k = pm[:, 256:512]
    v = pm[:, 512:1024]
    g = pm[:, 1024:1536]
    xm = pm[:, PM_XM:PM_XM + 512]
    opre = pm[:, PM_OP:PM_OP + 512]
    alow = pm[:, PM_AL:PM_AL + 128]
    hs = range(HEADS)
    cs = range(n_ch)
    pairs = [(i, h) for i in cs for h in hs]
    rs = [slice(i * CHUNK, (i + 1) * CHUNK) for i in cs]
    last = [slice((i + 1) * CHUNK - 1, (i + 1) * CHUNK) for i in cs]
    s6 = [slice(h * GLA_DK, (h + 1) * GLA_DK) for h in hs]
    s12 = [slice(h * 128, (h + 1) * 128) for h in hs]

    la = _log_sigmoid(mm_nn(alow, p["wau"]) + p["bau"]) * (1.0 / 16.0)
    cum = cmm(tri, la)
    cum_last = [cum[last[i], :] for i in cs]
    to_end = jnp.concatenate([cum_last[i] - cum[rs[i], :] for i in cs], axis=0)
    e_pos = jnp.exp(cum)
    e_neg = jnp.exp(-cum)
    qs = q * (GLA_DK ** -0.5)
    qp = qs * e_pos
    qn = qs * e_neg
    kp = k * e_pos
    kn = k * e_neg
    kl = k * jnp.exp(to_end)
    dec = [jnp.exp(cum_last[i]) for i in cs]
    a_fwd = {(i, h): mm_nt(qp[rs[i], s6[h]], kn[rs[i], s6[h]]) for i, h in pairs}
    a_bwd = {(i, h): mm_nt(qn[rs[i], s6[h]], kp[rs[i], s6[h]]) for i, h in pairs}
    s_chunk = {(i, h): mm_tn(v[rs[i], s12[h]], kl[rs[i], s6[h]]) for i, h in pairs}
    mem = {(0, h): st["S"][h] for h in hs}
    for i, h in pairs:
        mem[(i + 1, h)] = mem[(i, h)] * dec[i][:, s6[h]] + s_chunk[(i, h)]
    s_new = [mem[(n_ch, h)] for h in hs]
    o_inter = {(i, h): mm_nt(qp[rs[i], s6[h]], mem[(i, h)]) for i, h in pairs}
    scores = {ih: jnp.where(causal, a_fwd[ih], a_bwd[ih]) for ih in pairs}
    o = {(i, h): mm_nn(scores[(i, h)], v[rs[i], s12[h]]) + o_inter[(i, h)] for i, h in pairs}
    o = {ih: o[ih] * lax.rsqrt(_mean(o[ih] * o[ih]) + EPS) * p["ggla"] for ih in pairs}
    gate = g * _sigmoid(g)
    out_a = {(i, h): o[(i, h)] * gate[rs[i], s12[h]] for i, h in pairs}

    xx = jnp.concatenate([xprev8, xm], axis=0)
    pre = p["cb"]
    for j in range(4):
        pre = pre + p["cw"][j:j + 1, :] * xx[5 + j:5 + j + n_rows, :]
    xc = pre * _sigmoid(pre)
    qm = [mm_nn(xc[:, s12[h]], p["wq"][h]) for h in hs]
    km = [mm_nn(xc[:, s12[h]], p["wk"][h]) for h in hs]
    vm = [mm_nn(xm[:, s12[h]], p["wv"][h]) for h in hs]
    qcat = jnp.concatenate(qm, axis=1)
    kcat = jnp.concatenate(km, axis=1)
    vcat = jnp.concatenate(vm, axis=1)
    gates = (mm_nn(qcat, p["wif"][0:512]) + mm_nn(kcat, p["wif"][512:1024]) + mm_nn(vcat, p["wif"][1024:1536])
             + p["bif"])
    lf = _log_sigmoid(gates)
    fc = cmm(tri, lf)
    gates_t = gates.T
    fc_t = fc.T
    ks = [km[h] * (ML_DH ** -0.5) for h in hs]
    qk = {(i, h): mm_nt(qm[h][rs[i]], ks[h][rs[i]]) for i, h in pairs}
    li_c = {(i, h): gates[rs[i], h:h + 1] for i, h in pairs}
    fc_c = {(i, h): fc[rs[i], 4 + h:5 + h] for i, h in pairs}
    f_last = {(i, h): fc[last[i], 4 + h:5 + h] for i, h in pairs}
    a = {ih: f_last[ih] - fc_c[ih] + li_c[ih] for ih in pairs}
    m_loc = {ih: jnp.max(a[ih], axis=0, keepdims=True) for ih in pairs}
    kw = {(i, h): ks[h][rs[i]] * jnp.exp(a[(i, h)] - m_loc[(i, h)]) for i, h in pairs}
    c_chunk = {(i, h): mm_tn(kw[(i, h)], vm[h][rs[i]]) for i, h in pairs}
    c_in = {(0, h): st["C"][h] for h in hs}
    n_in = {(0, h): st["n"][h] for h in hs}
    m_in = {(0, h): st["m"][h][:, 0:1] for h in hs}
    for i, h in pairs:
        m_nx = jnp.maximum(f_last[(i, h)] + m_in[(i, h)], m_loc[(i, h)])
        sp = jnp.exp(f_last[(i, h)] + m_in[(i, h)] - m_nx)
        sl = jnp.exp(m_loc[(i, h)] - m_nx)
        c_in[(i + 1, h)] = sp * c_in[(i, h)] + sl * c_chunk[(i, h)]
        n_in[(i + 1, h)] = sp * n_in[(i, h)] + sl * jnp.sum(kw[(i, h)], axis=0, keepdims=True)
        m_in[(i + 1, h)] = m_nx
    q_c = {(i, h): mm_nn(qm[h][rs[i]], c_in[(i, h)]) for i, h in pairs}
    log_d = {(i, h): gates_t[h:h + 1, rs[i]] - jnp.abs(fc_c[(i, h)] - fc_t[4 + h:5 + h, rs[i]]) for i, h in pairs}
    g_int = {ih: fc_c[ih] + m_in[ih] for ih in pairs}
    m_t = {ih: jnp.maximum(g_int[ih], jnp.max(log_d[ih], axis=1, keepdims=True)) for ih in pairs}
    s = {ih: qk[ih] * jnp.exp(log_d[ih] - m_t[ih]) for ih in pairs}
    scl = {ih: jnp.exp(g_int[ih] - m_t[ih]) for ih in pairs}
    num = {(i, h): mm_nn(s[(i, h)], vm[h][rs[i]]) + scl[(i, h)] * q_c[(i, h)] for i, h in pairs}
    den = {(i, h): jnp.sum(s[(i, h)], axis=1, keepdims=True)
           + scl[(i, h)] * jnp.sum(qm[h][rs[i]] * n_in[(i, h)], axis=1, keepdims=True) for i, h in pairs}
    den = {ih: jnp.maximum(jnp.abs(den[ih]), jnp.exp(-m_t[ih])) for ih in pairs}
    open_gate = _sigmoid(opre)
    hc = {(i, h): num[(i, h)] / den[(i, h)] * open_gate[rs[i], s12[h]] for i, h in pairs}
    d0 = {ih: hc[ih] - _mean(hc[ih]) for ih in pairs}
    y = {ih: d0[ih] * lax.rsqrt(_mean(d0[ih] * d0[ih]) + EPS) for ih in pairs}
    skipped = p["skip"] * xc
    out_b = {(i, h): y[(i, h)] * p["gml"][:, s12[h]] + skipped[rs[i], s12[h]] for i, h in pairs}
    ab = jnp.concatenate([jnp.concatenate([out_a[(i, h)] for h in hs] + [out_b[(i, h)] for h in hs], axis=1) for i in cs],
                         axis=0)
    new = {"S": s_new, "C": [c_in[(n_ch, h)] for h in hs], "n": [n_in[(n_ch, h)] for h in hs],
           "m": [jnp.broadcast_to(m_in[(n_ch, h)], (1, ML_DH)) for h in hs]}
    return ab, new


_P_NAMES = ("wau", "bau", "ggla", "cw", "cb", "wq", "wk", "wv", "wif", "bif", "skip", "gml")
_P_SHAPES = {
    "wau": (128, 256), "bau": (1, 256), "ggla": (1, 128), "cw": (4, 512), "cb": (1, 512),
    "wq": (512, 128), "wk": (512, 128), "wv": (512, 128),
    "wif": (1536, 128), "bif": (1, 128), "skip": (1, 512), "gml": (1, 512),
}
_P_BLOCKDIAG = ("wq", "wk", "wv")
_S_NAMES = ("S", "C", "n", "m")
_S_SHAPES = {"S": (HEADS, GLA_DV, GLA_DK), "C": (HEADS, ML_DH, ML_DH), "n": (HEADS, 1, ML_DH), "m": (HEADS, 1, ML_DH)}


def _per_head(ref):
    return [ref[h] for h in range(HEADS)]


def _block_mask():
    r = lax.broadcasted_iota(jnp.int32, (128, 128), 0)
    c = lax.broadcasted_iota(jnp.int32, (128, 128), 1)
    same_block = (r >> 2) == (c >> 2)
    spread = jnp.logical_and(r < 4, (c & 3) == r)
    return same_block.astype(F32), spread.astype(F32)


def _expand_blockdiag(w_ref, dense_ref):
    same_block, spread = _block_mask()
    for h in range(HEADS):
        tiled = _pmm_nn(w_ref[h * 128:(h + 1) * 128, :], spread)
        dense_ref[h] = tiled * same_block


def _collect_blockdiag(ddense_ref, dw_ref):
    same_block, spread = _block_mask()
    for h in range(HEADS):
        dw_ref[h * 128:(h + 1) * 128, :] = lax.dot_general(
            ddense_ref[h] * same_block, spread, (((1,), (1,)), ((), ())), precision=lax.Precision.HIGHEST,
            preferred_element_type=F32)


def _const_spec(shape):
    zeros = (0,) * len(shape)
    return pl.BlockSpec(shape, lambda i: zeros)


def _split(refs, *counts):
    out, at = [], 0
    for c in counts:
        out.append(refs[at:at + c])
        at += c
    assert at == len(refs)
    return out


def _ride(rider, phases, cond, ins, outs, sems):
    if rider is None:
        return
    lands, (send_sems, recv_sems, flush_sems) = sems[:-3], sems[-3:]

    @pl.when(cond)
    def _():
        for phase in phases:
            getattr(rider, phase)(ins, lands, send_sems, recv_sems)
        if "last" in phases:
            flush = [pltpu.make_async_copy(lands[k], outs[k], flush_sems.at[k]) for k in range(len(outs))]
            for cp in flush:
                cp.start()
            for cp in flush:
                cp.wait()


def _rider_specs(rider, rider_ins):
    if rider is None:
        return [], [], [], []
    scratch = [pltpu.VMEM(s.shape, s.dtype) for s in rider.out_shape]
    scratch += [pltpu.SemaphoreType.DMA((rider.n_sems,)), pltpu.SemaphoreType.DMA((rider.n_sems,)),
                pltpu.SemaphoreType.DMA((len(rider.out_shape),))]
    return [VMEM_WHOLE] * len(rider_ins), [ANY] * len(rider.out_shape), list(rider.out_shape), scratch


def _mixer_fwd(pm, p, rider=None, rider_ins=()):
    n_p = len(_P_NAMES)
    r_in, r_out_specs, r_out_shape, r_sems = _rider_specs(rider, rider_ins)

    def body(*refs):
        (pm_ref, xprev_ref), p_list, ride_in, (ab_ref,), so_refs, ride_out, sc_refs, dense_list, sems = _split(
            refs, 2, n_p, len(r_in), 1, 4, len(r_out_specs), 4, 3, len(r_sems))
        p_refs = dict(zip(_P_NAMES, p_list))
        dense = dict(zip(_P_BLOCKDIAG, dense_list))
        n = pl.program_id(0)
        _ride(rider, ("first",), n == 0, ride_in, ride_out, sems)

        @pl.when(n == 0)
        def _():
            for r in sc_refs:
                r[...] = jnp.zeros_like(r)
            for nm in _P_BLOCKDIAG:
                _expand_blockdiag(p_refs[nm], dense[nm])

        st = {name: _per_head(r) for name, r in zip(_S_NAMES, sc_refs)}
        pv = {nm: (_per_head(dense[nm]) if nm in _P_BLOCKDIAG else p_refs[nm][...]) for nm in _P_NAMES}
        for name, r in zip(_S_NAMES, so_refs):
            for h in range(HEADS):
                r[0, h] = st[name][h]
        xprev8 = jnp.where(n > 0, xprev_ref[CHUNK - 8:CHUNK, :], 0.0)
        ab, st = _mixer_chunk(_PLAIN_OPS, pv, st, pm_ref[...], xprev8)
        ab_ref[...] = ab.astype(BF16)
        for name, r in zip(_S_NAMES, sc_refs):
            for h in range(HEADS):
                r[h] = st[name][h]
        _ride(rider, ("middle",), n == N_SWEEP - 2, ride_in, ride_out, sems)
        _ride(rider, ("last",), n == N_SWEEP - 1, ride_in, ride_out, sems)

    in_specs = [pl.BlockSpec((SWEEP * CHUNK, PM_W), lambda i: (i, 0)),
                pl.BlockSpec((CHUNK, 512), lambda i: (jnp.maximum(SWEEP * i - 1, 0), PM_XM // 512))]
    in_specs += [_const_spec(_P_SHAPES[nm]) for nm in _P_NAMES] + r_in
    out_specs = [pl.BlockSpec((SWEEP * CHUNK, 1024), lambda i: (i, 0))]
    out_shape = [jax.ShapeDtypeStruct((SEQ, 1024), BF16)]
    for nm in _S_NAMES:
        shp = _S_SHAPES[nm]
        out_specs.append(pl.BlockSpec((1,) + shp, lambda i: (i, 0, 0, 0)))
        out_shape.append(jax.ShapeDtypeStruct((N_SWEEP,) + shp, F32))
    return pl.pallas_call(
        body, grid=(N_SWEEP,), in_specs=in_specs, out_specs=out_specs + r_out_specs, out_shape=out_shape + r_out_shape,
        scratch_shapes=[pltpu.VMEM(_S_SHAPES[nm], F32) for nm in _S_NAMES]
        + [pltpu.VMEM((HEADS, 128, 128), F32) for _ in _P_BLOCKDIAG] + r_sems,
        compiler_params=_params(("arbitrary",)), name="mixer_fwd",
    )(pm, pm, *[p[nm] for nm in _P_NAMES], *rider_ins)


def _mixer_bwd(pm, dab, states, p, rider=None, rider_ins=()):
    n_p = len(_P_NAMES)
    r_in, r_out_specs, r_out_shape, r_sems = _rider_specs(rider, rider_ins)

    def body(*refs):
        ((pm_ref, xprev_ref, dab_ref), si_refs, p_list, ride_in, (dpm_ref,), dp_list, ride_out, ds_refs, (carry_ref,),
         dense_list, ddense_list, sems) = _split(refs, 3, 4, n_p, len(r_in), 1, n_p, len(r_out_specs), 4, 1, 3, 3, len(r_sems))
        p_refs = dict(zip(_P_NAMES, p_list))
        dp_refs = dict(zip(_P_NAMES, dp_list))
        dense = dict(zip(_P_BLOCKDIAG, dense_list))
        ddense = dict(zip(_P_BLOCKDIAG, ddense_list))
        i = pl.program_id(0)
        blk = N_SWEEP - 1 - i
        _ride(rider, ("first",), i == 0, ride_in, ride_out, sems)

        @pl.when(i == 0)
        def _():
            for r in ds_refs:
                r[...] = jnp.zeros_like(r)
            for nm in _P_NAMES:
                if nm in _P_BLOCKDIAG:
                    ddense[nm][...] = jnp.zeros_like(ddense[nm])
                    _expand_blockdiag(p_refs[nm], dense[nm])
                else:
                    dp_refs[nm][...] = jnp.zeros_like(dp_refs[nm])
            carry_ref[...] = jnp.zeros_like(carry_ref)

        pv = {nm: (_per_head(dense[nm]) if nm in _P_BLOCKDIAG else p_refs[nm][...]) for nm in _P_NAMES}
        dst = {name: _per_head(r) for name, r in zip(_S_NAMES, ds_refs)}
        st = {name: [r[0, h] for h in range(HEADS)] for name, r in zip(_S_NAMES, si_refs)}
        xprev8 = jnp.where(blk > 0, xprev_ref[CHUNK - 8:CHUNK, :], 0.0)
        _, vjp = jax.vjp(functools.partial(_mixer_chunk, _VJP_OPS), pv, st, pm_ref[...], xprev8)
        dp_sum, dst, dpm, dxprev8 = vjp((dab_ref[...], dst))
        reach = jnp.concatenate([jnp.zeros((SWEEP * CHUNK - 8, 512), F32), carry_ref[...]], axis=0)
        dpm_ref[:, 0:PM_XM] = dpm[:, 0:PM_XM].astype(BF16)
        dpm_ref[:, PM_XM:PM_XM + 512] = (dpm[:, PM_XM:PM_XM + 512] + reach).astype(BF16)
        dpm_ref[:, PM_XM + 512:PM_W] = dpm[:, PM_XM + 512:PM_W].astype(BF16)
        carry_ref[...] = dxprev8
        for name, r in zip(_S_NAMES, ds_refs):
            for h in range(HEADS):
                r[h] = dst[name][h]
        for nm in _P_NAMES:
            if nm in _P_BLOCKDIAG:
                for h in range(HEADS):
                    ddense[nm][h] += dp_sum[nm][h]
            else:
                dp_refs[nm][...] += dp_sum[nm]

        @pl.when(i == N_SWEEP - 1)
        def _():
            for nm in _P_BLOCKDIAG:
                _collect_blockdiag(ddense[nm], dp_refs[nm])

        _ride(rider, ("middle",), i == N_SWEEP - 2, ride_in, ride_out, sems)
        _ride(rider, ("last",), i == N_SWEEP - 1, ride_in, ride_out, sems)

    rev = lambda i: (N_SWEEP - 1 - i, 0)
    in_specs = [pl.BlockSpec((SWEEP * CHUNK, PM_W), rev),
                pl.BlockSpec((CHUNK, 512), lambda i: (jnp.maximum(SWEEP * (N_SWEEP - 1 - i) - 1, 0), PM_XM // 512)),
                pl.BlockSpec((SWEEP * CHUNK, 1024), rev)]
    for nm in _S_NAMES:
        in_specs.append(pl.BlockSpec((1,) + _S_SHAPES[nm], lambda i: (N_SWEEP - 1 - i, 0, 0, 0)))
    in_specs += [_const_spec(_P_SHAPES[nm]) for nm in _P_NAMES] + r_in
    out_specs = [pl.BlockSpec((SWEEP * CHUNK, PM_W), rev)] + [_const_spec(_P_SHAPES[nm]) for nm in _P_NAMES]
    out_shape = [jax.ShapeDtypeStruct((SEQ, PM_W), BF16)] + [jax.ShapeDtypeStruct(_P_SHAPES[nm], F32) for nm in _P_NAMES]
    res = pl.pallas_call(
        body, grid=(N_SWEEP,), in_specs=in_specs, out_specs=out_specs + r_out_specs, out_shape=out_shape + r_out_shape,
        scratch_shapes=[pltpu.VMEM(_S_SHAPES[nm], F32) for nm in _S_NAMES] + [pltpu.VMEM((8, 512), F32)]
        + [pltpu.VMEM((HEADS, 128, 128), F32) for _ in range(2 * len(_P_BLOCKDIAG))] + r_sems,
        compiler_params=_params(("arbitrary",)), name="mixer_bwd",
    )(pm, pm, dab, *states, *[p[nm] for nm in _P_NAMES], *rider_ins)
    return res[0], dict(zip(_P_NAMES, res[1:1 + n_p])), res[1 + n_p:]


def _tok(width):
    return pl.BlockSpec((TOK_TILE, width), lambda i: (i, 0))


def _once(shape):
    zeros = (0,) * len(shape)
    return pl.BlockSpec(shape, lambda i: zeros, pipeline_mode=pl.Buffered(1))


def _rms_fwd(x):
    r = lax.rsqrt(_mean(x * x) + EPS)
    return x * r, r


def _rms_bwd(dy, xn, r, g):
    gd = dy * g
    return r * (gd - xn * _mean(xn * gd))


def _tiled_call(body, in_specs, out_specs, out_shape, args, name, rider=None, rider_ins=()):
    r_in, r_out_specs, r_out_shape, r_scratch = _rider_specs(rider, rider_ins)
    n_in, n_out = len(in_specs), len(out_specs)

    def hosted(*refs):
        ins, ride_in, outs, ride_out, scratch = _split(refs, n_in, len(r_in), n_out, len(r_out_specs), len(r_scratch))
        i = pl.program_id(0)
        _ride(rider, ("first",), i == 0, ride_in, ride_out, scratch)
        body(*ins, *outs)
        _ride(rider, ("middle",), i == N_TOK_TILE - 2, ride_in, ride_out, scratch)
        _ride(rider, ("last",), i == N_TOK_TILE - 1, ride_in, ride_out, scratch)

    res = pl.pallas_call(
        hosted, grid=(N_TOK_TILE,), in_specs=list(in_specs) + r_in, out_specs=list(out_specs) + r_out_specs,
        out_shape=list(out_shape) + r_out_shape, scratch_shapes=r_scratch,
        compiler_params=_params(("arbitrary",)), name=name,
    )(*args, *rider_ins)
    return res[:n_out], res[n_out:]


def _in_proj(x, g_pre, wt_in, rider=None, rider_ins=()):
    def body(x_ref, g_ref, wt_ref, pm_ref, gab_ref, h_ref):
        xn, _ = _rms_fwd(x_ref[...])
        h = (xn * g_ref[...]).astype(BF16)
        h_ref[...] = h
        pm_ref[:, 0:PM_XM] = _nt(h, wt_ref[0:IN_ALOW, :])
        pm_ref[:, PM_XM:PM_AL] = _nt(h, wt_ref[IN_XM:IN_GATES, :])
        pm_ref[:, PM_AL:PM_W] = _nt(h, wt_ref[IN_ALOW:IN_ALOW + 128, :])
        gab_ref[...] = _nt(h, wt_ref[IN_GATES:D_IN, :])

    return _tiled_call(
        body, [_tok(D_MODEL), _once((1, D_MODEL)), _once((D_IN, D_MODEL))], [_tok(PM_W), _tok(GAB_W), _tok(D_MODEL)],
        [jax.ShapeDtypeStruct((SEQ, PM_W), F32), jax.ShapeDtypeStruct((SEQ, GAB_W), F32),
         jax.ShapeDtypeStruct((SEQ, D_MODEL), BF16)], (x, g_pre, wt_in), "in_proj", rider, rider_ins)


def _merge_fwd(ab, gab, x, w_pa4, w_pb4, w_o, g_post, rider=None, rider_ins=()):
    def body(ab_ref, gab_ref, x_ref, wpa_ref, wpb_ref, wo_ref, g_ref, x1_ref, mix_ref, mg_ref):
        a = ab_ref[:, 0:512]
        b = ab_ref[:, 512:1024]
        for j in range(N_CHIP):
            blk = slice(j * 256, (j + 1) * 256)
            ya = jnp.dot(a, wpa_ref[j], preferred_element_type=F32)
            yb = jnp.dot(b, wpb_ref[j], preferred_element_type=F32)
            sa = _sigmoid(gab_ref[:, j * 256:(j + 1) * 256])
            sb = _sigmoid(gab_ref[:, 1024 + j * 256:1024 + (j + 1) * 256])
            mg_ref[:, blk] = (sa * ya + sb * yb).astype(BF16)
        mix = jnp.dot(mg_ref[...], wo_ref[...], preferred_element_type=F32)
        mix_ref[...] = mix
        mn, _ = _rms_fwd(mix)
        x1_ref[...] = x_ref[...] + mn * g_ref[...]

    return _tiled_call(
        body, [_tok(1024), _tok(GAB_W), _tok(D_MODEL), _once((N_CHIP, 512, 256)), _once((N_CHIP, 512, 256)),
               _once((D_MODEL, D_MODEL)), _once((1, D_MODEL))], [_tok(D_MODEL), _tok(D_MODEL), _tok(D_MODEL)],
        [jax.ShapeDtypeStruct((SEQ, D_MODEL), F32), jax.ShapeDtypeStruct((SEQ, D_MODEL), F32),
         jax.ShapeDtypeStruct((SEQ, D_MODEL), BF16)], (ab, gab, x, w_pa4, w_pb4, w_o, g_post), "merge_fwd", rider, rider_ins)


def _mlp(x1, target, g_pre, g_post, w_up4, w_down_a4, w_down_b4):
    def body(x1_ref, t_ref, gpre_ref, gpost_ref, wup_ref, wda_ref, wdb_ref,
             dx1_ref, u_ref, dd_ref, h2_ref, dpre_ref, dgpost_ref, dgpre_ref, loss_ref):
        @pl.when(pl.program_id(0) == 0)
        def _():
            dgpost_ref[...] = jnp.zeros_like(dgpost_ref)
            dgpre_ref[...] = jnp.zeros_like(dgpre_ref)
            loss_ref[...] = jnp.zeros_like(loss_ref)

        x1 = x1_ref[...]
        gpre = gpre_ref[...]
        gpost = gpost_ref[...]
        xn2, r2 = _rms_fwd(x1)
        h2 = (xn2 * gpre).astype(BF16)
        h2_ref[...] = h2
        rl = []
        d = jnp.zeros((TOK_TILE, D_MODEL), F32)
        for j in range(N_CHIP):
            blk = slice(j * 1024, (j + 1) * 1024)
            r = jnp.maximum(jnp.dot(h2, wup_ref[j], preferred_element_type=F32), 0.0)
            rl.append(r)
            u = (r * r).astype(BF16)
            u_ref[:, blk] = u
            d = d + jnp.dot(u[:, 0:512], wda_ref[j], preferred_element_type=F32)
            d = d + jnp.dot(u[:, 512:1024], wdb_ref[j], preferred_element_type=F32)
        dn, r3 = _rms_fwd(d)
        diff = x1 + dn * gpost - t_ref[...]
        loss_ref[...] += jnp.sum(diff * diff, keepdims=True) * (0.5 / D_MODEL)
        dy = diff * (1.0 / D_MODEL)
        dgpost_ref[...] += jnp.sum(dy * dn, axis=0, keepdims=True)
        dd = _rms_bwd(dy, dn, r3, gpost).astype(BF16)
        dd_ref[...] = dd
        dh2 = jnp.zeros((TOK_TILE, D_MODEL), F32)
        for j in range(N_CHIP):
            blk = slice(j * 1024, (j + 1) * 1024)
            du = jnp.concatenate([_nt(dd, wda_ref[j]), _nt(dd, wdb_ref[j])], axis=1)
            dpre = (du * (2.0 * rl[j])).astype(BF16)
            dpre_ref[:, blk] = dpre
            dh2 = dh2 + _nt(dpre, wup_ref[j])
        dgpre_ref[...] += jnp.sum(dh2 * xn2, axis=0, keepdims=True)
        dx1_ref[...] = dy + _rms_bwd(dh2, xn2, r2, gpre)

    acc = pl.BlockSpec((1, D_MODEL), lambda i: (0, 0))
    return pl.pallas_call(
        body, grid=(N_TOK_TILE,),
        in_specs=[_tok(D_MODEL), _tok(D_MODEL), _once((1, D_MODEL)), _once((1, D_MODEL)),
                  _once((N_CHIP, D_MODEL, 1024)), _once((N_CHIP, 512, D_MODEL)), _once((N_CHIP, 512, D_MODEL))],
        out_specs=[_tok(D_MODEL), _tok(D_FF), _tok(D_MODEL), _tok(D_MODEL), _tok(D_FF), acc, acc,
                   pl.BlockSpec((1, 128), lambda i: (0, 0))],
        out_shape=[jax.ShapeDtypeStruct((SEQ, D_MODEL), F32), jax.ShapeDtypeStruct((SEQ, D_FF), BF16),
                   jax.ShapeDtypeStruct((SEQ, D_MODEL), BF16), jax.ShapeDtypeStruct((SEQ, D_MODEL), BF16),
                   jax.ShapeDtypeStruct((SEQ, D_FF), BF16), jax.ShapeDtypeStruct((1, D_MODEL), F32),
                   jax.ShapeDtypeStruct((1, D_MODEL), F32), jax.ShapeDtypeStruct((1, 128), F32)],
        compiler_params=_params(("arbitrary",)), name="mlp_fwd_bwd",
    )(x1, target, g_pre, g_post, w_up4, w_down_a4, w_down_b4)


def _merge_bwd(dx1, mix, ab, gab, w_pa4, w_pb4, w_o, g_post):
    def body(dx1_ref, mix_ref, ab_ref, gab_ref, wpa_ref, wpb_ref, wo_ref, g_ref,
             dmix_ref, dya_ref, dyb_ref, dgab_ref, dab_ref, dg_ref):
        @pl.when(pl.program_id(0) == 0)
        def _():
            dg_ref[...] = jnp.zeros_like(dg_ref)

        dx1 = dx1_ref[...]
        mn, r = _rms_fwd(mix_ref[...])
        dg_ref[...] += jnp.sum(dx1 * mn, axis=0, keepdims=True)
        dmix = _rms_bwd(dx1, mn, r, g_ref[...]).astype(BF16)
        dmix_ref[...] = dmix
        dmerged = _nt(dmix, wo_ref[...])
        a = ab_ref[:, 0:512]
        b = ab_ref[:, 512:1024]
        da = jnp.zeros((TOK_TILE, 512), F32)
        db = jnp.zeros((TOK_TILE, 512), F32)
        for j in range(N_CHIP):
            blk = slice(j * 256, (j + 1) * 256)
            blk_b = slice(1024 + j * 256, 1024 + (j + 1) * 256)
            dm = dmerged[:, blk]
            ya = jnp.dot(a, wpa_ref[j], preferred_element_type=F32)
            yb = jnp.dot(b, wpb_ref[j], preferred_element_type=F32)
            sa = _sigmoid(gab_ref[:, blk])
            sb = _sigmoid(gab_ref[:, blk_b])
            dya = (dm * sa).astype(BF16)
            dyb = (dm * sb).astype(BF16)
            dya_ref[:, blk] = dya
            dyb_ref[:, blk] = dyb
            dgab_ref[:, blk] = (dm * ya * sa * (1.0 - sa)).astype(BF16)
            dgab_ref[:, blk_b] = (dm * yb * sb * (1.0 - sb)).astype(BF16)
            da = da + _nt(dya, wpa_ref[j])
            db = db + _nt(dyb, wpb_ref[j])
        dab_ref[:, 0:512] = da
        dab_ref[:, 512:1024] = db

    return pl.pallas_call(
        body, grid=(N_TOK_TILE,),
        in_specs=[_tok(D_MODEL), _tok(D_MODEL), _tok(1024), _tok(GAB_W), _once((N_CHIP, 512, 256)),
                  _once((N_CHIP, 512, 256)), _once((D_MODEL, D_MODEL)), _once((1, D_MODEL))],
        out_specs=[_tok(D_MODEL), _tok(D_MODEL), _tok(D_MODEL), _tok(GAB_W), _tok(1024),
                   pl.BlockSpec((1, D_MODEL), lambda i: (0, 0))],
        out_shape=[jax.ShapeDtypeStruct((SEQ, D_MODEL), BF16), jax.ShapeDtypeStruct((SEQ, D_MODEL), BF16),
                   jax.ShapeDtypeStruct((SEQ, D_MODEL), BF16), jax.ShapeDtypeStruct((SEQ, GAB_W), BF16),
                   jax.ShapeDtypeStruct((SEQ, 1024), F32), jax.ShapeDtypeStruct((1, D_MODEL), F32)],
        compiler_params=_params(("arbitrary",)), name="merge_bwd",
    )(dx1, mix, ab, gab, w_pa4, w_pb4, w_o, g_post)


def _in_proj_bwd(dpm, dgab, x, dx1, g_pre, wt_in, rider=None, rider_ins=()):
    def body(dpm_ref, dgab_ref, x_ref, dx1_ref, g_ref, wt_ref, dx_ref, dg_ref):
        @pl.when(pl.program_id(0) == 0)
        def _():
            dg_ref[...] = jnp.zeros_like(dg_ref)

        dh = jnp.dot(dpm_ref[:, 0:PM_XM], wt_ref[0:IN_ALOW, :], preferred_element_type=F32)
        dh = dh + jnp.dot(dpm_ref[:, PM_XM:PM_AL], wt_ref[IN_XM:IN_GATES, :], preferred_element_type=F32)
        dh = dh + jnp.dot(dpm_ref[:, PM_AL:PM_W], wt_ref[IN_ALOW:IN_ALOW + 128, :], preferred_element_type=F32)
        dh = dh + jnp.dot(dgab_ref[...], wt_ref[IN_GATES:D_IN, :], preferred_element_type=F32)
        xn, r = _rms_fwd(x_ref[...])
        dg_ref[...] += jnp.sum(dh * xn, axis=0, keepdims=True)
        dx_ref[...] = dx1_ref[...] + _rms_bwd(dh, xn, r, g_ref[...])

    return _tiled_call(
        body, [_tok(PM_W), _tok(GAB_W), _tok(D_MODEL), _tok(D_MODEL), _once((1, D_MODEL)), _once((D_IN, D_MODEL))],
        [_tok(D_MODEL), pl.BlockSpec((1, D_MODEL), lambda i: (0, 0))],
        [jax.ShapeDtypeStruct((SEQ, D_MODEL), F32), jax.ShapeDtypeStruct((1, D_MODEL), F32)],
        (dpm, dgab, x, dx1, g_pre, wt_in), "in_proj_bwd", rider, rider_ins)


def _dw_in(dpm, dgab, h):
    n_pm = PM_AL // 512
    n_blk = n_pm + GAB_W // 512

    def body(dpm_ref, dgab_ref, dal_ref, h_ref, o_ref):
        i = pl.program_id(0)
        off = pl.multiple_of(i * 512 + 16 * (i >= 3).astype(jnp.int32), 16)

        @pl.when(i < n_pm)
        def _():
            o_ref[pl.ds(off, 512), :] = _tn(dpm_ref[...], h_ref[...]).astype(BF16)

        @pl.when(i >= n_pm)
        def _():
            o_ref[pl.ds(off, 512), :] = _tn(dgab_ref[...], h_ref[...]).astype(BF16)

        @pl.when(i == 0)
        def _():
            o_ref[IN_ALOW:IN_XM, :] = _tn(dal_ref[...], h_ref[...])[0:IN_XM - IN_ALOW].astype(BF16)

    return pl.pallas_call(
        body, grid=(n_blk,),
        in_specs=[pl.BlockSpec((SEQ, 512), lambda i: (0, jnp.minimum(i, n_pm - 1))),
                  pl.BlockSpec((SEQ, 512), lambda i: (0, jnp.maximum(i - n_pm, 0))),
                  pl.BlockSpec((SEQ, 128), lambda i: (0, PM_AL // 128)),
                  _once((SEQ, D_MODEL))],
        out_specs=pl.BlockSpec((D_IN, D_MODEL), lambda i: (0, 0)),
        out_shape=jax.ShapeDtypeStruct((D_IN, D_MODEL), BF16),
        compiler_params=_params(("arbitrary",)), name="dw_in",
    )(dpm, dgab, dpm, h)


def _tn_matmul(a, b, name, shards=1, tm=512):
    m, n = a.shape[1], b.shape[1]
    tm = min(tm, m)
    tn = n // shards if shards > 1 else min(n, 1024)

    def body(a_ref, b_ref, o_ref):
        o_ref[...] = _tn(a_ref[...], b_ref[...]).astype(BF16)

    if shards > 1:
        out_spec = pl.BlockSpec((None, tm, tn), lambda i, j: (j, i, 0))
        out_shape = jax.ShapeDtypeStruct((shards, m, tn), BF16)
    else:
        out_spec = pl.BlockSpec((tm, tn), lambda i, j: (i, j))
        out_shape = jax.ShapeDtypeStruct((m, n), BF16)
    return pl.pallas_call(
        body, grid=(m // tm, n // tn),
        in_specs=[pl.BlockSpec((SEQ, tm), lambda i, j: (0, i)), pl.BlockSpec((SEQ, tn), lambda i, j: (0, j))],
        out_specs=out_spec, out_shape=out_shape,
        compiler_params=_params(("arbitrary", "arbitrary")), name=name,
    )(a, b)


MESH = pl.DeviceIdType.MESH
ANY = pl.BlockSpec(memory_space=pl.ANY)
VMEM_WHOLE = pl.BlockSpec(memory_space=pltpu.VMEM)

_BIG = ("w_in", "w_pa", "w_pb", "w_o", "w_up", "w_down")
_BIG_SHARD = {"w_in": (IN_SHARD, D_MODEL), "w_pa": (512, 256), "w_pb": (512, 256), "w_o": (256, D_MODEL),
              "w_up": (D_MODEL, 1024), "w_down": (1024, D_MODEL),
              "w_down_a": (512, D_MODEL), "w_down_b": (512, D_MODEL)}
_BIG_SPLIT = {"w_in": 1, "w_pa": 0, "w_pb": 0, "w_o": 0, "w_up": 0, "w_down": 0, "w_down_a": 0, "w_down_b": 0}


def _half(ref, e, name, lead=0):
    axis = _BIG_SPLIT[name]
    size = _BIG_SHARD[name][axis] // 2
    start = pl.multiple_of(e * size, 128 if axis == 1 else 16)
    idx = [pl.ds(0, ref.shape[a]) for a in range(lead)]
    idx += [pl.ds(start, size), pl.ds(0, _BIG_SHARD[name][1])] if axis == 0 else [pl.ds(0, _BIG_SHARD[name][0]), pl.ds(start, size)]
    return ref.at[tuple(idx)]


def _half_shape(name):
    r, c = _BIG_SHARD[name]
    return (r // 2, c) if _BIG_SPLIT[name] == 0 else (r, c // 2)


def _remote(src, dst, send_sems, recv_sems, k, to):
    return pltpu.make_async_remote_copy(src_ref=src, dst_ref=dst, send_sem=send_sems.at[k], recv_sem=recv_sems.at[k],
                                        device_id=to, device_id_type=MESH)


def _mesh_place():
    x, y, c = lax.axis_index("x"), lax.axis_index("y"), lax.axis_index("c")
    return x, y, c, [(1 - x, y), (x, 1 - y), (1 - x, 1 - y)]


class _Gather:
    def __init__(self, names, small=()):
        self.names = tuple(names)
        self.nb = len(self.names)
        self.n = self.nb + len(small)
        self.n_sems = 6 * self.n
        self.out_shape = [jax.ShapeDtypeStruct((N_CHIP,) + _BIG_SHARD[nm], BF16) for nm in self.names]
        self.out_shape += [jax.ShapeDtypeStruct((N_CHIP,) + s.shape, s.dtype) for s in small]

    def _ici(self, ins, outs, ss, rs, k, j, peer, slot, c):
        if k < self.nb:
            return _remote(_half(ins[k], c, self.names[k]), _half(outs[k].at[slot], c, self.names[k]), ss, rs, 6 * k + j,
                           (*peer, c))
        return _remote(ins[k], outs[k].at[slot], ss, rs, 6 * k + j, (*peer, c))

    def _passed(self, outs, ss, rs, k, j, slot, e, sibling):
        part = _half(outs[k].at[slot], e, self.names[k])
        return _remote(part, part, ss, rs, 6 * k + 3 + j, sibling)

    def first(self, ins, outs, ss, rs):
        x, y, c, peers = _mesh_place()
        me = 2 * x + y
        for k in range(self.n):
            for j, peer in enumerate(peers):
                self._ici(ins, outs, ss, rs, k, j, peer, me, c).start()
        for k in range(self.n):
            outs[k][me] = ins[k][...]

    def middle(self, ins, outs, ss, rs):
        x, y, c, peers = _mesh_place()
        for j, (px, py) in enumerate(peers):
            for k in range(self.nb):
                self._ici(ins, outs, ss, rs, k, j, (px, py), 2 * px + py, c).wait_recv()
                self._passed(outs, ss, rs, k, j, 2 * px + py, c, (x, y, 1 - c)).start()

    def last(self, ins, outs, ss, rs):
        x, y, c, peers = _mesh_place()
        for j, (px, py) in enumerate(peers):
            for k in range(self.n):
                if k < self.nb:
                    self._passed(outs, ss, rs, k, j, 2 * px + py, 1 - c, (x, y, 1 - c)).wait_recv()
                    self._passed(outs, ss, rs, k, j, 2 * px + py, c, (x, y, 1 - c)).wait_send()
                else:
                    self._ici(ins, outs, ss, rs, k, j, (px, py), 2 * px + py, c).wait_recv()
                self._ici(ins, outs, ss, rs, k, j, (px, py), 2 * x + y, c).wait_send()


def _run_alone(rider, ins, name):
    def body(*refs):
        r_in, r_out, sems = _split(refs, len(ins), len(rider.out_shape), 2)
        rider.first(r_in, r_out, *sems)
        rider.middle(r_in, r_out, *sems)
        rider.last(r_in, r_out, *sems)

    return pl.pallas_call(
        body, in_specs=[VMEM_WHOLE] * len(ins), out_specs=[VMEM_WHOLE] * len(rider.out_shape), out_shape=rider.out_shape,
        scratch_shapes=[pltpu.SemaphoreType.DMA((rider.n_sems,)), pltpu.SemaphoreType.DMA((rider.n_sems,))],
        compiler_params=_params(), name=name,
    )(*ins)


def _presum(names, grads, name):
    n = len(grads)

    def body(*refs):
        g_refs, got_refs, stage_refs, (send_sems, recv_sems, local_sems) = _split(refs, n, n, n, 3)
        x, y, c = lax.axis_index("x"), lax.axis_index("y"), lax.axis_index("c")

        def stage(e):
            cps = [pltpu.make_async_copy(_half(g_refs[k], e, names[k], lead=1), stage_refs[k], local_sems.at[k])
                   for k in range(n)]
            for cp in cps:
                cp.start()
            return cps

        staged = stage(1 - c)
        sends = []
        for k in range(n):
            staged[k].wait()
            cp = _remote(stage_refs[k], got_refs[k], send_sems, recv_sems, k, (x, y, 1 - c))
            cp.start()
            sends.append(cp)
        for cp in sends:
            cp.wait_send()
        staged = stage(c)
        for k in range(n):
            sends[k].wait_recv()
            staged[k].wait()

            @pl.loop(0, N_CHIP)
            def _(j):
                got_refs[k][j] = (got_refs[k][j].astype(F32) + stage_refs[k][j].astype(F32)).astype(BF16)

    half = [jax.ShapeDtypeStruct((N_CHIP,) + _half_shape(nm), BF16) for nm in names]
    return pl.pallas_call(
        body, in_specs=[ANY] * n, out_specs=[VMEM_WHOLE] * n, out_shape=half,
        scratch_shapes=[pltpu.VMEM(h.shape, h.dtype) for h in half]
        + [pltpu.SemaphoreType.DMA((n,)), pltpu.SemaphoreType.DMA((n,)), pltpu.SemaphoreType.DMA((n,))],
        compiler_params=_params(), name=name,
    )(*grads)


class _SendPartials:
    def __init__(self, names, small_shape=None):
        self.n = len(names)
        self.small = small_shape is not None
        self.n_sems = 3 * self.n + 7
        self.out_shape = [jax.ShapeDtypeStruct((N_CHIP,) + _half_shape(nm), BF16) for nm in names]
        if self.small:
            self.out_shape.append(jax.ShapeDtypeStruct((N_DEV,) + small_shape, F32))

    def _piece(self, ins, outs, ss, rs, k, j, peer, src_slot, dst_slot, c):
        return _remote(ins[k].at[src_slot], outs[k].at[dst_slot], ss, rs, 3 * k + j, (*peer, c))

    def _small(self, ins, outs, ss, rs, r, other, slot):
        return _remote(ins[self.n], outs[self.n].at[slot], ss, rs, 3 * self.n + r, other)

    @staticmethod
    def _others(x, y, c):
        return [(x, y, 1 - c), (1 - x, y, c), (1 - x, y, 1 - c), (x, 1 - y, c), (x, 1 - y, 1 - c),
                (1 - x, 1 - y, c), (1 - x, 1 - y, 1 - c)]

    def first(self, ins, outs, ss, rs):
        x, y, c, peers = _mesh_place()
        me = 2 * x + y
        for k in range(self.n):
            for j, (px, py) in enumerate(peers):
                self._piece(ins, outs, ss, rs, k, j, (px, py), 2 * px + py, me, c).start()
        if self.small:
            for r, other in enumerate(self._others(x, y, c)):
                self._small(ins, outs, ss, rs, r, other, 4 * x + 2 * y + c).start()
            outs[self.n][4 * x + 2 * y + c] = ins[self.n][...]
        for k in range(self.n):
            outs[k][me] = ins[k][me]

    def middle(self, ins, outs, ss, rs):
        pass

    def last(self, ins, outs, ss, rs):
        x, y, c, peers = _mesh_place()
        me = 2 * x + y
        for k in range(self.n):
            for j, (px, py) in enumerate(peers):
                self._piece(ins, outs, ss, rs, k, j, (px, py), me, 2 * px + py, c).wait_recv()
                self._piece(ins, outs, ss, rs, k, j, (px, py), 2 * px + py, me, c).wait_send()
        if self.small:
            for r, (px, py, pc) in enumerate(self._others(x, y, c)):
                self._small(ins, outs, ss, rs, r, (px, py, pc), 4 * px + 2 * py + pc).wait_recv()
                self._small(ins, outs, ss, rs, r, (px, py, pc), 4 * x + 2 * y + c).wait_send()


def _sum_swap(names, parts, small):
    n = len(parts)
    everyone = _SendPartials((), small.shape)

    def body(*refs):
        p_refs, (small_ref,), o_refs, (osmall_ref,), (all_ref,), (send_sems, recv_sems, ss_small, rs_small) = _split(
            refs, n, 1, n, 1, 1, 4)
        x, y, c = lax.axis_index("x"), lax.axis_index("y"), lax.axis_index("c")
        everyone.first([small_ref], [all_ref], ss_small, rs_small)

        def mine(k):
            part = _half(o_refs[k], c, names[k])
            return _remote(part, part, send_sems, recv_sems, k, (x, y, 1 - c))

        for k in range(n):
            for e in range(2):
                @pl.when(c == e)
                def _():
                    g = p_refs[k][0].astype(F32)
                    for s in range(1, N_CHIP):
                        g = g + p_refs[k][s].astype(F32)
                    r, cols = _half_shape(names[k])
                    if _BIG_SPLIT[names[k]] == 0:
                        o_refs[k][e * r:(e + 1) * r, :] = g
                    else:
                        o_refs[k][:, e * cols:(e + 1) * cols] = g
            mine(k).start()
        for k in range(n):
            theirs = _half(o_refs[k], 1 - c, names[k])
            _remote(theirs, theirs, send_sems, recv_sems, k, (x, y, 1 - c)).wait_recv()
            mine(k).wait_send()
        everyone.last([small_ref], [all_ref], ss_small, rs_small)
        g = all_ref[0]
        for d in range(1, N_DEV):
            g = g + all_ref[d]
        osmall_ref[...] = g

    res = pl.pallas_call(
        body, in_specs=[VMEM_WHOLE] * (n + 1), out_specs=[VMEM_WHOLE] * (n + 1),
        out_shape=[jax.ShapeDtypeStruct(_BIG_SHARD[nm], F32) for nm in names] + [jax.ShapeDtypeStruct(small.shape, F32)],
        scratch_shapes=[pltpu.VMEM((N_DEV,) + small.shape, F32), pltpu.SemaphoreType.DMA((n,)), pltpu.SemaphoreType.DMA((n,)),
                        pltpu.SemaphoreType.DMA((everyone.n_sems,)), pltpu.SemaphoreType.DMA((everyone.n_sems,))],
        compiler_params=_params(), name="sum_swap",
    )(*parts, small)
    return res[:n], res[n]


def _tile(rows, cols, itemsize, budget):
    t = cols if rows % 16 else rows
    other = rows if rows % 16 else cols
    step = 256 if rows % 16 else 32
    while t % step == 0 and t * other * itemsize > budget:
        t //= 2
    return (rows, t) if rows % 16 else (t, cols)


def _adamw_math(w, g, m, v):
    m = ADAM_B1 * m + (1.0 - ADAM_B1) * g
    v = ADAM_B2 * v + (1.0 - ADAM_B2) * (g * g)
    m_hat = m / (1.0 - ADAM_B1 ** ADAM_STEP)
    v_hat = v / (1.0 - ADAM_B2 ** ADAM_STEP)
    delta = -ADAM_LR * (m_hat / (jnp.sqrt(v_hat) + ADAM_EPS) + ADAM_WD * w)
    return delta, m, v


def _adamw_big(g, w, m, v, name):
    r, c = w.shape
    tr, tc = _tile(r, c, 4, 1024 * 1024)

    def body(g_ref, w_ref, m_ref, v_ref, d_ref, nm_ref, nv_ref):
        d_ref[...], nm_ref[...], nv_ref[...] = _adamw_math(w_ref[...], g_ref[...], m_ref[...], v_ref[...])

    blk = pl.BlockSpec((tr, tc), lambda i, l: (i, l))
    return pl.pallas_call(
        body, grid=(r // tr, c // tc), in_specs=[blk, blk, blk, blk],
        out_specs=[blk, blk, blk], out_shape=[jax.ShapeDtypeStruct((r, c), F32)] * 3,
        compiler_params=_params(("arbitrary", "arbitrary")), name=name,
    )(g, w, m, v)


def _adamw_rows(g, w, m, v, name):
    r, _, c = w.shape
    tc = 128

    def body(g_ref, w_ref, m_ref, v_ref, g3_ref, d_ref, nm_ref, nv_ref):
        g = g_ref[...]
        g3_ref[:, 0, :] = g
        d_ref[:, 0, :], nm_ref[:, 0, :], nv_ref[:, 0, :] = _adamw_math(w_ref[:, 0, :], g, m_ref[:, 0, :], v_ref[:, 0, :])

    rows = pl.BlockSpec((r, 1, tc), lambda l: (0, 0, l))
    return pl.pallas_call(
        body, grid=(c // tc,), in_specs=[pl.BlockSpec((r, tc), lambda l: (0, l)), rows, rows, rows],
        out_specs=[rows] * 4, out_shape=[jax.ShapeDtypeStruct((r, 1, c), F32)] * 4,
        compiler_params=_params(("arbitrary",)), name=name,
    )(g, w, m, v)


def _adamw_small(ws, gs, ms, vs):
    n = len(ws)

    def body(*refs):
        w_refs, g_refs, m_refs, v_refs, d_refs, nm_refs, nv_refs = _split(refs, *([n] * 7))
        for k in range(n):
            d_refs[k][...], nm_refs[k][...], nv_refs[k][...] = _adamw_math(w_refs[k][...], g_refs[k][...], m_refs[k][...],
                                                                             v_refs[k][...])

    shapes = [jax.ShapeDtypeStruct(w.shape, F32) for w in ws]
    res = pl.pallas_call(body, out_shape=shapes * 3, name="adamw_small")(*ws, *gs, *ms, *vs)
    return res[:n], res[n:2 * n], res[2 * n:]


def _pack(arrs):
    flat = jnp.concatenate([a.reshape(-1) for a in arrs])
    rows = -(-flat.shape[0] // 1024) * 8
    return jnp.pad(flat, (0, rows * 128 - flat.shape[0])).reshape(rows, 128)


def _unpack(buf, shapes):
    flat = buf.reshape(-1)
    out, off = [], 0
    for s in shapes:
        size = 1
        for d in s:
            size *= d
        out.append(flat[off:off + size].reshape(s))
        off += size
    return out


def _block_rows(w):
    return jnp.pad(w.reshape(512, 4), ((0, 0), (0, 124)))


def _cols(a4):
    return jnp.transpose(a4, (1, 0, 2)).reshape(a4.shape[1], -1)


_LATE = ("w_pa", "w_pb", "w_o", "w_up", "w_down")
_RIDE_IN_PROJ = ("w_pa", "w_pb", "w_o")
_RIDE_MIXER = ("w_up", "w_down_a")
_RIDE_MERGE = ("w_down_b",)


def _full_weights(gathered):
    joined = {"w_in": (D_IN, D_MODEL), "w_o": (D_MODEL, D_MODEL)}
    return {n: (a.reshape(joined[n]) if n in joined else a) for n, a in gathered.items()}


def _local_step(x, target, w, sp, late_shards=None):
    sp = {n: (a.reshape(1, -1) if a.ndim == 1 else a) for n, a in sp.items()}
    wau = jnp.zeros((128, 256), F32).at[0:16].set(sp["w_a_up"])
    wif = jnp.zeros((1536, 128), F32).at[:, 0:8].set(sp["w_if"])
    bif = jnp.zeros((1, 128), F32).at[:, 0:8].set(sp["b_if"])
    p = {"wau": wau, "bau": sp["b_a_up"], "ggla": sp["g_gla_norm"], "cw": sp["conv_w"], "cb": sp["conv_b"],
         "wq": _block_rows(sp["w_q_ml"]), "wk": _block_rows(sp["w_k_ml"]), "wv": _block_rows(sp["w_v_ml"]),
         "wif": wif, "bif": bif, "skip": sp["ml_skip"], "gml": sp["g_ml_norm"]}

    if late_shards is None:
        (pm, gab, h), _ = _in_proj(x, sp["g_pre_mix"], w["w_in"])
        ab, *states = _mixer_fwd(pm, p)
        (x1, mix, merged), _ = _merge_fwd(ab, gab, x, w["w_pa"], w["w_pb"], w["w_o"], sp["g_post_mix"])
    else:
        shard = dict(zip(_LATE, late_shards))
        shard["w_down_a"], shard["w_down_b"] = shard["w_down"][0:512], shard["w_down"][512:1024]
        (pm, gab, h), got = _in_proj(x, sp["g_pre_mix"], w["w_in"], _Gather(_RIDE_IN_PROJ),
                                     [shard[n] for n in _RIDE_IN_PROJ])
        w = dict(w, **_full_weights(dict(zip(_RIDE_IN_PROJ, got))))
        ab, *rest = _mixer_fwd(pm, p, _Gather(_RIDE_MIXER), [shard[n] for n in _RIDE_MIXER])
        states = rest[:4]
        w.update(_full_weights(dict(zip(_RIDE_MIXER, rest[4:]))))
        (x1, mix, merged), got = _merge_fwd(ab, gab, x, w["w_pa"], w["w_pb"], w["w_o"], sp["g_post_mix"],
                                            _Gather(_RIDE_MERGE), [shard[n] for n in _RIDE_MERGE])
        w.update(_full_weights(dict(zip(_RIDE_MERGE, got))))
    dx1, u, dd, h2, dpre, dg_post_mlp, dg_pre_mlp, loss = _mlp(x1, target, sp["g_pre_mlp"], sp["g_post_mlp"],
                                                                w["w_up"], w["w_down_a"], w["w_down_b"])
    dmix, dya, dyb, dgab, dab, dg_post_mix = _merge_bwd(dx1, mix, ab, gab, w["w_pa"], w["w_pb"], w["w_o"], sp["g_post_mix"])
    big = {
        "w_pa": _tn_matmul(ab[:, 0:512], dya, "dw_pa", shards=N_CHIP),
        "w_pb": _tn_matmul(ab[:, 512:1024], dyb, "dw_pb", shards=N_CHIP),
        "w_o": _tn_matmul(merged, dmix, "dw_o"),
        "w_up": _tn_matmul(h2, dpre, "dw_up", shards=N_CHIP),
        "w_down": _tn_matmul(u, dd, "dw_down"),
    }
    if late_shards is None:
        dpm, dp, _ = _mixer_bwd(pm, dab, states, p)
    else:
        partial = _presum(_LATE, [big[n].reshape((N_CHIP,) + _BIG_SHARD[n]) for n in _LATE], "presum_late")
        dpm, dp, parts = _mixer_bwd(pm, dab, states, p, _SendPartials(_LATE), partial)
        big = dict(zip(_LATE, parts))
    big["w_in"] = _dw_in(dpm, dgab, h)
    if late_shards is None:
        (dx, dg_pre_mix), _ = _in_proj_bwd(dpm, dgab, x, dx1, sp["g_pre_mix"], w["w_in"])
    else:
        partial = _presum(("w_in",), [big["w_in"].reshape((N_CHIP,) + _BIG_SHARD["w_in"])], "presum_w_in")
        (dx, dg_pre_mix), parts = _in_proj_bwd(dpm, dgab, x, dx1, sp["g_pre_mix"], w["w_in"], _SendPartials(("w_in",)),
                                               partial)
        big["w_in"] = parts[0]
    small = {
        "g_pre_mix": dg_pre_mix, "b_a_up": dp["bau"], "g_gla_norm": dp["ggla"], "conv_b": dp["cb"],
        "w_q_ml": dp["wq"][:, 0:4].reshape(128, 4, 4), "w_k_ml": dp["wk"][:, 0:4].reshape(128, 4, 4),
        "w_v_ml": dp["wv"][:, 0:4].reshape(128, 4, 4),
        "b_if": dp["bif"][:, 0:8], "ml_skip": dp["skip"], "g_ml_norm": dp["gml"], "g_post_mix": dg_post_mix,
        "g_pre_mlp": dg_pre_mlp, "g_post_mlp": dg_post_mlp, "w_a_up": dp["wau"][0:16], "conv_w": dp["cw"],
        "w_if": dp["wif"][:, 0:8], "loss": loss[:, 0:1],
    }
    return dx, big, small


_SMALL_REPL = ("g_pre_mix", "b_a_up", "g_gla_norm", "conv_b", "w_q_ml", "w_k_ml", "w_v_ml", "b_if", "ml_skip",
               "g_ml_norm", "g_post_mix", "g_pre_mlp", "g_post_mlp")
_SMALL_SHARDED = ("w_a_up", "conv_w", "w_if")
_SMALL_ORDER = _SMALL_REPL + _SMALL_SHARDED + ("loss",)
_WEIGHTS = ("g_pre_mix", "w_in", "w_a_up", "b_a_up", "g_gla_norm", "conv_w", "conv_b", "w_q_ml", "w_k_ml", "w_v_ml",
            "w_if", "b_if", "ml_skip", "g_ml_norm", "w_pa", "w_pb", "w_o", "g_post_mix", "g_pre_mlp", "w_up", "w_down",
            "g_post_mlp")


def _as_shard(name, a):
    return jnp.transpose(a, (2, 0, 1)) if name == "w_in" else a[0]


def _from_shard(name, a):
    return jnp.transpose(a, (1, 2, 0)) if name == "w_in" else a[None]


def kernel(x, g_pre_mix, w_in, w_a_up, b_a_up, g_gla_norm, conv_w, conv_b, w_q_ml, w_k_ml, w_v_ml, w_if, b_if, ml_skip, g_ml_norm, w_pa, w_pb, w_o, g_post_mix, g_pre_mlp, w_up, w_down, g_post_mlp, loss_target, m_g_pre_mix, m_w_in, m_w_a_up, m_b_a_up, m_g_gla_norm, m_conv_w, m_conv_b, m_w_q_ml, m_w_k_ml, m_w_v_ml, m_w_if, m_b_if, m_ml_skip, m_g_ml_norm, m_w_pa, m_w_pb, m_w_o, m_g_post_mix, m_g_pre_mlp, m_w_up, m_w_down, m_g_post_mlp, v_g_pre_mix, v_w_in, v_w_a_up, v_b_a_up, v_g_gla_norm, v_conv_w, v_conv_b, v_w_q_ml, v_w_k_ml, v_w_v_ml, v_w_if, v_b_if, v_ml_skip, v_g_ml_norm, v_w_pa, v_w_pb, v_w_o, v_g_post_mix, v_g_pre_mlp, v_w_up, v_w_down, v_g_post_mlp):
    args = dict(locals())
    wts = {n: _as_shard(n, args[n]) for n in _WEIGHTS}
    mom = {n: _as_shard(n, args["m_" + n]) for n in _WEIGHTS}
    var = {n: _as_shard(n, args["v_" + n]) for n in _WEIGHTS}
    chip = 2 * lax.axis_index("x") + lax.axis_index("y")

    first = ("w_in",) + _SMALL_SHARDED
    gathered = dict(zip(first, _run_alone(_Gather(("w_in",), [wts[n] for n in _SMALL_SHARDED]),
                                          [wts[n][:, 0, :].astype(BF16) if n == "w_in" else wts[n] for n in first],
                                          "gather_first")))
    sp = {n: wts[n] for n in _SMALL_REPL}
    sp["w_a_up"] = _cols(gathered["w_a_up"])
    sp["conv_w"] = _cols(gathered["conv_w"])
    sp["w_if"] = gathered["w_if"].reshape(1536, 8)

    dx, big, small = _local_step(x[0], loss_target[0], _full_weights({"w_in": gathered["w_in"]}), sp,
                                 late_shards=[wts[n].astype(BF16) for n in _LATE])

    small_shapes = [small[n].shape for n in _SMALL_ORDER]
    packed = _pack([small[n] for n in _SMALL_ORDER])
    sums, small_sum = _sum_swap(_BIG, [big[n] for n in _BIG], packed)

    grads, delta, new_m, new_v = {}, {}, {}, {}
    for n, g in zip(_BIG, sums):
        if n == "w_in":
            g, d, nm, nv = _adamw_rows(g, wts[n], mom[n], var[n], "adamw_" + n)
        else:
            d, nm, nv = _adamw_big(g, wts[n], mom[n], var[n], "adamw_" + n)
        grads[n], delta[n], new_m[n], new_v[n] = (_from_shard(n, a) for a in (g, d, nm, nv))
    summed = dict(zip(_SMALL_ORDER, _unpack(small_sum, small_shapes)))
    loss = summed["loss"].reshape(())
    for n in _SMALL_REPL:
        grads[n] = summed[n].reshape(args[n].shape)
    grads["w_a_up"] = lax.dynamic_slice_in_dim(summed["w_a_up"], chip * 64, 64, axis=1)[None]
    grads["conv_w"] = lax.dynamic_slice_in_dim(summed["conv_w"], chip * 128, 128, axis=1)[None]
    grads["w_if"] = lax.dynamic_slice_in_dim(summed["w_if"], chip * 384, 384, axis=0)[None]
    small_names = _SMALL_REPL + _SMALL_SHARDED
    upd = _adamw_small([args[n] for n in small_names], [grads[n] for n in small_names],
                       [args["m_" + n] for n in small_names], [args["v_" + n] for n in small_names])
    for dst, arrs in zip((delta, new_m, new_v), upd):
        dst.update(zip(small_names, arrs))

    outs = [loss, dx[None]]
    for group in (grads, delta, new_m, new_v):
        outs += [group[n] for n in _WEIGHTS]
    return tuple(outs)
```

```python
import functools

import jax
import jax.numpy as jnp
from jax import lax
from jax.experimental import pallas as pl
from jax.experimental.pallas import tpu as pltpu

F32 = jnp.float32
BF16 = jnp.bfloat16

SEQ = 2048
D_MODEL = 1024
CHUNK = 64
N_CHUNK = SEQ // CHUNK
HEADS = 4
GLA_DK = 64
GLA_DV = 128
ML_DH = 128
D_FF = 4096
EPS = 1e-6
N_CHIP = 4
N_DEV = 8
TOK_TILE = 256
N_TOK_TILE = SEQ // TOK_TILE
SWEEP = 2
assert CHUNK == 64
N_SWEEP = N_CHUNK // SWEEP

PM_W = 2688
PM_XM = 1536
PM_OP = 2048
PM_AL = 2560
GAB_W = 2048
D_IN = 4624
IN_SHARD = D_IN // N_CHIP
IN_ALOW = 1536
IN_XM = 1552
IN_GATES = 2576

ADAM_LR = 0.001
ADAM_B1 = 0.9
ADAM_B2 = 0.999
ADAM_EPS = 1e-08
ADAM_WD = 0.01
ADAM_STEP = 10

VMEM_LIMIT = 56 * 1024 * 1024


def _params(sem=None):
    return pltpu.CompilerParams(dimension_semantics=sem, vmem_limit_bytes=VMEM_LIMIT)


def _dot(a, b, ca, cb):
    return lax.dot_general(a.astype(BF16), b.astype(BF16), (((ca,), (cb,)), ((), ())), preferred_element_type=F32)


def _pmm_nn(a, b):
    return _dot(a, b, 1, 0)


def _pmm_nt(a, b):
    return _dot(a, b, 1, 1)


def _pmm_tn(a, b):
    return _dot(a, b, 0, 0)


def _pcmm(c, x):
    return lax.dot_general(c, x, (((1,), (0,)), ((), ())), precision=lax.Precision.HIGHEST, preferred_element_type=F32)


@jax.custom_vjp
def _mm_nn(a, b):
    return _dot(a, b, 1, 0)


@jax.custom_vjp
def _mm_nt(a, b):
    return _dot(a, b, 1, 1)


@jax.custom_vjp
def _mm_tn(a, b):
    return _dot(a, b, 0, 0)


_mm_nn.defvjp(lambda a, b: (_dot(a, b, 1, 0), (a, b)), lambda r, g: (_mm_nt(g, r[1]), _mm_tn(r[0], g)))
_mm_nt.defvjp(lambda a, b: (_dot(a, b, 1, 1), (a, b)), lambda r, g: (_mm_nn(g, r[1]), _mm_tn(g, r[0])))
_mm_tn.defvjp(lambda a, b: (_dot(a, b, 0, 0), (a, b)), lambda r, g: (_mm_nt(r[1], g), _mm_nn(r[0], g)))


@jax.custom_vjp
def _cmm(c, x):
    return _pcmm(c, x)


_cmm.defvjp(
    lambda c, x: (_pcmm(c, x), c),
    lambda c, g: (jnp.zeros_like(c), lax.dot_general(c, g, (((0,), (0,)), ((), ())), precision=lax.Precision.HIGHEST,
                                                      preferred_element_type=F32)),
)

_PLAIN_OPS = (_pmm_nn, _pmm_nt, _pmm_tn, _pcmm)
_VJP_OPS = (_mm_nn, _mm_nt, _mm_tn, _cmm)


def _sigmoid(x):
    return 0.5 * (jnp.tanh(0.5 * x) + 1.0)


def _log_sigmoid(x):
    return jnp.minimum(x, 0.0) - jnp.log(1.0 + jnp.exp(-jnp.abs(x)))


def _mean(x):
    return jnp.mean(x, axis=-1, keepdims=True)


def _nt(a, b):
    return lax.dot_general(a, b, (((1,), (1,)), ((), ())), preferred_element_type=F32)


def _tn(a, b):
    return lax.dot_general(a, b, (((0,), (0,)), ((), ())), preferred_element_type=F32)


def _mixer_chunk(ops, p, st, pm, xprev8):
    mm_nn, mm_nt, mm_tn, cmm = ops
    n_rows = pm.shape[0]
    n_ch = n_rows // CHUNK
    row = lax.broadcasted_iota(jnp.int32, (n_rows, n_rows), 0)
    col = lax.broadcasted_iota(jnp.int32, (n_rows, n_rows), 1)
    tri = jnp.logical_and((row >> 6) == (col >> 6), row >= col).astype(F32)
    causal = tri[0:CHUNK, 0:CHUNK] > 0.0
    q = pm[:, 0:256]
    k = pm[:, 256:512]
    v = pm[:, 512:1024]
    g = pm[:, 1024:1536]
    xm = pm[:, PM_XM:PM_XM + 512]
    opre = pm[:, PM_OP:PM_OP + 512]
    alow = pm[:, PM_AL:PM_AL + 128]
    hs = range(HEADS)
    cs = range(n_ch)
    pairs = [(i, h) for i in cs for h in hs]
    rs = [slice(i * CHUNK, (i + 1) * CHUNK) for i in cs]
    last = [slice((i + 1) * CHUNK - 1, (i + 1) * CHUNK) for i in cs]
    s6 = [slice(h * GLA_DK, (h + 1) * GLA_DK) for h in hs]
    s12 = [slice(h * 128, (h + 1) * 128) for h in hs]

    la = _log_sigmoid(mm_nn(alow, p["wau"]) + p["bau"]) * (1.0 / 16.0)
    cum = cmm(tri, la)
    cum_last = [cum[last[i], :] for i in cs]
    to_end = jnp.concatenate([cum_last[i] - cum[rs[i], :] for i in cs], axis=0)
    e_pos = jnp.exp(cum)
    e_neg = jnp.exp(-cum)
    qs = q * (GLA_DK ** -0.5)
    qp = qs * e_pos
    qn = qs * e_neg
    kp = k * e_pos
    kn = k * e_neg
    kl = k * jnp.exp(to_end)
    dec = [jnp.exp(cum_last[i]) for i in cs]
    a_fwd = {(i, h): mm_nt(qp[rs[i], s6[h]], kn[rs[i], s6[h]]) for i, h in pairs}
    a_bwd = {(i, h): mm_nt(qn[rs[i], s6[h]], kp[rs[i], s6[h]]) for i, h in pairs}
    s_chunk = {(i, h): mm_tn(v[rs[i], s12[h]], kl[rs[i], s6[h]]) for i, h in pairs}
    mem = {(0, h): st["S"][h] for h in hs}
    for i, h in pairs:
        mem[(i + 1, h)] = mem[(i, h)] * dec[i][:, s6[h]] + s_chunk[(i, h)]
    s_new = [mem[(n_ch, h)] for h in hs]
    o_inter = {(i, h): mm_nt(qp[rs[i], s6[h]], mem[(i, h)]) for i, h in pairs}
    scores = {ih: jnp.where(causal, a_fwd[ih], a_bwd[ih]) for ih in pairs}
    o = {(i, h): mm_nn(scores[(i, h)], v[rs[i], s12[h]]) + o_inter[(i, h)] for i, h in pairs}
    o = {ih: o[ih] * lax.rsqrt(_mean(o[ih] * o[ih]) + EPS) * p["ggla"] for ih in pairs}
    gate = g * _sigmoid(g)
    out_a = {(i, h): o[(i, h)] * gate[rs[i], s12[h]] for i, h in pairs}

    xx = jnp.concatenate([xprev8, xm], axis=0)
    pre = p["cb"]
    for j in range(4):
        pre = pre + p["cw"][j:j + 1, :] * xx[5 + j:5 + j + n_rows, :]
    xc = pre * _sigmoid(pre)
    qm = [mm_nn(xc[:, s12[h]], p["wq"][h]) for h in hs]
    km = [mm_nn(xc[:, s12[h]], p["wk"][h]) for h in hs]
    vm = [mm_nn(xm[:, s12[h]], p["wv"][h]) for h in hs]
    qcat = jnp.concatenate(qm, axis=1)
    kcat = jnp.concatenate(km, axis=1)
    vcat = jnp.concatenate(vm, axis=1)
    gates = (mm_nn(qcat, p["wif"][0:512]) + mm_nn(kcat, p["wif"][512:1024]) + mm_nn(vcat, p["wif"][1024:1536])
             + p["bif"])
    lf = _log_sigmoid(gates)
    fc = cmm(tri, lf)
    gates_t = gates.T
    fc_t = fc.T
    ks = [km[h] * (ML_DH ** -0.5) for h in hs]
    qk = {(i, h): mm_nt(qm[h][rs[i]], ks[h][rs[i]]) for i, h in pairs}
    li_c = {(i, h): gates[rs[i], h:h + 1] for i, h in pairs}
    fc_c = {(i, h): fc[rs[i], 4 + h:5 + h] for i, h in pairs}
    f_last = {(i, h): fc[last[i], 4 + h:5 + h] for i, h in pairs}
    a = {ih: f_last[ih] - fc_c[ih] + li_c[ih] for ih in pairs}
    m_loc = {ih: jnp.max(a[ih], axis=0, keepdims=True) for ih in pairs}
    kw = {(i, h): ks[h][rs[i]] * jnp.exp(a[(i, h)] - m_loc[(i, h)]) for i, h in pairs}
    c_chunk = {(i, h): mm_tn(kw[(i, h)], vm[h][rs[i]]) for i, h in pairs}
    c_in = {(0, h): st["C"][h] for h in hs}
    n_in = {(0, h): st["n"][h] for h in hs}
    m_in = {(0, h): st["m"][h][:, 0:1] for h in hs}
    for i, h in pairs:
        m_nx = jnp.maximum(f_last[(i, h)] + m_in[(i, h)], m_loc[(i, h)])
        sp = jnp.exp(f_last[(i, h)] + m_in[(i, h)] - m_nx)
        sl = jnp.exp(m_loc[(i, h)] - m_nx)
        c_in[(i + 1, h)] = sp * c_in[(i, h)] + sl * c_chunk[(i, h)]
        n_in[(i + 1, h)] = sp * n_in[(i, h)] + sl * jnp.sum(kw[(i, h)], axis=0, keepdims=True)
        m_in[(i + 1, h)] = m_nx
    q_c = {(i, h): mm_nn(qm[h][rs[i]], c_in[(i, h)]) for i, h in pairs}
    log_d = {(i, h): gates_t[h:h + 1, rs[i]] - jnp.abs(fc_c[(i, h)] - fc_t[4 + h:5 + h, rs[i]]) for i, h in pairs}
    g_int = {ih: fc_c[ih] + m_in[ih] for ih in pairs}
    m_t = {ih: jnp.maximum(g_int[ih], jnp.max(log_d[ih], axis=1, keepdims=True)) for ih in pairs}
    s = {ih: qk[ih] * jnp.exp(log_d[ih] - m_t[ih]) for ih in pairs}
    scl = {ih: jnp.exp(g_int[ih] - m_t[ih]) for ih in pairs}
    num = {(i, h): mm_nn(s[(i, h)], vm[h][rs[i]]) + scl[(i, h)] * q_c[(i, h)] for i, h in pairs}
    den = {(i, h): jnp.sum(s[(i, h)], axis=1, keepdims=True)
           + scl[(i, h)] * jnp.sum(qm[h][rs[i]] * n_in[(i, h)], axis=1, keepdims=True) for i, h in pairs}
    den = {ih: jnp.maximum(jnp.abs(den[ih]), jnp.exp(-m_t[ih])) for ih in pairs}
    open_gate = _sigmoid(opre)
    hc = {(i, h): num[(i, h)] / den[(i, h)] * open_gate[rs[i], s12[h]] for i, h in pairs}
    d0 = {ih: hc[ih] - _mean(hc[ih]) for ih in pairs}
    y = {ih: d0[ih] * lax.rsqrt(_mean(d0[ih] * d0[ih]) + EPS) for ih in pairs}
    skipped = p["skip"] * xc
    out_b = {(i, h): y[(i, h)] * p["gml"][:, s12[h]] + skipped[rs[i], s12[h]] for i, h in pairs}
    ab = jnp.concatenate([jnp.concatenate([out_a[(i, h)] for h in hs] + [out_b[(i, h)] for h in hs], axis=1) for i in cs],
                         axis=0)
    new = {"S": s_new, "C": [c_in[(n_ch, h)] for h in hs], "n": [n_in[(n_ch, h)] for h in hs],
           "m": [jnp.broadcast_to(m_in[(n_ch, h)], (1, ML_DH)) for h in hs]}
    return ab, new


_P_NAMES = ("wau", "bau", "ggla", "cw", "cb", "wq", "wk", "wv", "wif", "bif", "skip", "gml")
_P_SHAPES = {
    "wau": (128, 256), "bau": (1, 256), "ggla": (1, 128), "cw": (4, 512), "cb": (1, 512),
    "wq": (512, 128), "wk": (512, 128), "wv": (512, 128),
    "wif": (1536, 128), "bif": (1, 128), "skip": (1, 512), "gml": (1, 512),
}
_P_BLOCKDIAG = ("wq", "wk", "wv")
_S_NAMES = ("S", "C", "n", "m")
_S_SHAPES = {"S": (HEADS, GLA_DV, GLA_DK), "C": (HEADS, ML_DH, ML_DH), "n": (HEADS, 1, ML_DH), "m": (HEADS, 1, ML_DH)}


def _per_head(ref):
    return [ref[h] for h in range(HEADS)]


def _block_mask():
    r = lax.broadcasted_iota(jnp.int32, (128, 128), 0)
    c = lax.broadcasted_iota(jnp.int32, (128, 128), 1)
    same_block = (r >> 2) == (c >> 2)
    spread = jnp.logical_and(r < 4, (c & 3) == r)
    return same_block.astype(F32), spread.astype(F32)


def _expand_blockdiag(w_ref, dense_ref):
    same_block, spread = _block_mask()
    for h in range(HEADS):
        tiled = _pmm_nn(w_ref[h * 128:(h + 1) * 128, :], spread)
        dense_ref[h] = tiled * same_block


def _collect_blockdiag(ddense_ref, dw_ref):
    same_block, spread = _block_mask()
    for h in range(HEADS):
        dw_ref[h * 128:(h + 1) * 128, :] = lax.dot_general(
            ddense_ref[h] * same_block, spread, (((1,), (1,)), ((), ())), precision=lax.Precision.HIGHEST,
            preferred_element_type=F32)


def _const_spec(shape):
    zeros = (0,) * len(shape)
    return pl.BlockSpec(shape, lambda i: zeros)


def _split(refs, *counts):
    out, at = [], 0
    for c in counts:
        out.append(refs[at:at + c])
        at += c
    assert at == len(refs)
    return out


def _ride(rider, phases, cond, ins, outs, sems):
    if rider is None:
        return
    lands, (send_sems, recv_sems, flush_sems) = sems[:-3], sems[-3:]

    @pl.when(cond)
    def _():
        for phase in phases:
            getattr(rider, phase)(ins, lands, send_sems, recv_sems)
        if "last" in phases:
            flush = [pltpu.make_async_copy(lands[k], outs[k], flush_sems.at[k]) for k in range(len(outs))]
            for cp in flush:
                cp.start()
            for cp in flush:
                cp.wait()


def _rider_specs(rider, rider_ins):
    if rider is None:
        return [], [], [], []
    scratch = [pltpu.VMEM(s.shape, s.dtype) for s in rider.out_shape]
    scratch += [pltpu.SemaphoreType.DMA((rider.n_sems,)), pltpu.SemaphoreType.DMA((rider.n_sems,)),
                pltpu.SemaphoreType.DMA((len(rider.out_shape),))]
    return [VMEM_WHOLE] * len(rider_ins), [ANY] * len(rider.out_shape), list(rider.out_shape), scratch


def _mixer_fwd(pm, p, rider=None, rider_ins=()):
    n_p = len(_P_NAMES)
    r_in, r_out_specs, r_out_shape, r_sems = _rider_specs(rider, rider_ins)

    def body(*refs):
        (pm_ref, xprev_ref), p_list, ride_in, (ab_ref,), so_refs, ride_out, sc_refs, dense_list, sems = _split(
            refs, 2, n_p, len(r_in), 1, 4, len(r_out_specs), 4, 3, len(r_sems))
        p_refs = dict(zip(_P_NAMES, p_list))
        dense = dict(zip(_P_BLOCKDIAG, dense_list))
        n = pl.program_id(0)
        _ride(rider, ("first",), n == 0, ride_in, ride_out, sems)

        @pl.when(n == 0)
        def _():
            for r in sc_refs:
                r[...] = jnp.zeros_like(r)
            for nm in _P_BLOCKDIAG:
                _expand_blockdiag(p_refs[nm], dense[nm])

        st = {name: _per_head(r) for name, r in zip(_S_NAMES, sc_refs)}
        pv = {nm: (_per_head(dense[nm]) if nm in _P_BLOCKDIAG else p_refs[nm][...]) for nm in _P_NAMES}
        for name, r in zip(_S_NAMES, so_refs):
            for h in range(HEADS):
                r[0, h] = st[name][h]
        xprev8 = jnp.where(n > 0, xprev_ref[CHUNK - 8:CHUNK, :], 0.0)
        ab, st = _mixer_chunk(_PLAIN_OPS, pv, st, pm_ref[...], xprev8)
        ab_ref[...] = ab.astype(BF16)
        for name, r in zip(_S_NAMES, sc_refs):
            for h in range(HEADS):
                r[h] = st[name][h]
        _ride(rider, ("middle",), n == N_SWEEP - 2, ride_in, ride_out, sems)
        _ride(rider, ("last",), n == N_SWEEP - 1, ride_in, ride_out, sems)

    in_specs = [pl.BlockSpec((SWEEP * CHUNK, PM_W), lambda i: (i, 0)),
                pl.BlockSpec((CHUNK, 512), lambda i: (jnp.maximum(SWEEP * i - 1, 0), PM_XM // 512))]
    in_specs += [_const_spec(_P_SHAPES[nm]) for nm in _P_NAMES] + r_in
    out_specs = [pl.BlockSpec((SWEEP * CHUNK, 1024), lambda i: (i, 0))]
    out_shape = [jax.ShapeDtypeStruct((SEQ, 1024), BF16)]
    for nm in _S_NAMES:
        shp = _S_SHAPES[nm]
        out_specs.append(pl.BlockSpec((1,) + shp, lambda i: (i, 0, 0, 0)))
        out_shape.append(jax.ShapeDtypeStruct((N_SWEEP,) + shp, F32))
    return pl.pallas_call(
        body, grid=(N_SWEEP,), in_specs=in_specs, out_specs=out_specs + r_out_specs, out_shape=out_shape + r_out_shape,
        scratch_shapes=[pltpu.VMEM(_S_SHAPES[nm], F32) for nm in _S_NAMES]
        + [pltpu.VMEM((HEADS, 128, 128), F32) for _ in _P_BLOCKDIAG] + r_sems,
        compiler_params=_params(("arbitrary",)), name="mixer_fwd",
    )(pm, pm, *[p[nm] for nm in _P_NAMES], *rider_ins)


def _mixer_bwd(pm, dab, states, p, rider=None, rider_ins=()):
    n_p = len(_P_NAMES)
    r_in, r_out_specs, r_out_shape, r_sems = _rider_specs(rider, rider_ins)

    def body(*refs):
        ((pm_ref, xprev_ref, dab_ref), si_refs, p_list, ride_in, (dpm_ref,), dp_list, ride_out, ds_refs, (carry_ref,),
         dense_list, ddense_list, sems) = _split(refs, 3, 4, n_p, len(r_in), 1, n_p, len(r_out_specs), 4, 1, 3, 3, len(r_sems))
        p_refs = dict(zip(_P_NAMES, p_list))
        dp_refs = dict(zip(_P_NAMES, dp_list))
        dense = dict(zip(_P_BLOCKDIAG, dense_list))
        ddense = dict(zip(_P_BLOCKDIAG, ddense_list))
        i = pl.program_id(0)
        blk = N_SWEEP - 1 - i
        _ride(rider, ("first",), i == 0, ride_in, ride_out, sems)

        @pl.when(i == 0)
        def _():
            for r in ds_refs:
                r[...] = jnp.zeros_like(r)
            for nm in _P_NAMES:
                if nm in _P_BLOCKDIAG:
                    ddense[nm][...] = jnp.zeros_like(ddense[nm])
                    _expand_blockdiag(p_refs[nm], dense[nm])
                else:
                    dp_refs[nm][...] = jnp.zeros_like(dp_refs[nm])
            carry_ref[...] = jnp.zeros_like(carry_ref)

        pv = {nm: (_per_head(dense[nm]) if nm in _P_BLOCKDIAG else p_refs[nm][...]) for nm in _P_NAMES}
        dst = {name: _per_head(r) for name, r in zip(_S_NAMES, ds_refs)}
        st = {name: [r[0, h] for h in range(HEADS)] for name, r in zip(_S_NAMES, si_refs)}
        xprev8 = jnp.where(blk > 0, xprev_ref[CHUNK - 8:CHUNK, :], 0.0)
        _, vjp = jax.vjp(functools.partial(_mixer_chunk, _VJP_OPS), pv, st, pm_ref[...], xprev8)
        dp_sum, dst, dpm, dxprev8 = vjp((dab_ref[...], dst))
        reach = jnp.concatenate([jnp.zeros((SWEEP * CHUNK - 8, 512), F32), carry_ref[...]], axis=0)
        dpm_ref[:, 0:PM_XM] = dpm[:, 0:PM_XM].astype(BF16)
        dpm_ref[:, PM_XM:PM_XM + 512] = (dpm[:, PM_XM:PM_XM + 512] + reach).astype(BF16)
        dpm_ref[:, PM_XM + 512:PM_W] = dpm[:, PM_XM + 512:PM_W].astype(BF16)
        carry_ref[...] = dxprev8
        for name, r in zip(_S_NAMES, ds_refs):
            for h in range(HEADS):
                r[h] = dst[name][h]
        for nm in _P_NAMES:
            if nm in _P_BLOCKDIAG:
                for h in range(HEADS):
                    ddense[nm][h] += dp_sum[nm][h]
            else:
                dp_refs[nm][...] += dp_sum[nm]

        @pl.when(i == N_SWEEP - 1)
        def _():
            for nm in _P_BLOCKDIAG:
                _collect_blockdiag(ddense[nm], dp_refs[nm])

        _ride(rider, ("middle",), i == N_SWEEP - 2, ride_in, ride_out, sems)
        _ride(rider, ("last",), i == N_SWEEP - 1, ride_in, ride_out, sems)

    rev = lambda i: (N_SWEEP - 1 - i, 0)
    in_specs = [pl.BlockSpec((SWEEP * CHUNK, PM_W), rev),
                pl.BlockSpec((CHUNK, 512), lambda i: (jnp.maximum(SWEEP * (N_SWEEP - 1 - i) - 1, 0), PM_XM // 512)),
                pl.BlockSpec((SWEEP * CHUNK, 1024), rev)]
    for nm in _S_NAMES:
        in_specs.append(pl.BlockSpec((1,) + _S_SHAPES[nm], lambda i: (N_SWEEP - 1 - i, 0, 0, 0)))
    in_specs += [_const_spec(_P_SHAPES[nm]) for nm in _P_NAMES] + r_in
    out_specs = [pl.BlockSpec((SWEEP * CHUNK, PM_W), rev)] + [_const_spec(_P_SHAPES[nm]) for nm in _P_NAMES]
    out_shape = [jax.ShapeDtypeStruct((SEQ, PM_W), BF16)] + [jax.ShapeDtypeStruct(_P_SHAPES[nm], F32) for nm in _P_NAMES]
    res = pl.pallas_call(
        body, grid=(N_SWEEP,), in_specs=in_specs, out_specs=out_specs + r_out_specs, out_shape=out_shape + r_out_shape,
        scratch_shapes=[pltpu.VMEM(_S_SHAPES[nm], F32) for nm in _S_NAMES] + [pltpu.VMEM((8, 512), F32)]
        + [pltpu.VMEM((HEADS, 128, 128), F32) for _ in range(2 * len(_P_BLOCKDIAG))] + r_sems,
        compiler_params=_params(("arbitrary",)), name="mixer_bwd",
    )(pm, pm, dab, *states, *[p[nm] for nm in _P_NAMES], *rider_ins)
    return res[0], dict(zip(_P_NAMES, res[1:1 + n_p])), res[1 + n_p:]


def _tok(width):
    return pl.BlockSpec((TOK_TILE, width), lambda i: (i, 0))


def _once(shape):
    zeros = (0,) * len(shape)
    return pl.BlockSpec(shape, lambda i: zeros, pipeline_mode=pl.Buffered(1))


def _rms_fwd(x):
    r = lax.rsqrt(_mean(x * x) + EPS)
    return x * r, r


def _rms_bwd(dy, xn, r, g):
    gd = dy * g
    return r * (gd - xn * _mean(xn * gd))


def _tiled_call(body, in_specs, out_specs, out_shape, args, name, rider=None, rider_ins=()):
    r_in, r_out_specs, r_out_shape, r_scratch = _rider_specs(rider, rider_ins)
    n_in, n_out = len(in_specs), len(out_specs)

    def hosted(*refs):
        ins, ride_in, outs, ride_out, scratch = _split(refs, n_in, len(r_in), n_out, len(r_out_specs), len(r_scratch))
        i = pl.program_id(0)
        _ride(rider, ("first",), i == 0, ride_in, ride_out, scratch)
        body(*ins, *outs)
        _ride(rider, ("middle",), i == N_TOK_TILE - 2, ride_in, ride_out, scratch)
        _ride(rider, ("last",), i == N_TOK_TILE - 1, ride_in, ride_out, scratch)

    res = pl.pallas_call(
        hosted, grid=(N_TOK_TILE,), in_specs=list(in_specs) + r_in, out_specs=list(out_specs) + r_out_specs,
        out_shape=list(out_shape) + r_out_shape, scratch_shapes=r_scratch,
        compiler_params=_params(("arbitrary",)), name=name,
    )(*args, *rider_ins)
    return res[:n_out], res[n_out:]


def _in_proj(x, g_pre, wt_in, rider=None, rider_ins=()):
    def body(x_ref, g_ref, wt_ref, pm_ref, gab_ref, h_ref):
        xn, _ = _rms_fwd(x_ref[...])
        h = (xn * g_ref[...]).astype(BF16)
        h_ref[...] = h
        pm_ref[:, 0:PM_XM] = _nt(h, wt_ref[0:IN_ALOW, :])
        pm_ref[:, PM_XM:PM_AL] = _nt(h, wt_ref[IN_XM:IN_GATES, :])
        pm_ref[:, PM_AL:PM_W] = _nt(h, wt_ref[IN_ALOW:IN_ALOW + 128, :])
        gab_ref[...] = _nt(h, wt_ref[IN_GATES:D_IN, :])

    return _tiled_call(
        body, [_tok(D_MODEL), _once((1, D_MODEL)), _once((D_IN, D_MODEL))], [_tok(PM_W), _tok(GAB_W), _tok(D_MODEL)],
        [jax.ShapeDtypeStruct((SEQ, PM_W), F32), jax.ShapeDtypeStruct((SEQ, GAB_W), F32),
         jax.ShapeDtypeStruct((SEQ, D_MODEL), BF16)], (x, g_pre, wt_in), "in_proj", rider, rider_ins)


def _merge_fwd(ab, gab, x, w_pa4, w_pb4, w_o, g_post, rider=None, rider_ins=()):
    def body(ab_ref, gab_ref, x_ref, wpa_ref, wpb_ref, wo_ref, g_ref, x1_ref, mix_ref, mg_ref):
        a = ab_ref[:, 0:512]
        b = ab_ref[:, 512:1024]
        for j in range(N_CHIP):
            blk = slice(j * 256, (j + 1) * 256)
            ya = jnp.dot(a, wpa_ref[j], preferred_element_type=F32)
            yb = jnp.dot(b, wpb_ref[j], preferred_element_type=F32)
            sa = _sigmoid(gab_ref[:, j * 256:(j + 1) * 256])
            sb = _sigmoid(gab_ref[:, 1024 + j * 256:1024 + (j + 1) * 256])
            mg_ref[:, blk] = (sa * ya + sb * yb).astype(BF16)
        mix = jnp.dot(mg_ref[...], wo_ref[...], preferred_element_type=F32)
        mix_ref[...] = mix
        mn, _ = _rms_fwd(mix)
        x1_ref[...] = x_ref[...] + mn * g_ref[...]

    return _tiled_call(
        body, [_tok(1024), _tok(GAB_W), _tok(D_MODEL), _once((N_CHIP, 512, 256)), _once((N_CHIP, 512, 256)),
               _once((D_MODEL, D_MODEL)), _once((1, D_MODEL))], [_tok(D_MODEL), _tok(D_MODEL), _tok(D_MODEL)],
        [jax.ShapeDtypeStruct((SEQ, D_MODEL), F32), jax.ShapeDtypeStruct((SEQ, D_MODEL), F32),
         jax.ShapeDtypeStruct((SEQ, D_MODEL), BF16)], (ab, gab, x, w_pa4, w_pb4, w_o, g_post), "merge_fwd", rider, rider_ins)


def _mlp(x1, target, g_pre, g_post, w_up4, w_down_a4, w_down_b4):
    def body(x1_ref, t_ref, gpre_ref, gpost_ref, wup_ref, wda_ref, wdb_ref,
             dx1_ref, u_ref, dd_ref, h2_ref, dpre_ref, dgpost_ref, dgpre_ref, loss_ref):
        @pl.when(pl.program_id(0) == 0)
        def _():
            dgpost_ref[...] = jnp.zeros_like(dgpost_ref)
            dgpre_ref[...] = jnp.zeros_like(dgpre_ref)
            loss_ref[...] = jnp.zeros_like(loss_ref)

        x1 = x1_ref[...]
        gpre = gpre_ref[...]
        gpost = gpost_ref[...]
        xn2, r2 = _rms_fwd(x1)
        h2 = (xn2 * gpre).astype(BF16)
        h2_ref[...] = h2
        rl = []
        d = jnp.zeros((TOK_TILE, D_MODEL), F32)
        for j in range(N_CHIP):
            blk = slice(j * 1024, (j + 1) * 1024)
            r = jnp.maximum(jnp.dot(h2, wup_ref[j], preferred_element_type=F32), 0.0)
            rl.append(r)
            u = (r * r).astype(BF16)
            u_ref[:, blk] = u
            d = d + jnp.dot(u[:, 0:512], wda_ref[j], preferred_element_type=F32)
            d = d + jnp.dot(u[:, 512:1024], wdb_ref[j], preferred_element_type=F32)
        dn, r3 = _rms_fwd(d)
        diff = x1 + dn * gpost - t_ref[...]
        loss_ref[...] += jnp.sum(diff * diff, keepdims=True) * (0.5 / D_MODEL)
        dy = diff * (1.0 / D_MODEL)
        dgpost_ref[...] += jnp.sum(dy * dn, axis=0, keepdims=True)
        dd = _rms_bwd(dy, dn, r3, gpost).astype(BF16)
        dd_ref[...] = dd
        dh2 = jnp.zeros((TOK_TILE, D_MODEL), F32)
        for j in range(N_CHIP):
            blk = slice(j * 1024, (j + 1) * 1024)
            du = jnp.concatenate([_nt(dd, wda_ref[j]), _nt(dd, wdb_ref[j])], axis=1)
            dpre = (du * (2.0 * rl[j])).astype(BF16)
            dpre_ref[:, blk] = dpre
            dh2 = dh2 + _nt(dpre, wup_ref[j])
        dgpre_ref[...] += jnp.sum(dh2 * xn2, axis=0, keepdims=True)
        dx1_ref[...] = dy + _rms_bwd(dh2, xn2, r2, gpre)

    acc = pl.BlockSpec((1, D_MODEL), lambda i: (0, 0))
    return pl.pallas_call(
        body, grid=(N_TOK_TILE,),
        in_specs=[_tok(D_MODEL), _tok(D_MODEL), _once((1, D_MODEL)), _once((1, D_MODEL)),
                  _once((N_CHIP, D_MODEL, 1024)), _once((N_CHIP, 512, D_MODEL)), _once((N_CHIP, 512, D_MODEL))],
        out_specs=[_tok(D_MODEL), _tok(D_FF), _tok(D_MODEL), _tok(D_MODEL), _tok(D_FF), acc, acc,
                   pl.BlockSpec((1, 128), lambda i: (0, 0))],
        out_shape=[jax.ShapeDtypeStruct((SEQ, D_MODEL), F32), jax.ShapeDtypeStruct((SEQ, D_FF), BF16),
                   jax.ShapeDtypeStruct((SEQ, D_MODEL), BF16), jax.ShapeDtypeStruct((SEQ, D_MODEL), BF16),
                   jax.ShapeDtypeStruct((SEQ, D_FF), BF16), jax.ShapeDtypeStruct((1, D_MODEL), F32),
                   jax.ShapeDtypeStruct((1, D_MODEL), F32), jax.ShapeDtypeStruct((1, 128), F32)],
        compiler_params=_params(("arbitrary",)), name="mlp_fwd_bwd",
    )(x1, target, g_pre, g_post, w_up4, w_down_a4, w_down_b4)


def _merge_bwd(dx1, mix, ab, gab, w_pa4, w_pb4, w_o, g_post):
    def body(dx1_ref, mix_ref, ab_ref, gab_ref, wpa_ref, wpb_ref, wo_ref, g_ref,
             dmix_ref, dya_ref, dyb_ref, dgab_ref, dab_ref, dg_ref):
        @pl.when(pl.program_id(0) == 0)
        def _():
            dg_ref[...] = jnp.zeros_like(dg_ref)

        dx1 = dx1_ref[...]
        mn, r = _rms_fwd(mix_ref[...])
        dg_ref[...] += jnp.sum(dx1 * mn, axis=0, keepdims=True)
        dmix = _rms_bwd(dx1, mn, r, g_ref[...]).astype(BF16)
        dmix_ref[...] = dmix
        dmerged = _nt(dmix, wo_ref[...])
        a = ab_ref[:, 0:512]
        b = ab_ref[:, 512:1024]
        da = jnp.zeros((TOK_TILE, 512), F32)
        db = jnp.zeros((TOK_TILE, 512), F32)
        for j in range(N_CHIP):
            blk = slice(j * 256, (j + 1) * 256)
            blk_b = slice(1024 + j * 256, 1024 + (j + 1) * 256)
            dm = dmerged[:, blk]
            ya = jnp.dot(a, wpa_ref[j], preferred_element_type=F32)
            yb = jnp.dot(b, wpb_ref[j], preferred_element_type=F32)
            sa = _sigmoid(gab_ref[:, blk])
            sb = _sigmoid(gab_ref[:, blk_b])
            dya = (dm * sa).astype(BF16)
            dyb = (dm * sb).astype(BF16)
            dya_ref[:, blk] = dya
            dyb_ref[:, blk] = dyb
            dgab_ref[:, blk] = (dm * ya * sa * (1.0 - sa)).astype(BF16)
            dgab_ref[:, blk_b] = (dm * yb * sb * (1.0 - sb)).astype(BF16)
            da = da + _nt(dya, wpa_ref[j])
            db = db + _nt(dyb, wpb_ref[j])
        dab_ref[:, 0:512] = da
        dab_ref[:, 512:1024] = db

    return pl.pallas_call(
        body, grid=(N_TOK_TILE,),
        in_specs=[_tok(D_MODEL), _tok(D_MODEL), _tok(1024), _tok(GAB_W), _once((N_CHIP, 512, 256)),
                  _once((N_CHIP, 512, 256)), _once((D_MODEL, D_MODEL)), _once((1, D_MODEL))],
        out_specs=[_tok(D_MODEL), _tok(D_MODEL), _tok(D_MODEL), _tok(GAB_W), _tok(1024),
                   pl.BlockSpec((1, D_MODEL), lambda i: (0, 0))],
        out_shape=[jax.ShapeDtypeStruct((SEQ, D_MODEL), BF16), jax.ShapeDtypeStruct((SEQ, D_MODEL), BF16),
                   jax.ShapeDtypeStruct((SEQ, D_MODEL), BF16), jax.ShapeDtypeStruct((SEQ, GAB_W), BF16),
                   jax.ShapeDtypeStruct((SEQ, 1024), F32), jax.ShapeDtypeStruct((1, D_MODEL), F32)],
        compiler_params=_params(("arbitrary",)), name="merge_bwd",
    )(dx1, mix, ab, gab, w_pa4, w_pb4, w_o, g_post)


def _in_proj_bwd(dpm, dgab, x, dx1, g_pre, wt_in, rider=None, rider_ins=()):
    def body(dpm_ref, dgab_ref, x_ref, dx1_ref, g_ref, wt_ref, dx_ref, dg_ref):
        @pl.when(pl.program_id(0) == 0)
        def _():
            dg_ref[...] = jnp.zeros_like(dg_ref)

        dh = jnp.dot(dpm_ref[:, 0:PM_XM], wt_ref[0:IN_ALOW, :], preferred_element_type=F32)
        dh = dh + jnp.dot(dpm_ref[:, PM_XM:PM_AL], wt_ref[IN_XM:IN_GATES, :], preferred_element_type=F32)
        dh = dh + jnp.dot(dpm_ref[:, PM_AL:PM_W], wt_ref[IN_ALOW:IN_ALOW + 128, :], preferred_element_type=F32)
        dh = dh + jnp.dot(dgab_ref[...], wt_ref[IN_GATES:D_IN, :], preferred_element_type=F32)
        xn, r = _rms_fwd(x_ref[...])
        dg_ref[...] += jnp.sum(dh * xn, axis=0, keepdims=True)
        dx_ref[...] = dx1_ref[...] + _rms_bwd(dh, xn, r, g_ref[...])

    return _tiled_call(
        body, [_tok(PM_W), _tok(GAB_W), _tok(D_MODEL), _tok(D_MODEL), _once((1, D_MODEL)), _once((D_IN, D_MODEL))],
        [_tok(D_MODEL), pl.BlockSpec((1, D_MODEL), lambda i: (0, 0))],
        [jax.ShapeDtypeStruct((SEQ, D_MODEL), F32), jax.ShapeDtypeStruct((1, D_MODEL), F32)],
        (dpm, dgab, x, dx1, g_pre, wt_in), "in_proj_bwd", rider, rider_ins)


def _dw_in(dpm, dgab, h):
    n_pm = PM_AL // 512
    n_blk = n_pm + GAB_W // 512

    def body(dpm_ref, dgab_ref, dal_ref, h_ref, o_ref):
        i = pl.program_id(0)
        off = pl.multiple_of(i * 512 + 16 * (i >= 3).astype(jnp.int32), 16)

        @pl.when(i < n_pm)
        def _():
            o_ref[pl.ds(off, 512), :] = _tn(dpm_ref[...], h_ref[...]).astype(BF16)

        @pl.when(i >= n_pm)
        def _():
            o_ref[pl.ds(off, 512), :] = _tn(dgab_ref[...], h_ref[...]).astype(BF16)

        @pl.when(i == 0)
        def _():
            o_ref[IN_ALOW:IN_XM, :] = _tn(dal_ref[...], h_ref[...])[0:IN_XM - IN_ALOW].astype(BF16)

    return pl.pallas_call(
        body, grid=(n_blk,),
        in_specs=[pl.BlockSpec((SEQ, 512), lambda i: (0, jnp.minimum(i, n_pm - 1))),
                  pl.BlockSpec((SEQ, 512), lambda i: (0, jnp.maximum(i - n_pm, 0))),
                  pl.BlockSpec((SEQ, 128), lambda i: (0, PM_AL // 128)),
                  _once((SEQ, D_MODEL))],
        out_specs=pl.BlockSpec((D_IN, D_MODEL), lambda i: (0, 0)),
        out_shape=jax.ShapeDtypeStruct((D_IN, D_MODEL), BF16),
        compiler_params=_params(("arbitrary",)), name="dw_in",
    )(dpm, dgab, dpm, h)


def _tn_matmul(a, b, name, shards=1, tm=512):
    m, n = a.shape[1], b.shape[1]
    tm = min(tm, m)
    tn = n // shards if shards > 1 else min(n, 1024)

    def body(a_ref, b_ref, o_ref):
        o_ref[...] = _tn(a_ref[...], b_ref[...]).astype(BF16)

    if shards > 1:
        out_spec = pl.BlockSpec((None, tm, tn), lambda i, j: (j, i, 0))
        out_shape = jax.ShapeDtypeStruct((shards, m, tn), BF16)
    else:
        out_spec = pl.BlockSpec((tm, tn), lambda i, j: (i, j))
        out_shape = jax.ShapeDtypeStruct((m, n), BF16)
    return pl.pallas_call(
        body, grid=(m // tm, n // tn),
        in_specs=[pl.BlockSpec((SEQ, tm), lambda i, j: (0, i)), pl.BlockSpec((SEQ, tn), lambda i, j: (0, j))],
        out_specs=out_spec, out_shape=out_shape,
        compiler_params=_params(("arbitrary", "arbitrary")), name=name,
    )(a, b)


MESH = pl.DeviceIdType.MESH
ANY = pl.BlockSpec(memory_space=pl.ANY)
VMEM_WHOLE = pl.BlockSpec(memory_space=pltpu.VMEM)

_BIG = ("w_in", "w_pa", "w_pb", "w_o", "w_up", "w_down")
_BIG_SHARD = {"w_in": (IN_SHARD, D_MODEL), "w_pa": (512, 256), "w_pb": (512, 256), "w_o": (256, D_MODEL),
              "w_up": (D_MODEL, 1024), "w_down": (1024, D_MODEL),
              "w_down_a": (512, D_MODEL), "w_down_b": (512, D_MODEL)}
_BIG_SPLIT = {"w_in": 1, "w_pa": 0, "w_pb": 0, "w_o": 0, "w_up": 0, "w_down": 0, "w_down_a": 0, "w_down_b": 0}


def _half(ref, e, name, lead=0):
    axis = _BIG_SPLIT[name]
    size = _BIG_SHARD[name][axis] // 2
    start = pl.multiple_of(e * size, 128 if axis == 1 else 16)
    idx = [pl.ds(0, ref.shape[a]) for a in range(lead)]
    idx += [pl.ds(start, size), pl.ds(0, _BIG_SHARD[name][1])] if axis == 0 else [pl.ds(0, _BIG_SHARD[name][0]), pl.ds(start, size)]
    return ref.at[tuple(idx)]


def _half_shape(name):
    r, c = _BIG_SHARD[name]
    return (r // 2, c) if _BIG_SPLIT[name] == 0 else (r, c // 2)


def _remote(src, dst, send_sems, recv_sems, k, to):
    return pltpu.make_async_remote_copy(src_ref=src, dst_ref=dst, send_sem=send_sems.at[k], recv_sem=recv_sems.at[k],
                                        device_id=to, device_id_type=MESH)


def _mesh_place():
    x, y, c = lax.axis_index("x"), lax.axis_index("y"), lax.axis_index("c")
    return x, y, c, [(1 - x, y), (x, 1 - y), (1 - x, 1 - y)]


class _Gather:
    def __init__(self, names, small=()):
        self.names = tuple(names)
        self.nb = len(self.names)
        self.n = self.nb + len(small)
        self.n_sems = 6 * self.n
        self.out_shape = [jax.ShapeDtypeStruct((N_CHIP,) + _BIG_SHARD[nm], BF16) for nm in self.names]
        self.out_shape += [jax.ShapeDtypeStruct((N_CHIP,) + s.shape, s.dtype) for s in small]

    def _ici(self, ins, outs, ss, rs, k, j, peer, slot, c):
        if k < self.nb:
            return _remote(_half(ins[k], c, self.names[k]), _half(outs[k].at[slot], c, self.names[k]), ss, rs, 6 * k + j,
                           (*peer, c))
        return _remote(ins[k], outs[k].at[slot], ss, rs, 6 * k + j, (*peer, c))

    def _passed(self, outs, ss, rs, k, j, slot, e, sibling):
        part = _half(outs[k].at[slot], e, self.names[k])
        return _remote(part, part, ss, rs, 6 * k + 3 + j, sibling)

    def first(self, ins, outs, ss, rs):
        x, y, c, peers = _mesh_place()
        me = 2 * x + y
        for k in range(self.n):
            for j, peer in enumerate(peers):
                self._ici(ins, outs, ss, rs, k, j, peer, me, c).start()
        for k in range(self.n):
            outs[k][me] = ins[k][...]

    def middle(self, ins, outs, ss, rs):
        x, y, c, peers = _mesh_place()
        for j, (px, py) in enumerate(peers):
            for k in range(self.nb):
                self._ici(ins, outs, ss, rs, k, j, (px, py), 2 * px + py, c).wait_recv()
                self._passed(outs, ss, rs, k, j, 2 * px + py, c, (x, y, 1 - c)).start()

    def last(self, ins, outs, ss, rs):
        x, y, c, peers = _mesh_place()
        for j, (px, py) in enumerate(peers):
            for k in range(self.n):
                if k < self.nb:
                    self._passed(outs, ss, rs, k, j, 2 * px + py, 1 - c, (x, y, 1 - c)).wait_recv()
                    self._passed(outs, ss, rs, k, j, 2 * px + py, c, (x, y, 1 - c)).wait_send()
                else:
                    self._ici(ins, outs, ss, rs, k, j, (px, py), 2 * px + py, c).wait_recv()
                self._ici(ins, outs, ss, rs, k, j, (px, py), 2 * x + y, c).wait_send()


def _run_alone(rider, ins, name):
    def body(*refs):
        r_in, r_out, sems = _split(refs, len(ins), len(rider.out_shape), 2)
        rider.first(r_in, r_out, *sems)
        rider.middle(r_in, r_out, *sems)
        rider.last(r_in, r_out, *sems)

    return pl.pallas_call(
        body, in_specs=[VMEM_WHOLE] * len(ins), out_specs=[VMEM_WHOLE] * len(rider.out_shape), out_shape=rider.out_shape,
        scratch_shapes=[pltpu.SemaphoreType.DMA((rider.n_sems,)), pltpu.SemaphoreType.DMA((rider.n_sems,))],
        compiler_params=_params(), name=name,
    )(*ins)


def _presum(names, grads, name):
    n = len(grads)

    def body(*refs):
        g_refs, got_refs, stage_refs, (send_sems, recv_sems, local_sems) = _split(refs, n, n, n, 3)
        x, y, c = lax.axis_index("x"), lax.axis_index("y"), lax.axis_index("c")

        def stage(e):
            cps = [pltpu.make_async_copy(_half(g_refs[k], e, names[k], lead=1), stage_refs[k], local_sems.at[k])
                   for k in range(n)]
            for cp in cps:
                cp.start()
            return cps

        staged = stage(1 - c)
        sends = []
        for k in range(n):
            staged[k].wait()
            cp = _remote(stage_refs[k], got_refs[k], send_sems, recv_sems, k, (x, y, 1 - c))
            cp.start()
            sends.append(cp)
        for cp in sends:
            cp.wait_send()
        staged = stage(c)
        for k in range(n):
            sends[k].wait_recv()
            staged[k].wait()

            @pl.loop(0, N_CHIP)
            def _(j):
                got_refs[k][j] = (got_refs[k][j].astype(F32) + stage_refs[k][j].astype(F32)).astype(BF16)

    half = [jax.ShapeDtypeStruct((N_CHIP,) + _half_shape(nm), BF16) for nm in names]
    return pl.pallas_call(
        body, in_specs=[ANY] * n, out_specs=[VMEM_WHOLE] * n, out_shape=half,
        scratch_shapes=[pltpu.VMEM(h.shape, h.dtype) for h in half]
        + [pltpu.SemaphoreType.DMA((n,)), pltpu.SemaphoreType.DMA((n,)), pltpu.SemaphoreType.DMA((n,))],
        compiler_params=_params(), name=name,
    )(*grads)


class _SendPartials:
    def __init__(self, names, small_shape=None):
        self.n = len(names)
        self.small = small_shape is not None
        self.n_sems = 3 * self.n + 7
        self.out_shape = [jax.ShapeDtypeStruct((N_CHIP,) + _half_shape(nm), BF16) for nm in names]
        if self.small:
            self.out_shape.append(jax.ShapeDtypeStruct((N_DEV,) + small_shape, F32))

    def _piece(self, ins, outs, ss, rs, k, j, peer, src_slot, dst_slot, c):
        return _remote(ins[k].at[src_slot], outs[k].at[dst_slot], ss, rs, 3 * k + j, (*peer, c))

    def _small(self, ins, outs, ss, rs, r, other, slot):
        return _remote(ins[self.n], outs[self.n].at[slot], ss, rs, 3 * self.n + r, other)

    @staticmethod
    def _others(x, y, c):
        return [(x, y, 1 - c), (1 - x, y, c), (1 - x, y, 1 - c), (x, 1 - y, c), (x, 1 - y, 1 - c),
                (1 - x, 1 - y, c), (1 - x, 1 - y, 1 - c)]

    def first(self, ins, outs, ss, rs):
        x, y, c, peers = _mesh_place()
        me = 2 * x + y
        for k in range(self.n):
            for j, (px, py) in enumerate(peers):
                self._piece(ins, outs, ss, rs, k, j, (px, py), 2 * px + py, me, c).start()
        if self.small:
            for r, other in enumerate(self._others(x, y, c)):
                self._small(ins, outs, ss, rs, r, other, 4 * x + 2 * y + c).start()
            outs[self.n][4 * x + 2 * y + c] = ins[self.n][...]
        for k in range(self.n):
            outs[k][me] = ins[k][me]

    def middle(self, ins, outs, ss, rs):
        pass

    def last(self, ins, outs, ss, rs):
        x, y, c, peers = _mesh_place()
        me = 2 * x + y
        for k in range(self.n):
            for j, (px, py) in enumerate(peers):
                self._piece(ins, outs, ss, rs, k, j, (px, py), me, 2 * px + py, c).wait_recv()
                self._piece(ins, outs, ss, rs, k, j, (px, py), 2 * px + py, me, c).wait_send()
        if self.small:
            for r, (px, py, pc) in enumerate(self._others(x, y, c)):
                self._small(ins, outs, ss, rs, r, (px, py, pc), 4 * px + 2 * py + pc).wait_recv()
                self._small(ins, outs, ss, rs, r, (px, py, pc), 4 * x + 2 * y + c).wait_send()


def _sum_swap(names, parts, small):
    n = len(parts)
    everyone = _SendPartials((), small.shape)

    def body(*refs):
        p_refs, (small_ref,), o_refs, (osmall_ref,), (all_ref,), (send_sems, recv_sems, ss_small, rs_small) = _split(
            refs, n, 1, n, 1, 1, 4)
        x, y, c = lax.axis_index("x"), lax.axis_index("y"), lax.axis_index("c")
        everyone.first([small_ref], [all_ref], ss_small, rs_small)

        def mine(k):
            part = _half(o_refs[k], c, names[k])
            return _remote(part, part, send_sems, recv_sems, k, (x, y, 1 - c))

        for k in range(n):
            for e in range(2):
                @pl.when(c == e)
                def _():
                    g = p_refs[k][0].astype(F32)
                    for s in range(1, N_CHIP):
                        g = g + p_refs[k][s].astype(F32)
                    r, cols = _half_shape(names[k])
                    if _BIG_SPLIT[names[k]] == 0:
                        o_refs[k][e * r:(e + 1) * r, :] = g
                    else:
                        o_refs[k][:, e * cols:(e + 1) * cols] = g
            mine(k).start()
        for k in range(n):
            theirs = _half(o_refs[k], 1 - c, names[k])
            _remote(theirs, theirs, send_sems, recv_sems, k, (x, y, 1 - c)).wait_recv()
            mine(k).wait_send()
        everyone.last([small_ref], [all_ref], ss_small, rs_small)
        g = all_ref[0]
        for d in range(1, N_DEV):
            g = g + all_ref[d]
        osmall_ref[...] = g

    res = pl.pallas_call(
        body, in_specs=[VMEM_WHOLE] * (n + 1), out_specs=[VMEM_WHOLE] * (n + 1),
        out_shape=[jax.ShapeDtypeStruct(_BIG_SHARD[nm], F32) for nm in names] + [jax.ShapeDtypeStruct(small.shape, F32)],
        scratch_shapes=[pltpu.VMEM((N_DEV,) + small.shape, F32), pltpu.SemaphoreType.DMA((n,)), pltpu.SemaphoreType.DMA((n,)),
                        pltpu.SemaphoreType.DMA((everyone.n_sems,)), pltpu.SemaphoreType.DMA((everyone.n_sems,))],
        compiler_params=_params(), name="sum_swap",
    )(*parts, small)
    return res[:n], res[n]


def _tile(rows, cols, itemsize, budget):
    t = cols if rows % 16 else rows
    other = rows if rows % 16 else cols
    step = 256 if rows % 16 else 32
    while t % step == 0 and t * other * itemsize > budget:
        t //= 2
    return (rows, t) if rows % 16 else (t, cols)


def _adamw_math(w, g, m, v):
    m = ADAM_B1 * m + (1.0 - ADAM_B1) * g
    v = ADAM_B2 * v + (1.0 - ADAM_B2) * (g * g)
    m_hat = m / (1.0 - ADAM_B1 ** ADAM_STEP)
    v_hat = v / (1.0 - ADAM_B2 ** ADAM_STEP)
    delta = -ADAM_LR * (m_hat / (jnp.sqrt(v_hat) + ADAM_EPS) + ADAM_WD * w)
    return delta, m, v


def _adamw_big(g, w, m, v, name):
    r, c = w.shape
    tr, tc = _tile(r, c, 4, 1024 * 1024)

    def body(g_ref, w_ref, m_ref, v_ref, d_ref, nm_ref, nv_ref):
        d_ref[...], nm_ref[...], nv_ref[...] = _adamw_math(w_ref[...], g_ref[...], m_ref[...], v_ref[...])

    blk = pl.BlockSpec((tr, tc), lambda i, l: (i, l))
    return pl.pallas_call(
        body, grid=(r // tr, c // tc), in_specs=[blk, blk, blk, blk],
        out_specs=[blk, blk, blk], out_shape=[jax.ShapeDtypeStruct((r, c), F32)] * 3,
        compiler_params=_params(("arbitrary", "arbitrary")), name=name,
    )(g, w, m, v)


def _adamw_rows(g, w, m, v, name):
    r, _, c = w.shape
    tc = 128

    def body(g_ref, w_ref, m_ref, v_ref, g3_ref, d_ref, nm_ref, nv_ref):
        g = g_ref[...]
        g3_ref[:, 0, :] = g
        d_ref[:, 0, :], nm_ref[:, 0, :], nv_ref[:, 0, :] = _adamw_math(w_ref[:, 0, :], g, m_ref[:, 0, :], v_ref[:, 0, :])

    rows = pl.BlockSpec((r, 1, tc), lambda l: (0, 0, l))
    return pl.pallas_call(
        body, grid=(c // tc,), in_specs=[pl.BlockSpec((r, tc), lambda l: (0, l)), rows, rows, rows],
        out_specs=[rows] * 4, out_shape=[jax.ShapeDtypeStruct((r, 1, c), F32)] * 4,
        compiler_params=_params(("arbitrary",)), name=name,
    )(g, w, m, v)


def _adamw_small(ws, gs, ms, vs):
    n = len(ws)

    def body(*refs):
        w_refs, g_refs, m_refs, v_refs, d_refs, nm_refs, nv_refs = _split(refs, *([n] * 7))
        for k in range(n):
            d_refs[k][...], nm_refs[k][...], nv_refs[k][...] = _adamw_math(w_refs[k][...], g_refs[k][...], m_refs[k][...],
                                                                             v_refs[k][...])

    shapes = [jax.ShapeDtypeStruct(w.shape, F32) for w in ws]
    res = pl.pallas_call(body, out_shape=shapes * 3, name="adamw_small")(*ws, *gs, *ms, *vs)
    return res[:n], res[n:2 * n], res[2 * n:]


def _pack(arrs):
    flat = jnp.concatenate([a.reshape(-1) for a in arrs])
    rows = -(-flat.shape[0] // 1024) * 8
    return jnp.pad(flat, (0, rows * 128 - flat.shape[0])).reshape(rows, 128)


def _unpack(buf, shapes):
    flat = buf.reshape(-1)
    out, off = [], 0
    for s in shapes:
        size = 1
        for d in s:
            size *= d
        out.append(flat[off:off + size].reshape(s))
        off += size
    return out


def _block_rows(w):
    return jnp.pad(w.reshape(512, 4), ((0, 0), (0, 124)))


def _cols(a4):
    return jnp.transpose(a4, (1, 0, 2)).reshape(a4.shape[1], -1)


_LATE = ("w_pa", "w_pb", "w_o", "w_up", "w_down")
_RIDE_IN_PROJ = ("w_pa", "w_pb", "w_o")
_RIDE_MIXER = ("w_up", "w_down_a")
_RIDE_MERGE = ("w_down_b",)


def _full_weights(gathered):
    joined = {"w_in": (D_IN, D_MODEL), "w_o": (D_MODEL, D_MODEL)}
    return {n: (a.reshape(joined[n]) if n in joined else a) for n, a in gathered.items()}


def _local_step(x, target, w, sp, late_shards=None):
    sp = {n: (a.reshape(1, -1) if a.ndim == 1 else a) for n, a in sp.items()}
    wau = jnp.zeros((128, 256), F32).at[0:16].set(sp["w_a_up"])
    wif = jnp.zeros((1536, 128), F32).at[:, 0:8].set(sp["w_if"])
    bif = jnp.zeros((1, 128), F32).at[:, 0:8].set(sp["b_if"])
    p = {"wau": wau, "bau": sp["b_a_up"], "ggla": sp["g_gla_norm"], "cw": sp["conv_w"], "cb": sp["conv_b"],
         "wq": _block_rows(sp["w_q_ml"]), "wk": _block_rows(sp["w_k_ml"]), "wv": _block_rows(sp["w_v_ml"]),
         "wif": wif, "bif": bif, "skip": sp["ml_skip"], "gml": sp["g_ml_norm"]}

    if late_shards is None:
        (pm, gab, h), _ = _in_proj(x, sp["g_pre_mix"], w["w_in"])
        ab, *states = _mixer_fwd(pm, p)
        (x1, mix, merged), _ = _merge_fwd(ab, gab, x, w["w_pa"], w["w_pb"], w["w_o"], sp["g_post_mix"])
    else:
        shard = dict(zip(_LATE, late_shards))
        shard["w_down_a"], shard["w_down_b"] = shard["w_down"][0:512], shard["w_down"][512:1024]
        (pm, gab, h), got = _in_proj(x, sp["g_pre_mix"], w["w_in"], _Gather(_RIDE_IN_PROJ),
                                     [shard[n] for n in _RIDE_IN_PROJ])
        w = dict(w, **_full_weights(dict(zip(_RIDE_IN_PROJ, got))))
        ab, *rest = _mixer_fwd(pm, p, _Gather(_RIDE_MIXER), [shard[n] for n in _RIDE_MIXER])
        states = rest[:4]
        w.update(_full_weights(dict(zip(_RIDE_MIXER, rest[4:]))))
        (x1, mix, merged), got = _merge_fwd(ab, gab, x, w["w_pa"], w["w_pb"], w["w_o"], sp["g_post_mix"],
                                            _Gather(_RIDE_MERGE), [shard[n] for n in _RIDE_MERGE])
        w.update(_full_weights(dict(zip(_RIDE_MERGE, got))))
    dx1, u, dd, h2, dpre, dg_post_mlp, dg_pre_mlp, loss = _mlp(x1, target, sp["g_pre_mlp"], sp["g_post_mlp"],
                                                                w["w_up"], w["w_down_a"], w["w_down_b"])
    dmix, dya, dyb, dgab, dab, dg_post_mix = _merge_bwd(dx1, mix, ab, gab, w["w_pa"], w["w_pb"], w["w_o"], sp["g_post_mix"])
    big = {
        "w_pa": _tn_matmul(ab[:, 0:512], dya, "dw_pa", shards=N_CHIP),
        "w_pb": _tn_matmul(ab[:, 512:1024], dyb, "dw_pb", shards=N_CHIP),
        "w_o": _tn_matmul(merged, dmix, "dw_o"),
        "w_up": _tn_matmul(h2, dpre, "dw_up", shards=N_CHIP),
        "w_down": _tn_matmul(u, dd, "dw_down"),
    }
    if late_shards is None:
        dpm, dp, _ = _mixer_bwd(pm, dab, states, p)
    else:
        partial = _presum(_LATE, [big[n].reshape((N_CHIP,) + _BIG_SHARD[n]) for n in _LATE], "presum_late")
        dpm, dp, parts = _mixer_bwd(pm, dab, states, p, _SendPartials(_LATE), partial)
        big = dict(zip(_LATE, parts))
    big["w_in"] = _dw_in(dpm, dgab, h)
    if late_shards is None:
        (dx, dg_pre_mix), _ = _in_proj_bwd(dpm, dgab, x, dx1, sp["g_pre_mix"], w["w_in"])
    else:
        partial = _presum(("w_in",), [big["w_in"].reshape((N_CHIP,) + _BIG_SHARD["w_in"])], "presum_w_in")
        (dx, dg_pre_mix), parts = _in_proj_bwd(dpm, dgab, x, dx1, sp["g_pre_mix"], w["w_in"], _SendPartials(("w_in",)),
                                               partial)
        big["w_in"] = parts[0]
    small = {
        "g_pre_mix": dg_pre_mix, "b_a_up": dp["bau"], "g_gla_norm": dp["ggla"], "conv_b": dp["cb"],
        "w_q_ml": dp["wq"][:, 0:4].reshape(128, 4, 4), "w_k_ml": dp["wk"][:, 0:4].reshape(128, 4, 4),
        "w_v_ml": dp["wv"][:, 0:4].reshape(128, 4, 4),
        "b_if": dp["bif"][:, 0:8], "ml_skip": dp["skip"], "g_ml_norm": dp["gml"], "g_post_mix": dg_post_mix,
        "g_pre_mlp": dg_pre_mlp, "g_post_mlp": dg_post_mlp, "w_a_up": dp["wau"][0:16], "conv_w": dp["cw"],
        "w_if": dp["wif"][:, 0:8], "loss": loss[:, 0:1],
    }
    return dx, big, small


_SMALL_REPL = ("g_pre_mix", "b_a_up", "g_gla_norm", "conv_b", "w_q_ml", "w_k_ml", "w_v_ml", "b_if", "ml_skip",
               "g_ml_norm", "g_post_mix", "g_pre_mlp", "g_post_mlp")
_SMALL_SHARDED = ("w_a_up", "conv_w", "w_if")
_SMALL_ORDER = _SMALL_REPL + _SMALL_SHARDED + ("loss",)
_WEIGHTS = ("g_pre_mix", "w_in", "w_a_up", "b_a_up", "g_gla_norm", "conv_w", "conv_b", "w_q_ml", "w_k_ml", "w_v_ml",
            "w_if", "b_if", "ml_skip", "g_ml_norm", "w_pa", "w_pb", "w_o", "g_post_mix", "g_pre_mlp", "w_up", "w_down",
            "g_post_mlp")


_BLOCK_WEIGHTS = ("w_q_ml", "w_k_ml", "w_v_ml")


def _stored(name, a):
    if name in _BLOCK_WEIGHTS:
        return jnp.transpose(a, (0, 2, 3, 1)).reshape(16, 128)
    if name == "w_if":
        return jnp.transpose(a, (0, 2, 1)).reshape(8, 384)
    return a


def _unstored(name, a):
    if name in _BLOCK_WEIGHTS:
        return jnp.transpose(a.reshape(1, 4, 4, 128), (0, 3, 1, 2))
    if name == "w_if":
        return jnp.transpose(a.reshape(1, 8, 384), (0, 2, 1))
    return a


def _as_shard(name, a):
    return jnp.transpose(a, (2, 0, 1)) if name == "w_in" else a[0]


def _from_shard(name, a):
    return jnp.transpose(a, (1, 2, 0)) if name == "w_in" else a[None]


def kernel(x, g_pre_mix, w_in, w_a_up, b_a_up, g_gla_norm, conv_w, conv_b, w_q_ml, w_k_ml, w_v_ml, w_if, b_if, ml_skip, g_ml_norm, w_pa, w_pb, w_o, g_post_mix, g_pre_mlp, w_up, w_down, g_post_mlp, loss_target, m_g_pre_mix, m_w_in, m_w_a_up, m_b_a_up, m_g_gla_norm, m_conv_w, m_conv_b, m_w_q_ml, m_w_k_ml, m_w_v_ml, m_w_if, m_b_if, m_ml_skip, m_g_ml_norm, m_w_pa, m_w_pb, m_w_o, m_g_post_mix, m_g_pre_mlp, m_w_up, m_w_down, m_g_post_mlp, v_g_pre_mix, v_w_in, v_w_a_up, v_b_a_up, v_g_gla_norm, v_conv_w, v_conv_b, v_w_q_ml, v_w_k_ml, v_w_v_ml, v_w_if, v_b_if, v_ml_skip, v_g_ml_norm, v_w_pa, v_w_pb, v_w_o, v_g_post_mix, v_g_pre_mlp, v_w_up, v_w_down, v_g_post_mlp):
    args = dict(locals())
    wts = {n: _as_shard(n, args[n]) for n in _WEIGHTS}
    mom = {n: _as_shard(n, args["m_" + n]) for n in _WEIGHTS}
    var = {n: _as_shard(n, args["v_" + n]) for n in _WEIGHTS}
    chip = 2 * lax.axis_index("x") + lax.axis_index("y")

    first = ("w_in",) + _SMALL_SHARDED
    gathered = dict(zip(first, _run_alone(_Gather(("w_in",), [wts[n] for n in _SMALL_SHARDED]),
                                          [wts[n][:, 0, :].astype(BF16) if n == "w_in" else wts[n] for n in first],
                                          "gather_first")))
    sp = {n: wts[n] for n in _SMALL_REPL}
    sp["w_a_up"] = _cols(gathered["w_a_up"])
    sp["conv_w"] = _cols(gathered["conv_w"])
    sp["w_if"] = gathered["w_if"].reshape(1536, 8)

    dx, big, small = _local_step(x[0], loss_target[0], _full_weights({"w_in": gathered["w_in"]}), sp,
                                 late_shards=[wts[n].astype(BF16) for n in _LATE])

    small_shapes = [small[n].shape for n in _SMALL_ORDER]
    packed = _pack([small[n] for n in _SMALL_ORDER])
    sums, small_sum = _sum_swap(_BIG, [big[n] for n in _BIG], packed)

    grads, delta, new_m, new_v = {}, {}, {}, {}
    for n, g in zip(_BIG, sums):
        if n == "w_in":
            g, d, nm, nv = _adamw_rows(g, wts[n], mom[n], var[n], "adamw_" + n)
        else:
            d, nm, nv = _adamw_big(g, wts[n], mom[n], var[n], "adamw_" + n)
        grads[n], delta[n], new_m[n], new_v[n] = (_from_shard(n, a) for a in (g, d, nm, nv))
    summed = dict(zip(_SMALL_ORDER, _unpack(small_sum, small_shapes)))
    loss = summed["loss"].reshape(())
    summed["w_a_up"] = lax.dynamic_slice_in_dim(summed["w_a_up"], chip * 64, 64, axis=1)
    summed["conv_w"] = lax.dynamic_slice_in_dim(summed["conv_w"], chip * 128, 128, axis=1)
    summed["w_if"] = lax.dynamic_slice_in_dim(summed["w_if"], chip * 384, 384, axis=0)
    small_names = _SMALL_REPL + _SMALL_SHARDED
    g_stored = [_stored(n, summed[n].reshape(args[n].shape)) for n in small_names]
    upd = _adamw_small([_stored(n, args[n]) for n in small_names], g_stored,
                       [_stored(n, args["m_" + n]) for n in small_names], [_stored(n, args["v_" + n]) for n in small_names])
    for dst, arrs in zip((grads, delta, new_m, new_v), (g_stored,) + tuple(upd)):
        dst.update({n: _unstored(n, a) for n, a in zip(small_names, arrs)})

    outs = [loss, dx[None]]
    for group in (grads, delta, new_m, new_v):
        outs += [group[n] for n in _WEIGHTS]
    return tuple(outs)
```

```python
import functools

import jax
import jax.numpy as jnp
from jax import lax
from jax.experimental import pallas as pl
from jax.experimental.pallas import tpu as pltpu

F32 = jnp.float32
BF16 = jnp.bfloat16

SEQ = 2048
D_MODEL = 1024
CHUNK = 64
N_CHUNK = SEQ // CHUNK
HEADS = 4
GLA_DK = 64
GLA_DV = 128
ML_DH = 128
D_FF = 4096
EPS = 1e-6
N_CHIP = 4
N_DEV = 8
TOK_TILE = 256
N_TOK_TILE = SEQ // TOK_TILE
SWEEP = 2
assert CHUNK == 64
N_SWEEP = N_CHUNK // SWEEP

PM_W = 2688
PM_XM = 1536
PM_OP = 2048
PM_AL = 2560
GAB_W = 2048
D_IN = 4624
IN_SHARD = D_IN // N_CHIP
IN_ALOW = 1536
IN_XM = 1552
IN_GATES = 2576

ADAM_LR = 0.001
ADAM_B1 = 0.9
ADAM_B2 = 0.999
ADAM_EPS = 1e-08
ADAM_WD = 0.01
ADAM_STEP = 10

VMEM_LIMIT = 56 * 1024 * 1024


def _params(sem=None):
    return pltpu.CompilerParams(dimension_semantics=sem, vmem_limit_bytes=VMEM_LIMIT)


def _dot(a, b, ca, cb):
    return lax.dot_general(a.astype(BF16), b.astype(BF16), (((ca,), (cb,)), ((), ())), preferred_element_type=F32)


def _pmm_nn(a, b):
    return _dot(a, b, 1, 0)


def _pmm_nt(a, b):
    return _dot(a, b, 1, 1)


def _pmm_tn(a, b):
    return _dot(a, b, 0, 0)


def _pcmm(c, x):
    return lax.dot_general(c, x, (((1,), (0,)), ((), ())), precision=lax.Precision.HIGHEST, preferred_element_type=F32)


@jax.custom_vjp
def _mm_nn(a, b):
    return _dot(a, b, 1, 0)


@jax.custom_vjp
def _mm_nt(a, b):
    return _dot(a, b, 1, 1)


@jax.custom_vjp
def _mm_tn(a, b):
    return _dot(a, b, 0, 0)


_mm_nn.defvjp(lambda a, b: (_dot(a, b, 1, 0), (a, b)), lambda r, g: (_mm_nt(g, r[1]), _mm_tn(r[0], g)))
_mm_nt.defvjp(lambda a, b: (_dot(a, b, 1, 1), (a, b)), lambda r, g: (_mm_nn(g, r[1]), _mm_tn(g, r[0])))
_mm_tn.defvjp(lambda a, b: (_dot(a, b, 0, 0), (a, b)), lambda r, g: (_mm_nt(r[1], g), _mm_nn(r[0], g)))


@jax.custom_vjp
def _cmm(c, x):
    return _pcmm(c, x)


_cmm.defvjp(
    lambda c, x: (_pcmm(c, x), c),
    lambda c, g: (jnp.zeros_like(c), lax.dot_general(c, g, (((0,), (0,)), ((), ())), precision=lax.Precision.HIGHEST,
                                                      preferred_element_type=F32)),
)

_PLAIN_OPS = (_pmm_nn, _pmm_nt, _pmm_tn, _pcmm)
_VJP_OPS = (_mm_nn, _mm_nt, _mm_tn, _cmm)


def _sigmoid(x):
    return 0.5 * (jnp.tanh(0.5 * x) + 1.0)


def _log_sigmoid(x):
    return jnp.minimum(x, 0.0) - jnp.log(1.0 + jnp.exp(-jnp.abs(x)))


def _mean(x):
    return jnp.mean(x, axis=-1, keepdims=True)


def _nt(a, b):
    return lax.dot_general(a, b, (((1,), (1,)), ((), ())), preferred_element_type=F32)


def _tn(a, b):
    return lax.dot_general(a, b, (((0,), (0,)), ((), ())), preferred_element_type=F32)


def _mixer_chunk(ops, p, st, pm, xprev8):
    mm_nn, mm_nt, mm_tn, cmm = ops
    n_rows = pm.shape[0]
    n_ch = n_rows // CHUNK
    row = lax.broadcasted_iota(jnp.int32, (n_rows, n_rows), 0)
    col = lax.broadcasted_iota(jnp.int32, (n_rows, n_rows), 1)
    tri = jnp.logical_and((row >> 6) == (col >> 6), row >= col).astype(F32)
    causal = tri[0:CHUNK, 0:CHUNK] > 0.0
    q = pm[:, 0:256]
    k = pm[:, 256:512]
    v = pm[:, 512:1024]
    g = pm[:, 1024:1536]
    xm = pm[:, PM_XM:PM_XM + 512]
    opre = pm[:, PM_OP:PM_OP + 512]
    alow = pm[:, PM_AL:PM_AL + 128]
    hs = range(HEADS)
    cs = range(n_ch)
    pairs = [(i, h) for i in cs for h in hs]
    rs = [slice(i * CHUNK, (i + 1) * CHUNK) for i in cs]
    last = [slice((i + 1) * CHUNK - 1, (i + 1) * CHUNK) for i in cs]
    s6 = [slice(h * GLA_DK, (h + 1) * GLA_DK) for h in hs]
    s12 = [slice(h * 128, (h + 1) * 128) for h in hs]

    la = _log_sigmoid(mm_nn(alow, p["wau"]) + p["bau"]) * (1.0 / 16.0)
    cum = cmm(tri, la)
    cum_last = [cum[last[i], :] for i in cs]
    to_end = jnp.concatenate([cum_last[i] - cum[rs[i], :] for i in cs], axis=0)
    e_pos = jnp.exp(cum)
    e_neg = jnp.exp(-cum)
    qs = q * (GLA_DK ** -0.5)
    qp = qs * e_pos
    qn = qs * e_neg
    kp = k * e_pos
    kn = k * e_neg
    kl = k * jnp.exp(to_end)
    dec = [jnp.exp(cum_last[i]) for i in cs]
    a_fwd = {(i, h): mm_nt(qp[rs[i], s6[h]], kn[rs[i], s6[h]]) for i, h in pairs}
    a_bwd = {(i, h): mm_nt(qn[rs[i], s6[h]], kp[rs[i], s6[h]]) for i, h in pairs}
    s_chunk = {(i, h): mm_tn(v[rs[i], s12[h]], kl[rs[i], s6[h]]) for i, h in pairs}
    mem = {(0, h): st["S"][h] for h in hs}
    for i, h in pairs:
        mem[(i + 1, h)] = mem[(i, h)] * dec[i][:, s6[h]] + s_chunk[(i, h)]
    s_new = [mem[(n_ch, h)] for h in hs]
    o_inter = {(i, h): mm_nt(qp[rs[i], s6[h]], mem[(i, h)]) for i, h in pairs}
    scores = {ih: jnp.where(causal, a_fwd[ih], a_bwd[ih]) for ih in pairs}
    o = {(i, h): mm_nn(scores[(i, h)], v[rs[i], s12[h]]) + o_inter[(i, h)] for i, h in pairs}
    o = {ih: o[ih] * lax.rsqrt(_mean(o[ih] * o[ih]) + EPS) * p["ggla"] for ih in pairs}
    gate = g * _sigmoid(g)
    out_a = {(i, h): o[(i, h)] * gate[rs[i], s12[h]] for i, h in pairs}

    xx = jnp.concatenate([xprev8, xm], axis=0)
    pre = p["cb"]
    for j in range(4):
        pre = pre + p["cw"][j:j + 1, :] * xx[5 + j:5 + j + n_rows, :]
    xc = pre * _sigmoid(pre)
    qm = [mm_nn(xc[:, s12[h]], p["wq"][h]) for h in hs]
    km = [mm_nn(xc[:, s12[h]], p["wk"][h]) for h in hs]
    vm = [mm_nn(xm[:, s12[h]], p["wv"][h]) for h in hs]
    qcat = jnp.concatenate(qm, axis=1)
    kcat = jnp.concatenate(km, axis=1)
    vcat = jnp.concatenate(vm, axis=1)
    gates = (mm_nn(qcat, p["wif"][0:512]) + mm_nn(kcat, p["wif"][512:1024]) + mm_nn(vcat, p["wif"][1024:1536])
             + p["bif"])
    lf = _log_sigmoid(gates)
    fc = cmm(tri, lf)
    gates_t = gates.T
    fc_t = fc.T
    ks = [km[h] * (ML_DH ** -0.5) for h in hs]
    qk = {(i, h): mm_nt(qm[h][rs[i]], ks[h][rs[i]]) for i, h in pairs}
    li_c = {(i, h): gates[rs[i], h:h + 1] for i, h in pairs}
    fc_c = {(i, h): fc[rs[i], 4 + h:5 + h] for i, h in pairs}
    f_last = {(i, h): fc[last[i], 4 + h:5 + h] for i, h in pairs}
    a = {ih: f_last[ih] - fc_c[ih] + li_c[ih] for ih in pairs}
    m_loc = {ih: jnp.max(a[ih], axis=0, keepdims=True) for ih in pairs}
    kw = {(i, h): ks[h][rs[i]] * jnp.exp(a[(i, h)] - m_loc[(i, h)]) for i, h in pairs}
    c_chunk = {(i, h): mm_tn(kw[(i, h)], vm[h][rs[i]]) for i, h in pairs}
    c_in = {(0, h): st["C"][h] for h in hs}
    n_in = {(0, h): st["n"][h] for h in hs}
    m_in = {(0, h): st["m"][h][:, 0:1] for h in hs}
    for i, h in pairs:
        m_nx = jnp.maximum(f_last[(i, h)] + m_in[(i, h)], m_loc[(i, h)])
        sp = jnp.exp(f_last[(i, h)] + m_in[(i, h)] - m_nx)
        sl = jnp.exp(m_loc[(i, h)] - m_nx)
        c_in[(i + 1, h)] = sp * c_in[(i, h)] + sl * c_chunk[(i, h)]
        n_in[(i + 1, h)] = sp * n_in[(i, h)] + sl * jnp.sum(kw[(i, h)], axis=0, keepdims=True)
        m_in[(i + 1, h)] = m_nx
    q_c = {(i, h): mm_nn(qm[h][rs[i]], c_in[(i, h)]) for i, h in pairs}
    log_d = {(i, h): gates_t[h:h + 1, rs[i]] - jnp.abs(fc_c[(i, h)] - fc_t[4 + h:5 + h, rs[i]]) for i, h in pairs}
    g_int = {ih: fc_c[ih] + m_in[ih] for ih in pairs}
    m_t = {ih: jnp.maximum(g_int[ih], jnp.max(log_d[ih], axis=1, keepdims=True)) for ih in pairs}
    s = {ih: qk[ih] * jnp.exp(log_d[ih] - m_t[ih]) for ih in pairs}
    scl = {ih: jnp.exp(g_int[ih] - m_t[ih]) for ih in pairs}
    num = {(i, h): mm_nn(s[(i, h)], vm[h][rs[i]]) + scl[(i, h)] * q_c[(i, h)] for i, h in pairs}
    den = {(i, h): jnp.sum(s[(i, h)], axis=1, keepdims=True)
           + scl[(i, h)] * jnp.sum(qm[h][rs[i]] * n_in[(i, h)], axis=1, keepdims=True) for i, h in pairs}
    den = {ih: jnp.maximum(jnp.abs(den[ih]), jnp.exp(-m_t[ih])) for ih in pairs}
    open_gate = _sigmoid(opre)
    hc = {(i, h): num[(i, h)] / den[(i, h)] * open_gate[rs[i], s12[h]] for i, h in pairs}
    d0 = {ih: hc[ih] - _mean(hc[ih]) for ih in pairs}
    y = {ih: d0[ih] * lax.rsqrt(_mean(d0[ih] * d0[ih]) + EPS) for ih in pairs}
    skipped = p["skip"] * xc
    out_b = {(i, h): y[(i, h)] * p["gml"][:, s12[h]] + skipped[rs[i], s12[h]] for i, h in pairs}
    ab = jnp.concatenate([jnp.concatenate([out_a[(i, h)] for h in hs] + [out_b[(i, h)] for h in hs], axis=1) for i in cs],
                         axis=0)
    new = {"S": s_new, "C": [c_in[(n_ch, h)] for h in hs], "n": [n_in[(n_ch, h)] for h in hs],
           "m": [jnp.broadcast_to(m_in[(n_ch, h)], (1, ML_DH)) for h in hs]}
    return ab, new


_P_NAMES = ("wau", "bau", "ggla", "cw", "cb", "wq", "wk", "wv", "wif", "bif", "skip", "gml")
_P_SHAPES = {
    "wau": (128, 256), "bau": (1, 256), "ggla": (1, 128), "cw": (4, 512), "cb": (1, 512),
    "wq": (512, 128), "wk": (512, 128), "wv": (512, 128),
    "wif": (1536, 128), "bif": (1, 128), "skip": (1, 512), "gml": (1, 512),
}
_P_BLOCKDIAG = ("wq", "wk", "wv")
_S_NAMES = ("S", "C", "n", "m")
_S_SHAPES = {"S": (HEADS, GLA_DV, GLA_DK), "C": (HEADS, ML_DH, ML_DH), "n": (HEADS, 1, ML_DH), "m": (HEADS, 1, ML_DH)}


def _per_head(ref):
    return [ref[h] for h in range(HEADS)]


def _block_mask():
    r = lax.broadcasted_iota(jnp.int32, (128, 128), 0)
    c = lax.broadcasted_iota(jnp.int32, (128, 128), 1)
    same_block = (r >> 2) == (c >> 2)
    spread = jnp.logical_and(r < 4, (c & 3) == r)
    return same_block.astype(F32), spread.astype(F32)


def _expand_blockdiag(w_ref, dense_ref):
    same_block, spread = _block_mask()
    for h in range(HEADS):
        tiled = _pmm_nn(w_ref[h * 128:(h + 1) * 128, :], spread)
        dense_ref[h] = tiled * same_block


def _collect_blockdiag(ddense_ref, dw_ref):
    same_block, spread = _block_mask()
    for h in range(HEADS):
        dw_ref[h * 128:(h + 1) * 128, :] = lax.dot_general(
            ddense_ref[h] * same_block, spread, (((1,), (1,)), ((), ())), precision=lax.Precision.HIGHEST,
            preferred_element_type=F32)


def _const_spec(shape):
    zeros = (0,) * len(shape)
    return pl.BlockSpec(shape, lambda i: zeros)


def _split(refs, *counts):
    out, at = [], 0
    for c in counts:
        out.append(refs[at:at + c])
        at += c
    assert at == len(refs)
    return out


def _ride(rider, phases, cond, ins, outs, sems):
    if rider is None:
        return
    lands, (send_sems, recv_sems, flush_sems) = sems[:-3], sems[-3:]

    @pl.when(cond)
    def _():
        for phase in phases:
            getattr(rider, phase)(ins, lands, send_sems, recv_sems)
        if "last" in phases:
            flush = [pltpu.make_async_copy(lands[k], outs[k], flush_sems.at[k]) for k in range(len(outs))]
            for cp in flush:
                cp.start()
            for cp in flush:
                cp.wait()


def _rider_specs(rider, rider_ins):
    if rider is None:
        return [], [], [], []
    scratch = [pltpu.VMEM(s.shape, s.dtype) for s in rider.out_shape]
    scratch += [pltpu.SemaphoreType.DMA((rider.n_sems,)), pltpu.SemaphoreType.DMA((rider.n_sems,)),
                pltpu.SemaphoreType.DMA((len(rider.out_shape),))]
    return [VMEM_WHOLE] * len(rider_ins), [ANY] * len(rider.out_shape), list(rider.out_shape), scratch


def _mixer_fwd(pm, p, rider=None, rider_ins=()):
    n_p = len(_P_NAMES)
    r_in, r_out_specs, r_out_shape, r_sems = _rider_specs(rider, rider_ins)

    def body(*refs):
        (pm_ref, xprev_ref), p_list, ride_in, (ab_ref,), so_refs, ride_out, sc_refs, dense_list, sems = _split(
            refs, 2, n_p, len(r_in), 1, 4, len(r_out_specs), 4, 3, len(r_sems))
        p_refs = dict(zip(_P_NAMES, p_list))
        dense = dict(zip(_P_BLOCKDIAG, dense_list))
        n = pl.program_id(0)
        _ride(rider, ("first",), n == 0, ride_in, ride_out, sems)

        @pl.when(n == 0)
        def _():
            for r in sc_refs:
                r[...] = jnp.zeros_like(r)
            for nm in _P_BLOCKDIAG:
                _expand_blockdiag(p_refs[nm], dense[nm])

        st = {name: _per_head(r) for name, r in zip(_S_NAMES, sc_refs)}
        pv = {nm: (_per_head(dense[nm]) if nm in _P_BLOCKDIAG else p_refs[nm][...]) for nm in _P_NAMES}
        for name, r in zip(_S_NAMES, so_refs):
            for h in range(HEADS):
                r[0, h] = st[name][h]
        xprev8 = jnp.where(n > 0, xprev_ref[CHUNK - 8:CHUNK, :], 0.0)
        ab, st = _mixer_chunk(_PLAIN_OPS, pv, st, pm_ref[...], xprev8)
        ab_ref[...] = ab.astype(BF16)
        for name, r in zip(_S_NAMES, sc_refs):
            for h in range(HEADS):
                r[h] = st[name][h]
        _ride(rider, ("middle",), n == N_SWEEP - 2, ride_in, ride_out, sems)
        _ride(rider, ("last",), n == N_SWEEP - 1, ride_in, ride_out, sems)

    in_specs = [pl.BlockSpec((SWEEP * CHUNK, PM_W), lambda i: (i, 0)),
                pl.BlockSpec((CHUNK, 512), lambda i: (jnp.maximum(SWEEP * i - 1, 0), PM_XM // 512))]
    in_specs += [_const_spec(_P_SHAPES[nm]) for nm in _P_NAMES] + r_in
    out_specs = [pl.BlockSpec((SWEEP * CHUNK, 1024), lambda i: (i, 0))]
    out_shape = [jax.ShapeDtypeStruct((SEQ, 1024), BF16)]
    for nm in _S_NAMES:
        shp = _S_SHAPES[nm]
        out_specs.append(pl.BlockSpec((1,) + shp, lambda i: (i, 0, 0, 0)))
        out_shape.append(jax.ShapeDtypeStruct((N_SWEEP,) + shp, F32))
    return pl.pallas_call(
        body, grid=(N_SWEEP,), in_specs=in_specs, out_specs=out_specs + r_out_specs, out_shape=out_shape + r_out_shape,
        scratch_shapes=[pltpu.VMEM(_S_SHAPES[nm], F32) for nm in _S_NAMES]
        + [pltpu.VMEM((HEADS, 128, 128), F32) for _ in _P_BLOCKDIAG] + r_sems,
        compiler_params=_params(("arbitrary",)), name="mixer_fwd",
    )(pm, pm, *[p[nm] for nm in _P_NAMES], *rider_ins)


def _mixer_bwd(pm, dab, states, p, rider=None, rider_ins=()):
    n_p = len(_P_NAMES)
    r_in, r_out_specs, r_out_shape, r_sems = _rider_specs(rider, rider_ins)

    def body(*refs):
        ((pm_ref, xprev_ref, dab_ref), si_refs, p_list, ride_in, (dpm_ref,), dp_list, ride_out, ds_refs, (carry_ref,),
         dense_list, ddense_list, sems) = _split(refs, 3, 4, n_p, len(r_in), 1, n_p, len(r_out_specs), 4, 1, 3, 3, len(r_sems))
        p_refs = dict(zip(_P_NAMES, p_list))
        dp_refs = dict(zip(_P_NAMES, dp_list))
        dense = dict(zip(_P_BLOCKDIAG, dense_list))
        ddense = dict(zip(_P_BLOCKDIAG, ddense_list))
        i = pl.program_id(0)
        blk = N_SWEEP - 1 - i
        _ride(rider, ("first",), i == 0, ride_in, ride_out, sems)

        @pl.when(i == 0)
        def _():
            for r in ds_refs:
                r[...] = jnp.zeros_like(r)
            for nm in _P_NAMES:
                if nm in _P_BLOCKDIAG:
                    ddense[nm][...] = jnp.zeros_like(ddense[nm])
                    _expand_blockdiag(p_refs[nm], dense[nm])
                else:
                    dp_refs[nm][...] = jnp.zeros_like(dp_refs[nm])
            carry_ref[...] = jnp.zeros_like(carry_ref)

        pv = {nm: (_per_head(dense[nm]) if nm in _P_BLOCKDIAG else p_refs[nm][...]) for nm in _P_NAMES}
        dst = {name: _per_head(r) for name, r in zip(_S_NAMES, ds_refs)}
        st = {name: [r[0, h] for h in range(HEADS)] for name, r in zip(_S_NAMES, si_refs)}
        xprev8 = jnp.where(blk > 0, xprev_ref[CHUNK - 8:CHUNK, :], 0.0)
        _, vjp = jax.vjp(functools.partial(_mixer_chunk, _VJP_OPS), pv, st, pm_ref[...], xprev8)
        dp_sum, dst, dpm, dxprev8 = vjp((dab_ref[...], dst))
        reach = jnp.concatenate([jnp.zeros((SWEEP * CHUNK - 8, 512), F32), carry_ref[...]], axis=0)
        dpm_ref[:, 0:PM_XM] = dpm[:, 0:PM_XM].astype(BF16)
        dpm_ref[:, PM_XM:PM_XM + 512] = (dpm[:, PM_XM:PM_XM + 512] + reach).astype(BF16)
        dpm_ref[:, PM_XM + 512:PM_W] = dpm[:, PM_XM + 512:PM_W].astype(BF16)
        carry_ref[...] = dxprev8
        for name, r in zip(_S_NAMES, ds_refs):
            for h in range(HEADS):
                r[h] = dst[name][h]
        for nm in _P_NAMES:
            if nm in _P_BLOCKDIAG:
                for h in range(HEADS):
                    ddense[nm][h] += dp_sum[nm][h]
            else:
                dp_refs[nm][...] += dp_sum[nm]

        @pl.when(i == N_SWEEP - 1)
        def _():
            for nm in _P_BLOCKDIAG:
                _collect_blockdiag(ddense[nm], dp_refs[nm])

        _ride(rider, ("middle",), i == N_SWEEP - 2, ride_in, ride_out, sems)
        _ride(rider, ("last",), i == N_SWEEP - 1, ride_in, ride_out, sems)

    rev = lambda i: (N_SWEEP - 1 - i, 0)
    in_specs = [pl.BlockSpec((SWEEP * CHUNK, PM_W), rev),
                pl.BlockSpec((CHUNK, 512), lambda i: (jnp.maximum(SWEEP * (N_SWEEP - 1 - i) - 1, 0), PM_XM // 512)),
                pl.BlockSpec((SWEEP * CHUNK, 1024), rev)]
    for nm in _S_NAMES:
        in_specs.append(pl.BlockSpec((1,) + _S_SHAPES[nm], lambda i: (N_SWEEP - 1 - i, 0, 0, 0)))
    in_specs += [_const_spec(_P_SHAPES[nm]) for nm in _P_NAMES] + r_in
    out_specs = [pl.BlockSpec((SWEEP * CHUNK, PM_W), rev)] + [_const_spec(_P_SHAPES[nm]) for nm in _P_NAMES]
    out_shape = [jax.ShapeDtypeStruct((SEQ, PM_W), BF16)] + [jax.ShapeDtypeStruct(_P_SHAPES[nm], F32) for nm in _P_NAMES]
    res = pl.pallas_call(
        body, grid=(N_SWEEP,), in_specs=in_specs, out_specs=out_specs + r_out_specs, out_shape=out_shape + r_out_shape,
        scratch_shapes=[pltpu.VMEM(_S_SHAPES[nm], F32) for nm in _S_NAMES] + [pltpu.VMEM((8, 512), F32)]
        + [pltpu.VMEM((HEADS, 128, 128), F32) for _ in range(2 * len(_P_BLOCKDIAG))] + r_sems,
        compiler_params=_params(("arbitrary",)), name="mixer_bwd",
    )(pm, pm, dab, *states, *[p[nm] for nm in _P_NAMES], *rider_ins)
    return res[0], dict(zip(_P_NAMES, res[1:1 + n_p])), res[1 + n_p:]


def _tok(width):
    return pl.BlockSpec((TOK_TILE, width), lambda i: (i, 0))


def _once(shape):
    zeros = (0,) * len(shape)
    return pl.BlockSpec(shape, lambda i: zeros, pipeline_mode=pl.Buffered(1))


def _rms_fwd(x):
    r = lax.rsqrt(_mean(x * x) + EPS)
    return x * r, r


def _rms_bwd(dy, xn, r, g):
    gd = dy * g
    return r * (gd - xn * _mean(xn * gd))


def _tiled_call(body, in_specs, out_specs, out_shape, args, name, rider=None, rider_ins=()):
    r_in, r_out_specs, r_out_shape, r_scratch = _rider_specs(rider, rider_ins)
    n_in, n_out = len(in_specs), len(out_specs)

    def hosted(*refs):
        ins, ride_in, outs, ride_out, scratch = _split(refs, n_in, len(r_in), n_out, len(r_out_specs), len(r_scratch))
        i = pl.program_id(0)
        _ride(rider, ("first",), i == 0, ride_in, ride_out, scratch)
        body(*ins, *outs)
        _ride(rider, ("middle",), i == N_TOK_TILE - 2, ride_in, ride_out, scratch)
        _ride(rider, ("last",), i == N_TOK_TILE - 1, ride_in, ride_out, scratch)

    res = pl.pallas_call(
        hosted, grid=(N_TOK_TILE,), in_specs=list(in_specs) + r_in, out_specs=list(out_specs) + r_out_specs,
        out_shape=list(out_shape) + r_out_shape, scratch_shapes=r_scratch,
        compiler_params=_params(("arbitrary",)), name=name,
    )(*args, *rider_ins)
    return res[:n_out], res[n_out:]


def _in_proj(x, g_pre, wt_in, rider=None, rider_ins=()):
    def body(x_ref, g_ref, wt_ref, pm_ref, gab_ref, h_ref):
        xn, _ = _rms_fwd(x_ref[...])
        h = (xn * g_ref[...]).astype(BF16)
        h_ref[...] = h
        pm_ref[:, 0:PM_XM] = _nt(h, wt_ref[0:IN_ALOW, :])
        pm_ref[:, PM_XM:PM_AL] = _nt(h, wt_ref[IN_XM:IN_GATES, :])
        pm_ref[:, PM_AL:PM_W] = _nt(h, wt_ref[IN_ALOW:IN_ALOW + 128, :])
        gab_ref[...] = _nt(h, wt_ref[IN_GATES:D_IN, :])

    return _tiled_call(
        body, [_tok(D_MODEL), _once((1, D_MODEL)), _once((D_IN, D_MODEL))], [_tok(PM_W), _tok(GAB_W), _tok(D_MODEL)],
        [jax.ShapeDtypeStruct((SEQ, PM_W), F32), jax.ShapeDtypeStruct((SEQ, GAB_W), F32),
         jax.ShapeDtypeStruct((SEQ, D_MODEL), BF16)], (x, g_pre, wt_in), "in_proj", rider, rider_ins)


def _merge_fwd(ab, gab, x, w_pa4, w_pb4, w_o, g_post, rider=None, rider_ins=()):
    def body(ab_ref, gab_ref, x_ref, wpa_ref, wpb_ref, wo_ref, g_ref, x1_ref, mix_ref, mg_ref):
        a = ab_ref[:, 0:512]
        b = ab_ref[:, 512:1024]
        for j in range(N_CHIP):
            blk = slice(j * 256, (j + 1) * 256)
            ya = jnp.dot(a, wpa_ref[j], preferred_element_type=F32)
            yb = jnp.dot(b, wpb_ref[j], preferred_element_type=F32)
            sa = _sigmoid(gab_ref[:, j * 256:(j + 1) * 256])
            sb = _sigmoid(gab_ref[:, 1024 + j * 256:1024 + (j + 1) * 256])
            mg_ref[:, blk] = (sa * ya + sb * yb).astype(BF16)
        mix = jnp.dot(mg_ref[...], wo_ref[...], preferred_element_type=F32)
        mix_ref[...] = mix
        mn, _ = _rms_fwd(mix)
        x1_ref[...] = x_ref[...] + mn * g_ref[...]

    return _tiled_call(
        body, [_tok(1024), _tok(GAB_W), _tok(D_MODEL), _once((N_CHIP, 512, 256)), _once((N_CHIP, 512, 256)),
               _once((D_MODEL, D_MODEL)), _once((1, D_MODEL))], [_tok(D_MODEL), _tok(D_MODEL), _tok(D_MODEL)],
        [jax.ShapeDtypeStruct((SEQ, D_MODEL), F32), jax.ShapeDtypeStruct((SEQ, D_MODEL), F32),
         jax.ShapeDtypeStruct((SEQ, D_MODEL), BF16)], (ab, gab, x, w_pa4, w_pb4, w_o, g_post), "merge_fwd", rider, rider_ins)


def _mlp(x1, target, g_pre, g_post, w_up4, w_down_a4, w_down_b4):
    def body(x1_ref, t_ref, gpre_ref, gpost_ref, wup_ref, wda_ref, wdb_ref,
             dx1_ref, u_ref, dd_ref, h2_ref, dpre_ref, dgpost_ref, dgpre_ref, loss_ref):
        @pl.when(pl.program_id(0) == 0)
        def _():
            dgpost_ref[...] = jnp.zeros_like(dgpost_ref)
            dgpre_ref[...] = jnp.zeros_like(dgpre_ref)
            loss_ref[...] = jnp.zeros_like(loss_ref)

        x1 = x1_ref[...]
        gpre = gpre_ref[...]
        gpost = gpost_ref[...]
        xn2, r2 = _rms_fwd(x1)
        h2 = (xn2 * gpre).astype(BF16)
        h2_ref[...] = h2
        rl = []
        d = jnp.zeros((TOK_TILE, D_MODEL), F32)
        for j in range(N_CHIP):
            blk = slice(j * 1024, (j + 1) * 1024)
            r = jnp.maximum(jnp.dot(h2, wup_ref[j], preferred_element_type=F32), 0.0)
            rl.append(r)
            u = (r * r).astype(BF16)
            u_ref[:, blk] = u
            d = d + jnp.dot(u[:, 0:512], wda_ref[j], preferred_element_type=F32)
            d = d + jnp.dot(u[:, 512:1024], wdb_ref[j], preferred_element_type=F32)
        dn, r3 = _rms_fwd(d)
        diff = x1 + dn * gpost - t_ref[...]
        loss_ref[...] += jnp.sum(diff * diff, keepdims=True) * (0.5 / D_MODEL)
        dy = diff * (1.0 / D_MODEL)
        dgpost_ref[...] += jnp.sum(dy * dn, axis=0, keepdims=True)
        dd = _rms_bwd(dy, dn, r3, gpost).astype(BF16)
        dd_ref[...] = dd
        dh2 = jnp.zeros((TOK_TILE, D_MODEL), F32)
        for j in range(N_CHIP):
            blk = slice(j * 1024, (j + 1) * 1024)
            du = jnp.concatenate([_nt(dd, wda_ref[j]), _nt(dd, wdb_ref[j])], axis=1)
            dpre = (du * (2.0 * rl[j])).astype(BF16)
            dpre_ref[:, blk] = dpre
            dh2 = dh2 + _nt(dpre, wup_ref[j])
        dgpre_ref[...] += jnp.sum(dh2 * xn2, axis=0, keepdims=True)
        dx1_ref[...] = dy + _rms_bwd(dh2, xn2, r2, gpre)

    acc = pl.BlockSpec((1, D_MODEL), lambda i: (0, 0))
    return pl.pallas_call(
        body, grid=(N_TOK_TILE,),
        in_specs=[_tok(D_MODEL), _tok(D_MODEL), _once((1, D_MODEL)), _once((1, D_MODEL)),
                  _once((N_CHIP, D_MODEL, 1024)), _once((N_CHIP, 512, D_MODEL)), _once((N_CHIP, 512, D_MODEL))],
        out_specs=[_tok(D_MODEL), _tok(D_FF), _tok(D_MODEL), _tok(D_MODEL), _tok(D_FF), acc, acc,
                   pl.BlockSpec((1, 128), lambda i: (0, 0))],
        out_shape=[jax.ShapeDtypeStruct((SEQ, D_MODEL), F32), jax.ShapeDtypeStruct((SEQ, D_FF), BF16),
                   jax.ShapeDtypeStruct((SEQ, D_MODEL), BF16), jax.ShapeDtypeStruct((SEQ, D_MODEL), BF16),
                   jax.ShapeDtypeStruct((SEQ, D_FF), BF16), jax.ShapeDtypeStruct((1, D_MODEL), F32),
                   jax.ShapeDtypeStruct((1, D_MODEL), F32), jax.ShapeDtypeStruct((1, 128), F32)],
        compiler_params=_params(("arbitrary",)), name="mlp_fwd_bwd",
    )(x1, target, g_pre, g_post, w_up4, w_down_a4, w_down_b4)


def _merge_bwd(dx1, mix, ab, gab, w_pa4, w_pb4, w_o, g_post):
    def body(dx1_ref, mix_ref, ab_ref, gab_ref, wpa_ref, wpb_ref, wo_ref, g_ref,
             dmix_ref, dya_ref, dyb_ref, dgab_ref, dab_ref, dg_ref):
        @pl.when(pl.program_id(0) == 0)
        def _():
            dg_ref[...] = jnp.zeros_like(dg_ref)

        dx1 = dx1_ref[...]
        mn, r = _rms_fwd(mix_ref[...])
        dg_ref[...] += jnp.sum(dx1 * mn, axis=0, keepdims=True)
        dmix = _rms_bwd(dx1, mn, r, g_ref[...]).astype(BF16)
        dmix_ref[...] = dmix
        dmerged = _nt(dmix, wo_ref[...])
        a = ab_ref[:, 0:512]
        b = ab_ref[:, 512:1024]
        da = jnp.zeros((TOK_TILE, 512), F32)
        db = jnp.zeros((TOK_TILE, 512), F32)
        for j in range(N_CHIP):
            blk = slice(j * 256, (j + 1) * 256)
            blk_b = slice(1024 + j * 256, 1024 + (j + 1) * 256)
            dm = dmerged[:, blk]
            ya = jnp.dot(a, wpa_ref[j], preferred_element_type=F32)
            yb = jnp.dot(b, wpb_ref[j], preferred_element_type=F32)
            sa = _sigmoid(gab_ref[:, blk])
            sb = _sigmoid(gab_ref[:, blk_b])
            dya = (dm * sa).astype(BF16)
            dyb = (dm * sb).astype(BF16)
            dya_ref[:, blk] = dya
            dyb_ref[:, blk] = dyb
            dgab_ref[:, blk] = (dm * ya * sa * (1.0 - sa)).astype(BF16)
            dgab_ref[:, blk_b] = (dm * yb * sb * (1.0 - sb)).astype(BF16)
            da = da + _nt(dya, wpa_ref[j])
            db = db + _nt(dyb, wpb_ref[j])
        dab_ref[:, 0:512] = da
        dab_ref[:, 512:1024] = db

    return pl.pallas_call(
        body, grid=(N_TOK_TILE,),
        in_specs=[_tok(D_MODEL), _tok(D_MODEL), _tok(1024), _tok(GAB_W), _once((N_CHIP, 512, 256)),
                  _once((N_CHIP, 512, 256)), _once((D_MODEL, D_MODEL)), _once((1, D_MODEL))],
        out_specs=[_tok(D_MODEL), _tok(D_MODEL), _tok(D_MODEL), _tok(GAB_W), _tok(1024),
                   pl.BlockSpec((1, D_MODEL), lambda i: (0, 0))],
        out_shape=[jax.ShapeDtypeStruct((SEQ, D_MODEL), BF16), jax.ShapeDtypeStruct((SEQ, D_MODEL), BF16),
                   jax.ShapeDtypeStruct((SEQ, D_MODEL), BF16), jax.ShapeDtypeStruct((SEQ, GAB_W), BF16),
                   jax.ShapeDtypeStruct((SEQ, 1024), F32), jax.ShapeDtypeStruct((1, D_MODEL), F32)],
        compiler_params=_params(("arbitrary",)), name="merge_bwd",
    )(dx1, mix, ab, gab, w_pa4, w_pb4, w_o, g_post)


def _in_proj_bwd(dpm, dgab, x, dx1, g_pre, wt_in, rider=None, rider_ins=()):
    def body(dpm_ref, dgab_ref, x_ref, dx1_ref, g_ref, wt_ref, dx_ref, dg_ref):
        @pl.when(pl.program_id(0) == 0)
        def _():
            dg_ref[...] = jnp.zeros_like(dg_ref)

        dh = jnp.dot(dpm_ref[:, 0:PM_XM], wt_ref[0:IN_ALOW, :], preferred_element_type=F32)
        dh = dh + jnp.dot(dpm_ref[:, PM_XM:PM_AL], wt_ref[IN_XM:IN_GATES, :], preferred_element_type=F32)
        dh = dh + jnp.dot(dpm_ref[:, PM_AL:PM_W], wt_ref[IN_ALOW:IN_ALOW + 128, :], preferred_element_type=F32)
        dh = dh + jnp.dot(dgab_ref[...], wt_ref[IN_GATES:D_IN, :], preferred_element_type=F32)
        xn, r = _rms_fwd(x_ref[...])
        dg_ref[...] += jnp.sum(dh * xn, axis=0, keepdims=True)
        dx_ref[...] = dx1_ref[...] + _rms_bwd(dh, xn, r, g_ref[...])

    return _tiled_call(
        body, [_tok(PM_W), _tok(GAB_W), _tok(D_MODEL), _tok(D_MODEL), _once((1, D_MODEL)), _once((D_IN, D_MODEL))],
        [_tok(D_MODEL), pl.BlockSpec((1, D_MODEL), lambda i: (0, 0))],
        [jax.ShapeDtypeStruct((SEQ, D_MODEL), F32), jax.ShapeDtypeStruct((1, D_MODEL), F32)],
        (dpm, dgab, x, dx1, g_pre, wt_in), "in_proj_bwd", rider, rider_ins)


def _dw_in(dpm, dgab, h):
    n_pm = PM_AL // 512
    n_blk = n_pm + GAB_W // 512

    def body(dpm_ref, dgab_ref, dal_ref, h_ref, o_ref):
        i = pl.program_id(0)
        off = pl.multiple_of(i * 512 + 16 * (i >= 3).astype(jnp.int32), 16)

        @pl.when(i < n_pm)
        def _():
            o_ref[pl.ds(off, 512), :] = _tn(dpm_ref[...], h_ref[...]).astype(BF16)

        @pl.when(i >= n_pm)
        def _():
            o_ref[pl.ds(off, 512), :] = _tn(dgab_ref[...], h_ref[...]).astype(BF16)

        @pl.when(i == 0)
        def _():
            o_ref[IN_ALOW:IN_XM, :] = _tn(dal_ref[...], h_ref[...])[0:IN_XM - IN_ALOW].astype(BF16)

    return pl.pallas_call(
        body, grid=(n_blk,),
        in_specs=[pl.BlockSpec((SEQ, 512), lambda i: (0, jnp.minimum(i, n_pm - 1))),
                  pl.BlockSpec((SEQ, 512), lambda i: (0, jnp.maximum(i - n_pm, 0))),
                  pl.BlockSpec((SEQ, 128), lambda i: (0, PM_AL // 128)),
                  _once((SEQ, D_MODEL))],
        out_specs=pl.BlockSpec((D_IN, D_MODEL), lambda i: (0, 0)),
        out_shape=jax.ShapeDtypeStruct((D_IN, D_MODEL), BF16),
        compiler_params=_params(("arbitrary",)), name="dw_in",
    )(dpm, dgab, dpm, h)


def _tn_matmul(a, b, name, shards=1, tm=512):
    m, n = a.shape[1], b.shape[1]
    tm = min(tm, m)
    tn = n // shards if shards > 1 else min(n, 1024)

    def body(a_ref, b_ref, o_ref):
        o_ref[...] = _tn(a_ref[...], b_ref[...]).astype(BF16)

    if shards > 1:
        out_spec = pl.BlockSpec((None, tm, tn), lambda i, j: (j, i, 0))
        out_shape = jax.ShapeDtypeStruct((shards, m, tn), BF16)
    else:
        out_spec = pl.BlockSpec((tm, tn), lambda i, j: (i, j))
        out_shape = jax.ShapeDtypeStruct((m, n), BF16)
    return pl.pallas_call(
        body, grid=(m // tm, n // tn),
        in_specs=[pl.BlockSpec((SEQ, tm), lambda i, j: (0, i)), pl.BlockSpec((SEQ, tn), lambda i, j: (0, j))],
        out_specs=out_spec, out_shape=out_shape,
        compiler_params=_params(("arbitrary", "arbitrary")), name=name,
    )(a, b)


MESH = pl.DeviceIdType.MESH
ANY = pl.BlockSpec(memory_space=pl.ANY)
VMEM_WHOLE = pl.BlockSpec(memory_space=pltpu.VMEM)

_BIG = ("w_in", "w_pa", "w_pb", "w_o", "w_up", "w_down")
_BIG_SHARD = {"w_in": (IN_SHARD, D_MODEL), "w_pa": (512, 256), "w_pb": (512, 256), "w_o": (256, D_MODEL),
              "w_up": (D_MODEL, 1024), "w_down": (1024, D_MODEL),
              "w_down_a": (512, D_MODEL), "w_down_b": (512, D_MODEL)}
_BIG_SPLIT = {"w_in": 1, "w_pa": 0, "w_pb": 0, "w_o": 0, "w_up": 0, "w_down": 0, "w_down_a": 0, "w_down_b": 0}


def _half(ref, e, name, lead=0):
    axis = _BIG_SPLIT[name]
    size = _BIG_SHARD[name][axis] // 2
    start = pl.multiple_of(e * size, 128 if axis == 1 else 16)
    idx = [pl.ds(0, ref.shape[a]) for a in range(lead)]
    idx += [pl.ds(start, size), pl.ds(0, _BIG_SHARD[name][1])] if axis == 0 else [pl.ds(0, _BIG_SHARD[name][0]), pl.ds(start, size)]
    return ref.at[tuple(idx)]


def _half_shape(name):
    r, c = _BIG_SHARD[name]
    return (r // 2, c) if _BIG_SPLIT[name] == 0 else (r, c // 2)


def _remote(src, dst, send_sems, recv_sems, k, to):
    return pltpu.make_async_remote_copy(src_ref=src, dst_ref=dst, send_sem=send_sems.at[k], recv_sem=recv_sems.at[k],
                                        device_id=to, device_id_type=MESH)


def _mesh_place():
    x, y, c = lax.axis_index("x"), lax.axis_index("y"), lax.axis_index("c")
    return x, y, c, [(1 - x, y), (x, 1 - y), (1 - x, 1 - y)]


class _Gather:
    def __init__(self, names, small=()):
        self.names = tuple(names)
        self.nb = len(self.names)
        self.n = self.nb + len(small)
        self.n_sems = 6 * self.n
        self.out_shape = [jax.ShapeDtypeStruct((N_CHIP,) + _BIG_SHARD[nm], BF16) for nm in self.names]
        self.out_shape += [jax.ShapeDtypeStruct((N_CHIP,) + s.shape, s.dtype) for s in small]

    def _ici(self, ins, outs, ss, rs, k, j, peer, slot, c):
        if k < self.nb:
            return _remote(_half(ins[k], c, self.names[k]), _half(outs[k].at[slot], c, self.names[k]), ss, rs, 6 * k + j,
                           (*peer, c))
        return _remote(ins[k], outs[k].at[slot], ss, rs, 6 * k + j, (*peer, c))

    def _passed(self, outs, ss, rs, k, j, slot, e, sibling):
        part = _half(outs[k].at[slot], e, self.names[k])
        return _remote(part, part, ss, rs, 6 * k + 3 + j, sibling)

    def first(self, ins, outs, ss, rs):
        x, y, c, peers = _mesh_place()
        me = 2 * x + y
        for k in range(self.n):
            for j, peer in enumerate(peers):
                self._ici(ins, outs, ss, rs, k, j, peer, me, c).start()
        for k in range(self.n):
            outs[k][me] = ins[k][...]

    def middle(self, ins, outs, ss, rs):
        x, y, c, peers = _mesh_place()
        for j, (px, py) in enumerate(peers):
            for k in range(self.nb):
                self._ici(ins, outs, ss, rs, k, j, (px, py), 2 * px + py, c).wait_recv()
                self._passed(outs, ss, rs, k, j, 2 * px + py, c, (x, y, 1 - c)).start()

    def last(self, ins, outs, ss, rs):
        x, y, c, peers = _mesh_place()
        for j, (px, py) in enumerate(peers):
            for k in range(self.n):
                if k < self.nb:
                    self._passed(outs, ss, rs, k, j, 2 * px + py, 1 - c, (x, y, 1 - c)).wait_recv()
                    self._passed(outs, ss, rs, k, j, 2 * px + py, c, (x, y, 1 - c)).wait_send()
                else:
                    self._ici(ins, outs, ss, rs, k, j, (px, py), 2 * px + py, c).wait_recv()
                self._ici(ins, outs, ss, rs, k, j, (px, py), 2 * x + y, c).wait_send()


def _run_alone(rider, ins, name):
    def body(*refs):
        r_in, r_out, sems = _split(refs, len(ins), len(rider.out_shape), 2)
        rider.first(r_in, r_out, *sems)
        rider.middle(r_in, r_out, *sems)
        rider.last(r_in, r_out, *sems)

    return pl.pallas_call(
        body, in_specs=[VMEM_WHOLE] * len(ins), out_specs=[VMEM_WHOLE] * len(rider.out_shape), out_shape=rider.out_shape,
        scratch_shapes=[pltpu.SemaphoreType.DMA((rider.n_sems,)), pltpu.SemaphoreType.DMA((rider.n_sems,))],
        compiler_params=_params(), name=name,
    )(*ins)


def _presum(names, grads, name):
    n = len(grads)

    def body(*refs):
        g_refs, got_refs, stage_refs, (send_sems, recv_sems, local_sems) = _split(refs, n, n, n, 3)
        x, y, c = lax.axis_index("x"), lax.axis_index("y"), lax.axis_index("c")

        def stage(e):
            cps = [pltpu.make_async_copy(_half(g_refs[k], e, names[k], lead=1), stage_refs[k], local_sems.at[k])
                   for k in range(n)]
            for cp in cps:
                cp.start()
            return cps

        staged = stage(1 - c)
        sends = []
        for k in range(n):
            staged[k].wait()
            cp = _remote(stage_refs[k], got_refs[k], send_sems, recv_sems, k, (x, y, 1 - c))
            cp.start()
            sends.append(cp)
        for cp in sends:
            cp.wait_send()
        staged = stage(c)
        for k in range(n):
            sends[k].wait_recv()
            staged[k].wait()

            @pl.loop(0, N_CHIP)
            def _(j):
                got_refs[k][j] = (got_refs[k][j].astype(F32) + stage_refs[k][j].astype(F32)).astype(BF16)

    half = [jax.ShapeDtypeStruct((N_CHIP,) + _half_shape(nm), BF16) for nm in names]
    return pl.pallas_call(
        body, in_specs=[ANY] * n, out_specs=[VMEM_WHOLE] * n, out_shape=half,
        scratch_shapes=[pltpu.VMEM(h.shape, h.dtype) for h in half]
        + [pltpu.SemaphoreType.DMA((n,)), pltpu.SemaphoreType.DMA((n,)), pltpu.SemaphoreType.DMA((n,))],
        compiler_params=_params(), name=name,
    )(*grads)


class _SendPartials:
    def __init__(self, names, small_shape=None):
        self.n = len(names)
        self.small = small_shape is not None
        self.n_sems = 3 * self.n + 7
        self.out_shape = [jax.ShapeDtypeStruct((N_CHIP,) + _half_shape(nm), BF16) for nm in names]
        if self.small:
            self.out_shape.append(jax.ShapeDtypeStruct((N_DEV,) + small_shape, F32))

    def _piece(self, ins, outs, ss, rs, k, j, peer, src_slot, dst_slot, c):
        return _remote(ins[k].at[src_slot], outs[k].at[dst_slot], ss, rs, 3 * k + j, (*peer, c))

    def _small(self, ins, outs, ss, rs, r, other, slot):
        return _remote(ins[self.n], outs[self.n].at[slot], ss, rs, 3 * self.n + r, other)

    @staticmethod
    def _others(x, y, c):
        return [(x, y, 1 - c), (1 - x, y, c), (1 - x, y, 1 - c), (x, 1 - y, c), (x, 1 - y, 1 - c),
                (1 - x, 1 - y, c), (1 - x, 1 - y, 1 - c)]

    def first(self, ins, outs, ss, rs):
        x, y, c, peers = _mesh_place()
        me = 2 * x + y
        for k in range(self.n):
            for j, (px, py) in enumerate(peers):
                self._piece(ins, outs, ss, rs, k, j, (px, py), 2 * px + py, me, c).start()
        if self.small:
            for r, other in enumerate(self._others(x, y, c)):
                self._small(ins, outs, ss, rs, r, other, 4 * x + 2 * y + c).start()
            outs[self.n][4 * x + 2 * y + c] = ins[self.n][...]
        for k in range(self.n):
            outs[k][me] = ins[k][me]

    def middle(self, ins, outs, ss, rs):
        pass

    def last(self, ins, outs, ss, rs):
        x, y, c, peers = _mesh_place()
        me = 2 * x + y
        for k in range(self.n):
            for j, (px, py) in enumerate(peers):
                self._piece(ins, outs, ss, rs, k, j, (px, py), me, 2 * px + py, c).wait_recv()
                self._piece(ins, outs, ss, rs, k, j, (px, py), 2 * px + py, me, c).wait_send()
        if self.small:
            for r, (px, py, pc) in enumerate(self._others(x, y, c)):
                self._small(ins, outs, ss, rs, r, (px, py, pc), 4 * px + 2 * py + pc).wait_recv()
                self._small(ins, outs, ss, rs, r, (px, py, pc), 4 * x + 2 * y + c).wait_send()


def _sum_swap(names, parts, small):
    n = len(parts)
    everyone = _SendPartials((), small.shape)

    def body(*refs):
        p_refs, (small_ref,), o_refs, (osmall_ref,), (all_ref,), (send_sems, recv_sems, ss_small, rs_small) = _split(
            refs, n, 1, n, 1, 1, 4)
        x, y, c = lax.axis_index("x"), lax.axis_index("y"), lax.axis_index("c")
        everyone.first([small_ref], [all_ref], ss_small, rs_small)

        def mine(k):
            part = _half(o_refs[k], c, names[k])
            return _remote(part, part, send_sems, recv_sems, k, (x, y, 1 - c))

        for k in range(n):
            for e in range(2):
                @pl.when(c == e)
                def _():
                    g = p_refs[k][0].astype(F32)
                    for s in range(1, N_CHIP):
                        g = g + p_refs[k][s].astype(F32)
                    r, cols = _half_shape(names[k])
                    if _BIG_SPLIT[names[k]] == 0:
                        o_refs[k][e * r:(e + 1) * r, :] = g
                    else:
                        o_refs[k][:, e * cols:(e + 1) * cols] = g
            mine(k).start()
        for k in range(n):
            theirs = _half(o_refs[k], 1 - c, names[k])
            _remote(theirs, theirs, send_sems, recv_sems, k, (x, y, 1 - c)).wait_recv()
            mine(k).wait_send()
        everyone.last([small_ref], [all_ref], ss_small, rs_small)
        g = all_ref[0]
        for d in range(1, N_DEV):
            g = g + all_ref[d]
        osmall_ref[...] = g

    res = pl.pallas_call(
        body, in_specs=[VMEM_WHOLE] * (n + 1), out_specs=[VMEM_WHOLE] * (n + 1),
        out_shape=[jax.ShapeDtypeStruct(_BIG_SHARD[nm], F32) for nm in names] + [jax.ShapeDtypeStruct(small.shape, F32)],
        scratch_shapes=[pltpu.VMEM((N_DEV,) + small.shape, F32), pltpu.SemaphoreType.DMA((n,)), pltpu.SemaphoreType.DMA((n,)),
                        pltpu.SemaphoreType.DMA((everyone.n_sems,)), pltpu.SemaphoreType.DMA((everyone.n_sems,))],
        compiler_params=_params(), name="sum_swap",
    )(*parts, small)
    return res[:n], res[n]


def _tile(rows, cols, itemsize, budget):
    t = cols if rows % 16 else rows
    other = rows if rows % 16 else cols
    step = 256 if rows % 16 else 32
    while t % step == 0 and t * other * itemsize > budget:
        t //= 2
    return (rows, t) if rows % 16 else (t, cols)


def _adamw_math(w, g, m, v):
    m = ADAM_B1 * m + (1.0 - ADAM_B1) * g
    v = ADAM_B2 * v + (1.0 - ADAM_B2) * (g * g)
    m_hat = m / (1.0 - ADAM_B1 ** ADAM_STEP)
    v_hat = v / (1.0 - ADAM_B2 ** ADAM_STEP)
    delta = -ADAM_LR * (m_hat / (jnp.sqrt(v_hat) + ADAM_EPS) + ADAM_WD * w)
    return delta, m, v


def _adamw_big(g, w, m, v, name):
    r, c = w.shape
    tr, tc = _tile(r, c, 4, 1024 * 1024)

    def body(g_ref, w_ref, m_ref, v_ref, d_ref, nm_ref, nv_ref):
        d_ref[...], nm_ref[...], nv_ref[...] = _adamw_math(w_ref[...], g_ref[...], m_ref[...], v_ref[...])

    blk = pl.BlockSpec((tr, tc), lambda i, l: (i, l))
    return pl.pallas_call(
        body, grid=(r // tr, c // tc), in_specs=[blk, blk, blk, blk],
        out_specs=[blk, blk, blk], out_shape=[jax.ShapeDtypeStruct((r, c), F32)] * 3,
        compiler_params=_params(("arbitrary", "arbitrary")), name=name,
    )(g, w, m, v)


def _adamw_rows(g, w, m, v, name):
    r, k, lanes = w.shape
    tr = 296

    def body(g_ref, w_ref, m_ref, v_ref, g3_ref, d_ref, nm_ref, nv_ref):
        g = g_ref[...].reshape(tr, k, lanes)
        g3_ref[...] = g
        d_ref[...], nm_ref[...], nv_ref[...] = _adamw_math(w_ref[...], g, m_ref[...], v_ref[...])

    rows = pl.BlockSpec((tr, k, lanes), lambda i: (i, 0, 0))
    return pl.pallas_call(
        body, grid=(pl.cdiv(r, tr),), in_specs=[pl.BlockSpec((tr, k * lanes), lambda i: (i, 0)), rows, rows, rows],
        out_specs=[rows] * 4, out_shape=[jax.ShapeDtypeStruct((r, k, lanes), F32)] * 4,
        compiler_params=_params(("arbitrary",)), name=name,
    )(g, w, m, v)


def _adamw_small(ws, gs, ms, vs):
    n = len(ws)

    def body(*refs):
        w_refs, g_refs, m_refs, v_refs, d_refs, nm_refs, nv_refs = _split(refs, *([n] * 7))
        for k in range(n):
            d_refs[k][...], nm_refs[k][...], nv_refs[k][...] = _adamw_math(w_refs[k][...], g_refs[k][...], m_refs[k][...],
                                                                             v_refs[k][...])

    shapes = [jax.ShapeDtypeStruct(w.shape, F32) for w in ws]
    res = pl.pallas_call(body, out_shape=shapes * 3, name="adamw_small")(*ws, *gs, *ms, *vs)
    return res[:n], res[n:2 * n], res[2 * n:]


def _pack(arrs):
    flat = jnp.concatenate([a.reshape(-1) for a in arrs])
    rows = -(-flat.shape[0] // 1024) * 8
    return jnp.pad(flat, (0, rows * 128 - flat.shape[0])).reshape(rows, 128)


def _unpack(buf, shapes):
    flat = buf.reshape(-1)
    out, off = [], 0
    for s in shapes:
        size = 1
        for d in s:
            size *= d
        out.append(flat[off:off + size].reshape(s))
        off += size
    return out


def _block_rows(w):
    return jnp.pad(w.reshape(512, 4), ((0, 0), (0, 124)))


def _cols(a4):
    return jnp.transpose(a4, (1, 0, 2)).reshape(a4.shape[1], -1)


_LATE = ("w_pa", "w_pb", "w_o", "w_up", "w_down")
_RIDE_IN_PROJ = ("w_pa", "w_pb", "w_o")
_RIDE_MIXER = ("w_up", "w_down_a")
_RIDE_MERGE = ("w_down_b",)


def _full_weights(gathered):
    joined = {"w_in": (D_IN, D_MODEL), "w_o": (D_MODEL, D_MODEL)}
    return {n: (a.reshape(joined[n]) if n in joined else a) for n, a in gathered.items()}


def _local_step(x, target, w, sp, late_shards=None):
    sp = {n: (a.reshape(1, -1) if a.ndim == 1 else a) for n, a in sp.items()}
    wau = jnp.zeros((128, 256), F32).at[0:16].set(sp["w_a_up"])
    wif = jnp.zeros((1536, 128), F32).at[:, 0:8].set(sp["w_if"])
    bif = jnp.zeros((1, 128), F32).at[:, 0:8].set(sp["b_if"])
    p = {"wau": wau, "bau": sp["b_a_up"], "ggla": sp["g_gla_norm"], "cw": sp["conv_w"], "cb": sp["conv_b"],
         "wq": _block_rows(sp["w_q_ml"]), "wk": _block_rows(sp["w_k_ml"]), "wv": _block_rows(sp["w_v_ml"]),
         "wif": wif, "bif": bif, "skip": sp["ml_skip"], "gml": sp["g_ml_norm"]}

    if late_shards is None:
        (pm, gab, h), _ = _in_proj(x, sp["g_pre_mix"], w["w_in"])
        ab, *states = _mixer_fwd(pm, p)
        (x1, mix, merged), _ = _merge_fwd(ab, gab, x, w["w_pa"], w["w_pb"], w["w_o"], sp["g_post_mix"])
    else:
        shard = dict(zip(_LATE, late_shards))
        shard["w_down_a"], shard["w_down_b"] = shard["w_down"][0:512], shard["w_down"][512:1024]
        (pm, gab, h), got = _in_proj(x, sp["g_pre_mix"], w["w_in"], _Gather(_RIDE_IN_PROJ),
                                     [shard[n] for n in _RIDE_IN_PROJ])
        w = dict(w, **_full_weights(dict(zip(_RIDE_IN_PROJ, got))))
        ab, *rest = _mixer_fwd(pm, p, _Gather(_RIDE_MIXER), [shard[n] for n in _RIDE_MIXER])
        states = rest[:4]
        w.update(_full_weights(dict(zip(_RIDE_MIXER, rest[4:]))))
        (x1, mix, merged), got = _merge_fwd(ab, gab, x, w["w_pa"], w["w_pb"], w["w_o"], sp["g_post_mix"],
                                            _Gather(_RIDE_MERGE), [shard[n] for n in _RIDE_MERGE])
        w.update(_full_weights(dict(zip(_RIDE_MERGE, got))))
    dx1, u, dd, h2, dpre, dg_post_mlp, dg_pre_mlp, loss = _mlp(x1, target, sp["g_pre_mlp"], sp["g_post_mlp"],
                                                                w["w_up"], w["w_down_a"], w["w_down_b"])
    dmix, dya, dyb, dgab, dab, dg_post_mix = _merge_bwd(dx1, mix, ab, gab, w["w_pa"], w["w_pb"], w["w_o"], sp["g_post_mix"])
    big = {
        "w_pa": _tn_matmul(ab[:, 0:512], dya, "dw_pa", shards=N_CHIP),
        "w_pb": _tn_matmul(ab[:, 512:1024], dyb, "dw_pb", shards=N_CHIP),
        "w_o": _tn_matmul(merged, dmix, "dw_o"),
        "w_up": _tn_matmul(h2, dpre, "dw_up", shards=N_CHIP),
        "w_down": _tn_matmul(u, dd, "dw_down"),
    }
    if late_shards is None:
        dpm, dp, _ = _mixer_bwd(pm, dab, states, p)
    else:
        partial = _presum(_LATE, [big[n].reshape((N_CHIP,) + _BIG_SHARD[n]) for n in _LATE], "presum_late")
        dpm, dp, parts = _mixer_bwd(pm, dab, states, p, _SendPartials(_LATE), partial)
        big = dict(zip(_LATE, parts))
    big["w_in"] = _dw_in(dpm, dgab, h)
    if late_shards is None:
        (dx, dg_pre_mix), _ = _in_proj_bwd(dpm, dgab, x, dx1, sp["g_pre_mix"], w["w_in"])
    else:
        partial = _presum(("w_in",), [big["w_in"].reshape((N_CHIP,) + _BIG_SHARD["w_in"])], "presum_w_in")
        (dx, dg_pre_mix), parts = _in_proj_bwd(dpm, dgab, x, dx1, sp["g_pre_mix"], w["w_in"], _SendPartials(("w_in",)),
                                               partial)
        big["w_in"] = parts[0]
    small = {
        "g_pre_mix": dg_pre_mix, "b_a_up": dp["bau"], "g_gla_norm": dp["ggla"], "conv_b": dp["cb"],
        "w_q_ml": dp["wq"][:, 0:4].reshape(128, 4, 4), "w_k_ml": dp["wk"][:, 0:4].reshape(128, 4, 4),
        "w_v_ml": dp["wv"][:, 0:4].reshape(128, 4, 4),
        "b_if": dp["bif"][:, 0:8], "ml_skip": dp["skip"], "g_ml_norm": dp["gml"], "g_post_mix": dg_post_mix,
        "g_pre_mlp": dg_pre_mlp, "g_post_mlp": dg_post_mlp, "w_a_up": dp["wau"][0:16], "conv_w": dp["cw"],
        "w_if": dp["wif"][:, 0:8], "loss": loss[:, 0:1],
    }
    return dx, big, small


_SMALL_REPL = ("g_pre_mix", "b_a_up", "g_gla_norm", "conv_b", "w_q_ml", "w_k_ml", "w_v_ml", "b_if", "ml_skip",
               "g_ml_norm", "g_post_mix", "g_pre_mlp", "g_post_mlp")
_SMALL_SHARDED = ("w_a_up", "conv_w", "w_if")
_SMALL_ORDER = _SMALL_REPL + _SMALL_SHARDED + ("loss",)
_WEIGHTS = ("g_pre_mix", "w_in", "w_a_up", "b_a_up", "g_gla_norm", "conv_w", "conv_b", "w_q_ml", "w_k_ml", "w_v_ml",
            "w_if", "b_if", "ml_skip", "g_ml_norm", "w_pa", "w_pb", "w_o", "g_post_mix", "g_pre_mlp", "w_up", "w_down",
            "g_post_mlp")


_BLOCK_WEIGHTS = ("w_q_ml", "w_k_ml", "w_v_ml")


def _stored(name, a):
    if name in _BLOCK_WEIGHTS:
        return jnp.transpose(a, (0, 2, 3, 1)).reshape(16, 128)
    if name == "w_if":
        return jnp.transpose(a, (0, 2, 1)).reshape(8, 384)
    return a


def _unstored(name, a):
    if name in _BLOCK_WEIGHTS:
        return jnp.transpose(a.reshape(1, 4, 4, 128), (0, 3, 1, 2))
    if name == "w_if":
        return jnp.transpose(a.reshape(1, 8, 384), (0, 2, 1))
    return a


def _as_shard(name, a):
    return jnp.transpose(a, (2, 0, 1)).reshape(IN_SHARD, D_MODEL // 128, 128) if name == "w_in" else a[0]


def _from_shard(name, a):
    return jnp.transpose(a, (1, 2, 0)).reshape(1, D_MODEL, IN_SHARD) if name == "w_in" else a[None]


def kernel(x, g_pre_mix, w_in, w_a_up, b_a_up, g_gla_norm, conv_w, conv_b, w_q_ml, w_k_ml, w_v_ml, w_if, b_if, ml_skip, g_ml_norm, w_pa, w_pb, w_o, g_post_mix, g_pre_mlp, w_up, w_down, g_post_mlp, loss_target, m_g_pre_mix, m_w_in, m_w_a_up, m_b_a_up, m_g_gla_norm, m_conv_w, m_conv_b, m_w_q_ml, m_w_k_ml, m_w_v_ml, m_w_if, m_b_if, m_ml_skip, m_g_ml_norm, m_w_pa, m_w_pb, m_w_o, m_g_post_mix, m_g_pre_mlp, m_w_up, m_w_down, m_g_post_mlp, v_g_pre_mix, v_w_in, v_w_a_up, v_b_a_up, v_g_gla_norm, v_conv_w, v_conv_b, v_w_q_ml, v_w_k_ml, v_w_v_ml, v_w_if, v_b_if, v_ml_skip, v_g_ml_norm, v_w_pa, v_w_pb, v_w_o, v_g_post_mix, v_g_pre_mlp, v_w_up, v_w_down, v_g_post_mlp):
    args = dict(locals())
    wts = {n: _as_shard(n, args[n]) for n in _WEIGHTS}
    mom = {n: _as_shard(n, args["m_" + n]) for n in _WEIGHTS}
    var = {n: _as_shard(n, args["v_" + n]) for n in _WEIGHTS}
    chip = 2 * lax.axis_index("x") + lax.axis_index("y")

    first = ("w_in",) + _SMALL_SHARDED
    gathered = dict(zip(first, _run_alone(_Gather(("w_in",), [wts[n] for n in _SMALL_SHARDED]),
                                          [wts[n].reshape(IN_SHARD, D_MODEL).astype(BF16) if n == "w_in" else wts[n]
                                           for n in first],
                                          "gather_first")))
    sp = {n: wts[n] for n in _SMALL_REPL}
    sp["w_a_up"] = _cols(gathered["w_a_up"])
    sp["conv_w"] = _cols(gathered["conv_w"])
    sp["w_if"] = gathered["w_if"].reshape(1536, 8)

    dx, big, small = _local_step(x[0], loss_target[0], _full_weights({"w_in": gathered["w_in"]}), sp,
                                 late_shards=[wts[n].astype(BF16) for n in _LATE])

    small_shapes = [small[n].shape for n in _SMALL_ORDER]
    packed = _pack([small[n] for n in _SMALL_ORDER])
    sums, small_sum = _sum_swap(_BIG, [big[n] for n in _BIG], packed)

    grads, delta, new_m, new_v = {}, {}, {}, {}
    for n, g in zip(_BIG, sums):
        if n == "w_in":
            g, d, nm, nv = _adamw_rows(g, wts[n], mom[n], var[n], "adamw_" + n)
        else:
            d, nm, nv = _adamw_big(g, wts[n], mom[n], var[n], "adamw_" + n)
        grads[n], delta[n], new_m[n], new_v[n] = (_from_shard(n, a) for a in (g, d, nm, nv))
    summed = dict(zip(_SMALL_ORDER, _unpack(small_sum, small_shapes)))
    loss = summed["loss"].reshape(())
    summed["w_a_up"] = lax.dynamic_slice_in_dim(summed["w_a_up"], chip * 64, 64, axis=1)
    summed["conv_w"] = lax.dynamic_slice_in_dim(summed["conv_w"], chip * 128, 128, axis=1)
    summed["w_if"] = lax.dynamic_slice_in_dim(summed["w_if"], chip * 384, 384, axis=0)
    small_names = _SMALL_REPL + _SMALL_SHARDED
    g_stored = [_stored(n, summed[n].reshape(args[n].shape)) for n in small_names]
    upd = _adamw_small([_stored(n, args[n]) for n in small_names], g_stored,
                       [_stored(n, args["m_" + n]) for n in small_names], [_stored(n, args["v_" + n]) for n in small_names])
    for dst, arrs in zip((grads, delta, new_m, new_v), (g_stored,) + tuple(upd)):
        dst.update({n: _unstored(n, a) for n, a in zip(small_names, arrs)})

    outs = [loss, dx[None]]
    for group in (grads, delta, new_m, new_v):
        outs += [group[n] for n in _WEIGHTS]
    return tuple(outs)
```

```python
import functools

import jax
import jax.numpy as jnp
from jax import lax
from jax.experimental import pallas as pl
from jax.experimental.pallas import tpu as pltpu

F32 = jnp.float32
BF16 = jnp.bfloat16

SEQ = 2048
D_MODEL = 1024
CHUNK = 64
N_CHUNK = SEQ // CHUNK
HEADS = 4
GLA_DK = 64
GLA_DV = 128
ML_DH = 128
D_FF = 4096
EPS = 1e-6
N_CHIP = 4
N_DEV = 8
TOK_TILE = 256
N_TOK_TILE = SEQ // TOK_TILE
SWEEP = 2
assert CHUNK == 64
N_SWEEP = N_CHUNK // SWEEP

PM_W = 2688
PM_XM = 1536
PM_OP = 2048
PM_AL = 2560
GAB_W = 2048
D_IN = 4624
IN_SHARD = D_IN // N_CHIP
IN_ALOW = 1536
IN_XM = 1552
IN_GATES = 2576

ADAM_LR = 0.001
ADAM_B1 = 0.9
ADAM_B2 = 0.999
ADAM_EPS = 1e-08
ADAM_WD = 0.01
ADAM_STEP = 10

VMEM_LIMIT = 56 * 1024 * 1024


def _params(sem=None):
    return pltpu.CompilerParams(dimension_semantics=sem, vmem_limit_bytes=VMEM_LIMIT)


def _dot(a, b, ca, cb):
    return lax.dot_general(a.astype(BF16), b.astype(BF16), (((ca,), (cb,)), ((), ())), preferred_element_type=F32)


def _pmm_nn(a, b):
    return _dot(a, b, 1, 0)


def _pmm_nt(a, b):
    return _dot(a, b, 1, 1)


def _pmm_tn(a, b):
    return _dot(a, b, 0, 0)


def _pcmm(c, x):
    return lax.dot_general(c, x, (((1,), (0,)), ((), ())), precision=lax.Precision.HIGHEST, preferred_element_type=F32)


@jax.custom_vjp
def _mm_nn(a, b):
    return _dot(a, b, 1, 0)


@jax.custom_vjp
def _mm_nt(a, b):
    return _dot(a, b, 1, 1)


@jax.custom_vjp
def _mm_tn(a, b):
    return _dot(a, b, 0, 0)


_mm_nn.defvjp(lambda a, b: (_dot(a, b, 1, 0), (a, b)), lambda r, g: (_mm_nt(g, r[1]), _mm_tn(r[0], g)))
_mm_nt.defvjp(lambda a, b: (_dot(a, b, 1, 1), (a, b)), lambda r, g: (_mm_nn(g, r[1]), _mm_tn(g, r[0])))
_mm_tn.defvjp(lambda a, b: (_dot(a, b, 0, 0), (a, b)), lambda r, g: (_mm_nt(r[1], g), _mm_nn(r[0], g)))


@jax.custom_vjp
def _cmm(c, x):
    return _pcmm(c, x)


_cmm.defvjp(
    lambda c, x: (_pcmm(c, x), c),
    lambda c, g: (jnp.zeros_like(c), lax.dot_general(c, g, (((0,), (0,)), ((), ())), precision=lax.Precision.HIGHEST,
                                                      preferred_element_type=F32)),
)

_PLAIN_OPS = (_pmm_nn, _pmm_nt, _pmm_tn, _pcmm)
_VJP_OPS = (_mm_nn, _mm_nt, _mm_tn, _cmm)


def _sigmoid(x):
    return 0.5 * (jnp.tanh(0.5 * x) + 1.0)


def _log_sigmoid(x):
    return jnp.minimum(x, 0.0) - jnp.log(1.0 + jnp.exp(-jnp.abs(x)))


def _mean(x):
    return jnp.mean(x, axis=-1, keepdims=True)


def _nt(a, b):
    return lax.dot_general(a, b, (((1,), (1,)), ((), ())), preferred_element_type=F32)


def _tn(a, b):
    return lax.dot_general(a, b, (((0,), (0,)), ((), ())), preferred_element_type=F32)


def _mixer_chunk(ops, p, st, pm, xprev8):
    mm_nn, mm_nt, mm_tn, cmm = ops
    n_rows = pm.shape[0]
    n_ch = n_rows // CHUNK
    row = lax.broadcasted_iota(jnp.int32, (n_rows, n_rows), 0)
    col = lax.broadcasted_iota(jnp.int32, (n_rows, n_rows), 1)
    tri = jnp.logical_and((row >> 6) == (col >> 6), row >= col).astype(F32)
    causal = tri[0:CHUNK, 0:CHUNK] > 0.0
    q = pm[:, 0:256]
    k = pm[:, 256:512]
    v = pm[:, 512:1024]
    g = pm[:, 1024:1536]
    xm = pm[:, PM_XM:PM_XM + 512]
    opre = pm[:, PM_OP:PM_OP + 512]
    alow = pm[:, PM_AL:PM_AL + 128]
    hs = range(HEADS)
    cs = range(n_ch)
    pairs = [(i, h) for i in cs for h in hs]
    rs = [slice(i * CHUNK, (i + 1) * CHUNK) for i in cs]
    last = [slice((i + 1) * CHUNK - 1, (i + 1) * CHUNK) for i in cs]
    s6 = [slice(h * GLA_DK, (h + 1) * GLA_DK) for h in hs]
    s12 = [slice(h * 128, (h + 1) * 128) for h in hs]

    la = _log_sigmoid(mm_nn(alow, p["wau"]) + p["bau"]) * (1.0 / 16.0)
    cum = cmm(tri, la)
    cum_last = [cum[last[i], :] for i in cs]
    to_end = jnp.concatenate([cum_last[i] - cum[rs[i], :] for i in cs], axis=0)
    e_pos = jnp.exp(cum)
    e_neg = jnp.exp(-cum)
    qs = q * (GLA_DK ** -0.5)
    qp = qs * e_pos
    qn = qs * e_neg
    kp = k * e_pos
    kn = k * e_neg
    kl = k * jnp.exp(to_end)
    dec = [jnp.exp(cum_last[i]) for i in cs]
    a_fwd = {(i, h): mm_nt(qp[rs[i], s6[h]], kn[rs[i], s6[h]]) for i, h in pairs}
    a_bwd = {(i, h): mm_nt(qn[rs[i], s6[h]], kp[rs[i], s6[h]]) for i, h in pairs}
    s_chunk = {(i, h): mm_tn(v[rs[i], s12[h]], kl[rs[i], s6[h]]) for i, h in pairs}
    mem = {(0, h): st["S"][h] for h in hs}
    for i, h in pairs:
        mem[(i + 1, h)] = mem[(i, h)] * dec[i][:, s6[h]] + s_chunk[(i, h)]
    s_new = [mem[(n_ch, h)] for h in hs]
    o_inter = {(i, h): mm_nt(qp[rs[i], s6[h]], mem[(i, h)]) for i, h in pairs}
    scores = {ih: jnp.where(causal, a_fwd[ih], a_bwd[ih]) for ih in pairs}
    o = {(i, h): mm_nn(scores[(i, h)], v[rs[i], s12[h]]) + o_inter[(i, h)] for i, h in pairs}
    o = {ih: o[ih] * lax.rsqrt(_mean(o[ih] * o[ih]) + EPS) * p["ggla"] for ih in pairs}
    gate = g * _sigmoid(g)
    out_a = {(i, h): o[(i, h)] * gate[rs[i], s12[h]] for i, h in pairs}

    xx = jnp.concatenate([xprev8, xm], axis=0)
    pre = p["cb"]
    for j in range(4):
        pre = pre + p["cw"][j:j + 1, :] * xx[5 + j:5 + j + n_rows, :]
    xc = pre * _sigmoid(pre)
    qm = [mm_nn(xc[:, s12[h]], p["wq"][h]) for h in hs]
    km = [mm_nn(xc[:, s12[h]], p["wk"][h]) for h in hs]
    vm = [mm_nn(xm[:, s12[h]], p["wv"][h]) for h in hs]
    qcat = jnp.concatenate(qm, axis=1)
    kcat = jnp.concatenate(km, axis=1)
    vcat = jnp.concatenate(vm, axis=1)
    gates = (mm_nn(qcat, p["wif"][0:512]) + mm_nn(kcat, p["wif"][512:1024]) + mm_nn(vcat, p["wif"][1024:1536])
             + p["bif"])
    lf = _log_sigmoid(gates)
    fc = cmm(tri, lf)
    gates_t = gates.T
    fc_t = fc.T
    ks = [km[h] * (ML_DH ** -0.5) for h in hs]
    qk = {(i, h): mm_nt(qm[h][rs[i]], ks[h][rs[i]]) for i, h in pairs}
    li_c = {(i, h): gates[rs[i], h:h + 1] for i, h in pairs}
    fc_c = {(i, h): fc[rs[i], 4 + h:5 + h] for i, h in pairs}
    f_last = {(i, h): fc[last[i], 4 + h:5 + h] for i, h in pairs}
    a = {ih: f_last[ih] - fc_c[ih] + li_c[ih] for ih in pairs}
    m_loc = {ih: jnp.max(a[ih], axis=0, keepdims=True) for ih in pairs}
    kw = {(i, h): ks[h][rs[i]] * jnp.exp(a[(i, h)] - m_loc[(i, h)]) for i, h in pairs}
    c_chunk = {(i, h): mm_tn(kw[(i, h)], vm[h][rs[i]]) for i, h in pairs}
    c_in = {(0, h): st["C"][h] for h in hs}
    n_in = {(0, h): st["n"][h] for h in hs}
    m_in = {(0, h): st["m"][h][:, 0:1] for h in hs}
    for i, h in pairs:
        m_nx = jnp.maximum(f_last[(i, h)] + m_in[(i, h)], m_loc[(i, h)])
        sp = jnp.exp(f_last[(i, h)] + m_in[(i, h)] - m_nx)
        sl = jnp.exp(m_loc[(i, h)] - m_nx)
        c_in[(i + 1, h)] = sp * c_in[(i, h)] + sl * c_chunk[(i, h)]
        n_in[(i + 1, h)] = sp * n_in[(i, h)] + sl * jnp.sum(kw[(i, h)], axis=0, keepdims=True)
        m_in[(i + 1, h)] = m_nx
    q_c = {(i, h): mm_nn(qm[h][rs[i]], c_in[(i, h)]) for i, h in pairs}
    log_d = {(i, h): gates_t[h:h + 1, rs[i]] - jnp.abs(fc_c[(i, h)] - fc_t[4 + h:5 + h, rs[i]]) for i, h in pairs}
    g_int = {ih: fc_c[ih] + m_in[ih] for ih in pairs}
    m_t = {ih: jnp.maximum(g_int[ih], jnp.max(log_d[ih], axis=1, keepdims=True)) for ih in pairs}
    s = {ih: qk[ih] * jnp.exp(log_d[ih] - m_t[ih]) for ih in pairs}
    scl = {ih: jnp.exp(g_int[ih] - m_t[ih]) for ih in pairs}
    num = {(i, h): mm_nn(s[(i, h)], vm[h][rs[i]]) + scl[(i, h)] * q_c[(i, h)] for i, h in pairs}
    den = {(i, h): jnp.sum(s[(i, h)], axis=1, keepdims=True)
           + scl[(i, h)] * jnp.sum(qm[h][rs[i]] * n_in[(i, h)], axis=1, keepdims=True) for i, h in pairs}
    den = {ih: jnp.maximum(jnp.abs(den[ih]), jnp.exp(-m_t[ih])) for ih in pairs}
    open_gate = _sigmoid(opre)
    hc = {(i, h): num[(i, h)] / den[(i, h)] * open_gate[rs[i], s12[h]] for i, h in pairs}
    d0 = {ih: hc[ih] - _mean(hc[ih]) for ih in pairs}
    y = {ih: d0[ih] * lax.rsqrt(_mean(d0[ih] * d0[ih]) + EPS) for ih in pairs}
    skipped = p["skip"] * xc
    out_b = {(i, h): y[(i, h)] * p["gml"][:, s12[h]] + skipped[rs[i], s12[h]] for i, h in pairs}
    ab = jnp.concatenate([jnp.concatenate([out_a[(i, h)] for h in hs] + [out_b[(i, h)] for h in hs], axis=1) for i in cs],
                         axis=0)
    new = {"S": s_new, "C": [c_in[(n_ch, h)] for h in hs], "n": [n_in[(n_ch, h)] for h in hs],
           "m": [jnp.broadcast_to(m_in[(n_ch, h)], (1, ML_DH)) for h in hs]}
    return ab, new


_P_NAMES = ("wau", "bau", "ggla", "cw", "cb", "wq", "wk", "wv", "wif", "bif", "skip", "gml")
_P_SHAPES = {
    "wau": (128, 256), "bau": (1, 256), "ggla": (1, 128), "cw": (4, 512), "cb": (1, 512),
    "wq": (512, 128), "wk": (512, 128), "wv": (512, 128),
    "wif": (1536, 128), "bif": (1, 128), "skip": (1, 512), "gml": (1, 512),
}
_P_BLOCKDIAG = ("wq", "wk", "wv")
_S_NAMES = ("S", "C", "n", "m")
_S_SHAPES = {"S": (HEADS, GLA_DV, GLA_DK), "C": (HEADS, ML_DH, ML_DH), "n": (HEADS, 1, ML_DH), "m": (HEADS, 1, ML_DH)}


def _per_head(ref):
    return [ref[h] for h in range(HEADS)]


def _block_mask():
    r = lax.broadcasted_iota(jnp.int32, (128, 128), 0)
    c = lax.broadcasted_iota(jnp.int32, (128, 128), 1)
    same_block = (r >> 2) == (c >> 2)
    spread = jnp.logical_and(r < 4, (c & 3) == r)
    return same_block.astype(F32), spread.astype(F32)


def _expand_blockdiag(w_ref, dense_ref):
    same_block, spread = _block_mask()
    for h in range(HEADS):
        tiled = _pmm_nn(w_ref[h * 128:(h + 1) * 128, :], spread)
        dense_ref[h] = tiled * same_block


def _collect_blockdiag(ddense_ref, dw_ref):
    same_block, spread = _block_mask()
    for h in range(HEADS):
        dw_ref[h * 128:(h + 1) * 128, :] = lax.dot_general(
            ddense_ref[h] * same_block, spread, (((1,), (1,)), ((), ())), precision=lax.Precision.HIGHEST,
            preferred_element_type=F32)


def _const_spec(shape):
    zeros = (0,) * len(shape)
    return pl.BlockSpec(shape, lambda i: zeros)


def _split(refs, *counts):
    out, at = [], 0
    for c in counts:
        out.append(refs[at:at + c])
        at += c
    assert at == len(refs)
    return out


def _ride(rider, phases, cond, ins, outs, sems):
    if rider is None:
        return
    lands, (send_sems, recv_sems, flush_sems) = sems[:-3], sems[-3:]

    @pl.when(cond)
    def _():
        for phase in phases:
            getattr(rider, phase)(ins, lands, send_sems, recv_sems)
        if "last" in phases:
            flush = [pltpu.make_async_copy(lands[k], outs[k], flush_sems.at[k]) for k in range(len(outs))]
            for cp in flush:
                cp.start()
            for cp in flush:
                cp.wait()


def _rider_specs(rider, rider_ins):
    if rider is None:
        return [], [], [], []
    scratch = [pltpu.VMEM(s.shape, s.dtype) for s in list(rider.out_shape) + list(getattr(rider, "work_shape", ()))]
    scratch += [pltpu.SemaphoreType.DMA((rider.n_sems,)), pltpu.SemaphoreType.DMA((rider.n_sems,)),
                pltpu.SemaphoreType.DMA((len(rider.out_shape),))]
    in_space = getattr(rider, "in_space", VMEM_WHOLE)
    return [in_space] * len(rider_ins), [ANY] * len(rider.out_shape), list(rider.out_shape), scratch


def _mixer_fwd(pm, p, rider=None, rider_ins=()):
    n_p = len(_P_NAMES)
    r_in, r_out_specs, r_out_shape, r_sems = _rider_specs(rider, rider_ins)

    def body(*refs):
        (pm_ref, xprev_ref), p_list, ride_in, (ab_ref,), so_refs, ride_out, sc_refs, dense_list, sems = _split(
            refs, 2, n_p, len(r_in), 1, 4, len(r_out_specs), 4, 3, len(r_sems))
        p_refs = dict(zip(_P_NAMES, p_list))
        dense = dict(zip(_P_BLOCKDIAG, dense_list))
        n = pl.program_id(0)
        _ride(rider, ("first",), n == 0, ride_in, ride_out, sems)

        @pl.when(n == 0)
        def _():
            for r in sc_refs:
                r[...] = jnp.zeros_like(r)
            for nm in _P_BLOCKDIAG:
                _expand_blockdiag(p_refs[nm], dense[nm])

        st = {name: _per_head(r) for name, r in zip(_S_NAMES, sc_refs)}
        pv = {nm: (_per_head(dense[nm]) if nm in _P_BLOCKDIAG else p_refs[nm][...]) for nm in _P_NAMES}
        for name, r in zip(_S_NAMES, so_refs):
            for h in range(HEADS):
                r[0, h] = st[name][h]
        xprev8 = jnp.where(n > 0, xprev_ref[CHUNK - 8:CHUNK, :], 0.0)
        ab, st = _mixer_chunk(_PLAIN_OPS, pv, st, pm_ref[...], xprev8)
        ab_ref[...] = ab.astype(BF16)
        for name, r in zip(_S_NAMES, sc_refs):
            for h in range(HEADS):
                r[h] = st[name][h]
        _ride(rider, ("middle",), n == N_SWEEP - 2, ride_in, ride_out, sems)
        _ride(rider, ("last",), n == N_SWEEP - 1, ride_in, ride_out, sems)

    in_specs = [pl.BlockSpec((SWEEP * CHUNK, PM_W), lambda i: (i, 0)),
                pl.BlockSpec((CHUNK, 512), lambda i: (jnp.maximum(SWEEP * i - 1, 0), PM_XM // 512))]
    in_specs += [_const_spec(_P_SHAPES[nm]) for nm in _P_NAMES] + r_in
    out_specs = [pl.BlockSpec((SWEEP * CHUNK, 1024), lambda i: (i, 0))]
    out_shape = [jax.ShapeDtypeStruct((SEQ, 1024), BF16)]
    for nm in _S_NAMES:
        shp = _S_SHAPES[nm]
        out_specs.append(pl.BlockSpec((1,) + shp, lambda i: (i, 0, 0, 0)))
        out_shape.append(jax.ShapeDtypeStruct((N_SWEEP,) + shp, F32))
    return pl.pallas_call(
        body, grid=(N_SWEEP,), in_specs=in_specs, out_specs=out_specs + r_out_specs, out_shape=out_shape + r_out_shape,
        scratch_shapes=[pltpu.VMEM(_S_SHAPES[nm], F32) for nm in _S_NAMES]
        + [pltpu.VMEM((HEADS, 128, 128), F32) for _ in _P_BLOCKDIAG] + r_sems,
        compiler_params=_params(("arbitrary",)), name="mixer_fwd",
    )(pm, pm, *[p[nm] for nm in _P_NAMES], *rider_ins)


def _mixer_bwd(pm, dab, states, p, rider=None, rider_ins=()):
    n_p = len(_P_NAMES)
    r_in, r_out_specs, r_out_shape, r_sems = _rider_specs(rider, rider_ins)

    def body(*refs):
        ((pm_ref, xprev_ref, dab_ref), si_refs, p_list, ride_in, (dpm_ref,), dp_list, ride_out, ds_refs, (carry_ref,),
         dense_list, ddense_list, sems) = _split(refs, 3, 4, n_p, len(r_in), 1, n_p, len(r_out_specs), 4, 1, 3, 3, len(r_sems))
        p_refs = dict(zip(_P_NAMES, p_list))
        dp_refs = dict(zip(_P_NAMES, dp_list))
        dense = dict(zip(_P_BLOCKDIAG, dense_list))
        ddense = dict(zip(_P_BLOCKDIAG, ddense_list))
        i = pl.program_id(0)
        blk = N_SWEEP - 1 - i
        _ride(rider, ("first",), i == 0, ride_in, ride_out, sems)

        @pl.when(i == 0)
        def _():
            for r in ds_refs:
                r[...] = jnp.zeros_like(r)
            for nm in _P_NAMES:
                if nm in _P_BLOCKDIAG:
                    ddense[nm][...] = jnp.zeros_like(ddense[nm])
                    _expand_blockdiag(p_refs[nm], dense[nm])
                else:
                    dp_refs[nm][...] = jnp.zeros_like(dp_refs[nm])
            carry_ref[...] = jnp.zeros_like(carry_ref)

        pv = {nm: (_per_head(dense[nm]) if nm in _P_BLOCKDIAG else p_refs[nm][...]) for nm in _P_NAMES}
        dst = {name: _per_head(r) for name, r in zip(_S_NAMES, ds_refs)}
        st = {name: [r[0, h] for h in range(HEADS)] for name, r in zip(_S_NAMES, si_refs)}
        xprev8 = jnp.where(blk > 0, xprev_ref[CHUNK - 8:CHUNK, :], 0.0)
        _, vjp = jax.vjp(functools.partial(_mixer_chunk, _VJP_OPS), pv, st, pm_ref[...], xprev8)
        dp_sum, dst, dpm, dxprev8 = vjp((dab_ref[...], dst))
        reach = jnp.concatenate([jnp.zeros((SWEEP * CHUNK - 8, 512), F32), carry_ref[...]], axis=0)
        dpm_ref[:, 0:PM_XM] = dpm[:, 0:PM_XM].astype(BF16)
        dpm_ref[:, PM_XM:PM_XM + 512] = (dpm[:, PM_XM:PM_XM + 512] + reach).astype(BF16)
        dpm_ref[:, PM_XM + 512:PM_W] = dpm[:, PM_XM + 512:PM_W].astype(BF16)
        carry_ref[...] = dxprev8
        for name, r in zip(_S_NAMES, ds_refs):
            for h in range(HEADS):
                r[h] = dst[name][h]
        for nm in _P_NAMES:
            if nm in _P_BLOCKDIAG:
                for h in range(HEADS):
                    ddense[nm][h] += dp_sum[nm][h]
            else:
                dp_refs[nm][...] += dp_sum[nm]

        @pl.when(i == N_SWEEP - 1)
        def _():
            for nm in _P_BLOCKDIAG:
                _collect_blockdiag(ddense[nm], dp_refs[nm])

        _ride(rider, ("middle",), i == N_SWEEP - 2, ride_in, ride_out, sems)
        _ride(rider, ("last",), i == N_SWEEP - 1, ride_in, ride_out, sems)

    rev = lambda i: (N_SWEEP - 1 - i, 0)
    in_specs = [pl.BlockSpec((SWEEP * CHUNK, PM_W), rev),
                pl.BlockSpec((CHUNK, 512), lambda i: (jnp.maximum(SWEEP * (N_SWEEP - 1 - i) - 1, 0), PM_XM // 512)),
                pl.BlockSpec((SWEEP * CHUNK, 1024), rev)]
    for nm in _S_NAMES:
        in_specs.append(pl.BlockSpec((1,) + _S_SHAPES[nm], lambda i: (N_SWEEP - 1 - i, 0, 0, 0)))
    in_specs += [_const_spec(_P_SHAPES[nm]) for nm in _P_NAMES] + r_in
    out_specs = [pl.BlockSpec((SWEEP * CHUNK, PM_W), rev)] + [_const_spec(_P_SHAPES[nm]) for nm in _P_NAMES]
    out_shape = [jax.ShapeDtypeStruct((SEQ, PM_W), BF16)] + [jax.ShapeDtypeStruct(_P_SHAPES[nm], F32) for nm in _P_NAMES]
    res = pl.pallas_call(
        body, grid=(N_SWEEP,), in_specs=in_specs, out_specs=out_specs + r_out_specs, out_shape=out_shape + r_out_shape,
        scratch_shapes=[pltpu.VMEM(_S_SHAPES[nm], F32) for nm in _S_NAMES] + [pltpu.VMEM((8, 512), F32)]
        + [pltpu.VMEM((HEADS, 128, 128), F32) for _ in range(2 * len(_P_BLOCKDIAG))] + r_sems,
        compiler_params=_params(("arbitrary",)), name="mixer_bwd",
    )(pm, pm, dab, *states, *[p[nm] for nm in _P_NAMES], *rider_ins)
    return res[0], dict(zip(_P_NAMES, res[1:1 + n_p])), res[1 + n_p:]


def _tok(width):
    return pl.BlockSpec((TOK_TILE, width), lambda i: (i, 0))


def _once(shape):
    zeros = (0,) * len(shape)
    return pl.BlockSpec(shape, lambda i: zeros, pipeline_mode=pl.Buffered(1))


def _rms_fwd(x):
    r = lax.rsqrt(_mean(x * x) + EPS)
    return x * r, r


def _rms_bwd(dy, xn, r, g):
    gd = dy * g
    return r * (gd - xn * _mean(xn * gd))


def _tiled_call(body, in_specs, out_specs, out_shape, args, name, rider=None, rider_ins=()):
    r_in, r_out_specs, r_out_shape, r_scratch = _rider_specs(rider, rider_ins)
    n_in, n_out = len(in_specs), len(out_specs)

    def hosted(*refs):
        ins, ride_in, outs, ride_out, scratch = _split(refs, n_in, len(r_in), n_out, len(r_out_specs), len(r_scratch))
        i = pl.program_id(0)
        _ride(rider, ("first",), i == 0, ride_in, ride_out, scratch)
        body(*ins, *outs)
        _ride(rider, ("middle",), i == N_TOK_TILE - 2, ride_in, ride_out, scratch)
        _ride(rider, ("last",), i == N_TOK_TILE - 1, ride_in, ride_out, scratch)

    res = pl.pallas_call(
        hosted, grid=(N_TOK_TILE,), in_specs=list(in_specs) + r_in, out_specs=list(out_specs) + r_out_specs,
        out_shape=list(out_shape) + r_out_shape, scratch_shapes=r_scratch,
        compiler_params=_params(("arbitrary",)), name=name,
    )(*args, *rider_ins)
    return res[:n_out], res[n_out:]


def _in_proj(x, g_pre, wt_in, rider=None, rider_ins=()):
    def body(x_ref, g_ref, wt_ref, pm_ref, gab_ref, h_ref):
        xn, _ = _rms_fwd(x_ref[...])
        h = (xn * g_ref[...]).astype(BF16)
        h_ref[...] = h
        pm_ref[:, 0:PM_XM] = _nt(h, wt_ref[0:IN_ALOW, :])
        pm_ref[:, PM_XM:PM_AL] = _nt(h, wt_ref[IN_XM:IN_GATES, :])
        pm_ref[:, PM_AL:PM_W] = _nt(h, wt_ref[IN_ALOW:IN_ALOW + 128, :])
        gab_ref[...] = _nt(h, wt_ref[IN_GATES:D_IN, :])

    return _tiled_call(
        body, [_tok(D_MODEL), _once((1, D_MODEL)), _once((D_IN, D_MODEL))], [_tok(PM_W), _tok(GAB_W), _tok(D_MODEL)],
        [jax.ShapeDtypeStruct((SEQ, PM_W), F32), jax.ShapeDtypeStruct((SEQ, GAB_W), F32),
         jax.ShapeDtypeStruct((SEQ, D_MODEL), BF16)], (x, g_pre, wt_in), "in_proj", rider, rider_ins)


def _merge_fwd(ab, gab, x, w_pa4, w_pb4, w_o, g_post, rider=None, rider_ins=()):
    def body(ab_ref, gab_ref, x_ref, wpa_ref, wpb_ref, wo_ref, g_ref, x1_ref, mix_ref, mg_ref):
        a = ab_ref[:, 0:512]
        b = ab_ref[:, 512:1024]
        for j in range(N_CHIP):
            blk = slice(j * 256, (j + 1) * 256)
            ya = jnp.dot(a, wpa_ref[j], preferred_element_type=F32)
            yb = jnp.dot(b, wpb_ref[j], preferred_element_type=F32)
            sa = _sigmoid(gab_ref[:, j * 256:(j + 1) * 256])
            sb = _sigmoid(gab_ref[:, 1024 + j * 256:1024 + (j + 1) * 256])
            mg_ref[:, blk] = (sa * ya + sb * yb).astype(BF16)
        mix = jnp.dot(mg_ref[...], wo_ref[...], preferred_element_type=F32)
        mix_ref[...] = mix
        mn, _ = _rms_fwd(mix)
        x1_ref[...] = x_ref[...] + mn * g_ref[...]

    return _tiled_call(
        body, [_tok(1024), _tok(GAB_W), _tok(D_MODEL), _once((N_CHIP, 512, 256)), _once((N_CHIP, 512, 256)),
               _once((D_MODEL, D_MODEL)), _once((1, D_MODEL))], [_tok(D_MODEL), _tok(D_MODEL), _tok(D_MODEL)],
        [jax.ShapeDtypeStruct((SEQ, D_MODEL), F32), jax.ShapeDtypeStruct((SEQ, D_MODEL), F32),
         jax.ShapeDtypeStruct((SEQ, D_MODEL), BF16)], (ab, gab, x, w_pa4, w_pb4, w_o, g_post), "merge_fwd", rider, rider_ins)


def _mlp(x1, target, g_pre, g_post, w_up4, w_down_a4, w_down_b4):
    def body(x1_ref, t_ref, gpre_ref, gpost_ref, wup_ref, wda_ref, wdb_ref,
             dx1_ref, u_ref, dd_ref, h2_ref, dpre_ref, dgpost_ref, dgpre_ref, loss_ref):
        @pl.when(pl.program_id(0) == 0)
        def _():
            dgpost_ref[...] = jnp.zeros_like(dgpost_ref)
            dgpre_ref[...] = jnp.zeros_like(dgpre_ref)
            loss_ref[...] = jnp.zeros_like(loss_ref)

        x1 = x1_ref[...]
        gpre = gpre_ref[...]
        gpost = gpost_ref[...]
        xn2, r2 = _rms_fwd(x1)
        h2 = (xn2 * gpre).astype(BF16)
        h2_ref[...] = h2
        rl = []
        d = jnp.zeros((TOK_TILE, D_MODEL), F32)
        for j in range(N_CHIP):
            blk = slice(j * 1024, (j + 1) * 1024)
            r = jnp.maximum(jnp.dot(h2, wup_ref[j], preferred_element_type=F32), 0.0)
            rl.append(r)
            u = (r * r).astype(BF16)
            u_ref[:, blk] = u
            d = d + jnp.dot(u[:, 0:512], wda_ref[j], preferred_element_type=F32)
            d = d + jnp.dot(u[:, 512:1024], wdb_ref[j], preferred_element_type=F32)
        dn, r3 = _rms_fwd(d)
        diff = x1 + dn * gpost - t_ref[...]
        loss_ref[...] += jnp.sum(diff * diff, keepdims=True) * (0.5 / D_MODEL)
        dy = diff * (1.0 / D_MODEL)
        dgpost_ref[...] += jnp.sum(dy * dn, axis=0, keepdims=True)
        dd = _rms_bwd(dy, dn, r3, gpost).astype(BF16)
        dd_ref[...] = dd
        dh2 = jnp.zeros((TOK_TILE, D_MODEL), F32)
        for j in range(N_CHIP):
            blk = slice(j * 1024, (j + 1) * 1024)
            du = jnp.concatenate([_nt(dd, wda_ref[j]), _nt(dd, wdb_ref[j])], axis=1)
            dpre = (du * (2.0 * rl[j])).astype(BF16)
            dpre_ref[:, blk] = dpre
            dh2 = dh2 + _nt(dpre, wup_ref[j])
        dgpre_ref[...] += jnp.sum(dh2 * xn2, axis=0, keepdims=True)
        dx1_ref[...] = dy + _rms_bwd(dh2, xn2, r2, gpre)

    acc = pl.BlockSpec((1, D_MODEL), lambda i: (0, 0))
    return pl.pallas_call(
        body, grid=(N_TOK_TILE,),
        in_specs=[_tok(D_MODEL), _tok(D_MODEL), _once((1, D_MODEL)), _once((1, D_MODEL)),
                  _once((N_CHIP, D_MODEL, 1024)), _once((N_CHIP, 512, D_MODEL)), _once((N_CHIP, 512, D_MODEL))],
        out_specs=[_tok(D_MODEL), _tok(D_FF), _tok(D_MODEL), _tok(D_MODEL), _tok(D_FF), acc, acc,
                   pl.BlockSpec((1, 128), lambda i: (0, 0))],
        out_shape=[jax.ShapeDtypeStruct((SEQ, D_MODEL), F32), jax.ShapeDtypeStruct((SEQ, D_FF), BF16),
                   jax.ShapeDtypeStruct((SEQ, D_MODEL), BF16), jax.ShapeDtypeStruct((SEQ, D_MODEL), BF16),
                   jax.ShapeDtypeStruct((SEQ, D_FF), BF16), jax.ShapeDtypeStruct((1, D_MODEL), F32),
                   jax.ShapeDtypeStruct((1, D_MODEL), F32), jax.ShapeDtypeStruct((1, 128), F32)],
        compiler_params=_params(("arbitrary",)), name="mlp_fwd_bwd",
    )(x1, target, g_pre, g_post, w_up4, w_down_a4, w_down_b4)


def _merge_bwd(dx1, mix, ab, gab, w_pa4, w_pb4, w_o, g_post):
    def body(dx1_ref, mix_ref, ab_ref, gab_ref, wpa_ref, wpb_ref, wo_ref, g_ref,
             dmix_ref, dya_ref, dyb_ref, dgab_ref, dab_ref, dg_ref):
        @pl.when(pl.program_id(0) == 0)
        def _():
            dg_ref[...] = jnp.zeros_like(dg_ref)

        dx1 = dx1_ref[...]
        mn, r = _rms_fwd(mix_ref[...])
        dg_ref[...] += jnp.sum(dx1 * mn, axis=0, keepdims=True)
        dmix = _rms_bwd(dx1, mn, r, g_ref[...]).astype(BF16)
        dmix_ref[...] = dmix
        dmerged = _nt(dmix, wo_ref[...])
        a = ab_ref[:, 0:512]
        b = ab_ref[:, 512:1024]
        da = jnp.zeros((TOK_TILE, 512), F32)
        db = jnp.zeros((TOK_TILE, 512), F32)
        for j in range(N_CHIP):
            blk = slice(j * 256, (j + 1) * 256)
            blk_b = slice(1024 + j * 256, 1024 + (j + 1) * 256)
            dm = dmerged[:, blk]
            ya = jnp.dot(a, wpa_ref[j], preferred_element_type=F32)
            yb = jnp.dot(b, wpb_ref[j], preferred_element_type=F32)
            sa = _sigmoid(gab_ref[:, blk])
            sb = _sigmoid(gab_ref[:, blk_b])
            dya = (dm * sa).astype(BF16)
            dyb = (dm * sb).astype(BF16)
            dya_ref[:, blk] = dya
            dyb_ref[:, blk] = dyb
            dgab_ref[:, blk] = (dm * ya * sa * (1.0 - sa)).astype(BF16)
            dgab_ref[:, blk_b] = (dm * yb * sb * (1.0 - sb)).astype(BF16)
            da = da + _nt(dya, wpa_ref[j])
            db = db + _nt(dyb, wpb_ref[j])
        dab_ref[:, 0:512] = da
        dab_ref[:, 512:1024] = db

    return pl.pallas_call(
        body, grid=(N_TOK_TILE,),
        in_specs=[_tok(D_MODEL), _tok(D_MODEL), _tok(1024), _tok(GAB_W), _once((N_CHIP, 512, 256)),
                  _once((N_CHIP, 512, 256)), _once((D_MODEL, D_MODEL)), _once((1, D_MODEL))],
        out_specs=[_tok(D_MODEL), _tok(D_MODEL), _tok(D_MODEL), _tok(GAB_W), _tok(1024),
                   pl.BlockSpec((1, D_MODEL), lambda i: (0, 0))],
        out_shape=[jax.ShapeDtypeStruct((SEQ, D_MODEL), BF16), jax.ShapeDtypeStruct((SEQ, D_MODEL), BF16),
                   jax.ShapeDtypeStruct((SEQ, D_MODEL), BF16), jax.ShapeDtypeStruct((SEQ, GAB_W), BF16),
                   jax.ShapeDtypeStruct((SEQ, 1024), F32), jax.ShapeDtypeStruct((1, D_MODEL), F32)],
        compiler_params=_params(("arbitrary",)), name="merge_bwd",
    )(dx1, mix, ab, gab, w_pa4, w_pb4, w_o, g_post)


def _in_proj_bwd(dpm, dgab, x, dx1, g_pre, wt_in, rider=None, rider_ins=()):
    def body(dpm_ref, dgab_ref, x_ref, dx1_ref, g_ref, wt_ref, dx_ref, dg_ref):
        @pl.when(pl.program_id(0) == 0)
        def _():
            dg_ref[...] = jnp.zeros_like(dg_ref)

        dh = jnp.dot(dpm_ref[:, 0:PM_XM], wt_ref[0:IN_ALOW, :], preferred_element_type=F32)
        dh = dh + jnp.dot(dpm_ref[:, PM_XM:PM_AL], wt_ref[IN_XM:IN_GATES, :], preferred_element_type=F32)
        dh = dh + jnp.dot(dpm_ref[:, PM_AL:PM_W], wt_ref[IN_ALOW:IN_ALOW + 128, :], preferred_element_type=F32)
        dh = dh + jnp.dot(dgab_ref[...], wt_ref[IN_GATES:D_IN, :], preferred_element_type=F32)
        xn, r = _rms_fwd(x_ref[...])
        dg_ref[...] += jnp.sum(dh * xn, axis=0, keepdims=True)
        dx_ref[...] = dx1_ref[...] + _rms_bwd(dh, xn, r, g_ref[...])

    return _tiled_call(
        body, [_tok(PM_W), _tok(GAB_W), _tok(D_MODEL), _tok(D_MODEL), _once((1, D_MODEL)), _once((D_IN, D_MODEL))],
        [_tok(D_MODEL), pl.BlockSpec((1, D_MODEL), lambda i: (0, 0))],
        [jax.ShapeDtypeStruct((SEQ, D_MODEL), F32), jax.ShapeDtypeStruct((1, D_MODEL), F32)],
        (dpm, dgab, x, dx1, g_pre, wt_in), "in_proj_bwd", rider, rider_ins)


def _dw_in(dpm, dgab, h):
    n_pm = PM_AL // 512
    n_blk = n_pm + GAB_W // 512

    def body(dpm_ref, dgab_ref, dal_ref, h_ref, o_ref):
        i = pl.program_id(0)
        off = pl.multiple_of(i * 512 + 16 * (i >= 3).astype(jnp.int32), 16)

        @pl.when(i < n_pm)
        def _():
            o_ref[pl.ds(off, 512), :] = _tn(dpm_ref[...], h_ref[...]).astype(BF16)

        @pl.when(i >= n_pm)
        def _():
            o_ref[pl.ds(off, 512), :] = _tn(dgab_ref[...], h_ref[...]).astype(BF16)

        @pl.when(i == 0)
        def _():
            o_ref[IN_ALOW:IN_XM, :] = _tn(dal_ref[...], h_ref[...])[0:IN_XM - IN_ALOW].astype(BF16)

    return pl.pallas_call(
        body, grid=(n_blk,),
        in_specs=[pl.BlockSpec((SEQ, 512), lambda i: (0, jnp.minimum(i, n_pm - 1))),
                  pl.BlockSpec((SEQ, 512), lambda i: (0, jnp.maximum(i - n_pm, 0))),
                  pl.BlockSpec((SEQ, 128), lambda i: (0, PM_AL // 128)),
                  _once((SEQ, D_MODEL))],
        out_specs=pl.BlockSpec((D_IN, D_MODEL), lambda i: (0, 0)),
        out_shape=jax.ShapeDtypeStruct((D_IN, D_MODEL), BF16),
        compiler_params=_params(("arbitrary",)), name="dw_in",
    )(dpm, dgab, dpm, h)


def _tn_matmul(a, b, name, shards=1, tm=512, rider=None, rider_ins=()):
    m, n = a.shape[1], b.shape[1]
    tm = min(tm, m)
    tn = n // shards if shards > 1 else min(n, 1024)
    steps_i, steps_j = m // tm, n // tn
    r_in, r_out_specs, r_out_shape, r_scratch = _rider_specs(rider, rider_ins)

    def body(*refs):
        (a_ref, b_ref), ride_in, (o_ref,), ride_out, scratch = _split(refs, 2, len(r_in), 1, len(r_out_specs), len(r_scratch))
        step = pl.program_id(0) * steps_j + pl.program_id(1)
        _ride(rider, ("first",), step == 0, ride_in, ride_out, scratch)
        o_ref[...] = _tn(a_ref[...], b_ref[...]).astype(BF16)
        _ride(rider, ("middle", "last"), step == steps_i * steps_j - 1, ride_in, ride_out, scratch)

    if shards > 1:
        out_spec = pl.BlockSpec((None, tm, tn), lambda i, j: (j, i, 0))
        out_shape = jax.ShapeDtypeStruct((shards, m, tn), BF16)
    else:
        out_spec = pl.BlockSpec((tm, tn), lambda i, j: (i, j))
        out_shape = jax.ShapeDtypeStruct((m, n), BF16)
    res = pl.pallas_call(
        body, grid=(steps_i, steps_j),
        in_specs=[pl.BlockSpec((SEQ, tm), lambda i, j: (0, i)), pl.BlockSpec((SEQ, tn), lambda i, j: (0, j))] + r_in,
        out_specs=[out_spec] + r_out_specs, out_shape=[out_shape] + r_out_shape, scratch_shapes=r_scratch,
        compiler_params=_params(("arbitrary", "arbitrary")), name=name,
    )(a, b, *rider_ins)
    return res[0] if rider is None else (res[0], res[1:])


MESH = pl.DeviceIdType.MESH
ANY = pl.BlockSpec(memory_space=pl.ANY)
VMEM_WHOLE = pl.BlockSpec(memory_space=pltpu.VMEM)

_BIG = ("w_in", "w_pa", "w_pb", "w_o", "w_up", "w_down")
_BIG_SHARD = {"w_in": (IN_SHARD, D_MODEL), "w_pa": (512, 256), "w_pb": (512, 256), "w_o": (256, D_MODEL),
              "w_up": (D_MODEL, 1024), "w_down": (1024, D_MODEL),
              "w_down_a": (512, D_MODEL), "w_down_b": (512, D_MODEL)}
_BIG_SPLIT = {"w_in": 1, "w_pa": 0, "w_pb": 0, "w_o": 0, "w_up": 0, "w_down": 0, "w_down_a": 0, "w_down_b": 0}


def _half(ref, e, name, lead=0):
    axis = _BIG_SPLIT[name]
    size = _BIG_SHARD[name][axis] // 2
    start = pl.multiple_of(e * size, 128 if axis == 1 else 16)
    idx = [pl.ds(0, ref.shape[a]) for a in range(lead)]
    idx += [pl.ds(start, size), pl.ds(0, _BIG_SHARD[name][1])] if axis == 0 else [pl.ds(0, _BIG_SHARD[name][0]), pl.ds(start, size)]
    return ref.at[tuple(idx)]


def _half_shape(name):
    r, c = _BIG_SHARD[name]
    return (r // 2, c) if _BIG_SPLIT[name] == 0 else (r, c // 2)


def _remote(src, dst, send_sems, recv_sems, k, to):
    return pltpu.make_async_remote_copy(src_ref=src, dst_ref=dst, send_sem=send_sems.at[k], recv_sem=recv_sems.at[k],
                                        device_id=to, device_id_type=MESH)


def _mesh_place():
    x, y, c = lax.axis_index("x"), lax.axis_index("y"), lax.axis_index("c")
    return x, y, c, [(1 - x, y), (x, 1 - y), (1 - x, 1 - y)]


class _Gather:
    def __init__(self, names, small=()):
        self.names = tuple(names)
        self.nb = len(self.names)
        self.n = self.nb + len(small)
        self.n_sems = 6 * self.n
        self.out_shape = [jax.ShapeDtypeStruct((N_CHIP,) + _BIG_SHARD[nm], BF16) for nm in self.names]
        self.out_shape += [jax.ShapeDtypeStruct((N_CHIP,) + s.shape, s.dtype) for s in small]

    def _ici(self, ins, outs, ss, rs, k, j, peer, slot, c):
        if k < self.nb:
            return _remote(_half(ins[k], c, self.names[k]), _half(outs[k].at[slot], c, self.names[k]), ss, rs, 6 * k + j,
                           (*peer, c))
        return _remote(ins[k], outs[k].at[slot], ss, rs, 6 * k + j, (*peer, c))

    def _passed(self, outs, ss, rs, k, j, slot, e, sibling):
        part = _half(outs[k].at[slot], e, self.names[k])
        return _remote(part, part, ss, rs, 6 * k + 3 + j, sibling)

    def first(self, ins, outs, ss, rs):
        x, y, c, peers = _mesh_place()
        me = 2 * x + y
        for k in range(self.n):
            for j, peer in enumerate(peers):
                self._ici(ins, outs, ss, rs, k, j, peer, me, c).start()
        for k in range(self.n):
            outs[k][me] = ins[k][...]

    def middle(self, ins, outs, ss, rs):
        x, y, c, peers = _mesh_place()
        for j, (px, py) in enumerate(peers):
            for k in range(self.nb):
                self._ici(ins, outs, ss, rs, k, j, (px, py), 2 * px + py, c).wait_recv()
                self._passed(outs, ss, rs, k, j, 2 * px + py, c, (x, y, 1 - c)).start()

    def last(self, ins, outs, ss, rs):
        x, y, c, peers = _mesh_place()
        for j, (px, py) in enumerate(peers):
            for k in range(self.n):
                if k < self.nb:
                    self._passed(outs, ss, rs, k, j, 2 * px + py, 1 - c, (x, y, 1 - c)).wait_recv()
                    self._passed(outs, ss, rs, k, j, 2 * px + py, c, (x, y, 1 - c)).wait_send()
                else:
                    self._ici(ins, outs, ss, rs, k, j, (px, py), 2 * px + py, c).wait_recv()
                self._ici(ins, outs, ss, rs, k, j, (px, py), 2 * x + y, c).wait_send()


def _run_alone(rider, ins, name):
    def body(*refs):
        r_in, r_out, sems = _split(refs, len(ins), len(rider.out_shape), 2)
        rider.first(r_in, r_out, *sems)
        rider.middle(r_in, r_out, *sems)
        rider.last(r_in, r_out, *sems)

    return pl.pallas_call(
        body, in_specs=[VMEM_WHOLE] * len(ins), out_specs=[VMEM_WHOLE] * len(rider.out_shape), out_shape=rider.out_shape,
        scratch_shapes=[pltpu.SemaphoreType.DMA((rider.n_sems,)), pltpu.SemaphoreType.DMA((rider.n_sems,))],
        compiler_params=_params(), name=name,
    )(*ins)


class _Presum:
    in_space = ANY

    def __init__(self, names):
        self.names = tuple(names)
        self.n = len(self.names)
        self.n_sems = 3 * self.n
        self.out_shape = [jax.ShapeDtypeStruct((N_CHIP,) + _half_shape(nm), BF16) for nm in self.names]
        self.work_shape = self.out_shape + self.out_shape

    def _stage(self, ins, bufs, ss, k, e, which):
        n = self.n
        return pltpu.make_async_copy(_half(ins[k], e, self.names[k], lead=1), bufs[which * n + k], ss.at[which * n + k])

    def _give(self, bufs, ss, rs, k, sibling):
        return _remote(bufs[self.n + k], bufs[k], ss, rs, k, sibling)

    def first(self, ins, bufs, ss, rs):
        x, y, c, _ = _mesh_place()
        for k in range(self.n):
            self._stage(ins, bufs, ss, k, 1 - c, 1).start()
        for k in range(self.n):
            self._stage(ins, bufs, ss, k, c, 2).start()
        for k in range(self.n):
            self._stage(ins, bufs, ss, k, 1 - c, 1).wait()
            self._give(bufs, ss, rs, k, (x, y, 1 - c)).start()

    def middle(self, ins, bufs, ss, rs):
        pass

    def last(self, ins, bufs, ss, rs):
        x, y, c, _ = _mesh_place()
        for k in range(self.n):
            self._give(bufs, ss, rs, k, (x, y, 1 - c)).wait_recv()
            self._stage(ins, bufs, ss, k, c, 2).wait()

            @pl.loop(0, N_CHIP)
            def _(j):
                bufs[k][j] = (bufs[k][j].astype(F32) + bufs[2 * self.n + k][j].astype(F32)).astype(BF16)
        for k in range(self.n):
            self._give(bufs, ss, rs, k, (x, y, 1 - c)).wait_send()


def _presum(names, grads, name):
    rider = _Presum(names)
    n = rider.n

    def body(*refs):
        g_refs, got_refs, work_refs, sems = _split(refs, n, n, 2 * n, 2)
        bufs = list(got_refs) + list(work_refs)
        rider.first(g_refs, bufs, *sems)
        rider.last(g_refs, bufs, *sems)

    return pl.pallas_call(
        body, in_specs=[ANY] * n, out_specs=[VMEM_WHOLE] * n, out_shape=rider.out_shape,
        scratch_shapes=[pltpu.VMEM(s.shape, s.dtype) for s in rider.work_shape]
        + [pltpu.SemaphoreType.DMA((rider.n_sems,)), pltpu.SemaphoreType.DMA((rider.n_sems,))],
        compiler_params=_params(), name=name,
    )(*grads)


class _SendPartials:
    def __init__(self, names, small_shape=None):
        self.n = len(names)
        self.small = small_shape is not None
        self.n_sems = 3 * self.n + 7
        self.out_shape = [jax.ShapeDtypeStruct((N_CHIP,) + _half_shape(nm), BF16) for nm in names]
        if self.small:
            self.out_shape.append(jax.ShapeDtypeStruct((N_DEV,) + small_shape, F32))

    def _piece(self, ins, outs, ss, rs, k, j, peer, src_slot, dst_slot, c):
        return _remote(ins[k].at[src_slot], outs[k].at[dst_slot], ss, rs, 3 * k + j, (*peer, c))

    def _small(self, ins, outs, ss, rs, r, other, slot):
        return _remote(ins[self.n], outs[self.n].at[slot], ss, rs, 3 * self.n + r, other)

    @staticmethod
    def _others(x, y, c):
        return [(x, y, 1 - c), (1 - x, y, c), (1 - x, y, 1 - c), (x, 1 - y, c), (x, 1 - y, 1 - c),
                (1 - x, 1 - y, c), (1 - x, 1 - y, 1 - c)]

    def first(self, ins, outs, ss, rs):
        x, y, c, peers = _mesh_place()
        me = 2 * x + y
        for k in range(self.n):
            for j, (px, py) in enumerate(peers):
                self._piece(ins, outs, ss, rs, k, j, (px, py), 2 * px + py, me, c).start()
        if self.small:
            for r, other in enumerate(self._others(x, y, c)):
                self._small(ins, outs, ss, rs, r, other, 4 * x + 2 * y + c).start()
            outs[self.n][4 * x + 2 * y + c] = ins[self.n][...]
        for k in range(self.n):
            outs[k][me] = ins[k][me]

    def middle(self, ins, outs, ss, rs):
        pass

    def last(self, ins, outs, ss, rs):
        x, y, c, peers = _mesh_place()
        me = 2 * x + y
        for k in range(self.n):
            for j, (px, py) in enumerate(peers):
                self._piece(ins, outs, ss, rs, k, j, (px, py), me, 2 * px + py, c).wait_recv()
                self._piece(ins, outs, ss, rs, k, j, (px, py), 2 * px + py, me, c).wait_send()
        if self.small:
            for r, (px, py, pc) in enumerate(self._others(x, y, c)):
                self._small(ins, outs, ss, rs, r, (px, py, pc), 4 * px + 2 * py + pc).wait_recv()
                self._small(ins, outs, ss, rs, r, (px, py, pc), 4 * x + 2 * y + c).wait_send()


def _sum_swap(names, parts, small):
    n = len(parts)
    everyone = _SendPartials((), small.shape)

    def body(*refs):
        p_refs, (small_ref,), o_refs, (osmall_ref,), (all_ref,), (send_sems, recv_sems, ss_small, rs_small) = _split(
            refs, n, 1, n, 1, 1, 4)
        x, y, c = lax.axis_index("x"), lax.axis_index("y"), lax.axis_index("c")
        everyone.first([small_ref], [all_ref], ss_small, rs_small)

        def mine(k):
            part = _half(o_refs[k], c, names[k])
            return _remote(part, part, send_sems, recv_sems, k, (x, y, 1 - c))

        for k in range(n):
            for e in range(2):
                @pl.when(c == e)
                def _():
                    g = p_refs[k][0].astype(F32)
                    for s in range(1, N_CHIP):
                        g = g + p_refs[k][s].astype(F32)
                    r, cols = _half_shape(names[k])
                    if _BIG_SPLIT[names[k]] == 0:
                        o_refs[k][e * r:(e + 1) * r, :] = g
                    else:
                        o_refs[k][:, e * cols:(e + 1) * cols] = g
            mine(k).start()
        for k in range(n):
            theirs = _half(o_refs[k], 1 - c, names[k])
            _remote(theirs, theirs, send_sems, recv_sems, k, (x, y, 1 - c)).wait_recv()
            mine(k).wait_send()
        everyone.last([small_ref], [all_ref], ss_small, rs_small)
        g = all_ref[0]
        for d in range(1, N_DEV):
            g = g + all_ref[d]
        osmall_ref[...] = g

    res = pl.pallas_call(
        body, in_specs=[VMEM_WHOLE] * (n + 1), out_specs=[VMEM_WHOLE] * (n + 1),
        out_shape=[jax.ShapeDtypeStruct(_BIG_SHARD[nm], F32) for nm in names] + [jax.ShapeDtypeStruct(small.shape, F32)],
        scratch_shapes=[pltpu.VMEM((N_DEV,) + small.shape, F32), pltpu.SemaphoreType.DMA((n,)), pltpu.SemaphoreType.DMA((n,)),
                        pltpu.SemaphoreType.DMA((everyone.n_sems,)), pltpu.SemaphoreType.DMA((everyone.n_sems,))],
        compiler_params=_params(), name="sum_swap",
    )(*parts, small)
    return res[:n], res[n]


def _tile(rows, cols, itemsize, budget):
    t = cols if rows % 16 else rows
    other = rows if rows % 16 else cols
    step = 256 if rows % 16 else 32
    while t % step == 0 and t * other * itemsize > budget:
        t //= 2
    return (rows, t) if rows % 16 else (t, cols)


def _adamw_math(w, g, m, v):
    m = ADAM_B1 * m + (1.0 - ADAM_B1) * g
    v = ADAM_B2 * v + (1.0 - ADAM_B2) * (g * g)
    m_hat = m / (1.0 - ADAM_B1 ** ADAM_STEP)
    v_hat = v / (1.0 - ADAM_B2 ** ADAM_STEP)
    delta = -ADAM_LR * (m_hat / (jnp.sqrt(v_hat) + ADAM_EPS) + ADAM_WD * w)
    return delta, m, v


def _adamw_big(g, w, m, v, name):
    r, c = w.shape
    tr, tc = _tile(r, c, 4, 1024 * 1024)

    def body(g_ref, w_ref, m_ref, v_ref, d_ref, nm_ref, nv_ref):
        d_ref[...], nm_ref[...], nv_ref[...] = _adamw_math(w_ref[...], g_ref[...], m_ref[...], v_ref[...])

    blk = pl.BlockSpec((tr, tc), lambda i, l: (i, l))
    return pl.pallas_call(
        body, grid=(r // tr, c // tc), in_specs=[blk, blk, blk, blk],
        out_specs=[blk, blk, blk], out_shape=[jax.ShapeDtypeStruct((r, c), F32)] * 3,
        compiler_params=_params(("arbitrary", "arbitrary")), name=name,
    )(g, w, m, v)


def _adamw_rows(g, w, m, v, name):
    r, k, lanes = w.shape
    tr = 296

    def body(g_ref, w_ref, m_ref, v_ref, g3_ref, d_ref, nm_ref, nv_ref):
        g = g_ref[...].reshape(tr, k, lanes)
        g3_ref[...] = g
        d_ref[...], nm_ref[...], nv_ref[...] = _adamw_math(w_ref[...], g, m_ref[...], v_ref[...])

    rows = pl.BlockSpec((tr, k, lanes), lambda i: (i, 0, 0))
    return pl.pallas_call(
        body, grid=(pl.cdiv(r, tr),), in_specs=[pl.BlockSpec((tr, k * lanes), lambda i: (i, 0)), rows, rows, rows],
        out_specs=[rows] * 4, out_shape=[jax.ShapeDtypeStruct((r, k, lanes), F32)] * 4,
        compiler_params=_params(("arbitrary",)), name=name,
    )(g, w, m, v)


def _adamw_small(ws, gs, ms, vs):
    n = len(ws)

    def body(*refs):
        w_refs, g_refs, m_refs, v_refs, d_refs, nm_refs, nv_refs = _split(refs, *([n] * 7))
        for k in range(n):
            d_refs[k][...], nm_refs[k][...], nv_refs[k][...] = _adamw_math(w_refs[k][...], g_refs[k][...], m_refs[k][...],
                                                                             v_refs[k][...])

    shapes = [jax.ShapeDtypeStruct(w.shape, F32) for w in ws]
    res = pl.pallas_call(body, out_shape=shapes * 3, name="adamw_small")(*ws, *gs, *ms, *vs)
    return res[:n], res[n:2 * n], res[2 * n:]


def _pack(arrs):
    flat = jnp.concatenate([a.reshape(-1) for a in arrs])
    rows = -(-flat.shape[0] // 1024) * 8
    return jnp.pad(flat, (0, rows * 128 - flat.shape[0])).reshape(rows, 128)


def _unpack(buf, shapes):
    flat = buf.reshape(-1)
    out, off = [], 0
    for s in shapes:
        size = 1
        for d in s:
            size *= d
        out.append(flat[off:off + size].reshape(s))
        off += size
    return out


def _block_rows(w):
    return jnp.pad(w.reshape(512, 4), ((0, 0), (0, 124)))


def _cols(a4):
    return jnp.transpose(a4, (1, 0, 2)).reshape(a4.shape[1], -1)


_LATE = ("w_pa", "w_pb", "w_o", "w_up", "w_down")
_RIDE_IN_PROJ = ("w_pa", "w_pb", "w_o")
_RIDE_MIXER = ("w_up", "w_down_a")
_RIDE_MERGE = ("w_down_b",)


def _full_weights(gathered):
    joined = {"w_in": (D_IN, D_MODEL), "w_o": (D_MODEL, D_MODEL)}
    return {n: (a.reshape(joined[n]) if n in joined else a) for n, a in gathered.items()}


def _local_step(x, target, w, sp, late_shards=None):
    sp = {n: (a.reshape(1, -1) if a.ndim == 1 else a) for n, a in sp.items()}
    wau = jnp.zeros((128, 256), F32).at[0:16].set(sp["w_a_up"])
    wif = jnp.zeros((1536, 128), F32).at[:, 0:8].set(sp["w_if"])
    bif = jnp.zeros((1, 128), F32).at[:, 0:8].set(sp["b_if"])
    p = {"wau": wau, "bau": sp["b_a_up"], "ggla": sp["g_gla_norm"], "cw": sp["conv_w"], "cb": sp["conv_b"],
         "wq": _block_rows(sp["w_q_ml"]), "wk": _block_rows(sp["w_k_ml"]), "wv": _block_rows(sp["w_v_ml"]),
         "wif": wif, "bif": bif, "skip": sp["ml_skip"], "gml": sp["g_ml_norm"]}

    if late_shards is None:
        (pm, gab, h), _ = _in_proj(x, sp["g_pre_mix"], w["w_in"])
        ab, *states = _mixer_fwd(pm, p)
        (x1, mix, merged), _ = _merge_fwd(ab, gab, x, w["w_pa"], w["w_pb"], w["w_o"], sp["g_post_mix"])
    else:
        shard = dict(zip(_LATE, late_shards))
        shard["w_down_a"], shard["w_down_b"] = shard["w_down"][0:512], shard["w_down"][512:1024]
        (pm, gab, h), got = _in_proj(x, sp["g_pre_mix"], w["w_in"], _Gather(_RIDE_IN_PROJ),
                                     [shard[n] for n in _RIDE_IN_PROJ])
        w = dict(w, **_full_weights(dict(zip(_RIDE_IN_PROJ, got))))
        ab, *rest = _mixer_fwd(pm, p, _Gather(_RIDE_MIXER), [shard[n] for n in _RIDE_MIXER])
        states = rest[:4]
        w.update(_full_weights(dict(zip(_RIDE_MIXER, rest[4:]))))
        (x1, mix, merged), got = _merge_fwd(ab, gab, x, w["w_pa"], w["w_pb"], w["w_o"], sp["g_post_mix"],
                                            _Gather(_RIDE_MERGE), [shard[n] for n in _RIDE_MERGE])
        w.update(_full_weights(dict(zip(_RIDE_MERGE, got))))
    dx1, u, dd, h2, dpre, dg_post_mlp, dg_pre_mlp, loss = _mlp(x1, target, sp["g_pre_mlp"], sp["g_post_mlp"],
                                                                w["w_up"], w["w_down_a"], w["w_down_b"])
    dmix, dya, dyb, dgab, dab, dg_post_mix = _merge_bwd(dx1, mix, ab, gab, w["w_pa"], w["w_pb"], w["w_o"], sp["g_post_mix"])
    big = {
        "w_pa": _tn_matmul(ab[:, 0:512], dya, "dw_pa", shards=N_CHIP),
        "w_pb": _tn_matmul(ab[:, 512:1024], dyb, "dw_pb", shards=N_CHIP),
        "w_o": _tn_matmul(merged, dmix, "dw_o"),
        "w_up": _tn_matmul(h2, dpre, "dw_up", shards=N_CHIP),
    }
    if late_shards is None:
        big["w_down"] = _tn_matmul(u, dd, "dw_down")
        dpm, dp, _ = _mixer_bwd(pm, dab, states, p)
    else:
        pieces = lambda n: big[n].reshape((N_CHIP,) + _BIG_SHARD[n])
        big["w_down"], partial = _tn_matmul(u, dd, "dw_down", rider=_Presum(_LATE[:4]),
                                            rider_ins=[pieces(n) for n in _LATE[:4]])
        partial = list(partial) + list(_presum(("w_down",), [pieces("w_down")], "presum_w_down"))
        dpm, dp, parts = _mixer_bwd(pm, dab, states, p, _SendPartials(_LATE), partial)
        big = dict(zip(_LATE, parts))
    big["w_in"] = _dw_in(dpm, dgab, h)
    if late_shards is None:
        (dx, dg_pre_mix), _ = _in_proj_bwd(dpm, dgab, x, dx1, sp["g_pre_mix"], w["w_in"])
    else:
        partial = _presum(("w_in",), [big["w_in"].reshape((N_CHIP,) + _BIG_SHARD["w_in"])], "presum_w_in")
        (dx, dg_pre_mix), parts = _in_proj_bwd(dpm, dgab, x, dx1, sp["g_pre_mix"], w["w_in"], _SendPartials(("w_in",)),
                                               partial)
        big["w_in"] = parts[0]
    small = {
        "g_pre_mix": dg_pre_mix, "b_a_up": dp["bau"], "g_gla_norm": dp["ggla"], "conv_b": dp["cb"],
        "w_q_ml": dp["wq"][:, 0:4].reshape(128, 4, 4), "w_k_ml": dp["wk"][:, 0:4].reshape(128, 4, 4),
        "w_v_ml": dp["wv"][:, 0:4].reshape(128, 4, 4),
        "b_if": dp["bif"][:, 0:8], "ml_skip": dp["skip"], "g_ml_norm": dp["gml"], "g_post_mix": dg_post_mix,
        "g_pre_mlp": dg_pre_mlp, "g_post_mlp": dg_post_mlp, "w_a_up": dp["wau"][0:16], "conv_w": dp["cw"],
        "w_if": dp["wif"][:, 0:8], "loss": loss[:, 0:1],
    }
    return dx, big, small


_SMALL_REPL = ("g_pre_mix", "b_a_up", "g_gla_norm", "conv_b", "w_q_ml", "w_k_ml", "w_v_ml", "b_if", "ml_skip",
               "g_ml_norm", "g_post_mix", "g_pre_mlp", "g_post_mlp")
_SMALL_SHARDED = ("w_a_up", "conv_w", "w_if")
_SMALL_ORDER = _SMALL_REPL + _SMALL_SHARDED + ("loss",)
_WEIGHTS = ("g_pre_mix", "w_in", "w_a_up", "b_a_up", "g_gla_norm", "conv_w", "conv_b", "w_q_ml", "w_k_ml", "w_v_ml",
            "w_if", "b_if", "ml_skip", "g_ml_norm", "w_pa", "w_pb", "w_o", "g_post_mix", "g_pre_mlp", "w_up", "w_down",
            "g_post_mlp")


_BLOCK_WEIGHTS = ("w_q_ml", "w_k_ml", "w_v_ml")


def _stored(name, a):
    if name in _BLOCK_WEIGHTS:
        return jnp.transpose(a, (0, 2, 3, 1)).reshape(16, 128)
    if name == "w_if":
        return jnp.transpose(a, (0, 2, 1)).reshape(8, 384)
    return a


def _unstored(name, a):
    if name in _BLOCK_WEIGHTS:
        return jnp.transpose(a.reshape(1, 4, 4, 128), (0, 3, 1, 2))
    if name == "w_if":
        return jnp.transpose(a.reshape(1, 8, 384), (0, 2, 1))
    return a


def _as_shard(name, a):
    return jnp.transpose(a, (2, 0, 1)).reshape(IN_SHARD, D_MODEL // 128, 128) if name == "w_in" else a[0]


def _from_shard(name, a):
    return jnp.transpose(a, (1, 2, 0)).reshape(1, D_MODEL, IN_SHARD) if name == "w_in" else a[None]


def kernel(x, g_pre_mix, w_in, w_a_up, b_a_up, g_gla_norm, conv_w, conv_b, w_q_ml, w_k_ml, w_v_ml, w_if, b_if, ml_skip, g_ml_norm, w_pa, w_pb, w_o, g_post_mix, g_pre_mlp, w_up, w_down, g_post_mlp, loss_target, m_g_pre_mix, m_w_in, m_w_a_up, m_b_a_up, m_g_gla_norm, m_conv_w, m_conv_b, m_w_q_ml, m_w_k_ml, m_w_v_ml, m_w_if, m_b_if, m_ml_skip, m_g_ml_norm, m_w_pa, m_w_pb, m_w_o, m_g_post_mix, m_g_pre_mlp, m_w_up, m_w_down, m_g_post_mlp, v_g_pre_mix, v_w_in, v_w_a_up, v_b_a_up, v_g_gla_norm, v_conv_w, v_conv_b, v_w_q_ml, v_w_k_ml, v_w_v_ml, v_w_if, v_b_if, v_ml_skip, v_g_ml_norm, v_w_pa, v_w_pb, v_w_o, v_g_post_mix, v_g_pre_mlp, v_w_up, v_w_down, v_g_post_mlp):
    args = dict(locals())
    wts = {n: _as_shard(n, args[n]) for n in _WEIGHTS}
    mom = {n: _as_shard(n, args["m_" + n]) for n in _WEIGHTS}
    var = {n: _as_shard(n, args["v_" + n]) for n in _WEIGHTS}
    chip = 2 * lax.axis_index("x") + lax.axis_index("y")

    first = ("w_in",) + _SMALL_SHARDED
    gathered = dict(zip(first, _run_alone(_Gather(("w_in",), [wts[n] for n in _SMALL_SHARDED]),
                                          [wts[n].reshape(IN_SHARD, D_MODEL).astype(BF16) if n == "w_in" else wts[n]
                                           for n in first],
                                          "gather_first")))
    sp = {n: wts[n] for n in _SMALL_REPL}
    sp["w_a_up"] = _cols(gathered["w_a_up"])
    sp["conv_w"] = _cols(gathered["conv_w"])
    sp["w_if"] = gathered["w_if"].reshape(1536, 8)

    dx, big, small = _local_step(x[0], loss_target[0], _full_weights({"w_in": gathered["w_in"]}), sp,
                                 late_shards=[wts[n].astype(BF16) for n in _LATE])

    small_shapes = [small[n].shape for n in _SMALL_ORDER]
    packed = _pack([small[n] for n in _SMALL_ORDER])
    sums, small_sum = _sum_swap(_BIG, [big[n] for n in _BIG], packed)

    grads, delta, new_m, new_v = {}, {}, {}, {}
    for n, g in zip(_BIG, sums):
        if n == "w_in":
            g, d, nm, nv = _adamw_rows(g, wts[n], mom[n], var[n], "adamw_" + n)
        else:
            d, nm, nv = _adamw_big(g, wts[n], mom[n], var[n], "adamw_" + n)
        grads[n], delta[n], new_m[n], new_v[n] = (_from_shard(n, a) for a in (g, d, nm, nv))
    summed = dict(zip(_SMALL_ORDER, _unpack(small_sum, small_shapes)))
    loss = summed["loss"].reshape(())
    summed["w_a_up"] = lax.dynamic_slice_in_dim(summed["w_a_up"], chip * 64, 64, axis=1)
    summed["conv_w"] = lax.dynamic_slice_in_dim(summed["conv_w"], chip * 128, 128, axis=1)
    summed["w_if"] = lax.dynamic_slice_in_dim(summed["w_if"], chip * 384, 384, axis=0)
    small_names = _SMALL_REPL + _SMALL_SHARDED
    g_stored = [_stored(n, summed[n].reshape(args[n].shape)) for n in small_names]
    upd = _adamw_small([_stored(n, args[n]) for n in small_names], g_stored,
                       [_stored(n, args["m_" + n]) for n in small_names], [_stored(n, args["v_" + n]) for n in small_names])
    for dst, arrs in zip((grads, delta, new_m, new_v), (g_stored,) + tuple(upd)):
        dst.update({n: _unstored(n, a) for n, a in zip(small_names, arrs)})

    outs = [loss, dx[None]]
    for group in (grads, delta, new_m, new_v):
        outs += [group[n] for n in _WEIGHTS]
    return tuple(outs)
```

```python
import functools

import jax
import jax.numpy as jnp
from jax import lax
from jax.experimental import pallas as pl
from jax.experimental.pallas import tpu as pltpu

F32 = jnp.float32
BF16 = jnp.bfloat16

SEQ = 2048
D_MODEL = 1024
CHUNK = 64
N_CHUNK = SEQ // CHUNK
HEADS = 4
GLA_DK = 64
GLA_DV = 128
ML_DH = 128
D_FF = 4096
EPS = 1e-6
N_CHIP = 4
N_DEV = 8
TOK_TILE = 256
N_TOK_TILE = SEQ // TOK_TILE
SWEEP = 2
assert CHUNK == 64
N_SWEEP = N_CHUNK // SWEEP

PM_W = 2688
PM_XM = 1536
PM_OP = 2048
PM_AL = 2560
GAB_W = 2048
D_IN = 4624
IN_SHARD = D_IN // N_CHIP
IN_ALOW = 1536
IN_XM = 1552
IN_GATES = 2576

ADAM_LR = 0.001
ADAM_B1 = 0.9
ADAM_B2 = 0.999
ADAM_EPS = 1e-08
ADAM_WD = 0.01
ADAM_STEP = 10

VMEM_LIMIT = 56 * 1024 * 1024


def _params(sem=None):
    return pltpu.CompilerParams(dimension_semantics=sem, vmem_limit_bytes=VMEM_LIMIT)


def _dot(a, b, ca, cb):
    return lax.dot_general(a.astype(BF16), b.astype(BF16), (((ca,), (cb,)), ((), ())), preferred_element_type=F32)


def _pmm_nn(a, b):
    return _dot(a, b, 1, 0)


def _pmm_nt(a, b):
    return _dot(a, b, 1, 1)


def _pmm_tn(a, b):
    return _dot(a, b, 0, 0)


def _pcmm(c, x):
    return lax.dot_general(c, x, (((1,), (0,)), ((), ())), precision=lax.Precision.HIGHEST, preferred_element_type=F32)


@jax.custom_vjp
def _mm_nn(a, b):
    return _dot(a, b, 1, 0)


@jax.custom_vjp
def _mm_nt(a, b):
    return _dot(a, b, 1, 1)


@jax.custom_vjp
def _mm_tn(a, b):
    return _dot(a, b, 0, 0)


_mm_nn.defvjp(lambda a, b: (_dot(a, b, 1, 0), (a, b)), lambda r, g: (_mm_nt(g, r[1]), _mm_tn(r[0], g)))
_mm_nt.defvjp(lambda a, b: (_dot(a, b, 1, 1), (a, b)), lambda r, g: (_mm_nn(g, r[1]), _mm_tn(g, r[0])))
_mm_tn.defvjp(lambda a, b: (_dot(a, b, 0, 0), (a, b)), lambda r, g: (_mm_nt(r[1], g), _mm_nn(r[0], g)))


@jax.custom_vjp
def _cmm(c, x):
    return _pcmm(c, x)


_cmm.defvjp(
    lambda c, x: (_pcmm(c, x), c),
    lambda c, g: (jnp.zeros_like(c), lax.dot_general(c, g, (((0,), (0,)), ((), ())), precision=lax.Precision.HIGHEST,
                                                      preferred_element_type=F32)),
)

_PLAIN_OPS = (_pmm_nn, _pmm_nt, _pmm_tn, _pcmm)
_VJP_OPS = (_mm_nn, _mm_nt, _mm_tn, _cmm)


def _sigmoid(x):
    return 0.5 * (jnp.tanh(0.5 * x) + 1.0)


def _log_sigmoid(x):
    return jnp.minimum(x, 0.0) - jnp.log(1.0 + jnp.exp(-jnp.abs(x)))


def _mean(x):
    return jnp.mean(x, axis=-1, keepdims=True)


def _nt(a, b):
    return lax.dot_general(a, b, (((1,), (1,)), ((), ())), preferred_element_type=F32)


def _tn(a, b):
    return lax.dot_general(a, b, (((0,), (0,)), ((), ())), preferred_element_type=F32)


def _mixer_chunk(ops, p, st, pm, xprev8):
    mm_nn, mm_nt, mm_tn, cmm = ops
    n_rows = pm.shape[0]
    n_ch = n_rows // CHUNK
    row = lax.broadcasted_iota(jnp.int32, (n_rows, n_rows), 0)
    col = lax.broadcasted_iota(jnp.int32, (n_rows, n_rows), 1)
    tri = jnp.logical_and((row >> 6) == (col >> 6), row >= col).astype(F32)
    causal = tri[0:CHUNK, 0:CHUNK] > 0.0
    q = pm[:, 0:256]
    k = pm[:, 256:512]
    v = pm[:, 512:1024]
    g = pm[:, 1024:1536]
    xm = pm[:, PM_XM:PM_XM + 512]
    opre = pm[:, PM_OP:PM_OP + 512]
    alow = pm[:, PM_AL:PM_AL + 128]
    hs = range(HEADS)
    cs = range(n_ch)
    pairs = [(i, h) for i in cs for h in hs]
    rs = [slice(i * CHUNK, (i + 1) * CHUNK) for i in cs]
    last = [slice((i + 1) * CHUNK - 1, (i + 1) * CHUNK) for i in cs]
    s6 = [slice(h * GLA_DK, (h + 1) * GLA_DK) for h in hs]
    s12 = [slice(h * 128, (h + 1) * 128) for h in hs]

    la = _log_sigmoid(mm_nn(alow, p["wau"]) + p["bau"]) * (1.0 / 16.0)
    cum = cmm(tri, la)
    cum_last = [cum[last[i], :] for i in cs]
    to_end = jnp.concatenate([cum_last[i] - cum[rs[i], :] for i in cs], axis=0)
    e_pos = jnp.exp(cum)
    e_neg = jnp.exp(-cum)
    qs = q * (GLA_DK ** -0.5)
    qp = qs * e_pos
    qn = qs * e_neg
    kp = k * e_pos
    kn = k * e_neg
    kl = k * jnp.exp(to_end)
    dec = [jnp.exp(cum_last[i]) for i in cs]
    a_fwd = {(i, h): mm_nt(qp[rs[i], s6[h]], kn[rs[i], s6[h]]) for i, h in pairs}
    a_bwd = {(i, h): mm_nt(qn[rs[i], s6[h]], kp[rs[i], s6[h]]) for i, h in pairs}
    s_chunk = {(i, h): mm_tn(v[rs[i], s12[h]], kl[rs[i], s6[h]]) for i, h in pairs}
    mem = {(0, h): st["S"][h] for h in hs}
    for i, h in pairs:
        mem[(i + 1, h)] = mem[(i, h)] * dec[i][:, s6[h]] + s_chunk[(i, h)]
    s_new = [mem[(n_ch, h)] for h in hs]
    o_inter = {(i, h): mm_nt(qp[rs[i], s6[h]], mem[(i, h)]) for i, h in pairs}
    scores = {ih: jnp.where(causal, a_fwd[ih], a_bwd[ih]) for ih in pairs}
    o = {(i, h): mm_nn(scores[(i, h)], v[rs[i], s12[h]]) + o_inter[(i, h)] for i, h in pairs}
    o = {ih: o[ih] * lax.rsqrt(_mean(o[ih] * o[ih]) + EPS) * p["ggla"] for ih in pairs}
    gate = g * _sigmoid(g)
    out_a = {(i, h): o[(i, h)] * gate[rs[i], s12[h]] for i, h in pairs}

    xx = jnp.concatenate([xprev8, xm], axis=0)
    pre = p["cb"]
    for j in range(4):
        pre = pre + p["cw"][j:j + 1, :] * xx[5 + j:5 + j + n_rows, :]
    xc = pre * _sigmoid(pre)
    qm = [mm_nn(xc[:, s12[h]], p["wq"][h]) for h in hs]
    km = [mm_nn(xc[:, s12[h]], p["wk"][h]) for h in hs]
    vm = [mm_nn(xm[:, s12[h]], p["wv"][h]) for h in hs]
    qcat = jnp.concatenate(qm, axis=1)
    kcat = jnp.concatenate(km, axis=1)
    vcat = jnp.concatenate(vm, axis=1)
    gates = (mm_nn(qcat, p["wif"][0:512]) + mm_nn(kcat, p["wif"][512:1024]) + mm_nn(vcat, p["wif"][1024:1536])
             + p["bif"])
    lf = _log_sigmoid(gates)
    fc = cmm(tri, lf)
    gates_t = gates.T
    fc_t = fc.T
    ks = [km[h] * (ML_DH ** -0.5) for h in hs]
    qk = {(i, h): mm_nt(qm[h][rs[i]], ks[h][rs[i]]) for i, h in pairs}
    li_c = {(i, h): gates[rs[i], h:h + 1] for i, h in pairs}
    fc_c = {(i, h): fc[rs[i], 4 + h:5 + h] for i, h in pairs}
    f_last = {(i, h): fc[last[i], 4 + h:5 + h] for i, h in pairs}
    a = {ih: f_last[ih] - fc_c[ih] + li_c[ih] for ih in pairs}
    m_loc = {ih: jnp.max(a[ih], axis=0, keepdims=True) for ih in pairs}
    kw = {(i, h): ks[h][rs[i]] * jnp.exp(a[(i, h)] - m_loc[(i, h)]) for i, h in pairs}
    c_chunk = {(i, h): mm_tn(kw[(i, h)], vm[h][rs[i]]) for i, h in pairs}
    c_in = {(0, h): st["C"][h] for h in hs}
    n_in = {(0, h): st["n"][h] for h in hs}
    m_in = {(0, h): st["m"][h][:, 0:1] for h in hs}
    for i, h in pairs:
        m_nx = jnp.maximum(f_last[(i, h)] + m_in[(i, h)], m_loc[(i, h)])
        sp = jnp.exp(f_last[(i, h)] + m_in[(i, h)] - m_nx)
        sl = jnp.exp(m_loc[(i, h)] - m_nx)
        c_in[(i + 1, h)] = sp * c_in[(i, h)] + sl * c_chunk[(i, h)]
        n_in[(i + 1, h)] = sp * n_in[(i, h)] + sl * jnp.sum(kw[(i, h)], axis=0, keepdims=True)
        m_in[(i + 1, h)] = m_nx
    q_c = {(i, h): mm_nn(qm[h][rs[i]], c_in[(i, h)]) for i, h in pairs}
    log_d = {(i, h): gates_t[h:h + 1, rs[i]] - jnp.abs(fc_c[(i, h)] - fc_t[4 + h:5 + h, rs[i]]) for i, h in pairs}
    g_int = {ih: fc_c[ih] + m_in[ih] for ih in pairs}
    m_t = {ih: jnp.maximum(g_int[ih], jnp.max(log_d[ih], axis=1, keepdims=True)) for ih in pairs}
    s = {ih: qk[ih] * jnp.exp(log_d[ih] - m_t[ih]) for ih in pairs}
    scl = {ih: jnp.exp(g_int[ih] - m_t[ih]) for ih in pairs}
    num = {(i, h): mm_nn(s[(i, h)], vm[h][rs[i]]) + scl[(i, h)] * q_c[(i, h)] for i, h in pairs}
    den = {(i, h): jnp.sum(s[(i, h)], axis=1, keepdims=True)
           + scl[(i, h)] * jnp.sum(qm[h][rs[i]] * n_in[(i, h)], axis=1, keepdims=True) for i, h in pairs}
    den = {ih: jnp.maximum(jnp.abs(den[ih]), jnp.exp(-m_t[ih])) for ih in pairs}
    open_gate = _sigmoid(opre)
    hc = {(i, h): num[(i, h)] / den[(i, h)] * open_gate[rs[i], s12[h]] for i, h in pairs}
    d0 = {ih: hc[ih] - _mean(hc[ih]) for ih in pairs}
    y = {ih: d0[ih] * lax.rsqrt(_mean(d0[ih] * d0[ih]) + EPS) for ih in pairs}
    skipped = p["skip"] * xc
    out_b = {(i, h): y[(i, h)] * p["gml"][:, s12[h]] + skipped[rs[i], s12[h]] for i, h in pairs}
    ab = jnp.concatenate([jnp.concatenate([out_a[(i, h)] for h in hs] + [out_b[(i, h)] for h in hs], axis=1) for i in cs],
                         axis=0)
    new = {"S": s_new, "C": [c_in[(n_ch, h)] for h in hs], "n": [n_in[(n_ch, h)] for h in hs],
           "m": [jnp.broadcast_to(m_in[(n_ch, h)], (1, ML_DH)) for h in hs]}
    return ab, new


_P_NAMES = ("wau", "bau", "ggla", "cw", "cb", "wq", "wk", "wv", "wif", "bif", "skip", "gml")
_P_SHAPES = {
    "wau": (128, 256), "bau": (1, 256), "ggla": (1, 128), "cw": (4, 512), "cb": (1, 512),
    "wq": (512, 128), "wk": (512, 128), "wv": (512, 128),
    "wif": (1536, 128), "bif": (1, 128), "skip": (1, 512), "gml": (1, 512),
}
_P_BLOCKDIAG = ("wq", "wk", "wv")
_S_NAMES = ("S", "C", "n", "m")
_S_SHAPES = {"S": (HEADS, GLA_DV, GLA_DK), "C": (HEADS, ML_DH, ML_DH), "n": (HEADS, 1, ML_DH), "m": (HEADS, 1, ML_DH)}


def _per_head(ref):
    return [ref[h] for h in range(HEADS)]


def _block_mask():
    r = lax.broadcasted_iota(jnp.int32, (128, 128), 0)
    c = lax.broadcasted_iota(jnp.int32, (128, 128), 1)
    same_block = (r >> 2) == (c >> 2)
    spread = jnp.logical_and(r < 4, (c & 3) == r)
    return same_block.astype(F32), spread.astype(F32)


def _expand_blockdiag(w_ref, dense_ref):
    same_block, spread = _block_mask()
    for h in range(HEADS):
        tiled = _pmm_nn(w_ref[h * 128:(h + 1) * 128, :], spread)
        dense_ref[h] = tiled * same_block


def _collect_blockdiag(ddense_ref, dw_ref):
    same_block, spread = _block_mask()
    for h in range(HEADS):
        dw_ref[h * 128:(h + 1) * 128, :] = lax.dot_general(
            ddense_ref[h] * same_block, spread, (((1,), (1,)), ((), ())), precision=lax.Precision.HIGHEST,
            preferred_element_type=F32)


def _const_spec(shape):
    zeros = (0,) * len(shape)
    return pl.BlockSpec(shape, lambda i: zeros)


def _split(refs, *counts):
    out, at = [], 0
    for c in counts:
        out.append(refs[at:at + c])
        at += c
    assert at == len(refs)
    return out


def _ride(rider, phases, cond, ins, outs, sems):
    if rider is None:
        return
    lands, (send_sems, recv_sems, flush_sems) = sems[:-3], sems[-3:]

    @pl.when(cond)
    def _():
        for phase in phases:
            getattr(rider, phase)(ins, lands, send_sems, recv_sems)
        if "last" in phases:
            flush = [pltpu.make_async_copy(lands[k], outs[k], flush_sems.at[k]) for k in range(len(outs))]
            for cp in flush:
                cp.start()
            for cp in flush:
                cp.wait()


def _middle_step(rider, n_steps):
    return min(n_steps - 2, int(getattr(rider, "middle_at", 1.0) * n_steps))


def _rider_specs(rider, rider_ins):
    if rider is None:
        return [], [], [], []
    scratch = [pltpu.VMEM(s.shape, s.dtype) for s in list(rider.out_shape) + list(getattr(rider, "work_shape", ()))]
    scratch += [pltpu.SemaphoreType.DMA((rider.n_sems,)), pltpu.SemaphoreType.DMA((rider.n_sems,)),
                pltpu.SemaphoreType.DMA((len(rider.out_shape),))]
    in_space = getattr(rider, "in_space", VMEM_WHOLE)
    return [in_space] * len(rider_ins), [ANY] * len(rider.out_shape), list(rider.out_shape), scratch


def _mixer_fwd(pm, p, rider=None, rider_ins=()):
    n_p = len(_P_NAMES)
    r_in, r_out_specs, r_out_shape, r_sems = _rider_specs(rider, rider_ins)

    def body(*refs):
        (pm_ref, xprev_ref), p_list, ride_in, (ab_ref,), so_refs, ride_out, sc_refs, dense_list, sems = _split(
            refs, 2, n_p, len(r_in), 1, 4, len(r_out_specs), 4, 3, len(r_sems))
        p_refs = dict(zip(_P_NAMES, p_list))
        dense = dict(zip(_P_BLOCKDIAG, dense_list))
        n = pl.program_id(0)
        _ride(rider, ("first",), n == 0, ride_in, ride_out, sems)

        @pl.when(n == 0)
        def _():
            for r in sc_refs:
                r[...] = jnp.zeros_like(r)
            for nm in _P_BLOCKDIAG:
                _expand_blockdiag(p_refs[nm], dense[nm])

        st = {name: _per_head(r) for name, r in zip(_S_NAMES, sc_refs)}
        pv = {nm: (_per_head(dense[nm]) if nm in _P_BLOCKDIAG else p_refs[nm][...]) for nm in _P_NAMES}
        for name, r in zip(_S_NAMES, so_refs):
            for h in range(HEADS):
                r[0, h] = st[name][h]
        xprev8 = jnp.where(n > 0, xprev_ref[CHUNK - 8:CHUNK, :], 0.0)
        ab, st = _mixer_chunk(_PLAIN_OPS, pv, st, pm_ref[...], xprev8)
        ab_ref[...] = ab.astype(BF16)
        for name, r in zip(_S_NAMES, sc_refs):
            for h in range(HEADS):
                r[h] = st[name][h]
        _ride(rider, ("middle",), n == _middle_step(rider, N_SWEEP), ride_in, ride_out, sems)
        _ride(rider, ("last",), n == N_SWEEP - 1, ride_in, ride_out, sems)

    in_specs = [pl.BlockSpec((SWEEP * CHUNK, PM_W), lambda i: (i, 0)),
                pl.BlockSpec((CHUNK, 512), lambda i: (jnp.maximum(SWEEP * i - 1, 0), PM_XM // 512))]
    in_specs += [_const_spec(_P_SHAPES[nm]) for nm in _P_NAMES] + r_in
    out_specs = [pl.BlockSpec((SWEEP * CHUNK, 1024), lambda i: (i, 0))]
    out_shape = [jax.ShapeDtypeStruct((SEQ, 1024), BF16)]
    for nm in _S_NAMES:
        shp = _S_SHAPES[nm]
        out_specs.append(pl.BlockSpec((1,) + shp, lambda i: (i, 0, 0, 0)))
        out_shape.append(jax.ShapeDtypeStruct((N_SWEEP,) + shp, F32))
    return pl.pallas_call(
        body, grid=(N_SWEEP,), in_specs=in_specs, out_specs=out_specs + r_out_specs, out_shape=out_shape + r_out_shape,
        scratch_shapes=[pltpu.VMEM(_S_SHAPES[nm], F32) for nm in _S_NAMES]
        + [pltpu.VMEM((HEADS, 128, 128), F32) for _ in _P_BLOCKDIAG] + r_sems,
        compiler_params=_params(("arbitrary",)), name="mixer_fwd",
    )(pm, pm, *[p[nm] for nm in _P_NAMES], *rider_ins)


def _mixer_bwd(pm, dab, states, p, rider=None, rider_ins=()):
    n_p = len(_P_NAMES)
    r_in, r_out_specs, r_out_shape, r_sems = _rider_specs(rider, rider_ins)

    def body(*refs):
        ((pm_ref, xprev_ref, dab_ref), si_refs, p_list, ride_in, (dpm_ref,), dp_list, ride_out, ds_refs, (carry_ref,),
         dense_list, ddense_list, sems) = _split(refs, 3, 4, n_p, len(r_in), 1, n_p, len(r_out_specs), 4, 1, 3, 3, len(r_sems))
        p_refs = dict(zip(_P_NAMES, p_list))
        dp_refs = dict(zip(_P_NAMES, dp_list))
        dense = dict(zip(_P_BLOCKDIAG, dense_list))
        ddense = dict(zip(_P_BLOCKDIAG, ddense_list))
        i = pl.program_id(0)
        blk = N_SWEEP - 1 - i
        _ride(rider, ("first",), i == 0, ride_in, ride_out, sems)

        @pl.when(i == 0)
        def _():
            for r in ds_refs:
                r[...] = jnp.zeros_like(r)
            for nm in _P_NAMES:
                if nm in _P_BLOCKDIAG:
                    ddense[nm][...] = jnp.zeros_like(ddense[nm])
                    _expand_blockdiag(p_refs[nm], dense[nm])
                else:
                    dp_refs[nm][...] = jnp.zeros_like(dp_refs[nm])
            carry_ref[...] = jnp.zeros_like(carry_ref)

        pv = {nm: (_per_head(dense[nm]) if nm in _P_BLOCKDIAG else p_refs[nm][...]) for nm in _P_NAMES}
        dst = {name: _per_head(r) for name, r in zip(_S_NAMES, ds_refs)}
        st = {name: [r[0, h] for h in range(HEADS)] for name, r in zip(_S_NAMES, si_refs)}
        xprev8 = jnp.where(blk > 0, xprev_ref[CHUNK - 8:CHUNK, :], 0.0)
        _, vjp = jax.vjp(functools.partial(_mixer_chunk, _VJP_OPS), pv, st, pm_ref[...], xprev8)
        dp_sum, dst, dpm, dxprev8 = vjp((dab_ref[...], dst))
        reach = jnp.concatenate([jnp.zeros((SWEEP * CHUNK - 8, 512), F32), carry_ref[...]], axis=0)
        dpm_ref[:, 0:PM_XM] = dpm[:, 0:PM_XM].astype(BF16)
        dpm_ref[:, PM_XM:PM_XM + 512] = (dpm[:, PM_XM:PM_XM + 512] + reach).astype(BF16)
        dpm_ref[:, PM_XM + 512:PM_W] = dpm[:, PM_XM + 512:PM_W].astype(BF16)
        carry_ref[...] = dxprev8
        for name, r in zip(_S_NAMES, ds_refs):
            for h in range(HEADS):
                r[h] = dst[name][h]
        for nm in _P_NAMES:
            if nm in _P_BLOCKDIAG:
                for h in range(HEADS):
                    ddense[nm][h] += dp_sum[nm][h]
            else:
                dp_refs[nm][...] += dp_sum[nm]

        @pl.when(i == N_SWEEP - 1)
        def _():
            for nm in _P_BLOCKDIAG:
                _collect_blockdiag(ddense[nm], dp_refs[nm])

        _ride(rider, ("middle",), i == _middle_step(rider, N_SWEEP), ride_in, ride_out, sems)
        _ride(rider, ("last",), i == N_SWEEP - 1, ride_in, ride_out, sems)

    rev = lambda i: (N_SWEEP - 1 - i, 0)
    in_specs = [pl.BlockSpec((SWEEP * CHUNK, PM_W), rev),
                pl.BlockSpec((CHUNK, 512), lambda i: (jnp.maximum(SWEEP * (N_SWEEP - 1 - i) - 1, 0), PM_XM // 512)),
                pl.BlockSpec((SWEEP * CHUNK, 1024), rev)]
    for nm in _S_NAMES:
        in_specs.append(pl.BlockSpec((1,) + _S_SHAPES[nm], lambda i: (N_SWEEP - 1 - i, 0, 0, 0)))
    in_specs += [_const_spec(_P_SHAPES[nm]) for nm in _P_NAMES] + r_in
    out_specs = [pl.BlockSpec((SWEEP * CHUNK, PM_W), rev)] + [_const_spec(_P_SHAPES[nm]) for nm in _P_NAMES]
    out_shape = [jax.ShapeDtypeStruct((SEQ, PM_W), BF16)] + [jax.ShapeDtypeStruct(_P_SHAPES[nm], F32) for nm in _P_NAMES]
    res = pl.pallas_call(
        body, grid=(N_SWEEP,), in_specs=in_specs, out_specs=out_specs + r_out_specs, out_shape=out_shape + r_out_shape,
        scratch_shapes=[pltpu.VMEM(_S_SHAPES[nm], F32) for nm in _S_NAMES] + [pltpu.VMEM((8, 512), F32)]
        + [pltpu.VMEM((HEADS, 128, 128), F32) for _ in range(2 * len(_P_BLOCKDIAG))] + r_sems,
        compiler_params=_params(("arbitrary",)), name="mixer_bwd",
    )(pm, pm, dab, *states, *[p[nm] for nm in _P_NAMES], *rider_ins)
    return res[0], dict(zip(_P_NAMES, res[1:1 + n_p])), res[1 + n_p:]


def _tok(width):
    return pl.BlockSpec((TOK_TILE, width), lambda i: (i, 0))


def _once(shape):
    zeros = (0,) * len(shape)
    return pl.BlockSpec(shape, lambda i: zeros, pipeline_mode=pl.Buffered(1))


def _rms_fwd(x):
    r = lax.rsqrt(_mean(x * x) + EPS)
    return x * r, r


def _rms_bwd(dy, xn, r, g):
    gd = dy * g
    return r * (gd - xn * _mean(xn * gd))


def _tiled_call(body, in_specs, out_specs, out_shape, args, name, rider=None, rider_ins=()):
    r_in, r_out_specs, r_out_shape, r_scratch = _rider_specs(rider, rider_ins)
    n_in, n_out = len(in_specs), len(out_specs)

    def hosted(*refs):
        ins, ride_in, outs, ride_out, scratch = _split(refs, n_in, len(r_in), n_out, len(r_out_specs), len(r_scratch))
        i = pl.program_id(0)
        _ride(rider, ("first",), i == 0, ride_in, ride_out, scratch)
        body(*ins, *outs)
        _ride(rider, ("middle",), i == _middle_step(rider, N_TOK_TILE), ride_in, ride_out, scratch)
        _ride(rider, ("last",), i == N_TOK_TILE - 1, ride_in, ride_out, scratch)

    res = pl.pallas_call(
        hosted, grid=(N_TOK_TILE,), in_specs=list(in_specs) + r_in, out_specs=list(out_specs) + r_out_specs,
        out_shape=list(out_shape) + r_out_shape, scratch_shapes=r_scratch,
        compiler_params=_params(("arbitrary",)), name=name,
    )(*args, *rider_ins)
    return res[:n_out], res[n_out:]


def _in_proj(x, g_pre, wt_in, rider=None, rider_ins=()):
    def body(x_ref, g_ref, wt_ref, pm_ref, gab_ref, h_ref):
        xn, _ = _rms_fwd(x_ref[...])
        h = (xn * g_ref[...]).astype(BF16)
        h_ref[...] = h
        pm_ref[:, 0:PM_XM] = _nt(h, wt_ref[0:IN_ALOW, :])
        pm_ref[:, PM_XM:PM_AL] = _nt(h, wt_ref[IN_XM:IN_GATES, :])
        pm_ref[:, PM_AL:PM_W] = _nt(h, wt_ref[IN_ALOW:IN_ALOW + 128, :])
        gab_ref[...] = _nt(h, wt_ref[IN_GATES:D_IN, :])

    return _tiled_call(
        body, [_tok(D_MODEL), _once((1, D_MODEL)), _once((D_IN, D_MODEL))], [_tok(PM_W), _tok(GAB_W), _tok(D_MODEL)],
        [jax.ShapeDtypeStruct((SEQ, PM_W), F32), jax.ShapeDtypeStruct((SEQ, GAB_W), F32),
         jax.ShapeDtypeStruct((SEQ, D_MODEL), BF16)], (x, g_pre, wt_in), "in_proj", rider, rider_ins)


def _merge_fwd(ab, gab, x, w_pa4, w_pb4, w_o, g_post, rider=None, rider_ins=()):
    def body(ab_ref, gab_ref, x_ref, wpa_ref, wpb_ref, wo_ref, g_ref, x1_ref, mix_ref, mg_ref):
        a = ab_ref[:, 0:512]
        b = ab_ref[:, 512:1024]
        for j in range(N_CHIP):
            blk = slice(j * 256, (j + 1) * 256)
            ya = jnp.dot(a, wpa_ref[j], preferred_element_type=F32)
            yb = jnp.dot(b, wpb_ref[j], preferred_element_type=F32)
            sa = _sigmoid(gab_ref[:, j * 256:(j + 1) * 256])
            sb = _sigmoid(gab_ref[:, 1024 + j * 256:1024 + (j + 1) * 256])
            mg_ref[:, blk] = (sa * ya + sb * yb).astype(BF16)
        mix = jnp.dot(mg_ref[...], wo_ref[...], preferred_element_type=F32)
        mix_ref[...] = mix
        mn, _ = _rms_fwd(mix)
        x1_ref[...] = x_ref[...] + mn * g_ref[...]

    return _tiled_call(
        body, [_tok(1024), _tok(GAB_W), _tok(D_MODEL), _once((N_CHIP, 512, 256)), _once((N_CHIP, 512, 256)),
               _once((D_MODEL, D_MODEL)), _once((1, D_MODEL))], [_tok(D_MODEL), _tok(D_MODEL), _tok(D_MODEL)],
        [jax.ShapeDtypeStruct((SEQ, D_MODEL), F32), jax.ShapeDtypeStruct((SEQ, D_MODEL), F32),
         jax.ShapeDtypeStruct((SEQ, D_MODEL), BF16)], (ab, gab, x, w_pa4, w_pb4, w_o, g_post), "merge_fwd", rider, rider_ins)


def _mlp(x1, target, g_pre, g_post, w_up4, w_down_a4, w_down_b4):
    def body(x1_ref, t_ref, gpre_ref, gpost_ref, wup_ref, wda_ref, wdb_ref,
             dx1_ref, u_ref, dd_ref, h2_ref, dpre_ref, dgpost_ref, dgpre_ref, loss_ref):
        @pl.when(pl.program_id(0) == 0)
        def _():
            dgpost_ref[...] = jnp.zeros_like(dgpost_ref)
            dgpre_ref[...] = jnp.zeros_like(dgpre_ref)
            loss_ref[...] = jnp.zeros_like(loss_ref)

        x1 = x1_ref[...]
        gpre = gpre_ref[...]
        gpost = gpost_ref[...]
        xn2, r2 = _rms_fwd(x1)
        h2 = (xn2 * gpre).astype(BF16)
        h2_ref[...] = h2
        rl = []
        d = jnp.zeros((TOK_TILE, D_MODEL), F32)
        for j in range(N_CHIP):
            blk = slice(j * 1024, (j + 1) * 1024)
            r = jnp.maximum(jnp.dot(h2, wup_ref[j], preferred_element_type=F32), 0.0)
            rl.append(r)
            u = (r * r).astype(BF16)
            u_ref[:, blk] = u
            d = d + jnp.dot(u[:, 0:512], wda_ref[j], preferred_element_type=F32)
            d = d + jnp.dot(u[:, 512:1024], wdb_ref[j], preferred_element_type=F32)
        dn, r3 = _rms_fwd(d)
        diff = x1 + dn * gpost - t_ref[...]
        loss_ref[...] += jnp.sum(diff * diff, keepdims=True) * (0.5 / D_MODEL)
        dy = diff * (1.0 / D_MODEL)
        dgpost_ref[...] += jnp.sum(dy * dn, axis=0, keepdims=True)
        dd = _rms_bwd(dy, dn, r3, gpost).astype(BF16)
        dd_ref[...] = dd
        dh2 = jnp.zeros((TOK_TILE, D_MODEL), F32)
        for j in range(N_CHIP):
            blk = slice(j * 1024, (j + 1) * 1024)
            du = jnp.concatenate([_nt(dd, wda_ref[j]), _nt(dd, wdb_ref[j])], axis=1)
            dpre = (du * (2.0 * rl[j])).astype(BF16)
            dpre_ref[:, blk] = dpre
            dh2 = dh2 + _nt(dpre, wup_ref[j])
        dgpre_ref[...] += jnp.sum(dh2 * xn2, axis=0, keepdims=True)
        dx1_ref[...] = dy + _rms_bwd(dh2, xn2, r2, gpre)

    acc = pl.BlockSpec((1, D_MODEL), lambda i: (0, 0))
    return pl.pallas_call(
        body, grid=(N_TOK_TILE,),
        in_specs=[_tok(D_MODEL), _tok(D_MODEL), _once((1, D_MODEL)), _once((1, D_MODEL)),
                  _once((N_CHIP, D_MODEL, 1024)), _once((N_CHIP, 512, D_MODEL)), _once((N_CHIP, 512, D_MODEL))],
        out_specs=[_tok(D_MODEL), _tok(D_FF), _tok(D_MODEL), _tok(D_MODEL), _tok(D_FF), acc, acc,
                   pl.BlockSpec((1, 128), lambda i: (0, 0))],
        out_shape=[jax.ShapeDtypeStruct((SEQ, D_MODEL), F32), jax.ShapeDtypeStruct((SEQ, D_FF), BF16),
                   jax.ShapeDtypeStruct((SEQ, D_MODEL), BF16), jax.ShapeDtypeStruct((SEQ, D_MODEL), BF16),
                   jax.ShapeDtypeStruct((SEQ, D_FF), BF16), jax.ShapeDtypeStruct((1, D_MODEL), F32),
                   jax.ShapeDtypeStruct((1, D_MODEL), F32), jax.ShapeDtypeStruct((1, 128), F32)],
        compiler_params=_params(("arbitrary",)), name="mlp_fwd_bwd",
    )(x1, target, g_pre, g_post, w_up4, w_down_a4, w_down_b4)


def _merge_bwd(dx1, mix, ab, gab, w_pa4, w_pb4, w_o, g_post):
    def body(dx1_ref, mix_ref, ab_ref, gab_ref, wpa_ref, wpb_ref, wo_ref, g_ref,
             dmix_ref, dya_ref, dyb_ref, dgab_ref, dab_ref, dg_ref):
        @pl.when(pl.program_id(0) == 0)
        def _():
            dg_ref[...] = jnp.zeros_like(dg_ref)

        dx1 = dx1_ref[...]
        mn, r = _rms_fwd(mix_ref[...])
        dg_ref[...] += jnp.sum(dx1 * mn, axis=0, keepdims=True)
        dmix = _rms_bwd(dx1, mn, r, g_ref[...]).astype(BF16)
        dmix_ref[...] = dmix
        dmerged = _nt(dmix, wo_ref[...])
        a = ab_ref[:, 0:512]
        b = ab_ref[:, 512:1024]
        da = jnp.zeros((TOK_TILE, 512), F32)
        db = jnp.zeros((TOK_TILE, 512), F32)
        for j in range(N_CHIP):
            blk = slice(j * 256, (j + 1) * 256)
            blk_b = slice(1024 + j * 256, 1024 + (j + 1) * 256)
            dm = dmerged[:, blk]
            ya = jnp.dot(a, wpa_ref[j], preferred_element_type=F32)
            yb = jnp.dot(b, wpb_ref[j], preferred_element_type=F32)
            sa = _sigmoid(gab_ref[:, blk])
            sb = _sigmoid(gab_ref[:, blk_b])
            dya = (dm * sa).astype(BF16)
            dyb = (dm * sb).astype(BF16)
            dya_ref[:, blk] = dya
            dyb_ref[:, blk] = dyb
            dgab_ref[:, blk] = (dm * ya * sa * (1.0 - sa)).astype(BF16)
            dgab_ref[:, blk_b] = (dm * yb * sb * (1.0 - sb)).astype(BF16)
            da = da + _nt(dya, wpa_ref[j])
            db = db + _nt(dyb, wpb_ref[j])
        dab_ref[:, 0:512] = da
        dab_ref[:, 512:1024] = db

    return pl.pallas_call(
        body, grid=(N_TOK_TILE,),
        in_specs=[_tok(D_MODEL), _tok(D_MODEL), _tok(1024), _tok(GAB_W), _once((N_CHIP, 512, 256)),
                  _once((N_CHIP, 512, 256)), _once((D_MODEL, D_MODEL)), _once((1, D_MODEL))],
        out_specs=[_tok(D_MODEL), _tok(D_MODEL), _tok(D_MODEL), _tok(GAB_W), _tok(1024),
                   pl.BlockSpec((1, D_MODEL), lambda i: (0, 0))],
        out_shape=[jax.ShapeDtypeStruct((SEQ, D_MODEL), BF16), jax.ShapeDtypeStruct((SEQ, D_MODEL), BF16),
                   jax.ShapeDtypeStruct((SEQ, D_MODEL), BF16), jax.ShapeDtypeStruct((SEQ, GAB_W), BF16),
                   jax.ShapeDtypeStruct((SEQ, 1024), F32), jax.ShapeDtypeStruct((1, D_MODEL), F32)],
        compiler_params=_params(("arbitrary",)), name="merge_bwd",
    )(dx1, mix, ab, gab, w_pa4, w_pb4, w_o, g_post)


def _in_proj_bwd(dpm, dgab, x, dx1, g_pre, wt_in, rider=None, rider_ins=()):
    def body(dpm_ref, dgab_ref, x_ref, dx1_ref, g_ref, wt_ref, dx_ref, dg_ref):
        @pl.when(pl.program_id(0) == 0)
        def _():
            dg_ref[...] = jnp.zeros_like(dg_ref)

        dh = jnp.dot(dpm_ref[:, 0:PM_XM], wt_ref[0:IN_ALOW, :], preferred_element_type=F32)
        dh = dh + jnp.dot(dpm_ref[:, PM_XM:PM_AL], wt_ref[IN_XM:IN_GATES, :], preferred_element_type=F32)
        dh = dh + jnp.dot(dpm_ref[:, PM_AL:PM_W], wt_ref[IN_ALOW:IN_ALOW + 128, :], preferred_element_type=F32)
        dh = dh + jnp.dot(dgab_ref[...], wt_ref[IN_GATES:D_IN, :], preferred_element_type=F32)
        xn, r = _rms_fwd(x_ref[...])
        dg_ref[...] += jnp.sum(dh * xn, axis=0, keepdims=True)
        dx_ref[...] = dx1_ref[...] + _rms_bwd(dh, xn, r, g_ref[...])

    return _tiled_call(
        body, [_tok(PM_W), _tok(GAB_W), _tok(D_MODEL), _tok(D_MODEL), _once((1, D_MODEL)), _once((D_IN, D_MODEL))],
        [_tok(D_MODEL), pl.BlockSpec((1, D_MODEL), lambda i: (0, 0))],
        [jax.ShapeDtypeStruct((SEQ, D_MODEL), F32), jax.ShapeDtypeStruct((1, D_MODEL), F32)],
        (dpm, dgab, x, dx1, g_pre, wt_in), "in_proj_bwd", rider, rider_ins)


def _dw_in(dpm, dgab, h):
    n_pm = PM_AL // 512
    n_blk = n_pm + GAB_W // 512

    def body(dpm_ref, dgab_ref, dal_ref, h_ref, o_ref):
        i = pl.program_id(0)
        off = pl.multiple_of(i * 512 + 16 * (i >= 3).astype(jnp.int32), 16)

        @pl.when(i < n_pm)
        def _():
            o_ref[pl.ds(off, 512), :] = _tn(dpm_ref[...], h_ref[...]).astype(BF16)

        @pl.when(i >= n_pm)
        def _():
            o_ref[pl.ds(off, 512), :] = _tn(dgab_ref[...], h_ref[...]).astype(BF16)

        @pl.when(i == 0)
        def _():
            o_ref[IN_ALOW:IN_XM, :] = _tn(dal_ref[...], h_ref[...])[0:IN_XM - IN_ALOW].astype(BF16)

    return pl.pallas_call(
        body, grid=(n_blk,),
        in_specs=[pl.BlockSpec((SEQ, 512), lambda i: (0, jnp.minimum(i, n_pm - 1))),
                  pl.BlockSpec((SEQ, 512), lambda i: (0, jnp.maximum(i - n_pm, 0))),
                  pl.BlockSpec((SEQ, 128), lambda i: (0, PM_AL // 128)),
                  _once((SEQ, D_MODEL))],
        out_specs=pl.BlockSpec((D_IN, D_MODEL), lambda i: (0, 0)),
        out_shape=jax.ShapeDtypeStruct((D_IN, D_MODEL), BF16),
        compiler_params=_params(("arbitrary",)), name="dw_in",
    )(dpm, dgab, dpm, h)


def _tn_matmul(a, b, name, shards=1, tm=512, rider=None, rider_ins=()):
    m, n = a.shape[1], b.shape[1]
    tm = min(tm, m)
    tn = n // shards if shards > 1 else min(n, 1024)
    steps_i, steps_j = m // tm, n // tn
    r_in, r_out_specs, r_out_shape, r_scratch = _rider_specs(rider, rider_ins)

    def body(*refs):
        (a_ref, b_ref), ride_in, (o_ref,), ride_out, scratch = _split(refs, 2, len(r_in), 1, len(r_out_specs), len(r_scratch))
        step = pl.program_id(0) * steps_j + pl.program_id(1)
        _ride(rider, ("first",), step == 0, ride_in, ride_out, scratch)
        o_ref[...] = _tn(a_ref[...], b_ref[...]).astype(BF16)
        _ride(rider, ("middle", "last"), step == steps_i * steps_j - 1, ride_in, ride_out, scratch)

    if shards > 1:
        out_spec = pl.BlockSpec((None, tm, tn), lambda i, j: (j, i, 0))
        out_shape = jax.ShapeDtypeStruct((shards, m, tn), BF16)
    else:
        out_spec = pl.BlockSpec((tm, tn), lambda i, j: (i, j))
        out_shape = jax.ShapeDtypeStruct((m, n), BF16)
    res = pl.pallas_call(
        body, grid=(steps_i, steps_j),
        in_specs=[pl.BlockSpec((SEQ, tm), lambda i, j: (0, i)), pl.BlockSpec((SEQ, tn), lambda i, j: (0, j))] + r_in,
        out_specs=[out_spec] + r_out_specs, out_shape=[out_shape] + r_out_shape, scratch_shapes=r_scratch,
        compiler_params=_params(("arbitrary", "arbitrary")), name=name,
    )(a, b, *rider_ins)
    return res[0] if rider is None else (res[0], res[1:])


MESH = pl.DeviceIdType.MESH
ANY = pl.BlockSpec(memory_space=pl.ANY)
VMEM_WHOLE = pl.BlockSpec(memory_space=pltpu.VMEM)

_BIG = ("w_in", "w_pa", "w_pb", "w_o", "w_up", "w_down")
_BIG_SHARD = {"w_in": (IN_SHARD, D_MODEL), "w_pa": (512, 256), "w_pb": (512, 256), "w_o": (256, D_MODEL),
              "w_up": (D_MODEL, 1024), "w_down": (1024, D_MODEL),
              "w_down_a": (512, D_MODEL), "w_down_b": (512, D_MODEL)}
_BIG_SPLIT = {"w_in": 1, "w_pa": 0, "w_pb": 0, "w_o": 0, "w_up": 0, "w_down": 0, "w_down_a": 0, "w_down_b": 0}


def _half(ref, e, name, lead=0, part=None):
    axis = _BIG_SPLIT[name]
    size = _BIG_SHARD[name][axis] // 2
    start = e * size
    if part is not None:
        size //= 2
        start = start + part * size
    start = pl.multiple_of(start, 128 if axis == 1 else 16)
    idx = [pl.ds(0, ref.shape[a]) for a in range(lead)]
    idx += [pl.ds(start, size), pl.ds(0, _BIG_SHARD[name][1])] if axis == 0 else [pl.ds(0, _BIG_SHARD[name][0]), pl.ds(start, size)]
    return ref.at[tuple(idx)]


def _half_shape(name):
    r, c = _BIG_SHARD[name]
    return (r // 2, c) if _BIG_SPLIT[name] == 0 else (r, c // 2)


def _remote(src, dst, send_sems, recv_sems, k, to):
    return pltpu.make_async_remote_copy(src_ref=src, dst_ref=dst, send_sem=send_sems.at[k], recv_sem=recv_sems.at[k],
                                        device_id=to, device_id_type=MESH)


def _mesh_place():
    x, y, c = lax.axis_index("x"), lax.axis_index("y"), lax.axis_index("c")
    return x, y, c, [(1 - x, y), (x, 1 - y), (1 - x, 1 - y)]


class _Gather:
    def __init__(self, names, small=(), middle_at=0.5):
        self.middle_at = middle_at
        self.names = tuple(names)
        self.nb = len(self.names)
        self.n = self.nb + len(small)
        self.n_sems = 8 * self.nb + 3 * len(small)
        self.out_shape = [jax.ShapeDtypeStruct((N_CHIP,) + _BIG_SHARD[nm], BF16) for nm in self.names]
        self.out_shape += [jax.ShapeDtypeStruct((N_CHIP,) + s.shape, s.dtype) for s in small]

    def _copies(self, ins, outs, ss, rs, k):
        x, y, c, _ = _mesh_place()
        name = self.names[k]
        me, xn, yn, dg = 2 * x + y, 2 * (1 - x) + y, 2 * x + (1 - y), 2 * (1 - x) + (1 - y)
        to_x, to_y, sibling = (1 - x, y, c), (x, 1 - y, c), (x, y, 1 - c)

        def region(slot, e, part=None):
            return _half(outs[k].at[slot], e, name, part=part)

        def copy(pair, src, dst, to):
            return _remote(src, dst, ss, rs, 8 * k + pair, to)

        mine = _half(ins[k], c, name)
        sent = [copy(0, mine, region(me, c), to_x), copy(1, mine, region(me, c), to_y),
                copy(2, region(xn, c, 0), region(xn, c, 0), to_y), copy(3, region(yn, c, 1), region(yn, c, 1), to_x),
                copy(4, region(xn, c), region(xn, c), sibling), copy(5, region(yn, c), region(yn, c), sibling),
                copy(6, region(dg, c, 0), region(dg, c, 0), sibling), copy(7, region(dg, c, 1), region(dg, c, 1), sibling)]
        landing = [region(xn, c), region(yn, c), region(dg, c, 0), region(dg, c, 1),
                   region(xn, 1 - c), region(yn, 1 - c), region(dg, 1 - c, 0), region(dg, 1 - c, 1)]
        received = [copy(pair, dst, dst, sibling) for pair, dst in enumerate(landing)]
        return sent, received

    def _small(self, ins, outs, ss, rs, k, j, peer, slot, c):
        return _remote(ins[k], outs[k].at[slot], ss, rs, 8 * self.nb + 3 * (k - self.nb) + j, (*peer, c))

    def first(self, ins, outs, ss, rs):
        x, y, c, peers = _mesh_place()
        me = 2 * x + y
        for k in range(self.nb):
            sent, _ = self._copies(ins, outs, ss, rs, k)
            sent[0].start()
            sent[1].start()
        for k in range(self.nb, self.n):
            for j, peer in enumerate(peers):
                self._small(ins, outs, ss, rs, k, j, peer, me, c).start()
        for k in range(self.n):
            outs[k][me] = ins[k][...]

    def middle(self, ins, outs, ss, rs):
        for k in range(self.nb):
            sent, received = self._copies(ins, outs, ss, rs, k)
            for pair in (0, 1):
                received[pair].wait_recv()
                sent[2 + pair].start()
                sent[4 + pair].start()

    def last(self, ins, outs, ss, rs):
        x, y, c, peers = _mesh_place()
        for k in range(self.nb):
            sent, received = self._copies(ins, outs, ss, rs, k)
            for pair in (2, 3):
                received[pair].wait_recv()
                sent[4 + pair].start()
        for k in range(self.nb):
            sent, received = self._copies(ins, outs, ss, rs, k)
            for pair in range(4, 8):
                received[pair].wait_recv()
            for cp in sent:
                cp.wait_send()
        for k in range(self.nb, self.n):
            for j, (px, py) in enumerate(peers):
                self._small(ins, outs, ss, rs, k, j, (px, py), 2 * px + py, c).wait_recv()
                self._small(ins, outs, ss, rs, k, j, (px, py), 2 * x + y, c).wait_send()


def _run_alone(rider, ins, name):
    def body(*refs):
        r_in, r_out, sems = _split(refs, len(ins), len(rider.out_shape), 2)
        rider.first(r_in, r_out, *sems)
        rider.middle(r_in, r_out, *sems)
        rider.last(r_in, r_out, *sems)

    return pl.pallas_call(
        body, in_specs=[VMEM_WHOLE] * len(ins), out_specs=[VMEM_WHOLE] * len(rider.out_shape), out_shape=rider.out_shape,
        scratch_shapes=[pltpu.SemaphoreType.DMA((rider.n_sems,)), pltpu.SemaphoreType.DMA((rider.n_sems,))],
        compiler_params=_params(), name=name,
    )(*ins)


class _Presum:
    in_space = ANY

    def __init__(self, names):
        self.names = tuple(names)
        self.n = len(self.names)
        self.n_sems = 3 * self.n
        self.out_shape = [jax.ShapeDtypeStruct((N_CHIP,) + _half_shape(nm), BF16) for nm in self.names]
        self.work_shape = self.out_shape + self.out_shape

    def _stage(self, ins, bufs, ss, k, e, which):
        n = self.n
        return pltpu.make_async_copy(_half(ins[k], e, self.names[k], lead=1), bufs[which * n + k], ss.at[which * n + k])

    def _give(self, bufs, ss, rs, k, sibling):
        return _remote(bufs[self.n + k], bufs[k], ss, rs, k, sibling)

    def first(self, ins, bufs, ss, rs):
        x, y, c, _ = _mesh_place()
        for k in range(self.n):
            self._stage(ins, bufs, ss, k, 1 - c, 1).start()
        for k in range(self.n):
            self._stage(ins, bufs, ss, k, c, 2).start()
        for k in range(self.n):
            self._stage(ins, bufs, ss, k, 1 - c, 1).wait()
            self._give(bufs, ss, rs, k, (x, y, 1 - c)).start()

    def middle(self, ins, bufs, ss, rs):
        pass

    def last(self, ins, bufs, ss, rs):
        x, y, c, _ = _mesh_place()
        for k in range(self.n):
            self._give(bufs, ss, rs, k, (x, y, 1 - c)).wait_recv()
            self._stage(ins, bufs, ss, k, c, 2).wait()

            @pl.loop(0, N_CHIP)
            def _(j):
                bufs[k][j] = (bufs[k][j].astype(F32) + bufs[2 * self.n + k][j].astype(F32)).astype(BF16)
        for k in range(self.n):
            self._give(bufs, ss, rs, k, (x, y, 1 - c)).wait_send()


def _presum(names, grads, name):
    rider = _Presum(names)
    n = rider.n

    def body(*refs):
        g_refs, got_refs, work_refs, sems = _split(refs, n, n, 2 * n, 2)
        bufs = list(got_refs) + list(work_refs)
        rider.first(g_refs, bufs, *sems)
        rider.last(g_refs, bufs, *sems)

    return pl.pallas_call(
        body, in_specs=[ANY] * n, out_specs=[VMEM_WHOLE] * n, out_shape=rider.out_shape,
        scratch_shapes=[pltpu.VMEM(s.shape, s.dtype) for s in rider.work_shape]
        + [pltpu.SemaphoreType.DMA((rider.n_sems,)), pltpu.SemaphoreType.DMA((rider.n_sems,))],
        compiler_params=_params(), name=name,
    )(*grads)


class _SendPartials:
    def __init__(self, names, small_shape=None):
        self.n = len(names)
        self.small = small_shape is not None
        self.n_sems = 3 * self.n + 7
        self.out_shape = [jax.ShapeDtypeStruct((N_CHIP,) + _half_shape(nm), BF16) for nm in names]
        if self.small:
            self.out_shape.append(jax.ShapeDtypeStruct((N_DEV,) + small_shape, F32))

    def _piece(self, ins, outs, ss, rs, k, j, peer, src_slot, dst_slot, c):
        return _remote(ins[k].at[src_slot], outs[k].at[dst_slot], ss, rs, 3 * k + j, (*peer, c))

    def _small(self, ins, outs, ss, rs, r, other, slot):
        return _remote(ins[self.n], outs[self.n].at[slot], ss, rs, 3 * self.n + r, other)

    @staticmethod
    def _others(x, y, c):
        return [(x, y, 1 - c), (1 - x, y, c), (1 - x, y, 1 - c), (x, 1 - y, c), (x, 1 - y, 1 - c),
                (1 - x, 1 - y, c), (1 - x, 1 - y, 1 - c)]

    def first(self, ins, outs, ss, rs):
        x, y, c, peers = _mesh_place()
        me = 2 * x + y
        for k in range(self.n):
            for j, (px, py) in enumerate(peers):
                self._piece(ins, outs, ss, rs, k, j, (px, py), 2 * px + py, me, c).start()
        if self.small:
            for r, other in enumerate(self._others(x, y, c)):
                self._small(ins, outs, ss, rs, r, other, 4 * x + 2 * y + c).start()
            outs[self.n][4 * x + 2 * y + c] = ins[self.n][...]
        for k in range(self.n):
            outs[k][me] = ins[k][me]

    def middle(self, ins, outs, ss, rs):
        pass

    def last(self, ins, outs, ss, rs):
        x, y, c, peers = _mesh_place()
        me = 2 * x + y
        for k in range(self.n):
            for j, (px, py) in enumerate(peers):
                self._piece(ins, outs, ss, rs, k, j, (px, py), me, 2 * px + py, c).wait_recv()
                self._piece(ins, outs, ss, rs, k, j, (px, py), 2 * px + py, me, c).wait_send()
        if self.small:
            for r, (px, py, pc) in enumerate(self._others(x, y, c)):
                self._small(ins, outs, ss, rs, r, (px, py, pc), 4 * px + 2 * py + pc).wait_recv()
                self._small(ins, outs, ss, rs, r, (px, py, pc), 4 * x + 2 * y + c).wait_send()


def _sum_swap(names, parts, small):
    n = len(parts)
    everyone = _SendPartials((), small.shape)

    def body(*refs):
        p_refs, (small_ref,), o_refs, (osmall_ref,), (all_ref,), (send_sems, recv_sems, ss_small, rs_small) = _split(
            refs, n, 1, n, 1, 1, 4)
        x, y, c = lax.axis_index("x"), lax.axis_index("y"), lax.axis_index("c")
        everyone.first([small_ref], [all_ref], ss_small, rs_small)

        def mine(k):
            part = _half(o_refs[k], c, names[k])
            return _remote(part, part, send_sems, recv_sems, k, (x, y, 1 - c))

        for k in range(n):
            for e in range(2):
                @pl.when(c == e)
                def _():
                    g = p_refs[k][0].astype(F32)
                    for s in range(1, N_CHIP):
                        g = g + p_refs[k][s].astype(F32)
                    r, cols = _half_shape(names[k])
                    if _BIG_SPLIT[names[k]] == 0:
                        o_refs[k][e * r:(e + 1) * r, :] = g
                    else:
                        o_refs[k][:, e * cols:(e + 1) * cols] = g
            mine(k).start()
        for k in range(n):
            theirs = _half(o_refs[k], 1 - c, names[k])
            _remote(theirs, theirs, send_sems, recv_sems, k, (x, y, 1 - c)).wait_recv()
            mine(k).wait_send()
        everyone.last([small_ref], [all_ref], ss_small, rs_small)
        g = all_ref[0]
        for d in range(1, N_DEV):
            g = g + all_ref[d]
        osmall_ref[...] = g

    res = pl.pallas_call(
        body, in_specs=[VMEM_WHOLE] * (n + 1), out_specs=[VMEM_WHOLE] * (n + 1),
        out_shape=[jax.ShapeDtypeStruct(_BIG_SHARD[nm], F32) for nm in names] + [jax.ShapeDtypeStruct(small.shape, F32)],
        scratch_shapes=[pltpu.VMEM((N_DEV,) + small.shape, F32), pltpu.SemaphoreType.DMA((n,)), pltpu.SemaphoreType.DMA((n,)),
                        pltpu.SemaphoreType.DMA((everyone.n_sems,)), pltpu.SemaphoreType.DMA((everyone.n_sems,))],
        compiler_params=_params(), name="sum_swap",
    )(*parts, small)
    return res[:n], res[n]


def _tile(rows, cols, itemsize, budget):
    t = cols if rows % 16 else rows
    other = rows if rows % 16 else cols
    step = 256 if rows % 16 else 32
    while t % step == 0 and t * other * itemsize > budget:
        t //= 2
    return (rows, t) if rows % 16 else (t, cols)


def _adamw_math(w, g, m, v):
    m = ADAM_B1 * m + (1.0 - ADAM_B1) * g
    v = ADAM_B2 * v + (1.0 - ADAM_B2) * (g * g)
    m_hat = m / (1.0 - ADAM_B1 ** ADAM_STEP)
    v_hat = v / (1.0 - ADAM_B2 ** ADAM_STEP)
    delta = -ADAM_LR * (m_hat / (jnp.sqrt(v_hat) + ADAM_EPS) + ADAM_WD * w)
    return delta, m, v


def _adamw_big(g, w, m, v, name):
    r, c = w.shape
    tr, tc = _tile(r, c, 4, 1024 * 1024)

    def body(g_ref, w_ref, m_ref, v_ref, d_ref, nm_ref, nv_ref):
        d_ref[...], nm_ref[...], nv_ref[...] = _adamw_math(w_ref[...], g_ref[...], m_ref[...], v_ref[...])

    blk = pl.BlockSpec((tr, tc), lambda i, l: (i, l))
    return pl.pallas_call(
        body, grid=(r // tr, c // tc), in_specs=[blk, blk, blk, blk],
        out_specs=[blk, blk, blk], out_shape=[jax.ShapeDtypeStruct((r, c), F32)] * 3,
        compiler_params=_params(("arbitrary", "arbitrary")), name=name,
    )(g, w, m, v)


def _adamw_rows(g, w, m, v, name):
    r, k, lanes = w.shape
    tr = 296

    def body(g_ref, w_ref, m_ref, v_ref, g3_ref, d_ref, nm_ref, nv_ref):
        g = g_ref[...].reshape(tr, k, lanes)
        g3_ref[...] = g
        d_ref[...], nm_ref[...], nv_ref[...] = _adamw_math(w_ref[...], g, m_ref[...], v_ref[...])

    rows = pl.BlockSpec((tr, k, lanes), lambda i: (i, 0, 0))
    return pl.pallas_call(
        body, grid=(pl.cdiv(r, tr),), in_specs=[pl.BlockSpec((tr, k * lanes), lambda i: (i, 0)), rows, rows, rows],
        out_specs=[rows] * 4, out_shape=[jax.ShapeDtypeStruct((r, k, lanes), F32)] * 4,
        compiler_params=_params(("arbitrary",)), name=name,
    )(g, w, m, v)


def _adamw_small(ws, gs, ms, vs):
    n = len(ws)

    def body(*refs):
        w_refs, g_refs, m_refs, v_refs, d_refs, nm_refs, nv_refs = _split(refs, *([n] * 7))
        for k in range(n):
            d_refs[k][...], nm_refs[k][...], nv_refs[k][...] = _adamw_math(w_refs[k][...], g_refs[k][...], m_refs[k][...],
                                                                             v_refs[k][...])

    shapes = [jax.ShapeDtypeStruct(w.shape, F32) for w in ws]
    res = pl.pallas_call(body, out_shape=shapes * 3, name="adamw_small")(*ws, *gs, *ms, *vs)
    return res[:n], res[n:2 * n], res[2 * n:]


def _pack(arrs):
    flat = jnp.concatenate([a.reshape(-1) for a in arrs])
    rows = -(-flat.shape[0] // 1024) * 8
    return jnp.pad(flat, (0, rows * 128 - flat.shape[0])).reshape(rows, 128)


def _unpack(buf, shapes):
    flat = buf.reshape(-1)
    out, off = [], 0
    for s in shapes:
        size = 1
        for d in s:
            size *= d
        out.append(flat[off:off + size].reshape(s))
        off += size
    return out


def _block_rows(w):
    return jnp.pad(w.reshape(512, 4), ((0, 0), (0, 124)))


def _cols(a4):
    return jnp.transpose(a4, (1, 0, 2)).reshape(a4.shape[1], -1)


_LATE = ("w_pa", "w_pb", "w_o", "w_up", "w_down")
_RIDE_IN_PROJ = ("w_pa", "w_pb", "w_o")
_RIDE_MIXER = ("w_up", "w_down_a")
_RIDE_MERGE = ("w_down_b",)


def _full_weights(gathered):
    joined = {"w_in": (D_IN, D_MODEL), "w_o": (D_MODEL, D_MODEL)}
    return {n: (a.reshape(joined[n]) if n in joined else a) for n, a in gathered.items()}


def _local_step(x, target, w, sp, late_shards=None):
    sp = {n: (a.reshape(1, -1) if a.ndim == 1 else a) for n, a in sp.items()}
    wau = jnp.zeros((128, 256), F32).at[0:16].set(sp["w_a_up"])
    wif = jnp.zeros((1536, 128), F32).at[:, 0:8].set(sp["w_if"])
    bif = jnp.zeros((1, 128), F32).at[:, 0:8].set(sp["b_if"])
    p = {"wau": wau, "bau": sp["b_a_up"], "ggla": sp["g_gla_norm"], "cw": sp["conv_w"], "cb": sp["conv_b"],
         "wq": _block_rows(sp["w_q_ml"]), "wk": _block_rows(sp["w_k_ml"]), "wv": _block_rows(sp["w_v_ml"]),
         "wif": wif, "bif": bif, "skip": sp["ml_skip"], "gml": sp["g_ml_norm"]}

    if late_shards is None:
        (pm, gab, h), _ = _in_proj(x, sp["g_pre_mix"], w["w_in"])
        ab, *states = _mixer_fwd(pm, p)
        (x1, mix, merged), _ = _merge_fwd(ab, gab, x, w["w_pa"], w["w_pb"], w["w_o"], sp["g_post_mix"])
    else:
        shard = dict(zip(_LATE, late_shards))
        shard["w_down_a"], shard["w_down_b"] = shard["w_down"][0:512], shard["w_down"][512:1024]
        (pm, gab, h), got = _in_proj(x, sp["g_pre_mix"], w["w_in"], _Gather(_RIDE_IN_PROJ, middle_at=0.45),
                                     [shard[n] for n in _RIDE_IN_PROJ])
        w = dict(w, **_full_weights(dict(zip(_RIDE_IN_PROJ, got))))
        ab, *rest = _mixer_fwd(pm, p, _Gather(_RIDE_MIXER, middle_at=0.62), [shard[n] for n in _RIDE_MIXER])
        states = rest[:4]
        w.update(_full_weights(dict(zip(_RIDE_MIXER, rest[4:]))))
        (x1, mix, merged), got = _merge_fwd(ab, gab, x, w["w_pa"], w["w_pb"], w["w_o"], sp["g_post_mix"],
                                            _Gather(_RIDE_MERGE, middle_at=0.46), [shard[n] for n in _RIDE_MERGE])
        w.update(_full_weights(dict(zip(_RIDE_MERGE, got))))
    dx1, u, dd, h2, dpre, dg_post_mlp, dg_pre_mlp, loss = _mlp(x1, target, sp["g_pre_mlp"], sp["g_post_mlp"],
                                                                w["w_up"], w["w_down_a"], w["w_down_b"])
    dmix, dya, dyb, dgab, dab, dg_post_mix = _merge_bwd(dx1, mix, ab, gab, w["w_pa"], w["w_pb"], w["w_o"], sp["g_post_mix"])
    big = {
        "w_pa": _tn_matmul(ab[:, 0:512], dya, "dw_pa", shards=N_CHIP),
        "w_pb": _tn_matmul(ab[:, 512:1024], dyb, "dw_pb", shards=N_CHIP),
        "w_o": _tn_matmul(merged, dmix, "dw_o"),
        "w_up": _tn_matmul(h2, dpre, "dw_up", shards=N_CHIP),
    }
    if late_shards is None:
        big["w_down"] = _tn_matmul(u, dd, "dw_down")
        dpm, dp, _ = _mixer_bwd(pm, dab, states, p)
    else:
        pieces = lambda n: big[n].reshape((N_CHIP,) + _BIG_SHARD[n])
        big["w_down"], partial = _tn_matmul(u, dd, "dw_down", rider=_Presum(_LATE[:4]),
                                            rider_ins=[pieces(n) for n in _LATE[:4]])
        partial = list(partial) + list(_presum(("w_down",), [pieces("w_down")], "presum_w_down"))
        dpm, dp, parts = _mixer_bwd(pm, dab, states, p, _SendPartials(_LATE), partial)
        big = dict(zip(_LATE, parts))
    big["w_in"] = _dw_in(dpm, dgab, h)
    if late_shards is None:
        (dx, dg_pre_mix), _ = _in_proj_bwd(dpm, dgab, x, dx1, sp["g_pre_mix"], w["w_in"])
    else:
        partial = _presum(("w_in",), [big["w_in"].reshape((N_CHIP,) + _BIG_SHARD["w_in"])], "presum_w_in")
        (dx, dg_pre_mix), parts = _in_proj_bwd(dpm, dgab, x, dx1, sp["g_pre_mix"], w["w_in"], _SendPartials(("w_in",)),
                                               partial)
        big["w_in"] = parts[0]
    small = {
        "g_pre_mix": dg_pre_mix, "b_a_up": dp["bau"], "g_gla_norm": dp["ggla"], "conv_b": dp["cb"],
        "w_q_ml": dp["wq"][:, 0:4].reshape(128, 4, 4), "w_k_ml": dp["wk"][:, 0:4].reshape(128, 4, 4),
        "w_v_ml": dp["wv"][:, 0:4].reshape(128, 4, 4),
        "b_if": dp["bif"][:, 0:8], "ml_skip": dp["skip"], "g_ml_norm": dp["gml"], "g_post_mix": dg_post_mix,
        "g_pre_mlp": dg_pre_mlp, "g_post_mlp": dg_post_mlp, "w_a_up": dp["wau"][0:16], "conv_w": dp["cw"],
        "w_if": dp["wif"][:, 0:8], "loss": loss[:, 0:1],
    }
    return dx, big, small


_SMALL_REPL = ("g_pre_mix", "b_a_up", "g_gla_norm", "conv_b", "w_q_ml", "w_k_ml", "w_v_ml", "b_if", "ml_skip",
               "g_ml_norm", "g_post_mix", "g_pre_mlp", "g_post_mlp")
_SMALL_SHARDED = ("w_a_up", "conv_w", "w_if")
_SMALL_ORDER = _SMALL_REPL + _SMALL_SHARDED + ("loss",)
_WEIGHTS = ("g_pre_mix", "w_in", "w_a_up", "b_a_up", "g_gla_norm", "conv_w", "conv_b", "w_q_ml", "w_k_ml", "w_v_ml",
            "w_if", "b_if", "ml_skip", "g_ml_norm", "w_pa", "w_pb", "w_o", "g_post_mix", "g_pre_mlp", "w_up", "w_down",
            "g_post_mlp")


_BLOCK_WEIGHTS = ("w_q_ml", "w_k_ml", "w_v_ml")


def _stored(name, a):
    if name in _BLOCK_WEIGHTS:
        return jnp.transpose(a, (0, 2, 3, 1)).reshape(16, 128)
    if name == "w_if":
        return jnp.transpose(a, (0, 2, 1)).reshape(8, 384)
    return a


def _unstored(name, a):
    if name in _BLOCK_WEIGHTS:
        return jnp.transpose(a.reshape(1, 4, 4, 128), (0, 3, 1, 2))
    if name == "w_if":
        return jnp.transpose(a.reshape(1, 8, 384), (0, 2, 1))
    return a


def _as_shard(name, a):
    return jnp.transpose(a, (2, 0, 1)).reshape(IN_SHARD, D_MODEL // 128, 128) if name == "w_in" else a[0]


def _from_shard(name, a):
    return jnp.transpose(a, (1, 2, 0)).reshape(1, D_MODEL, IN_SHARD) if name == "w_in" else a[None]


def kernel(x, g_pre_mix, w_in, w_a_up, b_a_up, g_gla_norm, conv_w, conv_b, w_q_ml, w_k_ml, w_v_ml, w_if, b_if, ml_skip, g_ml_norm, w_pa, w_pb, w_o, g_post_mix, g_pre_mlp, w_up, w_down, g_post_mlp, loss_target, m_g_pre_mix, m_w_in, m_w_a_up, m_b_a_up, m_g_gla_norm, m_conv_w, m_conv_b, m_w_q_ml, m_w_k_ml, m_w_v_ml, m_w_if, m_b_if, m_ml_skip, m_g_ml_norm, m_w_pa, m_w_pb, m_w_o, m_g_post_mix, m_g_pre_mlp, m_w_up, m_w_down, m_g_post_mlp, v_g_pre_mix, v_w_in, v_w_a_up, v_b_a_up, v_g_gla_norm, v_conv_w, v_conv_b, v_w_q_ml, v_w_k_ml, v_w_v_ml, v_w_if, v_b_if, v_ml_skip, v_g_ml_norm, v_w_pa, v_w_pb, v_w_o, v_g_post_mix, v_g_pre_mlp, v_w_up, v_w_down, v_g_post_mlp):
    args = dict(locals())
    wts = {n: _as_shard(n, args[n]) for n in _WEIGHTS}
    mom = {n: _as_shard(n, args["m_" + n]) for n in _WEIGHTS}
    var = {n: _as_shard(n, args["v_" + n]) for n in _WEIGHTS}
    chip = 2 * lax.axis_index("x") + lax.axis_index("y")

    first = ("w_in",) + _SMALL_SHARDED
    gathered = dict(zip(first, _run_alone(_Gather(("w_in",), [wts[n] for n in _SMALL_SHARDED]),
                                          [wts[n].reshape(IN_SHARD, D_MODEL).astype(BF16) if n == "w_in" else wts[n]
                                           for n in first],
                                          "gather_first")))
    sp = {n: wts[n] for n in _SMALL_REPL}
    sp["w_a_up"] = _cols(gathered["w_a_up"])
    sp["conv_w"] = _cols(gathered["conv_w"])
    sp["w_if"] = gathered["w_if"].reshape(1536, 8)

    dx, big, small = _local_step(x[0], loss_target[0], _full_weights({"w_in": gathered["w_in"]}), sp,
                                 late_shards=[wts[n].astype(BF16) for n in _LATE])

    small_shapes = [small[n].shape for n in _SMALL_ORDER]
    packed = _pack([small[n] for n in _SMALL_ORDER])
    sums, small_sum = _sum_swap(_BIG, [big[n] for n in _BIG], packed)

    grads, delta, new_m, new_v = {}, {}, {}, {}
    for n, g in zip(_BIG, sums):
        if n == "w_in":
            g, d, nm, nv = _adamw_rows(g, wts[n], mom[n], var[n], "adamw_" + n)
        else:
            d, nm, nv = _adamw_big(g, wts[n], mom[n], var[n], "adamw_" + n)
        grads[n], delta[n], new_m[n], new_v[n] = (_from_shard(n, a) for a in (g, d, nm, nv))
    summed = dict(zip(_SMALL_ORDER, _unpack(small_sum, small_shapes)))
    loss = summed["loss"].reshape(())
    summed["w_a_up"] = lax.dynamic_slice_in_dim(summed["w_a_up"], chip * 64, 64, axis=1)
    summed["conv_w"] = lax.dynamic_slice_in_dim(summed["conv_w"], chip * 128, 128, axis=1)
    summed["w_if"] = lax.dynamic_slice_in_dim(summed["w_if"], chip * 384, 384, axis=0)
    small_names = _SMALL_REPL + _SMALL_SHARDED
    g_stored = [_stored(n, summed[n].reshape(args[n].shape)) for n in small_names]
    upd = _adamw_small([_stored(n, args[n]) for n in small_names], g_stored,
                       [_stored(n, args["m_" + n]) for n in small_names], [_stored(n, args["v_" + n]) for n in small_names])
    for dst, arrs in zip((grads, delta, new_m, new_v), (g_stored,) + tuple(upd)):
        dst.update({n: _unstored(n, a) for n, a in zip(small_names, arrs)})

    outs = [loss, dx[None]]
    for group in (grads, delta, new_m, new_v):
        outs += [group[n] for n in _WEIGHTS]
    return tuple(outs)
```

```python
import functools

import jax
import jax.numpy as jnp
from jax import lax
from jax.experimental import pallas as pl
from jax.experimental.pallas import tpu as pltpu

F32 = jnp.float32
BF16 = jnp.bfloat16

SEQ = 2048
D_MODEL = 1024
CHUNK = 64
N_CHUNK = SEQ // CHUNK
HEADS = 4
GLA_DK = 64
GLA_DV = 128
ML_DH = 128
D_FF = 4096
EPS = 1e-6
N_CHIP = 4
N_DEV = 8
TOK_TILE = 256
N_TOK_TILE = SEQ // TOK_TILE
SWEEP = 2
assert CHUNK == 64
N_SWEEP = N_CHUNK // SWEEP

PM_W = 2688
PM_XM = 1536
PM_OP = 2048
PM_AL = 2560
GAB_W = 2048
D_IN = 4624
IN_SHARD = D_IN // N_CHIP
IN_ALOW = 1536
IN_XM = 1552
IN_GATES = 2576

ADAM_LR = 0.001
ADAM_B1 = 0.9
ADAM_B2 = 0.999
ADAM_EPS = 1e-08
ADAM_WD = 0.01
ADAM_STEP = 10

VMEM_LIMIT = 56 * 1024 * 1024


def _params(sem=None):
    return pltpu.CompilerParams(dimension_semantics=sem, vmem_limit_bytes=VMEM_LIMIT)


def _dot(a, b, ca, cb):
    return lax.dot_general(a.astype(BF16), b.astype(BF16), (((ca,), (cb,)), ((), ())), preferred_element_type=F32)


def _pmm_nn(a, b):
    return _dot(a, b, 1, 0)


def _pmm_nt(a, b):
    return _dot(a, b, 1, 1)


def _pmm_tn(a, b):
    return _dot(a, b, 0, 0)


def _pcmm(c, x):
    return lax.dot_general(c, x, (((1,), (0,)), ((), ())), precision=lax.Precision.HIGHEST, preferred_element_type=F32)


@jax.custom_vjp
def _mm_nn(a, b):
    return _dot(a, b, 1, 0)


@jax.custom_vjp
def _mm_nt(a, b):
    return _dot(a, b, 1, 1)


@jax.custom_vjp
def _mm_tn(a, b):
    return _dot(a, b, 0, 0)


_mm_nn.defvjp(lambda a, b: (_dot(a, b, 1, 0), (a, b)), lambda r, g: (_mm_nt(g, r[1]), _mm_tn(r[0], g)))
_mm_nt.defvjp(lambda a, b: (_dot(a, b, 1, 1), (a, b)), lambda r, g: (_mm_nn(g, r[1]), _mm_tn(g, r[0])))
_mm_tn.defvjp(lambda a, b: (_dot(a, b, 0, 0), (a, b)), lambda r, g: (_mm_nt(r[1], g), _mm_nn(r[0], g)))


@jax.custom_vjp
def _cmm(c, x):
    return _pcmm(c, x)


_cmm.defvjp(
    lambda c, x: (_pcmm(c, x), c),
    lambda c, g: (jnp.zeros_like(c), lax.dot_general(c, g, (((0,), (0,)), ((), ())), precision=lax.Precision.HIGHEST,
                                                      preferred_element_type=F32)),
)

_PLAIN_OPS = (_pmm_nn, _pmm_nt, _pmm_tn, _pcmm)
_VJP_OPS = (_mm_nn, _mm_nt, _mm_tn, _cmm)


def _sigmoid(x):
    return 0.5 * (jnp.tanh(0.5 * x) + 1.0)


def _log_sigmoid(x):
    return jnp.minimum(x, 0.0) - jnp.log(1.0 + jnp.exp(-jnp.abs(x)))


def _mean(x):
    return jnp.mean(x, axis=-1, keepdims=True)


def _nt(a, b):
    return lax.dot_general(a, b, (((1,), (1,)), ((), ())), preferred_element_type=F32)


def _tn(a, b):
    return lax.dot_general(a, b, (((0,), (0,)), ((), ())), preferred_element_type=F32)


def _mixer_chunk(ops, p, st, pm, xprev8):
    mm_nn, mm_nt, mm_tn, cmm = ops
    n_rows = pm.shape[0]
    n_ch = n_rows // CHUNK
    row = lax.broadcasted_iota(jnp.int32, (n_rows, n_rows), 0)
    col = lax.broadcasted_iota(jnp.int32, (n_rows, n_rows), 1)
    tri = jnp.logical_and((row >> 6) == (col >> 6), row >= col).astype(F32)
    causal = tri[0:CHUNK, 0:CHUNK] > 0.0
    q = pm[:, 0:256]
    k = pm[:, 256:512]
    v = pm[:, 512:1024]
    g = pm[:, 1024:1536]
    xm = pm[:, PM_XM:PM_XM + 512]
    opre = pm[:, PM_OP:PM_OP + 512]
    alow = pm[:, PM_AL:PM_AL + 128]
    hs = range(HEADS)
    cs = range(n_ch)
    pairs = [(i, h) for i in cs for h in hs]
    rs = [slice(i * CHUNK, (i + 1) * CHUNK) for i in cs]
    last = [slice((i + 1) * CHUNK - 1, (i + 1) * CHUNK) for i in cs]
    s6 = [slice(h * GLA_DK, (h + 1) * GLA_DK) for h in hs]
    s12 = [slice(h * 128, (h + 1) * 128) for h in hs]

    la = _log_sigmoid(mm_nn(alow, p["wau"]) + p["bau"]) * (1.0 / 16.0)
    cum = cmm(tri, la)
    cum_last = [cum[last[i], :] for i in cs]
    to_end = jnp.concatenate([cum_last[i] - cum[rs[i], :] for i in cs], axis=0)
    e_pos = jnp.exp(cum)
    e_neg = jnp.exp(-cum)
    qs = q * (GLA_DK ** -0.5)
    qp = qs * e_pos
    qn = qs * e_neg
    kp = k * e_pos
    kn = k * e_neg
    kl = k * jnp.exp(to_end)
    dec = [jnp.exp(cum_last[i]) for i in cs]
    a_fwd = {(i, h): mm_nt(qp[rs[i], s6[h]], kn[rs[i], s6[h]]) for i, h in pairs}
    a_bwd = {(i, h): mm_nt(qn[rs[i], s6[h]], kp[rs[i], s6[h]]) for i, h in pairs}
    s_chunk = {(i, h): mm_tn(v[rs[i], s12[h]], kl[rs[i], s6[h]]) for i, h in pairs}
    mem = {(0, h): st["S"][h] for h in hs}
    for i, h in pairs:
        mem[(i + 1, h)] = mem[(i, h)] * dec[i][:, s6[h]] + s_chunk[(i, h)]
    s_new = [mem[(n_ch, h)] for h in hs]
    o_inter = {(i, h): mm_nt(qp[rs[i], s6[h]], mem[(i, h)]) for i, h in pairs}
    scores = {ih: jnp.where(causal, a_fwd[ih], a_bwd[ih]) for ih in pairs}
    o = {(i, h): mm_nn(scores[(i, h)], v[rs[i], s12[h]]) + o_inter[(i, h)] for i, h in pairs}
    o = {ih: o[ih] * lax.rsqrt(_mean(o[ih] * o[ih]) + EPS) * p["ggla"] for ih in pairs}
    gate = g * _sigmoid(g)
    out_a = {(i, h): o[(i, h)] * gate[rs[i], s12[h]] for i, h in pairs}

    xx = jnp.concatenate([xprev8, xm], axis=0)
    pre = p["cb"]
    for j in range(4):
        pre = pre + p["cw"][j:j + 1, :] * xx[5 + j:5 + j + n_rows, :]
    xc = pre * _sigmoid(pre)
    qm = [mm_nn(xc[:, s12[h]], p["wq"][h]) for h in hs]
    km = [mm_nn(xc[:, s12[h]], p["wk"][h]) for h in hs]
    vm = [mm_nn(xm[:, s12[h]], p["wv"][h]) for h in hs]
    qcat = jnp.concatenate(qm, axis=1)
    kcat = jnp.concatenate(km, axis=1)
    vcat = jnp.concatenate(vm, axis=1)
    gates = (mm_nn(qcat, p["wif"][0:512]) + mm_nn(kcat, p["wif"][512:1024]) + mm_nn(vcat, p["wif"][1024:1536])
             + p["bif"])
    lf = _log_sigmoid(gates)
    fc = cmm(tri, lf)
    gates_t = gates.T
    fc_t = fc.T
    ks = [km[h] * (ML_DH ** -0.5) for h in hs]
    qk = {(i, h): mm_nt(qm[h][rs[i]], ks[h][rs[i]]) for i, h in pairs}
    li_c = {(i, h): gates[rs[i], h:h + 1] for i, h in pairs}
    fc_c = {(i, h): fc[rs[i], 4 + h:5 + h] for i, h in pairs}
    f_last = {(i, h): fc[last[i], 4 + h:5 + h] for i, h in pairs}
    a = {ih: f_last[ih] - fc_c[ih] + li_c[ih] for ih in pairs}
    m_loc = {ih: jnp.max(a[ih], axis=0, keepdims=True) for ih in pairs}
    kw = {(i, h): ks[h][rs[i]] * jnp.exp(a[(i, h)] - m_loc[(i, h)]) for i, h in pairs}
    c_chunk = {(i, h): mm_tn(kw[(i, h)], vm[h][rs[i]]) for i, h in pairs}
    c_in = {(0, h): st["C"][h] for h in hs}
    n_in = {(0, h): st["n"][h] for h in hs}
    m_in = {(0, h): st["m"][h][:, 0:1] for h in hs}
    for i, h in pairs:
        m_nx = jnp.maximum(f_last[(i, h)] + m_in[(i, h)], m_loc[(i, h)])
        sp = jnp.exp(f_last[(i, h)] + m_in[(i, h)] - m_nx)
        sl = jnp.exp(m_loc[(i, h)] - m_nx)
        c_in[(i + 1, h)] = sp * c_in[(i, h)] + sl * c_chunk[(i, h)]
        n_in[(i + 1, h)] = sp * n_in[(i, h)] + sl * jnp.sum(kw[(i, h)], axis=0, keepdims=True)
        m_in[(i + 1, h)] = m_nx
    q_c = {(i, h): mm_nn(qm[h][rs[i]], c_in[(i, h)]) for i, h in pairs}
    log_d = {(i, h): gates_t[h:h + 1, rs[i]] - jnp.abs(fc_c[(i, h)] - fc_t[4 + h:5 + h, rs[i]]) for i, h in pairs}
    g_int = {ih: fc_c[ih] + m_in[ih] for ih in pairs}
    m_t = {ih: jnp.maximum(g_int[ih], jnp.max(log_d[ih], axis=1, keepdims=True)) for ih in pairs}
    s = {ih: qk[ih] * jnp.exp(log_d[ih] - m_t[ih]) for ih in pairs}
    scl = {ih: jnp.exp(g_int[ih] - m_t[ih]) for ih in pairs}
    num = {(i, h): mm_nn(s[(i, h)], vm[h][rs[i]]) + scl[(i, h)] * q_c[(i, h)] for i, h in pairs}
    den = {(i, h): jnp.sum(s[(i, h)], axis=1, keepdims=True)
           + scl[(i, h)] * jnp.sum(qm[h][rs[i]] * n_in[(i, h)], axis=1, keepdims=True) for i, h in pairs}
    den = {ih: jnp.maximum(jnp.abs(den[ih]), jnp.exp(-m_t[ih])) for ih in pairs}
    open_gate = _sigmoid(opre)
    hc = {(i, h): num[(i, h)] / den[(i, h)] * open_gate[rs[i], s12[h]] for i, h in pairs}
    d0 = {ih: hc[ih] - _mean(hc[ih]) for ih in pairs}
    y = {ih: d0[ih] * lax.rsqrt(_mean(d0[ih] * d0[ih]) + EPS) for ih in pairs}
    skipped = p["skip"] * xc
    out_b = {(i, h): y[(i, h)] * p["gml"][:, s12[h]] + skipped[rs[i], s12[h]] for i, h in pairs}
    ab = jnp.concatenate([jnp.concatenate([out_a[(i, h)] for h in hs] + [out_b[(i, h)] for h in hs], axis=1) for i in cs],
                         axis=0)
    new = {"S": s_new, "C": [c_in[(n_ch, h)] for h in hs], "n": [n_in[(n_ch, h)] for h in hs],
           "m": [jnp.broadcast_to(m_in[(n_ch, h)], (1, ML_DH)) for h in hs]}
    return ab, new


_P_NAMES = ("wau", "bau", "ggla", "cw", "cb", "wq", "wk", "wv", "wif", "bif", "skip", "gml")
_P_SHAPES = {
    "wau": (128, 256), "bau": (1, 256), "ggla": (1, 128), "cw": (4, 512), "cb": (1, 512),
    "wq": (512, 128), "wk": (512, 128), "wv": (512, 128),
    "wif": (1536, 128), "bif": (1, 128), "skip": (1, 512), "gml": (1, 512),
}
_P_BLOCKDIAG = ("wq", "wk", "wv")
_S_NAMES = ("S", "C", "n", "m")
_S_SHAPES = {"S": (HEADS, GLA_DV, GLA_DK), "C": (HEADS, ML_DH, ML_DH), "n": (HEADS, 1, ML_DH), "m": (HEADS, 1, ML_DH)}


def _per_head(ref):
    return [ref[h] for h in range(HEADS)]


def _block_mask():
    r = lax.broadcasted_iota(jnp.int32, (128, 128), 0)
    c = lax.broadcasted_iota(jnp.int32, (128, 128), 1)
    same_block = (r >> 2) == (c >> 2)
    spread = jnp.logical_and(r < 4, (c & 3) == r)
    return same_block.astype(F32), spread.astype(F32)


def _expand_blockdiag(w_ref, dense_ref):
    same_block, spread = _block_mask()
    for h in range(HEADS):
        tiled = _pmm_nn(w_ref[h * 128:(h + 1) * 128, :], spread)
        dense_ref[h] = tiled * same_block


def _collect_blockdiag(ddense_ref, dw_ref):
    same_block, spread = _block_mask()
    for h in range(HEADS):
        dw_ref[h * 128:(h + 1) * 128, :] = lax.dot_general(
            ddense_ref[h] * same_block, spread, (((1,), (1,)), ((), ())), precision=lax.Precision.HIGHEST,
            preferred_element_type=F32)


def _const_spec(shape):
    zeros = (0,) * len(shape)
    return pl.BlockSpec(shape, lambda i: zeros)


def _split(refs, *counts):
    out, at = [], 0
    for c in counts:
        out.append(refs[at:at + c])
        at += c
    assert at == len(refs)
    return out


def _ride(rider, phases, cond, ins, outs, sems):
    if rider is None:
        return
    lands, (send_sems, recv_sems, flush_sems) = sems[:-3], sems[-3:]

    @pl.when(cond)
    def _():
        for phase in phases:
            getattr(rider, phase)(ins, lands, send_sems, recv_sems)
        if "last" in phases:
            flush = [pltpu.make_async_copy(lands[k], outs[k], flush_sems.at[k]) for k in range(len(outs))]
            for cp in flush:
                cp.start()
            for cp in flush:
                cp.wait()


def _middle_step(rider, n_steps):
    return min(n_steps - 2, int(getattr(rider, "middle_at", 1.0) * n_steps))


def _rider_specs(rider, rider_ins):
    if rider is None:
        return [], [], [], []
    scratch = [pltpu.VMEM(s.shape, s.dtype) for s in list(rider.out_shape) + list(getattr(rider, "work_shape", ()))]
    scratch += [pltpu.SemaphoreType.DMA((rider.n_sems,)), pltpu.SemaphoreType.DMA((rider.n_sems,)),
                pltpu.SemaphoreType.DMA((len(rider.out_shape),))]
    in_space = getattr(rider, "in_space", VMEM_WHOLE)
    return [in_space] * len(rider_ins), [ANY] * len(rider.out_shape), list(rider.out_shape), scratch


def _mixer_fwd(pm, p, rider=None, rider_ins=()):
    n_p = len(_P_NAMES)
    r_in, r_out_specs, r_out_shape, r_sems = _rider_specs(rider, rider_ins)

    def body(*refs):
        (pm_ref, xprev_ref), p_list, ride_in, (ab_ref,), so_refs, ride_out, sc_refs, dense_list, sems = _split(
            refs, 2, n_p, len(r_in), 1, 4, len(r_out_specs), 4, 3, len(r_sems))
        p_refs = dict(zip(_P_NAMES, p_list))
        dense = dict(zip(_P_BLOCKDIAG, dense_list))
        n = pl.program_id(0)
        _ride(rider, ("first",), n == 0, ride_in, ride_out, sems)

        @pl.when(n == 0)
        def _():
            for r in sc_refs:
                r[...] = jnp.zeros_like(r)
            for nm in _P_BLOCKDIAG:
                _expand_blockdiag(p_refs[nm], dense[nm])

        st = {name: _per_head(r) for name, r in zip(_S_NAMES, sc_refs)}
        pv = {nm: (_per_head(dense[nm]) if nm in _P_BLOCKDIAG else p_refs[nm][...]) for nm in _P_NAMES}
        for name, r in zip(_S_NAMES, so_refs):
            for h in range(HEADS):
                r[0, h] = st[name][h]
        xprev8 = jnp.where(n > 0, xprev_ref[CHUNK - 8:CHUNK, :], 0.0)
        ab, st = _mixer_chunk(_PLAIN_OPS, pv, st, pm_ref[...], xprev8)
        ab_ref[...] = ab.astype(BF16)
        for name, r in zip(_S_NAMES, sc_refs):
            for h in range(HEADS):
                r[h] = st[name][h]
        _ride(rider, ("middle",), n == _middle_step(rider, N_SWEEP), ride_in, ride_out, sems)
        _ride(rider, ("last",), n == N_SWEEP - 1, ride_in, ride_out, sems)

    in_specs = [pl.BlockSpec((SWEEP * CHUNK, PM_W), lambda i: (i, 0)),
                pl.BlockSpec((CHUNK, 512), lambda i: (jnp.maximum(SWEEP * i - 1, 0), PM_XM // 512))]
    in_specs += [_const_spec(_P_SHAPES[nm]) for nm in _P_NAMES] + r_in
    out_specs = [pl.BlockSpec((SWEEP * CHUNK, 1024), lambda i: (i, 0))]
    out_shape = [jax.ShapeDtypeStruct((SEQ, 1024), BF16)]
    for nm in _S_NAMES:
        shp = _S_SHAPES[nm]
        out_specs.append(pl.BlockSpec((1,) + shp, lambda i: (i, 0, 0, 0)))
        out_shape.append(jax.ShapeDtypeStruct((N_SWEEP,) + shp, F32))
    return pl.pallas_call(
        body, grid=(N_SWEEP,), in_specs=in_specs, out_specs=out_specs + r_out_specs, out_shape=out_shape + r_out_shape,
        scratch_shapes=[pltpu.VMEM(_S_SHAPES[nm], F32) for nm in _S_NAMES]
        + [pltpu.VMEM((HEADS, 128, 128), F32) for _ in _P_BLOCKDIAG] + r_sems,
        compiler_params=_params(("arbitrary",)), name="mixer_fwd",
    )(pm, pm, *[p[nm] for nm in _P_NAMES], *rider_ins)


def _mixer_bwd(pm, dab, states, p, rider=None, rider_ins=()):
    n_p = len(_P_NAMES)
    r_in, r_out_specs, r_out_shape, r_sems = _rider_specs(rider, rider_ins)

    def body(*refs):
        ((pm_ref, xprev_ref, dab_ref), si_refs, p_list, ride_in, (dpm_ref,), dp_list, ride_out, ds_refs, (carry_ref,),
         dense_list, ddense_list, sems) = _split(refs, 3, 4, n_p, len(r_in), 1, n_p, len(r_out_specs), 4, 1, 3, 3, len(r_sems))
        p_refs = dict(zip(_P_NAMES, p_list))
        dp_refs = dict(zip(_P_NAMES, dp_list))
        dense = dict(zip(_P_BLOCKDIAG, dense_list))
        ddense = dict(zip(_P_BLOCKDIAG, ddense_list))
        i = pl.program_id(0)
        blk = N_SWEEP - 1 - i
        _ride(rider, ("first",), i == 0, ride_in, ride_out, sems)

        @pl.when(i == 0)
        def _():
            for r in ds_refs:
                r[...] = jnp.zeros_like(r)
            for nm in _P_NAMES:
                if nm in _P_BLOCKDIAG:
                    ddense[nm][...] = jnp.zeros_like(ddense[nm])
                    _expand_blockdiag(p_refs[nm], dense[nm])
                else:
                    dp_refs[nm][...] = jnp.zeros_like(dp_refs[nm])
            carry_ref[...] = jnp.zeros_like(carry_ref)

        pv = {nm: (_per_head(dense[nm]) if nm in _P_BLOCKDIAG else p_refs[nm][...]) for nm in _P_NAMES}
        dst = {name: _per_head(r) for name, r in zip(_S_NAMES, ds_refs)}
        st = {name: [r[0, h] for h in range(HEADS)] for name, r in zip(_S_NAMES, si_refs)}
        xprev8 = jnp.where(blk > 0, xprev_ref[CHUNK - 8:CHUNK, :], 0.0)
        _, vjp = jax.vjp(functools.partial(_mixer_chunk, _VJP_OPS), pv, st, pm_ref[...], xprev8)
        dp_sum, dst, dpm, dxprev8 = vjp((dab_ref[...], dst))
        reach = jnp.concatenate([jnp.zeros((SWEEP * CHUNK - 8, 512), F32), carry_ref[...]], axis=0)
        dpm_ref[:, 0:PM_XM] = dpm[:, 0:PM_XM].astype(BF16)
        dpm_ref[:, PM_XM:PM_XM + 512] = (dpm[:, PM_XM:PM_XM + 512] + reach).astype(BF16)
        dpm_ref[:, PM_XM + 512:PM_W] = dpm[:, PM_XM + 512:PM_W].astype(BF16)
        carry_ref[...] = dxprev8
        for name, r in zip(_S_NAMES, ds_refs):
            for h in range(HEADS):
                r[h] = dst[name][h]
        for nm in _P_NAMES:
            if nm in _P_BLOCKDIAG:
                for h in range(HEADS):
                    ddense[nm][h] += dp_sum[nm][h]
            else:
                dp_refs[nm][...] += dp_sum[nm]

        @pl.when(i == N_SWEEP - 1)
        def _():
            for nm in _P_BLOCKDIAG:
                _collect_blockdiag(ddense[nm], dp_refs[nm])

        _ride(rider, ("middle",), i == _middle_step(rider, N_SWEEP), ride_in, ride_out, sems)
        _ride(rider, ("last",), i == N_SWEEP - 1, ride_in, ride_out, sems)

    rev = lambda i: (N_SWEEP - 1 - i, 0)
    in_specs = [pl.BlockSpec((SWEEP * CHUNK, PM_W), rev),
                pl.BlockSpec((CHUNK, 512), lambda i: (jnp.maximum(SWEEP * (N_SWEEP - 1 - i) - 1, 0), PM_XM // 512)),
                pl.BlockSpec((SWEEP * CHUNK, 1024), rev)]
    for nm in _S_NAMES:
        in_specs.append(pl.BlockSpec((1,) + _S_SHAPES[nm], lambda i: (N_SWEEP - 1 - i, 0, 0, 0)))
    in_specs += [_const_spec(_P_SHAPES[nm]) for nm in _P_NAMES] + r_in
    out_specs = [pl.BlockSpec((SWEEP * CHUNK, PM_W), rev)] + [_const_spec(_P_SHAPES[nm]) for nm in _P_NAMES]
    out_shape = [jax.ShapeDtypeStruct((SEQ, PM_W), BF16)] + [jax.ShapeDtypeStruct(_P_SHAPES[nm], F32) for nm in _P_NAMES]
    res = pl.pallas_call(
        body, grid=(N_SWEEP,), in_specs=in_specs, out_specs=out_specs + r_out_specs, out_shape=out_shape + r_out_shape,
        scratch_shapes=[pltpu.VMEM(_S_SHAPES[nm], F32) for nm in _S_NAMES] + [pltpu.VMEM((8, 512), F32)]
        + [pltpu.VMEM((HEADS, 128, 128), F32) for _ in range(2 * len(_P_BLOCKDIAG))] + r_sems,
        compiler_params=_params(("arbitrary",)), name="mixer_bwd",
    )(pm, pm, dab, *states, *[p[nm] for nm in _P_NAMES], *rider_ins)
    return res[0], dict(zip(_P_NAMES, res[1:1 + n_p])), res[1 + n_p:]


def _tok(width):
    return pl.BlockSpec((TOK_TILE, width), lambda i: (i, 0))


def _once(shape):
    zeros = (0,) * len(shape)
    return pl.BlockSpec(shape, lambda i: zeros, pipeline_mode=pl.Buffered(1))


def _rms_fwd(x):
    r = lax.rsqrt(_mean(x * x) + EPS)
    return x * r, r


def _rms_bwd(dy, xn, r, g):
    gd = dy * g
    return r * (gd - xn * _mean(xn * gd))


def _tiled_call(body, in_specs, out_specs, out_shape, args, name, rider=None, rider_ins=()):
    r_in, r_out_specs, r_out_shape, r_scratch = _rider_specs(rider, rider_ins)
    n_in, n_out = len(in_specs), len(out_specs)

    def hosted(*refs):
        ins, ride_in, outs, ride_out, scratch = _split(refs, n_in, len(r_in), n_out, len(r_out_specs), len(r_scratch))
        i = pl.program_id(0)
        _ride(rider, ("first",), i == 0, ride_in, ride_out, scratch)
        body(*ins, *outs)
        _ride(rider, ("middle",), i == _middle_step(rider, N_TOK_TILE), ride_in, ride_out, scratch)
        _ride(rider, ("last",), i == N_TOK_TILE - 1, ride_in, ride_out, scratch)

    res = pl.pallas_call(
        hosted, grid=(N_TOK_TILE,), in_specs=list(in_specs) + r_in, out_specs=list(out_specs) + r_out_specs,
        out_shape=list(out_shape) + r_out_shape, scratch_shapes=r_scratch,
        compiler_params=_params(("arbitrary",)), name=name,
    )(*args, *rider_ins)
    return res[:n_out], res[n_out:]


def _in_proj(x, g_pre, wt_in, rider=None, rider_ins=()):
    def body(x_ref, g_ref, wt_ref, pm_ref, gab_ref, h_ref):
        xn, _ = _rms_fwd(x_ref[...])
        h = (xn * g_ref[...]).astype(BF16)
        h_ref[...] = h
        pm_ref[:, 0:PM_XM] = _nt(h, wt_ref[0:IN_ALOW, :])
        pm_ref[:, PM_XM:PM_AL] = _nt(h, wt_ref[IN_XM:IN_GATES, :])
        pm_ref[:, PM_AL:PM_W] = _nt(h, wt_ref[IN_ALOW:IN_ALOW + 128, :])
        gab_ref[...] = _nt(h, wt_ref[IN_GATES:D_IN, :])

    return _tiled_call(
        body, [_tok(D_MODEL), _once((1, D_MODEL)), _once((D_IN, D_MODEL))], [_tok(PM_W), _tok(GAB_W), _tok(D_MODEL)],
        [jax.ShapeDtypeStruct((SEQ, PM_W), F32), jax.ShapeDtypeStruct((SEQ, GAB_W), F32),
         jax.ShapeDtypeStruct((SEQ, D_MODEL), BF16)], (x, g_pre, wt_in), "in_proj", rider, rider_ins)


def _merge_fwd(ab, gab, x, w_pa4, w_pb4, w_o, g_post, rider=None, rider_ins=()):
    def body(ab_ref, gab_ref, x_ref, wpa_ref, wpb_ref, wo_ref, g_ref, x1_ref, mix_ref, mg_ref):
        a = ab_ref[:, 0:512]
        b = ab_ref[:, 512:1024]
        for j in range(N_CHIP):
            blk = slice(j * 256, (j + 1) * 256)
            ya = jnp.dot(a, wpa_ref[j], preferred_element_type=F32)
            yb = jnp.dot(b, wpb_ref[j], preferred_element_type=F32)
            sa = _sigmoid(gab_ref[:, j * 256:(j + 1) * 256])
            sb = _sigmoid(gab_ref[:, 1024 + j * 256:1024 + (j + 1) * 256])
            mg_ref[:, blk] = (sa * ya + sb * yb).astype(BF16)
        mix = jnp.dot(mg_ref[...], wo_ref[...], preferred_element_type=F32)
        mix_ref[...] = mix
        mn, _ = _rms_fwd(mix)
        x1_ref[...] = x_ref[...] + mn * g_ref[...]

    return _tiled_call(
        body, [_tok(1024), _tok(GAB_W), _tok(D_MODEL), _once((N_CHIP, 512, 256)), _once((N_CHIP, 512, 256)),
               _once((D_MODEL, D_MODEL)), _once((1, D_MODEL))], [_tok(D_MODEL), _tok(D_MODEL), _tok(D_MODEL)],
        [jax.ShapeDtypeStruct((SEQ, D_MODEL), F32), jax.ShapeDtypeStruct((SEQ, D_MODEL), F32),
         jax.ShapeDtypeStruct((SEQ, D_MODEL), BF16)], (ab, gab, x, w_pa4, w_pb4, w_o, g_post), "merge_fwd", rider, rider_ins)


def _mlp(x1, target, g_pre, g_post, w_up4, w_down_a4, w_down_b4):
    def body(x1_ref, t_ref, gpre_ref, gpost_ref, wup_ref, wda_ref, wdb_ref,
             dx1_ref, u_ref, dd_ref, h2_ref, dpre_ref, dgpost_ref, dgpre_ref, loss_ref):
        @pl.when(pl.program_id(0) == 0)
        def _():
            dgpost_ref[...] = jnp.zeros_like(dgpost_ref)
            dgpre_ref[...] = jnp.zeros_like(dgpre_ref)
            loss_ref[...] = jnp.zeros_like(loss_ref)

        x1 = x1_ref[...]
        gpre = gpre_ref[...]
        gpost = gpost_ref[...]
        xn2, r2 = _rms_fwd(x1)
        h2 = (xn2 * gpre).astype(BF16)
        h2_ref[...] = h2
        rl = []
        d = jnp.zeros((TOK_TILE, D_MODEL), F32)
        for j in range(N_CHIP):
            blk = slice(j * 1024, (j + 1) * 1024)
            r = jnp.maximum(jnp.dot(h2, wup_ref[j], preferred_element_type=F32), 0.0)
            rl.append(r)
            u = (r * r).astype(BF16)
            u_ref[:, blk] = u
            d = d + jnp.dot(u[:, 0:512], wda_ref[j], preferred_element_type=F32)
            d = d + jnp.dot(u[:, 512:1024], wdb_ref[j], preferred_element_type=F32)
        dn, r3 = _rms_fwd(d)
        diff = x1 + dn * gpost - t_ref[...]
        loss_ref[...] += jnp.sum(diff * diff, keepdims=True) * (0.5 / D_MODEL)
        dy = diff * (1.0 / D_MODEL)
        dgpost_ref[...] += jnp.sum(dy * dn, axis=0, keepdims=True)
        dd = _rms_bwd(dy, dn, r3, gpost).astype(BF16)
        dd_ref[...] = dd
        dh2 = jnp.zeros((TOK_TILE, D_MODEL), F32)
        for j in range(N_CHIP):
            blk = slice(j * 1024, (j + 1) * 1024)
            du = jnp.concatenate([_nt(dd, wda_ref[j]), _nt(dd, wdb_ref[j])], axis=1)
            dpre = (du * (2.0 * rl[j])).astype(BF16)
            dpre_ref[:, blk] = dpre
            dh2 = dh2 + _nt(dpre, wup_ref[j])
        dgpre_ref[...] += jnp.sum(dh2 * xn2, axis=0, keepdims=True)
        dx1_ref[...] = dy + _rms_bwd(dh2, xn2, r2, gpre)

    acc = pl.BlockSpec((1, D_MODEL), lambda i: (0, 0))
    return pl.pallas_call(
        body, grid=(N_TOK_TILE,),
        in_specs=[_tok(D_MODEL), _tok(D_MODEL), _once((1, D_MODEL)), _once((1, D_MODEL)),
                  _once((N_CHIP, D_MODEL, 1024)), _once((N_CHIP, 512, D_MODEL)), _once((N_CHIP, 512, D_MODEL))],
        out_specs=[_tok(D_MODEL), _tok(D_FF), _tok(D_MODEL), _tok(D_MODEL), _tok(D_FF), acc, acc,
                   pl.BlockSpec((1, 128), lambda i: (0, 0))],
        out_shape=[jax.ShapeDtypeStruct((SEQ, D_MODEL), F32), jax.ShapeDtypeStruct((SEQ, D_FF), BF16),
                   jax.ShapeDtypeStruct((SEQ, D_MODEL), BF16), jax.ShapeDtypeStruct((SEQ, D_MODEL), BF16),
                   jax.ShapeDtypeStruct((SEQ, D_FF), BF16), jax.ShapeDtypeStruct((1, D_MODEL), F32),
                   jax.ShapeDtypeStruct((1, D_MODEL), F32), jax.ShapeDtypeStruct((1, 128), F32)],
        compiler_params=_params(("arbitrary",)), name="mlp_fwd_bwd",
    )(x1, target, g_pre, g_post, w_up4, w_down_a4, w_down_b4)


def _merge_bwd(dx1, mix, ab, gab, w_pa4, w_pb4, w_o, g_post):
    def body(dx1_ref, mix_ref, ab_ref, gab_ref, wpa_ref, wpb_ref, wo_ref, g_ref,
             dmix_ref, dya_ref, dyb_ref, dgab_ref, dab_ref, dg_ref):
        @pl.when(pl.program_id(0) == 0)
        def _():
            dg_ref[...] = jnp.zeros_like(dg_ref)

        dx1 = dx1_ref[...]
        mn, r = _rms_fwd(mix_ref[...])
        dg_ref[...] += jnp.sum(dx1 * mn, axis=0, keepdims=True)
        dmix = _rms_bwd(dx1, mn, r, g_ref[...]).astype(BF16)
        dmix_ref[...] = dmix
        dmerged = _nt(dmix, wo_ref[...])
        a = ab_ref[:, 0:512]
        b = ab_ref[:, 512:1024]
        da = jnp.zeros((TOK_TILE, 512), F32)
        db = jnp.zeros((TOK_TILE, 512), F32)
        for j in range(N_CHIP):
            blk = slice(j * 256, (j + 1) * 256)
            blk_b = slice(1024 + j * 256, 1024 + (j + 1) * 256)
            dm = dmerged[:, blk]
            ya = jnp.dot(a, wpa_ref[j], preferred_element_type=F32)
            yb = jnp.dot(b, wpb_ref[j], preferred_element_type=F32)
            sa = _sigmoid(gab_ref[:, blk])
            sb = _sigmoid(gab_ref[:, blk_b])
            dya = (dm * sa).astype(BF16)
            dyb = (dm * sb).astype(BF16)
            dya_ref[:, blk] = dya
            dyb_ref[:, blk] = dyb
            dgab_ref[:, blk] = (dm * ya * sa * (1.0 - sa)).astype(BF16)
            dgab_ref[:, blk_b] = (dm * yb * sb * (1.0 - sb)).astype(BF16)
            da = da + _nt(dya, wpa_ref[j])
            db = db + _nt(dyb, wpb_ref[j])
        dab_ref[:, 0:512] = da
        dab_ref[:, 512:1024] = db

    return pl.pallas_call(
        body, grid=(N_TOK_TILE,),
        in_specs=[_tok(D_MODEL), _tok(D_MODEL), _tok(1024), _tok(GAB_W), _once((N_CHIP, 512, 256)),
                  _once((N_CHIP, 512, 256)), _once((D_MODEL, D_MODEL)), _once((1, D_MODEL))],
        out_specs=[_tok(D_MODEL), _tok(D_MODEL), _tok(D_MODEL), _tok(GAB_W), _tok(1024),
                   pl.BlockSpec((1, D_MODEL), lambda i: (0, 0))],
        out_shape=[jax.ShapeDtypeStruct((SEQ, D_MODEL), BF16), jax.ShapeDtypeStruct((SEQ, D_MODEL), BF16),
                   jax.ShapeDtypeStruct((SEQ, D_MODEL), BF16), jax.ShapeDtypeStruct((SEQ, GAB_W), BF16),
                   jax.ShapeDtypeStruct((SEQ, 1024), F32), jax.ShapeDtypeStruct((1, D_MODEL), F32)],
        compiler_params=_params(("arbitrary",)), name="merge_bwd",
    )(dx1, mix, ab, gab, w_pa4, w_pb4, w_o, g_post)


def _in_proj_bwd(dpm, dgab, x, dx1, g_pre, wt_in, rider=None, rider_ins=()):
    def body(dpm_ref, dgab_ref, x_ref, dx1_ref, g_ref, wt_ref, dx_ref, dg_ref):
        @pl.when(pl.program_id(0) == 0)
        def _():
            dg_ref[...] = jnp.zeros_like(dg_ref)

        dh = jnp.dot(dpm_ref[:, 0:PM_XM], wt_ref[0:IN_ALOW, :], preferred_element_type=F32)
        dh = dh + jnp.dot(dpm_ref[:, PM_XM:PM_AL], wt_ref[IN_XM:IN_GATES, :], preferred_element_type=F32)
        dh = dh + jnp.dot(dpm_ref[:, PM_AL:PM_W], wt_ref[IN_ALOW:IN_ALOW + 128, :], preferred_element_type=F32)
        dh = dh + jnp.dot(dgab_ref[...], wt_ref[IN_GATES:D_IN, :], preferred_element_type=F32)
        xn, r = _rms_fwd(x_ref[...])
        dg_ref[...] += jnp.sum(dh * xn, axis=0, keepdims=True)
        dx_ref[...] = dx1_ref[...] + _rms_bwd(dh, xn, r, g_ref[...])

    return _tiled_call(
        body, [_tok(PM_W), _tok(GAB_W), _tok(D_MODEL), _tok(D_MODEL), _once((1, D_MODEL)), _once((D_IN, D_MODEL))],
        [_tok(D_MODEL), pl.BlockSpec((1, D_MODEL), lambda i: (0, 0))],
        [jax.ShapeDtypeStruct((SEQ, D_MODEL), F32), jax.ShapeDtypeStruct((1, D_MODEL), F32)],
        (dpm, dgab, x, dx1, g_pre, wt_in), "in_proj_bwd", rider, rider_ins)


def _dw_in(dpm, dgab, h):
    n_pm = PM_AL // 512
    n_blk = n_pm + GAB_W // 512

    def body(dpm_ref, dgab_ref, dal_ref, h_ref, o_ref):
        i = pl.program_id(0)
        off = pl.multiple_of(i * 512 + 16 * (i >= 3).astype(jnp.int32), 16)

        @pl.when(i < n_pm)
        def _():
            o_ref[pl.ds(off, 512), :] = _tn(dpm_ref[...], h_ref[...]).astype(BF16)

        @pl.when(i >= n_pm)
        def _():
            o_ref[pl.ds(off, 512), :] = _tn(dgab_ref[...], h_ref[...]).astype(BF16)

        @pl.when(i == 0)
        def _():
            o_ref[IN_ALOW:IN_XM, :] = _tn(dal_ref[...], h_ref[...])[0:IN_XM - IN_ALOW].astype(BF16)

    return pl.pallas_call(
        body, grid=(n_blk,),
        in_specs=[pl.BlockSpec((SEQ, 512), lambda i: (0, jnp.minimum(i, n_pm - 1))),
                  pl.BlockSpec((SEQ, 512), lambda i: (0, jnp.maximum(i - n_pm, 0))),
                  pl.BlockSpec((SEQ, 128), lambda i: (0, PM_AL // 128)),
                  _once((SEQ, D_MODEL))],
        out_specs=pl.BlockSpec((D_IN, D_MODEL), lambda i: (0, 0)),
        out_shape=jax.ShapeDtypeStruct((D_IN, D_MODEL), BF16),
        compiler_params=_params(("arbitrary",)), name="dw_in",
    )(dpm, dgab, dpm, h)


def _tn_matmul(a, b, name, shards=1, tm=512, rider=None, rider_ins=()):
    m, n = a.shape[1], b.shape[1]
    tm = min(tm, m)
    tn = n // shards if shards > 1 else min(n, 1024)
    steps_i, steps_j = m // tm, n // tn
    r_in, r_out_specs, r_out_shape, r_scratch = _rider_specs(rider, rider_ins)

    def body(*refs):
        (a_ref, b_ref), ride_in, (o_ref,), ride_out, scratch = _split(refs, 2, len(r_in), 1, len(r_out_specs), len(r_scratch))
        step = pl.program_id(0) * steps_j + pl.program_id(1)
        _ride(rider, ("first",), step == 0, ride_in, ride_out, scratch)
        o_ref[...] = _tn(a_ref[...], b_ref[...]).astype(BF16)
        _ride(rider, ("middle", "last"), step == steps_i * steps_j - 1, ride_in, ride_out, scratch)

    if shards > 1:
        out_spec = pl.BlockSpec((None, tm, tn), lambda i, j: (j, i, 0))
        out_shape = jax.ShapeDtypeStruct((shards, m, tn), BF16)
    else:
        out_spec = pl.BlockSpec((tm, tn), lambda i, j: (i, j))
        out_shape = jax.ShapeDtypeStruct((m, n), BF16)
    res = pl.pallas_call(
        body, grid=(steps_i, steps_j),
        in_specs=[pl.BlockSpec((SEQ, tm), lambda i, j: (0, i)), pl.BlockSpec((SEQ, tn), lambda i, j: (0, j))] + r_in,
        out_specs=[out_spec] + r_out_specs, out_shape=[out_shape] + r_out_shape, scratch_shapes=r_scratch,
        compiler_params=_params(("arbitrary", "arbitrary")), name=name,
    )(a, b, *rider_ins)
    return res[0] if rider is None else (res[0], res[1:])


MESH = pl.DeviceIdType.MESH
ANY = pl.BlockSpec(memory_space=pl.ANY)
VMEM_WHOLE = pl.BlockSpec(memory_space=pltpu.VMEM)

_BIG = ("w_in", "w_pa", "w_pb", "w_o", "w_up", "w_down")
_BIG_SHARD = {"w_in": (IN_SHARD, D_MODEL), "w_pa": (512, 256), "w_pb": (512, 256), "w_o": (256, D_MODEL),
              "w_up": (D_MODEL, 1024), "w_down": (1024, D_MODEL),
              "w_down_a": (512, D_MODEL), "w_down_b": (512, D_MODEL)}
_BIG_SPLIT = {"w_in": 1, "w_pa": 0, "w_pb": 0, "w_o": 0, "w_up": 0, "w_down": 0, "w_down_a": 0, "w_down_b": 0}


def _half(ref, e, name, lead=0, part=None):
    axis = _BIG_SPLIT[name]
    size = _BIG_SHARD[name][axis] // 2
    start = e * size
    if part is not None:
        size //= 2
        start = start + part * size
    start = pl.multiple_of(start, 128 if axis == 1 else 16)
    idx = [pl.ds(0, ref.shape[a]) for a in range(lead)]
    idx += [pl.ds(start, size), pl.ds(0, _BIG_SHARD[name][1])] if axis == 0 else [pl.ds(0, _BIG_SHARD[name][0]), pl.ds(start, size)]
    return ref.at[tuple(idx)]


def _half_shape(name):
    r, c = _BIG_SHARD[name]
    return (r // 2, c) if _BIG_SPLIT[name] == 0 else (r, c // 2)


def _remote(src, dst, send_sems, recv_sems, k, to):
    return pltpu.make_async_remote_copy(src_ref=src, dst_ref=dst, send_sem=send_sems.at[k], recv_sem=recv_sems.at[k],
                                        device_id=to, device_id_type=MESH)


def _mesh_place():
    x, y, c = lax.axis_index("x"), lax.axis_index("y"), lax.axis_index("c")
    return x, y, c, [(1 - x, y), (x, 1 - y), (1 - x, 1 - y)]


class _Gather:
    def __init__(self, names, small=(), middle_at=0.5):
        self.middle_at = middle_at
        self.names = tuple(names)
        self.nb = len(self.names)
        self.n = self.nb + len(small)
        self.n_sems = 8 * self.nb + 3 * len(small)
        self.out_shape = [jax.ShapeDtypeStruct((N_CHIP,) + _BIG_SHARD[nm], BF16) for nm in self.names]
        self.out_shape += [jax.ShapeDtypeStruct((N_CHIP,) + s.shape, s.dtype) for s in small]

    def _copies(self, ins, outs, ss, rs, k):
        x, y, c, _ = _mesh_place()
        name = self.names[k]
        me, xn, yn, dg = 2 * x + y, 2 * (1 - x) + y, 2 * x + (1 - y), 2 * (1 - x) + (1 - y)
        to_x, to_y, sibling = (1 - x, y, c), (x, 1 - y, c), (x, y, 1 - c)

        def region(slot, e, part=None):
            return _half(outs[k].at[slot], e, name, part=part)

        def copy(pair, src, dst, to):
            return _remote(src, dst, ss, rs, 8 * k + pair, to)

        mine = _half(ins[k], c, name)
        sent = [copy(0, mine, region(me, c), to_x), copy(1, mine, region(me, c), to_y),
                copy(2, region(xn, c, 0), region(xn, c, 0), to_y), copy(3, region(yn, c, 1), region(yn, c, 1), to_x),
                copy(4, region(xn, c), region(xn, c), sibling), copy(5, region(yn, c), region(yn, c), sibling),
                copy(6, region(dg, c, 0), region(dg, c, 0), sibling), copy(7, region(dg, c, 1), region(dg, c, 1), sibling)]
        landing = [region(xn, c), region(yn, c), region(dg, c, 0), region(dg, c, 1),
                   region(xn, 1 - c), region(yn, 1 - c), region(dg, 1 - c, 0), region(dg, 1 - c, 1)]
        received = [copy(pair, dst, dst, sibling) for pair, dst in enumerate(landing)]
        return sent, received

    def _small(self, ins, outs, ss, rs, k, j, peer, slot, c):
        return _remote(ins[k], outs[k].at[slot], ss, rs, 8 * self.nb + 3 * (k - self.nb) + j, (*peer, c))

    def first(self, ins, outs, ss, rs):
        x, y, c, peers = _mesh_place()
        me = 2 * x + y
        for k in range(self.nb):
            sent, _ = self._copies(ins, outs, ss, rs, k)
            sent[0].start()
            sent[1].start()
        for k in range(self.nb, self.n):
            for j, peer in enumerate(peers):
                self._small(ins, outs, ss, rs, k, j, peer, me, c).start()
        for k in range(self.n):
            outs[k][me] = ins[k][...]

    def middle(self, ins, outs, ss, rs):
        for k in range(self.nb):
            sent, received = self._copies(ins, outs, ss, rs, k)
            for pair in (0, 1):
                received[pair].wait_recv()
                sent[2 + pair].start()
                sent[4 + pair].start()

    def last(self, ins, outs, ss, rs):
        x, y, c, peers = _mesh_place()
        for k in range(self.nb):
            sent, received = self._copies(ins, outs, ss, rs, k)
            for pair in (2, 3):
                received[pair].wait_recv()
                sent[4 + pair].start()
        for k in range(self.nb):
            sent, received = self._copies(ins, outs, ss, rs, k)
            for pair in range(4, 8):
                received[pair].wait_recv()
            for cp in sent:
                cp.wait_send()
        for k in range(self.nb, self.n):
            for j, (px, py) in enumerate(peers):
                self._small(ins, outs, ss, rs, k, j, (px, py), 2 * px + py, c).wait_recv()
                self._small(ins, outs, ss, rs, k, j, (px, py), 2 * x + y, c).wait_send()


def _run_alone(rider, ins, name):
    def body(*refs):
        r_in, r_out, sems = _split(refs, len(ins), len(rider.out_shape), 2)
        rider.first(r_in, r_out, *sems)
        rider.middle(r_in, r_out, *sems)
        rider.last(r_in, r_out, *sems)

    return pl.pallas_call(
        body, in_specs=[VMEM_WHOLE] * len(ins), out_specs=[VMEM_WHOLE] * len(rider.out_shape), out_shape=rider.out_shape,
        scratch_shapes=[pltpu.SemaphoreType.DMA((rider.n_sems,)), pltpu.SemaphoreType.DMA((rider.n_sems,))],
        compiler_params=_params(), name=name,
    )(*ins)


class _Presum:
    in_space = ANY

    def __init__(self, names):
        self.names = tuple(names)
        self.n = len(self.names)
        self.n_sems = 3 * self.n
        self.out_shape = [jax.ShapeDtypeStruct((N_CHIP,) + _half_shape(nm), BF16) for nm in self.names]
        self.work_shape = self.out_shape + self.out_shape

    def _stage(self, ins, bufs, ss, k, e, which):
        n = self.n
        return pltpu.make_async_copy(_half(ins[k], e, self.names[k], lead=1), bufs[which * n + k], ss.at[which * n + k])

    def _give(self, bufs, ss, rs, k, sibling):
        return _remote(bufs[self.n + k], bufs[k], ss, rs, k, sibling)

    def first(self, ins, bufs, ss, rs):
        x, y, c, _ = _mesh_place()
        for k in range(self.n):
            self._stage(ins, bufs, ss, k, 1 - c, 1).start()
        for k in range(self.n):
            self._stage(ins, bufs, ss, k, c, 2).start()
        for k in range(self.n):
            self._stage(ins, bufs, ss, k, 1 - c, 1).wait()
            self._give(bufs, ss, rs, k, (x, y, 1 - c)).start()

    def middle(self, ins, bufs, ss, rs):
        pass

    def last(self, ins, bufs, ss, rs):
        x, y, c, _ = _mesh_place()
        for k in range(self.n):
            self._give(bufs, ss, rs, k, (x, y, 1 - c)).wait_recv()
            self._stage(ins, bufs, ss, k, c, 2).wait()

            @pl.loop(0, N_CHIP)
            def _(j):
                bufs[k][j] = (bufs[k][j].astype(F32) + bufs[2 * self.n + k][j].astype(F32)).astype(BF16)
        for k in range(self.n):
            self._give(bufs, ss, rs, k, (x, y, 1 - c)).wait_send()


def _presum(names, grads, name):
    rider = _Presum(names)
    n = rider.n

    def body(*refs):
        g_refs, got_refs, work_refs, sems = _split(refs, n, n, 2 * n, 2)
        bufs = list(got_refs) + list(work_refs)
        rider.first(g_refs, bufs, *sems)
        rider.last(g_refs, bufs, *sems)

    return pl.pallas_call(
        body, in_specs=[ANY] * n, out_specs=[VMEM_WHOLE] * n, out_shape=rider.out_shape,
        scratch_shapes=[pltpu.VMEM(s.shape, s.dtype) for s in rider.work_shape]
        + [pltpu.SemaphoreType.DMA((rider.n_sems,)), pltpu.SemaphoreType.DMA((rider.n_sems,))],
        compiler_params=_params(), name=name,
    )(*grads)


class _ReduceRelay:
    middle_at = 0.75

    def __init__(self, names):
        self.names = tuple(names)
        self.n = len(self.names)
        self.n_sems = 6 * self.n
        self.out_shape = [jax.ShapeDtypeStruct((N_CHIP,) + _half_shape(nm), BF16) for nm in self.names]
        quarter = [jax.ShapeDtypeStruct(self._part_shape(nm), BF16) for nm in self.names]
        self.work_shape = quarter + quarter

    @staticmethod
    def _part_shape(name):
        r, c = _half_shape(name)
        return (r // 2, c) if _BIG_SPLIT[name] == 0 else (r, c // 2)

    def _part(self, ref, name, p):
        r, c = self._part_shape(name)
        return ref.at[pl.ds(p * r, r), pl.ds(0, c)] if _BIG_SPLIT[name] == 0 else ref.at[pl.ds(0, r), pl.ds(p * c, c)]

    def _copies(self, ins, bufs, ss, rs, k):
        x, y, c, _ = _mesh_place()
        name, n = self.names[k], self.n
        me, xn, yn, dg = 2 * x + y, 2 * (1 - x) + y, 2 * x + (1 - y), 2 * (1 - x) + (1 - y)
        to_x, to_y = (1 - x, y, c), (x, 1 - y, c)
        mine = lambda slot, p: self._part(ins[k].at[slot], name, p)
        slot = lambda s, p: self._part(bufs[k].at[s], name, p)
        from_x, from_y = bufs[n + k], bufs[2 * n + k]

        def copy(pair, src, dst, to):
            return _remote(src, dst, ss, rs, 6 * k + pair, to)

        sent = [copy(0, mine(dg, 0), from_x, to_x), copy(1, mine(dg, 1), from_y, to_y),
                copy(2, mine(xn, 0), slot(me, 0), to_x), copy(3, mine(yn, 1), slot(me, 1), to_y),
                copy(4, from_y, slot(me, 1), to_x), copy(5, from_x, slot(me, 0), to_y)]
        landing = [from_x, from_y, slot(xn, 0), slot(yn, 1), slot(xn, 1), slot(yn, 0)]
        received = [copy(pair, dst, dst, to_x) for pair, dst in enumerate(landing)]
        return sent, received

    def first(self, ins, bufs, ss, rs):
        x, y, c, _ = _mesh_place()
        me, dg = 2 * x + y, 2 * (1 - x) + (1 - y)
        for k in range(self.n):
            sent, _ = self._copies(ins, bufs, ss, rs, k)
            for pair in range(4):
                sent[pair].start()
        for k in range(self.n):
            bufs[k][me] = ins[k][me]
            bufs[k][dg] = jnp.zeros(_half_shape(self.names[k]), BF16)

    def middle(self, ins, bufs, ss, rs):
        x, y, c, _ = _mesh_place()
        xn, yn = 2 * (1 - x) + y, 2 * x + (1 - y)
        for k in range(self.n):
            sent, received = self._copies(ins, bufs, ss, rs, k)
            name, n = self.names[k], self.n
            for pair, buf, own in ((0, bufs[n + k], self._part(ins[k].at[yn], name, 0)),
                                   (1, bufs[2 * n + k], self._part(ins[k].at[xn], name, 1))):
                received[pair].wait_recv()
                buf[...] = (buf[...].astype(F32) + own[...].astype(F32)).astype(BF16)
            sent[5].start()
            sent[4].start()

    def last(self, ins, bufs, ss, rs):
        for k in range(self.n):
            sent, received = self._copies(ins, bufs, ss, rs, k)
            for pair in range(2, 6):
                received[pair].wait_recv()
            for cp in sent:
                cp.wait_send()


class _SendPartials:
    def __init__(self, names, small_shape=None):
        self.n = len(names)
        self.small = small_shape is not None
        self.n_sems = 3 * self.n + 7
        self.out_shape = [jax.ShapeDtypeStruct((N_CHIP,) + _half_shape(nm), BF16) for nm in names]
        if self.small:
            self.out_shape.append(jax.ShapeDtypeStruct((N_DEV,) + small_shape, F32))

    def _piece(self, ins, outs, ss, rs, k, j, peer, src_slot, dst_slot, c):
        return _remote(ins[k].at[src_slot], outs[k].at[dst_slot], ss, rs, 3 * k + j, (*peer, c))

    def _small(self, ins, outs, ss, rs, r, other, slot):
        return _remote(ins[self.n], outs[self.n].at[slot], ss, rs, 3 * self.n + r, other)

    @staticmethod
    def _others(x, y, c):
        return [(x, y, 1 - c), (1 - x, y, c), (1 - x, y, 1 - c), (x, 1 - y, c), (x, 1 - y, 1 - c),
                (1 - x, 1 - y, c), (1 - x, 1 - y, 1 - c)]

    def first(self, ins, outs, ss, rs):
        x, y, c, peers = _mesh_place()
        me = 2 * x + y
        for k in range(self.n):
            for j, (px, py) in enumerate(peers):
                self._piece(ins, outs, ss, rs, k, j, (px, py), 2 * px + py, me, c).start()
        if self.small:
            for r, other in enumerate(self._others(x, y, c)):
                self._small(ins, outs, ss, rs, r, other, 4 * x + 2 * y + c).start()
            outs[self.n][4 * x + 2 * y + c] = ins[self.n][...]
        for k in range(self.n):
            outs[k][me] = ins[k][me]

    def middle(self, ins, outs, ss, rs):
        pass

    def last(self, ins, outs, ss, rs):
        x, y, c, peers = _mesh_place()
        me = 2 * x + y
        for k in range(self.n):
            for j, (px, py) in enumerate(peers):
                self._piece(ins, outs, ss, rs, k, j, (px, py), me, 2 * px + py, c).wait_recv()
                self._piece(ins, outs, ss, rs, k, j, (px, py), 2 * px + py, me, c).wait_send()
        if self.small:
            for r, (px, py, pc) in enumerate(self._others(x, y, c)):
                self._small(ins, outs, ss, rs, r, (px, py, pc), 4 * px + 2 * py + pc).wait_recv()
                self._small(ins, outs, ss, rs, r, (px, py, pc), 4 * x + 2 * y + c).wait_send()


def _sum_swap(names, parts, small):
    n = len(parts)
    everyone = _SendPartials((), small.shape)

    def body(*refs):
        p_refs, (small_ref,), o_refs, (osmall_ref,), (all_ref,), (send_sems, recv_sems, ss_small, rs_small) = _split(
            refs, n, 1, n, 1, 1, 4)
        x, y, c = lax.axis_index("x"), lax.axis_index("y"), lax.axis_index("c")
        everyone.first([small_ref], [all_ref], ss_small, rs_small)

        def mine(k):
            part = _half(o_refs[k], c, names[k])
            return _remote(part, part, send_sems, recv_sems, k, (x, y, 1 - c))

        for k in range(n):
            for e in range(2):
                @pl.when(c == e)
                def _():
                    g = p_refs[k][0].astype(F32)
                    for s in range(1, N_CHIP):
                        g = g + p_refs[k][s].astype(F32)
                    r, cols = _half_shape(names[k])
                    if _BIG_SPLIT[names[k]] == 0:
                        o_refs[k][e * r:(e + 1) * r, :] = g
                    else:
                        o_refs[k][:, e * cols:(e + 1) * cols] = g
            mine(k).start()
        for k in range(n):
            theirs = _half(o_refs[k], 1 - c, names[k])
            _remote(theirs, theirs, send_sems, recv_sems, k, (x, y, 1 - c)).wait_recv()
            mine(k).wait_send()
        everyone.last([small_ref], [all_ref], ss_small, rs_small)
        g = all_ref[0]
        for d in range(1, N_DEV):
            g = g + all_ref[d]
        osmall_ref[...] = g

    res = pl.pallas_call(
        body, in_specs=[VMEM_WHOLE] * (n + 1), out_specs=[VMEM_WHOLE] * (n + 1),
        out_shape=[jax.ShapeDtypeStruct(_BIG_SHARD[nm], F32) for nm in names] + [jax.ShapeDtypeStruct(small.shape, F32)],
        scratch_shapes=[pltpu.VMEM((N_DEV,) + small.shape, F32), pltpu.SemaphoreType.DMA((n,)), pltpu.SemaphoreType.DMA((n,)),
                        pltpu.SemaphoreType.DMA((everyone.n_sems,)), pltpu.SemaphoreType.DMA((everyone.n_sems,))],
        compiler_params=_params(), name="sum_swap",
    )(*parts, small)
    return res[:n], res[n]


def _tile(rows, cols, itemsize, budget):
    t = cols if rows % 16 else rows
    other = rows if rows % 16 else cols
    step = 256 if rows % 16 else 32
    while t % step == 0 and t * other * itemsize > budget:
        t //= 2
    return (rows, t) if rows % 16 else (t, cols)


def _adamw_math(w, g, m, v):
    m = ADAM_B1 * m + (1.0 - ADAM_B1) * g
    v = ADAM_B2 * v + (1.0 - ADAM_B2) * (g * g)
    m_hat = m / (1.0 - ADAM_B1 ** ADAM_STEP)
    v_hat = v / (1.0 - ADAM_B2 ** ADAM_STEP)
    delta = -ADAM_LR * (m_hat / (jnp.sqrt(v_hat) + ADAM_EPS) + ADAM_WD * w)
    return delta, m, v


def _adamw_big(g, w, m, v, name):
    r, c = w.shape
    tr, tc = _tile(r, c, 4, 1024 * 1024)

    def body(g_ref, w_ref, m_ref, v_ref, d_ref, nm_ref, nv_ref):
        d_ref[...], nm_ref[...], nv_ref[...] = _adamw_math(w_ref[...], g_ref[...], m_ref[...], v_ref[...])

    blk = pl.BlockSpec((tr, tc), lambda i, l: (i, l))
    return pl.pallas_call(
        body, grid=(r // tr, c // tc), in_specs=[blk, blk, blk, blk],
        out_specs=[blk, blk, blk], out_shape=[jax.ShapeDtypeStruct((r, c), F32)] * 3,
        compiler_params=_params(("arbitrary", "arbitrary")), name=name,
    )(g, w, m, v)


def _adamw_rows(g, w, m, v, name):
    r, k, lanes = w.shape
    tr = 296

    def body(g_ref, w_ref, m_ref, v_ref, g3_ref, d_ref, nm_ref, nv_ref):
        g = g_ref[...].reshape(tr, k, lanes)
        g3_ref[...] = g
        d_ref[...], nm_ref[...], nv_ref[...] = _adamw_math(w_ref[...], g, m_ref[...], v_ref[...])

    rows = pl.BlockSpec((tr, k, lanes), lambda i: (i, 0, 0))
    return pl.pallas_call(
        body, grid=(pl.cdiv(r, tr),), in_specs=[pl.BlockSpec((tr, k * lanes), lambda i: (i, 0)), rows, rows, rows],
        out_specs=[rows] * 4, out_shape=[jax.ShapeDtypeStruct((r, k, lanes), F32)] * 4,
        compiler_params=_params(("arbitrary",)), name=name,
    )(g, w, m, v)


def _adamw_small(ws, gs, ms, vs):
    n = len(ws)

    def body(*refs):
        w_refs, g_refs, m_refs, v_refs, d_refs, nm_refs, nv_refs = _split(refs, *([n] * 7))
        for k in range(n):
            d_refs[k][...], nm_refs[k][...], nv_refs[k][...] = _adamw_math(w_refs[k][...], g_refs[k][...], m_refs[k][...],
                                                                             v_refs[k][...])

    shapes = [jax.ShapeDtypeStruct(w.shape, F32) for w in ws]
    res = pl.pallas_call(body, out_shape=shapes * 3, name="adamw_small")(*ws, *gs, *ms, *vs)
    return res[:n], res[n:2 * n], res[2 * n:]


def _pack(arrs):
    flat = jnp.concatenate([a.reshape(-1) for a in arrs])
    rows = -(-flat.shape[0] // 1024) * 8
    return jnp.pad(flat, (0, rows * 128 - flat.shape[0])).reshape(rows, 128)


def _unpack(buf, shapes):
    flat = buf.reshape(-1)
    out, off = [], 0
    for s in shapes:
        size = 1
        for d in s:
            size *= d
        out.append(flat[off:off + size].reshape(s))
        off += size
    return out


def _block_rows(w):
    return jnp.pad(w.reshape(512, 4), ((0, 0), (0, 124)))


def _cols(a4):
    return jnp.transpose(a4, (1, 0, 2)).reshape(a4.shape[1], -1)


_LATE = ("w_pa", "w_pb", "w_o", "w_up", "w_down")
_RIDE_IN_PROJ = ("w_pa", "w_pb", "w_o")
_RIDE_MIXER = ("w_up", "w_down_a")
_RIDE_MERGE = ("w_down_b",)


def _full_weights(gathered):
    joined = {"w_in": (D_IN, D_MODEL), "w_o": (D_MODEL, D_MODEL)}
    return {n: (a.reshape(joined[n]) if n in joined else a) for n, a in gathered.items()}


def _local_step(x, target, w, sp, late_shards=None):
    sp = {n: (a.reshape(1, -1) if a.ndim == 1 else a) for n, a in sp.items()}
    wau = jnp.zeros((128, 256), F32).at[0:16].set(sp["w_a_up"])
    wif = jnp.zeros((1536, 128), F32).at[:, 0:8].set(sp["w_if"])
    bif = jnp.zeros((1, 128), F32).at[:, 0:8].set(sp["b_if"])
    p = {"wau": wau, "bau": sp["b_a_up"], "ggla": sp["g_gla_norm"], "cw": sp["conv_w"], "cb": sp["conv_b"],
         "wq": _block_rows(sp["w_q_ml"]), "wk": _block_rows(sp["w_k_ml"]), "wv": _block_rows(sp["w_v_ml"]),
         "wif": wif, "bif": bif, "skip": sp["ml_skip"], "gml": sp["g_ml_norm"]}

    if late_shards is None:
        (pm, gab, h), _ = _in_proj(x, sp["g_pre_mix"], w["w_in"])
        ab, *states = _mixer_fwd(pm, p)
        (x1, mix, merged), _ = _merge_fwd(ab, gab, x, w["w_pa"], w["w_pb"], w["w_o"], sp["g_post_mix"])
    else:
        shard = dict(zip(_LATE, late_shards))
        shard["w_down_a"], shard["w_down_b"] = shard["w_down"][0:512], shard["w_down"][512:1024]
        (pm, gab, h), got = _in_proj(x, sp["g_pre_mix"], w["w_in"], _Gather(_RIDE_IN_PROJ, middle_at=0.45),
                                     [shard[n] for n in _RIDE_IN_PROJ])
        w = dict(w, **_full_weights(dict(zip(_RIDE_IN_PROJ, got))))
        ab, *rest = _mixer_fwd(pm, p, _Gather(_RIDE_MIXER, middle_at=0.62), [shard[n] for n in _RIDE_MIXER])
        states = rest[:4]
        w.update(_full_weights(dict(zip(_RIDE_MIXER, rest[4:]))))
        (x1, mix, merged), got = _merge_fwd(ab, gab, x, w["w_pa"], w["w_pb"], w["w_o"], sp["g_post_mix"],
                                            _Gather(_RIDE_MERGE, middle_at=0.46), [shard[n] for n in _RIDE_MERGE])
        w.update(_full_weights(dict(zip(_RIDE_MERGE, got))))
    dx1, u, dd, h2, dpre, dg_post_mlp, dg_pre_mlp, loss = _mlp(x1, target, sp["g_pre_mlp"], sp["g_post_mlp"],
                                                                w["w_up"], w["w_down_a"], w["w_down_b"])
    dmix, dya, dyb, dgab, dab, dg_post_mix = _merge_bwd(dx1, mix, ab, gab, w["w_pa"], w["w_pb"], w["w_o"], sp["g_post_mix"])
    big = {
        "w_pa": _tn_matmul(ab[:, 0:512], dya, "dw_pa", shards=N_CHIP),
        "w_pb": _tn_matmul(ab[:, 512:1024], dyb, "dw_pb", shards=N_CHIP),
        "w_o": _tn_matmul(merged, dmix, "dw_o"),
        "w_up": _tn_matmul(h2, dpre, "dw_up", shards=N_CHIP),
    }
    if late_shards is None:
        big["w_down"] = _tn_matmul(u, dd, "dw_down")
        dpm, dp, _ = _mixer_bwd(pm, dab, states, p)
    else:
        pieces = lambda n: big[n].reshape((N_CHIP,) + _BIG_SHARD[n])
        big["w_down"], partial = _tn_matmul(u, dd, "dw_down", rider=_Presum(_LATE[:4]),
                                            rider_ins=[pieces(n) for n in _LATE[:4]])
        partial = list(partial) + list(_presum(("w_down",), [pieces("w_down")], "presum_w_down"))
        dpm, dp, parts = _mixer_bwd(pm, dab, states, p, _SendPartials(_LATE), partial)
        big = dict(zip(_LATE, parts))
    big["w_in"] = _dw_in(dpm, dgab, h)
    if late_shards is None:
        (dx, dg_pre_mix), _ = _in_proj_bwd(dpm, dgab, x, dx1, sp["g_pre_mix"], w["w_in"])
    else:
        partial = _presum(("w_in",), [big["w_in"].reshape((N_CHIP,) + _BIG_SHARD["w_in"])], "presum_w_in")
        (dx, dg_pre_mix), parts = _in_proj_bwd(dpm, dgab, x, dx1, sp["g_pre_mix"], w["w_in"], _ReduceRelay(("w_in",)),
                                               partial)
        big["w_in"] = parts[0]
    small = {
        "g_pre_mix": dg_pre_mix, "b_a_up": dp["bau"], "g_gla_norm": dp["ggla"], "conv_b": dp["cb"],
        "w_q_ml": dp["wq"][:, 0:4].reshape(128, 4, 4), "w_k_ml": dp["wk"][:, 0:4].reshape(128, 4, 4),
        "w_v_ml": dp["wv"][:, 0:4].reshape(128, 4, 4),
        "b_if": dp["bif"][:, 0:8], "ml_skip": dp["skip"], "g_ml_norm": dp["gml"], "g_post_mix": dg_post_mix,
        "g_pre_mlp": dg_pre_mlp, "g_post_mlp": dg_post_mlp, "w_a_up": dp["wau"][0:16], "conv_w": dp["cw"],
        "w_if": dp["wif"][:, 0:8], "loss": loss[:, 0:1],
    }
    return dx, big, small


_SMALL_REPL = ("g_pre_mix", "b_a_up", "g_gla_norm", "conv_b", "w_q_ml", "w_k_ml", "w_v_ml", "b_if", "ml_skip",
               "g_ml_norm", "g_post_mix", "g_pre_mlp", "g_post_mlp")
_SMALL_SHARDED = ("w_a_up", "conv_w", "w_if")
_SMALL_ORDER = _SMALL_REPL + _SMALL_SHARDED + ("loss",)
_WEIGHTS = ("g_pre_mix", "w_in", "w_a_up", "b_a_up", "g_gla_norm", "conv_w", "conv_b", "w_q_ml", "w_k_ml", "w_v_ml",
            "w_if", "b_if", "ml_skip", "g_ml_norm", "w_pa", "w_pb", "w_o", "g_post_mix", "g_pre_mlp", "w_up", "w_down",
            "g_post_mlp")


_BLOCK_WEIGHTS = ("w_q_ml", "w_k_ml", "w_v_ml")


def _stored(name, a):
    if name in _BLOCK_WEIGHTS:
        return jnp.transpose(a, (0, 2, 3, 1)).reshape(16, 128)
    if name == "w_if":
        return jnp.transpose(a, (0, 2, 1)).reshape(8, 384)
    return a


def _unstored(name, a):
    if name in _BLOCK_WEIGHTS:
        return jnp.transpose(a.reshape(1, 4, 4, 128), (0, 3, 1, 2))
    if name == "w_if":
        return jnp.transpose(a.reshape(1, 8, 384), (0, 2, 1))
    return a


def _as_shard(name, a):
    return jnp.transpose(a, (2, 0, 1)).reshape(IN_SHARD, D_MODEL // 128, 128) if name == "w_in" else a[0]


def _from_shard(name, a):
    return jnp.transpose(a, (1, 2, 0)).reshape(1, D_MODEL, IN_SHARD) if name == "w_in" else a[None]


def kernel(x, g_pre_mix, w_in, w_a_up, b_a_up, g_gla_norm, conv_w, conv_b, w_q_ml, w_k_ml, w_v_ml, w_if, b_if, ml_skip, g_ml_norm, w_pa, w_pb, w_o, g_post_mix, g_pre_mlp, w_up, w_down, g_post_mlp, loss_target, m_g_pre_mix, m_w_in, m_w_a_up, m_b_a_up, m_g_gla_norm, m_conv_w, m_conv_b, m_w_q_ml, m_w_k_ml, m_w_v_ml, m_w_if, m_b_if, m_ml_skip, m_g_ml_norm, m_w_pa, m_w_pb, m_w_o, m_g_post_mix, m_g_pre_mlp, m_w_up, m_w_down, m_g_post_mlp, v_g_pre_mix, v_w_in, v_w_a_up, v_b_a_up, v_g_gla_norm, v_conv_w, v_conv_b, v_w_q_ml, v_w_k_ml, v_w_v_ml, v_w_if, v_b_if, v_ml_skip, v_g_ml_norm, v_w_pa, v_w_pb, v_w_o, v_g_post_mix, v_g_pre_mlp, v_w_up, v_w_down, v_g_post_mlp):
    args = dict(locals())
    wts = {n: _as_shard(n, args[n]) for n in _WEIGHTS}
    mom = {n: _as_shard(n, args["m_" + n]) for n in _WEIGHTS}
    var = {n: _as_shard(n, args["v_" + n]) for n in _WEIGHTS}
    chip = 2 * lax.axis_index("x") + lax.axis_index("y")

    first = ("w_in",) + _SMALL_SHARDED
    gathered = dict(zip(first, _run_alone(_Gather(("w_in",), [wts[n] for n in _SMALL_SHARDED]),
                                          [wts[n].reshape(IN_SHARD, D_MODEL).astype(BF16) if n == "w_in" else wts[n]
                                           for n in first],
                                          "gather_first")))
    sp = {n: wts[n] for n in _SMALL_REPL}
    sp["w_a_up"] = _cols(gathered["w_a_up"])
    sp["conv_w"] = _cols(gathered["conv_w"])
    sp["w_if"] = gathered["w_if"].reshape(1536, 8)

    dx, big, small = _local_step(x[0], loss_target[0], _full_weights({"w_in": gathered["w_in"]}), sp,
                                 late_shards=[wts[n].astype(BF16) for n in _LATE])

    small_shapes = [small[n].shape for n in _SMALL_ORDER]
    packed = _pack([small[n] for n in _SMALL_ORDER])
    sums, small_sum = _sum_swap(_BIG, [big[n] for n in _BIG], packed)

    grads, delta, new_m, new_v = {}, {}, {}, {}
    for n, g in zip(_BIG, sums):
        if n == "w_in":
            g, d, nm, nv = _adamw_rows(g, wts[n], mom[n], var[n], "adamw_" + n)
        else:
            d, nm, nv = _adamw_big(g, wts[n], mom[n], var[n], "adamw_" + n)
        grads[n], delta[n], new_m[n], new_v[n] = (_from_shard(n, a) for a in (g, d, nm, nv))
    summed = dict(zip(_SMALL_ORDER, _unpack(small_sum, small_shapes)))
    loss = summed["loss"].reshape(())
    summed["w_a_up"] = lax.dynamic_slice_in_dim(summed["w_a_up"], chip * 64, 64, axis=1)
    summed["conv_w"] = lax.dynamic_slice_in_dim(summed["conv_w"], chip * 128, 128, axis=1)
    summed["w_if"] = lax.dynamic_slice_in_dim(summed["w_if"], chip * 384, 384, axis=0)
    small_names = _SMALL_REPL + _SMALL_SHARDED
    g_stored = [_stored(n, summed[n].reshape(args[n].shape)) for n in small_names]
    upd = _adamw_small([_stored(n, args[n]) for n in small_names], g_stored,
                       [_stored(n, args["m_" + n]) for n in small_names], [_stored(n, args["v_" + n]) for n in small_names])
    for dst, arrs in zip((grads, delta, new_m, new_v), (g_stored,) + tuple(upd)):
        dst.update({n: _unstored(n, a) for n, a in zip(small_names, arrs)})

    outs = [loss, dx[None]]
    for group in (grads, delta, new_m, new_v):
        outs += [group[n] for n in _WEIGHTS]
    return tuple(outs)
```

```python
import functools

import jax
import jax.numpy as jnp
from jax import lax
from jax.experimental import pallas as pl
from jax.experimental.pallas import tpu as pltpu

F32 = jnp.float32
BF16 = jnp.bfloat16

SEQ = 2048
D_MODEL = 1024
CHUNK = 64
N_CHUNK = SEQ // CHUNK
HEADS = 4
GLA_DK = 64
GLA_DV = 128
ML_DH = 128
D_FF = 4096
EPS = 1e-6
N_CHIP = 4
N_DEV = 8
TOK_TILE = 256
N_TOK_TILE = SEQ // TOK_TILE
SWEEP = 2
assert CHUNK == 64
N_SWEEP = N_CHUNK // SWEEP

PM_W = 2688
PM_XM = 1536
PM_OP = 2048
PM_AL = 2560
GAB_W = 2048
D_IN = 4624
IN_SHARD = D_IN // N_CHIP
IN_ALOW = 1536
IN_XM = 1552
IN_GATES = 2576

ADAM_LR = 0.001
ADAM_B1 = 0.9
ADAM_B2 = 0.999
ADAM_EPS = 1e-08
ADAM_WD = 0.01
ADAM_STEP = 10

VMEM_LIMIT = 56 * 1024 * 1024


def _params(sem=None):
    return pltpu.CompilerParams(dimension_semantics=sem, vmem_limit_bytes=VMEM_LIMIT)


def _dot(a, b, ca, cb):
    return lax.dot_general(a.astype(BF16), b.astype(BF16), (((ca,), (cb,)), ((), ())), preferred_element_type=F32)


def _pmm_nn(a, b):
    return _dot(a, b, 1, 0)


def _pmm_nt(a, b):
    return _dot(a, b, 1, 1)


def _pmm_tn(a, b):
    return _dot(a, b, 0, 0)


def _pcmm(c, x):
    return lax.dot_general(c, x, (((1,), (0,)), ((), ())), precision=lax.Precision.HIGHEST, preferred_element_type=F32)


@jax.custom_vjp
def _mm_nn(a, b):
    return _dot(a, b, 1, 0)


@jax.custom_vjp
def _mm_nt(a, b):
    return _dot(a, b, 1, 1)


@jax.custom_vjp
def _mm_tn(a, b):
    return _dot(a, b, 0, 0)


_mm_nn.defvjp(lambda a, b: (_dot(a, b, 1, 0), (a, b)), lambda r, g: (_mm_nt(g, r[1]), _mm_tn(r[0], g)))
_mm_nt.defvjp(lambda a, b: (_dot(a, b, 1, 1), (a, b)), lambda r, g: (_mm_nn(g, r[1]), _mm_tn(g, r[0])))
_mm_tn.defvjp(lambda a, b: (_dot(a, b, 0, 0), (a, b)), lambda r, g: (_mm_nt(r[1], g), _mm_nn(r[0], g)))


@jax.custom_vjp
def _cmm(c, x):
    return _pcmm(c, x)


_cmm.defvjp(
    lambda c, x: (_pcmm(c, x), c),
    lambda c, g: (jnp.zeros_like(c), lax.dot_general(c, g, (((0,), (0,)), ((), ())), precision=lax.Precision.HIGHEST,
                                                      preferred_element_type=F32)),
)

_PLAIN_OPS = (_pmm_nn, _pmm_nt, _pmm_tn, _pcmm)
_VJP_OPS = (_mm_nn, _mm_nt, _mm_tn, _cmm)


def _sigmoid(x):
    return 0.5 * (jnp.tanh(0.5 * x) + 1.0)


def _log_sigmoid(x):
    return jnp.minimum(x, 0.0) - jnp.log(1.0 + jnp.exp(-jnp.abs(x)))


def _mean(x):
    return jnp.mean(x, axis=-1, keepdims=True)


def _nt(a, b):
    return lax.dot_general(a, b, (((1,), (1,)), ((), ())), preferred_element_type=F32)


def _tn(a, b):
    return lax.dot_general(a, b, (((0,), (0,)), ((), ())), preferred_element_type=F32)


def _mixer_chunk(ops, p, st, pm, xprev8):
    mm_nn, mm_nt, mm_tn, cmm = ops
    n_rows = pm.shape[0]
    n_ch = n_rows // CHUNK
    row = lax.broadcasted_iota(jnp.int32, (n_rows, n_rows), 0)
    col = lax.broadcasted_iota(jnp.int32, (n_rows, n_rows), 1)
    tri = jnp.logical_and((row >> 6) == (col >> 6), row >= col).astype(F32)
    causal = tri[0:CHUNK, 0:CHUNK] > 0.0
    q = pm[:, 0:256]
    k = pm[:, 256:512]
    v = pm[:, 512:1024]
    g = pm[:, 1024:1536]
    xm = pm[:, PM_XM:PM_XM + 512]
    opre = pm[:, PM_OP:PM_OP + 512]
    alow = pm[:, PM_AL:PM_AL + 128]
    hs = range(HEADS)
    cs = range(n_ch)
    pairs = [(i, h) for i in cs for h in hs]
    rs = [slice(i * CHUNK, (i + 1) * CHUNK) for i in cs]
    last = [slice((i + 1) * CHUNK - 1, (i + 1) * CHUNK) for i in cs]
    s6 = [slice(h * GLA_DK, (h + 1) * GLA_DK) for h in hs]
    s12 = [slice(h * 128, (h + 1) * 128) for h in hs]

    xx = jnp.concatenate([xprev8, xm], axis=0)
    pre = p["cb"]
    for j in range(4):
        pre = pre + p["cw"][j:j + 1, :] * xx[5 + j:5 + j + n_rows, :]
    xc = pre * _sigmoid(pre)
    qm = [mm_nn(xc[:, s12[h]], p["wq"][h]) for h in hs]
    km = [mm_nn(xc[:, s12[h]], p["wk"][h]) for h in hs]
    vm = [mm_nn(xm[:, s12[h]], p["wv"][h]) for h in hs]
    qcat = jnp.concatenate(qm, axis=1)
    kcat = jnp.concatenate(km, axis=1)
    vcat = jnp.concatenate(vm, axis=1)
    gates = (mm_nn(qcat, p["wif"][0:512]) + mm_nn(kcat, p["wif"][512:1024]) + mm_nn(vcat, p["wif"][1024:1536])
             + p["bif"])
    lf = _log_sigmoid(gates)
    fc = cmm(tri, lf)
    gates_t = gates.T
    fc_t = fc.T

    la = _log_sigmoid(mm_nn(alow, p["wau"]) + p["bau"]) * (1.0 / 16.0)
    cum = cmm(tri, la)
    cum_last = [cum[last[i], :] for i in cs]
    to_end = jnp.concatenate([cum_last[i] - cum[rs[i], :] for i in cs], axis=0)
    e_pos = jnp.exp(cum)
    e_neg = jnp.exp(-cum)
    qs = q * (GLA_DK ** -0.5)
    qp = qs * e_pos
    qn = qs * e_neg
    kp = k * e_pos
    kn = k * e_neg
    kl = k * jnp.exp(to_end)
    dec = [jnp.exp(cum_last[i]) for i in cs]
    ks = [km[h] * (ML_DH ** -0.5) for h in hs]
    li_c = {(i, h): gates[rs[i], h:h + 1] for i, h in pairs}
    fc_c = {(i, h): fc[rs[i], 4 + h:5 + h] for i, h in pairs}
    f_last = {(i, h): fc[last[i], 4 + h:5 + h] for i, h in pairs}

    a_fwd = {(i, h): mm_nt(qp[rs[i], s6[h]], kn[rs[i], s6[h]]) for i, h in pairs}
    a_bwd = {(i, h): mm_nt(qn[rs[i], s6[h]], kp[rs[i], s6[h]]) for i, h in pairs}
    s_chunk = {(i, h): mm_tn(v[rs[i], s12[h]], kl[rs[i], s6[h]]) for i, h in pairs}
    qk = {(i, h): mm_nt(qm[h][rs[i]], ks[h][rs[i]]) for i, h in pairs}
    a = {ih: f_last[ih] - fc_c[ih] + li_c[ih] for ih in pairs}
    m_loc = {ih: jnp.max(a[ih], axis=0, keepdims=True) for ih in pairs}
    kw = {(i, h): ks[h][rs[i]] * jnp.exp(a[(i, h)] - m_loc[(i, h)]) for i, h in pairs}
    c_chunk = {(i, h): mm_tn(kw[(i, h)], vm[h][rs[i]]) for i, h in pairs}
    mem = {(0, h): st["S"][h] for h in hs}
    c_in = {(0, h): st["C"][h] for h in hs}
    n_in = {(0, h): st["n"][h] for h in hs}
    m_in = {(0, h): st["m"][h][:, 0:1] for h in hs}
    for i, h in pairs:
        mem[(i + 1, h)] = mem[(i, h)] * dec[i][:, s6[h]] + s_chunk[(i, h)]
        m_nx = jnp.maximum(f_last[(i, h)] + m_in[(i, h)], m_loc[(i, h)])
        sp = jnp.exp(f_last[(i, h)] + m_in[(i, h)] - m_nx)
        sl = jnp.exp(m_loc[(i, h)] - m_nx)
        c_in[(i + 1, h)] = sp * c_in[(i, h)] + sl * c_chunk[(i, h)]
        n_in[(i + 1, h)] = sp * n_in[(i, h)] + sl * jnp.sum(kw[(i, h)], axis=0, keepdims=True)
        m_in[(i + 1, h)] = m_nx
    s_new = [mem[(n_ch, h)] for h in hs]
    o_inter = {(i, h): mm_nt(qp[rs[i], s6[h]], mem[(i, h)]) for i, h in pairs}
    q_c = {(i, h): mm_nn(qm[h][rs[i]], c_in[(i, h)]) for i, h in pairs}
    scores = {ih: jnp.where(causal, a_fwd[ih], a_bwd[ih]) for ih in pairs}
    log_d = {(i, h): gates_t[h:h + 1, rs[i]] - jnp.abs(fc_c[(i, h)] - fc_t[4 + h:5 + h, rs[i]]) for i, h in pairs}
    g_int = {ih: fc_c[ih] + m_in[ih] for ih in pairs}
    m_t = {ih: jnp.maximum(g_int[ih], jnp.max(log_d[ih], axis=1, keepdims=True)) for ih in pairs}
    s = {ih: qk[ih] * jnp.exp(log_d[ih] - m_t[ih]) for ih in pairs}
    scl = {ih: jnp.exp(g_int[ih] - m_t[ih]) for ih in pairs}
    o = {(i, h): mm_nn(scores[(i, h)], v[rs[i], s12[h]]) + o_inter[(i, h)] for i, h in pairs}
    num = {(i, h): mm_nn(s[(i, h)], vm[h][rs[i]]) + scl[(i, h)] * q_c[(i, h)] for i, h in pairs}
    o = {ih: o[ih] * lax.rsqrt(_mean(o[ih] * o[ih]) + EPS) * p["ggla"] for ih in pairs}
    gate = g * _sigmoid(g)
    out_a = {(i, h): o[(i, h)] * gate[rs[i], s12[h]] for i, h in pairs}
    den = {(i, h): jnp.sum(s[(i, h)], axis=1, keepdims=True)
           + scl[(i, h)] * jnp.sum(qm[h][rs[i]] * n_in[(i, h)], axis=1, keepdims=True) for i, h in pairs}
    den = {ih: jnp.maximum(jnp.abs(den[ih]), jnp.exp(-m_t[ih])) for ih in pairs}
    open_gate = _sigmoid(opre)
    hc = {(i, h): num[(i, h)] / den[(i, h)] * open_gate[rs[i], s12[h]] for i, h in pairs}
    d0 = {ih: hc[ih] - _mean(hc[ih]) for ih in pairs}
    y = {ih: d0[ih] * lax.rsqrt(_mean(d0[ih] * d0[ih]) + EPS) for ih in pairs}
    skipped = p["skip"] * xc
    out_b = {(i, h): y[(i, h)] * p["gml"][:, s12[h]] + skipped[rs[i], s12[h]] for i, h in pairs}
    ab = jnp.concatenate([jnp.concatenate([out_a[(i, h)] for h in hs] + [out_b[(i, h)] for h in hs], axis=1) for i in cs],
                         axis=0)
    new = {"S": s_new, "C": [c_in[(n_ch, h)] for h in hs], "n": [n_in[(n_ch, h)] for h in hs],
           "m": [jnp.broadcast_to(m_in[(n_ch, h)], (1, ML_DH)) for h in hs]}
    return ab, new


_P_NAMES = ("wau", "bau", "ggla", "cw", "cb", "wq", "wk", "wv", "wif", "bif", "skip", "gml")
_P_SHAPES = {
    "wau": (128, 256), "bau": (1, 256), "ggla": (1, 128), "cw": (4, 512), "cb": (1, 512),
    "wq": (512, 128), "wk": (512, 128), "wv": (512, 128),
    "wif": (1536, 128), "bif": (1, 128), "skip": (1, 512), "gml": (1, 512),
}
_P_BLOCKDIAG = ("wq", "wk", "wv")
_S_NAMES = ("S", "C", "n", "m")
_S_SHAPES = {"S": (HEADS, GLA_DV, GLA_DK), "C": (HEADS, ML_DH, ML_DH), "n": (HEADS, 1, ML_DH), "m": (HEADS, 1, ML_DH)}


def _per_head(ref):
    return [ref[h] for h in range(HEADS)]


def _block_mask():
    r = lax.broadcasted_iota(jnp.int32, (128, 128), 0)
    c = lax.broadcasted_iota(jnp.int32, (128, 128), 1)
    same_block = (r >> 2) == (c >> 2)
    spread = jnp.logical_and(r < 4, (c & 3) == r)
    return same_block.astype(F32), spread.astype(F32)


def _expand_blockdiag(w_ref, dense_ref):
    same_block, spread = _block_mask()
    for h in range(HEADS):
        tiled = _pmm_nn(w_ref[h * 128:(h + 1) * 128, :], spread)
        dense_ref[h] = tiled * same_block


def _collect_blockdiag(ddense_ref, dw_ref):
    same_block, spread = _block_mask()
    for h in range(HEADS):
        dw_ref[h * 128:(h + 1) * 128, :] = lax.dot_general(
            ddense_ref[h] * same_block, spread, (((1,), (1,)), ((), ())), precision=lax.Precision.HIGHEST,
            preferred_element_type=F32)


def _const_spec(shape):
    zeros = (0,) * len(shape)
    return pl.BlockSpec(shape, lambda i: zeros)


def _split(refs, *counts):
    out, at = [], 0
    for c in counts:
        out.append(refs[at:at + c])
        at += c
    assert at == len(refs)
    return out


def _ride(rider, phases, cond, ins, outs, sems):
    if rider is None:
        return
    lands, (send_sems, recv_sems, flush_sems) = sems[:-3], sems[-3:]

    @pl.when(cond)
    def _():
        for phase in phases:
            getattr(rider, phase)(ins, lands, send_sems, recv_sems)
        if "last" in phases:
            flush = [pltpu.make_async_copy(lands[k], outs[k], flush_sems.at[k]) for k in range(len(outs))]
            for cp in flush:
                cp.start()
            for cp in flush:
                cp.wait()


def _middle_step(rider, n_steps):
    return min(n_steps - 2, int(getattr(rider, "middle_at", 1.0) * n_steps))


def _rider_specs(rider, rider_ins):
    if rider is None:
        return [], [], [], []
    scratch = [pltpu.VMEM(s.shape, s.dtype) for s in list(rider.out_shape) + list(getattr(rider, "work_shape", ()))]
    scratch += [pltpu.SemaphoreType.DMA((rider.n_sems,)), pltpu.SemaphoreType.DMA((rider.n_sems,)),
                pltpu.SemaphoreType.DMA((len(rider.out_shape),))]
    in_space = getattr(rider, "in_space", VMEM_WHOLE)
    return [in_space] * len(rider_ins), [ANY] * len(rider.out_shape), list(rider.out_shape), scratch


def _mixer_fwd(pm, p, rider=None, rider_ins=()):
    n_p = len(_P_NAMES)
    r_in, r_out_specs, r_out_shape, r_sems = _rider_specs(rider, rider_ins)

    def body(*refs):
        (pm_ref, xprev_ref), p_list, ride_in, (ab_ref,), so_refs, ride_out, sc_refs, dense_list, sems = _split(
            refs, 2, n_p, len(r_in), 1, 4, len(r_out_specs), 4, 3, len(r_sems))
        p_refs = dict(zip(_P_NAMES, p_list))
        dense = dict(zip(_P_BLOCKDIAG, dense_list))
        n = pl.program_id(0)
        _ride(rider, ("first",), n == 0, ride_in, ride_out, sems)

        @pl.when(n == 0)
        def _():
            for r in sc_refs:
                r[...] = jnp.zeros_like(r)
            for nm in _P_BLOCKDIAG:
                _expand_blockdiag(p_refs[nm], dense[nm])

        st = {name: _per_head(r) for name, r in zip(_S_NAMES, sc_refs)}
        pv = {nm: (_per_head(dense[nm]) if nm in _P_BLOCKDIAG else p_refs[nm][...]) for nm in _P_NAMES}
        for name, r in zip(_S_NAMES, so_refs):
            for h in range(HEADS):
                r[0, h] = st[name][h]
        xprev8 = jnp.where(n > 0, xprev_ref[CHUNK - 8:CHUNK, :], 0.0)
        ab, st = _mixer_chunk(_PLAIN_OPS, pv, st, pm_ref[...], xprev8)
        ab_ref[...] = ab.astype(BF16)
        for name, r in zip(_S_NAMES, sc_refs):
            for h in range(HEADS):
                r[h] = st[name][h]
        _ride(rider, ("middle",), n == _middle_step(rider, N_SWEEP), ride_in, ride_out, sems)
        _ride(rider, ("last",), n == N_SWEEP - 1, ride_in, ride_out, sems)

    in_specs = [pl.BlockSpec((SWEEP * CHUNK, PM_W), lambda i: (i, 0)),
                pl.BlockSpec((CHUNK, 512), lambda i: (jnp.maximum(SWEEP * i - 1, 0), PM_XM // 512))]
    in_specs += [_const_spec(_P_SHAPES[nm]) for nm in _P_NAMES] + r_in
    out_specs = [pl.BlockSpec((SWEEP * CHUNK, 1024), lambda i: (i, 0))]
    out_shape = [jax.ShapeDtypeStruct((SEQ, 1024), BF16)]
    for nm in _S_NAMES:
        shp = _S_SHAPES[nm]
        out_specs.append(pl.BlockSpec((1,) + shp, lambda i: (i, 0, 0, 0)))
        out_shape.append(jax.ShapeDtypeStruct((N_SWEEP,) + shp, F32))
    return pl.pallas_call(
        body, grid=(N_SWEEP,), in_specs=in_specs, out_specs=out_specs + r_out_specs, out_shape=out_shape + r_out_shape,
        scratch_shapes=[pltpu.VMEM(_S_SHAPES[nm], F32) for nm in _S_NAMES]
        + [pltpu.VMEM((HEADS, 128, 128), F32) for _ in _P_BLOCKDIAG] + r_sems,
        compiler_params=_params(("arbitrary",)), name="mixer_fwd",
    )(pm, pm, *[p[nm] for nm in _P_NAMES], *rider_ins)


def _mixer_bwd(pm, dab, states, p, rider=None, rider_ins=()):
    n_p = len(_P_NAMES)
    r_in, r_out_specs, r_out_shape, r_sems = _rider_specs(rider, rider_ins)

    def body(*refs):
        ((pm_ref, xprev_ref, dab_ref), si_refs, p_list, ride_in, (dpm_ref,), dp_list, ride_out, ds_refs, (carry_ref,),
         dense_list, ddense_list, sems) = _split(refs, 3, 4, n_p, len(r_in), 1, n_p, len(r_out_specs), 4, 1, 3, 3, len(r_sems))
        p_refs = dict(zip(_P_NAMES, p_list))
        dp_refs = dict(zip(_P_NAMES, dp_list))
        dense = dict(zip(_P_BLOCKDIAG, dense_list))
        ddense = dict(zip(_P_BLOCKDIAG, ddense_list))
        i = pl.program_id(0)
        blk = N_SWEEP - 1 - i
        _ride(rider, ("first",), i == 0, ride_in, ride_out, sems)

        @pl.when(i == 0)
        def _():
            for r in ds_refs:
                r[...] = jnp.zeros_like(r)
            for nm in _P_NAMES:
                if nm in _P_BLOCKDIAG:
                    ddense[nm][...] = jnp.zeros_like(ddense[nm])
                    _expand_blockdiag(p_refs[nm], dense[nm])
                else:
                    dp_refs[nm][...] = jnp.zeros_like(dp_refs[nm])
            carry_ref[...] = jnp.zeros_like(carry_ref)

        pv = {nm: (_per_head(dense[nm]) if nm in _P_BLOCKDIAG else p_refs[nm][...]) for nm in _P_NAMES}
        dst = {name: _per_head(r) for name, r in zip(_S_NAMES, ds_refs)}
        st = {name: [r[0, h] for h in range(HEADS)] for name, r in zip(_S_NAMES, si_refs)}
        xprev8 = jnp.where(blk > 0, xprev_ref[CHUNK - 8:CHUNK, :], 0.0)
        _, vjp = jax.vjp(functools.partial(_mixer_chunk, _VJP_OPS), pv, st, pm_ref[...], xprev8)
        dp_sum, dst, dpm, dxprev8 = vjp((dab_ref[...], dst))
        reach = jnp.concatenate([jnp.zeros((SWEEP * CHUNK - 8, 512), F32), carry_ref[...]], axis=0)
        dpm_ref[:, 0:PM_XM] = dpm[:, 0:PM_XM].astype(BF16)
        dpm_ref[:, PM_XM:PM_XM + 512] = (dpm[:, PM_XM:PM_XM + 512] + reach).astype(BF16)
        dpm_ref[:, PM_XM + 512:PM_W] = dpm[:, PM_XM + 512:PM_W].astype(BF16)
        carry_ref[...] = dxprev8
        for name, r in zip(_S_NAMES, ds_refs):
            for h in range(HEADS):
                r[h] = dst[name][h]
        for nm in _P_NAMES:
            if nm in _P_BLOCKDIAG:
                for h in range(HEADS):
                    ddense[nm][h] += dp_sum[nm][h]
            else:
                dp_refs[nm][...] += dp_sum[nm]

        @pl.when(i == N_SWEEP - 1)
        def _():
            for nm in _P_BLOCKDIAG:
                _collect_blockdiag(ddense[nm], dp_refs[nm])

        _ride(rider, ("middle",), i == _middle_step(rider, N_SWEEP), ride_in, ride_out, sems)
        _ride(rider, ("last",), i == N_SWEEP - 1, ride_in, ride_out, sems)

    rev = lambda i: (N_SWEEP - 1 - i, 0)
    in_specs = [pl.BlockSpec((SWEEP * CHUNK, PM_W), rev),
                pl.BlockSpec((CHUNK, 512), lambda i: (jnp.maximum(SWEEP * (N_SWEEP - 1 - i) - 1, 0), PM_XM // 512)),
                pl.BlockSpec((SWEEP * CHUNK, 1024), rev)]
    for nm in _S_NAMES:
        in_specs.append(pl.BlockSpec((1,) + _S_SHAPES[nm], lambda i: (N_SWEEP - 1 - i, 0, 0, 0)))
    in_specs += [_const_spec(_P_SHAPES[nm]) for nm in _P_NAMES] + r_in
    out_specs = [pl.BlockSpec((SWEEP * CHUNK, PM_W), rev)] + [_const_spec(_P_SHAPES[nm]) for nm in _P_NAMES]
    out_shape = [jax.ShapeDtypeStruct((SEQ, PM_W), BF16)] + [jax.ShapeDtypeStruct(_P_SHAPES[nm], F32) for nm in _P_NAMES]
    res = pl.pallas_call(
        body, grid=(N_SWEEP,), in_specs=in_specs, out_specs=out_specs + r_out_specs, out_shape=out_shape + r_out_shape,
        scratch_shapes=[pltpu.VMEM(_S_SHAPES[nm], F32) for nm in _S_NAMES] + [pltpu.VMEM((8, 512), F32)]
        + [pltpu.VMEM((HEADS, 128, 128), F32) for _ in range(2 * len(_P_BLOCKDIAG))] + r_sems,
        compiler_params=_params(("arbitrary",)), name="mixer_bwd",
    )(pm, pm, dab, *states, *[p[nm] for nm in _P_NAMES], *rider_ins)
    return res[0], dict(zip(_P_NAMES, res[1:1 + n_p])), res[1 + n_p:]


def _tok(width):
    return pl.BlockSpec((TOK_TILE, width), lambda i: (i, 0))


def _once(shape):
    zeros = (0,) * len(shape)
    return pl.BlockSpec(shape, lambda i: zeros, pipeline_mode=pl.Buffered(1))


def _rms_fwd(x):
    r = lax.rsqrt(_mean(x * x) + EPS)
    return x * r, r


def _rms_bwd(dy, xn, r, g):
    gd = dy * g
    return r * (gd - xn * _mean(xn * gd))


def _tiled_call(body, in_specs, out_specs, out_shape, args, name, rider=None, rider_ins=()):
    r_in, r_out_specs, r_out_shape, r_scratch = _rider_specs(rider, rider_ins)
    n_in, n_out = len(in_specs), len(out_specs)

    def hosted(*refs):
        ins, ride_in, outs, ride_out, scratch = _split(refs, n_in, len(r_in), n_out, len(r_out_specs), len(r_scratch))
        i = pl.program_id(0)
        _ride(rider, ("first",), i == 0, ride_in, ride_out, scratch)
        body(*ins, *outs)
        _ride(rider, ("middle",), i == _middle_step(rider, N_TOK_TILE), ride_in, ride_out, scratch)
        _ride(rider, ("last",), i == N_TOK_TILE - 1, ride_in, ride_out, scratch)

    res = pl.pallas_call(
        hosted, grid=(N_TOK_TILE,), in_specs=list(in_specs) + r_in, out_specs=list(out_specs) + r_out_specs,
        out_shape=list(out_shape) + r_out_shape, scratch_shapes=r_scratch,
        compiler_params=_params(("arbitrary",)), name=name,
    )(*args, *rider_ins)
    return res[:n_out], res[n_out:]


def _in_proj(x, g_pre, wt_in, rider=None, rider_ins=()):
    def body(x_ref, g_ref, wt_ref, pm_ref, gab_ref, h_ref):
        xn, _ = _rms_fwd(x_ref[...])
        h = (xn * g_ref[...]).astype(BF16)
        h_ref[...] = h
        pm_ref[:, 0:PM_XM] = _nt(h, wt_ref[0:IN_ALOW, :])
        pm_ref[:, PM_XM:PM_AL] = _nt(h, wt_ref[IN_XM:IN_GATES, :])
        pm_ref[:, PM_AL:PM_W] = _nt(h, wt_ref[IN_ALOW:IN_ALOW + 128, :])
        gab_ref[...] = _nt(h, wt_ref[IN_GATES:D_IN, :])

    return _tiled_call(
        body, [_tok(D_MODEL), _once((1, D_MODEL)), _once((D_IN, D_MODEL))], [_tok(PM_W), _tok(GAB_W), _tok(D_MODEL)],
        [jax.ShapeDtypeStruct((SEQ, PM_W), F32), jax.ShapeDtypeStruct((SEQ, GAB_W), F32),
         jax.ShapeDtypeStruct((SEQ, D_MODEL), BF16)], (x, g_pre, wt_in), "in_proj", rider, rider_ins)


def _merge_fwd(ab, gab, x, w_pa4, w_pb4, w_o, g_post, rider=None, rider_ins=()):
    def body(ab_ref, gab_ref, x_ref, wpa_ref, wpb_ref, wo_ref, g_ref, x1_ref, mix_ref, mg_ref):
        a = ab_ref[:, 0:512]
        b = ab_ref[:, 512:1024]
        for j in range(N_CHIP):
            blk = slice(j * 256, (j + 1) * 256)
            ya = jnp.dot(a, wpa_ref[j], preferred_element_type=F32)
            yb = jnp.dot(b, wpb_ref[j], preferred_element_type=F32)
            sa = _sigmoid(gab_ref[:, j * 256:(j + 1) * 256])
            sb = _sigmoid(gab_ref[:, 1024 + j * 256:1024 + (j + 1) * 256])
            mg_ref[:, blk] = (sa * ya + sb * yb).astype(BF16)
        mix = jnp.dot(mg_ref[...], wo_ref[...], preferred_element_type=F32)
        mix_ref[...] = mix
        mn, _ = _rms_fwd(mix)
        x1_ref[...] = x_ref[...] + mn * g_ref[...]

    return _tiled_call(
        body, [_tok(1024), _tok(GAB_W), _tok(D_MODEL), _once((N_CHIP, 512, 256)), _once((N_CHIP, 512, 256)),
               _once((D_MODEL, D_MODEL)), _once((1, D_MODEL))], [_tok(D_MODEL), _tok(D_MODEL), _tok(D_MODEL)],
        [jax.ShapeDtypeStruct((SEQ, D_MODEL), F32), jax.ShapeDtypeStruct((SEQ, D_MODEL), F32),
         jax.ShapeDtypeStruct((SEQ, D_MODEL), BF16)], (ab, gab, x, w_pa4, w_pb4, w_o, g_post), "merge_fwd", rider, rider_ins)


def _mlp(x1, target, g_pre, g_post, w_up4, w_down_a4, w_down_b4):
    def body(x1_ref, t_ref, gpre_ref, gpost_ref, wup_ref, wda_ref, wdb_ref,
             dx1_ref, u_ref, dd_ref, h2_ref, dpre_ref, dgpost_ref, dgpre_ref, loss_ref):
        @pl.when(pl.program_id(0) == 0)
        def _():
            dgpost_ref[...] = jnp.zeros_like(dgpost_ref)
            dgpre_ref[...] = jnp.zeros_like(dgpre_ref)
            loss_ref[...] = jnp.zeros_like(loss_ref)

        x1 = x1_ref[...]
        gpre = gpre_ref[...]
        gpost = gpost_ref[...]
        xn2, r2 = _rms_fwd(x1)
        h2 = (xn2 * gpre).astype(BF16)
        h2_ref[...] = h2
        rl = []
        d = jnp.zeros((TOK_TILE, D_MODEL), F32)
        for j in range(N_CHIP):
            blk = slice(j * 1024, (j + 1) * 1024)
            r = jnp.maximum(jnp.dot(h2, wup_ref[j], preferred_element_type=F32), 0.0)
            rl.append(r)
            u = (r * r).astype(BF16)
            u_ref[:, blk] = u
            d = d + jnp.dot(u[:, 0:512], wda_ref[j], preferred_element_type=F32)
            d = d + jnp.dot(u[:, 512:1024], wdb_ref[j], preferred_element_type=F32)
        dn, r3 = _rms_fwd(d)
        diff = x1 + dn * gpost - t_ref[...]
        loss_ref[...] += jnp.sum(diff * diff, keepdims=True) * (0.5 / D_MODEL)
        dy = diff * (1.0 / D_MODEL)
        dgpost_ref[...] += jnp.sum(dy * dn, axis=0, keepdims=True)
        dd = _rms_bwd(dy, dn, r3, gpost).astype(BF16)
        dd_ref[...] = dd
        dh2 = jnp.zeros((TOK_TILE, D_MODEL), F32)
        for j in range(N_CHIP):
            blk = slice(j * 1024, (j + 1) * 1024)
            du = jnp.concatenate([_nt(dd, wda_ref[j]), _nt(dd, wdb_ref[j])], axis=1)
            dpre = (du * (2.0 * rl[j])).astype(BF16)
            dpre_ref[:, blk] = dpre
            dh2 = dh2 + _nt(dpre, wup_ref[j])
        dgpre_ref[...] += jnp.sum(dh2 * xn2, axis=0, keepdims=True)
        dx1_ref[...] = dy + _rms_bwd(dh2, xn2, r2, gpre)

    acc = pl.BlockSpec((1, D_MODEL), lambda i: (0, 0))
    return pl.pallas_call(
        body, grid=(N_TOK_TILE,),
        in_specs=[_tok(D_MODEL), _tok(D_MODEL), _once((1, D_MODEL)), _once((1, D_MODEL)),
                  _once((N_CHIP, D_MODEL, 1024)), _once((N_CHIP, 512, D_MODEL)), _once((N_CHIP, 512, D_MODEL))],
        out_specs=[_tok(D_MODEL), _tok(D_FF), _tok(D_MODEL), _tok(D_MODEL), _tok(D_FF), acc, acc,
                   pl.BlockSpec((1, 128), lambda i: (0, 0))],
        out_shape=[jax.ShapeDtypeStruct((SEQ, D_MODEL), F32), jax.ShapeDtypeStruct((SEQ, D_FF), BF16),
                   jax.ShapeDtypeStruct((SEQ, D_MODEL), BF16), jax.ShapeDtypeStruct((SEQ, D_MODEL), BF16),
                   jax.ShapeDtypeStruct((SEQ, D_FF), BF16), jax.ShapeDtypeStruct((1, D_MODEL), F32),
                   jax.ShapeDtypeStruct((1, D_MODEL), F32), jax.ShapeDtypeStruct((1, 128), F32)],
        compiler_params=_params(("arbitrary",)), name="mlp_fwd_bwd",
    )(x1, target, g_pre, g_post, w_up4, w_down_a4, w_down_b4)


def _merge_bwd(dx1, mix, ab, gab, w_pa4, w_pb4, w_o, g_post):
    def body(dx1_ref, mix_ref, ab_ref, gab_ref, wpa_ref, wpb_ref, wo_ref, g_ref,
             dmix_ref, dya_ref, dyb_ref, dgab_ref, dab_ref, dg_ref):
        @pl.when(pl.program_id(0) == 0)
        def _():
            dg_ref[...] = jnp.zeros_like(dg_ref)

        dx1 = dx1_ref[...]
        mn, r = _rms_fwd(mix_ref[...])
        dg_ref[...] += jnp.sum(dx1 * mn, axis=0, keepdims=True)
        dmix = _rms_bwd(dx1, mn, r, g_ref[...]).astype(BF16)
        dmix_ref[...] = dmix
        dmerged = _nt(dmix, wo_ref[...])
        a = ab_ref[:, 0:512]
        b = ab_ref[:, 512:1024]
        da = jnp.zeros((TOK_TILE, 512), F32)
        db = jnp.zeros((TOK_TILE, 512), F32)
        for j in range(N_CHIP):
            blk = slice(j * 256, (j + 1) * 256)
            blk_b = slice(1024 + j * 256, 1024 + (j + 1) * 256)
            dm = dmerged[:, blk]
            ya = jnp.dot(a, wpa_ref[j], preferred_element_type=F32)
            yb = jnp.dot(b, wpb_ref[j], preferred_element_type=F32)
            sa = _sigmoid(gab_ref[:, blk])
            sb = _sigmoid(gab_ref[:, blk_b])
            dya = (dm * sa).astype(BF16)
            dyb = (dm * sb).astype(BF16)
            dya_ref[:, blk] = dya
            dyb_ref[:, blk] = dyb
            dgab_ref[:, blk] = (dm * ya * sa * (1.0 - sa)).astype(BF16)
            dgab_ref[:, blk_b] = (dm * yb * sb * (1.0 - sb)).astype(BF16)
            da = da + _nt(dya, wpa_ref[j])
            db = db + _nt(dyb, wpb_ref[j])
        dab_ref[:, 0:512] = da
        dab_ref[:, 512:1024] = db

    return pl.pallas_call(
        body, grid=(N_TOK_TILE,),
        in_specs=[_tok(D_MODEL), _tok(D_MODEL), _tok(1024), _tok(GAB_W), _once((N_CHIP, 512, 256)),
                  _once((N_CHIP, 512, 256)), _once((D_MODEL, D_MODEL)), _once((1, D_MODEL))],
        out_specs=[_tok(D_MODEL), _tok(D_MODEL), _tok(D_MODEL), _tok(GAB_W), _tok(1024),
                   pl.BlockSpec((1, D_MODEL), lambda i: (0, 0))],
        out_shape=[jax.ShapeDtypeStruct((SEQ, D_MODEL), BF16), jax.ShapeDtypeStruct((SEQ, D_MODEL), BF16),
                   jax.ShapeDtypeStruct((SEQ, D_MODEL), BF16), jax.ShapeDtypeStruct((SEQ, GAB_W), BF16),
                   jax.ShapeDtypeStruct((SEQ, 1024), F32), jax.ShapeDtypeStruct((1, D_MODEL), F32)],
        compiler_params=_params(("arbitrary",)), name="merge_bwd",
    )(dx1, mix, ab, gab, w_pa4, w_pb4, w_o, g_post)


def _in_proj_bwd(dpm, dgab, x, dx1, g_pre, wt_in, rider=None, rider_ins=()):
    def body(dpm_ref, dgab_ref, x_ref, dx1_ref, g_ref, wt_ref, dx_ref, dg_ref):
        @pl.when(pl.program_id(0) == 0)
        def _():
            dg_ref[...] = jnp.zeros_like(dg_ref)

        dh = jnp.dot(dpm_ref[:, 0:PM_XM], wt_ref[0:IN_ALOW, :], preferred_element_type=F32)
        dh = dh + jnp.dot(dpm_ref[:, PM_XM:PM_AL], wt_ref[IN_XM:IN_GATES, :], preferred_element_type=F32)
        dh = dh + jnp.dot(dpm_ref[:, PM_AL:PM_W], wt_ref[IN_ALOW:IN_ALOW + 128, :], preferred_element_type=F32)
        dh = dh + jnp.dot(dgab_ref[...], wt_ref[IN_GATES:D_IN, :], preferred_element_type=F32)
        xn, r = _rms_fwd(x_ref[...])
        dg_ref[...] += jnp.sum(dh * xn, axis=0, keepdims=True)
        dx_ref[...] = dx1_ref[...] + _rms_bwd(dh, xn, r, g_ref[...])

    return _tiled_call(
        body, [_tok(PM_W), _tok(GAB_W), _tok(D_MODEL), _tok(D_MODEL), _once((1, D_MODEL)), _once((D_IN, D_MODEL))],
        [_tok(D_MODEL), pl.BlockSpec((1, D_MODEL), lambda i: (0, 0))],
        [jax.ShapeDtypeStruct((SEQ, D_MODEL), F32), jax.ShapeDtypeStruct((1, D_MODEL), F32)],
        (dpm, dgab, x, dx1, g_pre, wt_in), "in_proj_bwd", rider, rider_ins)


def _dw_in(dpm, dgab, h):
    n_pm = PM_AL // 512
    n_blk = n_pm + GAB_W // 512

    def body(dpm_ref, dgab_ref, dal_ref, h_ref, o_ref):
        i = pl.program_id(0)
        off = pl.multiple_of(i * 512 + 16 * (i >= 3).astype(jnp.int32), 16)

        @pl.when(i < n_pm)
        def _():
            o_ref[pl.ds(off, 512), :] = _tn(dpm_ref[...], h_ref[...]).astype(BF16)

        @pl.when(i >= n_pm)
        def _():
            o_ref[pl.ds(off, 512), :] = _tn(dgab_ref[...], h_ref[...]).astype(BF16)

        @pl.when(i == 0)
        def _():
            o_ref[IN_ALOW:IN_XM, :] = _tn(dal_ref[...], h_ref[...])[0:IN_XM - IN_ALOW].astype(BF16)

    return pl.pallas_call(
        body, grid=(n_blk,),
        in_specs=[pl.BlockSpec((SEQ, 512), lambda i: (0, jnp.minimum(i, n_pm - 1))),
                  pl.BlockSpec((SEQ, 512), lambda i: (0, jnp.maximum(i - n_pm, 0))),
                  pl.BlockSpec((SEQ, 128), lambda i: (0, PM_AL // 128)),
                  _once((SEQ, D_MODEL))],
        out_specs=pl.BlockSpec((D_IN, D_MODEL), lambda i: (0, 0)),
        out_shape=jax.ShapeDtypeStruct((D_IN, D_MODEL), BF16),
        compiler_params=_params(("arbitrary",)), name="dw_in",
    )(dpm, dgab, dpm, h)


def _tn_matmul(a, b, name, shards=1, tm=512, rider=None, rider_ins=()):
    m, n = a.shape[1], b.shape[1]
    tm = min(tm, m)
    tn = n // shards if shards > 1 else min(n, 1024)
    steps_i, steps_j = m // tm, n // tn
    r_in, r_out_specs, r_out_shape, r_scratch = _rider_specs(rider, rider_ins)

    def body(*refs):
        (a_ref, b_ref), ride_in, (o_ref,), ride_out, scratch = _split(refs, 2, len(r_in), 1, len(r_out_specs), len(r_scratch))
        step = pl.program_id(0) * steps_j + pl.program_id(1)
        _ride(rider, ("first",), step == 0, ride_in, ride_out, scratch)
        o_ref[...] = _tn(a_ref[...], b_ref[...]).astype(BF16)
        _ride(rider, ("middle", "last"), step == steps_i * steps_j - 1, ride_in, ride_out, scratch)

    if shards > 1:
        out_spec = pl.BlockSpec((None, tm, tn), lambda i, j: (j, i, 0))
        out_shape = jax.ShapeDtypeStruct((shards, m, tn), BF16)
    else:
        out_spec = pl.BlockSpec((tm, tn), lambda i, j: (i, j))
        out_shape = jax.ShapeDtypeStruct((m, n), BF16)
    res = pl.pallas_call(
        body, grid=(steps_i, steps_j),
        in_specs=[pl.BlockSpec((SEQ, tm), lambda i, j: (0, i)), pl.BlockSpec((SEQ, tn), lambda i, j: (0, j))] + r_in,
        out_specs=[out_spec] + r_out_specs, out_shape=[out_shape] + r_out_shape, scratch_shapes=r_scratch,
        compiler_params=_params(("arbitrary", "arbitrary")), name=name,
    )(a, b, *rider_ins)
    return res[0] if rider is None else (res[0], res[1:])


MESH = pl.DeviceIdType.MESH
ANY = pl.BlockSpec(memory_space=pl.ANY)
VMEM_WHOLE = pl.BlockSpec(memory_space=pltpu.VMEM)

_BIG = ("w_in", "w_pa", "w_pb", "w_o", "w_up", "w_down")
_BIG_SHARD = {"w_in": (IN_SHARD, D_MODEL), "w_pa": (512, 256), "w_pb": (512, 256), "w_o": (256, D_MODEL),
              "w_up": (D_MODEL, 1024), "w_down": (1024, D_MODEL),
              "w_down_a": (512, D_MODEL), "w_down_b": (512, D_MODEL)}
_BIG_SPLIT = {"w_in": 1, "w_pa": 0, "w_pb": 0, "w_o": 0, "w_up": 0, "w_down": 0, "w_down_a": 0, "w_down_b": 0}


def _half(ref, e, name, lead=0, part=None):
    axis = _BIG_SPLIT[name]
    size = _BIG_SHARD[name][axis] // 2
    start = e * size
    if part is not None:
        size //= 2
        start = start + part * size
    start = pl.multiple_of(start, 128 if axis == 1 else 16)
    idx = [pl.ds(0, ref.shape[a]) for a in range(lead)]
    idx += [pl.ds(start, size), pl.ds(0, _BIG_SHARD[name][1])] if axis == 0 else [pl.ds(0, _BIG_SHARD[name][0]), pl.ds(start, size)]
    return ref.at[tuple(idx)]


def _half_shape(name):
    r, c = _BIG_SHARD[name]
    return (r // 2, c) if _BIG_SPLIT[name] == 0 else (r, c // 2)


def _remote(src, dst, send_sems, recv_sems, k, to):
    return pltpu.make_async_remote_copy(src_ref=src, dst_ref=dst, send_sem=send_sems.at[k], recv_sem=recv_sems.at[k],
                                        device_id=to, device_id_type=MESH)


def _mesh_place():
    x, y, c = lax.axis_index("x"), lax.axis_index("y"), lax.axis_index("c")
    return x, y, c, [(1 - x, y), (x, 1 - y), (1 - x, 1 - y)]


class _Gather:
    def __init__(self, names, small=(), middle_at=0.5):
        self.middle_at = middle_at
        self.names = tuple(names)
        self.nb = len(self.names)
        self.n = self.nb + len(small)
        self.n_sems = 8 * self.nb + 3 * len(small)
        self.out_shape = [jax.ShapeDtypeStruct((N_CHIP,) + _BIG_SHARD[nm], BF16) for nm in self.names]
        self.out_shape += [jax.ShapeDtypeStruct((N_CHIP,) + s.shape, s.dtype) for s in small]

    def _copies(self, ins, outs, ss, rs, k):
        x, y, c, _ = _mesh_place()
        name = self.names[k]
        me, xn, yn, dg = 2 * x + y, 2 * (1 - x) + y, 2 * x + (1 - y), 2 * (1 - x) + (1 - y)
        to_x, to_y, sibling = (1 - x, y, c), (x, 1 - y, c), (x, y, 1 - c)

        def region(slot, e, part=None):
            return _half(outs[k].at[slot], e, name, part=part)

        def copy(pair, src, dst, to):
            return _remote(src, dst, ss, rs, 8 * k + pair, to)

        mine = _half(ins[k], c, name)
        sent = [copy(0, mine, region(me, c), to_x), copy(1, mine, region(me, c), to_y),
                copy(2, region(xn, c, 0), region(xn, c, 0), to_y), copy(3, region(yn, c, 1), region(yn, c, 1), to_x),
                copy(4, region(xn, c), region(xn, c), sibling), copy(5, region(yn, c), region(yn, c), sibling),
                copy(6, region(dg, c, 0), region(dg, c, 0), sibling), copy(7, region(dg, c, 1), region(dg, c, 1), sibling)]
        landing = [region(xn, c), region(yn, c), region(dg, c, 0), region(dg, c, 1),
                   region(xn, 1 - c), region(yn, 1 - c), region(dg, 1 - c, 0), region(dg, 1 - c, 1)]
        received = [copy(pair, dst, dst, sibling) for pair, dst in enumerate(landing)]
        return sent, received

    def _small(self, ins, outs, ss, rs, k, j, peer, slot, c):
        return _remote(ins[k], outs[k].at[slot], ss, rs, 8 * self.nb + 3 * (k - self.nb) + j, (*peer, c))

    def first(self, ins, outs, ss, rs):
        x, y, c, peers = _mesh_place()
        me = 2 * x + y
        for k in range(self.nb):
            sent, _ = self._copies(ins, outs, ss, rs, k)
            sent[0].start()
            sent[1].start()
        for k in range(self.nb, self.n):
            for j, peer in enumerate(peers):
                self._small(ins, outs, ss, rs, k, j, peer, me, c).start()
        for k in range(self.n):
            outs[k][me] = ins[k][...]

    def middle(self, ins, outs, ss, rs):
        for k in range(self.nb):
            sent, received = self._copies(ins, outs, ss, rs, k)
            for pair in (0, 1):
                received[pair].wait_recv()
                sent[2 + pair].start()
                sent[4 + pair].start()

    def last(self, ins, outs, ss, rs):
        x, y, c, peers = _mesh_place()
        for k in range(self.nb):
            sent, received = self._copies(ins, outs, ss, rs, k)
            for pair in (2, 3):
                received[pair].wait_recv()
                sent[4 + pair].start()
        for k in range(self.nb):
            sent, received = self._copies(ins, outs, ss, rs, k)
            for pair in range(4, 8):
                received[pair].wait_recv()
            for cp in sent:
                cp.wait_send()
        for k in range(self.nb, self.n):
            for j, (px, py) in enumerate(peers):
                self._small(ins, outs, ss, rs, k, j, (px, py), 2 * px + py, c).wait_recv()
                self._small(ins, outs, ss, rs, k, j, (px, py), 2 * x + y, c).wait_send()


def _run_alone(rider, ins, name):
    def body(*refs):
        r_in, r_out, sems = _split(refs, len(ins), len(rider.out_shape), 2)
        rider.first(r_in, r_out, *sems)
        rider.middle(r_in, r_out, *sems)
        rider.last(r_in, r_out, *sems)

    return pl.pallas_call(
        body, in_specs=[VMEM_WHOLE] * len(ins), out_specs=[VMEM_WHOLE] * len(rider.out_shape), out_shape=rider.out_shape,
        scratch_shapes=[pltpu.SemaphoreType.DMA((rider.n_sems,)), pltpu.SemaphoreType.DMA((rider.n_sems,))],
        compiler_params=_params(), name=name,
    )(*ins)


class _Presum:
    in_space = ANY

    def __init__(self, names):
        self.names = tuple(names)
        self.n = len(self.names)
        self.n_sems = 3 * self.n
        self.out_shape = [jax.ShapeDtypeStruct((N_CHIP,) + _half_shape(nm), BF16) for nm in self.names]
        self.work_shape = self.out_shape + self.out_shape

    def _stage(self, ins, bufs, ss, k, e, which):
        n = self.n
        return pltpu.make_async_copy(_half(ins[k], e, self.names[k], lead=1), bufs[which * n + k], ss.at[which * n + k])

    def _give(self, bufs, ss, rs, k, sibling):
        return _remote(bufs[self.n + k], bufs[k], ss, rs, k, sibling)

    def first(self, ins, bufs, ss, rs):
        x, y, c, _ = _mesh_place()
        for k in range(self.n):
            self._stage(ins, bufs, ss, k, 1 - c, 1).start()
        for k in range(self.n):
            self._stage(ins, bufs, ss, k, c, 2).start()
        for k in range(self.n):
            self._stage(ins, bufs, ss, k, 1 - c, 1).wait()
            self._give(bufs, ss, rs, k, (x, y, 1 - c)).start()

    def middle(self, ins, bufs, ss, rs):
        pass

    def last(self, ins, bufs, ss, rs):
        x, y, c, _ = _mesh_place()
        for k in range(self.n):
            self._give(bufs, ss, rs, k, (x, y, 1 - c)).wait_recv()
            self._stage(ins, bufs, ss, k, c, 2).wait()

            @pl.loop(0, N_CHIP)
            def _(j):
                bufs[k][j] = (bufs[k][j].astype(F32) + bufs[2 * self.n + k][j].astype(F32)).astype(BF16)
        for k in range(self.n):
            self._give(bufs, ss, rs, k, (x, y, 1 - c)).wait_send()


def _presum(names, grads, name):
    rider = _Presum(names)
    n = rider.n

    def body(*refs):
        g_refs, got_refs, work_refs, sems = _split(refs, n, n, 2 * n, 2)
        bufs = list(got_refs) + list(work_refs)
        rider.first(g_refs, bufs, *sems)
        rider.last(g_refs, bufs, *sems)

    return pl.pallas_call(
        body, in_specs=[ANY] * n, out_specs=[VMEM_WHOLE] * n, out_shape=rider.out_shape,
        scratch_shapes=[pltpu.VMEM(s.shape, s.dtype) for s in rider.work_shape]
        + [pltpu.SemaphoreType.DMA((rider.n_sems,)), pltpu.SemaphoreType.DMA((rider.n_sems,))],
        compiler_params=_params(), name=name,
    )(*grads)


class _ReduceRelay:
    middle_at = 0.75

    def __init__(self, names):
        self.names = tuple(names)
        self.n = len(self.names)
        self.n_sems = 6 * self.n
        self.out_shape = [jax.ShapeDtypeStruct((N_CHIP,) + _half_shape(nm), BF16) for nm in self.names]
        quarter = [jax.ShapeDtypeStruct(self._part_shape(nm), BF16) for nm in self.names]
        self.work_shape = quarter + quarter

    @staticmethod
    def _part_shape(name):
        r, c = _half_shape(name)
        return (r // 2, c) if _BIG_SPLIT[name] == 0 else (r, c // 2)

    def _part(self, ref, name, p):
        r, c = self._part_shape(name)
        return ref.at[pl.ds(p * r, r), pl.ds(0, c)] if _BIG_SPLIT[name] == 0 else ref.at[pl.ds(0, r), pl.ds(p * c, c)]

    def _copies(self, ins, bufs, ss, rs, k):
        x, y, c, _ = _mesh_place()
        name, n = self.names[k], self.n
        me, xn, yn, dg = 2 * x + y, 2 * (1 - x) + y, 2 * x + (1 - y), 2 * (1 - x) + (1 - y)
        to_x, to_y = (1 - x, y, c), (x, 1 - y, c)
        mine = lambda slot, p: self._part(ins[k].at[slot], name, p)
        slot = lambda s, p: self._part(bufs[k].at[s], name, p)
        from_x, from_y = bufs[n + k], bufs[2 * n + k]

        def copy(pair, src, dst, to):
            return _remote(src, dst, ss, rs, 6 * k + pair, to)

        sent = [copy(0, mine(dg, 0), from_x, to_x), copy(1, mine(dg, 1), from_y, to_y),
                copy(2, mine(xn, 0), slot(me, 0), to_x), copy(3, mine(yn, 1), slot(me, 1), to_y),
                copy(4, from_y, slot(me, 1), to_x), copy(5, from_x, slot(me, 0), to_y)]
        landing = [from_x, from_y, slot(xn, 0), slot(yn, 1), slot(xn, 1), slot(yn, 0)]
        received = [copy(pair, dst, dst, to_x) for pair, dst in enumerate(landing)]
        return sent, received

    def first(self, ins, bufs, ss, rs):
        x, y, c, _ = _mesh_place()
        me, dg = 2 * x + y, 2 * (1 - x) + (1 - y)
        for k in range(self.n):
            sent, _ = self._copies(ins, bufs, ss, rs, k)
            for pair in range(4):
                sent[pair].start()
        for k in range(self.n):
            bufs[k][me] = ins[k][me]
            bufs[k][dg] = jnp.zeros(_half_shape(self.names[k]), BF16)

    def middle(self, ins, bufs, ss, rs):
        x, y, c, _ = _mesh_place()
        xn, yn = 2 * (1 - x) + y, 2 * x + (1 - y)
        for k in range(self.n):
            sent, received = self._copies(ins, bufs, ss, rs, k)
            name, n = self.names[k], self.n
            for pair, buf, own in ((0, bufs[n + k], self._part(ins[k].at[yn], name, 0)),
                                   (1, bufs[2 * n + k], self._part(ins[k].at[xn], name, 1))):
                received[pair].wait_recv()
                buf[...] = (buf[...].astype(F32) + own[...].astype(F32)).astype(BF16)
            sent[5].start()
            sent[4].start()

    def last(self, ins, bufs, ss, rs):
        for k in range(self.n):
            sent, received = self._copies(ins, bufs, ss, rs, k)
            for pair in range(2, 6):
                received[pair].wait_recv()
            for cp in sent:
                cp.wait_send()


class _SendPartials:
    def __init__(self, names, small_shape=None):
        self.n = len(names)
        self.small = small_shape is not None
        self.n_sems = 3 * self.n + 7
        self.out_shape = [jax.ShapeDtypeStruct((N_CHIP,) + _half_shape(nm), BF16) for nm in names]
        if self.small:
            self.out_shape.append(jax.ShapeDtypeStruct((N_DEV,) + small_shape, F32))

    def _piece(self, ins, outs, ss, rs, k, j, peer, src_slot, dst_slot, c):
        return _remote(ins[k].at[src_slot], outs[k].at[dst_slot], ss, rs, 3 * k + j, (*peer, c))

    def _small(self, ins, outs, ss, rs, r, other, slot):
        return _remote(ins[self.n], outs[self.n].at[slot], ss, rs, 3 * self.n + r, other)

    @staticmethod
    def _others(x, y, c):
        return [(x, y, 1 - c), (1 - x, y, c), (1 - x, y, 1 - c), (x, 1 - y, c), (x, 1 - y, 1 - c),
                (1 - x, 1 - y, c), (1 - x, 1 - y, 1 - c)]

    def first(self, ins, outs, ss, rs):
        x, y, c, peers = _mesh_place()
        me = 2 * x + y
        for k in range(self.n):
            for j, (px, py) in enumerate(peers):
                self._piece(ins, outs, ss, rs, k, j, (px, py), 2 * px + py, me, c).start()
        if self.small:
            for r, other in enumerate(self._others(x, y, c)):
                self._small(ins, outs, ss, rs, r, other, 4 * x + 2 * y + c).start()
            outs[self.n][4 * x + 2 * y + c] = ins[self.n][...]
        for k in range(self.n):
            outs[k][me] = ins[k][me]

    def middle(self, ins, outs, ss, rs):
        pass

    def last(self, ins, outs, ss, rs):
        x, y, c, peers = _mesh_place()
        me = 2 * x + y
        for k in range(self.n):
            for j, (px, py) in enumerate(peers):
                self._piece(ins, outs, ss, rs, k, j, (px, py), me, 2 * px + py, c).wait_recv()
                self._piece(ins, outs, ss, rs, k, j, (px, py), 2 * px + py, me, c).wait_send()
        if self.small:
            for r, (px, py, pc) in enumerate(self._others(x, y, c)):
                self._small(ins, outs, ss, rs, r, (px, py, pc), 4 * px + 2 * py + pc).wait_recv()
                self._small(ins, outs, ss, rs, r, (px, py, pc), 4 * x + 2 * y + c).wait_send()


def _sum_swap(names, parts, small):
    n = len(parts)
    everyone = _SendPartials((), small.shape)

    def body(*refs):
        p_refs, (small_ref,), o_refs, (osmall_ref,), (all_ref,), (send_sems, recv_sems, ss_small, rs_small) = _split(
            refs, n, 1, n, 1, 1, 4)
        x, y, c = lax.axis_index("x"), lax.axis_index("y"), lax.axis_index("c")
        everyone.first([small_ref], [all_ref], ss_small, rs_small)

        def mine(k):
            part = _half(o_refs[k], c, names[k])
            return _remote(part, part, send_sems, recv_sems, k, (x, y, 1 - c))

        for k in range(n):
            for e in range(2):
                @pl.when(c == e)
                def _():
                    g = p_refs[k][0].astype(F32)
                    for s in range(1, N_CHIP):
                        g = g + p_refs[k][s].astype(F32)
                    r, cols = _half_shape(names[k])
                    if _BIG_SPLIT[names[k]] == 0:
                        o_refs[k][e * r:(e + 1) * r, :] = g
                    else:
                        o_refs[k][:, e * cols:(e + 1) * cols] = g
            mine(k).start()
        for k in range(n):
            theirs = _half(o_refs[k], 1 - c, names[k])
            _remote(theirs, theirs, send_sems, recv_sems, k, (x, y, 1 - c)).wait_recv()
            mine(k).wait_send()
        everyone.last([small_ref], [all_ref], ss_small, rs_small)
        g = all_ref[0]
        for d in range(1, N_DEV):
            g = g + all_ref[d]
        osmall_ref[...] = g

    res = pl.pallas_call(
        body, in_specs=[VMEM_WHOLE] * (n + 1), out_specs=[VMEM_WHOLE] * (n + 1),
        out_shape=[jax.ShapeDtypeStruct(_BIG_SHARD[nm], F32) for nm in names] + [jax.ShapeDtypeStruct(small.shape, F32)],
        scratch_shapes=[pltpu.VMEM((N_DEV,) + small.shape, F32), pltpu.SemaphoreType.DMA((n,)), pltpu.SemaphoreType.DMA((n,)),
                        pltpu.SemaphoreType.DMA((everyone.n_sems,)), pltpu.SemaphoreType.DMA((everyone.n_sems,))],
        compiler_params=_params(), name="sum_swap",
    )(*parts, small)
    return res[:n], res[n]


def _tile(rows, cols, itemsize, budget):
    t = cols if rows % 16 else rows
    other = rows if rows % 16 else cols
    step = 256 if rows % 16 else 32
    while t % step == 0 and t * other * itemsize > budget:
        t //= 2
    return (rows, t) if rows % 16 else (t, cols)


def _adamw_math(w, g, m, v):
    m = ADAM_B1 * m + (1.0 - ADAM_B1) * g
    v = ADAM_B2 * v + (1.0 - ADAM_B2) * (g * g)
    m_hat = m / (1.0 - ADAM_B1 ** ADAM_STEP)
    v_hat = v / (1.0 - ADAM_B2 ** ADAM_STEP)
    delta = -ADAM_LR * (m_hat / (jnp.sqrt(v_hat) + ADAM_EPS) + ADAM_WD * w)
    return delta, m, v


def _adamw_big(g, w, m, v, name):
    r, c = w.shape
    tr, tc = _tile(r, c, 4, 1024 * 1024)

    def body(g_ref, w_ref, m_ref, v_ref, d_ref, nm_ref, nv_ref):
        d_ref[...], nm_ref[...], nv_ref[...] = _adamw_math(w_ref[...], g_ref[...], m_ref[...], v_ref[...])

    blk = pl.BlockSpec((tr, tc), lambda i, l: (i, l))
    return pl.pallas_call(
        body, grid=(r // tr, c // tc), in_specs=[blk, blk, blk, blk],
        out_specs=[blk, blk, blk], out_shape=[jax.ShapeDtypeStruct((r, c), F32)] * 3,
        compiler_params=_params(("arbitrary", "arbitrary")), name=name,
    )(g, w, m, v)


def _adamw_rows(g, w, m, v, name):
    r, k, lanes = w.shape
    tr = 296

    def body(g_ref, w_ref, m_ref, v_ref, g3_ref, d_ref, nm_ref, nv_ref):
        g = g_ref[...].reshape(tr, k, lanes)
        g3_ref[...] = g
        d_ref[...], nm_ref[...], nv_ref[...] = _adamw_math(w_ref[...], g, m_ref[...], v_ref[...])

    rows = pl.BlockSpec((tr, k, lanes), lambda i: (i, 0, 0))
    return pl.pallas_call(
        body, grid=(pl.cdiv(r, tr),), in_specs=[pl.BlockSpec((tr, k * lanes), lambda i: (i, 0)), rows, rows, rows],
        out_specs=[rows] * 4, out_shape=[jax.ShapeDtypeStruct((r, k, lanes), F32)] * 4,
        compiler_params=_params(("arbitrary",)), name=name,
    )(g, w, m, v)


def _adamw_small(ws, gs, ms, vs):
    n = len(ws)

    def body(*refs):
        w_refs, g_refs, m_refs, v_refs, d_refs, nm_refs, nv_refs = _split(refs, *([n] * 7))
        for k in range(n):
            d_refs[k][...], nm_refs[k][...], nv_refs[k][...] = _adamw_math(w_refs[k][...], g_refs[k][...], m_refs[k][...],
                                                                             v_refs[k][...])

    shapes = [jax.ShapeDtypeStruct(w.shape, F32) for w in ws]
    res = pl.pallas_call(body, out_shape=shapes * 3, name="adamw_small")(*ws, *gs, *ms, *vs)
    return res[:n], res[n:2 * n], res[2 * n:]


def _pack(arrs):
    flat = jnp.concatenate([a.reshape(-1) for a in arrs])
    rows = -(-flat.shape[0] // 1024) * 8
    return jnp.pad(flat, (0, rows * 128 - flat.shape[0])).reshape(rows, 128)


def _unpack(buf, shapes):
    flat = buf.reshape(-1)
    out, off = [], 0
    for s in shapes:
        size = 1
        for d in s:
            size *= d
        out.append(flat[off:off + size].reshape(s))
        off += size
    return out


def _block_rows(w):
    return jnp.pad(w.reshape(512, 4), ((0, 0), (0, 124)))


def _cols(a4):
    return jnp.transpose(a4, (1, 0, 2)).reshape(a4.shape[1], -1)


_LATE = ("w_pa", "w_pb", "w_o", "w_up", "w_down")
_RIDE_IN_PROJ = ("w_pa", "w_pb", "w_o")
_RIDE_MIXER = ("w_up", "w_down_a")
_RIDE_MERGE = ("w_down_b",)


def _full_weights(gathered):
    joined = {"w_in": (D_IN, D_MODEL), "w_o": (D_MODEL, D_MODEL)}
    return {n: (a.reshape(joined[n]) if n in joined else a) for n, a in gathered.items()}


def _local_step(x, target, w, sp, late_shards=None):
    sp = {n: (a.reshape(1, -1) if a.ndim == 1 else a) for n, a in sp.items()}
    wau = jnp.zeros((128, 256), F32).at[0:16].set(sp["w_a_up"])
    wif = jnp.zeros((1536, 128), F32).at[:, 0:8].set(sp["w_if"])
    bif = jnp.zeros((1, 128), F32).at[:, 0:8].set(sp["b_if"])
    p = {"wau": wau, "bau": sp["b_a_up"], "ggla": sp["g_gla_norm"], "cw": sp["conv_w"], "cb": sp["conv_b"],
         "wq": _block_rows(sp["w_q_ml"]), "wk": _block_rows(sp["w_k_ml"]), "wv": _block_rows(sp["w_v_ml"]),
         "wif": wif, "bif": bif, "skip": sp["ml_skip"], "gml": sp["g_ml_norm"]}

    if late_shards is None:
        (pm, gab, h), _ = _in_proj(x, sp["g_pre_mix"], w["w_in"])
        ab, *states = _mixer_fwd(pm, p)
        (x1, mix, merged), _ = _merge_fwd(ab, gab, x, w["w_pa"], w["w_pb"], w["w_o"], sp["g_post_mix"])
    else:
        shard = dict(zip(_LATE, late_shards))
        shard["w_down_a"], shard["w_down_b"] = shard["w_down"][0:512], shard["w_down"][512:1024]
        (pm, gab, h), got = _in_proj(x, sp["g_pre_mix"], w["w_in"], _Gather(_RIDE_IN_PROJ, middle_at=0.45),
                                     [shard[n] for n in _RIDE_IN_PROJ])
        w = dict(w, **_full_weights(dict(zip(_RIDE_IN_PROJ, got))))
        ab, *rest = _mixer_fwd(pm, p, _Gather(_RIDE_MIXER, middle_at=0.62), [shard[n] for n in _RIDE_MIXER])
        states = rest[:4]
        w.update(_full_weights(dict(zip(_RIDE_MIXER, rest[4:]))))
        (x1, mix, merged), got = _merge_fwd(ab, gab, x, w["w_pa"], w["w_pb"], w["w_o"], sp["g_post_mix"],
                                            _Gather(_RIDE_MERGE, middle_at=0.46), [shard[n] for n in _RIDE_MERGE])
        w.update(_full_weights(dict(zip(_RIDE_MERGE, got))))
    dx1, u, dd, h2, dpre, dg_post_mlp, dg_pre_mlp, loss = _mlp(x1, target, sp["g_pre_mlp"], sp["g_post_mlp"],
                                                                w["w_up"], w["w_down_a"], w["w_down_b"])
    dmix, dya, dyb, dgab, dab, dg_post_mix = _merge_bwd(dx1, mix, ab, gab, w["w_pa"], w["w_pb"], w["w_o"], sp["g_post_mix"])
    big = {
        "w_pa": _tn_matmul(ab[:, 0:512], dya, "dw_pa", shards=N_CHIP),
        "w_pb": _tn_matmul(ab[:, 512:1024], dyb, "dw_pb", shards=N_CHIP),
        "w_o": _tn_matmul(merged, dmix, "dw_o"),
        "w_up": _tn_matmul(h2, dpre, "dw_up", shards=N_CHIP),
    }
    if late_shards is None:
        big["w_down"] = _tn_matmul(u, dd, "dw_down")
        dpm, dp, _ = _mixer_bwd(pm, dab, states, p)
    else:
        pieces = lambda n: big[n].reshape((N_CHIP,) + _BIG_SHARD[n])
        big["w_down"], partial = _tn_matmul(u, dd, "dw_down", rider=_Presum(_LATE[:4]),
                                            rider_ins=[pieces(n) for n in _LATE[:4]])
        partial = list(partial) + list(_presum(("w_down",), [pieces("w_down")], "presum_w_down"))
        dpm, dp, parts = _mixer_bwd(pm, dab, states, p, _SendPartials(_LATE), partial)
        big = dict(zip(_LATE, parts))
    big["w_in"] = _dw_in(dpm, dgab, h)
    if late_shards is None:
        (dx, dg_pre_mix), _ = _in_proj_bwd(dpm, dgab, x, dx1, sp["g_pre_mix"], w["w_in"])
    else:
        partial = _presum(("w_in",), [big["w_in"].reshape((N_CHIP,) + _BIG_SHARD["w_in"])], "presum_w_in")
        (dx, dg_pre_mix), parts = _in_proj_bwd(dpm, dgab, x, dx1, sp["g_pre_mix"], w["w_in"], _ReduceRelay(("w_in",)),
                                               partial)
        big["w_in"] = parts[0]
    small = {
        "g_pre_mix": dg_pre_mix, "b_a_up": dp["bau"], "g_gla_norm": dp["ggla"], "conv_b": dp["cb"],
        "w_q_ml": dp["wq"][:, 0:4].reshape(128, 4, 4), "w_k_ml": dp["wk"][:, 0:4].reshape(128, 4, 4),
        "w_v_ml": dp["wv"][:, 0:4].reshape(128, 4, 4),
        "b_if": dp["bif"][:, 0:8], "ml_skip": dp["skip"], "g_ml_norm": dp["gml"], "g_post_mix": dg_post_mix,
        "g_pre_mlp": dg_pre_mlp, "g_post_mlp": dg_post_mlp, "w_a_up": dp["wau"][0:16], "conv_w": dp["cw"],
        "w_if": dp["wif"][:, 0:8], "loss": loss[:, 0:1],
    }
    return dx, big, small


_SMALL_REPL = ("g_pre_mix", "b_a_up", "g_gla_norm", "conv_b", "w_q_ml", "w_k_ml", "w_v_ml", "b_if", "ml_skip",
               "g_ml_norm", "g_post_mix", "g_pre_mlp", "g_post_mlp")
_SMALL_SHARDED = ("w_a_up", "conv_w", "w_if")
_SMALL_ORDER = _SMALL_REPL + _SMALL_SHARDED + ("loss",)
_WEIGHTS = ("g_pre_mix", "w_in", "w_a_up", "b_a_up", "g_gla_norm", "conv_w", "conv_b", "w_q_ml", "w_k_ml", "w_v_ml",
            "w_if", "b_if", "ml_skip", "g_ml_norm", "w_pa", "w_pb", "w_o", "g_post_mix", "g_pre_mlp", "w_up", "w_down",
            "g_post_mlp")


_BLOCK_WEIGHTS = ("w_q_ml", "w_k_ml", "w_v_ml")


def _stored(name, a):
    if name in _BLOCK_WEIGHTS:
        return jnp.transpose(a, (0, 2, 3, 1)).reshape(16, 128)
    if name == "w_if":
        return jnp.transpose(a, (0, 2, 1)).reshape(8, 384)
    return a


def _unstored(name, a):
    if name in _BLOCK_WEIGHTS:
        return jnp.transpose(a.reshape(1, 4, 4, 128), (0, 3, 1, 2))
    if name == "w_if":
        return jnp.transpose(a.reshape(1, 8, 384), (0, 2, 1))
    return a


def _as_shard(name, a):
    return jnp.transpose(a, (2, 0, 1)).reshape(IN_SHARD, D_MODEL // 128, 128) if name == "w_in" else a[0]


def _from_shard(name, a):
    return jnp.transpose(a, (1, 2, 0)).reshape(1, D_MODEL, IN_SHARD) if name == "w_in" else a[None]


def kernel(x, g_pre_mix, w_in, w_a_up, b_a_up, g_gla_norm, conv_w, conv_b, w_q_ml, w_k_ml, w_v_ml, w_if, b_if, ml_skip, g_ml_norm, w_pa, w_pb, w_o, g_post_mix, g_pre_mlp, w_up, w_down, g_post_mlp, loss_target, m_g_pre_mix, m_w_in, m_w_a_up, m_b_a_up, m_g_gla_norm, m_conv_w, m_conv_b, m_w_q_ml, m_w_k_ml, m_w_v_ml, m_w_if, m_b_if, m_ml_skip, m_g_ml_norm, m_w_pa, m_w_pb, m_w_o, m_g_post_mix, m_g_pre_mlp, m_w_up, m_w_down, m_g_post_mlp, v_g_pre_mix, v_w_in, v_w_a_up, v_b_a_up, v_g_gla_norm, v_conv_w, v_conv_b, v_w_q_ml, v_w_k_ml, v_w_v_ml, v_w_if, v_b_if, v_ml_skip, v_g_ml_norm, v_w_pa, v_w_pb, v_w_o, v_g_post_mix, v_g_pre_mlp, v_w_up, v_w_down, v_g_post_mlp):
    args = dict(locals())
    wts = {n: _as_shard(n, args[n]) for n in _WEIGHTS}
    mom = {n: _as_shard(n, args["m_" + n]) for n in _WEIGHTS}
    var = {n: _as_shard(n, args["v_" + n]) for n in _WEIGHTS}
    chip = 2 * lax.axis_index("x") + lax.axis_index("y")

    first = ("w_in",) + _SMALL_SHARDED
    gathered = dict(zip(first, _run_alone(_Gather(("w_in",), [wts[n] for n in _SMALL_SHARDED]),
                                          [wts[n].reshape(IN_SHARD, D_MODEL).astype(BF16) if n == "w_in" else wts[n]
                                           for n in first],
                                          "gather_first")))
    sp = {n: wts[n] for n in _SMALL_REPL}
    sp["w_a_up"] = _cols(gathered["w_a_up"])
    sp["conv_w"] = _cols(gathered["conv_w"])
    sp["w_if"] = gathered["w_if"].reshape(1536, 8)

    dx, big, small = _local_step(x[0], loss_target[0], _full_weights({"w_in": gathered["w_in"]}), sp,
                                 late_shards=[wts[n].astype(BF16) for n in _LATE])

    small_shapes = [small[n].shape for n in _SMALL_ORDER]
    packed = _pack([small[n] for n in _SMALL_ORDER])
    sums, small_sum = _sum_swap(_BIG, [big[n] for n in _BIG], packed)

    grads, delta, new_m, new_v = {}, {}, {}, {}
    for n, g in zip(_BIG, sums):
        if n == "w_in":
            g, d, nm, nv = _adamw_rows(g, wts[n], mom[n], var[n], "adamw_" + n)
        else:
            d, nm, nv = _adamw_big(g, wts[n], mom[n], var[n], "adamw_" + n)
        grads[n], delta[n], new_m[n], new_v[n] = (_from_shard(n, a) for a in (g, d, nm, nv))
    summed = dict(zip(_SMALL_ORDER, _unpack(small_sum, small_shapes)))
    loss = summed["loss"].reshape(())
    summed["w_a_up"] = lax.dynamic_slice_in_dim(summed["w_a_up"], chip * 64, 64, axis=1)
    summed["conv_w"] = lax.dynamic_slice_in_dim(summed["conv_w"], chip * 128, 128, axis=1)
    summed["w_if"] = lax.dynamic_slice_in_dim(summed["w_if"], chip * 384, 384, axis=0)
    small_names = _SMALL_REPL + _SMALL_SHARDED
    g_stored = [_stored(n, summed[n].reshape(args[n].shape)) for n in small_names]
    upd = _adamw_small([_stored(n, args[n]) for n in small_names], g_stored,
                       [_stored(n, args["m_" + n]) for n in small_names], [_stored(n, args["v_" + n]) for n in small_names])
    for dst, arrs in zip((grads, delta, new_m, new_v), (g_stored,) + tuple(upd)):
        dst.update({n: _unstored(n, a) for n, a in zip(small_names, arrs)})

    outs = [loss, dx[None]]
    for group in (grads, delta, new_m, new_v):
        outs += [group[n] for n in _WEIGHTS]
    return tuple(outs)
```

```python
import functools

import jax
import jax.numpy as jnp
from jax import lax
from jax.experimental import pallas as pl
from jax.experimental.pallas import tpu as pltpu

F32 = jnp.float32
BF16 = jnp.bfloat16

SEQ = 2048
D_MODEL = 1024
CHUNK = 64
N_CHUNK = SEQ // CHUNK
HEADS = 4
GLA_DK = 64
GLA_DV = 128
ML_DH = 128
D_FF = 4096
EPS = 1e-6
N_CHIP = 4
N_DEV = 8
TOK_TILE = 256
N_TOK_TILE = SEQ // TOK_TILE
SWEEP = 2
assert CHUNK == 64
N_SWEEP = N_CHUNK // SWEEP

PM_W = 2688
PM_XM = 1536
PM_OP = 2048
PM_AL = 2560
GAB_W = 2048
D_IN = 4624
IN_SHARD = D_IN // N_CHIP
IN_ALOW = 1536
IN_XM = 1552
IN_GATES = 2576

ADAM_LR = 0.001
ADAM_B1 = 0.9
ADAM_B2 = 0.999
ADAM_EPS = 1e-08
ADAM_WD = 0.01
ADAM_STEP = 10

VMEM_LIMIT = 56 * 1024 * 1024


def _params(sem=None):
    return pltpu.CompilerParams(dimension_semantics=sem, vmem_limit_bytes=VMEM_LIMIT)


def _dot(a, b, ca, cb):
    return lax.dot_general(a.astype(BF16), b.astype(BF16), (((ca,), (cb,)), ((), ())), preferred_element_type=F32)


def _pmm_nn(a, b):
    return _dot(a, b, 1, 0)


def _pmm_nt(a, b):
    return _dot(a, b, 1, 1)


def _pmm_tn(a, b):
    return _dot(a, b, 0, 0)


def _pcmm(c, x):
    return lax.dot_general(c, x, (((1,), (0,)), ((), ())), precision=lax.Precision.HIGHEST, preferred_element_type=F32)


@jax.custom_vjp
def _mm_nn(a, b):
    return _dot(a, b, 1, 0)


@jax.custom_vjp
def _mm_nt(a, b):
    return _dot(a, b, 1, 1)


@jax.custom_vjp
def _mm_tn(a, b):
    return _dot(a, b, 0, 0)


_mm_nn.defvjp(lambda a, b: (_dot(a, b, 1, 0), (a, b)), lambda r, g: (_mm_nt(g, r[1]), _mm_tn(r[0], g)))
_mm_nt.defvjp(lambda a, b: (_dot(a, b, 1, 1), (a, b)), lambda r, g: (_mm_nn(g, r[1]), _mm_tn(g, r[0])))
_mm_tn.defvjp(lambda a, b: (_dot(a, b, 0, 0), (a, b)), lambda r, g: (_mm_nt(r[1], g), _mm_nn(r[0], g)))


@jax.custom_vjp
def _cmm(c, x):
    return _pcmm(c, x)


_cmm.defvjp(
    lambda c, x: (_pcmm(c, x), c),
    lambda c, g: (jnp.zeros_like(c), lax.dot_general(c, g, (((0,), (0,)), ((), ())), precision=lax.Precision.HIGHEST,
                                                      preferred_element_type=F32)),
)

_PLAIN_OPS = (_pmm_nn, _pmm_nt, _pmm_tn, _pcmm)
_VJP_OPS = (_mm_nn, _mm_nt, _mm_tn, _cmm)


def _sigmoid(x):
    return 0.5 * (jnp.tanh(0.5 * x) + 1.0)


def _log_sigmoid(x):
    return jnp.minimum(x, 0.0) - jnp.log(1.0 + jnp.exp(-jnp.abs(x)))


def _mean(x):
    return jnp.mean(x, axis=-1, keepdims=True)


def _nt(a, b):
    return lax.dot_general(a, b, (((1,), (1,)), ((), ())), preferred_element_type=F32)


def _tn(a, b):
    return lax.dot_general(a, b, (((0,), (0,)), ((), ())), preferred_element_type=F32)


def _mixer_chunk(ops, p, st, pm, xprev8):
    mm_nn, mm_nt, mm_tn, cmm = ops
    n_rows = pm.shape[0]
    n_ch = n_rows // CHUNK
    row = lax.broadcasted_iota(jnp.int32, (n_rows, n_rows), 0)
    col = lax.broadcasted_iota(jnp.int32, (n_rows, n_rows), 1)
    tri = jnp.logical_and((row >> 6) == (col >> 6), row >= col).astype(F32)
    causal = tri[0:CHUNK, 0:CHUNK] > 0.0
    q = pm[:, 0:256]
    k = pm[:, 256:512]
    v = pm[:, 512:1024]
    g = pm[:, 1024:1536]
    xm = pm[:, PM_XM:PM_XM + 512]
    opre = pm[:, PM_OP:PM_OP + 512]
    alow = pm[:, PM_AL:PM_AL + 128]
    hs = range(HEADS)
    cs = range(n_ch)
    pairs = [(i, h) for i in cs for h in hs]
    rs = [slice(i * CHUNK, (i + 1) * CHUNK) for i in cs]
    last = [slice((i + 1) * CHUNK - 1, (i + 1) * CHUNK) for i in cs]
    s6 = [slice(h * GLA_DK, (h + 1) * GLA_DK) for h in hs]
    s12 = [slice(h * 128, (h + 1) * 128) for h in hs]

    xx = jnp.concatenate([xprev8, xm], axis=0)
    pre = p["cb"]
    for j in range(4):
        pre = pre + p["cw"][j:j + 1, :] * xx[5 + j:5 + j + n_rows, :]
    xc = pre * _sigmoid(pre)
    qm = [mm_nn(xc[:, s12[h]], p["wq"][h]) for h in hs]
    km = [mm_nn(xc[:, s12[h]], p["wk"][h]) for h in hs]
    vm = [mm_nn(xm[:, s12[h]], p["wv"][h]) for h in hs]
    qcat = jnp.concatenate(qm, axis=1)
    kcat = jnp.concatenate(km, axis=1)
    vcat = jnp.concatenate(vm, axis=1)
    gates = (mm_nn(qcat, p["wif"][0:512]) + mm_nn(kcat, p["wif"][512:1024]) + mm_nn(vcat, p["wif"][1024:1536])
             + p["bif"])
    lf = _log_sigmoid(gates)
    fc = cmm(tri, lf)
    gates_t = gates.T
    fc_t = fc.T

    la = _log_sigmoid(mm_nn(alow, p["wau"]) + p["bau"]) * (1.0 / 16.0)
    cum = cmm(tri, la)
    cum_last = [cum[last[i], :] for i in cs]
    to_end = jnp.concatenate([cum_last[i] - cum[rs[i], :] for i in cs], axis=0)
    e_pos = jnp.exp(cum)
    e_neg = jnp.exp(-cum)
    qs = q * (GLA_DK ** -0.5)
    qp = qs * e_pos
    qn = qs * e_neg
    kp = k * e_pos
    kn = k * e_neg
    kl = k * jnp.exp(to_end)
    dec = [jnp.exp(cum_last[i]) for i in cs]
    ks = [km[h] * (ML_DH ** -0.5) for h in hs]
    li_c = {(i, h): gates[rs[i], h:h + 1] for i, h in pairs}
    fc_c = {(i, h): fc[rs[i], 4 + h:5 + h] for i, h in pairs}
    f_last = {(i, h): fc[last[i], 4 + h:5 + h] for i, h in pairs}

    a_fwd = {(i, h): mm_nt(qp[rs[i], s6[h]], kn[rs[i], s6[h]]) for i, h in pairs}
    a_bwd = {(i, h): mm_nt(qn[rs[i], s6[h]], kp[rs[i], s6[h]]) for i, h in pairs}
    s_chunk = {(i, h): mm_tn(v[rs[i], s12[h]], kl[rs[i], s6[h]]) for i, h in pairs}
    qk = {(i, h): mm_nt(qm[h][rs[i]], ks[h][rs[i]]) for i, h in pairs}
    a = {ih: f_last[ih] - fc_c[ih] + li_c[ih] for ih in pairs}
    m_loc = {ih: jnp.max(a[ih], axis=0, keepdims=True) for ih in pairs}
    kw = {(i, h): ks[h][rs[i]] * jnp.exp(a[(i, h)] - m_loc[(i, h)]) for i, h in pairs}
    c_chunk = {(i, h): mm_tn(kw[(i, h)], vm[h][rs[i]]) for i, h in pairs}
    mem = {(0, h): st["S"][h] for h in hs}
    c_in = {(0, h): st["C"][h] for h in hs}
    n_in = {(0, h): st["n"][h] for h in hs}
    m_in = {(0, h): st["m"][h][:, 0:1] for h in hs}
    for i, h in pairs:
        mem[(i + 1, h)] = mem[(i, h)] * dec[i][:, s6[h]] + s_chunk[(i, h)]
        m_nx = jnp.maximum(f_last[(i, h)] + m_in[(i, h)], m_loc[(i, h)])
        sp = jnp.exp(f_last[(i, h)] + m_in[(i, h)] - m_nx)
        sl = jnp.exp(m_loc[(i, h)] - m_nx)
        c_in[(i + 1, h)] = sp * c_in[(i, h)] + sl * c_chunk[(i, h)]
        n_in[(i + 1, h)] = sp * n_in[(i, h)] + sl * jnp.sum(kw[(i, h)], axis=0, keepdims=True)
        m_in[(i + 1, h)] = m_nx
    s_new = [mem[(n_ch, h)] for h in hs]
    o_inter = {(i, h): mm_nt(qp[rs[i], s6[h]], mem[(i, h)]) for i, h in pairs}
    q_c = {(i, h): mm_nn(qm[h][rs[i]], c_in[(i, h)]) for i, h in pairs}
    scores = {ih: jnp.where(causal, a_fwd[ih], a_bwd[ih]) for ih in pairs}
    log_d = {(i, h): gates_t[h:h + 1, rs[i]] - jnp.abs(fc_c[(i, h)] - fc_t[4 + h:5 + h, rs[i]]) for i, h in pairs}
    g_int = {ih: fc_c[ih] + m_in[ih] for ih in pairs}
    m_t = {ih: jnp.maximum(g_int[ih], jnp.max(log_d[ih], axis=1, keepdims=True)) for ih in pairs}
    s = {ih: qk[ih] * jnp.exp(log_d[ih] - m_t[ih]) for ih in pairs}
    scl = {ih: jnp.exp(g_int[ih] - m_t[ih]) for ih in pairs}
    o = {(i, h): mm_nn(scores[(i, h)], v[rs[i], s12[h]]) + o_inter[(i, h)] for i, h in pairs}
    num = {(i, h): mm_nn(s[(i, h)], vm[h][rs[i]]) + scl[(i, h)] * q_c[(i, h)] for i, h in pairs}
    o = {ih: o[ih] * lax.rsqrt(_mean(o[ih] * o[ih]) + EPS) * p["ggla"] for ih in pairs}
    gate = g * _sigmoid(g)
    out_a = {(i, h): o[(i, h)] * gate[rs[i], s12[h]] for i, h in pairs}
    den = {(i, h): jnp.sum(s[(i, h)], axis=1, keepdims=True)
           + scl[(i, h)] * jnp.sum(qm[h][rs[i]] * n_in[(i, h)], axis=1, keepdims=True) for i, h in pairs}
    den = {ih: jnp.maximum(jnp.abs(den[ih]), jnp.exp(-m_t[ih])) for ih in pairs}
    open_gate = _sigmoid(opre)
    hc = {(i, h): num[(i, h)] / den[(i, h)] * open_gate[rs[i], s12[h]] for i, h in pairs}
    d0 = {ih: hc[ih] - _mean(hc[ih]) for ih in pairs}
    y = {ih: d0[ih] * lax.rsqrt(_mean(d0[ih] * d0[ih]) + EPS) for ih in pairs}
    skipped = p["skip"] * xc
    out_b = {(i, h): y[(i, h)] * p["gml"][:, s12[h]] + skipped[rs[i], s12[h]] for i, h in pairs}
    ab = jnp.concatenate([jnp.concatenate([out_a[(i, h)] for h in hs] + [out_b[(i, h)] for h in hs], axis=1) for i in cs],
                         axis=0)
    new = {"S": s_new, "C": [c_in[(n_ch, h)] for h in hs], "n": [n_in[(n_ch, h)] for h in hs],
           "m": [jnp.broadcast_to(m_in[(n_ch, h)], (1, ML_DH)) for h in hs]}
    return ab, new


_P_NAMES = ("wau", "bau", "ggla", "cw", "cb", "wq", "wk", "wv", "wif", "bif", "skip", "gml")
_P_SHAPES = {
    "wau": (128, 256), "bau": (1, 256), "ggla": (1, 128), "cw": (4, 512), "cb": (1, 512),
    "wq": (512, 128), "wk": (512, 128), "wv": (512, 128),
    "wif": (1536, 128), "bif": (1, 128), "skip": (1, 512), "gml": (1, 512),
}
_P_BLOCKDIAG = ("wq", "wk", "wv")
_S_NAMES = ("S", "C", "n", "m")
_S_SHAPES = {"S": (HEADS, GLA_DV, GLA_DK), "C": (HEADS, ML_DH, ML_DH), "n": (HEADS, 1, ML_DH), "m": (HEADS, 1, ML_DH)}


def _per_head(ref):
    return [ref[h] for h in range(HEADS)]


def _block_mask():
    r = lax.broadcasted_iota(jnp.int32, (128, 128), 0)
    c = lax.broadcasted_iota(jnp.int32, (128, 128), 1)
    same_block = (r >> 2) == (c >> 2)
    spread = jnp.logical_and(r < 4, (c & 3) == r)
    return same_block.astype(F32), spread.astype(F32)


def _expand_blockdiag(w_ref, dense_ref):
    same_block, spread = _block_mask()
    for h in range(HEADS):
        tiled = _pmm_nn(w_ref[h * 128:(h + 1) * 128, :], spread)
        dense_ref[h] = tiled * same_block


def _collect_blockdiag(ddense_ref, dw_ref):
    same_block, spread = _block_mask()
    for h in range(HEADS):
        dw_ref[h * 128:(h + 1) * 128, :] = lax.dot_general(
            ddense_ref[h] * same_block, spread, (((1,), (1,)), ((), ())), precision=lax.Precision.HIGHEST,
            preferred_element_type=F32)


def _const_spec(shape):
    zeros = (0,) * len(shape)
    return pl.BlockSpec(shape, lambda i: zeros)


def _split(refs, *counts):
    out, at = [], 0
    for c in counts:
        out.append(refs[at:at + c])
        at += c
    assert at == len(refs)
    return out


def _ride(rider, phases, cond, ins, outs, sems):
    if rider is None:
        return
    lands, (send_sems, recv_sems, flush_sems) = sems[:-3], sems[-3:]

    @pl.when(cond)
    def _():
        for phase in phases:
            getattr(rider, phase)(ins, lands, send_sems, recv_sems)
        if "last" in phases:
            flush = [pltpu.make_async_copy(lands[k], outs[k], flush_sems.at[k]) for k in range(len(outs))]
            for cp in flush:
                cp.start()
            for cp in flush:
                cp.wait()


def _middle_step(rider, n_steps):
    return min(n_steps - 2, int(getattr(rider, "middle_at", 1.0) * n_steps))


def _rider_specs(rider, rider_ins):
    if rider is None:
        return [], [], [], []
    scratch = [pltpu.VMEM(s.shape, s.dtype) for s in list(rider.out_shape) + list(getattr(rider, "work_shape", ()))]
    scratch += [pltpu.SemaphoreType.DMA((rider.n_sems,)), pltpu.SemaphoreType.DMA((rider.n_sems,)),
                pltpu.SemaphoreType.DMA((len(rider.out_shape),))]
    in_space = getattr(rider, "in_space", VMEM_WHOLE)
    return [in_space] * len(rider_ins), [ANY] * len(rider.out_shape), list(rider.out_shape), scratch


def _mixer_fwd(pm, p, rider=None, rider_ins=()):
    n_p = len(_P_NAMES)
    r_in, r_out_specs, r_out_shape, r_sems = _rider_specs(rider, rider_ins)

    def body(*refs):
        (pm_ref, xprev_ref), p_list, ride_in, (ab_ref,), so_refs, ride_out, sc_refs, dense_list, sems = _split(
            refs, 2, n_p, len(r_in), 1, 4, len(r_out_specs), 4, 3, len(r_sems))
        p_refs = dict(zip(_P_NAMES, p_list))
        dense = dict(zip(_P_BLOCKDIAG, dense_list))
        n = pl.program_id(0)
        _ride(rider, ("first",), n == 0, ride_in, ride_out, sems)

        @pl.when(n == 0)
        def _():
            for r in sc_refs:
                r[...] = jnp.zeros_like(r)
            for nm in _P_BLOCKDIAG:
                _expand_blockdiag(p_refs[nm], dense[nm])

        st = {name: _per_head(r) for name, r in zip(_S_NAMES, sc_refs)}
        pv = {nm: (_per_head(dense[nm]) if nm in _P_BLOCKDIAG else p_refs[nm][...]) for nm in _P_NAMES}
        for name, r in zip(_S_NAMES, so_refs):
            for h in range(HEADS):
                r[0, h] = st[name][h]
        xprev8 = jnp.where(n > 0, xprev_ref[CHUNK - 8:CHUNK, :], 0.0)
        ab, st = _mixer_chunk(_PLAIN_OPS, pv, st, pm_ref[...], xprev8)
        ab_ref[...] = ab.astype(BF16)
        for name, r in zip(_S_NAMES, sc_refs):
            for h in range(HEADS):
                r[h] = st[name][h]
        _ride(rider, ("middle",), n == _middle_step(rider, N_SWEEP), ride_in, ride_out, sems)
        _ride(rider, ("last",), n == N_SWEEP - 1, ride_in, ride_out, sems)

    in_specs = [pl.BlockSpec((SWEEP * CHUNK, PM_W), lambda i: (i, 0)),
                pl.BlockSpec((CHUNK, 512), lambda i: (jnp.maximum(SWEEP * i - 1, 0), PM_XM // 512))]
    in_specs += [_const_spec(_P_SHAPES[nm]) for nm in _P_NAMES] + r_in
    out_specs = [pl.BlockSpec((SWEEP * CHUNK, 1024), lambda i: (i, 0))]
    out_shape = [jax.ShapeDtypeStruct((SEQ, 1024), BF16)]
    for nm in _S_NAMES:
        shp = _S_SHAPES[nm]
        out_specs.append(pl.BlockSpec((1,) + shp, lambda i: (i, 0, 0, 0)))
        out_shape.append(jax.ShapeDtypeStruct((N_SWEEP,) + shp, F32))
    return pl.pallas_call(
        body, grid=(N_SWEEP,), in_specs=in_specs, out_specs=out_specs + r_out_specs, out_shape=out_shape + r_out_shape,
        scratch_shapes=[pltpu.VMEM(_S_SHAPES[nm], F32) for nm in _S_NAMES]
        + [pltpu.VMEM((HEADS, 128, 128), F32) for _ in _P_BLOCKDIAG] + r_sems,
        compiler_params=_params(("arbitrary",)), name="mixer_fwd",
    )(pm, pm, *[p[nm] for nm in _P_NAMES], *rider_ins)


def _mixer_bwd(pm, dab, states, p, rider=None, rider_ins=()):
    n_p = len(_P_NAMES)
    r_in, r_out_specs, r_out_shape, r_sems = _rider_specs(rider, rider_ins)

    def body(*refs):
        ((pm_ref, xprev_ref, dab_ref), si_refs, p_list, ride_in, (dpm_ref,), dp_list, ride_out, ds_refs, (carry_ref,),
         dense_list, ddense_list, sems) = _split(refs, 3, 4, n_p, len(r_in), 1, n_p, len(r_out_specs), 4, 1, 3, 3, len(r_sems))
        p_refs = dict(zip(_P_NAMES, p_list))
        dp_refs = dict(zip(_P_NAMES, dp_list))
        dense = dict(zip(_P_BLOCKDIAG, dense_list))
        ddense = dict(zip(_P_BLOCKDIAG, ddense_list))
        i = pl.program_id(0)
        blk = N_SWEEP - 1 - i
        _ride(rider, ("first",), i == 0, ride_in, ride_out, sems)

        @pl.when(i == 0)
        def _():
            for r in ds_refs:
                r[...] = jnp.zeros_like(r)
            for nm in _P_NAMES:
                if nm in _P_BLOCKDIAG:
                    ddense[nm][...] = jnp.zeros_like(ddense[nm])
                    _expand_blockdiag(p_refs[nm], dense[nm])
                else:
                    dp_refs[nm][...] = jnp.zeros_like(dp_refs[nm])
            carry_ref[...] = jnp.zeros_like(carry_ref)

        pv = {nm: (_per_head(dense[nm]) if nm in _P_BLOCKDIAG else p_refs[nm][...]) for nm in _P_NAMES}
        dst = {name: _per_head(r) for name, r in zip(_S_NAMES, ds_refs)}
        st = {name: [r[0, h] for h in range(HEADS)] for name, r in zip(_S_NAMES, si_refs)}
        xprev8 = jnp.where(blk > 0, xprev_ref[CHUNK - 8:CHUNK, :], 0.0)
        _, vjp = jax.vjp(functools.partial(_mixer_chunk, _VJP_OPS), pv, st, pm_ref[...], xprev8)
        dp_sum, dst, dpm, dxprev8 = vjp((dab_ref[...], dst))
        reach = jnp.concatenate([jnp.zeros((SWEEP * CHUNK - 8, 512), F32), carry_ref[...]], axis=0)
        dpm_ref[:, 0:PM_XM] = dpm[:, 0:PM_XM].astype(BF16)
        dpm_ref[:, PM_XM:PM_XM + 512] = (dpm[:, PM_XM:PM_XM + 512] + reach).astype(BF16)
        dpm_ref[:, PM_XM + 512:PM_W] = dpm[:, PM_XM + 512:PM_W].astype(BF16)
        carry_ref[...] = dxprev8
        for name, r in zip(_S_NAMES, ds_refs):
            for h in range(HEADS):
                r[h] = dst[name][h]
        for nm in _P_NAMES:
            if nm in _P_BLOCKDIAG:
                for h in range(HEADS):
                    ddense[nm][h] += dp_sum[nm][h]
            else:
                dp_refs[nm][...] += dp_sum[nm]

        @pl.when(i == N_SWEEP - 1)
        def _():
            for nm in _P_BLOCKDIAG:
                _collect_blockdiag(ddense[nm], dp_refs[nm])

        _ride(rider, ("middle",), i == _middle_step(rider, N_SWEEP), ride_in, ride_out, sems)
        _ride(rider, ("last",), i == N_SWEEP - 1, ride_in, ride_out, sems)

    rev = lambda i: (N_SWEEP - 1 - i, 0)
    in_specs = [pl.BlockSpec((SWEEP * CHUNK, PM_W), rev),
                pl.BlockSpec((CHUNK, 512), lambda i: (jnp.maximum(SWEEP * (N_SWEEP - 1 - i) - 1, 0), PM_XM // 512)),
                pl.BlockSpec((SWEEP * CHUNK, 1024), rev)]
    for nm in _S_NAMES:
        in_specs.append(pl.BlockSpec((1,) + _S_SHAPES[nm], lambda i: (N_SWEEP - 1 - i, 0, 0, 0)))
    in_specs += [_const_spec(_P_SHAPES[nm]) for nm in _P_NAMES] + r_in
    out_specs = [pl.BlockSpec((SWEEP * CHUNK, PM_W), rev)] + [_const_spec(_P_SHAPES[nm]) for nm in _P_NAMES]
    out_shape = [jax.ShapeDtypeStruct((SEQ, PM_W), BF16)] + [jax.ShapeDtypeStruct(_P_SHAPES[nm], F32) for nm in _P_NAMES]
    res = pl.pallas_call(
        body, grid=(N_SWEEP,), in_specs=in_specs, out_specs=out_specs + r_out_specs, out_shape=out_shape + r_out_shape,
        scratch_shapes=[pltpu.VMEM(_S_SHAPES[nm], F32) for nm in _S_NAMES] + [pltpu.VMEM((8, 512), F32)]
        + [pltpu.VMEM((HEADS, 128, 128), F32) for _ in range(2 * len(_P_BLOCKDIAG))] + r_sems,
        compiler_params=_params(("arbitrary",)), name="mixer_bwd",
    )(pm, pm, dab, *states, *[p[nm] for nm in _P_NAMES], *rider_ins)
    return res[0], dict(zip(_P_NAMES, res[1:1 + n_p])), res[1 + n_p:]


def _tok(width):
    return pl.BlockSpec((TOK_TILE, width), lambda i: (i, 0))


def _once(shape):
    zeros = (0,) * len(shape)
    return pl.BlockSpec(shape, lambda i: zeros, pipeline_mode=pl.Buffered(1))


def _rms_fwd(x):
    r = lax.rsqrt(_mean(x * x) + EPS)
    return x * r, r


def _rms_bwd(dy, xn, r, g):
    gd = dy * g
    return r * (gd - xn * _mean(xn * gd))


def _tiled_call(body, in_specs, out_specs, out_shape, args, name, rider=None, rider_ins=()):
    r_in, r_out_specs, r_out_shape, r_scratch = _rider_specs(rider, rider_ins)
    n_in, n_out = len(in_specs), len(out_specs)

    def hosted(*refs):
        ins, ride_in, outs, ride_out, scratch = _split(refs, n_in, len(r_in), n_out, len(r_out_specs), len(r_scratch))
        i = pl.program_id(0)
        _ride(rider, ("first",), i == 0, ride_in, ride_out, scratch)
        body(*ins, *outs)
        _ride(rider, ("middle",), i == _middle_step(rider, N_TOK_TILE), ride_in, ride_out, scratch)
        _ride(rider, ("last",), i == N_TOK_TILE - 1, ride_in, ride_out, scratch)

    res = pl.pallas_call(
        hosted, grid=(N_TOK_TILE,), in_specs=list(in_specs) + r_in, out_specs=list(out_specs) + r_out_specs,
        out_shape=list(out_shape) + r_out_shape, scratch_shapes=r_scratch,
        compiler_params=_params(("arbitrary",)), name=name,
    )(*args, *rider_ins)
    return res[:n_out], res[n_out:]


def _in_proj(x, g_pre, wt_in, rider=None, rider_ins=()):
    def body(x_ref, g_ref, wt_ref, pm_ref, gab_ref, h_ref):
        xn, _ = _rms_fwd(x_ref[...])
        h = (xn * g_ref[...]).astype(BF16)
        h_ref[...] = h
        pm_ref[:, 0:PM_XM] = _nt(h, wt_ref[0:IN_ALOW, :])
        pm_ref[:, PM_XM:PM_AL] = _nt(h, wt_ref[IN_XM:IN_GATES, :])
        pm_ref[:, PM_AL:PM_W] = _nt(h, wt_ref[IN_ALOW:IN_ALOW + 128, :])
        gab_ref[...] = _nt(h, wt_ref[IN_GATES:D_IN, :])

    return _tiled_call(
        body, [_tok(D_MODEL), _once((1, D_MODEL)), _once((D_IN, D_MODEL))], [_tok(PM_W), _tok(GAB_W), _tok(D_MODEL)],
        [jax.ShapeDtypeStruct((SEQ, PM_W), F32), jax.ShapeDtypeStruct((SEQ, GAB_W), F32),
         jax.ShapeDtypeStruct((SEQ, D_MODEL), BF16)], (x, g_pre, wt_in), "in_proj", rider, rider_ins)


def _merge_fwd(ab, gab, x, w_pa4, w_pb4, w_o, g_post, rider=None, rider_ins=()):
    def body(ab_ref, gab_ref, x_ref, wpa_ref, wpb_ref, wo_ref, g_ref, x1_ref, mix_ref, mg_ref):
        a = ab_ref[:, 0:512]
        b = ab_ref[:, 512:1024]
        for j in range(N_CHIP):
            blk = slice(j * 256, (j + 1) * 256)
            ya = jnp.dot(a, wpa_ref[j], preferred_element_type=F32)
            yb = jnp.dot(b, wpb_ref[j], preferred_element_type=F32)
            sa = _sigmoid(gab_ref[:, j * 256:(j + 1) * 256])
            sb = _sigmoid(gab_ref[:, 1024 + j * 256:1024 + (j + 1) * 256])
            mg_ref[:, blk] = (sa * ya + sb * yb).astype(BF16)
        mix = jnp.dot(mg_ref[...], wo_ref[...], preferred_element_type=F32)
        mix_ref[...] = mix
        mn, _ = _rms_fwd(mix)
        x1_ref[...] = x_ref[...] + mn * g_ref[...]

    return _tiled_call(
        body, [_tok(1024), _tok(GAB_W), _tok(D_MODEL), _once((N_CHIP, 512, 256)), _once((N_CHIP, 512, 256)),
               _once((D_MODEL, D_MODEL)), _once((1, D_MODEL))], [_tok(D_MODEL), _tok(D_MODEL), _tok(D_MODEL)],
        [jax.ShapeDtypeStruct((SEQ, D_MODEL), F32), jax.ShapeDtypeStruct((SEQ, D_MODEL), F32),
         jax.ShapeDtypeStruct((SEQ, D_MODEL), BF16)], (ab, gab, x, w_pa4, w_pb4, w_o, g_post), "merge_fwd", rider, rider_ins)


def _mlp(x1, target, g_pre, g_post, w_up4, w_down_a4, w_down_b4):
    def body(x1_ref, t_ref, gpre_ref, gpost_ref, wup_ref, wda_ref, wdb_ref,
             dx1_ref, u_ref, dd_ref, h2_ref, dpre_ref, dgpost_ref, dgpre_ref, loss_ref):
        @pl.when(pl.program_id(0) == 0)
        def _():
            dgpost_ref[...] = jnp.zeros_like(dgpost_ref)
            dgpre_ref[...] = jnp.zeros_like(dgpre_ref)
            loss_ref[...] = jnp.zeros_like(loss_ref)

        x1 = x1_ref[...]
        gpre = gpre_ref[...]
        gpost = gpost_ref[...]
        xn2, r2 = _rms_fwd(x1)
        h2 = (xn2 * gpre).astype(BF16)
        h2_ref[...] = h2
        rl = []
        d = jnp.zeros((TOK_TILE, D_MODEL), F32)
        for j in range(N_CHIP):
            blk = slice(j * 1024, (j + 1) * 1024)
            r = jnp.maximum(jnp.dot(h2, wup_ref[j], preferred_element_type=F32), 0.0)
            rl.append(r)
            u = (r * r).astype(BF16)
            u_ref[:, blk] = u
            d = d + jnp.dot(u[:, 0:512], wda_ref[j], preferred_element_type=F32)
            d = d + jnp.dot(u[:, 512:1024], wdb_ref[j], preferred_element_type=F32)
        dn, r3 = _rms_fwd(d)
        diff = x1 + dn * gpost - t_ref[...]
        loss_ref[...] += jnp.sum(diff * diff, keepdims=True) * (0.5 / D_MODEL)
        dy = diff * (1.0 / D_MODEL)
        dgpost_ref[...] += jnp.sum(dy * dn, axis=0, keepdims=True)
        dd = _rms_bwd(dy, dn, r3, gpost).astype(BF16)
        dd_ref[...] = dd
        dh2 = jnp.zeros((TOK_TILE, D_MODEL), F32)
        for j in range(N_CHIP):
            blk = slice(j * 1024, (j + 1) * 1024)
            du = jnp.concatenate([_nt(dd, wda_ref[j]), _nt(dd, wdb_ref[j])], axis=1)
            dpre = (du * (2.0 * rl[j])).astype(BF16)
            dpre_ref[:, blk] = dpre
            dh2 = dh2 + _nt(dpre, wup_ref[j])
        dgpre_ref[...] += jnp.sum(dh2 * xn2, axis=0, keepdims=True)
        dx1_ref[...] = dy + _rms_bwd(dh2, xn2, r2, gpre)

    acc = pl.BlockSpec((1, D_MODEL), lambda i: (0, 0))
    return pl.pallas_call(
        body, grid=(N_TOK_TILE,),
        in_specs=[_tok(D_MODEL), _tok(D_MODEL), _once((1, D_MODEL)), _once((1, D_MODEL)),
                  _once((N_CHIP, D_MODEL, 1024)), _once((N_CHIP, 512, D_MODEL)), _once((N_CHIP, 512, D_MODEL))],
        out_specs=[_tok(D_MODEL), _tok(D_FF), _tok(D_MODEL), _tok(D_MODEL), _tok(D_FF), acc, acc,
                   pl.BlockSpec((1, 128), lambda i: (0, 0))],
        out_shape=[jax.ShapeDtypeStruct((SEQ, D_MODEL), F32), jax.ShapeDtypeStruct((SEQ, D_FF), BF16),
                   jax.ShapeDtypeStruct((SEQ, D_MODEL), BF16), jax.ShapeDtypeStruct((SEQ, D_MODEL), BF16),
                   jax.ShapeDtypeStruct((SEQ, D_FF), BF16), jax.ShapeDtypeStruct((1, D_MODEL), F32),
                   jax.ShapeDtypeStruct((1, D_MODEL), F32), jax.ShapeDtypeStruct((1, 128), F32)],
        compiler_params=_params(("arbitrary",)), name="mlp_fwd_bwd",
    )(x1, target, g_pre, g_post, w_up4, w_down_a4, w_down_b4)


def _merge_bwd(dx1, mix, ab, gab, w_pa4, w_pb4, w_o, g_post):
    def body(dx1_ref, mix_ref, ab_ref, gab_ref, wpa_ref, wpb_ref, wo_ref, g_ref,
             dmix_ref, dya_ref, dyb_ref, dgab_ref, dab_ref, dg_ref):
        @pl.when(pl.program_id(0) == 0)
        def _():
            dg_ref[...] = jnp.zeros_like(dg_ref)

        dx1 = dx1_ref[...]
        mn, r = _rms_fwd(mix_ref[...])
        dg_ref[...] += jnp.sum(dx1 * mn, axis=0, keepdims=True)
        dmix = _rms_bwd(dx1, mn, r, g_ref[...]).astype(BF16)
        dmix_ref[...] = dmix
        dmerged = _nt(dmix, wo_ref[...])
        a = ab_ref[:, 0:512]
        b = ab_ref[:, 512:1024]
        da = jnp.zeros((TOK_TILE, 512), F32)
        db = jnp.zeros((TOK_TILE, 512), F32)
        for j in range(N_CHIP):
            blk = slice(j * 256, (j + 1) * 256)
            blk_b = slice(1024 + j * 256, 1024 + (j + 1) * 256)
            dm = dmerged[:, blk]
            ya = jnp.dot(a, wpa_ref[j], preferred_element_type=F32)
            yb = jnp.dot(b, wpb_ref[j], preferred_element_type=F32)
            sa = _sigmoid(gab_ref[:, blk])
            sb = _sigmoid(gab_ref[:, blk_b])
            dya = (dm * sa).astype(BF16)
            dyb = (dm * sb).astype(BF16)
            dya_ref[:, blk] = dya
            dyb_ref[:, blk] = dyb
            dgab_ref[:, blk] = (dm * ya * sa * (1.0 - sa)).astype(BF16)
            dgab_ref[:, blk_b] = (dm * yb * sb * (1.0 - sb)).astype(BF16)
            da = da + _nt(dya, wpa_ref[j])
            db = db + _nt(dyb, wpb_ref[j])
        dab_ref[:, 0:512] = da
        dab_ref[:, 512:1024] = db

    return pl.pallas_call(
        body, grid=(N_TOK_TILE,),
        in_specs=[_tok(D_MODEL), _tok(D_MODEL), _tok(1024), _tok(GAB_W), _once((N_CHIP, 512, 256)),
                  _once((N_CHIP, 512, 256)), _once((D_MODEL, D_MODEL)), _once((1, D_MODEL))],
        out_specs=[_tok(D_MODEL), _tok(D_MODEL), _tok(D_MODEL), _tok(GAB_W), _tok(1024),
                   pl.BlockSpec((1, D_MODEL), lambda i: (0, 0))],
        out_shape=[jax.ShapeDtypeStruct((SEQ, D_MODEL), BF16), jax.ShapeDtypeStruct((SEQ, D_MODEL), BF16),
                   jax.ShapeDtypeStruct((SEQ, D_MODEL), BF16), jax.ShapeDtypeStruct((SEQ, GAB_W), BF16),
                   jax.ShapeDtypeStruct((SEQ, 1024), F32), jax.ShapeDtypeStruct((1, D_MODEL), F32)],
        compiler_params=_params(("arbitrary",)), name="merge_bwd",
    )(dx1, mix, ab, gab, w_pa4, w_pb4, w_o, g_post)


def _in_proj_bwd(dpm, dgab, x, dx1, g_pre, wt_in, rider=None, rider_ins=()):
    def body(dpm_ref, dgab_ref, x_ref, dx1_ref, g_ref, wt_ref, dx_ref, dg_ref):
        @pl.when(pl.program_id(0) == 0)
        def _():
            dg_ref[...] = jnp.zeros_like(dg_ref)

        dh = jnp.dot(dpm_ref[:, 0:PM_XM], wt_ref[0:IN_ALOW, :], preferred_element_type=F32)
        dh = dh + jnp.dot(dpm_ref[:, PM_XM:PM_AL], wt_ref[IN_XM:IN_GATES, :], preferred_element_type=F32)
        dh = dh + jnp.dot(dpm_ref[:, PM_AL:PM_W], wt_ref[IN_ALOW:IN_ALOW + 128, :], preferred_element_type=F32)
        dh = dh + jnp.dot(dgab_ref[...], wt_ref[IN_GATES:D_IN, :], preferred_element_type=F32)
        xn, r = _rms_fwd(x_ref[...])
        dg_ref[...] += jnp.sum(dh * xn, axis=0, keepdims=True)
        dx_ref[...] = dx1_ref[...] + _rms_bwd(dh, xn, r, g_ref[...])

    return _tiled_call(
        body, [_tok(PM_W), _tok(GAB_W), _tok(D_MODEL), _tok(D_MODEL), _once((1, D_MODEL)), _once((D_IN, D_MODEL))],
        [_tok(D_MODEL), pl.BlockSpec((1, D_MODEL), lambda i: (0, 0))],
        [jax.ShapeDtypeStruct((SEQ, D_MODEL), F32), jax.ShapeDtypeStruct((1, D_MODEL), F32)],
        (dpm, dgab, x, dx1, g_pre, wt_in), "in_proj_bwd", rider, rider_ins)


def _dw_in(dpm, dgab, h):
    n_pm = PM_AL // 512
    n_blk = n_pm + GAB_W // 512

    def body(dpm_ref, dgab_ref, dal_ref, h_ref, o_ref):
        i = pl.program_id(0)
        off = pl.multiple_of(i * 512 + 16 * (i >= 3).astype(jnp.int32), 16)

        @pl.when(i < n_pm)
        def _():
            o_ref[pl.ds(off, 512), :] = _tn(dpm_ref[...], h_ref[...]).astype(BF16)

        @pl.when(i >= n_pm)
        def _():
            o_ref[pl.ds(off, 512), :] = _tn(dgab_ref[...], h_ref[...]).astype(BF16)

        @pl.when(i == 0)
        def _():
            o_ref[IN_ALOW:IN_XM, :] = _tn(dal_ref[...], h_ref[...])[0:IN_XM - IN_ALOW].astype(BF16)

    return pl.pallas_call(
        body, grid=(n_blk,),
        in_specs=[pl.BlockSpec((SEQ, 512), lambda i: (0, jnp.minimum(i, n_pm - 1))),
                  pl.BlockSpec((SEQ, 512), lambda i: (0, jnp.maximum(i - n_pm, 0))),
                  pl.BlockSpec((SEQ, 128), lambda i: (0, PM_AL // 128)),
                  _once((SEQ, D_MODEL))],
        out_specs=pl.BlockSpec((D_IN, D_MODEL), lambda i: (0, 0)),
        out_shape=jax.ShapeDtypeStruct((D_IN, D_MODEL), BF16),
        compiler_params=_params(("arbitrary",)), name="dw_in",
    )(dpm, dgab, dpm, h)


def _tn_matmul(a, b, name, shards=1, tm=512, rider=None, rider_ins=()):
    m, n = a.shape[1], b.shape[1]
    tm = min(tm, m)
    tn = n // shards if shards > 1 else min(n, 1024)
    steps_i, steps_j = m // tm, n // tn
    r_in, r_out_specs, r_out_shape, r_scratch = _rider_specs(rider, rider_ins)

    def body(*refs):
        (a_ref, b_ref), ride_in, (o_ref,), ride_out, scratch = _split(refs, 2, len(r_in), 1, len(r_out_specs), len(r_scratch))
        step = pl.program_id(0) * steps_j + pl.program_id(1)
        _ride(rider, ("first",), step == 0, ride_in, ride_out, scratch)
        o_ref[...] = _tn(a_ref[...], b_ref[...]).astype(BF16)
        _ride(rider, ("middle", "last"), step == steps_i * steps_j - 1, ride_in, ride_out, scratch)

    if shards > 1:
        out_spec = pl.BlockSpec((None, tm, tn), lambda i, j: (j, i, 0))
        out_shape = jax.ShapeDtypeStruct((shards, m, tn), BF16)
    else:
        out_spec = pl.BlockSpec((tm, tn), lambda i, j: (i, j))
        out_shape = jax.ShapeDtypeStruct((m, n), BF16)
    res = pl.pallas_call(
        body, grid=(steps_i, steps_j),
        in_specs=[pl.BlockSpec((SEQ, tm), lambda i, j: (0, i)), pl.BlockSpec((SEQ, tn), lambda i, j: (0, j))] + r_in,
        out_specs=[out_spec] + r_out_specs, out_shape=[out_shape] + r_out_shape, scratch_shapes=r_scratch,
        compiler_params=_params(("arbitrary", "arbitrary")), name=name,
    )(a, b, *rider_ins)
    return res[0] if rider is None else (res[0], res[1:])


MESH = pl.DeviceIdType.MESH
ANY = pl.BlockSpec(memory_space=pl.ANY)
VMEM_WHOLE = pl.BlockSpec(memory_space=pltpu.VMEM)

_BIG = ("w_in", "w_pa", "w_pb", "w_o", "w_up", "w_down")
_BIG_SHARD = {"w_in": (IN_SHARD, D_MODEL), "w_pa": (512, 256), "w_pb": (512, 256), "w_o": (256, D_MODEL),
              "w_up": (D_MODEL, 1024), "w_down": (1024, D_MODEL),
              "w_down_a": (512, D_MODEL), "w_down_b": (512, D_MODEL)}
_BIG_SPLIT = {"w_in": 1, "w_pa": 0, "w_pb": 0, "w_o": 0, "w_up": 0, "w_down": 0, "w_down_a": 0, "w_down_b": 0}


def _half(ref, e, name, lead=0, part=None):
    axis = _BIG_SPLIT[name]
    size = _BIG_SHARD[name][axis] // 2
    start = e * size
    if part is not None:
        size //= 2
        start = start + part * size
    start = pl.multiple_of(start, 128 if axis == 1 else 16)
    idx = [pl.ds(0, ref.shape[a]) for a in range(lead)]
    idx += [pl.ds(start, size), pl.ds(0, _BIG_SHARD[name][1])] if axis == 0 else [pl.ds(0, _BIG_SHARD[name][0]), pl.ds(start, size)]
    return ref.at[tuple(idx)]


def _half_shape(name):
    r, c = _BIG_SHARD[name]
    return (r // 2, c) if _BIG_SPLIT[name] == 0 else (r, c // 2)


def _remote(src, dst, send_sems, recv_sems, k, to):
    return pltpu.make_async_remote_copy(src_ref=src, dst_ref=dst, send_sem=send_sems.at[k], recv_sem=recv_sems.at[k],
                                        device_id=to, device_id_type=MESH)


def _mesh_place():
    x, y, c = lax.axis_index("x"), lax.axis_index("y"), lax.axis_index("c")
    return x, y, c, [(1 - x, y), (x, 1 - y), (1 - x, 1 - y)]


class _Gather:
    def __init__(self, names, small=(), middle_at=0.5):
        self.middle_at = middle_at
        self.names = tuple(names)
        self.nb = len(self.names)
        self.n = self.nb + len(small)
        self.n_sems = 8 * self.nb + 3 * len(small)
        self.out_shape = [jax.ShapeDtypeStruct((N_CHIP,) + _BIG_SHARD[nm], BF16) for nm in self.names]
        self.out_shape += [jax.ShapeDtypeStruct((N_CHIP,) + s.shape, s.dtype) for s in small]

    def _copies(self, ins, outs, ss, rs, k):
        x, y, c, _ = _mesh_place()
        name = self.names[k]
        me, xn, yn, dg = 2 * x + y, 2 * (1 - x) + y, 2 * x + (1 - y), 2 * (1 - x) + (1 - y)
        to_x, to_y, sibling = (1 - x, y, c), (x, 1 - y, c), (x, y, 1 - c)

        def region(slot, e, part=None):
            return _half(outs[k].at[slot], e, name, part=part)

        def copy(pair, src, dst, to):
            return _remote(src, dst, ss, rs, 8 * k + pair, to)

        mine = _half(ins[k], c, name)
        sent = [copy(0, mine, region(me, c), to_x), copy(1, mine, region(me, c), to_y),
                copy(2, region(xn, c, 0), region(xn, c, 0), to_y), copy(3, region(yn, c, 1), region(yn, c, 1), to_x),
                copy(4, region(xn, c), region(xn, c), sibling), copy(5, region(yn, c), region(yn, c), sibling),
                copy(6, region(dg, c, 0), region(dg, c, 0), sibling), copy(7, region(dg, c, 1), region(dg, c, 1), sibling)]
        landing = [region(xn, c), region(yn, c), region(dg, c, 0), region(dg, c, 1),
                   region(xn, 1 - c), region(yn, 1 - c), region(dg, 1 - c, 0), region(dg, 1 - c, 1)]
        received = [copy(pair, dst, dst, sibling) for pair, dst in enumerate(landing)]
        return sent, received

    def _small(self, ins, outs, ss, rs, k, j, peer, slot, c):
        return _remote(ins[k], outs[k].at[slot], ss, rs, 8 * self.nb + 3 * (k - self.nb) + j, (*peer, c))

    def first(self, ins, outs, ss, rs):
        x, y, c, peers = _mesh_place()
        me = 2 * x + y
        for k in range(self.nb):
            sent, _ = self._copies(ins, outs, ss, rs, k)
            sent[0].start()
            sent[1].start()
        for k in range(self.nb, self.n):
            for j, peer in enumerate(peers):
                self._small(ins, outs, ss, rs, k, j, peer, me, c).start()
        for k in range(self.n):
            outs[k][me] = ins[k][...]

    def middle(self, ins, outs, ss, rs):
        for k in range(self.nb):
            sent, received = self._copies(ins, outs, ss, rs, k)
            for pair in (0, 1):
                received[pair].wait_recv()
                sent[2 + pair].start()
                sent[4 + pair].start()

    def last(self, ins, outs, ss, rs):
        x, y, c, peers = _mesh_place()
        for k in range(self.nb):
            sent, received = self._copies(ins, outs, ss, rs, k)
            for pair in (2, 3):
                received[pair].wait_recv()
                sent[4 + pair].start()
        for k in range(self.nb):
            sent, received = self._copies(ins, outs, ss, rs, k)
            for pair in range(4, 8):
                received[pair].wait_recv()
            for cp in sent:
                cp.wait_send()
        for k in range(self.nb, self.n):
            for j, (px, py) in enumerate(peers):
                self._small(ins, outs, ss, rs, k, j, (px, py), 2 * px + py, c).wait_recv()
                self._small(ins, outs, ss, rs, k, j, (px, py), 2 * x + y, c).wait_send()


def _run_alone(rider, ins, name):
    def body(*refs):
        r_in, r_out, sems = _split(refs, len(ins), len(rider.out_shape), 2)
        rider.first(r_in, r_out, *sems)
        rider.middle(r_in, r_out, *sems)
        rider.last(r_in, r_out, *sems)

    return pl.pallas_call(
        body, in_specs=[VMEM_WHOLE] * len(ins), out_specs=[VMEM_WHOLE] * len(rider.out_shape), out_shape=rider.out_shape,
        scratch_shapes=[pltpu.SemaphoreType.DMA((rider.n_sems,)), pltpu.SemaphoreType.DMA((rider.n_sems,))],
        compiler_params=_params(), name=name,
    )(*ins)


class _Presum:
    in_space = ANY

    def __init__(self, names):
        self.names = tuple(names)
        self.n = len(self.names)
        self.n_sems = 3 * self.n
        self.out_shape = [jax.ShapeDtypeStruct((N_CHIP,) + _half_shape(nm), BF16) for nm in self.names]
        self.work_shape = self.out_shape + self.out_shape

    def _stage(self, ins, bufs, ss, k, e, which):
        n = self.n
        return pltpu.make_async_copy(_half(ins[k], e, self.names[k], lead=1), bufs[which * n + k], ss.at[which * n + k])

    def _give(self, bufs, ss, rs, k, sibling):
        return _remote(bufs[self.n + k], bufs[k], ss, rs, k, sibling)

    def first(self, ins, bufs, ss, rs):
        x, y, c, _ = _mesh_place()
        for k in range(self.n):
            self._stage(ins, bufs, ss, k, 1 - c, 1).start()
        for k in range(self.n):
            self._stage(ins, bufs, ss, k, c, 2).start()
        for k in range(self.n):
            self._stage(ins, bufs, ss, k, 1 - c, 1).wait()
            self._give(bufs, ss, rs, k, (x, y, 1 - c)).start()

    def middle(self, ins, bufs, ss, rs):
        pass

    def last(self, ins, bufs, ss, rs):
        x, y, c, _ = _mesh_place()
        for k in range(self.n):
            self._give(bufs, ss, rs, k, (x, y, 1 - c)).wait_recv()
            self._stage(ins, bufs, ss, k, c, 2).wait()

            @pl.loop(0, N_CHIP)
            def _(j):
                bufs[k][j] = (bufs[k][j].astype(F32) + bufs[2 * self.n + k][j].astype(F32)).astype(BF16)
        for k in range(self.n):
            self._give(bufs, ss, rs, k, (x, y, 1 - c)).wait_send()


def _presum(names, grads, name):
    rider = _Presum(names)
    n = rider.n

    def body(*refs):
        g_refs, got_refs, work_refs, sems = _split(refs, n, n, 2 * n, 2)
        bufs = list(got_refs) + list(work_refs)
        rider.first(g_refs, bufs, *sems)
        rider.last(g_refs, bufs, *sems)

    return pl.pallas_call(
        body, in_specs=[ANY] * n, out_specs=[VMEM_WHOLE] * n, out_shape=rider.out_shape,
        scratch_shapes=[pltpu.VMEM(s.shape, s.dtype) for s in rider.work_shape]
        + [pltpu.SemaphoreType.DMA((rider.n_sems,)), pltpu.SemaphoreType.DMA((rider.n_sems,))],
        compiler_params=_params(), name=name,
    )(*grads)


class _ReduceRelay:
    middle_at = 0.75

    def __init__(self, names):
        self.names = tuple(names)
        self.n = len(self.names)
        self.n_sems = 6 * self.n
        self.out_shape = [jax.ShapeDtypeStruct((N_CHIP,) + _half_shape(nm), BF16) for nm in self.names]
        quarter = [jax.ShapeDtypeStruct(self._part_shape(nm), BF16) for nm in self.names]
        self.work_shape = quarter + quarter

    @staticmethod
    def _part_shape(name):
        r, c = _half_shape(name)
        return (r // 2, c) if _BIG_SPLIT[name] == 0 else (r, c // 2)

    def _part(self, ref, name, p):
        r, c = self._part_shape(name)
        return ref.at[pl.ds(p * r, r), pl.ds(0, c)] if _BIG_SPLIT[name] == 0 else ref.at[pl.ds(0, r), pl.ds(p * c, c)]

    def _copies(self, ins, bufs, ss, rs, k):
        x, y, c, _ = _mesh_place()
        name, n = self.names[k], self.n
        me, xn, yn, dg = 2 * x + y, 2 * (1 - x) + y, 2 * x + (1 - y), 2 * (1 - x) + (1 - y)
        to_x, to_y = (1 - x, y, c), (x, 1 - y, c)
        mine = lambda slot, p: self._part(ins[k].at[slot], name, p)
        slot = lambda s, p: self._part(bufs[k].at[s], name, p)
        from_x, from_y = bufs[n + k], bufs[2 * n + k]

        def copy(pair, src, dst, to):
            return _remote(src, dst, ss, rs, 6 * k + pair, to)

        sent = [copy(0, mine(dg, 0), from_x, to_x), copy(1, mine(dg, 1), from_y, to_y),
                copy(2, mine(xn, 0), slot(me, 0), to_x), copy(3, mine(yn, 1), slot(me, 1), to_y),
                copy(4, from_y, slot(me, 1), to_x), copy(5, from_x, slot(me, 0), to_y)]
        landing = [from_x, from_y, slot(xn, 0), slot(yn, 1), slot(xn, 1), slot(yn, 0)]
        received = [copy(pair, dst, dst, to_x) for pair, dst in enumerate(landing)]
        return sent, received

    def first(self, ins, bufs, ss, rs):
        x, y, c, _ = _mesh_place()
        me, dg = 2 * x + y, 2 * (1 - x) + (1 - y)
        for k in range(self.n):
            sent, _ = self._copies(ins, bufs, ss, rs, k)
            for pair in range(4):
                sent[pair].start()
        for k in range(self.n):
            bufs[k][me] = ins[k][me]
            bufs[k][dg] = jnp.zeros(_half_shape(self.names[k]), BF16)

    def middle(self, ins, bufs, ss, rs):
        x, y, c, _ = _mesh_place()
        xn, yn = 2 * (1 - x) + y, 2 * x + (1 - y)
        for k in range(self.n):
            sent, received = self._copies(ins, bufs, ss, rs, k)
            name, n = self.names[k], self.n
            for pair, buf, own in ((0, bufs[n + k], self._part(ins[k].at[yn], name, 0)),
                                   (1, bufs[2 * n + k], self._part(ins[k].at[xn], name, 1))):
                received[pair].wait_recv()
                buf[...] = (buf[...].astype(F32) + own[...].astype(F32)).astype(BF16)
            sent[5].start()
            sent[4].start()

    def last(self, ins, bufs, ss, rs):
        for k in range(self.n):
            sent, received = self._copies(ins, bufs, ss, rs, k)
            for pair in range(2, 6):
                received[pair].wait_recv()
            for cp in sent:
                cp.wait_send()


class _SendPartials:
    def __init__(self, names, small_shape=None):
        self.n = len(names)
        self.small = small_shape is not None
        self.n_sems = 3 * self.n + 7
        self.out_shape = [jax.ShapeDtypeStruct((N_CHIP,) + _half_shape(nm), BF16) for nm in names]
        if self.small:
            self.out_shape.append(jax.ShapeDtypeStruct((N_DEV,) + small_shape, F32))

    def _piece(self, ins, outs, ss, rs, k, j, peer, src_slot, dst_slot, c):
        return _remote(ins[k].at[src_slot], outs[k].at[dst_slot], ss, rs, 3 * k + j, (*peer, c))

    def _small(self, ins, outs, ss, rs, r, other, slot):
        return _remote(ins[self.n], outs[self.n].at[slot], ss, rs, 3 * self.n + r, other)

    @staticmethod
    def _others(x, y, c):
        return [(x, y, 1 - c), (1 - x, y, c), (1 - x, y, 1 - c), (x, 1 - y, c), (x, 1 - y, 1 - c),
                (1 - x, 1 - y, c), (1 - x, 1 - y, 1 - c)]

    def first(self, ins, outs, ss, rs):
        x, y, c, peers = _mesh_place()
        me = 2 * x + y
        for k in range(self.n):
            for j, (px, py) in enumerate(peers):
                self._piece(ins, outs, ss, rs, k, j, (px, py), 2 * px + py, me, c).start()
        if self.small:
            for r, other in enumerate(self._others(x, y, c)):
                self._small(ins, outs, ss, rs, r, other, 4 * x + 2 * y + c).start()
            outs[self.n][4 * x + 2 * y + c] = ins[self.n][...]
        for k in range(self.n):
            outs[k][me] = ins[k][me]

    def middle(self, ins, outs, ss, rs):
        pass

    def last(self, ins, outs, ss, rs):
        x, y, c, peers = _mesh_place()
        me = 2 * x + y
        for k in range(self.n):
            for j, (px, py) in enumerate(peers):
                self._piece(ins, outs, ss, rs, k, j, (px, py), me, 2 * px + py, c).wait_recv()
                self._piece(ins, outs, ss, rs, k, j, (px, py), 2 * px + py, me, c).wait_send()
        if self.small:
            for r, (px, py, pc) in enumerate(self._others(x, y, c)):
                self._small(ins, outs, ss, rs, r, (px, py, pc), 4 * px + 2 * py + pc).wait_recv()
                self._small(ins, outs, ss, rs, r, (px, py, pc), 4 * x + 2 * y + c).wait_send()


def _sum_swap(names, parts, small):
    n = len(parts)
    everyone = _SendPartials((), small.shape)

    def body(*refs):
        p_refs, (small_ref,), o_refs, (osmall_ref,), (all_ref,), (send_sems, recv_sems, ss_small, rs_small) = _split(
            refs, n, 1, n, 1, 1, 4)
        x, y, c = lax.axis_index("x"), lax.axis_index("y"), lax.axis_index("c")
        everyone.first([small_ref], [all_ref], ss_small, rs_small)

        def mine(k):
            part = _half(o_refs[k], c, names[k])
            return _remote(part, part, send_sems, recv_sems, k, (x, y, 1 - c))

        for k in range(n):
            for e in range(2):
                @pl.when(c == e)
                def _():
                    g = p_refs[k][0].astype(F32)
                    for s in range(1, N_CHIP):
                        g = g + p_refs[k][s].astype(F32)
                    r, cols = _half_shape(names[k])
                    if _BIG_SPLIT[names[k]] == 0:
                        o_refs[k][e * r:(e + 1) * r, :] = g
                    else:
                        o_refs[k][:, e * cols:(e + 1) * cols] = g
            mine(k).start()
        for k in range(n):
            theirs = _half(o_refs[k], 1 - c, names[k])
            _remote(theirs, theirs, send_sems, recv_sems, k, (x, y, 1 - c)).wait_recv()
            mine(k).wait_send()
        everyone.last([small_ref], [all_ref], ss_small, rs_small)
        g = all_ref[0]
        for d in range(1, N_DEV):
            g = g + all_ref[d]
        osmall_ref[...] = g

    res = pl.pallas_call(
        body, in_specs=[VMEM_WHOLE] * (n + 1), out_specs=[VMEM_WHOLE] * (n + 1),
        out_shape=[jax.ShapeDtypeStruct(_BIG_SHARD[nm], F32) for nm in names] + [jax.ShapeDtypeStruct(small.shape, F32)],
        scratch_shapes=[pltpu.VMEM((N_DEV,) + small.shape, F32), pltpu.SemaphoreType.DMA((n,)), pltpu.SemaphoreType.DMA((n,)),
                        pltpu.SemaphoreType.DMA((everyone.n_sems,)), pltpu.SemaphoreType.DMA((everyone.n_sems,))],
        compiler_params=_params(), name="sum_swap",
    )(*parts, small)
    return res[:n], res[n]


def _tile(rows, cols, itemsize, budget):
    t = cols if rows % 16 else rows
    other = rows if rows % 16 else cols
    step = 256 if rows % 16 else 32
    while t % step == 0 and t * other * itemsize > budget:
        t //= 2
    return (rows, t) if rows % 16 else (t, cols)


def _adamw_math(w, g, m, v):
    m = ADAM_B1 * m + (1.0 - ADAM_B1) * g
    v = ADAM_B2 * v + (1.0 - ADAM_B2) * (g * g)
    m_hat = m / (1.0 - ADAM_B1 ** ADAM_STEP)
    v_hat = v / (1.0 - ADAM_B2 ** ADAM_STEP)
    delta = -ADAM_LR * (m_hat / (jnp.sqrt(v_hat) + ADAM_EPS) + ADAM_WD * w)
    return delta, m, v


def _adamw_big(g, w, m, v, name):
    r, c = w.shape
    tr, tc = _tile(r, c, 4, 1024 * 1024)

    def body(g_ref, w_ref, m_ref, v_ref, d_ref, nm_ref, nv_ref):
        d_ref[...], nm_ref[...], nv_ref[...] = _adamw_math(w_ref[...], g_ref[...], m_ref[...], v_ref[...])

    blk = pl.BlockSpec((tr, tc), lambda i, l: (i, l))
    return pl.pallas_call(
        body, grid=(r // tr, c // tc), in_specs=[blk, blk, blk, blk],
        out_specs=[blk, blk, blk], out_shape=[jax.ShapeDtypeStruct((r, c), F32)] * 3,
        compiler_params=_params(("arbitrary", "arbitrary")), name=name,
    )(g, w, m, v)


def _adamw_rows(g, w, m, v, name):
    r, k, lanes = w.shape
    tr = 296

    def body(g_ref, w_ref, m_ref, v_ref, g3_ref, d_ref, nm_ref, nv_ref):
        g = g_ref[...].reshape(tr, k, lanes)
        g3_ref[...] = g
        d_ref[...], nm_ref[...], nv_ref[...] = _adamw_math(w_ref[...], g, m_ref[...], v_ref[...])

    rows = pl.BlockSpec((tr, k, lanes), lambda i: (i, 0, 0))
    return pl.pallas_call(
        body, grid=(pl.cdiv(r, tr),), in_specs=[pl.BlockSpec((tr, k * lanes), lambda i: (i, 0)), rows, rows, rows],
        out_specs=[rows] * 4, out_shape=[jax.ShapeDtypeStruct((r, k, lanes), F32)] * 4,
        compiler_params=_params(("arbitrary",)), name=name,
    )(g, w, m, v)


def _adamw_small(ws, gs, ms, vs):
    n = len(ws)

    def body(*refs):
        w_refs, g_refs, m_refs, v_refs, d_refs, nm_refs, nv_refs = _split(refs, *([n] * 7))
        for k in range(n):
            d_refs[k][...], nm_refs[k][...], nv_refs[k][...] = _adamw_math(w_refs[k][...], g_refs[k][...], m_refs[k][...],
                                                                             v_refs[k][...])

    shapes = [jax.ShapeDtypeStruct(w.shape, F32) for w in ws]
    res = pl.pallas_call(body, out_shape=shapes * 3, name="adamw_small")(*ws, *gs, *ms, *vs)
    return res[:n], res[n:2 * n], res[2 * n:]


def _pack(arrs):
    flat = jnp.concatenate([a.reshape(-1) for a in arrs])
    rows = -(-flat.shape[0] // 1024) * 8
    return jnp.pad(flat, (0, rows * 128 - flat.shape[0])).reshape(rows, 128)


def _unpack(buf, shapes):
    flat = buf.reshape(-1)
    out, off = [], 0
    for s in shapes:
        size = 1
        for d in s:
            size *= d
        out.append(flat[off:off + size].reshape(s))
        off += size
    return out


def _block_rows(w):
    return jnp.pad(w.reshape(512, 4), ((0, 0), (0, 124)))


def _cols(a4):
    return jnp.transpose(a4, (1, 0, 2)).reshape(a4.shape[1], -1)


_LATE = ("w_pa", "w_pb", "w_o", "w_up", "w_down")
_RIDE_IN_PROJ = ("w_pa", "w_pb", "w_o", "w_down_a")
_RIDE_MIXER = ("w_up",)
_RIDE_MERGE = ("w_down_b",)


def _full_weights(gathered):
    joined = {"w_in": (D_IN, D_MODEL), "w_o": (D_MODEL, D_MODEL)}
    return {n: (a.reshape(joined[n]) if n in joined else a) for n, a in gathered.items()}


def _local_step(x, target, w, sp, late_shards=None):
    sp = {n: (a.reshape(1, -1) if a.ndim == 1 else a) for n, a in sp.items()}
    wau = jnp.zeros((128, 256), F32).at[0:16].set(sp["w_a_up"])
    wif = jnp.zeros((1536, 128), F32).at[:, 0:8].set(sp["w_if"])
    bif = jnp.zeros((1, 128), F32).at[:, 0:8].set(sp["b_if"])
    p = {"wau": wau, "bau": sp["b_a_up"], "ggla": sp["g_gla_norm"], "cw": sp["conv_w"], "cb": sp["conv_b"],
         "wq": _block_rows(sp["w_q_ml"]), "wk": _block_rows(sp["w_k_ml"]), "wv": _block_rows(sp["w_v_ml"]),
         "wif": wif, "bif": bif, "skip": sp["ml_skip"], "gml": sp["g_ml_norm"]}

    if late_shards is None:
        (pm, gab, h), _ = _in_proj(x, sp["g_pre_mix"], w["w_in"])
        ab, *states = _mixer_fwd(pm, p)
        (x1, mix, merged), _ = _merge_fwd(ab, gab, x, w["w_pa"], w["w_pb"], w["w_o"], sp["g_post_mix"])
    else:
        shard = dict(zip(_LATE, late_shards))
        shard["w_down_a"], shard["w_down_b"] = shard["w_down"][0:512], shard["w_down"][512:1024]
        (pm, gab, h), got = _in_proj(x, sp["g_pre_mix"], w["w_in"], _Gather(_RIDE_IN_PROJ, middle_at=0.7),
                                     [shard[n] for n in _RIDE_IN_PROJ])
        w = dict(w, **_full_weights(dict(zip(_RIDE_IN_PROJ, got))))
        ab, *rest = _mixer_fwd(pm, p, _Gather(_RIDE_MIXER, middle_at=0.48), [shard[n] for n in _RIDE_MIXER])
        states = rest[:4]
        w.update(_full_weights(dict(zip(_RIDE_MIXER, rest[4:]))))
        (x1, mix, merged), got = _merge_fwd(ab, gab, x, w["w_pa"], w["w_pb"], w["w_o"], sp["g_post_mix"],
                                            _Gather(_RIDE_MERGE, middle_at=0.46), [shard[n] for n in _RIDE_MERGE])
        w.update(_full_weights(dict(zip(_RIDE_MERGE, got))))
    dx1, u, dd, h2, dpre, dg_post_mlp, dg_pre_mlp, loss = _mlp(x1, target, sp["g_pre_mlp"], sp["g_post_mlp"],
                                                                w["w_up"], w["w_down_a"], w["w_down_b"])
    dmix, dya, dyb, dgab, dab, dg_post_mix = _merge_bwd(dx1, mix, ab, gab, w["w_pa"], w["w_pb"], w["w_o"], sp["g_post_mix"])
    big = {
        "w_pa": _tn_matmul(ab[:, 0:512], dya, "dw_pa", shards=N_CHIP),
        "w_pb": _tn_matmul(ab[:, 512:1024], dyb, "dw_pb", shards=N_CHIP),
        "w_o": _tn_matmul(merged, dmix, "dw_o"),
        "w_up": _tn_matmul(h2, dpre, "dw_up", shards=N_CHIP),
    }
    if late_shards is None:
        big["w_down"] = _tn_matmul(u, dd, "dw_down")
        dpm, dp, _ = _mixer_bwd(pm, dab, states, p)
    else:
        pieces = lambda n: big[n].reshape((N_CHIP,) + _BIG_SHARD[n])
        big["w_down"], partial = _tn_matmul(u, dd, "dw_down", rider=_Presum(_LATE[:4]),
                                            rider_ins=[pieces(n) for n in _LATE[:4]])
        partial = list(partial) + list(_presum(("w_down",), [pieces("w_down")], "presum_w_down"))
        dpm, dp, parts = _mixer_bwd(pm, dab, states, p, _SendPartials(_LATE), partial)
        big = dict(zip(_LATE, parts))
    big["w_in"] = _dw_in(dpm, dgab, h)
    if late_shards is None:
        (dx, dg_pre_mix), _ = _in_proj_bwd(dpm, dgab, x, dx1, sp["g_pre_mix"], w["w_in"])
    else:
        partial = _presum(("w_in",), [big["w_in"].reshape((N_CHIP,) + _BIG_SHARD["w_in"])], "presum_w_in")
        (dx, dg_pre_mix), parts = _in_proj_bwd(dpm, dgab, x, dx1, sp["g_pre_mix"], w["w_in"], _ReduceRelay(("w_in",)),
                                               partial)
        big["w_in"] = parts[0]
    small = {
        "g_pre_mix": dg_pre_mix, "b_a_up": dp["bau"], "g_gla_norm": dp["ggla"], "conv_b": dp["cb"],
        "w_q_ml": dp["wq"][:, 0:4].reshape(128, 4, 4), "w_k_ml": dp["wk"][:, 0:4].reshape(128, 4, 4),
        "w_v_ml": dp["wv"][:, 0:4].reshape(128, 4, 4),
        "b_if": dp["bif"][:, 0:8], "ml_skip": dp["skip"], "g_ml_norm": dp["gml"], "g_post_mix": dg_post_mix,
        "g_pre_mlp": dg_pre_mlp, "g_post_mlp": dg_post_mlp, "w_a_up": dp["wau"][0:16], "conv_w": dp["cw"],
        "w_if": dp["wif"][:, 0:8], "loss": loss[:, 0:1],
    }
    return dx, big, small


_SMALL_REPL = ("g_pre_mix", "b_a_up", "g_gla_norm", "conv_b", "w_q_ml", "w_k_ml", "w_v_ml", "b_if", "ml_skip",
               "g_ml_norm", "g_post_mix", "g_pre_mlp", "g_post_mlp")
_SMALL_SHARDED = ("w_a_up", "conv_w", "w_if")
_SMALL_ORDER = _SMALL_REPL + _SMALL_SHARDED + ("loss",)
_WEIGHTS = ("g_pre_mix", "w_in", "w_a_up", "b_a_up", "g_gla_norm", "conv_w", "conv_b", "w_q_ml", "w_k_ml", "w_v_ml",
            "w_if", "b_if", "ml_skip", "g_ml_norm", "w_pa", "w_pb", "w_o", "g_post_mix", "g_pre_mlp", "w_up", "w_down",
            "g_post_mlp")


_BLOCK_WEIGHTS = ("w_q_ml", "w_k_ml", "w_v_ml")


def _stored(name, a):
    if name in _BLOCK_WEIGHTS:
        return jnp.transpose(a, (0, 2, 3, 1)).reshape(16, 128)
    if name == "w_if":
        return jnp.transpose(a, (0, 2, 1)).reshape(8, 384)
    return a


def _unstored(name, a):
    if name in _BLOCK_WEIGHTS:
        return jnp.transpose(a.reshape(1, 4, 4, 128), (0, 3, 1, 2))
    if name == "w_if":
        return jnp.transpose(a.reshape(1, 8, 384), (0, 2, 1))
    return a


def _as_shard(name, a):
    return jnp.transpose(a, (2, 0, 1)).reshape(IN_SHARD, D_MODEL // 128, 128) if name == "w_in" else a[0]


def _from_shard(name, a):
    return jnp.transpose(a, (1, 2, 0)).reshape(1, D_MODEL, IN_SHARD) if name == "w_in" else a[None]


def kernel(x, g_pre_mix, w_in, w_a_up, b_a_up, g_gla_norm, conv_w, conv_b, w_q_ml, w_k_ml, w_v_ml, w_if, b_if, ml_skip, g_ml_norm, w_pa, w_pb, w_o, g_post_mix, g_pre_mlp, w_up, w_down, g_post_mlp, loss_target, m_g_pre_mix, m_w_in, m_w_a_up, m_b_a_up, m_g_gla_norm, m_conv_w, m_conv_b, m_w_q_ml, m_w_k_ml, m_w_v_ml, m_w_if, m_b_if, m_ml_skip, m_g_ml_norm, m_w_pa, m_w_pb, m_w_o, m_g_post_mix, m_g_pre_mlp, m_w_up, m_w_down, m_g_post_mlp, v_g_pre_mix, v_w_in, v_w_a_up, v_b_a_up, v_g_gla_norm, v_conv_w, v_conv_b, v_w_q_ml, v_w_k_ml, v_w_v_ml, v_w_if, v_b_if, v_ml_skip, v_g_ml_norm, v_w_pa, v_w_pb, v_w_o, v_g_post_mix, v_g_pre_mlp, v_w_up, v_w_down, v_g_post_mlp):
    args = dict(locals())
    wts = {n: _as_shard(n, args[n]) for n in _WEIGHTS}
    mom = {n: _as_shard(n, args["m_" + n]) for n in _WEIGHTS}
    var = {n: _as_shard(n, args["v_" + n]) for n in _WEIGHTS}
    chip = 2 * lax.axis_index("x") + lax.axis_index("y")

    first = ("w_in",) + _SMALL_SHARDED
    gathered = dict(zip(first, _run_alone(_Gather(("w_in",), [wts[n] for n in _SMALL_SHARDED]),
                                          [wts[n].reshape(IN_SHARD, D_MODEL).astype(BF16) if n == "w_in" else wts[n]
                                           for n in first],
                                          "gather_first")))
    sp = {n: wts[n] for n in _SMALL_REPL}
    sp["w_a_up"] = _cols(gathered["w_a_up"])
    sp["conv_w"] = _cols(gathered["conv_w"])
    sp["w_if"] = gathered["w_if"].reshape(1536, 8)

    dx, big, small = _local_step(x[0], loss_target[0], _full_weights({"w_in": gathered["w_in"]}), sp,
                                 late_shards=[wts[n].astype(BF16) for n in _LATE])

    small_shapes = [small[n].shape for n in _SMALL_ORDER]
    packed = _pack([small[n] for n in _SMALL_ORDER])
    sums, small_sum = _sum_swap(_BIG, [big[n] for n in _BIG], packed)

    grads, delta, new_m, new_v = {}, {}, {}, {}
    for n, g in zip(_BIG, sums):
        if n == "w_in":
            g, d, nm, nv = _adamw_rows(g, wts[n], mom[n], var[n], "adamw_" + n)
        else:
            d, nm, nv = _adamw_big(g, wts[n], mom[n], var[n], "adamw_" + n)
        grads[n], delta[n], new_m[n], new_v[n] = (_from_shard(n, a) for a in (g, d, nm, nv))
    summed = dict(zip(_SMALL_ORDER, _unpack(small_sum, small_shapes)))
    loss = summed["loss"].reshape(())
    summed["w_a_up"] = lax.dynamic_slice_in_dim(summed["w_a_up"], chip * 64, 64, axis=1)
    summed["conv_w"] = lax.dynamic_slice_in_dim(summed["conv_w"], chip * 128, 128, axis=1)
    summed["w_if"] = lax.dynamic_slice_in_dim(summed["w_if"], chip * 384, 384, axis=0)
    small_names = _SMALL_REPL + _SMALL_SHARDED
    g_stored = [_stored(n, summed[n].reshape(args[n].shape)) for n in small_names]
    upd = _adamw_small([_stored(n, args[n]) for n in small_names], g_stored,
                       [_stored(n, args["m_" + n]) for n in small_names], [_stored(n, args["v_" + n]) for n in small_names])
    for dst, arrs in zip((grads, delta, new_m, new_v), (g_stored,) + tuple(upd)):
        dst.update({n: _unstored(n, a) for n, a in zip(small_names, arrs)})

    outs = [loss, dx[None]]
    for group in (grads, delta, new_m, new_v):
        outs += [group[n] for n in _WEIGHTS]
    return tuple(outs)
```

```python
import functools

import jax
import jax.numpy as jnp
from jax import lax
from jax.experimental import pallas as pl
from jax.experimental.pallas import tpu as pltpu

F32 = jnp.float32
BF16 = jnp.bfloat16

SEQ = 2048
D_MODEL = 1024
CHUNK = 64
N_CHUNK = SEQ // CHUNK
HEADS = 4
GLA_DK = 64
GLA_DV = 128
ML_DH = 128
D_FF = 4096
EPS = 1e-6
N_CHIP = 4
N_DEV = 8
TOK_TILE = 256
N_TOK_TILE = SEQ // TOK_TILE
SWEEP = 2
assert CHUNK == 64
N_SWEEP = N_CHUNK // SWEEP

PM_W = 2688
PM_XM = 1536
PM_OP = 2048
PM_AL = 2560
GAB_W = 2048
D_IN = 4624
IN_SHARD = D_IN // N_CHIP
IN_ALOW = 1536
IN_XM = 1552
IN_GATES = 2576

ADAM_LR = 0.001
ADAM_B1 = 0.9
ADAM_B2 = 0.999
ADAM_EPS = 1e-08
ADAM_WD = 0.01
ADAM_STEP = 10

VMEM_LIMIT = 56 * 1024 * 1024


def _params(sem=None):
    return pltpu.CompilerParams(dimension_semantics=sem, vmem_limit_bytes=VMEM_LIMIT)


def _dot(a, b, ca, cb):
    return lax.dot_general(a.astype(BF16), b.astype(BF16), (((ca,), (cb,)), ((), ())), preferred_element_type=F32)


def _pmm_nn(a, b):
    return _dot(a, b, 1, 0)


def _pmm_nt(a, b):
    return _dot(a, b, 1, 1)


def _pmm_tn(a, b):
    return _dot(a, b, 0, 0)


def _pcmm(c, x):
    return lax.dot_general(c, x, (((1,), (0,)), ((), ())), precision=lax.Precision.HIGHEST, preferred_element_type=F32)


@jax.custom_vjp
def _mm_nn(a, b):
    return _dot(a, b, 1, 0)


@jax.custom_vjp
def _mm_nt(a, b):
    return _dot(a, b, 1, 1)


@jax.custom_vjp
def _mm_tn(a, b):
    return _dot(a, b, 0, 0)


_mm_nn.defvjp(lambda a, b: (_dot(a, b, 1, 0), (a, b)), lambda r, g: (_mm_nt(g, r[1]), _mm_tn(r[0], g)))
_mm_nt.defvjp(lambda a, b: (_dot(a, b, 1, 1), (a, b)), lambda r, g: (_mm_nn(g, r[1]), _mm_tn(g, r[0])))
_mm_tn.defvjp(lambda a, b: (_dot(a, b, 0, 0), (a, b)), lambda r, g: (_mm_nt(r[1], g), _mm_nn(r[0], g)))


@jax.custom_vjp
def _cmm(c, x):
    return _pcmm(c, x)


_cmm.defvjp(
    lambda c, x: (_pcmm(c, x), c),
    lambda c, g: (jnp.zeros_like(c), lax.dot_general(c, g, (((0,), (0,)), ((), ())), precision=lax.Precision.HIGHEST,
                                                      preferred_element_type=F32)),
)

_PLAIN_OPS = (_pmm_nn, _pmm_nt, _pmm_tn, _pcmm)
_VJP_OPS = (_mm_nn, _mm_nt, _mm_tn, _cmm)


def _sigmoid(x):
    return 0.5 * (jnp.tanh(0.5 * x) + 1.0)


def _log_sigmoid(x):
    return jnp.minimum(x, 0.0) - jnp.log(1.0 + jnp.exp(-jnp.abs(x)))


def _mean(x):
    return jnp.mean(x, axis=-1, keepdims=True)


def _nt(a, b):
    return lax.dot_general(a, b, (((1,), (1,)), ((), ())), preferred_element_type=F32)


def _tn(a, b):
    return lax.dot_general(a, b, (((0,), (0,)), ((), ())), preferred_element_type=F32)


def _mixer_chunk(ops, p, st, pm, xprev8):
    mm_nn, mm_nt, mm_tn, cmm = ops
    n_rows = pm.shape[0]
    n_ch = n_rows // CHUNK
    row = lax.broadcasted_iota(jnp.int32, (n_rows, n_rows), 0)
    col = lax.broadcasted_iota(jnp.int32, (n_rows, n_rows), 1)
    tri = jnp.logical_and((row >> 6) == (col >> 6), row >= col).astype(F32)
    causal = tri[0:CHUNK, 0:CHUNK] > 0.0
    q = pm[:, 0:256]
    k = pm[:, 256:512]
    v = pm[:, 512:1024]
    g = pm[:, 1024:1536]
    xm = pm[:, PM_XM:PM_XM + 512]
    opre = pm[:, PM_OP:PM_OP + 512]
    alow = pm[:, PM_AL:PM_AL + 128]
    hs = range(HEADS)
    cs = range(n_ch)
    pairs = [(i, h) for i in cs for h in hs]
    rs = [slice(i * CHUNK, (i + 1) * CHUNK) for i in cs]
    last = [slice((i + 1) * CHUNK - 1, (i + 1) * CHUNK) for i in cs]
    s6 = [slice(h * GLA_DK, (h + 1) * GLA_DK) for h in hs]
    s12 = [slice(h * 128, (h + 1) * 128) for h in hs]

    xx = jnp.concatenate([xprev8, xm], axis=0)
    pre = p["cb"]
    for j in range(4):
        pre = pre + p["cw"][j:j + 1, :] * xx[5 + j:5 + j + n_rows, :]
    xc = pre * _sigmoid(pre)
    qm = [mm_nn(xc[:, s12[h]], p["wq"][h]) for h in hs]
    km = [mm_nn(xc[:, s12[h]], p["wk"][h]) for h in hs]
    vm = [mm_nn(xm[:, s12[h]], p["wv"][h]) for h in hs]
    qcat = jnp.concatenate(qm, axis=1)
    kcat = jnp.concatenate(km, axis=1)
    vcat = jnp.concatenate(vm, axis=1)
    gates = (mm_nn(qcat, p["wif"][0:512]) + mm_nn(kcat, p["wif"][512:1024]) + mm_nn(vcat, p["wif"][1024:1536])
             + p["bif"])
    lf = _log_sigmoid(gates)
    fc = cmm(tri, lf)
    gates_t = gates.T
    fc_t = fc.T

    la = _log_sigmoid(mm_nn(alow, p["wau"]) + p["bau"]) * (1.0 / 16.0)
    cum = cmm(tri, la)
    cum_last = [cum[last[i], :] for i in cs]
    to_end = jnp.concatenate([cum_last[i] - cum[rs[i], :] for i in cs], axis=0)
    e_pos = jnp.exp(cum)
    e_neg = jnp.exp(-cum)
    qs = q * (GLA_DK ** -0.5)
    qp = qs * e_pos
    qn = qs * e_neg
    kp = k * e_pos
    kn = k * e_neg
    kl = k * jnp.exp(to_end)
    dec = [jnp.exp(cum_last[i]) for i in cs]
    ks = [km[h] * (ML_DH ** -0.5) for h in hs]
    li_c = {(i, h): gates[rs[i], h:h + 1] for i, h in pairs}
    fc_c = {(i, h): fc[rs[i], 4 + h:5 + h] for i, h in pairs}
    f_last = {(i, h): fc[last[i], 4 + h:5 + h] for i, h in pairs}

    a_fwd = {(i, h): mm_nt(qp[rs[i], s6[h]], kn[rs[i], s6[h]]) for i, h in pairs}
    a_bwd = {(i, h): mm_nt(qn[rs[i], s6[h]], kp[rs[i], s6[h]]) for i, h in pairs}
    s_chunk = {(i, h): mm_tn(v[rs[i], s12[h]], kl[rs[i], s6[h]]) for i, h in pairs}
    qk = {(i, h): mm_nt(qm[h][rs[i]], ks[h][rs[i]]) for i, h in pairs}
    a = {ih: f_last[ih] - fc_c[ih] + li_c[ih] for ih in pairs}
    m_loc = {ih: jnp.max(a[ih], axis=0, keepdims=True) for ih in pairs}
    kw = {(i, h): ks[h][rs[i]] * jnp.exp(a[(i, h)] - m_loc[(i, h)]) for i, h in pairs}
    c_chunk = {(i, h): mm_tn(kw[(i, h)], vm[h][rs[i]]) for i, h in pairs}
    mem = {(0, h): st["S"][h] for h in hs}
    c_in = {(0, h): st["C"][h] for h in hs}
    n_in = {(0, h): st["n"][h] for h in hs}
    m_in = {(0, h): st["m"][h][:, 0:1] for h in hs}
    for i, h in pairs:
        mem[(i + 1, h)] = mem[(i, h)] * dec[i][:, s6[h]] + s_chunk[(i, h)]
        m_nx = jnp.maximum(f_last[(i, h)] + m_in[(i, h)], m_loc[(i, h)])
        sp = jnp.exp(f_last[(i, h)] + m_in[(i, h)] - m_nx)
        sl = jnp.exp(m_loc[(i, h)] - m_nx)
        c_in[(i + 1, h)] = sp * c_in[(i, h)] + sl * c_chunk[(i, h)]
        n_in[(i + 1, h)] = sp * n_in[(i, h)] + sl * jnp.sum(kw[(i, h)], axis=0, keepdims=True)
        m_in[(i + 1, h)] = m_nx
    s_new = [mem[(n_ch, h)] for h in hs]
    o_inter = {(i, h): mm_nt(qp[rs[i], s6[h]], mem[(i, h)]) for i, h in pairs}
    q_c = {(i, h): mm_nn(qm[h][rs[i]], c_in[(i, h)]) for i, h in pairs}
    scores = {ih: jnp.where(causal, a_fwd[ih], a_bwd[ih]) for ih in pairs}
    log_d = {(i, h): gates_t[h:h + 1, rs[i]] - jnp.abs(fc_c[(i, h)] - fc_t[4 + h:5 + h, rs[i]]) for i, h in pairs}
    g_int = {ih: fc_c[ih] + m_in[ih] for ih in pairs}
    m_t = {ih: jnp.maximum(g_int[ih], jnp.max(log_d[ih], axis=1, keepdims=True)) for ih in pairs}
    s = {ih: qk[ih] * jnp.exp(log_d[ih] - m_t[ih]) for ih in pairs}
    scl = {ih: jnp.exp(g_int[ih] - m_t[ih]) for ih in pairs}
    o = {(i, h): mm_nn(scores[(i, h)], v[rs[i], s12[h]]) + o_inter[(i, h)] for i, h in pairs}
    num = {(i, h): mm_nn(s[(i, h)], vm[h][rs[i]]) + scl[(i, h)] * q_c[(i, h)] for i, h in pairs}
    o = {ih: o[ih] * lax.rsqrt(_mean(o[ih] * o[ih]) + EPS) * p["ggla"] for ih in pairs}
    gate = g * _sigmoid(g)
    out_a = {(i, h): o[(i, h)] * gate[rs[i], s12[h]] for i, h in pairs}
    den = {(i, h): jnp.sum(s[(i, h)], axis=1, keepdims=True)
           + scl[(i, h)] * jnp.sum(qm[h][rs[i]] * n_in[(i, h)], axis=1, keepdims=True) for i, h in pairs}
    den = {ih: jnp.maximum(jnp.abs(den[ih]), jnp.exp(-m_t[ih])) for ih in pairs}
    open_gate = _sigmoid(opre)
    hc = {(i, h): num[(i, h)] / den[(i, h)] * open_gate[rs[i], s12[h]] for i, h in pairs}
    d0 = {ih: hc[ih] - _mean(hc[ih]) for ih in pairs}
    y = {ih: d0[ih] * lax.rsqrt(_mean(d0[ih] * d0[ih]) + EPS) for ih in pairs}
    skipped = p["skip"] * xc
    out_b = {(i, h): y[(i, h)] * p["gml"][:, s12[h]] + skipped[rs[i], s12[h]] for i, h in pairs}
    ab = jnp.concatenate([jnp.concatenate([out_a[(i, h)] for h in hs] + [out_b[(i, h)] for h in hs], axis=1) for i in cs],
                         axis=0)
    new = {"S": s_new, "C": [c_in[(n_ch, h)] for h in hs], "n": [n_in[(n_ch, h)] for h in hs],
           "m": [jnp.broadcast_to(m_in[(n_ch, h)], (1, ML_DH)) for h in hs]}
    return ab, new


_P_NAMES = ("wau", "bau", "ggla", "cw", "cb", "wq", "wk", "wv", "wif", "bif", "skip", "gml")
_P_SHAPES = {
    "wau": (128, 256), "bau": (1, 256), "ggla": (1, 128), "cw": (4, 512), "cb": (1, 512),
    "wq": (512, 128), "wk": (512, 128), "wv": (512, 128),
    "wif": (1536, 128), "bif": (1, 128), "skip": (1, 512), "gml": (1, 512),
}
_P_BLOCKDIAG = ("wq", "wk", "wv")
_S_NAMES = ("S", "C", "n", "m")
_S_SHAPES = {"S": (HEADS, GLA_DV, GLA_DK), "C": (HEADS, ML_DH, ML_DH), "n": (HEADS, 1, ML_DH), "m": (HEADS, 1, ML_DH)}


def _per_head(ref):
    return [ref[h] for h in range(HEADS)]


def _block_mask():
    r = lax.broadcasted_iota(jnp.int32, (128, 128), 0)
    c = lax.broadcasted_iota(jnp.int32, (128, 128), 1)
    same_block = (r >> 2) == (c >> 2)
    spread = jnp.logical_and(r < 4, (c & 3) == r)
    return same_block.astype(F32), spread.astype(F32)


def _expand_blockdiag(w_ref, dense_ref):
    same_block, spread = _block_mask()
    for h in range(HEADS):
        tiled = _pmm_nn(w_ref[h * 128:(h + 1) * 128, :], spread)
        dense_ref[h] = tiled * same_block


def _collect_blockdiag(ddense_ref, dw_ref):
    same_block, spread = _block_mask()
    for h in range(HEADS):
        dw_ref[h * 128:(h + 1) * 128, :] = lax.dot_general(
            ddense_ref[h] * same_block, spread, (((1,), (1,)), ((), ())), precision=lax.Precision.HIGHEST,
            preferred_element_type=F32)


def _const_spec(shape):
    zeros = (0,) * len(shape)
    return pl.BlockSpec(shape, lambda i: zeros)


def _split(refs, *counts):
    out, at = [], 0
    for c in counts:
        out.append(refs[at:at + c])
        at += c
    assert at == len(refs)
    return out


def _ride(rider, phases, cond, ins, outs, sems):
    if rider is None:
        return
    lands, (send_sems, recv_sems, flush_sems) = sems[:-3], sems[-3:]

    @pl.when(cond)
    def _():
        for phase in phases:
            getattr(rider, phase)(ins, lands, send_sems, recv_sems)
            if hasattr(rider, "flush"):
                rider.flush(phase, lands, outs, flush_sems)
        if "last" in phases and not hasattr(rider, "flush"):
            flush = [pltpu.make_async_copy(lands[k], outs[k], flush_sems.at[k]) for k in range(len(outs))]
            for cp in flush:
                cp.start()
            for cp in flush:
                cp.wait()


def _middle_step(rider, n_steps):
    return min(n_steps - 2, int(getattr(rider, "middle_at", 1.0) * n_steps))


def _rider_specs(rider, rider_ins):
    if rider is None:
        return [], [], [], []
    scratch = [pltpu.VMEM(s.shape, s.dtype) for s in list(rider.out_shape) + list(getattr(rider, "work_shape", ()))]
    scratch += [pltpu.SemaphoreType.DMA((rider.n_sems,)), pltpu.SemaphoreType.DMA((rider.n_sems,)),
                pltpu.SemaphoreType.DMA((getattr(rider, "n_flush", len(rider.out_shape)),))]
    in_space = getattr(rider, "in_space", VMEM_WHOLE)
    return [in_space] * len(rider_ins), [ANY] * len(rider.out_shape), list(rider.out_shape), scratch


def _mixer_fwd(pm, p, rider=None, rider_ins=()):
    n_p = len(_P_NAMES)
    r_in, r_out_specs, r_out_shape, r_sems = _rider_specs(rider, rider_ins)

    def body(*refs):
        (pm_ref, xprev_ref), p_list, ride_in, (ab_ref,), so_refs, ride_out, sc_refs, dense_list, sems = _split(
            refs, 2, n_p, len(r_in), 1, 4, len(r_out_specs), 4, 3, len(r_sems))
        p_refs = dict(zip(_P_NAMES, p_list))
        dense = dict(zip(_P_BLOCKDIAG, dense_list))
        n = pl.program_id(0)
        _ride(rider, ("first",), n == 0, ride_in, ride_out, sems)

        @pl.when(n == 0)
        def _():
            for r in sc_refs:
                r[...] = jnp.zeros_like(r)
            for nm in _P_BLOCKDIAG:
                _expand_blockdiag(p_refs[nm], dense[nm])

        st = {name: _per_head(r) for name, r in zip(_S_NAMES, sc_refs)}
        pv = {nm: (_per_head(dense[nm]) if nm in _P_BLOCKDIAG else p_refs[nm][...]) for nm in _P_NAMES}
        for name, r in zip(_S_NAMES, so_refs):
            for h in range(HEADS):
                r[0, h] = st[name][h]
        xprev8 = jnp.where(n > 0, xprev_ref[CHUNK - 8:CHUNK, :], 0.0)
        ab, st = _mixer_chunk(_PLAIN_OPS, pv, st, pm_ref[...], xprev8)
        ab_ref[...] = ab.astype(BF16)
        for name, r in zip(_S_NAMES, sc_refs):
            for h in range(HEADS):
                r[h] = st[name][h]
        _ride(rider, ("middle",), n == _middle_step(rider, N_SWEEP), ride_in, ride_out, sems)
        _ride(rider, ("last",), n == N_SWEEP - 1, ride_in, ride_out, sems)

    in_specs = [pl.BlockSpec((SWEEP * CHUNK, PM_W), lambda i: (i, 0)),
                pl.BlockSpec((CHUNK, 512), lambda i: (jnp.maximum(SWEEP * i - 1, 0), PM_XM // 512))]
    in_specs += [_const_spec(_P_SHAPES[nm]) for nm in _P_NAMES] + r_in
    out_specs = [pl.BlockSpec((SWEEP * CHUNK, 1024), lambda i: (i, 0))]
    out_shape = [jax.ShapeDtypeStruct((SEQ, 1024), BF16)]
    for nm in _S_NAMES:
        shp = _S_SHAPES[nm]
        out_specs.append(pl.BlockSpec((1,) + shp, lambda i: (i, 0, 0, 0)))
        out_shape.append(jax.ShapeDtypeStruct((N_SWEEP,) + shp, F32))
    return pl.pallas_call(
        body, grid=(N_SWEEP,), in_specs=in_specs, out_specs=out_specs + r_out_specs, out_shape=out_shape + r_out_shape,
        scratch_shapes=[pltpu.VMEM(_S_SHAPES[nm], F32) for nm in _S_NAMES]
        + [pltpu.VMEM((HEADS, 128, 128), F32) for _ in _P_BLOCKDIAG] + r_sems,
        compiler_params=_params(("arbitrary",)), name="mixer_fwd",
    )(pm, pm, *[p[nm] for nm in _P_NAMES], *rider_ins)


def _mixer_bwd(pm, dab, states, p, rider=None, rider_ins=()):
    n_p = len(_P_NAMES)
    r_in, r_out_specs, r_out_shape, r_sems = _rider_specs(rider, rider_ins)

    def body(*refs):
        ((pm_ref, xprev_ref, dab_ref), si_refs, p_list, ride_in, (dpm_ref,), dp_list, ride_out, ds_refs, (carry_ref,),
         dense_list, ddense_list, sems) = _split(refs, 3, 4, n_p, len(r_in), 1, n_p, len(r_out_specs), 4, 1, 3, 3, len(r_sems))
        p_refs = dict(zip(_P_NAMES, p_list))
        dp_refs = dict(zip(_P_NAMES, dp_list))
        dense = dict(zip(_P_BLOCKDIAG, dense_list))
        ddense = dict(zip(_P_BLOCKDIAG, ddense_list))
        i = pl.program_id(0)
        blk = N_SWEEP - 1 - i
        _ride(rider, ("first",), i == 0, ride_in, ride_out, sems)

        @pl.when(i == 0)
        def _():
            for r in ds_refs:
                r[...] = jnp.zeros_like(r)
            for nm in _P_NAMES:
                if nm in _P_BLOCKDIAG:
                    ddense[nm][...] = jnp.zeros_like(ddense[nm])
                    _expand_blockdiag(p_refs[nm], dense[nm])
                else:
                    dp_refs[nm][...] = jnp.zeros_like(dp_refs[nm])
            carry_ref[...] = jnp.zeros_like(carry_ref)

        pv = {nm: (_per_head(dense[nm]) if nm in _P_BLOCKDIAG else p_refs[nm][...]) for nm in _P_NAMES}
        dst = {name: _per_head(r) for name, r in zip(_S_NAMES, ds_refs)}
        st = {name: [r[0, h] for h in range(HEADS)] for name, r in zip(_S_NAMES, si_refs)}
        xprev8 = jnp.where(blk > 0, xprev_ref[CHUNK - 8:CHUNK, :], 0.0)
        _, vjp = jax.vjp(functools.partial(_mixer_chunk, _VJP_OPS), pv, st, pm_ref[...], xprev8)
        dp_sum, dst, dpm, dxprev8 = vjp((dab_ref[...], dst))
        reach = jnp.concatenate([jnp.zeros((SWEEP * CHUNK - 8, 512), F32), carry_ref[...]], axis=0)
        dpm_ref[:, 0:PM_XM] = dpm[:, 0:PM_XM].astype(BF16)
        dpm_ref[:, PM_XM:PM_XM + 512] = (dpm[:, PM_XM:PM_XM + 512] + reach).astype(BF16)
        dpm_ref[:, PM_XM + 512:PM_W] = dpm[:, PM_XM + 512:PM_W].astype(BF16)
        carry_ref[...] = dxprev8
        for name, r in zip(_S_NAMES, ds_refs):
            for h in range(HEADS):
                r[h] = dst[name][h]
        for nm in _P_NAMES:
            if nm in _P_BLOCKDIAG:
                for h in range(HEADS):
                    ddense[nm][h] += dp_sum[nm][h]
            else:
                dp_refs[nm][...] += dp_sum[nm]

        @pl.when(i == N_SWEEP - 1)
        def _():
            for nm in _P_BLOCKDIAG:
                _collect_blockdiag(ddense[nm], dp_refs[nm])

        _ride(rider, ("middle",), i == _middle_step(rider, N_SWEEP), ride_in, ride_out, sems)
        _ride(rider, ("last",), i == N_SWEEP - 1, ride_in, ride_out, sems)

    rev = lambda i: (N_SWEEP - 1 - i, 0)
    in_specs = [pl.BlockSpec((SWEEP * CHUNK, PM_W), rev),
                pl.BlockSpec((CHUNK, 512), lambda i: (jnp.maximum(SWEEP * (N_SWEEP - 1 - i) - 1, 0), PM_XM // 512)),
                pl.BlockSpec((SWEEP * CHUNK, 1024), rev)]
    for nm in _S_NAMES:
        in_specs.append(pl.BlockSpec((1,) + _S_SHAPES[nm], lambda i: (N_SWEEP - 1 - i, 0, 0, 0)))
    in_specs += [_const_spec(_P_SHAPES[nm]) for nm in _P_NAMES] + r_in
    out_specs = [pl.BlockSpec((SWEEP * CHUNK, PM_W), rev)] + [_const_spec(_P_SHAPES[nm]) for nm in _P_NAMES]
    out_shape = [jax.ShapeDtypeStruct((SEQ, PM_W), BF16)] + [jax.ShapeDtypeStruct(_P_SHAPES[nm], F32) for nm in _P_NAMES]
    res = pl.pallas_call(
        body, grid=(N_SWEEP,), in_specs=in_specs, out_specs=out_specs + r_out_specs, out_shape=out_shape + r_out_shape,
        scratch_shapes=[pltpu.VMEM(_S_SHAPES[nm], F32) for nm in _S_NAMES] + [pltpu.VMEM((8, 512), F32)]
        + [pltpu.VMEM((HEADS, 128, 128), F32) for _ in range(2 * len(_P_BLOCKDIAG))] + r_sems,
        compiler_params=_params(("arbitrary",)), name="mixer_bwd",
    )(pm, pm, dab, *states, *[p[nm] for nm in _P_NAMES], *rider_ins)
    return res[0], dict(zip(_P_NAMES, res[1:1 + n_p])), res[1 + n_p:]


def _tok(width):
    return pl.BlockSpec((TOK_TILE, width), lambda i: (i, 0))


def _once(shape):
    zeros = (0,) * len(shape)
    return pl.BlockSpec(shape, lambda i: zeros, pipeline_mode=pl.Buffered(1))


def _rms_fwd(x):
    r = lax.rsqrt(_mean(x * x) + EPS)
    return x * r, r


def _rms_bwd(dy, xn, r, g):
    gd = dy * g
    return r * (gd - xn * _mean(xn * gd))


def _tiled_call(body, in_specs, out_specs, out_shape, args, name, rider=None, rider_ins=()):
    r_in, r_out_specs, r_out_shape, r_scratch = _rider_specs(rider, rider_ins)
    n_in, n_out = len(in_specs), len(out_specs)

    def hosted(*refs):
        ins, ride_in, outs, ride_out, scratch = _split(refs, n_in, len(r_in), n_out, len(r_out_specs), len(r_scratch))
        i = pl.program_id(0)
        _ride(rider, ("first",), i == 0, ride_in, ride_out, scratch)
        body(*ins, *outs)
        _ride(rider, ("middle",), i == _middle_step(rider, N_TOK_TILE), ride_in, ride_out, scratch)
        _ride(rider, ("last",), i == N_TOK_TILE - 1, ride_in, ride_out, scratch)

    res = pl.pallas_call(
        hosted, grid=(N_TOK_TILE,), in_specs=list(in_specs) + r_in, out_specs=list(out_specs) + r_out_specs,
        out_shape=list(out_shape) + r_out_shape, scratch_shapes=r_scratch,
        compiler_params=_params(("arbitrary",)), name=name,
    )(*args, *rider_ins)
    return res[:n_out], res[n_out:]


def _in_proj(x, g_pre, wt_in, rider=None, rider_ins=()):
    def body(x_ref, g_ref, wt_ref, pm_ref, gab_ref, h_ref):
        xn, _ = _rms_fwd(x_ref[...])
        h = (xn * g_ref[...]).astype(BF16)
        h_ref[...] = h
        pm_ref[:, 0:PM_XM] = _nt(h, wt_ref[0:IN_ALOW, :])
        pm_ref[:, PM_XM:PM_AL] = _nt(h, wt_ref[IN_XM:IN_GATES, :])
        pm_ref[:, PM_AL:PM_W] = _nt(h, wt_ref[IN_ALOW:IN_ALOW + 128, :])
        gab_ref[...] = _nt(h, wt_ref[IN_GATES:D_IN, :])

    return _tiled_call(
        body, [_tok(D_MODEL), _once((1, D_MODEL)), _once((D_IN, D_MODEL))], [_tok(PM_W), _tok(GAB_W), _tok(D_MODEL)],
        [jax.ShapeDtypeStruct((SEQ, PM_W), F32), jax.ShapeDtypeStruct((SEQ, GAB_W), F32),
         jax.ShapeDtypeStruct((SEQ, D_MODEL), BF16)], (x, g_pre, wt_in), "in_proj", rider, rider_ins)


def _merge_fwd(ab, gab, x, w_pa4, w_pb4, w_o, g_post, rider=None, rider_ins=()):
    def body(ab_ref, gab_ref, x_ref, wpa_ref, wpb_ref, wo_ref, g_ref, x1_ref, mix_ref, mg_ref):
        a = ab_ref[:, 0:512]
        b = ab_ref[:, 512:1024]
        for j in range(N_CHIP):
            blk = slice(j * 256, (j + 1) * 256)
            ya = jnp.dot(a, wpa_ref[j], preferred_element_type=F32)
            yb = jnp.dot(b, wpb_ref[j], preferred_element_type=F32)
            sa = _sigmoid(gab_ref[:, j * 256:(j + 1) * 256])
            sb = _sigmoid(gab_ref[:, 1024 + j * 256:1024 + (j + 1) * 256])
            mg_ref[:, blk] = (sa * ya + sb * yb).astype(BF16)
        mix = jnp.dot(mg_ref[...], wo_ref[...], preferred_element_type=F32)
        mix_ref[...] = mix
        mn, _ = _rms_fwd(mix)
        x1_ref[...] = x_ref[...] + mn * g_ref[...]

    return _tiled_call(
        body, [_tok(1024), _tok(GAB_W), _tok(D_MODEL), _once((N_CHIP, 512, 256)), _once((N_CHIP, 512, 256)),
               _once((D_MODEL, D_MODEL)), _once((1, D_MODEL))], [_tok(D_MODEL), _tok(D_MODEL), _tok(D_MODEL)],
        [jax.ShapeDtypeStruct((SEQ, D_MODEL), F32), jax.ShapeDtypeStruct((SEQ, D_MODEL), F32),
         jax.ShapeDtypeStruct((SEQ, D_MODEL), BF16)], (ab, gab, x, w_pa4, w_pb4, w_o, g_post), "merge_fwd", rider, rider_ins)


def _mlp(x1, target, g_pre, g_post, w_up4, w_down_a4, w_down_b4):
    def body(x1_ref, t_ref, gpre_ref, gpost_ref, wup_ref, wda_ref, wdb_ref,
             dx1_ref, u_ref, dd_ref, h2_ref, dpre_ref, dgpost_ref, dgpre_ref, loss_ref):
        @pl.when(pl.program_id(0) == 0)
        def _():
            dgpost_ref[...] = jnp.zeros_like(dgpost_ref)
            dgpre_ref[...] = jnp.zeros_like(dgpre_ref)
            loss_ref[...] = jnp.zeros_like(loss_ref)

        x1 = x1_ref[...]
        gpre = gpre_ref[...]
        gpost = gpost_ref[...]
        xn2, r2 = _rms_fwd(x1)
        h2 = (xn2 * gpre).astype(BF16)
        h2_ref[...] = h2
        rl = []
        d = jnp.zeros((TOK_TILE, D_MODEL), F32)
        for j in range(N_CHIP):
            blk = slice(j * 1024, (j + 1) * 1024)
            r = jnp.maximum(jnp.dot(h2, wup_ref[j], preferred_element_type=F32), 0.0)
            rl.append(r)
            u = (r * r).astype(BF16)
            u_ref[:, blk] = u
            d = d + jnp.dot(u[:, 0:512], wda_ref[j], preferred_element_type=F32)
            d = d + jnp.dot(u[:, 512:1024], wdb_ref[j], preferred_element_type=F32)
        dn, r3 = _rms_fwd(d)
        diff = x1 + dn * gpost - t_ref[...]
        loss_ref[...] += jnp.sum(diff * diff, keepdims=True) * (0.5 / D_MODEL)
        dy = diff * (1.0 / D_MODEL)
        dgpost_ref[...] += jnp.sum(dy * dn, axis=0, keepdims=True)
        dd = _rms_bwd(dy, dn, r3, gpost).astype(BF16)
        dd_ref[...] = dd
        dh2 = jnp.zeros((TOK_TILE, D_MODEL), F32)
        for j in range(N_CHIP):
            blk = slice(j * 1024, (j + 1) * 1024)
            du = jnp.concatenate([_nt(dd, wda_ref[j]), _nt(dd, wdb_ref[j])], axis=1)
            dpre = (du * (2.0 * rl[j])).astype(BF16)
            dpre_ref[:, blk] = dpre
            dh2 = dh2 + _nt(dpre, wup_ref[j])
        dgpre_ref[...] += jnp.sum(dh2 * xn2, axis=0, keepdims=True)
        dx1_ref[...] = dy + _rms_bwd(dh2, xn2, r2, gpre)

    acc = pl.BlockSpec((1, D_MODEL), lambda i: (0, 0))
    return pl.pallas_call(
        body, grid=(N_TOK_TILE,),
        in_specs=[_tok(D_MODEL), _tok(D_MODEL), _once((1, D_MODEL)), _once((1, D_MODEL)),
                  _once((N_CHIP, D_MODEL, 1024)), _once((N_CHIP, 512, D_MODEL)), _once((N_CHIP, 512, D_MODEL))],
        out_specs=[_tok(D_MODEL), _tok(D_FF), _tok(D_MODEL), _tok(D_MODEL), _tok(D_FF), acc, acc,
                   pl.BlockSpec((1, 128), lambda i: (0, 0))],
        out_shape=[jax.ShapeDtypeStruct((SEQ, D_MODEL), F32), jax.ShapeDtypeStruct((SEQ, D_FF), BF16),
                   jax.ShapeDtypeStruct((SEQ, D_MODEL), BF16), jax.ShapeDtypeStruct((SEQ, D_MODEL), BF16),
                   jax.ShapeDtypeStruct((SEQ, D_FF), BF16), jax.ShapeDtypeStruct((1, D_MODEL), F32),
                   jax.ShapeDtypeStruct((1, D_MODEL), F32), jax.ShapeDtypeStruct((1, 128), F32)],
        compiler_params=_params(("arbitrary",)), name="mlp_fwd_bwd",
    )(x1, target, g_pre, g_post, w_up4, w_down_a4, w_down_b4)


def _merge_bwd(dx1, mix, ab, gab, w_pa4, w_pb4, w_o, g_post):
    def body(dx1_ref, mix_ref, ab_ref, gab_ref, wpa_ref, wpb_ref, wo_ref, g_ref,
             dmix_ref, dya_ref, dyb_ref, dgab_ref, dab_ref, dg_ref):
        @pl.when(pl.program_id(0) == 0)
        def _():
            dg_ref[...] = jnp.zeros_like(dg_ref)

        dx1 = dx1_ref[...]
        mn, r = _rms_fwd(mix_ref[...])
        dg_ref[...] += jnp.sum(dx1 * mn, axis=0, keepdims=True)
        dmix = _rms_bwd(dx1, mn, r, g_ref[...]).astype(BF16)
        dmix_ref[...] = dmix
        dmerged = _nt(dmix, wo_ref[...])
        a = ab_ref[:, 0:512]
        b = ab_ref[:, 512:1024]
        da = jnp.zeros((TOK_TILE, 512), F32)
        db = jnp.zeros((TOK_TILE, 512), F32)
        for j in range(N_CHIP):
            blk = slice(j * 256, (j + 1) * 256)
            blk_b = slice(1024 + j * 256, 1024 + (j + 1) * 256)
            dm = dmerged[:, blk]
            ya = jnp.dot(a, wpa_ref[j], preferred_element_type=F32)
            yb = jnp.dot(b, wpb_ref[j], preferred_element_type=F32)
            sa = _sigmoid(gab_ref[:, blk])
            sb = _sigmoid(gab_ref[:, blk_b])
            dya = (dm * sa).astype(BF16)
            dyb = (dm * sb).astype(BF16)
            dya_ref[:, blk] = dya
            dyb_ref[:, blk] = dyb
            dgab_ref[:, blk] = (dm * ya * sa * (1.0 - sa)).astype(BF16)
            dgab_ref[:, blk_b] = (dm * yb * sb * (1.0 - sb)).astype(BF16)
            da = da + _nt(dya, wpa_ref[j])
            db = db + _nt(dyb, wpb_ref[j])
        dab_ref[:, 0:512] = da
        dab_ref[:, 512:1024] = db

    return pl.pallas_call(
        body, grid=(N_TOK_TILE,),
        in_specs=[_tok(D_MODEL), _tok(D_MODEL), _tok(1024), _tok(GAB_W), _once((N_CHIP, 512, 256)),
                  _once((N_CHIP, 512, 256)), _once((D_MODEL, D_MODEL)), _once((1, D_MODEL))],
        out_specs=[_tok(D_MODEL), _tok(D_MODEL), _tok(D_MODEL), _tok(GAB_W), _tok(1024),
                   pl.BlockSpec((1, D_MODEL), lambda i: (0, 0))],
        out_shape=[jax.ShapeDtypeStruct((SEQ, D_MODEL), BF16), jax.ShapeDtypeStruct((SEQ, D_MODEL), BF16),
                   jax.ShapeDtypeStruct((SEQ, D_MODEL), BF16), jax.ShapeDtypeStruct((SEQ, GAB_W), BF16),
                   jax.ShapeDtypeStruct((SEQ, 1024), F32), jax.ShapeDtypeStruct((1, D_MODEL), F32)],
        compiler_params=_params(("arbitrary",)), name="merge_bwd",
    )(dx1, mix, ab, gab, w_pa4, w_pb4, w_o, g_post)


def _in_proj_bwd(dpm, dgab, x, dx1, g_pre, wt_in, rider=None, rider_ins=()):
    def body(dpm_ref, dgab_ref, x_ref, dx1_ref, g_ref, wt_ref, dx_ref, dg_ref):
        @pl.when(pl.program_id(0) == 0)
        def _():
            dg_ref[...] = jnp.zeros_like(dg_ref)

        dh = jnp.dot(dpm_ref[:, 0:PM_XM], wt_ref[0:IN_ALOW, :], preferred_element_type=F32)
        dh = dh + jnp.dot(dpm_ref[:, PM_XM:PM_AL], wt_ref[IN_XM:IN_GATES, :], preferred_element_type=F32)
        dh = dh + jnp.dot(dpm_ref[:, PM_AL:PM_W], wt_ref[IN_ALOW:IN_ALOW + 128, :], preferred_element_type=F32)
        dh = dh + jnp.dot(dgab_ref[...], wt_ref[IN_GATES:D_IN, :], preferred_element_type=F32)
        xn, r = _rms_fwd(x_ref[...])
        dg_ref[...] += jnp.sum(dh * xn, axis=0, keepdims=True)
        dx_ref[...] = dx1_ref[...] + _rms_bwd(dh, xn, r, g_ref[...])

    return _tiled_call(
        body, [_tok(PM_W), _tok(GAB_W), _tok(D_MODEL), _tok(D_MODEL), _once((1, D_MODEL)), _once((D_IN, D_MODEL))],
        [_tok(D_MODEL), pl.BlockSpec((1, D_MODEL), lambda i: (0, 0))],
        [jax.ShapeDtypeStruct((SEQ, D_MODEL), F32), jax.ShapeDtypeStruct((1, D_MODEL), F32)],
        (dpm, dgab, x, dx1, g_pre, wt_in), "in_proj_bwd", rider, rider_ins)


def _dw_in(dpm, dgab, h):
    n_pm = PM_AL // 512
    n_blk = n_pm + GAB_W // 512

    def body(dpm_ref, dgab_ref, dal_ref, h_ref, o_ref):
        i = pl.program_id(0)
        off = pl.multiple_of(i * 512 + 16 * (i >= 3).astype(jnp.int32), 16)

        @pl.when(i < n_pm)
        def _():
            o_ref[pl.ds(off, 512), :] = _tn(dpm_ref[...], h_ref[...]).astype(BF16)

        @pl.when(i >= n_pm)
        def _():
            o_ref[pl.ds(off, 512), :] = _tn(dgab_ref[...], h_ref[...]).astype(BF16)

        @pl.when(i == 0)
        def _():
            o_ref[IN_ALOW:IN_XM, :] = _tn(dal_ref[...], h_ref[...])[0:IN_XM - IN_ALOW].astype(BF16)

    return pl.pallas_call(
        body, grid=(n_blk,),
        in_specs=[pl.BlockSpec((SEQ, 512), lambda i: (0, jnp.minimum(i, n_pm - 1))),
                  pl.BlockSpec((SEQ, 512), lambda i: (0, jnp.maximum(i - n_pm, 0))),
                  pl.BlockSpec((SEQ, 128), lambda i: (0, PM_AL // 128)),
                  _once((SEQ, D_MODEL))],
        out_specs=pl.BlockSpec((D_IN, D_MODEL), lambda i: (0, 0)),
        out_shape=jax.ShapeDtypeStruct((D_IN, D_MODEL), BF16),
        compiler_params=_params(("arbitrary",)), name="dw_in",
    )(dpm, dgab, dpm, h)


def _tn_matmul(a, b, name, shards=1, tm=512, rider=None, rider_ins=()):
    m, n = a.shape[1], b.shape[1]
    tm = min(tm, m)
    tn = n // shards if shards > 1 else min(n, 1024)
    steps_i, steps_j = m // tm, n // tn
    r_in, r_out_specs, r_out_shape, r_scratch = _rider_specs(rider, rider_ins)

    def body(*refs):
        (a_ref, b_ref), ride_in, (o_ref,), ride_out, scratch = _split(refs, 2, len(r_in), 1, len(r_out_specs), len(r_scratch))
        step = pl.program_id(0) * steps_j + pl.program_id(1)
        _ride(rider, ("first",), step == 0, ride_in, ride_out, scratch)
        o_ref[...] = _tn(a_ref[...], b_ref[...]).astype(BF16)
        _ride(rider, ("middle", "last"), step == steps_i * steps_j - 1, ride_in, ride_out, scratch)

    if shards > 1:
        out_spec = pl.BlockSpec((None, tm, tn), lambda i, j: (j, i, 0))
        out_shape = jax.ShapeDtypeStruct((shards, m, tn), BF16)
    else:
        out_spec = pl.BlockSpec((tm, tn), lambda i, j: (i, j))
        out_shape = jax.ShapeDtypeStruct((m, n), BF16)
    res = pl.pallas_call(
        body, grid=(steps_i, steps_j),
        in_specs=[pl.BlockSpec((SEQ, tm), lambda i, j: (0, i)), pl.BlockSpec((SEQ, tn), lambda i, j: (0, j))] + r_in,
        out_specs=[out_spec] + r_out_specs, out_shape=[out_shape] + r_out_shape, scratch_shapes=r_scratch,
        compiler_params=_params(("arbitrary", "arbitrary")), name=name,
    )(a, b, *rider_ins)
    return res[0] if rider is None else (res[0], res[1:])


MESH = pl.DeviceIdType.MESH
ANY = pl.BlockSpec(memory_space=pl.ANY)
VMEM_WHOLE = pl.BlockSpec(memory_space=pltpu.VMEM)

_BIG = ("w_in", "w_pa", "w_pb", "w_o", "w_up", "w_down")
_BIG_SHARD = {"w_in": (IN_SHARD, D_MODEL), "w_pa": (512, 256), "w_pb": (512, 256), "w_o": (256, D_MODEL),
              "w_up": (D_MODEL, 1024), "w_down": (1024, D_MODEL),
              "w_down_a": (512, D_MODEL), "w_down_b": (512, D_MODEL)}
_BIG_SPLIT = {"w_in": 1, "w_pa": 0, "w_pb": 0, "w_o": 0, "w_up": 0, "w_down": 0, "w_down_a": 0, "w_down_b": 0}


def _half(ref, e, name, lead=0, part=None):
    axis = _BIG_SPLIT[name]
    size = _BIG_SHARD[name][axis] // 2
    start = e * size
    if part is not None:
        size //= 2
        start = start + part * size
    start = pl.multiple_of(start, 128 if axis == 1 else 16)
    idx = [pl.ds(0, ref.shape[a]) for a in range(lead)]
    idx += [pl.ds(start, size), pl.ds(0, _BIG_SHARD[name][1])] if axis == 0 else [pl.ds(0, _BIG_SHARD[name][0]), pl.ds(start, size)]
    return ref.at[tuple(idx)]


def _half_shape(name):
    r, c = _BIG_SHARD[name]
    return (r // 2, c) if _BIG_SPLIT[name] == 0 else (r, c // 2)


def _remote(src, dst, send_sems, recv_sems, k, to):
    return pltpu.make_async_remote_copy(src_ref=src, dst_ref=dst, send_sem=send_sems.at[k], recv_sem=recv_sems.at[k],
                                        device_id=to, device_id_type=MESH)


def _mesh_place():
    x, y, c = lax.axis_index("x"), lax.axis_index("y"), lax.axis_index("c")
    return x, y, c, [(1 - x, y), (x, 1 - y), (1 - x, 1 - y)]


class _Gather:
    def __init__(self, names, small=(), middle_at=0.5):
        self.middle_at = middle_at
        self.names = tuple(names)
        self.nb = len(self.names)
        self.n = self.nb + len(small)
        self.n_sems = 8 * self.nb + 3 * len(small)
        self.n_flush = 6 * self.nb + len(small)
        self.out_shape = [jax.ShapeDtypeStruct((N_CHIP,) + _BIG_SHARD[nm], BF16) for nm in self.names]
        self.out_shape += [jax.ShapeDtypeStruct((N_CHIP,) + s.shape, s.dtype) for s in small]

    def _copies(self, ins, outs, ss, rs, k):
        x, y, c, _ = _mesh_place()
        name = self.names[k]
        me, xn, yn, dg = 2 * x + y, 2 * (1 - x) + y, 2 * x + (1 - y), 2 * (1 - x) + (1 - y)
        to_x, to_y, sibling = (1 - x, y, c), (x, 1 - y, c), (x, y, 1 - c)

        def region(slot, e, part=None):
            return _half(outs[k].at[slot], e, name, part=part)

        def copy(pair, src, dst, to):
            return _remote(src, dst, ss, rs, 8 * k + pair, to)

        mine = _half(ins[k], c, name)
        sent = [copy(0, mine, region(me, c), to_x), copy(1, mine, region(me, c), to_y),
                copy(2, region(xn, c, 0), region(xn, c, 0), to_y), copy(3, region(yn, c, 1), region(yn, c, 1), to_x),
                copy(4, region(xn, c), region(xn, c), sibling), copy(5, region(yn, c), region(yn, c), sibling),
                copy(6, region(dg, c, 0), region(dg, c, 0), sibling), copy(7, region(dg, c, 1), region(dg, c, 1), sibling)]
        landing = [region(xn, c), region(yn, c), region(dg, c, 0), region(dg, c, 1),
                   region(xn, 1 - c), region(yn, 1 - c), region(dg, 1 - c, 0), region(dg, 1 - c, 1)]
        received = [copy(pair, dst, dst, sibling) for pair, dst in enumerate(landing)]
        return sent, received

    def _small(self, ins, outs, ss, rs, k, j, peer, slot, c):
        return _remote(ins[k], outs[k].at[slot], ss, rs, 8 * self.nb + 3 * (k - self.nb) + j, (*peer, c))

    def flush(self, phase, lands, outs, fs):
        x, y, c, _ = _mesh_place()
        me, xn, yn, dg = 2 * x + y, 2 * (1 - x) + y, 2 * x + (1 - y), 2 * (1 - x) + (1 - y)

        def pieces(k):
            name = self.names[k]
            spots = [lambda r: r.at[me], lambda r: _half(r.at[xn], c, name), lambda r: _half(r.at[yn], c, name),
                     lambda r: _half(r.at[xn], 1 - c, name), lambda r: _half(r.at[yn], 1 - c, name), lambda r: r.at[dg]]
            return [pltpu.make_async_copy(spot(lands[k]), spot(outs[k]), fs.at[6 * k + t]) for t, spot in enumerate(spots)]

        ready = {"first": (0,), "middle": (1, 2), "last": (3, 4, 5)}[phase]
        for k in range(self.nb):
            cps = pieces(k)
            for t in ready:
                cps[t].start()
        if phase == "last":
            small = [pltpu.make_async_copy(lands[k], outs[k], fs.at[6 * self.nb + k - self.nb]) for k in range(self.nb, self.n)]
            for cp in small:
                cp.start()
            for k in range(self.nb):
                for cp in pieces(k):
                    cp.wait()
            for cp in small:
                cp.wait()

    def first(self, ins, outs, ss, rs):
        x, y, c, peers = _mesh_place()
        me = 2 * x + y
        for k in range(self.nb):
            sent, _ = self._copies(ins, outs, ss, rs, k)
            sent[0].start()
            sent[1].start()
        for k in range(self.nb, self.n):
            for j, peer in enumerate(peers):
                self._small(ins, outs, ss, rs, k, j, peer, me, c).start()
        for k in range(self.n):
            outs[k][me] = ins[k][...]

    def middle(self, ins, outs, ss, rs):
        for k in range(self.nb):
            sent, received = self._copies(ins, outs, ss, rs, k)
            for pair in (0, 1):
                received[pair].wait_recv()
                sent[2 + pair].start()
                sent[4 + pair].start()

    def last(self, ins, outs, ss, rs):
        x, y, c, peers = _mesh_place()
        for k in range(self.nb):
            sent, received = self._copies(ins, outs, ss, rs, k)
            for pair in (2, 3):
                received[pair].wait_recv()
                sent[4 + pair].start()
        for k in range(self.nb):
            sent, received = self._copies(ins, outs, ss, rs, k)
            for pair in range(4, 8):
                received[pair].wait_recv()
            for cp in sent:
                cp.wait_send()
        for k in range(self.nb, self.n):
            for j, (px, py) in enumerate(peers):
                self._small(ins, outs, ss, rs, k, j, (px, py), 2 * px + py, c).wait_recv()
                self._small(ins, outs, ss, rs, k, j, (px, py), 2 * x + y, c).wait_send()


def _run_alone(rider, ins, name):
    def body(*refs):
        r_in, r_out, sems = _split(refs, len(ins), len(rider.out_shape), 2)
        rider.first(r_in, r_out, *sems)
        rider.middle(r_in, r_out, *sems)
        rider.last(r_in, r_out, *sems)

    return pl.pallas_call(
        body, in_specs=[VMEM_WHOLE] * len(ins), out_specs=[VMEM_WHOLE] * len(rider.out_shape), out_shape=rider.out_shape,
        scratch_shapes=[pltpu.SemaphoreType.DMA((rider.n_sems,)), pltpu.SemaphoreType.DMA((rider.n_sems,))],
        compiler_params=_params(), name=name,
    )(*ins)


class _Presum:
    in_space = ANY

    def __init__(self, names):
        self.names = tuple(names)
        self.n = len(self.names)
        self.n_sems = 3 * self.n
        self.out_shape = [jax.ShapeDtypeStruct((N_CHIP,) + _half_shape(nm), BF16) for nm in self.names]
        self.work_shape = self.out_shape + self.out_shape

    def _stage(self, ins, bufs, ss, k, e, which):
        n = self.n
        return pltpu.make_async_copy(_half(ins[k], e, self.names[k], lead=1), bufs[which * n + k], ss.at[which * n + k])

    def _give(self, bufs, ss, rs, k, sibling):
        return _remote(bufs[self.n + k], bufs[k], ss, rs, k, sibling)

    def first(self, ins, bufs, ss, rs):
        x, y, c, _ = _mesh_place()
        for k in range(self.n):
            self._stage(ins, bufs, ss, k, 1 - c, 1).start()
        for k in range(self.n):
            self._stage(ins, bufs, ss, k, c, 2).start()
        for k in range(self.n):
            self._stage(ins, bufs, ss, k, 1 - c, 1).wait()
            self._give(bufs, ss, rs, k, (x, y, 1 - c)).start()

    def middle(self, ins, bufs, ss, rs):
        pass

    def last(self, ins, bufs, ss, rs):
        x, y, c, _ = _mesh_place()
        for k in range(self.n):
            self._give(bufs, ss, rs, k, (x, y, 1 - c)).wait_recv()
            self._stage(ins, bufs, ss, k, c, 2).wait()

            @pl.loop(0, N_CHIP)
            def _(j):
                bufs[k][j] = (bufs[k][j].astype(F32) + bufs[2 * self.n + k][j].astype(F32)).astype(BF16)
        for k in range(self.n):
            self._give(bufs, ss, rs, k, (x, y, 1 - c)).wait_send()


def _presum(names, grads, name):
    rider = _Presum(names)
    n = rider.n

    def body(*refs):
        g_refs, got_refs, work_refs, sems = _split(refs, n, n, 2 * n, 2)
        bufs = list(got_refs) + list(work_refs)
        rider.first(g_refs, bufs, *sems)
        rider.last(g_refs, bufs, *sems)

    return pl.pallas_call(
        body, in_specs=[ANY] * n, out_specs=[VMEM_WHOLE] * n, out_shape=rider.out_shape,
        scratch_shapes=[pltpu.VMEM(s.shape, s.dtype) for s in rider.work_shape]
        + [pltpu.SemaphoreType.DMA((rider.n_sems,)), pltpu.SemaphoreType.DMA((rider.n_sems,))],
        compiler_params=_params(), name=name,
    )(*grads)


class _ReduceRelay:
    middle_at = 0.75

    def __init__(self, names):
        self.names = tuple(names)
        self.n = len(self.names)
        self.n_sems = 6 * self.n
        self.out_shape = [jax.ShapeDtypeStruct((N_CHIP,) + _half_shape(nm), BF16) for nm in self.names]
        quarter = [jax.ShapeDtypeStruct(self._part_shape(nm), BF16) for nm in self.names]
        self.work_shape = quarter + quarter

    @staticmethod
    def _part_shape(name):
        r, c = _half_shape(name)
        return (r // 2, c) if _BIG_SPLIT[name] == 0 else (r, c // 2)

    def _part(self, ref, name, p):
        r, c = self._part_shape(name)
        return ref.at[pl.ds(p * r, r), pl.ds(0, c)] if _BIG_SPLIT[name] == 0 else ref.at[pl.ds(0, r), pl.ds(p * c, c)]

    def _copies(self, ins, bufs, ss, rs, k):
        x, y, c, _ = _mesh_place()
        name, n = self.names[k], self.n
        me, xn, yn, dg = 2 * x + y, 2 * (1 - x) + y, 2 * x + (1 - y), 2 * (1 - x) + (1 - y)
        to_x, to_y = (1 - x, y, c), (x, 1 - y, c)
        mine = lambda slot, p: self._part(ins[k].at[slot], name, p)
        slot = lambda s, p: self._part(bufs[k].at[s], name, p)
        from_x, from_y = bufs[n + k], bufs[2 * n + k]

        def copy(pair, src, dst, to):
            return _remote(src, dst, ss, rs, 6 * k + pair, to)

        sent = [copy(0, mine(dg, 0), from_x, to_x), copy(1, mine(dg, 1), from_y, to_y),
                copy(2, mine(xn, 0), slot(me, 0), to_x), copy(3, mine(yn, 1), slot(me, 1), to_y),
                copy(4, from_y, slot(me, 1), to_x), copy(5, from_x, slot(me, 0), to_y)]
        landing = [from_x, from_y, slot(xn, 0), slot(yn, 1), slot(xn, 1), slot(yn, 0)]
        received = [copy(pair, dst, dst, to_x) for pair, dst in enumerate(landing)]
        return sent, received

    def first(self, ins, bufs, ss, rs):
        x, y, c, _ = _mesh_place()
        me, dg = 2 * x + y, 2 * (1 - x) + (1 - y)
        for k in range(self.n):
            sent, _ = self._copies(ins, bufs, ss, rs, k)
            for pair in range(4):
                sent[pair].start()
        for k in range(self.n):
            bufs[k][me] = ins[k][me]
            bufs[k][dg] = jnp.zeros(_half_shape(self.names[k]), BF16)

    def middle(self, ins, bufs, ss, rs):
        x, y, c, _ = _mesh_place()
        xn, yn = 2 * (1 - x) + y, 2 * x + (1 - y)
        for k in range(self.n):
            sent, received = self._copies(ins, bufs, ss, rs, k)
            name, n = self.names[k], self.n
            for pair, buf, own in ((0, bufs[n + k], self._part(ins[k].at[yn], name, 0)),
                                   (1, bufs[2 * n + k], self._part(ins[k].at[xn], name, 1))):
                received[pair].wait_recv()
                buf[...] = (buf[...].astype(F32) + own[...].astype(F32)).astype(BF16)
            sent[5].start()
            sent[4].start()

    def last(self, ins, bufs, ss, rs):
        for k in range(self.n):
            sent, received = self._copies(ins, bufs, ss, rs, k)
            for pair in range(2, 6):
                received[pair].wait_recv()
            for cp in sent:
                cp.wait_send()


class _SendPartials:
    def __init__(self, names, small_shape=None):
        self.n = len(names)
        self.small = small_shape is not None
        self.n_sems = 3 * self.n + 7
        self.out_shape = [jax.ShapeDtypeStruct((N_CHIP,) + _half_shape(nm), BF16) for nm in names]
        if self.small:
            self.out_shape.append(jax.ShapeDtypeStruct((N_DEV,) + small_shape, F32))

    def _piece(self, ins, outs, ss, rs, k, j, peer, src_slot, dst_slot, c):
        return _remote(ins[k].at[src_slot], outs[k].at[dst_slot], ss, rs, 3 * k + j, (*peer, c))

    def _small(self, ins, outs, ss, rs, r, other, slot):
        return _remote(ins[self.n], outs[self.n].at[slot], ss, rs, 3 * self.n + r, other)

    @staticmethod
    def _others(x, y, c):
        return [(x, y, 1 - c), (1 - x, y, c), (1 - x, y, 1 - c), (x, 1 - y, c), (x, 1 - y, 1 - c),
                (1 - x, 1 - y, c), (1 - x, 1 - y, 1 - c)]

    def first(self, ins, outs, ss, rs):
        x, y, c, peers = _mesh_place()
        me = 2 * x + y
        for k in range(self.n):
            for j, (px, py) in enumerate(peers):
                self._piece(ins, outs, ss, rs, k, j, (px, py), 2 * px + py, me, c).start()
        if self.small:
            for r, other in enumerate(self._others(x, y, c)):
                self._small(ins, outs, ss, rs, r, other, 4 * x + 2 * y + c).start()
            outs[self.n][4 * x + 2 * y + c] = ins[self.n][...]
        for k in range(self.n):
            outs[k][me] = ins[k][me]

    def middle(self, ins, outs, ss, rs):
        pass

    def last(self, ins, outs, ss, rs):
        x, y, c, peers = _mesh_place()
        me = 2 * x + y
        for k in range(self.n):
            for j, (px, py) in enumerate(peers):
                self._piece(ins, outs, ss, rs, k, j, (px, py), me, 2 * px + py, c).wait_recv()
                self._piece(ins, outs, ss, rs, k, j, (px, py), 2 * px + py, me, c).wait_send()
        if self.small:
            for r, (px, py, pc) in enumerate(self._others(x, y, c)):
                self._small(ins, outs, ss, rs, r, (px, py, pc), 4 * px + 2 * py + pc).wait_recv()
                self._small(ins, outs, ss, rs, r, (px, py, pc), 4 * x + 2 * y + c).wait_send()


def _sum_swap(names, parts, small):
    n = len(parts)
    everyone = _SendPartials((), small.shape)

    def body(*refs):
        p_refs, (small_ref,), o_refs, (osmall_ref,), (all_ref,), (send_sems, recv_sems, ss_small, rs_small) = _split(
            refs, n, 1, n, 1, 1, 4)
        x, y, c = lax.axis_index("x"), lax.axis_index("y"), lax.axis_index("c")
        everyone.first([small_ref], [all_ref], ss_small, rs_small)

        def mine(k):
            part = _half(o_refs[k], c, names[k])
            return _remote(part, part, send_sems, recv_sems, k, (x, y, 1 - c))

        for k in range(n):
            for e in range(2):
                @pl.when(c == e)
                def _():
                    g = p_refs[k][0].astype(F32)
                    for s in range(1, N_CHIP):
                        g = g + p_refs[k][s].astype(F32)
                    r, cols = _half_shape(names[k])
                    if _BIG_SPLIT[names[k]] == 0:
                        o_refs[k][e * r:(e + 1) * r, :] = g
                    else:
                        o_refs[k][:, e * cols:(e + 1) * cols] = g
            mine(k).start()
        for k in range(n):
            theirs = _half(o_refs[k], 1 - c, names[k])
            _remote(theirs, theirs, send_sems, recv_sems, k, (x, y, 1 - c)).wait_recv()
            mine(k).wait_send()
        everyone.last([small_ref], [all_ref], ss_small, rs_small)
        g = all_ref[0]
        for d in range(1, N_DEV):
            g = g + all_ref[d]
        osmall_ref[...] = g

    res = pl.pallas_call(
        body, in_specs=[VMEM_WHOLE] * (n + 1), out_specs=[VMEM_WHOLE] * (n + 1),
        out_shape=[jax.ShapeDtypeStruct(_BIG_SHARD[nm], F32) for nm in names] + [jax.ShapeDtypeStruct(small.shape, F32)],
        scratch_shapes=[pltpu.VMEM((N_DEV,) + small.shape, F32), pltpu.SemaphoreType.DMA((n,)), pltpu.SemaphoreType.DMA((n,)),
                        pltpu.SemaphoreType.DMA((everyone.n_sems,)), pltpu.SemaphoreType.DMA((everyone.n_sems,))],
        compiler_params=_params(), name="sum_swap",
    )(*parts, small)
    return res[:n], res[n]


def _tile(rows, cols, itemsize, budget):
    t = cols if rows % 16 else rows
    other = rows if rows % 16 else cols
    step = 256 if rows % 16 else 32
    while t % step == 0 and t * other * itemsize > budget:
        t //= 2
    return (rows, t) if rows % 16 else (t, cols)


def _adamw_math(w, g, m, v):
    m = ADAM_B1 * m + (1.0 - ADAM_B1) * g
    v = ADAM_B2 * v + (1.0 - ADAM_B2) * (g * g)
    m_hat = m / (1.0 - ADAM_B1 ** ADAM_STEP)
    v_hat = v / (1.0 - ADAM_B2 ** ADAM_STEP)
    delta = -ADAM_LR * (m_hat / (jnp.sqrt(v_hat) + ADAM_EPS) + ADAM_WD * w)
    return delta, m, v


def _adamw_big(g, w, m, v, name):
    r, c = w.shape
    tr, tc = _tile(r, c, 4, 1024 * 1024)

    def body(g_ref, w_ref, m_ref, v_ref, d_ref, nm_ref, nv_ref):
        d_ref[...], nm_ref[...], nv_ref[...] = _adamw_math(w_ref[...], g_ref[...], m_ref[...], v_ref[...])

    blk = pl.BlockSpec((tr, tc), lambda i, l: (i, l))
    return pl.pallas_call(
        body, grid=(r // tr, c // tc), in_specs=[blk, blk, blk, blk],
        out_specs=[blk, blk, blk], out_shape=[jax.ShapeDtypeStruct((r, c), F32)] * 3,
        compiler_params=_params(("arbitrary", "arbitrary")), name=name,
    )(g, w, m, v)


def _adamw_rows(g, w, m, v, name):
    r, k, lanes = w.shape
    tr = 296

    def body(g_ref, w_ref, m_ref, v_ref, g3_ref, d_ref, nm_ref, nv_ref):
        g = g_ref[...].reshape(tr, k, lanes)
        g3_ref[...] = g
        d_ref[...], nm_ref[...], nv_ref[...] = _adamw_math(w_ref[...], g, m_ref[...], v_ref[...])

    rows = pl.BlockSpec((tr, k, lanes), lambda i: (i, 0, 0))
    return pl.pallas_call(
        body, grid=(pl.cdiv(r, tr),), in_specs=[pl.BlockSpec((tr, k * lanes), lambda i: (i, 0)), rows, rows, rows],
        out_specs=[rows] * 4, out_shape=[jax.ShapeDtypeStruct((r, k, lanes), F32)] * 4,
        compiler_params=_params(("arbitrary",)), name=name,
    )(g, w, m, v)


def _adamw_small(ws, gs, ms, vs):
    n = len(ws)

    def body(*refs):
        w_refs, g_refs, m_refs, v_refs, d_refs, nm_refs, nv_refs = _split(refs, *([n] * 7))
        for k in range(n):
            d_refs[k][...], nm_refs[k][...], nv_refs[k][...] = _adamw_math(w_refs[k][...], g_refs[k][...], m_refs[k][...],
                                                                             v_refs[k][...])

    shapes = [jax.ShapeDtypeStruct(w.shape, F32) for w in ws]
    res = pl.pallas_call(body, out_shape=shapes * 3, name="adamw_small")(*ws, *gs, *ms, *vs)
    return res[:n], res[n:2 * n], res[2 * n:]


def _pack(arrs):
    flat = jnp.concatenate([a.reshape(-1) for a in arrs])
    rows = -(-flat.shape[0] // 1024) * 8
    return jnp.pad(flat, (0, rows * 128 - flat.shape[0])).reshape(rows, 128)


def _unpack(buf, shapes):
    flat = buf.reshape(-1)
    out, off = [], 0
    for s in shapes:
        size = 1
        for d in s:
            size *= d
        out.append(flat[off:off + size].reshape(s))
        off += size
    return out


def _block_rows(w):
    return jnp.pad(w.reshape(512, 4), ((0, 0), (0, 124)))


def _cols(a4):
    return jnp.transpose(a4, (1, 0, 2)).reshape(a4.shape[1], -1)


_LATE = ("w_pa", "w_pb", "w_o", "w_up", "w_down")
_RIDE_IN_PROJ = ("w_pa", "w_pb", "w_o")
_RIDE_MIXER = ("w_up", "w_down_a")
_RIDE_MERGE = ("w_down_b",)


def _full_weights(gathered):
    joined = {"w_in": (D_IN, D_MODEL), "w_o": (D_MODEL, D_MODEL)}
    return {n: (a.reshape(joined[n]) if n in joined else a) for n, a in gathered.items()}


def _local_step(x, target, w, sp, late_shards=None):
    sp = {n: (a.reshape(1, -1) if a.ndim == 1 else a) for n, a in sp.items()}
    wau = jnp.zeros((128, 256), F32).at[0:16].set(sp["w_a_up"])
    wif = jnp.zeros((1536, 128), F32).at[:, 0:8].set(sp["w_if"])
    bif = jnp.zeros((1, 128), F32).at[:, 0:8].set(sp["b_if"])
    p = {"wau": wau, "bau": sp["b_a_up"], "ggla": sp["g_gla_norm"], "cw": sp["conv_w"], "cb": sp["conv_b"],
         "wq": _block_rows(sp["w_q_ml"]), "wk": _block_rows(sp["w_k_ml"]), "wv": _block_rows(sp["w_v_ml"]),
         "wif": wif, "bif": bif, "skip": sp["ml_skip"], "gml": sp["g_ml_norm"]}

    if late_shards is None:
        (pm, gab, h), _ = _in_proj(x, sp["g_pre_mix"], w["w_in"])
        ab, *states = _mixer_fwd(pm, p)
        (x1, mix, merged), _ = _merge_fwd(ab, gab, x, w["w_pa"], w["w_pb"], w["w_o"], sp["g_post_mix"])
    else:
        shard = dict(zip(_LATE, late_shards))
        shard["w_down_a"], shard["w_down_b"] = shard["w_down"][0:512], shard["w_down"][512:1024]
        (pm, gab, h), got = _in_proj(x, sp["g_pre_mix"], w["w_in"], _Gather(_RIDE_IN_PROJ, middle_at=0.45),
                                     [shard[n] for n in _RIDE_IN_PROJ])
        w = dict(w, **_full_weights(dict(zip(_RIDE_IN_PROJ, got))))
        ab, *rest = _mixer_fwd(pm, p, _Gather(_RIDE_MIXER, middle_at=0.62), [shard[n] for n in _RIDE_MIXER])
        states = rest[:4]
        w.update(_full_weights(dict(zip(_RIDE_MIXER, rest[4:]))))
        (x1, mix, merged), got = _merge_fwd(ab, gab, x, w["w_pa"], w["w_pb"], w["w_o"], sp["g_post_mix"],
                                            _Gather(_RIDE_MERGE, middle_at=0.46), [shard[n] for n in _RIDE_MERGE])
        w.update(_full_weights(dict(zip(_RIDE_MERGE, got))))
    dx1, u, dd, h2, dpre, dg_post_mlp, dg_pre_mlp, loss = _mlp(x1, target, sp["g_pre_mlp"], sp["g_post_mlp"],
                                                                w["w_up"], w["w_down_a"], w["w_down_b"])
    dmix, dya, dyb, dgab, dab, dg_post_mix = _merge_bwd(dx1, mix, ab, gab, w["w_pa"], w["w_pb"], w["w_o"], sp["g_post_mix"])
    big = {
        "w_pa": _tn_matmul(ab[:, 0:512], dya, "dw_pa", shards=N_CHIP),
        "w_pb": _tn_matmul(ab[:, 512:1024], dyb, "dw_pb", shards=N_CHIP),
        "w_o": _tn_matmul(merged, dmix, "dw_o"),
        "w_up": _tn_matmul(h2, dpre, "dw_up", shards=N_CHIP),
    }
    if late_shards is None:
        big["w_down"] = _tn_matmul(u, dd, "dw_down")
        dpm, dp, _ = _mixer_bwd(pm, dab, states, p)
    else:
        pieces = lambda n: big[n].reshape((N_CHIP,) + _BIG_SHARD[n])
        big["w_down"], partial = _tn_matmul(u, dd, "dw_down", rider=_Presum(_LATE[:4]),
                                            rider_ins=[pieces(n) for n in _LATE[:4]])
        partial = list(partial) + list(_presum(("w_down",), [pieces("w_down")], "presum_w_down"))
        dpm, dp, parts = _mixer_bwd(pm, dab, states, p, _SendPartials(_LATE), partial)
        big = dict(zip(_LATE, parts))
    big["w_in"] = _dw_in(dpm, dgab, h)
    if late_shards is None:
        (dx, dg_pre_mix), _ = _in_proj_bwd(dpm, dgab, x, dx1, sp["g_pre_mix"], w["w_in"])
    else:
        partial = _presum(("w_in",), [big["w_in"].reshape((N_CHIP,) + _BIG_SHARD["w_in"])], "presum_w_in")
        (dx, dg_pre_mix), parts = _in_proj_bwd(dpm, dgab, x, dx1, sp["g_pre_mix"], w["w_in"], _ReduceRelay(("w_in",)),
                                               partial)
        big["w_in"] = parts[0]
    small = {
        "g_pre_mix": dg_pre_mix, "b_a_up": dp["bau"], "g_gla_norm": dp["ggla"], "conv_b": dp["cb"],
        "w_q_ml": dp["wq"][:, 0:4].reshape(128, 4, 4), "w_k_ml": dp["wk"][:, 0:4].reshape(128, 4, 4),
        "w_v_ml": dp["wv"][:, 0:4].reshape(128, 4, 4),
        "b_if": dp["bif"][:, 0:8], "ml_skip": dp["skip"], "g_ml_norm": dp["gml"], "g_post_mix": dg_post_mix,
        "g_pre_mlp": dg_pre_mlp, "g_post_mlp": dg_post_mlp, "w_a_up": dp["wau"][0:16], "conv_w": dp["cw"],
        "w_if": dp["wif"][:, 0:8], "loss": loss[:, 0:1],
    }
    return dx, big, small


_SMALL_REPL = ("g_pre_mix", "b_a_up", "g_gla_norm", "conv_b", "w_q_ml", "w_k_ml", "w_v_ml", "b_if", "ml_skip",
               "g_ml_norm", "g_post_mix", "g_pre_mlp", "g_post_mlp")
_SMALL_SHARDED = ("w_a_up", "conv_w", "w_if")
_SMALL_ORDER = _SMALL_REPL + _SMALL_SHARDED + ("loss",)
_WEIGHTS = ("g_pre_mix", "w_in", "w_a_up", "b_a_up", "g_gla_norm", "conv_w", "conv_b", "w_q_ml", "w_k_ml", "w_v_ml",
            "w_if", "b_if", "ml_skip", "g_ml_norm", "w_pa", "w_pb", "w_o", "g_post_mix", "g_pre_mlp", "w_up", "w_down",
            "g_post_mlp")


_BLOCK_WEIGHTS = ("w_q_ml", "w_k_ml", "w_v_ml")


def _stored(name, a):
    if name in _BLOCK_WEIGHTS:
        return jnp.transpose(a, (0, 2, 3, 1)).reshape(16, 128)
    if name == "w_if":
        return jnp.transpose(a, (0, 2, 1)).reshape(8, 384)
    return a


def _unstored(name, a):
    if name in _BLOCK_WEIGHTS:
        return jnp.transpose(a.reshape(1, 4, 4, 128), (0, 3, 1, 2))
    if name == "w_if":
        return jnp.transpose(a.reshape(1, 8, 384), (0, 2, 1))
    return a


def _as_shard(name, a):
    return jnp.transpose(a, (2, 0, 1)).reshape(IN_SHARD, D_MODEL // 128, 128) if name == "w_in" else a[0]


def _from_shard(name, a):
    return jnp.transpose(a, (1, 2, 0)).reshape(1, D_MODEL, IN_SHARD) if name == "w_in" else a[None]


def kernel(x, g_pre_mix, w_in, w_a_up, b_a_up, g_gla_norm, conv_w, conv_b, w_q_ml, w_k_ml, w_v_ml, w_if, b_if, ml_skip, g_ml_norm, w_pa, w_pb, w_o, g_post_mix, g_pre_mlp, w_up, w_down, g_post_mlp, loss_target, m_g_pre_mix, m_w_in, m_w_a_up, m_b_a_up, m_g_gla_norm, m_conv_w, m_conv_b, m_w_q_ml, m_w_k_ml, m_w_v_ml, m_w_if, m_b_if, m_ml_skip, m_g_ml_norm, m_w_pa, m_w_pb, m_w_o, m_g_post_mix, m_g_pre_mlp, m_w_up, m_w_down, m_g_post_mlp, v_g_pre_mix, v_w_in, v_w_a_up, v_b_a_up, v_g_gla_norm, v_conv_w, v_conv_b, v_w_q_ml, v_w_k_ml, v_w_v_ml, v_w_if, v_b_if, v_ml_skip, v_g_ml_norm, v_w_pa, v_w_pb, v_w_o, v_g_post_mix, v_g_pre_mlp, v_w_up, v_w_down, v_g_post_mlp):
    args = dict(locals())
    wts = {n: _as_shard(n, args[n]) for n in _WEIGHTS}
    mom = {n: _as_shard(n, args["m_" + n]) for n in _WEIGHTS}
    var = {n: _as_shard(n, args["v_" + n]) for n in _WEIGHTS}
    chip = 2 * lax.axis_index("x") + lax.axis_index("y")

    first = ("w_in",) + _SMALL_SHARDED
    gathered = dict(zip(first, _run_alone(_Gather(("w_in",), [wts[n] for n in _SMALL_SHARDED]),
                                          [wts[n].reshape(IN_SHARD, D_MODEL).astype(BF16) if n == "w_in" else wts[n]
                                           for n in first],
                                          "gather_first")))
    sp = {n: wts[n] for n in _SMALL_REPL}
    sp["w_a_up"] = _cols(gathered["w_a_up"])
    sp["conv_w"] = _cols(gathered["conv_w"])
    sp["w_if"] = gathered["w_if"].reshape(1536, 8)

    dx, big, small = _local_step(x[0], loss_target[0], _full_weights({"w_in": gathered["w_in"]}), sp,
                                 late_shards=[wts[n].astype(BF16) for n in _LATE])

    small_shapes = [small[n].shape for n in _SMALL_ORDER]
    packed = _pack([small[n] for n in _SMALL_ORDER])
    sums, small_sum = _sum_swap(_BIG, [big[n] for n in _BIG], packed)

    grads, delta, new_m, new_v = {}, {}, {}, {}
    for n, g in zip(_BIG, sums):
        if n == "w_in":
            g, d, nm, nv = _adamw_rows(g, wts[n], mom[n], var[n], "adamw_" + n)
        else:
            d, nm, nv = _adamw_big(g, wts[n], mom[n], var[n], "adamw_" + n)
        grads[n], delta[n], new_m[n], new_v[n] = (_from_shard(n, a) for a in (g, d, nm, nv))
    summed = dict(zip(_SMALL_ORDER, _unpack(small_sum, small_shapes)))
    loss = summed["loss"].reshape(())
    summed["w_a_up"] = lax.dynamic_slice_in_dim(summed["w_a_up"], chip * 64, 64, axis=1)
    summed["conv_w"] = lax.dynamic_slice_in_dim(summed["conv_w"], chip * 128, 128, axis=1)
    summed["w_if"] = lax.dynamic_slice_in_dim(summed["w_if"], chip * 384, 384, axis=0)
    small_names = _SMALL_REPL + _SMALL_SHARDED
    g_stored = [_stored(n, summed[n].reshape(args[n].shape)) for n in small_names]
    upd = _adamw_small([_stored(n, args[n]) for n in small_names], g_stored,
                       [_stored(n, args["m_" + n]) for n in small_names], [_stored(n, args["v_" + n]) for n in small_names])
    for dst, arrs in zip((grads, delta, new_m, new_v), (g_stored,) + tuple(upd)):
        dst.update({n: _unstored(n, a) for n, a in zip(small_names, arrs)})

    outs = [loss, dx[None]]
    for group in (grads, delta, new_m, new_v):
        outs += [group[n] for n in _WEIGHTS]
    return tuple(outs)
```

```python
import functools

import jax
import jax.numpy as jnp
from jax import lax
from jax.experimental import pallas as pl
from jax.experimental.pallas import tpu as pltpu

F32 = jnp.float32
BF16 = jnp.bfloat16

SEQ = 2048
D_MODEL = 1024
CHUNK = 64
N_CHUNK = SEQ // CHUNK
HEADS = 4
GLA_DK = 64
GLA_DV = 128
ML_DH = 128
D_FF = 4096
EPS = 1e-6
N_CHIP = 4
N_DEV = 8
TOK_TILE = 256
N_TOK_TILE = SEQ // TOK_TILE
SWEEP = 2
assert CHUNK == 64
N_SWEEP = N_CHUNK // SWEEP

PM_W = 2688
PM_XM = 1536
PM_OP = 2048
PM_AL = 2560
GAB_W = 2048
D_IN = 4624
IN_SHARD = D_IN // N_CHIP
IN_ALOW = 1536
IN_XM = 1552
IN_GATES = 2576

ADAM_LR = 0.001
ADAM_B1 = 0.9
ADAM_B2 = 0.999
ADAM_EPS = 1e-08
ADAM_WD = 0.01
ADAM_STEP = 10

VMEM_LIMIT = 56 * 1024 * 1024


def _params(sem=None):
    return pltpu.CompilerParams(dimension_semantics=sem, vmem_limit_bytes=VMEM_LIMIT)


def _dot(a, b, ca, cb):
    return lax.dot_general(a.astype(BF16), b.astype(BF16), (((ca,), (cb,)), ((), ())), preferred_element_type=F32)


def _pmm_nn(a, b):
    return _dot(a, b, 1, 0)


def _pmm_nt(a, b):
    return _dot(a, b, 1, 1)


def _pmm_tn(a, b):
    return _dot(a, b, 0, 0)


def _pcmm(c, x):
    return lax.dot_general(c, x, (((1,), (0,)), ((), ())), precision=lax.Precision.HIGHEST, preferred_element_type=F32)


@jax.custom_vjp
def _mm_nn(a, b):
    return _dot(a, b, 1, 0)


@jax.custom_vjp
def _mm_nt(a, b):
    return _dot(a, b, 1, 1)


@jax.custom_vjp
def _mm_tn(a, b):
    return _dot(a, b, 0, 0)


_mm_nn.defvjp(lambda a, b: (_dot(a, b, 1, 0), (a, b)), lambda r, g: (_mm_nt(g, r[1]), _mm_tn(r[0], g)))
_mm_nt.defvjp(lambda a, b: (_dot(a, b, 1, 1), (a, b)), lambda r, g: (_mm_nn(g, r[1]), _mm_tn(g, r[0])))
_mm_tn.defvjp(lambda a, b: (_dot(a, b, 0, 0), (a, b)), lambda r, g: (_mm_nt(r[1], g), _mm_nn(r[0], g)))


@jax.custom_vjp
def _cmm(c, x):
    return _pcmm(c, x)


_cmm.defvjp(
    lambda c, x: (_pcmm(c, x), c),
    lambda c, g: (jnp.zeros_like(c), lax.dot_general(c, g, (((0,), (0,)), ((), ())), precision=lax.Precision.HIGHEST,
                                                      preferred_element_type=F32)),
)

_PLAIN_OPS = (_pmm_nn, _pmm_nt, _pmm_tn, _pcmm)
_VJP_OPS = (_mm_nn, _mm_nt, _mm_tn, _cmm)


def _sigmoid(x):
    return 0.5 * (jnp.tanh(0.5 * x) + 1.0)


def _log_sigmoid(x):
    return jnp.minimum(x, 0.0) - jnp.log(1.0 + jnp.exp(-jnp.abs(x)))


def _mean(x):
    return jnp.mean(x, axis=-1, keepdims=True)


def _nt(a, b):
    return lax.dot_general(a, b, (((1,), (1,)), ((), ())), preferred_element_type=F32)


def _tn(a, b):
    return lax.dot_general(a, b, (((0,), (0,)), ((), ())), preferred_element_type=F32)


def _mixer_chunk(ops, p, st, pm, xprev8):
    mm_nn, mm_nt, mm_tn, cmm = ops
    n_rows = pm.shape[0]
    n_ch = n_rows // CHUNK
    row = lax.broadcasted_iota(jnp.int32, (n_rows, n_rows), 0)
    col = lax.broadcasted_iota(jnp.int32, (n_rows, n_rows), 1)
    tri = jnp.logical_and((row >> 6) == (col >> 6), row >= col).astype(F32)
    causal = tri[0:CHUNK, 0:CHUNK] > 0.0
    q = pm[:, 0:256]
    k = pm[:, 256:512]
    v = pm[:, 512:1024]
    g = pm[:, 1024:1536]
    xm = pm[:, PM_XM:PM_XM + 512]
    opre = pm[:, PM_OP:PM_OP + 512]
    alow = pm[:, PM_AL:PM_AL + 128]
    hs = range(HEADS)
    cs = range(n_ch)
    pairs = [(i, h) for i in cs for h in hs]
    rs = [slice(i * CHUNK, (i + 1) * CHUNK) for i in cs]
    last = [slice((i + 1) * CHUNK - 1, (i + 1) * CHUNK) for i in cs]
    s6 = [slice(h * GLA_DK, (h + 1) * GLA_DK) for h in hs]
    s12 = [slice(h * 128, (h + 1) * 128) for h in hs]

    xx = jnp.concatenate([xprev8, xm], axis=0)
    pre = p["cb"]
    for j in range(4):
        pre = pre + p["cw"][j:j + 1, :] * xx[5 + j:5 + j + n_rows, :]
    xc = pre * _sigmoid(pre)
    qm = [mm_nn(xc[:, s12[h]], p["wq"][h]) for h in hs]
    km = [mm_nn(xc[:, s12[h]], p["wk"][h]) for h in hs]
    vm = [mm_nn(xm[:, s12[h]], p["wv"][h]) for h in hs]
    qcat = jnp.concatenate(qm, axis=1)
    kcat = jnp.concatenate(km, axis=1)
    vcat = jnp.concatenate(vm, axis=1)
    gates = (mm_nn(qcat, p["wif"][0:512]) + mm_nn(kcat, p["wif"][512:1024]) + mm_nn(vcat, p["wif"][1024:1536])
             + p["bif"])
    lf = _log_sigmoid(gates)
    fc = cmm(tri, lf)
    gates_t = gates.T
    fc_t = fc.T

    la = _log_sigmoid(mm_nn(alow, p["wau"]) + p["bau"]) * (1.0 / 16.0)
    cum = cmm(tri, la)
    cum_last = [cum[last[i], :] for i in cs]
    to_end = jnp.concatenate([cum_last[i] - cum[rs[i], :] for i in cs], axis=0)
    e_pos = jnp.exp(cum)
    e_neg = jnp.exp(-cum)
    qs = q * (GLA_DK ** -0.5)
    qp = qs * e_pos
    qn = qs * e_neg
    kp = k * e_pos
    kn = k * e_neg
    kl = k * jnp.exp(to_end)
    dec = [jnp.exp(cum_last[i]) for i in cs]
    ks = [km[h] * (ML_DH ** -0.5) for h in hs]
    li_c = {(i, h): gates[rs[i], h:h + 1] for i, h in pairs}
    fc_c = {(i, h): fc[rs[i], 4 + h:5 + h] for i, h in pairs}
    f_last = {(i, h): fc[last[i], 4 + h:5 + h] for i, h in pairs}

    a_fwd = {(i, h): mm_nt(qp[rs[i], s6[h]], kn[rs[i], s6[h]]) for i, h in pairs}
    a_bwd = {(i, h): mm_nt(qn[rs[i], s6[h]], kp[rs[i], s6[h]]) for i, h in pairs}
    s_chunk = {(i, h): mm_tn(v[rs[i], s12[h]], kl[rs[i], s6[h]]) for i, h in pairs}
    qk = {(i, h): mm_nt(qm[h][rs[i]], ks[h][rs[i]]) for i, h in pairs}
    a = {ih: f_last[ih] - fc_c[ih] + li_c[ih] for ih in pairs}
    m_loc = {ih: jnp.max(a[ih], axis=0, keepdims=True) for ih in pairs}
    kw = {(i, h): ks[h][rs[i]] * jnp.exp(a[(i, h)] - m_loc[(i, h)]) for i, h in pairs}
    c_chunk = {(i, h): mm_tn(kw[(i, h)], vm[h][rs[i]]) for i, h in pairs}
    mem = {(0, h): st["S"][h] for h in hs}
    c_in = {(0, h): st["C"][h] for h in hs}
    n_in = {(0, h): st["n"][h] for h in hs}
    m_in = {(0, h): st["m"][h][:, 0:1] for h in hs}
    for i, h in pairs:
        mem[(i + 1, h)] = mem[(i, h)] * dec[i][:, s6[h]] + s_chunk[(i, h)]
        m_nx = jnp.maximum(f_last[(i, h)] + m_in[(i, h)], m_loc[(i, h)])
        sp = jnp.exp(f_last[(i, h)] + m_in[(i, h)] - m_nx)
        sl = jnp.exp(m_loc[(i, h)] - m_nx)
        c_in[(i + 1, h)] = sp * c_in[(i, h)] + sl * c_chunk[(i, h)]
        n_in[(i + 1, h)] = sp * n_in[(i, h)] + sl * jnp.sum(kw[(i, h)], axis=0, keepdims=True)
        m_in[(i + 1, h)] = m_nx
    s_new = [mem[(n_ch, h)] for h in hs]
    o_inter = {(i, h): mm_nt(qp[rs[i], s6[h]], mem[(i, h)]) for i, h in pairs}
    q_c = {(i, h): mm_nn(qm[h][rs[i]], c_in[(i, h)]) for i, h in pairs}
    scores = {ih: jnp.where(causal, a_fwd[ih], a_bwd[ih]) for ih in pairs}
    log_d = {(i, h): gates_t[h:h + 1, rs[i]] - jnp.abs(fc_c[(i, h)] - fc_t[4 + h:5 + h, rs[i]]) for i, h in pairs}
    g_int = {ih: fc_c[ih] + m_in[ih] for ih in pairs}
    m_t = {ih: jnp.maximum(g_int[ih], jnp.max(log_d[ih], axis=1, keepdims=True)) for ih in pairs}
    s = {ih: qk[ih] * jnp.exp(log_d[ih] - m_t[ih]) for ih in pairs}
    scl = {ih: jnp.exp(g_int[ih] - m_t[ih]) for ih in pairs}
    o = {(i, h): mm_nn(scores[(i, h)], v[rs[i], s12[h]]) + o_inter[(i, h)] for i, h in pairs}
    num = {(i, h): mm_nn(s[(i, h)], vm[h][rs[i]]) + scl[(i, h)] * q_c[(i, h)] for i, h in pairs}
    o = {ih: o[ih] * lax.rsqrt(_mean(o[ih] * o[ih]) + EPS) * p["ggla"] for ih in pairs}
    gate = g * _sigmoid(g)
    out_a = {(i, h): o[(i, h)] * gate[rs[i], s12[h]] for i, h in pairs}
    den = {(i, h): jnp.sum(s[(i, h)], axis=1, keepdims=True)
           + scl[(i, h)] * jnp.sum(qm[h][rs[i]] * n_in[(i, h)], axis=1, keepdims=True) for i, h in pairs}
    den = {ih: jnp.maximum(jnp.abs(den[ih]), jnp.exp(-m_t[ih])) for ih in pairs}
    open_gate = _sigmoid(opre)
    hc = {(i, h): num[(i, h)] / den[(i, h)] * open_gate[rs[i], s12[h]] for i, h in pairs}
    d0 = {ih: hc[ih] - _mean(hc[ih]) for ih in pairs}
    y = {ih: d0[ih] * lax.rsqrt(_mean(d0[ih] * d0[ih]) + EPS) for ih in pairs}
    skipped = p["skip"] * xc
    out_b = {(i, h): y[(i, h)] * p["gml"][:, s12[h]] + skipped[rs[i], s12[h]] for i, h in pairs}
    ab = jnp.concatenate([jnp.concatenate([out_a[(i, h)] for h in hs] + [out_b[(i, h)] for h in hs], axis=1) for i in cs],
                         axis=0)
    new = {"S": s_new, "C": [c_in[(n_ch, h)] for h in hs], "n": [n_in[(n_ch, h)] for h in hs],
           "m": [jnp.broadcast_to(m_in[(n_ch, h)], (1, ML_DH)) for h in hs]}
    return ab, new


_P_NAMES = ("wau", "bau", "ggla", "cw", "cb", "wq", "wk", "wv", "wif", "bif", "skip", "gml")
_P_SHAPES = {
    "wau": (128, 256), "bau": (1, 256), "ggla": (1, 128), "cw": (4, 512), "cb": (1, 512),
    "wq": (512, 128), "wk": (512, 128), "wv": (512, 128),
    "wif": (1536, 128), "bif": (1, 128), "skip": (1, 512), "gml": (1, 512),
}
_P_BLOCKDIAG = ("wq", "wk", "wv")
_S_NAMES = ("S", "C", "n", "m")
_S_SHAPES = {"S": (HEADS, GLA_DV, GLA_DK), "C": (HEADS, ML_DH, ML_DH), "n": (HEADS, 1, ML_DH), "m": (HEADS, 1, ML_DH)}


def _per_head(ref):
    return [ref[h] for h in range(HEADS)]


def _block_mask():
    r = lax.broadcasted_iota(jnp.int32, (128, 128), 0)
    c = lax.broadcasted_iota(jnp.int32, (128, 128), 1)
    same_block = (r >> 2) == (c >> 2)
    spread = jnp.logical_and(r < 4, (c & 3) == r)
    return same_block.astype(F32), spread.astype(F32)


def _expand_blockdiag(w_ref, dense_ref):
    same_block, spread = _block_mask()
    for h in range(HEADS):
        tiled = _pmm_nn(w_ref[h * 128:(h + 1) * 128, :], spread)
        dense_ref[h] = tiled * same_block


def _collect_blockdiag(ddense_ref, dw_ref):
    same_block, spread = _block_mask()
    for h in range(HEADS):
        dw_ref[h * 128:(h + 1) * 128, :] = lax.dot_general(
            ddense_ref[h] * same_block, spread, (((1,), (1,)), ((), ())), precision=lax.Precision.HIGHEST,
            preferred_element_type=F32)


def _const_spec(shape):
    zeros = (0,) * len(shape)
    return pl.BlockSpec(shape, lambda i: zeros)


def _split(refs, *counts):
    out, at = [], 0
    for c in counts:
        out.append(refs[at:at + c])
        at += c
    assert at == len(refs)
    return out


def _ride(rider, phases, cond, ins, outs, sems):
    if rider is None:
        return
    lands, (send_sems, recv_sems, flush_sems) = sems[:-3], sems[-3:]

    @pl.when(cond)
    def _():
        for phase in phases:
            getattr(rider, phase)(ins, lands, send_sems, recv_sems)
            if hasattr(rider, "flush"):
                rider.flush(phase, lands, outs, flush_sems)
        if "last" in phases and not hasattr(rider, "flush"):
            flush = [pltpu.make_async_copy(lands[k], outs[k], flush_sems.at[k]) for k in range(len(outs))]
            for cp in flush:
                cp.start()
            for cp in flush:
                cp.wait()


def _middle_step(rider, n_steps):
    return min(n_steps - 2, int(getattr(rider, "middle_at", 1.0) * n_steps))


def _rider_specs(rider, rider_ins):
    if rider is None:
        return [], [], [], []
    scratch = [pltpu.VMEM(s.shape, s.dtype) for s in list(rider.out_shape) + list(getattr(rider, "work_shape", ()))]
    scratch += [pltpu.SemaphoreType.DMA((rider.n_sems,)), pltpu.SemaphoreType.DMA((rider.n_sems,)),
                pltpu.SemaphoreType.DMA((getattr(rider, "n_flush", len(rider.out_shape)),))]
    in_space = getattr(rider, "in_space", VMEM_WHOLE)
    return [in_space] * len(rider_ins), [ANY] * len(rider.out_shape), list(rider.out_shape), scratch


def _mixer_fwd(pm, p, rider=None, rider_ins=()):
    n_p = len(_P_NAMES)
    r_in, r_out_specs, r_out_shape, r_sems = _rider_specs(rider, rider_ins)

    def body(*refs):
        (pm_ref, xprev_ref), p_list, ride_in, (ab_ref,), so_refs, ride_out, sc_refs, dense_list, sems = _split(
            refs, 2, n_p, len(r_in), 1, 4, len(r_out_specs), 4, 3, len(r_sems))
        p_refs = dict(zip(_P_NAMES, p_list))
        dense = dict(zip(_P_BLOCKDIAG, dense_list))
        n = pl.program_id(0)
        _ride(rider, ("first",), n == 0, ride_in, ride_out, sems)

        @pl.when(n == 0)
        def _():
            for r in sc_refs:
                r[...] = jnp.zeros_like(r)
            for nm in _P_BLOCKDIAG:
                _expand_blockdiag(p_refs[nm], dense[nm])

        st = {name: _per_head(r) for name, r in zip(_S_NAMES, sc_refs)}
        pv = {nm: (_per_head(dense[nm]) if nm in _P_BLOCKDIAG else p_refs[nm][...]) for nm in _P_NAMES}
        for name, r in zip(_S_NAMES, so_refs):
            for h in range(HEADS):
                r[0, h] = st[name][h]
        xprev8 = jnp.where(n > 0, xprev_ref[CHUNK - 8:CHUNK, :], 0.0)
        ab, st = _mixer_chunk(_PLAIN_OPS, pv, st, pm_ref[...], xprev8)
        ab_ref[...] = ab.astype(BF16)
        for name, r in zip(_S_NAMES, sc_refs):
            for h in range(HEADS):
                r[h] = st[name][h]
        _ride(rider, ("middle",), n == _middle_step(rider, N_SWEEP), ride_in, ride_out, sems)
        _ride(rider, ("last",), n == N_SWEEP - 1, ride_in, ride_out, sems)

    in_specs = [pl.BlockSpec((SWEEP * CHUNK, PM_W), lambda i: (i, 0)),
                pl.BlockSpec((CHUNK, 512), lambda i: (jnp.maximum(SWEEP * i - 1, 0), PM_XM // 512))]
    in_specs += [_const_spec(_P_SHAPES[nm]) for nm in _P_NAMES] + r_in
    out_specs = [pl.BlockSpec((SWEEP * CHUNK, 1024), lambda i: (i, 0))]
    out_shape = [jax.ShapeDtypeStruct((SEQ, 1024), BF16)]
    for nm in _S_NAMES:
        shp = _S_SHAPES[nm]
        out_specs.append(pl.BlockSpec((1,) + shp, lambda i: (i, 0, 0, 0)))
        out_shape.append(jax.ShapeDtypeStruct((N_SWEEP,) + shp, F32))
    return pl.pallas_call(
        body, grid=(N_SWEEP,), in_specs=in_specs, out_specs=out_specs + r_out_specs, out_shape=out_shape + r_out_shape,
        scratch_shapes=[pltpu.VMEM(_S_SHAPES[nm], F32) for nm in _S_NAMES]
        + [pltpu.VMEM((HEADS, 128, 128), F32) for _ in _P_BLOCKDIAG] + r_sems,
        compiler_params=_params(("arbitrary",)), name="mixer_fwd",
    )(pm, pm, *[p[nm] for nm in _P_NAMES], *rider_ins)


def _mixer_bwd(pm, dab, states, p, rider=None, rider_ins=()):
    n_p = len(_P_NAMES)
    r_in, r_out_specs, r_out_shape, r_sems = _rider_specs(rider, rider_ins)

    def body(*refs):
        ((pm_ref, xprev_ref, dab_ref), si_refs, p_list, ride_in, (dpm_ref,), dp_list, ride_out, ds_refs, (carry_ref,),
         dense_list, ddense_list, sems) = _split(refs, 3, 4, n_p, len(r_in), 1, n_p, len(r_out_specs), 4, 1, 3, 3, len(r_sems))
        p_refs = dict(zip(_P_NAMES, p_list))
        dp_refs = dict(zip(_P_NAMES, dp_list))
        dense = dict(zip(_P_BLOCKDIAG, dense_list))
        ddense = dict(zip(_P_BLOCKDIAG, ddense_list))
        i = pl.program_id(0)
        blk = N_SWEEP - 1 - i
        _ride(rider, ("first",), i == 0, ride_in, ride_out, sems)

        @pl.when(i == 0)
        def _():
            for r in ds_refs:
                r[...] = jnp.zeros_like(r)
            for nm in _P_NAMES:
                if nm in _P_BLOCKDIAG:
                    ddense[nm][...] = jnp.zeros_like(ddense[nm])
                    _expand_blockdiag(p_refs[nm], dense[nm])
                else:
                    dp_refs[nm][...] = jnp.zeros_like(dp_refs[nm])
            carry_ref[...] = jnp.zeros_like(carry_ref)

        pv = {nm: (_per_head(dense[nm]) if nm in _P_BLOCKDIAG else p_refs[nm][...]) for nm in _P_NAMES}
        dst = {name: _per_head(r) for name, r in zip(_S_NAMES, ds_refs)}
        st = {name: [r[0, h] for h in range(HEADS)] for name, r in zip(_S_NAMES, si_refs)}
        xprev8 = jnp.where(blk > 0, xprev_ref[CHUNK - 8:CHUNK, :], 0.0)
        _, vjp = jax.vjp(functools.partial(_mixer_chunk, _VJP_OPS), pv, st, pm_ref[...], xprev8)
        dp_sum, dst, dpm, dxprev8 = vjp((dab_ref[...], dst))
        reach = jnp.concatenate([jnp.zeros((SWEEP * CHUNK - 8, 512), F32), carry_ref[...]], axis=0)
        dpm_ref[:, 0:PM_XM] = dpm[:, 0:PM_XM].astype(BF16)
        dpm_ref[:, PM_XM:PM_XM + 512] = (dpm[:, PM_XM:PM_XM + 512] + reach).astype(BF16)
        dpm_ref[:, PM_XM + 512:PM_W] = dpm[:, PM_XM + 512:PM_W].astype(BF16)
        carry_ref[...] = dxprev8
        for name, r in zip(_S_NAMES, ds_refs):
            for h in range(HEADS):
                r[h] = dst[name][h]
        for nm in _P_NAMES:
            if nm in _P_BLOCKDIAG:
                for h in range(HEADS):
                    ddense[nm][h] += dp_sum[nm][h]
            else:
                dp_refs[nm][...] += dp_sum[nm]

        @pl.when(i == N_SWEEP - 1)
        def _():
            for nm in _P_BLOCKDIAG:
                _collect_blockdiag(ddense[nm], dp_refs[nm])

        _ride(rider, ("middle",), i == _middle_step(rider, N_SWEEP), ride_in, ride_out, sems)
        _ride(rider, ("last",), i == N_SWEEP - 1, ride_in, ride_out, sems)

    rev = lambda i: (N_SWEEP - 1 - i, 0)
    in_specs = [pl.BlockSpec((SWEEP * CHUNK, PM_W), rev),
                pl.BlockSpec((CHUNK, 512), lambda i: (jnp.maximum(SWEEP * (N_SWEEP - 1 - i) - 1, 0), PM_XM // 512)),
                pl.BlockSpec((SWEEP * CHUNK, 1024), rev)]
    for nm in _S_NAMES:
        in_specs.append(pl.BlockSpec((1,) + _S_SHAPES[nm], lambda i: (N_SWEEP - 1 - i, 0, 0, 0)))
    in_specs += [_const_spec(_P_SHAPES[nm]) for nm in _P_NAMES] + r_in
    out_specs = [pl.BlockSpec((SWEEP * CHUNK, PM_W), rev)] + [_const_spec(_P_SHAPES[nm]) for nm in _P_NAMES]
    out_shape = [jax.ShapeDtypeStruct((SEQ, PM_W), BF16)] + [jax.ShapeDtypeStruct(_P_SHAPES[nm], F32) for nm in _P_NAMES]
    res = pl.pallas_call(
        body, grid=(N_SWEEP,), in_specs=in_specs, out_specs=out_specs + r_out_specs, out_shape=out_shape + r_out_shape,
        scratch_shapes=[pltpu.VMEM(_S_SHAPES[nm], F32) for nm in _S_NAMES] + [pltpu.VMEM((8, 512), F32)]
        + [pltpu.VMEM((HEADS, 128, 128), F32) for _ in range(2 * len(_P_BLOCKDIAG))] + r_sems,
        compiler_params=_params(("arbitrary",)), name="mixer_bwd",
    )(pm, pm, dab, *states, *[p[nm] for nm in _P_NAMES], *rider_ins)
    return res[0], dict(zip(_P_NAMES, res[1:1 + n_p])), res[1 + n_p:]


def _tok(width):
    return pl.BlockSpec((TOK_TILE, width), lambda i: (i, 0))


def _once(shape):
    zeros = (0,) * len(shape)
    return pl.BlockSpec(shape, lambda i: zeros, pipeline_mode=pl.Buffered(1))


def _rms_fwd(x):
    r = lax.rsqrt(_mean(x * x) + EPS)
    return x * r, r


def _rms_bwd(dy, xn, r, g):
    gd = dy * g
    return r * (gd - xn * _mean(xn * gd))


def _tiled_call(body, in_specs, out_specs, out_shape, args, name, rider=None, rider_ins=()):
    r_in, r_out_specs, r_out_shape, r_scratch = _rider_specs(rider, rider_ins)
    n_in, n_out = len(in_specs), len(out_specs)

    def hosted(*refs):
        ins, ride_in, outs, ride_out, scratch = _split(refs, n_in, len(r_in), n_out, len(r_out_specs), len(r_scratch))
        i = pl.program_id(0)
        _ride(rider, ("first",), i == 0, ride_in, ride_out, scratch)
        body(*ins, *outs)
        _ride(rider, ("middle",), i == _middle_step(rider, N_TOK_TILE), ride_in, ride_out, scratch)
        _ride(rider, ("last",), i == N_TOK_TILE - 1, ride_in, ride_out, scratch)

    res = pl.pallas_call(
        hosted, grid=(N_TOK_TILE,), in_specs=list(in_specs) + r_in, out_specs=list(out_specs) + r_out_specs,
        out_shape=list(out_shape) + r_out_shape, scratch_shapes=r_scratch,
        compiler_params=_params(("arbitrary",)), name=name,
    )(*args, *rider_ins)
    return res[:n_out], res[n_out:]


def _in_proj(x, g_pre, wt_in, rider=None, rider_ins=()):
    def body(x_ref, g_ref, wt_ref, pm_ref, gab_ref, h_ref):
        xn, _ = _rms_fwd(x_ref[...])
        h = (xn * g_ref[...]).astype(BF16)
        h_ref[...] = h
        pm_ref[:, 0:PM_XM] = _nt(h, wt_ref[0:IN_ALOW, :])
        pm_ref[:, PM_XM:PM_AL] = _nt(h, wt_ref[IN_XM:IN_GATES, :])
        pm_ref[:, PM_AL:PM_W] = _nt(h, wt_ref[IN_ALOW:IN_ALOW + 128, :])
        gab_ref[...] = _nt(h, wt_ref[IN_GATES:D_IN, :])

    return _tiled_call(
        body, [_tok(D_MODEL), _once((1, D_MODEL)), _once((D_IN, D_MODEL))], [_tok(PM_W), _tok(GAB_W), _tok(D_MODEL)],
        [jax.ShapeDtypeStruct((SEQ, PM_W), F32), jax.ShapeDtypeStruct((SEQ, GAB_W), F32),
         jax.ShapeDtypeStruct((SEQ, D_MODEL), BF16)], (x, g_pre, wt_in), "in_proj", rider, rider_ins)


def _merge_fwd(ab, gab, x, w_pa4, w_pb4, w_o, g_post, rider=None, rider_ins=()):
    def body(ab_ref, gab_ref, x_ref, wpa_ref, wpb_ref, wo_ref, g_ref, x1_ref, mix_ref, mg_ref):
        a = ab_ref[:, 0:512]
        b = ab_ref[:, 512:1024]
        for j in range(N_CHIP):
            blk = slice(j * 256, (j + 1) * 256)
            ya = jnp.dot(a, wpa_ref[j], preferred_element_type=F32)
            yb = jnp.dot(b, wpb_ref[j], preferred_element_type=F32)
            sa = _sigmoid(gab_ref[:, j * 256:(j + 1) * 256])
            sb = _sigmoid(gab_ref[:, 1024 + j * 256:1024 + (j + 1) * 256])
            mg_ref[:, blk] = (sa * ya + sb * yb).astype(BF16)
        mix = jnp.dot(mg_ref[...], wo_ref[...], preferred_element_type=F32)
        mix_ref[...] = mix
        mn, _ = _rms_fwd(mix)
        x1_ref[...] = x_ref[...] + mn * g_ref[...]

    return _tiled_call(
        body, [_tok(1024), _tok(GAB_W), _tok(D_MODEL), _once((N_CHIP, 512, 256)), _once((N_CHIP, 512, 256)),
               _once((D_MODEL, D_MODEL)), _once((1, D_MODEL))], [_tok(D_MODEL), _tok(D_MODEL), _tok(D_MODEL)],
        [jax.ShapeDtypeStruct((SEQ, D_MODEL), F32), jax.ShapeDtypeStruct((SEQ, D_MODEL), F32),
         jax.ShapeDtypeStruct((SEQ, D_MODEL), BF16)], (ab, gab, x, w_pa4, w_pb4, w_o, g_post), "merge_fwd", rider, rider_ins)


def _mlp(x1, target, g_pre, g_post, w_up4, w_down_a4, w_down_b4):
    def body(x1_ref, t_ref, gpre_ref, gpost_ref, wup_ref, wda_ref, wdb_ref,
             dx1_ref, u_ref, dd_ref, h2_ref, dpre_ref, dgpost_ref, dgpre_ref, loss_ref):
        @pl.when(pl.program_id(0) == 0)
        def _():
            dgpost_ref[...] = jnp.zeros_like(dgpost_ref)
            dgpre_ref[...] = jnp.zeros_like(dgpre_ref)
            loss_ref[...] = jnp.zeros_like(loss_ref)

        x1 = x1_ref[...]
        gpre = gpre_ref[...]
        gpost = gpost_ref[...]
        xn2, r2 = _rms_fwd(x1)
        h2 = (xn2 * gpre).astype(BF16)
        h2_ref[...] = h2
        rl = []
        d = jnp.zeros((TOK_TILE, D_MODEL), F32)
        for j in range(N_CHIP):
            blk = slice(j * 1024, (j + 1) * 1024)
            r = jnp.maximum(jnp.dot(h2, wup_ref[j], preferred_element_type=F32), 0.0)
            rl.append(r)
            u = (r * r).astype(BF16)
            u_ref[:, blk] = u
            d = d + jnp.dot(u[:, 0:512], wda_ref[j], preferred_element_type=F32)
            d = d + jnp.dot(u[:, 512:1024], wdb_ref[j], preferred_element_type=F32)
        dn, r3 = _rms_fwd(d)
        diff = x1 + dn * gpost - t_ref[...]
        loss_ref[...] += jnp.sum(diff * diff, keepdims=True) * (0.5 / D_MODEL)
        dy = diff * (1.0 / D_MODEL)
        dgpost_ref[...] += jnp.sum(dy * dn, axis=0, keepdims=True)
        dd = _rms_bwd(dy, dn, r3, gpost).astype(BF16)
        dd_ref[...] = dd
        dh2 = jnp.zeros((TOK_TILE, D_MODEL), F32)
        for j in range(N_CHIP):
            blk = slice(j * 1024, (j + 1) * 1024)
            du = jnp.concatenate([_nt(dd, wda_ref[j]), _nt(dd, wdb_ref[j])], axis=1)
            dpre = (du * (2.0 * rl[j])).astype(BF16)
            dpre_ref[:, blk] = dpre
            dh2 = dh2 + _nt(dpre, wup_ref[j])
        dgpre_ref[...] += jnp.sum(dh2 * xn2, axis=0, keepdims=True)
        dx1_ref[...] = dy + _rms_bwd(dh2, xn2, r2, gpre)

    acc = pl.BlockSpec((1, D_MODEL), lambda i: (0, 0))
    return pl.pallas_call(
        body, grid=(N_TOK_TILE,),
        in_specs=[_tok(D_MODEL), _tok(D_MODEL), _once((1, D_MODEL)), _once((1, D_MODEL)),
                  _once((N_CHIP, D_MODEL, 1024)), _once((N_CHIP, 512, D_MODEL)), _once((N_CHIP, 512, D_MODEL))],
        out_specs=[_tok(D_MODEL), _tok(D_FF), _tok(D_MODEL), _tok(D_MODEL), _tok(D_FF), acc, acc,
                   pl.BlockSpec((1, 128), lambda i: (0, 0))],
        out_shape=[jax.ShapeDtypeStruct((SEQ, D_MODEL), F32), jax.ShapeDtypeStruct((SEQ, D_FF), BF16),
                   jax.ShapeDtypeStruct((SEQ, D_MODEL), BF16), jax.ShapeDtypeStruct((SEQ, D_MODEL), BF16),
                   jax.ShapeDtypeStruct((SEQ, D_FF), BF16), jax.ShapeDtypeStruct((1, D_MODEL), F32),
                   jax.ShapeDtypeStruct((1, D_MODEL), F32), jax.ShapeDtypeStruct((1, 128), F32)],
        compiler_params=_params(("arbitrary",)), name="mlp_fwd_bwd",
    )(x1, target, g_pre, g_post, w_up4, w_down_a4, w_down_b4)


def _merge_bwd(dx1, mix, ab, gab, w_pa4, w_pb4, w_o, g_post):
    def body(dx1_ref, mix_ref, ab_ref, gab_ref, wpa_ref, wpb_ref, wo_ref, g_ref,
             dmix_ref, dya_ref, dyb_ref, dgab_ref, dab_ref, dg_ref):
        @pl.when(pl.program_id(0) == 0)
        def _():
            dg_ref[...] = jnp.zeros_like(dg_ref)

        dx1 = dx1_ref[...]
        mn, r = _rms_fwd(mix_ref[...])
        dg_ref[...] += jnp.sum(dx1 * mn, axis=0, keepdims=True)
        dmix = _rms_bwd(dx1, mn, r, g_ref[...]).astype(BF16)
        dmix_ref[...] = dmix
        dmerged = _nt(dmix, wo_ref[...])
        a = ab_ref[:, 0:512]
        b = ab_ref[:, 512:1024]
        da = jnp.zeros((TOK_TILE, 512), F32)
        db = jnp.zeros((TOK_TILE, 512), F32)
        for j in range(N_CHIP):
            blk = slice(j * 256, (j + 1) * 256)
            blk_b = slice(1024 + j * 256, 1024 + (j + 1) * 256)
            dm = dmerged[:, blk]
            ya = jnp.dot(a, wpa_ref[j], preferred_element_type=F32)
            yb = jnp.dot(b, wpb_ref[j], preferred_element_type=F32)
            sa = _sigmoid(gab_ref[:, blk])
            sb = _sigmoid(gab_ref[:, blk_b])
            dya = (dm * sa).astype(BF16)
            dyb = (dm * sb).astype(BF16)
            dya_ref[:, blk] = dya
            dyb_ref[:, blk] = dyb
            dgab_ref[:, blk] = (dm * ya * sa * (1.0 - sa)).astype(BF16)
            dgab_ref[:, blk_b] = (dm * yb * sb * (1.0 - sb)).astype(BF16)
            da = da + _nt(dya, wpa_ref[j])
            db = db + _nt(dyb, wpb_ref[j])
        dab_ref[:, 0:512] = da
        dab_ref[:, 512:1024] = db

    return pl.pallas_call(
        body, grid=(N_TOK_TILE,),
        in_specs=[_tok(D_MODEL), _tok(D_MODEL), _tok(1024), _tok(GAB_W), _once((N_CHIP, 512, 256)),
                  _once((N_CHIP, 512, 256)), _once((D_MODEL, D_MODEL)), _once((1, D_MODEL))],
        out_specs=[_tok(D_MODEL), _tok(D_MODEL), _tok(D_MODEL), _tok(GAB_W), _tok(1024),
                   pl.BlockSpec((1, D_MODEL), lambda i: (0, 0))],
        out_shape=[jax.ShapeDtypeStruct((SEQ, D_MODEL), BF16), jax.ShapeDtypeStruct((SEQ, D_MODEL), BF16),
                   jax.ShapeDtypeStruct((SEQ, D_MODEL), BF16), jax.ShapeDtypeStruct((SEQ, GAB_W), BF16),
                   jax.ShapeDtypeStruct((SEQ, 1024), F32), jax.ShapeDtypeStruct((1, D_MODEL), F32)],
        compiler_params=_params(("arbitrary",)), name="merge_bwd",
    )(dx1, mix, ab, gab, w_pa4, w_pb4, w_o, g_post)


def _in_proj_bwd(dpm, dgab, x, dx1, g_pre, wt_in, rider=None, rider_ins=()):
    def body(dpm_ref, dgab_ref, x_ref, dx1_ref, g_ref, wt_ref, dx_ref, dg_ref):
        @pl.when(pl.program_id(0) == 0)
        def _():
            dg_ref[...] = jnp.zeros_like(dg_ref)

        dh = jnp.dot(dpm_ref[:, 0:PM_XM], wt_ref[0:IN_ALOW, :], preferred_element_type=F32)
        dh = dh + jnp.dot(dpm_ref[:, PM_XM:PM_AL], wt_ref[IN_XM:IN_GATES, :], preferred_element_type=F32)
        dh = dh + jnp.dot(dpm_ref[:, PM_AL:PM_W], wt_ref[IN_ALOW:IN_ALOW + 128, :], preferred_element_type=F32)
        dh = dh + jnp.dot(dgab_ref[...], wt_ref[IN_GATES:D_IN, :], preferred_element_type=F32)
        xn, r = _rms_fwd(x_ref[...])
        dg_ref[...] += jnp.sum(dh * xn, axis=0, keepdims=True)
        dx_ref[...] = dx1_ref[...] + _rms_bwd(dh, xn, r, g_ref[...])

    return _tiled_call(
        body, [_tok(PM_W), _tok(GAB_W), _tok(D_MODEL), _tok(D_MODEL), _once((1, D_MODEL)), _once((D_IN, D_MODEL))],
        [_tok(D_MODEL), pl.BlockSpec((1, D_MODEL), lambda i: (0, 0))],
        [jax.ShapeDtypeStruct((SEQ, D_MODEL), F32), jax.ShapeDtypeStruct((1, D_MODEL), F32)],
        (dpm, dgab, x, dx1, g_pre, wt_in), "in_proj_bwd", rider, rider_ins)


def _dw_in(dpm, dgab, h):
    n_pm = PM_AL // 512
    n_blk = n_pm + GAB_W // 512

    def body(dpm_ref, dgab_ref, dal_ref, h_ref, o_ref):
        i = pl.program_id(0)
        off = pl.multiple_of(i * 512 + 16 * (i >= 3).astype(jnp.int32), 16)

        @pl.when(i < n_pm)
        def _():
            o_ref[pl.ds(off, 512), :] = _tn(dpm_ref[...], h_ref[...]).astype(BF16)

        @pl.when(i >= n_pm)
        def _():
            o_ref[pl.ds(off, 512), :] = _tn(dgab_ref[...], h_ref[...]).astype(BF16)

        @pl.when(i == 0)
        def _():
            o_ref[IN_ALOW:IN_XM, :] = _tn(dal_ref[...], h_ref[...])[0:IN_XM - IN_ALOW].astype(BF16)

    return pl.pallas_call(
        body, grid=(n_blk,),
        in_specs=[pl.BlockSpec((SEQ, 512), lambda i: (0, jnp.minimum(i, n_pm - 1))),
                  pl.BlockSpec((SEQ, 512), lambda i: (0, jnp.maximum(i - n_pm, 0))),
                  pl.BlockSpec((SEQ, 128), lambda i: (0, PM_AL // 128)),
                  _once((SEQ, D_MODEL))],
        out_specs=pl.BlockSpec((D_IN, D_MODEL), lambda i: (0, 0)),
        out_shape=jax.ShapeDtypeStruct((D_IN, D_MODEL), BF16),
        compiler_params=_params(("arbitrary",)), name="dw_in",
    )(dpm, dgab, dpm, h)


def _tn_matmul(a, b, name, shards=1, tm=1024, rider=None, rider_ins=()):
    m, n = a.shape[1], b.shape[1]
    tm = min(tm, m)
    tn = n // shards if shards > 1 else min(n, 1024)
    steps_i, steps_j = m // tm, n // tn
    r_in, r_out_specs, r_out_shape, r_scratch = _rider_specs(rider, rider_ins)

    def body(*refs):
        (a_ref, b_ref), ride_in, (o_ref,), ride_out, scratch = _split(refs, 2, len(r_in), 1, len(r_out_specs), len(r_scratch))
        step = pl.program_id(0) * steps_j + pl.program_id(1)
        _ride(rider, ("first",), step == 0, ride_in, ride_out, scratch)
        o_ref[...] = _tn(a_ref[...], b_ref[...]).astype(BF16)
        _ride(rider, ("middle", "last"), step == steps_i * steps_j - 1, ride_in, ride_out, scratch)

    if shards > 1:
        out_spec = pl.BlockSpec((None, tm, tn), lambda i, j: (j, i, 0))
        out_shape = jax.ShapeDtypeStruct((shards, m, tn), BF16)
    else:
        out_spec = pl.BlockSpec((tm, tn), lambda i, j: (i, j))
        out_shape = jax.ShapeDtypeStruct((m, n), BF16)
    res = pl.pallas_call(
        body, grid=(steps_i, steps_j),
        in_specs=[pl.BlockSpec((SEQ, tm), lambda i, j: (0, i)), pl.BlockSpec((SEQ, tn), lambda i, j: (0, j))] + r_in,
        out_specs=[out_spec] + r_out_specs, out_shape=[out_shape] + r_out_shape, scratch_shapes=r_scratch,
        compiler_params=_params(("arbitrary", "arbitrary")), name=name,
    )(a, b, *rider_ins)
    return res[0] if rider is None else (res[0], res[1:])


MESH = pl.DeviceIdType.MESH
ANY = pl.BlockSpec(memory_space=pl.ANY)
VMEM_WHOLE = pl.BlockSpec(memory_space=pltpu.VMEM)

_BIG = ("w_in", "w_pa", "w_pb", "w_o", "w_up", "w_down")
_BIG_SHARD = {"w_in": (IN_SHARD, D_MODEL), "w_pa": (512, 256), "w_pb": (512, 256), "w_o": (256, D_MODEL),
              "w_up": (D_MODEL, 1024), "w_down": (1024, D_MODEL),
              "w_down_a": (512, D_MODEL), "w_down_b": (512, D_MODEL)}
_BIG_SPLIT = {"w_in": 1, "w_pa": 0, "w_pb": 0, "w_o": 0, "w_up": 0, "w_down": 0, "w_down_a": 0, "w_down_b": 0}


def _half(ref, e, name, lead=0, part=None):
    axis = _BIG_SPLIT[name]
    size = _BIG_SHARD[name][axis] // 2
    start = e * size
    if part is not None:
        size //= 2
        start = start + part * size
    start = pl.multiple_of(start, 128 if axis == 1 else 16)
    idx = [pl.ds(0, ref.shape[a]) for a in range(lead)]
    idx += [pl.ds(start, size), pl.ds(0, _BIG_SHARD[name][1])] if axis == 0 else [pl.ds(0, _BIG_SHARD[name][0]), pl.ds(start, size)]
    return ref.at[tuple(idx)]


def _half_shape(name):
    r, c = _BIG_SHARD[name]
    return (r // 2, c) if _BIG_SPLIT[name] == 0 else (r, c // 2)


def _remote(src, dst, send_sems, recv_sems, k, to):
    return pltpu.make_async_remote_copy(src_ref=src, dst_ref=dst, send_sem=send_sems.at[k], recv_sem=recv_sems.at[k],
                                        device_id=to, device_id_type=MESH)


def _mesh_place():
    x, y, c = lax.axis_index("x"), lax.axis_index("y"), lax.axis_index("c")
    return x, y, c, [(1 - x, y), (x, 1 - y), (1 - x, 1 - y)]


class _Gather:
    def __init__(self, names, small=(), middle_at=0.5):
        self.middle_at = middle_at
        self.names = tuple(names)
        self.nb = len(self.names)
        self.n = self.nb + len(small)
        self.n_sems = 8 * self.nb + 3 * len(small)
        self.n_flush = 6 * self.nb + len(small)
        self.out_shape = [jax.ShapeDtypeStruct((N_CHIP,) + _BIG_SHARD[nm], BF16) for nm in self.names]
        self.out_shape += [jax.ShapeDtypeStruct((N_CHIP,) + s.shape, s.dtype) for s in small]

    def _copies(self, ins, outs, ss, rs, k):
        x, y, c, _ = _mesh_place()
        name = self.names[k]
        me, xn, yn, dg = 2 * x + y, 2 * (1 - x) + y, 2 * x + (1 - y), 2 * (1 - x) + (1 - y)
        to_x, to_y, sibling = (1 - x, y, c), (x, 1 - y, c), (x, y, 1 - c)

        def region(slot, e, part=None):
            return _half(outs[k].at[slot], e, name, part=part)

        def copy(pair, src, dst, to):
            return _remote(src, dst, ss, rs, 8 * k + pair, to)

        mine = _half(ins[k], c, name)
        sent = [copy(0, mine, region(me, c), to_x), copy(1, mine, region(me, c), to_y),
                copy(2, region(xn, c, 0), region(xn, c, 0), to_y), copy(3, region(yn, c, 1), region(yn, c, 1), to_x),
                copy(4, region(xn, c), region(xn, c), sibling), copy(5, region(yn, c), region(yn, c), sibling),
                copy(6, region(dg, c, 0), region(dg, c, 0), sibling), copy(7, region(dg, c, 1), region(dg, c, 1), sibling)]
        landing = [region(xn, c), region(yn, c), region(dg, c, 0), region(dg, c, 1),
                   region(xn, 1 - c), region(yn, 1 - c), region(dg, 1 - c, 0), region(dg, 1 - c, 1)]
        received = [copy(pair, dst, dst, sibling) for pair, dst in enumerate(landing)]
        return sent, received

    def _small(self, ins, outs, ss, rs, k, j, peer, slot, c):
        return _remote(ins[k], outs[k].at[slot], ss, rs, 8 * self.nb + 3 * (k - self.nb) + j, (*peer, c))

    def flush(self, phase, lands, outs, fs):
        x, y, c, _ = _mesh_place()
        me, xn, yn, dg = 2 * x + y, 2 * (1 - x) + y, 2 * x + (1 - y), 2 * (1 - x) + (1 - y)

        def pieces(k):
            name = self.names[k]
            spots = [lambda r: r.at[me], lambda r: _half(r.at[xn], c, name), lambda r: _half(r.at[yn], c, name),
                     lambda r: _half(r.at[xn], 1 - c, name), lambda r: _half(r.at[yn], 1 - c, name), lambda r: r.at[dg]]
            return [pltpu.make_async_copy(spot(lands[k]), spot(outs[k]), fs.at[6 * k + t]) for t, spot in enumerate(spots)]

        ready = {"first": (0,), "middle": (1, 2), "last": (3, 4, 5)}[phase]
        for k in range(self.nb):
            cps = pieces(k)
            for t in ready:
                cps[t].start()
        if phase == "last":
            small = [pltpu.make_async_copy(lands[k], outs[k], fs.at[6 * self.nb + k - self.nb]) for k in range(self.nb, self.n)]
            for cp in small:
                cp.start()
            for k in range(self.nb):
                for cp in pieces(k):
                    cp.wait()
            for cp in small:
                cp.wait()

    def first(self, ins, outs, ss, rs):
        x, y, c, peers = _mesh_place()
        me = 2 * x + y
        for k in range(self.nb):
            sent, _ = self._copies(ins, outs, ss, rs, k)
            sent[0].start()
            sent[1].start()
        for k in range(self.nb, self.n):
            for j, peer in enumerate(peers):
                self._small(ins, outs, ss, rs, k, j, peer, me, c).start()
        for k in range(self.n):
            outs[k][me] = ins[k][...]

    def middle(self, ins, outs, ss, rs):
        for k in range(self.nb):
            sent, received = self._copies(ins, outs, ss, rs, k)
            for pair in (0, 1):
                received[pair].wait_recv()
                sent[2 + pair].start()
                sent[4 + pair].start()

    def last(self, ins, outs, ss, rs):
        x, y, c, peers = _mesh_place()
        for k in range(self.nb):
            sent, received = self._copies(ins, outs, ss, rs, k)
            for pair in (2, 3):
                received[pair].wait_recv()
                sent[4 + pair].start()
        for k in range(self.nb):
            sent, received = self._copies(ins, outs, ss, rs, k)
            for pair in range(4, 8):
                received[pair].wait_recv()
            for cp in sent:
                cp.wait_send()
        for k in range(self.nb, self.n):
            for j, (px, py) in enumerate(peers):
                self._small(ins, outs, ss, rs, k, j, (px, py), 2 * px + py, c).wait_recv()
                self._small(ins, outs, ss, rs, k, j, (px, py), 2 * x + y, c).wait_send()


def _run_alone(rider, ins, name):
    def body(*refs):
        r_in, r_out, sems = _split(refs, len(ins), len(rider.out_shape), 2)
        rider.first(r_in, r_out, *sems)
        rider.middle(r_in, r_out, *sems)
        rider.last(r_in, r_out, *sems)

    return pl.pallas_call(
        body, in_specs=[VMEM_WHOLE] * len(ins), out_specs=[VMEM_WHOLE] * len(rider.out_shape), out_shape=rider.out_shape,
        scratch_shapes=[pltpu.SemaphoreType.DMA((rider.n_sems,)), pltpu.SemaphoreType.DMA((rider.n_sems,))],
        compiler_params=_params(), name=name,
    )(*ins)


class _Presum:
    in_space = ANY

    def __init__(self, names):
        self.names = tuple(names)
        self.n = len(self.names)
        self.n_sems = 3 * self.n
        self.out_shape = [jax.ShapeDtypeStruct((N_CHIP,) + _half_shape(nm), BF16) for nm in self.names]
        self.work_shape = self.out_shape + self.out_shape

    def _stage(self, ins, bufs, ss, k, e, which):
        n = self.n
        return pltpu.make_async_copy(_half(ins[k], e, self.names[k], lead=1), bufs[which * n + k], ss.at[which * n + k])

    def _give(self, bufs, ss, rs, k, sibling):
        return _remote(bufs[self.n + k], bufs[k], ss, rs, k, sibling)

    def first(self, ins, bufs, ss, rs):
        x, y, c, _ = _mesh_place()
        for k in range(self.n):
            self._stage(ins, bufs, ss, k, 1 - c, 1).start()
        for k in range(self.n):
            self._stage(ins, bufs, ss, k, c, 2).start()
        for k in range(self.n):
            self._stage(ins, bufs, ss, k, 1 - c, 1).wait()
            self._give(bufs, ss, rs, k, (x, y, 1 - c)).start()

    def middle(self, ins, bufs, ss, rs):
        pass

    def last(self, ins, bufs, ss, rs):
        x, y, c, _ = _mesh_place()
        for k in range(self.n):
            self._give(bufs, ss, rs, k, (x, y, 1 - c)).wait_recv()
            self._stage(ins, bufs, ss, k, c, 2).wait()

            @pl.loop(0, N_CHIP)
            def _(j):
                bufs[k][j] = (bufs[k][j].astype(F32) + bufs[2 * self.n + k][j].astype(F32)).astype(BF16)
        for k in range(self.n):
            self._give(bufs, ss, rs, k, (x, y, 1 - c)).wait_send()


def _presum(names, grads, name):
    rider = _Presum(names)
    n = rider.n

    def body(*refs):
        g_refs, got_refs, work_refs, sems = _split(refs, n, n, 2 * n, 2)
        bufs = list(got_refs) + list(work_refs)
        rider.first(g_refs, bufs, *sems)
        rider.last(g_refs, bufs, *sems)

    return pl.pallas_call(
        body, in_specs=[ANY] * n, out_specs=[VMEM_WHOLE] * n, out_shape=rider.out_shape,
        scratch_shapes=[pltpu.VMEM(s.shape, s.dtype) for s in rider.work_shape]
        + [pltpu.SemaphoreType.DMA((rider.n_sems,)), pltpu.SemaphoreType.DMA((rider.n_sems,))],
        compiler_params=_params(), name=name,
    )(*grads)


class _ReduceRelay:
    middle_at = 0.75

    def __init__(self, names):
        self.names = tuple(names)
        self.n = len(self.names)
        self.n_sems = 6 * self.n
        self.out_shape = [jax.ShapeDtypeStruct((N_CHIP,) + _half_shape(nm), BF16) for nm in self.names]
        quarter = [jax.ShapeDtypeStruct(self._part_shape(nm), BF16) for nm in self.names]
        self.work_shape = quarter + quarter

    @staticmethod
    def _part_shape(name):
        r, c = _half_shape(name)
        return (r // 2, c) if _BIG_SPLIT[name] == 0 else (r, c // 2)

    def _part(self, ref, name, p):
        r, c = self._part_shape(name)
        return ref.at[pl.ds(p * r, r), pl.ds(0, c)] if _BIG_SPLIT[name] == 0 else ref.at[pl.ds(0, r), pl.ds(p * c, c)]

    def _copies(self, ins, bufs, ss, rs, k):
        x, y, c, _ = _mesh_place()
        name, n = self.names[k], self.n
        me, xn, yn, dg = 2 * x + y, 2 * (1 - x) + y, 2 * x + (1 - y), 2 * (1 - x) + (1 - y)
        to_x, to_y = (1 - x, y, c), (x, 1 - y, c)
        mine = lambda slot, p: self._part(ins[k].at[slot], name, p)
        slot = lambda s, p: self._part(bufs[k].at[s], name, p)
        from_x, from_y = bufs[n + k], bufs[2 * n + k]

        def copy(pair, src, dst, to):
            return _remote(src, dst, ss, rs, 6 * k + pair, to)

        sent = [copy(0, mine(dg, 0), from_x, to_x), copy(1, mine(dg, 1), from_y, to_y),
                copy(2, mine(xn, 0), slot(me, 0), to_x), copy(3, mine(yn, 1), slot(me, 1), to_y),
                copy(4, from_y, slot(me, 1), to_x), copy(5, from_x, slot(me, 0), to_y)]
        landing = [from_x, from_y, slot(xn, 0), slot(yn, 1), slot(xn, 1), slot(yn, 0)]
        received = [copy(pair, dst, dst, to_x) for pair, dst in enumerate(landing)]
        return sent, received

    def first(self, ins, bufs, ss, rs):
        x, y, c, _ = _mesh_place()
        me, dg = 2 * x + y, 2 * (1 - x) + (1 - y)
        for k in range(self.n):
            sent, _ = self._copies(ins, bufs, ss, rs, k)
            for pair in range(4):
                sent[pair].start()
        for k in range(self.n):
            bufs[k][me] = ins[k][me]
            bufs[k][dg] = jnp.zeros(_half_shape(self.names[k]), BF16)

    def middle(self, ins, bufs, ss, rs):
        x, y, c, _ = _mesh_place()
        xn, yn = 2 * (1 - x) + y, 2 * x + (1 - y)
        for k in range(self.n):
            sent, received = self._copies(ins, bufs, ss, rs, k)
            name, n = self.names[k], self.n
            for pair, buf, own in ((0, bufs[n + k], self._part(ins[k].at[yn], name, 0)),
                                   (1, bufs[2 * n + k], self._part(ins[k].at[xn], name, 1))):
                received[pair].wait_recv()
                buf[...] = (buf[...].astype(F32) + own[...].astype(F32)).astype(BF16)
            sent[5].start()
            sent[4].start()

    def last(self, ins, bufs, ss, rs):
        for k in range(self.n):
            sent, received = self._copies(ins, bufs, ss, rs, k)
            for pair in range(2, 6):
                received[pair].wait_recv()
            for cp in sent:
                cp.wait_send()


class _SendPartials:
    def __init__(self, names, small_shape=None):
        self.n = len(names)
        self.small = small_shape is not None
        self.n_sems = 3 * self.n + 7
        self.out_shape = [jax.ShapeDtypeStruct((N_CHIP,) + _half_shape(nm), BF16) for nm in names]
        if self.small:
            self.out_shape.append(jax.ShapeDtypeStruct((N_DEV,) + small_shape, F32))

    def _piece(self, ins, outs, ss, rs, k, j, peer, src_slot, dst_slot, c):
        return _remote(ins[k].at[src_slot], outs[k].at[dst_slot], ss, rs, 3 * k + j, (*peer, c))

    def _small(self, ins, outs, ss, rs, r, other, slot):
        return _remote(ins[self.n], outs[self.n].at[slot], ss, rs, 3 * self.n + r, other)

    @staticmethod
    def _others(x, y, c):
        return [(x, y, 1 - c), (1 - x, y, c), (1 - x, y, 1 - c), (x, 1 - y, c), (x, 1 - y, 1 - c),
                (1 - x, 1 - y, c), (1 - x, 1 - y, 1 - c)]

    def first(self, ins, outs, ss, rs):
        x, y, c, peers = _mesh_place()
        me = 2 * x + y
        for k in range(self.n):
            for j, (px, py) in enumerate(peers):
                self._piece(ins, outs, ss, rs, k, j, (px, py), 2 * px + py, me, c).start()
        if self.small:
            for r, other in enumerate(self._others(x, y, c)):
                self._small(ins, outs, ss, rs, r, other, 4 * x + 2 * y + c).start()
            outs[self.n][4 * x + 2 * y + c] = ins[self.n][...]
        for k in range(self.n):
            outs[k][me] = ins[k][me]

    def middle(self, ins, outs, ss, rs):
        pass

    def last(self, ins, outs, ss, rs):
        x, y, c, peers = _mesh_place()
        me = 2 * x + y
        for k in range(self.n):
            for j, (px, py) in enumerate(peers):
                self._piece(ins, outs, ss, rs, k, j, (px, py), me, 2 * px + py, c).wait_recv()
                self._piece(ins, outs, ss, rs, k, j, (px, py), 2 * px + py, me, c).wait_send()
        if self.small:
            for r, (px, py, pc) in enumerate(self._others(x, y, c)):
                self._small(ins, outs, ss, rs, r, (px, py, pc), 4 * px + 2 * py + pc).wait_recv()
                self._small(ins, outs, ss, rs, r, (px, py, pc), 4 * x + 2 * y + c).wait_send()


def _sum_swap(names, parts, small):
    n = len(parts)
    everyone = _SendPartials((), small.shape)

    def body(*refs):
        p_refs, (small_ref,), o_refs, (osmall_ref,), (all_ref,), (send_sems, recv_sems, ss_small, rs_small) = _split(
            refs, n, 1, n, 1, 1, 4)
        x, y, c = lax.axis_index("x"), lax.axis_index("y"), lax.axis_index("c")
        everyone.first([small_ref], [all_ref], ss_small, rs_small)

        def mine(k):
            part = _half(o_refs[k], c, names[k])
            return _remote(part, part, send_sems, recv_sems, k, (x, y, 1 - c))

        for k in range(n):
            for e in range(2):
                @pl.when(c == e)
                def _():
                    g = p_refs[k][0].astype(F32)
                    for s in range(1, N_CHIP):
                        g = g + p_refs[k][s].astype(F32)
                    r, cols = _half_shape(names[k])
                    if _BIG_SPLIT[names[k]] == 0:
                        o_refs[k][e * r:(e + 1) * r, :] = g
                    else:
                        o_refs[k][:, e * cols:(e + 1) * cols] = g
            mine(k).start()
        for k in range(n):
            theirs = _half(o_refs[k], 1 - c, names[k])
            _remote(theirs, theirs, send_sems, recv_sems, k, (x, y, 1 - c)).wait_recv()
            mine(k).wait_send()
        everyone.last([small_ref], [all_ref], ss_small, rs_small)
        g = all_ref[0]
        for d in range(1, N_DEV):
            g = g + all_ref[d]
        osmall_ref[...] = g

    res = pl.pallas_call(
        body, in_specs=[VMEM_WHOLE] * (n + 1), out_specs=[VMEM_WHOLE] * (n + 1),
        out_shape=[jax.ShapeDtypeStruct(_BIG_SHARD[nm], F32) for nm in names] + [jax.ShapeDtypeStruct(small.shape, F32)],
        scratch_shapes=[pltpu.VMEM((N_DEV,) + small.shape, F32), pltpu.SemaphoreType.DMA((n,)), pltpu.SemaphoreType.DMA((n,)),
                        pltpu.SemaphoreType.DMA((everyone.n_sems,)), pltpu.SemaphoreType.DMA((everyone.n_sems,))],
        compiler_params=_params(), name="sum_swap",
    )(*parts, small)
    return res[:n], res[n]


def _tile(rows, cols, itemsize, budget):
    t = cols if rows % 16 else rows
    other = rows if rows % 16 else cols
    step = 256 if rows % 16 else 32
    while t % step == 0 and t * other * itemsize > budget:
        t //= 2
    return (rows, t) if rows % 16 else (t, cols)


def _adamw_math(w, g, m, v):
    m = ADAM_B1 * m + (1.0 - ADAM_B1) * g
    v = ADAM_B2 * v + (1.0 - ADAM_B2) * (g * g)
    m_hat = m / (1.0 - ADAM_B1 ** ADAM_STEP)
    v_hat = v / (1.0 - ADAM_B2 ** ADAM_STEP)
    delta = -ADAM_LR * (m_hat / (jnp.sqrt(v_hat) + ADAM_EPS) + ADAM_WD * w)
    return delta, m, v


def _adamw_big(g, w, m, v, name):
    r, c = w.shape
    tr, tc = _tile(r, c, 4, 2 * 1024 * 1024)

    def body(g_ref, w_ref, m_ref, v_ref, d_ref, nm_ref, nv_ref):
        d_ref[...], nm_ref[...], nv_ref[...] = _adamw_math(w_ref[...], g_ref[...], m_ref[...], v_ref[...])

    blk = pl.BlockSpec((tr, tc), lambda i, l: (i, l))
    return pl.pallas_call(
        body, grid=(r // tr, c // tc), in_specs=[blk, blk, blk, blk],
        out_specs=[blk, blk, blk], out_shape=[jax.ShapeDtypeStruct((r, c), F32)] * 3,
        compiler_params=_params(("arbitrary", "arbitrary")), name=name,
    )(g, w, m, v)


def _adamw_rows(g, w, m, v, name):
    r, k, lanes = w.shape
    tr = 296

    def body(g_ref, w_ref, m_ref, v_ref, g3_ref, d_ref, nm_ref, nv_ref):
        g = g_ref[...].reshape(tr, k, lanes)
        g3_ref[...] = g
        d_ref[...], nm_ref[...], nv_ref[...] = _adamw_math(w_ref[...], g, m_ref[...], v_ref[...])

    rows = pl.BlockSpec((tr, k, lanes), lambda i: (i, 0, 0))
    return pl.pallas_call(
        body, grid=(pl.cdiv(r, tr),), in_specs=[pl.BlockSpec((tr, k * lanes), lambda i: (i, 0)), rows, rows, rows],
        out_specs=[rows] * 4, out_shape=[jax.ShapeDtypeStruct((r, k, lanes), F32)] * 4,
        compiler_params=_params(("arbitrary",)), name=name,
    )(g, w, m, v)


def _adamw_small(ws, gs, ms, vs):
    n = len(ws)

    def body(*refs):
        w_refs, g_refs, m_refs, v_refs, d_refs, nm_refs, nv_refs = _split(refs, *([n] * 7))
        for k in range(n):
            d_refs[k][...], nm_refs[k][...], nv_refs[k][...] = _adamw_math(w_refs[k][...], g_refs[k][...], m_refs[k][...],
                                                                             v_refs[k][...])

    shapes = [jax.ShapeDtypeStruct(w.shape, F32) for w in ws]
    res = pl.pallas_call(body, out_shape=shapes * 3, name="adamw_small")(*ws, *gs, *ms, *vs)
    return res[:n], res[n:2 * n], res[2 * n:]


def _pack(arrs):
    flat = jnp.concatenate([a.reshape(-1) for a in arrs])
    rows = -(-flat.shape[0] // 1024) * 8
    return jnp.pad(flat, (0, rows * 128 - flat.shape[0])).reshape(rows, 128)


def _unpack(buf, shapes):
    flat = buf.reshape(-1)
    out, off = [], 0
    for s in shapes:
        size = 1
        for d in s:
            size *= d
        out.append(flat[off:off + size].reshape(s))
        off += size
    return out


def _block_rows(w):
    return jnp.pad(w.reshape(512, 4), ((0, 0), (0, 124)))


def _cols(a4):
    return jnp.transpose(a4, (1, 0, 2)).reshape(a4.shape[1], -1)


_LATE = ("w_pa", "w_pb", "w_o", "w_up", "w_down")
_RIDE_IN_PROJ = ("w_pa", "w_pb", "w_o")
_RIDE_MIXER = ("w_up", "w_down_a")
_RIDE_MERGE = ("w_down_b",)


def _full_weights(gathered):
    joined = {"w_in": (D_IN, D_MODEL), "w_o": (D_MODEL, D_MODEL)}
    return {n: (a.reshape(joined[n]) if n in joined else a) for n, a in gathered.items()}


def _local_step(x, target, w, sp, late_shards=None):
    sp = {n: (a.reshape(1, -1) if a.ndim == 1 else a) for n, a in sp.items()}
    wau = jnp.zeros((128, 256), F32).at[0:16].set(sp["w_a_up"])
    wif = jnp.zeros((1536, 128), F32).at[:, 0:8].set(sp["w_if"])
    bif = jnp.zeros((1, 128), F32).at[:, 0:8].set(sp["b_if"])
    p = {"wau": wau, "bau": sp["b_a_up"], "ggla": sp["g_gla_norm"], "cw": sp["conv_w"], "cb": sp["conv_b"],
         "wq": _block_rows(sp["w_q_ml"]), "wk": _block_rows(sp["w_k_ml"]), "wv": _block_rows(sp["w_v_ml"]),
         "wif": wif, "bif": bif, "skip": sp["ml_skip"], "gml": sp["g_ml_norm"]}

    if late_shards is None:
        (pm, gab, h), _ = _in_proj(x, sp["g_pre_mix"], w["w_in"])
        ab, *states = _mixer_fwd(pm, p)
        (x1, mix, merged), _ = _merge_fwd(ab, gab, x, w["w_pa"], w["w_pb"], w["w_o"], sp["g_post_mix"])
    else:
        shard = dict(zip(_LATE, late_shards))
        shard["w_down_a"], shard["w_down_b"] = shard["w_down"][0:512], shard["w_down"][512:1024]
        (pm, gab, h), got = _in_proj(x, sp["g_pre_mix"], w["w_in"], _Gather(_RIDE_IN_PROJ, middle_at=0.45),
                                     [shard[n] for n in _RIDE_IN_PROJ])
        w = dict(w, **_full_weights(dict(zip(_RIDE_IN_PROJ, got))))
        ab, *rest = _mixer_fwd(pm, p, _Gather(_RIDE_MIXER, middle_at=0.62), [shard[n] for n in _RIDE_MIXER])
        states = rest[:4]
        w.update(_full_weights(dict(zip(_RIDE_MIXER, rest[4:]))))
        (x1, mix, merged), got = _merge_fwd(ab, gab, x, w["w_pa"], w["w_pb"], w["w_o"], sp["g_post_mix"],
                                            _Gather(_RIDE_MERGE, middle_at=0.46), [shard[n] for n in _RIDE_MERGE])
        w.update(_full_weights(dict(zip(_RIDE_MERGE, got))))
    dx1, u, dd, h2, dpre, dg_post_mlp, dg_pre_mlp, loss = _mlp(x1, target, sp["g_pre_mlp"], sp["g_post_mlp"],
                                                                w["w_up"], w["w_down_a"], w["w_down_b"])
    dmix, dya, dyb, dgab, dab, dg_post_mix = _merge_bwd(dx1, mix, ab, gab, w["w_pa"], w["w_pb"], w["w_o"], sp["g_post_mix"])
    big = {
        "w_pa": _tn_matmul(ab[:, 0:512], dya, "dw_pa", shards=N_CHIP),
        "w_pb": _tn_matmul(ab[:, 512:1024], dyb, "dw_pb", shards=N_CHIP),
        "w_o": _tn_matmul(merged, dmix, "dw_o"),
        "w_up": _tn_matmul(h2, dpre, "dw_up", shards=N_CHIP),
    }
    if late_shards is None:
        big["w_down"] = _tn_matmul(u, dd, "dw_down")
        dpm, dp, _ = _mixer_bwd(pm, dab, states, p)
    else:
        pieces = lambda n: big[n].reshape((N_CHIP,) + _BIG_SHARD[n])
        big["w_down"], partial = _tn_matmul(u, dd, "dw_down", rider=_Presum(_LATE[:4]),
                                            rider_ins=[pieces(n) for n in _LATE[:4]])
        partial = list(partial) + list(_presum(("w_down",), [pieces("w_down")], "presum_w_down"))
        dpm, dp, parts = _mixer_bwd(pm, dab, states, p, _SendPartials(_LATE), partial)
        big = dict(zip(_LATE, parts))
    big["w_in"] = _dw_in(dpm, dgab, h)
    if late_shards is None:
        (dx, dg_pre_mix), _ = _in_proj_bwd(dpm, dgab, x, dx1, sp["g_pre_mix"], w["w_in"])
    else:
        partial = _presum(("w_in",), [big["w_in"].reshape((N_CHIP,) + _BIG_SHARD["w_in"])], "presum_w_in")
        (dx, dg_pre_mix), parts = _in_proj_bwd(dpm, dgab, x, dx1, sp["g_pre_mix"], w["w_in"], _ReduceRelay(("w_in",)),
                                               partial)
        big["w_in"] = parts[0]
    small = {
        "g_pre_mix": dg_pre_mix, "b_a_up": dp["bau"], "g_gla_norm": dp["ggla"], "conv_b": dp["cb"],
        "w_q_ml": dp["wq"][:, 0:4].reshape(128, 4, 4), "w_k_ml": dp["wk"][:, 0:4].reshape(128, 4, 4),
        "w_v_ml": dp["wv"][:, 0:4].reshape(128, 4, 4),
        "b_if": dp["bif"][:, 0:8], "ml_skip": dp["skip"], "g_ml_norm": dp["gml"], "g_post_mix": dg_post_mix,
        "g_pre_mlp": dg_pre_mlp, "g_post_mlp": dg_post_mlp, "w_a_up": dp["wau"][0:16], "conv_w": dp["cw"],
        "w_if": dp["wif"][:, 0:8], "loss": loss[:, 0:1],
    }
    return dx, big, small


_SMALL_REPL = ("g_pre_mix", "b_a_up", "g_gla_norm", "conv_b", "w_q_ml", "w_k_ml", "w_v_ml", "b_if", "ml_skip",
               "g_ml_norm", "g_post_mix", "g_pre_mlp", "g_post_mlp")
_SMALL_SHARDED = ("w_a_up", "conv_w", "w_if")
_SMALL_ORDER = _SMALL_REPL + _SMALL_SHARDED + ("loss",)
_WEIGHTS = ("g_pre_mix", "w_in", "w_a_up", "b_a_up", "g_gla_norm", "conv_w", "conv_b", "w_q_ml", "w_k_ml", "w_v_ml",
            "w_if", "b_if", "ml_skip", "g_ml_norm", "w_pa", "w_pb", "w_o", "g_post_mix", "g_pre_mlp", "w_up", "w_down",
            "g_post_mlp")


_BLOCK_WEIGHTS = ("w_q_ml", "w_k_ml", "w_v_ml")


def _stored(name, a):
    if name in _BLOCK_WEIGHTS:
        return jnp.transpose(a, (0, 2, 3, 1)).reshape(16, 128)
    if name == "w_if":
        return jnp.transpose(a, (0, 2, 1)).reshape(8, 384)
    return a


def _unstored(name, a):
    if name in _BLOCK_WEIGHTS:
        return jnp.transpose(a.reshape(1, 4, 4, 128), (0, 3, 1, 2))
    if name == "w_if":
        return jnp.transpose(a.reshape(1, 8, 384), (0, 2, 1))
    return a


def _as_shard(name, a):
    return jnp.transpose(a, (2, 0, 1)).reshape(IN_SHARD, D_MODEL // 128, 128) if name == "w_in" else a[0]


def _from_shard(name, a):
    return jnp.transpose(a, (1, 2, 0)).reshape(1, D_MODEL, IN_SHARD) if name == "w_in" else a[None]


def kernel(x, g_pre_mix, w_in, w_a_up, b_a_up, g_gla_norm, conv_w, conv_b, w_q_ml, w_k_ml, w_v_ml, w_if, b_if, ml_skip, g_ml_norm, w_pa, w_pb, w_o, g_post_mix, g_pre_mlp, w_up, w_down, g_post_mlp, loss_target, m_g_pre_mix, m_w_in, m_w_a_up, m_b_a_up, m_g_gla_norm, m_conv_w, m_conv_b, m_w_q_ml, m_w_k_ml, m_w_v_ml, m_w_if, m_b_if, m_ml_skip, m_g_ml_norm, m_w_pa, m_w_pb, m_w_o, m_g_post_mix, m_g_pre_mlp, m_w_up, m_w_down, m_g_post_mlp, v_g_pre_mix, v_w_in, v_w_a_up, v_b_a_up, v_g_gla_norm, v_conv_w, v_conv_b, v_w_q_ml, v_w_k_ml, v_w_v_ml, v_w_if, v_b_if, v_ml_skip, v_g_ml_norm, v_w_pa, v_w_pb, v_w_o, v_g_post_mix, v_g_pre_mlp, v_w_up, v_w_down, v_g_post_mlp):
    args = dict(locals())
    wts = {n: _as_shard(n, args[n]) for n in _WEIGHTS}
    mom = {n: _as_shard(n, args["m_" + n]) for n in _WEIGHTS}
    var = {n: _as_shard(n, args["v_" + n]) for n in _WEIGHTS}
    chip = 2 * lax.axis_index("x") + lax.axis_index("y")

    first = ("w_in",) + _SMALL_SHARDED
    gathered = dict(zip(first, _run_alone(_Gather(("w_in",), [wts[n] for n in _SMALL_SHARDED]),
                                          [wts[n].reshape(IN_SHARD, D_MODEL).astype(BF16) if n == "w_in" else wts[n]
                                           for n in first],
                                          "gather_first")))
    sp = {n: wts[n] for n in _SMALL_REPL}
    sp["w_a_up"] = _cols(gathered["w_a_up"])
    sp["conv_w"] = _cols(gathered["conv_w"])
    sp["w_if"] = gathered["w_if"].reshape(1536, 8)

    dx, big, small = _local_step(x[0], loss_target[0], _full_weights({"w_in": gathered["w_in"]}), sp,
                                 late_shards=[wts[n].astype(BF16) for n in _LATE])

    small_shapes = [small[n].shape for n in _SMALL_ORDER]
    packed = _pack([small[n] for n in _SMALL_ORDER])
    sums, small_sum = _sum_swap(_BIG, [big[n] for n in _BIG], packed)

    grads, delta, new_m, new_v = {}, {}, {}, {}
    for n, g in zip(_BIG, sums):
        if n == "w_in":
            g, d, nm, nv = _adamw_rows(g, wts[n], mom[n], var[n], "adamw_" + n)
        else:
            d, nm, nv = _adamw_big(g, wts[n], mom[n], var[n], "adamw_" + n)
        grads[n], delta[n], new_m[n], new_v[n] = (_from_shard(n, a) for a in (g, d, nm, nv))
    summed = dict(zip(_SMALL_ORDER, _unpack(small_sum, small_shapes)))
    loss = summed["loss"].reshape(())
    summed["w_a_up"] = lax.dynamic_slice_in_dim(summed["w_a_up"], chip * 64, 64, axis=1)
    summed["conv_w"] = lax.dynamic_slice_in_dim(summed["conv_w"], chip * 128, 128, axis=1)
    summed["w_if"] = lax.dynamic_slice_in_dim(summed["w_if"], chip * 384, 384, axis=0)
    small_names = _SMALL_REPL + _SMALL_SHARDED
    g_stored = [_stored(n, summed[n].reshape(args[n].shape)) for n in small_names]
    upd = _adamw_small([_stored(n, args[n]) for n in small_names], g_stored,
                       [_stored(n, args["m_" + n]) for n in small_names], [_stored(n, args["v_" + n]) for n in small_names])
    for dst, arrs in zip((grads, delta, new_m, new_v), (g_stored,) + tuple(upd)):
        dst.update({n: _unstored(n, a) for n, a in zip(small_names, arrs)})

    outs = [loss, dx[None]]
    for group in (grads, delta, new_m, new_v):
        outs += [group[n] for n in _WEIGHTS]
    return tuple(outs)
```

```python
import functools

import jax
import jax.numpy as jnp
from jax import lax
from jax.experimental import pallas as pl
from jax.experimental.pallas import tpu as pltpu

F32 = jnp.float32
BF16 = jnp.bfloat16

SEQ = 2048
D_MODEL = 1024
CHUNK = 64
N_CHUNK = SEQ // CHUNK
HEADS = 4
GLA_DK = 64
GLA_DV = 128
ML_DH = 128
D_FF = 4096
EPS = 1e-6
N_CHIP = 4
N_DEV = 8
TOK_TILE = 256
N_TOK_TILE = SEQ // TOK_TILE
SWEEP = 2
assert CHUNK == 64
N_SWEEP = N_CHUNK // SWEEP

PM_W = 2688
PM_XM = 1536
PM_OP = 2048
PM_AL = 2560
GAB_W = 2048
D_IN = 4624
IN_SHARD = D_IN // N_CHIP
IN_ALOW = 1536
IN_XM = 1552
IN_GATES = 2576

ADAM_LR = 0.001
ADAM_B1 = 0.9
ADAM_B2 = 0.999
ADAM_EPS = 1e-08
ADAM_WD = 0.01
ADAM_STEP = 10

VMEM_LIMIT = 56 * 1024 * 1024


def _params(sem=None):
    return pltpu.CompilerParams(dimension_semantics=sem, vmem_limit_bytes=VMEM_LIMIT)


def _dot(a, b, ca, cb):
    return lax.dot_general(a.astype(BF16), b.astype(BF16), (((ca,), (cb,)), ((), ())), preferred_element_type=F32)


def _pmm_nn(a, b):
    return _dot(a, b, 1, 0)


def _pmm_nt(a, b):
    return _dot(a, b, 1, 1)


def _pmm_tn(a, b):
    return _dot(a, b, 0, 0)


def _pcmm(c, x):
    return lax.dot_general(c, x, (((1,), (0,)), ((), ())), precision=lax.Precision.HIGHEST, preferred_element_type=F32)


@jax.custom_vjp
def _mm_nn(a, b):
    return _dot(a, b, 1, 0)


@jax.custom_vjp
def _mm_nt(a, b):
    return _dot(a, b, 1, 1)


@jax.custom_vjp
def _mm_tn(a, b):
    return _dot(a, b, 0, 0)


_mm_nn.defvjp(lambda a, b: (_dot(a, b, 1, 0), (a, b)), lambda r, g: (_mm_nt(g, r[1]), _mm_tn(r[0], g)))
_mm_nt.defvjp(lambda a, b: (_dot(a, b, 1, 1), (a, b)), lambda r, g: (_mm_nn(g, r[1]), _mm_tn(g, r[0])))
_mm_tn.defvjp(lambda a, b: (_dot(a, b, 0, 0), (a, b)), lambda r, g: (_mm_nt(r[1], g), _mm_nn(r[0], g)))


@jax.custom_vjp
def _cmm(c, x):
    return _pcmm(c, x)


_cmm.defvjp(
    lambda c, x: (_pcmm(c, x), c),
    lambda c, g: (jnp.zeros_like(c), lax.dot_general(c, g, (((0,), (0,)), ((), ())), precision=lax.Precision.HIGHEST,
                                                      preferred_element_type=F32)),
)

_PLAIN_OPS = (_pmm_nn, _pmm_nt, _pmm_tn, _pcmm)
_VJP_OPS = (_mm_nn, _mm_nt, _mm_tn, _cmm)


def _sigmoid(x):
    return 0.5 * (jnp.tanh(0.5 * x) + 1.0)


def _log_sigmoid(x):
    return jnp.minimum(x, 0.0) - jnp.log(1.0 + jnp.exp(-jnp.abs(x)))


def _mean(x):
    return jnp.mean(x, axis=-1, keepdims=True)


def _nt(a, b):
    return lax.dot_general(a, b, (((1,), (1,)), ((), ())), preferred_element_type=F32)


def _tn(a, b):
    return lax.dot_general(a, b, (((0,), (0,)), ((), ())), preferred_element_type=F32)


def _mixer_chunk(ops, p, st, pm, xprev8):
    mm_nn, mm_nt, mm_tn, cmm = ops
    n_rows = pm.shape[0]
    n_ch = n_rows // CHUNK
    row = lax.broadcasted_iota(jnp.int32, (n_rows, n_rows), 0)
    col = lax.broadcasted_iota(jnp.int32, (n_rows, n_rows), 1)
    tri = jnp.logical_and((row >> 6) == (col >> 6), row >= col).astype(F32)
    causal = tri[0:CHUNK, 0:CHUNK] > 0.0
    q = pm[:, 0:256]
    k = pm[:, 256:512]
    v = pm[:, 512:1024]
    g = pm[:, 1024:1536]
    xm = pm[:, PM_XM:PM_XM + 512]
    opre = pm[:, PM_OP:PM_OP + 512]
    alow = pm[:, PM_AL:PM_AL + 128]
    hs = range(HEADS)
    cs = range(n_ch)
    pairs = [(i, h) for i in cs for h in hs]
    rs = [slice(i * CHUNK, (i + 1) * CHUNK) for i in cs]
    last = [slice((i + 1) * CHUNK - 1, (i + 1) * CHUNK) for i in cs]
    s6 = [slice(h * GLA_DK, (h + 1) * GLA_DK) for h in hs]
    s12 = [slice(h * 128, (h + 1) * 128) for h in hs]

    xx = jnp.concatenate([xprev8, xm], axis=0)
    pre = p["cb"]
    for j in range(4):
        pre = pre + p["cw"][j:j + 1, :] * xx[5 + j:5 + j + n_rows, :]
    xc = pre * _sigmoid(pre)
    qm = [mm_nn(xc[:, s12[h]], p["wq"][h]) for h in hs]
    km = [mm_nn(xc[:, s12[h]], p["wk"][h]) for h in hs]
    vm = [mm_nn(xm[:, s12[h]], p["wv"][h]) for h in hs]
    qcat = jnp.concatenate(qm, axis=1)
    kcat = jnp.concatenate(km, axis=1)
    vcat = jnp.concatenate(vm, axis=1)
    gates = (mm_nn(qcat, p["wif"][0:512]) + mm_nn(kcat, p["wif"][512:1024]) + mm_nn(vcat, p["wif"][1024:1536])
             + p["bif"])
    lf = _log_sigmoid(gates)
    fc = cmm(tri, lf)
    gates_t = gates.T
    fc_t = fc.T

    la = _log_sigmoid(mm_nn(alow, p["wau"]) + p["bau"]) * (1.0 / 16.0)
    cum = cmm(tri, la)
    cum_last = [cum[last[i], :] for i in cs]
    to_end = jnp.concatenate([cum_last[i] - cum[rs[i], :] for i in cs], axis=0)
    e_pos = jnp.exp(cum)
    e_neg = jnp.exp(-cum)
    qs = q * (GLA_DK ** -0.5)
    qp = qs * e_pos
    qn = qs * e_neg
    kp = k * e_pos
    kn = k * e_neg
    kl = k * jnp.exp(to_end)
    dec = [jnp.exp(cum_last[i]) for i in cs]
    ks = [km[h] * (ML_DH ** -0.5) for h in hs]
    li_c = {(i, h): gates[rs[i], h:h + 1] for i, h in pairs}
    fc_c = {(i, h): fc[rs[i], 4 + h:5 + h] for i, h in pairs}
    f_last = {(i, h): fc[last[i], 4 + h:5 + h] for i, h in pairs}

    a_fwd = {(i, h): mm_nt(qp[rs[i], s6[h]], kn[rs[i], s6[h]]) for i, h in pairs}
    a_bwd = {(i, h): mm_nt(qn[rs[i], s6[h]], kp[rs[i], s6[h]]) for i, h in pairs}
    s_chunk = {(i, h): mm_tn(v[rs[i], s12[h]], kl[rs[i], s6[h]]) for i, h in pairs}
    qk = {(i, h): mm_nt(qm[h][rs[i]], ks[h][rs[i]]) for i, h in pairs}
    a = {ih: f_last[ih] - fc_c[ih] + li_c[ih] for ih in pairs}
    m_loc = {ih: jnp.max(a[ih], axis=0, keepdims=True) for ih in pairs}
    kw = {(i, h): ks[h][rs[i]] * jnp.exp(a[(i, h)] - m_loc[(i, h)]) for i, h in pairs}
    c_chunk = {(i, h): mm_tn(kw[(i, h)], vm[h][rs[i]]) for i, h in pairs}
    mem = {(0, h): st["S"][h] for h in hs}
    c_in = {(0, h): st["C"][h] for h in hs}
    n_in = {(0, h): st["n"][h] for h in hs}
    m_in = {(0, h): st["m"][h][:, 0:1] for h in hs}
    for i, h in pairs:
        mem[(i + 1, h)] = mem[(i, h)] * dec[i][:, s6[h]] + s_chunk[(i, h)]
        m_nx = jnp.maximum(f_last[(i, h)] + m_in[(i, h)], m_loc[(i, h)])
        sp = jnp.exp(f_last[(i, h)] + m_in[(i, h)] - m_nx)
        sl = jnp.exp(m_loc[(i, h)] - m_nx)
        c_in[(i + 1, h)] = sp * c_in[(i, h)] + sl * c_chunk[(i, h)]
        n_in[(i + 1, h)] = sp * n_in[(i, h)] + sl * jnp.sum(kw[(i, h)], axis=0, keepdims=True)
        m_in[(i + 1, h)] = m_nx
    s_new = [mem[(n_ch, h)] for h in hs]
    o_inter = {(i, h): mm_nt(qp[rs[i], s6[h]], mem[(i, h)]) for i, h in pairs}
    q_c = {(i, h): mm_nn(qm[h][rs[i]], c_in[(i, h)]) for i, h in pairs}
    scores = {ih: jnp.where(causal, a_fwd[ih], a_bwd[ih]) for ih in pairs}
    log_d = {(i, h): gates_t[h:h + 1, rs[i]] - jnp.abs(fc_c[(i, h)] - fc_t[4 + h:5 + h, rs[i]]) for i, h in pairs}
    g_int = {ih: fc_c[ih] + m_in[ih] for ih in pairs}
    m_t = {ih: jnp.maximum(g_int[ih], jnp.max(log_d[ih], axis=1, keepdims=True)) for ih in pairs}
    s = {ih: qk[ih] * jnp.exp(log_d[ih] - m_t[ih]) for ih in pairs}
    scl = {ih: jnp.exp(g_int[ih] - m_t[ih]) for ih in pairs}
    o = {(i, h): mm_nn(scores[(i, h)], v[rs[i], s12[h]]) + o_inter[(i, h)] for i, h in pairs}
    num = {(i, h): mm_nn(s[(i, h)], vm[h][rs[i]]) + scl[(i, h)] * q_c[(i, h)] for i, h in pairs}
    o = {ih: o[ih] * lax.rsqrt(_mean(o[ih] * o[ih]) + EPS) * p["ggla"] for ih in pairs}
    gate = g * _sigmoid(g)
    out_a = {(i, h): o[(i, h)] * gate[rs[i], s12[h]] for i, h in pairs}
    den = {(i, h): jnp.sum(s[(i, h)], axis=1, keepdims=True)
           + scl[(i, h)] * jnp.sum(qm[h][rs[i]] * n_in[(i, h)], axis=1, keepdims=True) for i, h in pairs}
    den = {ih: jnp.maximum(jnp.abs(den[ih]), jnp.exp(-m_t[ih])) for ih in pairs}
    open_gate = _sigmoid(opre)
    hc = {(i, h): num[(i, h)] / den[(i, h)] * open_gate[rs[i], s12[h]] for i, h in pairs}
    d0 = {ih: hc[ih] - _mean(hc[ih]) for ih in pairs}
    y = {ih: d0[ih] * lax.rsqrt(_mean(d0[ih] * d0[ih]) + EPS) for ih in pairs}
    skipped = p["skip"] * xc
    out_b = {(i, h): y[(i, h)] * p["gml"][:, s12[h]] + skipped[rs[i], s12[h]] for i, h in pairs}
    ab = jnp.concatenate([jnp.concatenate([out_a[(i, h)] for h in hs] + [out_b[(i, h)] for h in hs], axis=1) for i in cs],
                         axis=0)
    new = {"S": s_new, "C": [c_in[(n_ch, h)] for h in hs], "n": [n_in[(n_ch, h)] for h in hs],
           "m": [jnp.broadcast_to(m_in[(n_ch, h)], (1, ML_DH)) for h in hs]}
    return ab, new


_P_NAMES = ("wau", "bau", "ggla", "cw", "cb", "wq", "wk", "wv", "wif", "bif", "skip", "gml")
_P_SHAPES = {
    "wau": (128, 256), "bau": (1, 256), "ggla": (1, 128), "cw": (4, 512), "cb": (1, 512),
    "wq": (512, 128), "wk": (512, 128), "wv": (512, 128),
    "wif": (1536, 128), "bif": (1, 128), "skip": (1, 512), "gml": (1, 512),
}
_P_BLOCKDIAG = ("wq", "wk", "wv")
_S_NAMES = ("S", "C", "n", "m")
_S_SHAPES = {"S": (HEADS, GLA_DV, GLA_DK), "C": (HEADS, ML_DH, ML_DH), "n": (HEADS, 1, ML_DH), "m": (HEADS, 1, ML_DH)}


def _per_head(ref):
    return [ref[h] for h in range(HEADS)]


def _block_mask():
    r = lax.broadcasted_iota(jnp.int32, (128, 128), 0)
    c = lax.broadcasted_iota(jnp.int32, (128, 128), 1)
    same_block = (r >> 2) == (c >> 2)
    spread = jnp.logical_and(r < 4, (c & 3) == r)
    return same_block.astype(F32), spread.astype(F32)


def _expand_blockdiag(w_ref, dense_ref):
    same_block, spread = _block_mask()
    for h in range(HEADS):
        tiled = _pmm_nn(w_ref[h * 128:(h + 1) * 128, :], spread)
        dense_ref[h] = tiled * same_block


def _collect_blockdiag(ddense_ref, dw_ref):
    same_block, spread = _block_mask()
    for h in range(HEADS):
        dw_ref[h * 128:(h + 1) * 128, :] = lax.dot_general(
            ddense_ref[h] * same_block, spread, (((1,), (1,)), ((), ())), precision=lax.Precision.HIGHEST,
            preferred_element_type=F32)


def _const_spec(shape):
    zeros = (0,) * len(shape)
    return pl.BlockSpec(shape, lambda i: zeros)


def _split(refs, *counts):
    out, at = [], 0
    for c in counts:
        out.append(refs[at:at + c])
        at += c
    assert at == len(refs)
    return out


def _ride(rider, phases, cond, ins, outs, sems):
    if rider is None or not any(hasattr(rider, phase) for phase in phases):
        return
    lands, (send_sems, recv_sems, flush_sems) = sems[:-3], sems[-3:]

    @pl.when(cond)
    def _():
        for phase in phases:
            getattr(rider, phase)(ins, lands, send_sems, recv_sems)
            if hasattr(rider, "flush"):
                rider.flush(phase, lands, outs, flush_sems)
        if "last" in phases and not hasattr(rider, "flush"):
            flush = [pltpu.make_async_copy(lands[k], outs[k], flush_sems.at[k]) for k in range(len(outs))]
            for cp in flush:
                cp.start()
            for cp in flush:
                cp.wait()


def _middle_step(rider, n_steps):
    return min(n_steps - 2, int(getattr(rider, "middle_at", 1.0) * n_steps))


def _rider_specs(rider, rider_ins):
    if rider is None:
        return [], [], [], []
    scratch = [pltpu.VMEM(s.shape, s.dtype) for s in list(rider.out_shape) + list(getattr(rider, "work_shape", ()))]
    scratch += [pltpu.SemaphoreType.DMA((rider.n_sems,)), pltpu.SemaphoreType.DMA((rider.n_sems,)),
                pltpu.SemaphoreType.DMA((getattr(rider, "n_flush", len(rider.out_shape)),))]
    in_space = getattr(rider, "in_space", VMEM_WHOLE)
    return [in_space] * len(rider_ins), [ANY] * len(rider.out_shape), list(rider.out_shape), scratch


def _mixer_fwd(pm, p, rider=None, rider_ins=()):
    n_p = len(_P_NAMES)
    r_in, r_out_specs, r_out_shape, r_sems = _rider_specs(rider, rider_ins)

    def body(*refs):
        (pm_ref, xprev_ref), p_list, ride_in, (ab_ref,), so_refs, ride_out, sc_refs, dense_list, sems = _split(
            refs, 2, n_p, len(r_in), 1, 4, len(r_out_specs), 4, 3, len(r_sems))
        p_refs = dict(zip(_P_NAMES, p_list))
        dense = dict(zip(_P_BLOCKDIAG, dense_list))
        n = pl.program_id(0)
        _ride(rider, ("first",), n == 0, ride_in, ride_out, sems)

        @pl.when(n == 0)
        def _():
            for r in sc_refs:
                r[...] = jnp.zeros_like(r)
            for nm in _P_BLOCKDIAG:
                _expand_blockdiag(p_refs[nm], dense[nm])

        st = {name: _per_head(r) for name, r in zip(_S_NAMES, sc_refs)}
        pv = {nm: (_per_head(dense[nm]) if nm in _P_BLOCKDIAG else p_refs[nm][...]) for nm in _P_NAMES}
        for name, r in zip(_S_NAMES, so_refs):
            for h in range(HEADS):
                r[0, h] = st[name][h]
        xprev8 = jnp.where(n > 0, xprev_ref[CHUNK - 8:CHUNK, :], 0.0)
        ab, st = _mixer_chunk(_PLAIN_OPS, pv, st, pm_ref[...], xprev8)
        ab_ref[...] = ab.astype(BF16)
        for name, r in zip(_S_NAMES, sc_refs):
            for h in range(HEADS):
                r[h] = st[name][h]
        _ride(rider, ("middle",), n == _middle_step(rider, N_SWEEP), ride_in, ride_out, sems)
        _ride(rider, ("last",), n == N_SWEEP - 1, ride_in, ride_out, sems)

    in_specs = [pl.BlockSpec((SWEEP * CHUNK, PM_W), lambda i: (i, 0)),
                pl.BlockSpec((CHUNK, 512), lambda i: (jnp.maximum(SWEEP * i - 1, 0), PM_XM // 512))]
    in_specs += [_const_spec(_P_SHAPES[nm]) for nm in _P_NAMES] + r_in
    out_specs = [pl.BlockSpec((SWEEP * CHUNK, 1024), lambda i: (i, 0))]
    out_shape = [jax.ShapeDtypeStruct((SEQ, 1024), BF16)]
    for nm in _S_NAMES:
        shp = _S_SHAPES[nm]
        out_specs.append(pl.BlockSpec((1,) + shp, lambda i: (i, 0, 0, 0)))
        out_shape.append(jax.ShapeDtypeStruct((N_SWEEP,) + shp, F32))
    return pl.pallas_call(
        body, grid=(N_SWEEP,), in_specs=in_specs, out_specs=out_specs + r_out_specs, out_shape=out_shape + r_out_shape,
        scratch_shapes=[pltpu.VMEM(_S_SHAPES[nm], F32) for nm in _S_NAMES]
        + [pltpu.VMEM((HEADS, 128, 128), F32) for _ in _P_BLOCKDIAG] + r_sems,
        compiler_params=_params(("arbitrary",)), name="mixer_fwd",
    )(pm, pm, *[p[nm] for nm in _P_NAMES], *rider_ins)


def _mixer_bwd(pm, dab, states, p, rider=None, rider_ins=()):
    n_p = len(_P_NAMES)
    r_in, r_out_specs, r_out_shape, r_sems = _rider_specs(rider, rider_ins)

    def body(*refs):
        ((pm_ref, xprev_ref, dab_ref), si_refs, p_list, ride_in, (dpm_ref,), dp_list, ride_out, ds_refs, (carry_ref,),
         dense_list, ddense_list, sems) = _split(refs, 3, 4, n_p, len(r_in), 1, n_p, len(r_out_specs), 4, 1, 3, 3, len(r_sems))
        p_refs = dict(zip(_P_NAMES, p_list))
        dp_refs = dict(zip(_P_NAMES, dp_list))
        dense = dict(zip(_P_BLOCKDIAG, dense_list))
        ddense = dict(zip(_P_BLOCKDIAG, ddense_list))
        i = pl.program_id(0)
        blk = N_SWEEP - 1 - i
        _ride(rider, ("first",), i == 0, ride_in, ride_out, sems)

        @pl.when(i == 0)
        def _():
            for r in ds_refs:
                r[...] = jnp.zeros_like(r)
            for nm in _P_NAMES:
                if nm in _P_BLOCKDIAG:
                    ddense[nm][...] = jnp.zeros_like(ddense[nm])
                    _expand_blockdiag(p_refs[nm], dense[nm])
                else:
                    dp_refs[nm][...] = jnp.zeros_like(dp_refs[nm])
            carry_ref[...] = jnp.zeros_like(carry_ref)

        pv = {nm: (_per_head(dense[nm]) if nm in _P_BLOCKDIAG else p_refs[nm][...]) for nm in _P_NAMES}
        dst = {name: _per_head(r) for name, r in zip(_S_NAMES, ds_refs)}
        st = {name: [r[0, h] for h in range(HEADS)] for name, r in zip(_S_NAMES, si_refs)}
        xprev8 = jnp.where(blk > 0, xprev_ref[CHUNK - 8:CHUNK, :], 0.0)
        _, vjp = jax.vjp(functools.partial(_mixer_chunk, _VJP_OPS), pv, st, pm_ref[...], xprev8)
        dp_sum, dst, dpm, dxprev8 = vjp((dab_ref[...], dst))
        reach = jnp.concatenate([jnp.zeros((SWEEP * CHUNK - 8, 512), F32), carry_ref[...]], axis=0)
        dpm_ref[:, 0:PM_XM] = dpm[:, 0:PM_XM].astype(BF16)
        dpm_ref[:, PM_XM:PM_XM + 512] = (dpm[:, PM_XM:PM_XM + 512] + reach).astype(BF16)
        dpm_ref[:, PM_XM + 512:PM_W] = dpm[:, PM_XM + 512:PM_W].astype(BF16)
        carry_ref[...] = dxprev8
        for name, r in zip(_S_NAMES, ds_refs):
            for h in range(HEADS):
                r[h] = dst[name][h]
        for nm in _P_NAMES:
            if nm in _P_BLOCKDIAG:
                for h in range(HEADS):
                    ddense[nm][h] += dp_sum[nm][h]
            else:
                dp_refs[nm][...] += dp_sum[nm]

        @pl.when(i == N_SWEEP - 1)
        def _():
            for nm in _P_BLOCKDIAG:
                _collect_blockdiag(ddense[nm], dp_refs[nm])

        _ride(rider, ("middle",), i == _middle_step(rider, N_SWEEP), ride_in, ride_out, sems)
        _ride(rider, ("last",), i == N_SWEEP - 1, ride_in, ride_out, sems)

    rev = lambda i: (N_SWEEP - 1 - i, 0)
    in_specs = [pl.BlockSpec((SWEEP * CHUNK, PM_W), rev),
                pl.BlockSpec((CHUNK, 512), lambda i: (jnp.maximum(SWEEP * (N_SWEEP - 1 - i) - 1, 0), PM_XM // 512)),
                pl.BlockSpec((SWEEP * CHUNK, 1024), rev)]
    for nm in _S_NAMES:
        in_specs.append(pl.BlockSpec((1,) + _S_SHAPES[nm], lambda i: (N_SWEEP - 1 - i, 0, 0, 0)))
    in_specs += [_const_spec(_P_SHAPES[nm]) for nm in _P_NAMES] + r_in
    out_specs = [pl.BlockSpec((SWEEP * CHUNK, PM_W), rev)] + [_const_spec(_P_SHAPES[nm]) for nm in _P_NAMES]
    out_shape = [jax.ShapeDtypeStruct((SEQ, PM_W), BF16)] + [jax.ShapeDtypeStruct(_P_SHAPES[nm], F32) for nm in _P_NAMES]
    res = pl.pallas_call(
        body, grid=(N_SWEEP,), in_specs=in_specs, out_specs=out_specs + r_out_specs, out_shape=out_shape + r_out_shape,
        scratch_shapes=[pltpu.VMEM(_S_SHAPES[nm], F32) for nm in _S_NAMES] + [pltpu.VMEM((8, 512), F32)]
        + [pltpu.VMEM((HEADS, 128, 128), F32) for _ in range(2 * len(_P_BLOCKDIAG))] + r_sems,
        compiler_params=_params(("arbitrary",)), name="mixer_bwd",
    )(pm, pm, dab, *states, *[p[nm] for nm in _P_NAMES], *rider_ins)
    return res[0], dict(zip(_P_NAMES, res[1:1 + n_p])), res[1 + n_p:]


def _tok(width):
    return pl.BlockSpec((TOK_TILE, width), lambda i: (i, 0))


def _once(shape):
    zeros = (0,) * len(shape)
    return pl.BlockSpec(shape, lambda i: zeros, pipeline_mode=pl.Buffered(1))


def _rms_fwd(x):
    r = lax.rsqrt(_mean(x * x) + EPS)
    return x * r, r


def _rms_bwd(dy, xn, r, g):
    gd = dy * g
    return r * (gd - xn * _mean(xn * gd))


def _tiled_call(body, in_specs, out_specs, out_shape, args, name, rider=None, rider_ins=()):
    r_in, r_out_specs, r_out_shape, r_scratch = _rider_specs(rider, rider_ins)
    n_in, n_out = len(in_specs), len(out_specs)

    def hosted(*refs):
        ins, ride_in, outs, ride_out, scratch = _split(refs, n_in, len(r_in), n_out, len(r_out_specs), len(r_scratch))
        i = pl.program_id(0)
        _ride(rider, ("first",), i == 0, ride_in, ride_out, scratch)
        body(*ins, *outs)
        _ride(rider, ("early",), i == 1, ride_in, ride_out, scratch)
        _ride(rider, ("middle",), i == _middle_step(rider, N_TOK_TILE), ride_in, ride_out, scratch)
        _ride(rider, ("last",), i == N_TOK_TILE - 1, ride_in, ride_out, scratch)

    res = pl.pallas_call(
        hosted, grid=(N_TOK_TILE,), in_specs=list(in_specs) + r_in, out_specs=list(out_specs) + r_out_specs,
        out_shape=list(out_shape) + r_out_shape, scratch_shapes=r_scratch,
        compiler_params=_params(("arbitrary",)), name=name,
    )(*args, *rider_ins)
    return res[:n_out], res[n_out:]


def _in_proj(x, g_pre, wt_in, rider=None, rider_ins=()):
    def body(x_ref, g_ref, wt_ref, pm_ref, gab_ref, h_ref):
        xn, _ = _rms_fwd(x_ref[...])
        h = (xn * g_ref[...]).astype(BF16)
        h_ref[...] = h
        pm_ref[:, 0:PM_XM] = _nt(h, wt_ref[0:IN_ALOW, :])
        pm_ref[:, PM_XM:PM_AL] = _nt(h, wt_ref[IN_XM:IN_GATES, :])
        pm_ref[:, PM_AL:PM_W] = _nt(h, wt_ref[IN_ALOW:IN_ALOW + 128, :])
        gab_ref[...] = _nt(h, wt_ref[IN_GATES:D_IN, :])

    return _tiled_call(
        body, [_tok(D_MODEL), _once((1, D_MODEL)), _once((D_IN, D_MODEL))], [_tok(PM_W), _tok(GAB_W), _tok(D_MODEL)],
        [jax.ShapeDtypeStruct((SEQ, PM_W), F32), jax.ShapeDtypeStruct((SEQ, GAB_W), F32),
         jax.ShapeDtypeStruct((SEQ, D_MODEL), BF16)], (x, g_pre, wt_in), "in_proj", rider, rider_ins)


def _merge_fwd(ab, gab, x, w_pa4, w_pb4, w_o, g_post, rider=None, rider_ins=()):
    def body(ab_ref, gab_ref, x_ref, wpa_ref, wpb_ref, wo_ref, g_ref, x1_ref, mix_ref, mg_ref):
        a = ab_ref[:, 0:512]
        b = ab_ref[:, 512:1024]
        for j in range(N_CHIP):
            blk = slice(j * 256, (j + 1) * 256)
            ya = jnp.dot(a, wpa_ref[j], preferred_element_type=F32)
            yb = jnp.dot(b, wpb_ref[j], preferred_element_type=F32)
            sa = _sigmoid(gab_ref[:, j * 256:(j + 1) * 256])
            sb = _sigmoid(gab_ref[:, 1024 + j * 256:1024 + (j + 1) * 256])
            mg_ref[:, blk] = (sa * ya + sb * yb).astype(BF16)
        mix = jnp.dot(mg_ref[...], wo_ref[...], preferred_element_type=F32)
        mix_ref[...] = mix
        mn, _ = _rms_fwd(mix)
        x1_ref[...] = x_ref[...] + mn * g_ref[...]

    return _tiled_call(
        body, [_tok(1024), _tok(GAB_W), _tok(D_MODEL), _once((N_CHIP, 512, 256)), _once((N_CHIP, 512, 256)),
               _once((D_MODEL, D_MODEL)), _once((1, D_MODEL))], [_tok(D_MODEL), _tok(D_MODEL), _tok(D_MODEL)],
        [jax.ShapeDtypeStruct((SEQ, D_MODEL), F32), jax.ShapeDtypeStruct((SEQ, D_MODEL), F32),
         jax.ShapeDtypeStruct((SEQ, D_MODEL), BF16)], (ab, gab, x, w_pa4, w_pb4, w_o, g_post), "merge_fwd", rider, rider_ins)


def _mlp(x1, target, g_pre, g_post, w_up4, w_down_a4, w_down_b4):
    def body(x1_ref, t_ref, gpre_ref, gpost_ref, wup_ref, wda_ref, wdb_ref,
             dx1_ref, u_ref, dd_ref, h2_ref, dpre_ref, dgpost_ref, dgpre_ref, loss_ref):
        @pl.when(pl.program_id(0) == 0)
        def _():
            dgpost_ref[...] = jnp.zeros_like(dgpost_ref)
            dgpre_ref[...] = jnp.zeros_like(dgpre_ref)
            loss_ref[...] = jnp.zeros_like(loss_ref)

        x1 = x1_ref[...]
        gpre = gpre_ref[...]
        gpost = gpost_ref[...]
        xn2, r2 = _rms_fwd(x1)
        h2 = (xn2 * gpre).astype(BF16)
        h2_ref[...] = h2
        rl = []
        d = jnp.zeros((TOK_TILE, D_MODEL), F32)
        for j in range(N_CHIP):
            blk = slice(j * 1024, (j + 1) * 1024)
            r = jnp.maximum(jnp.dot(h2, wup_ref[j], preferred_element_type=F32), 0.0)
            rl.append(r)
            u = (r * r).astype(BF16)
            u_ref[:, blk] = u
            d = d + jnp.dot(u[:, 0:512], wda_ref[j], preferred_element_type=F32)
            d = d + jnp.dot(u[:, 512:1024], wdb_ref[j], preferred_element_type=F32)
        dn, r3 = _rms_fwd(d)
        diff = x1 + dn * gpost - t_ref[...]
        loss_ref[...] += jnp.sum(diff * diff, keepdims=True) * (0.5 / D_MODEL)
        dy = diff * (1.0 / D_MODEL)
        dgpost_ref[...] += jnp.sum(dy * dn, axis=0, keepdims=True)
        dd = _rms_bwd(dy, dn, r3, gpost).astype(BF16)
        dd_ref[...] = dd
        dh2 = jnp.zeros((TOK_TILE, D_MODEL), F32)
        for j in range(N_CHIP):
            blk = slice(j * 1024, (j + 1) * 1024)
            du = jnp.concatenate([_nt(dd, wda_ref[j]), _nt(dd, wdb_ref[j])], axis=1)
            dpre = (du * (2.0 * rl[j])).astype(BF16)
            dpre_ref[:, blk] = dpre
            dh2 = dh2 + _nt(dpre, wup_ref[j])
        dgpre_ref[...] += jnp.sum(dh2 * xn2, axis=0, keepdims=True)
        dx1_ref[...] = dy + _rms_bwd(dh2, xn2, r2, gpre)

    acc = pl.BlockSpec((1, D_MODEL), lambda i: (0, 0))
    return pl.pallas_call(
        body, grid=(N_TOK_TILE,),
        in_specs=[_tok(D_MODEL), _tok(D_MODEL), _once((1, D_MODEL)), _once((1, D_MODEL)),
                  _once((N_CHIP, D_MODEL, 1024)), _once((N_CHIP, 512, D_MODEL)), _once((N_CHIP, 512, D_MODEL))],
        out_specs=[_tok(D_MODEL), _tok(D_FF), _tok(D_MODEL), _tok(D_MODEL), _tok(D_FF), acc, acc,
                   pl.BlockSpec((1, 128), lambda i: (0, 0))],
        out_shape=[jax.ShapeDtypeStruct((SEQ, D_MODEL), F32), jax.ShapeDtypeStruct((SEQ, D_FF), BF16),
                   jax.ShapeDtypeStruct((SEQ, D_MODEL), BF16), jax.ShapeDtypeStruct((SEQ, D_MODEL), BF16),
                   jax.ShapeDtypeStruct((SEQ, D_FF), BF16), jax.ShapeDtypeStruct((1, D_MODEL), F32),
                   jax.ShapeDtypeStruct((1, D_MODEL), F32), jax.ShapeDtypeStruct((1, 128), F32)],
        compiler_params=_params(("arbitrary",)), name="mlp_fwd_bwd",
    )(x1, target, g_pre, g_post, w_up4, w_down_a4, w_down_b4)


def _merge_bwd(dx1, mix, ab, gab, w_pa4, w_pb4, w_o, g_post):
    def body(dx1_ref, mix_ref, ab_ref, gab_ref, wpa_ref, wpb_ref, wo_ref, g_ref,
             dmix_ref, dya_ref, dyb_ref, dgab_ref, dab_ref, dg_ref):
        @pl.when(pl.program_id(0) == 0)
        def _():
            dg_ref[...] = jnp.zeros_like(dg_ref)

        dx1 = dx1_ref[...]
        mn, r = _rms_fwd(mix_ref[...])
        dg_ref[...] += jnp.sum(dx1 * mn, axis=0, keepdims=True)
        dmix = _rms_bwd(dx1, mn, r, g_ref[...]).astype(BF16)
        dmix_ref[...] = dmix
        dmerged = _nt(dmix, wo_ref[...])
        a = ab_ref[:, 0:512]
        b = ab_ref[:, 512:1024]
        da = jnp.zeros((TOK_TILE, 512), F32)
        db = jnp.zeros((TOK_TILE, 512), F32)
        for j in range(N_CHIP):
            blk = slice(j * 256, (j + 1) * 256)
            blk_b = slice(1024 + j * 256, 1024 + (j + 1) * 256)
            dm = dmerged[:, blk]
            ya = jnp.dot(a, wpa_ref[j], preferred_element_type=F32)
            yb = jnp.dot(b, wpb_ref[j], preferred_element_type=F32)
            sa = _sigmoid(gab_ref[:, blk])
            sb = _sigmoid(gab_ref[:, blk_b])
            dya = (dm * sa).astype(BF16)
            dyb = (dm * sb).astype(BF16)
            dya_ref[:, blk] = dya
            dyb_ref[:, blk] = dyb
            dgab_ref[:, blk] = (dm * ya * sa * (1.0 - sa)).astype(BF16)
            dgab_ref[:, blk_b] = (dm * yb * sb * (1.0 - sb)).astype(BF16)
            da = da + _nt(dya, wpa_ref[j])
            db = db + _nt(dyb, wpb_ref[j])
        dab_ref[:, 0:512] = da
        dab_ref[:, 512:1024] = db

    return pl.pallas_call(
        body, grid=(N_TOK_TILE,),
        in_specs=[_tok(D_MODEL), _tok(D_MODEL), _tok(1024), _tok(GAB_W), _once((N_CHIP, 512, 256)),
                  _once((N_CHIP, 512, 256)), _once((D_MODEL, D_MODEL)), _once((1, D_MODEL))],
        out_specs=[_tok(D_MODEL), _tok(D_MODEL), _tok(D_MODEL), _tok(GAB_W), _tok(1024),
                   pl.BlockSpec((1, D_MODEL), lambda i: (0, 0))],
        out_shape=[jax.ShapeDtypeStruct((SEQ, D_MODEL), BF16), jax.ShapeDtypeStruct((SEQ, D_MODEL), BF16),
                   jax.ShapeDtypeStruct((SEQ, D_MODEL), BF16), jax.ShapeDtypeStruct((SEQ, GAB_W), BF16),
                   jax.ShapeDtypeStruct((SEQ, 1024), F32), jax.ShapeDtypeStruct((1, D_MODEL), F32)],
        compiler_params=_params(("arbitrary",)), name="merge_bwd",
    )(dx1, mix, ab, gab, w_pa4, w_pb4, w_o, g_post)


def _in_proj_bwd(dpm, dgab, x, dx1, g_pre, wt_in, rider=None, rider_ins=()):
    def body(dpm_ref, dgab_ref, x_ref, dx1_ref, g_ref, wt_ref, dx_ref, dg_ref):
        @pl.when(pl.program_id(0) == 0)
        def _():
            dg_ref[...] = jnp.zeros_like(dg_ref)

        dh = jnp.dot(dpm_ref[:, 0:PM_XM], wt_ref[0:IN_ALOW, :], preferred_element_type=F32)
        dh = dh + jnp.dot(dpm_ref[:, PM_XM:PM_AL], wt_ref[IN_XM:IN_GATES, :], preferred_element_type=F32)
        dh = dh + jnp.dot(dpm_ref[:, PM_AL:PM_W], wt_ref[IN_ALOW:IN_ALOW + 128, :], preferred_element_type=F32)
        dh = dh + jnp.dot(dgab_ref[...], wt_ref[IN_GATES:D_IN, :], preferred_element_type=F32)
        xn, r = _rms_fwd(x_ref[...])
        dg_ref[...] += jnp.sum(dh * xn, axis=0, keepdims=True)
        dx_ref[...] = dx1_ref[...] + _rms_bwd(dh, xn, r, g_ref[...])

    return _tiled_call(
        body, [_tok(PM_W), _tok(GAB_W), _tok(D_MODEL), _tok(D_MODEL), _once((1, D_MODEL)), _once((D_IN, D_MODEL))],
        [_tok(D_MODEL), pl.BlockSpec((1, D_MODEL), lambda i: (0, 0))],
        [jax.ShapeDtypeStruct((SEQ, D_MODEL), F32), jax.ShapeDtypeStruct((1, D_MODEL), F32)],
        (dpm, dgab, x, dx1, g_pre, wt_in), "in_proj_bwd", rider, rider_ins)


def _dw_in(dpm, dgab, h):
    n_pm = PM_AL // 512
    n_blk = n_pm + GAB_W // 512

    def body(dpm_ref, dgab_ref, dal_ref, h_ref, o_ref):
        i = pl.program_id(0)
        off = pl.multiple_of(i * 512 + 16 * (i >= 3).astype(jnp.int32), 16)

        @pl.when(i < n_pm)
        def _():
            o_ref[pl.ds(off, 512), :] = _tn(dpm_ref[...], h_ref[...]).astype(BF16)

        @pl.when(i >= n_pm)
        def _():
            o_ref[pl.ds(off, 512), :] = _tn(dgab_ref[...], h_ref[...]).astype(BF16)

        @pl.when(i == 0)
        def _():
            o_ref[IN_ALOW:IN_XM, :] = _tn(dal_ref[...], h_ref[...])[0:IN_XM - IN_ALOW].astype(BF16)

    return pl.pallas_call(
        body, grid=(n_blk,),
        in_specs=[pl.BlockSpec((SEQ, 512), lambda i: (0, jnp.minimum(i, n_pm - 1))),
                  pl.BlockSpec((SEQ, 512), lambda i: (0, jnp.maximum(i - n_pm, 0))),
                  pl.BlockSpec((SEQ, 128), lambda i: (0, PM_AL // 128)),
                  _once((SEQ, D_MODEL))],
        out_specs=pl.BlockSpec((D_IN, D_MODEL), lambda i: (0, 0)),
        out_shape=jax.ShapeDtypeStruct((D_IN, D_MODEL), BF16),
        compiler_params=_params(("arbitrary",)), name="dw_in",
    )(dpm, dgab, dpm, h)


def _tn_matmul(a, b, name, shards=1, tm=1024, rider=None, rider_ins=()):
    m, n = a.shape[1], b.shape[1]
    tm = min(tm, m)
    tn = n // shards if shards > 1 else min(n, 1024)
    steps_i, steps_j = m // tm, n // tn
    r_in, r_out_specs, r_out_shape, r_scratch = _rider_specs(rider, rider_ins)

    def body(*refs):
        (a_ref, b_ref), ride_in, (o_ref,), ride_out, scratch = _split(refs, 2, len(r_in), 1, len(r_out_specs), len(r_scratch))
        step = pl.program_id(0) * steps_j + pl.program_id(1)
        _ride(rider, ("first",), step == 0, ride_in, ride_out, scratch)
        o_ref[...] = _tn(a_ref[...], b_ref[...]).astype(BF16)
        _ride(rider, ("middle", "last"), step == steps_i * steps_j - 1, ride_in, ride_out, scratch)

    if shards > 1:
        out_spec = pl.BlockSpec((None, tm, tn), lambda i, j: (j, i, 0))
        out_shape = jax.ShapeDtypeStruct((shards, m, tn), BF16)
    else:
        out_spec = pl.BlockSpec((tm, tn), lambda i, j: (i, j))
        out_shape = jax.ShapeDtypeStruct((m, n), BF16)
    res = pl.pallas_call(
        body, grid=(steps_i, steps_j),
        in_specs=[pl.BlockSpec((SEQ, tm), lambda i, j: (0, i)), pl.BlockSpec((SEQ, tn), lambda i, j: (0, j))] + r_in,
        out_specs=[out_spec] + r_out_specs, out_shape=[out_shape] + r_out_shape, scratch_shapes=r_scratch,
        compiler_params=_params(("arbitrary", "arbitrary")), name=name,
    )(a, b, *rider_ins)
    return res[0] if rider is None else (res[0], res[1:])


MESH = pl.DeviceIdType.MESH
ANY = pl.BlockSpec(memory_space=pl.ANY)
VMEM_WHOLE = pl.BlockSpec(memory_space=pltpu.VMEM)

_BIG = ("w_in", "w_pa", "w_pb", "w_o", "w_up", "w_down")
_BIG_SHARD = {"w_in": (IN_SHARD, D_MODEL), "w_pa": (512, 256), "w_pb": (512, 256), "w_o": (256, D_MODEL),
              "w_up": (D_MODEL, 1024), "w_down": (1024, D_MODEL),
              "w_down_a": (512, D_MODEL), "w_down_b": (512, D_MODEL)}
_BIG_SPLIT = {"w_in": 1, "w_pa": 0, "w_pb": 0, "w_o": 0, "w_up": 0, "w_down": 0, "w_down_a": 0, "w_down_b": 0}


def _half(ref, e, name, lead=0, part=None):
    axis = _BIG_SPLIT[name]
    size = _BIG_SHARD[name][axis] // 2
    start = e * size
    if part is not None:
        size //= 2
        start = start + part * size
    start = pl.multiple_of(start, 128 if axis == 1 else 16)
    idx = [pl.ds(0, ref.shape[a]) for a in range(lead)]
    idx += [pl.ds(start, size), pl.ds(0, _BIG_SHARD[name][1])] if axis == 0 else [pl.ds(0, _BIG_SHARD[name][0]), pl.ds(start, size)]
    return ref.at[tuple(idx)]


def _half_shape(name):
    r, c = _BIG_SHARD[name]
    return (r // 2, c) if _BIG_SPLIT[name] == 0 else (r, c // 2)


def _remote(src, dst, send_sems, recv_sems, k, to):
    return pltpu.make_async_remote_copy(src_ref=src, dst_ref=dst, send_sem=send_sems.at[k], recv_sem=recv_sems.at[k],
                                        device_id=to, device_id_type=MESH)


def _mesh_place():
    x, y, c = lax.axis_index("x"), lax.axis_index("y"), lax.axis_index("c")
    return x, y, c, [(1 - x, y), (x, 1 - y), (1 - x, 1 - y)]


class _Gather:
    def __init__(self, names, small=(), middle_at=0.5):
        self.middle_at = middle_at
        self.names = tuple(names)
        self.nb = len(self.names)
        self.n = self.nb + len(small)
        self.n_sems = 8 * self.nb + 3 * len(small)
        self.n_flush = 6 * self.nb + len(small)
        self.out_shape = [jax.ShapeDtypeStruct((N_CHIP,) + _BIG_SHARD[nm], BF16) for nm in self.names]
        self.out_shape += [jax.ShapeDtypeStruct((N_CHIP,) + s.shape, s.dtype) for s in small]

    def _copies(self, ins, outs, ss, rs, k):
        x, y, c, _ = _mesh_place()
        name = self.names[k]
        me, xn, yn, dg = 2 * x + y, 2 * (1 - x) + y, 2 * x + (1 - y), 2 * (1 - x) + (1 - y)
        to_x, to_y, sibling = (1 - x, y, c), (x, 1 - y, c), (x, y, 1 - c)

        def region(slot, e, part=None):
            return _half(outs[k].at[slot], e, name, part=part)

        def copy(pair, src, dst, to):
            return _remote(src, dst, ss, rs, 8 * k + pair, to)

        mine = _half(ins[k], c, name)
        sent = [copy(0, mine, region(me, c), to_x), copy(1, mine, region(me, c), to_y),
                copy(2, region(xn, c, 0), region(xn, c, 0), to_y), copy(3, region(yn, c, 1), region(yn, c, 1), to_x),
                copy(4, region(xn, c), region(xn, c), sibling), copy(5, region(yn, c), region(yn, c), sibling),
                copy(6, region(dg, c, 0), region(dg, c, 0), sibling), copy(7, region(dg, c, 1), region(dg, c, 1), sibling)]
        landing = [region(xn, c), region(yn, c), region(dg, c, 0), region(dg, c, 1),
                   region(xn, 1 - c), region(yn, 1 - c), region(dg, 1 - c, 0), region(dg, 1 - c, 1)]
        received = [copy(pair, dst, dst, sibling) for pair, dst in enumerate(landing)]
        return sent, received

    def _small(self, ins, outs, ss, rs, k, j, peer, slot, c):
        return _remote(ins[k], outs[k].at[slot], ss, rs, 8 * self.nb + 3 * (k - self.nb) + j, (*peer, c))

    def flush(self, phase, lands, outs, fs):
        x, y, c, _ = _mesh_place()
        me, xn, yn, dg = 2 * x + y, 2 * (1 - x) + y, 2 * x + (1 - y), 2 * (1 - x) + (1 - y)

        def pieces(k):
            name = self.names[k]
            spots = [lambda r: r.at[me], lambda r: _half(r.at[xn], c, name), lambda r: _half(r.at[yn], c, name),
                     lambda r: _half(r.at[xn], 1 - c, name), lambda r: _half(r.at[yn], 1 - c, name), lambda r: r.at[dg]]
            return [pltpu.make_async_copy(spot(lands[k]), spot(outs[k]), fs.at[6 * k + t]) for t, spot in enumerate(spots)]

        ready = {"first": (0,), "middle": (1, 2), "last": (3, 4, 5)}[phase]
        for k in range(self.nb):
            cps = pieces(k)
            for t in ready:
                cps[t].start()
        if phase == "last":
            small = [pltpu.make_async_copy(lands[k], outs[k], fs.at[6 * self.nb + k - self.nb]) for k in range(self.nb, self.n)]
            for cp in small:
                cp.start()
            for k in range(self.nb):
                for cp in pieces(k):
                    cp.wait()
            for cp in small:
                cp.wait()

    def first(self, ins, outs, ss, rs):
        x, y, c, peers = _mesh_place()
        me = 2 * x + y
        for k in range(self.nb):
            sent, _ = self._copies(ins, outs, ss, rs, k)
            sent[0].start()
            sent[1].start()
        for k in range(self.nb, self.n):
            for j, peer in enumerate(peers):
                self._small(ins, outs, ss, rs, k, j, peer, me, c).start()
        for k in range(self.n):
            outs[k][me] = ins[k][...]

    def middle(self, ins, outs, ss, rs):
        for k in range(self.nb):
            sent, received = self._copies(ins, outs, ss, rs, k)
            for pair in (0, 1):
                received[pair].wait_recv()
                sent[2 + pair].start()
                sent[4 + pair].start()

    def last(self, ins, outs, ss, rs):
        x, y, c, peers = _mesh_place()
        for k in range(self.nb):
            sent, received = self._copies(ins, outs, ss, rs, k)
            for pair in (2, 3):
                received[pair].wait_recv()
                sent[4 + pair].start()
        for k in range(self.nb):
            sent, received = self._copies(ins, outs, ss, rs, k)
            for pair in range(4, 8):
                received[pair].wait_recv()
            for cp in sent:
                cp.wait_send()
        for k in range(self.nb, self.n):
            for j, (px, py) in enumerate(peers):
                self._small(ins, outs, ss, rs, k, j, (px, py), 2 * px + py, c).wait_recv()
                self._small(ins, outs, ss, rs, k, j, (px, py), 2 * x + y, c).wait_send()


def _run_alone(rider, ins, name):
    def body(*refs):
        r_in, r_out, sems = _split(refs, len(ins), len(rider.out_shape), 2)
        rider.first(r_in, r_out, *sems)
        rider.middle(r_in, r_out, *sems)
        rider.last(r_in, r_out, *sems)

    return pl.pallas_call(
        body, in_specs=[VMEM_WHOLE] * len(ins), out_specs=[VMEM_WHOLE] * len(rider.out_shape), out_shape=rider.out_shape,
        scratch_shapes=[pltpu.SemaphoreType.DMA((rider.n_sems,)), pltpu.SemaphoreType.DMA((rider.n_sems,))],
        compiler_params=_params(), name=name,
    )(*ins)


class _Presum:
    in_space = ANY

    def __init__(self, names, base=0):
        self.names = tuple(names)
        self.n = len(self.names)
        self.base = base
        self.n_sems = 3 * self.n
        self.out_shape = [jax.ShapeDtypeStruct((N_CHIP,) + _half_shape(nm), BF16) for nm in self.names]
        self.work_shape = self.out_shape + self.out_shape

    def _stage(self, ins, bufs, ss, k, e, which):
        n = self.n
        return pltpu.make_async_copy(_half(ins[k], e, self.names[k], lead=1), bufs[which * n + k],
                                     ss.at[self.base + which * n + k])

    def _give(self, bufs, ss, rs, k, sibling):
        return _remote(bufs[self.n + k], bufs[k], ss, rs, self.base + k, sibling)

    def first(self, ins, bufs, ss, rs):
        x, y, c, _ = _mesh_place()
        for k in range(self.n):
            self._stage(ins, bufs, ss, k, 1 - c, 1).start()
        for k in range(self.n):
            self._stage(ins, bufs, ss, k, c, 2).start()
        for k in range(self.n):
            self._stage(ins, bufs, ss, k, 1 - c, 1).wait()
            self._give(bufs, ss, rs, k, (x, y, 1 - c)).start()

    def middle(self, ins, bufs, ss, rs):
        pass

    def last(self, ins, bufs, ss, rs):
        x, y, c, _ = _mesh_place()
        for k in range(self.n):
            self._give(bufs, ss, rs, k, (x, y, 1 - c)).wait_recv()
            self._stage(ins, bufs, ss, k, c, 2).wait()

            @pl.loop(0, N_CHIP)
            def _(j):
                bufs[k][j] = (bufs[k][j].astype(F32) + bufs[2 * self.n + k][j].astype(F32)).astype(BF16)
        for k in range(self.n):
            self._give(bufs, ss, rs, k, (x, y, 1 - c)).wait_send()


def _presum(names, grads, name):
    rider = _Presum(names)
    n = rider.n

    def body(*refs):
        g_refs, got_refs, work_refs, sems = _split(refs, n, n, 2 * n, 2)
        bufs = list(got_refs) + list(work_refs)
        rider.first(g_refs, bufs, *sems)
        rider.last(g_refs, bufs, *sems)

    return pl.pallas_call(
        body, in_specs=[ANY] * n, out_specs=[VMEM_WHOLE] * n, out_shape=rider.out_shape,
        scratch_shapes=[pltpu.VMEM(s.shape, s.dtype) for s in rider.work_shape]
        + [pltpu.SemaphoreType.DMA((rider.n_sems,)), pltpu.SemaphoreType.DMA((rider.n_sems,))],
        compiler_params=_params(), name=name,
    )(*grads)


class _ReduceRelay:
    middle_at = 0.75

    def __init__(self, names, base=0):
        self.names = tuple(names)
        self.n = len(self.names)
        self.base = base
        self.n_sems = 6 * self.n
        self.out_shape = [jax.ShapeDtypeStruct((N_CHIP,) + _half_shape(nm), BF16) for nm in self.names]
        quarter = [jax.ShapeDtypeStruct(self._part_shape(nm), BF16) for nm in self.names]
        self.work_shape = quarter + quarter

    @staticmethod
    def _part_shape(name):
        r, c = _half_shape(name)
        return (r // 2, c) if _BIG_SPLIT[name] == 0 else (r, c // 2)

    def _part(self, ref, name, p):
        r, c = self._part_shape(name)
        return ref.at[pl.ds(p * r, r), pl.ds(0, c)] if _BIG_SPLIT[name] == 0 else ref.at[pl.ds(0, r), pl.ds(p * c, c)]

    def _copies(self, ins, bufs, ss, rs, k):
        x, y, c, _ = _mesh_place()
        name, n = self.names[k], self.n
        me, xn, yn, dg = 2 * x + y, 2 * (1 - x) + y, 2 * x + (1 - y), 2 * (1 - x) + (1 - y)
        to_x, to_y = (1 - x, y, c), (x, 1 - y, c)
        mine = lambda slot, p: self._part(ins[k].at[slot], name, p)
        slot = lambda s, p: self._part(bufs[k].at[s], name, p)
        from_x, from_y = bufs[n + k], bufs[2 * n + k]

        def copy(pair, src, dst, to):
            return _remote(src, dst, ss, rs, self.base + 6 * k + pair, to)

        sent = [copy(0, mine(dg, 0), from_x, to_x), copy(1, mine(dg, 1), from_y, to_y),
                copy(2, mine(xn, 0), slot(me, 0), to_x), copy(3, mine(yn, 1), slot(me, 1), to_y),
                copy(4, from_y, slot(me, 1), to_x), copy(5, from_x, slot(me, 0), to_y)]
        landing = [from_x, from_y, slot(xn, 0), slot(yn, 1), slot(xn, 1), slot(yn, 0)]
        received = [copy(pair, dst, dst, to_x) for pair, dst in enumerate(landing)]
        return sent, received

    def first(self, ins, bufs, ss, rs):
        x, y, c, _ = _mesh_place()
        me, dg = 2 * x + y, 2 * (1 - x) + (1 - y)
        for k in range(self.n):
            sent, _ = self._copies(ins, bufs, ss, rs, k)
            for pair in range(4):
                sent[pair].start()
        for k in range(self.n):
            bufs[k][me] = ins[k][me]
            bufs[k][dg] = jnp.zeros(_half_shape(self.names[k]), BF16)

    def middle(self, ins, bufs, ss, rs):
        x, y, c, _ = _mesh_place()
        xn, yn = 2 * (1 - x) + y, 2 * x + (1 - y)
        for k in range(self.n):
            sent, received = self._copies(ins, bufs, ss, rs, k)
            name, n = self.names[k], self.n
            for pair, buf, own in ((0, bufs[n + k], self._part(ins[k].at[yn], name, 0)),
                                   (1, bufs[2 * n + k], self._part(ins[k].at[xn], name, 1))):
                received[pair].wait_recv()
                buf[...] = (buf[...].astype(F32) + own[...].astype(F32)).astype(BF16)
            sent[5].start()
            sent[4].start()

    def last(self, ins, bufs, ss, rs):
        for k in range(self.n):
            sent, received = self._copies(ins, bufs, ss, rs, k)
            for pair in range(2, 6):
                received[pair].wait_recv()
            for cp in sent:
                cp.wait_send()


class _PresumThenRelay:
    in_space = ANY
    middle_at = _ReduceRelay.middle_at

    def __init__(self, names):
        self.relay = _ReduceRelay(names)
        self.pre = _Presum(names, base=self.relay.n_sems)
        self.n_sems = self.relay.n_sems + self.pre.n_sems
        self.out_shape = self.relay.out_shape
        self.work_shape = list(self.relay.work_shape) + list(self.pre.out_shape) + list(self.pre.work_shape)
        self.n_relay = len(self.relay.out_shape) + len(self.relay.work_shape)

    def first(self, ins, bufs, ss, rs):
        self.pre.first(ins, bufs[self.n_relay:], ss, rs)

    def early(self, ins, bufs, ss, rs):
        self.pre.last(ins, bufs[self.n_relay:], ss, rs)
        self.relay.first(bufs[self.n_relay:], bufs[:self.n_relay], ss, rs)

    def middle(self, ins, bufs, ss, rs):
        self.relay.middle(bufs[self.n_relay:], bufs[:self.n_relay], ss, rs)

    def last(self, ins, bufs, ss, rs):
        self.relay.last(bufs[self.n_relay:], bufs[:self.n_relay], ss, rs)


class _SendPartials:
    def __init__(self, names, small_shape=None):
        self.n = len(names)
        self.small = small_shape is not None
        self.n_sems = 3 * self.n + 7
        self.out_shape = [jax.ShapeDtypeStruct((N_CHIP,) + _half_shape(nm), BF16) for nm in names]
        if self.small:
            self.out_shape.append(jax.ShapeDtypeStruct((N_DEV,) + small_shape, F32))

    def _piece(self, ins, outs, ss, rs, k, j, peer, src_slot, dst_slot, c):
        return _remote(ins[k].at[src_slot], outs[k].at[dst_slot], ss, rs, 3 * k + j, (*peer, c))

    def _small(self, ins, outs, ss, rs, r, other, slot):
        return _remote(ins[self.n], outs[self.n].at[slot], ss, rs, 3 * self.n + r, other)

    @staticmethod
    def _others(x, y, c):
        return [(x, y, 1 - c), (1 - x, y, c), (1 - x, y, 1 - c), (x, 1 - y, c), (x, 1 - y, 1 - c),
                (1 - x, 1 - y, c), (1 - x, 1 - y, 1 - c)]

    def first(self, ins, outs, ss, rs):
        x, y, c, peers = _mesh_place()
        me = 2 * x + y
        for k in range(self.n):
            for j, (px, py) in enumerate(peers):
                self._piece(ins, outs, ss, rs, k, j, (px, py), 2 * px + py, me, c).start()
        if self.small:
            for r, other in enumerate(self._others(x, y, c)):
                self._small(ins, outs, ss, rs, r, other, 4 * x + 2 * y + c).start()
            outs[self.n][4 * x + 2 * y + c] = ins[self.n][...]
        for k in range(self.n):
            outs[k][me] = ins[k][me]

    def middle(self, ins, outs, ss, rs):
        pass

    def last(self, ins, outs, ss, rs):
        x, y, c, peers = _mesh_place()
        me = 2 * x + y
        for k in range(self.n):
            for j, (px, py) in enumerate(peers):
                self._piece(ins, outs, ss, rs, k, j, (px, py), me, 2 * px + py, c).wait_recv()
                self._piece(ins, outs, ss, rs, k, j, (px, py), 2 * px + py, me, c).wait_send()
        if self.small:
            for r, (px, py, pc) in enumerate(self._others(x, y, c)):
                self._small(ins, outs, ss, rs, r, (px, py, pc), 4 * px + 2 * py + pc).wait_recv()
                self._small(ins, outs, ss, rs, r, (px, py, pc), 4 * x + 2 * y + c).wait_send()


def _sum_swap(names, parts, small):
    n = len(parts)
    everyone = _SendPartials((), small.shape)

    def body(*refs):
        p_refs, (small_ref,), o_refs, (osmall_ref,), (all_ref,), (send_sems, recv_sems, ss_small, rs_small) = _split(
            refs, n, 1, n, 1, 1, 4)
        x, y, c = lax.axis_index("x"), lax.axis_index("y"), lax.axis_index("c")
        everyone.first([small_ref], [all_ref], ss_small, rs_small)

        def mine(k):
            part = _half(o_refs[k], c, names[k])
            return _remote(part, part, send_sems, recv_sems, k, (x, y, 1 - c))

        for k in range(n):
            for e in range(2):
                @pl.when(c == e)
                def _():
                    g = p_refs[k][0].astype(F32)
                    for s in range(1, N_CHIP):
                        g = g + p_refs[k][s].astype(F32)
                    r, cols = _half_shape(names[k])
                    if _BIG_SPLIT[names[k]] == 0:
                        o_refs[k][e * r:(e + 1) * r, :] = g
                    else:
                        o_refs[k][:, e * cols:(e + 1) * cols] = g
            mine(k).start()
        for k in range(n):
            theirs = _half(o_refs[k], 1 - c, names[k])
            _remote(theirs, theirs, send_sems, recv_sems, k, (x, y, 1 - c)).wait_recv()
            mine(k).wait_send()
        everyone.last([small_ref], [all_ref], ss_small, rs_small)
        g = all_ref[0]
        for d in range(1, N_DEV):
            g = g + all_ref[d]
        osmall_ref[...] = g

    res = pl.pallas_call(
        body, in_specs=[VMEM_WHOLE] * (n + 1), out_specs=[VMEM_WHOLE] * (n + 1),
        out_shape=[jax.ShapeDtypeStruct(_BIG_SHARD[nm], F32) for nm in names] + [jax.ShapeDtypeStruct(small.shape, F32)],
        scratch_shapes=[pltpu.VMEM((N_DEV,) + small.shape, F32), pltpu.SemaphoreType.DMA((n,)), pltpu.SemaphoreType.DMA((n,)),
                        pltpu.SemaphoreType.DMA((everyone.n_sems,)), pltpu.SemaphoreType.DMA((everyone.n_sems,))],
        compiler_params=_params(), name="sum_swap",
    )(*parts, small)
    return res[:n], res[n]


def _tile(rows, cols, itemsize, budget):
    t = cols if rows % 16 else rows
    other = rows if rows % 16 else cols
    step = 256 if rows % 16 else 32
    while t % step == 0 and t * other * itemsize > budget:
        t //= 2
    return (rows, t) if rows % 16 else (t, cols)


def _adamw_math(w, g, m, v):
    m = ADAM_B1 * m + (1.0 - ADAM_B1) * g
    v = ADAM_B2 * v + (1.0 - ADAM_B2) * (g * g)
    m_hat = m / (1.0 - ADAM_B1 ** ADAM_STEP)
    v_hat = v / (1.0 - ADAM_B2 ** ADAM_STEP)
    delta = -ADAM_LR * (m_hat / (jnp.sqrt(v_hat) + ADAM_EPS) + ADAM_WD * w)
    return delta, m, v


def _adamw_big(g, w, m, v, name):
    r, c = w.shape
    tr, tc = _tile(r, c, 4, 2 * 1024 * 1024)

    def body(g_ref, w_ref, m_ref, v_ref, d_ref, nm_ref, nv_ref):
        d_ref[...], nm_ref[...], nv_ref[...] = _adamw_math(w_ref[...], g_ref[...], m_ref[...], v_ref[...])

    blk = pl.BlockSpec((tr, tc), lambda i, l: (i, l))
    return pl.pallas_call(
        body, grid=(r // tr, c // tc), in_specs=[blk, blk, blk, blk],
        out_specs=[blk, blk, blk], out_shape=[jax.ShapeDtypeStruct((r, c), F32)] * 3,
        compiler_params=_params(("arbitrary", "arbitrary")), name=name,
    )(g, w, m, v)


def _adamw_rows(g, w, m, v, name):
    r, k, lanes = w.shape
    tr = 296

    def body(g_ref, w_ref, m_ref, v_ref, g3_ref, d_ref, nm_ref, nv_ref):
        g = g_ref[...].reshape(tr, k, lanes)
        g3_ref[...] = g
        d_ref[...], nm_ref[...], nv_ref[...] = _adamw_math(w_ref[...], g, m_ref[...], v_ref[...])

    rows = pl.BlockSpec((tr, k, lanes), lambda i: (i, 0, 0))
    return pl.pallas_call(
        body, grid=(pl.cdiv(r, tr),), in_specs=[pl.BlockSpec((tr, k * lanes), lambda i: (i, 0)), rows, rows, rows],
        out_specs=[rows] * 4, out_shape=[jax.ShapeDtypeStruct((r, k, lanes), F32)] * 4,
        compiler_params=_params(("arbitrary",)), name=name,
    )(g, w, m, v)


def _adamw_small(ws, gs, ms, vs):
    n = len(ws)

    def body(*refs):
        w_refs, g_refs, m_refs, v_refs, d_refs, nm_refs, nv_refs = _split(refs, *([n] * 7))
        for k in range(n):
            d_refs[k][...], nm_refs[k][...], nv_refs[k][...] = _adamw_math(w_refs[k][...], g_refs[k][...], m_refs[k][...],
                                                                             v_refs[k][...])

    shapes = [jax.ShapeDtypeStruct(w.shape, F32) for w in ws]
    res = pl.pallas_call(body, out_shape=shapes * 3, name="adamw_small")(*ws, *gs, *ms, *vs)
    return res[:n], res[n:2 * n], res[2 * n:]


def _pack(arrs):
    flat = jnp.concatenate([a.reshape(-1) for a in arrs])
    rows = -(-flat.shape[0] // 1024) * 8
    return jnp.pad(flat, (0, rows * 128 - flat.shape[0])).reshape(rows, 128)


def _unpack(buf, shapes):
    flat = buf.reshape(-1)
    out, off = [], 0
    for s in shapes:
        size = 1
        for d in s:
            size *= d
        out.append(flat[off:off + size].reshape(s))
        off += size
    return out


def _block_rows(w):
    return jnp.pad(w.reshape(512, 4), ((0, 0), (0, 124)))


def _cols(a4):
    return jnp.transpose(a4, (1, 0, 2)).reshape(a4.shape[1], -1)


_LATE = ("w_pa", "w_pb", "w_o", "w_up", "w_down")
_RIDE_IN_PROJ = ("w_pa", "w_pb", "w_o")
_RIDE_MIXER = ("w_up", "w_down_a")
_RIDE_MERGE = ("w_down_b",)


def _full_weights(gathered):
    joined = {"w_in": (D_IN, D_MODEL), "w_o": (D_MODEL, D_MODEL)}
    return {n: (a.reshape(joined[n]) if n in joined else a) for n, a in gathered.items()}


def _local_step(x, target, w, sp, late_shards=None):
    sp = {n: (a.reshape(1, -1) if a.ndim == 1 else a) for n, a in sp.items()}
    wau = jnp.zeros((128, 256), F32).at[0:16].set(sp["w_a_up"])
    wif = jnp.zeros((1536, 128), F32).at[:, 0:8].set(sp["w_if"])
    bif = jnp.zeros((1, 128), F32).at[:, 0:8].set(sp["b_if"])
    p = {"wau": wau, "bau": sp["b_a_up"], "ggla": sp["g_gla_norm"], "cw": sp["conv_w"], "cb": sp["conv_b"],
         "wq": _block_rows(sp["w_q_ml"]), "wk": _block_rows(sp["w_k_ml"]), "wv": _block_rows(sp["w_v_ml"]),
         "wif": wif, "bif": bif, "skip": sp["ml_skip"], "gml": sp["g_ml_norm"]}

    if late_shards is None:
        (pm, gab, h), _ = _in_proj(x, sp["g_pre_mix"], w["w_in"])
        ab, *states = _mixer_fwd(pm, p)
        (x1, mix, merged), _ = _merge_fwd(ab, gab, x, w["w_pa"], w["w_pb"], w["w_o"], sp["g_post_mix"])
    else:
        shard = dict(zip(_LATE, late_shards))
        shard["w_down_a"], shard["w_down_b"] = shard["w_down"][0:512], shard["w_down"][512:1024]
        (pm, gab, h), got = _in_proj(x, sp["g_pre_mix"], w["w_in"], _Gather(_RIDE_IN_PROJ, middle_at=0.45),
                                     [shard[n] for n in _RIDE_IN_PROJ])
        w = dict(w, **_full_weights(dict(zip(_RIDE_IN_PROJ, got))))
        ab, *rest = _mixer_fwd(pm, p, _Gather(_RIDE_MIXER, middle_at=0.62), [shard[n] for n in _RIDE_MIXER])
        states = rest[:4]
        w.update(_full_weights(dict(zip(_RIDE_MIXER, rest[4:]))))
        (x1, mix, merged), got = _merge_fwd(ab, gab, x, w["w_pa"], w["w_pb"], w["w_o"], sp["g_post_mix"],
                                            _Gather(_RIDE_MERGE, middle_at=0.46), [shard[n] for n in _RIDE_MERGE])
        w.update(_full_weights(dict(zip(_RIDE_MERGE, got))))
    dx1, u, dd, h2, dpre, dg_post_mlp, dg_pre_mlp, loss = _mlp(x1, target, sp["g_pre_mlp"], sp["g_post_mlp"],
                                                                w["w_up"], w["w_down_a"], w["w_down_b"])
    dmix, dya, dyb, dgab, dab, dg_post_mix = _merge_bwd(dx1, mix, ab, gab, w["w_pa"], w["w_pb"], w["w_o"], sp["g_post_mix"])
    big = {
        "w_pa": _tn_matmul(ab[:, 0:512], dya, "dw_pa", shards=N_CHIP),
        "w_pb": _tn_matmul(ab[:, 512:1024], dyb, "dw_pb", shards=N_CHIP),
        "w_o": _tn_matmul(merged, dmix, "dw_o"),
        "w_up": _tn_matmul(h2, dpre, "dw_up", shards=N_CHIP),
    }
    if late_shards is None:
        big["w_down"] = _tn_matmul(u, dd, "dw_down")
        dpm, dp, _ = _mixer_bwd(pm, dab, states, p)
    else:
        pieces = lambda n: big[n].reshape((N_CHIP,) + _BIG_SHARD[n])
        big["w_down"], partial = _tn_matmul(u, dd, "dw_down", rider=_Presum(_LATE[:4]),
                                            rider_ins=[pieces(n) for n in _LATE[:4]])
        partial = list(partial) + list(_presum(("w_down",), [pieces("w_down")], "presum_w_down"))
        dpm, dp, parts = _mixer_bwd(pm, dab, states, p, _SendPartials(_LATE), partial)
        big = dict(zip(_LATE, parts))
    big["w_in"] = _dw_in(dpm, dgab, h)
    if late_shards is None:
        (dx, dg_pre_mix), _ = _in_proj_bwd(dpm, dgab, x, dx1, sp["g_pre_mix"], w["w_in"])
    else:
        (dx, dg_pre_mix), parts = _in_proj_bwd(dpm, dgab, x, dx1, sp["g_pre_mix"], w["w_in"], _PresumThenRelay(("w_in",)),
                                               [big["w_in"].reshape((N_CHIP,) + _BIG_SHARD["w_in"])])
        big["w_in"] = parts[0]
    small = {
        "g_pre_mix": dg_pre_mix, "b_a_up": dp["bau"], "g_gla_norm": dp["ggla"], "conv_b": dp["cb"],
        "w_q_ml": dp["wq"][:, 0:4].reshape(128, 4, 4), "w_k_ml": dp["wk"][:, 0:4].reshape(128, 4, 4),
        "w_v_ml": dp["wv"][:, 0:4].reshape(128, 4, 4),
        "b_if": dp["bif"][:, 0:8], "ml_skip": dp["skip"], "g_ml_norm": dp["gml"], "g_post_mix": dg_post_mix,
        "g_pre_mlp": dg_pre_mlp, "g_post_mlp": dg_post_mlp, "w_a_up": dp["wau"][0:16], "conv_w": dp["cw"],
        "w_if": dp["wif"][:, 0:8], "loss": loss[:, 0:1],
    }
    return dx, big, small


_SMALL_REPL = ("g_pre_mix", "b_a_up", "g_gla_norm", "conv_b", "w_q_ml", "w_k_ml", "w_v_ml", "b_if", "ml_skip",
               "g_ml_norm", "g_post_mix", "g_pre_mlp", "g_post_mlp")
_SMALL_SHARDED = ("w_a_up", "conv_w", "w_if")
_SMALL_ORDER = _SMALL_REPL + _SMALL_SHARDED + ("loss",)
_WEIGHTS = ("g_pre_mix", "w_in", "w_a_up", "b_a_up", "g_gla_norm", "conv_w", "conv_b", "w_q_ml", "w_k_ml", "w_v_ml",
            "w_if", "b_if", "ml_skip", "g_ml_norm", "w_pa", "w_pb", "w_o", "g_post_mix", "g_pre_mlp", "w_up", "w_down",
            "g_post_mlp")


_BLOCK_WEIGHTS = ("w_q_ml", "w_k_ml", "w_v_ml")


def _stored(name, a):
    if name in _BLOCK_WEIGHTS:
        return jnp.transpose(a, (0, 2, 3, 1)).reshape(16, 128)
    if name == "w_if":
        return jnp.transpose(a, (0, 2, 1)).reshape(8, 384)
    return a


def _unstored(name, a):
    if name in _BLOCK_WEIGHTS:
        return jnp.transpose(a.reshape(1, 4, 4, 128), (0, 3, 1, 2))
    if name == "w_if":
        return jnp.transpose(a.reshape(1, 8, 384), (0, 2, 1))
    return a


def _as_shard(name, a):
    return jnp.transpose(a, (2, 0, 1)).reshape(IN_SHARD, D_MODEL // 128, 128) if name == "w_in" else a[0]


def _from_shard(name, a):
    return jnp.transpose(a, (1, 2, 0)).reshape(1, D_MODEL, IN_SHARD) if name == "w_in" else a[None]


def kernel(x, g_pre_mix, w_in, w_a_up, b_a_up, g_gla_norm, conv_w, conv_b, w_q_ml, w_k_ml, w_v_ml, w_if, b_if, ml_skip, g_ml_norm, w_pa, w_pb, w_o, g_post_mix, g_pre_mlp, w_up, w_down, g_post_mlp, loss_target, m_g_pre_mix, m_w_in, m_w_a_up, m_b_a_up, m_g_gla_norm, m_conv_w, m_conv_b, m_w_q_ml, m_w_k_ml, m_w_v_ml, m_w_if, m_b_if, m_ml_skip, m_g_ml_norm, m_w_pa, m_w_pb, m_w_o, m_g_post_mix, m_g_pre_mlp, m_w_up, m_w_down, m_g_post_mlp, v_g_pre_mix, v_w_in, v_w_a_up, v_b_a_up, v_g_gla_norm, v_conv_w, v_conv_b, v_w_q_ml, v_w_k_ml, v_w_v_ml, v_w_if, v_b_if, v_ml_skip, v_g_ml_norm, v_w_pa, v_w_pb, v_w_o, v_g_post_mix, v_g_pre_mlp, v_w_up, v_w_down, v_g_post_mlp):
    args = dict(locals())
    wts = {n: _as_shard(n, args[n]) for n in _WEIGHTS}
    mom = {n: _as_shard(n, args["m_" + n]) for n in _WEIGHTS}
    var = {n: _as_shard(n, args["v_" + n]) for n in _WEIGHTS}
    chip = 2 * lax.axis_index("x") + lax.axis_index("y")

    first = ("w_in",) + _SMALL_SHARDED
    gathered = dict(zip(first, _run_alone(_Gather(("w_in",), [wts[n] for n in _SMALL_SHARDED]),
                                          [wts[n].reshape(IN_SHARD, D_MODEL).astype(BF16) if n == "w_in" else wts[n]
                                           for n in first],
                                          "gather_first")))
    sp = {n: wts[n] for n in _SMALL_REPL}
    sp["w_a_up"] = _cols(gathered["w_a_up"])
    sp["conv_w"] = _cols(gathered["conv_w"])
    sp["w_if"] = gathered["w_if"].reshape(1536, 8)

    dx, big, small = _local_step(x[0], loss_target[0], _full_weights({"w_in": gathered["w_in"]}), sp,
                                 late_shards=[wts[n].astype(BF16) for n in _LATE])

    small_shapes = [small[n].shape for n in _SMALL_ORDER]
    packed = _pack([small[n] for n in _SMALL_ORDER])
    sums, small_sum = _sum_swap(_BIG, [big[n] for n in _BIG], packed)

    grads, delta, new_m, new_v = {}, {}, {}, {}
    for n, g in zip(_BIG, sums):
        if n == "w_in":
            g, d, nm, nv = _adamw_rows(g, wts[n], mom[n], var[n], "adamw_" + n)
        else:
            d, nm, nv = _adamw_big(g, wts[n], mom[n], var[n], "adamw_" + n)
        grads[n], delta[n], new_m[n], new_v[n] = (_from_shard(n, a) for a in (g, d, nm, nv))
    summed = dict(zip(_SMALL_ORDER, _unpack(small_sum, small_shapes)))
    loss = summed["loss"].reshape(())
    summed["w_a_up"] = lax.dynamic_slice_in_dim(summed["w_a_up"], chip * 64, 64, axis=1)
    summed["conv_w"] = lax.dynamic_slice_in_dim(summed["conv_w"], chip * 128, 128, axis=1)
    summed["w_if"] = lax.dynamic_slice_in_dim(summed["w_if"], chip * 384, 384, axis=0)
    small_names = _SMALL_REPL + _SMALL_SHARDED
    g_stored = [_stored(n, summed[n].reshape(args[n].shape)) for n in small_names]
    upd = _adamw_small([_stored(n, args[n]) for n in small_names], g_stored,
                       [_stored(n, args["m_" + n]) for n in small_names], [_stored(n, args["v_" + n]) for n in small_names])
    for dst, arrs in zip((grads, delta, new_m, new_v), (g_stored,) + tuple(upd)):
        dst.update({n: _unstored(n, a) for n, a in zip(small_names, arrs)})

    outs = [loss, dx[None]]
    for group in (grads, delta, new_m, new_v):
        outs += [group[n] for n in _WEIGHTS]
    return tuple(outs)
```

```python
import functools

import jax
import jax.numpy as jnp
from jax import lax
from jax.experimental import pallas as pl
from jax.experimental.pallas import tpu as pltpu

F32 = jnp.float32
BF16 = jnp.bfloat16

SEQ = 2048
D_MODEL = 1024
CHUNK = 64
N_CHUNK = SEQ // CHUNK
HEADS = 4
GLA_DK = 64
GLA_DV = 128
ML_DH = 128
D_FF = 4096
EPS = 1e-6
N_CHIP = 4
N_DEV = 8
TOK_TILE = 256
N_TOK_TILE = SEQ // TOK_TILE
SWEEP = 2
assert CHUNK == 64
N_SWEEP = N_CHUNK // SWEEP

PM_W = 2688
PM_XM = 1536
PM_OP = 2048
PM_AL = 2560
GAB_W = 2048
D_IN = 4624
IN_SHARD = D_IN // N_CHIP
IN_ALOW = 1536
IN_XM = 1552
IN_GATES = 2576

ADAM_LR = 0.001
ADAM_B1 = 0.9
ADAM_B2 = 0.999
ADAM_EPS = 1e-08
ADAM_WD = 0.01
ADAM_STEP = 10

VMEM_LIMIT = 56 * 1024 * 1024


def _params(sem=None):
    return pltpu.CompilerParams(dimension_semantics=sem, vmem_limit_bytes=VMEM_LIMIT)


def _dot(a, b, ca, cb):
    return lax.dot_general(a.astype(BF16), b.astype(BF16), (((ca,), (cb,)), ((), ())), preferred_element_type=F32)


def _pmm_nn(a, b):
    return _dot(a, b, 1, 0)


def _pmm_nt(a, b):
    return _dot(a, b, 1, 1)


def _pmm_tn(a, b):
    return _dot(a, b, 0, 0)


def _pcmm(c, x):
    return lax.dot_general(c, x, (((1,), (0,)), ((), ())), precision=lax.Precision.HIGHEST, preferred_element_type=F32)


@jax.custom_vjp
def _mm_nn(a, b):
    return _dot(a, b, 1, 0)


@jax.custom_vjp
def _mm_nt(a, b):
    return _dot(a, b, 1, 1)


@jax.custom_vjp
def _mm_tn(a, b):
    return _dot(a, b, 0, 0)


_mm_nn.defvjp(lambda a, b: (_dot(a, b, 1, 0), (a, b)), lambda r, g: (_mm_nt(g, r[1]), _mm_tn(r[0], g)))
_mm_nt.defvjp(lambda a, b: (_dot(a, b, 1, 1), (a, b)), lambda r, g: (_mm_nn(g, r[1]), _mm_tn(g, r[0])))
_mm_tn.defvjp(lambda a, b: (_dot(a, b, 0, 0), (a, b)), lambda r, g: (_mm_nt(r[1], g), _mm_nn(r[0], g)))


@jax.custom_vjp
def _cmm(c, x):
    return _pcmm(c, x)


_cmm.defvjp(
    lambda c, x: (_pcmm(c, x), c),
    lambda c, g: (jnp.zeros_like(c), lax.dot_general(c, g, (((0,), (0,)), ((), ())), precision=lax.Precision.HIGHEST,
                                                      preferred_element_type=F32)),
)

_PLAIN_OPS = (_pmm_nn, _pmm_nt, _pmm_tn, _pcmm)
_VJP_OPS = (_mm_nn, _mm_nt, _mm_tn, _cmm)


def _sigmoid(x):
    return 0.5 * (jnp.tanh(0.5 * x) + 1.0)


def _log_sigmoid(x):
    return jnp.minimum(x, 0.0) - jnp.log(1.0 + jnp.exp(-jnp.abs(x)))


def _mean(x):
    return jnp.mean(x, axis=-1, keepdims=True)


def _nt(a, b):
    return lax.dot_general(a, b, (((1,), (1,)), ((), ())), preferred_element_type=F32)


def _tn(a, b):
    return lax.dot_general(a, b, (((0,), (0,)), ((), ())), preferred_element_type=F32)


def _mixer_chunk(ops, p, st, pm, xprev8):
    mm_nn, mm_nt, mm_tn, cmm = ops
    n_rows = pm.shape[0]
    n_ch = n_rows // CHUNK
    row = lax.broadcasted_iota(jnp.int32, (n_rows, n_rows), 0)
    col = lax.broadcasted_iota(jnp.int32, (n_rows, n_rows), 1)
    tri = jnp.logical_and((row >> 6) == (col >> 6), row >= col).astype(F32)
    causal = tri[0:CHUNK, 0:CHUNK] > 0.0
    q = pm[:, 0:256]
    k = pm[:, 256:512]
    v = pm[:, 512:1024]
    g = pm[:, 1024:1536]
    xm = pm[:, PM_XM:PM_XM + 512]
    opre = pm[:, PM_OP:PM_OP + 512]
    alow = pm[:, PM_AL:PM_AL + 128]
    hs = range(HEADS)
    cs = range(n_ch)
    pairs = [(i, h) for i in cs for h in hs]
    rs = [slice(i * CHUNK, (i + 1) * CHUNK) for i in cs]
    last = [slice((i + 1) * CHUNK - 1, (i + 1) * CHUNK) for i in cs]
    s6 = [slice(h * GLA_DK, (h + 1) * GLA_DK) for h in hs]
    s12 = [slice(h * 128, (h + 1) * 128) for h in hs]

    xx = jnp.concatenate([xprev8, xm], axis=0)
    pre = p["cb"]
    for j in range(4):
        pre = pre + p["cw"][j:j + 1, :] * xx[5 + j:5 + j + n_rows, :]
    xc = pre * _sigmoid(pre)
    qm = [mm_nn(xc[:, s12[h]], p["wq"][h]) for h in hs]
    km = [mm_nn(xc[:, s12[h]], p["wk"][h]) for h in hs]
    vm = [mm_nn(xm[:, s12[h]], p["wv"][h]) for h in hs]
    qcat = jnp.concatenate(qm, axis=1)
    kcat = jnp.concatenate(km, axis=1)
    vcat = jnp.concatenate(vm, axis=1)
    gates = (mm_nn(qcat, p["wif"][0:512]) + mm_nn(kcat, p["wif"][512:1024]) + mm_nn(vcat, p["wif"][1024:1536])
             + p["bif"])
    lf = _log_sigmoid(gates)
    fc = cmm(tri, lf)
    gates_t = gates.T
    fc_t = fc.T

    la = _log_sigmoid(mm_nn(alow, p["wau"]) + p["bau"]) * (1.0 / 16.0)
    cum = cmm(tri, la)
    cum_last = [cum[last[i], :] for i in cs]
    to_end = jnp.concatenate([cum_last[i] - cum[rs[i], :] for i in cs], axis=0)
    e_pos = jnp.exp(cum)
    e_neg = jnp.exp(-cum)
    qs = q * (GLA_DK ** -0.5)
    qp = qs * e_pos
    qn = qs * e_neg
    kp = k * e_pos
    kn = k * e_neg
    kl = k * jnp.exp(to_end)
    dec = [jnp.exp(cum_last[i]) for i in cs]
    ks = [km[h] * (ML_DH ** -0.5) for h in hs]
    li_c = {(i, h): gates[rs[i], h:h + 1] for i, h in pairs}
    fc_c = {(i, h): fc[rs[i], 4 + h:5 + h] for i, h in pairs}
    f_last = {(i, h): fc[last[i], 4 + h:5 + h] for i, h in pairs}

    a_fwd = {(i, h): mm_nt(qp[rs[i], s6[h]], kn[rs[i], s6[h]]) for i, h in pairs}
    a_bwd = {(i, h): mm_nt(qn[rs[i], s6[h]], kp[rs[i], s6[h]]) for i, h in pairs}
    s_chunk = {(i, h): mm_tn(v[rs[i], s12[h]], kl[rs[i], s6[h]]) for i, h in pairs}
    qk = {(i, h): mm_nt(qm[h][rs[i]], ks[h][rs[i]]) for i, h in pairs}
    a = {ih: f_last[ih] - fc_c[ih] + li_c[ih] for ih in pairs}
    m_loc = {ih: jnp.max(a[ih], axis=0, keepdims=True) for ih in pairs}
    kw = {(i, h): ks[h][rs[i]] * jnp.exp(a[(i, h)] - m_loc[(i, h)]) for i, h in pairs}
    c_chunk = {(i, h): mm_tn(kw[(i, h)], vm[h][rs[i]]) for i, h in pairs}
    mem = {(0, h): st["S"][h] for h in hs}
    c_in = {(0, h): st["C"][h] for h in hs}
    n_in = {(0, h): st["n"][h] for h in hs}
    m_in = {(0, h): st["m"][h][:, 0:1] for h in hs}
    for i, h in pairs:
        mem[(i + 1, h)] = mem[(i, h)] * dec[i][:, s6[h]] + s_chunk[(i, h)]
        m_nx = jnp.maximum(f_last[(i, h)] + m_in[(i, h)], m_loc[(i, h)])
        sp = jnp.exp(f_last[(i, h)] + m_in[(i, h)] - m_nx)
        sl = jnp.exp(m_loc[(i, h)] - m_nx)
        c_in[(i + 1, h)] = sp * c_in[(i, h)] + sl * c_chunk[(i, h)]
        n_in[(i + 1, h)] = sp * n_in[(i, h)] + sl * jnp.sum(kw[(i, h)], axis=0, keepdims=True)
        m_in[(i + 1, h)] = m_nx
    s_new = [mem[(n_ch, h)] for h in hs]
    o_inter = {(i, h): mm_nt(qp[rs[i], s6[h]], mem[(i, h)]) for i, h in pairs}
    q_c = {(i, h): mm_nn(qm[h][rs[i]], c_in[(i, h)]) for i, h in pairs}
    scores = {ih: jnp.where(causal, a_fwd[ih], a_bwd[ih]) for ih in pairs}
    log_d = {(i, h): gates_t[h:h + 1, rs[i]] - jnp.abs(fc_c[(i, h)] - fc_t[4 + h:5 + h, rs[i]]) for i, h in pairs}
    g_int = {ih: fc_c[ih] + m_in[ih] for ih in pairs}
    m_t = {ih: jnp.maximum(g_int[ih], jnp.max(log_d[ih], axis=1, keepdims=True)) for ih in pairs}
    s = {ih: qk[ih] * jnp.exp(log_d[ih] - m_t[ih]) for ih in pairs}
    scl = {ih: jnp.exp(g_int[ih] - m_t[ih]) for ih in pairs}
    o = {(i, h): mm_nn(scores[(i, h)], v[rs[i], s12[h]]) + o_inter[(i, h)] for i, h in pairs}
    num = {(i, h): mm_nn(s[(i, h)], vm[h][rs[i]]) + scl[(i, h)] * q_c[(i, h)] for i, h in pairs}
    o = {ih: o[ih] * lax.rsqrt(_mean(o[ih] * o[ih]) + EPS) * p["ggla"] for ih in pairs}
    gate = g * _sigmoid(g)
    out_a = {(i, h): o[(i, h)] * gate[rs[i], s12[h]] for i, h in pairs}
    den = {(i, h): jnp.sum(s[(i, h)], axis=1, keepdims=True)
           + scl[(i, h)] * jnp.sum(qm[h][rs[i]] * n_in[(i, h)], axis=1, keepdims=True) for i, h in pairs}
    den = {ih: jnp.maximum(jnp.abs(den[ih]), jnp.exp(-m_t[ih])) for ih in pairs}
    open_gate = _sigmoid(opre)
    hc = {(i, h): num[(i, h)] / den[(i, h)] * open_gate[rs[i], s12[h]] for i, h in pairs}
    d0 = {ih: hc[ih] - _mean(hc[ih]) for ih in pairs}
    y = {ih: d0[ih] * lax.rsqrt(_mean(d0[ih] * d0[ih]) + EPS) for ih in pairs}
    skipped = p["skip"] * xc
    out_b = {(i, h): y[(i, h)] * p["gml"][:, s12[h]] + skipped[rs[i], s12[h]] for i, h in pairs}
    ab = jnp.concatenate([jnp.concatenate([out_a[(i, h)] for h in hs] + [out_b[(i, h)] for h in hs], axis=1) for i in cs],
                         axis=0)
    new = {"S": s_new, "C": [c_in[(n_ch, h)] for h in hs], "n": [n_in[(n_ch, h)] for h in hs],
           "m": [jnp.broadcast_to(m_in[(n_ch, h)], (1, ML_DH)) for h in hs]}
    return ab, new


_P_NAMES = ("wau", "bau", "ggla", "cw", "cb", "wq", "wk", "wv", "wif", "bif", "skip", "gml")
_P_SHAPES = {
    "wau": (128, 256), "bau": (1, 256), "ggla": (1, 128), "cw": (4, 512), "cb": (1, 512),
    "wq": (512, 128), "wk": (512, 128), "wv": (512, 128),
    "wif": (1536, 128), "bif": (1, 128), "skip": (1, 512), "gml": (1, 512),
}
_P_BLOCKDIAG = ("wq", "wk", "wv")
_S_NAMES = ("S", "C", "n", "m")
_S_SHAPES = {"S": (HEADS, GLA_DV, GLA_DK), "C": (HEADS, ML_DH, ML_DH), "n": (HEADS, 1, ML_DH), "m": (HEADS, 1, ML_DH)}


def _per_head(ref):
    return [ref[h] for h in range(HEADS)]


def _block_mask():
    r = lax.broadcasted_iota(jnp.int32, (128, 128), 0)
    c = lax.broadcasted_iota(jnp.int32, (128, 128), 1)
    same_block = (r >> 2) == (c >> 2)
    spread = jnp.logical_and(r < 4, (c & 3) == r)
    return same_block.astype(F32), spread.astype(F32)


def _expand_blockdiag(w_ref, dense_ref):
    same_block, spread = _block_mask()
    for h in range(HEADS):
        tiled = _pmm_nn(w_ref[h * 128:(h + 1) * 128, :], spread)
        dense_ref[h] = tiled * same_block


def _collect_blockdiag(ddense_ref, dw_ref):
    same_block, spread = _block_mask()
    for h in range(HEADS):
        dw_ref[h * 128:(h + 1) * 128, :] = lax.dot_general(
            ddense_ref[h] * same_block, spread, (((1,), (1,)), ((), ())), precision=lax.Precision.HIGHEST,
            preferred_element_type=F32)


def _const_spec(shape):
    zeros = (0,) * len(shape)
    return pl.BlockSpec(shape, lambda i: zeros)


def _split(refs, *counts):
    out, at = [], 0
    for c in counts:
        out.append(refs[at:at + c])
        at += c
    assert at == len(refs)
    return out


def _ride(rider, phases, cond, ins, outs, sems):
    if rider is None or not any(hasattr(rider, phase) for phase in phases):
        return
    lands, (send_sems, recv_sems, flush_sems) = sems[:-3], sems[-3:]

    @pl.when(cond)
    def _():
        for phase in phases:
            getattr(rider, phase)(ins, lands, send_sems, recv_sems)
            if hasattr(rider, "flush"):
                rider.flush(phase, lands, outs, flush_sems)
        if "last" in phases and not hasattr(rider, "flush"):
            flush = [pltpu.make_async_copy(lands[k], outs[k], flush_sems.at[k]) for k in range(len(outs))]
            for cp in flush:
                cp.start()
            for cp in flush:
                cp.wait()


def _middle_step(rider, n_steps):
    return min(n_steps - 2, int(getattr(rider, "middle_at", 1.0) * n_steps))


def _rider_specs(rider, rider_ins):
    if rider is None:
        return [], [], [], []
    scratch = [pltpu.VMEM(s.shape, s.dtype) for s in list(rider.out_shape) + list(getattr(rider, "work_shape", ()))]
    scratch += [pltpu.SemaphoreType.DMA((rider.n_sems,)), pltpu.SemaphoreType.DMA((rider.n_sems,)),
                pltpu.SemaphoreType.DMA((getattr(rider, "n_flush", len(rider.out_shape)),))]
    in_space = getattr(rider, "in_space", VMEM_WHOLE)
    in_specs = list(in_space) if isinstance(in_space, (list, tuple)) else [in_space] * len(rider_ins)
    return in_specs, [ANY] * len(rider.out_shape), list(rider.out_shape), scratch


def _mixer_fwd(pm, p, rider=None, rider_ins=()):
    n_p = len(_P_NAMES)
    r_in, r_out_specs, r_out_shape, r_sems = _rider_specs(rider, rider_ins)

    def body(*refs):
        (pm_ref, xprev_ref), p_list, ride_in, (ab_ref,), so_refs, ride_out, sc_refs, dense_list, sems = _split(
            refs, 2, n_p, len(r_in), 1, 4, len(r_out_specs), 4, 3, len(r_sems))
        p_refs = dict(zip(_P_NAMES, p_list))
        dense = dict(zip(_P_BLOCKDIAG, dense_list))
        n = pl.program_id(0)
        _ride(rider, ("first",), n == 0, ride_in, ride_out, sems)

        @pl.when(n == 0)
        def _():
            for r in sc_refs:
                r[...] = jnp.zeros_like(r)
            for nm in _P_BLOCKDIAG:
                _expand_blockdiag(p_refs[nm], dense[nm])

        st = {name: _per_head(r) for name, r in zip(_S_NAMES, sc_refs)}
        pv = {nm: (_per_head(dense[nm]) if nm in _P_BLOCKDIAG else p_refs[nm][...]) for nm in _P_NAMES}
        for name, r in zip(_S_NAMES, so_refs):
            for h in range(HEADS):
                r[0, h] = st[name][h]
        xprev8 = jnp.where(n > 0, xprev_ref[CHUNK - 8:CHUNK, :], 0.0)
        ab, st = _mixer_chunk(_PLAIN_OPS, pv, st, pm_ref[...], xprev8)
        ab_ref[...] = ab.astype(BF16)
        for name, r in zip(_S_NAMES, sc_refs):
            for h in range(HEADS):
                r[h] = st[name][h]
        _ride(rider, ("middle",), n == _middle_step(rider, N_SWEEP), ride_in, ride_out, sems)
        _ride(rider, ("last",), n == N_SWEEP - 1, ride_in, ride_out, sems)

    in_specs = [pl.BlockSpec((SWEEP * CHUNK, PM_W), lambda i: (i, 0)),
                pl.BlockSpec((CHUNK, 512), lambda i: (jnp.maximum(SWEEP * i - 1, 0), PM_XM // 512))]
    in_specs += [_const_spec(_P_SHAPES[nm]) for nm in _P_NAMES] + r_in
    out_specs = [pl.BlockSpec((SWEEP * CHUNK, 1024), lambda i: (i, 0))]
    out_shape = [jax.ShapeDtypeStruct((SEQ, 1024), BF16)]
    for nm in _S_NAMES:
        shp = _S_SHAPES[nm]
        out_specs.append(pl.BlockSpec((1,) + shp, lambda i: (i, 0, 0, 0)))
        out_shape.append(jax.ShapeDtypeStruct((N_SWEEP,) + shp, F32))
    return pl.pallas_call(
        body, grid=(N_SWEEP,), in_specs=in_specs, out_specs=out_specs + r_out_specs, out_shape=out_shape + r_out_shape,
        scratch_shapes=[pltpu.VMEM(_S_SHAPES[nm], F32) for nm in _S_NAMES]
        + [pltpu.VMEM((HEADS, 128, 128), F32) for _ in _P_BLOCKDIAG] + r_sems,
        compiler_params=_params(("arbitrary",)), name="mixer_fwd",
    )(pm, pm, *[p[nm] for nm in _P_NAMES], *rider_ins)


def _mixer_bwd(pm, dab, states, p, rider=None, rider_ins=()):
    n_p = len(_P_NAMES)
    r_in, r_out_specs, r_out_shape, r_sems = _rider_specs(rider, rider_ins)

    def body(*refs):
        ((pm_ref, xprev_ref, dab_ref), si_refs, p_list, ride_in, (dpm_ref,), dp_list, ride_out, ds_refs, (carry_ref,),
         dense_list, ddense_list, sems) = _split(refs, 3, 4, n_p, len(r_in), 1, n_p, len(r_out_specs), 4, 1, 3, 3, len(r_sems))
        p_refs = dict(zip(_P_NAMES, p_list))
        dp_refs = dict(zip(_P_NAMES, dp_list))
        dense = dict(zip(_P_BLOCKDIAG, dense_list))
        ddense = dict(zip(_P_BLOCKDIAG, ddense_list))
        i = pl.program_id(0)
        blk = N_SWEEP - 1 - i
        _ride(rider, ("first",), i == 0, ride_in, ride_out, sems)

        @pl.when(i == 0)
        def _():
            for r in ds_refs:
                r[...] = jnp.zeros_like(r)
            for nm in _P_NAMES:
                if nm in _P_BLOCKDIAG:
                    ddense[nm][...] = jnp.zeros_like(ddense[nm])
                    _expand_blockdiag(p_refs[nm], dense[nm])
                else:
                    dp_refs[nm][...] = jnp.zeros_like(dp_refs[nm])
            carry_ref[...] = jnp.zeros_like(carry_ref)

        pv = {nm: (_per_head(dense[nm]) if nm in _P_BLOCKDIAG else p_refs[nm][...]) for nm in _P_NAMES}
        dst = {name: _per_head(r) for name, r in zip(_S_NAMES, ds_refs)}
        st = {name: [r[0, h] for h in range(HEADS)] for name, r in zip(_S_NAMES, si_refs)}
        xprev8 = jnp.where(blk > 0, xprev_ref[CHUNK - 8:CHUNK, :], 0.0)
        _, vjp = jax.vjp(functools.partial(_mixer_chunk, _VJP_OPS), pv, st, pm_ref[...], xprev8)
        dp_sum, dst, dpm, dxprev8 = vjp((dab_ref[...], dst))
        reach = jnp.concatenate([jnp.zeros((SWEEP * CHUNK - 8, 512), F32), carry_ref[...]], axis=0)
        dpm_ref[:, 0:PM_XM] = dpm[:, 0:PM_XM].astype(BF16)
        dpm_ref[:, PM_XM:PM_XM + 512] = (dpm[:, PM_XM:PM_XM + 512] + reach).astype(BF16)
        dpm_ref[:, PM_XM + 512:PM_W] = dpm[:, PM_XM + 512:PM_W].astype(BF16)
        carry_ref[...] = dxprev8
        for name, r in zip(_S_NAMES, ds_refs):
            for h in range(HEADS):
                r[h] = dst[name][h]
        for nm in _P_NAMES:
            if nm in _P_BLOCKDIAG:
                for h in range(HEADS):
                    ddense[nm][h] += dp_sum[nm][h]
            else:
                dp_refs[nm][...] += dp_sum[nm]

        @pl.when(i == N_SWEEP - 1)
        def _():
            for nm in _P_BLOCKDIAG:
                _collect_blockdiag(ddense[nm], dp_refs[nm])

        _ride(rider, ("early",), i == 1, ride_in, ride_out, sems)
        _ride(rider, ("middle",), i == _middle_step(rider, N_SWEEP), ride_in, ride_out, sems)
        _ride(rider, ("last",), i == N_SWEEP - 1, ride_in, ride_out, sems)

    rev = lambda i: (N_SWEEP - 1 - i, 0)
    in_specs = [pl.BlockSpec((SWEEP * CHUNK, PM_W), rev),
                pl.BlockSpec((CHUNK, 512), lambda i: (jnp.maximum(SWEEP * (N_SWEEP - 1 - i) - 1, 0), PM_XM // 512)),
                pl.BlockSpec((SWEEP * CHUNK, 1024), rev)]
    for nm in _S_NAMES:
        in_specs.append(pl.BlockSpec((1,) + _S_SHAPES[nm], lambda i: (N_SWEEP - 1 - i, 0, 0, 0)))
    in_specs += [_const_spec(_P_SHAPES[nm]) for nm in _P_NAMES] + r_in
    out_specs = [pl.BlockSpec((SWEEP * CHUNK, PM_W), rev)] + [_const_spec(_P_SHAPES[nm]) for nm in _P_NAMES]
    out_shape = [jax.ShapeDtypeStruct((SEQ, PM_W), BF16)] + [jax.ShapeDtypeStruct(_P_SHAPES[nm], F32) for nm in _P_NAMES]
    res = pl.pallas_call(
        body, grid=(N_SWEEP,), in_specs=in_specs, out_specs=out_specs + r_out_specs, out_shape=out_shape + r_out_shape,
        scratch_shapes=[pltpu.VMEM(_S_SHAPES[nm], F32) for nm in _S_NAMES] + [pltpu.VMEM((8, 512), F32)]
        + [pltpu.VMEM((HEADS, 128, 128), F32) for _ in range(2 * len(_P_BLOCKDIAG))] + r_sems,
        compiler_params=_params(("arbitrary",)), name="mixer_bwd",
    )(pm, pm, dab, *states, *[p[nm] for nm in _P_NAMES], *rider_ins)
    return res[0], dict(zip(_P_NAMES, res[1:1 + n_p])), res[1 + n_p:]


def _tok(width):
    return pl.BlockSpec((TOK_TILE, width), lambda i: (i, 0))


def _once(shape):
    zeros = (0,) * len(shape)
    return pl.BlockSpec(shape, lambda i: zeros, pipeline_mode=pl.Buffered(1))


def _rms_fwd(x):
    r = lax.rsqrt(_mean(x * x) + EPS)
    return x * r, r


def _rms_bwd(dy, xn, r, g):
    gd = dy * g
    return r * (gd - xn * _mean(xn * gd))


def _tiled_call(body, in_specs, out_specs, out_shape, args, name, rider=None, rider_ins=()):
    r_in, r_out_specs, r_out_shape, r_scratch = _rider_specs(rider, rider_ins)
    n_in, n_out = len(in_specs), len(out_specs)

    def hosted(*refs):
        ins, ride_in, outs, ride_out, scratch = _split(refs, n_in, len(r_in), n_out, len(r_out_specs), len(r_scratch))
        i = pl.program_id(0)
        _ride(rider, ("first",), i == 0, ride_in, ride_out, scratch)
        body(*ins, *outs)
        _ride(rider, ("early",), i == 1, ride_in, ride_out, scratch)
        _ride(rider, ("middle",), i == _middle_step(rider, N_TOK_TILE), ride_in, ride_out, scratch)
        _ride(rider, ("last",), i == N_TOK_TILE - 1, ride_in, ride_out, scratch)

    res = pl.pallas_call(
        hosted, grid=(N_TOK_TILE,), in_specs=list(in_specs) + r_in, out_specs=list(out_specs) + r_out_specs,
        out_shape=list(out_shape) + r_out_shape, scratch_shapes=r_scratch,
        compiler_params=_params(("arbitrary",)), name=name,
    )(*args, *rider_ins)
    return res[:n_out], res[n_out:]


def _in_proj(x, g_pre, wt_in, rider=None, rider_ins=()):
    def body(x_ref, g_ref, wt_ref, pm_ref, gab_ref, h_ref):
        xn, _ = _rms_fwd(x_ref[...])
        h = (xn * g_ref[...]).astype(BF16)
        h_ref[...] = h
        pm_ref[:, 0:PM_XM] = _nt(h, wt_ref[0:IN_ALOW, :])
        pm_ref[:, PM_XM:PM_AL] = _nt(h, wt_ref[IN_XM:IN_GATES, :])
        pm_ref[:, PM_AL:PM_W] = _nt(h, wt_ref[IN_ALOW:IN_ALOW + 128, :])
        gab_ref[...] = _nt(h, wt_ref[IN_GATES:D_IN, :])

    return _tiled_call(
        body, [_tok(D_MODEL), _once((1, D_MODEL)), _once((D_IN, D_MODEL))], [_tok(PM_W), _tok(GAB_W), _tok(D_MODEL)],
        [jax.ShapeDtypeStruct((SEQ, PM_W), F32), jax.ShapeDtypeStruct((SEQ, GAB_W), F32),
         jax.ShapeDtypeStruct((SEQ, D_MODEL), BF16)], (x, g_pre, wt_in), "in_proj", rider, rider_ins)


def _merge_fwd(ab, gab, x, w_pa4, w_pb4, w_o, g_post, rider=None, rider_ins=()):
    def body(ab_ref, gab_ref, x_ref, wpa_ref, wpb_ref, wo_ref, g_ref, x1_ref, mix_ref, mg_ref):
        a = ab_ref[:, 0:512]
        b = ab_ref[:, 512:1024]
        for j in range(N_CHIP):
            blk = slice(j * 256, (j + 1) * 256)
            ya = jnp.dot(a, wpa_ref[j], preferred_element_type=F32)
            yb = jnp.dot(b, wpb_ref[j], preferred_element_type=F32)
            sa = _sigmoid(gab_ref[:, j * 256:(j + 1) * 256])
            sb = _sigmoid(gab_ref[:, 1024 + j * 256:1024 + (j + 1) * 256])
            mg_ref[:, blk] = (sa * ya + sb * yb).astype(BF16)
        mix = jnp.dot(mg_ref[...], wo_ref[...], preferred_element_type=F32)
        mix_ref[...] = mix
        mn, _ = _rms_fwd(mix)
        x1_ref[...] = x_ref[...] + mn * g_ref[...]

    return _tiled_call(
        body, [_tok(1024), _tok(GAB_W), _tok(D_MODEL), _once((N_CHIP, 512, 256)), _once((N_CHIP, 512, 256)),
               _once((D_MODEL, D_MODEL)), _once((1, D_MODEL))], [_tok(D_MODEL), _tok(D_MODEL), _tok(D_MODEL)],
        [jax.ShapeDtypeStruct((SEQ, D_MODEL), F32), jax.ShapeDtypeStruct((SEQ, D_MODEL), F32),
         jax.ShapeDtypeStruct((SEQ, D_MODEL), BF16)], (ab, gab, x, w_pa4, w_pb4, w_o, g_post), "merge_fwd", rider, rider_ins)


def _mlp(x1, target, g_pre, g_post, w_up4, w_down_a4, w_down_b4):
    def body(x1_ref, t_ref, gpre_ref, gpost_ref, wup_ref, wda_ref, wdb_ref,
             dx1_ref, u_ref, dd_ref, h2_ref, dpre_ref, dgpost_ref, dgpre_ref, loss_ref):
        @pl.when(pl.program_id(0) == 0)
        def _():
            dgpost_ref[...] = jnp.zeros_like(dgpost_ref)
            dgpre_ref[...] = jnp.zeros_like(dgpre_ref)
            loss_ref[...] = jnp.zeros_like(loss_ref)

        x1 = x1_ref[...]
        gpre = gpre_ref[...]
        gpost = gpost_ref[...]
        xn2, r2 = _rms_fwd(x1)
        h2 = (xn2 * gpre).astype(BF16)
        h2_ref[...] = h2
        rl = []
        d = jnp.zeros((TOK_TILE, D_MODEL), F32)
        for j in range(N_CHIP):
            blk = slice(j * 1024, (j + 1) * 1024)
            r = jnp.maximum(jnp.dot(h2, wup_ref[j], preferred_element_type=F32), 0.0)
            rl.append(r)
            u = (r * r).astype(BF16)
            u_ref[:, blk] = u
            d = d + jnp.dot(u[:, 0:512], wda_ref[j], preferred_element_type=F32)
            d = d + jnp.dot(u[:, 512:1024], wdb_ref[j], preferred_element_type=F32)
        dn, r3 = _rms_fwd(d)
        diff = x1 + dn * gpost - t_ref[...]
        loss_ref[...] += jnp.sum(diff * diff, keepdims=True) * (0.5 / D_MODEL)
        dy = diff * (1.0 / D_MODEL)
        dgpost_ref[...] += jnp.sum(dy * dn, axis=0, keepdims=True)
        dd = _rms_bwd(dy, dn, r3, gpost).astype(BF16)
        dd_ref[...] = dd
        dh2 = jnp.zeros((TOK_TILE, D_MODEL), F32)
        for j in range(N_CHIP):
            blk = slice(j * 1024, (j + 1) * 1024)
            du = jnp.concatenate([_nt(dd, wda_ref[j]), _nt(dd, wdb_ref[j])], axis=1)
            dpre = (du * (2.0 * rl[j])).astype(BF16)
            dpre_ref[:, blk] = dpre
            dh2 = dh2 + _nt(dpre, wup_ref[j])
        dgpre_ref[...] += jnp.sum(dh2 * xn2, axis=0, keepdims=True)
        dx1_ref[...] = dy + _rms_bwd(dh2, xn2, r2, gpre)

    acc = pl.BlockSpec((1, D_MODEL), lambda i: (0, 0))
    return pl.pallas_call(
        body, grid=(N_TOK_TILE,),
        in_specs=[_tok(D_MODEL), _tok(D_MODEL), _once((1, D_MODEL)), _once((1, D_MODEL)),
                  _once((N_CHIP, D_MODEL, 1024)), _once((N_CHIP, 512, D_MODEL)), _once((N_CHIP, 512, D_MODEL))],
        out_specs=[_tok(D_MODEL), _tok(D_FF), _tok(D_MODEL), _tok(D_MODEL), _tok(D_FF), acc, acc,
                   pl.BlockSpec((1, 128), lambda i: (0, 0))],
        out_shape=[jax.ShapeDtypeStruct((SEQ, D_MODEL), F32), jax.ShapeDtypeStruct((SEQ, D_FF), BF16),
                   jax.ShapeDtypeStruct((SEQ, D_MODEL), BF16), jax.ShapeDtypeStruct((SEQ, D_MODEL), BF16),
                   jax.ShapeDtypeStruct((SEQ, D_FF), BF16), jax.ShapeDtypeStruct((1, D_MODEL), F32),
                   jax.ShapeDtypeStruct((1, D_MODEL), F32), jax.ShapeDtypeStruct((1, 128), F32)],
        compiler_params=_params(("arbitrary",)), name="mlp_fwd_bwd",
    )(x1, target, g_pre, g_post, w_up4, w_down_a4, w_down_b4)


def _merge_bwd(dx1, mix, ab, gab, w_pa4, w_pb4, w_o, g_post):
    def body(dx1_ref, mix_ref, ab_ref, gab_ref, wpa_ref, wpb_ref, wo_ref, g_ref,
             dmix_ref, dya_ref, dyb_ref, dgab_ref, dab_ref, dg_ref):
        @pl.when(pl.program_id(0) == 0)
        def _():
            dg_ref[...] = jnp.zeros_like(dg_ref)

        dx1 = dx1_ref[...]
        mn, r = _rms_fwd(mix_ref[...])
        dg_ref[...] += jnp.sum(dx1 * mn, axis=0, keepdims=True)
        dmix = _rms_bwd(dx1, mn, r, g_ref[...]).astype(BF16)
        dmix_ref[...] = dmix
        dmerged = _nt(dmix, wo_ref[...])
        a = ab_ref[:, 0:512]
        b = ab_ref[:, 512:1024]
        da = jnp.zeros((TOK_TILE, 512), F32)
        db = jnp.zeros((TOK_TILE, 512), F32)
        for j in range(N_CHIP):
            blk = slice(j * 256, (j + 1) * 256)
            blk_b = slice(1024 + j * 256, 1024 + (j + 1) * 256)
            dm = dmerged[:, blk]
            ya = jnp.dot(a, wpa_ref[j], preferred_element_type=F32)
            yb = jnp.dot(b, wpb_ref[j], preferred_element_type=F32)
            sa = _sigmoid(gab_ref[:, blk])
            sb = _sigmoid(gab_ref[:, blk_b])
            dya = (dm * sa).astype(BF16)
            dyb = (dm * sb).astype(BF16)
            dya_ref[:, blk] = dya
            dyb_ref[:, blk] = dyb
            dgab_ref[:, blk] = (dm * ya * sa * (1.0 - sa)).astype(BF16)
            dgab_ref[:, blk_b] = (dm * yb * sb * (1.0 - sb)).astype(BF16)
            da = da + _nt(dya, wpa_ref[j])
            db = db + _nt(dyb, wpb_ref[j])
        dab_ref[:, 0:512] = da
        dab_ref[:, 512:1024] = db

    return pl.pallas_call(
        body, grid=(N_TOK_TILE,),
        in_specs=[_tok(D_MODEL), _tok(D_MODEL), _tok(1024), _tok(GAB_W), _once((N_CHIP, 512, 256)),
                  _once((N_CHIP, 512, 256)), _once((D_MODEL, D_MODEL)), _once((1, D_MODEL))],
        out_specs=[_tok(D_MODEL), _tok(D_MODEL), _tok(D_MODEL), _tok(GAB_W), _tok(1024),
                   pl.BlockSpec((1, D_MODEL), lambda i: (0, 0))],
        out_shape=[jax.ShapeDtypeStruct((SEQ, D_MODEL), BF16), jax.ShapeDtypeStruct((SEQ, D_MODEL), BF16),
                   jax.ShapeDtypeStruct((SEQ, D_MODEL), BF16), jax.ShapeDtypeStruct((SEQ, GAB_W), BF16),
                   jax.ShapeDtypeStruct((SEQ, 1024), F32), jax.ShapeDtypeStruct((1, D_MODEL), F32)],
        compiler_params=_params(("arbitrary",)), name="merge_bwd",
    )(dx1, mix, ab, gab, w_pa4, w_pb4, w_o, g_post)


def _in_proj_bwd(dpm, dgab, x, dx1, g_pre, wt_in, rider=None, rider_ins=()):
    def body(dpm_ref, dgab_ref, x_ref, dx1_ref, g_ref, wt_ref, dx_ref, dg_ref):
        @pl.when(pl.program_id(0) == 0)
        def _():
            dg_ref[...] = jnp.zeros_like(dg_ref)

        dh = jnp.dot(dpm_ref[:, 0:PM_XM], wt_ref[0:IN_ALOW, :], preferred_element_type=F32)
        dh = dh + jnp.dot(dpm_ref[:, PM_XM:PM_AL], wt_ref[IN_XM:IN_GATES, :], preferred_element_type=F32)
        dh = dh + jnp.dot(dpm_ref[:, PM_AL:PM_W], wt_ref[IN_ALOW:IN_ALOW + 128, :], preferred_element_type=F32)
        dh = dh + jnp.dot(dgab_ref[...], wt_ref[IN_GATES:D_IN, :], preferred_element_type=F32)
        xn, r = _rms_fwd(x_ref[...])
        dg_ref[...] += jnp.sum(dh * xn, axis=0, keepdims=True)
        dx_ref[...] = dx1_ref[...] + _rms_bwd(dh, xn, r, g_ref[...])

    return _tiled_call(
        body, [_tok(PM_W), _tok(GAB_W), _tok(D_MODEL), _tok(D_MODEL), _once((1, D_MODEL)), _once((D_IN, D_MODEL))],
        [_tok(D_MODEL), pl.BlockSpec((1, D_MODEL), lambda i: (0, 0))],
        [jax.ShapeDtypeStruct((SEQ, D_MODEL), F32), jax.ShapeDtypeStruct((1, D_MODEL), F32)],
        (dpm, dgab, x, dx1, g_pre, wt_in), "in_proj_bwd", rider, rider_ins)


def _dw_in(dpm, dgab, h):
    n_pm = PM_AL // 512
    n_blk = n_pm + GAB_W // 512

    def body(dpm_ref, dgab_ref, dal_ref, h_ref, o_ref):
        i = pl.program_id(0)
        off = pl.multiple_of(i * 512 + 16 * (i >= 3).astype(jnp.int32), 16)

        @pl.when(i < n_pm)
        def _():
            o_ref[pl.ds(off, 512), :] = _tn(dpm_ref[...], h_ref[...]).astype(BF16)

        @pl.when(i >= n_pm)
        def _():
            o_ref[pl.ds(off, 512), :] = _tn(dgab_ref[...], h_ref[...]).astype(BF16)

        @pl.when(i == 0)
        def _():
            o_ref[IN_ALOW:IN_XM, :] = _tn(dal_ref[...], h_ref[...])[0:IN_XM - IN_ALOW].astype(BF16)

    return pl.pallas_call(
        body, grid=(n_blk,),
        in_specs=[pl.BlockSpec((SEQ, 512), lambda i: (0, jnp.minimum(i, n_pm - 1))),
                  pl.BlockSpec((SEQ, 512), lambda i: (0, jnp.maximum(i - n_pm, 0))),
                  pl.BlockSpec((SEQ, 128), lambda i: (0, PM_AL // 128)),
                  _once((SEQ, D_MODEL))],
        out_specs=pl.BlockSpec((D_IN, D_MODEL), lambda i: (0, 0)),
        out_shape=jax.ShapeDtypeStruct((D_IN, D_MODEL), BF16),
        compiler_params=_params(("arbitrary",)), name="dw_in",
    )(dpm, dgab, dpm, h)


def _tn_matmul(a, b, name, shards=1, tm=1024, rider=None, rider_ins=()):
    m, n = a.shape[1], b.shape[1]
    tm = min(tm, m)
    tn = n // shards if shards > 1 else min(n, 1024)
    steps_i, steps_j = m // tm, n // tn
    r_in, r_out_specs, r_out_shape, r_scratch = _rider_specs(rider, rider_ins)

    def body(*refs):
        (a_ref, b_ref), ride_in, (o_ref,), ride_out, scratch = _split(refs, 2, len(r_in), 1, len(r_out_specs), len(r_scratch))
        step = pl.program_id(0) * steps_j + pl.program_id(1)
        _ride(rider, ("first",), step == 0, ride_in, ride_out, scratch)
        o_ref[...] = _tn(a_ref[...], b_ref[...]).astype(BF16)
        _ride(rider, ("middle", "last"), step == steps_i * steps_j - 1, ride_in, ride_out, scratch)

    if shards > 1:
        out_spec = pl.BlockSpec((None, tm, tn), lambda i, j: (j, i, 0))
        out_shape = jax.ShapeDtypeStruct((shards, m, tn), BF16)
    else:
        out_spec = pl.BlockSpec((tm, tn), lambda i, j: (i, j))
        out_shape = jax.ShapeDtypeStruct((m, n), BF16)
    res = pl.pallas_call(
        body, grid=(steps_i, steps_j),
        in_specs=[pl.BlockSpec((SEQ, tm), lambda i, j: (0, i)), pl.BlockSpec((SEQ, tn), lambda i, j: (0, j))] + r_in,
        out_specs=[out_spec] + r_out_specs, out_shape=[out_shape] + r_out_shape, scratch_shapes=r_scratch,
        compiler_params=_params(("arbitrary", "arbitrary")), name=name,
    )(a, b, *rider_ins)
    return res[0] if rider is None else (res[0], res[1:])


MESH = pl.DeviceIdType.MESH
ANY = pl.BlockSpec(memory_space=pl.ANY)
VMEM_WHOLE = pl.BlockSpec(memory_space=pltpu.VMEM)

_BIG = ("w_in", "w_pa", "w_pb", "w_o", "w_up", "w_down")
_BIG_SHARD = {"w_in": (IN_SHARD, D_MODEL), "w_pa": (512, 256), "w_pb": (512, 256), "w_o": (256, D_MODEL),
              "w_up": (D_MODEL, 1024), "w_down": (1024, D_MODEL),
              "w_down_a": (512, D_MODEL), "w_down_b": (512, D_MODEL)}
_BIG_SPLIT = {"w_in": 1, "w_pa": 0, "w_pb": 0, "w_o": 0, "w_up": 0, "w_down": 0, "w_down_a": 0, "w_down_b": 0}


def _half(ref, e, name, lead=0, part=None):
    axis = _BIG_SPLIT[name]
    size = _BIG_SHARD[name][axis] // 2
    start = e * size
    if part is not None:
        size //= 2
        start = start + part * size
    start = pl.multiple_of(start, 128 if axis == 1 else 16)
    idx = [pl.ds(0, ref.shape[a]) for a in range(lead)]
    idx += [pl.ds(start, size), pl.ds(0, _BIG_SHARD[name][1])] if axis == 0 else [pl.ds(0, _BIG_SHARD[name][0]), pl.ds(start, size)]
    return ref.at[tuple(idx)]


def _half_shape(name):
    r, c = _BIG_SHARD[name]
    return (r // 2, c) if _BIG_SPLIT[name] == 0 else (r, c // 2)


def _remote(src, dst, send_sems, recv_sems, k, to):
    return pltpu.make_async_remote_copy(src_ref=src, dst_ref=dst, send_sem=send_sems.at[k], recv_sem=recv_sems.at[k],
                                        device_id=to, device_id_type=MESH)


def _mesh_place():
    x, y, c = lax.axis_index("x"), lax.axis_index("y"), lax.axis_index("c")
    return x, y, c, [(1 - x, y), (x, 1 - y), (1 - x, 1 - y)]


class _Gather:
    def __init__(self, names, small=(), middle_at=0.5):
        self.middle_at = middle_at
        self.names = tuple(names)
        self.nb = len(self.names)
        self.n = self.nb + len(small)
        self.n_sems = 8 * self.nb + 3 * len(small)
        self.n_flush = 6 * self.nb + len(small)
        self.out_shape = [jax.ShapeDtypeStruct((N_CHIP,) + _BIG_SHARD[nm], BF16) for nm in self.names]
        self.out_shape += [jax.ShapeDtypeStruct((N_CHIP,) + s.shape, s.dtype) for s in small]

    def _copies(self, ins, outs, ss, rs, k):
        x, y, c, _ = _mesh_place()
        name = self.names[k]
        me, xn, yn, dg = 2 * x + y, 2 * (1 - x) + y, 2 * x + (1 - y), 2 * (1 - x) + (1 - y)
        to_x, to_y, sibling = (1 - x, y, c), (x, 1 - y, c), (x, y, 1 - c)

        def region(slot, e, part=None):
            return _half(outs[k].at[slot], e, name, part=part)

        def copy(pair, src, dst, to):
            return _remote(src, dst, ss, rs, 8 * k + pair, to)

        mine = _half(ins[k], c, name)
        sent = [copy(0, mine, region(me, c), to_x), copy(1, mine, region(me, c), to_y),
                copy(2, region(xn, c, 0), region(xn, c, 0), to_y), copy(3, region(yn, c, 1), region(yn, c, 1), to_x),
                copy(4, region(xn, c), region(xn, c), sibling), copy(5, region(yn, c), region(yn, c), sibling),
                copy(6, region(dg, c, 0), region(dg, c, 0), sibling), copy(7, region(dg, c, 1), region(dg, c, 1), sibling)]
        landing = [region(xn, c), region(yn, c), region(dg, c, 0), region(dg, c, 1),
                   region(xn, 1 - c), region(yn, 1 - c), region(dg, 1 - c, 0), region(dg, 1 - c, 1)]
        received = [copy(pair, dst, dst, sibling) for pair, dst in enumerate(landing)]
        return sent, received

    def _small(self, ins, outs, ss, rs, k, j, peer, slot, c):
        return _remote(ins[k], outs[k].at[slot], ss, rs, 8 * self.nb + 3 * (k - self.nb) + j, (*peer, c))

    def flush(self, phase, lands, outs, fs):
        x, y, c, _ = _mesh_place()
        me, xn, yn, dg = 2 * x + y, 2 * (1 - x) + y, 2 * x + (1 - y), 2 * (1 - x) + (1 - y)

        def pieces(k):
            name = self.names[k]
            spots = [lambda r: r.at[me], lambda r: _half(r.at[xn], c, name), lambda r: _half(r.at[yn], c, name),
                     lambda r: _half(r.at[xn], 1 - c, name), lambda r: _half(r.at[yn], 1 - c, name), lambda r: r.at[dg]]
            return [pltpu.make_async_copy(spot(lands[k]), spot(outs[k]), fs.at[6 * k + t]) for t, spot in enumerate(spots)]

        ready = {"first": (0,), "middle": (1, 2), "last": (3, 4, 5)}[phase]
        for k in range(self.nb):
            cps = pieces(k)
            for t in ready:
                cps[t].start()
        if phase == "last":
            small = [pltpu.make_async_copy(lands[k], outs[k], fs.at[6 * self.nb + k - self.nb]) for k in range(self.nb, self.n)]
            for cp in small:
                cp.start()
            for k in range(self.nb):
                for cp in pieces(k):
                    cp.wait()
            for cp in small:
                cp.wait()

    def first(self, ins, outs, ss, rs):
        x, y, c, peers = _mesh_place()
        me = 2 * x + y
        for k in range(self.nb):
            sent, _ = self._copies(ins, outs, ss, rs, k)
            sent[0].start()
            sent[1].start()
        for k in range(self.nb, self.n):
            for j, peer in enumerate(peers):
                self._small(ins, outs, ss, rs, k, j, peer, me, c).start()
        for k in range(self.n):
            outs[k][me] = ins[k][...]

    def middle(self, ins, outs, ss, rs):
        for k in range(self.nb):
            sent, received = self._copies(ins, outs, ss, rs, k)
            for pair in (0, 1):
                received[pair].wait_recv()
                sent[2 + pair].start()
                sent[4 + pair].start()

    def last(self, ins, outs, ss, rs):
        x, y, c, peers = _mesh_place()
        for k in range(self.nb):
            sent, received = self._copies(ins, outs, ss, rs, k)
            for pair in (2, 3):
                received[pair].wait_recv()
                sent[4 + pair].start()
        for k in range(self.nb):
            sent, received = self._copies(ins, outs, ss, rs, k)
            for pair in range(4, 8):
                received[pair].wait_recv()
            for cp in sent:
                cp.wait_send()
        for k in range(self.nb, self.n):
            for j, (px, py) in enumerate(peers):
                self._small(ins, outs, ss, rs, k, j, (px, py), 2 * px + py, c).wait_recv()
                self._small(ins, outs, ss, rs, k, j, (px, py), 2 * x + y, c).wait_send()


def _run_alone(rider, ins, name):
    def body(*refs):
        r_in, r_out, sems = _split(refs, len(ins), len(rider.out_shape), 2)
        rider.first(r_in, r_out, *sems)
        rider.middle(r_in, r_out, *sems)
        rider.last(r_in, r_out, *sems)

    return pl.pallas_call(
        body, in_specs=[VMEM_WHOLE] * len(ins), out_specs=[VMEM_WHOLE] * len(rider.out_shape), out_shape=rider.out_shape,
        scratch_shapes=[pltpu.SemaphoreType.DMA((rider.n_sems,)), pltpu.SemaphoreType.DMA((rider.n_sems,))],
        compiler_params=_params(), name=name,
    )(*ins)


class _Presum:
    in_space = ANY

    def __init__(self, names, base=0):
        self.names = tuple(names)
        self.n = len(self.names)
        self.base = base
        self.n_sems = 3 * self.n
        self.out_shape = [jax.ShapeDtypeStruct((N_CHIP,) + _half_shape(nm), BF16) for nm in self.names]
        self.work_shape = self.out_shape + self.out_shape

    def _stage(self, ins, bufs, ss, k, e, which):
        n = self.n
        return pltpu.make_async_copy(_half(ins[k], e, self.names[k], lead=1), bufs[which * n + k],
                                     ss.at[self.base + which * n + k])

    def _give(self, bufs, ss, rs, k, sibling):
        return _remote(bufs[self.n + k], bufs[k], ss, rs, self.base + k, sibling)

    def first(self, ins, bufs, ss, rs):
        x, y, c, _ = _mesh_place()
        for k in range(self.n):
            self._stage(ins, bufs, ss, k, 1 - c, 1).start()
        for k in range(self.n):
            self._stage(ins, bufs, ss, k, c, 2).start()
        for k in range(self.n):
            self._stage(ins, bufs, ss, k, 1 - c, 1).wait()
            self._give(bufs, ss, rs, k, (x, y, 1 - c)).start()

    def middle(self, ins, bufs, ss, rs):
        pass

    def last(self, ins, bufs, ss, rs):
        x, y, c, _ = _mesh_place()
        for k in range(self.n):
            self._give(bufs, ss, rs, k, (x, y, 1 - c)).wait_recv()
            self._stage(ins, bufs, ss, k, c, 2).wait()

            @pl.loop(0, N_CHIP)
            def _(j):
                bufs[k][j] = (bufs[k][j].astype(F32) + bufs[2 * self.n + k][j].astype(F32)).astype(BF16)
        for k in range(self.n):
            self._give(bufs, ss, rs, k, (x, y, 1 - c)).wait_send()


def _presum(names, grads, name):
    rider = _Presum(names)
    n = rider.n

    def body(*refs):
        g_refs, got_refs, work_refs, sems = _split(refs, n, n, 2 * n, 2)
        bufs = list(got_refs) + list(work_refs)
        rider.first(g_refs, bufs, *sems)
        rider.last(g_refs, bufs, *sems)

    return pl.pallas_call(
        body, in_specs=[ANY] * n, out_specs=[VMEM_WHOLE] * n, out_shape=rider.out_shape,
        scratch_shapes=[pltpu.VMEM(s.shape, s.dtype) for s in rider.work_shape]
        + [pltpu.SemaphoreType.DMA((rider.n_sems,)), pltpu.SemaphoreType.DMA((rider.n_sems,))],
        compiler_params=_params(), name=name,
    )(*grads)


class _ReduceRelay:
    middle_at = 0.75

    def __init__(self, names, base=0):
        self.names = tuple(names)
        self.n = len(self.names)
        self.base = base
        self.n_sems = 6 * self.n
        self.out_shape = [jax.ShapeDtypeStruct((N_CHIP,) + _half_shape(nm), BF16) for nm in self.names]
        quarter = [jax.ShapeDtypeStruct(self._part_shape(nm), BF16) for nm in self.names]
        self.work_shape = quarter + quarter

    @staticmethod
    def _part_shape(name):
        r, c = _half_shape(name)
        return (r // 2, c) if _BIG_SPLIT[name] == 0 else (r, c // 2)

    def _part(self, ref, name, p):
        r, c = self._part_shape(name)
        return ref.at[pl.ds(p * r, r), pl.ds(0, c)] if _BIG_SPLIT[name] == 0 else ref.at[pl.ds(0, r), pl.ds(p * c, c)]

    def _copies(self, ins, bufs, ss, rs, k):
        x, y, c, _ = _mesh_place()
        name, n = self.names[k], self.n
        me, xn, yn, dg = 2 * x + y, 2 * (1 - x) + y, 2 * x + (1 - y), 2 * (1 - x) + (1 - y)
        to_x, to_y = (1 - x, y, c), (x, 1 - y, c)
        mine = lambda slot, p: self._part(ins[k].at[slot], name, p)
        slot = lambda s, p: self._part(bufs[k].at[s], name, p)
        from_x, from_y = bufs[n + k], bufs[2 * n + k]

        def copy(pair, src, dst, to):
            return _remote(src, dst, ss, rs, self.base + 6 * k + pair, to)

        sent = [copy(0, mine(dg, 0), from_x, to_x), copy(1, mine(dg, 1), from_y, to_y),
                copy(2, mine(xn, 0), slot(me, 0), to_x), copy(3, mine(yn, 1), slot(me, 1), to_y),
                copy(4, from_y, slot(me, 1), to_x), copy(5, from_x, slot(me, 0), to_y)]
        landing = [from_x, from_y, slot(xn, 0), slot(yn, 1), slot(xn, 1), slot(yn, 0)]
        received = [copy(pair, dst, dst, to_x) for pair, dst in enumerate(landing)]
        return sent, received

    def first(self, ins, bufs, ss, rs):
        x, y, c, _ = _mesh_place()
        me, dg = 2 * x + y, 2 * (1 - x) + (1 - y)
        for k in range(self.n):
            sent, _ = self._copies(ins, bufs, ss, rs, k)
            for pair in range(4):
                sent[pair].start()
        for k in range(self.n):
            bufs[k][me] = ins[k][me]
            bufs[k][dg] = jnp.zeros(_half_shape(self.names[k]), BF16)

    def middle(self, ins, bufs, ss, rs):
        x, y, c, _ = _mesh_place()
        xn, yn = 2 * (1 - x) + y, 2 * x + (1 - y)
        for k in range(self.n):
            sent, received = self._copies(ins, bufs, ss, rs, k)
            name, n = self.names[k], self.n
            for pair, buf, own in ((0, bufs[n + k], self._part(ins[k].at[yn], name, 0)),
                                   (1, bufs[2 * n + k], self._part(ins[k].at[xn], name, 1))):
                received[pair].wait_recv()
                buf[...] = (buf[...].astype(F32) + own[...].astype(F32)).astype(BF16)
            sent[5].start()
            sent[4].start()

    def last(self, ins, bufs, ss, rs):
        for k in range(self.n):
            sent, received = self._copies(ins, bufs, ss, rs, k)
            for pair in range(2, 6):
                received[pair].wait_recv()
            for cp in sent:
                cp.wait_send()


class _PresumThenRelay:
    in_space = ANY
    middle_at = _ReduceRelay.middle_at

    def __init__(self, names):
        self.relay = _ReduceRelay(names)
        self.pre = _Presum(names, base=self.relay.n_sems)
        self.n_sems = self.relay.n_sems + self.pre.n_sems
        self.out_shape = self.relay.out_shape
        self.work_shape = list(self.relay.work_shape) + list(self.pre.out_shape) + list(self.pre.work_shape)
        self.n_relay = len(self.relay.out_shape) + len(self.relay.work_shape)

    def first(self, ins, bufs, ss, rs):
        self.pre.first(ins, bufs[self.n_relay:], ss, rs)

    def early(self, ins, bufs, ss, rs):
        self.pre.last(ins, bufs[self.n_relay:], ss, rs)
        self.relay.first(bufs[self.n_relay:], bufs[:self.n_relay], ss, rs)

    def middle(self, ins, bufs, ss, rs):
        self.relay.middle(bufs[self.n_relay:], bufs[:self.n_relay], ss, rs)

    def last(self, ins, bufs, ss, rs):
        self.relay.last(bufs[self.n_relay:], bufs[:self.n_relay], ss, rs)


class _SendPartials:
    def __init__(self, names, small_shape=None):
        self.n = len(names)
        self.small = small_shape is not None
        self.n_sems = 3 * self.n + 7
        self.out_shape = [jax.ShapeDtypeStruct((N_CHIP,) + _half_shape(nm), BF16) for nm in names]
        if self.small:
            self.out_shape.append(jax.ShapeDtypeStruct((N_DEV,) + small_shape, F32))

    def _piece(self, ins, outs, ss, rs, k, j, peer, src_slot, dst_slot, c):
        return _remote(ins[k].at[src_slot], outs[k].at[dst_slot], ss, rs, 3 * k + j, (*peer, c))

    def _small(self, ins, outs, ss, rs, r, other, slot):
        return _remote(ins[self.n], outs[self.n].at[slot], ss, rs, 3 * self.n + r, other)

    @staticmethod
    def _others(x, y, c):
        return [(x, y, 1 - c), (1 - x, y, c), (1 - x, y, 1 - c), (x, 1 - y, c), (x, 1 - y, 1 - c),
                (1 - x, 1 - y, c), (1 - x, 1 - y, 1 - c)]

    def first(self, ins, outs, ss, rs, only=None):
        x, y, c, peers = _mesh_place()
        me = 2 * x + y
        which = range(self.n) if only is None else only
        for k in which:
            for j, (px, py) in enumerate(peers):
                self._piece(ins, outs, ss, rs, k, j, (px, py), 2 * px + py, me, c).start()
        if self.small:
            for r, other in enumerate(self._others(x, y, c)):
                self._small(ins, outs, ss, rs, r, other, 4 * x + 2 * y + c).start()
            outs[self.n][4 * x + 2 * y + c] = ins[self.n][...]
        for k in which:
            outs[k][me] = ins[k][me]

    def middle(self, ins, outs, ss, rs):
        pass

    def last(self, ins, outs, ss, rs):
        x, y, c, peers = _mesh_place()
        me = 2 * x + y
        for k in range(self.n):
            for j, (px, py) in enumerate(peers):
                self._piece(ins, outs, ss, rs, k, j, (px, py), me, 2 * px + py, c).wait_recv()
                self._piece(ins, outs, ss, rs, k, j, (px, py), 2 * px + py, me, c).wait_send()
        if self.small:
            for r, (px, py, pc) in enumerate(self._others(x, y, c)):
                self._small(ins, outs, ss, rs, r, (px, py, pc), 4 * px + 2 * py + pc).wait_recv()
                self._small(ins, outs, ss, rs, r, (px, py, pc), 4 * x + 2 * y + c).wait_send()


class _PresumThenSend:
    def __init__(self, names):
        self.send = _SendPartials(names)
        self.pre = _Presum(names[-1:], base=self.send.n_sems)
        self.n = self.send.n
        self.n_sems = self.send.n_sems + self.pre.n_sems
        self.out_shape = self.send.out_shape
        self.work_shape = list(self.pre.out_shape) + list(self.pre.work_shape)
        self.in_space = [VMEM_WHOLE] * (self.n - 1) + [ANY]

    def _partials(self, ins, bufs):
        return list(ins[:self.n - 1]) + [bufs[self.n]]

    def first(self, ins, bufs, ss, rs):
        self.pre.first(ins[self.n - 1:], bufs[self.n:], ss, rs)
        self.send.first(ins, bufs[:self.n], ss, rs, only=range(self.n - 1))

    def early(self, ins, bufs, ss, rs):
        self.pre.last(ins[self.n - 1:], bufs[self.n:], ss, rs)
        self.send.first(self._partials(ins, bufs), bufs[:self.n], ss, rs, only=(self.n - 1,))

    def middle(self, ins, bufs, ss, rs):
        pass

    def last(self, ins, bufs, ss, rs):
        self.send.last(self._partials(ins, bufs), bufs[:self.n], ss, rs)


def _sum_swap(names, parts, small):
    n = len(parts)
    everyone = _SendPartials((), small.shape)

    def body(*refs):
        p_refs, (small_ref,), o_refs, (osmall_ref,), (all_ref,), (send_sems, recv_sems, ss_small, rs_small) = _split(
            refs, n, 1, n, 1, 1, 4)
        x, y, c = lax.axis_index("x"), lax.axis_index("y"), lax.axis_index("c")
        everyone.first([small_ref], [all_ref], ss_small, rs_small)

        def mine(k):
            part = _half(o_refs[k], c, names[k])
            return _remote(part, part, send_sems, recv_sems, k, (x, y, 1 - c))

        for k in range(n):
            for e in range(2):
                @pl.when(c == e)
                def _():
                    g = p_refs[k][0].astype(F32)
                    for s in range(1, N_CHIP):
                        g = g + p_refs[k][s].astype(F32)
                    r, cols = _half_shape(names[k])
                    if _BIG_SPLIT[names[k]] == 0:
                        o_refs[k][e * r:(e + 1) * r, :] = g
                    else:
                        o_refs[k][:, e * cols:(e + 1) * cols] = g
            mine(k).start()
        for k in range(n):
            theirs = _half(o_refs[k], 1 - c, names[k])
            _remote(theirs, theirs, send_sems, recv_sems, k, (x, y, 1 - c)).wait_recv()
            mine(k).wait_send()
        everyone.last([small_ref], [all_ref], ss_small, rs_small)
        g = all_ref[0]
        for d in range(1, N_DEV):
            g = g + all_ref[d]
        osmall_ref[...] = g

    res = pl.pallas_call(
        body, in_specs=[VMEM_WHOLE] * (n + 1), out_specs=[VMEM_WHOLE] * (n + 1),
        out_shape=[jax.ShapeDtypeStruct(_BIG_SHARD[nm], F32) for nm in names] + [jax.ShapeDtypeStruct(small.shape, F32)],
        scratch_shapes=[pltpu.VMEM((N_DEV,) + small.shape, F32), pltpu.SemaphoreType.DMA((n,)), pltpu.SemaphoreType.DMA((n,)),
                        pltpu.SemaphoreType.DMA((everyone.n_sems,)), pltpu.SemaphoreType.DMA((everyone.n_sems,))],
        compiler_params=_params(), name="sum_swap",
    )(*parts, small)
    return res[:n], res[n]


def _tile(rows, cols, itemsize, budget):
    t = cols if rows % 16 else rows
    other = rows if rows % 16 else cols
    step = 256 if rows % 16 else 32
    while t % step == 0 and t * other * itemsize > budget:
        t //= 2
    return (rows, t) if rows % 16 else (t, cols)


def _adamw_math(w, g, m, v):
    m = ADAM_B1 * m + (1.0 - ADAM_B1) * g
    v = ADAM_B2 * v + (1.0 - ADAM_B2) * (g * g)
    m_hat = m / (1.0 - ADAM_B1 ** ADAM_STEP)
    v_hat = v / (1.0 - ADAM_B2 ** ADAM_STEP)
    delta = -ADAM_LR * (m_hat / (jnp.sqrt(v_hat) + ADAM_EPS) + ADAM_WD * w)
    return delta, m, v


def _adamw_big(g, w, m, v, name):
    r, c = w.shape
    tr, tc = _tile(r, c, 4, 2 * 1024 * 1024)

    def body(g_ref, w_ref, m_ref, v_ref, d_ref, nm_ref, nv_ref):
        d_ref[...], nm_ref[...], nv_ref[...] = _adamw_math(w_ref[...], g_ref[...], m_ref[...], v_ref[...])

    blk = pl.BlockSpec((tr, tc), lambda i, l: (i, l))
    return pl.pallas_call(
        body, grid=(r // tr, c // tc), in_specs=[blk, blk, blk, blk],
        out_specs=[blk, blk, blk], out_shape=[jax.ShapeDtypeStruct((r, c), F32)] * 3,
        compiler_params=_params(("arbitrary", "arbitrary")), name=name,
    )(g, w, m, v)


def _adamw_rows(g, w, m, v, name):
    r, k, lanes = w.shape
    tr = 296

    def body(g_ref, w_ref, m_ref, v_ref, g3_ref, d_ref, nm_ref, nv_ref):
        g = g_ref[...].reshape(tr, k, lanes)
        g3_ref[...] = g
        d_ref[...], nm_ref[...], nv_ref[...] = _adamw_math(w_ref[...], g, m_ref[...], v_ref[...])

    rows = pl.BlockSpec((tr, k, lanes), lambda i: (i, 0, 0))
    return pl.pallas_call(
        body, grid=(pl.cdiv(r, tr),), in_specs=[pl.BlockSpec((tr, k * lanes), lambda i: (i, 0)), rows, rows, rows],
        out_specs=[rows] * 4, out_shape=[jax.ShapeDtypeStruct((r, k, lanes), F32)] * 4,
        compiler_params=_params(("arbitrary",)), name=name,
    )(g, w, m, v)


def _adamw_small(ws, gs, ms, vs):
    n = len(ws)

    def body(*refs):
        w_refs, g_refs, m_refs, v_refs, d_refs, nm_refs, nv_refs = _split(refs, *([n] * 7))
        for k in range(n):
            d_refs[k][...], nm_refs[k][...], nv_refs[k][...] = _adamw_math(w_refs[k][...], g_refs[k][...], m_refs[k][...],
                                                                             v_refs[k][...])

    shapes = [jax.ShapeDtypeStruct(w.shape, F32) for w in ws]
    res = pl.pallas_call(body, out_shape=shapes * 3, name="adamw_small")(*ws, *gs, *ms, *vs)
    return res[:n], res[n:2 * n], res[2 * n:]


def _pack(arrs):
    flat = jnp.concatenate([a.reshape(-1) for a in arrs])
    rows = -(-flat.shape[0] // 1024) * 8
    return jnp.pad(flat, (0, rows * 128 - flat.shape[0])).reshape(rows, 128)


def _unpack(buf, shapes):
    flat = buf.reshape(-1)
    out, off = [], 0
    for s in shapes:
        size = 1
        for d in s:
            size *= d
        out.append(flat[off:off + size].reshape(s))
        off += size
    return out


def _block_rows(w):
    return jnp.pad(w.reshape(512, 4), ((0, 0), (0, 124)))


def _cols(a4):
    return jnp.transpose(a4, (1, 0, 2)).reshape(a4.shape[1], -1)


_LATE = ("w_pa", "w_pb", "w_o", "w_up", "w_down")
_RIDE_IN_PROJ = ("w_pa", "w_pb", "w_o")
_RIDE_MIXER = ("w_up", "w_down_a")
_RIDE_MERGE = ("w_down_b",)


def _full_weights(gathered):
    joined = {"w_in": (D_IN, D_MODEL), "w_o": (D_MODEL, D_MODEL)}
    return {n: (a.reshape(joined[n]) if n in joined else a) for n, a in gathered.items()}


def _local_step(x, target, w, sp, late_shards=None):
    sp = {n: (a.reshape(1, -1) if a.ndim == 1 else a) for n, a in sp.items()}
    wau = jnp.zeros((128, 256), F32).at[0:16].set(sp["w_a_up"])
    wif = jnp.zeros((1536, 128), F32).at[:, 0:8].set(sp["w_if"])
    bif = jnp.zeros((1, 128), F32).at[:, 0:8].set(sp["b_if"])
    p = {"wau": wau, "bau": sp["b_a_up"], "ggla": sp["g_gla_norm"], "cw": sp["conv_w"], "cb": sp["conv_b"],
         "wq": _block_rows(sp["w_q_ml"]), "wk": _block_rows(sp["w_k_ml"]), "wv": _block_rows(sp["w_v_ml"]),
         "wif": wif, "bif": bif, "skip": sp["ml_skip"], "gml": sp["g_ml_norm"]}

    if late_shards is None:
        (pm, gab, h), _ = _in_proj(x, sp["g_pre_mix"], w["w_in"])
        ab, *states = _mixer_fwd(pm, p)
        (x1, mix, merged), _ = _merge_fwd(ab, gab, x, w["w_pa"], w["w_pb"], w["w_o"], sp["g_post_mix"])
    else:
        shard = dict(zip(_LATE, late_shards))
        shard["w_down_a"], shard["w_down_b"] = shard["w_down"][0:512], shard["w_down"][512:1024]
        (pm, gab, h), got = _in_proj(x, sp["g_pre_mix"], w["w_in"], _Gather(_RIDE_IN_PROJ, middle_at=0.45),
                                     [shard[n] for n in _RIDE_IN_PROJ])
        w = dict(w, **_full_weights(dict(zip(_RIDE_IN_PROJ, got))))
        ab, *rest = _mixer_fwd(pm, p, _Gather(_RIDE_MIXER, middle_at=0.62), [shard[n] for n in _RIDE_MIXER])
        states = rest[:4]
        w.update(_full_weights(dict(zip(_RIDE_MIXER, rest[4:]))))
        (x1, mix, merged), got = _merge_fwd(ab, gab, x, w["w_pa"], w["w_pb"], w["w_o"], sp["g_post_mix"],
                                            _Gather(_RIDE_MERGE, middle_at=0.46), [shard[n] for n in _RIDE_MERGE])
        w.update(_full_weights(dict(zip(_RIDE_MERGE, got))))
    dx1, u, dd, h2, dpre, dg_post_mlp, dg_pre_mlp, loss = _mlp(x1, target, sp["g_pre_mlp"], sp["g_post_mlp"],
                                                                w["w_up"], w["w_down_a"], w["w_down_b"])
    dmix, dya, dyb, dgab, dab, dg_post_mix = _merge_bwd(dx1, mix, ab, gab, w["w_pa"], w["w_pb"], w["w_o"], sp["g_post_mix"])
    big = {
        "w_pa": _tn_matmul(ab[:, 0:512], dya, "dw_pa", shards=N_CHIP),
        "w_pb": _tn_matmul(ab[:, 512:1024], dyb, "dw_pb", shards=N_CHIP),
        "w_o": _tn_matmul(merged, dmix, "dw_o"),
        "w_up": _tn_matmul(h2, dpre, "dw_up", shards=N_CHIP),
    }
    if late_shards is None:
        big["w_down"] = _tn_matmul(u, dd, "dw_down")
        dpm, dp, _ = _mixer_bwd(pm, dab, states, p)
    else:
        pieces = lambda n: big[n].reshape((N_CHIP,) + _BIG_SHARD[n])
        big["w_down"], partial = _tn_matmul(u, dd, "dw_down", rider=_Presum(_LATE[:4]),
                                            rider_ins=[pieces(n) for n in _LATE[:4]])
        dpm, dp, parts = _mixer_bwd(pm, dab, states, p, _PresumThenSend(_LATE), list(partial) + [pieces("w_down")])
        big = dict(zip(_LATE, parts))
    big["w_in"] = _dw_in(dpm, dgab, h)
    if late_shards is None:
        (dx, dg_pre_mix), _ = _in_proj_bwd(dpm, dgab, x, dx1, sp["g_pre_mix"], w["w_in"])
    else:
        (dx, dg_pre_mix), parts = _in_proj_bwd(dpm, dgab, x, dx1, sp["g_pre_mix"], w["w_in"], _PresumThenRelay(("w_in",)),
                                               [big["w_in"].reshape((N_CHIP,) + _BIG_SHARD["w_in"])])
        big["w_in"] = parts[0]
    small = {
        "g_pre_mix": dg_pre_mix, "b_a_up": dp["bau"], "g_gla_norm": dp["ggla"], "conv_b": dp["cb"],
        "w_q_ml": dp["wq"][:, 0:4].reshape(128, 4, 4), "w_k_ml": dp["wk"][:, 0:4].reshape(128, 4, 4),
        "w_v_ml": dp["wv"][:, 0:4].reshape(128, 4, 4),
        "b_if": dp["bif"][:, 0:8], "ml_skip": dp["skip"], "g_ml_norm": dp["gml"], "g_post_mix": dg_post_mix,
        "g_pre_mlp": dg_pre_mlp, "g_post_mlp": dg_post_mlp, "w_a_up": dp["wau"][0:16], "conv_w": dp["cw"],
        "w_if": dp["wif"][:, 0:8], "loss": loss[:, 0:1],
    }
    return dx, big, small


_SMALL_REPL = ("g_pre_mix", "b_a_up", "g_gla_norm", "conv_b", "w_q_ml", "w_k_ml", "w_v_ml", "b_if", "ml_skip",
               "g_ml_norm", "g_post_mix", "g_pre_mlp", "g_post_mlp")
_SMALL_SHARDED = ("w_a_up", "conv_w", "w_if")
_SMALL_ORDER = _SMALL_REPL + _SMALL_SHARDED + ("loss",)
_WEIGHTS = ("g_pre_mix", "w_in", "w_a_up", "b_a_up", "g_gla_norm", "conv_w", "conv_b", "w_q_ml", "w_k_ml", "w_v_ml",
            "w_if", "b_if", "ml_skip", "g_ml_norm", "w_pa", "w_pb", "w_o", "g_post_mix", "g_pre_mlp", "w_up", "w_down",
            "g_post_mlp")


_BLOCK_WEIGHTS = ("w_q_ml", "w_k_ml", "w_v_ml")


def _stored(name, a):
    if name in _BLOCK_WEIGHTS:
        return jnp.transpose(a, (0, 2, 3, 1)).reshape(16, 128)
    if name == "w_if":
        return jnp.transpose(a, (0, 2, 1)).reshape(8, 384)
    return a


def _unstored(name, a):
    if name in _BLOCK_WEIGHTS:
        return jnp.transpose(a.reshape(1, 4, 4, 128), (0, 3, 1, 2))
    if name == "w_if":
        return jnp.transpose(a.reshape(1, 8, 384), (0, 2, 1))
    return a


def _as_shard(name, a):
    return jnp.transpose(a, (2, 0, 1)).reshape(IN_SHARD, D_MODEL // 128, 128) if name == "w_in" else a[0]


def _from_shard(name, a):
    return jnp.transpose(a, (1, 2, 0)).reshape(1, D_MODEL, IN_SHARD) if name == "w_in" else a[None]


def kernel(x, g_pre_mix, w_in, w_a_up, b_a_up, g_gla_norm, conv_w, conv_b, w_q_ml, w_k_ml, w_v_ml, w_if, b_if, ml_skip, g_ml_norm, w_pa, w_pb, w_o, g_post_mix, g_pre_mlp, w_up, w_down, g_post_mlp, loss_target, m_g_pre_mix, m_w_in, m_w_a_up, m_b_a_up, m_g_gla_norm, m_conv_w, m_conv_b, m_w_q_ml, m_w_k_ml, m_w_v_ml, m_w_if, m_b_if, m_ml_skip, m_g_ml_norm, m_w_pa, m_w_pb, m_w_o, m_g_post_mix, m_g_pre_mlp, m_w_up, m_w_down, m_g_post_mlp, v_g_pre_mix, v_w_in, v_w_a_up, v_b_a_up, v_g_gla_norm, v_conv_w, v_conv_b, v_w_q_ml, v_w_k_ml, v_w_v_ml, v_w_if, v_b_if, v_ml_skip, v_g_ml_norm, v_w_pa, v_w_pb, v_w_o, v_g_post_mix, v_g_pre_mlp, v_w_up, v_w_down, v_g_post_mlp):
    args = dict(locals())
    wts = {n: _as_shard(n, args[n]) for n in _WEIGHTS}
    mom = {n: _as_shard(n, args["m_" + n]) for n in _WEIGHTS}
    var = {n: _as_shard(n, args["v_" + n]) for n in _WEIGHTS}
    chip = 2 * lax.axis_index("x") + lax.axis_index("y")

    first = ("w_in",) + _SMALL_SHARDED
    gathered = dict(zip(first, _run_alone(_Gather(("w_in",), [wts[n] for n in _SMALL_SHARDED]),
                                          [wts[n].reshape(IN_SHARD, D_MODEL).astype(BF16) if n == "w_in" else wts[n]
                                           for n in first],
                                          "gather_first")))
    sp = {n: wts[n] for n in _SMALL_REPL}
    sp["w_a_up"] = _cols(gathered["w_a_up"])
    sp["conv_w"] = _cols(gathered["conv_w"])
    sp["w_if"] = gathered["w_if"].reshape(1536, 8)

    dx, big, small = _local_step(x[0], loss_target[0], _full_weights({"w_in": gathered["w_in"]}), sp,
                                 late_shards=[wts[n].astype(BF16) for n in _LATE])

    small_shapes = [small[n].shape for n in _SMALL_ORDER]
    packed = _pack([small[n] for n in _SMALL_ORDER])
    sums, small_sum = _sum_swap(_BIG, [big[n] for n in _BIG], packed)

    grads, delta, new_m, new_v = {}, {}, {}, {}
    for n, g in zip(_BIG, sums):
        if n == "w_in":
            g, d, nm, nv = _adamw_rows(g, wts[n], mom[n], var[n], "adamw_" + n)
        else:
            d, nm, nv = _adamw_big(g, wts[n], mom[n], var[n], "adamw_" + n)
        grads[n], delta[n], new_m[n], new_v[n] = (_from_shard(n, a) for a in (g, d, nm, nv))
    summed = dict(zip(_SMALL_ORDER, _unpack(small_sum, small_shapes)))
    loss = summed["loss"].reshape(())
    summed["w_a_up"] = lax.dynamic_slice_in_dim(summed["w_a_up"], chip * 64, 64, axis=1)
    summed["conv_w"] = lax.dynamic_slice_in_dim(summed["conv_w"], chip * 128, 128, axis=1)
    summed["w_if"] = lax.dynamic_slice_in_dim(summed["w_if"], chip * 384, 384, axis=0)
    small_names = _SMALL_REPL + _SMALL_SHARDED
    g_stored = [_stored(n, summed[n].reshape(args[n].shape)) for n in small_names]
    upd = _adamw_small([_stored(n, args[n]) for n in small_names], g_stored,
                       [_stored(n, args["m_" + n]) for n in small_names], [_stored(n, args["v_" + n]) for n in small_names])
    for dst, arrs in zip((grads, delta, new_m, new_v), (g_stored,) + tuple(upd)):
        dst.update({n: _unstored(n, a) for n, a in zip(small_names, arrs)})

    outs = [loss, dx[None]]
    for group in (grads, delta, new_m, new_v):
        outs += [group[n] for n in _WEIGHTS]
    return tuple(outs)
```

```python
import functools

import jax
import jax.numpy as jnp
from jax import lax
from jax.experimental import pallas as pl
from jax.experimental.pallas import tpu as pltpu

F32 = jnp.float32
BF16 = jnp.bfloat16

SEQ = 2048
D_MODEL = 1024
CHUNK = 64
N_CHUNK = SEQ // CHUNK
HEADS = 4
GLA_DK = 64
GLA_DV = 128
ML_DH = 128
D_FF = 4096
EPS = 1e-6
N_CHIP = 4
N_DEV = 8
TOK_TILE = 256
N_TOK_TILE = SEQ // TOK_TILE
SWEEP = 2
assert CHUNK == 64
N_SWEEP = N_CHUNK // SWEEP

PM_W = 2688
PM_XM = 1536
PM_OP = 2048
PM_AL = 2560
GAB_W = 2048
D_IN = 4624
IN_SHARD = D_IN // N_CHIP
IN_ALOW = 1536
IN_XM = 1552
IN_GATES = 2576

ADAM_LR = 0.001
ADAM_B1 = 0.9
ADAM_B2 = 0.999
ADAM_EPS = 1e-08
ADAM_WD = 0.01
ADAM_STEP = 10

VMEM_LIMIT = 56 * 1024 * 1024


def _params(sem=None):
    return pltpu.CompilerParams(dimension_semantics=sem, vmem_limit_bytes=VMEM_LIMIT)


def _dot(a, b, ca, cb):
    return lax.dot_general(a.astype(BF16), b.astype(BF16), (((ca,), (cb,)), ((), ())), preferred_element_type=F32)


def _pmm_nn(a, b):
    return _dot(a, b, 1, 0)


def _pmm_nt(a, b):
    return _dot(a, b, 1, 1)


def _pmm_tn(a, b):
    return _dot(a, b, 0, 0)


def _pcmm(c, x):
    return lax.dot_general(c, x, (((1,), (0,)), ((), ())), precision=lax.Precision.HIGHEST, preferred_element_type=F32)


@jax.custom_vjp
def _mm_nn(a, b):
    return _dot(a, b, 1, 0)


@jax.custom_vjp
def _mm_nt(a, b):
    return _dot(a, b, 1, 1)


@jax.custom_vjp
def _mm_tn(a, b):
    return _dot(a, b, 0, 0)


_mm_nn.defvjp(lambda a, b: (_dot(a, b, 1, 0), (a, b)), lambda r, g: (_mm_nt(g, r[1]), _mm_tn(r[0], g)))
_mm_nt.defvjp(lambda a, b: (_dot(a, b, 1, 1), (a, b)), lambda r, g: (_mm_nn(g, r[1]), _mm_tn(g, r[0])))
_mm_tn.defvjp(lambda a, b: (_dot(a, b, 0, 0), (a, b)), lambda r, g: (_mm_nt(r[1], g), _mm_nn(r[0], g)))


@jax.custom_vjp
def _cmm(c, x):
    return _pcmm(c, x)


_cmm.defvjp(
    lambda c, x: (_pcmm(c, x), c),
    lambda c, g: (jnp.zeros_like(c), lax.dot_general(c, g, (((0,), (0,)), ((), ())), precision=lax.Precision.HIGHEST,
                                                      preferred_element_type=F32)),
)

_PLAIN_OPS = (_pmm_nn, _pmm_nt, _pmm_tn, _pcmm)
_VJP_OPS = (_mm_nn, _mm_nt, _mm_tn, _cmm)


def _sigmoid(x):
    return 0.5 * (jnp.tanh(0.5 * x) + 1.0)


def _log_sigmoid(x):
    return jnp.minimum(x, 0.0) - jnp.log(1.0 + jnp.exp(-jnp.abs(x)))


def _mean(x):
    return jnp.mean(x, axis=-1, keepdims=True)


def _nt(a, b):
    return lax.dot_general(a, b, (((1,), (1,)), ((), ())), preferred_element_type=F32)


def _tn(a, b):
    return lax.dot_general(a, b, (((0,), (0,)), ((), ())), preferred_element_type=F32)


def _mixer_chunk(ops, p, st, pm, xprev8):
    mm_nn, mm_nt, mm_tn, cmm = ops
    n_rows = pm.shape[0]
    n_ch = n_rows // CHUNK
    row = lax.broadcasted_iota(jnp.int32, (n_rows, n_rows), 0)
    col = lax.broadcasted_iota(jnp.int32, (n_rows, n_rows), 1)
    tri = jnp.logical_and((row >> 6) == (col >> 6), row >= col).astype(F32)
    causal = tri[0:CHUNK, 0:CHUNK] > 0.0
    q = pm[:, 0:256]
    k = pm[:, 256:512]
    v = pm[:, 512:1024]
    g = pm[:, 1024:1536]
    xm = pm[:, PM_XM:PM_XM + 512]
    opre = pm[:, PM_OP:PM_OP + 512]
    alow = pm[:, PM_AL:PM_AL + 128]
    hs = range(HEADS)
    cs = range(n_ch)
    pairs = [(i, h) for i in cs for h in hs]
    rs = [slice(i * CHUNK, (i + 1) * CHUNK) for i in cs]
    last = [slice((i + 1) * CHUNK - 1, (i + 1) * CHUNK) for i in cs]
    s6 = [slice(h * GLA_DK, (h + 1) * GLA_DK) for h in hs]
    s12 = [slice(h * 128, (h + 1) * 128) for h in hs]

    xx = jnp.concatenate([xprev8, xm], axis=0)
    pre = p["cb"]
    for j in range(4):
        pre = pre + p["cw"][j:j + 1, :] * xx[5 + j:5 + j + n_rows, :]
    xc = pre * _sigmoid(pre)
    qm = [mm_nn(xc[:, s12[h]], p["wq"][h]) for h in hs]
    km = [mm_nn(xc[:, s12[h]], p["wk"][h]) for h in hs]
    vm = [mm_nn(xm[:, s12[h]], p["wv"][h]) for h in hs]
    qcat = jnp.concatenate(qm, axis=1)
    kcat = jnp.concatenate(km, axis=1)
    vcat = jnp.concatenate(vm, axis=1)
    gates = (mm_nn(qcat, p["wif"][0:512]) + mm_nn(kcat, p["wif"][512:1024]) + mm_nn(vcat, p["wif"][1024:1536])
             + p["bif"])
    lf = _log_sigmoid(gates)
    fc = cmm(tri, lf)
    gates_t = gates.T
    fc_t = fc.T

    la = _log_sigmoid(mm_nn(alow, p["wau"]) + p["bau"]) * (1.0 / 16.0)
    cum = cmm(tri, la)
    cum_last = [cum[last[i], :] for i in cs]
    to_end = jnp.concatenate([cum_last[i] - cum[rs[i], :] for i in cs], axis=0)
    e_pos = jnp.exp(cum)
    e_neg = jnp.exp(-cum)
    qs = q * (GLA_DK ** -0.5)
    qp = qs * e_pos
    qn = qs * e_neg
    kp = k * e_pos
    kn = k * e_neg
    kl = k * jnp.exp(to_end)
    dec = [jnp.exp(cum_last[i]) for i in cs]
    ks = [km[h] * (ML_DH ** -0.5) for h in hs]
    li_c = {(i, h): gates[rs[i], h:h + 1] for i, h in pairs}
    fc_c = {(i, h): fc[rs[i], 4 + h:5 + h] for i, h in pairs}
    f_last = {(i, h): fc[last[i], 4 + h:5 + h] for i, h in pairs}

    a_fwd = {(i, h): mm_nt(qp[rs[i], s6[h]], kn[rs[i], s6[h]]) for i, h in pairs}
    a_bwd = {(i, h): mm_nt(qn[rs[i], s6[h]], kp[rs[i], s6[h]]) for i, h in pairs}
    s_chunk = {(i, h): mm_tn(v[rs[i], s12[h]], kl[rs[i], s6[h]]) for i, h in pairs}
    qk = {(i, h): mm_nt(qm[h][rs[i]], ks[h][rs[i]]) for i, h in pairs}
    a = {ih: f_last[ih] - fc_c[ih] + li_c[ih] for ih in pairs}
    m_loc = {ih: jnp.max(a[ih], axis=0, keepdims=True) for ih in pairs}
    kw = {(i, h): ks[h][rs[i]] * jnp.exp(a[(i, h)] - m_loc[(i, h)]) for i, h in pairs}
    c_chunk = {(i, h): mm_tn(kw[(i, h)], vm[h][rs[i]]) for i, h in pairs}
    mem = {(0, h): st["S"][h] for h in hs}
    c_in = {(0, h): st["C"][h] for h in hs}
    n_in = {(0, h): st["n"][h] for h in hs}
    m_in = {(0, h): st["m"][h][:, 0:1] for h in hs}
    for i, h in pairs:
        mem[(i + 1, h)] = mem[(i, h)] * dec[i][:, s6[h]] + s_chunk[(i, h)]
        m_nx = jnp.maximum(f_last[(i, h)] + m_in[(i, h)], m_loc[(i, h)])
        sp = jnp.exp(f_last[(i, h)] + m_in[(i, h)] - m_nx)
        sl = jnp.exp(m_loc[(i, h)] - m_nx)
        c_in[(i + 1, h)] = sp * c_in[(i, h)] + sl * c_chunk[(i, h)]
        n_in[(i + 1, h)] = sp * n_in[(i, h)] + sl * jnp.sum(kw[(i, h)], axis=0, keepdims=True)
        m_in[(i + 1, h)] = m_nx
    s_new = [mem[(n_ch, h)] for h in hs]
    o_inter = {(i, h): mm_nt(qp[rs[i], s6[h]], mem[(i, h)]) for i, h in pairs}
    q_c = {(i, h): mm_nn(qm[h][rs[i]], c_in[(i, h)]) for i, h in pairs}
    scores = {ih: jnp.where(causal, a_fwd[ih], a_bwd[ih]) for ih in pairs}
    log_d = {(i, h): gates_t[h:h + 1, rs[i]] - jnp.abs(fc_c[(i, h)] - fc_t[4 + h:5 + h, rs[i]]) for i, h in pairs}
    g_int = {ih: fc_c[ih] + m_in[ih] for ih in pairs}
    m_t = {ih: jnp.maximum(g_int[ih], jnp.max(log_d[ih], axis=1, keepdims=True)) for ih in pairs}
    s = {ih: qk[ih] * jnp.exp(log_d[ih] - m_t[ih]) for ih in pairs}
    scl = {ih: jnp.exp(g_int[ih] - m_t[ih]) for ih in pairs}
    o = {(i, h): mm_nn(scores[(i, h)], v[rs[i], s12[h]]) + o_inter[(i, h)] for i, h in pairs}
    num = {(i, h): mm_nn(s[(i, h)], vm[h][rs[i]]) + scl[(i, h)] * q_c[(i, h)] for i, h in pairs}
    o = {ih: o[ih] * lax.rsqrt(_mean(o[ih] * o[ih]) + EPS) * p["ggla"] for ih in pairs}
    gate = g * _sigmoid(g)
    out_a = {(i, h): o[(i, h)] * gate[rs[i], s12[h]] for i, h in pairs}
    den = {(i, h): jnp.sum(s[(i, h)], axis=1, keepdims=True)
           + scl[(i, h)] * jnp.sum(qm[h][rs[i]] * n_in[(i, h)], axis=1, keepdims=True) for i, h in pairs}
    den = {ih: jnp.maximum(jnp.abs(den[ih]), jnp.exp(-m_t[ih])) for ih in pairs}
    open_gate = _sigmoid(opre)
    hc = {(i, h): num[(i, h)] / den[(i, h)] * open_gate[rs[i], s12[h]] for i, h in pairs}
    d0 = {ih: hc[ih] - _mean(hc[ih]) for ih in pairs}
    y = {ih: d0[ih] * lax.rsqrt(_mean(d0[ih] * d0[ih]) + EPS) for ih in pairs}
    skipped = p["skip"] * xc
    out_b = {(i, h): y[(i, h)] * p["gml"][:, s12[h]] + skipped[rs[i], s12[h]] for i, h in pairs}
    ab = jnp.concatenate([jnp.concatenate([out_a[(i, h)] for h in hs] + [out_b[(i, h)] for h in hs], axis=1) for i in cs],
                         axis=0)
    new = {"S": s_new, "C": [c_in[(n_ch, h)] for h in hs], "n": [n_in[(n_ch, h)] for h in hs],
           "m": [jnp.broadcast_to(m_in[(n_ch, h)], (1, ML_DH)) for h in hs]}
    return ab, new


_P_NAMES = ("wau", "bau", "ggla", "cw", "cb", "wq", "wk", "wv", "wif", "bif", "skip", "gml")
_P_SHAPES = {
    "wau": (128, 256), "bau": (1, 256), "ggla": (1, 128), "cw": (4, 512), "cb": (1, 512),
    "wq": (512, 128), "wk": (512, 128), "wv": (512, 128),
    "wif": (1536, 128), "bif": (1, 128), "skip": (1, 512), "gml": (1, 512),
}
_P_BLOCKDIAG = ("wq", "wk", "wv")
_S_NAMES = ("S", "C", "n", "m")
_S_SHAPES = {"S": (HEADS, GLA_DV, GLA_DK), "C": (HEADS, ML_DH, ML_DH), "n": (HEADS, 1, ML_DH), "m": (HEADS, 1, ML_DH)}


def _per_head(ref):
    return [ref[h] for h in range(HEADS)]


def _block_mask():
    r = lax.broadcasted_iota(jnp.int32, (128, 128), 0)
    c = lax.broadcasted_iota(jnp.int32, (128, 128), 1)
    same_block = (r >> 2) == (c >> 2)
    spread = jnp.logical_and(r < 4, (c & 3) == r)
    return same_block.astype(F32), spread.astype(F32)


def _expand_blockdiag(w_ref, dense_ref):
    same_block, spread = _block_mask()
    for h in range(HEADS):
        tiled = _pmm_nn(w_ref[h * 128:(h + 1) * 128, :], spread)
        dense_ref[h] = tiled * same_block


def _collect_blockdiag(ddense_ref, dw_ref):
    same_block, spread = _block_mask()
    for h in range(HEADS):
        dw_ref[h * 128:(h + 1) * 128, :] = lax.dot_general(
            ddense_ref[h] * same_block, spread, (((1,), (1,)), ((), ())), precision=lax.Precision.HIGHEST,
            preferred_element_type=F32)


def _const_spec(shape):
    zeros = (0,) * len(shape)
    return pl.BlockSpec(shape, lambda i: zeros)


def _split(refs, *counts):
    out, at = [], 0
    for c in counts:
        out.append(refs[at:at + c])
        at += c
    assert at == len(refs)
    return out


def _ride(rider, phases, cond, ins, outs, sems):
    if rider is None or not any(hasattr(rider, phase) for phase in phases):
        return
    lands, (send_sems, recv_sems, flush_sems) = sems[:-3], sems[-3:]

    @pl.when(cond)
    def _():
        for phase in phases:
            getattr(rider, phase)(ins, lands, send_sems, recv_sems)
            if hasattr(rider, "flush"):
                rider.flush(phase, lands, outs, flush_sems)
        if "last" in phases and not hasattr(rider, "flush"):
            flush = [pltpu.make_async_copy(lands[k], outs[k], flush_sems.at[k]) for k in range(len(outs))]
            for cp in flush:
                cp.start()
            for cp in flush:
                cp.wait()


def _middle_step(rider, n_steps):
    return min(n_steps - 2, int(getattr(rider, "middle_at", 1.0) * n_steps))


def _rider_specs(rider, rider_ins):
    if rider is None:
        return [], [], [], []
    scratch = [pltpu.VMEM(s.shape, s.dtype) for s in list(rider.out_shape) + list(getattr(rider, "work_shape", ()))]
    scratch += [pltpu.SemaphoreType.DMA((rider.n_sems,)), pltpu.SemaphoreType.DMA((rider.n_sems,)),
                pltpu.SemaphoreType.DMA((getattr(rider, "n_flush", len(rider.out_shape)),))]
    in_space = getattr(rider, "in_space", VMEM_WHOLE)
    in_specs = list(in_space) if isinstance(in_space, (list, tuple)) else [in_space] * len(rider_ins)
    return in_specs, [ANY] * len(rider.out_shape), list(rider.out_shape), scratch


def _mixer_fwd(pm, p, rider=None, rider_ins=()):
    n_p = len(_P_NAMES)
    r_in, r_out_specs, r_out_shape, r_sems = _rider_specs(rider, rider_ins)

    def body(*refs):
        (pm_ref, xprev_ref), p_list, ride_in, (ab_ref,), so_refs, ride_out, sc_refs, dense_list, sems = _split(
            refs, 2, n_p, len(r_in), 1, 4, len(r_out_specs), 4, 3, len(r_sems))
        p_refs = dict(zip(_P_NAMES, p_list))
        dense = dict(zip(_P_BLOCKDIAG, dense_list))
        n = pl.program_id(0)
        _ride(rider, ("first",), n == 0, ride_in, ride_out, sems)

        @pl.when(n == 0)
        def _():
            for r in sc_refs:
                r[...] = jnp.zeros_like(r)
            for nm in _P_BLOCKDIAG:
                _expand_blockdiag(p_refs[nm], dense[nm])

        st = {name: _per_head(r) for name, r in zip(_S_NAMES, sc_refs)}
        pv = {nm: (_per_head(dense[nm]) if nm in _P_BLOCKDIAG else p_refs[nm][...]) for nm in _P_NAMES}
        for name, r in zip(_S_NAMES, so_refs):
            for h in range(HEADS):
                r[0, h] = st[name][h]
        xprev8 = jnp.where(n > 0, xprev_ref[CHUNK - 8:CHUNK, :], 0.0)
        ab, st = _mixer_chunk(_PLAIN_OPS, pv, st, pm_ref[...], xprev8)
        ab_ref[...] = ab.astype(BF16)
        for name, r in zip(_S_NAMES, sc_refs):
            for h in range(HEADS):
                r[h] = st[name][h]
        _ride(rider, ("middle",), n == _middle_step(rider, N_SWEEP), ride_in, ride_out, sems)
        _ride(rider, ("last",), n == N_SWEEP - 1, ride_in, ride_out, sems)

    in_specs = [pl.BlockSpec((SWEEP * CHUNK, PM_W), lambda i: (i, 0)),
                pl.BlockSpec((CHUNK, 512), lambda i: (jnp.maximum(SWEEP * i - 1, 0), PM_XM // 512))]
    in_specs += [_const_spec(_P_SHAPES[nm]) for nm in _P_NAMES] + r_in
    out_specs = [pl.BlockSpec((SWEEP * CHUNK, 1024), lambda i: (i, 0))]
    out_shape = [jax.ShapeDtypeStruct((SEQ, 1024), BF16)]
    for nm in _S_NAMES:
        shp = _S_SHAPES[nm]
        out_specs.append(pl.BlockSpec((1,) + shp, lambda i: (i, 0, 0, 0)))
        out_shape.append(jax.ShapeDtypeStruct((N_SWEEP,) + shp, F32))
    return pl.pallas_call(
        body, grid=(N_SWEEP,), in_specs=in_specs, out_specs=out_specs + r_out_specs, out_shape=out_shape + r_out_shape,
        scratch_shapes=[pltpu.VMEM(_S_SHAPES[nm], F32) for nm in _S_NAMES]
        + [pltpu.VMEM((HEADS, 128, 128), F32) for _ in _P_BLOCKDIAG] + r_sems,
        compiler_params=_params(("arbitrary",)), name="mixer_fwd",
    )(pm, pm, *[p[nm] for nm in _P_NAMES], *rider_ins)


def _mixer_bwd(pm, dab, states, p, rider=None, rider_ins=()):
    n_p = len(_P_NAMES)
    r_in, r_out_specs, r_out_shape, r_sems = _rider_specs(rider, rider_ins)

    def body(*refs):
        ((pm_ref, xprev_ref, dab_ref), si_refs, p_list, ride_in, (dpm_ref,), dp_list, ride_out, ds_refs, (carry_ref,),
         dense_list, ddense_list, sems) = _split(refs, 3, 4, n_p, len(r_in), 1, n_p, len(r_out_specs), 4, 1, 3, 3, len(r_sems))
        p_refs = dict(zip(_P_NAMES, p_list))
        dp_refs = dict(zip(_P_NAMES, dp_list))
        dense = dict(zip(_P_BLOCKDIAG, dense_list))
        ddense = dict(zip(_P_BLOCKDIAG, ddense_list))
        i = pl.program_id(0)
        blk = N_SWEEP - 1 - i
        _ride(rider, ("first",), i == 0, ride_in, ride_out, sems)

        @pl.when(i == 0)
        def _():
            for r in ds_refs:
                r[...] = jnp.zeros_like(r)
            for nm in _P_NAMES:
                if nm in _P_BLOCKDIAG:
                    ddense[nm][...] = jnp.zeros_like(ddense[nm])
                    _expand_blockdiag(p_refs[nm], dense[nm])
                else:
                    dp_refs[nm][...] = jnp.zeros_like(dp_refs[nm])
            carry_ref[...] = jnp.zeros_like(carry_ref)

        pv = {nm: (_per_head(dense[nm]) if nm in _P_BLOCKDIAG else p_refs[nm][...]) for nm in _P_NAMES}
        dst = {name: _per_head(r) for name, r in zip(_S_NAMES, ds_refs)}
        st = {name: [r[0, h] for h in range(HEADS)] for name, r in zip(_S_NAMES, si_refs)}
        xprev8 = jnp.where(blk > 0, xprev_ref[CHUNK - 8:CHUNK, :], 0.0)
        _, vjp = jax.vjp(functools.partial(_mixer_chunk, _VJP_OPS), pv, st, pm_ref[...], xprev8)
        dp_sum, dst, dpm, dxprev8 = vjp((dab_ref[...], dst))
        reach = jnp.concatenate([jnp.zeros((SWEEP * CHUNK - 8, 512), F32), carry_ref[...]], axis=0)
        dpm_ref[:, 0:PM_XM] = dpm[:, 0:PM_XM].astype(BF16)
        dpm_ref[:, PM_XM:PM_XM + 512] = (dpm[:, PM_XM:PM_XM + 512] + reach).astype(BF16)
        dpm_ref[:, PM_XM + 512:PM_W] = dpm[:, PM_XM + 512:PM_W].astype(BF16)
        carry_ref[...] = dxprev8
        for name, r in zip(_S_NAMES, ds_refs):
            for h in range(HEADS):
                r[h] = dst[name][h]
        for nm in _P_NAMES:
            if nm in _P_BLOCKDIAG:
                for h in range(HEADS):
                    ddense[nm][h] += dp_sum[nm][h]
            else:
                dp_refs[nm][...] += dp_sum[nm]

        @pl.when(i == N_SWEEP - 1)
        def _():
            for nm in _P_BLOCKDIAG:
                _collect_blockdiag(ddense[nm], dp_refs[nm])

        _ride(rider, ("early",), i == 1, ride_in, ride_out, sems)
        _ride(rider, ("middle",), i == _middle_step(rider, N_SWEEP), ride_in, ride_out, sems)
        _ride(rider, ("last",), i == N_SWEEP - 1, ride_in, ride_out, sems)

    rev = lambda i: (N_SWEEP - 1 - i, 0)
    in_specs = [pl.BlockSpec((SWEEP * CHUNK, PM_W), rev),
                pl.BlockSpec((CHUNK, 512), lambda i: (jnp.maximum(SWEEP * (N_SWEEP - 1 - i) - 1, 0), PM_XM // 512)),
                pl.BlockSpec((SWEEP * CHUNK, 1024), rev)]
    for nm in _S_NAMES:
        in_specs.append(pl.BlockSpec((1,) + _S_SHAPES[nm], lambda i: (N_SWEEP - 1 - i, 0, 0, 0)))
    in_specs += [_const_spec(_P_SHAPES[nm]) for nm in _P_NAMES] + r_in
    out_specs = [pl.BlockSpec((SWEEP * CHUNK, PM_W), rev)] + [_const_spec(_P_SHAPES[nm]) for nm in _P_NAMES]
    out_shape = [jax.ShapeDtypeStruct((SEQ, PM_W), BF16)] + [jax.ShapeDtypeStruct(_P_SHAPES[nm], F32) for nm in _P_NAMES]
    res = pl.pallas_call(
        body, grid=(N_SWEEP,), in_specs=in_specs, out_specs=out_specs + r_out_specs, out_shape=out_shape + r_out_shape,
        scratch_shapes=[pltpu.VMEM(_S_SHAPES[nm], F32) for nm in _S_NAMES] + [pltpu.VMEM((8, 512), F32)]
        + [pltpu.VMEM((HEADS, 128, 128), F32) for _ in range(2 * len(_P_BLOCKDIAG))] + r_sems,
        compiler_params=_params(("arbitrary",)), name="mixer_bwd",
    )(pm, pm, dab, *states, *[p[nm] for nm in _P_NAMES], *rider_ins)
    return res[0], dict(zip(_P_NAMES, res[1:1 + n_p])), res[1 + n_p:]


def _tok(width):
    return pl.BlockSpec((TOK_TILE, width), lambda i: (i, 0))


def _once(shape):
    zeros = (0,) * len(shape)
    return pl.BlockSpec(shape, lambda i: zeros, pipeline_mode=pl.Buffered(1))


def _rms_fwd(x):
    r = lax.rsqrt(_mean(x * x) + EPS)
    return x * r, r


def _rms_bwd(dy, xn, r, g):
    gd = dy * g
    return r * (gd - xn * _mean(xn * gd))


def _tiled_call(body, in_specs, out_specs, out_shape, args, name, rider=None, rider_ins=()):
    r_in, r_out_specs, r_out_shape, r_scratch = _rider_specs(rider, rider_ins)
    n_in, n_out = len(in_specs), len(out_specs)

    def hosted(*refs):
        ins, ride_in, outs, ride_out, scratch = _split(refs, n_in, len(r_in), n_out, len(r_out_specs), len(r_scratch))
        i = pl.program_id(0)
        _ride(rider, ("first",), i == 0, ride_in, ride_out, scratch)
        body(*ins, *outs)
        _ride(rider, ("early",), i == 1, ride_in, ride_out, scratch)
        _ride(rider, ("middle",), i == _middle_step(rider, N_TOK_TILE), ride_in, ride_out, scratch)
        _ride(rider, ("last",), i == N_TOK_TILE - 1, ride_in, ride_out, scratch)

    res = pl.pallas_call(
        hosted, grid=(N_TOK_TILE,), in_specs=list(in_specs) + r_in, out_specs=list(out_specs) + r_out_specs,
        out_shape=list(out_shape) + r_out_shape, scratch_shapes=r_scratch,
        compiler_params=_params(("arbitrary",)), name=name,
    )(*args, *rider_ins)
    return res[:n_out], res[n_out:]


def _in_proj(x, g_pre, wt_in, rider=None, rider_ins=()):
    def body(x_ref, g_ref, wt_ref, pm_ref, gab_ref, h_ref):
        xn, _ = _rms_fwd(x_ref[...])
        h = (xn * g_ref[...]).astype(BF16)
        h_ref[...] = h
        pm_ref[:, 0:PM_XM] = _nt(h, wt_ref[0:IN_ALOW, :])
        pm_ref[:, PM_XM:PM_AL] = _nt(h, wt_ref[IN_XM:IN_GATES, :])
        pm_ref[:, PM_AL:PM_W] = _nt(h, wt_ref[IN_ALOW:IN_ALOW + 128, :])
        gab_ref[...] = _nt(h, wt_ref[IN_GATES:D_IN, :])

    return _tiled_call(
        body, [_tok(D_MODEL), _once((1, D_MODEL)), _once((D_IN, D_MODEL))], [_tok(PM_W), _tok(GAB_W), _tok(D_MODEL)],
        [jax.ShapeDtypeStruct((SEQ, PM_W), F32), jax.ShapeDtypeStruct((SEQ, GAB_W), F32),
         jax.ShapeDtypeStruct((SEQ, D_MODEL), BF16)], (x, g_pre, wt_in), "in_proj", rider, rider_ins)


def _merge_fwd(ab, gab, x, w_pa4, w_pb4, w_o, g_post, rider=None, rider_ins=()):
    def body(ab_ref, gab_ref, x_ref, wpa_ref, wpb_ref, wo_ref, g_ref, x1_ref, mix_ref, mg_ref):
        a = ab_ref[:, 0:512]
        b = ab_ref[:, 512:1024]
        for j in range(N_CHIP):
            blk = slice(j * 256, (j + 1) * 256)
            ya = jnp.dot(a, wpa_ref[j], preferred_element_type=F32)
            yb = jnp.dot(b, wpb_ref[j], preferred_element_type=F32)
            sa = _sigmoid(gab_ref[:, j * 256:(j + 1) * 256])
            sb = _sigmoid(gab_ref[:, 1024 + j * 256:1024 + (j + 1) * 256])
            mg_ref[:, blk] = (sa * ya + sb * yb).astype(BF16)
        mix = jnp.dot(mg_ref[...], wo_ref[...], preferred_element_type=F32)
        mix_ref[...] = mix
        mn, _ = _rms_fwd(mix)
        x1_ref[...] = x_ref[...] + mn * g_ref[...]

    return _tiled_call(
        body, [_tok(1024), _tok(GAB_W), _tok(D_MODEL), _once((N_CHIP, 512, 256)), _once((N_CHIP, 512, 256)),
               _once((D_MODEL, D_MODEL)), _once((1, D_MODEL))], [_tok(D_MODEL), _tok(D_MODEL), _tok(D_MODEL)],
        [jax.ShapeDtypeStruct((SEQ, D_MODEL), F32), jax.ShapeDtypeStruct((SEQ, D_MODEL), F32),
         jax.ShapeDtypeStruct((SEQ, D_MODEL), BF16)], (ab, gab, x, w_pa4, w_pb4, w_o, g_post), "merge_fwd", rider, rider_ins)


def _mlp(x1, target, g_pre, g_post, w_up4, w_down_a4, w_down_b4):
    def body(x1_ref, t_ref, gpre_ref, gpost_ref, wup_ref, wda_ref, wdb_ref,
             dx1_ref, u_ref, dd_ref, h2_ref, dpre_ref, dgpost_ref, dgpre_ref, loss_ref):
        @pl.when(pl.program_id(0) == 0)
        def _():
            dgpost_ref[...] = jnp.zeros_like(dgpost_ref)
            dgpre_ref[...] = jnp.zeros_like(dgpre_ref)
            loss_ref[...] = jnp.zeros_like(loss_ref)

        x1 = x1_ref[...]
        gpre = gpre_ref[...]
        gpost = gpost_ref[...]
        xn2, r2 = _rms_fwd(x1)
        h2 = (xn2 * gpre).astype(BF16)
        h2_ref[...] = h2
        rl = []
        d = jnp.zeros((TOK_TILE, D_MODEL), F32)
        for j in range(N_CHIP):
            blk = slice(j * 1024, (j + 1) * 1024)
            r = jnp.maximum(jnp.dot(h2, wup_ref[j], preferred_element_type=F32), 0.0)
            rl.append(r)
            u = (r * r).astype(BF16)
            u_ref[:, blk] = u
            d = d + jnp.dot(u[:, 0:512], wda_ref[j], preferred_element_type=F32)
            d = d + jnp.dot(u[:, 512:1024], wdb_ref[j], preferred_element_type=F32)
        dn, r3 = _rms_fwd(d)
        diff = x1 + dn * gpost - t_ref[...]
        loss_ref[...] += jnp.sum(diff * diff, keepdims=True) * (0.5 / D_MODEL)
        dy = diff * (1.0 / D_MODEL)
        dgpost_ref[...] += jnp.sum(dy * dn, axis=0, keepdims=True)
        dd = _rms_bwd(dy, dn, r3, gpost).astype(BF16)
        dd_ref[...] = dd
        dh2 = jnp.zeros((TOK_TILE, D_MODEL), F32)
        for j in range(N_CHIP):
            blk = slice(j * 1024, (j + 1) * 1024)
            du = jnp.concatenate([_nt(dd, wda_ref[j]), _nt(dd, wdb_ref[j])], axis=1)
            dpre = (du * (2.0 * rl[j])).astype(BF16)
            dpre_ref[:, blk] = dpre
            dh2 = dh2 + _nt(dpre, wup_ref[j])
        dgpre_ref[...] += jnp.sum(dh2 * xn2, axis=0, keepdims=True)
        dx1_ref[...] = dy + _rms_bwd(dh2, xn2, r2, gpre)

    acc = pl.BlockSpec((1, D_MODEL), lambda i: (0, 0))
    return pl.pallas_call(
        body, grid=(N_TOK_TILE,),
        in_specs=[_tok(D_MODEL), _tok(D_MODEL), _once((1, D_MODEL)), _once((1, D_MODEL)),
                  _once((N_CHIP, D_MODEL, 1024)), _once((N_CHIP, 512, D_MODEL)), _once((N_CHIP, 512, D_MODEL))],
        out_specs=[_tok(D_MODEL), _tok(D_FF), _tok(D_MODEL), _tok(D_MODEL), _tok(D_FF), acc, acc,
                   pl.BlockSpec((1, 128), lambda i: (0, 0))],
        out_shape=[jax.ShapeDtypeStruct((SEQ, D_MODEL), F32), jax.ShapeDtypeStruct((SEQ, D_FF), BF16),
                   jax.ShapeDtypeStruct((SEQ, D_MODEL), BF16), jax.ShapeDtypeStruct((SEQ, D_MODEL), BF16),
                   jax.ShapeDtypeStruct((SEQ, D_FF), BF16), jax.ShapeDtypeStruct((1, D_MODEL), F32),
                   jax.ShapeDtypeStruct((1, D_MODEL), F32), jax.ShapeDtypeStruct((1, 128), F32)],
        compiler_params=_params(("arbitrary",)), name="mlp_fwd_bwd",
    )(x1, target, g_pre, g_post, w_up4, w_down_a4, w_down_b4)


def _merge_bwd(dx1, mix, ab, gab, merged, w_pa4, w_pb4, w_o, g_post):
    def body(dx1_ref, mix_ref, ab_ref, gab_ref, mg_ref, wpa_ref, wpb_ref, wo_ref, g_ref,
             dgab_ref, dab_ref, dg_ref, dwpa_ref, dwpb_ref, dwo_ref, acc_pa, acc_pb, acc_o):
        @pl.when(pl.program_id(0) == 0)
        def _():
            dg_ref[...] = jnp.zeros_like(dg_ref)
            acc_pa[...] = jnp.zeros_like(acc_pa)
            acc_pb[...] = jnp.zeros_like(acc_pb)
            acc_o[...] = jnp.zeros_like(acc_o)

        dx1 = dx1_ref[...]
        mn, r = _rms_fwd(mix_ref[...])
        dg_ref[...] += jnp.sum(dx1 * mn, axis=0, keepdims=True)
        dmix = _rms_bwd(dx1, mn, r, g_ref[...]).astype(BF16)
        acc_o[...] += _tn(mg_ref[...], dmix)
        dmerged = _nt(dmix, wo_ref[...])
        a = ab_ref[:, 0:512]
        b = ab_ref[:, 512:1024]
        da = jnp.zeros((TOK_TILE, 512), F32)
        db = jnp.zeros((TOK_TILE, 512), F32)
        dyas, dybs = [], []
        for j in range(N_CHIP):
            blk = slice(j * 256, (j + 1) * 256)
            blk_b = slice(1024 + j * 256, 1024 + (j + 1) * 256)
            dm = dmerged[:, blk]
            ya = jnp.dot(a, wpa_ref[j], preferred_element_type=F32)
            yb = jnp.dot(b, wpb_ref[j], preferred_element_type=F32)
            sa = _sigmoid(gab_ref[:, blk])
            sb = _sigmoid(gab_ref[:, blk_b])
            dya = (dm * sa).astype(BF16)
            dyb = (dm * sb).astype(BF16)
            dyas.append(dya)
            dybs.append(dyb)
            dgab_ref[:, blk] = (dm * ya * sa * (1.0 - sa)).astype(BF16)
            dgab_ref[:, blk_b] = (dm * yb * sb * (1.0 - sb)).astype(BF16)
            da = da + _nt(dya, wpa_ref[j])
            db = db + _nt(dyb, wpb_ref[j])
        dab_ref[:, 0:512] = da
        dab_ref[:, 512:1024] = db
        acc_pa[...] += _tn(a, jnp.concatenate(dyas, axis=1))
        acc_pb[...] += _tn(b, jnp.concatenate(dybs, axis=1))

        @pl.when(pl.program_id(0) == N_TOK_TILE - 1)
        def _():
            dwo_ref[...] = acc_o[...].astype(BF16)
            for j in range(N_CHIP):
                dwpa_ref[j] = acc_pa[:, j * 256:(j + 1) * 256].astype(BF16)
                dwpb_ref[j] = acc_pb[:, j * 256:(j + 1) * 256].astype(BF16)

    whole = lambda shape: pl.BlockSpec(shape, lambda i: (0,) * len(shape))
    return pl.pallas_call(
        body, grid=(N_TOK_TILE,),
        in_specs=[_tok(D_MODEL), _tok(D_MODEL), _tok(1024), _tok(GAB_W), _tok(D_MODEL), _once((N_CHIP, 512, 256)),
                  _once((N_CHIP, 512, 256)), _once((D_MODEL, D_MODEL)), _once((1, D_MODEL))],
        out_specs=[_tok(GAB_W), _tok(1024), whole((1, D_MODEL)), whole((N_CHIP, 512, 256)), whole((N_CHIP, 512, 256)),
                   whole((D_MODEL, D_MODEL))],
        out_shape=[jax.ShapeDtypeStruct((SEQ, GAB_W), BF16), jax.ShapeDtypeStruct((SEQ, 1024), F32),
                   jax.ShapeDtypeStruct((1, D_MODEL), F32), jax.ShapeDtypeStruct((N_CHIP, 512, 256), BF16),
                   jax.ShapeDtypeStruct((N_CHIP, 512, 256), BF16), jax.ShapeDtypeStruct((D_MODEL, D_MODEL), BF16)],
        scratch_shapes=[pltpu.VMEM((512, D_MODEL), F32), pltpu.VMEM((512, D_MODEL), F32),
                        pltpu.VMEM((D_MODEL, D_MODEL), F32)],
        compiler_params=_params(("arbitrary",)), name="merge_bwd",
    )(dx1, mix, ab, gab, merged, w_pa4, w_pb4, w_o, g_post)


def _in_proj_bwd(dpm, dgab, x, dx1, g_pre, wt_in, rider=None, rider_ins=()):
    def body(dpm_ref, dgab_ref, x_ref, dx1_ref, g_ref, wt_ref, dx_ref, dg_ref):
        @pl.when(pl.program_id(0) == 0)
        def _():
            dg_ref[...] = jnp.zeros_like(dg_ref)

        dh = jnp.dot(dpm_ref[:, 0:PM_XM], wt_ref[0:IN_ALOW, :], preferred_element_type=F32)
        dh = dh + jnp.dot(dpm_ref[:, PM_XM:PM_AL], wt_ref[IN_XM:IN_GATES, :], preferred_element_type=F32)
        dh = dh + jnp.dot(dpm_ref[:, PM_AL:PM_W], wt_ref[IN_ALOW:IN_ALOW + 128, :], preferred_element_type=F32)
        dh = dh + jnp.dot(dgab_ref[...], wt_ref[IN_GATES:D_IN, :], preferred_element_type=F32)
        xn, r = _rms_fwd(x_ref[...])
        dg_ref[...] += jnp.sum(dh * xn, axis=0, keepdims=True)
        dx_ref[...] = dx1_ref[...] + _rms_bwd(dh, xn, r, g_ref[...])

    return _tiled_call(
        body, [_tok(PM_W), _tok(GAB_W), _tok(D_MODEL), _tok(D_MODEL), _once((1, D_MODEL)), _once((D_IN, D_MODEL))],
        [_tok(D_MODEL), pl.BlockSpec((1, D_MODEL), lambda i: (0, 0))],
        [jax.ShapeDtypeStruct((SEQ, D_MODEL), F32), jax.ShapeDtypeStruct((1, D_MODEL), F32)],
        (dpm, dgab, x, dx1, g_pre, wt_in), "in_proj_bwd", rider, rider_ins)


def _dw_in(dpm, dgab, h):
    n_pm = PM_AL // 512
    n_blk = n_pm + GAB_W // 512

    def body(dpm_ref, dgab_ref, dal_ref, h_ref, o_ref):
        i = pl.program_id(0)
        off = pl.multiple_of(i * 512 + 16 * (i >= 3).astype(jnp.int32), 16)

        @pl.when(i < n_pm)
        def _():
            o_ref[pl.ds(off, 512), :] = _tn(dpm_ref[...], h_ref[...]).astype(BF16)

        @pl.when(i >= n_pm)
        def _():
            o_ref[pl.ds(off, 512), :] = _tn(dgab_ref[...], h_ref[...]).astype(BF16)

        @pl.when(i == 0)
        def _():
            o_ref[IN_ALOW:IN_XM, :] = _tn(dal_ref[...], h_ref[...])[0:IN_XM - IN_ALOW].astype(BF16)

    return pl.pallas_call(
        body, grid=(n_blk,),
        in_specs=[pl.BlockSpec((SEQ, 512), lambda i: (0, jnp.minimum(i, n_pm - 1))),
                  pl.BlockSpec((SEQ, 512), lambda i: (0, jnp.maximum(i - n_pm, 0))),
                  pl.BlockSpec((SEQ, 128), lambda i: (0, PM_AL // 128)),
                  _once((SEQ, D_MODEL))],
        out_specs=pl.BlockSpec((D_IN, D_MODEL), lambda i: (0, 0)),
        out_shape=jax.ShapeDtypeStruct((D_IN, D_MODEL), BF16),
        compiler_params=_params(("arbitrary",)), name="dw_in",
    )(dpm, dgab, dpm, h)


def _tn_matmul(a, b, name, shards=1, tm=1024, rider=None, rider_ins=()):
    m, n = a.shape[1], b.shape[1]
    tm = min(tm, m)
    tn = n // shards if shards > 1 else min(n, 1024)
    steps_i, steps_j = m // tm, n // tn
    r_in, r_out_specs, r_out_shape, r_scratch = _rider_specs(rider, rider_ins)

    def body(*refs):
        (a_ref, b_ref), ride_in, (o_ref,), ride_out, scratch = _split(refs, 2, len(r_in), 1, len(r_out_specs), len(r_scratch))
        step = pl.program_id(0) * steps_j + pl.program_id(1)
        _ride(rider, ("first",), step == 0, ride_in, ride_out, scratch)
        o_ref[...] = _tn(a_ref[...], b_ref[...]).astype(BF16)
        _ride(rider, ("middle", "last"), step == steps_i * steps_j - 1, ride_in, ride_out, scratch)

    if shards > 1:
        out_spec = pl.BlockSpec((None, tm, tn), lambda i, j: (j, i, 0))
        out_shape = jax.ShapeDtypeStruct((shards, m, tn), BF16)
    else:
        out_spec = pl.BlockSpec((tm, tn), lambda i, j: (i, j))
        out_shape = jax.ShapeDtypeStruct((m, n), BF16)
    res = pl.pallas_call(
        body, grid=(steps_i, steps_j),
        in_specs=[pl.BlockSpec((SEQ, tm), lambda i, j: (0, i)), pl.BlockSpec((SEQ, tn), lambda i, j: (0, j))] + r_in,
        out_specs=[out_spec] + r_out_specs, out_shape=[out_shape] + r_out_shape, scratch_shapes=r_scratch,
        compiler_params=_params(("arbitrary", "arbitrary")), name=name,
    )(a, b, *rider_ins)
    return res[0] if rider is None else (res[0], res[1:])


MESH = pl.DeviceIdType.MESH
ANY = pl.BlockSpec(memory_space=pl.ANY)
VMEM_WHOLE = pl.BlockSpec(memory_space=pltpu.VMEM)

_BIG = ("w_in", "w_pa", "w_pb", "w_o", "w_up", "w_down")
_BIG_SHARD = {"w_in": (IN_SHARD, D_MODEL), "w_pa": (512, 256), "w_pb": (512, 256), "w_o": (256, D_MODEL),
              "w_up": (D_MODEL, 1024), "w_down": (1024, D_MODEL),
              "w_down_a": (512, D_MODEL), "w_down_b": (512, D_MODEL)}
_BIG_SPLIT = {"w_in": 1, "w_pa": 0, "w_pb": 0, "w_o": 0, "w_up": 0, "w_down": 0, "w_down_a": 0, "w_down_b": 0}


def _half(ref, e, name, lead=0, part=None):
    axis = _BIG_SPLIT[name]
    size = _BIG_SHARD[name][axis] // 2
    start = e * size
    if part is not None:
        size //= 2
        start = start + part * size
    start = pl.multiple_of(start, 128 if axis == 1 else 16)
    idx = [pl.ds(0, ref.shape[a]) for a in range(lead)]
    idx += [pl.ds(start, size), pl.ds(0, _BIG_SHARD[name][1])] if axis == 0 else [pl.ds(0, _BIG_SHARD[name][0]), pl.ds(start, size)]
    return ref.at[tuple(idx)]


def _half_shape(name):
    r, c = _BIG_SHARD[name]
    return (r // 2, c) if _BIG_SPLIT[name] == 0 else (r, c // 2)


def _remote(src, dst, send_sems, recv_sems, k, to):
    return pltpu.make_async_remote_copy(src_ref=src, dst_ref=dst, send_sem=send_sems.at[k], recv_sem=recv_sems.at[k],
                                        device_id=to, device_id_type=MESH)


def _mesh_place():
    x, y, c = lax.axis_index("x"), lax.axis_index("y"), lax.axis_index("c")
    return x, y, c, [(1 - x, y), (x, 1 - y), (1 - x, 1 - y)]


class _Gather:
    def __init__(self, names, small=(), middle_at=0.5):
        self.middle_at = middle_at
        self.names = tuple(names)
        self.nb = len(self.names)
        self.n = self.nb + len(small)
        self.n_sems = 8 * self.nb + 3 * len(small)
        self.n_flush = 6 * self.nb + len(small)
        self.out_shape = [jax.ShapeDtypeStruct((N_CHIP,) + _BIG_SHARD[nm], BF16) for nm in self.names]
        self.out_shape += [jax.ShapeDtypeStruct((N_CHIP,) + s.shape, s.dtype) for s in small]

    def _copies(self, ins, outs, ss, rs, k):
        x, y, c, _ = _mesh_place()
        name = self.names[k]
        me, xn, yn, dg = 2 * x + y, 2 * (1 - x) + y, 2 * x + (1 - y), 2 * (1 - x) + (1 - y)
        to_x, to_y, sibling = (1 - x, y, c), (x, 1 - y, c), (x, y, 1 - c)

        def region(slot, e, part=None):
            return _half(outs[k].at[slot], e, name, part=part)

        def copy(pair, src, dst, to):
            return _remote(src, dst, ss, rs, 8 * k + pair, to)

        mine = _half(ins[k], c, name)
        sent = [copy(0, mine, region(me, c), to_x), copy(1, mine, region(me, c), to_y),
                copy(2, region(xn, c, 0), region(xn, c, 0), to_y), copy(3, region(yn, c, 1), region(yn, c, 1), to_x),
                copy(4, region(xn, c), region(xn, c), sibling), copy(5, region(yn, c), region(yn, c), sibling),
                copy(6, region(dg, c, 0), region(dg, c, 0), sibling), copy(7, region(dg, c, 1), region(dg, c, 1), sibling)]
        landing = [region(xn, c), region(yn, c), region(dg, c, 0), region(dg, c, 1),
                   region(xn, 1 - c), region(yn, 1 - c), region(dg, 1 - c, 0), region(dg, 1 - c, 1)]
        received = [copy(pair, dst, dst, sibling) for pair, dst in enumerate(landing)]
        return sent, received

    def _small(self, ins, outs, ss, rs, k, j, peer, slot, c):
        return _remote(ins[k], outs[k].at[slot], ss, rs, 8 * self.nb + 3 * (k - self.nb) + j, (*peer, c))

    def flush(self, phase, lands, outs, fs):
        x, y, c, _ = _mesh_place()
        me, xn, yn, dg = 2 * x + y, 2 * (1 - x) + y, 2 * x + (1 - y), 2 * (1 - x) + (1 - y)

        def pieces(k):
            name = self.names[k]
            spots = [lambda r: r.at[me], lambda r: _half(r.at[xn], c, name), lambda r: _half(r.at[yn], c, name),
                     lambda r: _half(r.at[xn], 1 - c, name), lambda r: _half(r.at[yn], 1 - c, name), lambda r: r.at[dg]]
            return [pltpu.make_async_copy(spot(lands[k]), spot(outs[k]), fs.at[6 * k + t]) for t, spot in enumerate(spots)]

        ready = {"first": (0,), "middle": (1, 2), "last": (3, 4, 5)}[phase]
        for k in range(self.nb):
            cps = pieces(k)
            for t in ready:
                cps[t].start()
        if phase == "last":
            small = [pltpu.make_async_copy(lands[k], outs[k], fs.at[6 * self.nb + k - self.nb]) for k in range(self.nb, self.n)]
            for cp in small:
                cp.start()
            for k in range(self.nb):
                for cp in pieces(k):
                    cp.wait()
            for cp in small:
                cp.wait()

    def first(self, ins, outs, ss, rs):
        x, y, c, peers = _mesh_place()
        me = 2 * x + y
        for k in range(self.nb):
            sent, _ = self._copies(ins, outs, ss, rs, k)
            sent[0].start()
            sent[1].start()
        for k in range(self.nb, self.n):
            for j, peer in enumerate(peers):
                self._small(ins, outs, ss, rs, k, j, peer, me, c).start()
        for k in range(self.n):
            outs[k][me] = ins[k][...]

    def middle(self, ins, outs, ss, rs):
        for k in range(self.nb):
            sent, received = self._copies(ins, outs, ss, rs, k)
            for pair in (0, 1):
                received[pair].wait_recv()
                sent[2 + pair].start()
                sent[4 + pair].start()

    def last(self, ins, outs, ss, rs):
        x, y, c, peers = _mesh_place()
        for k in range(self.nb):
            sent, received = self._copies(ins, outs, ss, rs, k)
            for pair in (2, 3):
                received[pair].wait_recv()
                sent[4 + pair].start()
        for k in range(self.nb):
            sent, received = self._copies(ins, outs, ss, rs, k)
            for pair in range(4, 8):
                received[pair].wait_recv()
            for cp in sent:
                cp.wait_send()
        for k in range(self.nb, self.n):
            for j, (px, py) in enumerate(peers):
                self._small(ins, outs, ss, rs, k, j, (px, py), 2 * px + py, c).wait_recv()
                self._small(ins, outs, ss, rs, k, j, (px, py), 2 * x + y, c).wait_send()


def _run_alone(rider, ins, name):
    def body(*refs):
        r_in, r_out, sems = _split(refs, len(ins), len(rider.out_shape), 2)
        rider.first(r_in, r_out, *sems)
        rider.middle(r_in, r_out, *sems)
        rider.last(r_in, r_out, *sems)

    return pl.pallas_call(
        body, in_specs=[VMEM_WHOLE] * len(ins), out_specs=[VMEM_WHOLE] * len(rider.out_shape), out_shape=rider.out_shape,
        scratch_shapes=[pltpu.SemaphoreType.DMA((rider.n_sems,)), pltpu.SemaphoreType.DMA((rider.n_sems,))],
        compiler_params=_params(), name=name,
    )(*ins)


class _Presum:
    in_space = ANY

    def __init__(self, names, base=0):
        self.names = tuple(names)
        self.n = len(self.names)
        self.base = base
        self.n_sems = 3 * self.n
        self.out_shape = [jax.ShapeDtypeStruct((N_CHIP,) + _half_shape(nm), BF16) for nm in self.names]
        self.work_shape = self.out_shape + self.out_shape

    def _stage(self, ins, bufs, ss, k, e, which):
        n = self.n
        return pltpu.make_async_copy(_half(ins[k], e, self.names[k], lead=1), bufs[which * n + k],
                                     ss.at[self.base + which * n + k])

    def _give(self, bufs, ss, rs, k, sibling):
        return _remote(bufs[self.n + k], bufs[k], ss, rs, self.base + k, sibling)

    def first(self, ins, bufs, ss, rs):
        x, y, c, _ = _mesh_place()
        for k in range(self.n):
            self._stage(ins, bufs, ss, k, 1 - c, 1).start()
        for k in range(self.n):
            self._stage(ins, bufs, ss, k, c, 2).start()
        for k in range(self.n):
            self._stage(ins, bufs, ss, k, 1 - c, 1).wait()
            self._give(bufs, ss, rs, k, (x, y, 1 - c)).start()

    def middle(self, ins, bufs, ss, rs):
        pass

    def last(self, ins, bufs, ss, rs):
        x, y, c, _ = _mesh_place()
        for k in range(self.n):
            self._give(bufs, ss, rs, k, (x, y, 1 - c)).wait_recv()
            self._stage(ins, bufs, ss, k, c, 2).wait()

            @pl.loop(0, N_CHIP)
            def _(j):
                bufs[k][j] = (bufs[k][j].astype(F32) + bufs[2 * self.n + k][j].astype(F32)).astype(BF16)
        for k in range(self.n):
            self._give(bufs, ss, rs, k, (x, y, 1 - c)).wait_send()


class _ReduceRelay:
    middle_at = 0.75

    def __init__(self, names, base=0):
        self.names = tuple(names)
        self.n = len(self.names)
        self.base = base
        self.n_sems = 6 * self.n
        self.out_shape = [jax.ShapeDtypeStruct((N_CHIP,) + _half_shape(nm), BF16) for nm in self.names]
        quarter = [jax.ShapeDtypeStruct(self._part_shape(nm), BF16) for nm in self.names]
        self.work_shape = quarter + quarter

    @staticmethod
    def _part_shape(name):
        r, c = _half_shape(name)
        return (r // 2, c) if _BIG_SPLIT[name] == 0 else (r, c // 2)

    def _part(self, ref, name, p):
        r, c = self._part_shape(name)
        return ref.at[pl.ds(p * r, r), pl.ds(0, c)] if _BIG_SPLIT[name] == 0 else ref.at[pl.ds(0, r), pl.ds(p * c, c)]

    def _copies(self, ins, bufs, ss, rs, k):
        x, y, c, _ = _mesh_place()
        name, n = self.names[k], self.n
        me, xn, yn, dg = 2 * x + y, 2 * (1 - x) + y, 2 * x + (1 - y), 2 * (1 - x) + (1 - y)
        to_x, to_y = (1 - x, y, c), (x, 1 - y, c)
        mine = lambda slot, p: self._part(ins[k].at[slot], name, p)
        slot = lambda s, p: self._part(bufs[k].at[s], name, p)
        from_x, from_y = bufs[n + k], bufs[2 * n + k]

        def copy(pair, src, dst, to):
            return _remote(src, dst, ss, rs, self.base + 6 * k + pair, to)

        sent = [copy(0, mine(dg, 0), from_x, to_x), copy(1, mine(dg, 1), from_y, to_y),
                copy(2, mine(xn, 0), slot(me, 0), to_x), copy(3, mine(yn, 1), slot(me, 1), to_y),
                copy(4, from_y, slot(me, 1), to_x), copy(5, from_x, slot(me, 0), to_y)]
        landing = [from_x, from_y, slot(xn, 0), slot(yn, 1), slot(xn, 1), slot(yn, 0)]
        received = [copy(pair, dst, dst, to_x) for pair, dst in enumerate(landing)]
        return sent, received

    def first(self, ins, bufs, ss, rs):
        x, y, c, _ = _mesh_place()
        me, dg = 2 * x + y, 2 * (1 - x) + (1 - y)
        for k in range(self.n):
            sent, _ = self._copies(ins, bufs, ss, rs, k)
            for pair in range(4):
                sent[pair].start()
        for k in range(self.n):
            bufs[k][me] = ins[k][me]
            bufs[k][dg] = jnp.zeros(_half_shape(self.names[k]), BF16)

    def middle(self, ins, bufs, ss, rs):
        x, y, c, _ = _mesh_place()
        xn, yn = 2 * (1 - x) + y, 2 * x + (1 - y)
        for k in range(self.n):
            sent, received = self._copies(ins, bufs, ss, rs, k)
            name, n = self.names[k], self.n
            for pair, buf, own in ((0, bufs[n + k], self._part(ins[k].at[yn], name, 0)),
                                   (1, bufs[2 * n + k], self._part(ins[k].at[xn], name, 1))):
                received[pair].wait_recv()
                buf[...] = (buf[...].astype(F32) + own[...].astype(F32)).astype(BF16)
            sent[5].start()
            sent[4].start()

    def last(self, ins, bufs, ss, rs):
        for k in range(self.n):
            sent, received = self._copies(ins, bufs, ss, rs, k)
            for pair in range(2, 6):
                received[pair].wait_recv()
            for cp in sent:
                cp.wait_send()


class _PresumThenRelay:
    in_space = ANY
    middle_at = _ReduceRelay.middle_at

    def __init__(self, names):
        self.relay = _ReduceRelay(names)
        self.pre = _Presum(names, base=self.relay.n_sems)
        self.n_sems = self.relay.n_sems + self.pre.n_sems
        self.out_shape = self.relay.out_shape
        self.work_shape = list(self.relay.work_shape) + list(self.pre.out_shape) + list(self.pre.work_shape)
        self.n_relay = len(self.relay.out_shape) + len(self.relay.work_shape)

    def first(self, ins, bufs, ss, rs):
        self.pre.first(ins, bufs[self.n_relay:], ss, rs)

    def early(self, ins, bufs, ss, rs):
        self.pre.last(ins, bufs[self.n_relay:], ss, rs)
        self.relay.first(bufs[self.n_relay:], bufs[:self.n_relay], ss, rs)

    def middle(self, ins, bufs, ss, rs):
        self.relay.middle(bufs[self.n_relay:], bufs[:self.n_relay], ss, rs)

    def last(self, ins, bufs, ss, rs):
        self.relay.last(bufs[self.n_relay:], bufs[:self.n_relay], ss, rs)


class _SendPartials:
    def __init__(self, names, small_shape=None):
        self.n = len(names)
        self.small = small_shape is not None
        self.n_sems = 3 * self.n + 7
        self.out_shape = [jax.ShapeDtypeStruct((N_CHIP,) + _half_shape(nm), BF16) for nm in names]
        if self.small:
            self.out_shape.append(jax.ShapeDtypeStruct((N_DEV,) + small_shape, F32))

    def _piece(self, ins, outs, ss, rs, k, j, peer, src_slot, dst_slot, c):
        return _remote(ins[k].at[src_slot], outs[k].at[dst_slot], ss, rs, 3 * k + j, (*peer, c))

    def _small(self, ins, outs, ss, rs, r, other, slot):
        return _remote(ins[self.n], outs[self.n].at[slot], ss, rs, 3 * self.n + r, other)

    @staticmethod
    def _others(x, y, c):
        return [(x, y, 1 - c), (1 - x, y, c), (1 - x, y, 1 - c), (x, 1 - y, c), (x, 1 - y, 1 - c),
                (1 - x, 1 - y, c), (1 - x, 1 - y, 1 - c)]

    def first(self, ins, outs, ss, rs, only=None):
        x, y, c, peers = _mesh_place()
        me = 2 * x + y
        which = range(self.n) if only is None else only
        for k in which:
            for j, (px, py) in enumerate(peers):
                self._piece(ins, outs, ss, rs, k, j, (px, py), 2 * px + py, me, c).start()
        if self.small:
            for r, other in enumerate(self._others(x, y, c)):
                self._small(ins, outs, ss, rs, r, other, 4 * x + 2 * y + c).start()
            outs[self.n][4 * x + 2 * y + c] = ins[self.n][...]
        for k in which:
            outs[k][me] = ins[k][me]

    def middle(self, ins, outs, ss, rs):
        pass

    def last(self, ins, outs, ss, rs):
        x, y, c, peers = _mesh_place()
        me = 2 * x + y
        for k in range(self.n):
            for j, (px, py) in enumerate(peers):
                self._piece(ins, outs, ss, rs, k, j, (px, py), me, 2 * px + py, c).wait_recv()
                self._piece(ins, outs, ss, rs, k, j, (px, py), 2 * px + py, me, c).wait_send()
        if self.small:
            for r, (px, py, pc) in enumerate(self._others(x, y, c)):
                self._small(ins, outs, ss, rs, r, (px, py, pc), 4 * px + 2 * py + pc).wait_recv()
                self._small(ins, outs, ss, rs, r, (px, py, pc), 4 * x + 2 * y + c).wait_send()


class _PresumThenSend:
    def __init__(self, names):
        self.send = _SendPartials(names)
        self.pre = _Presum(names[-1:], base=self.send.n_sems)
        self.n = self.send.n
        self.n_sems = self.send.n_sems + self.pre.n_sems
        self.out_shape = self.send.out_shape
        self.work_shape = list(self.pre.out_shape) + list(self.pre.work_shape)
        self.in_space = [VMEM_WHOLE] * (self.n - 1) + [ANY]

    def _partials(self, ins, bufs):
        return list(ins[:self.n - 1]) + [bufs[self.n]]

    def first(self, ins, bufs, ss, rs):
        self.pre.first(ins[self.n - 1:], bufs[self.n:], ss, rs)
        self.send.first(ins, bufs[:self.n], ss, rs, only=range(self.n - 1))

    def early(self, ins, bufs, ss, rs):
        self.pre.last(ins[self.n - 1:], bufs[self.n:], ss, rs)
        self.send.first(self._partials(ins, bufs), bufs[:self.n], ss, rs, only=(self.n - 1,))

    def middle(self, ins, bufs, ss, rs):
        pass

    def last(self, ins, bufs, ss, rs):
        self.send.last(self._partials(ins, bufs), bufs[:self.n], ss, rs)


def _sum_swap(names, parts, small):
    n = len(parts)
    everyone = _SendPartials((), small.shape)

    def body(*refs):
        p_refs, (small_ref,), o_refs, (osmall_ref,), (all_ref,), (send_sems, recv_sems, ss_small, rs_small) = _split(
            refs, n, 1, n, 1, 1, 4)
        x, y, c = lax.axis_index("x"), lax.axis_index("y"), lax.axis_index("c")
        everyone.first([small_ref], [all_ref], ss_small, rs_small)

        def mine(k):
            part = _half(o_refs[k], c, names[k])
            return _remote(part, part, send_sems, recv_sems, k, (x, y, 1 - c))

        for k in range(n):
            for e in range(2):
                @pl.when(c == e)
                def _():
                    g = p_refs[k][0].astype(F32)
                    for s in range(1, N_CHIP):
                        g = g + p_refs[k][s].astype(F32)
                    r, cols = _half_shape(names[k])
                    if _BIG_SPLIT[names[k]] == 0:
                        o_refs[k][e * r:(e + 1) * r, :] = g
                    else:
                        o_refs[k][:, e * cols:(e + 1) * cols] = g
            mine(k).start()
        for k in range(n):
            theirs = _half(o_refs[k], 1 - c, names[k])
            _remote(theirs, theirs, send_sems, recv_sems, k, (x, y, 1 - c)).wait_recv()
            mine(k).wait_send()
        everyone.last([small_ref], [all_ref], ss_small, rs_small)
        g = all_ref[0]
        for d in range(1, N_DEV):
            g = g + all_ref[d]
        osmall_ref[...] = g

    res = pl.pallas_call(
        body, in_specs=[VMEM_WHOLE] * (n + 1), out_specs=[VMEM_WHOLE] * (n + 1),
        out_shape=[jax.ShapeDtypeStruct(_BIG_SHARD[nm], F32) for nm in names] + [jax.ShapeDtypeStruct(small.shape, F32)],
        scratch_shapes=[pltpu.VMEM((N_DEV,) + small.shape, F32), pltpu.SemaphoreType.DMA((n,)), pltpu.SemaphoreType.DMA((n,)),
                        pltpu.SemaphoreType.DMA((everyone.n_sems,)), pltpu.SemaphoreType.DMA((everyone.n_sems,))],
        compiler_params=_params(), name="sum_swap",
    )(*parts, small)
    return res[:n], res[n]


def _tile(rows, cols, itemsize, budget):
    t = cols if rows % 16 else rows
    other = rows if rows % 16 else cols
    step = 256 if rows % 16 else 32
    while t % step == 0 and t * other * itemsize > budget:
        t //= 2
    return (rows, t) if rows % 16 else (t, cols)


def _adamw_math(w, g, m, v):
    m = ADAM_B1 * m + (1.0 - ADAM_B1) * g
    v = ADAM_B2 * v + (1.0 - ADAM_B2) * (g * g)
    m_hat = m / (1.0 - ADAM_B1 ** ADAM_STEP)
    v_hat = v / (1.0 - ADAM_B2 ** ADAM_STEP)
    delta = -ADAM_LR * (m_hat / (jnp.sqrt(v_hat) + ADAM_EPS) + ADAM_WD * w)
    return delta, m, v


def _adamw_big(g, w, m, v, name):
    r, c = w.shape
    tr, tc = _tile(r, c, 4, 2 * 1024 * 1024)

    def body(g_ref, w_ref, m_ref, v_ref, d_ref, nm_ref, nv_ref):
        d_ref[...], nm_ref[...], nv_ref[...] = _adamw_math(w_ref[...], g_ref[...], m_ref[...], v_ref[...])

    blk = pl.BlockSpec((tr, tc), lambda i, l: (i, l))
    return pl.pallas_call(
        body, grid=(r // tr, c // tc), in_specs=[blk, blk, blk, blk],
        out_specs=[blk, blk, blk], out_shape=[jax.ShapeDtypeStruct((r, c), F32)] * 3,
        compiler_params=_params(("arbitrary", "arbitrary")), name=name,
    )(g, w, m, v)


def _adamw_rows(g, w, m, v, name):
    r, k, lanes = w.shape
    tr = 296

    def body(g_ref, w_ref, m_ref, v_ref, g3_ref, d_ref, nm_ref, nv_ref):
        g = g_ref[...].reshape(tr, k, lanes)
        g3_ref[...] = g
        d_ref[...], nm_ref[...], nv_ref[...] = _adamw_math(w_ref[...], g, m_ref[...], v_ref[...])

    rows = pl.BlockSpec((tr, k, lanes), lambda i: (i, 0, 0))
    return pl.pallas_call(
        body, grid=(pl.cdiv(r, tr),), in_specs=[pl.BlockSpec((tr, k * lanes), lambda i: (i, 0)), rows, rows, rows],
        out_specs=[rows] * 4, out_shape=[jax.ShapeDtypeStruct((r, k, lanes), F32)] * 4,
        compiler_params=_params(("arbitrary",)), name=name,
    )(g, w, m, v)


def _adamw_small(ws, gs, ms, vs):
    n = len(ws)

    def body(*refs):
        w_refs, g_refs, m_refs, v_refs, d_refs, nm_refs, nv_refs = _split(refs, *([n] * 7))
        for k in range(n):
            d_refs[k][...], nm_refs[k][...], nv_refs[k][...] = _adamw_math(w_refs[k][...], g_refs[k][...], m_refs[k][...],
                                                                             v_refs[k][...])

    shapes = [jax.ShapeDtypeStruct(w.shape, F32) for w in ws]
    res = pl.pallas_call(body, out_shape=shapes * 3, name="adamw_small")(*ws, *gs, *ms, *vs)
    return res[:n], res[n:2 * n], res[2 * n:]


def _pack(arrs):
    flat = jnp.concatenate([a.reshape(-1) for a in arrs])
    rows = -(-flat.shape[0] // 1024) * 8
    return jnp.pad(flat, (0, rows * 128 - flat.shape[0])).reshape(rows, 128)


def _unpack(buf, shapes):
    flat = buf.reshape(-1)
    out, off = [], 0
    for s in shapes:
        size = 1
        for d in s:
            size *= d
        out.append(flat[off:off + size].reshape(s))
        off += size
    return out


def _block_rows(w):
    return jnp.pad(w.reshape(512, 4), ((0, 0), (0, 124)))


def _cols(a4):
    return jnp.transpose(a4, (1, 0, 2)).reshape(a4.shape[1], -1)


_LATE = ("w_pa", "w_pb", "w_o", "w_up", "w_down")
_RIDE_IN_PROJ = ("w_pa", "w_pb", "w_o")
_RIDE_MIXER = ("w_up", "w_down_a")
_RIDE_MERGE = ("w_down_b",)


def _full_weights(gathered):
    joined = {"w_in": (D_IN, D_MODEL), "w_o": (D_MODEL, D_MODEL)}
    return {n: (a.reshape(joined[n]) if n in joined else a) for n, a in gathered.items()}


def _local_step(x, target, w, sp, late_shards=None):
    sp = {n: (a.reshape(1, -1) if a.ndim == 1 else a) for n, a in sp.items()}
    wau = jnp.zeros((128, 256), F32).at[0:16].set(sp["w_a_up"])
    wif = jnp.zeros((1536, 128), F32).at[:, 0:8].set(sp["w_if"])
    bif = jnp.zeros((1, 128), F32).at[:, 0:8].set(sp["b_if"])
    p = {"wau": wau, "bau": sp["b_a_up"], "ggla": sp["g_gla_norm"], "cw": sp["conv_w"], "cb": sp["conv_b"],
         "wq": _block_rows(sp["w_q_ml"]), "wk": _block_rows(sp["w_k_ml"]), "wv": _block_rows(sp["w_v_ml"]),
         "wif": wif, "bif": bif, "skip": sp["ml_skip"], "gml": sp["g_ml_norm"]}

    if late_shards is None:
        (pm, gab, h), _ = _in_proj(x, sp["g_pre_mix"], w["w_in"])
        ab, *states = _mixer_fwd(pm, p)
        (x1, mix, merged), _ = _merge_fwd(ab, gab, x, w["w_pa"], w["w_pb"], w["w_o"], sp["g_post_mix"])
    else:
        shard = dict(zip(_LATE, late_shards))
        shard["w_down_a"], shard["w_down_b"] = shard["w_down"][0:512], shard["w_down"][512:1024]
        (pm, gab, h), got = _in_proj(x, sp["g_pre_mix"], w["w_in"], _Gather(_RIDE_IN_PROJ, middle_at=0.45),
                                     [shard[n] for n in _RIDE_IN_PROJ])
        w = dict(w, **_full_weights(dict(zip(_RIDE_IN_PROJ, got))))
        ab, *rest = _mixer_fwd(pm, p, _Gather(_RIDE_MIXER, middle_at=0.62), [shard[n] for n in _RIDE_MIXER])
        states = rest[:4]
        w.update(_full_weights(dict(zip(_RIDE_MIXER, rest[4:]))))
        (x1, mix, merged), got = _merge_fwd(ab, gab, x, w["w_pa"], w["w_pb"], w["w_o"], sp["g_post_mix"],
                                            _Gather(_RIDE_MERGE, middle_at=0.46), [shard[n] for n in _RIDE_MERGE])
        w.update(_full_weights(dict(zip(_RIDE_MERGE, got))))
    dx1, u, dd, h2, dpre, dg_post_mlp, dg_pre_mlp, loss = _mlp(x1, target, sp["g_pre_mlp"], sp["g_post_mlp"],
                                                                w["w_up"], w["w_down_a"], w["w_down_b"])
    dgab, dab, dg_post_mix, dw_pa, dw_pb, dw_o = _merge_bwd(dx1, mix, ab, gab, merged, w["w_pa"], w["w_pb"], w["w_o"],
                                                            sp["g_post_mix"])
    big = {"w_pa": dw_pa, "w_pb": dw_pb, "w_o": dw_o, "w_up": _tn_matmul(h2, dpre, "dw_up", shards=N_CHIP)}
    if late_shards is None:
        big["w_down"] = _tn_matmul(u, dd, "dw_down")
        dpm, dp, _ = _mixer_bwd(pm, dab, states, p)
    else:
        pieces = lambda n: big[n].reshape((N_CHIP,) + _BIG_SHARD[n])
        big["w_down"], partial = _tn_matmul(u, dd, "dw_down", rider=_Presum(_LATE[:4]),
                                            rider_ins=[pieces(n) for n in _LATE[:4]])
        dpm, dp, parts = _mixer_bwd(pm, dab, states, p, _PresumThenSend(_LATE), list(partial) + [pieces("w_down")])
        big = dict(zip(_LATE, parts))
    big["w_in"] = _dw_in(dpm, dgab, h)
    if late_shards is None:
        (dx, dg_pre_mix), _ = _in_proj_bwd(dpm, dgab, x, dx1, sp["g_pre_mix"], w["w_in"])
    else:
        (dx, dg_pre_mix), parts = _in_proj_bwd(dpm, dgab, x, dx1, sp["g_pre_mix"], w["w_in"], _PresumThenRelay(("w_in",)),
                                               [big["w_in"].reshape((N_CHIP,) + _BIG_SHARD["w_in"])])
        big["w_in"] = parts[0]
    small = {
        "g_pre_mix": dg_pre_mix, "b_a_up": dp["bau"], "g_gla_norm": dp["ggla"], "conv_b": dp["cb"],
        "w_q_ml": dp["wq"][:, 0:4].reshape(128, 4, 4), "w_k_ml": dp["wk"][:, 0:4].reshape(128, 4, 4),
        "w_v_ml": dp["wv"][:, 0:4].reshape(128, 4, 4),
        "b_if": dp["bif"][:, 0:8], "ml_skip": dp["skip"], "g_ml_norm": dp["gml"], "g_post_mix": dg_post_mix,
        "g_pre_mlp": dg_pre_mlp, "g_post_mlp": dg_post_mlp, "w_a_up": dp["wau"][0:16], "conv_w": dp["cw"],
        "w_if": dp["wif"][:, 0:8], "loss": loss[:, 0:1],
    }
    return dx, big, small


_SMALL_REPL = ("g_pre_mix", "b_a_up", "g_gla_norm", "conv_b", "w_q_ml", "w_k_ml", "w_v_ml", "b_if", "ml_skip",
               "g_ml_norm", "g_post_mix", "g_pre_mlp", "g_post_mlp")
_SMALL_SHARDED = ("w_a_up", "conv_w", "w_if")
_SMALL_ORDER = _SMALL_REPL + _SMALL_SHARDED + ("loss",)
_WEIGHTS = ("g_pre_mix", "w_in", "w_a_up", "b_a_up", "g_gla_norm", "conv_w", "conv_b", "w_q_ml", "w_k_ml", "w_v_ml",
            "w_if", "b_if", "ml_skip", "g_ml_norm", "w_pa", "w_pb", "w_o", "g_post_mix", "g_pre_mlp", "w_up", "w_down",
            "g_post_mlp")


_BLOCK_WEIGHTS = ("w_q_ml", "w_k_ml", "w_v_ml")


def _stored(name, a):
    if name in _BLOCK_WEIGHTS:
        return jnp.transpose(a, (0, 2, 3, 1)).reshape(16, 128)
    if name == "w_if":
        return jnp.transpose(a, (0, 2, 1)).reshape(8, 384)
    return a


def _unstored(name, a):
    if name in _BLOCK_WEIGHTS:
        return jnp.transpose(a.reshape(1, 4, 4, 128), (0, 3, 1, 2))
    if name == "w_if":
        return jnp.transpose(a.reshape(1, 8, 384), (0, 2, 1))
    return a


def _as_shard(name, a):
    return jnp.transpose(a, (2, 0, 1)).reshape(IN_SHARD, D_MODEL // 128, 128) if name == "w_in" else a[0]


def _in_shard_bf16(w_in):
    return jnp.transpose(w_in.astype(BF16), (2, 0, 1)).reshape(IN_SHARD, D_MODEL)


def _from_shard(name, a):
    return jnp.transpose(a, (1, 2, 0)).reshape(1, D_MODEL, IN_SHARD) if name == "w_in" else a[None]


def kernel(x, g_pre_mix, w_in, w_a_up, b_a_up, g_gla_norm, conv_w, conv_b, w_q_ml, w_k_ml, w_v_ml, w_if, b_if, ml_skip, g_ml_norm, w_pa, w_pb, w_o, g_post_mix, g_pre_mlp, w_up, w_down, g_post_mlp, loss_target, m_g_pre_mix, m_w_in, m_w_a_up, m_b_a_up, m_g_gla_norm, m_conv_w, m_conv_b, m_w_q_ml, m_w_k_ml, m_w_v_ml, m_w_if, m_b_if, m_ml_skip, m_g_ml_norm, m_w_pa, m_w_pb, m_w_o, m_g_post_mix, m_g_pre_mlp, m_w_up, m_w_down, m_g_post_mlp, v_g_pre_mix, v_w_in, v_w_a_up, v_b_a_up, v_g_gla_norm, v_conv_w, v_conv_b, v_w_q_ml, v_w_k_ml, v_w_v_ml, v_w_if, v_b_if, v_ml_skip, v_g_ml_norm, v_w_pa, v_w_pb, v_w_o, v_g_post_mix, v_g_pre_mlp, v_w_up, v_w_down, v_g_post_mlp):
    args = dict(locals())
    wts = {n: _as_shard(n, args[n]) for n in _WEIGHTS}
    mom = {n: _as_shard(n, args["m_" + n]) for n in _WEIGHTS}
    var = {n: _as_shard(n, args["v_" + n]) for n in _WEIGHTS}
    chip = 2 * lax.axis_index("x") + lax.axis_index("y")

    first = ("w_in",) + _SMALL_SHARDED
    gathered = dict(zip(first, _run_alone(_Gather(("w_in",), [wts[n] for n in _SMALL_SHARDED]),
                                          [_in_shard_bf16(w_in)] + [wts[n] for n in _SMALL_SHARDED],
                                          "gather_first")))
    sp = {n: wts[n] for n in _SMALL_REPL}
    sp["w_a_up"] = _cols(gathered["w_a_up"])
    sp["conv_w"] = _cols(gathered["conv_w"])
    sp["w_if"] = gathered["w_if"].reshape(1536, 8)

    dx, big, small = _local_step(x[0], loss_target[0], _full_weights({"w_in": gathered["w_in"]}), sp,
                                 late_shards=[wts[n].astype(BF16) for n in _LATE])

    small_shapes = [small[n].shape for n in _SMALL_ORDER]
    packed = _pack([small[n] for n in _SMALL_ORDER])
    sums, small_sum = _sum_swap(_BIG, [big[n] for n in _BIG], packed)

    grads, delta, new_m, new_v = {}, {}, {}, {}
    for n, g in zip(_BIG, sums):
        if n == "w_in":
            g, d, nm, nv = _adamw_rows(g, wts[n], mom[n], var[n], "adamw_" + n)
        else:
            d, nm, nv = _adamw_big(g, wts[n], mom[n], var[n], "adamw_" + n)
        grads[n], delta[n], new_m[n], new_v[n] = (_from_shard(n, a) for a in (g, d, nm, nv))
    summed = dict(zip(_SMALL_ORDER, _unpack(small_sum, small_shapes)))
    loss = summed["loss"].reshape(())
    summed["w_a_up"] = lax.dynamic_slice_in_dim(summed["w_a_up"], chip * 64, 64, axis=1)
    summed["conv_w"] = lax.dynamic_slice_in_dim(summed["conv_w"], chip * 128, 128, axis=1)
    summed["w_if"] = lax.dynamic_slice_in_dim(summed["w_if"], chip * 384, 384, axis=0)
    small_names = _SMALL_REPL + _SMALL_SHARDED
    g_stored = [_stored(n, summed[n].reshape(args[n].shape)) for n in small_names]
    upd = _adamw_small([_stored(n, args[n]) for n in small_names], g_stored,
                       [_stored(n, args["m_" + n]) for n in small_names], [_stored(n, args["v_" + n]) for n in small_names])
    for dst, arrs in zip((grads, delta, new_m, new_v), (g_stored,) + tuple(upd)):
        dst.update({n: _unstored(n, a) for n, a in zip(small_names, arrs)})

    outs = [loss, dx[None]]
    for group in (grads, delta, new_m, new_v):
        outs += [group[n] for n in _WEIGHTS]
    return tuple(outs)
```

```python
import functools

import jax
import jax.numpy as jnp
from jax import lax
from jax.experimental import pallas as pl
from jax.experimental.pallas import tpu as pltpu

F32 = jnp.float32
BF16 = jnp.bfloat16

SEQ = 2048
D_MODEL = 1024
CHUNK = 64
N_CHUNK = SEQ // CHUNK
HEADS = 4
GLA_DK = 64
GLA_DV = 128
ML_DH = 128
D_FF = 4096
EPS = 1e-6
N_CHIP = 4
N_DEV = 8
TOK_TILE = 256
N_TOK_TILE = SEQ // TOK_TILE
SWEEP = 2
assert CHUNK == 64
N_SWEEP = N_CHUNK // SWEEP

PM_W = 2688
PM_XM = 1536
PM_OP = 2048
PM_AL = 2560
GAB_W = 2048
D_IN = 4624
IN_SHARD = D_IN // N_CHIP
IN_ALOW = 1536
IN_XM = 1552
IN_GATES = 2576

ADAM_LR = 0.001
ADAM_B1 = 0.9
ADAM_B2 = 0.999
ADAM_EPS = 1e-08
ADAM_WD = 0.01
ADAM_STEP = 10

VMEM_LIMIT = 56 * 1024 * 1024


def _params(sem=None):
    return pltpu.CompilerParams(dimension_semantics=sem, vmem_limit_bytes=VMEM_LIMIT)


def _dot(a, b, ca, cb):
    return lax.dot_general(a.astype(BF16), b.astype(BF16), (((ca,), (cb,)), ((), ())), preferred_element_type=F32)


def _pmm_nn(a, b):
    return _dot(a, b, 1, 0)


def _pmm_nt(a, b):
    return _dot(a, b, 1, 1)


def _pmm_tn(a, b):
    return _dot(a, b, 0, 0)


def _pcmm(c, x):
    return lax.dot_general(c, x, (((1,), (0,)), ((), ())), precision=lax.Precision.HIGHEST, preferred_element_type=F32)


@jax.custom_vjp
def _mm_nn(a, b):
    return _dot(a, b, 1, 0)


@jax.custom_vjp
def _mm_nt(a, b):
    return _dot(a, b, 1, 1)


@jax.custom_vjp
def _mm_tn(a, b):
    return _dot(a, b, 0, 0)


_mm_nn.defvjp(lambda a, b: (_dot(a, b, 1, 0), (a, b)), lambda r, g: (_mm_nt(g, r[1]), _mm_tn(r[0], g)))
_mm_nt.defvjp(lambda a, b: (_dot(a, b, 1, 1), (a, b)), lambda r, g: (_mm_nn(g, r[1]), _mm_tn(g, r[0])))
_mm_tn.defvjp(lambda a, b: (_dot(a, b, 0, 0), (a, b)), lambda r, g: (_mm_nt(r[1], g), _mm_nn(r[0], g)))


@jax.custom_vjp
def _cmm(c, x):
    return _pcmm(c, x)


_cmm.defvjp(
    lambda c, x: (_pcmm(c, x), c),
    lambda c, g: (jnp.zeros_like(c), lax.dot_general(c, g, (((0,), (0,)), ((), ())), precision=lax.Precision.HIGHEST,
                                                      preferred_element_type=F32)),
)

_PLAIN_OPS = (_pmm_nn, _pmm_nt, _pmm_tn, _pcmm)
_VJP_OPS = (_mm_nn, _mm_nt, _mm_tn, _cmm)


def _sigmoid(x):
    return 0.5 * (jnp.tanh(0.5 * x) + 1.0)


def _log_sigmoid(x):
    return jnp.minimum(x, 0.0) - jnp.log(1.0 + jnp.exp(-jnp.abs(x)))


def _mean(x):
    return jnp.mean(x, axis=-1, keepdims=True)


def _nt(a, b):
    return lax.dot_general(a, b, (((1,), (1,)), ((), ())), preferred_element_type=F32)


def _tn(a, b):
    return lax.dot_general(a, b, (((0,), (0,)), ((), ())), preferred_element_type=F32)


def _mixer_chunk(ops, p, st, pm, xprev8):
    mm_nn, mm_nt, mm_tn, cmm = ops
    n_rows = pm.shape[0]
    n_ch = n_rows // CHUNK
    row = lax.broadcasted_iota(jnp.int32, (n_rows, n_rows), 0)
    col = lax.broadcasted_iota(jnp.int32, (n_rows, n_rows), 1)
    tri = jnp.logical_and((row >> 6) == (col >> 6), row >= col).astype(F32)
    causal = tri[0:CHUNK, 0:CHUNK] > 0.0
    q = pm[:, 0:256]
    k = pm[:, 256:512]
    v = pm[:, 512:1024]
    g = pm[:, 1024:1536]
    xm = pm[:, PM_XM:PM_XM + 512]
    opre = pm[:, PM_OP:PM_OP + 512]
    alow = pm[:, PM_AL:PM_AL + 128]
    hs = range(HEADS)
    cs = range(n_ch)
    pairs = [(i, h) for i in cs for h in hs]
    rs = [slice(i * CHUNK, (i + 1) * CHUNK) for i in cs]
    last = [slice((i + 1) * CHUNK - 1, (i + 1) * CHUNK) for i in cs]
    s6 = [slice(h * GLA_DK, (h + 1) * GLA_DK) for h in hs]
    s12 = [slice(h * 128, (h + 1) * 128) for h in hs]

    xx = jnp.concatenate([xprev8, xm], axis=0)
    pre = p["cb"]
    for j in range(4):
        pre = pre + p["cw"][j:j + 1, :] * xx[5 + j:5 + j + n_rows, :]
    xc = pre * _sigmoid(pre)
    qm = [mm_nn(xc[:, s12[h]], p["wq"][h]) for h in hs]
    km = [mm_nn(xc[:, s12[h]], p["wk"][h]) for h in hs]
    vm = [mm_nn(xm[:, s12[h]], p["wv"][h]) for h in hs]
    qcat = jnp.concatenate(qm, axis=1)
    kcat = jnp.concatenate(km, axis=1)
    vcat = jnp.concatenate(vm, axis=1)
    gates = (mm_nn(qcat, p["wif"][0:512]) + mm_nn(kcat, p["wif"][512:1024]) + mm_nn(vcat, p["wif"][1024:1536])
             + p["bif"])
    lf = _log_sigmoid(gates)
    fc = cmm(tri, lf)
    gates_t = gates.T
    fc_t = fc.T

    la = _log_sigmoid(mm_nn(alow, p["wau"]) + p["bau"]) * (1.0 / 16.0)
    cum = cmm(tri, la)
    cum_last = [cum[last[i], :] for i in cs]
    to_end = jnp.concatenate([cum_last[i] - cum[rs[i], :] for i in cs], axis=0)
    e_pos = jnp.exp(cum)
    e_neg = jnp.exp(-cum)
    qs = q * (GLA_DK ** -0.5)
    qp = qs * e_pos
    qn = qs * e_neg
    kp = k * e_pos
    kn = k * e_neg
    kl = k * jnp.exp(to_end)
    dec = [jnp.exp(cum_last[i]) for i in cs]
    ks = [km[h] * (ML_DH ** -0.5) for h in hs]
    li_c = {(i, h): gates[rs[i], h:h + 1] for i, h in pairs}
    fc_c = {(i, h): fc[rs[i], 4 + h:5 + h] for i, h in pairs}
    f_last = {(i, h): fc[last[i], 4 + h:5 + h] for i, h in pairs}

    a_fwd = {(i, h): mm_nt(qp[rs[i], s6[h]], kn[rs[i], s6[h]]) for i, h in pairs}
    a_bwd = {(i, h): mm_nt(qn[rs[i], s6[h]], kp[rs[i], s6[h]]) for i, h in pairs}
    s_chunk = {(i, h): mm_tn(v[rs[i], s12[h]], kl[rs[i], s6[h]]) for i, h in pairs}
    qk = {(i, h): mm_nt(qm[h][rs[i]], ks[h][rs[i]]) for i, h in pairs}
    a = {ih: f_last[ih] - fc_c[ih] + li_c[ih] for ih in pairs}
    m_loc = {ih: jnp.max(a[ih], axis=0, keepdims=True) for ih in pairs}
    kw = {(i, h): ks[h][rs[i]] * jnp.exp(a[(i, h)] - m_loc[(i, h)]) for i, h in pairs}
    c_chunk = {(i, h): mm_tn(kw[(i, h)], vm[h][rs[i]]) for i, h in pairs}
    mem = {(0, h): st["S"][h] for h in hs}
    c_in = {(0, h): st["C"][h] for h in hs}
    n_in = {(0, h): st["n"][h] for h in hs}
    m_in = {(0, h): st["m"][h][:, 0:1] for h in hs}
    for i, h in pairs:
        mem[(i + 1, h)] = mem[(i, h)] * dec[i][:, s6[h]] + s_chunk[(i, h)]
        m_nx = jnp.maximum(f_last[(i, h)] + m_in[(i, h)], m_loc[(i, h)])
        sp = jnp.exp(f_last[(i, h)] + m_in[(i, h)] - m_nx)
        sl = jnp.exp(m_loc[(i, h)] - m_nx)
        c_in[(i + 1, h)] = sp * c_in[(i, h)] + sl * c_chunk[(i, h)]
        n_in[(i + 1, h)] = sp * n_in[(i, h)] + sl * jnp.sum(kw[(i, h)], axis=0, keepdims=True)
        m_in[(i + 1, h)] = m_nx
    s_new = [mem[(n_ch, h)] for h in hs]
    o_inter = {(i, h): mm_nt(qp[rs[i], s6[h]], mem[(i, h)]) for i, h in pairs}
    q_c = {(i, h): mm_nn(qm[h][rs[i]], c_in[(i, h)]) for i, h in pairs}
    scores = {ih: jnp.where(causal, a_fwd[ih], a_bwd[ih]) for ih in pairs}
    log_d = {(i, h): gates_t[h:h + 1, rs[i]] - jnp.abs(fc_c[(i, h)] - fc_t[4 + h:5 + h, rs[i]]) for i, h in pairs}
    g_int = {ih: fc_c[ih] + m_in[ih] for ih in pairs}
    m_t = {ih: jnp.maximum(g_int[ih], jnp.max(log_d[ih], axis=1, keepdims=True)) for ih in pairs}
    s = {ih: qk[ih] * jnp.exp(log_d[ih] - m_t[ih]) for ih in pairs}
    scl = {ih: jnp.exp(g_int[ih] - m_t[ih]) for ih in pairs}
    o = {(i, h): mm_nn(scores[(i, h)], v[rs[i], s12[h]]) + o_inter[(i, h)] for i, h in pairs}
    num = {(i, h): mm_nn(s[(i, h)], vm[h][rs[i]]) + scl[(i, h)] * q_c[(i, h)] for i, h in pairs}
    o = {ih: o[ih] * lax.rsqrt(_mean(o[ih] * o[ih]) + EPS) * p["ggla"] for ih in pairs}
    gate = g * _sigmoid(g)
    out_a = {(i, h): o[(i, h)] * gate[rs[i], s12[h]] for i, h in pairs}
    den = {(i, h): jnp.sum(s[(i, h)], axis=1, keepdims=True)
           + scl[(i, h)] * jnp.sum(qm[h][rs[i]] * n_in[(i, h)], axis=1, keepdims=True) for i, h in pairs}
    den = {ih: jnp.maximum(jnp.abs(den[ih]), jnp.exp(-m_t[ih])) for ih in pairs}
    open_gate = _sigmoid(opre)
    hc = {(i, h): num[(i, h)] / den[(i, h)] * open_gate[rs[i], s12[h]] for i, h in pairs}
    d0 = {ih: hc[ih] - _mean(hc[ih]) for ih in pairs}
    y = {ih: d0[ih] * lax.rsqrt(_mean(d0[ih] * d0[ih]) + EPS) for ih in pairs}
    skipped = p["skip"] * xc
    out_b = {(i, h): y[(i, h)] * p["gml"][:, s12[h]] + skipped[rs[i], s12[h]] for i, h in pairs}
    ab = jnp.concatenate([jnp.concatenate([out_a[(i, h)] for h in hs] + [out_b[(i, h)] for h in hs], axis=1) for i in cs],
                         axis=0)
    new = {"S": s_new, "C": [c_in[(n_ch, h)] for h in hs], "n": [n_in[(n_ch, h)] for h in hs],
           "m": [jnp.broadcast_to(m_in[(n_ch, h)], (1, ML_DH)) for h in hs]}
    return ab, new


_P_NAMES = ("wau", "bau", "ggla", "cw", "cb", "wq", "wk", "wv", "wif", "bif", "skip", "gml")
_P_SHAPES = {
    "wau": (128, 256), "bau": (1, 256), "ggla": (1, 128), "cw": (4, 512), "cb": (1, 512),
    "wq": (512, 128), "wk": (512, 128), "wv": (512, 128),
    "wif": (1536, 128), "bif": (1, 128), "skip": (1, 512), "gml": (1, 512),
}
_P_BLOCKDIAG = ("wq", "wk", "wv")
_S_NAMES = ("S", "C", "n", "m")
_S_SHAPES = {"S": (HEADS, GLA_DV, GLA_DK), "C": (HEADS, ML_DH, ML_DH), "n": (HEADS, 1, ML_DH), "m": (HEADS, 1, ML_DH)}


def _per_head(ref):
    return [ref[h] for h in range(HEADS)]


def _block_mask():
    r = lax.broadcasted_iota(jnp.int32, (128, 128), 0)
    c = lax.broadcasted_iota(jnp.int32, (128, 128), 1)
    same_block = (r >> 2) == (c >> 2)
    spread = jnp.logical_and(r < 4, (c & 3) == r)
    return same_block.astype(F32), spread.astype(F32)


def _expand_blockdiag(w_ref, dense_ref):
    same_block, spread = _block_mask()
    for h in range(HEADS):
        tiled = _pmm_nn(w_ref[h * 128:(h + 1) * 128, :], spread)
        dense_ref[h] = tiled * same_block


def _collect_blockdiag(ddense_ref, dw_ref):
    same_block, spread = _block_mask()
    for h in range(HEADS):
        dw_ref[h * 128:(h + 1) * 128, :] = lax.dot_general(
            ddense_ref[h] * same_block, spread, (((1,), (1,)), ((), ())), precision=lax.Precision.HIGHEST,
            preferred_element_type=F32)


def _const_spec(shape):
    zeros = (0,) * len(shape)
    return pl.BlockSpec(shape, lambda i: zeros)


def _split(refs, *counts):
    out, at = [], 0
    for c in counts:
        out.append(refs[at:at + c])
        at += c
    assert at == len(refs)
    return out


def _ride(rider, phases, cond, ins, outs, sems):
    if rider is None or not any(hasattr(rider, phase) for phase in phases):
        return
    lands, (send_sems, recv_sems, flush_sems) = sems[:-3], sems[-3:]

    @pl.when(cond)
    def _():
        for phase in phases:
            getattr(rider, phase)(ins, lands, send_sems, recv_sems)
            if hasattr(rider, "flush"):
                rider.flush(phase, lands, outs, flush_sems)
        if "last" in phases and not hasattr(rider, "flush"):
            flush = [pltpu.make_async_copy(lands[k], outs[k], flush_sems.at[k]) for k in range(len(outs))]
            for cp in flush:
                cp.start()
            for cp in flush:
                cp.wait()


def _middle_step(rider, n_steps):
    return min(n_steps - 2, int(getattr(rider, "middle_at", 1.0) * n_steps))


def _rider_specs(rider, rider_ins):
    if rider is None:
        return [], [], [], []
    scratch = [pltpu.VMEM(s.shape, s.dtype) for s in list(rider.out_shape) + list(getattr(rider, "work_shape", ()))]
    scratch += [pltpu.SemaphoreType.DMA((rider.n_sems,)), pltpu.SemaphoreType.DMA((rider.n_sems,)),
                pltpu.SemaphoreType.DMA((getattr(rider, "n_flush", len(rider.out_shape)),))]
    in_space = getattr(rider, "in_space", VMEM_WHOLE)
    in_specs = list(in_space) if isinstance(in_space, (list, tuple)) else [in_space] * len(rider_ins)
    return in_specs, [ANY] * len(rider.out_shape), list(rider.out_shape), scratch


def _mixer_fwd(pm, p, rider=None, rider_ins=()):
    n_p = len(_P_NAMES)
    r_in, r_out_specs, r_out_shape, r_sems = _rider_specs(rider, rider_ins)

    def body(*refs):
        (pm_ref, xprev_ref), p_list, ride_in, (ab_ref,), so_refs, ride_out, sc_refs, dense_list, sems = _split(
            refs, 2, n_p, len(r_in), 1, 4, len(r_out_specs), 4, 3, len(r_sems))
        p_refs = dict(zip(_P_NAMES, p_list))
        dense = dict(zip(_P_BLOCKDIAG, dense_list))
        n = pl.program_id(0)
        _ride(rider, ("first",), n == 0, ride_in, ride_out, sems)

        @pl.when(n == 0)
        def _():
            for r in sc_refs:
                r[...] = jnp.zeros_like(r)
            for nm in _P_BLOCKDIAG:
                _expand_blockdiag(p_refs[nm], dense[nm])

        st = {name: _per_head(r) for name, r in zip(_S_NAMES, sc_refs)}
        pv = {nm: (_per_head(dense[nm]) if nm in _P_BLOCKDIAG else p_refs[nm][...]) for nm in _P_NAMES}
        for name, r in zip(_S_NAMES, so_refs):
            for h in range(HEADS):
                r[0, h] = st[name][h]
        xprev8 = jnp.where(n > 0, xprev_ref[CHUNK - 8:CHUNK, :], 0.0)
        ab, st = _mixer_chunk(_PLAIN_OPS, pv, st, pm_ref[...], xprev8)
        ab_ref[...] = ab.astype(BF16)
        for name, r in zip(_S_NAMES, sc_refs):
            for h in range(HEADS):
                r[h] = st[name][h]
        _ride(rider, ("middle",), n == _middle_step(rider, N_SWEEP), ride_in, ride_out, sems)
        _ride(rider, ("last",), n == N_SWEEP - 1, ride_in, ride_out, sems)

    in_specs = [pl.BlockSpec((SWEEP * CHUNK, PM_W), lambda i: (i, 0)),
                pl.BlockSpec((CHUNK, 512), lambda i: (jnp.maximum(SWEEP * i - 1, 0), PM_XM // 512))]
    in_specs += [_const_spec(_P_SHAPES[nm]) for nm in _P_NAMES] + r_in
    out_specs = [pl.BlockSpec((SWEEP * CHUNK, 1024), lambda i: (i, 0))]
    out_shape = [jax.ShapeDtypeStruct((SEQ, 1024), BF16)]
    for nm in _S_NAMES:
        shp = _S_SHAPES[nm]
        out_specs.append(pl.BlockSpec((1,) + shp, lambda i: (i, 0, 0, 0)))
        out_shape.append(jax.ShapeDtypeStruct((N_SWEEP,) + shp, F32))
    return pl.pallas_call(
        body, grid=(N_SWEEP,), in_specs=in_specs, out_specs=out_specs + r_out_specs, out_shape=out_shape + r_out_shape,
        scratch_shapes=[pltpu.VMEM(_S_SHAPES[nm], F32) for nm in _S_NAMES]
        + [pltpu.VMEM((HEADS, 128, 128), F32) for _ in _P_BLOCKDIAG] + r_sems,
        compiler_params=_params(("arbitrary",)), name="mixer_fwd",
    )(pm, pm, *[p[nm] for nm in _P_NAMES], *rider_ins)


def _mixer_bwd(pm, dab, states, p, rider=None, rider_ins=()):
    n_p = len(_P_NAMES)
    r_in, r_out_specs, r_out_shape, r_sems = _rider_specs(rider, rider_ins)

    def body(*refs):
        ((pm_ref, xprev_ref, dab_ref), si_refs, p_list, ride_in, (dpm_ref,), dp_list, ride_out, ds_refs, (carry_ref,),
         dense_list, ddense_list, sems) = _split(refs, 3, 4, n_p, len(r_in), 1, n_p, len(r_out_specs), 4, 1, 3, 3, len(r_sems))
        p_refs = dict(zip(_P_NAMES, p_list))
        dp_refs = dict(zip(_P_NAMES, dp_list))
        dense = dict(zip(_P_BLOCKDIAG, dense_list))
        ddense = dict(zip(_P_BLOCKDIAG, ddense_list))
        i = pl.program_id(0)
        blk = N_SWEEP - 1 - i
        _ride(rider, ("first",), i == 0, ride_in, ride_out, sems)

        @pl.when(i == 0)
        def _():
            for r in ds_refs:
                r[...] = jnp.zeros_like(r)
            for nm in _P_NAMES:
                if nm in _P_BLOCKDIAG:
                    ddense[nm][...] = jnp.zeros_like(ddense[nm])
                    _expand_blockdiag(p_refs[nm], dense[nm])
                else:
                    dp_refs[nm][...] = jnp.zeros_like(dp_refs[nm])
            carry_ref[...] = jnp.zeros_like(carry_ref)

        pv = {nm: (_per_head(dense[nm]) if nm in _P_BLOCKDIAG else p_refs[nm][...]) for nm in _P_NAMES}
        dst = {name: _per_head(r) for name, r in zip(_S_NAMES, ds_refs)}
        st = {name: [r[0, h] for h in range(HEADS)] for name, r in zip(_S_NAMES, si_refs)}
        xprev8 = jnp.where(blk > 0, xprev_ref[CHUNK - 8:CHUNK, :], 0.0)
        _, vjp = jax.vjp(functools.partial(_mixer_chunk, _VJP_OPS), pv, st, pm_ref[...], xprev8)
        dp_sum, dst, dpm, dxprev8 = vjp((dab_ref[...], dst))
        reach = jnp.concatenate([jnp.zeros((SWEEP * CHUNK - 8, 512), F32), carry_ref[...]], axis=0)
        dpm_ref[:, 0:PM_XM] = dpm[:, 0:PM_XM].astype(BF16)
        dpm_ref[:, PM_XM:PM_XM + 512] = (dpm[:, PM_XM:PM_XM + 512] + reach).astype(BF16)
        dpm_ref[:, PM_XM + 512:PM_W] = dpm[:, PM_XM + 512:PM_W].astype(BF16)
        carry_ref[...] = dxprev8
        for name, r in zip(_S_NAMES, ds_refs):
            for h in range(HEADS):
                r[h] = dst[name][h]
        for nm in _P_NAMES:
            if nm in _P_BLOCKDIAG:
                for h in range(HEADS):
                    ddense[nm][h] += dp_sum[nm][h]
            else:
                dp_refs[nm][...] += dp_sum[nm]

        @pl.when(i == N_SWEEP - 1)
        def _():
            for nm in _P_BLOCKDIAG:
                _collect_blockdiag(ddense[nm], dp_refs[nm])

        _ride(rider, ("early",), i == 1, ride_in, ride_out, sems)
        _ride(rider, ("middle",), i == _middle_step(rider, N_SWEEP), ride_in, ride_out, sems)
        _ride(rider, ("last",), i == N_SWEEP - 1, ride_in, ride_out, sems)

    rev = lambda i: (N_SWEEP - 1 - i, 0)
    in_specs = [pl.BlockSpec((SWEEP * CHUNK, PM_W), rev),
                pl.BlockSpec((CHUNK, 512), lambda i: (jnp.maximum(SWEEP * (N_SWEEP - 1 - i) - 1, 0), PM_XM // 512)),
                pl.BlockSpec((SWEEP * CHUNK, 1024), rev)]
    for nm in _S_NAMES:
        in_specs.append(pl.BlockSpec((1,) + _S_SHAPES[nm], lambda i: (N_SWEEP - 1 - i, 0, 0, 0)))
    in_specs += [_const_spec(_P_SHAPES[nm]) for nm in _P_NAMES] + r_in
    out_specs = [pl.BlockSpec((SWEEP * CHUNK, PM_W), rev)] + [_const_spec(_P_SHAPES[nm]) for nm in _P_NAMES]
    out_shape = [jax.ShapeDtypeStruct((SEQ, PM_W), BF16)] + [jax.ShapeDtypeStruct(_P_SHAPES[nm], F32) for nm in _P_NAMES]
    res = pl.pallas_call(
        body, grid=(N_SWEEP,), in_specs=in_specs, out_specs=out_specs + r_out_specs, out_shape=out_shape + r_out_shape,
        scratch_shapes=[pltpu.VMEM(_S_SHAPES[nm], F32) for nm in _S_NAMES] + [pltpu.VMEM((8, 512), F32)]
        + [pltpu.VMEM((HEADS, 128, 128), F32) for _ in range(2 * len(_P_BLOCKDIAG))] + r_sems,
        compiler_params=_params(("arbitrary",)), name="mixer_bwd",
    )(pm, pm, dab, *states, *[p[nm] for nm in _P_NAMES], *rider_ins)
    return res[0], dict(zip(_P_NAMES, res[1:1 + n_p])), res[1 + n_p:]


def _tok(width):
    return pl.BlockSpec((TOK_TILE, width), lambda i: (i, 0))


def _once(shape):
    zeros = (0,) * len(shape)
    return pl.BlockSpec(shape, lambda i: zeros, pipeline_mode=pl.Buffered(1))


def _rms_fwd(x):
    r = lax.rsqrt(_mean(x * x) + EPS)
    return x * r, r


def _rms_bwd(dy, xn, r, g):
    gd = dy * g
    return r * (gd - xn * _mean(xn * gd))


def _tiled_call(body, in_specs, out_specs, out_shape, args, name, rider=None, rider_ins=(), scratch=()):
    r_in, r_out_specs, r_out_shape, r_scratch = _rider_specs(rider, rider_ins)
    n_in, n_out = len(in_specs), len(out_specs)

    def hosted(*refs):
        ins, ride_in, outs, ride_out, own, r_scr = _split(refs, n_in, len(r_in), n_out, len(r_out_specs), len(scratch),
                                                          len(r_scratch))
        i = pl.program_id(0)
        _ride(rider, ("first",), i == 0, ride_in, ride_out, r_scr)
        body(*ins, *outs, *own)
        _ride(rider, ("early",), i == 1, ride_in, ride_out, r_scr)
        _ride(rider, ("middle",), i == _middle_step(rider, N_TOK_TILE), ride_in, ride_out, r_scr)
        _ride(rider, ("last",), i == N_TOK_TILE - 1, ride_in, ride_out, r_scr)

    res = pl.pallas_call(
        hosted, grid=(N_TOK_TILE,), in_specs=list(in_specs) + r_in, out_specs=list(out_specs) + r_out_specs,
        out_shape=list(out_shape) + r_out_shape, scratch_shapes=list(scratch) + r_scratch,
        compiler_params=_params(("arbitrary",)), name=name,
    )(*args, *rider_ins)
    return res[:n_out], res[n_out:]


_IN_CHUNKS = ((0, IN_ALOW), (IN_ALOW, IN_GATES), (IN_GATES, IN_GATES + 1024), (IN_GATES + 1024, D_IN))


def _in_scratch():
    return [pltpu.VMEM((D_IN, D_MODEL), BF16), pltpu.SemaphoreType.DMA((len(_IN_CHUNKS),))]


def _chunked_load(src, dst, sems, bounds):
    first = pl.program_id(0) == 0

    def cp(k):
        lo, hi = bounds[k]
        return pltpu.make_async_copy(src.at[pl.ds(lo, hi - lo)], dst.at[pl.ds(lo, hi - lo)], sems.at[k])

    def start():
        @pl.when(first)
        def _():
            for k in range(min(2, len(bounds))):
                cp(k).start()

    def fetch(k):
        @pl.when(first)
        def _():
            cp(k).wait()
            if k + 2 < len(bounds):
                cp(k + 2).start()

    return start, fetch


def _in_proj(x, g_pre, wt_in, rider=None, rider_ins=()):
    def body(x_ref, g_ref, wt_hbm, pm_ref, gab_ref, h_ref, wt_ref, sems):
        start, fetch = _chunked_load(wt_hbm, wt_ref, sems, _IN_CHUNKS)
        start()
        xn, _ = _rms_fwd(x_ref[...])
        h = (xn * g_ref[...]).astype(BF16)
        h_ref[...] = h
        fetch(0)
        pm_ref[:, 0:PM_XM] = _nt(h, wt_ref[0:IN_ALOW, :])
        fetch(1)
        pm_ref[:, PM_XM:PM_AL] = _nt(h, wt_ref[IN_XM:IN_GATES, :])
        pm_ref[:, PM_AL:PM_W] = _nt(h, wt_ref[IN_ALOW:IN_ALOW + 128, :])
        for k in (2, 3):
            lo, hi = _IN_CHUNKS[k]
            fetch(k)
            gab_ref[:, lo - IN_GATES:hi - IN_GATES] = _nt(h, wt_ref[lo:hi, :])

    return _tiled_call(
        body, [_tok(D_MODEL), _once((1, D_MODEL)), ANY], [_tok(PM_W), _tok(GAB_W), _tok(D_MODEL)],
        [jax.ShapeDtypeStruct((SEQ, PM_W), F32), jax.ShapeDtypeStruct((SEQ, GAB_W), F32),
         jax.ShapeDtypeStruct((SEQ, D_MODEL), BF16)], (x, g_pre, wt_in), "in_proj", rider, rider_ins, _in_scratch())


def _merge_fwd(ab, gab, x, w_pa4, w_pb4, w_o, g_post, rider=None, rider_ins=()):
    def body(ab_ref, gab_ref, x_ref, wpa_ref, wpb_ref, wo_ref, g_ref, x1_ref, mix_ref, mg_ref):
        a = ab_ref[:, 0:512]
        b = ab_ref[:, 512:1024]
        for j in range(N_CHIP):
            blk = slice(j * 256, (j + 1) * 256)
            ya = jnp.dot(a, wpa_ref[j], preferred_element_type=F32)
            yb = jnp.dot(b, wpb_ref[j], preferred_element_type=F32)
            sa = _sigmoid(gab_ref[:, j * 256:(j + 1) * 256])
            sb = _sigmoid(gab_ref[:, 1024 + j * 256:1024 + (j + 1) * 256])
            mg_ref[:, blk] = (sa * ya + sb * yb).astype(BF16)
        mix = jnp.dot(mg_ref[...], wo_ref[...], preferred_element_type=F32)
        mix_ref[...] = mix
        mn, _ = _rms_fwd(mix)
        x1_ref[...] = x_ref[...] + mn * g_ref[...]

    return _tiled_call(
        body, [_tok(1024), _tok(GAB_W), _tok(D_MODEL), _once((N_CHIP, 512, 256)), _once((N_CHIP, 512, 256)),
               _once((D_MODEL, D_MODEL)), _once((1, D_MODEL))], [_tok(D_MODEL), _tok(D_MODEL), _tok(D_MODEL)],
        [jax.ShapeDtypeStruct((SEQ, D_MODEL), F32), jax.ShapeDtypeStruct((SEQ, D_MODEL), F32),
         jax.ShapeDtypeStruct((SEQ, D_MODEL), BF16)], (ab, gab, x, w_pa4, w_pb4, w_o, g_post), "merge_fwd", rider, rider_ins)


def _mlp(x1, target, g_pre, g_post, w_up4, w_down_a4, w_down_b4):
    def body(x1_ref, t_ref, gpre_ref, gpost_ref, wup_hbm, wda_hbm, wdb_hbm,
             dx1_ref, u_ref, dd_ref, h2_ref, dpre_ref, dgpost_ref, dgpre_ref, loss_ref, wup_ref, wda_ref, wdb_ref, sems):
        first = pl.program_id(0) == 0

        def load(j):
            pairs = ((wup_hbm, wup_ref), (wda_hbm, wda_ref), (wdb_hbm, wdb_ref))
            return [pltpu.make_async_copy(src.at[j], dst.at[j], sems.at[3 * j + t]) for t, (src, dst) in enumerate(pairs)]

        def fetch(j, then=None):
            @pl.when(first)
            def _():
                for cp in load(j):
                    cp.wait()
                if then is not None:
                    for cp in load(then):
                        cp.start()

        @pl.when(first)
        def _():
            for j in (0, 1):
                for cp in load(j):
                    cp.start()
            dgpost_ref[...] = jnp.zeros_like(dgpost_ref)
            dgpre_ref[...] = jnp.zeros_like(dgpre_ref)
            loss_ref[...] = jnp.zeros_like(loss_ref)

        x1 = x1_ref[...]
        gpre = gpre_ref[...]
        gpost = gpost_ref[...]
        xn2, r2 = _rms_fwd(x1)
        h2 = (xn2 * gpre).astype(BF16)
        h2_ref[...] = h2
        rl = []
        d = jnp.zeros((TOK_TILE, D_MODEL), F32)
        for j in range(N_CHIP):
            blk = slice(j * 1024, (j + 1) * 1024)
            fetch(j, then=j + 2 if j + 2 < N_CHIP else None)
            r = jnp.maximum(jnp.dot(h2, wup_ref[j], preferred_element_type=F32), 0.0)
            rl.append(r)
            u = (r * r).astype(BF16)
            u_ref[:, blk] = u
            d = d + jnp.dot(u[:, 0:512], wda_ref[j], preferred_element_type=F32)
            d = d + jnp.dot(u[:, 512:1024], wdb_ref[j], preferred_element_type=F32)
        dn, r3 = _rms_fwd(d)
        diff = x1 + dn * gpost - t_ref[...]
        loss_ref[...] += jnp.sum(diff * diff, keepdims=True) * (0.5 / D_MODEL)
        dy = diff * (1.0 / D_MODEL)
        dgpost_ref[...] += jnp.sum(dy * dn, axis=0, keepdims=True)
        dd = _rms_bwd(dy, dn, r3, gpost).astype(BF16)
        dd_ref[...] = dd
        dh2 = jnp.zeros((TOK_TILE, D_MODEL), F32)
        for j in range(N_CHIP):
            blk = slice(j * 1024, (j + 1) * 1024)
            du = jnp.concatenate([_nt(dd, wda_ref[j]), _nt(dd, wdb_ref[j])], axis=1)
            dpre = (du * (2.0 * rl[j])).astype(BF16)
            dpre_ref[:, blk] = dpre
            dh2 = dh2 + _nt(dpre, wup_ref[j])
        dgpre_ref[...] += jnp.sum(dh2 * xn2, axis=0, keepdims=True)
        dx1_ref[...] = dy + _rms_bwd(dh2, xn2, r2, gpre)

    acc = pl.BlockSpec((1, D_MODEL), lambda i: (0, 0))
    return pl.pallas_call(
        body, grid=(N_TOK_TILE,),
        in_specs=[_tok(D_MODEL), _tok(D_MODEL), _once((1, D_MODEL)), _once((1, D_MODEL)), ANY, ANY, ANY],
        scratch_shapes=[pltpu.VMEM((N_CHIP, D_MODEL, 1024), BF16), pltpu.VMEM((N_CHIP, 512, D_MODEL), BF16),
                        pltpu.VMEM((N_CHIP, 512, D_MODEL), BF16), pltpu.SemaphoreType.DMA((3 * N_CHIP,))],
        out_specs=[_tok(D_MODEL), _tok(D_FF), _tok(D_MODEL), _tok(D_MODEL), _tok(D_FF), acc, acc,
                   pl.BlockSpec((1, 128), lambda i: (0, 0))],
        out_shape=[jax.ShapeDtypeStruct((SEQ, D_MODEL), F32), jax.ShapeDtypeStruct((SEQ, D_FF), BF16),
                   jax.ShapeDtypeStruct((SEQ, D_MODEL), BF16), jax.ShapeDtypeStruct((SEQ, D_MODEL), BF16),
                   jax.ShapeDtypeStruct((SEQ, D_FF), BF16), jax.ShapeDtypeStruct((1, D_MODEL), F32),
                   jax.ShapeDtypeStruct((1, D_MODEL), F32), jax.ShapeDtypeStruct((1, 128), F32)],
        compiler_params=_params(("arbitrary",)), name="mlp_fwd_bwd",
    )(x1, target, g_pre, g_post, w_up4, w_down_a4, w_down_b4)


def _merge_bwd(dx1, mix, ab, gab, merged, w_pa4, w_pb4, w_o, g_post):
    def body(dx1_ref, mix_ref, ab_ref, gab_ref, mg_ref, wpa_ref, wpb_ref, wo_ref, g_ref,
             dgab_ref, dab_ref, dg_ref, dwpa_ref, dwpb_ref, dwo_ref, acc_pa, acc_pb, acc_o):
        @pl.when(pl.program_id(0) == 0)
        def _():
            dg_ref[...] = jnp.zeros_like(dg_ref)
            acc_pa[...] = jnp.zeros_like(acc_pa)
            acc_pb[...] = jnp.zeros_like(acc_pb)
            acc_o[...] = jnp.zeros_like(acc_o)

        dx1 = dx1_ref[...]
        mn, r = _rms_fwd(mix_ref[...])
        dg_ref[...] += jnp.sum(dx1 * mn, axis=0, keepdims=True)
        dmix = _rms_bwd(dx1, mn, r, g_ref[...]).astype(BF16)
        acc_o[...] += _tn(mg_ref[...], dmix)
        dmerged = _nt(dmix, wo_ref[...])
        a = ab_ref[:, 0:512]
        b = ab_ref[:, 512:1024]
        da = jnp.zeros((TOK_TILE, 512), F32)
        db = jnp.zeros((TOK_TILE, 512), F32)
        dyas, dybs = [], []
        for j in range(N_CHIP):
            blk = slice(j * 256, (j + 1) * 256)
            blk_b = slice(1024 + j * 256, 1024 + (j + 1) * 256)
            dm = dmerged[:, blk]
            ya = jnp.dot(a, wpa_ref[j], preferred_element_type=F32)
            yb = jnp.dot(b, wpb_ref[j], preferred_element_type=F32)
            sa = _sigmoid(gab_ref[:, blk])
            sb = _sigmoid(gab_ref[:, blk_b])
            dya = (dm * sa).astype(BF16)
            dyb = (dm * sb).astype(BF16)
            dyas.append(dya)
            dybs.append(dyb)
            dgab_ref[:, blk] = (dm * ya * sa * (1.0 - sa)).astype(BF16)
            dgab_ref[:, blk_b] = (dm * yb * sb * (1.0 - sb)).astype(BF16)
            da = da + _nt(dya, wpa_ref[j])
            db = db + _nt(dyb, wpb_ref[j])
        dab_ref[:, 0:512] = da
        dab_ref[:, 512:1024] = db
        acc_pa[...] += _tn(a, jnp.concatenate(dyas, axis=1))
        acc_pb[...] += _tn(b, jnp.concatenate(dybs, axis=1))

        @pl.when(pl.program_id(0) == N_TOK_TILE - 1)
        def _():
            dwo_ref[...] = acc_o[...].astype(BF16)
            for j in range(N_CHIP):
                dwpa_ref[j] = acc_pa[:, j * 256:(j + 1) * 256].astype(BF16)
                dwpb_ref[j] = acc_pb[:, j * 256:(j + 1) * 256].astype(BF16)

    whole = lambda shape: pl.BlockSpec(shape, lambda i: (0,) * len(shape))
    return pl.pallas_call(
        body, grid=(N_TOK_TILE,),
        in_specs=[_tok(D_MODEL), _tok(D_MODEL), _tok(1024), _tok(GAB_W), _tok(D_MODEL), _once((N_CHIP, 512, 256)),
                  _once((N_CHIP, 512, 256)), _once((D_MODEL, D_MODEL)), _once((1, D_MODEL))],
        out_specs=[_tok(GAB_W), _tok(1024), whole((1, D_MODEL)), whole((N_CHIP, 512, 256)), whole((N_CHIP, 512, 256)),
                   whole((D_MODEL, D_MODEL))],
        out_shape=[jax.ShapeDtypeStruct((SEQ, GAB_W), BF16), jax.ShapeDtypeStruct((SEQ, 1024), F32),
                   jax.ShapeDtypeStruct((1, D_MODEL), F32), jax.ShapeDtypeStruct((N_CHIP, 512, 256), BF16),
                   jax.ShapeDtypeStruct((N_CHIP, 512, 256), BF16), jax.ShapeDtypeStruct((D_MODEL, D_MODEL), BF16)],
        scratch_shapes=[pltpu.VMEM((512, D_MODEL), F32), pltpu.VMEM((512, D_MODEL), F32),
                        pltpu.VMEM((D_MODEL, D_MODEL), F32)],
        compiler_params=_params(("arbitrary",)), name="merge_bwd",
    )(dx1, mix, ab, gab, merged, w_pa4, w_pb4, w_o, g_post)


def _in_proj_bwd(dpm, dgab, x, dx1, g_pre, wt_in, rider=None, rider_ins=()):
    def body(dpm_ref, dgab_ref, x_ref, dx1_ref, g_ref, wt_hbm, dx_ref, dg_ref, wt_ref, sems):
        start, fetch = _chunked_load(wt_hbm, wt_ref, sems, _IN_CHUNKS)
        start()

        @pl.when(pl.program_id(0) == 0)
        def _():
            dg_ref[...] = jnp.zeros_like(dg_ref)

        fetch(0)
        dh = jnp.dot(dpm_ref[:, 0:PM_XM], wt_ref[0:IN_ALOW, :], preferred_element_type=F32)
        fetch(1)
        dh = dh + jnp.dot(dpm_ref[:, PM_XM:PM_AL], wt_ref[IN_XM:IN_GATES, :], preferred_element_type=F32)
        dh = dh + jnp.dot(dpm_ref[:, PM_AL:PM_W], wt_ref[IN_ALOW:IN_ALOW + 128, :], preferred_element_type=F32)
        for k in (2, 3):
            lo, hi = _IN_CHUNKS[k]
            fetch(k)
            dh = dh + jnp.dot(dgab_ref[:, lo - IN_GATES:hi - IN_GATES], wt_ref[lo:hi, :], preferred_element_type=F32)
        xn, r = _rms_fwd(x_ref[...])
        dg_ref[...] += jnp.sum(dh * xn, axis=0, keepdims=True)
        dx_ref[...] = dx1_ref[...] + _rms_bwd(dh, xn, r, g_ref[...])

    return _tiled_call(
        body, [_tok(PM_W), _tok(GAB_W), _tok(D_MODEL), _tok(D_MODEL), _once((1, D_MODEL)), ANY],
        [_tok(D_MODEL), pl.BlockSpec((1, D_MODEL), lambda i: (0, 0))],
        [jax.ShapeDtypeStruct((SEQ, D_MODEL), F32), jax.ShapeDtypeStruct((1, D_MODEL), F32)],
        (dpm, dgab, x, dx1, g_pre, wt_in), "in_proj_bwd", rider, rider_ins, _in_scratch())


def _dw_in(dpm, dgab, h):
    n_pm = PM_AL // 512
    n_blk = n_pm + GAB_W // 512

    def body(dpm_ref, dgab_ref, dal_ref, h_ref, o_ref):
        i = pl.program_id(0)
        off = pl.multiple_of(i * 512 + 16 * (i >= 3).astype(jnp.int32), 16)

        @pl.when(i < n_pm)
        def _():
            o_ref[pl.ds(off, 512), :] = _tn(dpm_ref[...], h_ref[...]).astype(BF16)

        @pl.when(i >= n_pm)
        def _():
            o_ref[pl.ds(off, 512), :] = _tn(dgab_ref[...], h_ref[...]).astype(BF16)

        @pl.when(i == 0)
        def _():
            o_ref[IN_ALOW:IN_XM, :] = _tn(dal_ref[...], h_ref[...])[0:IN_XM - IN_ALOW].astype(BF16)

    return pl.pallas_call(
        body, grid=(n_blk,),
        in_specs=[pl.BlockSpec((SEQ, 512), lambda i: (0, jnp.minimum(i, n_pm - 1))),
                  pl.BlockSpec((SEQ, 512), lambda i: (0, jnp.maximum(i - n_pm, 0))),
                  pl.BlockSpec((SEQ, 128), lambda i: (0, PM_AL // 128)),
                  _once((SEQ, D_MODEL))],
        out_specs=pl.BlockSpec((D_IN, D_MODEL), lambda i: (0, 0)),
        out_shape=jax.ShapeDtypeStruct((D_IN, D_MODEL), BF16),
        compiler_params=_params(("arbitrary",)), name="dw_in",
    )(dpm, dgab, dpm, h)


def _tn_matmul(a, b, name, shards=1, tm=1024, rider=None, rider_ins=()):
    m, n = a.shape[1], b.shape[1]
    tm = min(tm, m)
    tn = n // shards if shards > 1 else min(n, 1024)
    steps_i, steps_j = m // tm, n // tn
    r_in, r_out_specs, r_out_shape, r_scratch = _rider_specs(rider, rider_ins)

    def body(*refs):
        (a_ref, b_ref), ride_in, (o_ref,), ride_out, scratch = _split(refs, 2, len(r_in), 1, len(r_out_specs), len(r_scratch))
        step = pl.program_id(0) * steps_j + pl.program_id(1)
        _ride(rider, ("first",), step == 0, ride_in, ride_out, scratch)
        o_ref[...] = _tn(a_ref[...], b_ref[...]).astype(BF16)
        _ride(rider, ("middle", "last"), step == steps_i * steps_j - 1, ride_in, ride_out, scratch)

    if shards > 1:
        out_spec = pl.BlockSpec((None, tm, tn), lambda i, j: (j, i, 0))
        out_shape = jax.ShapeDtypeStruct((shards, m, tn), BF16)
    else:
        out_spec = pl.BlockSpec((tm, tn), lambda i, j: (i, j))
        out_shape = jax.ShapeDtypeStruct((m, n), BF16)
    res = pl.pallas_call(
        body, grid=(steps_i, steps_j),
        in_specs=[pl.BlockSpec((SEQ, tm), lambda i, j: (0, i)), pl.BlockSpec((SEQ, tn), lambda i, j: (0, j))] + r_in,
        out_specs=[out_spec] + r_out_specs, out_shape=[out_shape] + r_out_shape, scratch_shapes=r_scratch,
        compiler_params=_params(("arbitrary", "arbitrary")), name=name,
    )(a, b, *rider_ins)
    return res[0] if rider is None else (res[0], res[1:])


MESH = pl.DeviceIdType.MESH
ANY = pl.BlockSpec(memory_space=pl.ANY)
VMEM_WHOLE = pl.BlockSpec(memory_space=pltpu.VMEM)

_BIG = ("w_in", "w_pa", "w_pb", "w_o", "w_up", "w_down")
_BIG_SHARD = {"w_in": (IN_SHARD, D_MODEL), "w_pa": (512, 256), "w_pb": (512, 256), "w_o": (256, D_MODEL),
              "w_up": (D_MODEL, 1024), "w_down": (1024, D_MODEL),
              "w_down_a": (512, D_MODEL), "w_down_b": (512, D_MODEL)}
_BIG_SPLIT = {"w_in": 1, "w_pa": 0, "w_pb": 0, "w_o": 0, "w_up": 0, "w_down": 0, "w_down_a": 0, "w_down_b": 0}


def _half(ref, e, name, lead=0, part=None):
    axis = _BIG_SPLIT[name]
    size = _BIG_SHARD[name][axis] // 2
    start = e * size
    if part is not None:
        size //= 2
        start = start + part * size
    start = pl.multiple_of(start, 128 if axis == 1 else 16)
    idx = [pl.ds(0, ref.shape[a]) for a in range(lead)]
    idx += [pl.ds(start, size), pl.ds(0, _BIG_SHARD[name][1])] if axis == 0 else [pl.ds(0, _BIG_SHARD[name][0]), pl.ds(start, size)]
    return ref.at[tuple(idx)]


def _half_shape(name):
    r, c = _BIG_SHARD[name]
    return (r // 2, c) if _BIG_SPLIT[name] == 0 else (r, c // 2)


def _remote(src, dst, send_sems, recv_sems, k, to):
    return pltpu.make_async_remote_copy(src_ref=src, dst_ref=dst, send_sem=send_sems.at[k], recv_sem=recv_sems.at[k],
                                        device_id=to, device_id_type=MESH)


def _mesh_place():
    x, y, c = lax.axis_index("x"), lax.axis_index("y"), lax.axis_index("c")
    return x, y, c, [(1 - x, y), (x, 1 - y), (1 - x, 1 - y)]


class _Gather:
    def __init__(self, names, small=(), middle_at=0.5):
        self.middle_at = middle_at
        self.names = tuple(names)
        self.nb = len(self.names)
        self.n = self.nb + len(small)
        self.n_sems = 8 * self.nb + 3 * len(small)
        self.n_flush = 6 * self.nb + len(small)
        self.out_shape = [jax.ShapeDtypeStruct((N_CHIP,) + _BIG_SHARD[nm], BF16) for nm in self.names]
        self.out_shape += [jax.ShapeDtypeStruct((N_CHIP,) + s.shape, s.dtype) for s in small]

    def _copies(self, ins, outs, ss, rs, k):
        x, y, c, _ = _mesh_place()
        name = self.names[k]
        me, xn, yn, dg = 2 * x + y, 2 * (1 - x) + y, 2 * x + (1 - y), 2 * (1 - x) + (1 - y)
        to_x, to_y, sibling = (1 - x, y, c), (x, 1 - y, c), (x, y, 1 - c)

        def region(slot, e, part=None):
            return _half(outs[k].at[slot], e, name, part=part)

        def copy(pair, src, dst, to):
            return _remote(src, dst, ss, rs, 8 * k + pair, to)

        mine = _half(ins[k], c, name)
        sent = [copy(0, mine, region(me, c), to_x), copy(1, mine, region(me, c), to_y),
                copy(2, region(xn, c, 0), region(xn, c, 0), to_y), copy(3, region(yn, c, 1), region(yn, c, 1), to_x),
                copy(4, region(xn, c), region(xn, c), sibling), copy(5, region(yn, c), region(yn, c), sibling),
                copy(6, region(dg, c, 0), region(dg, c, 0), sibling), copy(7, region(dg, c, 1), region(dg, c, 1), sibling)]
        landing = [region(xn, c), region(yn, c), region(dg, c, 0), region(dg, c, 1),
                   region(xn, 1 - c), region(yn, 1 - c), region(dg, 1 - c, 0), region(dg, 1 - c, 1)]
        received = [copy(pair, dst, dst, sibling) for pair, dst in enumerate(landing)]
        return sent, received

    def _small(self, ins, outs, ss, rs, k, j, peer, slot, c):
        return _remote(ins[k], outs[k].at[slot], ss, rs, 8 * self.nb + 3 * (k - self.nb) + j, (*peer, c))

    def flush(self, phase, lands, outs, fs):
        x, y, c, _ = _mesh_place()
        me, xn, yn, dg = 2 * x + y, 2 * (1 - x) + y, 2 * x + (1 - y), 2 * (1 - x) + (1 - y)

        def pieces(k):
            name = self.names[k]
            spots = [lambda r: r.at[me], lambda r: _half(r.at[xn], c, name), lambda r: _half(r.at[yn], c, name),
                     lambda r: _half(r.at[xn], 1 - c, name), lambda r: _half(r.at[yn], 1 - c, name), lambda r: r.at[dg]]
            return [pltpu.make_async_copy(spot(lands[k]), spot(outs[k]), fs.at[6 * k + t]) for t, spot in enumerate(spots)]

        ready = {"first": (0,), "middle": (1, 2), "last": (3, 4, 5)}[phase]
        for k in range(self.nb):
            cps = pieces(k)
            for t in ready:
                cps[t].start()
        if phase == "last":
            small = [pltpu.make_async_copy(lands[k], outs[k], fs.at[6 * self.nb + k - self.nb]) for k in range(self.nb, self.n)]
            for cp in small:
                cp.start()
            for k in range(self.nb):
                for cp in pieces(k):
                    cp.wait()
            for cp in small:
                cp.wait()

    def first(self, ins, outs, ss, rs):
        x, y, c, peers = _mesh_place()
        me = 2 * x + y
        for k in range(self.nb):
            sent, _ = self._copies(ins, outs, ss, rs, k)
            sent[0].start()
            sent[1].start()
        for k in range(self.nb, self.n):
            for j, peer in enumerate(peers):
                self._small(ins, outs, ss, rs, k, j, peer, me, c).start()
        for k in range(self.n):
            outs[k][me] = ins[k][...]

    def middle(self, ins, outs, ss, rs):
        for k in range(self.nb):
            sent, received = self._copies(ins, outs, ss, rs, k)
            for pair in (0, 1):
                received[pair].wait_recv()
                sent[2 + pair].start()
                sent[4 + pair].start()

    def last(self, ins, outs, ss, rs):
        x, y, c, peers = _mesh_place()
        for k in range(self.nb):
            sent, received = self._copies(ins, outs, ss, rs, k)
            for pair in (2, 3):
                received[pair].wait_recv()
                sent[4 + pair].start()
        for k in range(self.nb):
            sent, received = self._copies(ins, outs, ss, rs, k)
            for pair in range(4, 8):
                received[pair].wait_recv()
            for cp in sent:
                cp.wait_send()
        for k in range(self.nb, self.n):
            for j, (px, py) in enumerate(peers):
                self._small(ins, outs, ss, rs, k, j, (px, py), 2 * px + py, c).wait_recv()
                self._small(ins, outs, ss, rs, k, j, (px, py), 2 * x + y, c).wait_send()


def _run_alone(rider, ins, name):
    r_in, r_out_specs, r_out_shape, r_scratch = _rider_specs(rider, ins)

    def body(*refs):
        ride_in, ride_out, scratch = _split(refs, len(r_in), len(r_out_specs), len(r_scratch))
        _ride(rider, ("first", "middle", "last"), pl.program_id(0) == 0, ride_in, ride_out, scratch)

    return pl.pallas_call(
        body, grid=(1,), in_specs=r_in, out_specs=r_out_specs, out_shape=r_out_shape, scratch_shapes=r_scratch,
        compiler_params=_params(("arbitrary",)), name=name,
    )(*ins)


class _Presum:
    in_space = ANY

    def __init__(self, names, base=0):
        self.names = tuple(names)
        self.n = len(self.names)
        self.base = base
        self.n_sems = 3 * self.n
        self.out_shape = [jax.ShapeDtypeStruct((N_CHIP,) + _half_shape(nm), BF16) for nm in self.names]
        self.work_shape = self.out_shape + self.out_shape

    def _stage(self, ins, bufs, ss, k, e, which):
        n = self.n
        return pltpu.make_async_copy(_half(ins[k], e, self.names[k], lead=1), bufs[which * n + k],
                                     ss.at[self.base + which * n + k])

    def _give(self, bufs, ss, rs, k, sibling):
        return _remote(bufs[self.n + k], bufs[k], ss, rs, self.base + k, sibling)

    def first(self, ins, bufs, ss, rs):
        x, y, c, _ = _mesh_place()
        for k in range(self.n):
            self._stage(ins, bufs, ss, k, 1 - c, 1).start()
        for k in range(self.n):
            self._stage(ins, bufs, ss, k, c, 2).start()
        for k in range(self.n):
            self._stage(ins, bufs, ss, k, 1 - c, 1).wait()
            self._give(bufs, ss, rs, k, (x, y, 1 - c)).start()

    def middle(self, ins, bufs, ss, rs):
        pass

    def last(self, ins, bufs, ss, rs):
        x, y, c, _ = _mesh_place()
        for k in range(self.n):
            self._give(bufs, ss, rs, k, (x, y, 1 - c)).wait_recv()
            self._stage(ins, bufs, ss, k, c, 2).wait()

            @pl.loop(0, N_CHIP)
            def _(j):
                bufs[k][j] = (bufs[k][j].astype(F32) + bufs[2 * self.n + k][j].astype(F32)).astype(BF16)
        for k in range(self.n):
            self._give(bufs, ss, rs, k, (x, y, 1 - c)).wait_send()


class _ReduceRelay:
    middle_at = 0.75

    def __init__(self, names, base=0):
        self.names = tuple(names)
        self.n = len(self.names)
        self.base = base
        self.n_sems = 6 * self.n
        self.out_shape = [jax.ShapeDtypeStruct((N_CHIP,) + _half_shape(nm), BF16) for nm in self.names]
        quarter = [jax.ShapeDtypeStruct(self._part_shape(nm), BF16) for nm in self.names]
        self.work_shape = quarter + quarter

    @staticmethod
    def _part_shape(name):
        r, c = _half_shape(name)
        return (r // 2, c) if _BIG_SPLIT[name] == 0 else (r, c // 2)

    def _part(self, ref, name, p):
        r, c = self._part_shape(name)
        return ref.at[pl.ds(p * r, r), pl.ds(0, c)] if _BIG_SPLIT[name] == 0 else ref.at[pl.ds(0, r), pl.ds(p * c, c)]

    def _copies(self, ins, bufs, ss, rs, k):
        x, y, c, _ = _mesh_place()
        name, n = self.names[k], self.n
        me, xn, yn, dg = 2 * x + y, 2 * (1 - x) + y, 2 * x + (1 - y), 2 * (1 - x) + (1 - y)
        to_x, to_y = (1 - x, y, c), (x, 1 - y, c)
        mine = lambda slot, p: self._part(ins[k].at[slot], name, p)
        slot = lambda s, p: self._part(bufs[k].at[s], name, p)
        from_x, from_y = bufs[n + k], bufs[2 * n + k]

        def copy(pair, src, dst, to):
            return _remote(src, dst, ss, rs, self.base + 6 * k + pair, to)

        sent = [copy(0, mine(dg, 0), from_x, to_x), copy(1, mine(dg, 1), from_y, to_y),
                copy(2, mine(xn, 0), slot(me, 0), to_x), copy(3, mine(yn, 1), slot(me, 1), to_y),
                copy(4, from_y, slot(me, 1), to_x), copy(5, from_x, slot(me, 0), to_y)]
        landing = [from_x, from_y, slot(xn, 0), slot(yn, 1), slot(xn, 1), slot(yn, 0)]
        received = [copy(pair, dst, dst, to_x) for pair, dst in enumerate(landing)]
        return sent, received

    def first(self, ins, bufs, ss, rs):
        x, y, c, _ = _mesh_place()
        me, dg = 2 * x + y, 2 * (1 - x) + (1 - y)
        for k in range(self.n):
            sent, _ = self._copies(ins, bufs, ss, rs, k)
            for pair in range(4):
                sent[pair].start()
        for k in range(self.n):
            bufs[k][me] = ins[k][me]
            bufs[k][dg] = jnp.zeros(_half_shape(self.names[k]), BF16)

    def middle(self, ins, bufs, ss, rs):
        x, y, c, _ = _mesh_place()
        xn, yn = 2 * (1 - x) + y, 2 * x + (1 - y)
        for k in range(self.n):
            sent, received = self._copies(ins, bufs, ss, rs, k)
            name, n = self.names[k], self.n
            for pair, buf, own in ((0, bufs[n + k], self._part(ins[k].at[yn], name, 0)),
                                   (1, bufs[2 * n + k], self._part(ins[k].at[xn], name, 1))):
                received[pair].wait_recv()
                buf[...] = (buf[...].astype(F32) + own[...].astype(F32)).astype(BF16)
            sent[5].start()
            sent[4].start()

    def last(self, ins, bufs, ss, rs):
        for k in range(self.n):
            sent, received = self._copies(ins, bufs, ss, rs, k)
            for pair in range(2, 6):
                received[pair].wait_recv()
            for cp in sent:
                cp.wait_send()


class _PresumThenRelay:
    in_space = ANY
    middle_at = _ReduceRelay.middle_at

    def __init__(self, names):
        self.relay = _ReduceRelay(names)
        self.pre = _Presum(names, base=self.relay.n_sems)
        self.n_sems = self.relay.n_sems + self.pre.n_sems
        self.out_shape = self.relay.out_shape
        self.work_shape = list(self.relay.work_shape) + list(self.pre.out_shape) + list(self.pre.work_shape)
        self.n_relay = len(self.relay.out_shape) + len(self.relay.work_shape)

    def first(self, ins, bufs, ss, rs):
        self.pre.first(ins, bufs[self.n_relay:], ss, rs)

    def early(self, ins, bufs, ss, rs):
        self.pre.last(ins, bufs[self.n_relay:], ss, rs)
        self.relay.first(bufs[self.n_relay:], bufs[:self.n_relay], ss, rs)

    def middle(self, ins, bufs, ss, rs):
        self.relay.middle(bufs[self.n_relay:], bufs[:self.n_relay], ss, rs)

    def last(self, ins, bufs, ss, rs):
        self.relay.last(bufs[self.n_relay:], bufs[:self.n_relay], ss, rs)


class _SendPartials:
    def __init__(self, names, small_shape=None):
        self.n = len(names)
        self.small = small_shape is not None
        self.n_sems = 3 * self.n + 7
        self.out_shape = [jax.ShapeDtypeStruct((N_CHIP,) + _half_shape(nm), BF16) for nm in names]
        if self.small:
            self.out_shape.append(jax.ShapeDtypeStruct((N_DEV,) + small_shape, F32))

    def _piece(self, ins, outs, ss, rs, k, j, peer, src_slot, dst_slot, c):
        return _remote(ins[k].at[src_slot], outs[k].at[dst_slot], ss, rs, 3 * k + j, (*peer, c))

    def _small(self, ins, outs, ss, rs, r, other, slot):
        return _remote(ins[self.n], outs[self.n].at[slot], ss, rs, 3 * self.n + r, other)

    @staticmethod
    def _others(x, y, c):
        return [(x, y, 1 - c), (1 - x, y, c), (1 - x, y, 1 - c), (x, 1 - y, c), (x, 1 - y, 1 - c),
                (1 - x, 1 - y, c), (1 - x, 1 - y, 1 - c)]

    def first(self, ins, outs, ss, rs, only=None):
        x, y, c, peers = _mesh_place()
        me = 2 * x + y
        which = range(self.n) if only is None else only
        for k in which:
            for j, (px, py) in enumerate(peers):
                self._piece(ins, outs, ss, rs, k, j, (px, py), 2 * px + py, me, c).start()
        if self.small:
            for r, other in enumerate(self._others(x, y, c)):
                self._small(ins, outs, ss, rs, r, other, 4 * x + 2 * y + c).start()
            outs[self.n][4 * x + 2 * y + c] = ins[self.n][...]
        for k in which:
            outs[k][me] = ins[k][me]

    def middle(self, ins, outs, ss, rs):
        pass

    def last(self, ins, outs, ss, rs):
        x, y, c, peers = _mesh_place()
        me = 2 * x + y
        for k in range(self.n):
            for j, (px, py) in enumerate(peers):
                self._piece(ins, outs, ss, rs, k, j, (px, py), me, 2 * px + py, c).wait_recv()
                self._piece(ins, outs, ss, rs, k, j, (px, py), 2 * px + py, me, c).wait_send()
        if self.small:
            for r, (px, py, pc) in enumerate(self._others(x, y, c)):
                self._small(ins, outs, ss, rs, r, (px, py, pc), 4 * px + 2 * py + pc).wait_recv()
                self._small(ins, outs, ss, rs, r, (px, py, pc), 4 * x + 2 * y + c).wait_send()


class _PresumThenSend:
    def __init__(self, names):
        self.send = _SendPartials(names)
        self.pre = _Presum(names[-1:], base=self.send.n_sems)
        self.n = self.send.n
        self.n_sems = self.send.n_sems + self.pre.n_sems
        self.out_shape = self.send.out_shape
        self.work_shape = list(self.pre.out_shape) + list(self.pre.work_shape)
        self.in_space = [VMEM_WHOLE] * (self.n - 1) + [ANY]

    def _partials(self, ins, bufs):
        return list(ins[:self.n - 1]) + [bufs[self.n]]

    def first(self, ins, bufs, ss, rs):
        self.pre.first(ins[self.n - 1:], bufs[self.n:], ss, rs)
        self.send.first(ins, bufs[:self.n], ss, rs, only=range(self.n - 1))

    def early(self, ins, bufs, ss, rs):
        self.pre.last(ins[self.n - 1:], bufs[self.n:], ss, rs)
        self.send.first(self._partials(ins, bufs), bufs[:self.n], ss, rs, only=(self.n - 1,))

    def middle(self, ins, bufs, ss, rs):
        pass

    def last(self, ins, bufs, ss, rs):
        self.send.last(self._partials(ins, bufs), bufs[:self.n], ss, rs)


def _sum_swap(names, parts, small):
    n = len(parts)
    everyone = _SendPartials((), small.shape)

    def body(*refs):
        p_refs, (small_ref,), o_refs, (osmall_ref,), (all_ref,), (send_sems, recv_sems, ss_small, rs_small) = _split(
            refs, n, 1, n, 1, 1, 4)
        x, y, c = lax.axis_index("x"), lax.axis_index("y"), lax.axis_index("c")
        everyone.first([small_ref], [all_ref], ss_small, rs_small)

        def mine(k):
            part = _half(o_refs[k], c, names[k])
            return _remote(part, part, send_sems, recv_sems, k, (x, y, 1 - c))

        for k in range(n):
            for e in range(2):
                @pl.when(c == e)
                def _():
                    g = p_refs[k][0].astype(F32)
                    for s in range(1, N_CHIP):
                        g = g + p_refs[k][s].astype(F32)
                    r, cols = _half_shape(names[k])
                    if _BIG_SPLIT[names[k]] == 0:
                        o_refs[k][e * r:(e + 1) * r, :] = g
                    else:
                        o_refs[k][:, e * cols:(e + 1) * cols] = g
            mine(k).start()
        for k in range(n):
            theirs = _half(o_refs[k], 1 - c, names[k])
            _remote(theirs, theirs, send_sems, recv_sems, k, (x, y, 1 - c)).wait_recv()
            mine(k).wait_send()
        everyone.last([small_ref], [all_ref], ss_small, rs_small)
        g = all_ref[0]
        for d in range(1, N_DEV):
            g = g + all_ref[d]
        osmall_ref[...] = g

    res = pl.pallas_call(
        body, in_specs=[VMEM_WHOLE] * (n + 1), out_specs=[VMEM_WHOLE] * (n + 1),
        out_shape=[jax.ShapeDtypeStruct(_BIG_SHARD[nm], F32) for nm in names] + [jax.ShapeDtypeStruct(small.shape, F32)],
        scratch_shapes=[pltpu.VMEM((N_DEV,) + small.shape, F32), pltpu.SemaphoreType.DMA((n,)), pltpu.SemaphoreType.DMA((n,)),
                        pltpu.SemaphoreType.DMA((everyone.n_sems,)), pltpu.SemaphoreType.DMA((everyone.n_sems,))],
        compiler_params=_params(), name="sum_swap",
    )(*parts, small)
    return res[:n], res[n]


def _tile(rows, cols, itemsize, budget):
    t = cols if rows % 16 else rows
    other = rows if rows % 16 else cols
    step = 256 if rows % 16 else 32
    while t % step == 0 and t * other * itemsize > budget:
        t //= 2
    return (rows, t) if rows % 16 else (t, cols)


def _adamw_math(w, g, m, v):
    m = ADAM_B1 * m + (1.0 - ADAM_B1) * g
    v = ADAM_B2 * v + (1.0 - ADAM_B2) * (g * g)
    m_hat = m / (1.0 - ADAM_B1 ** ADAM_STEP)
    v_hat = v / (1.0 - ADAM_B2 ** ADAM_STEP)
    delta = -ADAM_LR * (m_hat / (jnp.sqrt(v_hat) + ADAM_EPS) + ADAM_WD * w)
    return delta, m, v


def _adamw_big(g, w, m, v, name):
    r, c = w.shape
    tr, tc = _tile(r, c, 4, 2 * 1024 * 1024)

    def body(g_ref, w_ref, m_ref, v_ref, d_ref, nm_ref, nv_ref):
        d_ref[...], nm_ref[...], nv_ref[...] = _adamw_math(w_ref[...], g_ref[...], m_ref[...], v_ref[...])

    blk = pl.BlockSpec((tr, tc), lambda i, l: (i, l))
    return pl.pallas_call(
        body, grid=(r // tr, c // tc), in_specs=[blk, blk, blk, blk],
        out_specs=[blk, blk, blk], out_shape=[jax.ShapeDtypeStruct((r, c), F32)] * 3,
        compiler_params=_params(("arbitrary", "arbitrary")), name=name,
    )(g, w, m, v)


def _adamw_rows(g, w, m, v, name):
    r, k, lanes = w.shape
    tr = 296

    def body(g_ref, w_ref, m_ref, v_ref, g3_ref, d_ref, nm_ref, nv_ref):
        g = g_ref[...].reshape(tr, k, lanes)
        g3_ref[...] = g
        d_ref[...], nm_ref[...], nv_ref[...] = _adamw_math(w_ref[...], g, m_ref[...], v_ref[...])

    rows = pl.BlockSpec((tr, k, lanes), lambda i: (i, 0, 0))
    return pl.pallas_call(
        body, grid=(pl.cdiv(r, tr),), in_specs=[pl.BlockSpec((tr, k * lanes), lambda i: (i, 0)), rows, rows, rows],
        out_specs=[rows] * 4, out_shape=[jax.ShapeDtypeStruct((r, k, lanes), F32)] * 4,
        compiler_params=_params(("arbitrary",)), name=name,
    )(g, w, m, v)


def _adamw_small(ws, gs, ms, vs):
    n = len(ws)

    def body(*refs):
        w_refs, g_refs, m_refs, v_refs, d_refs, nm_refs, nv_refs = _split(refs, *([n] * 7))
        for k in range(n):
            d_refs[k][...], nm_refs[k][...], nv_refs[k][...] = _adamw_math(w_refs[k][...], g_refs[k][...], m_refs[k][...],
                                                                             v_refs[k][...])

    shapes = [jax.ShapeDtypeStruct(w.shape, F32) for w in ws]
    res = pl.pallas_call(body, out_shape=shapes * 3, name="adamw_small")(*ws, *gs, *ms, *vs)
    return res[:n], res[n:2 * n], res[2 * n:]


def _pack(arrs):
    flat = jnp.concatenate([a.reshape(-1) for a in arrs])
    rows = -(-flat.shape[0] // 1024) * 8
    return jnp.pad(flat, (0, rows * 128 - flat.shape[0])).reshape(rows, 128)


def _unpack(buf, shapes):
    flat = buf.reshape(-1)
    out, off = [], 0
    for s in shapes:
        size = 1
        for d in s:
            size *= d
        out.append(flat[off:off + size].reshape(s))
        off += size
    return out


def _block_rows(w):
    return jnp.pad(w.reshape(512, 4), ((0, 0), (0, 124)))


def _cols(a4):
    return jnp.transpose(a4, (1, 0, 2)).reshape(a4.shape[1], -1)


_LATE = ("w_pa", "w_pb", "w_o", "w_up", "w_down")
_RIDE_IN_PROJ = ("w_pa", "w_pb", "w_o")
_RIDE_MIXER = ("w_up", "w_down_a")
_RIDE_MERGE = ("w_down_b",)


def _full_weights(gathered):
    joined = {"w_in": (D_IN, D_MODEL), "w_o": (D_MODEL, D_MODEL)}
    return {n: (a.reshape(joined[n]) if n in joined else a) for n, a in gathered.items()}


def _local_step(x, target, w, sp, late_shards=None):
    sp = {n: (a.reshape(1, -1) if a.ndim == 1 else a) for n, a in sp.items()}
    wau = jnp.pad(sp["w_a_up"], ((0, 112), (0, 0)))
    wif = jnp.pad(sp["w_if"], ((0, 0), (0, 120)))
    bif = jnp.pad(sp["b_if"], ((0, 0), (0, 120)))
    p = {"wau": wau, "bau": sp["b_a_up"], "ggla": sp["g_gla_norm"], "cw": sp["conv_w"], "cb": sp["conv_b"],
         "wq": _block_rows(sp["w_q_ml"]), "wk": _block_rows(sp["w_k_ml"]), "wv": _block_rows(sp["w_v_ml"]),
         "wif": wif, "bif": bif, "skip": sp["ml_skip"], "gml": sp["g_ml_norm"]}

    if late_shards is None:
        (pm, gab, h), _ = _in_proj(x, sp["g_pre_mix"], w["w_in"])
        ab, *states = _mixer_fwd(pm, p)
        (x1, mix, merged), _ = _merge_fwd(ab, gab, x, w["w_pa"], w["w_pb"], w["w_o"], sp["g_post_mix"])
    else:
        shard = dict(zip(_LATE, late_shards))
        shard["w_down_a"], shard["w_down_b"] = shard["w_down"][0:512], shard["w_down"][512:1024]
        (pm, gab, h), got = _in_proj(x, sp["g_pre_mix"], w["w_in"], _Gather(_RIDE_IN_PROJ, middle_at=0.45),
                                     [shard[n] for n in _RIDE_IN_PROJ])
        w = dict(w, **_full_weights(dict(zip(_RIDE_IN_PROJ, got))))
        ab, *rest = _mixer_fwd(pm, p, _Gather(_RIDE_MIXER, middle_at=0.62), [shard[n] for n in _RIDE_MIXER])
        states = rest[:4]
        w.update(_full_weights(dict(zip(_RIDE_MIXER, rest[4:]))))
        (x1, mix, merged), got = _merge_fwd(ab, gab, x, w["w_pa"], w["w_pb"], w["w_o"], sp["g_post_mix"],
                                            _Gather(_RIDE_MERGE, middle_at=0.46), [shard[n] for n in _RIDE_MERGE])
        w.update(_full_weights(dict(zip(_RIDE_MERGE, got))))
    dx1, u, dd, h2, dpre, dg_post_mlp, dg_pre_mlp, loss = _mlp(x1, target, sp["g_pre_mlp"], sp["g_post_mlp"],
                                                                w["w_up"], w["w_down_a"], w["w_down_b"])
    dgab, dab, dg_post_mix, dw_pa, dw_pb, dw_o = _merge_bwd(dx1, mix, ab, gab, merged, w["w_pa"], w["w_pb"], w["w_o"],
                                                            sp["g_post_mix"])
    big = {"w_pa": dw_pa, "w_pb": dw_pb, "w_o": dw_o, "w_up": _tn_matmul(h2, dpre, "dw_up", shards=N_CHIP)}
    if late_shards is None:
        big["w_down"] = _tn_matmul(u, dd, "dw_down")
        dpm, dp, _ = _mixer_bwd(pm, dab, states, p)
    else:
        pieces = lambda n: big[n].reshape((N_CHIP,) + _BIG_SHARD[n])
        big["w_down"], partial = _tn_matmul(u, dd, "dw_down", rider=_Presum(_LATE[:4]),
                                            rider_ins=[pieces(n) for n in _LATE[:4]])
        dpm, dp, parts = _mixer_bwd(pm, dab, states, p, _PresumThenSend(_LATE), list(partial) + [pieces("w_down")])
        big = dict(zip(_LATE, parts))
    big["w_in"] = _dw_in(dpm, dgab, h)
    if late_shards is None:
        (dx, dg_pre_mix), _ = _in_proj_bwd(dpm, dgab, x, dx1, sp["g_pre_mix"], w["w_in"])
    else:
        (dx, dg_pre_mix), parts = _in_proj_bwd(dpm, dgab, x, dx1, sp["g_pre_mix"], w["w_in"], _PresumThenRelay(("w_in",)),
                                               [big["w_in"].reshape((N_CHIP,) + _BIG_SHARD["w_in"])])
        big["w_in"] = parts[0]
    small = {
        "g_pre_mix": dg_pre_mix, "b_a_up": dp["bau"], "g_gla_norm": dp["ggla"], "conv_b": dp["cb"],
        "w_q_ml": dp["wq"][:, 0:4].reshape(128, 4, 4), "w_k_ml": dp["wk"][:, 0:4].reshape(128, 4, 4),
        "w_v_ml": dp["wv"][:, 0:4].reshape(128, 4, 4),
        "b_if": dp["bif"][:, 0:8], "ml_skip": dp["skip"], "g_ml_norm": dp["gml"], "g_post_mix": dg_post_mix,
        "g_pre_mlp": dg_pre_mlp, "g_post_mlp": dg_post_mlp, "w_a_up": dp["wau"][0:16], "conv_w": dp["cw"],
        "w_if": dp["wif"][:, 0:8], "loss": loss[:, 0:1],
    }
    return dx, big, small


_SMALL_REPL = ("g_pre_mix", "b_a_up", "g_gla_norm", "conv_b", "w_q_ml", "w_k_ml", "w_v_ml", "b_if", "ml_skip",
               "g_ml_norm", "g_post_mix", "g_pre_mlp", "g_post_mlp")
_SMALL_SHARDED = ("w_a_up", "conv_w", "w_if")
_SMALL_ORDER = _SMALL_REPL + _SMALL_SHARDED + ("loss",)
_WEIGHTS = ("g_pre_mix", "w_in", "w_a_up", "b_a_up", "g_gla_norm", "conv_w", "conv_b", "w_q_ml", "w_k_ml", "w_v_ml",
            "w_if", "b_if", "ml_skip", "g_ml_norm", "w_pa", "w_pb", "w_o", "g_post_mix", "g_pre_mlp", "w_up", "w_down",
            "g_post_mlp")


_BLOCK_WEIGHTS = ("w_q_ml", "w_k_ml", "w_v_ml")


def _stored(name, a):
    if name in _BLOCK_WEIGHTS:
        return jnp.transpose(a, (0, 2, 3, 1)).reshape(16, 128)
    if name == "w_if":
        return jnp.transpose(a, (0, 2, 1)).reshape(8, 384)
    return a


def _unstored(name, a):
    if name in _BLOCK_WEIGHTS:
        return jnp.transpose(a.reshape(1, 4, 4, 128), (0, 3, 1, 2))
    if name == "w_if":
        return jnp.transpose(a.reshape(1, 8, 384), (0, 2, 1))
    return a


def _as_shard(name, a):
    return jnp.transpose(a, (2, 0, 1)).reshape(IN_SHARD, D_MODEL // 128, 128) if name == "w_in" else a[0]


def _in_shard_bf16(w_in):
    return jnp.transpose(w_in.astype(BF16), (2, 0, 1)).reshape(IN_SHARD, D_MODEL)


def _from_shard(name, a):
    return jnp.transpose(a, (1, 2, 0)).reshape(1, D_MODEL, IN_SHARD) if name == "w_in" else a[None]


def kernel(x, g_pre_mix, w_in, w_a_up, b_a_up, g_gla_norm, conv_w, conv_b, w_q_ml, w_k_ml, w_v_ml, w_if, b_if, ml_skip, g_ml_norm, w_pa, w_pb, w_o, g_post_mix, g_pre_mlp, w_up, w_down, g_post_mlp, loss_target, m_g_pre_mix, m_w_in, m_w_a_up, m_b_a_up, m_g_gla_norm, m_conv_w, m_conv_b, m_w_q_ml, m_w_k_ml, m_w_v_ml, m_w_if, m_b_if, m_ml_skip, m_g_ml_norm, m_w_pa, m_w_pb, m_w_o, m_g_post_mix, m_g_pre_mlp, m_w_up, m_w_down, m_g_post_mlp, v_g_pre_mix, v_w_in, v_w_a_up, v_b_a_up, v_g_gla_norm, v_conv_w, v_conv_b, v_w_q_ml, v_w_k_ml, v_w_v_ml, v_w_if, v_b_if, v_ml_skip, v_g_ml_norm, v_w_pa, v_w_pb, v_w_o, v_g_post_mix, v_g_pre_mlp, v_w_up, v_w_down, v_g_post_mlp):
    args = dict(locals())
    wts = {n: _as_shard(n, args[n]) for n in _WEIGHTS}
    mom = {n: _as_shard(n, args["m_" + n]) for n in _WEIGHTS}
    var = {n: _as_shard(n, args["v_" + n]) for n in _WEIGHTS}
    chip = 2 * lax.axis_index("x") + lax.axis_index("y")

    first = ("w_in",) + _SMALL_SHARDED
    gathered = dict(zip(first, _run_alone(_Gather(("w_in",), [wts[n] for n in _SMALL_SHARDED]),
                                          [_in_shard_bf16(w_in)] + [wts[n] for n in _SMALL_SHARDED],
                                          "gather_first")))
    sp = {n: wts[n] for n in _SMALL_REPL}
    sp["w_a_up"] = _cols(gathered["w_a_up"])
    sp["conv_w"] = _cols(gathered["conv_w"])
    sp["w_if"] = gathered["w_if"].reshape(1536, 8)

    dx, big, small = _local_step(x[0], loss_target[0], _full_weights({"w_in": gathered["w_in"]}), sp,
                                 late_shards=[wts[n].astype(BF16) for n in _LATE])

    small_shapes = [small[n].shape for n in _SMALL_ORDER]
    packed = _pack([small[n] for n in _SMALL_ORDER])
    sums, small_sum = _sum_swap(_BIG, [big[n] for n in _BIG], packed)

    grads, delta, new_m, new_v = {}, {}, {}, {}
    for n, g in zip(_BIG, sums):
        if n == "w_in":
            g, d, nm, nv = _adamw_rows(g, wts[n], mom[n], var[n], "adamw_" + n)
        else:
            d, nm, nv = _adamw_big(g, wts[n], mom[n], var[n], "adamw_" + n)
        grads[n], delta[n], new_m[n], new_v[n] = (_from_shard(n, a) for a in (g, d, nm, nv))
    summed = dict(zip(_SMALL_ORDER, _unpack(small_sum, small_shapes)))
    loss = summed["loss"].reshape(())
    summed["w_a_up"] = lax.dynamic_slice_in_dim(summed["w_a_up"], chip * 64, 64, axis=1)
    summed["conv_w"] = lax.dynamic_slice_in_dim(summed["conv_w"], chip * 128, 128, axis=1)
    summed["w_if"] = lax.dynamic_slice_in_dim(summed["w_if"], chip * 384, 384, axis=0)
    small_names = _SMALL_REPL + _SMALL_SHARDED
    g_stored = [_stored(n, summed[n].reshape(args[n].shape)) for n in small_names]
    upd = _adamw_small([_stored(n, args[n]) for n in small_names], g_stored,
                       [_stored(n, args["m_" + n]) for n in small_names], [_stored(n, args["v_" + n]) for n in small_names])
    for dst, arrs in zip((grads, delta, new_m, new_v), (g_stored,) + tuple(upd)):
        dst.update({n: _unstored(n, a) for n, a in zip(small_names, arrs)})

    outs = [loss, dx[None]]
    for group in (grads, delta, new_m, new_v):
        outs += [group[n] for n in _WEIGHTS]
    return tuple(outs)
```

```python
import functools

import jax
import jax.numpy as jnp
from jax import lax
from jax.experimental import pallas as pl
from jax.experimental.pallas import tpu as pltpu

F32 = jnp.float32
BF16 = jnp.bfloat16

SEQ = 2048
D_MODEL = 1024
CHUNK = 64
N_CHUNK = SEQ // CHUNK
HEADS = 4
GLA_DK = 64
GLA_DV = 128
ML_DH = 128
D_FF = 4096
EPS = 1e-6
N_CHIP = 4
N_DEV = 8
TOK_TILE = 256
N_TOK_TILE = SEQ // TOK_TILE
SWEEP = 2
assert CHUNK == 64
N_SWEEP = N_CHUNK // SWEEP

PM_W = 2688
PM_XM = 1536
PM_OP = 2048
PM_AL = 2560
GAB_W = 2048
D_IN = 4624
IN_SHARD = D_IN // N_CHIP
IN_ALOW = 1536
IN_XM = 1552
IN_GATES = 2576

ADAM_LR = 0.001
ADAM_B1 = 0.9
ADAM_B2 = 0.999
ADAM_EPS = 1e-08
ADAM_WD = 0.01
ADAM_STEP = 10

VMEM_LIMIT = 56 * 1024 * 1024


def _params(sem=None):
    return pltpu.CompilerParams(dimension_semantics=sem, vmem_limit_bytes=VMEM_LIMIT)


def _dot(a, b, ca, cb):
    return lax.dot_general(a.astype(BF16), b.astype(BF16), (((ca,), (cb,)), ((), ())), preferred_element_type=F32)


def _pmm_nn(a, b):
    return _dot(a, b, 1, 0)


def _pmm_nt(a, b):
    return _dot(a, b, 1, 1)


def _pmm_tn(a, b):
    return _dot(a, b, 0, 0)


def _pcmm(c, x):
    return lax.dot_general(c, x, (((1,), (0,)), ((), ())), precision=lax.Precision.HIGHEST, preferred_element_type=F32)


@jax.custom_vjp
def _mm_nn(a, b):
    return _dot(a, b, 1, 0)


@jax.custom_vjp
def _mm_nt(a, b):
    return _dot(a, b, 1, 1)


@jax.custom_vjp
def _mm_tn(a, b):
    return _dot(a, b, 0, 0)


_mm_nn.defvjp(lambda a, b: (_dot(a, b, 1, 0), (a, b)), lambda r, g: (_mm_nt(g, r[1]), _mm_tn(r[0], g)))
_mm_nt.defvjp(lambda a, b: (_dot(a, b, 1, 1), (a, b)), lambda r, g: (_mm_nn(g, r[1]), _mm_tn(g, r[0])))
_mm_tn.defvjp(lambda a, b: (_dot(a, b, 0, 0), (a, b)), lambda r, g: (_mm_nt(r[1], g), _mm_nn(r[0], g)))


@jax.custom_vjp
def _cmm(c, x):
    return _pcmm(c, x)


_cmm.defvjp(
    lambda c, x: (_pcmm(c, x), c),
    lambda c, g: (jnp.zeros_like(c), lax.dot_general(c, g, (((0,), (0,)), ((), ())), precision=lax.Precision.HIGHEST,
                                                      preferred_element_type=F32)),
)

_PLAIN_OPS = (_pmm_nn, _pmm_nt, _pmm_tn, _pcmm)
_VJP_OPS = (_mm_nn, _mm_nt, _mm_tn, _cmm)


def _sigmoid(x):
    return 0.5 * (jnp.tanh(0.5 * x) + 1.0)


def _log_sigmoid(x):
    return jnp.minimum(x, 0.0) - jnp.log(1.0 + jnp.exp(-jnp.abs(x)))


def _mean(x):
    return jnp.mean(x, axis=-1, keepdims=True)


def _nt(a, b):
    return lax.dot_general(a, b, (((1,), (1,)), ((), ())), preferred_element_type=F32)


def _tn(a, b):
    return lax.dot_general(a, b, (((0,), (0,)), ((), ())), preferred_element_type=F32)


def _mixer_chunk(ops, p, st, pm, xprev8):
    mm_nn, mm_nt, mm_tn, cmm = ops
    n_rows = pm.shape[0]
    n_ch = n_rows // CHUNK
    row = lax.broadcasted_iota(jnp.int32, (n_rows, n_rows), 0)
    col = lax.broadcasted_iota(jnp.int32, (n_rows, n_rows), 1)
    tri = jnp.logical_and((row >> 6) == (col >> 6), row >= col).astype(F32)
    causal = tri[0:CHUNK, 0:CHUNK] > 0.0
    q = pm[:, 0:256]
    k = pm[:, 256:512]
    v = pm[:, 512:1024]
    g = pm[:, 1024:1536]
    xm = pm[:, PM_XM:PM_XM + 512]
    opre = pm[:, PM_OP:PM_OP + 512]
    alow = pm[:, PM_AL:PM_AL + 128]
    hs = range(HEADS)
    cs = range(n_ch)
    pairs = [(i, h) for i in cs for h in hs]
    rs = [slice(i * CHUNK, (i + 1) * CHUNK) for i in cs]
    last = [slice((i + 1) * CHUNK - 1, (i + 1) * CHUNK) for i in cs]
    s6 = [slice(h * GLA_DK, (h + 1) * GLA_DK) for h in hs]
    s12 = [slice(h * 128, (h + 1) * 128) for h in hs]

    xx = jnp.concatenate([xprev8, xm], axis=0)
    pre = p["cb"]
    for j in range(4):
        pre = pre + p["cw"][j:j + 1, :] * xx[5 + j:5 + j + n_rows, :]
    xc = pre * _sigmoid(pre)
    qm = [mm_nn(xc[:, s12[h]], p["wq"][h]) for h in hs]
    km = [mm_nn(xc[:, s12[h]], p["wk"][h]) for h in hs]
    vm = [mm_nn(xm[:, s12[h]], p["wv"][h]) for h in hs]
    qcat = jnp.concatenate(qm, axis=1)
    kcat = jnp.concatenate(km, axis=1)
    vcat = jnp.concatenate(vm, axis=1)
    gates = (mm_nn(qcat, p["wif"][0:512]) + mm_nn(kcat, p["wif"][512:1024]) + mm_nn(vcat, p["wif"][1024:1536])
             + p["bif"])
    lf = _log_sigmoid(gates)
    fc = cmm(tri, lf)
    gates_t = gates.T
    fc_t = fc.T

    la = _log_sigmoid(mm_nn(alow, p["wau"]) + p["bau"]) * (1.0 / 16.0)
    cum = cmm(tri, la)
    cum_last = [cum[last[i], :] for i in cs]
    to_end = jnp.concatenate([cum_last[i] - cum[rs[i], :] for i in cs], axis=0)
    e_pos = jnp.exp(cum)
    e_neg = jnp.exp(-cum)
    qs = q * (GLA_DK ** -0.5)
    qp = qs * e_pos
    qn = qs * e_neg
    kp = k * e_pos
    kn = k * e_neg
    kl = k * jnp.exp(to_end)
    dec = [jnp.exp(cum_last[i]) for i in cs]
    ks = [km[h] * (ML_DH ** -0.5) for h in hs]
    li_c = {(i, h): gates[rs[i], h:h + 1] for i, h in pairs}
    fc_c = {(i, h): fc[rs[i], 4 + h:5 + h] for i, h in pairs}
    f_last = {(i, h): fc[last[i], 4 + h:5 + h] for i, h in pairs}

    a_fwd = {(i, h): mm_nt(qp[rs[i], s6[h]], kn[rs[i], s6[h]]) for i, h in pairs}
    a_bwd = {(i, h): mm_nt(qn[rs[i], s6[h]], kp[rs[i], s6[h]]) for i, h in pairs}
    s_chunk = {(i, h): mm_tn(v[rs[i], s12[h]], kl[rs[i], s6[h]]) for i, h in pairs}
    qk = {(i, h): mm_nt(qm[h][rs[i]], ks[h][rs[i]]) for i, h in pairs}
    a = {ih: f_last[ih] - fc_c[ih] + li_c[ih] for ih in pairs}
    m_loc = {ih: jnp.max(a[ih], axis=0, keepdims=True) for ih in pairs}
    kw = {(i, h): ks[h][rs[i]] * jnp.exp(a[(i, h)] - m_loc[(i, h)]) for i, h in pairs}
    c_chunk = {(i, h): mm_tn(kw[(i, h)], vm[h][rs[i]]) for i, h in pairs}
    mem = {(0, h): st["S"][h] for h in hs}
    c_in = {(0, h): st["C"][h] for h in hs}
    n_in = {(0, h): st["n"][h] for h in hs}
    m_in = {(0, h): st["m"][h][:, 0:1] for h in hs}
    for i, h in pairs:
        mem[(i + 1, h)] = mem[(i, h)] * dec[i][:, s6[h]] + s_chunk[(i, h)]
        m_nx = jnp.maximum(f_last[(i, h)] + m_in[(i, h)], m_loc[(i, h)])
        sp = jnp.exp(f_last[(i, h)] + m_in[(i, h)] - m_nx)
        sl = jnp.exp(m_loc[(i, h)] - m_nx)
        c_in[(i + 1, h)] = sp * c_in[(i, h)] + sl * c_chunk[(i, h)]
        n_in[(i + 1, h)] = sp * n_in[(i, h)] + sl * jnp.sum(kw[(i, h)], axis=0, keepdims=True)
        m_in[(i + 1, h)] = m_nx
    s_new = [mem[(n_ch, h)] for h in hs]
    o_inter = {(i, h): mm_nt(qp[rs[i], s6[h]], mem[(i, h)]) for i, h in pairs}
    q_c = {(i, h): mm_nn(qm[h][rs[i]], c_in[(i, h)]) for i, h in pairs}
    scores = {ih: jnp.where(causal, a_fwd[ih], a_bwd[ih]) for ih in pairs}
    log_d = {(i, h): gates_t[h:h + 1, rs[i]] - jnp.abs(fc_c[(i, h)] - fc_t[4 + h:5 + h, rs[i]]) for i, h in pairs}
    g_int = {ih: fc_c[ih] + m_in[ih] for ih in pairs}
    m_t = {ih: jnp.maximum(g_int[ih], jnp.max(log_d[ih], axis=1, keepdims=True)) for ih in pairs}
    s = {ih: qk[ih] * jnp.exp(log_d[ih] - m_t[ih]) for ih in pairs}
    scl = {ih: jnp.exp(g_int[ih] - m_t[ih]) for ih in pairs}
    o = {(i, h): mm_nn(scores[(i, h)], v[rs[i], s12[h]]) + o_inter[(i, h)] for i, h in pairs}
    num = {(i, h): mm_nn(s[(i, h)], vm[h][rs[i]]) + scl[(i, h)] * q_c[(i, h)] for i, h in pairs}
    o = {ih: o[ih] * lax.rsqrt(_mean(o[ih] * o[ih]) + EPS) * p["ggla"] for ih in pairs}
    gate = g * _sigmoid(g)
    out_a = {(i, h): o[(i, h)] * gate[rs[i], s12[h]] for i, h in pairs}
    den = {(i, h): jnp.sum(s[(i, h)], axis=1, keepdims=True)
           + scl[(i, h)] * jnp.sum(qm[h][rs[i]] * n_in[(i, h)], axis=1, keepdims=True) for i, h in pairs}
    den = {ih: jnp.maximum(jnp.abs(den[ih]), jnp.exp(-m_t[ih])) for ih in pairs}
    open_gate = _sigmoid(opre)
    hc = {(i, h): num[(i, h)] / den[(i, h)] * open_gate[rs[i], s12[h]] for i, h in pairs}
    d0 = {ih: hc[ih] - _mean(hc[ih]) for ih in pairs}
    y = {ih: d0[ih] * lax.rsqrt(_mean(d0[ih] * d0[ih]) + EPS) for ih in pairs}
    skipped = p["skip"] * xc
    out_b = {(i, h): y[(i, h)] * p["gml"][:, s12[h]] + skipped[rs[i], s12[h]] for i, h in pairs}
    ab = jnp.concatenate([jnp.concatenate([out_a[(i, h)] for h in hs] + [out_b[(i, h)] for h in hs], axis=1) for i in cs],
                         axis=0)
    new = {"S": s_new, "C": [c_in[(n_ch, h)] for h in hs], "n": [n_in[(n_ch, h)] for h in hs],
           "m": [jnp.broadcast_to(m_in[(n_ch, h)], (1, ML_DH)) for h in hs]}
    return ab, new


_P_NAMES = ("wau", "bau", "ggla", "cw", "cb", "wq", "wk", "wv", "wif", "bif", "skip", "gml")
_P_SHAPES = {
    "wau": (128, 256), "bau": (1, 256), "ggla": (1, 128), "cw": (4, 512), "cb": (1, 512),
    "wq": (512, 128), "wk": (512, 128), "wv": (512, 128),
    "wif": (1536, 128), "bif": (1, 128), "skip": (1, 512), "gml": (1, 512),
}
_P_BLOCKDIAG = ("wq", "wk", "wv")
_S_NAMES = ("S", "C", "n", "m")
_S_SHAPES = {"S": (HEADS, GLA_DV, GLA_DK), "C": (HEADS, ML_DH, ML_DH), "n": (HEADS, 1, ML_DH), "m": (HEADS, 1, ML_DH)}


def _per_head(ref):
    return [ref[h] for h in range(HEADS)]


def _block_mask():
    r = lax.broadcasted_iota(jnp.int32, (128, 128), 0)
    c = lax.broadcasted_iota(jnp.int32, (128, 128), 1)
    same_block = (r >> 2) == (c >> 2)
    spread = jnp.logical_and(r < 4, (c & 3) == r)
    return same_block.astype(F32), spread.astype(F32)


def _expand_blockdiag(w_ref, dense_ref):
    same_block, spread = _block_mask()
    for h in range(HEADS):
        tiled = _pmm_nn(w_ref[h * 128:(h + 1) * 128, :], spread)
        dense_ref[h] = tiled * same_block


def _collect_blockdiag(ddense_ref, dw_ref):
    same_block, spread = _block_mask()
    for h in range(HEADS):
        dw_ref[h * 128:(h + 1) * 128, :] = lax.dot_general(
            ddense_ref[h] * same_block, spread, (((1,), (1,)), ((), ())), precision=lax.Precision.HIGHEST,
            preferred_element_type=F32)


def _const_spec(shape):
    zeros = (0,) * len(shape)
    return pl.BlockSpec(shape, lambda i: zeros)


def _split(refs, *counts):
    out, at = [], 0
    for c in counts:
        out.append(refs[at:at + c])
        at += c
    assert at == len(refs)
    return out


def _ride(rider, phases, cond, ins, outs, sems):
    if rider is None or not any(hasattr(rider, phase) for phase in phases):
        return
    lands, (send_sems, recv_sems, flush_sems) = sems[:-3], sems[-3:]

    @pl.when(cond)
    def _():
        for phase in phases:
            if phase == "last" and hasattr(rider, "late"):
                rider.late(ins, lands, send_sems, recv_sems)
                rider.flush("late", lands, outs, flush_sems)
            getattr(rider, phase)(ins, lands, send_sems, recv_sems)
            if hasattr(rider, "flush"):
                rider.flush(phase, lands, outs, flush_sems)
        if "last" in phases and not hasattr(rider, "flush"):
            flush = [pltpu.make_async_copy(lands[k], outs[k], flush_sems.at[k]) for k in range(len(outs))]
            for cp in flush:
                cp.start()
            for cp in flush:
                cp.wait()


def _middle_step(rider, n_steps):
    return min(n_steps - 2, int(getattr(rider, "middle_at", 1.0) * n_steps))


def _rider_specs(rider, rider_ins):
    if rider is None:
        return [], [], [], []
    scratch = [pltpu.VMEM(s.shape, s.dtype) for s in list(rider.out_shape) + list(getattr(rider, "work_shape", ()))]
    scratch += [pltpu.SemaphoreType.DMA((rider.n_sems,)), pltpu.SemaphoreType.DMA((rider.n_sems,)),
                pltpu.SemaphoreType.DMA((getattr(rider, "n_flush", len(rider.out_shape)),))]
    in_space = getattr(rider, "in_space", VMEM_WHOLE)
    in_specs = list(in_space) if isinstance(in_space, (list, tuple)) else [in_space] * len(rider_ins)
    return in_specs, [ANY] * len(rider.out_shape), list(rider.out_shape), scratch


def _mixer_fwd(pm, p, rider=None, rider_ins=()):
    n_p = len(_P_NAMES)
    r_in, r_out_specs, r_out_shape, r_sems = _rider_specs(rider, rider_ins)

    def body(*refs):
        (pm_ref, xprev_ref), p_list, ride_in, (ab_ref,), so_refs, ride_out, sc_refs, dense_list, sems = _split(
            refs, 2, n_p, len(r_in), 1, 4, len(r_out_specs), 4, 3, len(r_sems))
        p_refs = dict(zip(_P_NAMES, p_list))
        dense = dict(zip(_P_BLOCKDIAG, dense_list))
        n = pl.program_id(0)
        _ride(rider, ("first",), n == 0, ride_in, ride_out, sems)

        @pl.when(n == 0)
        def _():
            for r in sc_refs:
                r[...] = jnp.zeros_like(r)
            for nm in _P_BLOCKDIAG:
                _expand_blockdiag(p_refs[nm], dense[nm])

        st = {name: _per_head(r) for name, r in zip(_S_NAMES, sc_refs)}
        pv = {nm: (_per_head(dense[nm]) if nm in _P_BLOCKDIAG else p_refs[nm][...]) for nm in _P_NAMES}
        for name, r in zip(_S_NAMES, so_refs):
            for h in range(HEADS):
                r[0, h] = st[name][h]
        xprev8 = jnp.where(n > 0, xprev_ref[CHUNK - 8:CHUNK, :], 0.0)
        ab, st = _mixer_chunk(_PLAIN_OPS, pv, st, pm_ref[...], xprev8)
        ab_ref[...] = ab.astype(BF16)
        for name, r in zip(_S_NAMES, sc_refs):
            for h in range(HEADS):
                r[h] = st[name][h]
        _ride(rider, ("middle",), n == _middle_step(rider, N_SWEEP), ride_in, ride_out, sems)
        _ride(rider, ("last",), n == N_SWEEP - 1, ride_in, ride_out, sems)

    in_specs = [pl.BlockSpec((SWEEP * CHUNK, PM_W), lambda i: (i, 0)),
                pl.BlockSpec((CHUNK, 512), lambda i: (jnp.maximum(SWEEP * i - 1, 0), PM_XM // 512))]
    in_specs += [_const_spec(_P_SHAPES[nm]) for nm in _P_NAMES] + r_in
    out_specs = [pl.BlockSpec((SWEEP * CHUNK, 1024), lambda i: (i, 0))]
    out_shape = [jax.ShapeDtypeStruct((SEQ, 1024), BF16)]
    for nm in _S_NAMES:
        shp = _S_SHAPES[nm]
        out_specs.append(pl.BlockSpec((1,) + shp, lambda i: (i, 0, 0, 0)))
        out_shape.append(jax.ShapeDtypeStruct((N_SWEEP,) + shp, F32))
    return pl.pallas_call(
        body, grid=(N_SWEEP,), in_specs=in_specs, out_specs=out_specs + r_out_specs, out_shape=out_shape + r_out_shape,
        scratch_shapes=[pltpu.VMEM(_S_SHAPES[nm], F32) for nm in _S_NAMES]
        + [pltpu.VMEM((HEADS, 128, 128), F32) for _ in _P_BLOCKDIAG] + r_sems,
        compiler_params=_params(("arbitrary",)), name="mixer_fwd",
    )(pm, pm, *[p[nm] for nm in _P_NAMES], *rider_ins)


def _mixer_bwd(pm, dab, states, p, rider=None, rider_ins=()):
    n_p = len(_P_NAMES)
    r_in, r_out_specs, r_out_shape, r_sems = _rider_specs(rider, rider_ins)

    def body(*refs):
        ((pm_ref, xprev_ref, dab_ref), si_refs, p_list, ride_in, (dpm_ref,), dp_list, ride_out, ds_refs, (carry_ref,),
         dense_list, ddense_list, sems) = _split(refs, 3, 4, n_p, len(r_in), 1, n_p, len(r_out_specs), 4, 1, 3, 3, len(r_sems))
        p_refs = dict(zip(_P_NAMES, p_list))
        dp_refs = dict(zip(_P_NAMES, dp_list))
        dense = dict(zip(_P_BLOCKDIAG, dense_list))
        ddense = dict(zip(_P_BLOCKDIAG, ddense_list))
        i = pl.program_id(0)
        blk = N_SWEEP - 1 - i
        _ride(rider, ("first",), i == 0, ride_in, ride_out, sems)

        @pl.when(i == 0)
        def _():
            for r in ds_refs:
                r[...] = jnp.zeros_like(r)
            for nm in _P_NAMES:
                if nm in _P_BLOCKDIAG:
                    ddense[nm][...] = jnp.zeros_like(ddense[nm])
                    _expand_blockdiag(p_refs[nm], dense[nm])
                else:
                    dp_refs[nm][...] = jnp.zeros_like(dp_refs[nm])
            carry_ref[...] = jnp.zeros_like(carry_ref)

        pv = {nm: (_per_head(dense[nm]) if nm in _P_BLOCKDIAG else p_refs[nm][...]) for nm in _P_NAMES}
        dst = {name: _per_head(r) for name, r in zip(_S_NAMES, ds_refs)}
        st = {name: [r[0, h] for h in range(HEADS)] for name, r in zip(_S_NAMES, si_refs)}
        xprev8 = jnp.where(blk > 0, xprev_ref[CHUNK - 8:CHUNK, :], 0.0)
        _, vjp = jax.vjp(functools.partial(_mixer_chunk, _VJP_OPS), pv, st, pm_ref[...], xprev8)
        dp_sum, dst, dpm, dxprev8 = vjp((dab_ref[...], dst))
        reach = jnp.concatenate([jnp.zeros((SWEEP * CHUNK - 8, 512), F32), carry_ref[...]], axis=0)
        dpm_ref[:, 0:PM_XM] = dpm[:, 0:PM_XM].astype(BF16)
        dpm_ref[:, PM_XM:PM_XM + 512] = (dpm[:, PM_XM:PM_XM + 512] + reach).astype(BF16)
        dpm_ref[:, PM_XM + 512:PM_W] = dpm[:, PM_XM + 512:PM_W].astype(BF16)
        carry_ref[...] = dxprev8
        for name, r in zip(_S_NAMES, ds_refs):
            for h in range(HEADS):
                r[h] = dst[name][h]
        for nm in _P_NAMES:
            if nm in _P_BLOCKDIAG:
                for h in range(HEADS):
                    ddense[nm][h] += dp_sum[nm][h]
            else:
                dp_refs[nm][...] += dp_sum[nm]

        @pl.when(i == N_SWEEP - 1)
        def _():
            for nm in _P_BLOCKDIAG:
                _collect_blockdiag(ddense[nm], dp_refs[nm])

        _ride(rider, ("early",), i == 1, ride_in, ride_out, sems)
        _ride(rider, ("middle",), i == _middle_step(rider, N_SWEEP), ride_in, ride_out, sems)
        _ride(rider, ("last",), i == N_SWEEP - 1, ride_in, ride_out, sems)

    rev = lambda i: (N_SWEEP - 1 - i, 0)
    in_specs = [pl.BlockSpec((SWEEP * CHUNK, PM_W), rev),
                pl.BlockSpec((CHUNK, 512), lambda i: (jnp.maximum(SWEEP * (N_SWEEP - 1 - i) - 1, 0), PM_XM // 512)),
                pl.BlockSpec((SWEEP * CHUNK, 1024), rev)]
    for nm in _S_NAMES:
        in_specs.append(pl.BlockSpec((1,) + _S_SHAPES[nm], lambda i: (N_SWEEP - 1 - i, 0, 0, 0)))
    in_specs += [_const_spec(_P_SHAPES[nm]) for nm in _P_NAMES] + r_in
    out_specs = [pl.BlockSpec((SWEEP * CHUNK, PM_W), rev)] + [_const_spec(_P_SHAPES[nm]) for nm in _P_NAMES]
    out_shape = [jax.ShapeDtypeStruct((SEQ, PM_W), BF16)] + [jax.ShapeDtypeStruct(_P_SHAPES[nm], F32) for nm in _P_NAMES]
    res = pl.pallas_call(
        body, grid=(N_SWEEP,), in_specs=in_specs, out_specs=out_specs + r_out_specs, out_shape=out_shape + r_out_shape,
        scratch_shapes=[pltpu.VMEM(_S_SHAPES[nm], F32) for nm in _S_NAMES] + [pltpu.VMEM((8, 512), F32)]
        + [pltpu.VMEM((HEADS, 128, 128), F32) for _ in range(2 * len(_P_BLOCKDIAG))] + r_sems,
        compiler_params=_params(("arbitrary",)), name="mixer_bwd",
    )(pm, pm, dab, *states, *[p[nm] for nm in _P_NAMES], *rider_ins)
    return res[0], dict(zip(_P_NAMES, res[1:1 + n_p])), res[1 + n_p:]


def _tok(width):
    return pl.BlockSpec((TOK_TILE, width), lambda i: (i, 0))


def _once(shape):
    zeros = (0,) * len(shape)
    return pl.BlockSpec(shape, lambda i: zeros, pipeline_mode=pl.Buffered(1))


def _rms_fwd(x):
    r = lax.rsqrt(_mean(x * x) + EPS)
    return x * r, r


def _rms_bwd(dy, xn, r, g):
    gd = dy * g
    return r * (gd - xn * _mean(xn * gd))


def _tiled_call(body, in_specs, out_specs, out_shape, args, name, rider=None, rider_ins=()):
    r_in, r_out_specs, r_out_shape, r_scratch = _rider_specs(rider, rider_ins)
    n_in, n_out = len(in_specs), len(out_specs)

    def hosted(*refs):
        ins, ride_in, outs, ride_out, scratch = _split(refs, n_in, len(r_in), n_out, len(r_out_specs), len(r_scratch))
        i = pl.program_id(0)
        _ride(rider, ("first",), i == 0, ride_in, ride_out, scratch)
        body(*ins, *outs)
        _ride(rider, ("early",), i == 1, ride_in, ride_out, scratch)
        _ride(rider, ("middle",), i == _middle_step(rider, N_TOK_TILE), ride_in, ride_out, scratch)
        _ride(rider, ("last",), i == N_TOK_TILE - 1, ride_in, ride_out, scratch)

    res = pl.pallas_call(
        hosted, grid=(N_TOK_TILE,), in_specs=list(in_specs) + r_in, out_specs=list(out_specs) + r_out_specs,
        out_shape=list(out_shape) + r_out_shape, scratch_shapes=r_scratch,
        compiler_params=_params(("arbitrary",)), name=name,
    )(*args, *rider_ins)
    return res[:n_out], res[n_out:]


def _in_proj(x, g_pre, wt_in, rider=None, rider_ins=()):
    def body(x_ref, g_ref, wt_ref, pm_ref, gab_ref, h_ref):
        xn, _ = _rms_fwd(x_ref[...])
        h = (xn * g_ref[...]).astype(BF16)
        h_ref[...] = h
        pm_ref[:, 0:PM_XM] = _nt(h, wt_ref[0:IN_ALOW, :])
        pm_ref[:, PM_XM:PM_AL] = _nt(h, wt_ref[IN_XM:IN_GATES, :])
        pm_ref[:, PM_AL:PM_W] = _nt(h, wt_ref[IN_ALOW:IN_ALOW + 128, :])
        gab_ref[...] = _nt(h, wt_ref[IN_GATES:D_IN, :])

    return _tiled_call(
        body, [_tok(D_MODEL), _once((1, D_MODEL)), _once((D_IN, D_MODEL))], [_tok(PM_W), _tok(GAB_W), _tok(D_MODEL)],
        [jax.ShapeDtypeStruct((SEQ, PM_W), F32), jax.ShapeDtypeStruct((SEQ, GAB_W), F32),
         jax.ShapeDtypeStruct((SEQ, D_MODEL), BF16)], (x, g_pre, wt_in), "in_proj", rider, rider_ins)


def _merge_fwd(ab, gab, x, w_pa4, w_pb4, w_o, g_post, rider=None, rider_ins=()):
    def body(ab_ref, gab_ref, x_ref, wpa_ref, wpb_ref, wo_ref, g_ref, x1_ref, mix_ref, mg_ref):
        a = ab_ref[:, 0:512]
        b = ab_ref[:, 512:1024]
        for j in range(N_CHIP):
            blk = slice(j * 256, (j + 1) * 256)
            ya = jnp.dot(a, wpa_ref[j], preferred_element_type=F32)
            yb = jnp.dot(b, wpb_ref[j], preferred_element_type=F32)
            sa = _sigmoid(gab_ref[:, j * 256:(j + 1) * 256])
            sb = _sigmoid(gab_ref[:, 1024 + j * 256:1024 + (j + 1) * 256])
            mg_ref[:, blk] = (sa * ya + sb * yb).astype(BF16)
        mix = jnp.dot(mg_ref[...], wo_ref[...], preferred_element_type=F32)
        mix_ref[...] = mix
        mn, _ = _rms_fwd(mix)
        x1_ref[...] = x_ref[...] + mn * g_ref[...]

    return _tiled_call(
        body, [_tok(1024), _tok(GAB_W), _tok(D_MODEL), _once((N_CHIP, 512, 256)), _once((N_CHIP, 512, 256)),
               _once((D_MODEL, D_MODEL)), _once((1, D_MODEL))], [_tok(D_MODEL), _tok(D_MODEL), _tok(D_MODEL)],
        [jax.ShapeDtypeStruct((SEQ, D_MODEL), F32), jax.ShapeDtypeStruct((SEQ, D_MODEL), F32),
         jax.ShapeDtypeStruct((SEQ, D_MODEL), BF16)], (ab, gab, x, w_pa4, w_pb4, w_o, g_post), "merge_fwd", rider, rider_ins)


def _mlp(x1, target, g_pre, g_post, w_up4, w_down_a4, w_down_b4):
    def body(x1_ref, t_ref, gpre_ref, gpost_ref, wup_ref, wda_ref, wdb_ref,
             dx1_ref, u_ref, dd_ref, h2_ref, dpre_ref, dgpost_ref, dgpre_ref, loss_ref):
        @pl.when(pl.program_id(0) == 0)
        def _():
            dgpost_ref[...] = jnp.zeros_like(dgpost_ref)
            dgpre_ref[...] = jnp.zeros_like(dgpre_ref)
            loss_ref[...] = jnp.zeros_like(loss_ref)

        x1 = x1_ref[...]
        gpre = gpre_ref[...]
        gpost = gpost_ref[...]
        xn2, r2 = _rms_fwd(x1)
        h2 = (xn2 * gpre).astype(BF16)
        h2_ref[...] = h2
        rl = []
        d = jnp.zeros((TOK_TILE, D_MODEL), F32)
        for j in range(N_CHIP):
            blk = slice(j * 1024, (j + 1) * 1024)
            r = jnp.maximum(jnp.dot(h2, wup_ref[j], preferred_element_type=F32), 0.0)
            rl.append(r)
            u = (r * r).astype(BF16)
            u_ref[:, blk] = u
            d = d + jnp.dot(u[:, 0:512], wda_ref[j], preferred_element_type=F32)
            d = d + jnp.dot(u[:, 512:1024], wdb_ref[j], preferred_element_type=F32)
        dn, r3 = _rms_fwd(d)
        diff = x1 + dn * gpost - t_ref[...]
        loss_ref[...] += jnp.sum(diff * diff, keepdims=True) * (0.5 / D_MODEL)
        dy = diff * (1.0 / D_MODEL)
        dgpost_ref[...] += jnp.sum(dy * dn, axis=0, keepdims=True)
        dd = _rms_bwd(dy, dn, r3, gpost).astype(BF16)
        dd_ref[...] = dd
        dh2 = jnp.zeros((TOK_TILE, D_MODEL), F32)
        for j in range(N_CHIP):
            blk = slice(j * 1024, (j + 1) * 1024)
            du = jnp.concatenate([_nt(dd, wda_ref[j]), _nt(dd, wdb_ref[j])], axis=1)
            dpre = (du * (2.0 * rl[j])).astype(BF16)
            dpre_ref[:, blk] = dpre
            dh2 = dh2 + _nt(dpre, wup_ref[j])
        dgpre_ref[...] += jnp.sum(dh2 * xn2, axis=0, keepdims=True)
        dx1_ref[...] = dy + _rms_bwd(dh2, xn2, r2, gpre)

    acc = pl.BlockSpec((1, D_MODEL), lambda i: (0, 0))
    return pl.pallas_call(
        body, grid=(N_TOK_TILE,),
        in_specs=[_tok(D_MODEL), _tok(D_MODEL), _once((1, D_MODEL)), _once((1, D_MODEL)),
                  _once((N_CHIP, D_MODEL, 1024)), _once((N_CHIP, 512, D_MODEL)), _once((N_CHIP, 512, D_MODEL))],
        out_specs=[_tok(D_MODEL), _tok(D_FF), _tok(D_MODEL), _tok(D_MODEL), _tok(D_FF), acc, acc,
                   pl.BlockSpec((1, 128), lambda i: (0, 0))],
        out_shape=[jax.ShapeDtypeStruct((SEQ, D_MODEL), F32), jax.ShapeDtypeStruct((SEQ, D_FF), BF16),
                   jax.ShapeDtypeStruct((SEQ, D_MODEL), BF16), jax.ShapeDtypeStruct((SEQ, D_MODEL), BF16),
                   jax.ShapeDtypeStruct((SEQ, D_FF), BF16), jax.ShapeDtypeStruct((1, D_MODEL), F32),
                   jax.ShapeDtypeStruct((1, D_MODEL), F32), jax.ShapeDtypeStruct((1, 128), F32)],
        compiler_params=_params(("arbitrary",)), name="mlp_fwd_bwd",
    )(x1, target, g_pre, g_post, w_up4, w_down_a4, w_down_b4)


def _merge_bwd(dx1, mix, ab, gab, merged, w_pa4, w_pb4, w_o, g_post):
    def body(dx1_ref, mix_ref, ab_ref, gab_ref, mg_ref, wpa_ref, wpb_ref, wo_ref, g_ref,
             dgab_ref, dab_ref, dg_ref, dwpa_ref, dwpb_ref, dwo_ref, acc_pa, acc_pb, acc_o):
        @pl.when(pl.program_id(0) == 0)
        def _():
            dg_ref[...] = jnp.zeros_like(dg_ref)
            acc_pa[...] = jnp.zeros_like(acc_pa)
            acc_pb[...] = jnp.zeros_like(acc_pb)
            acc_o[...] = jnp.zeros_like(acc_o)

        dx1 = dx1_ref[...]
        mn, r = _rms_fwd(mix_ref[...])
        dg_ref[...] += jnp.sum(dx1 * mn, axis=0, keepdims=True)
        dmix = _rms_bwd(dx1, mn, r, g_ref[...]).astype(BF16)
        acc_o[...] += _tn(mg_ref[...], dmix)
        dmerged = _nt(dmix, wo_ref[...])
        a = ab_ref[:, 0:512]
        b = ab_ref[:, 512:1024]
        da = jnp.zeros((TOK_TILE, 512), F32)
        db = jnp.zeros((TOK_TILE, 512), F32)
        dyas, dybs = [], []
        for j in range(N_CHIP):
            blk = slice(j * 256, (j + 1) * 256)
            blk_b = slice(1024 + j * 256, 1024 + (j + 1) * 256)
            dm = dmerged[:, blk]
            ya = jnp.dot(a, wpa_ref[j], preferred_element_type=F32)
            yb = jnp.dot(b, wpb_ref[j], preferred_element_type=F32)
            sa = _sigmoid(gab_ref[:, blk])
            sb = _sigmoid(gab_ref[:, blk_b])
            dya = (dm * sa).astype(BF16)
            dyb = (dm * sb).astype(BF16)
            dyas.append(dya)
            dybs.append(dyb)
            dgab_ref[:, blk] = (dm * ya * sa * (1.0 - sa)).astype(BF16)
            dgab_ref[:, blk_b] = (dm * yb * sb * (1.0 - sb)).astype(BF16)
            da = da + _nt(dya, wpa_ref[j])
            db = db + _nt(dyb, wpb_ref[j])
        dab_ref[:, 0:512] = da
        dab_ref[:, 512:1024] = db
        acc_pa[...] += _tn(a, jnp.concatenate(dyas, axis=1))
        acc_pb[...] += _tn(b, jnp.concatenate(dybs, axis=1))

        @pl.when(pl.program_id(0) == N_TOK_TILE - 1)
        def _():
            dwo_ref[...] = acc_o[...].astype(BF16)
            for j in range(N_CHIP):
                dwpa_ref[j] = acc_pa[:, j * 256:(j + 1) * 256].astype(BF16)
                dwpb_ref[j] = acc_pb[:, j * 256:(j + 1) * 256].astype(BF16)

    whole = lambda shape: pl.BlockSpec(shape, lambda i: (0,) * len(shape))
    return pl.pallas_call(
        body, grid=(N_TOK_TILE,),
        in_specs=[_tok(D_MODEL), _tok(D_MODEL), _tok(1024), _tok(GAB_W), _tok(D_MODEL), _once((N_CHIP, 512, 256)),
                  _once((N_CHIP, 512, 256)), _once((D_MODEL, D_MODEL)), _once((1, D_MODEL))],
        out_specs=[_tok(GAB_W), _tok(1024), whole((1, D_MODEL)), whole((N_CHIP, 512, 256)), whole((N_CHIP, 512, 256)),
                   whole((D_MODEL, D_MODEL))],
        out_shape=[jax.ShapeDtypeStruct((SEQ, GAB_W), BF16), jax.ShapeDtypeStruct((SEQ, 1024), F32),
                   jax.ShapeDtypeStruct((1, D_MODEL), F32), jax.ShapeDtypeStruct((N_CHIP, 512, 256), BF16),
                   jax.ShapeDtypeStruct((N_CHIP, 512, 256), BF16), jax.ShapeDtypeStruct((D_MODEL, D_MODEL), BF16)],
        scratch_shapes=[pltpu.VMEM((512, D_MODEL), F32), pltpu.VMEM((512, D_MODEL), F32),
                        pltpu.VMEM((D_MODEL, D_MODEL), F32)],
        compiler_params=_params(("arbitrary",)), name="merge_bwd",
    )(dx1, mix, ab, gab, merged, w_pa4, w_pb4, w_o, g_post)


def _in_proj_bwd(dpm, dgab, x, dx1, g_pre, wt_in, rider=None, rider_ins=()):
    def body(dpm_ref, dgab_ref, x_ref, dx1_ref, g_ref, wt_ref, dx_ref, dg_ref):
        @pl.when(pl.program_id(0) == 0)
        def _():
            dg_ref[...] = jnp.zeros_like(dg_ref)

        dh = jnp.dot(dpm_ref[:, 0:PM_XM], wt_ref[0:IN_ALOW, :], preferred_element_type=F32)
        dh = dh + jnp.dot(dpm_ref[:, PM_XM:PM_AL], wt_ref[IN_XM:IN_GATES, :], preferred_element_type=F32)
        dh = dh + jnp.dot(dpm_ref[:, PM_AL:PM_W], wt_ref[IN_ALOW:IN_ALOW + 128, :], preferred_element_type=F32)
        dh = dh + jnp.dot(dgab_ref[...], wt_ref[IN_GATES:D_IN, :], preferred_element_type=F32)
        xn, r = _rms_fwd(x_ref[...])
        dg_ref[...] += jnp.sum(dh * xn, axis=0, keepdims=True)
        dx_ref[...] = dx1_ref[...] + _rms_bwd(dh, xn, r, g_ref[...])

    return _tiled_call(
        body, [_tok(PM_W), _tok(GAB_W), _tok(D_MODEL), _tok(D_MODEL), _once((1, D_MODEL)), _once((D_IN, D_MODEL))],
        [_tok(D_MODEL), pl.BlockSpec((1, D_MODEL), lambda i: (0, 0))],
        [jax.ShapeDtypeStruct((SEQ, D_MODEL), F32), jax.ShapeDtypeStruct((1, D_MODEL), F32)],
        (dpm, dgab, x, dx1, g_pre, wt_in), "in_proj_bwd", rider, rider_ins)


def _dw_in(dpm, dgab, h):
    n_pm = PM_AL // 512
    n_blk = n_pm + GAB_W // 512

    def body(dpm_ref, dgab_ref, dal_ref, h_ref, o_ref):
        i = pl.program_id(0)
        off = pl.multiple_of(i * 512 + 16 * (i >= 3).astype(jnp.int32), 16)

        @pl.when(i < n_pm)
        def _():
            o_ref[pl.ds(off, 512), :] = _tn(dpm_ref[...], h_ref[...]).astype(BF16)

        @pl.when(i >= n_pm)
        def _():
            o_ref[pl.ds(off, 512), :] = _tn(dgab_ref[...], h_ref[...]).astype(BF16)

        @pl.when(i == 0)
        def _():
            o_ref[IN_ALOW:IN_XM, :] = _tn(dal_ref[...], h_ref[...])[0:IN_XM - IN_ALOW].astype(BF16)

    return pl.pallas_call(
        body, grid=(n_blk,),
        in_specs=[pl.BlockSpec((SEQ, 512), lambda i: (0, jnp.minimum(i, n_pm - 1))),
                  pl.BlockSpec((SEQ, 512), lambda i: (0, jnp.maximum(i - n_pm, 0))),
                  pl.BlockSpec((SEQ, 128), lambda i: (0, PM_AL // 128)),
                  _once((SEQ, D_MODEL))],
        out_specs=pl.BlockSpec((D_IN, D_MODEL), lambda i: (0, 0)),
        out_shape=jax.ShapeDtypeStruct((D_IN, D_MODEL), BF16),
        compiler_params=_params(("arbitrary",)), name="dw_in",
    )(dpm, dgab, dpm, h)


def _tn_matmul(a, b, name, shards=1, tm=1024, rider=None, rider_ins=()):
    m, n = a.shape[1], b.shape[1]
    tm = min(tm, m)
    tn = n // shards if shards > 1 else min(n, 1024)
    steps_i, steps_j = m // tm, n // tn
    r_in, r_out_specs, r_out_shape, r_scratch = _rider_specs(rider, rider_ins)

    def body(*refs):
        (a_ref, b_ref), ride_in, (o_ref,), ride_out, scratch = _split(refs, 2, len(r_in), 1, len(r_out_specs), len(r_scratch))
        step = pl.program_id(0) * steps_j + pl.program_id(1)
        _ride(rider, ("first",), step == 0, ride_in, ride_out, scratch)
        o_ref[...] = _tn(a_ref[...], b_ref[...]).astype(BF16)
        _ride(rider, ("middle", "last"), step == steps_i * steps_j - 1, ride_in, ride_out, scratch)

    if shards > 1:
        out_spec = pl.BlockSpec((None, tm, tn), lambda i, j: (j, i, 0))
        out_shape = jax.ShapeDtypeStruct((shards, m, tn), BF16)
    else:
        out_spec = pl.BlockSpec((tm, tn), lambda i, j: (i, j))
        out_shape = jax.ShapeDtypeStruct((m, n), BF16)
    res = pl.pallas_call(
        body, grid=(steps_i, steps_j),
        in_specs=[pl.BlockSpec((SEQ, tm), lambda i, j: (0, i)), pl.BlockSpec((SEQ, tn), lambda i, j: (0, j))] + r_in,
        out_specs=[out_spec] + r_out_specs, out_shape=[out_shape] + r_out_shape, scratch_shapes=r_scratch,
        compiler_params=_params(("arbitrary", "arbitrary")), name=name,
    )(a, b, *rider_ins)
    return res[0] if rider is None else (res[0], res[1:])


MESH = pl.DeviceIdType.MESH
ANY = pl.BlockSpec(memory_space=pl.ANY)
VMEM_WHOLE = pl.BlockSpec(memory_space=pltpu.VMEM)

_BIG = ("w_in", "w_pa", "w_pb", "w_o", "w_up", "w_down")
_BIG_SHARD = {"w_in": (IN_SHARD, D_MODEL), "w_pa": (512, 256), "w_pb": (512, 256), "w_o": (256, D_MODEL),
              "w_up": (D_MODEL, 1024), "w_down": (1024, D_MODEL),
              "w_down_a": (512, D_MODEL), "w_down_b": (512, D_MODEL)}
_BIG_SPLIT = {"w_in": 1, "w_pa": 0, "w_pb": 0, "w_o": 0, "w_up": 0, "w_down": 0, "w_down_a": 0, "w_down_b": 0}


def _half(ref, e, name, lead=0, part=None):
    axis = _BIG_SPLIT[name]
    size = _BIG_SHARD[name][axis] // 2
    start = e * size
    if part is not None:
        size //= 2
        start = start + part * size
    start = pl.multiple_of(start, 128 if axis == 1 else 16)
    idx = [pl.ds(0, ref.shape[a]) for a in range(lead)]
    idx += [pl.ds(start, size), pl.ds(0, _BIG_SHARD[name][1])] if axis == 0 else [pl.ds(0, _BIG_SHARD[name][0]), pl.ds(start, size)]
    return ref.at[tuple(idx)]


def _half_shape(name):
    r, c = _BIG_SHARD[name]
    return (r // 2, c) if _BIG_SPLIT[name] == 0 else (r, c // 2)


def _remote(src, dst, send_sems, recv_sems, k, to):
    return pltpu.make_async_remote_copy(src_ref=src, dst_ref=dst, send_sem=send_sems.at[k], recv_sem=recv_sems.at[k],
                                        device_id=to, device_id_type=MESH)


def _mesh_place():
    x, y, c = lax.axis_index("x"), lax.axis_index("y"), lax.axis_index("c")
    return x, y, c, [(1 - x, y), (x, 1 - y), (1 - x, 1 - y)]


class _Gather:
    def __init__(self, names, small=(), middle_at=0.5):
        self.middle_at = middle_at
        self.names = tuple(names)
        self.nb = len(self.names)
        self.n = self.nb + len(small)
        self.n_sems = 8 * self.nb + 3 * len(small)
        self.n_flush = 6 * self.nb + len(small)
        self.out_shape = [jax.ShapeDtypeStruct((N_CHIP,) + _BIG_SHARD[nm], BF16) for nm in self.names]
        self.out_shape += [jax.ShapeDtypeStruct((N_CHIP,) + s.shape, s.dtype) for s in small]

    def _copies(self, ins, outs, ss, rs, k):
        x, y, c, _ = _mesh_place()
        name = self.names[k]
        me, xn, yn, dg = 2 * x + y, 2 * (1 - x) + y, 2 * x + (1 - y), 2 * (1 - x) + (1 - y)
        to_x, to_y, sibling = (1 - x, y, c), (x, 1 - y, c), (x, y, 1 - c)

        def region(slot, e, part=None):
            return _half(outs[k].at[slot], e, name, part=part)

        def copy(pair, src, dst, to):
            return _remote(src, dst, ss, rs, 8 * k + pair, to)

        mine = _half(ins[k], c, name)
        sent = [copy(0, mine, region(me, c), to_x), copy(1, mine, region(me, c), to_y),
                copy(2, region(xn, c, 0), region(xn, c, 0), to_y), copy(3, region(yn, c, 1), region(yn, c, 1), to_x),
                copy(4, region(xn, c), region(xn, c), sibling), copy(5, region(yn, c), region(yn, c), sibling),
                copy(6, region(dg, c, 0), region(dg, c, 0), sibling), copy(7, region(dg, c, 1), region(dg, c, 1), sibling)]
        landing = [region(xn, c), region(yn, c), region(dg, c, 0), region(dg, c, 1),
                   region(xn, 1 - c), region(yn, 1 - c), region(dg, 1 - c, 0), region(dg, 1 - c, 1)]
        received = [copy(pair, dst, dst, sibling) for pair, dst in enumerate(landing)]
        return sent, received

    def _small(self, ins, outs, ss, rs, k, j, peer, slot, c):
        return _remote(ins[k], outs[k].at[slot], ss, rs, 8 * self.nb + 3 * (k - self.nb) + j, (*peer, c))

    def flush(self, phase, lands, outs, fs):
        x, y, c, _ = _mesh_place()
        me, xn, yn, dg = 2 * x + y, 2 * (1 - x) + y, 2 * x + (1 - y), 2 * (1 - x) + (1 - y)

        def pieces(k):
            name = self.names[k]
            spots = [lambda r: r.at[me], lambda r: _half(r.at[xn], c, name), lambda r: _half(r.at[yn], c, name),
                     lambda r: _half(r.at[xn], 1 - c, name), lambda r: _half(r.at[yn], 1 - c, name), lambda r: r.at[dg]]
            return [pltpu.make_async_copy(spot(lands[k]), spot(outs[k]), fs.at[6 * k + t]) for t, spot in enumerate(spots)]

        ready = {"first": (0,), "middle": (1, 2), "late": (3, 4), "last": (5,)}[phase]
        for k in range(self.nb):
            cps = pieces(k)
            for t in ready:
                cps[t].start()
        if phase == "last":
            small = [pltpu.make_async_copy(lands[k], outs[k], fs.at[6 * self.nb + k - self.nb]) for k in range(self.nb, self.n)]
            for cp in small:
                cp.start()
            for k in range(self.nb):
                for cp in pieces(k):
                    cp.wait()
            for cp in small:
                cp.wait()

    def first(self, ins, outs, ss, rs):
        x, y, c, peers = _mesh_place()
        me = 2 * x + y
        for k in range(self.nb):
            sent, _ = self._copies(ins, outs, ss, rs, k)
            sent[0].start()
            sent[1].start()
        for k in range(self.nb, self.n):
            for j, peer in enumerate(peers):
                self._small(ins, outs, ss, rs, k, j, peer, me, c).start()
        for k in range(self.n):
            outs[k][me] = ins[k][...]

    def middle(self, ins, outs, ss, rs):
        for k in range(self.nb):
            sent, received = self._copies(ins, outs, ss, rs, k)
            for pair in (0, 1):
                received[pair].wait_recv()
                sent[2 + pair].start()
                sent[4 + pair].start()

    def late(self, ins, outs, ss, rs):
        for k in range(self.nb):
            _, received = self._copies(ins, outs, ss, rs, k)
            for pair in (4, 5):
                received[pair].wait_recv()

    def last(self, ins, outs, ss, rs):
        x, y, c, peers = _mesh_place()
        for k in range(self.nb):
            sent, received = self._copies(ins, outs, ss, rs, k)
            for pair in (2, 3):
                received[pair].wait_recv()
                sent[4 + pair].start()
        for k in range(self.nb):
            sent, received = self._copies(ins, outs, ss, rs, k)
            for pair in (6, 7):
                received[pair].wait_recv()
            for cp in sent:
                cp.wait_send()
        for k in range(self.nb, self.n):
            for j, (px, py) in enumerate(peers):
                self._small(ins, outs, ss, rs, k, j, (px, py), 2 * px + py, c).wait_recv()
                self._small(ins, outs, ss, rs, k, j, (px, py), 2 * x + y, c).wait_send()


def _run_alone(rider, ins, name):
    r_in, r_out_specs, r_out_shape, r_scratch = _rider_specs(rider, ins)

    def body(*refs):
        ride_in, ride_out, scratch = _split(refs, len(r_in), len(r_out_specs), len(r_scratch))
        _ride(rider, ("first", "middle", "last"), pl.program_id(0) == 0, ride_in, ride_out, scratch)

    return pl.pallas_call(
        body, grid=(1,), in_specs=r_in, out_specs=r_out_specs, out_shape=r_out_shape, scratch_shapes=r_scratch,
        compiler_params=_params(("arbitrary",)), name=name,
    )(*ins)


class _Presum:
    in_space = ANY

    def __init__(self, names, base=0):
        self.names = tuple(names)
        self.n = len(self.names)
        self.base = base
        self.n_sems = 3 * self.n
        self.out_shape = [jax.ShapeDtypeStruct((N_CHIP,) + _half_shape(nm), BF16) for nm in self.names]
        self.work_shape = self.out_shape + self.out_shape

    def _stage(self, ins, bufs, ss, k, e, which):
        n = self.n
        return pltpu.make_async_copy(_half(ins[k], e, self.names[k], lead=1), bufs[which * n + k],
                                     ss.at[self.base + which * n + k])

    def _give(self, bufs, ss, rs, k, sibling):
        return _remote(bufs[self.n + k], bufs[k], ss, rs, self.base + k, sibling)

    def first(self, ins, bufs, ss, rs):
        x, y, c, _ = _mesh_place()
        for k in range(self.n):
            self._stage(ins, bufs, ss, k, 1 - c, 1).start()
        for k in range(self.n):
            self._stage(ins, bufs, ss, k, c, 2).start()
        for k in range(self.n):
            self._stage(ins, bufs, ss, k, 1 - c, 1).wait()
            self._give(bufs, ss, rs, k, (x, y, 1 - c)).start()

    def middle(self, ins, bufs, ss, rs):
        pass

    def last(self, ins, bufs, ss, rs):
        x, y, c, _ = _mesh_place()
        for k in range(self.n):
            self._give(bufs, ss, rs, k, (x, y, 1 - c)).wait_recv()
            self._stage(ins, bufs, ss, k, c, 2).wait()

            @pl.loop(0, N_CHIP)
            def _(j):
                bufs[k][j] = (bufs[k][j].astype(F32) + bufs[2 * self.n + k][j].astype(F32)).astype(BF16)
        for k in range(self.n):
            self._give(bufs, ss, rs, k, (x, y, 1 - c)).wait_send()


class _ReduceRelay:
    middle_at = 0.75

    def __init__(self, names, base=0):
        self.names = tuple(names)
        self.n = len(self.names)
        self.base = base
        self.n_sems = 6 * self.n
        self.out_shape = [jax.ShapeDtypeStruct((N_CHIP,) + _half_shape(nm), BF16) for nm in self.names]
        quarter = [jax.ShapeDtypeStruct(self._part_shape(nm), BF16) for nm in self.names]
        self.work_shape = quarter + quarter

    @staticmethod
    def _part_shape(name):
        r, c = _half_shape(name)
        return (r // 2, c) if _BIG_SPLIT[name] == 0 else (r, c // 2)

    def _part(self, ref, name, p):
        r, c = self._part_shape(name)
        return ref.at[pl.ds(p * r, r), pl.ds(0, c)] if _BIG_SPLIT[name] == 0 else ref.at[pl.ds(0, r), pl.ds(p * c, c)]

    def _copies(self, ins, bufs, ss, rs, k):
        x, y, c, _ = _mesh_place()
        name, n = self.names[k], self.n
        me, xn, yn, dg = 2 * x + y, 2 * (1 - x) + y, 2 * x + (1 - y), 2 * (1 - x) + (1 - y)
        to_x, to_y = (1 - x, y, c), (x, 1 - y, c)
        mine = lambda slot, p: self._part(ins[k].at[slot], name, p)
        slot = lambda s, p: self._part(bufs[k].at[s], name, p)
        from_x, from_y = bufs[n + k], bufs[2 * n + k]

        def copy(pair, src, dst, to):
            return _remote(src, dst, ss, rs, self.base + 6 * k + pair, to)

        sent = [copy(0, mine(dg, 0), from_x, to_x), copy(1, mine(dg, 1), from_y, to_y),
                copy(2, mine(xn, 0), slot(me, 0), to_x), copy(3, mine(yn, 1), slot(me, 1), to_y),
                copy(4, from_y, slot(me, 1), to_x), copy(5, from_x, slot(me, 0), to_y)]
        landing = [from_x, from_y, slot(xn, 0), slot(yn, 1), slot(xn, 1), slot(yn, 0)]
        received = [copy(pair, dst, dst, to_x) for pair, dst in enumerate(landing)]
        return sent, received

    def first(self, ins, bufs, ss, rs):
        x, y, c, _ = _mesh_place()
        me, dg = 2 * x + y, 2 * (1 - x) + (1 - y)
        for k in range(self.n):
            sent, _ = self._copies(ins, bufs, ss, rs, k)
            for pair in range(4):
                sent[pair].start()
        for k in range(self.n):
            bufs[k][me] = ins[k][me]
            bufs[k][dg] = jnp.zeros(_half_shape(self.names[k]), BF16)

    def middle(self, ins, bufs, ss, rs):
        x, y, c, _ = _mesh_place()
        xn, yn = 2 * (1 - x) + y, 2 * x + (1 - y)
        for k in range(self.n):
            sent, received = self._copies(ins, bufs, ss, rs, k)
            name, n = self.names[k], self.n
            for pair, buf, own in ((0, bufs[n + k], self._part(ins[k].at[yn], name, 0)),
                                   (1, bufs[2 * n + k], self._part(ins[k].at[xn], name, 1))):
                received[pair].wait_recv()
                buf[...] = (buf[...].astype(F32) + own[...].astype(F32)).astype(BF16)
            sent[5].start()
            sent[4].start()

    def last(self, ins, bufs, ss, rs):
        for k in range(self.n):
            sent, received = self._copies(ins, bufs, ss, rs, k)
            for pair in range(2, 6):
                received[pair].wait_recv()
            for cp in sent:
                cp.wait_send()


class _PresumThenRelay:
    in_space = ANY
    middle_at = _ReduceRelay.middle_at

    def __init__(self, names):
        self.relay = _ReduceRelay(names)
        self.pre = _Presum(names, base=self.relay.n_sems)
        self.n_sems = self.relay.n_sems + self.pre.n_sems
        self.out_shape = self.relay.out_shape
        self.work_shape = list(self.relay.work_shape) + list(self.pre.out_shape) + list(self.pre.work_shape)
        self.n_relay = len(self.relay.out_shape) + len(self.relay.work_shape)

    def first(self, ins, bufs, ss, rs):
        self.pre.first(ins, bufs[self.n_relay:], ss, rs)

    def early(self, ins, bufs, ss, rs):
        self.pre.last(ins, bufs[self.n_relay:], ss, rs)
        self.relay.first(bufs[self.n_relay:], bufs[:self.n_relay], ss, rs)

    def middle(self, ins, bufs, ss, rs):
        self.relay.middle(bufs[self.n_relay:], bufs[:self.n_relay], ss, rs)

    def last(self, ins, bufs, ss, rs):
        self.relay.last(bufs[self.n_relay:], bufs[:self.n_relay], ss, rs)


class _SendPartials:
    def __init__(self, names, small_shape=None):
        self.n = len(names)
        self.small = small_shape is not None
        self.n_sems = 3 * self.n + 7
        self.out_shape = [jax.ShapeDtypeStruct((N_CHIP,) + _half_shape(nm), BF16) for nm in names]
        if self.small:
            self.out_shape.append(jax.ShapeDtypeStruct((N_DEV,) + small_shape, F32))

    def _piece(self, ins, outs, ss, rs, k, j, peer, src_slot, dst_slot, c):
        return _remote(ins[k].at[src_slot], outs[k].at[dst_slot], ss, rs, 3 * k + j, (*peer, c))

    def _small(self, ins, outs, ss, rs, r, other, slot):
        return _remote(ins[self.n], outs[self.n].at[slot], ss, rs, 3 * self.n + r, other)

    @staticmethod
    def _others(x, y, c):
        return [(x, y, 1 - c), (1 - x, y, c), (1 - x, y, 1 - c), (x, 1 - y, c), (x, 1 - y, 1 - c),
                (1 - x, 1 - y, c), (1 - x, 1 - y, 1 - c)]

    def first(self, ins, outs, ss, rs, only=None):
        x, y, c, peers = _mesh_place()
        me = 2 * x + y
        which = range(self.n) if only is None else only
        for k in which:
            for j, (px, py) in enumerate(peers):
                self._piece(ins, outs, ss, rs, k, j, (px, py), 2 * px + py, me, c).start()
        if self.small:
            for r, other in enumerate(self._others(x, y, c)):
                self._small(ins, outs, ss, rs, r, other, 4 * x + 2 * y + c).start()
            outs[self.n][4 * x + 2 * y + c] = ins[self.n][...]
        for k in which:
            outs[k][me] = ins[k][me]

    def middle(self, ins, outs, ss, rs):
        pass

    def last(self, ins, outs, ss, rs):
        x, y, c, peers = _mesh_place()
        me = 2 * x + y
        for k in range(self.n):
            for j, (px, py) in enumerate(peers):
                self._piece(ins, outs, ss, rs, k, j, (px, py), me, 2 * px + py, c).wait_recv()
                self._piece(ins, outs, ss, rs, k, j, (px, py), 2 * px + py, me, c).wait_send()
        if self.small:
            for r, (px, py, pc) in enumerate(self._others(x, y, c)):
                self._small(ins, outs, ss, rs, r, (px, py, pc), 4 * px + 2 * py + pc).wait_recv()
                self._small(ins, outs, ss, rs, r, (px, py, pc), 4 * x + 2 * y + c).wait_send()


class _PresumThenSend:
    def __init__(self, names):
        self.send = _SendPartials(names)
        self.pre = _Presum(names[-1:], base=self.send.n_sems)
        self.n = self.send.n
        self.n_sems = self.send.n_sems + self.pre.n_sems
        self.out_shape = self.send.out_shape
        self.work_shape = list(self.pre.out_shape) + list(self.pre.work_shape)
        self.in_space = [VMEM_WHOLE] * (self.n - 1) + [ANY]

    def _partials(self, ins, bufs):
        return list(ins[:self.n - 1]) + [bufs[self.n]]

    def first(self, ins, bufs, ss, rs):
        self.pre.first(ins[self.n - 1:], bufs[self.n:], ss, rs)
        self.send.first(ins, bufs[:self.n], ss, rs, only=range(self.n - 1))

    def early(self, ins, bufs, ss, rs):
        self.pre.last(ins[self.n - 1:], bufs[self.n:], ss, rs)
        self.send.first(self._partials(ins, bufs), bufs[:self.n], ss, rs, only=(self.n - 1,))

    def middle(self, ins, bufs, ss, rs):
        pass

    def last(self, ins, bufs, ss, rs):
        self.send.last(self._partials(ins, bufs), bufs[:self.n], ss, rs)


def _sum_swap(names, parts, small):
    n = len(parts)
    everyone = _SendPartials((), small.shape)

    def body(*refs):
        p_refs, (small_ref,), o_refs, (osmall_ref,), (all_ref,), (send_sems, recv_sems, ss_small, rs_small) = _split(
            refs, n, 1, n, 1, 1, 4)
        x, y, c = lax.axis_index("x"), lax.axis_index("y"), lax.axis_index("c")
        everyone.first([small_ref], [all_ref], ss_small, rs_small)

        def mine(k):
            part = _half(o_refs[k], c, names[k])
            return _remote(part, part, send_sems, recv_sems, k, (x, y, 1 - c))

        for k in range(n):
            for e in range(2):
                @pl.when(c == e)
                def _():
                    g = p_refs[k][0].astype(F32)
                    for s in range(1, N_CHIP):
                        g = g + p_refs[k][s].astype(F32)
                    r, cols = _half_shape(names[k])
                    if _BIG_SPLIT[names[k]] == 0:
                        o_refs[k][e * r:(e + 1) * r, :] = g
                    else:
                        o_refs[k][:, e * cols:(e + 1) * cols] = g
            mine(k).start()
        for k in range(n):
            theirs = _half(o_refs[k], 1 - c, names[k])
            _remote(theirs, theirs, send_sems, recv_sems, k, (x, y, 1 - c)).wait_recv()
            mine(k).wait_send()
        everyone.last([small_ref], [all_ref], ss_small, rs_small)
        g = all_ref[0]
        for d in range(1, N_DEV):
            g = g + all_ref[d]
        osmall_ref[...] = g

    res = pl.pallas_call(
        body, in_specs=[VMEM_WHOLE] * (n + 1), out_specs=[VMEM_WHOLE] * (n + 1),
        out_shape=[jax.ShapeDtypeStruct(_BIG_SHARD[nm], F32) for nm in names] + [jax.ShapeDtypeStruct(small.shape, F32)],
        scratch_shapes=[pltpu.VMEM((N_DEV,) + small.shape, F32), pltpu.SemaphoreType.DMA((n,)), pltpu.SemaphoreType.DMA((n,)),
                        pltpu.SemaphoreType.DMA((everyone.n_sems,)), pltpu.SemaphoreType.DMA((everyone.n_sems,))],
        compiler_params=_params(), name="sum_swap",
    )(*parts, small)
    return res[:n], res[n]


def _tile(rows, cols, itemsize, budget):
    t = cols if rows % 16 else rows
    other = rows if rows % 16 else cols
    step = 256 if rows % 16 else 32
    while t % step == 0 and t * other * itemsize > budget:
        t //= 2
    return (rows, t) if rows % 16 else (t, cols)


def _adamw_math(w, g, m, v):
    m = ADAM_B1 * m + (1.0 - ADAM_B1) * g
    v = ADAM_B2 * v + (1.0 - ADAM_B2) * (g * g)
    m_hat = m / (1.0 - ADAM_B1 ** ADAM_STEP)
    v_hat = v / (1.0 - ADAM_B2 ** ADAM_STEP)
    delta = -ADAM_LR * (m_hat / (jnp.sqrt(v_hat) + ADAM_EPS) + ADAM_WD * w)
    return delta, m, v


def _adamw_big(g, w, m, v, name):
    r, c = w.shape
    tr, tc = _tile(r, c, 4, 2 * 1024 * 1024)

    def body(g_ref, w_ref, m_ref, v_ref, d_ref, nm_ref, nv_ref):
        d_ref[...], nm_ref[...], nv_ref[...] = _adamw_math(w_ref[...], g_ref[...], m_ref[...], v_ref[...])

    blk = pl.BlockSpec((tr, tc), lambda i, l: (i, l))
    return pl.pallas_call(
        body, grid=(r // tr, c // tc), in_specs=[blk, blk, blk, blk],
        out_specs=[blk, blk, blk], out_shape=[jax.ShapeDtypeStruct((r, c), F32)] * 3,
        compiler_params=_params(("arbitrary", "arbitrary")), name=name,
    )(g, w, m, v)


def _adamw_rows(g, w, m, v, name):
    r, k, lanes = w.shape
    tr = 296

    def body(g_ref, w_ref, m_ref, v_ref, g3_ref, d_ref, nm_ref, nv_ref):
        g = g_ref[...].reshape(tr, k, lanes)
        g3_ref[...] = g
        d_ref[...], nm_ref[...], nv_ref[...] = _adamw_math(w_ref[...], g, m_ref[...], v_ref[...])

    rows = pl.BlockSpec((tr, k, lanes), lambda i: (i, 0, 0))
    return pl.pallas_call(
        body, grid=(pl.cdiv(r, tr),), in_specs=[pl.BlockSpec((tr, k * lanes), lambda i: (i, 0)), rows, rows, rows],
        out_specs=[rows] * 4, out_shape=[jax.ShapeDtypeStruct((r, k, lanes), F32)] * 4,
        compiler_params=_params(("arbitrary",)), name=name,
    )(g, w, m, v)


def _adamw_small(ws, gs, ms, vs):
    n = len(ws)

    def body(*refs):
        w_refs, g_refs, m_refs, v_refs, d_refs, nm_refs, nv_refs = _split(refs, *([n] * 7))
        for k in range(n):
            d_refs[k][...], nm_refs[k][...], nv_refs[k][...] = _adamw_math(w_refs[k][...], g_refs[k][...], m_refs[k][...],
                                                                             v_refs[k][...])

    shapes = [jax.ShapeDtypeStruct(w.shape, F32) for w in ws]
    res = pl.pallas_call(body, out_shape=shapes * 3, name="adamw_small")(*ws, *gs, *ms, *vs)
    return res[:n], res[n:2 * n], res[2 * n:]


def _pack(arrs):
    flat = jnp.concatenate([a.reshape(-1) for a in arrs])
    rows = -(-flat.shape[0] // 1024) * 8
    return jnp.pad(flat, (0, rows * 128 - flat.shape[0])).reshape(rows, 128)


def _unpack(buf, shapes):
    flat = buf.reshape(-1)
    out, off = [], 0
    for s in shapes:
        size = 1
        for d in s:
            size *= d
        out.append(flat[off:off + size].reshape(s))
        off += size
    return out


def _block_rows(w):
    return jnp.pad(w.reshape(512, 4), ((0, 0), (0, 124)))


def _cols(a4):
    return jnp.transpose(a4, (1, 0, 2)).reshape(a4.shape[1], -1)


_LATE = ("w_pa", "w_pb", "w_o", "w_up", "w_down")
_RIDE_IN_PROJ = ("w_pa", "w_pb", "w_o", "w_down_b")
_RIDE_MIXER = ("w_up", "w_down_a")


def _full_weights(gathered):
    joined = {"w_in": (D_IN, D_MODEL), "w_o": (D_MODEL, D_MODEL)}
    return {n: (a.reshape(joined[n]) if n in joined else a) for n, a in gathered.items()}


def _local_step(x, target, w, sp, late_shards=None):
    sp = {n: (a.reshape(1, -1) if a.ndim == 1 else a) for n, a in sp.items()}
    wau = jnp.pad(sp["w_a_up"], ((0, 112), (0, 0)))
    wif = jnp.pad(sp["w_if"], ((0, 0), (0, 120)))
    bif = jnp.pad(sp["b_if"], ((0, 0), (0, 120)))
    p = {"wau": wau, "bau": sp["b_a_up"], "ggla": sp["g_gla_norm"], "cw": sp["conv_w"], "cb": sp["conv_b"],
         "wq": _block_rows(sp["w_q_ml"]), "wk": _block_rows(sp["w_k_ml"]), "wv": _block_rows(sp["w_v_ml"]),
         "wif": wif, "bif": bif, "skip": sp["ml_skip"], "gml": sp["g_ml_norm"]}

    if late_shards is None:
        (pm, gab, h), _ = _in_proj(x, sp["g_pre_mix"], w["w_in"])
        ab, *states = _mixer_fwd(pm, p)
        (x1, mix, merged), _ = _merge_fwd(ab, gab, x, w["w_pa"], w["w_pb"], w["w_o"], sp["g_post_mix"])
    else:
        shard = dict(zip(_LATE, late_shards))
        shard["w_down_a"], shard["w_down_b"] = shard["w_down"][0:512], shard["w_down"][512:1024]
        (pm, gab, h), got = _in_proj(x, sp["g_pre_mix"], w["w_in"], _Gather(_RIDE_IN_PROJ, middle_at=0.7),
                                     [shard[n] for n in _RIDE_IN_PROJ])
        w = dict(w, **_full_weights(dict(zip(_RIDE_IN_PROJ, got))))
        ab, *rest = _mixer_fwd(pm, p, _Gather(_RIDE_MIXER, middle_at=0.62), [shard[n] for n in _RIDE_MIXER])
        states = rest[:4]
        w.update(_full_weights(dict(zip(_RIDE_MIXER, rest[4:]))))
        (x1, mix, merged), _ = _merge_fwd(ab, gab, x, w["w_pa"], w["w_pb"], w["w_o"], sp["g_post_mix"])
    dx1, u, dd, h2, dpre, dg_post_mlp, dg_pre_mlp, loss = _mlp(x1, target, sp["g_pre_mlp"], sp["g_post_mlp"],
                                                                w["w_up"], w["w_down_a"], w["w_down_b"])
    dgab, dab, dg_post_mix, dw_pa, dw_pb, dw_o = _merge_bwd(dx1, mix, ab, gab, merged, w["w_pa"], w["w_pb"], w["w_o"],
                                                            sp["g_post_mix"])
    big = {"w_pa": dw_pa, "w_pb": dw_pb, "w_o": dw_o, "w_up": _tn_matmul(h2, dpre, "dw_up", shards=N_CHIP)}
    if late_shards is None:
        big["w_down"] = _tn_matmul(u, dd, "dw_down")
        dpm, dp, _ = _mixer_bwd(pm, dab, states, p)
    else:
        pieces = lambda n: big[n].reshape((N_CHIP,) + _BIG_SHARD[n])
        big["w_down"], partial = _tn_matmul(u, dd, "dw_down", rider=_Presum(_LATE[:4]),
                                            rider_ins=[pieces(n) for n in _LATE[:4]])
        dpm, dp, parts = _mixer_bwd(pm, dab, states, p, _PresumThenSend(_LATE), list(partial) + [pieces("w_down")])
        big = dict(zip(_LATE, parts))
    big["w_in"] = _dw_in(dpm, dgab, h)
    if late_shards is None:
        (dx, dg_pre_mix), _ = _in_proj_bwd(dpm, dgab, x, dx1, sp["g_pre_mix"], w["w_in"])
    else:
        (dx, dg_pre_mix), parts = _in_proj_bwd(dpm, dgab, x, dx1, sp["g_pre_mix"], w["w_in"], _PresumThenRelay(("w_in",)),
                                               [big["w_in"].reshape((N_CHIP,) + _BIG_SHARD["w_in"])])
        big["w_in"] = parts[0]
    small = {
        "g_pre_mix": dg_pre_mix, "b_a_up": dp["bau"], "g_gla_norm": dp["ggla"], "conv_b": dp["cb"],
        "w_q_ml": dp["wq"][:, 0:4].reshape(128, 4, 4), "w_k_ml": dp["wk"][:, 0:4].reshape(128, 4, 4),
        "w_v_ml": dp["wv"][:, 0:4].reshape(128, 4, 4),
        "b_if": dp["bif"][:, 0:8], "ml_skip": dp["skip"], "g_ml_norm": dp["gml"], "g_post_mix": dg_post_mix,
        "g_pre_mlp": dg_pre_mlp, "g_post_mlp": dg_post_mlp, "w_a_up": dp["wau"][0:16], "conv_w": dp["cw"],
        "w_if": dp["wif"][:, 0:8], "loss": loss[:, 0:1],
    }
    return dx, big, small


_SMALL_REPL = ("g_pre_mix", "b_a_up", "g_gla_norm", "conv_b", "w_q_ml", "w_k_ml", "w_v_ml", "b_if", "ml_skip",
               "g_ml_norm", "g_post_mix", "g_pre_mlp", "g_post_mlp")
_SMALL_SHARDED = ("w_a_up", "conv_w", "w_if")
_SMALL_ORDER = _SMALL_REPL + _SMALL_SHARDED + ("loss",)
_WEIGHTS = ("g_pre_mix", "w_in", "w_a_up", "b_a_up", "g_gla_norm", "conv_w", "conv_b", "w_q_ml", "w_k_ml", "w_v_ml",
            "w_if", "b_if", "ml_skip", "g_ml_norm", "w_pa", "w_pb", "w_o", "g_post_mix", "g_pre_mlp", "w_up", "w_down",
            "g_post_mlp")


_BLOCK_WEIGHTS = ("w_q_ml", "w_k_ml", "w_v_ml")


def _stored(name, a):
    if name in _BLOCK_WEIGHTS:
        return jnp.transpose(a, (0, 2, 3, 1)).reshape(16, 128)
    if name == "w_if":
        return jnp.transpose(a, (0, 2, 1)).reshape(8, 384)
    return a


def _unstored(name, a):
    if name in _BLOCK_WEIGHTS:
        return jnp.transpose(a.reshape(1, 4, 4, 128), (0, 3, 1, 2))
    if name == "w_if":
        return jnp.transpose(a.reshape(1, 8, 384), (0, 2, 1))
    return a


def _as_shard(name, a):
    return jnp.transpose(a, (2, 0, 1)).reshape(IN_SHARD, D_MODEL // 128, 128) if name == "w_in" else a[0]


def _in_shard_bf16(w_in):
    return jnp.transpose(w_in.astype(BF16), (2, 0, 1)).reshape(IN_SHARD, D_MODEL)


def _from_shard(name, a):
    return jnp.transpose(a, (1, 2, 0)).reshape(1, D_MODEL, IN_SHARD) if name == "w_in" else a[None]


def kernel(x, g_pre_mix, w_in, w_a_up, b_a_up, g_gla_norm, conv_w, conv_b, w_q_ml, w_k_ml, w_v_ml, w_if, b_if, ml_skip, g_ml_norm, w_pa, w_pb, w_o, g_post_mix, g_pre_mlp, w_up, w_down, g_post_mlp, loss_target, m_g_pre_mix, m_w_in, m_w_a_up, m_b_a_up, m_g_gla_norm, m_conv_w, m_conv_b, m_w_q_ml, m_w_k_ml, m_w_v_ml, m_w_if, m_b_if, m_ml_skip, m_g_ml_norm, m_w_pa, m_w_pb, m_w_o, m_g_post_mix, m_g_pre_mlp, m_w_up, m_w_down, m_g_post_mlp, v_g_pre_mix, v_w_in, v_w_a_up, v_b_a_up, v_g_gla_norm, v_conv_w, v_conv_b, v_w_q_ml, v_w_k_ml, v_w_v_ml, v_w_if, v_b_if, v_ml_skip, v_g_ml_norm, v_w_pa, v_w_pb, v_w_o, v_g_post_mix, v_g_pre_mlp, v_w_up, v_w_down, v_g_post_mlp):
    args = dict(locals())
    wts = {n: _as_shard(n, args[n]) for n in _WEIGHTS}
    mom = {n: _as_shard(n, args["m_" + n]) for n in _WEIGHTS}
    var = {n: _as_shard(n, args["v_" + n]) for n in _WEIGHTS}
    chip = 2 * lax.axis_index("x") + lax.axis_index("y")

    first = ("w_in",) + _SMALL_SHARDED
    gathered = dict(zip(first, _run_alone(_Gather(("w_in",), [wts[n] for n in _SMALL_SHARDED]),
                                          [_in_shard_bf16(w_in)] + [wts[n] for n in _SMALL_SHARDED],
                                          "gather_first")))
    sp = {n: wts[n] for n in _SMALL_REPL}
    sp["w_a_up"] = _cols(gathered["w_a_up"])
    sp["conv_w"] = _cols(gathered["conv_w"])
    sp["w_if"] = gathered["w_if"].reshape(1536, 8)

    dx, big, small = _local_step(x[0], loss_target[0], _full_weights({"w_in": gathered["w_in"]}), sp,
                                 late_shards=[wts[n].astype(BF16) for n in _LATE])

    small_shapes = [small[n].shape for n in _SMALL_ORDER]
    packed = _pack([small[n] for n in _SMALL_ORDER])
    sums, small_sum = _sum_swap(_BIG, [big[n] for n in _BIG], packed)

    grads, delta, new_m, new_v = {}, {}, {}, {}
    for n, g in zip(_BIG, sums):
        if n == "w_in":
            g, d, nm, nv = _adamw_rows(g, wts[n], mom[n], var[n], "adamw_" + n)
        else:
            d, nm, nv = _adamw_big(g, wts[n], mom[n], var[n], "adamw_" + n)
        grads[n], delta[n], new_m[n], new_v[n] = (_from_shard(n, a) for a in (g, d, nm, nv))
    summed = dict(zip(_SMALL_ORDER, _unpack(small_sum, small_shapes)))
    loss = summed["loss"].reshape(())
    summed["w_a_up"] = lax.dynamic_slice_in_dim(summed["w_a_up"], chip * 64, 64, axis=1)
    summed["conv_w"] = lax.dynamic_slice_in_dim(summed["conv_w"], chip * 128, 128, axis=1)
    summed["w_if"] = lax.dynamic_slice_in_dim(summed["w_if"], chip * 384, 384, axis=0)
    small_names = _SMALL_REPL + _SMALL_SHARDED
    g_stored = [_stored(n, summed[n].reshape(args[n].shape)) for n in small_names]
    upd = _adamw_small([_stored(n, args[n]) for n in small_names], g_stored,
                       [_stored(n, args["m_" + n]) for n in small_names], [_stored(n, args["v_" + n]) for n in small_names])
    for dst, arrs in zip((grads, delta, new_m, new_v), (g_stored,) + tuple(upd)):
        dst.update({n: _unstored(n, a) for n, a in zip(small_names, arrs)})

    outs = [loss, dx[None]]
    for group in (grads, delta, new_m, new_v):
        outs += [group[n] for n in _WEIGHTS]
    return tuple(outs)
```

```python
import functools

import jax
import jax.numpy as jnp
from jax import lax
from jax.experimental import pallas as pl
from jax.experimental.pallas import tpu as pltpu

F32 = jnp.float32
BF16 = jnp.bfloat16

SEQ = 2048
D_MODEL = 1024
CHUNK = 64
N_CHUNK = SEQ // CHUNK
HEADS = 4
GLA_DK = 64
GLA_DV = 128
ML_DH = 128
D_FF = 4096
EPS = 1e-6
N_CHIP = 4
N_DEV = 8
TOK_TILE = 256
N_TOK_TILE = SEQ // TOK_TILE
SWEEP = 2
assert CHUNK == 64
N_SWEEP = N_CHUNK // SWEEP

PM_W = 2688
PM_XM = 1536
PM_OP = 2048
PM_AL = 2560
GAB_W = 2048
D_IN = 4624
IN_SHARD = D_IN // N_CHIP
IN_ALOW = 1536
IN_XM = 1552
IN_GATES = 2576

ADAM_LR = 0.001
ADAM_B1 = 0.9
ADAM_B2 = 0.999
ADAM_EPS = 1e-08
ADAM_WD = 0.01
ADAM_STEP = 10

VMEM_LIMIT = 56 * 1024 * 1024


def _params(sem=None):
    return pltpu.CompilerParams(dimension_semantics=sem, vmem_limit_bytes=VMEM_LIMIT)


def _dot(a, b, ca, cb):
    return lax.dot_general(a.astype(BF16), b.astype(BF16), (((ca,), (cb,)), ((), ())), preferred_element_type=F32)


def _pmm_nn(a, b):
    return _dot(a, b, 1, 0)


def _pmm_nt(a, b):
    return _dot(a, b, 1, 1)


def _pmm_tn(a, b):
    return _dot(a, b, 0, 0)


def _pcmm(c, x):
    return lax.dot_general(c, x, (((1,), (0,)), ((), ())), precision=lax.Precision.HIGHEST, preferred_element_type=F32)


@jax.custom_vjp
def _mm_nn(a, b):
    return _dot(a, b, 1, 0)


@jax.custom_vjp
def _mm_nt(a, b):
    return _dot(a, b, 1, 1)


@jax.custom_vjp
def _mm_tn(a, b):
    return _dot(a, b, 0, 0)


_mm_nn.defvjp(lambda a, b: (_dot(a, b, 1, 0), (a, b)), lambda r, g: (_mm_nt(g, r[1]), _mm_tn(r[0], g)))
_mm_nt.defvjp(lambda a, b: (_dot(a, b, 1, 1), (a, b)), lambda r, g: (_mm_nn(g, r[1]), _mm_tn(g, r[0])))
_mm_tn.defvjp(lambda a, b: (_dot(a, b, 0, 0), (a, b)), lambda r, g: (_mm_nt(r[1], g), _mm_nn(r[0], g)))


@jax.custom_vjp
def _cmm(c, x):
    return _pcmm(c, x)


_cmm.defvjp(
    lambda c, x: (_pcmm(c, x), c),
    lambda c, g: (jnp.zeros_like(c), lax.dot_general(c, g, (((0,), (0,)), ((), ())), precision=lax.Precision.HIGHEST,
                                                      preferred_element_type=F32)),
)

_PLAIN_OPS = (_pmm_nn, _pmm_nt, _pmm_tn, _pcmm)
_VJP_OPS = (_mm_nn, _mm_nt, _mm_tn, _cmm)


def _sigmoid(x):
    return 0.5 * (jnp.tanh(0.5 * x) + 1.0)


def _log_sigmoid(x):
    return jnp.minimum(x, 0.0) - jnp.log(1.0 + jnp.exp(-jnp.abs(x)))


def _mean(x):
    return jnp.mean(x, axis=-1, keepdims=True)


def _nt(a, b):
    return lax.dot_general(a, b, (((1,), (1,)), ((), ())), preferred_element_type=F32)


def _tn(a, b):
    return lax.dot_general(a, b, (((0,), (0,)), ((), ())), preferred_element_type=F32)


def _mixer_chunk(ops, p, st, pm, xprev8):
    mm_nn, mm_nt, mm_tn, cmm = ops
    n_rows = pm.shape[0]
    n_ch = n_rows // CHUNK
    row = lax.broadcasted_iota(jnp.int32, (n_rows, n_rows), 0)
    col = lax.broadcasted_iota(jnp.int32, (n_rows, n_rows), 1)
    tri = jnp.logical_and((row >> 6) == (col >> 6), row >= col).astype(F32)
    causal = tri[0:CHUNK, 0:CHUNK] > 0.0
    q = pm[:, 0:256]
    k = pm[:, 256:512]
    v = pm[:, 512:1024]
    g = pm[:, 1024:1536]
    xm = pm[:, PM_XM:PM_XM + 512]
    opre = pm[:, PM_OP:PM_OP + 512]
    alow = pm[:, PM_AL:PM_AL + 128]
    hs = range(HEADS)
    cs = range(n_ch)
    pairs = [(i, h) for i in cs for h in hs]
    rs = [slice(i * CHUNK, (i + 1) * CHUNK) for i in cs]
    last = [slice((i + 1) * CHUNK - 1, (i + 1) * CHUNK) for i in cs]
    s6 = [slice(h * GLA_DK, (h + 1) * GLA_DK) for h in hs]
    s12 = [slice(h * 128, (h + 1) * 128) for h in hs]

    xx = jnp.concatenate([xprev8, xm], axis=0)
    pre = p["cb"]
    for j in range(4):
        pre = pre + p["cw"][j:j + 1, :] * xx[5 + j:5 + j + n_rows, :]
    xc = pre * _sigmoid(pre)
    qm = [mm_nn(xc[:, s12[h]], p["wq"][h]) for h in hs]
    km = [mm_nn(xc[:, s12[h]], p["wk"][h]) for h in hs]
    vm = [mm_nn(xm[:, s12[h]], p["wv"][h]) for h in hs]
    qcat = jnp.concatenate(qm, axis=1)
    kcat = jnp.concatenate(km, axis=1)
    vcat = jnp.concatenate(vm, axis=1)
    gates = (mm_nn(qcat, p["wif"][0:512]) + mm_nn(kcat, p["wif"][512:1024]) + mm_nn(vcat, p["wif"][1024:1536])
             + p["bif"])
    lf = _log_sigmoid(gates)
    fc = cmm(tri, lf)
    gates_t = gates.T
    fc_t = fc.T

    la = _log_sigmoid(mm_nn(alow, p["wau"]) + p["bau"]) * (1.0 / 16.0)
    cum = cmm(tri, la)
    cum_last = [cum[last[i], :] for i in cs]
    to_end = jnp.concatenate([cum_last[i] - cum[rs[i], :] for i in cs], axis=0)
    e_pos = jnp.exp(cum)
    e_neg = jnp.exp(-cum)
    qs = q * (GLA_DK ** -0.5)
    qp = qs * e_pos
    qn = qs * e_neg
    kp = k * e_pos
    kn = k * e_neg
    kl = k * jnp.exp(to_end)
    dec = [jnp.exp(cum_last[i]) for i in cs]
    ks = [km[h] * (ML_DH ** -0.5) for h in hs]
    li_c = {(i, h): gates[rs[i], h:h + 1] for i, h in pairs}
    fc_c = {(i, h): fc[rs[i], 4 + h:5 + h] for i, h in pairs}
    f_last = {(i, h): fc[last[i], 4 + h:5 + h] for i, h in pairs}

    a_fwd = {(i, h): mm_nt(qp[rs[i], s6[h]], kn[rs[i], s6[h]]) for i, h in pairs}
    a_bwd = {(i, h): mm_nt(qn[rs[i], s6[h]], kp[rs[i], s6[h]]) for i, h in pairs}
    s_chunk = {(i, h): mm_tn(v[rs[i], s12[h]], kl[rs[i], s6[h]]) for i, h in pairs}
    qk = {(i, h): mm_nt(qm[h][rs[i]], ks[h][rs[i]]) for i, h in pairs}
    a = {ih: f_last[ih] - fc_c[ih] + li_c[ih] for ih in pairs}
    m_loc = {ih: jnp.max(a[ih], axis=0, keepdims=True) for ih in pairs}
    kw = {(i, h): ks[h][rs[i]] * jnp.exp(a[(i, h)] - m_loc[(i, h)]) for i, h in pairs}
    c_chunk = {(i, h): mm_tn(kw[(i, h)], vm[h][rs[i]]) for i, h in pairs}
    mem = {(0, h): st["S"][h] for h in hs}
    c_in = {(0, h): st["C"][h] for h in hs}
    n_in = {(0, h): st["n"][h] for h in hs}
    m_in = {(0, h): st["m"][h][:, 0:1] for h in hs}
    for i, h in pairs:
        mem[(i + 1, h)] = mem[(i, h)] * dec[i][:, s6[h]] + s_chunk[(i, h)]
        m_nx = jnp.maximum(f_last[(i, h)] + m_in[(i, h)], m_loc[(i, h)])
        sp = jnp.exp(f_last[(i, h)] + m_in[(i, h)] - m_nx)
        sl = jnp.exp(m_loc[(i, h)] - m_nx)
        c_in[(i + 1, h)] = sp * c_in[(i, h)] + sl * c_chunk[(i, h)]
        n_in[(i + 1, h)] = sp * n_in[(i, h)] + sl * jnp.sum(kw[(i, h)], axis=0, keepdims=True)
        m_in[(i + 1, h)] = m_nx
    s_new = [mem[(n_ch, h)] for h in hs]
    o_inter = {(i, h): mm_nt(qp[rs[i], s6[h]], mem[(i, h)]) for i, h in pairs}
    q_c = {(i, h): mm_nn(qm[h][rs[i]], c_in[(i, h)]) for i, h in pairs}
    scores = {ih: jnp.where(causal, a_fwd[ih], a_bwd[ih]) for ih in pairs}
    log_d = {(i, h): gates_t[h:h + 1, rs[i]] - jnp.abs(fc_c[(i, h)] - fc_t[4 + h:5 + h, rs[i]]) for i, h in pairs}
    g_int = {ih: fc_c[ih] + m_in[ih] for ih in pairs}
    m_t = {ih: jnp.maximum(g_int[ih], jnp.max(log_d[ih], axis=1, keepdims=True)) for ih in pairs}
    s = {ih: qk[ih] * jnp.exp(log_d[ih] - m_t[ih]) for ih in pairs}
    scl = {ih: jnp.exp(g_int[ih] - m_t[ih]) for ih in pairs}
    o = {(i, h): mm_nn(scores[(i, h)], v[rs[i], s12[h]]) + o_inter[(i, h)] for i, h in pairs}
    num = {(i, h): mm_nn(s[(i, h)], vm[h][rs[i]]) + scl[(i, h)] * q_c[(i, h)] for i, h in pairs}
    o = {ih: o[ih] * lax.rsqrt(_mean(o[ih] * o[ih]) + EPS) * p["ggla"] for ih in pairs}
    gate = g * _sigmoid(g)
    out_a = {(i, h): o[(i, h)] * gate[rs[i], s12[h]] for i, h in pairs}
    den = {(i, h): jnp.sum(s[(i, h)], axis=1, keepdims=True)
           + scl[(i, h)] * jnp.sum(qm[h][rs[i]] * n_in[(i, h)], axis=1, keepdims=True) for i, h in pairs}
    den = {ih: jnp.maximum(jnp.abs(den[ih]), jnp.exp(-m_t[ih])) for ih in pairs}
    open_gate = _sigmoid(opre)
    hc = {(i, h): num[(i, h)] / den[(i, h)] * open_gate[rs[i], s12[h]] for i, h in pairs}
    d0 = {ih: hc[ih] - _mean(hc[ih]) for ih in pairs}
    y = {ih: d0[ih] * lax.rsqrt(_mean(d0[ih] * d0[ih]) + EPS) for ih in pairs}
    skipped = p["skip"] * xc
    out_b = {(i, h): y[(i, h)] * p["gml"][:, s12[h]] + skipped[rs[i], s12[h]] for i, h in pairs}
    ab = jnp.concatenate([jnp.concatenate([out_a[(i, h)] for h in hs] + [out_b[(i, h)] for h in hs], axis=1) for i in cs],
                         axis=0)
    new = {"S": s_new, "C": [c_in[(n_ch, h)] for h in hs], "n": [n_in[(n_ch, h)] for h in hs],
           "m": [jnp.broadcast_to(m_in[(n_ch, h)], (1, ML_DH)) for h in hs]}
    return ab, new


_P_NAMES = ("wau", "bau", "ggla", "cw", "cb", "wq", "wk", "wv", "wif", "bif", "skip", "gml")
_P_SHAPES = {
    "wau": (128, 256), "bau": (1, 256), "ggla": (1, 128), "cw": (4, 512), "cb": (1, 512),
    "wq": (512, 128), "wk": (512, 128), "wv": (512, 128),
    "wif": (1536, 128), "bif": (1, 128), "skip": (1, 512), "gml": (1, 512),
}
_P_BLOCKDIAG = ("wq", "wk", "wv")
_S_NAMES = ("S", "C", "n", "m")
_S_SHAPES = {"S": (HEADS, GLA_DV, GLA_DK), "C": (HEADS, ML_DH, ML_DH), "n": (HEADS, 1, ML_DH), "m": (HEADS, 1, ML_DH)}


def _per_head(ref):
    return [ref[h] for h in range(HEADS)]


def _block_mask():
    r = lax.broadcasted_iota(jnp.int32, (128, 128), 0)
    c = lax.broadcasted_iota(jnp.int32, (128, 128), 1)
    same_block = (r >> 2) == (c >> 2)
    spread = jnp.logical_and(r < 4, (c & 3) == r)
    return same_block.astype(F32), spread.astype(F32)


def _expand_blockdiag(w_ref, dense_ref):
    same_block, spread = _block_mask()
    for h in range(HEADS):
        tiled = _pmm_nn(w_ref[h * 128:(h + 1) * 128, :], spread)
        dense_ref[h] = tiled * same_block


def _collect_blockdiag(ddense_ref, dw_ref):
    same_block, spread = _block_mask()
    for h in range(HEADS):
        dw_ref[h * 128:(h + 1) * 128, :] = lax.dot_general(
            ddense_ref[h] * same_block, spread, (((1,), (1,)), ((), ())), precision=lax.Precision.HIGHEST,
            preferred_element_type=F32)


def _const_spec(shape):
    zeros = (0,) * len(shape)
    return pl.BlockSpec(shape, lambda i: zeros)


def _split(refs, *counts):
    out, at = [], 0
    for c in counts:
        out.append(refs[at:at + c])
        at += c
    assert at == len(refs)
    return out


def _ride(rider, phases, cond, ins, outs, sems):
    if rider is None or not any(hasattr(rider, phase) for phase in phases):
        return
    lands, (send_sems, recv_sems, flush_sems) = sems[:-3], sems[-3:]

    @pl.when(cond)
    def _():
        for phase in phases:
            if phase == "last" and hasattr(rider, "late"):
                rider.late(ins, lands, send_sems, recv_sems)
                rider.flush("late", lands, outs, flush_sems)
            getattr(rider, phase)(ins, lands, send_sems, recv_sems)
            if hasattr(rider, "flush"):
                rider.flush(phase, lands, outs, flush_sems)
        if "last" in phases and not hasattr(rider, "flush"):
            flush = [pltpu.make_async_copy(lands[k], outs[k], flush_sems.at[k]) for k in range(len(outs))]
            for cp in flush:
                cp.start()
            for cp in flush:
                cp.wait()


def _middle_step(rider, n_steps):
    return min(n_steps - 2, int(getattr(rider, "middle_at", 1.0) * n_steps))


def _rider_specs(rider, rider_ins):
    if rider is None:
        return [], [], [], []
    scratch = [pltpu.VMEM(s.shape, s.dtype) for s in list(rider.out_shape) + list(getattr(rider, "work_shape", ()))]
    scratch += [pltpu.SemaphoreType.DMA((rider.n_sems,)), pltpu.SemaphoreType.DMA((rider.n_sems,)),
                pltpu.SemaphoreType.DMA((getattr(rider, "n_flush", len(rider.out_shape)),))]
    in_space = getattr(rider, "in_space", VMEM_WHOLE)
    in_specs = list(in_space) if isinstance(in_space, (list, tuple)) else [in_space] * len(rider_ins)
    return in_specs, [ANY] * len(rider.out_shape), list(rider.out_shape), scratch


def _mixer_fwd(pm, p, rider=None, rider_ins=()):
    n_p = len(_P_NAMES)
    r_in, r_out_specs, r_out_shape, r_sems = _rider_specs(rider, rider_ins)

    def body(*refs):
        (pm_ref, xprev_ref), p_list, ride_in, (ab_ref,), so_refs, ride_out, sc_refs, dense_list, sems = _split(
            refs, 2, n_p, len(r_in), 1, 4, len(r_out_specs), 4, 3, len(r_sems))
        p_refs = dict(zip(_P_NAMES, p_list))
        dense = dict(zip(_P_BLOCKDIAG, dense_list))
        n = pl.program_id(0)
        _ride(rider, ("first",), n == 0, ride_in, ride_out, sems)

        @pl.when(n == 0)
        def _():
            for r in sc_refs:
                r[...] = jnp.zeros_like(r)
            for nm in _P_BLOCKDIAG:
                _expand_blockdiag(p_refs[nm], dense[nm])

        st = {name: _per_head(r) for name, r in zip(_S_NAMES, sc_refs)}
        pv = {nm: (_per_head(dense[nm]) if nm in _P_BLOCKDIAG else p_refs[nm][...]) for nm in _P_NAMES}
        for name, r in zip(_S_NAMES, so_refs):
            for h in range(HEADS):
                r[0, h] = st[name][h]
        xprev8 = jnp.where(n > 0, xprev_ref[CHUNK - 8:CHUNK, :], 0.0)
        ab, st = _mixer_chunk(_PLAIN_OPS, pv, st, pm_ref[...], xprev8)
        ab_ref[...] = ab.astype(BF16)
        for name, r in zip(_S_NAMES, sc_refs):
            for h in range(HEADS):
                r[h] = st[name][h]
        _ride(rider, ("middle",), n == _middle_step(rider, N_SWEEP), ride_in, ride_out, sems)
        _ride(rider, ("last",), n == N_SWEEP - 1, ride_in, ride_out, sems)

    in_specs = [pl.BlockSpec((SWEEP * CHUNK, PM_W), lambda i: (i, 0)),
                pl.BlockSpec((CHUNK, 512), lambda i: (jnp.maximum(SWEEP * i - 1, 0), PM_XM // 512))]
    in_specs += [_const_spec(_P_SHAPES[nm]) for nm in _P_NAMES] + r_in
    out_specs = [pl.BlockSpec((SWEEP * CHUNK, 1024), lambda i: (i, 0))]
    out_shape = [jax.ShapeDtypeStruct((SEQ, 1024), BF16)]
    for nm in _S_NAMES:
        shp = _S_SHAPES[nm]
        out_specs.append(pl.BlockSpec((1,) + shp, lambda i: (i, 0, 0, 0)))
        out_shape.append(jax.ShapeDtypeStruct((N_SWEEP,) + shp, F32))
    return pl.pallas_call(
        body, grid=(N_SWEEP,), in_specs=in_specs, out_specs=out_specs + r_out_specs, out_shape=out_shape + r_out_shape,
        scratch_shapes=[pltpu.VMEM(_S_SHAPES[nm], F32) for nm in _S_NAMES]
        + [pltpu.VMEM((HEADS, 128, 128), F32) for _ in _P_BLOCKDIAG] + r_sems,
        compiler_params=_params(("arbitrary",)), name="mixer_fwd",
    )(pm, pm, *[p[nm] for nm in _P_NAMES], *rider_ins)


def _mixer_bwd(pm, dab, states, p, rider=None, rider_ins=()):
    n_p = len(_P_NAMES)
    r_in, r_out_specs, r_out_shape, r_sems = _rider_specs(rider, rider_ins)

    def body(*refs):
        ((pm_ref, xprev_ref, dab_ref), si_refs, p_list, ride_in, (dpm_ref,), dp_list, ride_out, ds_refs, (carry_ref,),
         dense_list, ddense_list, sems) = _split(refs, 3, 4, n_p, len(r_in), 1, n_p, len(r_out_specs), 4, 1, 3, 3, len(r_sems))
        p_refs = dict(zip(_P_NAMES, p_list))
        dp_refs = dict(zip(_P_NAMES, dp_list))
        dense = dict(zip(_P_BLOCKDIAG, dense_list))
        ddense = dict(zip(_P_BLOCKDIAG, ddense_list))
        i = pl.program_id(0)
        blk = N_SWEEP - 1 - i
        _ride(rider, ("first",), i == 0, ride_in, ride_out, sems)

        @pl.when(i == 0)
        def _():
            for r in ds_refs:
                r[...] = jnp.zeros_like(r)
            for nm in _P_NAMES:
                if nm in _P_BLOCKDIAG:
                    ddense[nm][...] = jnp.zeros_like(ddense[nm])
                    _expand_blockdiag(p_refs[nm], dense[nm])
                else:
                    dp_refs[nm][...] = jnp.zeros_like(dp_refs[nm])
            carry_ref[...] = jnp.zeros_like(carry_ref)

        pv = {nm: (_per_head(dense[nm]) if nm in _P_BLOCKDIAG else p_refs[nm][...]) for nm in _P_NAMES}
        dst = {name: _per_head(r) for name, r in zip(_S_NAMES, ds_refs)}
        st = {name: [r[0, h] for h in range(HEADS)] for name, r in zip(_S_NAMES, si_refs)}
        xprev8 = jnp.where(blk > 0, xprev_ref[CHUNK - 8:CHUNK, :], 0.0)
        _, vjp = jax.vjp(functools.partial(_mixer_chunk, _VJP_OPS), pv, st, pm_ref[...], xprev8)
        dp_sum, dst, dpm, dxprev8 = vjp((dab_ref[...], dst))
        reach = jnp.concatenate([jnp.zeros((SWEEP * CHUNK - 8, 512), F32), carry_ref[...]], axis=0)
        dpm_ref[:, 0:PM_XM] = dpm[:, 0:PM_XM].astype(BF16)
        dpm_ref[:, PM_XM:PM_XM + 512] = (dpm[:, PM_XM:PM_XM + 512] + reach).astype(BF16)
        dpm_ref[:, PM_XM + 512:PM_W] = dpm[:, PM_XM + 512:PM_W].astype(BF16)
        carry_ref[...] = dxprev8
        for name, r in zip(_S_NAMES, ds_refs):
            for h in range(HEADS):
                r[h] = dst[name][h]
        for nm in _P_NAMES:
            if nm in _P_BLOCKDIAG:
                for h in range(HEADS):
                    ddense[nm][h] += dp_sum[nm][h]
            else:
                dp_refs[nm][...] += dp_sum[nm]

        @pl.when(i == N_SWEEP - 1)
        def _():
            for nm in _P_BLOCKDIAG:
                _collect_blockdiag(ddense[nm], dp_refs[nm])

        _ride(rider, ("early",), i == 1, ride_in, ride_out, sems)
        _ride(rider, ("middle",), i == _middle_step(rider, N_SWEEP), ride_in, ride_out, sems)
        _ride(rider, ("last",), i == N_SWEEP - 1, ride_in, ride_out, sems)

    rev = lambda i: (N_SWEEP - 1 - i, 0)
    in_specs = [pl.BlockSpec((SWEEP * CHUNK, PM_W), rev),
                pl.BlockSpec((CHUNK, 512), lambda i: (jnp.maximum(SWEEP * (N_SWEEP - 1 - i) - 1, 0), PM_XM // 512)),
                pl.BlockSpec((SWEEP * CHUNK, 1024), rev)]
    for nm in _S_NAMES:
        in_specs.append(pl.BlockSpec((1,) + _S_SHAPES[nm], lambda i: (N_SWEEP - 1 - i, 0, 0, 0)))
    in_specs += [_const_spec(_P_SHAPES[nm]) for nm in _P_NAMES] + r_in
    out_specs = [pl.BlockSpec((SWEEP * CHUNK, PM_W), rev)] + [_const_spec(_P_SHAPES[nm]) for nm in _P_NAMES]
    out_shape = [jax.ShapeDtypeStruct((SEQ, PM_W), BF16)] + [jax.ShapeDtypeStruct(_P_SHAPES[nm], F32) for nm in _P_NAMES]
    res = pl.pallas_call(
        body, grid=(N_SWEEP,), in_specs=in_specs, out_specs=out_specs + r_out_specs, out_shape=out_shape + r_out_shape,
        scratch_shapes=[pltpu.VMEM(_S_SHAPES[nm], F32) for nm in _S_NAMES] + [pltpu.VMEM((8, 512), F32)]
        + [pltpu.VMEM((HEADS, 128, 128), F32) for _ in range(2 * len(_P_BLOCKDIAG))] + r_sems,
        compiler_params=_params(("arbitrary",)), name="mixer_bwd",
    )(pm, pm, dab, *states, *[p[nm] for nm in _P_NAMES], *rider_ins)
    return res[0], dict(zip(_P_NAMES, res[1:1 + n_p])), res[1 + n_p:]


def _tok(width):
    return pl.BlockSpec((TOK_TILE, width), lambda i: (i, 0))


def _once(shape):
    zeros = (0,) * len(shape)
    return pl.BlockSpec(shape, lambda i: zeros, pipeline_mode=pl.Buffered(1))


def _rms_fwd(x):
    r = lax.rsqrt(_mean(x * x) + EPS)
    return x * r, r


def _rms_bwd(dy, xn, r, g):
    gd = dy * g
    return r * (gd - xn * _mean(xn * gd))


def _tiled_call(body, in_specs, out_specs, out_shape, args, name, rider=None, rider_ins=(), scratch=()):
    r_in, r_out_specs, r_out_shape, r_scratch = _rider_specs(rider, rider_ins)
    n_in, n_out = len(in_specs), len(out_specs)

    def hosted(*refs):
        ins, ride_in, outs, ride_out, own, r_scr = _split(refs, n_in, len(r_in), n_out, len(r_out_specs), len(scratch),
                                                          len(r_scratch))
        i = pl.program_id(0)
        _ride(rider, ("first",), i == 0, ride_in, ride_out, r_scr)
        body(*ins, *outs, *own)
        _ride(rider, ("early",), i == 1, ride_in, ride_out, r_scr)
        _ride(rider, ("middle",), i == _middle_step(rider, N_TOK_TILE), ride_in, ride_out, r_scr)
        _ride(rider, ("last",), i == N_TOK_TILE - 1, ride_in, ride_out, r_scr)

    res = pl.pallas_call(
        hosted, grid=(N_TOK_TILE,), in_specs=list(in_specs) + r_in, out_specs=list(out_specs) + r_out_specs,
        out_shape=list(out_shape) + r_out_shape, scratch_shapes=list(scratch) + r_scratch,
        compiler_params=_params(("arbitrary",)), name=name,
    )(*args, *rider_ins)
    return res[:n_out], res[n_out:]


def _in_proj(x, g_pre, wt_in, rider=None, rider_ins=()):
    def body(x_ref, g_ref, wt_ref, pm_ref, gab_ref, h_ref):
        xn, _ = _rms_fwd(x_ref[...])
        h = (xn * g_ref[...]).astype(BF16)
        h_ref[...] = h
        pm_ref[:, 0:PM_XM] = _nt(h, wt_ref[0:IN_ALOW, :])
        pm_ref[:, PM_XM:PM_AL] = _nt(h, wt_ref[IN_XM:IN_GATES, :])
        pm_ref[:, PM_AL:PM_W] = _nt(h, wt_ref[IN_ALOW:IN_ALOW + 128, :])
        gab_ref[...] = _nt(h, wt_ref[IN_GATES:D_IN, :])

    return _tiled_call(
        body, [_tok(D_MODEL), _once((1, D_MODEL)), _once((D_IN, D_MODEL))], [_tok(PM_W), _tok(GAB_W), _tok(D_MODEL)],
        [jax.ShapeDtypeStruct((SEQ, PM_W), F32), jax.ShapeDtypeStruct((SEQ, GAB_W), F32),
         jax.ShapeDtypeStruct((SEQ, D_MODEL), BF16)], (x, g_pre, wt_in), "in_proj", rider, rider_ins)


def _merge_fwd(ab, gab, x, w_pa4, w_pb4, w_o, g_post, rider=None, rider_ins=()):
    def body(ab_ref, gab_ref, x_ref, wpa_ref, wpb_ref, wo_ref, g_ref, x1_ref, mix_ref, mg_ref):
        a = ab_ref[:, 0:512]
        b = ab_ref[:, 512:1024]
        for j in range(N_CHIP):
            blk = slice(j * 256, (j + 1) * 256)
            ya = jnp.dot(a, wpa_ref[j], preferred_element_type=F32)
            yb = jnp.dot(b, wpb_ref[j], preferred_element_type=F32)
            sa = _sigmoid(gab_ref[:, j * 256:(j + 1) * 256])
            sb = _sigmoid(gab_ref[:, 1024 + j * 256:1024 + (j + 1) * 256])
            mg_ref[:, blk] = (sa * ya + sb * yb).astype(BF16)
        mix = jnp.dot(mg_ref[...], wo_ref[...], preferred_element_type=F32)
        mix_ref[...] = mix
        mn, _ = _rms_fwd(mix)
        x1_ref[...] = x_ref[...] + mn * g_ref[...]

    return _tiled_call(
        body, [_tok(1024), _tok(GAB_W), _tok(D_MODEL), _once((N_CHIP, 512, 256)), _once((N_CHIP, 512, 256)),
               _once((D_MODEL, D_MODEL)), _once((1, D_MODEL))], [_tok(D_MODEL), _tok(D_MODEL), _tok(D_MODEL)],
        [jax.ShapeDtypeStruct((SEQ, D_MODEL), F32), jax.ShapeDtypeStruct((SEQ, D_MODEL), F32),
         jax.ShapeDtypeStruct((SEQ, D_MODEL), BF16)], (ab, gab, x, w_pa4, w_pb4, w_o, g_post), "merge_fwd", rider, rider_ins)


def _mlp(x1, target, g_pre, g_post, w_up4, w_down_a4, w_down_b4):
    def body(x1_ref, t_ref, gpre_ref, gpost_ref, wup_ref, wda_ref, wdb_ref,
             dx1_ref, u_ref, dd_ref, h2_ref, dpre_ref, dgpost_ref, dgpre_ref, loss_ref):
        @pl.when(pl.program_id(0) == 0)
        def _():
            dgpost_ref[...] = jnp.zeros_like(dgpost_ref)
            dgpre_ref[...] = jnp.zeros_like(dgpre_ref)
            loss_ref[...] = jnp.zeros_like(loss_ref)

        x1 = x1_ref[...]
        gpre = gpre_ref[...]
        gpost = gpost_ref[...]
        xn2, r2 = _rms_fwd(x1)
        h2 = (xn2 * gpre).astype(BF16)
        h2_ref[...] = h2
        rl = []
        d = jnp.zeros((TOK_TILE, D_MODEL), F32)
        for j in range(N_CHIP):
            blk = slice(j * 1024, (j + 1) * 1024)
            r = jnp.maximum(jnp.dot(h2, wup_ref[j], preferred_element_type=F32), 0.0)
            rl.append(r)
            u = (r * r).astype(BF16)
            u_ref[:, blk] = u
            d = d + jnp.dot(u[:, 0:512], wda_ref[j], preferred_element_type=F32)
            d = d + jnp.dot(u[:, 512:1024], wdb_ref[j], preferred_element_type=F32)
        dn, r3 = _rms_fwd(d)
        diff = x1 + dn * gpost - t_ref[...]
        loss_ref[...] += jnp.sum(diff * diff, keepdims=True) * (0.5 / D_MODEL)
        dy = diff * (1.0 / D_MODEL)
        dgpost_ref[...] += jnp.sum(dy * dn, axis=0, keepdims=True)
        dd = _rms_bwd(dy, dn, r3, gpost).astype(BF16)
        dd_ref[...] = dd
        dh2 = jnp.zeros((TOK_TILE, D_MODEL), F32)
        for j in range(N_CHIP):
            blk = slice(j * 1024, (j + 1) * 1024)
            du = jnp.concatenate([_nt(dd, wda_ref[j]), _nt(dd, wdb_ref[j])], axis=1)
            dpre = (du * (2.0 * rl[j])).astype(BF16)
            dpre_ref[:, blk] = dpre
            dh2 = dh2 + _nt(dpre, wup_ref[j])
        dgpre_ref[...] += jnp.sum(dh2 * xn2, axis=0, keepdims=True)
        dx1_ref[...] = dy + _rms_bwd(dh2, xn2, r2, gpre)

    acc = pl.BlockSpec((1, D_MODEL), lambda i: (0, 0))
    return pl.pallas_call(
        body, grid=(N_TOK_TILE,),
        in_specs=[_tok(D_MODEL), _tok(D_MODEL), _once((1, D_MODEL)), _once((1, D_MODEL)),
                  _once((N_CHIP, D_MODEL, 1024)), _once((N_CHIP, 512, D_MODEL)), _once((N_CHIP, 512, D_MODEL))],
        out_specs=[_tok(D_MODEL), _tok(D_FF), _tok(D_MODEL), _tok(D_MODEL), _tok(D_FF), acc, acc,
                   pl.BlockSpec((1, 128), lambda i: (0, 0))],
        out_shape=[jax.ShapeDtypeStruct((SEQ, D_MODEL), F32), jax.ShapeDtypeStruct((SEQ, D_FF), BF16),
                   jax.ShapeDtypeStruct((SEQ, D_MODEL), BF16), jax.ShapeDtypeStruct((SEQ, D_MODEL), BF16),
                   jax.ShapeDtypeStruct((SEQ, D_FF), BF16), jax.ShapeDtypeStruct((1, D_MODEL), F32),
                   jax.ShapeDtypeStruct((1, D_MODEL), F32), jax.ShapeDtypeStruct((1, 128), F32)],
        compiler_params=_params(("arbitrary",)), name="mlp_fwd_bwd",
    )(x1, target, g_pre, g_post, w_up4, w_down_a4, w_down_b4)


def _merge_bwd(dx1, mix, ab, gab, merged, w_pa4, w_pb4, w_o, g_post, rider=None, rider_ins=()):
    def body(dx1_ref, mix_ref, ab_ref, gab_ref, mg_ref, wpa_ref, wpb_ref, wo_ref, g_ref,
             dgab_ref, dab_ref, dg_ref, dwpa_ref, dwpb_ref, dwo_ref, acc_pa, acc_pb, acc_o):
        @pl.when(pl.program_id(0) == 0)
        def _():
            dg_ref[...] = jnp.zeros_like(dg_ref)
            acc_pa[...] = jnp.zeros_like(acc_pa)
            acc_pb[...] = jnp.zeros_like(acc_pb)
            acc_o[...] = jnp.zeros_like(acc_o)

        dx1 = dx1_ref[...]
        mn, r = _rms_fwd(mix_ref[...])
        dg_ref[...] += jnp.sum(dx1 * mn, axis=0, keepdims=True)
        dmix = _rms_bwd(dx1, mn, r, g_ref[...]).astype(BF16)
        acc_o[...] += _tn(mg_ref[...], dmix)
        dmerged = _nt(dmix, wo_ref[...])
        a = ab_ref[:, 0:512]
        b = ab_ref[:, 512:1024]
        da = jnp.zeros((TOK_TILE, 512), F32)
        db = jnp.zeros((TOK_TILE, 512), F32)
        dyas, dybs = [], []
        for j in range(N_CHIP):
            blk = slice(j * 256, (j + 1) * 256)
            blk_b = slice(1024 + j * 256, 1024 + (j + 1) * 256)
            dm = dmerged[:, blk]
            ya = jnp.dot(a, wpa_ref[j], preferred_element_type=F32)
            yb = jnp.dot(b, wpb_ref[j], preferred_element_type=F32)
            sa = _sigmoid(gab_ref[:, blk])
            sb = _sigmoid(gab_ref[:, blk_b])
            dya = (dm * sa).astype(BF16)
            dyb = (dm * sb).astype(BF16)
            dyas.append(dya)
            dybs.append(dyb)
            dgab_ref[:, blk] = (dm * ya * sa * (1.0 - sa)).astype(BF16)
            dgab_ref[:, blk_b] = (dm * yb * sb * (1.0 - sb)).astype(BF16)
            da = da + _nt(dya, wpa_ref[j])
            db = db + _nt(dyb, wpb_ref[j])
        dab_ref[:, 0:512] = da
        dab_ref[:, 512:1024] = db
        acc_pa[...] += _tn(a, jnp.concatenate(dyas, axis=1))
        acc_pb[...] += _tn(b, jnp.concatenate(dybs, axis=1))

        @pl.when(pl.program_id(0) == N_TOK_TILE - 1)
        def _():
            dwo_ref[...] = acc_o[...].astype(BF16)
            for j in range(N_CHIP):
                dwpa_ref[j] = acc_pa[:, j * 256:(j + 1) * 256].astype(BF16)
                dwpb_ref[j] = acc_pb[:, j * 256:(j + 1) * 256].astype(BF16)

    whole = lambda shape: pl.BlockSpec(shape, lambda i: (0,) * len(shape))
    return _tiled_call(
        body,
        [_tok(D_MODEL), _tok(D_MODEL), _tok(1024), _tok(GAB_W), _tok(D_MODEL), _once((N_CHIP, 512, 256)),
         _once((N_CHIP, 512, 256)), _once((D_MODEL, D_MODEL)), _once((1, D_MODEL))],
        [_tok(GAB_W), _tok(1024), whole((1, D_MODEL)), whole((N_CHIP, 512, 256)), whole((N_CHIP, 512, 256)),
         whole((D_MODEL, D_MODEL))],
        [jax.ShapeDtypeStruct((SEQ, GAB_W), BF16), jax.ShapeDtypeStruct((SEQ, 1024), F32),
         jax.ShapeDtypeStruct((1, D_MODEL), F32), jax.ShapeDtypeStruct((N_CHIP, 512, 256), BF16),
         jax.ShapeDtypeStruct((N_CHIP, 512, 256), BF16), jax.ShapeDtypeStruct((D_MODEL, D_MODEL), BF16)],
        (dx1, mix, ab, gab, merged, w_pa4, w_pb4, w_o, g_post), "merge_bwd", rider, rider_ins,
        scratch=[pltpu.VMEM((512, D_MODEL), F32), pltpu.VMEM((512, D_MODEL), F32), pltpu.VMEM((D_MODEL, D_MODEL), F32)])


def _in_proj_bwd(dpm, dgab, x, dx1, g_pre, wt_in, rider=None, rider_ins=()):
    def body(dpm_ref, dgab_ref, x_ref, dx1_ref, g_ref, wt_ref, dx_ref, dg_ref):
        @pl.when(pl.program_id(0) == 0)
        def _():
            dg_ref[...] = jnp.zeros_like(dg_ref)

        dh = jnp.dot(dpm_ref[:, 0:PM_XM], wt_ref[0:IN_ALOW, :], preferred_element_type=F32)
        dh = dh + jnp.dot(dpm_ref[:, PM_XM:PM_AL], wt_ref[IN_XM:IN_GATES, :], preferred_element_type=F32)
        dh = dh + jnp.dot(dpm_ref[:, PM_AL:PM_W], wt_ref[IN_ALOW:IN_ALOW + 128, :], preferred_element_type=F32)
        dh = dh + jnp.dot(dgab_ref[...], wt_ref[IN_GATES:D_IN, :], preferred_element_type=F32)
        xn, r = _rms_fwd(x_ref[...])
        dg_ref[...] += jnp.sum(dh * xn, axis=0, keepdims=True)
        dx_ref[...] = dx1_ref[...] + _rms_bwd(dh, xn, r, g_ref[...])

    return _tiled_call(
        body, [_tok(PM_W), _tok(GAB_W), _tok(D_MODEL), _tok(D_MODEL), _once((1, D_MODEL)), _once((D_IN, D_MODEL))],
        [_tok(D_MODEL), pl.BlockSpec((1, D_MODEL), lambda i: (0, 0))],
        [jax.ShapeDtypeStruct((SEQ, D_MODEL), F32), jax.ShapeDtypeStruct((1, D_MODEL), F32)],
        (dpm, dgab, x, dx1, g_pre, wt_in), "in_proj_bwd", rider, rider_ins)


def _dw_in(dpm, dgab, h):
    n_pm = PM_AL // 512
    n_blk = n_pm + GAB_W // 512

    def body(dpm_ref, dgab_ref, dal_ref, h_ref, o_ref):
        i = pl.program_id(0)
        off = pl.multiple_of(i * 512 + 16 * (i >= 3).astype(jnp.int32), 16)

        @pl.when(i < n_pm)
        def _():
            o_ref[pl.ds(off, 512), :] = _tn(dpm_ref[...], h_ref[...]).astype(BF16)

        @pl.when(i >= n_pm)
        def _():
            o_ref[pl.ds(off, 512), :] = _tn(dgab_ref[...], h_ref[...]).astype(BF16)

        @pl.when(i == 0)
        def _():
            o_ref[IN_ALOW:IN_XM, :] = _tn(dal_ref[...], h_ref[...])[0:IN_XM - IN_ALOW].astype(BF16)

    return pl.pallas_call(
        body, grid=(n_blk,),
        in_specs=[pl.BlockSpec((SEQ, 512), lambda i: (0, jnp.minimum(i, n_pm - 1))),
                  pl.BlockSpec((SEQ, 512), lambda i: (0, jnp.maximum(i - n_pm, 0))),
                  pl.BlockSpec((SEQ, 128), lambda i: (0, PM_AL // 128)),
                  _once((SEQ, D_MODEL))],
        out_specs=pl.BlockSpec((D_IN, D_MODEL), lambda i: (0, 0)),
        out_shape=jax.ShapeDtypeStruct((D_IN, D_MODEL), BF16),
        compiler_params=_params(("arbitrary",)), name="dw_in",
    )(dpm, dgab, dpm, h)


def _tn_matmul(a, b, name, shards=1, tm=1024, rider=None, rider_ins=()):
    m, n = a.shape[1], b.shape[1]
    tm = min(tm, m)
    tn = n // shards if shards > 1 else min(n, 1024)
    steps_i, steps_j = m // tm, n // tn
    r_in, r_out_specs, r_out_shape, r_scratch = _rider_specs(rider, rider_ins)

    def body(*refs):
        (a_ref, b_ref), ride_in, (o_ref,), ride_out, scratch = _split(refs, 2, len(r_in), 1, len(r_out_specs), len(r_scratch))
        step = pl.program_id(0) * steps_j + pl.program_id(1)
        _ride(rider, ("first",), step == 0, ride_in, ride_out, scratch)
        o_ref[...] = _tn(a_ref[...], b_ref[...]).astype(BF16)
        _ride(rider, ("middle", "last"), step == steps_i * steps_j - 1, ride_in, ride_out, scratch)

    if shards > 1:
        out_spec = pl.BlockSpec((None, tm, tn), lambda i, j: (j, i, 0))
        out_shape = jax.ShapeDtypeStruct((shards, m, tn), BF16)
    else:
        out_spec = pl.BlockSpec((tm, tn), lambda i, j: (i, j))
        out_shape = jax.ShapeDtypeStruct((m, n), BF16)
    res = pl.pallas_call(
        body, grid=(steps_i, steps_j),
        in_specs=[pl.BlockSpec((SEQ, tm), lambda i, j: (0, i)), pl.BlockSpec((SEQ, tn), lambda i, j: (0, j))] + r_in,
        out_specs=[out_spec] + r_out_specs, out_shape=[out_shape] + r_out_shape, scratch_shapes=r_scratch,
        compiler_params=_params(("arbitrary", "arbitrary")), name=name,
    )(a, b, *rider_ins)
    return res[0] if rider is None else (res[0], res[1:])


MESH = pl.DeviceIdType.MESH
ANY = pl.BlockSpec(memory_space=pl.ANY)
VMEM_WHOLE = pl.BlockSpec(memory_space=pltpu.VMEM)

_BIG = ("w_in", "w_pa", "w_pb", "w_o", "w_up", "w_down")
_BIG_SHARD = {"w_in": (IN_SHARD, D_MODEL), "w_pa": (512, 256), "w_pb": (512, 256), "w_o": (256, D_MODEL),
              "w_up": (D_MODEL, 1024), "w_down": (1024, D_MODEL),
              "w_down_a": (512, D_MODEL), "w_down_b": (512, D_MODEL)}
_BIG_SPLIT = {"w_in": 1, "w_pa": 0, "w_pb": 0, "w_o": 0, "w_up": 0, "w_down": 0, "w_down_a": 0, "w_down_b": 0}


def _half(ref, e, name, lead=0, part=None):
    axis = _BIG_SPLIT[name]
    size = _BIG_SHARD[name][axis] // 2
    start = e * size
    if part is not None:
        size //= 2
        start = start + part * size
    start = pl.multiple_of(start, 128 if axis == 1 else 16)
    idx = [pl.ds(0, ref.shape[a]) for a in range(lead)]
    idx += [pl.ds(start, size), pl.ds(0, _BIG_SHARD[name][1])] if axis == 0 else [pl.ds(0, _BIG_SHARD[name][0]), pl.ds(start, size)]
    return ref.at[tuple(idx)]


def _half_shape(name):
    r, c = _BIG_SHARD[name]
    return (r // 2, c) if _BIG_SPLIT[name] == 0 else (r, c // 2)


def _remote(src, dst, send_sems, recv_sems, k, to):
    return pltpu.make_async_remote_copy(src_ref=src, dst_ref=dst, send_sem=send_sems.at[k], recv_sem=recv_sems.at[k],
                                        device_id=to, device_id_type=MESH)


def _mesh_place():
    x, y, c = lax.axis_index("x"), lax.axis_index("y"), lax.axis_index("c")
    return x, y, c, [(1 - x, y), (x, 1 - y), (1 - x, 1 - y)]


class _Gather:
    def __init__(self, names, small=(), middle_at=0.5):
        self.middle_at = middle_at
        self.names = tuple(names)
        self.nb = len(self.names)
        self.n = self.nb + len(small)
        self.n_sems = 8 * self.nb + 3 * len(small)
        self.n_flush = 6 * self.nb + len(small)
        self.out_shape = [jax.ShapeDtypeStruct((N_CHIP,) + _BIG_SHARD[nm], BF16) for nm in self.names]
        self.out_shape += [jax.ShapeDtypeStruct((N_CHIP,) + s.shape, s.dtype) for s in small]

    def _copies(self, ins, outs, ss, rs, k):
        x, y, c, _ = _mesh_place()
        name = self.names[k]
        me, xn, yn, dg = 2 * x + y, 2 * (1 - x) + y, 2 * x + (1 - y), 2 * (1 - x) + (1 - y)
        to_x, to_y, sibling = (1 - x, y, c), (x, 1 - y, c), (x, y, 1 - c)

        def region(slot, e, part=None):
            return _half(outs[k].at[slot], e, name, part=part)

        def copy(pair, src, dst, to):
            return _remote(src, dst, ss, rs, 8 * k + pair, to)

        mine = _half(ins[k], c, name)
        sent = [copy(0, mine, region(me, c), to_x), copy(1, mine, region(me, c), to_y),
                copy(2, region(xn, c, 0), region(xn, c, 0), to_y), copy(3, region(yn, c, 1), region(yn, c, 1), to_x),
                copy(4, region(xn, c), region(xn, c), sibling), copy(5, region(yn, c), region(yn, c), sibling),
                copy(6, region(dg, c, 0), region(dg, c, 0), sibling), copy(7, region(dg, c, 1), region(dg, c, 1), sibling)]
        landing = [region(xn, c), region(yn, c), region(dg, c, 0), region(dg, c, 1),
                   region(xn, 1 - c), region(yn, 1 - c), region(dg, 1 - c, 0), region(dg, 1 - c, 1)]
        received = [copy(pair, dst, dst, sibling) for pair, dst in enumerate(landing)]
        return sent, received

    def _small(self, ins, outs, ss, rs, k, j, peer, slot, c):
        return _remote(ins[k], outs[k].at[slot], ss, rs, 8 * self.nb + 3 * (k - self.nb) + j, (*peer, c))

    def flush(self, phase, lands, outs, fs):
        x, y, c, _ = _mesh_place()
        me, xn, yn, dg = 2 * x + y, 2 * (1 - x) + y, 2 * x + (1 - y), 2 * (1 - x) + (1 - y)

        def pieces(k):
            name = self.names[k]
            spots = [lambda r: r.at[me], lambda r: _half(r.at[xn], c, name), lambda r: _half(r.at[yn], c, name),
                     lambda r: _half(r.at[xn], 1 - c, name), lambda r: _half(r.at[yn], 1 - c, name), lambda r: r.at[dg]]
            return [pltpu.make_async_copy(spot(lands[k]), spot(outs[k]), fs.at[6 * k + t]) for t, spot in enumerate(spots)]

        ready = {"first": (0,), "middle": (1, 2), "late": (3, 4), "last": (5,)}[phase]
        for k in range(self.nb):
            cps = pieces(k)
            for t in ready:
                cps[t].start()
        if phase == "last":
            small = [pltpu.make_async_copy(lands[k], outs[k], fs.at[6 * self.nb + k - self.nb]) for k in range(self.nb, self.n)]
            for cp in small:
                cp.start()
            for k in range(self.nb):
                for cp in pieces(k):
                    cp.wait()
            for cp in small:
                cp.wait()

    def first(self, ins, outs, ss, rs):
        x, y, c, peers = _mesh_place()
        me = 2 * x + y
        for k in range(self.nb):
            sent, _ = self._copies(ins, outs, ss, rs, k)
            sent[0].start()
            sent[1].start()
        for k in range(self.nb, self.n):
            for j, peer in enumerate(peers):
                self._small(ins, outs, ss, rs, k, j, peer, me, c).start()
        for k in range(self.n):
            outs[k][me] = ins[k][...]

    def middle(self, ins, outs, ss, rs):
        for k in range(self.nb):
            sent, received = self._copies(ins, outs, ss, rs, k)
            for pair in (0, 1):
                received[pair].wait_recv()
                sent[2 + pair].start()
                sent[4 + pair].start()

    def late(self, ins, outs, ss, rs):
        for k in range(self.nb):
            _, received = self._copies(ins, outs, ss, rs, k)
            for pair in (4, 5):
                received[pair].wait_recv()

    def last(self, ins, outs, ss, rs):
        x, y, c, peers = _mesh_place()
        for k in range(self.nb):
            sent, received = self._copies(ins, outs, ss, rs, k)
            for pair in (2, 3):
                received[pair].wait_recv()
                sent[4 + pair].start()
        for k in range(self.nb):
            sent, received = self._copies(ins, outs, ss, rs, k)
            for pair in (6, 7):
                received[pair].wait_recv()
            for cp in sent:
                cp.wait_send()
        for k in range(self.nb, self.n):
            for j, (px, py) in enumerate(peers):
                self._small(ins, outs, ss, rs, k, j, (px, py), 2 * px + py, c).wait_recv()
                self._small(ins, outs, ss, rs, k, j, (px, py), 2 * x + y, c).wait_send()


def _run_alone(rider, ins, name):
    r_in, r_out_specs, r_out_shape, r_scratch = _rider_specs(rider, ins)

    def body(*refs):
        ride_in, ride_out, scratch = _split(refs, len(r_in), len(r_out_specs), len(r_scratch))
        _ride(rider, ("first", "middle", "last"), pl.program_id(0) == 0, ride_in, ride_out, scratch)

    return pl.pallas_call(
        body, grid=(1,), in_specs=r_in, out_specs=r_out_specs, out_shape=r_out_shape, scratch_shapes=r_scratch,
        compiler_params=_params(("arbitrary",)), name=name,
    )(*ins)


class _Presum:
    in_space = ANY

    def __init__(self, names, base=0):
        self.names = tuple(names)
        self.n = len(self.names)
        self.base = base
        self.n_sems = 3 * self.n
        self.out_shape = [jax.ShapeDtypeStruct((N_CHIP,) + _half_shape(nm), BF16) for nm in self.names]
        self.work_shape = self.out_shape + self.out_shape

    def _stage(self, ins, bufs, ss, k, e, which):
        n = self.n
        return pltpu.make_async_copy(_half(ins[k], e, self.names[k], lead=1), bufs[which * n + k],
                                     ss.at[self.base + which * n + k])

    def _give(self, bufs, ss, rs, k, sibling):
        return _remote(bufs[self.n + k], bufs[k], ss, rs, self.base + k, sibling)

    def first(self, ins, bufs, ss, rs):
        x, y, c, _ = _mesh_place()
        for k in range(self.n):
            self._stage(ins, bufs, ss, k, 1 - c, 1).start()
        for k in range(self.n):
            self._stage(ins, bufs, ss, k, c, 2).start()
        for k in range(self.n):
            self._stage(ins, bufs, ss, k, 1 - c, 1).wait()
            self._give(bufs, ss, rs, k, (x, y, 1 - c)).start()

    def middle(self, ins, bufs, ss, rs):
        pass

    def last(self, ins, bufs, ss, rs):
        x, y, c, _ = _mesh_place()
        for k in range(self.n):
            self._give(bufs, ss, rs, k, (x, y, 1 - c)).wait_recv()
            self._stage(ins, bufs, ss, k, c, 2).wait()

            @pl.loop(0, N_CHIP)
            def _(j):
                bufs[k][j] = (bufs[k][j].astype(F32) + bufs[2 * self.n + k][j].astype(F32)).astype(BF16)
        for k in range(self.n):
            self._give(bufs, ss, rs, k, (x, y, 1 - c)).wait_send()


class _ReduceRelay:
    middle_at = 0.75

    def __init__(self, names, base=0):
        self.names = tuple(names)
        self.n = len(self.names)
        self.base = base
        self.n_sems = 6 * self.n
        self.out_shape = [jax.ShapeDtypeStruct((N_CHIP,) + _half_shape(nm), BF16) for nm in self.names]
        quarter = [jax.ShapeDtypeStruct(self._part_shape(nm), BF16) for nm in self.names]
        self.work_shape = quarter + quarter

    @staticmethod
    def _part_shape(name):
        r, c = _half_shape(name)
        return (r // 2, c) if _BIG_SPLIT[name] == 0 else (r, c // 2)

    def _part(self, ref, name, p):
        r, c = self._part_shape(name)
        return ref.at[pl.ds(p * r, r), pl.ds(0, c)] if _BIG_SPLIT[name] == 0 else ref.at[pl.ds(0, r), pl.ds(p * c, c)]

    def _copies(self, ins, bufs, ss, rs, k):
        x, y, c, _ = _mesh_place()
        name, n = self.names[k], self.n
        me, xn, yn, dg = 2 * x + y, 2 * (1 - x) + y, 2 * x + (1 - y), 2 * (1 - x) + (1 - y)
        to_x, to_y = (1 - x, y, c), (x, 1 - y, c)
        mine = lambda slot, p: self._part(ins[k].at[slot], name, p)
        slot = lambda s, p: self._part(bufs[k].at[s], name, p)
        from_x, from_y = bufs[n + k], bufs[2 * n + k]

        def copy(pair, src, dst, to):
            return _remote(src, dst, ss, rs, self.base + 6 * k + pair, to)

        sent = [copy(0, mine(dg, 0), from_x, to_x), copy(1, mine(dg, 1), from_y, to_y),
                copy(2, mine(xn, 0), slot(me, 0), to_x), copy(3, mine(yn, 1), slot(me, 1), to_y),
                copy(4, from_y, slot(me, 1), to_x), copy(5, from_x, slot(me, 0), to_y)]
        landing = [from_x, from_y, slot(xn, 0), slot(yn, 1), slot(xn, 1), slot(yn, 0)]
        received = [copy(pair, dst, dst, to_x) for pair, dst in enumerate(landing)]
        return sent, received

    def first(self, ins, bufs, ss, rs):
        x, y, c, _ = _mesh_place()
        me, dg = 2 * x + y, 2 * (1 - x) + (1 - y)
        for k in range(self.n):
            sent, _ = self._copies(ins, bufs, ss, rs, k)
            for pair in range(4):
                sent[pair].start()
        for k in range(self.n):
            bufs[k][me] = ins[k][me]
            bufs[k][dg] = jnp.zeros(_half_shape(self.names[k]), BF16)

    def middle(self, ins, bufs, ss, rs):
        x, y, c, _ = _mesh_place()
        xn, yn = 2 * (1 - x) + y, 2 * x + (1 - y)
        for k in range(self.n):
            sent, received = self._copies(ins, bufs, ss, rs, k)
            name, n = self.names[k], self.n
            for pair, buf, own in ((0, bufs[n + k], self._part(ins[k].at[yn], name, 0)),
                                   (1, bufs[2 * n + k], self._part(ins[k].at[xn], name, 1))):
                received[pair].wait_recv()
                buf[...] = (buf[...].astype(F32) + own[...].astype(F32)).astype(BF16)
            sent[5].start()
            sent[4].start()

    def last(self, ins, bufs, ss, rs):
        for k in range(self.n):
            sent, received = self._copies(ins, bufs, ss, rs, k)
            for pair in range(2, 6):
                received[pair].wait_recv()
            for cp in sent:
                cp.wait_send()


class _PresumThenRelay:
    in_space = ANY
    middle_at = _ReduceRelay.middle_at

    def __init__(self, names):
        self.relay = _ReduceRelay(names)
        self.pre = _Presum(names, base=self.relay.n_sems)
        self.n_sems = self.relay.n_sems + self.pre.n_sems
        self.out_shape = self.relay.out_shape
        self.work_shape = list(self.relay.work_shape) + list(self.pre.out_shape) + list(self.pre.work_shape)
        self.n_relay = len(self.relay.out_shape) + len(self.relay.work_shape)

    def first(self, ins, bufs, ss, rs):
        self.pre.first(ins, bufs[self.n_relay:], ss, rs)

    def early(self, ins, bufs, ss, rs):
        self.pre.last(ins, bufs[self.n_relay:], ss, rs)
        self.relay.first(bufs[self.n_relay:], bufs[:self.n_relay], ss, rs)

    def middle(self, ins, bufs, ss, rs):
        self.relay.middle(bufs[self.n_relay:], bufs[:self.n_relay], ss, rs)

    def last(self, ins, bufs, ss, rs):
        self.relay.last(bufs[self.n_relay:], bufs[:self.n_relay], ss, rs)


class _SendPartials:
    def __init__(self, names, small_shape=None):
        self.n = len(names)
        self.small = small_shape is not None
        self.n_sems = 3 * self.n + 7
        self.out_shape = [jax.ShapeDtypeStruct((N_CHIP,) + _half_shape(nm), BF16) for nm in names]
        if self.small:
            self.out_shape.append(jax.ShapeDtypeStruct((N_DEV,) + small_shape, F32))

    def _piece(self, ins, outs, ss, rs, k, j, peer, src_slot, dst_slot, c):
        return _remote(ins[k].at[src_slot], outs[k].at[dst_slot], ss, rs, 3 * k + j, (*peer, c))

    def _small(self, ins, outs, ss, rs, r, other, slot):
        return _remote(ins[self.n], outs[self.n].at[slot], ss, rs, 3 * self.n + r, other)

    @staticmethod
    def _others(x, y, c):
        return [(x, y, 1 - c), (1 - x, y, c), (1 - x, y, 1 - c), (x, 1 - y, c), (x, 1 - y, 1 - c),
                (1 - x, 1 - y, c), (1 - x, 1 - y, 1 - c)]

    def first(self, ins, outs, ss, rs, only=None):
        x, y, c, peers = _mesh_place()
        me = 2 * x + y
        which = range(self.n) if only is None else only
        for k in which:
            for j, (px, py) in enumerate(peers):
                self._piece(ins, outs, ss, rs, k, j, (px, py), 2 * px + py, me, c).start()
        if self.small:
            for r, other in enumerate(self._others(x, y, c)):
                self._small(ins, outs, ss, rs, r, other, 4 * x + 2 * y + c).start()
            outs[self.n][4 * x + 2 * y + c] = ins[self.n][...]
        for k in which:
            outs[k][me] = ins[k][me]

    def middle(self, ins, outs, ss, rs):
        pass

    def last(self, ins, outs, ss, rs):
        x, y, c, peers = _mesh_place()
        me = 2 * x + y
        for k in range(self.n):
            for j, (px, py) in enumerate(peers):
                self._piece(ins, outs, ss, rs, k, j, (px, py), me, 2 * px + py, c).wait_recv()
                self._piece(ins, outs, ss, rs, k, j, (px, py), 2 * px + py, me, c).wait_send()
        if self.small:
            for r, (px, py, pc) in enumerate(self._others(x, y, c)):
                self._small(ins, outs, ss, rs, r, (px, py, pc), 4 * px + 2 * py + pc).wait_recv()
                self._small(ins, outs, ss, rs, r, (px, py, pc), 4 * x + 2 * y + c).wait_send()


class _PresumThenSend:
    def __init__(self, names):
        self.send = _SendPartials(names)
        self.pre = _Presum(names[-1:], base=self.send.n_sems)
        self.n = self.send.n
        self.n_sems = self.send.n_sems + self.pre.n_sems
        self.out_shape = self.send.out_shape
        self.work_shape = list(self.pre.out_shape) + list(self.pre.work_shape)
        self.in_space = [VMEM_WHOLE] * (self.n - 1) + [ANY]

    def _partials(self, ins, bufs):
        return list(ins[:self.n - 1]) + [bufs[self.n]]

    def first(self, ins, bufs, ss, rs):
        self.pre.first(ins[self.n - 1:], bufs[self.n:], ss, rs)
        self.send.first(ins, bufs[:self.n], ss, rs, only=range(self.n - 1))

    def early(self, ins, bufs, ss, rs):
        self.pre.last(ins[self.n - 1:], bufs[self.n:], ss, rs)
        self.send.first(self._partials(ins, bufs), bufs[:self.n], ss, rs, only=(self.n - 1,))

    def middle(self, ins, bufs, ss, rs):
        pass

    def last(self, ins, bufs, ss, rs):
        self.send.last(self._partials(ins, bufs), bufs[:self.n], ss, rs)


def _sum_swap(names, parts, small):
    n = len(parts)
    everyone = _SendPartials((), small.shape)

    def body(*refs):
        p_refs, (small_ref,), o_refs, (osmall_ref,), (all_ref,), (send_sems, recv_sems, ss_small, rs_small) = _split(
            refs, n, 1, n, 1, 1, 4)
        x, y, c = lax.axis_index("x"), lax.axis_index("y"), lax.axis_index("c")
        everyone.first([small_ref], [all_ref], ss_small, rs_small)

        def mine(k):
            part = _half(o_refs[k], c, names[k])
            return _remote(part, part, send_sems, recv_sems, k, (x, y, 1 - c))

        for k in range(n):
            for e in range(2):
                @pl.when(c == e)
                def _():
                    g = p_refs[k][0].astype(F32)
                    for s in range(1, N_CHIP):
                        g = g + p_refs[k][s].astype(F32)
                    r, cols = _half_shape(names[k])
                    if _BIG_SPLIT[names[k]] == 0:
                        o_refs[k][e * r:(e + 1) * r, :] = g
                    else:
                        o_refs[k][:, e * cols:(e + 1) * cols] = g
            mine(k).start()
        for k in range(n):
            theirs = _half(o_refs[k], 1 - c, names[k])
            _remote(theirs, theirs, send_sems, recv_sems, k, (x, y, 1 - c)).wait_recv()
            mine(k).wait_send()
        everyone.last([small_ref], [all_ref], ss_small, rs_small)
        g = all_ref[0]
        for d in range(1, N_DEV):
            g = g + all_ref[d]
        osmall_ref[...] = g

    res = pl.pallas_call(
        body, in_specs=[VMEM_WHOLE] * (n + 1), out_specs=[VMEM_WHOLE] * (n + 1),
        out_shape=[jax.ShapeDtypeStruct(_BIG_SHARD[nm], F32) for nm in names] + [jax.ShapeDtypeStruct(small.shape, F32)],
        scratch_shapes=[pltpu.VMEM((N_DEV,) + small.shape, F32), pltpu.SemaphoreType.DMA((n,)), pltpu.SemaphoreType.DMA((n,)),
                        pltpu.SemaphoreType.DMA((everyone.n_sems,)), pltpu.SemaphoreType.DMA((everyone.n_sems,))],
        compiler_params=_params(), name="sum_swap",
    )(*parts, small)
    return res[:n], res[n]


def _tile(rows, cols, itemsize, budget):
    t = cols if rows % 16 else rows
    other = rows if rows % 16 else cols
    step = 256 if rows % 16 else 32
    while t % step == 0 and t * other * itemsize > budget:
        t //= 2
    return (rows, t) if rows % 16 else (t, cols)


def _adamw_math(w, g, m, v):
    m = ADAM_B1 * m + (1.0 - ADAM_B1) * g
    v = ADAM_B2 * v + (1.0 - ADAM_B2) * (g * g)
    m_hat = m / (1.0 - ADAM_B1 ** ADAM_STEP)
    v_hat = v / (1.0 - ADAM_B2 ** ADAM_STEP)
    delta = -ADAM_LR * (m_hat / (jnp.sqrt(v_hat) + ADAM_EPS) + ADAM_WD * w)
    return delta, m, v


def _adamw_big(g, w, m, v, name):
    r, c = w.shape
    tr, tc = _tile(r, c, 4, 2 * 1024 * 1024)

    def body(g_ref, w_ref, m_ref, v_ref, d_ref, nm_ref, nv_ref):
        d_ref[...], nm_ref[...], nv_ref[...] = _adamw_math(w_ref[...], g_ref[...], m_ref[...], v_ref[...])

    blk = pl.BlockSpec((tr, tc), lambda i, l: (i, l))
    return pl.pallas_call(
        body, grid=(r // tr, c // tc), in_specs=[blk, blk, blk, blk],
        out_specs=[blk, blk, blk], out_shape=[jax.ShapeDtypeStruct((r, c), F32)] * 3,
        compiler_params=_params(("arbitrary", "arbitrary")), name=name,
    )(g, w, m, v)


def _adamw_rows(g, w, m, v, name):
    r, k, lanes = w.shape
    tr = 296

    def body(g_ref, w_ref, m_ref, v_ref, g3_ref, d_ref, nm_ref, nv_ref):
        g = g_ref[...].reshape(tr, k, lanes)
        g3_ref[...] = g
        d_ref[...], nm_ref[...], nv_ref[...] = _adamw_math(w_ref[...], g, m_ref[...], v_ref[...])

    rows = pl.BlockSpec((tr, k, lanes), lambda i: (i, 0, 0))
    return pl.pallas_call(
        body, grid=(pl.cdiv(r, tr),), in_specs=[pl.BlockSpec((tr, k * lanes), lambda i: (i, 0)), rows, rows, rows],
        out_specs=[rows] * 4, out_shape=[jax.ShapeDtypeStruct((r, k, lanes), F32)] * 4,
        compiler_params=_params(("arbitrary",)), name=name,
    )(g, w, m, v)


def _adamw_small(ws, gs, ms, vs):
    n = len(ws)

    def body(*refs):
        w_refs, g_refs, m_refs, v_refs, d_refs, nm_refs, nv_refs = _split(refs, *([n] * 7))
        for k in range(n):
            d_refs[k][...], nm_refs[k][...], nv_refs[k][...] = _adamw_math(w_refs[k][...], g_refs[k][...], m_refs[k][...],
                                                                             v_refs[k][...])

    shapes = [jax.ShapeDtypeStruct(w.shape, F32) for w in ws]
    res = pl.pallas_call(body, out_shape=shapes * 3, name="adamw_small")(*ws, *gs, *ms, *vs)
    return res[:n], res[n:2 * n], res[2 * n:]


def _pack(arrs):
    flat = jnp.concatenate([a.reshape(-1) for a in arrs])
    rows = -(-flat.shape[0] // 1024) * 8
    return jnp.pad(flat, (0, rows * 128 - flat.shape[0])).reshape(rows, 128)


def _unpack(buf, shapes):
    flat = buf.reshape(-1)
    out, off = [], 0
    for s in shapes:
        size = 1
        for d in s:
            size *= d
        out.append(flat[off:off + size].reshape(s))
        off += size
    return out


def _block_rows(w):
    return jnp.pad(w.reshape(512, 4), ((0, 0), (0, 124)))


def _cols(a4):
    return jnp.transpose(a4, (1, 0, 2)).reshape(a4.shape[1], -1)


_LATE = ("w_pa", "w_pb", "w_o", "w_up", "w_down")
_RIDE_IN_PROJ = ("w_pa", "w_pb", "w_o", "w_down_b")
_RIDE_MIXER = ("w_up", "w_down_a")


def _full_weights(gathered):
    joined = {"w_in": (D_IN, D_MODEL), "w_o": (D_MODEL, D_MODEL)}
    return {n: (a.reshape(joined[n]) if n in joined else a) for n, a in gathered.items()}


def _local_step(x, target, w, sp, late_shards=None):
    sp = {n: (a.reshape(1, -1) if a.ndim == 1 else a) for n, a in sp.items()}
    wau = jnp.pad(sp["w_a_up"], ((0, 112), (0, 0)))
    wif = jnp.pad(sp["w_if"], ((0, 0), (0, 120)))
    bif = jnp.pad(sp["b_if"], ((0, 0), (0, 120)))
    p = {"wau": wau, "bau": sp["b_a_up"], "ggla": sp["g_gla_norm"], "cw": sp["conv_w"], "cb": sp["conv_b"],
         "wq": _block_rows(sp["w_q_ml"]), "wk": _block_rows(sp["w_k_ml"]), "wv": _block_rows(sp["w_v_ml"]),
         "wif": wif, "bif": bif, "skip": sp["ml_skip"], "gml": sp["g_ml_norm"]}

    if late_shards is None:
        (pm, gab, h), _ = _in_proj(x, sp["g_pre_mix"], w["w_in"])
        ab, *states = _mixer_fwd(pm, p)
        (x1, mix, merged), _ = _merge_fwd(ab, gab, x, w["w_pa"], w["w_pb"], w["w_o"], sp["g_post_mix"])
    else:
        shard = dict(zip(_LATE, late_shards))
        shard["w_down_a"], shard["w_down_b"] = shard["w_down"][0:512], shard["w_down"][512:1024]
        (pm, gab, h), got = _in_proj(x, sp["g_pre_mix"], w["w_in"], _Gather(_RIDE_IN_PROJ, middle_at=0.7),
                                     [shard[n] for n in _RIDE_IN_PROJ])
        w = dict(w, **_full_weights(dict(zip(_RIDE_IN_PROJ, got))))
        ab, *rest = _mixer_fwd(pm, p, _Gather(_RIDE_MIXER, middle_at=0.62), [shard[n] for n in _RIDE_MIXER])
        states = rest[:4]
        w.update(_full_weights(dict(zip(_RIDE_MIXER, rest[4:]))))
        (x1, mix, merged), _ = _merge_fwd(ab, gab, x, w["w_pa"], w["w_pb"], w["w_o"], sp["g_post_mix"])
    dx1, u, dd, h2, dpre, dg_post_mlp, dg_pre_mlp, loss = _mlp(x1, target, sp["g_pre_mlp"], sp["g_post_mlp"],
                                                                w["w_up"], w["w_down_a"], w["w_down_b"])
    big = {"w_up": _tn_matmul(h2, dpre, "dw_up", shards=N_CHIP)}
    merge_bwd = functools.partial(_merge_bwd, dx1, mix, ab, gab, merged, w["w_pa"], w["w_pb"], w["w_o"], sp["g_post_mix"])
    if late_shards is None:
        (dgab, dab, dg_post_mix, big["w_pa"], big["w_pb"], big["w_o"]), _ = merge_bwd()
        big["w_down"] = _tn_matmul(u, dd, "dw_down")
        dpm, dp, _ = _mixer_bwd(pm, dab, states, p)
    else:
        pieces = lambda n: big[n].reshape((N_CHIP,) + _BIG_SHARD[n])
        (dgab, dab, dg_post_mix, big["w_pa"], big["w_pb"], big["w_o"]), up = merge_bwd(_Presum(("w_up",)), [pieces("w_up")])
        big["w_down"], partial = _tn_matmul(u, dd, "dw_down", rider=_Presum(_LATE[:3]),
                                            rider_ins=[pieces(n) for n in _LATE[:3]])
        dpm, dp, parts = _mixer_bwd(pm, dab, states, p, _PresumThenSend(_LATE),
                                    list(partial) + list(up) + [pieces("w_down")])
        big = dict(zip(_LATE, parts))
    big["w_in"] = _dw_in(dpm, dgab, h)
    if late_shards is None:
        (dx, dg_pre_mix), _ = _in_proj_bwd(dpm, dgab, x, dx1, sp["g_pre_mix"], w["w_in"])
    else:
        (dx, dg_pre_mix), parts = _in_proj_bwd(dpm, dgab, x, dx1, sp["g_pre_mix"], w["w_in"], _PresumThenRelay(("w_in",)),
                                               [big["w_in"].reshape((N_CHIP,) + _BIG_SHARD["w_in"])])
        big["w_in"] = parts[0]
    small = {
        "g_pre_mix": dg_pre_mix, "b_a_up": dp["bau"], "g_gla_norm": dp["ggla"], "conv_b": dp["cb"],
        "w_q_ml": dp["wq"][:, 0:4].reshape(128, 4, 4), "w_k_ml": dp["wk"][:, 0:4].reshape(128, 4, 4),
        "w_v_ml": dp["wv"][:, 0:4].reshape(128, 4, 4),
        "b_if": dp["bif"][:, 0:8], "ml_skip": dp["skip"], "g_ml_norm": dp["gml"], "g_post_mix": dg_post_mix,
        "g_pre_mlp": dg_pre_mlp, "g_post_mlp": dg_post_mlp, "w_a_up": dp["wau"][0:16], "conv_w": dp["cw"],
        "w_if": dp["wif"][:, 0:8], "loss": loss[:, 0:1],
    }
    return dx, big, small


_SMALL_REPL = ("g_pre_mix", "b_a_up", "g_gla_norm", "conv_b", "w_q_ml", "w_k_ml", "w_v_ml", "b_if", "ml_skip",
               "g_ml_norm", "g_post_mix", "g_pre_mlp", "g_post_mlp")
_SMALL_SHARDED = ("w_a_up", "conv_w", "w_if")
_SMALL_ORDER = _SMALL_REPL + _SMALL_SHARDED + ("loss",)
_WEIGHTS = ("g_pre_mix", "w_in", "w_a_up", "b_a_up", "g_gla_norm", "conv_w", "conv_b", "w_q_ml", "w_k_ml", "w_v_ml",
            "w_if", "b_if", "ml_skip", "g_ml_norm", "w_pa", "w_pb", "w_o", "g_post_mix", "g_pre_mlp", "w_up", "w_down",
            "g_post_mlp")


_BLOCK_WEIGHTS = ("w_q_ml", "w_k_ml", "w_v_ml")


def _stored(name, a):
    if name in _BLOCK_WEIGHTS:
        return jnp.transpose(a, (0, 2, 3, 1)).reshape(16, 128)
    if name == "w_if":
        return jnp.transpose(a, (0, 2, 1)).reshape(8, 384)
    return a


def _unstored(name, a):
    if name in _BLOCK_WEIGHTS:
        return jnp.transpose(a.reshape(1, 4, 4, 128), (0, 3, 1, 2))
    if name == "w_if":
        return jnp.transpose(a.reshape(1, 8, 384), (0, 2, 1))
    return a


def _as_shard(name, a):
    return jnp.transpose(a, (2, 0, 1)).reshape(IN_SHARD, D_MODEL // 128, 128) if name == "w_in" else a[0]


def _in_shard_bf16(w_in):
    return jnp.transpose(w_in.astype(BF16), (2, 0, 1)).reshape(IN_SHARD, D_MODEL)


def _from_shard(name, a):
    return jnp.transpose(a, (1, 2, 0)).reshape(1, D_MODEL, IN_SHARD) if name == "w_in" else a[None]


def kernel(x, g_pre_mix, w_in, w_a_up, b_a_up, g_gla_norm, conv_w, conv_b, w_q_ml, w_k_ml, w_v_ml, w_if, b_if, ml_skip, g_ml_norm, w_pa, w_pb, w_o, g_post_mix, g_pre_mlp, w_up, w_down, g_post_mlp, loss_target, m_g_pre_mix, m_w_in, m_w_a_up, m_b_a_up, m_g_gla_norm, m_conv_w, m_conv_b, m_w_q_ml, m_w_k_ml, m_w_v_ml, m_w_if, m_b_if, m_ml_skip, m_g_ml_norm, m_w_pa, m_w_pb, m_w_o, m_g_post_mix, m_g_pre_mlp, m_w_up, m_w_down, m_g_post_mlp, v_g_pre_mix, v_w_in, v_w_a_up, v_b_a_up, v_g_gla_norm, v_conv_w, v_conv_b, v_w_q_ml, v_w_k_ml, v_w_v_ml, v_w_if, v_b_if, v_ml_skip, v_g_ml_norm, v_w_pa, v_w_pb, v_w_o, v_g_post_mix, v_g_pre_mlp, v_w_up, v_w_down, v_g_post_mlp):
    args = dict(locals())
    wts = {n: _as_shard(n, args[n]) for n in _WEIGHTS}
    mom = {n: _as_shard(n, args["m_" + n]) for n in _WEIGHTS}
    var = {n: _as_shard(n, args["v_" + n]) for n in _WEIGHTS}
    chip = 2 * lax.axis_index("x") + lax.axis_index("y")

    first = ("w_in",) + _SMALL_SHARDED
    gathered = dict(zip(first, _run_alone(_Gather(("w_in",), [wts[n] for n in _SMALL_SHARDED]),
                                          [_in_shard_bf16(w_in)] + [wts[n] for n in _SMALL_SHARDED],
                                          "gather_first")))
    sp = {n: wts[n] for n in _SMALL_REPL}
    sp["w_a_up"] = _cols(gathered["w_a_up"])
    sp["conv_w"] = _cols(gathered["conv_w"])
    sp["w_if"] = gathered["w_if"].reshape(1536, 8)

    dx, big, small = _local_step(x[0], loss_target[0], _full_weights({"w_in": gathered["w_in"]}), sp,
                                 late_shards=[wts[n].astype(BF16) for n in _LATE])

    small_shapes = [small[n].shape for n in _SMALL_ORDER]
    packed = _pack([small[n] for n in _SMALL_ORDER])
    sums, small_sum = _sum_swap(_BIG, [big[n] for n in _BIG], packed)

    grads, delta, new_m, new_v = {}, {}, {}, {}
    for n, g in zip(_BIG, sums):
        if n == "w_in":
            g, d, nm, nv = _adamw_rows(g, wts[n], mom[n], var[n], "adamw_" + n)
        else:
            d, nm, nv = _adamw_big(g, wts[n], mom[n], var[n], "adamw_" + n)
        grads[n], delta[n], new_m[n], new_v[n] = (_from_shard(n, a) for a in (g, d, nm, nv))
    summed = dict(zip(_SMALL_ORDER, _unpack(small_sum, small_shapes)))
    loss = summed["loss"].reshape(())
    summed["w_a_up"] = lax.dynamic_slice_in_dim(summed["w_a_up"], chip * 64, 64, axis=1)
    summed["conv_w"] = lax.dynamic_slice_in_dim(summed["conv_w"], chip * 128, 128, axis=1)
    summed["w_if"] = lax.dynamic_slice_in_dim(summed["w_if"], chip * 384, 384, axis=0)
    small_names = _SMALL_REPL + _SMALL_SHARDED
    g_stored = [_stored(n, summed[n].reshape(args[n].shape)) for n in small_names]
    upd = _adamw_small([_stored(n, args[n]) for n in small_names], g_stored,
                       [_stored(n, args["m_" + n]) for n in small_names], [_stored(n, args["v_" + n]) for n in small_names])
    for dst, arrs in zip((grads, delta, new_m, new_v), (g_stored,) + tuple(upd)):
        dst.update({n: _unstored(n, a) for n, a in zip(small_names, arrs)})

    outs = [loss, dx[None]]
    for group in (grads, delta, new_m, new_v):
        outs += [group[n] for n in _WEIGHTS]
    return tuple(outs)
```

```python
import functools

import jax
import jax.numpy as jnp
from jax import lax
from jax.experimental import pallas as pl
from jax.experimental.pallas import tpu as pltpu

F32 = jnp.float32
BF16 = jnp.bfloat16

SEQ = 2048
D_MODEL = 1024
CHUNK = 64
N_CHUNK = SEQ // CHUNK
HEADS = 4
GLA_DK = 64
GLA_DV = 128
ML_DH = 128
D_FF = 4096
EPS = 1e-6
N_CHIP = 4
N_DEV = 8
TOK_TILE = 256
N_TOK_TILE = SEQ // TOK_TILE
SWEEP = 2
assert CHUNK == 64
N_SWEEP = N_CHUNK // SWEEP

PM_W = 2688
PM_XM = 1536
PM_OP = 2048
PM_AL = 2560
GAB_W = 2048
D_IN = 4624
IN_SHARD = D_IN // N_CHIP
IN_ALOW = 1536
IN_XM = 1552
IN_GATES = 2576

ADAM_LR = 0.001
ADAM_B1 = 0.9
ADAM_B2 = 0.999
ADAM_EPS = 1e-08
ADAM_WD = 0.01
ADAM_STEP = 10

VMEM_LIMIT = 56 * 1024 * 1024


def _params(sem=None):
    return pltpu.CompilerParams(dimension_semantics=sem, vmem_limit_bytes=VMEM_LIMIT)


def _dot(a, b, ca, cb):
    return lax.dot_general(a.astype(BF16), b.astype(BF16), (((ca,), (cb,)), ((), ())), preferred_element_type=F32)


def _pmm_nn(a, b):
    return _dot(a, b, 1, 0)


def _pmm_nt(a, b):
    return _dot(a, b, 1, 1)


def _pmm_tn(a, b):
    return _dot(a, b, 0, 0)


def _pcmm(c, x):
    return lax.dot_general(c, x, (((1,), (0,)), ((), ())), precision=lax.Precision.HIGHEST, preferred_element_type=F32)


@jax.custom_vjp
def _mm_nn(a, b):
    return _dot(a, b, 1, 0)


@jax.custom_vjp
def _mm_nt(a, b):
    return _dot(a, b, 1, 1)


@jax.custom_vjp
def _mm_tn(a, b):
    return _dot(a, b, 0, 0)


_mm_nn.defvjp(lambda a, b: (_dot(a, b, 1, 0), (a, b)), lambda r, g: (_mm_nt(g, r[1]), _mm_tn(r[0], g)))
_mm_nt.defvjp(lambda a, b: (_dot(a, b, 1, 1), (a, b)), lambda r, g: (_mm_nn(g, r[1]), _mm_tn(g, r[0])))
_mm_tn.defvjp(lambda a, b: (_dot(a, b, 0, 0), (a, b)), lambda r, g: (_mm_nt(r[1], g), _mm_nn(r[0], g)))


@jax.custom_vjp
def _cmm(c, x):
    return _pcmm(c, x)


_cmm.defvjp(
    lambda c, x: (_pcmm(c, x), c),
    lambda c, g: (jnp.zeros_like(c), lax.dot_general(c, g, (((0,), (0,)), ((), ())), precision=lax.Precision.HIGHEST,
                                                      preferred_element_type=F32)),
)

_PLAIN_OPS = (_pmm_nn, _pmm_nt, _pmm_tn, _pcmm)
_VJP_OPS = (_mm_nn, _mm_nt, _mm_tn, _cmm)


def _sigmoid(x):
    return 0.5 * (jnp.tanh(0.5 * x) + 1.0)


def _log_sigmoid(x):
    return jnp.minimum(x, 0.0) - jnp.log(1.0 + jnp.exp(-jnp.abs(x)))


def _mean(x):
    return jnp.mean(x, axis=-1, keepdims=True)


def _nt(a, b):
    return lax.dot_general(a, b, (((1,), (1,)), ((), ())), preferred_element_type=F32)


def _tn(a, b):
    return lax.dot_general(a, b, (((0,), (0,)), ((), ())), preferred_element_type=F32)


def _mixer_chunk(ops, p, st, pm, xprev8):
    mm_nn, mm_nt, mm_tn, cmm = ops
    n_rows = pm.shape[0]
    n_ch = n_rows // CHUNK
    row = lax.broadcasted_iota(jnp.int32, (n_rows, n_rows), 0)
    col = lax.broadcasted_iota(jnp.int32, (n_rows, n_rows), 1)
    tri = jnp.logical_and((row >> 6) == (col >> 6), row >= col).astype(F32)
    causal = tri[0:CHUNK, 0:CHUNK] > 0.0
    q = pm[:, 0:256]
    k = pm[:, 256:512]
    v = pm[:, 512:1024]
    g = pm[:, 1024:1536]
    xm = pm[:, PM_XM:PM_XM + 512]
    opre = pm[:, PM_OP:PM_OP + 512]
    alow = pm[:, PM_AL:PM_AL + 128]
    hs = range(HEADS)
    cs = range(n_ch)
    pairs = [(i, h) for i in cs for h in hs]
    rs = [slice(i * CHUNK, (i + 1) * CHUNK) for i in cs]
    last = [slice((i + 1) * CHUNK - 1, (i + 1) * CHUNK) for i in cs]
    s6 = [slice(h * GLA_DK, (h + 1) * GLA_DK) for h in hs]
    s12 = [slice(h * 128, (h + 1) * 128) for h in hs]

    xx = jnp.concatenate([xprev8, xm], axis=0)
    pre = p["cb"]
    for j in range(4):
        pre = pre + p["cw"][j:j + 1, :] * xx[5 + j:5 + j + n_rows, :]
    xc = pre * _sigmoid(pre)
    qm = [mm_nn(xc[:, s12[h]], p["wq"][h]) for h in hs]
    km = [mm_nn(xc[:, s12[h]], p["wk"][h]) for h in hs]
    vm = [mm_nn(xm[:, s12[h]], p["wv"][h]) for h in hs]
    qcat = jnp.concatenate(qm, axis=1)
    kcat = jnp.concatenate(km, axis=1)
    vcat = jnp.concatenate(vm, axis=1)
    gates = (mm_nn(qcat, p["wif"][0:512]) + mm_nn(kcat, p["wif"][512:1024]) + mm_nn(vcat, p["wif"][1024:1536])
             + p["bif"])
    lf = _log_sigmoid(gates)
    fc = cmm(tri, lf)
    gates_t = gates.T
    fc_t = fc.T

    la = _log_sigmoid(mm_nn(alow, p["wau"]) + p["bau"]) * (1.0 / 16.0)
    cum = cmm(tri, la)
    cum_last = [cum[last[i], :] for i in cs]
    to_end = jnp.concatenate([cum_last[i] - cum[rs[i], :] for i in cs], axis=0)
    e_pos = jnp.exp(cum)
    e_neg = jnp.exp(-cum)
    qs = q * (GLA_DK ** -0.5)
    qp = qs * e_pos
    qn = qs * e_neg
    kp = k * e_pos
    kn = k * e_neg
    kl = k * jnp.exp(to_end)
    dec = [jnp.exp(cum_last[i]) for i in cs]
    ks = [km[h] * (ML_DH ** -0.5) for h in hs]
    li_c = {(i, h): gates[rs[i], h:h + 1] for i, h in pairs}
    fc_c = {(i, h): fc[rs[i], 4 + h:5 + h] for i, h in pairs}
    f_last = {(i, h): fc[last[i], 4 + h:5 + h] for i, h in pairs}

    a_fwd = {(i, h): mm_nt(qp[rs[i], s6[h]], kn[rs[i], s6[h]]) for i, h in pairs}
    a_bwd = {(i, h): mm_nt(qn[rs[i], s6[h]], kp[rs[i], s6[h]]) for i, h in pairs}
    s_chunk = {(i, h): mm_tn(v[rs[i], s12[h]], kl[rs[i], s6[h]]) for i, h in pairs}
    qk = {(i, h): mm_nt(qm[h][rs[i]], ks[h][rs[i]]) for i, h in pairs}
    a = {ih: f_last[ih] - fc_c[ih] + li_c[ih] for ih in pairs}
    m_loc = {ih: jnp.max(a[ih], axis=0, keepdims=True) for ih in pairs}
    kw = {(i, h): ks[h][rs[i]] * jnp.exp(a[(i, h)] - m_loc[(i, h)]) for i, h in pairs}
    c_chunk = {(i, h): mm_tn(kw[(i, h)], vm[h][rs[i]]) for i, h in pairs}
    mem = {(0, h): st["S"][h] for h in hs}
    c_in = {(0, h): st["C"][h] for h in hs}
    n_in = {(0, h): st["n"][h] for h in hs}
    m_in = {(0, h): st["m"][h][:, 0:1] for h in hs}
    for i, h in pairs:
        mem[(i + 1, h)] = mem[(i, h)] * dec[i][:, s6[h]] + s_chunk[(i, h)]
        m_nx = jnp.maximum(f_last[(i, h)] + m_in[(i, h)], m_loc[(i, h)])
        sp = jnp.exp(f_last[(i, h)] + m_in[(i, h)] - m_nx)
        sl = jnp.exp(m_loc[(i, h)] - m_nx)
        c_in[(i + 1, h)] = sp * c_in[(i, h)] + sl * c_chunk[(i, h)]
        n_in[(i + 1, h)] = sp * n_in[(i, h)] + sl * jnp.sum(kw[(i, h)], axis=0, keepdims=True)
        m_in[(i + 1, h)] = m_nx
    s_new = [mem[(n_ch, h)] for h in hs]
    o_inter = {(i, h): mm_nt(qp[rs[i], s6[h]], mem[(i, h)]) for i, h in pairs}
    q_c = {(i, h): mm_nn(qm[h][rs[i]], c_in[(i, h)]) for i, h in pairs}
    scores = {ih: jnp.where(causal, a_fwd[ih], a_bwd[ih]) for ih in pairs}
    log_d = {(i, h): gates_t[h:h + 1, rs[i]] - jnp.abs(fc_c[(i, h)] - fc_t[4 + h:5 + h, rs[i]]) for i, h in pairs}
    g_int = {ih: fc_c[ih] + m_in[ih] for ih in pairs}
    m_t = {ih: jnp.maximum(g_int[ih], jnp.max(log_d[ih], axis=1, keepdims=True)) for ih in pairs}
    s = {ih: qk[ih] * jnp.exp(log_d[ih] - m_t[ih]) for ih in pairs}
    scl = {ih: jnp.exp(g_int[ih] - m_t[ih]) for ih in pairs}
    o = {(i, h): mm_nn(scores[(i, h)], v[rs[i], s12[h]]) + o_inter[(i, h)] for i, h in pairs}
    num = {(i, h): mm_nn(s[(i, h)], vm[h][rs[i]]) + scl[(i, h)] * q_c[(i, h)] for i, h in pairs}
    o = {ih: o[ih] * lax.rsqrt(_mean(o[ih] * o[ih]) + EPS) * p["ggla"] for ih in pairs}
    gate = g * _sigmoid(g)
    out_a = {(i, h): o[(i, h)] * gate[rs[i], s12[h]] for i, h in pairs}
    den = {(i, h): jnp.sum(s[(i, h)], axis=1, keepdims=True)
           + scl[(i, h)] * jnp.sum(qm[h][rs[i]] * n_in[(i, h)], axis=1, keepdims=True) for i, h in pairs}
    den = {ih: jnp.maximum(jnp.abs(den[ih]), jnp.exp(-m_t[ih])) for ih in pairs}
    open_gate = _sigmoid(opre)
    hc = {(i, h): num[(i, h)] / den[(i, h)] * open_gate[rs[i], s12[h]] for i, h in pairs}
    d0 = {ih: hc[ih] - _mean(hc[ih]) for ih in pairs}
    y = {ih: d0[ih] * lax.rsqrt(_mean(d0[ih] * d0[ih]) + EPS) for ih in pairs}
    skipped = p["skip"] * xc
    out_b = {(i, h): y[(i, h)] * p["gml"][:, s12[h]] + skipped[rs[i], s12[h]] for i, h in pairs}
    ab = jnp.concatenate([jnp.concatenate([out_a[(i, h)] for h in hs] + [out_b[(i, h)] for h in hs], axis=1) for i in cs],
                         axis=0)
    new = {"S": s_new, "C": [c_in[(n_ch, h)] for h in hs], "n": [n_in[(n_ch, h)] for h in hs],
           "m": [jnp.broadcast_to(m_in[(n_ch, h)], (1, ML_DH)) for h in hs]}
    return ab, new


_P_NAMES = ("wau", "bau", "ggla", "cw", "cb", "wq", "wk", "wv", "wif", "bif", "skip", "gml")
_P_SHAPES = {
    "wau": (128, 256), "bau": (1, 256), "ggla": (1, 128), "cw": (4, 512), "cb": (1, 512),
    "wq": (512, 128), "wk": (512, 128), "wv": (512, 128),
    "wif": (1536, 128), "bif": (1, 128), "skip": (1, 512), "gml": (1, 512),
}
_P_BLOCKDIAG = ("wq", "wk", "wv")
_S_NAMES = ("S", "C", "n", "m")
_S_SHAPES = {"S": (HEADS, GLA_DV, GLA_DK), "C": (HEADS, ML_DH, ML_DH), "n": (HEADS, 1, ML_DH), "m": (HEADS, 1, ML_DH)}


def _per_head(ref):
    return [ref[h] for h in range(HEADS)]


def _block_mask():
    r = lax.broadcasted_iota(jnp.int32, (128, 128), 0)
    c = lax.broadcasted_iota(jnp.int32, (128, 128), 1)
    same_block = (r >> 2) == (c >> 2)
    spread = jnp.logical_and(r < 4, (c & 3) == r)
    return same_block.astype(F32), spread.astype(F32)


def _expand_blockdiag(w_ref, dense_ref):
    same_block, spread = _block_mask()
    for h in range(HEADS):
        tiled = _pmm_nn(w_ref[h * 128:(h + 1) * 128, :], spread)
        dense_ref[h] = tiled * same_block


def _collect_blockdiag(ddense_ref, dw_ref):
    same_block, spread = _block_mask()
    for h in range(HEADS):
        dw_ref[h * 128:(h + 1) * 128, :] = lax.dot_general(
            ddense_ref[h] * same_block, spread, (((1,), (1,)), ((), ())), precision=lax.Precision.HIGHEST,
            preferred_element_type=F32)


def _const_spec(shape):
    zeros = (0,) * len(shape)
    return pl.BlockSpec(shape, lambda i: zeros)


def _split(refs, *counts):
    out, at = [], 0
    for c in counts:
        out.append(refs[at:at + c])
        at += c
    assert at == len(refs)
    return out


def _ride(rider, phases, cond, ins, outs, sems):
    if rider is None or not any(hasattr(rider, phase) for phase in phases):
        return
    lands, (send_sems, recv_sems, flush_sems) = sems[:-3], sems[-3:]

    @pl.when(cond)
    def _():
        for phase in phases:
            if phase == "last" and hasattr(rider, "late"):
                rider.late(ins, lands, send_sems, recv_sems)
                rider.flush("late", lands, outs, flush_sems)
            getattr(rider, phase)(ins, lands, send_sems, recv_sems)
            if hasattr(rider, "flush"):
                rider.flush(phase, lands, outs, flush_sems)
        if "last" in phases and not hasattr(rider, "flush"):
            flush = [pltpu.make_async_copy(lands[k], outs[k], flush_sems.at[k]) for k in range(len(outs))]
            for cp in flush:
                cp.start()
            for cp in flush:
                cp.wait()


def _middle_step(rider, n_steps):
    return min(n_steps - 2, int(getattr(rider, "middle_at", 1.0) * n_steps))


def _rider_specs(rider, rider_ins):
    if rider is None:
        return [], [], [], []
    scratch = [pltpu.VMEM(s.shape, s.dtype) for s in list(rider.out_shape) + list(getattr(rider, "work_shape", ()))]
    scratch += [pltpu.SemaphoreType.DMA((rider.n_sems,)), pltpu.SemaphoreType.DMA((rider.n_sems,)),
                pltpu.SemaphoreType.DMA((getattr(rider, "n_flush", len(rider.out_shape)),))]
    in_space = getattr(rider, "in_space", VMEM_WHOLE)
    in_specs = list(in_space) if isinstance(in_space, (list, tuple)) else [in_space] * len(rider_ins)
    return in_specs, [ANY] * len(rider.out_shape), list(rider.out_shape), scratch


def _mixer_fwd(pm, p, rider=None, rider_ins=()):
    n_p = len(_P_NAMES)
    r_in, r_out_specs, r_out_shape, r_sems = _rider_specs(rider, rider_ins)

    def body(*refs):
        (pm_ref, xprev_ref), p_list, ride_in, (ab_ref,), so_refs, ride_out, sc_refs, dense_list, sems = _split(
            refs, 2, n_p, len(r_in), 1, 4, len(r_out_specs), 4, 3, len(r_sems))
        p_refs = dict(zip(_P_NAMES, p_list))
        dense = dict(zip(_P_BLOCKDIAG, dense_list))
        n = pl.program_id(0)
        _ride(rider, ("first",), n == 0, ride_in, ride_out, sems)

        @pl.when(n == 0)
        def _():
            for r in sc_refs:
                r[...] = jnp.zeros_like(r)
            for nm in _P_BLOCKDIAG:
                _expand_blockdiag(p_refs[nm], dense[nm])

        st = {name: _per_head(r) for name, r in zip(_S_NAMES, sc_refs)}
        pv = {nm: (_per_head(dense[nm]) if nm in _P_BLOCKDIAG else p_refs[nm][...]) for nm in _P_NAMES}
        for name, r in zip(_S_NAMES, so_refs):
            for h in range(HEADS):
                r[0, h] = st[name][h]
        xprev8 = jnp.where(n > 0, xprev_ref[CHUNK - 8:CHUNK, :], 0.0)
        ab, st = _mixer_chunk(_PLAIN_OPS, pv, st, pm_ref[...], xprev8)
        ab_ref[...] = ab.astype(BF16)
        for name, r in zip(_S_NAMES, sc_refs):
            for h in range(HEADS):
                r[h] = st[name][h]
        _ride(rider, ("middle",), n == _middle_step(rider, N_SWEEP), ride_in, ride_out, sems)
        _ride(rider, ("last",), n == N_SWEEP - 1, ride_in, ride_out, sems)

    in_specs = [pl.BlockSpec((SWEEP * CHUNK, PM_W), lambda i: (i, 0)),
                pl.BlockSpec((CHUNK, 512), lambda i: (jnp.maximum(SWEEP * i - 1, 0), PM_XM // 512))]
    in_specs += [_const_spec(_P_SHAPES[nm]) for nm in _P_NAMES] + r_in
    out_specs = [pl.BlockSpec((SWEEP * CHUNK, 1024), lambda i: (i, 0))]
    out_shape = [jax.ShapeDtypeStruct((SEQ, 1024), BF16)]
    for nm in _S_NAMES:
        shp = _S_SHAPES[nm]
        out_specs.append(pl.BlockSpec((1,) + shp, lambda i: (i, 0, 0, 0)))
        out_shape.append(jax.ShapeDtypeStruct((N_SWEEP,) + shp, F32))
    return pl.pallas_call(
        body, grid=(N_SWEEP,), in_specs=in_specs, out_specs=out_specs + r_out_specs, out_shape=out_shape + r_out_shape,
        scratch_shapes=[pltpu.VMEM(_S_SHAPES[nm], F32) for nm in _S_NAMES]
        + [pltpu.VMEM((HEADS, 128, 128), F32) for _ in _P_BLOCKDIAG] + r_sems,
        compiler_params=_params(("arbitrary",)), name="mixer_fwd",
    )(pm, pm, *[p[nm] for nm in _P_NAMES], *rider_ins)


def _mixer_bwd(pm, dab, states, p, rider=None, rider_ins=()):
    n_p = len(_P_NAMES)
    r_in, r_out_specs, r_out_shape, r_sems = _rider_specs(rider, rider_ins)

    def body(*refs):
        ((pm_ref, xprev_ref, dab_ref), si_refs, p_list, ride_in, (dpm_ref,), dp_list, ride_out, ds_refs, (carry_ref,),
         dense_list, ddense_list, sems) = _split(refs, 3, 4, n_p, len(r_in), 1, n_p, len(r_out_specs), 4, 1, 3, 3, len(r_sems))
        p_refs = dict(zip(_P_NAMES, p_list))
        dp_refs = dict(zip(_P_NAMES, dp_list))
        dense = dict(zip(_P_BLOCKDIAG, dense_list))
        ddense = dict(zip(_P_BLOCKDIAG, ddense_list))
        i = pl.program_id(0)
        blk = N_SWEEP - 1 - i
        _ride(rider, ("first",), i == 0, ride_in, ride_out, sems)

        @pl.when(i == 0)
        def _():
            for r in ds_refs:
                r[...] = jnp.zeros_like(r)
            for nm in _P_NAMES:
                if nm in _P_BLOCKDIAG:
                    ddense[nm][...] = jnp.zeros_like(ddense[nm])
                    _expand_blockdiag(p_refs[nm], dense[nm])
                else:
                    dp_refs[nm][...] = jnp.zeros_like(dp_refs[nm])
            carry_ref[...] = jnp.zeros_like(carry_ref)

        pv = {nm: (_per_head(dense[nm]) if nm in _P_BLOCKDIAG else p_refs[nm][...]) for nm in _P_NAMES}
        dst = {name: _per_head(r) for name, r in zip(_S_NAMES, ds_refs)}
        st = {name: [r[0, h] for h in range(HEADS)] for name, r in zip(_S_NAMES, si_refs)}
        xprev8 = jnp.where(blk > 0, xprev_ref[CHUNK - 8:CHUNK, :], 0.0)
        _, vjp = jax.vjp(functools.partial(_mixer_chunk, _VJP_OPS), pv, st, pm_ref[...], xprev8)
        dp_sum, dst, dpm, dxprev8 = vjp((dab_ref[...], dst))
        reach = jnp.concatenate([jnp.zeros((SWEEP * CHUNK - 8, 512), F32), carry_ref[...]], axis=0)
        dpm_ref[:, 0:PM_XM] = dpm[:, 0:PM_XM].astype(BF16)
        dpm_ref[:, PM_XM:PM_XM + 512] = (dpm[:, PM_XM:PM_XM + 512] + reach).astype(BF16)
        dpm_ref[:, PM_XM + 512:PM_W] = dpm[:, PM_XM + 512:PM_W].astype(BF16)
        carry_ref[...] = dxprev8
        for name, r in zip(_S_NAMES, ds_refs):
            for h in range(HEADS):
                r[h] = dst[name][h]
        for nm in _P_NAMES:
            if nm in _P_BLOCKDIAG:
                for h in range(HEADS):
                    ddense[nm][h] += dp_sum[nm][h]
            else:
                dp_refs[nm][...] += dp_sum[nm]

        @pl.when(i == N_SWEEP - 1)
        def _():
            for nm in _P_BLOCKDIAG:
                _collect_blockdiag(ddense[nm], dp_refs[nm])

        _ride(rider, ("early",), i == 1, ride_in, ride_out, sems)
        _ride(rider, ("middle",), i == _middle_step(rider, N_SWEEP), ride_in, ride_out, sems)
        _ride(rider, ("last",), i == N_SWEEP - 1, ride_in, ride_out, sems)

    rev = lambda i: (N_SWEEP - 1 - i, 0)
    in_specs = [pl.BlockSpec((SWEEP * CHUNK, PM_W), rev),
                pl.BlockSpec((CHUNK, 512), lambda i: (jnp.maximum(SWEEP * (N_SWEEP - 1 - i) - 1, 0), PM_XM // 512)),
                pl.BlockSpec((SWEEP * CHUNK, 1024), rev)]
    for nm in _S_NAMES:
        in_specs.append(pl.BlockSpec((1,) + _S_SHAPES[nm], lambda i: (N_SWEEP - 1 - i, 0, 0, 0)))
    in_specs += [_const_spec(_P_SHAPES[nm]) for nm in _P_NAMES] + r_in
    out_specs = [pl.BlockSpec((SWEEP * CHUNK, PM_W), rev)] + [_const_spec(_P_SHAPES[nm]) for nm in _P_NAMES]
    out_shape = [jax.ShapeDtypeStruct((SEQ, PM_W), BF16)] + [jax.ShapeDtypeStruct(_P_SHAPES[nm], F32) for nm in _P_NAMES]
    res = pl.pallas_call(
        body, grid=(N_SWEEP,), in_specs=in_specs, out_specs=out_specs + r_out_specs, out_shape=out_shape + r_out_shape,
        scratch_shapes=[pltpu.VMEM(_S_SHAPES[nm], F32) for nm in _S_NAMES] + [pltpu.VMEM((8, 512), F32)]
        + [pltpu.VMEM((HEADS, 128, 128), F32) for _ in range(2 * len(_P_BLOCKDIAG))] + r_sems,
        compiler_params=_params(("arbitrary",)), name="mixer_bwd",
    )(pm, pm, dab, *states, *[p[nm] for nm in _P_NAMES], *rider_ins)
    return res[0], dict(zip(_P_NAMES, res[1:1 + n_p])), res[1 + n_p:]


def _tok(width):
    return pl.BlockSpec((TOK_TILE, width), lambda i: (i, 0))


def _once(shape):
    zeros = (0,) * len(shape)
    return pl.BlockSpec(shape, lambda i: zeros, pipeline_mode=pl.Buffered(1))


def _rms_fwd(x):
    r = lax.rsqrt(_mean(x * x) + EPS)
    return x * r, r


def _rms_bwd(dy, xn, r, g):
    gd = dy * g
    return r * (gd - xn * _mean(xn * gd))


def _tiled_call(body, in_specs, out_specs, out_shape, args, name, rider=None, rider_ins=()):
    r_in, r_out_specs, r_out_shape, r_scratch = _rider_specs(rider, rider_ins)
    n_in, n_out = len(in_specs), len(out_specs)

    def hosted(*refs):
        ins, ride_in, outs, ride_out, scratch = _split(refs, n_in, len(r_in), n_out, len(r_out_specs), len(r_scratch))
        i = pl.program_id(0)
        _ride(rider, ("first",), i == 0, ride_in, ride_out, scratch)
        body(*ins, *outs)
        _ride(rider, ("early",), i == 1, ride_in, ride_out, scratch)
        _ride(rider, ("middle",), i == _middle_step(rider, N_TOK_TILE), ride_in, ride_out, scratch)
        _ride(rider, ("last",), i == N_TOK_TILE - 1, ride_in, ride_out, scratch)

    res = pl.pallas_call(
        hosted, grid=(N_TOK_TILE,), in_specs=list(in_specs) + r_in, out_specs=list(out_specs) + r_out_specs,
        out_shape=list(out_shape) + r_out_shape, scratch_shapes=r_scratch,
        compiler_params=_params(("arbitrary",)), name=name,
    )(*args, *rider_ins)
    return res[:n_out], res[n_out:]


def _in_proj(x, g_pre, wt_in, rider=None, rider_ins=()):
    def body(x_ref, g_ref, wt_ref, pm_ref, gab_ref, h_ref):
        xn, _ = _rms_fwd(x_ref[...])
        h = (xn * g_ref[...]).astype(BF16)
        h_ref[...] = h
        pm_ref[:, 0:PM_XM] = _nt(h, wt_ref[0:IN_ALOW, :])
        pm_ref[:, PM_XM:PM_AL] = _nt(h, wt_ref[IN_XM:IN_GATES, :])
        pm_ref[:, PM_AL:PM_W] = _nt(h, wt_ref[IN_ALOW:IN_ALOW + 128, :])
        gab_ref[...] = _nt(h, wt_ref[IN_GATES:D_IN, :])

    return _tiled_call(
        body, [_tok(D_MODEL), _once((1, D_MODEL)), _once((D_IN, D_MODEL))], [_tok(PM_W), _tok(GAB_W), _tok(D_MODEL)],
        [jax.ShapeDtypeStruct((SEQ, PM_W), F32), jax.ShapeDtypeStruct((SEQ, GAB_W), F32),
         jax.ShapeDtypeStruct((SEQ, D_MODEL), BF16)], (x, g_pre, wt_in), "in_proj", rider, rider_ins)


def _merge_fwd(ab, gab, x, w_pa4, w_pb4, w_o, g_post, rider=None, rider_ins=()):
    def body(ab_ref, gab_ref, x_ref, wpa_ref, wpb_ref, wo_ref, g_ref, x1_ref, mix_ref, mg_ref):
        a = ab_ref[:, 0:512]
        b = ab_ref[:, 512:1024]
        for j in range(N_CHIP):
            blk = slice(j * 256, (j + 1) * 256)
            ya = jnp.dot(a, wpa_ref[j], preferred_element_type=F32)
            yb = jnp.dot(b, wpb_ref[j], preferred_element_type=F32)
            sa = _sigmoid(gab_ref[:, j * 256:(j + 1) * 256])
            sb = _sigmoid(gab_ref[:, 1024 + j * 256:1024 + (j + 1) * 256])
            mg_ref[:, blk] = (sa * ya + sb * yb).astype(BF16)
        mix = jnp.dot(mg_ref[...], wo_ref[...], preferred_element_type=F32)
        mix_ref[...] = mix
        mn, _ = _rms_fwd(mix)
        x1_ref[...] = x_ref[...] + mn * g_ref[...]

    return _tiled_call(
        body, [_tok(1024), _tok(GAB_W), _tok(D_MODEL), _once((N_CHIP, 512, 256)), _once((N_CHIP, 512, 256)),
               _once((D_MODEL, D_MODEL)), _once((1, D_MODEL))], [_tok(D_MODEL), _tok(D_MODEL), _tok(D_MODEL)],
        [jax.ShapeDtypeStruct((SEQ, D_MODEL), F32), jax.ShapeDtypeStruct((SEQ, D_MODEL), F32),
         jax.ShapeDtypeStruct((SEQ, D_MODEL), BF16)], (ab, gab, x, w_pa4, w_pb4, w_o, g_post), "merge_fwd", rider, rider_ins)


def _mlp(x1, target, g_pre, g_post, w_up4, w_down_a4, w_down_b4):
    def body(x1_ref, t_ref, gpre_ref, gpost_ref, wup_ref, wda_ref, wdb_ref,
             dx1_ref, u_ref, dd_ref, h2_ref, dpre_ref, dgpost_ref, dgpre_ref, loss_ref):
        @pl.when(pl.program_id(0) == 0)
        def _():
            dgpost_ref[...] = jnp.zeros_like(dgpost_ref)
            dgpre_ref[...] = jnp.zeros_like(dgpre_ref)
            loss_ref[...] = jnp.zeros_like(loss_ref)

        x1 = x1_ref[...]
        gpre = gpre_ref[...]
        gpost = gpost_ref[...]
        xn2, r2 = _rms_fwd(x1)
        h2 = (xn2 * gpre).astype(BF16)
        h2_ref[...] = h2
        rl = []
        d = jnp.zeros((TOK_TILE, D_MODEL), F32)
        for j in range(N_CHIP):
            blk = slice(j * 1024, (j + 1) * 1024)
            r = jnp.maximum(jnp.dot(h2, wup_ref[j], preferred_element_type=F32), 0.0)
            rl.append(r)
            u = (r * r).astype(BF16)
            u_ref[:, blk] = u
            d = d + jnp.dot(u[:, 0:512], wda_ref[j], preferred_element_type=F32)
            d = d + jnp.dot(u[:, 512:1024], wdb_ref[j], preferred_element_type=F32)
        dn, r3 = _rms_fwd(d)
        diff = x1 + dn * gpost - t_ref[...]
        loss_ref[...] += jnp.sum(diff * diff, keepdims=True) * (0.5 / D_MODEL)
        dy = diff * (1.0 / D_MODEL)
        dgpost_ref[...] += jnp.sum(dy * dn, axis=0, keepdims=True)
        dd = _rms_bwd(dy, dn, r3, gpost).astype(BF16)
        dd_ref[...] = dd
        dh2 = jnp.zeros((TOK_TILE, D_MODEL), F32)
        for j in range(N_CHIP):
            blk = slice(j * 1024, (j + 1) * 1024)
            du = jnp.concatenate([_nt(dd, wda_ref[j]), _nt(dd, wdb_ref[j])], axis=1)
            dpre = (du * (2.0 * rl[j])).astype(BF16)
            dpre_ref[:, blk] = dpre
            dh2 = dh2 + _nt(dpre, wup_ref[j])
        dgpre_ref[...] += jnp.sum(dh2 * xn2, axis=0, keepdims=True)
        dx1_ref[...] = dy + _rms_bwd(dh2, xn2, r2, gpre)

    acc = pl.BlockSpec((1, D_MODEL), lambda i: (0, 0))
    return pl.pallas_call(
        body, grid=(N_TOK_TILE,),
        in_specs=[_tok(D_MODEL), _tok(D_MODEL), _once((1, D_MODEL)), _once((1, D_MODEL)),
                  _once((N_CHIP, D_MODEL, 1024)), _once((N_CHIP, 512, D_MODEL)), _once((N_CHIP, 512, D_MODEL))],
        out_specs=[_tok(D_MODEL), _tok(D_FF), _tok(D_MODEL), _tok(D_MODEL), _tok(D_FF), acc, acc,
                   pl.BlockSpec((1, 128), lambda i: (0, 0))],
        out_shape=[jax.ShapeDtypeStruct((SEQ, D_MODEL), F32), jax.ShapeDtypeStruct((SEQ, D_FF), BF16),
                   jax.ShapeDtypeStruct((SEQ, D_MODEL), BF16), jax.ShapeDtypeStruct((SEQ, D_MODEL), BF16),
                   jax.ShapeDtypeStruct((SEQ, D_FF), BF16), jax.ShapeDtypeStruct((1, D_MODEL), F32),
                   jax.ShapeDtypeStruct((1, D_MODEL), F32), jax.ShapeDtypeStruct((1, 128), F32)],
        compiler_params=_params(("arbitrary",)), name="mlp_fwd_bwd",
    )(x1, target, g_pre, g_post, w_up4, w_down_a4, w_down_b4)


def _merge_bwd(dx1, mix, ab, gab, merged, w_pa4, w_pb4, w_o, g_post):
    def body(dx1_ref, mix_ref, ab_ref, gab_ref, mg_ref, wpa_ref, wpb_ref, wo_ref, g_ref,
             dgab_ref, dab_ref, dg_ref, dwpa_ref, dwpb_ref, dwo_ref, acc_pa, acc_pb, acc_o):
        @pl.when(pl.program_id(0) == 0)
        def _():
            dg_ref[...] = jnp.zeros_like(dg_ref)
            acc_pa[...] = jnp.zeros_like(acc_pa)
            acc_pb[...] = jnp.zeros_like(acc_pb)
            acc_o[...] = jnp.zeros_like(acc_o)

        dx1 = dx1_ref[...]
        mn, r = _rms_fwd(mix_ref[...])
        dg_ref[...] += jnp.sum(dx1 * mn, axis=0, keepdims=True)
        dmix = _rms_bwd(dx1, mn, r, g_ref[...]).astype(BF16)
        acc_o[...] += _tn(mg_ref[...], dmix)
        dmerged = _nt(dmix, wo_ref[...])
        a = ab_ref[:, 0:512]
        b = ab_ref[:, 512:1024]
        da = jnp.zeros((TOK_TILE, 512), F32)
        db = jnp.zeros((TOK_TILE, 512), F32)
        dyas, dybs = [], []
        for j in range(N_CHIP):
            blk = slice(j * 256, (j + 1) * 256)
            blk_b = slice(1024 + j * 256, 1024 + (j + 1) * 256)
            dm = dmerged[:, blk]
            ya = jnp.dot(a, wpa_ref[j], preferred_element_type=F32)
            yb = jnp.dot(b, wpb_ref[j], preferred_element_type=F32)
            sa = _sigmoid(gab_ref[:, blk])
            sb = _sigmoid(gab_ref[:, blk_b])
            dya = (dm * sa).astype(BF16)
            dyb = (dm * sb).astype(BF16)
            dyas.append(dya)
            dybs.append(dyb)
            dgab_ref[:, blk] = (dm * ya * sa * (1.0 - sa)).astype(BF16)
            dgab_ref[:, blk_b] = (dm * yb * sb * (1.0 - sb)).astype(BF16)
            da = da + _nt(dya, wpa_ref[j])
            db = db + _nt(dyb, wpb_ref[j])
        dab_ref[:, 0:512] = da
        dab_ref[:, 512:1024] = db
        acc_pa[...] += _tn(a, jnp.concatenate(dyas, axis=1))
        acc_pb[...] += _tn(b, jnp.concatenate(dybs, axis=1))

        @pl.when(pl.program_id(0) == N_TOK_TILE - 1)
        def _():
            dwo_ref[...] = acc_o[...].astype(BF16)
            for j in range(N_CHIP):
                dwpa_ref[j] = acc_pa[:, j * 256:(j + 1) * 256].astype(BF16)
                dwpb_ref[j] = acc_pb[:, j * 256:(j + 1) * 256].astype(BF16)

    whole = lambda shape: pl.BlockSpec(shape, lambda i: (0,) * len(shape))
    return pl.pallas_call(
        body, grid=(N_TOK_TILE,),
        in_specs=[_tok(D_MODEL), _tok(D_MODEL), _tok(1024), _tok(GAB_W), _tok(D_MODEL), _once((N_CHIP, 512, 256)),
                  _once((N_CHIP, 512, 256)), _once((D_MODEL, D_MODEL)), _once((1, D_MODEL))],
        out_specs=[_tok(GAB_W), _tok(1024), whole((1, D_MODEL)), whole((N_CHIP, 512, 256)), whole((N_CHIP, 512, 256)),
                   whole((D_MODEL, D_MODEL))],
        out_shape=[jax.ShapeDtypeStruct((SEQ, GAB_W), BF16), jax.ShapeDtypeStruct((SEQ, 1024), F32),
                   jax.ShapeDtypeStruct((1, D_MODEL), F32), jax.ShapeDtypeStruct((N_CHIP, 512, 256), BF16),
                   jax.ShapeDtypeStruct((N_CHIP, 512, 256), BF16), jax.ShapeDtypeStruct((D_MODEL, D_MODEL), BF16)],
        scratch_shapes=[pltpu.VMEM((512, D_MODEL), F32), pltpu.VMEM((512, D_MODEL), F32),
                        pltpu.VMEM((D_MODEL, D_MODEL), F32)],
        compiler_params=_params(("arbitrary",)), name="merge_bwd",
    )(dx1, mix, ab, gab, merged, w_pa4, w_pb4, w_o, g_post)


def _in_proj_bwd(dpm, dgab, x, dx1, g_pre, wt_in, rider=None, rider_ins=()):
    def body(dpm_ref, dgab_ref, x_ref, dx1_ref, g_ref, wt_ref, dx_ref, dg_ref):
        @pl.when(pl.program_id(0) == 0)
        def _():
            dg_ref[...] = jnp.zeros_like(dg_ref)

        dh = jnp.dot(dpm_ref[:, 0:PM_XM], wt_ref[0:IN_ALOW, :], preferred_element_type=F32)
        dh = dh + jnp.dot(dpm_ref[:, PM_XM:PM_AL], wt_ref[IN_XM:IN_GATES, :], preferred_element_type=F32)
        dh = dh + jnp.dot(dpm_ref[:, PM_AL:PM_W], wt_ref[IN_ALOW:IN_ALOW + 128, :], preferred_element_type=F32)
        dh = dh + jnp.dot(dgab_ref[...], wt_ref[IN_GATES:D_IN, :], preferred_element_type=F32)
        xn, r = _rms_fwd(x_ref[...])
        dg_ref[...] += jnp.sum(dh * xn, axis=0, keepdims=True)
        dx_ref[...] = dx1_ref[...] + _rms_bwd(dh, xn, r, g_ref[...])

    return _tiled_call(
        body, [_tok(PM_W), _tok(GAB_W), _tok(D_MODEL), _tok(D_MODEL), _once((1, D_MODEL)), _once((D_IN, D_MODEL))],
        [_tok(D_MODEL), pl.BlockSpec((1, D_MODEL), lambda i: (0, 0))],
        [jax.ShapeDtypeStruct((SEQ, D_MODEL), F32), jax.ShapeDtypeStruct((1, D_MODEL), F32)],
        (dpm, dgab, x, dx1, g_pre, wt_in), "in_proj_bwd", rider, rider_ins)


def _dw_in(dpm, dgab, h):
    n_pm = PM_AL // 512
    n_blk = n_pm + GAB_W // 512

    def body(dpm_ref, dgab_ref, dal_ref, h_ref, o_ref):
        i = pl.program_id(0)
        off = pl.multiple_of(i * 512 + 16 * (i >= 3).astype(jnp.int32), 16)

        @pl.when(i < n_pm)
        def _():
            o_ref[pl.ds(off, 512), :] = _tn(dpm_ref[...], h_ref[...]).astype(BF16)

        @pl.when(i >= n_pm)
        def _():
            o_ref[pl.ds(off, 512), :] = _tn(dgab_ref[...], h_ref[...]).astype(BF16)

        @pl.when(i == 0)
        def _():
            o_ref[IN_ALOW:IN_XM, :] = _tn(dal_ref[...], h_ref[...])[0:IN_XM - IN_ALOW].astype(BF16)

    return pl.pallas_call(
        body, grid=(n_blk,),
        in_specs=[pl.BlockSpec((SEQ, 512), lambda i: (0, jnp.minimum(i, n_pm - 1))),
                  pl.BlockSpec((SEQ, 512), lambda i: (0, jnp.maximum(i - n_pm, 0))),
                  pl.BlockSpec((SEQ, 128), lambda i: (0, PM_AL // 128)),
                  _once((SEQ, D_MODEL))],
        out_specs=pl.BlockSpec((D_IN, D_MODEL), lambda i: (0, 0)),
        out_shape=jax.ShapeDtypeStruct((D_IN, D_MODEL), BF16),
        compiler_params=_params(("arbitrary",)), name="dw_in",
    )(dpm, dgab, dpm, h)


def _tn_matmul(a, b, name, shards=1, tm=1024, rider=None, rider_ins=()):
    m, n = a.shape[1], b.shape[1]
    tm = min(tm, m)
    tn = n // shards if shards > 1 else min(n, 1024)
    steps_i, steps_j = m // tm, n // tn
    r_in, r_out_specs, r_out_shape, r_scratch = _rider_specs(rider, rider_ins)

    def body(*refs):
        (a_ref, b_ref), ride_in, (o_ref,), ride_out, scratch = _split(refs, 2, len(r_in), 1, len(r_out_specs), len(r_scratch))
        step = pl.program_id(0) * steps_j + pl.program_id(1)
        _ride(rider, ("first",), step == 0, ride_in, ride_out, scratch)
        o_ref[...] = _tn(a_ref[...], b_ref[...]).astype(BF16)
        _ride(rider, ("middle", "last"), step == steps_i * steps_j - 1, ride_in, ride_out, scratch)

    if shards > 1:
        out_spec = pl.BlockSpec((None, tm, tn), lambda i, j: (j, i, 0))
        out_shape = jax.ShapeDtypeStruct((shards, m, tn), BF16)
    else:
        out_spec = pl.BlockSpec((tm, tn), lambda i, j: (i, j))
        out_shape = jax.ShapeDtypeStruct((m, n), BF16)
    res = pl.pallas_call(
        body, grid=(steps_i, steps_j),
        in_specs=[pl.BlockSpec((SEQ, tm), lambda i, j: (0, i)), pl.BlockSpec((SEQ, tn), lambda i, j: (0, j))] + r_in,
        out_specs=[out_spec] + r_out_specs, out_shape=[out_shape] + r_out_shape, scratch_shapes=r_scratch,
        compiler_params=_params(("arbitrary", "arbitrary")), name=name,
    )(a, b, *rider_ins)
    return res[0] if rider is None else (res[0], res[1:])


MESH = pl.DeviceIdType.MESH
ANY = pl.BlockSpec(memory_space=pl.ANY)
VMEM_WHOLE = pl.BlockSpec(memory_space=pltpu.VMEM)

_BIG = ("w_in", "w_pa", "w_pb", "w_o", "w_up", "w_down")
_BIG_SHARD = {"w_in": (IN_SHARD, D_MODEL), "w_pa": (512, 256), "w_pb": (512, 256), "w_o": (256, D_MODEL),
              "w_up": (D_MODEL, 1024), "w_down": (1024, D_MODEL),
              "w_down_a": (512, D_MODEL), "w_down_b": (512, D_MODEL)}
_BIG_SPLIT = {"w_in": 1, "w_pa": 0, "w_pb": 0, "w_o": 0, "w_up": 0, "w_down": 0, "w_down_a": 0, "w_down_b": 0}


def _half(ref, e, name, lead=0, part=None):
    axis = _BIG_SPLIT[name]
    size = _BIG_SHARD[name][axis] // 2
    start = e * size
    if part is not None:
        size //= 2
        start = start + part * size
    start = pl.multiple_of(start, 128 if axis == 1 else 16)
    idx = [pl.ds(0, ref.shape[a]) for a in range(lead)]
    idx += [pl.ds(start, size), pl.ds(0, _BIG_SHARD[name][1])] if axis == 0 else [pl.ds(0, _BIG_SHARD[name][0]), pl.ds(start, size)]
    return ref.at[tuple(idx)]


def _half_shape(name):
    r, c = _BIG_SHARD[name]
    return (r // 2, c) if _BIG_SPLIT[name] == 0 else (r, c // 2)


def _remote(src, dst, send_sems, recv_sems, k, to):
    return pltpu.make_async_remote_copy(src_ref=src, dst_ref=dst, send_sem=send_sems.at[k], recv_sem=recv_sems.at[k],
                                        device_id=to, device_id_type=MESH)


def _mesh_place():
    x, y, c = lax.axis_index("x"), lax.axis_index("y"), lax.axis_index("c")
    return x, y, c, [(1 - x, y), (x, 1 - y), (1 - x, 1 - y)]


class _Gather:
    def __init__(self, names, small=(), middle_at=0.5):
        self.middle_at = middle_at
        self.names = tuple(names)
        self.nb = len(self.names)
        self.n = self.nb + len(small)
        self.n_sems = 8 * self.nb + 3 * len(small)
        self.n_flush = 6 * self.nb + len(small)
        self.out_shape = [jax.ShapeDtypeStruct((N_CHIP,) + _BIG_SHARD[nm], BF16) for nm in self.names]
        self.out_shape += [jax.ShapeDtypeStruct((N_CHIP,) + s.shape, s.dtype) for s in small]

    def _copies(self, ins, outs, ss, rs, k):
        x, y, c, _ = _mesh_place()
        name = self.names[k]
        me, xn, yn, dg = 2 * x + y, 2 * (1 - x) + y, 2 * x + (1 - y), 2 * (1 - x) + (1 - y)
        to_x, to_y, sibling = (1 - x, y, c), (x, 1 - y, c), (x, y, 1 - c)

        def region(slot, e, part=None):
            return _half(outs[k].at[slot], e, name, part=part)

        def copy(pair, src, dst, to):
            return _remote(src, dst, ss, rs, 8 * k + pair, to)

        mine = _half(ins[k], c, name)
        sent = [copy(0, mine, region(me, c), to_x), copy(1, mine, region(me, c), to_y),
                copy(2, region(xn, c, 0), region(xn, c, 0), to_y), copy(3, region(yn, c, 1), region(yn, c, 1), to_x),
                copy(4, region(xn, c), region(xn, c), sibling), copy(5, region(yn, c), region(yn, c), sibling),
                copy(6, region(dg, c, 0), region(dg, c, 0), sibling), copy(7, region(dg, c, 1), region(dg, c, 1), sibling)]
        landing = [region(xn, c), region(yn, c), region(dg, c, 0), region(dg, c, 1),
                   region(xn, 1 - c), region(yn, 1 - c), region(dg, 1 - c, 0), region(dg, 1 - c, 1)]
        received = [copy(pair, dst, dst, sibling) for pair, dst in enumerate(landing)]
        return sent, received

    def _small(self, ins, outs, ss, rs, k, j, peer, slot, c):
        return _remote(ins[k], outs[k].at[slot], ss, rs, 8 * self.nb + 3 * (k - self.nb) + j, (*peer, c))

    def flush(self, phase, lands, outs, fs):
        x, y, c, _ = _mesh_place()
        me, xn, yn, dg = 2 * x + y, 2 * (1 - x) + y, 2 * x + (1 - y), 2 * (1 - x) + (1 - y)

        def pieces(k):
            name = self.names[k]
            spots = [lambda r: r.at[me], lambda r: _half(r.at[xn], c, name), lambda r: _half(r.at[yn], c, name),
                     lambda r: _half(r.at[xn], 1 - c, name), lambda r: _half(r.at[yn], 1 - c, name), lambda r: r.at[dg]]
            return [pltpu.make_async_copy(spot(lands[k]), spot(outs[k]), fs.at[6 * k + t]) for t, spot in enumerate(spots)]

        ready = {"first": (0,), "middle": (1, 2), "late": (3, 4), "last": (5,)}[phase]
        for k in range(self.nb):
            cps = pieces(k)
            for t in ready:
                cps[t].start()
        if phase == "last":
            small = [pltpu.make_async_copy(lands[k], outs[k], fs.at[6 * self.nb + k - self.nb]) for k in range(self.nb, self.n)]
            for cp in small:
                cp.start()
            for k in range(self.nb):
                for cp in pieces(k):
                    cp.wait()
            for cp in small:
                cp.wait()

    def first(self, ins, outs, ss, rs):
        x, y, c, peers = _mesh_place()
        me = 2 * x + y
        for k in range(self.nb):
            sent, _ = self._copies(ins, outs, ss, rs, k)
            sent[0].start()
            sent[1].start()
        for k in range(self.nb, self.n):
            for j, peer in enumerate(peers):
                self._small(ins, outs, ss, rs, k, j, peer, me, c).start()
        for k in range(self.n):
            outs[k][me] = ins[k][...]

    def middle(self, ins, outs, ss, rs):
        for k in range(self.nb):
            sent, received = self._copies(ins, outs, ss, rs, k)
            for pair in (0, 1):
                received[pair].wait_recv()
                sent[2 + pair].start()
                sent[4 + pair].start()

    def late(self, ins, outs, ss, rs):
        for k in range(self.nb):
            _, received = self._copies(ins, outs, ss, rs, k)
            for pair in (4, 5):
                received[pair].wait_recv()

    def last(self, ins, outs, ss, rs):
        x, y, c, peers = _mesh_place()
        for k in range(self.nb):
            sent, received = self._copies(ins, outs, ss, rs, k)
            for pair in (2, 3):
                received[pair].wait_recv()
                sent[4 + pair].start()
        for k in range(self.nb):
            sent, received = self._copies(ins, outs, ss, rs, k)
            for pair in (6, 7):
                received[pair].wait_recv()
            for cp in sent:
                cp.wait_send()
        for k in range(self.nb, self.n):
            for j, (px, py) in enumerate(peers):
                self._small(ins, outs, ss, rs, k, j, (px, py), 2 * px + py, c).wait_recv()
                self._small(ins, outs, ss, rs, k, j, (px, py), 2 * x + y, c).wait_send()


def _run_alone(rider, ins, name):
    r_in, r_out_specs, r_out_shape, r_scratch = _rider_specs(rider, ins)

    def body(*refs):
        ride_in, ride_out, scratch = _split(refs, len(r_in), len(r_out_specs), len(r_scratch))
        _ride(rider, ("first", "middle", "last"), pl.program_id(0) == 0, ride_in, ride_out, scratch)

    return pl.pallas_call(
        body, grid=(1,), in_specs=r_in, out_specs=r_out_specs, out_shape=r_out_shape, scratch_shapes=r_scratch,
        compiler_params=_params(("arbitrary",)), name=name,
    )(*ins)


class _Presum:
    in_space = ANY

    def __init__(self, names, base=0):
        self.names = tuple(names)
        self.n = len(self.names)
        self.base = base
        self.n_sems = 3 * self.n
        self.out_shape = [jax.ShapeDtypeStruct((N_CHIP,) + _half_shape(nm), BF16) for nm in self.names]
        self.work_shape = self.out_shape + self.out_shape

    def _stage(self, ins, bufs, ss, k, e, which):
        n = self.n
        return pltpu.make_async_copy(_half(ins[k], e, self.names[k], lead=1), bufs[which * n + k],
                                     ss.at[self.base + which * n + k])

    def _give(self, bufs, ss, rs, k, sibling):
        return _remote(bufs[self.n + k], bufs[k], ss, rs, self.base + k, sibling)

    def first(self, ins, bufs, ss, rs):
        x, y, c, _ = _mesh_place()
        for k in range(self.n):
            self._stage(ins, bufs, ss, k, 1 - c, 1).start()
        for k in range(self.n):
            self._stage(ins, bufs, ss, k, c, 2).start()
        for k in range(self.n):
            self._stage(ins, bufs, ss, k, 1 - c, 1).wait()
            self._give(bufs, ss, rs, k, (x, y, 1 - c)).start()

    def middle(self, ins, bufs, ss, rs):
        pass

    def last(self, ins, bufs, ss, rs):
        x, y, c, _ = _mesh_place()
        for k in range(self.n):
            self._give(bufs, ss, rs, k, (x, y, 1 - c)).wait_recv()
            self._stage(ins, bufs, ss, k, c, 2).wait()

            @pl.loop(0, N_CHIP)
            def _(j):
                bufs[k][j] = (bufs[k][j].astype(F32) + bufs[2 * self.n + k][j].astype(F32)).astype(BF16)
        for k in range(self.n):
            self._give(bufs, ss, rs, k, (x, y, 1 - c)).wait_send()


class _ReduceRelay:
    middle_at = 0.75

    def __init__(self, names, base=0):
        self.names = tuple(names)
        self.n = len(self.names)
        self.base = base
        self.n_sems = 6 * self.n
        self.out_shape = [jax.ShapeDtypeStruct((N_CHIP,) + _half_shape(nm), BF16) for nm in self.names]
        quarter = [jax.ShapeDtypeStruct(self._part_shape(nm), BF16) for nm in self.names]
        self.work_shape = quarter + quarter

    @staticmethod
    def _part_shape(name):
        r, c = _half_shape(name)
        return (r // 2, c) if _BIG_SPLIT[name] == 0 else (r, c // 2)

    def _part(self, ref, name, p):
        r, c = self._part_shape(name)
        return ref.at[pl.ds(p * r, r), pl.ds(0, c)] if _BIG_SPLIT[name] == 0 else ref.at[pl.ds(0, r), pl.ds(p * c, c)]

    def _copies(self, ins, bufs, ss, rs, k):
        x, y, c, _ = _mesh_place()
        name, n = self.names[k], self.n
        me, xn, yn, dg = 2 * x + y, 2 * (1 - x) + y, 2 * x + (1 - y), 2 * (1 - x) + (1 - y)
        to_x, to_y = (1 - x, y, c), (x, 1 - y, c)
        mine = lambda slot, p: self._part(ins[k].at[slot], name, p)
        slot = lambda s, p: self._part(bufs[k].at[s], name, p)
        from_x, from_y = bufs[n + k], bufs[2 * n + k]

        def copy(pair, src, dst, to):
            return _remote(src, dst, ss, rs, self.base + 6 * k + pair, to)

        sent = [copy(0, mine(dg, 0), from_x, to_x), copy(1, mine(dg, 1), from_y, to_y),
                copy(2, mine(xn, 0), slot(me, 0), to_x), copy(3, mine(yn, 1), slot(me, 1), to_y),
                copy(4, from_y, slot(me, 1), to_x), copy(5, from_x, slot(me, 0), to_y)]
        landing = [from_x, from_y, slot(xn, 0), slot(yn, 1), slot(xn, 1), slot(yn, 0)]
        received = [copy(pair, dst, dst, to_x) for pair, dst in enumerate(landing)]
        return sent, received

    def first(self, ins, bufs, ss, rs):
        x, y, c, _ = _mesh_place()
        me, dg = 2 * x + y, 2 * (1 - x) + (1 - y)
        for k in range(self.n):
            sent, _ = self._copies(ins, bufs, ss, rs, k)
            for pair in range(4):
                sent[pair].start()
        for k in range(self.n):
            bufs[k][me] = ins[k][me]
            bufs[k][dg] = jnp.zeros(_half_shape(self.names[k]), BF16)

    def middle(self, ins, bufs, ss, rs):
        x, y, c, _ = _mesh_place()
        xn, yn = 2 * (1 - x) + y, 2 * x + (1 - y)
        for k in range(self.n):
            sent, received = self._copies(ins, bufs, ss, rs, k)
            name, n = self.names[k], self.n
            for pair, buf, own in ((0, bufs[n + k], self._part(ins[k].at[yn], name, 0)),
                                   (1, bufs[2 * n + k], self._part(ins[k].at[xn], name, 1))):
                received[pair].wait_recv()
                buf[...] = (buf[...].astype(F32) + own[...].astype(F32)).astype(BF16)
            sent[5].start()
            sent[4].start()

    def last(self, ins, bufs, ss, rs):
        for k in range(self.n):
            sent, received = self._copies(ins, bufs, ss, rs, k)
            for pair in range(2, 6):
                received[pair].wait_recv()
            for cp in sent:
                cp.wait_send()


class _PresumThenRelay:
    in_space = ANY
    middle_at = _ReduceRelay.middle_at

    def __init__(self, names):
        self.relay = _ReduceRelay(names)
        self.pre = _Presum(names, base=self.relay.n_sems)
        self.n_sems = self.relay.n_sems + self.pre.n_sems
        self.out_shape = self.relay.out_shape
        self.work_shape = list(self.relay.work_shape) + list(self.pre.out_shape) + list(self.pre.work_shape)
        self.n_relay = len(self.relay.out_shape) + len(self.relay.work_shape)

    def first(self, ins, bufs, ss, rs):
        self.pre.first(ins, bufs[self.n_relay:], ss, rs)

    def early(self, ins, bufs, ss, rs):
        self.pre.last(ins, bufs[self.n_relay:], ss, rs)
        self.relay.first(bufs[self.n_relay:], bufs[:self.n_relay], ss, rs)

    def middle(self, ins, bufs, ss, rs):
        self.relay.middle(bufs[self.n_relay:], bufs[:self.n_relay], ss, rs)

    def last(self, ins, bufs, ss, rs):
        self.relay.last(bufs[self.n_relay:], bufs[:self.n_relay], ss, rs)


class _SendPartials:
    def __init__(self, names, small_shape=None):
        self.n = len(names)
        self.small = small_shape is not None
        self.n_sems = 3 * self.n + 7
        self.out_shape = [jax.ShapeDtypeStruct((N_CHIP,) + _half_shape(nm), BF16) for nm in names]
        if self.small:
            self.out_shape.append(jax.ShapeDtypeStruct((N_DEV,) + small_shape, F32))

    def _piece(self, ins, outs, ss, rs, k, j, peer, src_slot, dst_slot, c):
        return _remote(ins[k].at[src_slot], outs[k].at[dst_slot], ss, rs, 3 * k + j, (*peer, c))

    def _small(self, ins, outs, ss, rs, r, other, slot):
        return _remote(ins[self.n], outs[self.n].at[slot], ss, rs, 3 * self.n + r, other)

    @staticmethod
    def _others(x, y, c):
        return [(x, y, 1 - c), (1 - x, y, c), (1 - x, y, 1 - c), (x, 1 - y, c), (x, 1 - y, 1 - c),
                (1 - x, 1 - y, c), (1 - x, 1 - y, 1 - c)]

    def first(self, ins, outs, ss, rs, only=None):
        x, y, c, peers = _mesh_place()
        me = 2 * x + y
        which = range(self.n) if only is None else only
        for k in which:
            for j, (px, py) in enumerate(peers):
                self._piece(ins, outs, ss, rs, k, j, (px, py), 2 * px + py, me, c).start()
        if self.small:
            for r, other in enumerate(self._others(x, y, c)):
                self._small(ins, outs, ss, rs, r, other, 4 * x + 2 * y + c).start()
            outs[self.n][4 * x + 2 * y + c] = ins[self.n][...]
        for k in which:
            outs[k][me] = ins[k][me]

    def middle(self, ins, outs, ss, rs):
        pass

    def last(self, ins, outs, ss, rs):
        x, y, c, peers = _mesh_place()
        me = 2 * x + y
        for k in range(self.n):
            for j, (px, py) in enumerate(peers):
                self._piece(ins, outs, ss, rs, k, j, (px, py), me, 2 * px + py, c).wait_recv()
                self._piece(ins, outs, ss, rs, k, j, (px, py), 2 * px + py, me, c).wait_send()
        if self.small:
            for r, (px, py, pc) in enumerate(self._others(x, y, c)):
                self._small(ins, outs, ss, rs, r, (px, py, pc), 4 * px + 2 * py + pc).wait_recv()
                self._small(ins, outs, ss, rs, r, (px, py, pc), 4 * x + 2 * y + c).wait_send()


class _PresumThenSend:
    def __init__(self, names):
        self.send = _SendPartials(names)
        self.pre = _Presum(names[-1:], base=self.send.n_sems)
        self.n = self.send.n
        self.n_sems = self.send.n_sems + self.pre.n_sems
        self.out_shape = self.send.out_shape
        self.work_shape = list(self.pre.out_shape) + list(self.pre.work_shape)
        self.in_space = [VMEM_WHOLE] * (self.n - 1) + [ANY]

    def _partials(self, ins, bufs):
        return list(ins[:self.n - 1]) + [bufs[self.n]]

    def first(self, ins, bufs, ss, rs):
        self.pre.first(ins[self.n - 1:], bufs[self.n:], ss, rs)
        self.send.first(ins, bufs[:self.n], ss, rs, only=range(self.n - 1))

    def early(self, ins, bufs, ss, rs):
        self.pre.last(ins[self.n - 1:], bufs[self.n:], ss, rs)
        self.send.first(self._partials(ins, bufs), bufs[:self.n], ss, rs, only=(self.n - 1,))

    def middle(self, ins, bufs, ss, rs):
        pass

    def last(self, ins, bufs, ss, rs):
        self.send.last(self._partials(ins, bufs), bufs[:self.n], ss, rs)


def _sum_swap(names, parts, small):
    n = len(parts)
    everyone = _SendPartials((), small.shape)

    def body(*refs):
        (p_hbm, (small_ref,), o_hbm, (osmall_ref,), p_refs, o_refs, (all_ref,),
         (send_sems, recv_sems, ss_small, rs_small, load_sems, leave_sems)) = _split(refs, n, 1, n, 1, n, n, 1, 6)
        x, y, c = lax.axis_index("x"), lax.axis_index("y"), lax.axis_index("c")
        loads = [pltpu.make_async_copy(p_hbm[k], p_refs[k], load_sems.at[k]) for k in range(n)]
        for cp in loads:
            cp.start()
        everyone.first([small_ref], [all_ref], ss_small, rs_small)

        def mine(k):
            part = _half(o_refs[k], c, names[k])
            return _remote(part, part, send_sems, recv_sems, k, (x, y, 1 - c))

        def leave(k, whose):
            e = c if whose == 0 else 1 - c
            return pltpu.make_async_copy(_half(o_refs[k], e, names[k]), _half(o_hbm[k], e, names[k]),
                                         leave_sems.at[2 * k + whose])

        for k in range(n):
            loads[k].wait()
            for e in range(2):
                @pl.when(c == e)
                def _():
                    g = p_refs[k][0].astype(F32)
                    for s in range(1, N_CHIP):
                        g = g + p_refs[k][s].astype(F32)
                    r, cols = _half_shape(names[k])
                    if _BIG_SPLIT[names[k]] == 0:
                        o_refs[k][e * r:(e + 1) * r, :] = g
                    else:
                        o_refs[k][:, e * cols:(e + 1) * cols] = g
            mine(k).start()
            leave(k, 0).start()
        for k in range(n):
            theirs = _half(o_refs[k], 1 - c, names[k])
            _remote(theirs, theirs, send_sems, recv_sems, k, (x, y, 1 - c)).wait_recv()
            leave(k, 1).start()
        everyone.last([small_ref], [all_ref], ss_small, rs_small)
        g = all_ref[0]
        for d in range(1, N_DEV):
            g = g + all_ref[d]
        osmall_ref[...] = g
        for k in range(n):
            mine(k).wait_send()
            leave(k, 0).wait()
            leave(k, 1).wait()

    shards = [jax.ShapeDtypeStruct(_BIG_SHARD[nm], F32) for nm in names]
    res = pl.pallas_call(
        body, in_specs=[ANY] * n + [VMEM_WHOLE], out_specs=[ANY] * n + [VMEM_WHOLE],
        out_shape=shards + [jax.ShapeDtypeStruct(small.shape, F32)],
        scratch_shapes=[pltpu.VMEM(q.shape, q.dtype) for q in parts] + [pltpu.VMEM(s.shape, s.dtype) for s in shards]
        + [pltpu.VMEM((N_DEV,) + small.shape, F32), pltpu.SemaphoreType.DMA((n,)), pltpu.SemaphoreType.DMA((n,)),
           pltpu.SemaphoreType.DMA((everyone.n_sems,)), pltpu.SemaphoreType.DMA((everyone.n_sems,)),
           pltpu.SemaphoreType.DMA((n,)), pltpu.SemaphoreType.DMA((2 * n,))],
        compiler_params=_params(), name="sum_swap",
    )(*parts, small)
    return res[:n], res[n]


def _tile(rows, cols, itemsize, budget):
    t = cols if rows % 16 else rows
    other = rows if rows % 16 else cols
    step = 256 if rows % 16 else 32
    while t % step == 0 and t * other * itemsize > budget:
        t //= 2
    return (rows, t) if rows % 16 else (t, cols)


def _adamw_math(w, g, m, v):
    m = ADAM_B1 * m + (1.0 - ADAM_B1) * g
    v = ADAM_B2 * v + (1.0 - ADAM_B2) * (g * g)
    m_hat = m / (1.0 - ADAM_B1 ** ADAM_STEP)
    v_hat = v / (1.0 - ADAM_B2 ** ADAM_STEP)
    delta = -ADAM_LR * (m_hat / (jnp.sqrt(v_hat) + ADAM_EPS) + ADAM_WD * w)
    return delta, m, v


def _adamw_big(g, w, m, v, name):
    r, c = w.shape
    tr, tc = _tile(r, c, 4, 2 * 1024 * 1024)

    def body(g_ref, w_ref, m_ref, v_ref, d_ref, nm_ref, nv_ref):
        d_ref[...], nm_ref[...], nv_ref[...] = _adamw_math(w_ref[...], g_ref[...], m_ref[...], v_ref[...])

    blk = pl.BlockSpec((tr, tc), lambda i, l: (i, l))
    return pl.pallas_call(
        body, grid=(r // tr, c // tc), in_specs=[blk, blk, blk, blk],
        out_specs=[blk, blk, blk], out_shape=[jax.ShapeDtypeStruct((r, c), F32)] * 3,
        compiler_params=_params(("arbitrary", "arbitrary")), name=name,
    )(g, w, m, v)


def _adamw_rows(g, w, m, v, name):
    r, k, lanes = w.shape
    tr = 296

    def body(g_ref, w_ref, m_ref, v_ref, g3_ref, d_ref, nm_ref, nv_ref):
        g = g_ref[...].reshape(tr, k, lanes)
        g3_ref[...] = g
        d_ref[...], nm_ref[...], nv_ref[...] = _adamw_math(w_ref[...], g, m_ref[...], v_ref[...])

    rows = pl.BlockSpec((tr, k, lanes), lambda i: (i, 0, 0))
    return pl.pallas_call(
        body, grid=(pl.cdiv(r, tr),), in_specs=[pl.BlockSpec((tr, k * lanes), lambda i: (i, 0)), rows, rows, rows],
        out_specs=[rows] * 4, out_shape=[jax.ShapeDtypeStruct((r, k, lanes), F32)] * 4,
        compiler_params=_params(("arbitrary",)), name=name,
    )(g, w, m, v)


def _adamw_small(ws, gs, ms, vs):
    n = len(ws)

    def body(*refs):
        w_refs, g_refs, m_refs, v_refs, d_refs, nm_refs, nv_refs = _split(refs, *([n] * 7))
        for k in range(n):
            d_refs[k][...], nm_refs[k][...], nv_refs[k][...] = _adamw_math(w_refs[k][...], g_refs[k][...], m_refs[k][...],
                                                                             v_refs[k][...])

    shapes = [jax.ShapeDtypeStruct(w.shape, F32) for w in ws]
    res = pl.pallas_call(body, out_shape=shapes * 3, name="adamw_small")(*ws, *gs, *ms, *vs)
    return res[:n], res[n:2 * n], res[2 * n:]


def _pack(arrs):
    flat = jnp.concatenate([a.reshape(-1) for a in arrs])
    rows = -(-flat.shape[0] // 1024) * 8
    return jnp.pad(flat, (0, rows * 128 - flat.shape[0])).reshape(rows, 128)


def _unpack(buf, shapes):
    flat = buf.reshape(-1)
    out, off = [], 0
    for s in shapes:
        size = 1
        for d in s:
            size *= d
        out.append(flat[off:off + size].reshape(s))
        off += size
    return out


def _block_rows(w):
    return jnp.pad(w.reshape(512, 4), ((0, 0), (0, 124)))


def _cols(a4):
    return jnp.transpose(a4, (1, 0, 2)).reshape(a4.shape[1], -1)


_LATE = ("w_pa", "w_pb", "w_o", "w_up", "w_down")
_RIDE_IN_PROJ = ("w_pa", "w_pb", "w_o", "w_down_b")
_RIDE_MIXER = ("w_up", "w_down_a")


def _full_weights(gathered):
    joined = {"w_in": (D_IN, D_MODEL), "w_o": (D_MODEL, D_MODEL)}
    return {n: (a.reshape(joined[n]) if n in joined else a) for n, a in gathered.items()}


def _local_step(x, target, w, sp, late_shards=None):
    sp = {n: (a.reshape(1, -1) if a.ndim == 1 else a) for n, a in sp.items()}
    wau = jnp.pad(sp["w_a_up"], ((0, 112), (0, 0)))
    wif = jnp.pad(sp["w_if"], ((0, 0), (0, 120)))
    bif = jnp.pad(sp["b_if"], ((0, 0), (0, 120)))
    p = {"wau": wau, "bau": sp["b_a_up"], "ggla": sp["g_gla_norm"], "cw": sp["conv_w"], "cb": sp["conv_b"],
         "wq": _block_rows(sp["w_q_ml"]), "wk": _block_rows(sp["w_k_ml"]), "wv": _block_rows(sp["w_v_ml"]),
         "wif": wif, "bif": bif, "skip": sp["ml_skip"], "gml": sp["g_ml_norm"]}

    if late_shards is None:
        (pm, gab, h), _ = _in_proj(x, sp["g_pre_mix"], w["w_in"])
        ab, *states = _mixer_fwd(pm, p)
        (x1, mix, merged), _ = _merge_fwd(ab, gab, x, w["w_pa"], w["w_pb"], w["w_o"], sp["g_post_mix"])
    else:
        shard = dict(zip(_LATE, late_shards))
        shard["w_down_a"], shard["w_down_b"] = shard["w_down"][0:512], shard["w_down"][512:1024]
        (pm, gab, h), got = _in_proj(x, sp["g_pre_mix"], w["w_in"], _Gather(_RIDE_IN_PROJ, middle_at=0.7),
                                     [shard[n] for n in _RIDE_IN_PROJ])
        w = dict(w, **_full_weights(dict(zip(_RIDE_IN_PROJ, got))))
        ab, *rest = _mixer_fwd(pm, p, _Gather(_RIDE_MIXER, middle_at=0.62), [shard[n] for n in _RIDE_MIXER])
        states = rest[:4]
        w.update(_full_weights(dict(zip(_RIDE_MIXER, rest[4:]))))
        (x1, mix, merged), _ = _merge_fwd(ab, gab, x, w["w_pa"], w["w_pb"], w["w_o"], sp["g_post_mix"])
    dx1, u, dd, h2, dpre, dg_post_mlp, dg_pre_mlp, loss = _mlp(x1, target, sp["g_pre_mlp"], sp["g_post_mlp"],
                                                                w["w_up"], w["w_down_a"], w["w_down_b"])
    dgab, dab, dg_post_mix, dw_pa, dw_pb, dw_o = _merge_bwd(dx1, mix, ab, gab, merged, w["w_pa"], w["w_pb"], w["w_o"],
                                                            sp["g_post_mix"])
    big = {"w_pa": dw_pa, "w_pb": dw_pb, "w_o": dw_o, "w_up": _tn_matmul(h2, dpre, "dw_up", shards=N_CHIP)}
    if late_shards is None:
        big["w_down"] = _tn_matmul(u, dd, "dw_down")
        dpm, dp, _ = _mixer_bwd(pm, dab, states, p)
    else:
        pieces = lambda n: big[n].reshape((N_CHIP,) + _BIG_SHARD[n])
        big["w_down"], partial = _tn_matmul(u, dd, "dw_down", rider=_Presum(_LATE[:4]),
                                            rider_ins=[pieces(n) for n in _LATE[:4]])
        dpm, dp, parts = _mixer_bwd(pm, dab, states, p, _PresumThenSend(_LATE), list(partial) + [pieces("w_down")])
        big = dict(zip(_LATE, parts))
    big["w_in"] = _dw_in(dpm, dgab, h)
    if late_shards is None:
        (dx, dg_pre_mix), _ = _in_proj_bwd(dpm, dgab, x, dx1, sp["g_pre_mix"], w["w_in"])
    else:
        (dx, dg_pre_mix), parts = _in_proj_bwd(dpm, dgab, x, dx1, sp["g_pre_mix"], w["w_in"], _PresumThenRelay(("w_in",)),
                                               [big["w_in"].reshape((N_CHIP,) + _BIG_SHARD["w_in"])])
        big["w_in"] = parts[0]
    small = {
        "g_pre_mix": dg_pre_mix, "b_a_up": dp["bau"], "g_gla_norm": dp["ggla"], "conv_b": dp["cb"],
        "w_q_ml": dp["wq"][:, 0:4].reshape(128, 4, 4), "w_k_ml": dp["wk"][:, 0:4].reshape(128, 4, 4),
        "w_v_ml": dp["wv"][:, 0:4].reshape(128, 4, 4),
        "b_if": dp["bif"][:, 0:8], "ml_skip": dp["skip"], "g_ml_norm": dp["gml"], "g_post_mix": dg_post_mix,
        "g_pre_mlp": dg_pre_mlp, "g_post_mlp": dg_post_mlp, "w_a_up": dp["wau"][0:16], "conv_w": dp["cw"],
        "w_if": dp["wif"][:, 0:8], "loss": loss[:, 0:1],
    }
    return dx, big, small


_SMALL_REPL = ("g_pre_mix", "b_a_up", "g_gla_norm", "conv_b", "w_q_ml", "w_k_ml", "w_v_ml", "b_if", "ml_skip",
               "g_ml_norm", "g_post_mix", "g_pre_mlp", "g_post_mlp")
_SMALL_SHARDED = ("w_a_up", "conv_w", "w_if")
_SMALL_ORDER = _SMALL_REPL + _SMALL_SHARDED + ("loss",)
_WEIGHTS = ("g_pre_mix", "w_in", "w_a_up", "b_a_up", "g_gla_norm", "conv_w", "conv_b", "w_q_ml", "w_k_ml", "w_v_ml",
            "w_if", "b_if", "ml_skip", "g_ml_norm", "w_pa", "w_pb", "w_o", "g_post_mix", "g_pre_mlp", "w_up", "w_down",
            "g_post_mlp")


_BLOCK_WEIGHTS = ("w_q_ml", "w_k_ml", "w_v_ml")


def _stored(name, a):
    if name in _BLOCK_WEIGHTS:
        return jnp.transpose(a, (0, 2, 3, 1)).reshape(16, 128)
    if name == "w_if":
        return jnp.transpose(a, (0, 2, 1)).reshape(8, 384)
    return a


def _unstored(name, a):
    if name in _BLOCK_WEIGHTS:
        return jnp.transpose(a.reshape(1, 4, 4, 128), (0, 3, 1, 2))
    if name == "w_if":
        return jnp.transpose(a.reshape(1, 8, 384), (0, 2, 1))
    return a


def _as_shard(name, a):
    return jnp.transpose(a, (2, 0, 1)).reshape(IN_SHARD, D_MODEL // 128, 128) if name == "w_in" else a[0]


def _in_shard_bf16(w_in):
    return jnp.transpose(w_in.astype(BF16), (2, 0, 1)).reshape(IN_SHARD, D_MODEL)


def _from_shard(name, a):
    return jnp.transpose(a, (1, 2, 0)).reshape(1, D_MODEL, IN_SHARD) if name == "w_in" else a[None]


def kernel(x, g_pre_mix, w_in, w_a_up, b_a_up, g_gla_norm, conv_w, conv_b, w_q_ml, w_k_ml, w_v_ml, w_if, b_if, ml_skip, g_ml_norm, w_pa, w_pb, w_o, g_post_mix, g_pre_mlp, w_up, w_down, g_post_mlp, loss_target, m_g_pre_mix, m_w_in, m_w_a_up, m_b_a_up, m_g_gla_norm, m_conv_w, m_conv_b, m_w_q_ml, m_w_k_ml, m_w_v_ml, m_w_if, m_b_if, m_ml_skip, m_g_ml_norm, m_w_pa, m_w_pb, m_w_o, m_g_post_mix, m_g_pre_mlp, m_w_up, m_w_down, m_g_post_mlp, v_g_pre_mix, v_w_in, v_w_a_up, v_b_a_up, v_g_gla_norm, v_conv_w, v_conv_b, v_w_q_ml, v_w_k_ml, v_w_v_ml, v_w_if, v_b_if, v_ml_skip, v_g_ml_norm, v_w_pa, v_w_pb, v_w_o, v_g_post_mix, v_g_pre_mlp, v_w_up, v_w_down, v_g_post_mlp):
    args = dict(locals())
    wts = {n: _as_shard(n, args[n]) for n in _WEIGHTS}
    mom = {n: _as_shard(n, args["m_" + n]) for n in _WEIGHTS}
    var = {n: _as_shard(n, args["v_" + n]) for n in _WEIGHTS}
    chip = 2 * lax.axis_index("x") + lax.axis_index("y")

    first = ("w_in",) + _SMALL_SHARDED
    gathered = dict(zip(first, _run_alone(_Gather(("w_in",), [wts[n] for n in _SMALL_SHARDED]),
                                          [_in_shard_bf16(w_in)] + [wts[n] for n in _SMALL_SHARDED],
                                          "gather_first")))
    sp = {n: wts[n] for n in _SMALL_REPL}
    sp["w_a_up"] = _cols(gathered["w_a_up"])
    sp["conv_w"] = _cols(gathered["conv_w"])
    sp["w_if"] = gathered["w_if"].reshape(1536, 8)

    dx, big, small = _local_step(x[0], loss_target[0], _full_weights({"w_in": gathered["w_in"]}), sp,
                                 late_shards=[wts[n].astype(BF16) for n in _LATE])

    small_shapes = [small[n].shape for n in _SMALL_ORDER]
    packed = _pack([small[n] for n in _SMALL_ORDER])
    sums, small_sum = _sum_swap(_BIG, [big[n] for n in _BIG], packed)

    grads, delta, new_m, new_v = {}, {}, {}, {}
    for n, g in zip(_BIG, sums):
        if n == "w_in":
            g, d, nm, nv = _adamw_rows(g, wts[n], mom[n], var[n], "adamw_" + n)
        else:
            d, nm, nv = _adamw_big(g, wts[n], mom[n], var[n], "adamw_" + n)
        grads[n], delta[n], new_m[n], new_v[n] = (_from_shard(n, a) for a in (g, d, nm, nv))
    summed = dict(zip(_SMALL_ORDER, _unpack(small_sum, small_shapes)))
    loss = summed["loss"].reshape(())
    summed["w_a_up"] = lax.dynamic_slice_in_dim(summed["w_a_up"], chip * 64, 64, axis=1)
    summed["conv_w"] = lax.dynamic_slice_in_dim(summed["conv_w"], chip * 128, 128, axis=1)
    summed["w_if"] = lax.dynamic_slice_in_dim(summed["w_if"], chip * 384, 384, axis=0)
    small_names = _SMALL_REPL + _SMALL_SHARDED
    g_stored = [_stored(n, summed[n].reshape(args[n].shape)) for n in small_names]
    upd = _adamw_small([_stored(n, args[n]) for n in small_names], g_stored,
                       [_stored(n, args["m_" + n]) for n in small_names], [_stored(n, args["v_" + n]) for n in small_names])
    for dst, arrs in zip((grads, delta, new_m, new_v), (g_stored,) + tuple(upd)):
        dst.update({n: _unstored(n, a) for n, a in zip(small_names, arrs)})

    outs = [loss, dx[None]]
    for group in (grads, delta, new_m, new_v):
        outs += [group[n] for n in _WEIGHTS]
    return tuple(outs)
```

```python
import functools

import jax
import jax.numpy as jnp
from jax import lax
from jax.experimental import pallas as pl
from jax.experimental.pallas import tpu as pltpu

F32 = jnp.float32
BF16 = jnp.bfloat16

SEQ = 2048
D_MODEL = 1024
CHUNK = 64
N_CHUNK = SEQ // CHUNK
HEADS = 4
GLA_DK = 64
GLA_DV = 128
ML_DH = 128
D_FF = 4096
EPS = 1e-6
N_CHIP = 4
N_DEV = 8
TOK_TILE = 256
N_TOK_TILE = SEQ // TOK_TILE
SWEEP = 2
assert CHUNK == 64
N_SWEEP = N_CHUNK // SWEEP

PM_W = 2688
PM_XM = 1536
PM_OP = 2048
PM_AL = 2560
GAB_W = 2048
D_IN = 4624
IN_SHARD = D_IN // N_CHIP
IN_ALOW = 1536
IN_XM = 1552
IN_GATES = 2576

ADAM_LR = 0.001
ADAM_B1 = 0.9
ADAM_B2 = 0.999
ADAM_EPS = 1e-08
ADAM_WD = 0.01
ADAM_STEP = 10

VMEM_LIMIT = 56 * 1024 * 1024


def _params(sem=None):
    return pltpu.CompilerParams(dimension_semantics=sem, vmem_limit_bytes=VMEM_LIMIT)


def _dot(a, b, ca, cb):
    return lax.dot_general(a.astype(BF16), b.astype(BF16), (((ca,), (cb,)), ((), ())), preferred_element_type=F32)


def _pmm_nn(a, b):
    return _dot(a, b, 1, 0)


def _pmm_nt(a, b):
    return _dot(a, b, 1, 1)


def _pmm_tn(a, b):
    return _dot(a, b, 0, 0)


def _pcmm(c, x):
    return lax.dot_general(c, x, (((1,), (0,)), ((), ())), precision=lax.Precision.HIGHEST, preferred_element_type=F32)


@jax.custom_vjp
def _mm_nn(a, b):
    return _dot(a, b, 1, 0)


@jax.custom_vjp
def _mm_nt(a, b):
    return _dot(a, b, 1, 1)


@jax.custom_vjp
def _mm_tn(a, b):
    return _dot(a, b, 0, 0)


_mm_nn.defvjp(lambda a, b: (_dot(a, b, 1, 0), (a, b)), lambda r, g: (_mm_nt(g, r[1]), _mm_tn(r[0], g)))
_mm_nt.defvjp(lambda a, b: (_dot(a, b, 1, 1), (a, b)), lambda r, g: (_mm_nn(g, r[1]), _mm_tn(g, r[0])))
_mm_tn.defvjp(lambda a, b: (_dot(a, b, 0, 0), (a, b)), lambda r, g: (_mm_nt(r[1], g), _mm_nn(r[0], g)))


@jax.custom_vjp
def _cmm(c, x):
    return _pcmm(c, x)


_cmm.defvjp(
    lambda c, x: (_pcmm(c, x), c),
    lambda c, g: (jnp.zeros_like(c), lax.dot_general(c, g, (((0,), (0,)), ((), ())), precision=lax.Precision.HIGHEST,
                                                      preferred_element_type=F32)),
)

_PLAIN_OPS = (_pmm_nn, _pmm_nt, _pmm_tn, _pcmm)
_VJP_OPS = (_mm_nn, _mm_nt, _mm_tn, _cmm)


def _sigmoid(x):
    return 0.5 * (jnp.tanh(0.5 * x) + 1.0)


def _log_sigmoid(x):
    return jnp.minimum(x, 0.0) - jnp.log(1.0 + jnp.exp(-jnp.abs(x)))


def _mean(x):
    return jnp.mean(x, axis=-1, keepdims=True)


def _nt(a, b):
    return lax.dot_general(a, b, (((1,), (1,)), ((), ())), preferred_element_type=F32)


def _tn(a, b):
    return lax.dot_general(a, b, (((0,), (0,)), ((), ())), preferred_element_type=F32)


def _mixer_chunk(ops, p, st, pm, xprev8):
    mm_nn, mm_nt, mm_tn, cmm = ops
    n_rows = pm.shape[0]
    n_ch = n_rows // CHUNK
    row = lax.broadcasted_iota(jnp.int32, (n_rows, n_rows), 0)
    col = lax.broadcasted_iota(jnp.int32, (n_rows, n_rows), 1)
    tri = jnp.logical_and((row >> 6) == (col >> 6), row >= col).astype(F32)
    causal = tri[0:CHUNK, 0:CHUNK] > 0.0
    q = pm[:, 0:256]
    k = pm[:, 256:512]
    v = pm[:, 512:1024]
    g = pm[:, 1024:1536]
    xm = pm[:, PM_XM:PM_XM + 512]
    opre = pm[:, PM_OP:PM_OP + 512]
    alow = pm[:, PM_AL:PM_AL + 128]
    hs = range(HEADS)
    cs = range(n_ch)
    pairs = [(i, h) for i in cs for h in hs]
    rs = [slice(i * CHUNK, (i + 1) * CHUNK) for i in cs]
    last = [slice((i + 1) * CHUNK - 1, (i + 1) * CHUNK) for i in cs]
    s6 = [slice(h * GLA_DK, (h + 1) * GLA_DK) for h in hs]
    s12 = [slice(h * 128, (h + 1) * 128) for h in hs]

    xx = jnp.concatenate([xprev8, xm], axis=0)
    pre = p["cb"]
    for j in range(4):
        pre = pre + p["cw"][j:j + 1, :] * xx[5 + j:5 + j + n_rows, :]
    xc = pre * _sigmoid(pre)
    qm = [mm_nn(xc[:, s12[h]], p["wq"][h]) for h in hs]
    km = [mm_nn(xc[:, s12[h]], p["wk"][h]) for h in hs]
    vm = [mm_nn(xm[:, s12[h]], p["wv"][h]) for h in hs]
    qcat = jnp.concatenate(qm, axis=1)
    kcat = jnp.concatenate(km, axis=1)
    vcat = jnp.concatenate(vm, axis=1)
    gates = (mm_nn(qcat, p["wif"][0:512]) + mm_nn(kcat, p["wif"][512:1024]) + mm_nn(vcat, p["wif"][1024:1536])
             + p["bif"])
    lf = _log_sigmoid(gates)
    fc = cmm(tri, lf)
    gates_t = gates.T
    fc_t = fc.T

    la = _log_sigmoid(mm_nn(alow, p["wau"]) + p["bau"]) * (1.0 / 16.0)
    cum = cmm(tri, la)
    cum_last = [cum[last[i], :] for i in cs]
    to_end = jnp.concatenate([cum_last[i] - cum[rs[i], :] for i in cs], axis=0)
    e_pos = jnp.exp(cum)
    e_neg = jnp.exp(-cum)
    qs = q * (GLA_DK ** -0.5)
    qp = qs * e_pos
    qn = qs * e_neg
    kp = k * e_pos
    kn = k * e_neg
    kl = k * jnp.exp(to_end)
    dec = [jnp.exp(cum_last[i]) for i in cs]
    ks = [km[h] * (ML_DH ** -0.5) for h in hs]
    li_c = {(i, h): gates[rs[i], h:h + 1] for i, h in pairs}
    fc_c = {(i, h): fc[rs[i], 4 + h:5 + h] for i, h in pairs}
    f_last = {(i, h): fc[last[i], 4 + h:5 + h] for i, h in pairs}

    a_fwd = {(i, h): mm_nt(qp[rs[i], s6[h]], kn[rs[i], s6[h]]) for i, h in pairs}
    a_bwd = {(i, h): mm_nt(qn[rs[i], s6[h]], kp[rs[i], s6[h]]) for i, h in pairs}
    s_chunk = {(i, h): mm_tn(v[rs[i], s12[h]], kl[rs[i], s6[h]]) for i, h in pairs}
    qk = {(i, h): mm_nt(qm[h][rs[i]], ks[h][rs[i]]) for i, h in pairs}
    a = {ih: f_last[ih] - fc_c[ih] + li_c[ih] for ih in pairs}
    m_loc = {ih: jnp.max(a[ih], axis=0, keepdims=True) for ih in pairs}
    kw = {(i, h): ks[h][rs[i]] * jnp.exp(a[(i, h)] - m_loc[(i, h)]) for i, h in pairs}
    c_chunk = {(i, h): mm_tn(kw[(i, h)], vm[h][rs[i]]) for i, h in pairs}
    mem = {(0, h): st["S"][h] for h in hs}
    c_in = {(0, h): st["C"][h] for h in hs}
    n_in = {(0, h): st["n"][h] for h in hs}
    m_in = {(0, h): st["m"][h][:, 0:1] for h in hs}
    for i, h in pairs:
        mem[(i + 1, h)] = mem[(i, h)] * dec[i][:, s6[h]] + s_chunk[(i, h)]
        m_nx = jnp.maximum(f_last[(i, h)] + m_in[(i, h)], m_loc[(i, h)])
        sp = jnp.exp(f_last[(i, h)] + m_in[(i, h)] - m_nx)
        sl = jnp.exp(m_loc[(i, h)] - m_nx)
        c_in[(i + 1, h)] = sp * c_in[(i, h)] + sl * c_chunk[(i, h)]
        n_in[(i + 1, h)] = sp * n_in[(i, h)] + sl * jnp.sum(kw[(i, h)], axis=0, keepdims=True)
        m_in[(i + 1, h)] = m_nx
    s_new = [mem[(n_ch, h)] for h in hs]
    o_inter = {(i, h): mm_nt(qp[rs[i], s6[h]], mem[(i, h)]) for i, h in pairs}
    q_c = {(i, h): mm_nn(qm[h][rs[i]], c_in[(i, h)]) for i, h in pairs}
    scores = {ih: jnp.where(causal, a_fwd[ih], a_bwd[ih]) for ih in pairs}
    log_d = {(i, h): gates_t[h:h + 1, rs[i]] - jnp.abs(fc_c[(i, h)] - fc_t[4 + h:5 + h, rs[i]]) for i, h in pairs}
    g_int = {ih: fc_c[ih] + m_in[ih] for ih in pairs}
    m_t = {ih: jnp.maximum(g_int[ih], jnp.max(log_d[ih], axis=1, keepdims=True)) for ih in pairs}
    s = {ih: qk[ih] * jnp.exp(log_d[ih] - m_t[ih]) for ih in pairs}
    scl = {ih: jnp.exp(g_int[ih] - m_t[ih]) for ih in pairs}
    o = {(i, h): mm_nn(scores[(i, h)], v[rs[i], s12[h]]) + o_inter[(i, h)] for i, h in pairs}
    num = {(i, h): mm_nn(s[(i, h)], vm[h][rs[i]]) + scl[(i, h)] * q_c[(i, h)] for i, h in pairs}
    o = {ih: o[ih] * lax.rsqrt(_mean(o[ih] * o[ih]) + EPS) * p["ggla"] for ih in pairs}
    gate = g * _sigmoid(g)
    out_a = {(i, h): o[(i, h)] * gate[rs[i], s12[h]] for i, h in pairs}
    den = {(i, h): jnp.sum(s[(i, h)], axis=1, keepdims=True)
           + scl[(i, h)] * jnp.sum(qm[h][rs[i]] * n_in[(i, h)], axis=1, keepdims=True) for i, h in pairs}
    den = {ih: jnp.maximum(jnp.abs(den[ih]), jnp.exp(-m_t[ih])) for ih in pairs}
    open_gate = _sigmoid(opre)
    hc = {(i, h): num[(i, h)] / den[(i, h)] * open_gate[rs[i], s12[h]] for i, h in pairs}
    d0 = {ih: hc[ih] - _mean(hc[ih]) for ih in pairs}
    y = {ih: d0[ih] * lax.rsqrt(_mean(d0[ih] * d0[ih]) + EPS) for ih in pairs}
    skipped = p["skip"] * xc
    out_b = {(i, h): y[(i, h)] * p["gml"][:, s12[h]] + skipped[rs[i], s12[h]] for i, h in pairs}
    ab = jnp.concatenate([jnp.concatenate([out_a[(i, h)] for h in hs] + [out_b[(i, h)] for h in hs], axis=1) for i in cs],
                         axis=0)
    new = {"S": s_new, "C": [c_in[(n_ch, h)] for h in hs], "n": [n_in[(n_ch, h)] for h in hs],
           "m": [jnp.broadcast_to(m_in[(n_ch, h)], (1, ML_DH)) for h in hs]}
    return ab, new


_P_NAMES = ("wau", "bau", "ggla", "cw", "cb", "wq", "wk", "wv", "wif", "bif", "skip", "gml")
_P_SHAPES = {
    "wau": (128, 256), "bau": (1, 256), "ggla": (1, 128), "cw": (4, 512), "cb": (1, 512),
    "wq": (512, 128), "wk": (512, 128), "wv": (512, 128),
    "wif": (1536, 128), "bif": (1, 128), "skip": (1, 512), "gml": (1, 512),
}
_P_BLOCKDIAG = ("wq", "wk", "wv")
_S_NAMES = ("S", "C", "n", "m")
_S_SHAPES = {"S": (HEADS, GLA_DV, GLA_DK), "C": (HEADS, ML_DH, ML_DH), "n": (HEADS, 1, ML_DH), "m": (HEADS, 1, ML_DH)}


def _per_head(ref):
    return [ref[h] for h in range(HEADS)]


def _block_mask():
    r = lax.broadcasted_iota(jnp.int32, (128, 128), 0)
    c = lax.broadcasted_iota(jnp.int32, (128, 128), 1)
    same_block = (r >> 2) == (c >> 2)
    spread = jnp.logical_and(r < 4, (c & 3) == r)
    return same_block.astype(F32), spread.astype(F32)


def _expand_blockdiag(w_ref, dense_ref):
    same_block, spread = _block_mask()
    for h in range(HEADS):
        tiled = _pmm_nn(w_ref[h * 128:(h + 1) * 128, :], spread)
        dense_ref[h] = tiled * same_block


def _collect_blockdiag(ddense_ref, dw_ref):
    same_block, spread = _block_mask()
    for h in range(HEADS):
        dw_ref[h * 128:(h + 1) * 128, :] = lax.dot_general(
            ddense_ref[h] * same_block, spread, (((1,), (1,)), ((), ())), precision=lax.Precision.HIGHEST,
            preferred_element_type=F32)


def _const_spec(shape):
    zeros = (0,) * len(shape)
    return pl.BlockSpec(shape, lambda i: zeros)


def _split(refs, *counts):
    out, at = [], 0
    for c in counts:
        out.append(refs[at:at + c])
        at += c
    assert at == len(refs)
    return out


def _ride(rider, phases, cond, ins, outs, sems):
    if rider is None or not any(hasattr(rider, phase) for phase in phases):
        return
    lands, (send_sems, recv_sems, flush_sems) = sems[:-3], sems[-3:]

    @pl.when(cond)
    def _():
        for phase in phases:
            if phase == "last" and hasattr(rider, "late"):
                rider.late(ins, lands, send_sems, recv_sems)
                rider.flush("late", lands, outs, flush_sems)
            getattr(rider, phase)(ins, lands, send_sems, recv_sems)
            if hasattr(rider, "flush"):
                rider.flush(phase, lands, outs, flush_sems)
        if "last" in phases and not hasattr(rider, "flush"):
            flush = [pltpu.make_async_copy(lands[k], outs[k], flush_sems.at[k]) for k in range(len(outs))]
            for cp in flush:
                cp.start()
            for cp in flush:
                cp.wait()


def _middle_step(rider, n_steps):
    return min(n_steps - 2, int(getattr(rider, "middle_at", 1.0) * n_steps))


def _rider_specs(rider, rider_ins):
    if rider is None:
        return [], [], [], []
    scratch = [pltpu.VMEM(s.shape, s.dtype) for s in list(rider.out_shape) + list(getattr(rider, "work_shape", ()))]
    scratch += [pltpu.SemaphoreType.DMA((rider.n_sems,)), pltpu.SemaphoreType.DMA((rider.n_sems,)),
                pltpu.SemaphoreType.DMA((getattr(rider, "n_flush", len(rider.out_shape)),))]
    in_space = getattr(rider, "in_space", VMEM_WHOLE)
    in_specs = list(in_space) if isinstance(in_space, (list, tuple)) else [in_space] * len(rider_ins)
    return in_specs, [ANY] * len(rider.out_shape), list(rider.out_shape), scratch


def _mixer_fwd(pm, p, rider=None, rider_ins=()):
    n_p = len(_P_NAMES)
    r_in, r_out_specs, r_out_shape, r_sems = _rider_specs(rider, rider_ins)

    def body(*refs):
        (pm_ref, xprev_ref), p_list, ride_in, (ab_ref,), so_refs, ride_out, sc_refs, dense_list, sems = _split(
            refs, 2, n_p, len(r_in), 1, 4, len(r_out_specs), 4, 3, len(r_sems))
        p_refs = dict(zip(_P_NAMES, p_list))
        dense = dict(zip(_P_BLOCKDIAG, dense_list))
        n = pl.program_id(0)
        _ride(rider, ("first",), n == 0, ride_in, ride_out, sems)

        @pl.when(n == 0)
        def _():
            for r in sc_refs:
                r[...] = jnp.zeros_like(r)
            for nm in _P_BLOCKDIAG:
                _expand_blockdiag(p_refs[nm], dense[nm])

        st = {name: _per_head(r) for name, r in zip(_S_NAMES, sc_refs)}
        pv = {nm: (_per_head(dense[nm]) if nm in _P_BLOCKDIAG else p_refs[nm][...]) for nm in _P_NAMES}
        for name, r in zip(_S_NAMES, so_refs):
            for h in range(HEADS):
                r[0, h] = st[name][h]
        xprev8 = jnp.where(n > 0, xprev_ref[CHUNK - 8:CHUNK, :], 0.0)
        ab, st = _mixer_chunk(_PLAIN_OPS, pv, st, pm_ref[...], xprev8)
        ab_ref[...] = ab.astype(BF16)
        for name, r in zip(_S_NAMES, sc_refs):
            for h in range(HEADS):
                r[h] = st[name][h]
        _ride(rider, ("middle",), n == _middle_step(rider, N_SWEEP), ride_in, ride_out, sems)
        _ride(rider, ("last",), n == N_SWEEP - 1, ride_in, ride_out, sems)

    in_specs = [pl.BlockSpec((SWEEP * CHUNK, PM_W), lambda i: (i, 0)),
                pl.BlockSpec((CHUNK, 512), lambda i: (jnp.maximum(SWEEP * i - 1, 0), PM_XM // 512))]
    in_specs += [_const_spec(_P_SHAPES[nm]) for nm in _P_NAMES] + r_in
    out_specs = [pl.BlockSpec((SWEEP * CHUNK, 1024), lambda i: (i, 0))]
    out_shape = [jax.ShapeDtypeStruct((SEQ, 1024), BF16)]
    for nm in _S_NAMES:
        shp = _S_SHAPES[nm]
        out_specs.append(pl.BlockSpec((1,) + shp, lambda i: (i, 0, 0, 0)))
        out_shape.append(jax.ShapeDtypeStruct((N_SWEEP,) + shp, F32))
    return pl.pallas_call(
        body, grid=(N_SWEEP,), in_specs=in_specs, out_specs=out_specs + r_out_specs, out_shape=out_shape + r_out_shape,
        scratch_shapes=[pltpu.VMEM(_S_SHAPES[nm], F32) for nm in _S_NAMES]
        + [pltpu.VMEM((HEADS, 128, 128), F32) for _ in _P_BLOCKDIAG] + r_sems,
        compiler_params=_params(("arbitrary",)), name="mixer_fwd",
    )(pm, pm, *[p[nm] for nm in _P_NAMES], *rider_ins)


def _mixer_bwd(pm, dab, states, p, rider=None, rider_ins=()):
    n_p = len(_P_NAMES)
    r_in, r_out_specs, r_out_shape, r_sems = _rider_specs(rider, rider_ins)

    def body(*refs):
        ((pm_ref, xprev_ref, dab_ref), si_refs, p_list, ride_in, (dpm_ref,), dp_list, ride_out, ds_refs, (carry_ref,),
         dense_list, ddense_list, sems) = _split(refs, 3, 4, n_p, len(r_in), 1, n_p, len(r_out_specs), 4, 1, 3, 3, len(r_sems))
        p_refs = dict(zip(_P_NAMES, p_list))
        dp_refs = dict(zip(_P_NAMES, dp_list))
        dense = dict(zip(_P_BLOCKDIAG, dense_list))
        ddense = dict(zip(_P_BLOCKDIAG, ddense_list))
        i = pl.program_id(0)
        blk = N_SWEEP - 1 - i
        _ride(rider, ("first",), i == 0, ride_in, ride_out, sems)

        @pl.when(i == 0)
        def _():
            for r in ds_refs:
                r[...] = jnp.zeros_like(r)
            for nm in _P_NAMES:
                if nm in _P_BLOCKDIAG:
                    ddense[nm][...] = jnp.zeros_like(ddense[nm])
                    _expand_blockdiag(p_refs[nm], dense[nm])
                else:
                    dp_refs[nm][...] = jnp.zeros_like(dp_refs[nm])
            carry_ref[...] = jnp.zeros_like(carry_ref)

        pv = {nm: (_per_head(dense[nm]) if nm in _P_BLOCKDIAG else p_refs[nm][...]) for nm in _P_NAMES}
        dst = {name: _per_head(r) for name, r in zip(_S_NAMES, ds_refs)}
        st = {name: [r[0, h] for h in range(HEADS)] for name, r in zip(_S_NAMES, si_refs)}
        xprev8 = jnp.where(blk > 0, xprev_ref[CHUNK - 8:CHUNK, :], 0.0)
        _, vjp = jax.vjp(functools.partial(_mixer_chunk, _VJP_OPS), pv, st, pm_ref[...], xprev8)
        dp_sum, dst, dpm, dxprev8 = vjp((dab_ref[...], dst))
        reach = jnp.concatenate([jnp.zeros((SWEEP * CHUNK - 8, 512), F32), carry_ref[...]], axis=0)
        dpm_ref[:, 0:PM_XM] = dpm[:, 0:PM_XM].astype(BF16)
        dpm_ref[:, PM_XM:PM_XM + 512] = (dpm[:, PM_XM:PM_XM + 512] + reach).astype(BF16)
        dpm_ref[:, PM_XM + 512:PM_W] = dpm[:, PM_XM + 512:PM_W].astype(BF16)
        carry_ref[...] = dxprev8
        for name, r in zip(_S_NAMES, ds_refs):
            for h in range(HEADS):
                r[h] = dst[name][h]
        for nm in _P_NAMES:
            if nm in _P_BLOCKDIAG:
                for h in range(HEADS):
                    ddense[nm][h] += dp_sum[nm][h]
            else:
                dp_refs[nm][...] += dp_sum[nm]

        @pl.when(i == N_SWEEP - 1)
        def _():
            for nm in _P_BLOCKDIAG:
                _collect_blockdiag(ddense[nm], dp_refs[nm])

        _ride(rider, ("early",), i == 1, ride_in, ride_out, sems)
        _ride(rider, ("middle",), i == _middle_step(rider, N_SWEEP), ride_in, ride_out, sems)
        _ride(rider, ("last",), i == N_SWEEP - 1, ride_in, ride_out, sems)

    rev = lambda i: (N_SWEEP - 1 - i, 0)
    in_specs = [pl.BlockSpec((SWEEP * CHUNK, PM_W), rev),
                pl.BlockSpec((CHUNK, 512), lambda i: (jnp.maximum(SWEEP * (N_SWEEP - 1 - i) - 1, 0), PM_XM // 512)),
                pl.BlockSpec((SWEEP * CHUNK, 1024), rev)]
    for nm in _S_NAMES:
        in_specs.append(pl.BlockSpec((1,) + _S_SHAPES[nm], lambda i: (N_SWEEP - 1 - i, 0, 0, 0)))
    in_specs += [_const_spec(_P_SHAPES[nm]) for nm in _P_NAMES] + r_in
    out_specs = [pl.BlockSpec((SWEEP * CHUNK, PM_W), rev)] + [_const_spec(_P_SHAPES[nm]) for nm in _P_NAMES]
    out_shape = [jax.ShapeDtypeStruct((SEQ, PM_W), BF16)] + [jax.ShapeDtypeStruct(_P_SHAPES[nm], F32) for nm in _P_NAMES]
    res = pl.pallas_call(
        body, grid=(N_SWEEP,), in_specs=in_specs, out_specs=out_specs + r_out_specs, out_shape=out_shape + r_out_shape,
        scratch_shapes=[pltpu.VMEM(_S_SHAPES[nm], F32) for nm in _S_NAMES] + [pltpu.VMEM((8, 512), F32)]
        + [pltpu.VMEM((HEADS, 128, 128), F32) for _ in range(2 * len(_P_BLOCKDIAG))] + r_sems,
        compiler_params=_params(("arbitrary",)), name="mixer_bwd",
    )(pm, pm, dab, *states, *[p[nm] for nm in _P_NAMES], *rider_ins)
    return res[0], dict(zip(_P_NAMES, res[1:1 + n_p])), res[1 + n_p:]


def _tok(width):
    return pl.BlockSpec((TOK_TILE, width), lambda i: (i, 0))


def _once(shape):
    zeros = (0,) * len(shape)
    return pl.BlockSpec(shape, lambda i: zeros, pipeline_mode=pl.Buffered(1))


def _rms_fwd(x):
    r = lax.rsqrt(_mean(x * x) + EPS)
    return x * r, r


def _rms_bwd(dy, xn, r, g):
    gd = dy * g
    return r * (gd - xn * _mean(xn * gd))


def _tiled_call(body, in_specs, out_specs, out_shape, args, name, rider=None, rider_ins=(), scratch=()):
    r_in, r_out_specs, r_out_shape, r_scratch = _rider_specs(rider, rider_ins)
    n_in, n_out = len(in_specs), len(out_specs)

    def hosted(*refs):
        ins, ride_in, outs, ride_out, own, r_scr = _split(refs, n_in, len(r_in), n_out, len(r_out_specs), len(scratch),
                                                          len(r_scratch))
        i = pl.program_id(0)
        _ride(rider, ("first",), i == 0, ride_in, ride_out, r_scr)
        body(*ins, *outs, *own)
        _ride(rider, ("early",), i == 1, ride_in, ride_out, r_scr)
        _ride(rider, ("middle",), i == _middle_step(rider, N_TOK_TILE), ride_in, ride_out, r_scr)
        _ride(rider, ("last",), i == N_TOK_TILE - 1, ride_in, ride_out, r_scr)

    res = pl.pallas_call(
        hosted, grid=(N_TOK_TILE,), in_specs=list(in_specs) + r_in, out_specs=list(out_specs) + r_out_specs,
        out_shape=list(out_shape) + r_out_shape, scratch_shapes=list(scratch) + r_scratch,
        compiler_params=_params(("arbitrary",)), name=name,
    )(*args, *rider_ins)
    return res[:n_out], res[n_out:]


def _join_rows(w4_ref, wt_ref):
    @pl.when(pl.program_id(0) == 0)
    def _():
        for j in range(N_CHIP):
            wt_ref[j * IN_SHARD:(j + 1) * IN_SHARD, :] = w4_ref[j]


def _joined_scratch():
    return [pltpu.VMEM((D_IN, D_MODEL), BF16)]


def _in_proj(x, g_pre, w4_in, rider=None, rider_ins=()):
    def body(x_ref, g_ref, w4_ref, pm_ref, gab_ref, h_ref, wt_ref):
        _join_rows(w4_ref, wt_ref)
        xn, _ = _rms_fwd(x_ref[...])
        h = (xn * g_ref[...]).astype(BF16)
        h_ref[...] = h
        pm_ref[:, 0:PM_XM] = _nt(h, wt_ref[0:IN_ALOW, :])
        pm_ref[:, PM_XM:PM_AL] = _nt(h, wt_ref[IN_XM:IN_GATES, :])
        pm_ref[:, PM_AL:PM_W] = _nt(h, wt_ref[IN_ALOW:IN_ALOW + 128, :])
        gab_ref[...] = _nt(h, wt_ref[IN_GATES:D_IN, :])

    return _tiled_call(
        body, [_tok(D_MODEL), _once((1, D_MODEL)), _once((N_CHIP, IN_SHARD, D_MODEL))],
        [_tok(PM_W), _tok(GAB_W), _tok(D_MODEL)],
        [jax.ShapeDtypeStruct((SEQ, PM_W), F32), jax.ShapeDtypeStruct((SEQ, GAB_W), F32),
         jax.ShapeDtypeStruct((SEQ, D_MODEL), BF16)], (x, g_pre, w4_in), "in_proj", rider, rider_ins, _joined_scratch())


def _merge_fwd(ab, gab, x, w_pa4, w_pb4, w_o, g_post, rider=None, rider_ins=()):
    def body(ab_ref, gab_ref, x_ref, wpa_ref, wpb_ref, wo_ref, g_ref, x1_ref, mix_ref, mg_ref):
        a = ab_ref[:, 0:512]
        b = ab_ref[:, 512:1024]
        for j in range(N_CHIP):
            blk = slice(j * 256, (j + 1) * 256)
            ya = jnp.dot(a, wpa_ref[j], preferred_element_type=F32)
            yb = jnp.dot(b, wpb_ref[j], preferred_element_type=F32)
            sa = _sigmoid(gab_ref[:, j * 256:(j + 1) * 256])
            sb = _sigmoid(gab_ref[:, 1024 + j * 256:1024 + (j + 1) * 256])
            mg_ref[:, blk] = (sa * ya + sb * yb).astype(BF16)
        mix = jnp.dot(mg_ref[...], wo_ref[...], preferred_element_type=F32)
        mix_ref[...] = mix
        mn, _ = _rms_fwd(mix)
        x1_ref[...] = x_ref[...] + mn * g_ref[...]

    return _tiled_call(
        body, [_tok(1024), _tok(GAB_W), _tok(D_MODEL), _once((N_CHIP, 512, 256)), _once((N_CHIP, 512, 256)),
               _once((D_MODEL, D_MODEL)), _once((1, D_MODEL))], [_tok(D_MODEL), _tok(D_MODEL), _tok(D_MODEL)],
        [jax.ShapeDtypeStruct((SEQ, D_MODEL), F32), jax.ShapeDtypeStruct((SEQ, D_MODEL), F32),
         jax.ShapeDtypeStruct((SEQ, D_MODEL), BF16)], (ab, gab, x, w_pa4, w_pb4, w_o, g_post), "merge_fwd", rider, rider_ins)


def _mlp(x1, target, g_pre, g_post, w_up4, w_down_a4, w_down_b4):
    def body(x1_ref, t_ref, gpre_ref, gpost_ref, wup_ref, wda_ref, wdb_ref,
             dx1_ref, u_ref, dd_ref, h2_ref, dpre_ref, dgpost_ref, dgpre_ref, loss_ref):
        @pl.when(pl.program_id(0) == 0)
        def _():
            dgpost_ref[...] = jnp.zeros_like(dgpost_ref)
            dgpre_ref[...] = jnp.zeros_like(dgpre_ref)
            loss_ref[...] = jnp.zeros_like(loss_ref)

        x1 = x1_ref[...]
        gpre = gpre_ref[...]
        gpost = gpost_ref[...]
        xn2, r2 = _rms_fwd(x1)
        h2 = (xn2 * gpre).astype(BF16)
        h2_ref[...] = h2
        rl = []
        d = jnp.zeros((TOK_TILE, D_MODEL), F32)
        for j in range(N_CHIP):
            blk = slice(j * 1024, (j + 1) * 1024)
            r = jnp.maximum(jnp.dot(h2, wup_ref[j], preferred_element_type=F32), 0.0)
            rl.append(r)
            u = (r * r).astype(BF16)
            u_ref[:, blk] = u
            d = d + jnp.dot(u[:, 0:512], wda_ref[j], preferred_element_type=F32)
            d = d + jnp.dot(u[:, 512:1024], wdb_ref[j], preferred_element_type=F32)
        dn, r3 = _rms_fwd(d)
        diff = x1 + dn * gpost - t_ref[...]
        loss_ref[...] += jnp.sum(diff * diff, keepdims=True) * (0.5 / D_MODEL)
        dy = diff * (1.0 / D_MODEL)
        dgpost_ref[...] += jnp.sum(dy * dn, axis=0, keepdims=True)
        dd = _rms_bwd(dy, dn, r3, gpost).astype(BF16)
        dd_ref[...] = dd
        dh2 = jnp.zeros((TOK_TILE, D_MODEL), F32)
        for j in range(N_CHIP):
            blk = slice(j * 1024, (j + 1) * 1024)
            du = jnp.concatenate([_nt(dd, wda_ref[j]), _nt(dd, wdb_ref[j])], axis=1)
            dpre = (du * (2.0 * rl[j])).astype(BF16)
            dpre_ref[:, blk] = dpre
            dh2 = dh2 + _nt(dpre, wup_ref[j])
        dgpre_ref[...] += jnp.sum(dh2 * xn2, axis=0, keepdims=True)
        dx1_ref[...] = dy + _rms_bwd(dh2, xn2, r2, gpre)

    acc = pl.BlockSpec((1, D_MODEL), lambda i: (0, 0))
    return pl.pallas_call(
        body, grid=(N_TOK_TILE,),
        in_specs=[_tok(D_MODEL), _tok(D_MODEL), _once((1, D_MODEL)), _once((1, D_MODEL)),
                  _once((N_CHIP, D_MODEL, 1024)), _once((N_CHIP, 512, D_MODEL)), _once((N_CHIP, 512, D_MODEL))],
        out_specs=[_tok(D_MODEL), _tok(D_FF), _tok(D_MODEL), _tok(D_MODEL), _tok(D_FF), acc, acc,
                   pl.BlockSpec((1, 128), lambda i: (0, 0))],
        out_shape=[jax.ShapeDtypeStruct((SEQ, D_MODEL), F32), jax.ShapeDtypeStruct((SEQ, D_FF), BF16),
                   jax.ShapeDtypeStruct((SEQ, D_MODEL), BF16), jax.ShapeDtypeStruct((SEQ, D_MODEL), BF16),
                   jax.ShapeDtypeStruct((SEQ, D_FF), BF16), jax.ShapeDtypeStruct((1, D_MODEL), F32),
                   jax.ShapeDtypeStruct((1, D_MODEL), F32), jax.ShapeDtypeStruct((1, 128), F32)],
        compiler_params=_params(("arbitrary",)), name="mlp_fwd_bwd",
    )(x1, target, g_pre, g_post, w_up4, w_down_a4, w_down_b4)


def _merge_bwd(dx1, mix, ab, gab, merged, w_pa4, w_pb4, w_o, g_post):
    def body(dx1_ref, mix_ref, ab_ref, gab_ref, mg_ref, wpa_ref, wpb_ref, wo_ref, g_ref,
             dgab_ref, dab_ref, dg_ref, dwpa_ref, dwpb_ref, dwo_ref, acc_pa, acc_pb, acc_o):
        @pl.when(pl.program_id(0) == 0)
        def _():
            dg_ref[...] = jnp.zeros_like(dg_ref)
            acc_pa[...] = jnp.zeros_like(acc_pa)
            acc_pb[...] = jnp.zeros_like(acc_pb)
            acc_o[...] = jnp.zeros_like(acc_o)

        dx1 = dx1_ref[...]
        mn, r = _rms_fwd(mix_ref[...])
        dg_ref[...] += jnp.sum(dx1 * mn, axis=0, keepdims=True)
        dmix = _rms_bwd(dx1, mn, r, g_ref[...]).astype(BF16)
        acc_o[...] += _tn(mg_ref[...], dmix)
        dmerged = _nt(dmix, wo_ref[...])
        a = ab_ref[:, 0:512]
        b = ab_ref[:, 512:1024]
        da = jnp.zeros((TOK_TILE, 512), F32)
        db = jnp.zeros((TOK_TILE, 512), F32)
        dyas, dybs = [], []
        for j in range(N_CHIP):
            blk = slice(j * 256, (j + 1) * 256)
            blk_b = slice(1024 + j * 256, 1024 + (j + 1) * 256)
            dm = dmerged[:, blk]
            ya = jnp.dot(a, wpa_ref[j], preferred_element_type=F32)
            yb = jnp.dot(b, wpb_ref[j], preferred_element_type=F32)
            sa = _sigmoid(gab_ref[:, blk])
            sb = _sigmoid(gab_ref[:, blk_b])
            dya = (dm * sa).astype(BF16)
            dyb = (dm * sb).astype(BF16)
            dyas.append(dya)
            dybs.append(dyb)
            dgab_ref[:, blk] = (dm * ya * sa * (1.0 - sa)).astype(BF16)
            dgab_ref[:, blk_b] = (dm * yb * sb * (1.0 - sb)).astype(BF16)
            da = da + _nt(dya, wpa_ref[j])
            db = db + _nt(dyb, wpb_ref[j])
        dab_ref[:, 0:512] = da
        dab_ref[:, 512:1024] = db
        acc_pa[...] += _tn(a, jnp.concatenate(dyas, axis=1))
        acc_pb[...] += _tn(b, jnp.concatenate(dybs, axis=1))

        @pl.when(pl.program_id(0) == N_TOK_TILE - 1)
        def _():
            dwo_ref[...] = acc_o[...].astype(BF16)
            for j in range(N_CHIP):
                dwpa_ref[j] = acc_pa[:, j * 256:(j + 1) * 256].astype(BF16)
                dwpb_ref[j] = acc_pb[:, j * 256:(j + 1) * 256].astype(BF16)

    whole = lambda shape: pl.BlockSpec(shape, lambda i: (0,) * len(shape))
    return pl.pallas_call(
        body, grid=(N_TOK_TILE,),
        in_specs=[_tok(D_MODEL), _tok(D_MODEL), _tok(1024), _tok(GAB_W), _tok(D_MODEL), _once((N_CHIP, 512, 256)),
                  _once((N_CHIP, 512, 256)), _once((D_MODEL, D_MODEL)), _once((1, D_MODEL))],
        out_specs=[_tok(GAB_W), _tok(1024), whole((1, D_MODEL)), whole((N_CHIP, 512, 256)), whole((N_CHIP, 512, 256)),
                   whole((D_MODEL, D_MODEL))],
        out_shape=[jax.ShapeDtypeStruct((SEQ, GAB_W), BF16), jax.ShapeDtypeStruct((SEQ, 1024), F32),
                   jax.ShapeDtypeStruct((1, D_MODEL), F32), jax.ShapeDtypeStruct((N_CHIP, 512, 256), BF16),
                   jax.ShapeDtypeStruct((N_CHIP, 512, 256), BF16), jax.ShapeDtypeStruct((D_MODEL, D_MODEL), BF16)],
        scratch_shapes=[pltpu.VMEM((512, D_MODEL), F32), pltpu.VMEM((512, D_MODEL), F32),
                        pltpu.VMEM((D_MODEL, D_MODEL), F32)],
        compiler_params=_params(("arbitrary",)), name="merge_bwd",
    )(dx1, mix, ab, gab, merged, w_pa4, w_pb4, w_o, g_post)


def _in_proj_bwd(dpm, dgab, x, dx1, g_pre, w4_in, rider=None, rider_ins=()):
    def body(dpm_ref, dgab_ref, x_ref, dx1_ref, g_ref, w4_ref, dx_ref, dg_ref, wt_ref):
        _join_rows(w4_ref, wt_ref)

        @pl.when(pl.program_id(0) == 0)
        def _():
            dg_ref[...] = jnp.zeros_like(dg_ref)

        dh = jnp.dot(dpm_ref[:, 0:PM_XM], wt_ref[0:IN_ALOW, :], preferred_element_type=F32)
        dh = dh + jnp.dot(dpm_ref[:, PM_XM:PM_AL], wt_ref[IN_XM:IN_GATES, :], preferred_element_type=F32)
        dh = dh + jnp.dot(dpm_ref[:, PM_AL:PM_W], wt_ref[IN_ALOW:IN_ALOW + 128, :], preferred_element_type=F32)
        dh = dh + jnp.dot(dgab_ref[...], wt_ref[IN_GATES:D_IN, :], preferred_element_type=F32)
        xn, r = _rms_fwd(x_ref[...])
        dg_ref[...] += jnp.sum(dh * xn, axis=0, keepdims=True)
        dx_ref[...] = dx1_ref[...] + _rms_bwd(dh, xn, r, g_ref[...])

    return _tiled_call(
        body, [_tok(PM_W), _tok(GAB_W), _tok(D_MODEL), _tok(D_MODEL), _once((1, D_MODEL)),
               _once((N_CHIP, IN_SHARD, D_MODEL))],
        [_tok(D_MODEL), pl.BlockSpec((1, D_MODEL), lambda i: (0, 0))],
        [jax.ShapeDtypeStruct((SEQ, D_MODEL), F32), jax.ShapeDtypeStruct((1, D_MODEL), F32)],
        (dpm, dgab, x, dx1, g_pre, w4_in), "in_proj_bwd", rider, rider_ins, _joined_scratch())


def _dw_in(dpm, dgab, h):
    n_pm = PM_AL // 512
    n_blk = n_pm + GAB_W // 512

    def place(o_ref, rows, lo, hi):
        for j in range(N_CHIP):
            a, b = max(lo, j * IN_SHARD), min(hi, (j + 1) * IN_SHARD)
            if a < b:
                o_ref[j, a - j * IN_SHARD:b - j * IN_SHARD, :] = rows(a - lo, b - lo)

    def body(dpm_ref, dgab_ref, dal_ref, h_ref, o_ref, blk_ref):
        i = pl.program_id(0)

        @pl.when(i < n_pm)
        def _():
            blk_ref[...] = _tn(dpm_ref[...], h_ref[...]).astype(BF16)

        @pl.when(i >= n_pm)
        def _():
            blk_ref[...] = _tn(dgab_ref[...], h_ref[...]).astype(BF16)

        for k in range(n_blk):
            off = k * 512 + (IN_XM - IN_ALOW) * (k >= IN_ALOW // 512)

            @pl.when(i == k)
            def _():
                place(o_ref, lambda a, b: blk_ref[a:b, :], off, off + 512)

        @pl.when(i == 0)
        def _():
            a_low = _tn(dal_ref[...], h_ref[...])[0:IN_XM - IN_ALOW].astype(BF16)
            place(o_ref, lambda a, b: a_low[a:b], IN_ALOW, IN_XM)

    return pl.pallas_call(
        body, grid=(n_blk,),
        in_specs=[pl.BlockSpec((SEQ, 512), lambda i: (0, jnp.minimum(i, n_pm - 1))),
                  pl.BlockSpec((SEQ, 512), lambda i: (0, jnp.maximum(i - n_pm, 0))),
                  pl.BlockSpec((SEQ, 128), lambda i: (0, PM_AL // 128)),
                  _once((SEQ, D_MODEL))],
        out_specs=pl.BlockSpec((N_CHIP, IN_SHARD, D_MODEL), lambda i: (0, 0, 0)),
        out_shape=jax.ShapeDtypeStruct((N_CHIP, IN_SHARD, D_MODEL), BF16),
        scratch_shapes=[pltpu.VMEM((512, D_MODEL), BF16)],
        compiler_params=_params(("arbitrary",)), name="dw_in",
    )(dpm, dgab, dpm, h)


def _tn_matmul(a, b, name, shards=1, tm=1024, rider=None, rider_ins=()):
    m, n = a.shape[1], b.shape[1]
    tm = min(tm, m)
    tn = n // shards if shards > 1 else min(n, 1024)
    steps_i, steps_j = m // tm, n // tn
    r_in, r_out_specs, r_out_shape, r_scratch = _rider_specs(rider, rider_ins)

    def body(*refs):
        (a_ref, b_ref), ride_in, (o_ref,), ride_out, scratch = _split(refs, 2, len(r_in), 1, len(r_out_specs), len(r_scratch))
        step = pl.program_id(0) * steps_j + pl.program_id(1)
        _ride(rider, ("first",), step == 0, ride_in, ride_out, scratch)
        o_ref[...] = _tn(a_ref[...], b_ref[...]).astype(BF16)
        _ride(rider, ("middle", "last"), step == steps_i * steps_j - 1, ride_in, ride_out, scratch)

    if shards > 1:
        out_spec = pl.BlockSpec((None, tm, tn), lambda i, j: (j, i, 0))
        out_shape = jax.ShapeDtypeStruct((shards, m, tn), BF16)
    else:
        out_spec = pl.BlockSpec((tm, tn), lambda i, j: (i, j))
        out_shape = jax.ShapeDtypeStruct((m, n), BF16)
    res = pl.pallas_call(
        body, grid=(steps_i, steps_j),
        in_specs=[pl.BlockSpec((SEQ, tm), lambda i, j: (0, i)), pl.BlockSpec((SEQ, tn), lambda i, j: (0, j))] + r_in,
        out_specs=[out_spec] + r_out_specs, out_shape=[out_shape] + r_out_shape, scratch_shapes=r_scratch,
        compiler_params=_params(("arbitrary", "arbitrary")), name=name,
    )(a, b, *rider_ins)
    return res[0] if rider is None else (res[0], res[1:])


MESH = pl.DeviceIdType.MESH
ANY = pl.BlockSpec(memory_space=pl.ANY)
VMEM_WHOLE = pl.BlockSpec(memory_space=pltpu.VMEM)

_BIG = ("w_in", "w_pa", "w_pb", "w_o", "w_up", "w_down")
_BIG_SHARD = {"w_in": (IN_SHARD, D_MODEL), "w_pa": (512, 256), "w_pb": (512, 256), "w_o": (256, D_MODEL),
              "w_up": (D_MODEL, 1024), "w_down": (1024, D_MODEL),
              "w_down_a": (512, D_MODEL), "w_down_b": (512, D_MODEL)}
_BIG_SPLIT = {"w_in": 1, "w_pa": 0, "w_pb": 0, "w_o": 0, "w_up": 0, "w_down": 0, "w_down_a": 0, "w_down_b": 0}


def _half(ref, e, name, lead=0, part=None):
    axis = _BIG_SPLIT[name]
    size = _BIG_SHARD[name][axis] // 2
    start = e * size
    if part is not None:
        size //= 2
        start = start + part * size
    start = pl.multiple_of(start, 128 if axis == 1 else 16)
    idx = [pl.ds(0, ref.shape[a]) for a in range(lead)]
    idx += [pl.ds(start, size), pl.ds(0, _BIG_SHARD[name][1])] if axis == 0 else [pl.ds(0, _BIG_SHARD[name][0]), pl.ds(start, size)]
    return ref.at[tuple(idx)]


def _half_shape(name):
    r, c = _BIG_SHARD[name]
    return (r // 2, c) if _BIG_SPLIT[name] == 0 else (r, c // 2)


def _remote(src, dst, send_sems, recv_sems, k, to):
    return pltpu.make_async_remote_copy(src_ref=src, dst_ref=dst, send_sem=send_sems.at[k], recv_sem=recv_sems.at[k],
                                        device_id=to, device_id_type=MESH)


def _mesh_place():
    x, y, c = lax.axis_index("x"), lax.axis_index("y"), lax.axis_index("c")
    return x, y, c, [(1 - x, y), (x, 1 - y), (1 - x, 1 - y)]


class _Gather:
    def __init__(self, names, small=(), middle_at=0.5):
        self.middle_at = middle_at
        self.names = tuple(names)
        self.nb = len(self.names)
        self.n = self.nb + len(small)
        self.n_sems = 8 * self.nb + 3 * len(small)
        self.n_flush = 6 * self.nb + len(small)
        self.out_shape = [jax.ShapeDtypeStruct((N_CHIP,) + _BIG_SHARD[nm], BF16) for nm in self.names]
        self.out_shape += [jax.ShapeDtypeStruct((N_CHIP,) + s.shape, s.dtype) for s in small]

    def _copies(self, ins, outs, ss, rs, k):
        x, y, c, _ = _mesh_place()
        name = self.names[k]
        me, xn, yn, dg = 2 * x + y, 2 * (1 - x) + y, 2 * x + (1 - y), 2 * (1 - x) + (1 - y)
        to_x, to_y, sibling = (1 - x, y, c), (x, 1 - y, c), (x, y, 1 - c)

        def region(slot, e, part=None):
            return _half(outs[k].at[slot], e, name, part=part)

        def copy(pair, src, dst, to):
            return _remote(src, dst, ss, rs, 8 * k + pair, to)

        mine = _half(ins[k], c, name)
        sent = [copy(0, mine, region(me, c), to_x), copy(1, mine, region(me, c), to_y),
                copy(2, region(xn, c, 0), region(xn, c, 0), to_y), copy(3, region(yn, c, 1), region(yn, c, 1), to_x),
                copy(4, region(xn, c), region(xn, c), sibling), copy(5, region(yn, c), region(yn, c), sibling),
                copy(6, region(dg, c, 0), region(dg, c, 0), sibling), copy(7, region(dg, c, 1), region(dg, c, 1), sibling)]
        landing = [region(xn, c), region(yn, c), region(dg, c, 0), region(dg, c, 1),
                   region(xn, 1 - c), region(yn, 1 - c), region(dg, 1 - c, 0), region(dg, 1 - c, 1)]
        received = [copy(pair, dst, dst, sibling) for pair, dst in enumerate(landing)]
        return sent, received

    def _small(self, ins, outs, ss, rs, k, j, peer, slot, c):
        return _remote(ins[k], outs[k].at[slot], ss, rs, 8 * self.nb + 3 * (k - self.nb) + j, (*peer, c))

    def flush(self, phase, lands, outs, fs):
        x, y, c, _ = _mesh_place()
        me, xn, yn, dg = 2 * x + y, 2 * (1 - x) + y, 2 * x + (1 - y), 2 * (1 - x) + (1 - y)

        def pieces(k):
            name = self.names[k]
            spots = [lambda r: r.at[me], lambda r: _half(r.at[xn], c, name), lambda r: _half(r.at[yn], c, name),
                     lambda r: _half(r.at[xn], 1 - c, name), lambda r: _half(r.at[yn], 1 - c, name), lambda r: r.at[dg]]
            return [pltpu.make_async_copy(spot(lands[k]), spot(outs[k]), fs.at[6 * k + t]) for t, spot in enumerate(spots)]

        ready = {"first": (0,), "middle": (1, 2), "late": (3, 4), "last": (5,)}[phase]
        for k in range(self.nb):
            cps = pieces(k)
            for t in ready:
                cps[t].start()
        if phase == "last":
            small = [pltpu.make_async_copy(lands[k], outs[k], fs.at[6 * self.nb + k - self.nb]) for k in range(self.nb, self.n)]
            for cp in small:
                cp.start()
            for k in range(self.nb):
                for cp in pieces(k):
                    cp.wait()
            for cp in small:
                cp.wait()

    def first(self, ins, outs, ss, rs):
        x, y, c, peers = _mesh_place()
        me = 2 * x + y
        for k in range(self.nb):
            sent, _ = self._copies(ins, outs, ss, rs, k)
            sent[0].start()
            sent[1].start()
        for k in range(self.nb, self.n):
            for j, peer in enumerate(peers):
                self._small(ins, outs, ss, rs, k, j, peer, me, c).start()
        for k in range(self.n):
            outs[k][me] = ins[k][...]

    def middle(self, ins, outs, ss, rs):
        for k in range(self.nb):
            sent, received = self._copies(ins, outs, ss, rs, k)
            for pair in (0, 1):
                received[pair].wait_recv()
                sent[2 + pair].start()
                sent[4 + pair].start()

    def late(self, ins, outs, ss, rs):
        for k in range(self.nb):
            _, received = self._copies(ins, outs, ss, rs, k)
            for pair in (4, 5):
                received[pair].wait_recv()

    def last(self, ins, outs, ss, rs):
        x, y, c, peers = _mesh_place()
        for k in range(self.nb):
            sent, received = self._copies(ins, outs, ss, rs, k)
            for pair in (2, 3):
                received[pair].wait_recv()
                sent[4 + pair].start()
        for k in range(self.nb):
            sent, received = self._copies(ins, outs, ss, rs, k)
            for pair in (6, 7):
                received[pair].wait_recv()
            for cp in sent:
                cp.wait_send()
        for k in range(self.nb, self.n):
            for j, (px, py) in enumerate(peers):
                self._small(ins, outs, ss, rs, k, j, (px, py), 2 * px + py, c).wait_recv()
                self._small(ins, outs, ss, rs, k, j, (px, py), 2 * x + y, c).wait_send()


def _run_alone(rider, ins, name):
    r_in, r_out_specs, r_out_shape, r_scratch = _rider_specs(rider, ins)

    def body(*refs):
        ride_in, ride_out, scratch = _split(refs, len(r_in), len(r_out_specs), len(r_scratch))
        _ride(rider, ("first", "middle", "last"), pl.program_id(0) == 0, ride_in, ride_out, scratch)

    return pl.pallas_call(
        body, grid=(1,), in_specs=r_in, out_specs=r_out_specs, out_shape=r_out_shape, scratch_shapes=r_scratch,
        compiler_params=_params(("arbitrary",)), name=name,
    )(*ins)


class _Presum:
    in_space = ANY

    def __init__(self, names, base=0):
        self.names = tuple(names)
        self.n = len(self.names)
        self.base = base
        self.n_sems = 3 * self.n
        self.out_shape = [jax.ShapeDtypeStruct((N_CHIP,) + _half_shape(nm), BF16) for nm in self.names]
        self.work_shape = self.out_shape + self.out_shape

    def _stage(self, ins, bufs, ss, k, e, which):
        n = self.n
        return pltpu.make_async_copy(_half(ins[k], e, self.names[k], lead=1), bufs[which * n + k],
                                     ss.at[self.base + which * n + k])

    def _give(self, bufs, ss, rs, k, sibling):
        return _remote(bufs[self.n + k], bufs[k], ss, rs, self.base + k, sibling)

    def first(self, ins, bufs, ss, rs):
        x, y, c, _ = _mesh_place()
        for k in range(self.n):
            self._stage(ins, bufs, ss, k, 1 - c, 1).start()
        for k in range(self.n):
            self._stage(ins, bufs, ss, k, c, 2).start()
        for k in range(self.n):
            self._stage(ins, bufs, ss, k, 1 - c, 1).wait()
            self._give(bufs, ss, rs, k, (x, y, 1 - c)).start()

    def middle(self, ins, bufs, ss, rs):
        pass

    def last(self, ins, bufs, ss, rs):
        x, y, c, _ = _mesh_place()
        for k in range(self.n):
            self._give(bufs, ss, rs, k, (x, y, 1 - c)).wait_recv()
            self._stage(ins, bufs, ss, k, c, 2).wait()

            @pl.loop(0, N_CHIP)
            def _(j):
                bufs[k][j] = (bufs[k][j].astype(F32) + bufs[2 * self.n + k][j].astype(F32)).astype(BF16)
        for k in range(self.n):
            self._give(bufs, ss, rs, k, (x, y, 1 - c)).wait_send()


class _ReduceRelay:
    middle_at = 0.75

    def __init__(self, names, base=0):
        self.names = tuple(names)
        self.n = len(self.names)
        self.base = base
        self.n_sems = 6 * self.n
        self.out_shape = [jax.ShapeDtypeStruct((N_CHIP,) + _half_shape(nm), BF16) for nm in self.names]
        quarter = [jax.ShapeDtypeStruct(self._part_shape(nm), BF16) for nm in self.names]
        self.work_shape = quarter + quarter

    @staticmethod
    def _part_shape(name):
        r, c = _half_shape(name)
        return (r // 2, c) if _BIG_SPLIT[name] == 0 else (r, c // 2)

    def _part(self, ref, name, p):
        r, c = self._part_shape(name)
        return ref.at[pl.ds(p * r, r), pl.ds(0, c)] if _BIG_SPLIT[name] == 0 else ref.at[pl.ds(0, r), pl.ds(p * c, c)]

    def _copies(self, ins, bufs, ss, rs, k):
        x, y, c, _ = _mesh_place()
        name, n = self.names[k], self.n
        me, xn, yn, dg = 2 * x + y, 2 * (1 - x) + y, 2 * x + (1 - y), 2 * (1 - x) + (1 - y)
        to_x, to_y = (1 - x, y, c), (x, 1 - y, c)
        mine = lambda slot, p: self._part(ins[k].at[slot], name, p)
        slot = lambda s, p: self._part(bufs[k].at[s], name, p)
        from_x, from_y = bufs[n + k], bufs[2 * n + k]

        def copy(pair, src, dst, to):
            return _remote(src, dst, ss, rs, self.base + 6 * k + pair, to)

        sent = [copy(0, mine(dg, 0), from_x, to_x), copy(1, mine(dg, 1), from_y, to_y),
                copy(2, mine(xn, 0), slot(me, 0), to_x), copy(3, mine(yn, 1), slot(me, 1), to_y),
                copy(4, from_y, slot(me, 1), to_x), copy(5, from_x, slot(me, 0), to_y)]
        landing = [from_x, from_y, slot(xn, 0), slot(yn, 1), slot(xn, 1), slot(yn, 0)]
        received = [copy(pair, dst, dst, to_x) for pair, dst in enumerate(landing)]
        return sent, received

    def first(self, ins, bufs, ss, rs):
        x, y, c, _ = _mesh_place()
        me, dg = 2 * x + y, 2 * (1 - x) + (1 - y)
        for k in range(self.n):
            sent, _ = self._copies(ins, bufs, ss, rs, k)
            for pair in range(4):
                sent[pair].start()
        for k in range(self.n):
            bufs[k][me] = ins[k][me]
            bufs[k][dg] = jnp.zeros(_half_shape(self.names[k]), BF16)

    def middle(self, ins, bufs, ss, rs):
        x, y, c, _ = _mesh_place()
        xn, yn = 2 * (1 - x) + y, 2 * x + (1 - y)
        for k in range(self.n):
            sent, received = self._copies(ins, bufs, ss, rs, k)
            name, n = self.names[k], self.n
            for pair, buf, own in ((0, bufs[n + k], self._part(ins[k].at[yn], name, 0)),
                                   (1, bufs[2 * n + k], self._part(ins[k].at[xn], name, 1))):
                received[pair].wait_recv()
                buf[...] = (buf[...].astype(F32) + own[...].astype(F32)).astype(BF16)
            sent[5].start()
            sent[4].start()

    def last(self, ins, bufs, ss, rs):
        for k in range(self.n):
            sent, received = self._copies(ins, bufs, ss, rs, k)
            for pair in range(2, 6):
                received[pair].wait_recv()
            for cp in sent:
                cp.wait_send()


class _PresumThenRelay:
    in_space = ANY
    middle_at = _ReduceRelay.middle_at

    def __init__(self, names):
        self.relay = _ReduceRelay(names)
        self.pre = _Presum(names, base=self.relay.n_sems)
        self.n_sems = self.relay.n_sems + self.pre.n_sems
        self.out_shape = self.relay.out_shape
        self.work_shape = list(self.relay.work_shape) + list(self.pre.out_shape) + list(self.pre.work_shape)
        self.n_relay = len(self.relay.out_shape) + len(self.relay.work_shape)

    def first(self, ins, bufs, ss, rs):
        self.pre.first(ins, bufs[self.n_relay:], ss, rs)

    def early(self, ins, bufs, ss, rs):
        self.pre.last(ins, bufs[self.n_relay:], ss, rs)
        self.relay.first(bufs[self.n_relay:], bufs[:self.n_relay], ss, rs)

    def middle(self, ins, bufs, ss, rs):
        self.relay.middle(bufs[self.n_relay:], bufs[:self.n_relay], ss, rs)

    def last(self, ins, bufs, ss, rs):
        self.relay.last(bufs[self.n_relay:], bufs[:self.n_relay], ss, rs)


class _SendPartials:
    def __init__(self, names, small_shape=None):
        self.n = len(names)
        self.small = small_shape is not None
        self.n_sems = 3 * self.n + 7
        self.out_shape = [jax.ShapeDtypeStruct((N_CHIP,) + _half_shape(nm), BF16) for nm in names]
        if self.small:
            self.out_shape.append(jax.ShapeDtypeStruct((N_DEV,) + small_shape, F32))

    def _piece(self, ins, outs, ss, rs, k, j, peer, src_slot, dst_slot, c):
        return _remote(ins[k].at[src_slot], outs[k].at[dst_slot], ss, rs, 3 * k + j, (*peer, c))

    def _small(self, ins, outs, ss, rs, r, other, slot):
        return _remote(ins[self.n], outs[self.n].at[slot], ss, rs, 3 * self.n + r, other)

    @staticmethod
    def _others(x, y, c):
        return [(x, y, 1 - c), (1 - x, y, c), (1 - x, y, 1 - c), (x, 1 - y, c), (x, 1 - y, 1 - c),
                (1 - x, 1 - y, c), (1 - x, 1 - y, 1 - c)]

    def first(self, ins, outs, ss, rs, only=None):
        x, y, c, peers = _mesh_place()
        me = 2 * x + y
        which = range(self.n) if only is None else only
        for k in which:
            for j, (px, py) in enumerate(peers):
                self._piece(ins, outs, ss, rs, k, j, (px, py), 2 * px + py, me, c).start()
        if self.small:
            for r, other in enumerate(self._others(x, y, c)):
                self._small(ins, outs, ss, rs, r, other, 4 * x + 2 * y + c).start()
            outs[self.n][4 * x + 2 * y + c] = ins[self.n][...]
        for k in which:
            outs[k][me] = ins[k][me]

    def middle(self, ins, outs, ss, rs):
        pass

    def last(self, ins, outs, ss, rs):
        x, y, c, peers = _mesh_place()
        me = 2 * x + y
        for k in range(self.n):
            for j, (px, py) in enumerate(peers):
                self._piece(ins, outs, ss, rs, k, j, (px, py), me, 2 * px + py, c).wait_recv()
                self._piece(ins, outs, ss, rs, k, j, (px, py), 2 * px + py, me, c).wait_send()
        if self.small:
            for r, (px, py, pc) in enumerate(self._others(x, y, c)):
                self._small(ins, outs, ss, rs, r, (px, py, pc), 4 * px + 2 * py + pc).wait_recv()
                self._small(ins, outs, ss, rs, r, (px, py, pc), 4 * x + 2 * y + c).wait_send()


class _PresumThenSend:
    def __init__(self, names):
        self.send = _SendPartials(names)
        self.pre = _Presum(names[-1:], base=self.send.n_sems)
        self.n = self.send.n
        self.n_sems = self.send.n_sems + self.pre.n_sems
        self.out_shape = self.send.out_shape
        self.work_shape = list(self.pre.out_shape) + list(self.pre.work_shape)
        self.in_space = [VMEM_WHOLE] * (self.n - 1) + [ANY]

    def _partials(self, ins, bufs):
        return list(ins[:self.n - 1]) + [bufs[self.n]]

    def first(self, ins, bufs, ss, rs):
        self.pre.first(ins[self.n - 1:], bufs[self.n:], ss, rs)
        self.send.first(ins, bufs[:self.n], ss, rs, only=range(self.n - 1))

    def early(self, ins, bufs, ss, rs):
        self.pre.last(ins[self.n - 1:], bufs[self.n:], ss, rs)
        self.send.first(self._partials(ins, bufs), bufs[:self.n], ss, rs, only=(self.n - 1,))

    def middle(self, ins, bufs, ss, rs):
        pass

    def last(self, ins, bufs, ss, rs):
        self.send.last(self._partials(ins, bufs), bufs[:self.n], ss, rs)


def _sum_swap(names, parts, small):
    n = len(parts)
    everyone = _SendPartials((), small.shape)

    def body(*refs):
        (p_hbm, (small_ref,), o_hbm, (osmall_ref,), p_refs, o_refs, (all_ref,),
         (send_sems, recv_sems, ss_small, rs_small, load_sems, leave_sems)) = _split(refs, n, 1, n, 1, n, n, 1, 6)
        x, y, c = lax.axis_index("x"), lax.axis_index("y"), lax.axis_index("c")
        loads = [pltpu.make_async_copy(p_hbm[k], p_refs[k], load_sems.at[k]) for k in range(n)]
        for cp in loads:
            cp.start()
        everyone.first([small_ref], [all_ref], ss_small, rs_small)

        def mine(k):
            part = _half(o_refs[k], c, names[k])
            return _remote(part, part, send_sems, recv_sems, k, (x, y, 1 - c))

        def leave(k, whose):
            e = c if whose == 0 else 1 - c
            return pltpu.make_async_copy(_half(o_refs[k], e, names[k]), _half(o_hbm[k], e, names[k]),
                                         leave_sems.at[2 * k + whose])

        for k in range(n):
            loads[k].wait()
            for e in range(2):
                @pl.when(c == e)
                def _():
                    g = p_refs[k][0].astype(F32)
                    for s in range(1, N_CHIP):
                        g = g + p_refs[k][s].astype(F32)
                    r, cols = _half_shape(names[k])
                    if _BIG_SPLIT[names[k]] == 0:
                        o_refs[k][e * r:(e + 1) * r, :] = g
                    else:
                        o_refs[k][:, e * cols:(e + 1) * cols] = g
            mine(k).start()
            leave(k, 0).start()
        for k in range(n):
            theirs = _half(o_refs[k], 1 - c, names[k])
            _remote(theirs, theirs, send_sems, recv_sems, k, (x, y, 1 - c)).wait_recv()
            leave(k, 1).start()
        everyone.last([small_ref], [all_ref], ss_small, rs_small)
        g = all_ref[0]
        for d in range(1, N_DEV):
            g = g + all_ref[d]
        osmall_ref[...] = g
        for k in range(n):
            mine(k).wait_send()
            leave(k, 0).wait()
            leave(k, 1).wait()

    shards = [jax.ShapeDtypeStruct(_BIG_SHARD[nm], F32) for nm in names]
    res = pl.pallas_call(
        body, in_specs=[ANY] * n + [VMEM_WHOLE], out_specs=[ANY] * n + [VMEM_WHOLE],
        out_shape=shards + [jax.ShapeDtypeStruct(small.shape, F32)],
        scratch_shapes=[pltpu.VMEM(q.shape, q.dtype) for q in parts] + [pltpu.VMEM(s.shape, s.dtype) for s in shards]
        + [pltpu.VMEM((N_DEV,) + small.shape, F32), pltpu.SemaphoreType.DMA((n,)), pltpu.SemaphoreType.DMA((n,)),
           pltpu.SemaphoreType.DMA((everyone.n_sems,)), pltpu.SemaphoreType.DMA((everyone.n_sems,)),
           pltpu.SemaphoreType.DMA((n,)), pltpu.SemaphoreType.DMA((2 * n,))],
        compiler_params=_params(), name="sum_swap",
    )(*parts, small)
    return res[:n], res[n]


def _tile(rows, cols, itemsize, budget):
    t = cols if rows % 16 else rows
    other = rows if rows % 16 else cols
    step = 256 if rows % 16 else 32
    while t % step == 0 and t * other * itemsize > budget:
        t //= 2
    return (rows, t) if rows % 16 else (t, cols)


def _adamw_math(w, g, m, v):
    m = ADAM_B1 * m + (1.0 - ADAM_B1) * g
    v = ADAM_B2 * v + (1.0 - ADAM_B2) * (g * g)
    m_hat = m / (1.0 - ADAM_B1 ** ADAM_STEP)
    v_hat = v / (1.0 - ADAM_B2 ** ADAM_STEP)
    delta = -ADAM_LR * (m_hat / (jnp.sqrt(v_hat) + ADAM_EPS) + ADAM_WD * w)
    return delta, m, v


def _adamw_big(g, w, m, v, name):
    r, c = w.shape
    tr, tc = _tile(r, c, 4, 2 * 1024 * 1024)

    def body(g_ref, w_ref, m_ref, v_ref, d_ref, nm_ref, nv_ref):
        d_ref[...], nm_ref[...], nv_ref[...] = _adamw_math(w_ref[...], g_ref[...], m_ref[...], v_ref[...])

    blk = pl.BlockSpec((tr, tc), lambda i, l: (i, l))
    return pl.pallas_call(
        body, grid=(r // tr, c // tc), in_specs=[blk, blk, blk, blk],
        out_specs=[blk, blk, blk], out_shape=[jax.ShapeDtypeStruct((r, c), F32)] * 3,
        compiler_params=_params(("arbitrary", "arbitrary")), name=name,
    )(g, w, m, v)


def _adamw_rows(g, w, m, v, name):
    r, k, lanes = w.shape
    tr = 296

    def body(g_ref, w_ref, m_ref, v_ref, g3_ref, d_ref, nm_ref, nv_ref):
        g = g_ref[...].reshape(tr, k, lanes)
        g3_ref[...] = g
        d_ref[...], nm_ref[...], nv_ref[...] = _adamw_math(w_ref[...], g, m_ref[...], v_ref[...])

    rows = pl.BlockSpec((tr, k, lanes), lambda i: (i, 0, 0))
    return pl.pallas_call(
        body, grid=(pl.cdiv(r, tr),), in_specs=[pl.BlockSpec((tr, k * lanes), lambda i: (i, 0)), rows, rows, rows],
        out_specs=[rows] * 4, out_shape=[jax.ShapeDtypeStruct((r, k, lanes), F32)] * 4,
        compiler_params=_params(("arbitrary",)), name=name,
    )(g, w, m, v)


def _adamw_small(ws, gs, ms, vs):
    n = len(ws)

    def body(*refs):
        w_refs, g_refs, m_refs, v_refs, d_refs, nm_refs, nv_refs = _split(refs, *([n] * 7))
        for k in range(n):
            d_refs[k][...], nm_refs[k][...], nv_refs[k][...] = _adamw_math(w_refs[k][...], g_refs[k][...], m_refs[k][...],
                                                                             v_refs[k][...])

    shapes = [jax.ShapeDtypeStruct(w.shape, F32) for w in ws]
    res = pl.pallas_call(body, out_shape=shapes * 3, name="adamw_small")(*ws, *gs, *ms, *vs)
    return res[:n], res[n:2 * n], res[2 * n:]


def _pack(arrs):
    flat = jnp.concatenate([a.reshape(-1) for a in arrs])
    rows = -(-flat.shape[0] // 1024) * 8
    return jnp.pad(flat, (0, rows * 128 - flat.shape[0])).reshape(rows, 128)


def _unpack(buf, shapes):
    flat = buf.reshape(-1)
    out, off = [], 0
    for s in shapes:
        size = 1
        for d in s:
            size *= d
        out.append(flat[off:off + size].reshape(s))
        off += size
    return out


def _block_rows(w):
    return jnp.pad(w.reshape(512, 4), ((0, 0), (0, 124)))


def _cols(a4):
    return jnp.transpose(a4, (1, 0, 2)).reshape(a4.shape[1], -1)


_LATE = ("w_pa", "w_pb", "w_o", "w_up", "w_down")
_RIDE_IN_PROJ = ("w_pa", "w_pb", "w_o", "w_down_b")
_RIDE_MIXER = ("w_up", "w_down_a")


def _full_weights(gathered):
    joined = {"w_o": (D_MODEL, D_MODEL)}
    return {n: (a.reshape(joined[n]) if n in joined else a) for n, a in gathered.items()}


def _local_step(x, target, w, sp, late_shards=None):
    sp = {n: (a.reshape(1, -1) if a.ndim == 1 else a) for n, a in sp.items()}
    wau = jnp.pad(sp["w_a_up"], ((0, 112), (0, 0)))
    wif = jnp.pad(sp["w_if"], ((0, 0), (0, 120)))
    bif = jnp.pad(sp["b_if"], ((0, 0), (0, 120)))
    p = {"wau": wau, "bau": sp["b_a_up"], "ggla": sp["g_gla_norm"], "cw": sp["conv_w"], "cb": sp["conv_b"],
         "wq": _block_rows(sp["w_q_ml"]), "wk": _block_rows(sp["w_k_ml"]), "wv": _block_rows(sp["w_v_ml"]),
         "wif": wif, "bif": bif, "skip": sp["ml_skip"], "gml": sp["g_ml_norm"]}

    if late_shards is None:
        (pm, gab, h), _ = _in_proj(x, sp["g_pre_mix"], w["w_in"])
        ab, *states = _mixer_fwd(pm, p)
        (x1, mix, merged), _ = _merge_fwd(ab, gab, x, w["w_pa"], w["w_pb"], w["w_o"], sp["g_post_mix"])
    else:
        shard = dict(zip(_LATE, late_shards))
        shard["w_down_a"], shard["w_down_b"] = shard["w_down"][0:512], shard["w_down"][512:1024]
        (pm, gab, h), got = _in_proj(x, sp["g_pre_mix"], w["w_in"], _Gather(_RIDE_IN_PROJ, middle_at=0.7),
                                     [shard[n] for n in _RIDE_IN_PROJ])
        w = dict(w, **_full_weights(dict(zip(_RIDE_IN_PROJ, got))))
        ab, *rest = _mixer_fwd(pm, p, _Gather(_RIDE_MIXER, middle_at=0.62), [shard[n] for n in _RIDE_MIXER])
        states = rest[:4]
        w.update(_full_weights(dict(zip(_RIDE_MIXER, rest[4:]))))
        (x1, mix, merged), _ = _merge_fwd(ab, gab, x, w["w_pa"], w["w_pb"], w["w_o"], sp["g_post_mix"])
    dx1, u, dd, h2, dpre, dg_post_mlp, dg_pre_mlp, loss = _mlp(x1, target, sp["g_pre_mlp"], sp["g_post_mlp"],
                                                                w["w_up"], w["w_down_a"], w["w_down_b"])
    dgab, dab, dg_post_mix, dw_pa, dw_pb, dw_o = _merge_bwd(dx1, mix, ab, gab, merged, w["w_pa"], w["w_pb"], w["w_o"],
                                                            sp["g_post_mix"])
    big = {"w_pa": dw_pa, "w_pb": dw_pb, "w_o": dw_o, "w_up": _tn_matmul(h2, dpre, "dw_up", shards=N_CHIP)}
    if late_shards is None:
        big["w_down"] = _tn_matmul(u, dd, "dw_down")
        dpm, dp, _ = _mixer_bwd(pm, dab, states, p)
    else:
        pieces = lambda n: big[n].reshape((N_CHIP,) + _BIG_SHARD[n])
        big["w_down"], partial = _tn_matmul(u, dd, "dw_down", rider=_Presum(_LATE[:4]),
                                            rider_ins=[pieces(n) for n in _LATE[:4]])
        dpm, dp, parts = _mixer_bwd(pm, dab, states, p, _PresumThenSend(_LATE), list(partial) + [pieces("w_down")])
        big = dict(zip(_LATE, parts))
    big["w_in"] = _dw_in(dpm, dgab, h)
    if late_shards is None:
        (dx, dg_pre_mix), _ = _in_proj_bwd(dpm, dgab, x, dx1, sp["g_pre_mix"], w["w_in"])
    else:
        (dx, dg_pre_mix), parts = _in_proj_bwd(dpm, dgab, x, dx1, sp["g_pre_mix"], w["w_in"], _PresumThenRelay(("w_in",)),
                                               [big["w_in"]])
        big["w_in"] = parts[0]
    small = {
        "g_pre_mix": dg_pre_mix, "b_a_up": dp["bau"], "g_gla_norm": dp["ggla"], "conv_b": dp["cb"],
        "w_q_ml": dp["wq"][:, 0:4].reshape(128, 4, 4), "w_k_ml": dp["wk"][:, 0:4].reshape(128, 4, 4),
        "w_v_ml": dp["wv"][:, 0:4].reshape(128, 4, 4),
        "b_if": dp["bif"][:, 0:8], "ml_skip": dp["skip"], "g_ml_norm": dp["gml"], "g_post_mix": dg_post_mix,
        "g_pre_mlp": dg_pre_mlp, "g_post_mlp": dg_post_mlp, "w_a_up": dp["wau"][0:16], "conv_w": dp["cw"],
        "w_if": dp["wif"][:, 0:8], "loss": loss[:, 0:1],
    }
    return dx, big, small


_SMALL_REPL = ("g_pre_mix", "b_a_up", "g_gla_norm", "conv_b", "w_q_ml", "w_k_ml", "w_v_ml", "b_if", "ml_skip",
               "g_ml_norm", "g_post_mix", "g_pre_mlp", "g_post_mlp")
_SMALL_SHARDED = ("w_a_up", "conv_w", "w_if")
_SMALL_ORDER = _SMALL_REPL + _SMALL_SHARDED + ("loss",)
_WEIGHTS = ("g_pre_mix", "w_in", "w_a_up", "b_a_up", "g_gla_norm", "conv_w", "conv_b", "w_q_ml", "w_k_ml", "w_v_ml",
            "w_if", "b_if", "ml_skip", "g_ml_norm", "w_pa", "w_pb", "w_o", "g_post_mix", "g_pre_mlp", "w_up", "w_down",
            "g_post_mlp")


_BLOCK_WEIGHTS = ("w_q_ml", "w_k_ml", "w_v_ml")


def _stored(name, a):
    if name in _BLOCK_WEIGHTS:
        return jnp.transpose(a, (0, 2, 3, 1)).reshape(16, 128)
    if name == "w_if":
        return jnp.transpose(a, (0, 2, 1)).reshape(8, 384)
    return a


def _unstored(name, a):
    if name in _BLOCK_WEIGHTS:
        return jnp.transpose(a.reshape(1, 4, 4, 128), (0, 3, 1, 2))
    if name == "w_if":
        return jnp.transpose(a.reshape(1, 8, 384), (0, 2, 1))
    return a


def _as_shard(name, a):
    return jnp.transpose(a, (2, 0, 1)).reshape(IN_SHARD, D_MODEL // 128, 128) if name == "w_in" else a[0]


def _in_shard_bf16(w_in):
    return jnp.transpose(w_in.astype(BF16), (2, 0, 1)).reshape(IN_SHARD, D_MODEL)


def _from_shard(name, a):
    return jnp.transpose(a, (1, 2, 0)).reshape(1, D_MODEL, IN_SHARD) if name == "w_in" else a[None]


def kernel(x, g_pre_mix, w_in, w_a_up, b_a_up, g_gla_norm, conv_w, conv_b, w_q_ml, w_k_ml, w_v_ml, w_if, b_if, ml_skip, g_ml_norm, w_pa, w_pb, w_o, g_post_mix, g_pre_mlp, w_up, w_down, g_post_mlp, loss_target, m_g_pre_mix, m_w_in, m_w_a_up, m_b_a_up, m_g_gla_norm, m_conv_w, m_conv_b, m_w_q_ml, m_w_k_ml, m_w_v_ml, m_w_if, m_b_if, m_ml_skip, m_g_ml_norm, m_w_pa, m_w_pb, m_w_o, m_g_post_mix, m_g_pre_mlp, m_w_up, m_w_down, m_g_post_mlp, v_g_pre_mix, v_w_in, v_w_a_up, v_b_a_up, v_g_gla_norm, v_conv_w, v_conv_b, v_w_q_ml, v_w_k_ml, v_w_v_ml, v_w_if, v_b_if, v_ml_skip, v_g_ml_norm, v_w_pa, v_w_pb, v_w_o, v_g_post_mix, v_g_pre_mlp, v_w_up, v_w_down, v_g_post_mlp):
    args = dict(locals())
    wts = {n: _as_shard(n, args[n]) for n in _WEIGHTS}
    mom = {n: _as_shard(n, args["m_" + n]) for n in _WEIGHTS}
    var = {n: _as_shard(n, args["v_" + n]) for n in _WEIGHTS}
    chip = 2 * lax.axis_index("x") + lax.axis_index("y")

    first = ("w_in",) + _SMALL_SHARDED
    gathered = dict(zip(first, _run_alone(_Gather(("w_in",), [wts[n] for n in _SMALL_SHARDED]),
                                          [_in_shard_bf16(w_in)] + [wts[n] for n in _SMALL_SHARDED],
                                          "gather_first")))
    sp = {n: wts[n] for n in _SMALL_REPL}
    sp["w_a_up"] = _cols(gathered["w_a_up"])
    sp["conv_w"] = _cols(gathered["conv_w"])
    sp["w_if"] = gathered["w_if"].reshape(1536, 8)

    dx, big, small = _local_step(x[0], loss_target[0], _full_weights({"w_in": gathered["w_in"]}), sp,
                                 late_shards=[wts[n].astype(BF16) for n in _LATE])

    small_shapes = [small[n].shape for n in _SMALL_ORDER]
    packed = _pack([small[n] for n in _SMALL_ORDER])
    sums, small_sum = _sum_swap(_BIG, [big[n] for n in _BIG], packed)

    grads, delta, new_m, new_v = {}, {}, {}, {}
    for n, g in zip(_BIG, sums):
        if n == "w_in":
            g, d, nm, nv = _adamw_rows(g, wts[n], mom[n], var[n], "adamw_" + n)
        else:
            d, nm, nv = _adamw_big(g, wts[n], mom[n], var[n], "adamw_" + n)
        grads[n], delta[n], new_m[n], new_v[n] = (_from_shard(n, a) for a in (g, d, nm, nv))
    summed = dict(zip(_SMALL_ORDER, _unpack(small_sum, small_shapes)))
    loss = summed["loss"].reshape(())
    summed["w_a_up"] = lax.dynamic_slice_in_dim(summed["w_a_up"], chip * 64, 64, axis=1)
    summed["conv_w"] = lax.dynamic_slice_in_dim(summed["conv_w"], chip * 128, 128, axis=1)
    summed["w_if"] = lax.dynamic_slice_in_dim(summed["w_if"], chip * 384, 384, axis=0)
    small_names = _SMALL_REPL + _SMALL_SHARDED
    g_stored = [_stored(n, summed[n].reshape(args[n].shape)) for n in small_names]
    upd = _adamw_small([_stored(n, args[n]) for n in small_names], g_stored,
                       [_stored(n, args["m_" + n]) for n in small_names], [_stored(n, args["v_" + n]) for n in small_names])
    for dst, arrs in zip((grads, delta, new_m, new_v), (g_stored,) + tuple(upd)):
        dst.update({n: _unstored(n, a) for n, a in zip(small_names, arrs)})

    outs = [loss, dx[None]]
    for group in (grads, delta, new_m, new_v):
        outs += [group[n] for n in _WEIGHTS]
    return tuple(outs)
```

```python
import functools

import jax
import jax.numpy as jnp
from jax import lax
from jax.experimental import pallas as pl
from jax.experimental.pallas import tpu as pltpu

F32 = jnp.float32
BF16 = jnp.bfloat16

SEQ = 2048
D_MODEL = 1024
CHUNK = 64
N_CHUNK = SEQ // CHUNK
HEADS = 4
GLA_DK = 64
GLA_DV = 128
ML_DH = 128
D_FF = 4096
EPS = 1e-6
N_CHIP = 4
N_DEV = 8
TOK_TILE = 256
N_TOK_TILE = SEQ // TOK_TILE
SWEEP = 2
assert CHUNK == 64
N_SWEEP = N_CHUNK // SWEEP

PM_W = 2688
PM_XM = 1536
PM_OP = 2048
PM_AL = 2560
GAB_W = 2048
D_IN = 4624
IN_SHARD = D_IN // N_CHIP
IN_ALOW = 1536
IN_XM = 1552
IN_GATES = 2576

ADAM_LR = 0.001
ADAM_B1 = 0.9
ADAM_B2 = 0.999
ADAM_EPS = 1e-08
ADAM_WD = 0.01
ADAM_STEP = 10

VMEM_LIMIT = 56 * 1024 * 1024


def _params(sem=None):
    return pltpu.CompilerParams(dimension_semantics=sem, vmem_limit_bytes=VMEM_LIMIT)


def _dot(a, b, ca, cb):
    return lax.dot_general(a.astype(BF16), b.astype(BF16), (((ca,), (cb,)), ((), ())), preferred_element_type=F32)


def _pmm_nn(a, b):
    return _dot(a, b, 1, 0)


def _pmm_nt(a, b):
    return _dot(a, b, 1, 1)


def _pmm_tn(a, b):
    return _dot(a, b, 0, 0)


def _pcmm(c, x):
    return lax.dot_general(c, x, (((1,), (0,)), ((), ())), precision=lax.Precision.HIGHEST, preferred_element_type=F32)


@jax.custom_vjp
def _mm_nn(a, b):
    return _dot(a, b, 1, 0)


@jax.custom_vjp
def _mm_nt(a, b):
    return _dot(a, b, 1, 1)


@jax.custom_vjp
def _mm_tn(a, b):
    return _dot(a, b, 0, 0)


_mm_nn.defvjp(lambda a, b: (_dot(a, b, 1, 0), (a, b)), lambda r, g: (_mm_nt(g, r[1]), _mm_tn(r[0], g)))
_mm_nt.defvjp(lambda a, b: (_dot(a, b, 1, 1), (a, b)), lambda r, g: (_mm_nn(g, r[1]), _mm_tn(g, r[0])))
_mm_tn.defvjp(lambda a, b: (_dot(a, b, 0, 0), (a, b)), lambda r, g: (_mm_nt(r[1], g), _mm_nn(r[0], g)))


@jax.custom_vjp
def _cmm(c, x):
    return _pcmm(c, x)


_cmm.defvjp(
    lambda c, x: (_pcmm(c, x), c),
    lambda c, g: (jnp.zeros_like(c), lax.dot_general(c, g, (((0,), (0,)), ((), ())), precision=lax.Precision.HIGHEST,
                                                      preferred_element_type=F32)),
)

_PLAIN_OPS = (_pmm_nn, _pmm_nt, _pmm_tn, _pcmm)
_VJP_OPS = (_mm_nn, _mm_nt, _mm_tn, _cmm)


def _sigmoid(x):
    return 0.5 * (jnp.tanh(0.5 * x) + 1.0)


def _log_sigmoid(x):
    return jnp.minimum(x, 0.0) - jnp.log(1.0 + jnp.exp(-jnp.abs(x)))


def _mean(x):
    return jnp.mean(x, axis=-1, keepdims=True)


def _nt(a, b):
    return lax.dot_general(a, b, (((1,), (1,)), ((), ())), preferred_element_type=F32)


def _tn(a, b):
    return lax.dot_general(a, b, (((0,), (0,)), ((), ())), preferred_element_type=F32)


def _mixer_chunk(ops, p, st, pm, xprev8):
    mm_nn, mm_nt, mm_tn, cmm = ops
    n_rows = pm.shape[0]
    n_ch = n_rows // CHUNK
    row = lax.broadcasted_iota(jnp.int32, (n_rows, n_rows), 0)
    col = lax.broadcasted_iota(jnp.int32, (n_rows, n_rows), 1)
    tri = jnp.logical_and((row >> 6) == (col >> 6), row >= col).astype(F32)
    causal = tri[0:CHUNK, 0:CHUNK] > 0.0
    q = pm[:, 0:256]
    k = pm[:, 256:512]
    v = pm[:, 512:1024]
    g = pm[:, 1024:1536]
    xm = pm[:, PM_XM:PM_XM + 512]
    opre = pm[:, PM_OP:PM_OP + 512]
    alow = pm[:, PM_AL:PM_AL + 128]
    hs = range(HEADS)
    cs = range(n_ch)
    pairs = [(i, h) for i in cs for h in hs]
    rs = [slice(i * CHUNK, (i + 1) * CHUNK) for i in cs]
    last = [slice((i + 1) * CHUNK - 1, (i + 1) * CHUNK) for i in cs]
    s6 = [slice(h * GLA_DK, (h + 1) * GLA_DK) for h in hs]
    s12 = [slice(h * 128, (h + 1) * 128) for h in hs]

    xx = jnp.concatenate([xprev8, xm], axis=0)
    pre = p["cb"]
    for j in range(4):
        pre = pre + p["cw"][j:j + 1, :] * xx[5 + j:5 + j + n_rows, :]
    xc = pre * _sigmoid(pre)
    qm = [mm_nn(xc[:, s12[h]], p["wq"][h]) for h in hs]
    km = [mm_nn(xc[:, s12[h]], p["wk"][h]) for h in hs]
    vm = [mm_nn(xm[:, s12[h]], p["wv"][h]) for h in hs]
    qcat = jnp.concatenate(qm, axis=1)
    kcat = jnp.concatenate(km, axis=1)
    vcat = jnp.concatenate(vm, axis=1)
    gates = (mm_nn(qcat, p["wif"][0:512]) + mm_nn(kcat, p["wif"][512:1024]) + mm_nn(vcat, p["wif"][1024:1536])
             + p["bif"])
    lf = _log_sigmoid(gates)
    fc = cmm(tri, lf)
    gates_t = gates.T
    fc_t = fc.T

    la = _log_sigmoid(mm_nn(alow, p["wau"]) + p["bau"]) * (1.0 / 16.0)
    cum = cmm(tri, la)
    cum_last = [cum[last[i], :] for i in cs]
    to_end = jnp.concatenate([cum_last[i] - cum[rs[i], :] for i in cs], axis=0)
    e_pos = jnp.exp(cum)
    e_neg = jnp.exp(-cum)
    qs = q * (GLA_DK ** -0.5)
    qp = qs * e_pos
    qn = qs * e_neg
    kp = k * e_pos
    kn = k * e_neg
    kl = k * jnp.exp(to_end)
    dec = [jnp.exp(cum_last[i]) for i in cs]
    ks = [km[h] * (ML_DH ** -0.5) for h in hs]
    li_c = {(i, h): gates[rs[i], h:h + 1] for i, h in pairs}
    fc_c = {(i, h): fc[rs[i], 4 + h:5 + h] for i, h in pairs}
    f_last = {(i, h): fc[last[i], 4 + h:5 + h] for i, h in pairs}

    a_fwd = {(i, h): mm_nt(qp[rs[i], s6[h]], kn[rs[i], s6[h]]) for i, h in pairs}
    a_bwd = {(i, h): mm_nt(qn[rs[i], s6[h]], kp[rs[i], s6[h]]) for i, h in pairs}
    s_chunk = {(i, h): mm_tn(v[rs[i], s12[h]], kl[rs[i], s6[h]]) for i, h in pairs}
    qk = {(i, h): mm_nt(qm[h][rs[i]], ks[h][rs[i]]) for i, h in pairs}
    a = {ih: f_last[ih] - fc_c[ih] + li_c[ih] for ih in pairs}
    m_loc = {ih: jnp.max(a[ih], axis=0, keepdims=True) for ih in pairs}
    kw = {(i, h): ks[h][rs[i]] * jnp.exp(a[(i, h)] - m_loc[(i, h)]) for i, h in pairs}
    c_chunk = {(i, h): mm_tn(kw[(i, h)], vm[h][rs[i]]) for i, h in pairs}
    mem = {(0, h): st["S"][h] for h in hs}
    c_in = {(0, h): st["C"][h] for h in hs}
    n_in = {(0, h): st["n"][h] for h in hs}
    m_in = {(0, h): st["m"][h][:, 0:1] for h in hs}
    for i, h in pairs:
        mem[(i + 1, h)] = mem[(i, h)] * dec[i][:, s6[h]] + s_chunk[(i, h)]
        m_nx = jnp.maximum(f_last[(i, h)] + m_in[(i, h)], m_loc[(i, h)])
        sp = jnp.exp(f_last[(i, h)] + m_in[(i, h)] - m_nx)
        sl = jnp.exp(m_loc[(i, h)] - m_nx)
        c_in[(i + 1, h)] = sp * c_in[(i, h)] + sl * c_chunk[(i, h)]
        n_in[(i + 1, h)] = sp * n_in[(i, h)] + sl * jnp.sum(kw[(i, h)], axis=0, keepdims=True)
        m_in[(i + 1, h)] = m_nx
    s_new = [mem[(n_ch, h)] for h in hs]
    o_inter = {(i, h): mm_nt(qp[rs[i], s6[h]], mem[(i, h)]) for i, h in pairs}
    q_c = {(i, h): mm_nn(qm[h][rs[i]], c_in[(i, h)]) for i, h in pairs}
    scores = {ih: jnp.where(causal, a_fwd[ih], a_bwd[ih]) for ih in pairs}
    log_d = {(i, h): gates_t[h:h + 1, rs[i]] - jnp.abs(fc_c[(i, h)] - fc_t[4 + h:5 + h, rs[i]]) for i, h in pairs}
    g_int = {ih: fc_c[ih] + m_in[ih] for ih in pairs}
    m_t = {ih: jnp.maximum(g_int[ih], jnp.max(log_d[ih], axis=1, keepdims=True)) for ih in pairs}
    s = {ih: qk[ih] * jnp.exp(log_d[ih] - m_t[ih]) for ih in pairs}
    scl = {ih: jnp.exp(g_int[ih] - m_t[ih]) for ih in pairs}
    o = {(i, h): mm_nn(scores[(i, h)], v[rs[i], s12[h]]) + o_inter[(i, h)] for i, h in pairs}
    num = {(i, h): mm_nn(s[(i, h)], vm[h][rs[i]]) + scl[(i, h)] * q_c[(i, h)] for i, h in pairs}
    o = {ih: o[ih] * lax.rsqrt(_mean(o[ih] * o[ih]) + EPS) * p["ggla"] for ih in pairs}
    gate = g * _sigmoid(g)
    out_a = {(i, h): o[(i, h)] * gate[rs[i], s12[h]] for i, h in pairs}
    den = {(i, h): jnp.sum(s[(i, h)], axis=1, keepdims=True)
           + scl[(i, h)] * jnp.sum(qm[h][rs[i]] * n_in[(i, h)], axis=1, keepdims=True) for i, h in pairs}
    den = {ih: jnp.maximum(jnp.abs(den[ih]), jnp.exp(-m_t[ih])) for ih in pairs}
    open_gate = _sigmoid(opre)
    hc = {(i, h): num[(i, h)] / den[(i, h)] * open_gate[rs[i], s12[h]] for i, h in pairs}
    d0 = {ih: hc[ih] - _mean(hc[ih]) for ih in pairs}
    y = {ih: d0[ih] * lax.rsqrt(_mean(d0[ih] * d0[ih]) + EPS) for ih in pairs}
    skipped = p["skip"] * xc
    out_b = {(i, h): y[(i, h)] * p["gml"][:, s12[h]] + skipped[rs[i], s12[h]] for i, h in pairs}
    ab = jnp.concatenate([jnp.concatenate([out_a[(i, h)] for h in hs] + [out_b[(i, h)] for h in hs], axis=1) for i in cs],
                         axis=0)
    new = {"S": s_new, "C": [c_in[(n_ch, h)] for h in hs], "n": [n_in[(n_ch, h)] for h in hs],
           "m": [jnp.broadcast_to(m_in[(n_ch, h)], (1, ML_DH)) for h in hs]}
    return ab, new


_P_NAMES = ("wau", "bau", "ggla", "cw", "cb", "wq", "wk", "wv", "wif", "bif", "skip", "gml")
_P_SHAPES = {
    "wau": (128, 256), "bau": (1, 256), "ggla": (1, 128), "cw": (4, 512), "cb": (1, 512),
    "wq": (512, 128), "wk": (512, 128), "wv": (512, 128),
    "wif": (1536, 128), "bif": (1, 128), "skip": (1, 512), "gml": (1, 512),
}
_P_BLOCKDIAG = ("wq", "wk", "wv")
_S_NAMES = ("S", "C", "n", "m")
_S_SHAPES = {"S": (HEADS, GLA_DV, GLA_DK), "C": (HEADS, ML_DH, ML_DH), "n": (HEADS, 1, ML_DH), "m": (HEADS, 1, ML_DH)}


def _per_head(ref):
    return [ref[h] for h in range(HEADS)]


def _block_mask():
    r = lax.broadcasted_iota(jnp.int32, (128, 128), 0)
    c = lax.broadcasted_iota(jnp.int32, (128, 128), 1)
    same_block = (r >> 2) == (c >> 2)
    spread = jnp.logical_and(r < 4, (c & 3) == r)
    return same_block.astype(F32), spread.astype(F32)


def _expand_blockdiag(w_ref, dense_ref):
    same_block, spread = _block_mask()
    for h in range(HEADS):
        tiled = _pmm_nn(w_ref[h * 128:(h + 1) * 128, :], spread)
        dense_ref[h] = tiled * same_block


def _collect_blockdiag(ddense_ref, dw_ref):
    same_block, spread = _block_mask()
    for h in range(HEADS):
        dw_ref[h * 128:(h + 1) * 128, :] = lax.dot_general(
            ddense_ref[h] * same_block, spread, (((1,), (1,)), ((), ())), precision=lax.Precision.HIGHEST,
            preferred_element_type=F32)


def _const_spec(shape):
    zeros = (0,) * len(shape)
    return pl.BlockSpec(shape, lambda i: zeros)


def _split(refs, *counts):
    out, at = [], 0
    for c in counts:
        out.append(refs[at:at + c])
        at += c
    assert at == len(refs)
    return out


def _ride(rider, phases, cond, ins, outs, sems):
    if rider is None or not any(hasattr(rider, phase) for phase in phases):
        return
    lands, (send_sems, recv_sems, flush_sems) = sems[:-3], sems[-3:]

    @pl.when(cond)
    def _():
        for phase in phases:
            if phase == "last" and hasattr(rider, "late"):
                rider.late(ins, lands, send_sems, recv_sems)
                rider.flush("late", lands, outs, flush_sems)
            getattr(rider, phase)(ins, lands, send_sems, recv_sems)
            if hasattr(rider, "flush"):
                rider.flush(phase, lands, outs, flush_sems)
        if "last" in phases and not hasattr(rider, "flush"):
            flush = [pltpu.make_async_copy(lands[k], outs[k], flush_sems.at[k]) for k in range(len(outs))]
            for cp in flush:
                cp.start()
            for cp in flush:
                cp.wait()


def _middle_step(rider, n_steps):
    return min(n_steps - 2, int(getattr(rider, "middle_at", 1.0) * n_steps))


def _rider_specs(rider, rider_ins):
    if rider is None:
        return [], [], [], []
    scratch = [pltpu.VMEM(s.shape, s.dtype) for s in list(rider.out_shape) + list(getattr(rider, "work_shape", ()))]
    scratch += [pltpu.SemaphoreType.DMA((rider.n_sems,)), pltpu.SemaphoreType.DMA((rider.n_sems,)),
                pltpu.SemaphoreType.DMA((getattr(rider, "n_flush", len(rider.out_shape)),))]
    in_space = getattr(rider, "in_space", VMEM_WHOLE)
    in_specs = list(in_space) if isinstance(in_space, (list, tuple)) else [in_space] * len(rider_ins)
    return in_specs, [ANY] * len(rider.out_shape), list(rider.out_shape), scratch


def _mixer_fwd(pm, p, rider=None, rider_ins=()):
    n_p = len(_P_NAMES)
    r_in, r_out_specs, r_out_shape, r_sems = _rider_specs(rider, rider_ins)

    def body(*refs):
        (pm_ref, xprev_ref), p_list, ride_in, (ab_ref,), so_refs, ride_out, sc_refs, dense_list, sems = _split(
            refs, 2, n_p, len(r_in), 1, 4, len(r_out_specs), 4, 3, len(r_sems))
        p_refs = dict(zip(_P_NAMES, p_list))
        dense = dict(zip(_P_BLOCKDIAG, dense_list))
        n = pl.program_id(0)
        _ride(rider, ("first",), n == 0, ride_in, ride_out, sems)

        @pl.when(n == 0)
        def _():
            for r in sc_refs:
                r[...] = jnp.zeros_like(r)
            for nm in _P_BLOCKDIAG:
                _expand_blockdiag(p_refs[nm], dense[nm])

        st = {name: _per_head(r) for name, r in zip(_S_NAMES, sc_refs)}
        pv = {nm: (_per_head(dense[nm]) if nm in _P_BLOCKDIAG else p_refs[nm][...]) for nm in _P_NAMES}
        for name, r in zip(_S_NAMES, so_refs):
            for h in range(HEADS):
                r[0, h] = st[name][h]
        xprev8 = jnp.where(n > 0, xprev_ref[CHUNK - 8:CHUNK, :], 0.0)
        ab, st = _mixer_chunk(_PLAIN_OPS, pv, st, pm_ref[...], xprev8)
        ab_ref[...] = ab.astype(BF16)
        for name, r in zip(_S_NAMES, sc_refs):
            for h in range(HEADS):
                r[h] = st[name][h]
        _ride(rider, ("middle",), n == _middle_step(rider, N_SWEEP), ride_in, ride_out, sems)
        _ride(rider, ("last",), n == N_SWEEP - 1, ride_in, ride_out, sems)

    in_specs = [pl.BlockSpec((SWEEP * CHUNK, PM_W), lambda i: (i, 0)),
                pl.BlockSpec((CHUNK, 512), lambda i: (jnp.maximum(SWEEP * i - 1, 0), PM_XM // 512))]
    in_specs += [_const_spec(_P_SHAPES[nm]) for nm in _P_NAMES] + r_in
    out_specs = [pl.BlockSpec((SWEEP * CHUNK, 1024), lambda i: (i, 0))]
    out_shape = [jax.ShapeDtypeStruct((SEQ, 1024), BF16)]
    for nm in _S_NAMES:
        shp = _S_SHAPES[nm]
        out_specs.append(pl.BlockSpec((1,) + shp, lambda i: (i, 0, 0, 0)))
        out_shape.append(jax.ShapeDtypeStruct((N_SWEEP,) + shp, F32))
    return pl.pallas_call(
        body, grid=(N_SWEEP,), in_specs=in_specs, out_specs=out_specs + r_out_specs, out_shape=out_shape + r_out_shape,
        scratch_shapes=[pltpu.VMEM(_S_SHAPES[nm], F32) for nm in _S_NAMES]
        + [pltpu.VMEM((HEADS, 128, 128), F32) for _ in _P_BLOCKDIAG] + r_sems,
        compiler_params=_params(("arbitrary",)), name="mixer_fwd",
    )(pm, pm, *[p[nm] for nm in _P_NAMES], *rider_ins)


def _mixer_bwd(pm, dab, states, p, rider=None, rider_ins=()):
    n_p = len(_P_NAMES)
    r_in, r_out_specs, r_out_shape, r_sems = _rider_specs(rider, rider_ins)

    def body(*refs):
        ((pm_ref, xprev_ref, dab_ref), si_refs, p_list, ride_in, (dpm_ref,), dp_list, ride_out, ds_refs, (carry_ref,),
         dense_list, ddense_list, sems) = _split(refs, 3, 4, n_p, len(r_in), 1, n_p, len(r_out_specs), 4, 1, 3, 3, len(r_sems))
        p_refs = dict(zip(_P_NAMES, p_list))
        dp_refs = dict(zip(_P_NAMES, dp_list))
        dense = dict(zip(_P_BLOCKDIAG, dense_list))
        ddense = dict(zip(_P_BLOCKDIAG, ddense_list))
        i = pl.program_id(0)
        blk = N_SWEEP - 1 - i
        _ride(rider, ("first",), i == 0, ride_in, ride_out, sems)

        @pl.when(i == 0)
        def _():
            for r in ds_refs:
                r[...] = jnp.zeros_like(r)
            for nm in _P_NAMES:
                if nm in _P_BLOCKDIAG:
                    ddense[nm][...] = jnp.zeros_like(ddense[nm])
                    _expand_blockdiag(p_refs[nm], dense[nm])
                else:
                    dp_refs[nm][...] = jnp.zeros_like(dp_refs[nm])
            carry_ref[...] = jnp.zeros_like(carry_ref)

        pv = {nm: (_per_head(dense[nm]) if nm in _P_BLOCKDIAG else p_refs[nm][...]) for nm in _P_NAMES}
        dst = {name: _per_head(r) for name, r in zip(_S_NAMES, ds_refs)}
        st = {name: [r[0, h] for h in range(HEADS)] for name, r in zip(_S_NAMES, si_refs)}
        xprev8 = jnp.where(blk > 0, xprev_ref[CHUNK - 8:CHUNK, :], 0.0)
        _, vjp = jax.vjp(functools.partial(_mixer_chunk, _VJP_OPS), pv, st, pm_ref[...], xprev8)
        dp_sum, dst, dpm, dxprev8 = vjp((dab_ref[...], dst))
        reach = jnp.concatenate([jnp.zeros((SWEEP * CHUNK - 8, 512), F32), carry_ref[...]], axis=0)
        dpm_ref[:, 0:PM_XM] = dpm[:, 0:PM_XM].astype(BF16)
        dpm_ref[:, PM_XM:PM_XM + 512] = (dpm[:, PM_XM:PM_XM + 512] + reach).astype(BF16)
        dpm_ref[:, PM_XM + 512:PM_W] = dpm[:, PM_XM + 512:PM_W].astype(BF16)
        carry_ref[...] = dxprev8
        for name, r in zip(_S_NAMES, ds_refs):
            for h in range(HEADS):
                r[h] = dst[name][h]
        for nm in _P_NAMES:
            if nm in _P_BLOCKDIAG:
                for h in range(HEADS):
                    ddense[nm][h] += dp_sum[nm][h]
            else:
                dp_refs[nm][...] += dp_sum[nm]

        @pl.when(i == N_SWEEP - 1)
        def _():
            for nm in _P_BLOCKDIAG:
                _collect_blockdiag(ddense[nm], dp_refs[nm])

        _ride(rider, ("early",), i == 1, ride_in, ride_out, sems)
        _ride(rider, ("middle",), i == _middle_step(rider, N_SWEEP), ride_in, ride_out, sems)
        _ride(rider, ("last",), i == N_SWEEP - 1, ride_in, ride_out, sems)

    rev = lambda i: (N_SWEEP - 1 - i, 0)
    in_specs = [pl.BlockSpec((SWEEP * CHUNK, PM_W), rev),
                pl.BlockSpec((CHUNK, 512), lambda i: (jnp.maximum(SWEEP * (N_SWEEP - 1 - i) - 1, 0), PM_XM // 512)),
                pl.BlockSpec((SWEEP * CHUNK, 1024), rev)]
    for nm in _S_NAMES:
        in_specs.append(pl.BlockSpec((1,) + _S_SHAPES[nm], lambda i: (N_SWEEP - 1 - i, 0, 0, 0)))
    in_specs += [_const_spec(_P_SHAPES[nm]) for nm in _P_NAMES] + r_in
    out_specs = [pl.BlockSpec((SWEEP * CHUNK, PM_W), rev)] + [_const_spec(_P_SHAPES[nm]) for nm in _P_NAMES]
    out_shape = [jax.ShapeDtypeStruct((SEQ, PM_W), BF16)] + [jax.ShapeDtypeStruct(_P_SHAPES[nm], F32) for nm in _P_NAMES]
    res = pl.pallas_call(
        body, grid=(N_SWEEP,), in_specs=in_specs, out_specs=out_specs + r_out_specs, out_shape=out_shape + r_out_shape,
        scratch_shapes=[pltpu.VMEM(_S_SHAPES[nm], F32) for nm in _S_NAMES] + [pltpu.VMEM((8, 512), F32)]
        + [pltpu.VMEM((HEADS, 128, 128), F32) for _ in range(2 * len(_P_BLOCKDIAG))] + r_sems,
        compiler_params=_params(("arbitrary",)), name="mixer_bwd",
    )(pm, pm, dab, *states, *[p[nm] for nm in _P_NAMES], *rider_ins)
    return res[0], dict(zip(_P_NAMES, res[1:1 + n_p])), res[1 + n_p:]


def _tok(width):
    return pl.BlockSpec((TOK_TILE, width), lambda i: (i, 0))


def _once(shape):
    zeros = (0,) * len(shape)
    return pl.BlockSpec(shape, lambda i: zeros, pipeline_mode=pl.Buffered(1))


def _rms_fwd(x):
    r = lax.rsqrt(_mean(x * x) + EPS)
    return x * r, r


def _rms_bwd(dy, xn, r, g):
    gd = dy * g
    return r * (gd - xn * _mean(xn * gd))


def _tiled_call(body, in_specs, out_specs, out_shape, args, name, rider=None, rider_ins=(), scratch=()):
    r_in, r_out_specs, r_out_shape, r_scratch = _rider_specs(rider, rider_ins)
    n_in, n_out = len(in_specs), len(out_specs)

    def hosted(*refs):
        ins, ride_in, outs, ride_out, own, r_scr = _split(refs, n_in, len(r_in), n_out, len(r_out_specs), len(scratch),
                                                          len(r_scratch))
        i = pl.program_id(0)
        _ride(rider, ("first",), i == 0, ride_in, ride_out, r_scr)
        body(*ins, *outs, *own)
        _ride(rider, ("early",), i == 1, ride_in, ride_out, r_scr)
        _ride(rider, ("middle",), i == _middle_step(rider, N_TOK_TILE), ride_in, ride_out, r_scr)
        _ride(rider, ("last",), i == N_TOK_TILE - 1, ride_in, ride_out, r_scr)

    res = pl.pallas_call(
        hosted, grid=(N_TOK_TILE,), in_specs=list(in_specs) + r_in, out_specs=list(out_specs) + r_out_specs,
        out_shape=list(out_shape) + r_out_shape, scratch_shapes=list(scratch) + r_scratch,
        compiler_params=_params(("arbitrary",)), name=name,
    )(*args, *rider_ins)
    return res[:n_out], res[n_out:]


def _join_rows(w4_ref, wt_ref):
    @pl.when(pl.program_id(0) == 0)
    def _():
        for j in range(N_CHIP):
            wt_ref[j * IN_SHARD:(j + 1) * IN_SHARD, :] = w4_ref[j]


def _joined_scratch():
    return [pltpu.VMEM((D_IN, D_MODEL), BF16)]


def _in_proj(x, g_pre, w4_in, rider=None, rider_ins=()):
    def body(x_ref, g_ref, w4_ref, pm_ref, gab_ref, h_ref, wt_ref):
        _join_rows(w4_ref, wt_ref)
        xn, _ = _rms_fwd(x_ref[...])
        h = (xn * g_ref[...]).astype(BF16)
        h_ref[...] = h
        pm_ref[:, 0:PM_XM] = _nt(h, wt_ref[0:IN_ALOW, :])
        pm_ref[:, PM_XM:PM_AL] = _nt(h, wt_ref[IN_XM:IN_GATES, :])
        pm_ref[:, PM_AL:PM_W] = _nt(h, wt_ref[IN_ALOW:IN_ALOW + 128, :])
        gab_ref[...] = _nt(h, wt_ref[IN_GATES:D_IN, :])

    return _tiled_call(
        body, [_tok(D_MODEL), _once((1, D_MODEL)), _once((N_CHIP, IN_SHARD, D_MODEL))],
        [_tok(PM_W), _tok(GAB_W), _tok(D_MODEL)],
        [jax.ShapeDtypeStruct((SEQ, PM_W), F32), jax.ShapeDtypeStruct((SEQ, GAB_W), F32),
         jax.ShapeDtypeStruct((SEQ, D_MODEL), BF16)], (x, g_pre, w4_in), "in_proj", rider, rider_ins, _joined_scratch())


def _merge_fwd(ab, gab, x, w_pa4, w_pb4, w_o, g_post, rider=None, rider_ins=()):
    def body(ab_ref, gab_ref, x_ref, wpa_ref, wpb_ref, wo_ref, g_ref, x1_ref, mix_ref, mg_ref):
        a = ab_ref[:, 0:512]
        b = ab_ref[:, 512:1024]
        for j in range(N_CHIP):
            blk = slice(j * 256, (j + 1) * 256)
            ya = jnp.dot(a, wpa_ref[j], preferred_element_type=F32)
            yb = jnp.dot(b, wpb_ref[j], preferred_element_type=F32)
            sa = _sigmoid(gab_ref[:, j * 256:(j + 1) * 256])
            sb = _sigmoid(gab_ref[:, 1024 + j * 256:1024 + (j + 1) * 256])
            mg_ref[:, blk] = (sa * ya + sb * yb).astype(BF16)
        mix = jnp.dot(mg_ref[...], wo_ref[...], preferred_element_type=F32)
        mix_ref[...] = mix
        mn, _ = _rms_fwd(mix)
        x1_ref[...] = x_ref[...] + mn * g_ref[...]

    return _tiled_call(
        body, [_tok(1024), _tok(GAB_W), _tok(D_MODEL), _once((N_CHIP, 512, 256)), _once((N_CHIP, 512, 256)),
               _once((D_MODEL, D_MODEL)), _once((1, D_MODEL))], [_tok(D_MODEL), _tok(D_MODEL), _tok(D_MODEL)],
        [jax.ShapeDtypeStruct((SEQ, D_MODEL), F32), jax.ShapeDtypeStruct((SEQ, D_MODEL), F32),
         jax.ShapeDtypeStruct((SEQ, D_MODEL), BF16)], (ab, gab, x, w_pa4, w_pb4, w_o, g_post), "merge_fwd", rider, rider_ins)


def _mlp(x1, target, g_pre, g_post, w_up4, w_down_a4, w_down_b4):
    def body(x1_ref, t_ref, gpre_ref, gpost_ref, wup_ref, wda_ref, wdb_ref,
             dx1_ref, u_ref, dd_ref, h2_ref, dpre_ref, dgpost_ref, dgpre_ref, loss_ref):
        @pl.when(pl.program_id(0) == 0)
        def _():
            dgpost_ref[...] = jnp.zeros_like(dgpost_ref)
            dgpre_ref[...] = jnp.zeros_like(dgpre_ref)
            loss_ref[...] = jnp.zeros_like(loss_ref)

        x1 = x1_ref[...]
        gpre = gpre_ref[...]
        gpost = gpost_ref[...]
        xn2, r2 = _rms_fwd(x1)
        h2 = (xn2 * gpre).astype(BF16)
        h2_ref[...] = h2
        rl = []
        d = jnp.zeros((TOK_TILE, D_MODEL), F32)
        for j in range(N_CHIP):
            blk = slice(j * 1024, (j + 1) * 1024)
            r = jnp.maximum(jnp.dot(h2, wup_ref[j], preferred_element_type=F32), 0.0)
            rl.append(r)
            u = (r * r).astype(BF16)
            u_ref[:, blk] = u
            d = d + jnp.dot(u[:, 0:512], wda_ref[j], preferred_element_type=F32)
            d = d + jnp.dot(u[:, 512:1024], wdb_ref[j], preferred_element_type=F32)
        dn, r3 = _rms_fwd(d)
        diff = x1 + dn * gpost - t_ref[...]
        loss_ref[...] += jnp.sum(diff * diff, keepdims=True) * (0.5 / D_MODEL)
        dy = diff * (1.0 / D_MODEL)
        dgpost_ref[...] += jnp.sum(dy * dn, axis=0, keepdims=True)
        dd = _rms_bwd(dy, dn, r3, gpost).astype(BF16)
        dd_ref[...] = dd
        dh2 = jnp.zeros((TOK_TILE, D_MODEL), F32)
        for j in range(N_CHIP):
            blk = slice(j * 1024, (j + 1) * 1024)
            du = jnp.concatenate([_nt(dd, wda_ref[j]), _nt(dd, wdb_ref[j])], axis=1)
            dpre = (du * (2.0 * rl[j])).astype(BF16)
            dpre_ref[:, blk] = dpre
            dh2 = dh2 + _nt(dpre, wup_ref[j])
        dgpre_ref[...] += jnp.sum(dh2 * xn2, axis=0, keepdims=True)
        dx1_ref[...] = dy + _rms_bwd(dh2, xn2, r2, gpre)

    acc = pl.BlockSpec((1, D_MODEL), lambda i: (0, 0))
    return pl.pallas_call(
        body, grid=(N_TOK_TILE,),
        in_specs=[_tok(D_MODEL), _tok(D_MODEL), _once((1, D_MODEL)), _once((1, D_MODEL)),
                  _once((N_CHIP, D_MODEL, 1024)), _once((N_CHIP, 512, D_MODEL)), _once((N_CHIP, 512, D_MODEL))],
        out_specs=[_tok(D_MODEL), _tok(D_FF), _tok(D_MODEL), _tok(D_MODEL), _tok(D_FF), acc, acc,
                   pl.BlockSpec((1, 128), lambda i: (0, 0))],
        out_shape=[jax.ShapeDtypeStruct((SEQ, D_MODEL), F32), jax.ShapeDtypeStruct((SEQ, D_FF), BF16),
                   jax.ShapeDtypeStruct((SEQ, D_MODEL), BF16), jax.ShapeDtypeStruct((SEQ, D_MODEL), BF16),
                   jax.ShapeDtypeStruct((SEQ, D_FF), BF16), jax.ShapeDtypeStruct((1, D_MODEL), F32),
                   jax.ShapeDtypeStruct((1, D_MODEL), F32), jax.ShapeDtypeStruct((1, 128), F32)],
        compiler_params=_params(("arbitrary",)), name="mlp_fwd_bwd",
    )(x1, target, g_pre, g_post, w_up4, w_down_a4, w_down_b4)


def _merge_bwd(dx1, mix, ab, gab, merged, w_pa4, w_pb4, w_o, g_post):
    def body(dx1_ref, mix_ref, ab_ref, gab_ref, mg_ref, wpa_ref, wpb_ref, wo_ref, g_ref,
             dgab_ref, dab_ref, dg_ref, dwpa_ref, dwpb_ref, dwo_ref, acc_pa, acc_pb, acc_o):
        @pl.when(pl.program_id(0) == 0)
        def _():
            dg_ref[...] = jnp.zeros_like(dg_ref)
            acc_pa[...] = jnp.zeros_like(acc_pa)
            acc_pb[...] = jnp.zeros_like(acc_pb)
            acc_o[...] = jnp.zeros_like(acc_o)

        dx1 = dx1_ref[...]
        mn, r = _rms_fwd(mix_ref[...])
        dg_ref[...] += jnp.sum(dx1 * mn, axis=0, keepdims=True)
        dmix = _rms_bwd(dx1, mn, r, g_ref[...]).astype(BF16)
        acc_o[...] += _tn(mg_ref[...], dmix)
        dmerged = _nt(dmix, wo_ref[...])
        a = ab_ref[:, 0:512]
        b = ab_ref[:, 512:1024]
        da = jnp.zeros((TOK_TILE, 512), F32)
        db = jnp.zeros((TOK_TILE, 512), F32)
        dyas, dybs = [], []
        for j in range(N_CHIP):
            blk = slice(j * 256, (j + 1) * 256)
            blk_b = slice(1024 + j * 256, 1024 + (j + 1) * 256)
            dm = dmerged[:, blk]
            ya = jnp.dot(a, wpa_ref[j], preferred_element_type=F32)
            yb = jnp.dot(b, wpb_ref[j], preferred_element_type=F32)
            sa = _sigmoid(gab_ref[:, blk])
            sb = _sigmoid(gab_ref[:, blk_b])
            dya = (dm * sa).astype(BF16)
            dyb = (dm * sb).astype(BF16)
            dyas.append(dya)
            dybs.append(dyb)
            dgab_ref[:, blk] = (dm * ya * sa * (1.0 - sa)).astype(BF16)
            dgab_ref[:, blk_b] = (dm * yb * sb * (1.0 - sb)).astype(BF16)
            da = da + _nt(dya, wpa_ref[j])
            db = db + _nt(dyb, wpb_ref[j])
        dab_ref[:, 0:512] = da
        dab_ref[:, 512:1024] = db
        acc_pa[...] += _tn(a, jnp.concatenate(dyas, axis=1))
        acc_pb[...] += _tn(b, jnp.concatenate(dybs, axis=1))

        @pl.when(pl.program_id(0) == N_TOK_TILE - 1)
        def _():
            dwo_ref[...] = acc_o[...].astype(BF16)
            for j in range(N_CHIP):
                dwpa_ref[j] = acc_pa[:, j * 256:(j + 1) * 256].astype(BF16)
                dwpb_ref[j] = acc_pb[:, j * 256:(j + 1) * 256].astype(BF16)

    whole = lambda shape: pl.BlockSpec(shape, lambda i: (0,) * len(shape))
    return pl.pallas_call(
        body, grid=(N_TOK_TILE,),
        in_specs=[_tok(D_MODEL), _tok(D_MODEL), _tok(1024), _tok(GAB_W), _tok(D_MODEL), _once((N_CHIP, 512, 256)),
                  _once((N_CHIP, 512, 256)), _once((D_MODEL, D_MODEL)), _once((1, D_MODEL))],
        out_specs=[_tok(GAB_W), _tok(1024), whole((1, D_MODEL)), whole((N_CHIP, 512, 256)), whole((N_CHIP, 512, 256)),
                   whole((D_MODEL, D_MODEL))],
        out_shape=[jax.ShapeDtypeStruct((SEQ, GAB_W), BF16), jax.ShapeDtypeStruct((SEQ, 1024), F32),
                   jax.ShapeDtypeStruct((1, D_MODEL), F32), jax.ShapeDtypeStruct((N_CHIP, 512, 256), BF16),
                   jax.ShapeDtypeStruct((N_CHIP, 512, 256), BF16), jax.ShapeDtypeStruct((D_MODEL, D_MODEL), BF16)],
        scratch_shapes=[pltpu.VMEM((512, D_MODEL), F32), pltpu.VMEM((512, D_MODEL), F32),
                        pltpu.VMEM((D_MODEL, D_MODEL), F32)],
        compiler_params=_params(("arbitrary",)), name="merge_bwd",
    )(dx1, mix, ab, gab, merged, w_pa4, w_pb4, w_o, g_post)


def _in_proj_bwd(dpm, dgab, x, dx1, g_pre, w4_in, rider=None, rider_ins=()):
    def body(dpm_ref, dgab_ref, x_ref, dx1_ref, g_ref, w4_ref, dx_ref, dg_ref, wt_ref):
        _join_rows(w4_ref, wt_ref)

        @pl.when(pl.program_id(0) == 0)
        def _():
            dg_ref[...] = jnp.zeros_like(dg_ref)

        dh = jnp.dot(dpm_ref[:, 0:PM_XM], wt_ref[0:IN_ALOW, :], preferred_element_type=F32)
        dh = dh + jnp.dot(dpm_ref[:, PM_XM:PM_AL], wt_ref[IN_XM:IN_GATES, :], preferred_element_type=F32)
        dh = dh + jnp.dot(dpm_ref[:, PM_AL:PM_W], wt_ref[IN_ALOW:IN_ALOW + 128, :], preferred_element_type=F32)
        dh = dh + jnp.dot(dgab_ref[...], wt_ref[IN_GATES:D_IN, :], preferred_element_type=F32)
        xn, r = _rms_fwd(x_ref[...])
        dg_ref[...] += jnp.sum(dh * xn, axis=0, keepdims=True)
        dx_ref[...] = dx1_ref[...] + _rms_bwd(dh, xn, r, g_ref[...])

    return _tiled_call(
        body, [_tok(PM_W), _tok(GAB_W), _tok(D_MODEL), _tok(D_MODEL), _once((1, D_MODEL)),
               _once((N_CHIP, IN_SHARD, D_MODEL))],
        [_tok(D_MODEL), pl.BlockSpec((1, D_MODEL), lambda i: (0, 0))],
        [jax.ShapeDtypeStruct((SEQ, D_MODEL), F32), jax.ShapeDtypeStruct((1, D_MODEL), F32)],
        (dpm, dgab, x, dx1, g_pre, w4_in), "in_proj_bwd", rider, rider_ins, _joined_scratch())


def _dw_in(dpm, dgab, h):
    n_pm = PM_AL // 512
    n_blk = n_pm + GAB_W // 512

    def place(o_ref, rows, lo, hi):
        for j in range(N_CHIP):
            a, b = max(lo, j * IN_SHARD), min(hi, (j + 1) * IN_SHARD)
            if a < b:
                o_ref[j, a - j * IN_SHARD:b - j * IN_SHARD, :] = rows(a - lo, b - lo)

    def body(dpm_ref, dgab_ref, dal_ref, h_ref, o_ref, blk_ref):
        i = pl.program_id(0)

        @pl.when(i < n_pm)
        def _():
            blk_ref[...] = _tn(dpm_ref[...], h_ref[...]).astype(BF16)

        @pl.when(i >= n_pm)
        def _():
            blk_ref[...] = _tn(dgab_ref[...], h_ref[...]).astype(BF16)

        for k in range(n_blk):
            off = k * 512 + (IN_XM - IN_ALOW) * (k >= IN_ALOW // 512)

            @pl.when(i == k)
            def _():
                place(o_ref, lambda a, b: blk_ref[a:b, :], off, off + 512)

        @pl.when(i == 0)
        def _():
            a_low = _tn(dal_ref[...], h_ref[...])[0:IN_XM - IN_ALOW].astype(BF16)
            place(o_ref, lambda a, b: a_low[a:b], IN_ALOW, IN_XM)

    return pl.pallas_call(
        body, grid=(n_blk,),
        in_specs=[pl.BlockSpec((SEQ, 512), lambda i: (0, jnp.minimum(i, n_pm - 1))),
                  pl.BlockSpec((SEQ, 512), lambda i: (0, jnp.maximum(i - n_pm, 0))),
                  pl.BlockSpec((SEQ, 128), lambda i: (0, PM_AL // 128)),
                  _once((SEQ, D_MODEL))],
        out_specs=pl.BlockSpec((N_CHIP, IN_SHARD, D_MODEL), lambda i: (0, 0, 0)),
        out_shape=jax.ShapeDtypeStruct((N_CHIP, IN_SHARD, D_MODEL), BF16),
        scratch_shapes=[pltpu.VMEM((512, D_MODEL), BF16)],
        compiler_params=_params(("arbitrary",)), name="dw_in",
    )(dpm, dgab, dpm, h)


def _tn_matmul(a, b, name, shards=1, tm=1024, rider=None, rider_ins=()):
    m, n = a.shape[1], b.shape[1]
    tm = min(tm, m)
    tn = n // shards if shards > 1 else min(n, 1024)
    steps_i, steps_j = m // tm, n // tn
    r_in, r_out_specs, r_out_shape, r_scratch = _rider_specs(rider, rider_ins)

    def body(*refs):
        (a_ref, b_ref), ride_in, (o_ref,), ride_out, scratch = _split(refs, 2, len(r_in), 1, len(r_out_specs), len(r_scratch))
        step = pl.program_id(0) * steps_j + pl.program_id(1)
        _ride(rider, ("first",), step == 0, ride_in, ride_out, scratch)
        o_ref[...] = _tn(a_ref[...], b_ref[...]).astype(BF16)
        _ride(rider, ("middle", "last"), step == steps_i * steps_j - 1, ride_in, ride_out, scratch)

    if shards > 1:
        out_spec = pl.BlockSpec((None, tm, tn), lambda i, j: (j, i, 0))
        out_shape = jax.ShapeDtypeStruct((shards, m, tn), BF16)
    else:
        out_spec = pl.BlockSpec((tm, tn), lambda i, j: (i, j))
        out_shape = jax.ShapeDtypeStruct((m, n), BF16)
    res = pl.pallas_call(
        body, grid=(steps_i, steps_j),
        in_specs=[pl.BlockSpec((SEQ, tm), lambda i, j: (0, i)), pl.BlockSpec((SEQ, tn), lambda i, j: (0, j))] + r_in,
        out_specs=[out_spec] + r_out_specs, out_shape=[out_shape] + r_out_shape, scratch_shapes=r_scratch,
        compiler_params=_params(("arbitrary", "arbitrary")), name=name,
    )(a, b, *rider_ins)
    return res[0] if rider is None else (res[0], res[1:])


MESH = pl.DeviceIdType.MESH
ANY = pl.BlockSpec(memory_space=pl.ANY)
VMEM_WHOLE = pl.BlockSpec(memory_space=pltpu.VMEM)

_BIG = ("w_in", "w_pa", "w_pb", "w_o", "w_up", "w_down")
_BIG_SHARD = {"w_in": (IN_SHARD, D_MODEL), "w_pa": (512, 256), "w_pb": (512, 256), "w_o": (256, D_MODEL),
              "w_up": (D_MODEL, 1024), "w_down": (1024, D_MODEL),
              "w_down_a": (512, D_MODEL), "w_down_b": (512, D_MODEL)}
_BIG_SPLIT = {"w_in": 1, "w_pa": 0, "w_pb": 0, "w_o": 0, "w_up": 0, "w_down": 0, "w_down_a": 0, "w_down_b": 0}


def _half(ref, e, name, lead=0, part=None):
    axis = _BIG_SPLIT[name]
    size = _BIG_SHARD[name][axis] // 2
    start = e * size
    if part is not None:
        size //= 2
        start = start + part * size
    start = pl.multiple_of(start, 128 if axis == 1 else 16)
    idx = [pl.ds(0, ref.shape[a]) for a in range(lead)]
    idx += [pl.ds(start, size), pl.ds(0, _BIG_SHARD[name][1])] if axis == 0 else [pl.ds(0, _BIG_SHARD[name][0]), pl.ds(start, size)]
    return ref.at[tuple(idx)]


def _half_shape(name):
    r, c = _BIG_SHARD[name]
    return (r // 2, c) if _BIG_SPLIT[name] == 0 else (r, c // 2)


def _remote(src, dst, send_sems, recv_sems, k, to):
    return pltpu.make_async_remote_copy(src_ref=src, dst_ref=dst, send_sem=send_sems.at[k], recv_sem=recv_sems.at[k],
                                        device_id=to, device_id_type=MESH)


def _mesh_place():
    x, y, c = lax.axis_index("x"), lax.axis_index("y"), lax.axis_index("c")
    return x, y, c, [(1 - x, y), (x, 1 - y), (1 - x, 1 - y)]


class _Gather:
    def __init__(self, names, small=(), middle_at=0.5):
        self.middle_at = middle_at
        self.names = tuple(names)
        self.nb = len(self.names)
        self.n = self.nb + len(small)
        self.n_sems = 8 * self.nb + 3 * len(small)
        self.n_flush = 6 * self.nb + len(small)
        self.out_shape = [jax.ShapeDtypeStruct((N_CHIP,) + _BIG_SHARD[nm], BF16) for nm in self.names]
        self.out_shape += [jax.ShapeDtypeStruct((N_CHIP,) + s.shape, s.dtype) for s in small]

    def _copies(self, ins, outs, ss, rs, k):
        x, y, c, _ = _mesh_place()
        name = self.names[k]
        me, xn, yn, dg = 2 * x + y, 2 * (1 - x) + y, 2 * x + (1 - y), 2 * (1 - x) + (1 - y)
        to_x, to_y, sibling = (1 - x, y, c), (x, 1 - y, c), (x, y, 1 - c)

        def region(slot, e, part=None):
            return _half(outs[k].at[slot], e, name, part=part)

        def copy(pair, src, dst, to):
            return _remote(src, dst, ss, rs, 8 * k + pair, to)

        mine = _half(ins[k], c, name)
        sent = [copy(0, mine, region(me, c), to_x), copy(1, mine, region(me, c), to_y),
                copy(2, region(xn, c, 0), region(xn, c, 0), to_y), copy(3, region(yn, c, 1), region(yn, c, 1), to_x),
                copy(4, region(xn, c), region(xn, c), sibling), copy(5, region(yn, c), region(yn, c), sibling),
                copy(6, region(dg, c, 0), region(dg, c, 0), sibling), copy(7, region(dg, c, 1), region(dg, c, 1), sibling)]
        landing = [region(xn, c), region(yn, c), region(dg, c, 0), region(dg, c, 1),
                   region(xn, 1 - c), region(yn, 1 - c), region(dg, 1 - c, 0), region(dg, 1 - c, 1)]
        received = [copy(pair, dst, dst, sibling) for pair, dst in enumerate(landing)]
        return sent, received

    def _small(self, ins, outs, ss, rs, k, j, peer, slot, c):
        return _remote(ins[k], outs[k].at[slot], ss, rs, 8 * self.nb + 3 * (k - self.nb) + j, (*peer, c))

    def flush(self, phase, lands, outs, fs):
        x, y, c, _ = _mesh_place()
        me, xn, yn, dg = 2 * x + y, 2 * (1 - x) + y, 2 * x + (1 - y), 2 * (1 - x) + (1 - y)

        def pieces(k):
            name = self.names[k]
            spots = [lambda r: r.at[me], lambda r: _half(r.at[xn], c, name), lambda r: _half(r.at[yn], c, name),
                     lambda r: _half(r.at[xn], 1 - c, name), lambda r: _half(r.at[yn], 1 - c, name), lambda r: r.at[dg]]
            return [pltpu.make_async_copy(spot(lands[k]), spot(outs[k]), fs.at[6 * k + t]) for t, spot in enumerate(spots)]

        ready = {"first": (0,), "middle": (1, 2), "late": (3, 4), "last": (5,)}[phase]
        for k in range(self.nb):
            cps = pieces(k)
            for t in ready:
                cps[t].start()
        if phase == "last":
            small = [pltpu.make_async_copy(lands[k], outs[k], fs.at[6 * self.nb + k - self.nb]) for k in range(self.nb, self.n)]
            for cp in small:
                cp.start()
            for k in range(self.nb):
                for cp in pieces(k):
                    cp.wait()
            for cp in small:
                cp.wait()

    def first(self, ins, outs, ss, rs):
        x, y, c, peers = _mesh_place()
        me = 2 * x + y
        for k in range(self.nb):
            sent, _ = self._copies(ins, outs, ss, rs, k)
            sent[0].start()
            sent[1].start()
        for k in range(self.nb, self.n):
            for j, peer in enumerate(peers):
                self._small(ins, outs, ss, rs, k, j, peer, me, c).start()
        for k in range(self.n):
            outs[k][me] = ins[k][...]

    def middle(self, ins, outs, ss, rs):
        for k in range(self.nb):
            sent, received = self._copies(ins, outs, ss, rs, k)
            for pair in (0, 1):
                received[pair].wait_recv()
                sent[2 + pair].start()
                sent[4 + pair].start()

    def late(self, ins, outs, ss, rs):
        for k in range(self.nb):
            _, received = self._copies(ins, outs, ss, rs, k)
            for pair in (4, 5):
                received[pair].wait_recv()

    def last(self, ins, outs, ss, rs):
        x, y, c, peers = _mesh_place()
        for k in range(self.nb):
            sent, received = self._copies(ins, outs, ss, rs, k)
            for pair in (2, 3):
                received[pair].wait_recv()
                sent[4 + pair].start()
        for k in range(self.nb):
            sent, received = self._copies(ins, outs, ss, rs, k)
            for pair in (6, 7):
                received[pair].wait_recv()
            for cp in sent:
                cp.wait_send()
        for k in range(self.nb, self.n):
            for j, (px, py) in enumerate(peers):
                self._small(ins, outs, ss, rs, k, j, (px, py), 2 * px + py, c).wait_recv()
                self._small(ins, outs, ss, rs, k, j, (px, py), 2 * x + y, c).wait_send()


def _run_alone(rider, ins, name):
    r_in, r_out_specs, r_out_shape, r_scratch = _rider_specs(rider, ins)

    def body(*refs):
        ride_in, ride_out, scratch = _split(refs, len(r_in), len(r_out_specs), len(r_scratch))
        _ride(rider, ("first", "middle", "last"), pl.program_id(0) == 0, ride_in, ride_out, scratch)

    return pl.pallas_call(
        body, grid=(1,), in_specs=r_in, out_specs=r_out_specs, out_shape=r_out_shape, scratch_shapes=r_scratch,
        compiler_params=_params(("arbitrary",)), name=name,
    )(*ins)


class _Presum:
    in_space = ANY

    def __init__(self, names, base=0):
        self.names = tuple(names)
        self.n = len(self.names)
        self.base = base
        self.n_sems = 3 * self.n
        self.out_shape = [jax.ShapeDtypeStruct((N_CHIP,) + _half_shape(nm), BF16) for nm in self.names]
        self.work_shape = self.out_shape + self.out_shape

    def _stage(self, ins, bufs, ss, k, e, which):
        n = self.n
        return pltpu.make_async_copy(_half(ins[k], e, self.names[k], lead=1), bufs[which * n + k],
                                     ss.at[self.base + which * n + k])

    def _give(self, bufs, ss, rs, k, sibling):
        return _remote(bufs[self.n + k], bufs[k], ss, rs, self.base + k, sibling)

    def first(self, ins, bufs, ss, rs):
        x, y, c, _ = _mesh_place()
        for k in range(self.n):
            self._stage(ins, bufs, ss, k, 1 - c, 1).start()
        for k in range(self.n):
            self._stage(ins, bufs, ss, k, c, 2).start()
        for k in range(self.n):
            self._stage(ins, bufs, ss, k, 1 - c, 1).wait()
            self._give(bufs, ss, rs, k, (x, y, 1 - c)).start()

    def middle(self, ins, bufs, ss, rs):
        pass

    def last(self, ins, bufs, ss, rs):
        x, y, c, _ = _mesh_place()
        for k in range(self.n):
            self._give(bufs, ss, rs, k, (x, y, 1 - c)).wait_recv()
            self._stage(ins, bufs, ss, k, c, 2).wait()

            @pl.loop(0, N_CHIP)
            def _(j):
                bufs[k][j] = (bufs[k][j].astype(F32) + bufs[2 * self.n + k][j].astype(F32)).astype(BF16)
        for k in range(self.n):
            self._give(bufs, ss, rs, k, (x, y, 1 - c)).wait_send()


class _ReduceRelay:
    middle_at = 0.75

    def __init__(self, names, base=0):
        self.names = tuple(names)
        self.n = len(self.names)
        self.base = base
        self.n_sems = 6 * self.n
        self.out_shape = [jax.ShapeDtypeStruct((N_CHIP,) + _half_shape(nm), BF16) for nm in self.names]
        quarter = [jax.ShapeDtypeStruct(self._part_shape(nm), BF16) for nm in self.names]
        self.work_shape = quarter + quarter

    @staticmethod
    def _part_shape(name):
        r, c = _half_shape(name)
        return (r // 2, c) if _BIG_SPLIT[name] == 0 else (r, c // 2)

    def _part(self, ref, name, p):
        r, c = self._part_shape(name)
        return ref.at[pl.ds(p * r, r), pl.ds(0, c)] if _BIG_SPLIT[name] == 0 else ref.at[pl.ds(0, r), pl.ds(p * c, c)]

    def _copies(self, ins, bufs, ss, rs, k):
        x, y, c, _ = _mesh_place()
        name, n = self.names[k], self.n
        me, xn, yn, dg = 2 * x + y, 2 * (1 - x) + y, 2 * x + (1 - y), 2 * (1 - x) + (1 - y)
        to_x, to_y = (1 - x, y, c), (x, 1 - y, c)
        mine = lambda slot, p: self._part(ins[k].at[slot], name, p)
        slot = lambda s, p: self._part(bufs[k].at[s], name, p)
        from_x, from_y = bufs[n + k], bufs[2 * n + k]

        def copy(pair, src, dst, to):
            return _remote(src, dst, ss, rs, self.base + 6 * k + pair, to)

        sent = [copy(0, mine(dg, 0), from_x, to_x), copy(1, mine(dg, 1), from_y, to_y),
                copy(2, mine(xn, 0), slot(me, 0), to_x), copy(3, mine(yn, 1), slot(me, 1), to_y),
                copy(4, from_y, slot(me, 1), to_x), copy(5, from_x, slot(me, 0), to_y)]
        landing = [from_x, from_y, slot(xn, 0), slot(yn, 1), slot(xn, 1), slot(yn, 0)]
        received = [copy(pair, dst, dst, to_x) for pair, dst in enumerate(landing)]
        return sent, received

    def first(self, ins, bufs, ss, rs):
        x, y, c, _ = _mesh_place()
        me, dg = 2 * x + y, 2 * (1 - x) + (1 - y)
        for k in range(self.n):
            sent, _ = self._copies(ins, bufs, ss, rs, k)
            for pair in range(4):
                sent[pair].start()
        for k in range(self.n):
            bufs[k][me] = ins[k][me]
            bufs[k][dg] = jnp.zeros(_half_shape(self.names[k]), BF16)

    def middle(self, ins, bufs, ss, rs):
        x, y, c, _ = _mesh_place()
        xn, yn = 2 * (1 - x) + y, 2 * x + (1 - y)
        for k in range(self.n):
            sent, received = self._copies(ins, bufs, ss, rs, k)
            name, n = self.names[k], self.n
            for pair, buf, own in ((0, bufs[n + k], self._part(ins[k].at[yn], name, 0)),
                                   (1, bufs[2 * n + k], self._part(ins[k].at[xn], name, 1))):
                received[pair].wait_recv()
                buf[...] = (buf[...].astype(F32) + own[...].astype(F32)).astype(BF16)
            sent[5].start()
            sent[4].start()

    def last(self, ins, bufs, ss, rs):
        for k in range(self.n):
            sent, received = self._copies(ins, bufs, ss, rs, k)
            for pair in range(2, 6):
                received[pair].wait_recv()
            for cp in sent:
                cp.wait_send()


class _PresumThenRelay:
    in_space = ANY
    middle_at = _ReduceRelay.middle_at

    def __init__(self, names):
        self.relay = _ReduceRelay(names)
        self.pre = _Presum(names, base=self.relay.n_sems)
        self.n_sems = self.relay.n_sems + self.pre.n_sems
        self.out_shape = self.relay.out_shape
        self.work_shape = list(self.relay.work_shape) + list(self.pre.out_shape) + list(self.pre.work_shape)
        self.n_relay = len(self.relay.out_shape) + len(self.relay.work_shape)

    def first(self, ins, bufs, ss, rs):
        self.pre.first(ins, bufs[self.n_relay:], ss, rs)

    def early(self, ins, bufs, ss, rs):
        self.pre.last(ins, bufs[self.n_relay:], ss, rs)
        self.relay.first(bufs[self.n_relay:], bufs[:self.n_relay], ss, rs)

    def middle(self, ins, bufs, ss, rs):
        self.relay.middle(bufs[self.n_relay:], bufs[:self.n_relay], ss, rs)

    def last(self, ins, bufs, ss, rs):
        self.relay.last(bufs[self.n_relay:], bufs[:self.n_relay], ss, rs)


class _SendPartials:
    def __init__(self, names, small_shape=None):
        self.n = len(names)
        self.small = small_shape is not None
        self.n_sems = 3 * self.n + 7
        self.out_shape = [jax.ShapeDtypeStruct((N_CHIP,) + _half_shape(nm), BF16) for nm in names]
        if self.small:
            self.out_shape.append(jax.ShapeDtypeStruct((N_DEV,) + small_shape, F32))

    def _piece(self, ins, outs, ss, rs, k, j, peer, src_slot, dst_slot, c):
        return _remote(ins[k].at[src_slot], outs[k].at[dst_slot], ss, rs, 3 * k + j, (*peer, c))

    def _small(self, ins, outs, ss, rs, r, other, slot):
        return _remote(ins[self.n], outs[self.n].at[slot], ss, rs, 3 * self.n + r, other)

    @staticmethod
    def _others(x, y, c):
        return [(x, y, 1 - c), (1 - x, y, c), (1 - x, y, 1 - c), (x, 1 - y, c), (x, 1 - y, 1 - c),
                (1 - x, 1 - y, c), (1 - x, 1 - y, 1 - c)]

    def first(self, ins, outs, ss, rs, only=None):
        x, y, c, peers = _mesh_place()
        me = 2 * x + y
        which = range(self.n) if only is None else only
        for k in which:
            for j, (px, py) in enumerate(peers):
                self._piece(ins, outs, ss, rs, k, j, (px, py), 2 * px + py, me, c).start()
        if self.small:
            for r, other in enumerate(self._others(x, y, c)):
                self._small(ins, outs, ss, rs, r, other, 4 * x + 2 * y + c).start()
            outs[self.n][4 * x + 2 * y + c] = ins[self.n][...]
        for k in which:
            outs[k][me] = ins[k][me]

    def middle(self, ins, outs, ss, rs):
        pass

    def last(self, ins, outs, ss, rs):
        x, y, c, peers = _mesh_place()
        me = 2 * x + y
        for k in range(self.n):
            for j, (px, py) in enumerate(peers):
                self._piece(ins, outs, ss, rs, k, j, (px, py), me, 2 * px + py, c).wait_recv()
                self._piece(ins, outs, ss, rs, k, j, (px, py), 2 * px + py, me, c).wait_send()
        if self.small:
            for r, (px, py, pc) in enumerate(self._others(x, y, c)):
                self._small(ins, outs, ss, rs, r, (px, py, pc), 4 * px + 2 * py + pc).wait_recv()
                self._small(ins, outs, ss, rs, r, (px, py, pc), 4 * x + 2 * y + c).wait_send()


class _PresumThenSend:
    def __init__(self, names):
        self.send = _SendPartials(names)
        self.pre = _Presum(names[-1:], base=self.send.n_sems)
        self.n = self.send.n
        self.n_sems = self.send.n_sems + self.pre.n_sems
        self.out_shape = self.send.out_shape
        self.work_shape = list(self.pre.out_shape) + list(self.pre.work_shape)
        self.in_space = [VMEM_WHOLE] * (self.n - 1) + [ANY]

    def _partials(self, ins, bufs):
        return list(ins[:self.n - 1]) + [bufs[self.n]]

    def first(self, ins, bufs, ss, rs):
        self.pre.first(ins[self.n - 1:], bufs[self.n:], ss, rs)
        self.send.first(ins, bufs[:self.n], ss, rs, only=range(self.n - 1))

    def early(self, ins, bufs, ss, rs):
        self.pre.last(ins[self.n - 1:], bufs[self.n:], ss, rs)
        self.send.first(self._partials(ins, bufs), bufs[:self.n], ss, rs, only=(self.n - 1,))

    def middle(self, ins, bufs, ss, rs):
        pass

    def last(self, ins, bufs, ss, rs):
        self.send.last(self._partials(ins, bufs), bufs[:self.n], ss, rs)


def _sum_swap(names, parts, small):
    n = len(parts)
    everyone = _SendPartials((), small.shape)

    def body(*refs):
        (p_hbm, (small_ref,), o_hbm, (osmall_ref,), p_refs, o_refs, (all_ref,),
         (send_sems, recv_sems, ss_small, rs_small, load_sems, leave_sems)) = _split(refs, n, 1, n, 1, n, n, 1, 6)
        x, y, c = lax.axis_index("x"), lax.axis_index("y"), lax.axis_index("c")
        loads = [pltpu.make_async_copy(p_hbm[k], p_refs[k], load_sems.at[k]) for k in range(n)]
        for cp in loads:
            cp.start()
        everyone.first([small_ref], [all_ref], ss_small, rs_small)

        def mine(k):
            part = _half(o_refs[k], c, names[k])
            return _remote(part, part, send_sems, recv_sems, k, (x, y, 1 - c))

        def leave(k, whose):
            e = c if whose == 0 else 1 - c
            return pltpu.make_async_copy(_half(o_refs[k], e, names[k]), _half(o_hbm[k], e, names[k]),
                                         leave_sems.at[2 * k + whose])

        for k in range(n):
            loads[k].wait()
            for e in range(2):
                @pl.when(c == e)
                def _():
                    g = p_refs[k][0].astype(F32)
                    for s in range(1, N_CHIP):
                        g = g + p_refs[k][s].astype(F32)
                    r, cols = _half_shape(names[k])
                    if _BIG_SPLIT[names[k]] == 0:
                        o_refs[k][e * r:(e + 1) * r, :] = g
                    else:
                        o_refs[k][:, e * cols:(e + 1) * cols] = g
            mine(k).start()
            leave(k, 0).start()
        for k in range(n):
            theirs = _half(o_refs[k], 1 - c, names[k])
            _remote(theirs, theirs, send_sems, recv_sems, k, (x, y, 1 - c)).wait_recv()
            leave(k, 1).start()
        everyone.last([small_ref], [all_ref], ss_small, rs_small)
        g = all_ref[0]
        for d in range(1, N_DEV):
            g = g + all_ref[d]
        osmall_ref[...] = g
        for k in range(n):
            mine(k).wait_send()
            leave(k, 0).wait()
            leave(k, 1).wait()

    shards = [jax.ShapeDtypeStruct(_BIG_SHARD[nm], F32) for nm in names]
    res = pl.pallas_call(
        body, in_specs=[ANY] * n + [VMEM_WHOLE], out_specs=[ANY] * n + [VMEM_WHOLE],
        out_shape=shards + [jax.ShapeDtypeStruct(small.shape, F32)],
        scratch_shapes=[pltpu.VMEM(q.shape, q.dtype) for q in parts] + [pltpu.VMEM(s.shape, s.dtype) for s in shards]
        + [pltpu.VMEM((N_DEV,) + small.shape, F32), pltpu.SemaphoreType.DMA((n,)), pltpu.SemaphoreType.DMA((n,)),
           pltpu.SemaphoreType.DMA((everyone.n_sems,)), pltpu.SemaphoreType.DMA((everyone.n_sems,)),
           pltpu.SemaphoreType.DMA((n,)), pltpu.SemaphoreType.DMA((2 * n,))],
        compiler_params=_params(), name="sum_swap",
    )(*parts, small)
    return res[:n], res[n]


def _tile(rows, cols, itemsize, budget):
    t = cols if rows % 16 else rows
    other = rows if rows % 16 else cols
    step = 256 if rows % 16 else 32
    while t % step == 0 and t * other * itemsize > budget:
        t //= 2
    return (rows, t) if rows % 16 else (t, cols)


def _adamw_math(w, g, m, v):
    m = ADAM_B1 * m + (1.0 - ADAM_B1) * g
    v = ADAM_B2 * v + (1.0 - ADAM_B2) * (g * g)
    m_hat = m / (1.0 - ADAM_B1 ** ADAM_STEP)
    v_hat = v / (1.0 - ADAM_B2 ** ADAM_STEP)
    delta = -ADAM_LR * (m_hat / (jnp.sqrt(v_hat) + ADAM_EPS) + ADAM_WD * w)
    return delta, m, v


def _adamw_big(g, w, m, v, name):
    r, c = w.shape
    tr, tc = _tile(r, c, 4, 2 * 1024 * 1024)

    def body(g_ref, w_ref, m_ref, v_ref, g_out_ref, d_ref, nm_ref, nv_ref):
        g = g_ref[...]
        g_out_ref[...] = g
        d_ref[...], nm_ref[...], nv_ref[...] = _adamw_math(w_ref[...], g, m_ref[...], v_ref[...])

    blk = pl.BlockSpec((tr, tc), lambda i, l: (i, l))
    return pl.pallas_call(
        body, grid=(r // tr, c // tc), in_specs=[blk, blk, blk, blk],
        out_specs=[blk] * 4, out_shape=[jax.ShapeDtypeStruct((r, c), F32)] * 4,
        compiler_params=_params(("arbitrary", "arbitrary")), name=name,
    )(g, w, m, v)


def _adamw_rows(g, w, m, v, name):
    r, k, lanes = w.shape
    tr = 296

    def body(g_ref, w_ref, m_ref, v_ref, g3_ref, d_ref, nm_ref, nv_ref):
        g = g_ref[...].reshape(tr, k, lanes)
        g3_ref[...] = g
        d_ref[...], nm_ref[...], nv_ref[...] = _adamw_math(w_ref[...], g, m_ref[...], v_ref[...])

    rows = pl.BlockSpec((tr, k, lanes), lambda i: (i, 0, 0))
    return pl.pallas_call(
        body, grid=(pl.cdiv(r, tr),), in_specs=[pl.BlockSpec((tr, k * lanes), lambda i: (i, 0)), rows, rows, rows],
        out_specs=[rows] * 4, out_shape=[jax.ShapeDtypeStruct((r, k, lanes), F32)] * 4,
        compiler_params=_params(("arbitrary",)), name=name,
    )(g, w, m, v)


def _adamw_small(ws, gs, ms, vs):
    n = len(ws)

    def body(*refs):
        w_refs, g_refs, m_refs, v_refs, d_refs, nm_refs, nv_refs = _split(refs, *([n] * 7))
        for k in range(n):
            d_refs[k][...], nm_refs[k][...], nv_refs[k][...] = _adamw_math(w_refs[k][...], g_refs[k][...], m_refs[k][...],
                                                                             v_refs[k][...])

    shapes = [jax.ShapeDtypeStruct(w.shape, F32) for w in ws]
    res = pl.pallas_call(body, out_shape=shapes * 3, name="adamw_small")(*ws, *gs, *ms, *vs)
    return res[:n], res[n:2 * n], res[2 * n:]


def _pack(arrs):
    flat = jnp.concatenate([a.reshape(-1) for a in arrs])
    rows = -(-flat.shape[0] // 1024) * 8
    return jnp.pad(flat, (0, rows * 128 - flat.shape[0])).reshape(rows, 128)


def _unpack(buf, shapes):
    flat = buf.reshape(-1)
    out, off = [], 0
    for s in shapes:
        size = 1
        for d in s:
            size *= d
        out.append(flat[off:off + size].reshape(s))
        off += size
    return out


def _block_rows(w):
    return jnp.pad(w.reshape(512, 4), ((0, 0), (0, 124)))


def _block_stored(dw):
    return jnp.transpose(dw[:, 0:4].reshape(128, 4, 4), (1, 2, 0)).reshape(16, 128)


def _cols(a4):
    return jnp.transpose(a4, (1, 0, 2)).reshape(a4.shape[1], -1)


_LATE = ("w_pa", "w_pb", "w_o", "w_up", "w_down")
_RIDE_IN_PROJ = ("w_pa", "w_pb", "w_o", "w_down_b")
_RIDE_MIXER = ("w_up", "w_down_a")


def _full_weights(gathered):
    joined = {"w_o": (D_MODEL, D_MODEL)}
    return {n: (a.reshape(joined[n]) if n in joined else a) for n, a in gathered.items()}


def _local_step(x, target, w, sp, late_shards=None):
    sp = {n: (a.reshape(1, -1) if a.ndim == 1 else a) for n, a in sp.items()}
    wau = jnp.pad(sp["w_a_up"], ((0, 112), (0, 0)))
    wif = jnp.pad(sp["w_if"], ((0, 0), (0, 120)))
    bif = jnp.pad(sp["b_if"], ((0, 0), (0, 120)))
    p = {"wau": wau, "bau": sp["b_a_up"], "ggla": sp["g_gla_norm"], "cw": sp["conv_w"], "cb": sp["conv_b"],
         "wq": _block_rows(sp["w_q_ml"]), "wk": _block_rows(sp["w_k_ml"]), "wv": _block_rows(sp["w_v_ml"]),
         "wif": wif, "bif": bif, "skip": sp["ml_skip"], "gml": sp["g_ml_norm"]}

    if late_shards is None:
        (pm, gab, h), _ = _in_proj(x, sp["g_pre_mix"], w["w_in"])
        ab, *states = _mixer_fwd(pm, p)
        (x1, mix, merged), _ = _merge_fwd(ab, gab, x, w["w_pa"], w["w_pb"], w["w_o"], sp["g_post_mix"])
    else:
        shard = dict(zip(_LATE, late_shards))
        shard["w_down_a"], shard["w_down_b"] = shard["w_down"][0:512], shard["w_down"][512:1024]
        (pm, gab, h), got = _in_proj(x, sp["g_pre_mix"], w["w_in"], _Gather(_RIDE_IN_PROJ, middle_at=0.7),
                                     [shard[n] for n in _RIDE_IN_PROJ])
        w = dict(w, **_full_weights(dict(zip(_RIDE_IN_PROJ, got))))
        ab, *rest = _mixer_fwd(pm, p, _Gather(_RIDE_MIXER, middle_at=0.62), [shard[n] for n in _RIDE_MIXER])
        states = rest[:4]
        w.update(_full_weights(dict(zip(_RIDE_MIXER, rest[4:]))))
        (x1, mix, merged), _ = _merge_fwd(ab, gab, x, w["w_pa"], w["w_pb"], w["w_o"], sp["g_post_mix"])
    dx1, u, dd, h2, dpre, dg_post_mlp, dg_pre_mlp, loss = _mlp(x1, target, sp["g_pre_mlp"], sp["g_post_mlp"],
                                                                w["w_up"], w["w_down_a"], w["w_down_b"])
    dgab, dab, dg_post_mix, dw_pa, dw_pb, dw_o = _merge_bwd(dx1, mix, ab, gab, merged, w["w_pa"], w["w_pb"], w["w_o"],
                                                            sp["g_post_mix"])
    big = {"w_pa": dw_pa, "w_pb": dw_pb, "w_o": dw_o, "w_up": _tn_matmul(h2, dpre, "dw_up", shards=N_CHIP)}
    if late_shards is None:
        big["w_down"] = _tn_matmul(u, dd, "dw_down")
        dpm, dp, _ = _mixer_bwd(pm, dab, states, p)
    else:
        pieces = lambda n: big[n].reshape((N_CHIP,) + _BIG_SHARD[n])
        big["w_down"], partial = _tn_matmul(u, dd, "dw_down", rider=_Presum(_LATE[:4]),
                                            rider_ins=[pieces(n) for n in _LATE[:4]])
        dpm, dp, parts = _mixer_bwd(pm, dab, states, p, _PresumThenSend(_LATE), list(partial) + [pieces("w_down")])
        big = dict(zip(_LATE, parts))
    big["w_in"] = _dw_in(dpm, dgab, h)
    if late_shards is None:
        (dx, dg_pre_mix), _ = _in_proj_bwd(dpm, dgab, x, dx1, sp["g_pre_mix"], w["w_in"])
    else:
        (dx, dg_pre_mix), parts = _in_proj_bwd(dpm, dgab, x, dx1, sp["g_pre_mix"], w["w_in"], _PresumThenRelay(("w_in",)),
                                               [big["w_in"]])
        big["w_in"] = parts[0]
    small = {
        "g_pre_mix": dg_pre_mix, "b_a_up": dp["bau"], "g_gla_norm": dp["ggla"], "conv_b": dp["cb"],
        "w_q_ml": _block_stored(dp["wq"]), "w_k_ml": _block_stored(dp["wk"]), "w_v_ml": _block_stored(dp["wv"]),
        "w_if": dp["wif"][:, 0:8].T,
        "b_if": dp["bif"][:, 0:8], "ml_skip": dp["skip"], "g_ml_norm": dp["gml"], "g_post_mix": dg_post_mix,
        "g_pre_mlp": dg_pre_mlp, "g_post_mlp": dg_post_mlp, "w_a_up": dp["wau"][0:16], "conv_w": dp["cw"],
        "loss": loss[:, 0:1],
    }
    return dx, big, small


_SMALL_REPL = ("g_pre_mix", "b_a_up", "g_gla_norm", "conv_b", "w_q_ml", "w_k_ml", "w_v_ml", "b_if", "ml_skip",
               "g_ml_norm", "g_post_mix", "g_pre_mlp", "g_post_mlp")
_SMALL_SHARDED = ("w_a_up", "conv_w", "w_if")
_SMALL_ORDER = _SMALL_REPL + _SMALL_SHARDED + ("loss",)
_WEIGHTS = ("g_pre_mix", "w_in", "w_a_up", "b_a_up", "g_gla_norm", "conv_w", "conv_b", "w_q_ml", "w_k_ml", "w_v_ml",
            "w_if", "b_if", "ml_skip", "g_ml_norm", "w_pa", "w_pb", "w_o", "g_post_mix", "g_pre_mlp", "w_up", "w_down",
            "g_post_mlp")


_BLOCK_WEIGHTS = ("w_q_ml", "w_k_ml", "w_v_ml")


def _stored(name, a):
    if name in _BLOCK_WEIGHTS:
        return jnp.transpose(a, (0, 2, 3, 1)).reshape(16, 128)
    if name == "w_if":
        return jnp.transpose(a, (0, 2, 1)).reshape(8, 384)
    return a


def _unstored(name, a):
    if name in _BLOCK_WEIGHTS:
        return jnp.transpose(a.reshape(1, 4, 4, 128), (0, 3, 1, 2))
    if name == "w_if":
        return jnp.transpose(a.reshape(1, 8, 384), (0, 2, 1))
    return a


def _as_shard(name, a):
    return jnp.transpose(a, (2, 0, 1)).reshape(IN_SHARD, D_MODEL // 128, 128) if name == "w_in" else a[0]


def _in_shard_bf16(w_in):
    return jnp.transpose(w_in.astype(BF16), (2, 0, 1)).reshape(IN_SHARD, D_MODEL)


def _from_shard(name, a):
    return jnp.transpose(a, (1, 2, 0)).reshape(1, D_MODEL, IN_SHARD) if name == "w_in" else a[None]


def kernel(x, g_pre_mix, w_in, w_a_up, b_a_up, g_gla_norm, conv_w, conv_b, w_q_ml, w_k_ml, w_v_ml, w_if, b_if, ml_skip, g_ml_norm, w_pa, w_pb, w_o, g_post_mix, g_pre_mlp, w_up, w_down, g_post_mlp, loss_target, m_g_pre_mix, m_w_in, m_w_a_up, m_b_a_up, m_g_gla_norm, m_conv_w, m_conv_b, m_w_q_ml, m_w_k_ml, m_w_v_ml, m_w_if, m_b_if, m_ml_skip, m_g_ml_norm, m_w_pa, m_w_pb, m_w_o, m_g_post_mix, m_g_pre_mlp, m_w_up, m_w_down, m_g_post_mlp, v_g_pre_mix, v_w_in, v_w_a_up, v_b_a_up, v_g_gla_norm, v_conv_w, v_conv_b, v_w_q_ml, v_w_k_ml, v_w_v_ml, v_w_if, v_b_if, v_ml_skip, v_g_ml_norm, v_w_pa, v_w_pb, v_w_o, v_g_post_mix, v_g_pre_mlp, v_w_up, v_w_down, v_g_post_mlp):
    args = dict(locals())
    wts = {n: _as_shard(n, args[n]) for n in _WEIGHTS}
    mom = {n: _as_shard(n, args["m_" + n]) for n in _WEIGHTS}
    var = {n: _as_shard(n, args["v_" + n]) for n in _WEIGHTS}
    chip = 2 * lax.axis_index("x") + lax.axis_index("y")

    first = ("w_in",) + _SMALL_SHARDED
    gathered = dict(zip(first, _run_alone(_Gather(("w_in",), [wts[n] for n in _SMALL_SHARDED]),
                                          [_in_shard_bf16(w_in)] + [wts[n] for n in _SMALL_SHARDED],
                                          "gather_first")))
    sp = {n: wts[n] for n in _SMALL_REPL}
    sp["w_a_up"] = _cols(gathered["w_a_up"])
    sp["conv_w"] = _cols(gathered["conv_w"])
    sp["w_if"] = gathered["w_if"].reshape(1536, 8)

    dx, big, small = _local_step(x[0], loss_target[0], _full_weights({"w_in": gathered["w_in"]}), sp,
                                 late_shards=[wts[n].astype(BF16) for n in _LATE])

    small_shapes = [small[n].shape for n in _SMALL_ORDER]
    packed = _pack([small[n] for n in _SMALL_ORDER])
    sums, small_sum = _sum_swap(_BIG, [big[n] for n in _BIG], packed)

    grads, delta, new_m, new_v = {}, {}, {}, {}
    for n, g in zip(_BIG, sums):
        adamw = _adamw_rows if n == "w_in" else _adamw_big
        g, d, nm, nv = adamw(g, wts[n], mom[n], var[n], "adamw_" + n)
        grads[n], delta[n], new_m[n], new_v[n] = (_from_shard(n, a) for a in (g, d, nm, nv))
    summed = dict(zip(_SMALL_ORDER, _unpack(small_sum, small_shapes)))
    loss = summed["loss"].reshape(())
    summed["w_a_up"] = lax.dynamic_slice_in_dim(summed["w_a_up"], chip * 64, 64, axis=1)
    summed["conv_w"] = lax.dynamic_slice_in_dim(summed["conv_w"], chip * 128, 128, axis=1)
    summed["w_if"] = lax.dynamic_slice_in_dim(summed["w_if"], chip * 384, 384, axis=1)
    small_names = _SMALL_REPL + _SMALL_SHARDED
    came_stored = _BLOCK_WEIGHTS + ("w_if",)
    g_stored = [summed[n] if n in came_stored else _stored(n, summed[n].reshape(args[n].shape)) for n in small_names]
    upd = _adamw_small([_stored(n, args[n]) for n in small_names], g_stored,
                       [_stored(n, args["m_" + n]) for n in small_names], [_stored(n, args["v_" + n]) for n in small_names])
    for dst, arrs in zip((grads, delta, new_m, new_v), (g_stored,) + tuple(upd)):
        dst.update({n: _unstored(n, a) for n, a in zip(small_names, arrs)})

    outs = [loss, dx[None]]
    for group in (grads, delta, new_m, new_v):
        outs += [group[n] for n in _WEIGHTS]
    return tuple(outs)
```

```python
import functools

import jax
import jax.numpy as jnp
from jax import lax
from jax.experimental import pallas as pl
from jax.experimental.pallas import tpu as pltpu

F32 = jnp.float32
BF16 = jnp.bfloat16

SEQ = 2048
D_MODEL = 1024
CHUNK = 64
N_CHUNK = SEQ // CHUNK
HEADS = 4
GLA_DK = 64
GLA_DV = 128
ML_DH = 128
D_FF = 4096
EPS = 1e-6
N_CHIP = 4
N_DEV = 8
TOK_TILE = 256
N_TOK_TILE = SEQ // TOK_TILE
SWEEP = 2
assert CHUNK == 64
N_SWEEP = N_CHUNK // SWEEP

PM_W = 2688
PM_XM = 1536
PM_OP = 2048
PM_AL = 2560
GAB_W = 2048
D_IN = 4624
IN_SHARD = D_IN // N_CHIP
IN_ALOW = 1536
IN_XM = 1552
IN_GATES = 2576

ADAM_LR = 0.001
ADAM_B1 = 0.9
ADAM_B2 = 0.999
ADAM_EPS = 1e-08
ADAM_WD = 0.01
ADAM_STEP = 10

VMEM_LIMIT = 56 * 1024 * 1024


def _params(sem=None):
    return pltpu.CompilerParams(dimension_semantics=sem, vmem_limit_bytes=VMEM_LIMIT)


def _dot(a, b, ca, cb):
    return lax.dot_general(a.astype(BF16), b.astype(BF16), (((ca,), (cb,)), ((), ())), preferred_element_type=F32)


def _pmm_nn(a, b):
    return _dot(a, b, 1, 0)


def _pmm_nt(a, b):
    return _dot(a, b, 1, 1)


def _pmm_tn(a, b):
    return _dot(a, b, 0, 0)


def _pcmm(c, x):
    return lax.dot_general(c, x, (((1,), (0,)), ((), ())), precision=lax.Precision.HIGHEST, preferred_element_type=F32)


@jax.custom_vjp
def _mm_nn(a, b):
    return _dot(a, b, 1, 0)


@jax.custom_vjp
def _mm_nt(a, b):
    return _dot(a, b, 1, 1)


@jax.custom_vjp
def _mm_tn(a, b):
    return _dot(a, b, 0, 0)


_mm_nn.defvjp(lambda a, b: (_dot(a, b, 1, 0), (a, b)), lambda r, g: (_mm_nt(g, r[1]), _mm_tn(r[0], g)))
_mm_nt.defvjp(lambda a, b: (_dot(a, b, 1, 1), (a, b)), lambda r, g: (_mm_nn(g, r[1]), _mm_tn(g, r[0])))
_mm_tn.defvjp(lambda a, b: (_dot(a, b, 0, 0), (a, b)), lambda r, g: (_mm_nt(r[1], g), _mm_nn(r[0], g)))


@jax.custom_vjp
def _cmm(c, x):
    return _pcmm(c, x)


_cmm.defvjp(
    lambda c, x: (_pcmm(c, x), c),
    lambda c, g: (jnp.zeros_like(c), lax.dot_general(c, g, (((0,), (0,)), ((), ())), precision=lax.Precision.HIGHEST,
                                                      preferred_element_type=F32)),
)

_PLAIN_OPS = (_pmm_nn, _pmm_nt, _pmm_tn, _pcmm)
_VJP_OPS = (_mm_nn, _mm_nt, _mm_tn, _cmm)


def _sigmoid(x):
    return 0.5 * (jnp.tanh(0.5 * x) + 1.0)


def _log_sigmoid(x):
    return jnp.minimum(x, 0.0) - jnp.log(1.0 + jnp.exp(-jnp.abs(x)))


def _mean(x):
    return jnp.mean(x, axis=-1, keepdims=True)


def _nt(a, b):
    return lax.dot_general(a, b, (((1,), (1,)), ((), ())), preferred_element_type=F32)


def _tn(a, b):
    return lax.dot_general(a, b, (((0,), (0,)), ((), ())), preferred_element_type=F32)


def _mixer_chunk(ops, p, st, pm, xprev8):
    mm_nn, mm_nt, mm_tn, cmm = ops
    n_rows = pm.shape[0]
    n_ch = n_rows // CHUNK
    row = lax.broadcasted_iota(jnp.int32, (n_rows, n_rows), 0)
    col = lax.broadcasted_iota(jnp.int32, (n_rows, n_rows), 1)
    tri = jnp.logical_and((row >> 6) == (col >> 6), row >= col).astype(F32)
    causal = tri[0:CHUNK, 0:CHUNK] > 0.0
    q = pm[:, 0:256]
    k = pm[:, 256:512]
    v = pm[:, 512:1024]
    g = pm[:, 1024:1536]
    xm = pm[:, PM_XM:PM_XM + 512]
    opre = pm[:, PM_OP:PM_OP + 512]
    alow = pm[:, PM_AL:PM_AL + 128]
    hs = range(HEADS)
    cs = range(n_ch)
    pairs = [(i, h) for i in cs for h in hs]
    rs = [slice(i * CHUNK, (i + 1) * CHUNK) for i in cs]
    last = [slice((i + 1) * CHUNK - 1, (i + 1) * CHUNK) for i in cs]
    s6 = [slice(h * GLA_DK, (h + 1) * GLA_DK) for h in hs]
    s12 = [slice(h * 128, (h + 1) * 128) for h in hs]

    xx = jnp.concatenate([xprev8, xm], axis=0)
    pre = p["cb"]
    for j in range(4):
        pre = pre + p["cw"][j:j + 1, :] * xx[5 + j:5 + j + n_rows, :]
    xc = pre * _sigmoid(pre)
    qm = [mm_nn(xc[:, s12[h]], p["wq"][h]) for h in hs]
    km = [mm_nn(xc[:, s12[h]], p["wk"][h]) for h in hs]
    vm = [mm_nn(xm[:, s12[h]], p["wv"][h]) for h in hs]
    qcat = jnp.concatenate(qm, axis=1)
    kcat = jnp.concatenate(km, axis=1)
    vcat = jnp.concatenate(vm, axis=1)
    gates = (mm_nn(qcat, p["wif"][0:512]) + mm_nn(kcat, p["wif"][512:1024]) + mm_nn(vcat, p["wif"][1024:1536])
             + p["bif"])
    lf = _log_sigmoid(gates)
    fc = cmm(tri, lf)
    gates_t = gates.T
    fc_t = fc.T

    la = _log_sigmoid(mm_nn(alow, p["wau"]) + p["bau"]) * (1.0 / 16.0)
    cum = cmm(tri, la)
    cum_last = [cum[last[i], :] for i in cs]
    to_end = jnp.concatenate([cum_last[i] - cum[rs[i], :] for i in cs], axis=0)
    e_pos = jnp.exp(cum)
    e_neg = jnp.exp(-cum)
    qs = q * (GLA_DK ** -0.5)
    qp = qs * e_pos
    qn = qs * e_neg
    kp = k * e_pos
    kn = k * e_neg
    kl = k * jnp.exp(to_end)
    dec = [jnp.exp(cum_last[i]) for i in cs]
    ks = [km[h] * (ML_DH ** -0.5) for h in hs]
    li_c = {(i, h): gates[rs[i], h:h + 1] for i, h in pairs}
    fc_c = {(i, h): fc[rs[i], 4 + h:5 + h] for i, h in pairs}
    f_last = {(i, h): fc[last[i], 4 + h:5 + h] for i, h in pairs}

    a_fwd = {(i, h): mm_nt(qp[rs[i], s6[h]], kn[rs[i], s6[h]]) for i, h in pairs}
    a_bwd = {(i, h): mm_nt(qn[rs[i], s6[h]], kp[rs[i], s6[h]]) for i, h in pairs}
    s_chunk = {(i, h): mm_tn(v[rs[i], s12[h]], kl[rs[i], s6[h]]) for i, h in pairs}
    qk = {(i, h): mm_nt(qm[h][rs[i]], ks[h][rs[i]]) for i, h in pairs}
    a = {ih: f_last[ih] - fc_c[ih] + li_c[ih] for ih in pairs}
    m_loc = {ih: jnp.max(a[ih], axis=0, keepdims=True) for ih in pairs}
    kw = {(i, h): ks[h][rs[i]] * jnp.exp(a[(i, h)] - m_loc[(i, h)]) for i, h in pairs}
    c_chunk = {(i, h): mm_tn(kw[(i, h)], vm[h][rs[i]]) for i, h in pairs}
    mem = {(0, h): st["S"][h] for h in hs}
    c_in = {(0, h): st["C"][h] for h in hs}
    n_in = {(0, h): st["n"][h] for h in hs}
    m_in = {(0, h): st["m"][h][:, 0:1] for h in hs}
    for i, h in pairs:
        mem[(i + 1, h)] = mem[(i, h)] * dec[i][:, s6[h]] + s_chunk[(i, h)]
        m_nx = jnp.maximum(f_last[(i, h)] + m_in[(i, h)], m_loc[(i, h)])
        sp = jnp.exp(f_last[(i, h)] + m_in[(i, h)] - m_nx)
        sl = jnp.exp(m_loc[(i, h)] - m_nx)
        c_in[(i + 1, h)] = sp * c_in[(i, h)] + sl * c_chunk[(i, h)]
        n_in[(i + 1, h)] = sp * n_in[(i, h)] + sl * jnp.sum(kw[(i, h)], axis=0, keepdims=True)
        m_in[(i + 1, h)] = m_nx
    s_new = [mem[(n_ch, h)] for h in hs]
    o_inter = {(i, h): mm_nt(qp[rs[i], s6[h]], mem[(i, h)]) for i, h in pairs}
    q_c = {(i, h): mm_nn(qm[h][rs[i]], c_in[(i, h)]) for i, h in pairs}
    scores = {ih: jnp.where(causal, a_fwd[ih], a_bwd[ih]) for ih in pairs}
    log_d = {(i, h): gates_t[h:h + 1, rs[i]] - jnp.abs(fc_c[(i, h)] - fc_t[4 + h:5 + h, rs[i]]) for i, h in pairs}
    g_int = {ih: fc_c[ih] + m_in[ih] for ih in pairs}
    m_t = {ih: jnp.maximum(g_int[ih], jnp.max(log_d[ih], axis=1, keepdims=True)) for ih in pairs}
    s = {ih: qk[ih] * jnp.exp(log_d[ih] - m_t[ih]) for ih in pairs}
    scl = {ih: jnp.exp(g_int[ih] - m_t[ih]) for ih in pairs}
    o = {(i, h): mm_nn(scores[(i, h)], v[rs[i], s12[h]]) + o_inter[(i, h)] for i, h in pairs}
    num = {(i, h): mm_nn(s[(i, h)], vm[h][rs[i]]) + scl[(i, h)] * q_c[(i, h)] for i, h in pairs}
    o = {ih: o[ih] * lax.rsqrt(_mean(o[ih] * o[ih]) + EPS) * p["ggla"] for ih in pairs}
    gate = g * _sigmoid(g)
    out_a = {(i, h): o[(i, h)] * gate[rs[i], s12[h]] for i, h in pairs}
    den = {(i, h): jnp.sum(s[(i, h)], axis=1, keepdims=True)
           + scl[(i, h)] * jnp.sum(qm[h][rs[i]] * n_in[(i, h)], axis=1, keepdims=True) for i, h in pairs}
    den = {ih: jnp.maximum(jnp.abs(den[ih]), jnp.exp(-m_t[ih])) for ih in pairs}
    open_gate = _sigmoid(opre)
    hc = {(i, h): num[(i, h)] / den[(i, h)] * open_gate[rs[i], s12[h]] for i, h in pairs}
    d0 = {ih: hc[ih] - _mean(hc[ih]) for ih in pairs}
    y = {ih: d0[ih] * lax.rsqrt(_mean(d0[ih] * d0[ih]) + EPS) for ih in pairs}
    skipped = p["skip"] * xc
    out_b = {(i, h): y[(i, h)] * p["gml"][:, s12[h]] + skipped[rs[i], s12[h]] for i, h in pairs}
    ab = jnp.concatenate([jnp.concatenate([out_a[(i, h)] for h in hs] + [out_b[(i, h)] for h in hs], axis=1) for i in cs],
                         axis=0)
    new = {"S": s_new, "C": [c_in[(n_ch, h)] for h in hs], "n": [n_in[(n_ch, h)] for h in hs],
           "m": [jnp.broadcast_to(m_in[(n_ch, h)], (1, ML_DH)) for h in hs]}
    return ab, new


_P_NAMES = ("wau", "bau", "ggla", "cw", "cb", "wq", "wk", "wv", "wif", "bif", "skip", "gml")
_P_SHAPES = {
    "wau": (128, 256), "bau": (1, 256), "ggla": (1, 128), "cw": (4, 512), "cb": (1, 512),
    "wq": (512, 128), "wk": (512, 128), "wv": (512, 128),
    "wif": (1536, 128), "bif": (1, 128), "skip": (1, 512), "gml": (1, 512),
}
_P_BLOCKDIAG = ("wq", "wk", "wv")
_S_NAMES = ("S", "C", "n", "m")
_S_SHAPES = {"S": (HEADS, GLA_DV, GLA_DK), "C": (HEADS, ML_DH, ML_DH), "n": (HEADS, 1, ML_DH), "m": (HEADS, 1, ML_DH)}


def _per_head(ref):
    return [ref[h] for h in range(HEADS)]


def _block_mask():
    r = lax.broadcasted_iota(jnp.int32, (128, 128), 0)
    c = lax.broadcasted_iota(jnp.int32, (128, 128), 1)
    same_block = (r >> 2) == (c >> 2)
    spread = jnp.logical_and(r < 4, (c & 3) == r)
    return same_block.astype(F32), spread.astype(F32)


def _expand_blockdiag(w_ref, dense_ref):
    same_block, spread = _block_mask()
    for h in range(HEADS):
        tiled = _pmm_nn(w_ref[h * 128:(h + 1) * 128, :], spread)
        dense_ref[h] = tiled * same_block


def _collect_blockdiag(ddense_ref, dw_ref):
    same_block, spread = _block_mask()
    for h in range(HEADS):
        dw_ref[h * 128:(h + 1) * 128, :] = lax.dot_general(
            ddense_ref[h] * same_block, spread, (((1,), (1,)), ((), ())), precision=lax.Precision.HIGHEST,
            preferred_element_type=F32)


def _const_spec(shape):
    zeros = (0,) * len(shape)
    return pl.BlockSpec(shape, lambda i: zeros)


def _split(refs, *counts):
    out, at = [], 0
    for c in counts:
        out.append(refs[at:at + c])
        at += c
    assert at == len(refs)
    return out


def _ride(rider, phases, cond, ins, outs, sems):
    if rider is None or not any(hasattr(rider, phase) for phase in phases):
        return
    lands, (send_sems, recv_sems, flush_sems) = sems[:-3], sems[-3:]

    @pl.when(cond)
    def _():
        for phase in phases:
            if phase == "last" and hasattr(rider, "late"):
                rider.late(ins, lands, send_sems, recv_sems)
                rider.flush("late", lands, outs, flush_sems)
            getattr(rider, phase)(ins, lands, send_sems, recv_sems)
            if hasattr(rider, "flush"):
                rider.flush(phase, lands, outs, flush_sems)
        if "last" in phases and not hasattr(rider, "flush"):
            flush = [pltpu.make_async_copy(lands[k], outs[k], flush_sems.at[k]) for k in range(len(outs))]
            for cp in flush:
                cp.start()
            for cp in flush:
                cp.wait()


def _middle_step(rider, n_steps):
    return min(n_steps - 2, int(getattr(rider, "middle_at", 1.0) * n_steps))


def _rider_specs(rider, rider_ins):
    if rider is None:
        return [], [], [], []
    scratch = [pltpu.VMEM(s.shape, s.dtype) for s in list(rider.out_shape) + list(getattr(rider, "work_shape", ()))]
    scratch += [pltpu.SemaphoreType.DMA((rider.n_sems,)), pltpu.SemaphoreType.DMA((rider.n_sems,)),
                pltpu.SemaphoreType.DMA((getattr(rider, "n_flush", len(rider.out_shape)),))]
    in_space = getattr(rider, "in_space", VMEM_WHOLE)
    in_specs = list(in_space) if isinstance(in_space, (list, tuple)) else [in_space] * len(rider_ins)
    return in_specs, [ANY] * len(rider.out_shape), list(rider.out_shape), scratch


def _mixer_fwd(pm, p, rider=None, rider_ins=()):
    n_p = len(_P_NAMES)
    r_in, r_out_specs, r_out_shape, r_sems = _rider_specs(rider, rider_ins)

    def body(*refs):
        (pm_ref, xprev_ref), p_list, ride_in, (ab_ref,), so_refs, ride_out, sc_refs, dense_list, sems = _split(
            refs, 2, n_p, len(r_in), 1, 4, len(r_out_specs), 4, 3, len(r_sems))
        p_refs = dict(zip(_P_NAMES, p_list))
        dense = dict(zip(_P_BLOCKDIAG, dense_list))
        n = pl.program_id(0)
        _ride(rider, ("first",), n == 0, ride_in, ride_out, sems)

        @pl.when(n == 0)
        def _():
            for r in sc_refs:
                r[...] = jnp.zeros_like(r)
            for nm in _P_BLOCKDIAG:
                _expand_blockdiag(p_refs[nm], dense[nm])

        st = {name: _per_head(r) for name, r in zip(_S_NAMES, sc_refs)}
        pv = {nm: (_per_head(dense[nm]) if nm in _P_BLOCKDIAG else p_refs[nm][...]) for nm in _P_NAMES}
        for name, r in zip(_S_NAMES, so_refs):
            for h in range(HEADS):
                r[0, h] = st[name][h]
        xprev8 = jnp.where(n > 0, xprev_ref[CHUNK - 8:CHUNK, :], 0.0)
        ab, st = _mixer_chunk(_PLAIN_OPS, pv, st, pm_ref[...], xprev8)
        ab_ref[...] = ab.astype(BF16)
        for name, r in zip(_S_NAMES, sc_refs):
            for h in range(HEADS):
                r[h] = st[name][h]
        _ride(rider, ("middle",), n == _middle_step(rider, N_SWEEP), ride_in, ride_out, sems)
        _ride(rider, ("last",), n == N_SWEEP - 1, ride_in, ride_out, sems)

    in_specs = [pl.BlockSpec((SWEEP * CHUNK, PM_W), lambda i: (i, 0)),
                pl.BlockSpec((CHUNK, 512), lambda i: (jnp.maximum(SWEEP * i - 1, 0), PM_XM // 512))]
    in_specs += [_const_spec(_P_SHAPES[nm]) for nm in _P_NAMES] + r_in
    out_specs = [pl.BlockSpec((SWEEP * CHUNK, 1024), lambda i: (i, 0))]
    out_shape = [jax.ShapeDtypeStruct((SEQ, 1024), BF16)]
    for nm in _S_NAMES:
        shp = _S_SHAPES[nm]
        out_specs.append(pl.BlockSpec((1,) + shp, lambda i: (i, 0, 0, 0)))
        out_shape.append(jax.ShapeDtypeStruct((N_SWEEP,) + shp, F32))
    return pl.pallas_call(
        body, grid=(N_SWEEP,), in_specs=in_specs, out_specs=out_specs + r_out_specs, out_shape=out_shape + r_out_shape,
        scratch_shapes=[pltpu.VMEM(_S_SHAPES[nm], F32) for nm in _S_NAMES]
        + [pltpu.VMEM((HEADS, 128, 128), F32) for _ in _P_BLOCKDIAG] + r_sems,
        compiler_params=_params(("arbitrary",)), name="mixer_fwd",
    )(pm, pm, *[p[nm] for nm in _P_NAMES], *rider_ins)


def _mixer_bwd(pm, dab, states, p, rider=None, rider_ins=()):
    n_p = len(_P_NAMES)
    r_in, r_out_specs, r_out_shape, r_sems = _rider_specs(rider, rider_ins)

    def body(*refs):
        ((pm_ref, xprev_ref, dab_ref), si_refs, p_list, ride_in, (dpm_ref,), dp_list, ride_out, ds_refs, (carry_ref,),
         dense_list, ddense_list, sems) = _split(refs, 3, 4, n_p, len(r_in), 1, n_p, len(r_out_specs), 4, 1, 3, 3, len(r_sems))
        p_refs = dict(zip(_P_NAMES, p_list))
        dp_refs = dict(zip(_P_NAMES, dp_list))
        dense = dict(zip(_P_BLOCKDIAG, dense_list))
        ddense = dict(zip(_P_BLOCKDIAG, ddense_list))
        i = pl.program_id(0)
        blk = N_SWEEP - 1 - i
        _ride(rider, ("first",), i == 0, ride_in, ride_out, sems)

        @pl.when(i == 0)
        def _():
            for r in ds_refs:
                r[...] = jnp.zeros_like(r)
            for nm in _P_NAMES:
                if nm in _P_BLOCKDIAG:
                    ddense[nm][...] = jnp.zeros_like(ddense[nm])
                    _expand_blockdiag(p_refs[nm], dense[nm])
                else:
                    dp_refs[nm][...] = jnp.zeros_like(dp_refs[nm])
            carry_ref[...] = jnp.zeros_like(carry_ref)

        pv = {nm: (_per_head(dense[nm]) if nm in _P_BLOCKDIAG else p_refs[nm][...]) for nm in _P_NAMES}
        dst = {name: _per_head(r) for name, r in zip(_S_NAMES, ds_refs)}
        st = {name: [r[0, h] for h in range(HEADS)] for name, r in zip(_S_NAMES, si_refs)}
        xprev8 = jnp.where(blk > 0, xprev_ref[CHUNK - 8:CHUNK, :], 0.0)
        _, vjp = jax.vjp(functools.partial(_mixer_chunk, _VJP_OPS), pv, st, pm_ref[...], xprev8)
        dp_sum, dst, dpm, dxprev8 = vjp((dab_ref[...], dst))
        reach = jnp.concatenate([jnp.zeros((SWEEP * CHUNK - 8, 512), F32), carry_ref[...]], axis=0)
        dpm_ref[:, 0:PM_XM] = dpm[:, 0:PM_XM].astype(BF16)
        dpm_ref[:, PM_XM:PM_XM + 512] = (dpm[:, PM_XM:PM_XM + 512] + reach).astype(BF16)
        dpm_ref[:, PM_XM + 512:PM_W] = dpm[:, PM_XM + 512:PM_W].astype(BF16)
        carry_ref[...] = dxprev8
        for name, r in zip(_S_NAMES, ds_refs):
            for h in range(HEADS):
                r[h] = dst[name][h]
        for nm in _P_NAMES:
            if nm in _P_BLOCKDIAG:
                for h in range(HEADS):
                    ddense[nm][h] += dp_sum[nm][h]
            else:
                dp_refs[nm][...] += dp_sum[nm]

        @pl.when(i == N_SWEEP - 1)
        def _():
            for nm in _P_BLOCKDIAG:
                _collect_blockdiag(ddense[nm], dp_refs[nm])

        _ride(rider, ("early",), i == 1, ride_in, ride_out, sems)
        _ride(rider, ("middle",), i == _middle_step(rider, N_SWEEP), ride_in, ride_out, sems)
        _ride(rider, ("last",), i == N_SWEEP - 1, ride_in, ride_out, sems)

    rev = lambda i: (N_SWEEP - 1 - i, 0)
    in_specs = [pl.BlockSpec((SWEEP * CHUNK, PM_W), rev),
                pl.BlockSpec((CHUNK, 512), lambda i: (jnp.maximum(SWEEP * (N_SWEEP - 1 - i) - 1, 0), PM_XM // 512)),
                pl.BlockSpec((SWEEP * CHUNK, 1024), rev)]
    for nm in _S_NAMES:
        in_specs.append(pl.BlockSpec((1,) + _S_SHAPES[nm], lambda i: (N_SWEEP - 1 - i, 0, 0, 0)))
    in_specs += [_const_spec(_P_SHAPES[nm]) for nm in _P_NAMES] + r_in
    out_specs = [pl.BlockSpec((SWEEP * CHUNK, PM_W), rev)] + [_const_spec(_P_SHAPES[nm]) for nm in _P_NAMES]
    out_shape = [jax.ShapeDtypeStruct((SEQ, PM_W), BF16)] + [jax.ShapeDtypeStruct(_P_SHAPES[nm], F32) for nm in _P_NAMES]
    res = pl.pallas_call(
        body, grid=(N_SWEEP,), in_specs=in_specs, out_specs=out_specs + r_out_specs, out_shape=out_shape + r_out_shape,
        scratch_shapes=[pltpu.VMEM(_S_SHAPES[nm], F32) for nm in _S_NAMES] + [pltpu.VMEM((8, 512), F32)]
        + [pltpu.VMEM((HEADS, 128, 128), F32) for _ in range(2 * len(_P_BLOCKDIAG))] + r_sems,
        compiler_params=_params(("arbitrary",)), name="mixer_bwd",
    )(pm, pm, dab, *states, *[p[nm] for nm in _P_NAMES], *rider_ins)
    return res[0], dict(zip(_P_NAMES, res[1:1 + n_p])), res[1 + n_p:]


def _tok(width):
    return pl.BlockSpec((TOK_TILE, width), lambda i: (i, 0))


def _once(shape):
    zeros = (0,) * len(shape)
    return pl.BlockSpec(shape, lambda i: zeros, pipeline_mode=pl.Buffered(1))


def _rms_fwd(x):
    r = lax.rsqrt(_mean(x * x) + EPS)
    return x * r, r


def _rms_bwd(dy, xn, r, g):
    gd = dy * g
    return r * (gd - xn * _mean(xn * gd))


def _tiled_call(body, in_specs, out_specs, out_shape, args, name, rider=None, rider_ins=(), scratch=()):
    r_in, r_out_specs, r_out_shape, r_scratch = _rider_specs(rider, rider_ins)
    n_in, n_out = len(in_specs), len(out_specs)

    def hosted(*refs):
        ins, ride_in, outs, ride_out, own, r_scr = _split(refs, n_in, len(r_in), n_out, len(r_out_specs), len(scratch),
                                                          len(r_scratch))
        i = pl.program_id(0)
        _ride(rider, ("first",), i == 0, ride_in, ride_out, r_scr)
        body(*ins, *outs, *own)
        _ride(rider, ("early",), i == 1, ride_in, ride_out, r_scr)
        _ride(rider, ("middle",), i == _middle_step(rider, N_TOK_TILE), ride_in, ride_out, r_scr)
        _ride(rider, ("last",), i == N_TOK_TILE - 1, ride_in, ride_out, r_scr)

    res = pl.pallas_call(
        hosted, grid=(N_TOK_TILE,), in_specs=list(in_specs) + r_in, out_specs=list(out_specs) + r_out_specs,
        out_shape=list(out_shape) + r_out_shape, scratch_shapes=list(scratch) + r_scratch,
        compiler_params=_params(("arbitrary",)), name=name,
    )(*args, *rider_ins)
    return res[:n_out], res[n_out:]


def _join_rows(w4_ref, wt_ref):
    @pl.when(pl.program_id(0) == 0)
    def _():
        for j in range(N_CHIP):
            wt_ref[j * IN_SHARD:(j + 1) * IN_SHARD, :] = w4_ref[j]


def _joined_scratch():
    return [pltpu.VMEM((D_IN, D_MODEL), BF16)]


def _in_proj(x, g_pre, w4_in, rider=None, rider_ins=()):
    def body(x_ref, g_ref, w4_ref, pm_ref, gab_ref, h_ref, wt_ref):
        _join_rows(w4_ref, wt_ref)
        xn, _ = _rms_fwd(x_ref[...])
        h = (xn * g_ref[...]).astype(BF16)
        h_ref[...] = h
        pm_ref[:, 0:PM_XM] = _nt(h, wt_ref[0:IN_ALOW, :])
        pm_ref[:, PM_XM:PM_AL] = _nt(h, wt_ref[IN_XM:IN_GATES, :])
        pm_ref[:, PM_AL:PM_W] = _nt(h, wt_ref[IN_ALOW:IN_ALOW + 128, :])
        gab_ref[...] = _nt(h, wt_ref[IN_GATES:D_IN, :])

    return _tiled_call(
        body, [_tok(D_MODEL), _once((1, D_MODEL)), _once((N_CHIP, IN_SHARD, D_MODEL))],
        [_tok(PM_W), _tok(GAB_W), _tok(D_MODEL)],
        [jax.ShapeDtypeStruct((SEQ, PM_W), F32), jax.ShapeDtypeStruct((SEQ, GAB_W), F32),
         jax.ShapeDtypeStruct((SEQ, D_MODEL), BF16)], (x, g_pre, w4_in), "in_proj", rider, rider_ins, _joined_scratch())


def _merge_fwd(ab, gab, x, w_pa4, w_pb4, w_o, g_post, rider=None, rider_ins=()):
    def body(ab_ref, gab_ref, x_ref, wpa_ref, wpb_ref, wo_ref, g_ref, x1_ref, mix_ref, mg_ref):
        a = ab_ref[:, 0:512]
        b = ab_ref[:, 512:1024]
        for j in range(N_CHIP):
            blk = slice(j * 256, (j + 1) * 256)
            ya = jnp.dot(a, wpa_ref[j], preferred_element_type=F32)
            yb = jnp.dot(b, wpb_ref[j], preferred_element_type=F32)
            sa = _sigmoid(gab_ref[:, j * 256:(j + 1) * 256])
            sb = _sigmoid(gab_ref[:, 1024 + j * 256:1024 + (j + 1) * 256])
            mg_ref[:, blk] = (sa * ya + sb * yb).astype(BF16)
        mix = jnp.dot(mg_ref[...], wo_ref[...], preferred_element_type=F32)
        mix_ref[...] = mix
        mn, _ = _rms_fwd(mix)
        x1_ref[...] = x_ref[...] + mn * g_ref[...]

    return _tiled_call(
        body, [_tok(1024), _tok(GAB_W), _tok(D_MODEL), _once((N_CHIP, 512, 256)), _once((N_CHIP, 512, 256)),
               _once((D_MODEL, D_MODEL)), _once((1, D_MODEL))], [_tok(D_MODEL), _tok(D_MODEL), _tok(D_MODEL)],
        [jax.ShapeDtypeStruct((SEQ, D_MODEL), F32), jax.ShapeDtypeStruct((SEQ, D_MODEL), F32),
         jax.ShapeDtypeStruct((SEQ, D_MODEL), BF16)], (ab, gab, x, w_pa4, w_pb4, w_o, g_post), "merge_fwd", rider, rider_ins)


def _mlp(x1, target, g_pre, g_post, w_up4, w_down_a4, w_down_b4):
    def body(x1_ref, t_ref, gpre_ref, gpost_ref, wup_ref, wda_ref, wdb_ref,
             dx1_ref, u_ref, dd_ref, h2_ref, dpre_ref, dgpost_ref, dgpre_ref, loss_ref):
        @pl.when(pl.program_id(0) == 0)
        def _():
            dgpost_ref[...] = jnp.zeros_like(dgpost_ref)
            dgpre_ref[...] = jnp.zeros_like(dgpre_ref)
            loss_ref[...] = jnp.zeros_like(loss_ref)

        x1 = x1_ref[...]
        gpre = gpre_ref[...]
        gpost = gpost_ref[...]
        xn2, r2 = _rms_fwd(x1)
        h2 = (xn2 * gpre).astype(BF16)
        h2_ref[...] = h2
        rl = []
        d = jnp.zeros((TOK_TILE, D_MODEL), F32)
        for j in range(N_CHIP):
            blk = slice(j * 1024, (j + 1) * 1024)
            r = jnp.maximum(jnp.dot(h2, wup_ref[j], preferred_element_type=F32), 0.0)
            rl.append(r)
            u = (r * r).astype(BF16)
            u_ref[:, blk] = u
            d = d + jnp.dot(u[:, 0:512], wda_ref[j], preferred_element_type=F32)
            d = d + jnp.dot(u[:, 512:1024], wdb_ref[j], preferred_element_type=F32)
        dn, r3 = _rms_fwd(d)
        diff = x1 + dn * gpost - t_ref[...]
        loss_ref[...] += jnp.sum(diff * diff, keepdims=True) * (0.5 / D_MODEL)
        dy = diff * (1.0 / D_MODEL)
        dgpost_ref[...] += jnp.sum(dy * dn, axis=0, keepdims=True)
        dd = _rms_bwd(dy, dn, r3, gpost).astype(BF16)
        dd_ref[...] = dd
        dh2 = jnp.zeros((TOK_TILE, D_MODEL), F32)
        for j in range(N_CHIP):
            blk = slice(j * 1024, (j + 1) * 1024)
            du = jnp.concatenate([_nt(dd, wda_ref[j]), _nt(dd, wdb_ref[j])], axis=1)
            dpre = (du * (2.0 * rl[j])).astype(BF16)
            dpre_ref[:, blk] = dpre
            dh2 = dh2 + _nt(dpre, wup_ref[j])
        dgpre_ref[...] += jnp.sum(dh2 * xn2, axis=0, keepdims=True)
        dx1_ref[...] = dy + _rms_bwd(dh2, xn2, r2, gpre)

    acc = pl.BlockSpec((1, D_MODEL), lambda i: (0, 0))
    return pl.pallas_call(
        body, grid=(N_TOK_TILE,),
        in_specs=[_tok(D_MODEL), _tok(D_MODEL), _once((1, D_MODEL)), _once((1, D_MODEL)),
                  _once((N_CHIP, D_MODEL, 1024)), _once((N_CHIP, 512, D_MODEL)), _once((N_CHIP, 512, D_MODEL))],
        out_specs=[_tok(D_MODEL), _tok(D_FF), _tok(D_MODEL), _tok(D_MODEL), _tok(D_FF), acc, acc,
                   pl.BlockSpec((1, 128), lambda i: (0, 0))],
        out_shape=[jax.ShapeDtypeStruct((SEQ, D_MODEL), F32), jax.ShapeDtypeStruct((SEQ, D_FF), BF16),
                   jax.ShapeDtypeStruct((SEQ, D_MODEL), BF16), jax.ShapeDtypeStruct((SEQ, D_MODEL), BF16),
                   jax.ShapeDtypeStruct((SEQ, D_FF), BF16), jax.ShapeDtypeStruct((1, D_MODEL), F32),
                   jax.ShapeDtypeStruct((1, D_MODEL), F32), jax.ShapeDtypeStruct((1, 128), F32)],
        compiler_params=_params(("arbitrary",)), name="mlp_fwd_bwd",
    )(x1, target, g_pre, g_post, w_up4, w_down_a4, w_down_b4)


def _merge_bwd(dx1, mix, ab, gab, merged, w_pa4, w_pb4, w_o, g_post):
    def body(dx1_ref, mix_ref, ab_ref, gab_ref, mg_ref, wpa_ref, wpb_ref, wo_ref, g_ref,
             dgab_ref, dab_ref, dg_ref, dwpa_ref, dwpb_ref, dwo_ref, acc_pa, acc_pb, acc_o):
        @pl.when(pl.program_id(0) == 0)
        def _():
            dg_ref[...] = jnp.zeros_like(dg_ref)
            acc_pa[...] = jnp.zeros_like(acc_pa)
            acc_pb[...] = jnp.zeros_like(acc_pb)
            acc_o[...] = jnp.zeros_like(acc_o)

        dx1 = dx1_ref[...]
        mn, r = _rms_fwd(mix_ref[...])
        dg_ref[...] += jnp.sum(dx1 * mn, axis=0, keepdims=True)
        dmix = _rms_bwd(dx1, mn, r, g_ref[...]).astype(BF16)
        acc_o[...] += _tn(mg_ref[...], dmix)
        dmerged = _nt(dmix, wo_ref[...])
        a = ab_ref[:, 0:512]
        b = ab_ref[:, 512:1024]
        da = jnp.zeros((TOK_TILE, 512), F32)
        db = jnp.zeros((TOK_TILE, 512), F32)
        dyas, dybs = [], []
        for j in range(N_CHIP):
            blk = slice(j * 256, (j + 1) * 256)
            blk_b = slice(1024 + j * 256, 1024 + (j + 1) * 256)
            dm = dmerged[:, blk]
            ya = jnp.dot(a, wpa_ref[j], preferred_element_type=F32)
            yb = jnp.dot(b, wpb_ref[j], preferred_element_type=F32)
            sa = _sigmoid(gab_ref[:, blk])
            sb = _sigmoid(gab_ref[:, blk_b])
            dya = (dm * sa).astype(BF16)
            dyb = (dm * sb).astype(BF16)
            dyas.append(dya)
            dybs.append(dyb)
            dgab_ref[:, blk] = (dm * ya * sa * (1.0 - sa)).astype(BF16)
            dgab_ref[:, blk_b] = (dm * yb * sb * (1.0 - sb)).astype(BF16)
            da = da + _nt(dya, wpa_ref[j])
            db = db + _nt(dyb, wpb_ref[j])
        dab_ref[:, 0:512] = da
        dab_ref[:, 512:1024] = db
        acc_pa[...] += _tn(a, jnp.concatenate(dyas, axis=1))
        acc_pb[...] += _tn(b, jnp.concatenate(dybs, axis=1))

        @pl.when(pl.program_id(0) == N_TOK_TILE - 1)
        def _():
            dwo_ref[...] = acc_o[...].astype(BF16)
            for j in range(N_CHIP):
                dwpa_ref[j] = acc_pa[:, j * 256:(j + 1) * 256].astype(BF16)
                dwpb_ref[j] = acc_pb[:, j * 256:(j + 1) * 256].astype(BF16)

    whole = lambda shape: pl.BlockSpec(shape, lambda i: (0,) * len(shape))
    return pl.pallas_call(
        body, grid=(N_TOK_TILE,),
        in_specs=[_tok(D_MODEL), _tok(D_MODEL), _tok(1024), _tok(GAB_W), _tok(D_MODEL), _once((N_CHIP, 512, 256)),
                  _once((N_CHIP, 512, 256)), _once((D_MODEL, D_MODEL)), _once((1, D_MODEL))],
        out_specs=[_tok(GAB_W), _tok(1024), whole((1, D_MODEL)), whole((N_CHIP, 512, 256)), whole((N_CHIP, 512, 256)),
                   whole((D_MODEL, D_MODEL))],
        out_shape=[jax.ShapeDtypeStruct((SEQ, GAB_W), BF16), jax.ShapeDtypeStruct((SEQ, 1024), F32),
                   jax.ShapeDtypeStruct((1, D_MODEL), F32), jax.ShapeDtypeStruct((N_CHIP, 512, 256), BF16),
                   jax.ShapeDtypeStruct((N_CHIP, 512, 256), BF16), jax.ShapeDtypeStruct((D_MODEL, D_MODEL), BF16)],
        scratch_shapes=[pltpu.VMEM((512, D_MODEL), F32), pltpu.VMEM((512, D_MODEL), F32),
                        pltpu.VMEM((D_MODEL, D_MODEL), F32)],
        compiler_params=_params(("arbitrary",)), name="merge_bwd",
    )(dx1, mix, ab, gab, merged, w_pa4, w_pb4, w_o, g_post)


def _in_proj_bwd(dpm, dgab, x, dx1, g_pre, w4_in, rider=None, rider_ins=()):
    def body(dpm_ref, dgab_ref, x_ref, dx1_ref, g_ref, w4_ref, dx_ref, dg_ref, wt_ref):
        _join_rows(w4_ref, wt_ref)

        @pl.when(pl.program_id(0) == 0)
        def _():
            dg_ref[...] = jnp.zeros_like(dg_ref)

        dh = jnp.dot(dpm_ref[:, 0:PM_XM], wt_ref[0:IN_ALOW, :], preferred_element_type=F32)
        dh = dh + jnp.dot(dpm_ref[:, PM_XM:PM_AL], wt_ref[IN_XM:IN_GATES, :], preferred_element_type=F32)
        dh = dh + jnp.dot(dpm_ref[:, PM_AL:PM_W], wt_ref[IN_ALOW:IN_ALOW + 128, :], preferred_element_type=F32)
        dh = dh + jnp.dot(dgab_ref[...], wt_ref[IN_GATES:D_IN, :], preferred_element_type=F32)
        xn, r = _rms_fwd(x_ref[...])
        dg_ref[...] += jnp.sum(dh * xn, axis=0, keepdims=True)
        dx_ref[...] = dx1_ref[...] + _rms_bwd(dh, xn, r, g_ref[...])

    return _tiled_call(
        body, [_tok(PM_W), _tok(GAB_W), _tok(D_MODEL), _tok(D_MODEL), _once((1, D_MODEL)),
               _once((N_CHIP, IN_SHARD, D_MODEL))],
        [_tok(D_MODEL), pl.BlockSpec((1, D_MODEL), lambda i: (0, 0))],
        [jax.ShapeDtypeStruct((SEQ, D_MODEL), F32), jax.ShapeDtypeStruct((1, D_MODEL), F32)],
        (dpm, dgab, x, dx1, g_pre, w4_in), "in_proj_bwd", rider, rider_ins, _joined_scratch())


def _dw_in(dpm, dgab, h):
    n_pm = PM_AL // 512
    n_blk = n_pm + GAB_W // 512

    def place(o_ref, rows, lo, hi):
        for j in range(N_CHIP):
            a, b = max(lo, j * IN_SHARD), min(hi, (j + 1) * IN_SHARD)
            if a < b:
                o_ref[j, a - j * IN_SHARD:b - j * IN_SHARD, :] = rows(a - lo, b - lo)

    def body(dpm_ref, dgab_ref, dal_ref, h_ref, o_ref, blk_ref):
        i = pl.program_id(0)

        @pl.when(i < n_pm)
        def _():
            blk_ref[...] = _tn(dpm_ref[...], h_ref[...]).astype(BF16)

        @pl.when(i >= n_pm)
        def _():
            blk_ref[...] = _tn(dgab_ref[...], h_ref[...]).astype(BF16)

        for k in range(n_blk):
            off = k * 512 + (IN_XM - IN_ALOW) * (k >= IN_ALOW // 512)

            @pl.when(i == k)
            def _():
                place(o_ref, lambda a, b: blk_ref[a:b, :], off, off + 512)

        @pl.when(i == 0)
        def _():
            a_low = _tn(dal_ref[...], h_ref[...])[0:IN_XM - IN_ALOW].astype(BF16)
            place(o_ref, lambda a, b: a_low[a:b], IN_ALOW, IN_XM)

    return pl.pallas_call(
        body, grid=(n_blk,),
        in_specs=[pl.BlockSpec((SEQ, 512), lambda i: (0, jnp.minimum(i, n_pm - 1))),
                  pl.BlockSpec((SEQ, 512), lambda i: (0, jnp.maximum(i - n_pm, 0))),
                  pl.BlockSpec((SEQ, 128), lambda i: (0, PM_AL // 128)),
                  _once((SEQ, D_MODEL))],
        out_specs=pl.BlockSpec((N_CHIP, IN_SHARD, D_MODEL), lambda i: (0, 0, 0)),
        out_shape=jax.ShapeDtypeStruct((N_CHIP, IN_SHARD, D_MODEL), BF16),
        scratch_shapes=[pltpu.VMEM((512, D_MODEL), BF16)],
        compiler_params=_params(("arbitrary",)), name="dw_in",
    )(dpm, dgab, dpm, h)


def _tn_matmul(a, b, name, shards=1, tm=1024, rider=None, rider_ins=()):
    m, n = a.shape[1], b.shape[1]
    tm = min(tm, m)
    tn = n // shards if shards > 1 else min(n, 1024)
    steps_i, steps_j = m // tm, n // tn
    r_in, r_out_specs, r_out_shape, r_scratch = _rider_specs(rider, rider_ins)

    def body(*refs):
        (a_ref, b_ref), ride_in, (o_ref,), ride_out, scratch = _split(refs, 2, len(r_in), 1, len(r_out_specs), len(r_scratch))
        step = pl.program_id(0) * steps_j + pl.program_id(1)
        _ride(rider, ("first",), step == 0, ride_in, ride_out, scratch)
        o_ref[...] = _tn(a_ref[...], b_ref[...]).astype(BF16)
        _ride(rider, ("middle", "last"), step == steps_i * steps_j - 1, ride_in, ride_out, scratch)

    if shards > 1:
        out_spec = pl.BlockSpec((None, tm, tn), lambda i, j: (j, i, 0))
        out_shape = jax.ShapeDtypeStruct((shards, m, tn), BF16)
    else:
        out_spec = pl.BlockSpec((tm, tn), lambda i, j: (i, j))
        out_shape = jax.ShapeDtypeStruct((m, n), BF16)
    res = pl.pallas_call(
        body, grid=(steps_i, steps_j),
        in_specs=[pl.BlockSpec((SEQ, tm), lambda i, j: (0, i)), pl.BlockSpec((SEQ, tn), lambda i, j: (0, j))] + r_in,
        out_specs=[out_spec] + r_out_specs, out_shape=[out_shape] + r_out_shape, scratch_shapes=r_scratch,
        compiler_params=_params(("arbitrary", "arbitrary")), name=name,
    )(a, b, *rider_ins)
    return res[0] if rider is None else (res[0], res[1:])


MESH = pl.DeviceIdType.MESH
ANY = pl.BlockSpec(memory_space=pl.ANY)
VMEM_WHOLE = pl.BlockSpec(memory_space=pltpu.VMEM)

_BIG = ("w_in", "w_pa", "w_pb", "w_o", "w_up", "w_down")
_BIG_SHARD = {"w_in": (IN_SHARD, D_MODEL), "w_pa": (512, 256), "w_pb": (512, 256), "w_o": (256, D_MODEL),
              "w_up": (D_MODEL, 1024), "w_down": (1024, D_MODEL),
              "w_down_a": (512, D_MODEL), "w_down_b": (512, D_MODEL)}
_BIG_SPLIT = {"w_in": 1, "w_pa": 0, "w_pb": 0, "w_o": 0, "w_up": 0, "w_down": 0, "w_down_a": 0, "w_down_b": 0}


def _half(ref, e, name, lead=0, part=None):
    axis = _BIG_SPLIT[name]
    size = _BIG_SHARD[name][axis] // 2
    start = e * size
    if part is not None:
        size //= 2
        start = start + part * size
    start = pl.multiple_of(start, 128 if axis == 1 else 16)
    idx = [pl.ds(0, ref.shape[a]) for a in range(lead)]
    idx += [pl.ds(start, size), pl.ds(0, _BIG_SHARD[name][1])] if axis == 0 else [pl.ds(0, _BIG_SHARD[name][0]), pl.ds(start, size)]
    return ref.at[tuple(idx)]


def _half_shape(name):
    r, c = _BIG_SHARD[name]
    return (r // 2, c) if _BIG_SPLIT[name] == 0 else (r, c // 2)


def _remote(src, dst, send_sems, recv_sems, k, to):
    return pltpu.make_async_remote_copy(src_ref=src, dst_ref=dst, send_sem=send_sems.at[k], recv_sem=recv_sems.at[k],
                                        device_id=to, device_id_type=MESH)


def _mesh_place():
    x, y, c = lax.axis_index("x"), lax.axis_index("y"), lax.axis_index("c")
    return x, y, c, [(1 - x, y), (x, 1 - y), (1 - x, 1 - y)]


class _Gather:
    def __init__(self, names, small=(), middle_at=0.5):
        self.middle_at = middle_at
        self.names = tuple(names)
        self.nb = len(self.names)
        self.n = self.nb + len(small)
        self.n_sems = 8 * self.nb + 3 * len(small)
        self.n_flush = 6 * self.nb + len(small)
        self.out_shape = [jax.ShapeDtypeStruct((N_CHIP,) + _BIG_SHARD[nm], BF16) for nm in self.names]
        self.out_shape += [jax.ShapeDtypeStruct((N_CHIP,) + s.shape, s.dtype) for s in small]
        self.in_space = [self._in_spec(nm) for nm in self.names] + [VMEM_WHOLE] * len(small)

    @staticmethod
    def _in_spec(name):
        if name in ("w_down_a", "w_down_b"):
            half = 0 if name == "w_down_a" else 1
            return pl.BlockSpec(_BIG_SHARD[name], lambda *_: (half, 0), pipeline_mode=pl.Buffered(1))
        return VMEM_WHOLE

    def _copies(self, ins, outs, ss, rs, k):
        x, y, c, _ = _mesh_place()
        name = self.names[k]
        me, xn, yn, dg = 2 * x + y, 2 * (1 - x) + y, 2 * x + (1 - y), 2 * (1 - x) + (1 - y)
        to_x, to_y, sibling = (1 - x, y, c), (x, 1 - y, c), (x, y, 1 - c)

        def region(slot, e, part=None):
            return _half(outs[k].at[slot], e, name, part=part)

        def copy(pair, src, dst, to):
            return _remote(src, dst, ss, rs, 8 * k + pair, to)

        mine = region(me, c)
        sent = [copy(0, mine, mine, to_x), copy(1, mine, mine, to_y),
                copy(2, region(xn, c, 0), region(xn, c, 0), to_y), copy(3, region(yn, c, 1), region(yn, c, 1), to_x),
                copy(4, region(xn, c), region(xn, c), sibling), copy(5, region(yn, c), region(yn, c), sibling),
                copy(6, region(dg, c, 0), region(dg, c, 0), sibling), copy(7, region(dg, c, 1), region(dg, c, 1), sibling)]
        landing = [region(xn, c), region(yn, c), region(dg, c, 0), region(dg, c, 1),
                   region(xn, 1 - c), region(yn, 1 - c), region(dg, 1 - c, 0), region(dg, 1 - c, 1)]
        received = [copy(pair, dst, dst, sibling) for pair, dst in enumerate(landing)]
        return sent, received

    def _small(self, ins, outs, ss, rs, k, j, peer, slot, c):
        return _remote(ins[k], outs[k].at[slot], ss, rs, 8 * self.nb + 3 * (k - self.nb) + j, (*peer, c))

    def flush(self, phase, lands, outs, fs):
        x, y, c, _ = _mesh_place()
        me, xn, yn, dg = 2 * x + y, 2 * (1 - x) + y, 2 * x + (1 - y), 2 * (1 - x) + (1 - y)

        def pieces(k):
            name = self.names[k]
            spots = [lambda r: r.at[me], lambda r: _half(r.at[xn], c, name), lambda r: _half(r.at[yn], c, name),
                     lambda r: _half(r.at[xn], 1 - c, name), lambda r: _half(r.at[yn], 1 - c, name), lambda r: r.at[dg]]
            return [pltpu.make_async_copy(spot(lands[k]), spot(outs[k]), fs.at[6 * k + t]) for t, spot in enumerate(spots)]

        ready = {"first": (0,), "middle": (1, 2), "late": (3, 4), "last": (5,)}[phase]
        for k in range(self.nb):
            cps = pieces(k)
            for t in ready:
                cps[t].start()
        if phase == "last":
            small = [pltpu.make_async_copy(lands[k], outs[k], fs.at[6 * self.nb + k - self.nb]) for k in range(self.nb, self.n)]
            for cp in small:
                cp.start()
            for k in range(self.nb):
                for cp in pieces(k):
                    cp.wait()
            for cp in small:
                cp.wait()

    def first(self, ins, outs, ss, rs):
        x, y, c, peers = _mesh_place()
        me = 2 * x + y
        for k in range(self.nb):
            outs[k][me] = ins[k][...].astype(BF16)
            sent, _ = self._copies(ins, outs, ss, rs, k)
            sent[0].start()
            sent[1].start()
        for k in range(self.nb, self.n):
            for j, peer in enumerate(peers):
                self._small(ins, outs, ss, rs, k, j, peer, me, c).start()
            outs[k][me] = ins[k][...]

    def middle(self, ins, outs, ss, rs):
        for k in range(self.nb):
            sent, received = self._copies(ins, outs, ss, rs, k)
            for pair in (0, 1):
                received[pair].wait_recv()
                sent[2 + pair].start()
                sent[4 + pair].start()

    def late(self, ins, outs, ss, rs):
        for k in range(self.nb):
            _, received = self._copies(ins, outs, ss, rs, k)
            for pair in (4, 5):
                received[pair].wait_recv()

    def last(self, ins, outs, ss, rs):
        x, y, c, peers = _mesh_place()
        for k in range(self.nb):
            sent, received = self._copies(ins, outs, ss, rs, k)
            for pair in (2, 3):
                received[pair].wait_recv()
                sent[4 + pair].start()
        for k in range(self.nb):
            sent, received = self._copies(ins, outs, ss, rs, k)
            for pair in (6, 7):
                received[pair].wait_recv()
            for cp in sent:
                cp.wait_send()
        for k in range(self.nb, self.n):
            for j, (px, py) in enumerate(peers):
                self._small(ins, outs, ss, rs, k, j, (px, py), 2 * px + py, c).wait_recv()
                self._small(ins, outs, ss, rs, k, j, (px, py), 2 * x + y, c).wait_send()


def _run_alone(rider, ins, name):
    r_in, r_out_specs, r_out_shape, r_scratch = _rider_specs(rider, ins)

    def body(*refs):
        ride_in, ride_out, scratch = _split(refs, len(r_in), len(r_out_specs), len(r_scratch))
        _ride(rider, ("first", "middle", "last"), pl.program_id(0) == 0, ride_in, ride_out, scratch)

    return pl.pallas_call(
        body, grid=(1,), in_specs=r_in, out_specs=r_out_specs, out_shape=r_out_shape, scratch_shapes=r_scratch,
        compiler_params=_params(("arbitrary",)), name=name,
    )(*ins)


class _Presum:
    in_space = ANY

    def __init__(self, names, base=0):
        self.names = tuple(names)
        self.n = len(self.names)
        self.base = base
        self.n_sems = 3 * self.n
        self.out_shape = [jax.ShapeDtypeStruct((N_CHIP,) + _half_shape(nm), BF16) for nm in self.names]
        self.work_shape = self.out_shape + self.out_shape

    def _stage(self, ins, bufs, ss, k, e, which):
        n = self.n
        return pltpu.make_async_copy(_half(ins[k], e, self.names[k], lead=1), bufs[which * n + k],
                                     ss.at[self.base + which * n + k])

    def _give(self, bufs, ss, rs, k, sibling):
        return _remote(bufs[self.n + k], bufs[k], ss, rs, self.base + k, sibling)

    def first(self, ins, bufs, ss, rs):
        x, y, c, _ = _mesh_place()
        for k in range(self.n):
            self._stage(ins, bufs, ss, k, 1 - c, 1).start()
        for k in range(self.n):
            self._stage(ins, bufs, ss, k, c, 2).start()
        for k in range(self.n):
            self._stage(ins, bufs, ss, k, 1 - c, 1).wait()
            self._give(bufs, ss, rs, k, (x, y, 1 - c)).start()

    def middle(self, ins, bufs, ss, rs):
        pass

    def last(self, ins, bufs, ss, rs):
        x, y, c, _ = _mesh_place()
        for k in range(self.n):
            self._give(bufs, ss, rs, k, (x, y, 1 - c)).wait_recv()
            self._stage(ins, bufs, ss, k, c, 2).wait()

            @pl.loop(0, N_CHIP)
            def _(j):
                bufs[k][j] = (bufs[k][j].astype(F32) + bufs[2 * self.n + k][j].astype(F32)).astype(BF16)
        for k in range(self.n):
            self._give(bufs, ss, rs, k, (x, y, 1 - c)).wait_send()


class _ReduceRelay:
    middle_at = 0.75

    def __init__(self, names, base=0):
        self.names = tuple(names)
        self.n = len(self.names)
        self.base = base
        self.n_sems = 6 * self.n
        self.out_shape = [jax.ShapeDtypeStruct((N_CHIP,) + _half_shape(nm), BF16) for nm in self.names]
        quarter = [jax.ShapeDtypeStruct(self._part_shape(nm), BF16) for nm in self.names]
        self.work_shape = quarter + quarter

    @staticmethod
    def _part_shape(name):
        r, c = _half_shape(name)
        return (r // 2, c) if _BIG_SPLIT[name] == 0 else (r, c // 2)

    def _part(self, ref, name, p):
        r, c = self._part_shape(name)
        return ref.at[pl.ds(p * r, r), pl.ds(0, c)] if _BIG_SPLIT[name] == 0 else ref.at[pl.ds(0, r), pl.ds(p * c, c)]

    def _copies(self, ins, bufs, ss, rs, k):
        x, y, c, _ = _mesh_place()
        name, n = self.names[k], self.n
        me, xn, yn, dg = 2 * x + y, 2 * (1 - x) + y, 2 * x + (1 - y), 2 * (1 - x) + (1 - y)
        to_x, to_y = (1 - x, y, c), (x, 1 - y, c)
        mine = lambda slot, p: self._part(ins[k].at[slot], name, p)
        slot = lambda s, p: self._part(bufs[k].at[s], name, p)
        from_x, from_y = bufs[n + k], bufs[2 * n + k]

        def copy(pair, src, dst, to):
            return _remote(src, dst, ss, rs, self.base + 6 * k + pair, to)

        sent = [copy(0, mine(dg, 0), from_x, to_x), copy(1, mine(dg, 1), from_y, to_y),
                copy(2, mine(xn, 0), slot(me, 0), to_x), copy(3, mine(yn, 1), slot(me, 1), to_y),
                copy(4, from_y, slot(me, 1), to_x), copy(5, from_x, slot(me, 0), to_y)]
        landing = [from_x, from_y, slot(xn, 0), slot(yn, 1), slot(xn, 1), slot(yn, 0)]
        received = [copy(pair, dst, dst, to_x) for pair, dst in enumerate(landing)]
        return sent, received

    def first(self, ins, bufs, ss, rs):
        x, y, c, _ = _mesh_place()
        me, dg = 2 * x + y, 2 * (1 - x) + (1 - y)
        for k in range(self.n):
            sent, _ = self._copies(ins, bufs, ss, rs, k)
            for pair in range(4):
                sent[pair].start()
        for k in range(self.n):
            bufs[k][me] = ins[k][me]
            bufs[k][dg] = jnp.zeros(_half_shape(self.names[k]), BF16)

    def middle(self, ins, bufs, ss, rs):
        x, y, c, _ = _mesh_place()
        xn, yn = 2 * (1 - x) + y, 2 * x + (1 - y)
        for k in range(self.n):
            sent, received = self._copies(ins, bufs, ss, rs, k)
            name, n = self.names[k], self.n
            for pair, buf, own in ((0, bufs[n + k], self._part(ins[k].at[yn], name, 0)),
                                   (1, bufs[2 * n + k], self._part(ins[k].at[xn], name, 1))):
                received[pair].wait_recv()
                buf[...] = (buf[...].astype(F32) + own[...].astype(F32)).astype(BF16)
            sent[5].start()
            sent[4].start()

    def last(self, ins, bufs, ss, rs):
        for k in range(self.n):
            sent, received = self._copies(ins, bufs, ss, rs, k)
            for pair in range(2, 6):
                received[pair].wait_recv()
            for cp in sent:
                cp.wait_send()


class _PresumThenRelay:
    in_space = ANY
    middle_at = _ReduceRelay.middle_at

    def __init__(self, names):
        self.relay = _ReduceRelay(names)
        self.pre = _Presum(names, base=self.relay.n_sems)
        self.n_sems = self.relay.n_sems + self.pre.n_sems
        self.out_shape = self.relay.out_shape
        self.work_shape = list(self.relay.work_shape) + list(self.pre.out_shape) + list(self.pre.work_shape)
        self.n_relay = len(self.relay.out_shape) + len(self.relay.work_shape)

    def first(self, ins, bufs, ss, rs):
        self.pre.first(ins, bufs[self.n_relay:], ss, rs)

    def early(self, ins, bufs, ss, rs):
        self.pre.last(ins, bufs[self.n_relay:], ss, rs)
        self.relay.first(bufs[self.n_relay:], bufs[:self.n_relay], ss, rs)

    def middle(self, ins, bufs, ss, rs):
        self.relay.middle(bufs[self.n_relay:], bufs[:self.n_relay], ss, rs)

    def last(self, ins, bufs, ss, rs):
        self.relay.last(bufs[self.n_relay:], bufs[:self.n_relay], ss, rs)


class _SendPartials:
    def __init__(self, names, small_shape=None):
        self.n = len(names)
        self.small = small_shape is not None
        self.n_sems = 3 * self.n + 7
        self.out_shape = [jax.ShapeDtypeStruct((N_CHIP,) + _half_shape(nm), BF16) for nm in names]
        if self.small:
            self.out_shape.append(jax.ShapeDtypeStruct((N_DEV,) + small_shape, F32))

    def _piece(self, ins, outs, ss, rs, k, j, peer, src_slot, dst_slot, c):
        return _remote(ins[k].at[src_slot], outs[k].at[dst_slot], ss, rs, 3 * k + j, (*peer, c))

    def _small(self, ins, outs, ss, rs, r, other, slot):
        return _remote(ins[self.n], outs[self.n].at[slot], ss, rs, 3 * self.n + r, other)

    @staticmethod
    def _others(x, y, c):
        return [(x, y, 1 - c), (1 - x, y, c), (1 - x, y, 1 - c), (x, 1 - y, c), (x, 1 - y, 1 - c),
                (1 - x, 1 - y, c), (1 - x, 1 - y, 1 - c)]

    def first(self, ins, outs, ss, rs, only=None):
        x, y, c, peers = _mesh_place()
        me = 2 * x + y
        which = range(self.n) if only is None else only
        for k in which:
            for j, (px, py) in enumerate(peers):
                self._piece(ins, outs, ss, rs, k, j, (px, py), 2 * px + py, me, c).start()
        if self.small:
            for r, other in enumerate(self._others(x, y, c)):
                self._small(ins, outs, ss, rs, r, other, 4 * x + 2 * y + c).start()
            outs[self.n][4 * x + 2 * y + c] = ins[self.n][...]
        for k in which:
            outs[k][me] = ins[k][me]

    def middle(self, ins, outs, ss, rs):
        pass

    def last(self, ins, outs, ss, rs):
        x, y, c, peers = _mesh_place()
        me = 2 * x + y
        for k in range(self.n):
            for j, (px, py) in enumerate(peers):
                self._piece(ins, outs, ss, rs, k, j, (px, py), me, 2 * px + py, c).wait_recv()
                self._piece(ins, outs, ss, rs, k, j, (px, py), 2 * px + py, me, c).wait_send()
        if self.small:
            for r, (px, py, pc) in enumerate(self._others(x, y, c)):
                self._small(ins, outs, ss, rs, r, (px, py, pc), 4 * px + 2 * py + pc).wait_recv()
                self._small(ins, outs, ss, rs, r, (px, py, pc), 4 * x + 2 * y + c).wait_send()


class _PresumThenSend:
    def __init__(self, names):
        self.send = _SendPartials(names)
        self.pre = _Presum(names[-1:], base=self.send.n_sems)
        self.n = self.send.n
        self.n_sems = self.send.n_sems + self.pre.n_sems
        self.out_shape = self.send.out_shape
        self.work_shape = list(self.pre.out_shape) + list(self.pre.work_shape)
        self.in_space = [VMEM_WHOLE] * (self.n - 1) + [ANY]

    def _partials(self, ins, bufs):
        return list(ins[:self.n - 1]) + [bufs[self.n]]

    def first(self, ins, bufs, ss, rs):
        self.pre.first(ins[self.n - 1:], bufs[self.n:], ss, rs)
        self.send.first(ins, bufs[:self.n], ss, rs, only=range(self.n - 1))

    def early(self, ins, bufs, ss, rs):
        self.pre.last(ins[self.n - 1:], bufs[self.n:], ss, rs)
        self.send.first(self._partials(ins, bufs), bufs[:self.n], ss, rs, only=(self.n - 1,))

    def middle(self, ins, bufs, ss, rs):
        pass

    def last(self, ins, bufs, ss, rs):
        self.send.last(self._partials(ins, bufs), bufs[:self.n], ss, rs)


def _sum_swap(names, parts, small):
    n = len(parts)
    everyone = _SendPartials((), small.shape)

    def body(*refs):
        (p_hbm, (small_ref,), o_hbm, (osmall_ref,), p_refs, o_refs, (all_ref,),
         (send_sems, recv_sems, ss_small, rs_small, load_sems, leave_sems)) = _split(refs, n, 1, n, 1, n, n, 1, 6)
        x, y, c = lax.axis_index("x"), lax.axis_index("y"), lax.axis_index("c")
        loads = [pltpu.make_async_copy(p_hbm[k], p_refs[k], load_sems.at[k]) for k in range(n)]
        for cp in loads:
            cp.start()
        everyone.first([small_ref], [all_ref], ss_small, rs_small)

        def mine(k):
            part = _half(o_refs[k], c, names[k])
            return _remote(part, part, send_sems, recv_sems, k, (x, y, 1 - c))

        def leave(k, whose):
            e = c if whose == 0 else 1 - c
            return pltpu.make_async_copy(_half(o_refs[k], e, names[k]), _half(o_hbm[k], e, names[k]),
                                         leave_sems.at[2 * k + whose])

        for k in range(n):
            loads[k].wait()
            for e in range(2):
                @pl.when(c == e)
                def _():
                    g = p_refs[k][0].astype(F32)
                    for s in range(1, N_CHIP):
                        g = g + p_refs[k][s].astype(F32)
                    r, cols = _half_shape(names[k])
                    if _BIG_SPLIT[names[k]] == 0:
                        o_refs[k][e * r:(e + 1) * r, :] = g
                    else:
                        o_refs[k][:, e * cols:(e + 1) * cols] = g
            mine(k).start()
            leave(k, 0).start()
        for k in range(n):
            theirs = _half(o_refs[k], 1 - c, names[k])
            _remote(theirs, theirs, send_sems, recv_sems, k, (x, y, 1 - c)).wait_recv()
            leave(k, 1).start()
        everyone.last([small_ref], [all_ref], ss_small, rs_small)
        g = all_ref[0]
        for d in range(1, N_DEV):
            g = g + all_ref[d]
        osmall_ref[...] = g
        for k in range(n):
            mine(k).wait_send()
            leave(k, 0).wait()
            leave(k, 1).wait()

    shards = [jax.ShapeDtypeStruct(_BIG_SHARD[nm], F32) for nm in names]
    res = pl.pallas_call(
        body, in_specs=[ANY] * n + [VMEM_WHOLE], out_specs=[ANY] * n + [VMEM_WHOLE],
        out_shape=shards + [jax.ShapeDtypeStruct(small.shape, F32)],
        scratch_shapes=[pltpu.VMEM(q.shape, q.dtype) for q in parts] + [pltpu.VMEM(s.shape, s.dtype) for s in shards]
        + [pltpu.VMEM((N_DEV,) + small.shape, F32), pltpu.SemaphoreType.DMA((n,)), pltpu.SemaphoreType.DMA((n,)),
           pltpu.SemaphoreType.DMA((everyone.n_sems,)), pltpu.SemaphoreType.DMA((everyone.n_sems,)),
           pltpu.SemaphoreType.DMA((n,)), pltpu.SemaphoreType.DMA((2 * n,))],
        compiler_params=_params(), name="sum_swap",
    )(*parts, small)
    return res[:n], res[n]


def _tile(rows, cols, itemsize, budget):
    t = cols if rows % 16 else rows
    other = rows if rows % 16 else cols
    step = 256 if rows % 16 else 32
    while t % step == 0 and t * other * itemsize > budget:
        t //= 2
    return (rows, t) if rows % 16 else (t, cols)


def _adamw_math(w, g, m, v):
    m = ADAM_B1 * m + (1.0 - ADAM_B1) * g
    v = ADAM_B2 * v + (1.0 - ADAM_B2) * (g * g)
    m_hat = m / (1.0 - ADAM_B1 ** ADAM_STEP)
    v_hat = v / (1.0 - ADAM_B2 ** ADAM_STEP)
    delta = -ADAM_LR * (m_hat / (jnp.sqrt(v_hat) + ADAM_EPS) + ADAM_WD * w)
    return delta, m, v


def _adamw_big(g, w, m, v, name):
    r, c = w.shape
    tr, tc = _tile(r, c, 4, 2 * 1024 * 1024)

    def body(g_ref, w_ref, m_ref, v_ref, g_out_ref, d_ref, nm_ref, nv_ref):
        g = g_ref[...]
        g_out_ref[...] = g
        d_ref[...], nm_ref[...], nv_ref[...] = _adamw_math(w_ref[...], g, m_ref[...], v_ref[...])

    blk = pl.BlockSpec((tr, tc), lambda i, l: (i, l))
    return pl.pallas_call(
        body, grid=(r // tr, c // tc), in_specs=[blk, blk, blk, blk],
        out_specs=[blk] * 4, out_shape=[jax.ShapeDtypeStruct((r, c), F32)] * 4,
        compiler_params=_params(("arbitrary", "arbitrary")), name=name,
    )(g, w, m, v)


def _adamw_rows(g, w, m, v, name):
    r, k, lanes = w.shape
    tr = 296

    def body(g_ref, w_ref, m_ref, v_ref, g3_ref, d_ref, nm_ref, nv_ref):
        g = g_ref[...].reshape(tr, k, lanes)
        g3_ref[...] = g
        d_ref[...], nm_ref[...], nv_ref[...] = _adamw_math(w_ref[...], g, m_ref[...], v_ref[...])

    rows = pl.BlockSpec((tr, k, lanes), lambda i: (i, 0, 0))
    return pl.pallas_call(
        body, grid=(pl.cdiv(r, tr),), in_specs=[pl.BlockSpec((tr, k * lanes), lambda i: (i, 0)), rows, rows, rows],
        out_specs=[rows] * 4, out_shape=[jax.ShapeDtypeStruct((r, k, lanes), F32)] * 4,
        compiler_params=_params(("arbitrary",)), name=name,
    )(g, w, m, v)


def _adamw_small(ws, gs, ms, vs):
    n = len(ws)

    def body(*refs):
        w_refs, g_refs, m_refs, v_refs, d_refs, nm_refs, nv_refs = _split(refs, *([n] * 7))
        for k in range(n):
            d_refs[k][...], nm_refs[k][...], nv_refs[k][...] = _adamw_math(w_refs[k][...], g_refs[k][...], m_refs[k][...],
                                                                             v_refs[k][...])

    shapes = [jax.ShapeDtypeStruct(w.shape, F32) for w in ws]
    res = pl.pallas_call(body, out_shape=shapes * 3, name="adamw_small")(*ws, *gs, *ms, *vs)
    return res[:n], res[n:2 * n], res[2 * n:]


def _pack(arrs):
    flat = jnp.concatenate([a.reshape(-1) for a in arrs])
    rows = -(-flat.shape[0] // 1024) * 8
    return jnp.pad(flat, (0, rows * 128 - flat.shape[0])).reshape(rows, 128)


def _unpack(buf, shapes):
    flat = buf.reshape(-1)
    out, off = [], 0
    for s in shapes:
        size = 1
        for d in s:
            size *= d
        out.append(flat[off:off + size].reshape(s))
        off += size
    return out


def _block_rows(w):
    return jnp.pad(w.reshape(512, 4), ((0, 0), (0, 124)))


def _block_stored(dw):
    return jnp.transpose(dw[:, 0:4].reshape(128, 4, 4), (1, 2, 0)).reshape(16, 128)


def _cols(a4):
    return jnp.transpose(a4, (1, 0, 2)).reshape(a4.shape[1], -1)


_LATE = ("w_pa", "w_pb", "w_o", "w_up", "w_down")
_RIDE_IN_PROJ = ("w_pa", "w_pb", "w_o", "w_down_b")
_RIDE_MIXER = ("w_up", "w_down_a")


def _full_weights(gathered):
    joined = {"w_o": (D_MODEL, D_MODEL)}
    return {n: (a.reshape(joined[n]) if n in joined else a) for n, a in gathered.items()}


def _local_step(x, target, w, sp, late_shards=None):
    sp = {n: (a.reshape(1, -1) if a.ndim == 1 else a) for n, a in sp.items()}
    wau = jnp.pad(sp["w_a_up"], ((0, 112), (0, 0)))
    wif = jnp.pad(sp["w_if"], ((0, 0), (0, 120)))
    bif = jnp.pad(sp["b_if"], ((0, 0), (0, 120)))
    p = {"wau": wau, "bau": sp["b_a_up"], "ggla": sp["g_gla_norm"], "cw": sp["conv_w"], "cb": sp["conv_b"],
         "wq": _block_rows(sp["w_q_ml"]), "wk": _block_rows(sp["w_k_ml"]), "wv": _block_rows(sp["w_v_ml"]),
         "wif": wif, "bif": bif, "skip": sp["ml_skip"], "gml": sp["g_ml_norm"]}

    if late_shards is None:
        (pm, gab, h), _ = _in_proj(x, sp["g_pre_mix"], w["w_in"])
        ab, *states = _mixer_fwd(pm, p)
        (x1, mix, merged), _ = _merge_fwd(ab, gab, x, w["w_pa"], w["w_pb"], w["w_o"], sp["g_post_mix"])
    else:
        shard = dict(zip(_LATE, late_shards))
        shard["w_down_a"] = shard["w_down_b"] = shard["w_down"]
        (pm, gab, h), got = _in_proj(x, sp["g_pre_mix"], w["w_in"], _Gather(_RIDE_IN_PROJ, middle_at=0.7),
                                     [shard[n] for n in _RIDE_IN_PROJ])
        w = dict(w, **_full_weights(dict(zip(_RIDE_IN_PROJ, got))))
        ab, *rest = _mixer_fwd(pm, p, _Gather(_RIDE_MIXER, middle_at=0.62), [shard[n] for n in _RIDE_MIXER])
        states = rest[:4]
        w.update(_full_weights(dict(zip(_RIDE_MIXER, rest[4:]))))
        (x1, mix, merged), _ = _merge_fwd(ab, gab, x, w["w_pa"], w["w_pb"], w["w_o"], sp["g_post_mix"])
    dx1, u, dd, h2, dpre, dg_post_mlp, dg_pre_mlp, loss = _mlp(x1, target, sp["g_pre_mlp"], sp["g_post_mlp"],
                                                                w["w_up"], w["w_down_a"], w["w_down_b"])
    dgab, dab, dg_post_mix, dw_pa, dw_pb, dw_o = _merge_bwd(dx1, mix, ab, gab, merged, w["w_pa"], w["w_pb"], w["w_o"],
                                                            sp["g_post_mix"])
    big = {"w_pa": dw_pa, "w_pb": dw_pb, "w_o": dw_o, "w_up": _tn_matmul(h2, dpre, "dw_up", shards=N_CHIP)}
    if late_shards is None:
        big["w_down"] = _tn_matmul(u, dd, "dw_down")
        dpm, dp, _ = _mixer_bwd(pm, dab, states, p)
    else:
        pieces = lambda n: big[n].reshape((N_CHIP,) + _BIG_SHARD[n])
        big["w_down"], partial = _tn_matmul(u, dd, "dw_down", rider=_Presum(_LATE[:4]),
                                            rider_ins=[pieces(n) for n in _LATE[:4]])
        dpm, dp, parts = _mixer_bwd(pm, dab, states, p, _PresumThenSend(_LATE), list(partial) + [pieces("w_down")])
        big = dict(zip(_LATE, parts))
    big["w_in"] = _dw_in(dpm, dgab, h)
    if late_shards is None:
        (dx, dg_pre_mix), _ = _in_proj_bwd(dpm, dgab, x, dx1, sp["g_pre_mix"], w["w_in"])
    else:
        (dx, dg_pre_mix), parts = _in_proj_bwd(dpm, dgab, x, dx1, sp["g_pre_mix"], w["w_in"], _PresumThenRelay(("w_in",)),
                                               [big["w_in"]])
        big["w_in"] = parts[0]
    small = {
        "g_pre_mix": dg_pre_mix, "b_a_up": dp["bau"], "g_gla_norm": dp["ggla"], "conv_b": dp["cb"],
        "w_q_ml": _block_stored(dp["wq"]), "w_k_ml": _block_stored(dp["wk"]), "w_v_ml": _block_stored(dp["wv"]),
        "w_if": dp["wif"][:, 0:8].T,
        "b_if": dp["bif"][:, 0:8], "ml_skip": dp["skip"], "g_ml_norm": dp["gml"], "g_post_mix": dg_post_mix,
        "g_pre_mlp": dg_pre_mlp, "g_post_mlp": dg_post_mlp, "w_a_up": dp["wau"][0:16], "conv_w": dp["cw"],
        "loss": loss[:, 0:1],
    }
    return dx, big, small


_SMALL_REPL = ("g_pre_mix", "b_a_up", "g_gla_norm", "conv_b", "w_q_ml", "w_k_ml", "w_v_ml", "b_if", "ml_skip",
               "g_ml_norm", "g_post_mix", "g_pre_mlp", "g_post_mlp")
_SMALL_SHARDED = ("w_a_up", "conv_w", "w_if")
_SMALL_ORDER = _SMALL_REPL + _SMALL_SHARDED + ("loss",)
_WEIGHTS = ("g_pre_mix", "w_in", "w_a_up", "b_a_up", "g_gla_norm", "conv_w", "conv_b", "w_q_ml", "w_k_ml", "w_v_ml",
            "w_if", "b_if", "ml_skip", "g_ml_norm", "w_pa", "w_pb", "w_o", "g_post_mix", "g_pre_mlp", "w_up", "w_down",
            "g_post_mlp")


_BLOCK_WEIGHTS = ("w_q_ml", "w_k_ml", "w_v_ml")


def _stored(name, a):
    if name in _BLOCK_WEIGHTS:
        return jnp.transpose(a, (0, 2, 3, 1)).reshape(16, 128)
    if name == "w_if":
        return jnp.transpose(a, (0, 2, 1)).reshape(8, 384)
    return a


def _unstored(name, a):
    if name in _BLOCK_WEIGHTS:
        return jnp.transpose(a.reshape(1, 4, 4, 128), (0, 3, 1, 2))
    if name == "w_if":
        return jnp.transpose(a.reshape(1, 8, 384), (0, 2, 1))
    return a


def _as_shard(name, a):
    return jnp.transpose(a, (2, 0, 1)).reshape(IN_SHARD, D_MODEL // 128, 128) if name == "w_in" else a[0]


def _in_shard_bf16(w_in):
    return jnp.transpose(w_in.astype(BF16), (2, 0, 1)).reshape(IN_SHARD, D_MODEL)


def _from_shard(name, a):
    return jnp.transpose(a, (1, 2, 0)).reshape(1, D_MODEL, IN_SHARD) if name == "w_in" else a[None]


def kernel(x, g_pre_mix, w_in, w_a_up, b_a_up, g_gla_norm, conv_w, conv_b, w_q_ml, w_k_ml, w_v_ml, w_if, b_if, ml_skip, g_ml_norm, w_pa, w_pb, w_o, g_post_mix, g_pre_mlp, w_up, w_down, g_post_mlp, loss_target, m_g_pre_mix, m_w_in, m_w_a_up, m_b_a_up, m_g_gla_norm, m_conv_w, m_conv_b, m_w_q_ml, m_w_k_ml, m_w_v_ml, m_w_if, m_b_if, m_ml_skip, m_g_ml_norm, m_w_pa, m_w_pb, m_w_o, m_g_post_mix, m_g_pre_mlp, m_w_up, m_w_down, m_g_post_mlp, v_g_pre_mix, v_w_in, v_w_a_up, v_b_a_up, v_g_gla_norm, v_conv_w, v_conv_b, v_w_q_ml, v_w_k_ml, v_w_v_ml, v_w_if, v_b_if, v_ml_skip, v_g_ml_norm, v_w_pa, v_w_pb, v_w_o, v_g_post_mix, v_g_pre_mlp, v_w_up, v_w_down, v_g_post_mlp):
    args = dict(locals())
    wts = {n: _as_shard(n, args[n]) for n in _WEIGHTS}
    mom = {n: _as_shard(n, args["m_" + n]) for n in _WEIGHTS}
    var = {n: _as_shard(n, args["v_" + n]) for n in _WEIGHTS}
    chip = 2 * lax.axis_index("x") + lax.axis_index("y")

    first = ("w_in",) + _SMALL_SHARDED
    gathered = dict(zip(first, _run_alone(_Gather(("w_in",), [wts[n] for n in _SMALL_SHARDED]),
                                          [_in_shard_bf16(w_in)] + [wts[n] for n in _SMALL_SHARDED],
                                          "gather_first")))
    sp = {n: wts[n] for n in _SMALL_REPL}
    sp["w_a_up"] = _cols(gathered["w_a_up"])
    sp["conv_w"] = _cols(gathered["conv_w"])
    sp["w_if"] = gathered["w_if"].reshape(1536, 8)

    dx, big, small = _local_step(x[0], loss_target[0], _full_weights({"w_in": gathered["w_in"]}), sp,
                                 late_shards=[wts[n] for n in _LATE])

    small_shapes = [small[n].shape for n in _SMALL_ORDER]
    packed = _pack([small[n] for n in _SMALL_ORDER])
    sums, small_sum = _sum_swap(_BIG, [big[n] for n in _BIG], packed)

    grads, delta, new_m, new_v = {}, {}, {}, {}
    for n, g in zip(_BIG, sums):
        adamw = _adamw_rows if n == "w_in" else _adamw_big
        g, d, nm, nv = adamw(g, wts[n], mom[n], var[n], "adamw_" + n)
        grads[n], delta[n], new_m[n], new_v[n] = (_from_shard(n, a) for a in (g, d, nm, nv))
    summed = dict(zip(_SMALL_ORDER, _unpack(small_sum, small_shapes)))
    loss = summed["loss"].reshape(())
    summed["w_a_up"] = lax.dynamic_slice_in_dim(summed["w_a_up"], chip * 64, 64, axis=1)
    summed["conv_w"] = lax.dynamic_slice_in_dim(summed["conv_w"], chip * 128, 128, axis=1)
    summed["w_if"] = lax.dynamic_slice_in_dim(summed["w_if"], chip * 384, 384, axis=1)
    small_names = _SMALL_REPL + _SMALL_SHARDED
    came_stored = _BLOCK_WEIGHTS + ("w_if",)
    g_stored = [summed[n] if n in came_stored else _stored(n, summed[n].reshape(args[n].shape)) for n in small_names]
    upd = _adamw_small([_stored(n, args[n]) for n in small_names], g_stored,
                       [_stored(n, args["m_" + n]) for n in small_names], [_stored(n, args["v_" + n]) for n in small_names])
    for dst, arrs in zip((grads, delta, new_m, new_v), (g_stored,) + tuple(upd)):
        dst.update({n: _unstored(n, a) for n, a in zip(small_names, arrs)})

    outs = [loss, dx[None]]
    for group in (grads, delta, new_m, new_v):
        outs += [group[n] for n in _WEIGHTS]
    return tuple(outs)
```

```python
import functools

import jax
import jax.numpy as jnp
from jax import lax
from jax.experimental import pallas as pl
from jax.experimental.pallas import tpu as pltpu

F32 = jnp.float32
BF16 = jnp.bfloat16

SEQ = 2048
D_MODEL = 1024
CHUNK = 64
N_CHUNK = SEQ // CHUNK
HEADS = 4
GLA_DK = 64
GLA_DV = 128
ML_DH = 128
D_FF = 4096
EPS = 1e-6
N_CHIP = 4
N_DEV = 8
TOK_TILE = 256
N_TOK_TILE = SEQ // TOK_TILE
SWEEP = 2
assert CHUNK == 64
N_SWEEP = N_CHUNK // SWEEP

PM_W = 2688
PM_XM = 1536
PM_OP = 2048
PM_AL = 2560
GAB_W = 2048
D_IN = 4624
IN_SHARD = D_IN // N_CHIP
IN_ALOW = 1536
IN_XM = 1552
IN_GATES = 2576

ADAM_LR = 0.001
ADAM_B1 = 0.9
ADAM_B2 = 0.999
ADAM_EPS = 1e-08
ADAM_WD = 0.01
ADAM_STEP = 10

VMEM_LIMIT = 56 * 1024 * 1024


def _params(sem=None):
    return pltpu.CompilerParams(dimension_semantics=sem, vmem_limit_bytes=VMEM_LIMIT)


def _dot(a, b, ca, cb):
    return lax.dot_general(a.astype(BF16), b.astype(BF16), (((ca,), (cb,)), ((), ())), preferred_element_type=F32)


def _pmm_nn(a, b):
    return _dot(a, b, 1, 0)


def _pmm_nt(a, b):
    return _dot(a, b, 1, 1)


def _pmm_tn(a, b):
    return _dot(a, b, 0, 0)


def _pcmm(c, x):
    return lax.dot_general(c, x, (((1,), (0,)), ((), ())), precision=lax.Precision.HIGHEST, preferred_element_type=F32)


@jax.custom_vjp
def _mm_nn(a, b):
    return _dot(a, b, 1, 0)


@jax.custom_vjp
def _mm_nt(a, b):
    return _dot(a, b, 1, 1)


@jax.custom_vjp
def _mm_tn(a, b):
    return _dot(a, b, 0, 0)


_mm_nn.defvjp(lambda a, b: (_dot(a, b, 1, 0), (a, b)), lambda r, g: (_mm_nt(g, r[1]), _mm_tn(r[0], g)))
_mm_nt.defvjp(lambda a, b: (_dot(a, b, 1, 1), (a, b)), lambda r, g: (_mm_nn(g, r[1]), _mm_tn(g, r[0])))
_mm_tn.defvjp(lambda a, b: (_dot(a, b, 0, 0), (a, b)), lambda r, g: (_mm_nt(r[1], g), _mm_nn(r[0], g)))


@jax.custom_vjp
def _cmm(c, x):
    return _pcmm(c, x)


_cmm.defvjp(
    lambda c, x: (_pcmm(c, x), c),
    lambda c, g: (jnp.zeros_like(c), lax.dot_general(c, g, (((0,), (0,)), ((), ())), precision=lax.Precision.HIGHEST,
                                                      preferred_element_type=F32)),
)

_PLAIN_OPS = (_pmm_nn, _pmm_nt, _pmm_tn, _pcmm)
_VJP_OPS = (_mm_nn, _mm_nt, _mm_tn, _cmm)


def _sigmoid(x):
    return 0.5 * (jnp.tanh(0.5 * x) + 1.0)


def _log_sigmoid(x):
    return jnp.minimum(x, 0.0) - jnp.log(1.0 + jnp.exp(-jnp.abs(x)))


def _mean(x):
    return jnp.mean(x, axis=-1, keepdims=True)


def _nt(a, b):
    return lax.dot_general(a, b, (((1,), (1,)), ((), ())), preferred_element_type=F32)


def _tn(a, b):
    return lax.dot_general(a, b, (((0,), (0,)), ((), ())), preferred_element_type=F32)


def _mixer_chunk(ops, p, st, pm, xprev8):
    mm_nn, mm_nt, mm_tn, cmm = ops
    n_rows = pm.shape[0]
    n_ch = n_rows // CHUNK
    row = lax.broadcasted_iota(jnp.int32, (n_rows, n_rows), 0)
    col = lax.broadcasted_iota(jnp.int32, (n_rows, n_rows), 1)
    tri = jnp.logical_and((row >> 6) == (col >> 6), row >= col).astype(F32)
    causal = tri[0:CHUNK, 0:CHUNK] > 0.0
    q = pm[:, 0:256]
    k = pm[:, 256:512]
    v = pm[:, 512:1024]
    g = pm[:, 1024:1536]
    xm = pm[:, PM_XM:PM_XM + 512]
    opre = pm[:, PM_OP:PM_OP + 512]
    alow = pm[:, PM_AL:PM_AL + 128]
    hs = range(HEADS)
    cs = range(n_ch)
    pairs = [(i, h) for i in cs for h in hs]
    rs = [slice(i * CHUNK, (i + 1) * CHUNK) for i in cs]
    last = [slice((i + 1) * CHUNK - 1, (i + 1) * CHUNK) for i in cs]
    s6 = [slice(h * GLA_DK, (h + 1) * GLA_DK) for h in hs]
    s12 = [slice(h * 128, (h + 1) * 128) for h in hs]

    xx = jnp.concatenate([xprev8, xm], axis=0)
    pre = p["cb"]
    for j in range(4):
        pre = pre + p["cw"][j:j + 1, :] * xx[5 + j:5 + j + n_rows, :]
    xc = pre * _sigmoid(pre)
    qm = [mm_nn(xc[:, s12[h]], p["wq"][h]) for h in hs]
    km = [mm_nn(xc[:, s12[h]], p["wk"][h]) for h in hs]
    vm = [mm_nn(xm[:, s12[h]], p["wv"][h]) for h in hs]
    qcat = jnp.concatenate(qm, axis=1)
    kcat = jnp.concatenate(km, axis=1)
    vcat = jnp.concatenate(vm, axis=1)
    gates = (mm_nn(qcat, p["wif"][0:512]) + mm_nn(kcat, p["wif"][512:1024]) + mm_nn(vcat, p["wif"][1024:1536])
             + p["bif"])
    lf = _log_sigmoid(gates)
    fc = cmm(tri, lf)
    gates_t = gates.T
    fc_t = fc.T

    la = _log_sigmoid(mm_nn(alow, p["wau"]) + p["bau"]) * (1.0 / 16.0)
    cum = cmm(tri, la)
    cum_last = [cum[last[i], :] for i in cs]
    to_end = jnp.concatenate([cum_last[i] - cum[rs[i], :] for i in cs], axis=0)
    e_pos = jnp.exp(cum)
    e_neg = jnp.exp(-cum)
    qs = q * (GLA_DK ** -0.5)
    qp = qs * e_pos
    qn = qs * e_neg
    kp = k * e_pos
    kn = k * e_neg
    kl = k * jnp.exp(to_end)
    dec = [jnp.exp(cum_last[i]) for i in cs]
    ks = [km[h] * (ML_DH ** -0.5) for h in hs]
    li_c = {(i, h): gates[rs[i], h:h + 1] for i, h in pairs}
    fc_c = {(i, h): fc[rs[i], 4 + h:5 + h] for i, h in pairs}
    f_last = {(i, h): fc[last[i], 4 + h:5 + h] for i, h in pairs}

    a_fwd = {(i, h): mm_nt(qp[rs[i], s6[h]], kn[rs[i], s6[h]]) for i, h in pairs}
    a_bwd = {(i, h): mm_nt(qn[rs[i], s6[h]], kp[rs[i], s6[h]]) for i, h in pairs}
    s_chunk = {(i, h): mm_tn(v[rs[i], s12[h]], kl[rs[i], s6[h]]) for i, h in pairs}
    qk = {(i, h): mm_nt(qm[h][rs[i]], ks[h][rs[i]]) for i, h in pairs}
    a = {ih: f_last[ih] - fc_c[ih] + li_c[ih] for ih in pairs}
    m_loc = {ih: jnp.max(a[ih], axis=0, keepdims=True) for ih in pairs}
    kw = {(i, h): ks[h][rs[i]] * jnp.exp(a[(i, h)] - m_loc[(i, h)]) for i, h in pairs}
    c_chunk = {(i, h): mm_tn(kw[(i, h)], vm[h][rs[i]]) for i, h in pairs}
    mem = {(0, h): st["S"][h] for h in hs}
    c_in = {(0, h): st["C"][h] for h in hs}
    n_in = {(0, h): st["n"][h] for h in hs}
    m_in = {(0, h): st["m"][h][:, 0:1] for h in hs}
    for i, h in pairs:
        mem[(i + 1, h)] = mem[(i, h)] * dec[i][:, s6[h]] + s_chunk[(i, h)]
        m_nx = jnp.maximum(f_last[(i, h)] + m_in[(i, h)], m_loc[(i, h)])
        sp = jnp.exp(f_last[(i, h)] + m_in[(i, h)] - m_nx)
        sl = jnp.exp(m_loc[(i, h)] - m_nx)
        c_in[(i + 1, h)] = sp * c_in[(i, h)] + sl * c_chunk[(i, h)]
        n_in[(i + 1, h)] = sp * n_in[(i, h)] + sl * jnp.sum(kw[(i, h)], axis=0, keepdims=True)
        m_in[(i + 1, h)] = m_nx
    s_new = [mem[(n_ch, h)] for h in hs]
    o_inter = {(i, h): mm_nt(qp[rs[i], s6[h]], mem[(i, h)]) for i, h in pairs}
    q_c = {(i, h): mm_nn(qm[h][rs[i]], c_in[(i, h)]) for i, h in pairs}
    scores = {ih: jnp.where(causal, a_fwd[ih], a_bwd[ih]) for ih in pairs}
    log_d = {(i, h): gates_t[h:h + 1, rs[i]] - jnp.abs(fc_c[(i, h)] - fc_t[4 + h:5 + h, rs[i]]) for i, h in pairs}
    g_int = {ih: fc_c[ih] + m_in[ih] for ih in pairs}
    m_t = {ih: jnp.maximum(g_int[ih], jnp.max(log_d[ih], axis=1, keepdims=True)) for ih in pairs}
    s = {ih: qk[ih] * jnp.exp(log_d[ih] - m_t[ih]) for ih in pairs}
    scl = {ih: jnp.exp(g_int[ih] - m_t[ih]) for ih in pairs}
    o = {(i, h): mm_nn(scores[(i, h)], v[rs[i], s12[h]]) + o_inter[(i, h)] for i, h in pairs}
    num = {(i, h): mm_nn(s[(i, h)], vm[h][rs[i]]) + scl[(i, h)] * q_c[(i, h)] for i, h in pairs}
    o = {ih: o[ih] * lax.rsqrt(_mean(o[ih] * o[ih]) + EPS) * p["ggla"] for ih in pairs}
    gate = g * _sigmoid(g)
    out_a = {(i, h): o[(i, h)] * gate[rs[i], s12[h]] for i, h in pairs}
    den = {(i, h): jnp.sum(s[(i, h)], axis=1, keepdims=True)
           + scl[(i, h)] * jnp.sum(qm[h][rs[i]] * n_in[(i, h)], axis=1, keepdims=True) for i, h in pairs}
    den = {ih: jnp.maximum(jnp.abs(den[ih]), jnp.exp(-m_t[ih])) for ih in pairs}
    open_gate = _sigmoid(opre)
    hc = {(i, h): num[(i, h)] / den[(i, h)] * open_gate[rs[i], s12[h]] for i, h in pairs}
    d0 = {ih: hc[ih] - _mean(hc[ih]) for ih in pairs}
    y = {ih: d0[ih] * lax.rsqrt(_mean(d0[ih] * d0[ih]) + EPS) for ih in pairs}
    skipped = p["skip"] * xc
    out_b = {(i, h): y[(i, h)] * p["gml"][:, s12[h]] + skipped[rs[i], s12[h]] for i, h in pairs}
    ab = jnp.concatenate([jnp.concatenate([out_a[(i, h)] for h in hs] + [out_b[(i, h)] for h in hs], axis=1) for i in cs],
                         axis=0)
    new = {"S": s_new, "C": [c_in[(n_ch, h)] for h in hs], "n": [n_in[(n_ch, h)] for h in hs],
           "m": [jnp.broadcast_to(m_in[(n_ch, h)], (1, ML_DH)) for h in hs]}
    return ab, new


_P_NAMES = ("wau", "bau", "ggla", "cw", "cb", "wq", "wk", "wv", "wif", "bif", "skip", "gml")
_P_SHAPES = {
    "wau": (128, 256), "bau": (1, 256), "ggla": (1, 128), "cw": (4, 512), "cb": (1, 512),
    "wq": (512, 128), "wk": (512, 128), "wv": (512, 128),
    "wif": (1536, 128), "bif": (1, 128), "skip": (1, 512), "gml": (1, 512),
}
_P_BLOCKDIAG = ("wq", "wk", "wv")
_S_NAMES = ("S", "C", "n", "m")
_S_SHAPES = {"S": (HEADS, GLA_DV, GLA_DK), "C": (HEADS, ML_DH, ML_DH), "n": (HEADS, 1, ML_DH), "m": (HEADS, 1, ML_DH)}


def _per_head(ref):
    return [ref[h] for h in range(HEADS)]


def _block_mask():
    r = lax.broadcasted_iota(jnp.int32, (128, 128), 0)
    c = lax.broadcasted_iota(jnp.int32, (128, 128), 1)
    same_block = (r >> 2) == (c >> 2)
    spread = jnp.logical_and(r < 4, (c & 3) == r)
    return same_block.astype(F32), spread.astype(F32)


def _expand_blockdiag(w_ref, dense_ref):
    same_block, spread = _block_mask()
    for h in range(HEADS):
        tiled = _pmm_nn(w_ref[h * 128:(h + 1) * 128, :], spread)
        dense_ref[h] = tiled * same_block


def _collect_blockdiag(ddense_ref, dw_ref):
    same_block, spread = _block_mask()
    for h in range(HEADS):
        dw_ref[h * 128:(h + 1) * 128, :] = lax.dot_general(
            ddense_ref[h] * same_block, spread, (((1,), (1,)), ((), ())), precision=lax.Precision.HIGHEST,
            preferred_element_type=F32)


def _const_spec(shape):
    zeros = (0,) * len(shape)
    return pl.BlockSpec(shape, lambda i: zeros)


def _split(refs, *counts):
    out, at = [], 0
    for c in counts:
        out.append(refs[at:at + c])
        at += c
    assert at == len(refs)
    return out


def _ride(rider, phases, cond, ins, outs, sems):
    if rider is None or not any(hasattr(rider, phase) for phase in phases):
        return
    lands, (send_sems, recv_sems, flush_sems) = sems[:-3], sems[-3:]

    @pl.when(cond)
    def _():
        for phase in phases:
            if phase == "last" and hasattr(rider, "late"):
                rider.late(ins, lands, send_sems, recv_sems)
                rider.flush("late", lands, outs, flush_sems)
            getattr(rider, phase)(ins, lands, send_sems, recv_sems)
            if hasattr(rider, "flush"):
                rider.flush(phase, lands, outs, flush_sems)
        if "last" in phases and not hasattr(rider, "flush"):
            flush = [pltpu.make_async_copy(lands[k], outs[k], flush_sems.at[k]) for k in range(len(outs))]
            for cp in flush:
                cp.start()
            for cp in flush:
                cp.wait()


def _middle_step(rider, n_steps):
    return min(n_steps - 2, int(getattr(rider, "middle_at", 1.0) * n_steps))


def _rider_specs(rider, rider_ins):
    if rider is None:
        return [], [], [], []
    scratch = [pltpu.VMEM(s.shape, s.dtype) for s in list(rider.out_shape) + list(getattr(rider, "work_shape", ()))]
    scratch += [pltpu.SemaphoreType.DMA((rider.n_sems,)), pltpu.SemaphoreType.DMA((rider.n_sems,)),
                pltpu.SemaphoreType.DMA((getattr(rider, "n_flush", len(rider.out_shape)),))]
    in_space = getattr(rider, "in_space", VMEM_WHOLE)
    in_specs = list(in_space) if isinstance(in_space, (list, tuple)) else [in_space] * len(rider_ins)
    return in_specs, [ANY] * len(rider.out_shape), list(rider.out_shape), scratch


def _merge_tile(ab, gab_ref, x_ref, wpa_ref, wpb_ref, wo_ref, g_ref, x1_ref, mix_ref, mg_ref):
    a = ab[:, 0:512]
    b = ab[:, 512:1024]
    for j in range(N_CHIP):
        blk = slice(j * 256, (j + 1) * 256)
        ya = jnp.dot(a, wpa_ref[j], preferred_element_type=F32)
        yb = jnp.dot(b, wpb_ref[j], preferred_element_type=F32)
        sa = _sigmoid(gab_ref[:, j * 256:(j + 1) * 256])
        sb = _sigmoid(gab_ref[:, 1024 + j * 256:1024 + (j + 1) * 256])
        mg_ref[:, blk] = (sa * ya + sb * yb).astype(BF16)
    mix = jnp.dot(mg_ref[...], wo_ref[...], preferred_element_type=F32)
    mix_ref[...] = mix
    mn, _ = _rms_fwd(mix)
    x1_ref[...] = x_ref[...] + mn * g_ref[...]


def _mixer_fwd(pm, p, gab, x, w_pa4, w_pb4, w_o, g_post, rider=None, rider_ins=()):
    n_p = len(_P_NAMES)
    r_in, r_out_specs, r_out_shape, r_sems = _rider_specs(rider, rider_ins)

    def body(*refs):
        ((pm_ref, xprev_ref), p_list, merge_in, ride_in, (ab_ref,), merge_out, so_refs, ride_out, sc_refs, dense_list,
         sems) = _split(refs, 2, n_p, 6, len(r_in), 1, 3, 4, len(r_out_specs), 4, 3, len(r_sems))
        p_refs = dict(zip(_P_NAMES, p_list))
        dense = dict(zip(_P_BLOCKDIAG, dense_list))
        n = pl.program_id(0)
        _ride(rider, ("first",), n == 0, ride_in, ride_out, sems)

        @pl.when(n == 0)
        def _():
            for r in sc_refs:
                r[...] = jnp.zeros_like(r)
            for nm in _P_BLOCKDIAG:
                _expand_blockdiag(p_refs[nm], dense[nm])

        st = {name: _per_head(r) for name, r in zip(_S_NAMES, sc_refs)}
        pv = {nm: (_per_head(dense[nm]) if nm in _P_BLOCKDIAG else p_refs[nm][...]) for nm in _P_NAMES}
        for name, r in zip(_S_NAMES, so_refs):
            for h in range(HEADS):
                r[0, h] = st[name][h]
        xprev8 = jnp.where(n > 0, xprev_ref[CHUNK - 8:CHUNK, :], 0.0)
        ab, st = _mixer_chunk(_PLAIN_OPS, pv, st, pm_ref[...], xprev8)
        ab = ab.astype(BF16)
        ab_ref[...] = ab
        for name, r in zip(_S_NAMES, sc_refs):
            for h in range(HEADS):
                r[h] = st[name][h]
        _merge_tile(ab, *merge_in, *merge_out)
        _ride(rider, ("middle",), n == _middle_step(rider, N_SWEEP), ride_in, ride_out, sems)
        _ride(rider, ("last",), n == N_SWEEP - 1, ride_in, ride_out, sems)

    rows = lambda width: pl.BlockSpec((SWEEP * CHUNK, width), lambda i: (i, 0))
    in_specs = [rows(PM_W), pl.BlockSpec((CHUNK, 512), lambda i: (jnp.maximum(SWEEP * i - 1, 0), PM_XM // 512))]
    in_specs += [_const_spec(_P_SHAPES[nm]) for nm in _P_NAMES]
    in_specs += [rows(GAB_W), rows(D_MODEL), _once((N_CHIP, 512, 256)), _once((N_CHIP, 512, 256)), _once((D_MODEL, D_MODEL)),
                 _once((1, D_MODEL))] + r_in
    out_specs = [rows(1024), rows(D_MODEL), rows(D_MODEL), rows(D_MODEL)]
    out_shape = [jax.ShapeDtypeStruct((SEQ, 1024), BF16), jax.ShapeDtypeStruct((SEQ, D_MODEL), F32),
                 jax.ShapeDtypeStruct((SEQ, D_MODEL), F32), jax.ShapeDtypeStruct((SEQ, D_MODEL), BF16)]
    for nm in _S_NAMES:
        shp = _S_SHAPES[nm]
        out_specs.append(pl.BlockSpec((1,) + shp, lambda i: (i, 0, 0, 0)))
        out_shape.append(jax.ShapeDtypeStruct((N_SWEEP,) + shp, F32))
    return pl.pallas_call(
        body, grid=(N_SWEEP,), in_specs=in_specs, out_specs=out_specs + r_out_specs, out_shape=out_shape + r_out_shape,
        scratch_shapes=[pltpu.VMEM(_S_SHAPES[nm], F32) for nm in _S_NAMES]
        + [pltpu.VMEM((HEADS, 128, 128), F32) for _ in _P_BLOCKDIAG] + r_sems,
        compiler_params=_params(("arbitrary",)), name="mixer_fwd",
    )(pm, pm, *[p[nm] for nm in _P_NAMES], gab, x, w_pa4, w_pb4, w_o, g_post, *rider_ins)


def _mixer_bwd(pm, dab, states, p, rider=None, rider_ins=()):
    n_p = len(_P_NAMES)
    r_in, r_out_specs, r_out_shape, r_sems = _rider_specs(rider, rider_ins)

    def body(*refs):
        ((pm_ref, xprev_ref, dab_ref), si_refs, p_list, ride_in, (dpm_ref,), dp_list, ride_out, ds_refs, (carry_ref,),
         dense_list, ddense_list, sems) = _split(refs, 3, 4, n_p, len(r_in), 1, n_p, len(r_out_specs), 4, 1, 3, 3, len(r_sems))
        p_refs = dict(zip(_P_NAMES, p_list))
        dp_refs = dict(zip(_P_NAMES, dp_list))
        dense = dict(zip(_P_BLOCKDIAG, dense_list))
        ddense = dict(zip(_P_BLOCKDIAG, ddense_list))
        i = pl.program_id(0)
        blk = N_SWEEP - 1 - i
        _ride(rider, ("first",), i == 0, ride_in, ride_out, sems)

        @pl.when(i == 0)
        def _():
            for r in ds_refs:
                r[...] = jnp.zeros_like(r)
            for nm in _P_NAMES:
                if nm in _P_BLOCKDIAG:
                    ddense[nm][...] = jnp.zeros_like(ddense[nm])
                    _expand_blockdiag(p_refs[nm], dense[nm])
                else:
                    dp_refs[nm][...] = jnp.zeros_like(dp_refs[nm])
            carry_ref[...] = jnp.zeros_like(carry_ref)

        pv = {nm: (_per_head(dense[nm]) if nm in _P_BLOCKDIAG else p_refs[nm][...]) for nm in _P_NAMES}
        dst = {name: _per_head(r) for name, r in zip(_S_NAMES, ds_refs)}
        st = {name: [r[0, h] for h in range(HEADS)] for name, r in zip(_S_NAMES, si_refs)}
        xprev8 = jnp.where(blk > 0, xprev_ref[CHUNK - 8:CHUNK, :], 0.0)
        _, vjp = jax.vjp(functools.partial(_mixer_chunk, _VJP_OPS), pv, st, pm_ref[...], xprev8)
        dp_sum, dst, dpm, dxprev8 = vjp((dab_ref[...], dst))
        reach = jnp.concatenate([jnp.zeros((SWEEP * CHUNK - 8, 512), F32), carry_ref[...]], axis=0)
        dpm_ref[:, 0:PM_XM] = dpm[:, 0:PM_XM].astype(BF16)
        dpm_ref[:, PM_XM:PM_XM + 512] = (dpm[:, PM_XM:PM_XM + 512] + reach).astype(BF16)
        dpm_ref[:, PM_XM + 512:PM_W] = dpm[:, PM_XM + 512:PM_W].astype(BF16)
        carry_ref[...] = dxprev8
        for name, r in zip(_S_NAMES, ds_refs):
            for h in range(HEADS):
                r[h] = dst[name][h]
        for nm in _P_NAMES:
            if nm in _P_BLOCKDIAG:
                for h in range(HEADS):
                    ddense[nm][h] += dp_sum[nm][h]
            else:
                dp_refs[nm][...] += dp_sum[nm]

        @pl.when(i == N_SWEEP - 1)
        def _():
            for nm in _P_BLOCKDIAG:
                _collect_blockdiag(ddense[nm], dp_refs[nm])

        _ride(rider, ("early",), i == 1, ride_in, ride_out, sems)
        _ride(rider, ("middle",), i == _middle_step(rider, N_SWEEP), ride_in, ride_out, sems)
        _ride(rider, ("last",), i == N_SWEEP - 1, ride_in, ride_out, sems)

    rev = lambda i: (N_SWEEP - 1 - i, 0)
    in_specs = [pl.BlockSpec((SWEEP * CHUNK, PM_W), rev),
                pl.BlockSpec((CHUNK, 512), lambda i: (jnp.maximum(SWEEP * (N_SWEEP - 1 - i) - 1, 0), PM_XM // 512)),
                pl.BlockSpec((SWEEP * CHUNK, 1024), rev)]
    for nm in _S_NAMES:
        in_specs.append(pl.BlockSpec((1,) + _S_SHAPES[nm], lambda i: (N_SWEEP - 1 - i, 0, 0, 0)))
    in_specs += [_const_spec(_P_SHAPES[nm]) for nm in _P_NAMES] + r_in
    out_specs = [pl.BlockSpec((SWEEP * CHUNK, PM_W), rev)] + [_const_spec(_P_SHAPES[nm]) for nm in _P_NAMES]
    out_shape = [jax.ShapeDtypeStruct((SEQ, PM_W), BF16)] + [jax.ShapeDtypeStruct(_P_SHAPES[nm], F32) for nm in _P_NAMES]
    res = pl.pallas_call(
        body, grid=(N_SWEEP,), in_specs=in_specs, out_specs=out_specs + r_out_specs, out_shape=out_shape + r_out_shape,
        scratch_shapes=[pltpu.VMEM(_S_SHAPES[nm], F32) for nm in _S_NAMES] + [pltpu.VMEM((8, 512), F32)]
        + [pltpu.VMEM((HEADS, 128, 128), F32) for _ in range(2 * len(_P_BLOCKDIAG))] + r_sems,
        compiler_params=_params(("arbitrary",)), name="mixer_bwd",
    )(pm, pm, dab, *states, *[p[nm] for nm in _P_NAMES], *rider_ins)
    return res[0], dict(zip(_P_NAMES, res[1:1 + n_p])), res[1 + n_p:]


def _tok(width):
    return pl.BlockSpec((TOK_TILE, width), lambda i: (i, 0))


def _once(shape):
    zeros = (0,) * len(shape)
    return pl.BlockSpec(shape, lambda i: zeros, pipeline_mode=pl.Buffered(1))


def _rms_fwd(x):
    r = lax.rsqrt(_mean(x * x) + EPS)
    return x * r, r


def _rms_bwd(dy, xn, r, g):
    gd = dy * g
    return r * (gd - xn * _mean(xn * gd))


def _tiled_call(body, in_specs, out_specs, out_shape, args, name, rider=None, rider_ins=(), scratch=()):
    r_in, r_out_specs, r_out_shape, r_scratch = _rider_specs(rider, rider_ins)
    n_in, n_out = len(in_specs), len(out_specs)

    def hosted(*refs):
        ins, ride_in, outs, ride_out, own, r_scr = _split(refs, n_in, len(r_in), n_out, len(r_out_specs), len(scratch),
                                                          len(r_scratch))
        i = pl.program_id(0)
        _ride(rider, ("first",), i == 0, ride_in, ride_out, r_scr)
        body(*ins, *outs, *own)
        _ride(rider, ("early",), i == 1, ride_in, ride_out, r_scr)
        _ride(rider, ("middle",), i == _middle_step(rider, N_TOK_TILE), ride_in, ride_out, r_scr)
        _ride(rider, ("last",), i == N_TOK_TILE - 1, ride_in, ride_out, r_scr)

    res = pl.pallas_call(
        hosted, grid=(N_TOK_TILE,), in_specs=list(in_specs) + r_in, out_specs=list(out_specs) + r_out_specs,
        out_shape=list(out_shape) + r_out_shape, scratch_shapes=list(scratch) + r_scratch,
        compiler_params=_params(("arbitrary",)), name=name,
    )(*args, *rider_ins)
    return res[:n_out], res[n_out:]


def _join_rows(w4_ref, wt_ref):
    @pl.when(pl.program_id(0) == 0)
    def _():
        for j in range(N_CHIP):
            wt_ref[j * IN_SHARD:(j + 1) * IN_SHARD, :] = w4_ref[j]


def _joined_scratch():
    return [pltpu.VMEM((D_IN, D_MODEL), BF16)]


def _in_proj(x, g_pre, w4_in, rider=None, rider_ins=()):
    def body(x_ref, g_ref, w4_ref, pm_ref, gab_ref, h_ref, wt_ref):
        _join_rows(w4_ref, wt_ref)
        xn, _ = _rms_fwd(x_ref[...])
        h = (xn * g_ref[...]).astype(BF16)
        h_ref[...] = h
        pm_ref[:, 0:PM_XM] = _nt(h, wt_ref[0:IN_ALOW, :])
        pm_ref[:, PM_XM:PM_AL] = _nt(h, wt_ref[IN_XM:IN_GATES, :])
        pm_ref[:, PM_AL:PM_W] = _nt(h, wt_ref[IN_ALOW:IN_ALOW + 128, :])
        gab_ref[...] = _nt(h, wt_ref[IN_GATES:D_IN, :])

    return _tiled_call(
        body, [_tok(D_MODEL), _once((1, D_MODEL)), _once((N_CHIP, IN_SHARD, D_MODEL))],
        [_tok(PM_W), _tok(GAB_W), _tok(D_MODEL)],
        [jax.ShapeDtypeStruct((SEQ, PM_W), F32), jax.ShapeDtypeStruct((SEQ, GAB_W), F32),
         jax.ShapeDtypeStruct((SEQ, D_MODEL), BF16)], (x, g_pre, w4_in), "in_proj", rider, rider_ins, _joined_scratch())


def _mlp(x1, target, g_pre, g_post, w_up4, w_down_a4, w_down_b4):
    def body(x1_ref, t_ref, gpre_ref, gpost_ref, wup_ref, wda_ref, wdb_ref,
             dx1_ref, u_ref, dd_ref, h2_ref, dpre_ref, dgpost_ref, dgpre_ref, loss_ref):
        @pl.when(pl.program_id(0) == 0)
        def _():
            dgpost_ref[...] = jnp.zeros_like(dgpost_ref)
            dgpre_ref[...] = jnp.zeros_like(dgpre_ref)
            loss_ref[...] = jnp.zeros_like(loss_ref)

        x1 = x1_ref[...]
        gpre = gpre_ref[...]
        gpost = gpost_ref[...]
        xn2, r2 = _rms_fwd(x1)
        h2 = (xn2 * gpre).astype(BF16)
        h2_ref[...] = h2
        rl = []
        d = jnp.zeros((TOK_TILE, D_MODEL), F32)
        for j in range(N_CHIP):
            blk = slice(j * 1024, (j + 1) * 1024)
            r = jnp.maximum(jnp.dot(h2, wup_ref[j], preferred_element_type=F32), 0.0)
            rl.append(r)
            u = (r * r).astype(BF16)
            u_ref[:, blk] = u
            d = d + jnp.dot(u[:, 0:512], wda_ref[j], preferred_element_type=F32)
            d = d + jnp.dot(u[:, 512:1024], wdb_ref[j], preferred_element_type=F32)
        dn, r3 = _rms_fwd(d)
        diff = x1 + dn * gpost - t_ref[...]
        loss_ref[...] += jnp.sum(diff * diff, keepdims=True) * (0.5 / D_MODEL)
        dy = diff * (1.0 / D_MODEL)
        dgpost_ref[...] += jnp.sum(dy * dn, axis=0, keepdims=True)
        dd = _rms_bwd(dy, dn, r3, gpost).astype(BF16)
        dd_ref[...] = dd
        dh2 = jnp.zeros((TOK_TILE, D_MODEL), F32)
        for j in range(N_CHIP):
            blk = slice(j * 1024, (j + 1) * 1024)
            du = jnp.concatenate([_nt(dd, wda_ref[j]), _nt(dd, wdb_ref[j])], axis=1)
            dpre = (du * (2.0 * rl[j])).astype(BF16)
            dpre_ref[:, blk] = dpre
            dh2 = dh2 + _nt(dpre, wup_ref[j])
        dgpre_ref[...] += jnp.sum(dh2 * xn2, axis=0, keepdims=True)
        dx1_ref[...] = dy + _rms_bwd(dh2, xn2, r2, gpre)

    acc = pl.BlockSpec((1, D_MODEL), lambda i: (0, 0))
    return pl.pallas_call(
        body, grid=(N_TOK_TILE,),
        in_specs=[_tok(D_MODEL), _tok(D_MODEL), _once((1, D_MODEL)), _once((1, D_MODEL)),
                  _once((N_CHIP, D_MODEL, 1024)), _once((N_CHIP, 512, D_MODEL)), _once((N_CHIP, 512, D_MODEL))],
        out_specs=[_tok(D_MODEL), _tok(D_FF), _tok(D_MODEL), _tok(D_MODEL), _tok(D_FF), acc, acc,
                   pl.BlockSpec((1, 128), lambda i: (0, 0))],
        out_shape=[jax.ShapeDtypeStruct((SEQ, D_MODEL), F32), jax.ShapeDtypeStruct((SEQ, D_FF), BF16),
                   jax.ShapeDtypeStruct((SEQ, D_MODEL), BF16), jax.ShapeDtypeStruct((SEQ, D_MODEL), BF16),
                   jax.ShapeDtypeStruct((SEQ, D_FF), BF16), jax.ShapeDtypeStruct((1, D_MODEL), F32),
                   jax.ShapeDtypeStruct((1, D_MODEL), F32), jax.ShapeDtypeStruct((1, 128), F32)],
        compiler_params=_params(("arbitrary",)), name="mlp_fwd_bwd",
    )(x1, target, g_pre, g_post, w_up4, w_down_a4, w_down_b4)


def _merge_bwd(dx1, mix, ab, gab, merged, w_pa4, w_pb4, w_o, g_post):
    def body(dx1_ref, mix_ref, ab_ref, gab_ref, mg_ref, wpa_ref, wpb_ref, wo_ref, g_ref,
             dgab_ref, dab_ref, dg_ref, dwpa_ref, dwpb_ref, dwo_ref, acc_pa, acc_pb, acc_o):
        @pl.when(pl.program_id(0) == 0)
        def _():
            dg_ref[...] = jnp.zeros_like(dg_ref)
            acc_pa[...] = jnp.zeros_like(acc_pa)
            acc_pb[...] = jnp.zeros_like(acc_pb)
            acc_o[...] = jnp.zeros_like(acc_o)

        dx1 = dx1_ref[...]
        mn, r = _rms_fwd(mix_ref[...])
        dg_ref[...] += jnp.sum(dx1 * mn, axis=0, keepdims=True)
        dmix = _rms_bwd(dx1, mn, r, g_ref[...]).astype(BF16)
        acc_o[...] += _tn(mg_ref[...], dmix)
        dmerged = _nt(dmix, wo_ref[...])
        a = ab_ref[:, 0:512]
        b = ab_ref[:, 512:1024]
        da = jnp.zeros((TOK_TILE, 512), F32)
        db = jnp.zeros((TOK_TILE, 512), F32)
        dyas, dybs = [], []
        for j in range(N_CHIP):
            blk = slice(j * 256, (j + 1) * 256)
            blk_b = slice(1024 + j * 256, 1024 + (j + 1) * 256)
            dm = dmerged[:, blk]
            ya = jnp.dot(a, wpa_ref[j], preferred_element_type=F32)
            yb = jnp.dot(b, wpb_ref[j], preferred_element_type=F32)
            sa = _sigmoid(gab_ref[:, blk])
            sb = _sigmoid(gab_ref[:, blk_b])
            dya = (dm * sa).astype(BF16)
            dyb = (dm * sb).astype(BF16)
            dyas.append(dya)
            dybs.append(dyb)
            dgab_ref[:, blk] = (dm * ya * sa * (1.0 - sa)).astype(BF16)
            dgab_ref[:, blk_b] = (dm * yb * sb * (1.0 - sb)).astype(BF16)
            da = da + _nt(dya, wpa_ref[j])
            db = db + _nt(dyb, wpb_ref[j])
        dab_ref[:, 0:512] = da
        dab_ref[:, 512:1024] = db
        acc_pa[...] += _tn(a, jnp.concatenate(dyas, axis=1))
        acc_pb[...] += _tn(b, jnp.concatenate(dybs, axis=1))

        @pl.when(pl.program_id(0) == N_TOK_TILE - 1)
        def _():
            dwo_ref[...] = acc_o[...].astype(BF16)
            for j in range(N_CHIP):
                dwpa_ref[j] = acc_pa[:, j * 256:(j + 1) * 256].astype(BF16)
                dwpb_ref[j] = acc_pb[:, j * 256:(j + 1) * 256].astype(BF16)

    whole = lambda shape: pl.BlockSpec(shape, lambda i: (0,) * len(shape))
    return pl.pallas_call(
        body, grid=(N_TOK_TILE,),
        in_specs=[_tok(D_MODEL), _tok(D_MODEL), _tok(1024), _tok(GAB_W), _tok(D_MODEL), _once((N_CHIP, 512, 256)),
                  _once((N_CHIP, 512, 256)), _once((D_MODEL, D_MODEL)), _once((1, D_MODEL))],
        out_specs=[_tok(GAB_W), _tok(1024), whole((1, D_MODEL)), whole((N_CHIP, 512, 256)), whole((N_CHIP, 512, 256)),
                   whole((D_MODEL, D_MODEL))],
        out_shape=[jax.ShapeDtypeStruct((SEQ, GAB_W), BF16), jax.ShapeDtypeStruct((SEQ, 1024), F32),
                   jax.ShapeDtypeStruct((1, D_MODEL), F32), jax.ShapeDtypeStruct((N_CHIP, 512, 256), BF16),
                   jax.ShapeDtypeStruct((N_CHIP, 512, 256), BF16), jax.ShapeDtypeStruct((D_MODEL, D_MODEL), BF16)],
        scratch_shapes=[pltpu.VMEM((512, D_MODEL), F32), pltpu.VMEM((512, D_MODEL), F32),
                        pltpu.VMEM((D_MODEL, D_MODEL), F32)],
        compiler_params=_params(("arbitrary",)), name="merge_bwd",
    )(dx1, mix, ab, gab, merged, w_pa4, w_pb4, w_o, g_post)


def _in_proj_bwd(dpm, dgab, x, dx1, g_pre, w4_in, rider=None, rider_ins=()):
    def body(dpm_ref, dgab_ref, x_ref, dx1_ref, g_ref, w4_ref, dx_ref, dg_ref, wt_ref):
        _join_rows(w4_ref, wt_ref)

        @pl.when(pl.program_id(0) == 0)
        def _():
            dg_ref[...] = jnp.zeros_like(dg_ref)

        dh = jnp.dot(dpm_ref[:, 0:PM_XM], wt_ref[0:IN_ALOW, :], preferred_element_type=F32)
        dh = dh + jnp.dot(dpm_ref[:, PM_XM:PM_AL], wt_ref[IN_XM:IN_GATES, :], preferred_element_type=F32)
        dh = dh + jnp.dot(dpm_ref[:, PM_AL:PM_W], wt_ref[IN_ALOW:IN_ALOW + 128, :], preferred_element_type=F32)
        dh = dh + jnp.dot(dgab_ref[...], wt_ref[IN_GATES:D_IN, :], preferred_element_type=F32)
        xn, r = _rms_fwd(x_ref[...])
        dg_ref[...] += jnp.sum(dh * xn, axis=0, keepdims=True)
        dx_ref[...] = dx1_ref[...] + _rms_bwd(dh, xn, r, g_ref[...])

    return _tiled_call(
        body, [_tok(PM_W), _tok(GAB_W), _tok(D_MODEL), _tok(D_MODEL), _once((1, D_MODEL)),
               _once((N_CHIP, IN_SHARD, D_MODEL))],
        [_tok(D_MODEL), pl.BlockSpec((1, D_MODEL), lambda i: (0, 0))],
        [jax.ShapeDtypeStruct((SEQ, D_MODEL), F32), jax.ShapeDtypeStruct((1, D_MODEL), F32)],
        (dpm, dgab, x, dx1, g_pre, w4_in), "in_proj_bwd", rider, rider_ins, _joined_scratch())


def _dw_in(dpm, dgab, h):
    n_pm = PM_AL // 512
    n_blk = n_pm + GAB_W // 512

    def place(o_ref, rows, lo, hi):
        for j in range(N_CHIP):
            a, b = max(lo, j * IN_SHARD), min(hi, (j + 1) * IN_SHARD)
            if a < b:
                o_ref[j, a - j * IN_SHARD:b - j * IN_SHARD, :] = rows(a - lo, b - lo)

    def body(dpm_ref, dgab_ref, dal_ref, h_ref, o_ref, blk_ref):
        i = pl.program_id(0)

        @pl.when(i < n_pm)
        def _():
            blk_ref[...] = _tn(dpm_ref[...], h_ref[...]).astype(BF16)

        @pl.when(i >= n_pm)
        def _():
            blk_ref[...] = _tn(dgab_ref[...], h_ref[...]).astype(BF16)

        for k in range(n_blk):
            off = k * 512 + (IN_XM - IN_ALOW) * (k >= IN_ALOW // 512)

            @pl.when(i == k)
            def _():
                place(o_ref, lambda a, b: blk_ref[a:b, :], off, off + 512)

        @pl.when(i == 0)
        def _():
            a_low = _tn(dal_ref[...], h_ref[...])[0:IN_XM - IN_ALOW].astype(BF16)
            place(o_ref, lambda a, b: a_low[a:b], IN_ALOW, IN_XM)

    return pl.pallas_call(
        body, grid=(n_blk,),
        in_specs=[pl.BlockSpec((SEQ, 512), lambda i: (0, jnp.minimum(i, n_pm - 1))),
                  pl.BlockSpec((SEQ, 512), lambda i: (0, jnp.maximum(i - n_pm, 0))),
                  pl.BlockSpec((SEQ, 128), lambda i: (0, PM_AL // 128)),
                  _once((SEQ, D_MODEL))],
        out_specs=pl.BlockSpec((N_CHIP, IN_SHARD, D_MODEL), lambda i: (0, 0, 0)),
        out_shape=jax.ShapeDtypeStruct((N_CHIP, IN_SHARD, D_MODEL), BF16),
        scratch_shapes=[pltpu.VMEM((512, D_MODEL), BF16)],
        compiler_params=_params(("arbitrary",)), name="dw_in",
    )(dpm, dgab, dpm, h)


def _tn_matmul(a, b, name, shards=1, tm=1024, rider=None, rider_ins=()):
    m, n = a.shape[1], b.shape[1]
    tm = min(tm, m)
    tn = n // shards if shards > 1 else min(n, 1024)
    steps_i, steps_j = m // tm, n // tn
    r_in, r_out_specs, r_out_shape, r_scratch = _rider_specs(rider, rider_ins)

    def body(*refs):
        (a_ref, b_ref), ride_in, (o_ref,), ride_out, scratch = _split(refs, 2, len(r_in), 1, len(r_out_specs), len(r_scratch))
        step = pl.program_id(0) * steps_j + pl.program_id(1)
        _ride(rider, ("first",), step == 0, ride_in, ride_out, scratch)
        o_ref[...] = _tn(a_ref[...], b_ref[...]).astype(BF16)
        _ride(rider, ("middle", "last"), step == steps_i * steps_j - 1, ride_in, ride_out, scratch)

    if shards > 1:
        out_spec = pl.BlockSpec((None, tm, tn), lambda i, j: (j, i, 0))
        out_shape = jax.ShapeDtypeStruct((shards, m, tn), BF16)
    else:
        out_spec = pl.BlockSpec((tm, tn), lambda i, j: (i, j))
        out_shape = jax.ShapeDtypeStruct((m, n), BF16)
    res = pl.pallas_call(
        body, grid=(steps_i, steps_j),
        in_specs=[pl.BlockSpec((SEQ, tm), lambda i, j: (0, i)), pl.BlockSpec((SEQ, tn), lambda i, j: (0, j))] + r_in,
        out_specs=[out_spec] + r_out_specs, out_shape=[out_shape] + r_out_shape, scratch_shapes=r_scratch,
        compiler_params=_params(("arbitrary", "arbitrary")), name=name,
    )(a, b, *rider_ins)
    return res[0] if rider is None else (res[0], res[1:])


MESH = pl.DeviceIdType.MESH
ANY = pl.BlockSpec(memory_space=pl.ANY)
VMEM_WHOLE = pl.BlockSpec(memory_space=pltpu.VMEM)

_BIG = ("w_in", "w_pa", "w_pb", "w_o", "w_up", "w_down")
_BIG_SHARD = {"w_in": (IN_SHARD, D_MODEL), "w_pa": (512, 256), "w_pb": (512, 256), "w_o": (256, D_MODEL),
              "w_up": (D_MODEL, 1024), "w_down": (1024, D_MODEL),
              "w_down_a": (512, D_MODEL), "w_down_b": (512, D_MODEL)}
_BIG_SPLIT = {"w_in": 1, "w_pa": 0, "w_pb": 0, "w_o": 0, "w_up": 0, "w_down": 0, "w_down_a": 0, "w_down_b": 0}


def _half(ref, e, name, lead=0, part=None):
    axis = _BIG_SPLIT[name]
    size = _BIG_SHARD[name][axis] // 2
    start = e * size
    if part is not None:
        size //= 2
        start = start + part * size
    start = pl.multiple_of(start, 128 if axis == 1 else 16)
    idx = [pl.ds(0, ref.shape[a]) for a in range(lead)]
    idx += [pl.ds(start, size), pl.ds(0, _BIG_SHARD[name][1])] if axis == 0 else [pl.ds(0, _BIG_SHARD[name][0]), pl.ds(start, size)]
    return ref.at[tuple(idx)]


def _half_shape(name):
    r, c = _BIG_SHARD[name]
    return (r // 2, c) if _BIG_SPLIT[name] == 0 else (r, c // 2)


def _remote(src, dst, send_sems, recv_sems, k, to):
    return pltpu.make_async_remote_copy(src_ref=src, dst_ref=dst, send_sem=send_sems.at[k], recv_sem=recv_sems.at[k],
                                        device_id=to, device_id_type=MESH)


def _mesh_place():
    x, y, c = lax.axis_index("x"), lax.axis_index("y"), lax.axis_index("c")
    return x, y, c, [(1 - x, y), (x, 1 - y), (1 - x, 1 - y)]


class _Gather:
    def __init__(self, names, small=(), middle_at=0.5):
        self.middle_at = middle_at
        self.names = tuple(names)
        self.nb = len(self.names)
        self.n = self.nb + len(small)
        self.n_sems = 8 * self.nb + 3 * len(small)
        self.n_flush = 6 * self.nb + len(small)
        self.out_shape = [jax.ShapeDtypeStruct((N_CHIP,) + _BIG_SHARD[nm], BF16) for nm in self.names]
        self.out_shape += [jax.ShapeDtypeStruct((N_CHIP,) + s.shape, s.dtype) for s in small]
        self.in_space = [self._in_spec(nm) for nm in self.names] + [VMEM_WHOLE] * len(small)

    @staticmethod
    def _in_spec(name):
        if name in ("w_down_a", "w_down_b"):
            half = 0 if name == "w_down_a" else 1
            return pl.BlockSpec(_BIG_SHARD[name], lambda *_: (half, 0), pipeline_mode=pl.Buffered(1))
        return VMEM_WHOLE

    def _copies(self, ins, outs, ss, rs, k):
        x, y, c, _ = _mesh_place()
        name = self.names[k]
        me, xn, yn, dg = 2 * x + y, 2 * (1 - x) + y, 2 * x + (1 - y), 2 * (1 - x) + (1 - y)
        to_x, to_y, sibling = (1 - x, y, c), (x, 1 - y, c), (x, y, 1 - c)

        def region(slot, e, part=None):
            return _half(outs[k].at[slot], e, name, part=part)

        def copy(pair, src, dst, to):
            return _remote(src, dst, ss, rs, 8 * k + pair, to)

        mine = region(me, c)
        sent = [copy(0, mine, mine, to_x), copy(1, mine, mine, to_y),
                copy(2, region(xn, c, 0), region(xn, c, 0), to_y), copy(3, region(yn, c, 1), region(yn, c, 1), to_x),
                copy(4, region(xn, c), region(xn, c), sibling), copy(5, region(yn, c), region(yn, c), sibling),
                copy(6, region(dg, c, 0), region(dg, c, 0), sibling), copy(7, region(dg, c, 1), region(dg, c, 1), sibling)]
        landing = [region(xn, c), region(yn, c), region(dg, c, 0), region(dg, c, 1),
                   region(xn, 1 - c), region(yn, 1 - c), region(dg, 1 - c, 0), region(dg, 1 - c, 1)]
        received = [copy(pair, dst, dst, sibling) for pair, dst in enumerate(landing)]
        return sent, received

    def _small(self, ins, outs, ss, rs, k, j, peer, slot, c):
        return _remote(ins[k], outs[k].at[slot], ss, rs, 8 * self.nb + 3 * (k - self.nb) + j, (*peer, c))

    def flush(self, phase, lands, outs, fs):
        x, y, c, _ = _mesh_place()
        me, xn, yn, dg = 2 * x + y, 2 * (1 - x) + y, 2 * x + (1 - y), 2 * (1 - x) + (1 - y)

        def pieces(k):
            name = self.names[k]
            spots = [lambda r: r.at[me], lambda r: _half(r.at[xn], c, name), lambda r: _half(r.at[yn], c, name),
                     lambda r: _half(r.at[xn], 1 - c, name), lambda r: _half(r.at[yn], 1 - c, name), lambda r: r.at[dg]]
            return [pltpu.make_async_copy(spot(lands[k]), spot(outs[k]), fs.at[6 * k + t]) for t, spot in enumerate(spots)]

        ready = {"first": (0,), "middle": (1, 2), "late": (3, 4), "last": (5,)}[phase]
        for k in range(self.nb):
            cps = pieces(k)
            for t in ready:
                cps[t].start()
        if phase == "last":
            small = [pltpu.make_async_copy(lands[k], outs[k], fs.at[6 * self.nb + k - self.nb]) for k in range(self.nb, self.n)]
            for cp in small:
                cp.start()
            for k in range(self.nb):
                for cp in pieces(k):
                    cp.wait()
            for cp in small:
                cp.wait()

    def first(self, ins, outs, ss, rs):
        x, y, c, peers = _mesh_place()
        me = 2 * x + y
        for k in range(self.nb):
            outs[k][me] = ins[k][...].astype(BF16)
            sent, _ = self._copies(ins, outs, ss, rs, k)
            sent[0].start()
            sent[1].start()
        for k in range(self.nb, self.n):
            for j, peer in enumerate(peers):
                self._small(ins, outs, ss, rs, k, j, peer, me, c).start()
            outs[k][me] = ins[k][...]

    def middle(self, ins, outs, ss, rs):
        for k in range(self.nb):
            sent, received = self._copies(ins, outs, ss, rs, k)
            for pair in (0, 1):
                received[pair].wait_recv()
                sent[2 + pair].start()
                sent[4 + pair].start()

    def late(self, ins, outs, ss, rs):
        for k in range(self.nb):
            _, received = self._copies(ins, outs, ss, rs, k)
            for pair in (4, 5):
                received[pair].wait_recv()

    def last(self, ins, outs, ss, rs):
        x, y, c, peers = _mesh_place()
        for k in range(self.nb):
            sent, received = self._copies(ins, outs, ss, rs, k)
            for pair in (2, 3):
                received[pair].wait_recv()
                sent[4 + pair].start()
        for k in range(self.nb):
            sent, received = self._copies(ins, outs, ss, rs, k)
            for pair in (6, 7):
                received[pair].wait_recv()
            for cp in sent:
                cp.wait_send()
        for k in range(self.nb, self.n):
            for j, (px, py) in enumerate(peers):
                self._small(ins, outs, ss, rs, k, j, (px, py), 2 * px + py, c).wait_recv()
                self._small(ins, outs, ss, rs, k, j, (px, py), 2 * x + y, c).wait_send()


def _run_alone(rider, ins, name):
    r_in, r_out_specs, r_out_shape, r_scratch = _rider_specs(rider, ins)

    def body(*refs):
        ride_in, ride_out, scratch = _split(refs, len(r_in), len(r_out_specs), len(r_scratch))
        _ride(rider, ("first", "middle", "last"), pl.program_id(0) == 0, ride_in, ride_out, scratch)

    return pl.pallas_call(
        body, grid=(1,), in_specs=r_in, out_specs=r_out_specs, out_shape=r_out_shape, scratch_shapes=r_scratch,
        compiler_params=_params(("arbitrary",)), name=name,
    )(*ins)


class _Presum:
    in_space = ANY

    def __init__(self, names, base=0):
        self.names = tuple(names)
        self.n = len(self.names)
        self.base = base
        self.n_sems = 3 * self.n
        self.out_shape = [jax.ShapeDtypeStruct((N_CHIP,) + _half_shape(nm), BF16) for nm in self.names]
        self.work_shape = self.out_shape + self.out_shape

    def _stage(self, ins, bufs, ss, k, e, which):
        n = self.n
        return pltpu.make_async_copy(_half(ins[k], e, self.names[k], lead=1), bufs[which * n + k],
                                     ss.at[self.base + which * n + k])

    def _give(self, bufs, ss, rs, k, sibling):
        return _remote(bufs[self.n + k], bufs[k], ss, rs, self.base + k, sibling)

    def first(self, ins, bufs, ss, rs):
        x, y, c, _ = _mesh_place()
        for k in range(self.n):
            self._stage(ins, bufs, ss, k, 1 - c, 1).start()
        for k in range(self.n):
            self._stage(ins, bufs, ss, k, c, 2).start()
        for k in range(self.n):
            self._stage(ins, bufs, ss, k, 1 - c, 1).wait()
            self._give(bufs, ss, rs, k, (x, y, 1 - c)).start()

    def middle(self, ins, bufs, ss, rs):
        pass

    def last(self, ins, bufs, ss, rs):
        x, y, c, _ = _mesh_place()
        for k in range(self.n):
            self._give(bufs, ss, rs, k, (x, y, 1 - c)).wait_recv()
            self._stage(ins, bufs, ss, k, c, 2).wait()

            @pl.loop(0, N_CHIP)
            def _(j):
                bufs[k][j] = (bufs[k][j].astype(F32) + bufs[2 * self.n + k][j].astype(F32)).astype(BF16)
        for k in range(self.n):
            self._give(bufs, ss, rs, k, (x, y, 1 - c)).wait_send()


class _ReduceRelay:
    middle_at = 0.75

    def __init__(self, names, base=0):
        self.names = tuple(names)
        self.n = len(self.names)
        self.base = base
        self.n_sems = 6 * self.n
        self.out_shape = [jax.ShapeDtypeStruct((N_CHIP,) + _half_shape(nm), BF16) for nm in self.names]
        quarter = [jax.ShapeDtypeStruct(self._part_shape(nm), BF16) for nm in self.names]
        self.work_shape = quarter + quarter

    @staticmethod
    def _part_shape(name):
        r, c = _half_shape(name)
        return (r // 2, c) if _BIG_SPLIT[name] == 0 else (r, c // 2)

    def _part(self, ref, name, p):
        r, c = self._part_shape(name)
        return ref.at[pl.ds(p * r, r), pl.ds(0, c)] if _BIG_SPLIT[name] == 0 else ref.at[pl.ds(0, r), pl.ds(p * c, c)]

    def _copies(self, ins, bufs, ss, rs, k):
        x, y, c, _ = _mesh_place()
        name, n = self.names[k], self.n
        me, xn, yn, dg = 2 * x + y, 2 * (1 - x) + y, 2 * x + (1 - y), 2 * (1 - x) + (1 - y)
        to_x, to_y = (1 - x, y, c), (x, 1 - y, c)
        mine = lambda slot, p: self._part(ins[k].at[slot], name, p)
        slot = lambda s, p: self._part(bufs[k].at[s], name, p)
        from_x, from_y = bufs[n + k], bufs[2 * n + k]

        def copy(pair, src, dst, to):
            return _remote(src, dst, ss, rs, self.base + 6 * k + pair, to)

        sent = [copy(0, mine(dg, 0), from_x, to_x), copy(1, mine(dg, 1), from_y, to_y),
                copy(2, mine(xn, 0), slot(me, 0), to_x), copy(3, mine(yn, 1), slot(me, 1), to_y),
                copy(4, from_y, slot(me, 1), to_x), copy(5, from_x, slot(me, 0), to_y)]
        landing = [from_x, from_y, slot(xn, 0), slot(yn, 1), slot(xn, 1), slot(yn, 0)]
        received = [copy(pair, dst, dst, to_x) for pair, dst in enumerate(landing)]
        return sent, received

    def first(self, ins, bufs, ss, rs):
        x, y, c, _ = _mesh_place()
        me, dg = 2 * x + y, 2 * (1 - x) + (1 - y)
        for k in range(self.n):
            sent, _ = self._copies(ins, bufs, ss, rs, k)
            for pair in range(4):
                sent[pair].start()
        for k in range(self.n):
            bufs[k][me] = ins[k][me]
            bufs[k][dg] = jnp.zeros(_half_shape(self.names[k]), BF16)

    def middle(self, ins, bufs, ss, rs):
        x, y, c, _ = _mesh_place()
        xn, yn = 2 * (1 - x) + y, 2 * x + (1 - y)
        for k in range(self.n):
            sent, received = self._copies(ins, bufs, ss, rs, k)
            name, n = self.names[k], self.n
            for pair, buf, own in ((0, bufs[n + k], self._part(ins[k].at[yn], name, 0)),
                                   (1, bufs[2 * n + k], self._part(ins[k].at[xn], name, 1))):
                received[pair].wait_recv()
                buf[...] = (buf[...].astype(F32) + own[...].astype(F32)).astype(BF16)
            sent[5].start()
            sent[4].start()

    def last(self, ins, bufs, ss, rs):
        for k in range(self.n):
            sent, received = self._copies(ins, bufs, ss, rs, k)
            for pair in range(2, 6):
                received[pair].wait_recv()
            for cp in sent:
                cp.wait_send()


class _PresumThenRelay:
    in_space = ANY
    middle_at = _ReduceRelay.middle_at

    def __init__(self, names):
        self.relay = _ReduceRelay(names)
        self.pre = _Presum(names, base=self.relay.n_sems)
        self.n_sems = self.relay.n_sems + self.pre.n_sems
        self.out_shape = self.relay.out_shape
        self.work_shape = list(self.relay.work_shape) + list(self.pre.out_shape) + list(self.pre.work_shape)
        self.n_relay = len(self.relay.out_shape) + len(self.relay.work_shape)

    def first(self, ins, bufs, ss, rs):
        self.pre.first(ins, bufs[self.n_relay:], ss, rs)

    def early(self, ins, bufs, ss, rs):
        self.pre.last(ins, bufs[self.n_relay:], ss, rs)
        self.relay.first(bufs[self.n_relay:], bufs[:self.n_relay], ss, rs)

    def middle(self, ins, bufs, ss, rs):
        self.relay.middle(bufs[self.n_relay:], bufs[:self.n_relay], ss, rs)

    def last(self, ins, bufs, ss, rs):
        self.relay.last(bufs[self.n_relay:], bufs[:self.n_relay], ss, rs)


class _SendPartials:
    def __init__(self, names, small_shape=None):
        self.n = len(names)
        self.small = small_shape is not None
        self.n_sems = 3 * self.n + 7
        self.out_shape = [jax.ShapeDtypeStruct((N_CHIP,) + _half_shape(nm), BF16) for nm in names]
        if self.small:
            self.out_shape.append(jax.ShapeDtypeStruct((N_DEV,) + small_shape, F32))

    def _piece(self, ins, outs, ss, rs, k, j, peer, src_slot, dst_slot, c):
        return _remote(ins[k].at[src_slot], outs[k].at[dst_slot], ss, rs, 3 * k + j, (*peer, c))

    def _small(self, ins, outs, ss, rs, r, other, slot):
        return _remote(ins[self.n], outs[self.n].at[slot], ss, rs, 3 * self.n + r, other)

    @staticmethod
    def _others(x, y, c):
        return [(x, y, 1 - c), (1 - x, y, c), (1 - x, y, 1 - c), (x, 1 - y, c), (x, 1 - y, 1 - c),
                (1 - x, 1 - y, c), (1 - x, 1 - y, 1 - c)]

    def first(self, ins, outs, ss, rs, only=None):
        x, y, c, peers = _mesh_place()
        me = 2 * x + y
        which = range(self.n) if only is None else only
        for k in which:
            for j, (px, py) in enumerate(peers):
                self._piece(ins, outs, ss, rs, k, j, (px, py), 2 * px + py, me, c).start()
        if self.small:
            for r, other in enumerate(self._others(x, y, c)):
                self._small(ins, outs, ss, rs, r, other, 4 * x + 2 * y + c).start()
            outs[self.n][4 * x + 2 * y + c] = ins[self.n][...]
        for k in which:
            outs[k][me] = ins[k][me]

    def middle(self, ins, outs, ss, rs):
        pass

    def last(self, ins, outs, ss, rs):
        x, y, c, peers = _mesh_place()
        me = 2 * x + y
        for k in range(self.n):
            for j, (px, py) in enumerate(peers):
                self._piece(ins, outs, ss, rs, k, j, (px, py), me, 2 * px + py, c).wait_recv()
                self._piece(ins, outs, ss, rs, k, j, (px, py), 2 * px + py, me, c).wait_send()
        if self.small:
            for r, (px, py, pc) in enumerate(self._others(x, y, c)):
                self._small(ins, outs, ss, rs, r, (px, py, pc), 4 * px + 2 * py + pc).wait_recv()
                self._small(ins, outs, ss, rs, r, (px, py, pc), 4 * x + 2 * y + c).wait_send()


class _PresumThenSend:
    def __init__(self, names):
        self.send = _SendPartials(names)
        self.pre = _Presum(names[-1:], base=self.send.n_sems)
        self.n = self.send.n
        self.n_sems = self.send.n_sems + self.pre.n_sems
        self.out_shape = self.send.out_shape
        self.work_shape = list(self.pre.out_shape) + list(self.pre.work_shape)
        self.in_space = [VMEM_WHOLE] * (self.n - 1) + [ANY]

    def _partials(self, ins, bufs):
        return list(ins[:self.n - 1]) + [bufs[self.n]]

    def first(self, ins, bufs, ss, rs):
        self.pre.first(ins[self.n - 1:], bufs[self.n:], ss, rs)
        self.send.first(ins, bufs[:self.n], ss, rs, only=range(self.n - 1))

    def early(self, ins, bufs, ss, rs):
        self.pre.last(ins[self.n - 1:], bufs[self.n:], ss, rs)
        self.send.first(self._partials(ins, bufs), bufs[:self.n], ss, rs, only=(self.n - 1,))

    def middle(self, ins, bufs, ss, rs):
        pass

    def last(self, ins, bufs, ss, rs):
        self.send.last(self._partials(ins, bufs), bufs[:self.n], ss, rs)


def _sum_swap(names, parts, small):
    n = len(parts)
    everyone = _SendPartials((), small.shape)

    def body(*refs):
        (p_hbm, (small_ref,), o_hbm, (osmall_ref,), p_refs, o_refs, (all_ref,),
         (send_sems, recv_sems, ss_small, rs_small, load_sems, leave_sems)) = _split(refs, n, 1, n, 1, n, n, 1, 6)
        x, y, c = lax.axis_index("x"), lax.axis_index("y"), lax.axis_index("c")
        loads = [pltpu.make_async_copy(p_hbm[k], p_refs[k], load_sems.at[k]) for k in range(n)]
        for cp in loads:
            cp.start()
        everyone.first([small_ref], [all_ref], ss_small, rs_small)

        def mine(k):
            part = _half(o_refs[k], c, names[k])
            return _remote(part, part, send_sems, recv_sems, k, (x, y, 1 - c))

        def leave(k, whose):
            e = c if whose == 0 else 1 - c
            return pltpu.make_async_copy(_half(o_refs[k], e, names[k]), _half(o_hbm[k], e, names[k]),
                                         leave_sems.at[2 * k + whose])

        for k in range(n):
            loads[k].wait()
            for e in range(2):
                @pl.when(c == e)
                def _():
                    g = p_refs[k][0].astype(F32)
                    for s in range(1, N_CHIP):
                        g = g + p_refs[k][s].astype(F32)
                    r, cols = _half_shape(names[k])
                    if _BIG_SPLIT[names[k]] == 0:
                        o_refs[k][e * r:(e + 1) * r, :] = g
                    else:
                        o_refs[k][:, e * cols:(e + 1) * cols] = g
            mine(k).start()
            leave(k, 0).start()
        for k in range(n):
            theirs = _half(o_refs[k], 1 - c, names[k])
            _remote(theirs, theirs, send_sems, recv_sems, k, (x, y, 1 - c)).wait_recv()
            leave(k, 1).start()
        everyone.last([small_ref], [all_ref], ss_small, rs_small)
        g = all_ref[0]
        for d in range(1, N_DEV):
            g = g + all_ref[d]
        osmall_ref[...] = g
        for k in range(n):
            mine(k).wait_send()
            leave(k, 0).wait()
            leave(k, 1).wait()

    shards = [jax.ShapeDtypeStruct(_BIG_SHARD[nm], F32) for nm in names]
    res = pl.pallas_call(
        body, in_specs=[ANY] * n + [VMEM_WHOLE], out_specs=[ANY] * n + [VMEM_WHOLE],
        out_shape=shards + [jax.ShapeDtypeStruct(small.shape, F32)],
        scratch_shapes=[pltpu.VMEM(q.shape, q.dtype) for q in parts] + [pltpu.VMEM(s.shape, s.dtype) for s in shards]
        + [pltpu.VMEM((N_DEV,) + small.shape, F32), pltpu.SemaphoreType.DMA((n,)), pltpu.SemaphoreType.DMA((n,)),
           pltpu.SemaphoreType.DMA((everyone.n_sems,)), pltpu.SemaphoreType.DMA((everyone.n_sems,)),
           pltpu.SemaphoreType.DMA((n,)), pltpu.SemaphoreType.DMA((2 * n,))],
        compiler_params=_params(), name="sum_swap",
    )(*parts, small)
    return res[:n], res[n]


def _tile(rows, cols, itemsize, budget):
    t = cols if rows % 16 else rows
    other = rows if rows % 16 else cols
    step = 256 if rows % 16 else 32
    while t % step == 0 and t * other * itemsize > budget:
        t //= 2
    return (rows, t) if rows % 16 else (t, cols)


def _adamw_math(w, g, m, v):
    m = ADAM_B1 * m + (1.0 - ADAM_B1) * g
    v = ADAM_B2 * v + (1.0 - ADAM_B2) * (g * g)
    m_hat = m / (1.0 - ADAM_B1 ** ADAM_STEP)
    v_hat = v / (1.0 - ADAM_B2 ** ADAM_STEP)
    delta = -ADAM_LR * (m_hat / (jnp.sqrt(v_hat) + ADAM_EPS) + ADAM_WD * w)
    return delta, m, v


def _adamw_big(g, w, m, v, name):
    r, c = w.shape
    tr, tc = _tile(r, c, 4, 2 * 1024 * 1024)

    def body(g_ref, w_ref, m_ref, v_ref, g_out_ref, d_ref, nm_ref, nv_ref):
        g = g_ref[...]
        g_out_ref[...] = g
        d_ref[...], nm_ref[...], nv_ref[...] = _adamw_math(w_ref[...], g, m_ref[...], v_ref[...])

    blk = pl.BlockSpec((tr, tc), lambda i, l: (i, l))
    return pl.pallas_call(
        body, grid=(r // tr, c // tc), in_specs=[blk, blk, blk, blk],
        out_specs=[blk] * 4, out_shape=[jax.ShapeDtypeStruct((r, c), F32)] * 4,
        compiler_params=_params(("arbitrary", "arbitrary")), name=name,
    )(g, w, m, v)


def _adamw_rows(g, w, m, v, name):
    r, k, lanes = w.shape
    tr = 296

    def body(g_ref, w_ref, m_ref, v_ref, g3_ref, d_ref, nm_ref, nv_ref):
        g = g_ref[...].reshape(tr, k, lanes)
        g3_ref[...] = g
        d_ref[...], nm_ref[...], nv_ref[...] = _adamw_math(w_ref[...], g, m_ref[...], v_ref[...])

    rows = pl.BlockSpec((tr, k, lanes), lambda i: (i, 0, 0))
    return pl.pallas_call(
        body, grid=(pl.cdiv(r, tr),), in_specs=[pl.BlockSpec((tr, k * lanes), lambda i: (i, 0)), rows, rows, rows],
        out_specs=[rows] * 4, out_shape=[jax.ShapeDtypeStruct((r, k, lanes), F32)] * 4,
        compiler_params=_params(("arbitrary",)), name=name,
    )(g, w, m, v)


def _adamw_small(ws, gs, ms, vs):
    n = len(ws)

    def body(*refs):
        w_refs, g_refs, m_refs, v_refs, d_refs, nm_refs, nv_refs = _split(refs, *([n] * 7))
        for k in range(n):
            d_refs[k][...], nm_refs[k][...], nv_refs[k][...] = _adamw_math(w_refs[k][...], g_refs[k][...], m_refs[k][...],
                                                                             v_refs[k][...])

    shapes = [jax.ShapeDtypeStruct(w.shape, F32) for w in ws]
    res = pl.pallas_call(body, out_shape=shapes * 3, name="adamw_small")(*ws, *gs, *ms, *vs)
    return res[:n], res[n:2 * n], res[2 * n:]


def _pack(arrs):
    flat = jnp.concatenate([a.reshape(-1) for a in arrs])
    rows = -(-flat.shape[0] // 1024) * 8
    return jnp.pad(flat, (0, rows * 128 - flat.shape[0])).reshape(rows, 128)


def _unpack(buf, shapes):
    flat = buf.reshape(-1)
    out, off = [], 0
    for s in shapes:
        size = 1
        for d in s:
            size *= d
        out.append(flat[off:off + size].reshape(s))
        off += size
    return out


def _block_rows(w):
    return jnp.pad(w.reshape(512, 4), ((0, 0), (0, 124)))


def _block_stored(dw):
    return jnp.transpose(dw[:, 0:4].reshape(128, 4, 4), (1, 2, 0)).reshape(16, 128)


def _cols(a4):
    return jnp.transpose(a4, (1, 0, 2)).reshape(a4.shape[1], -1)


_LATE = ("w_pa", "w_pb", "w_o", "w_up", "w_down")
_RIDE_IN_PROJ = ("w_pa", "w_pb", "w_o", "w_down_b")
_RIDE_MIXER = ("w_up", "w_down_a")


def _full_weights(gathered):
    joined = {"w_o": (D_MODEL, D_MODEL)}
    return {n: (a.reshape(joined[n]) if n in joined else a) for n, a in gathered.items()}


def _local_step(x, target, w, sp, late_shards=None):
    sp = {n: (a.reshape(1, -1) if a.ndim == 1 else a) for n, a in sp.items()}
    wau = jnp.pad(sp["w_a_up"], ((0, 112), (0, 0)))
    wif = jnp.pad(sp["w_if"], ((0, 0), (0, 120)))
    bif = jnp.pad(sp["b_if"], ((0, 0), (0, 120)))
    p = {"wau": wau, "bau": sp["b_a_up"], "ggla": sp["g_gla_norm"], "cw": sp["conv_w"], "cb": sp["conv_b"],
         "wq": _block_rows(sp["w_q_ml"]), "wk": _block_rows(sp["w_k_ml"]), "wv": _block_rows(sp["w_v_ml"]),
         "wif": wif, "bif": bif, "skip": sp["ml_skip"], "gml": sp["g_ml_norm"]}

    if late_shards is None:
        (pm, gab, h), _ = _in_proj(x, sp["g_pre_mix"], w["w_in"])
        ab, x1, mix, merged, *states = _mixer_fwd(pm, p, gab, x, w["w_pa"], w["w_pb"], w["w_o"], sp["g_post_mix"])
    else:
        shard = dict(zip(_LATE, late_shards))
        shard["w_down_a"] = shard["w_down_b"] = shard["w_down"]
        (pm, gab, h), got = _in_proj(x, sp["g_pre_mix"], w["w_in"], _Gather(_RIDE_IN_PROJ, middle_at=0.7),
                                     [shard[n] for n in _RIDE_IN_PROJ])
        w = dict(w, **_full_weights(dict(zip(_RIDE_IN_PROJ, got))))
        ab, x1, mix, merged, *rest = _mixer_fwd(pm, p, gab, x, w["w_pa"], w["w_pb"], w["w_o"], sp["g_post_mix"],
                                                _Gather(_RIDE_MIXER, middle_at=0.62), [shard[n] for n in _RIDE_MIXER])
        states = rest[:4]
        w.update(_full_weights(dict(zip(_RIDE_MIXER, rest[4:]))))
    dx1, u, dd, h2, dpre, dg_post_mlp, dg_pre_mlp, loss = _mlp(x1, target, sp["g_pre_mlp"], sp["g_post_mlp"],
                                                                w["w_up"], w["w_down_a"], w["w_down_b"])
    dgab, dab, dg_post_mix, dw_pa, dw_pb, dw_o = _merge_bwd(dx1, mix, ab, gab, merged, w["w_pa"], w["w_pb"], w["w_o"],
                                                            sp["g_post_mix"])
    big = {"w_pa": dw_pa, "w_pb": dw_pb, "w_o": dw_o, "w_up": _tn_matmul(h2, dpre, "dw_up", shards=N_CHIP)}
    if late_shards is None:
        big["w_down"] = _tn_matmul(u, dd, "dw_down")
        dpm, dp, _ = _mixer_bwd(pm, dab, states, p)
    else:
        pieces = lambda n: big[n].reshape((N_CHIP,) + _BIG_SHARD[n])
        big["w_down"], partial = _tn_matmul(u, dd, "dw_down", rider=_Presum(_LATE[:4]),
                                            rider_ins=[pieces(n) for n in _LATE[:4]])
        dpm, dp, parts = _mixer_bwd(pm, dab, states, p, _PresumThenSend(_LATE), list(partial) + [pieces("w_down")])
        big = dict(zip(_LATE, parts))
    big["w_in"] = _dw_in(dpm, dgab, h)
    if late_shards is None:
        (dx, dg_pre_mix), _ = _in_proj_bwd(dpm, dgab, x, dx1, sp["g_pre_mix"], w["w_in"])
    else:
        (dx, dg_pre_mix), parts = _in_proj_bwd(dpm, dgab, x, dx1, sp["g_pre_mix"], w["w_in"], _PresumThenRelay(("w_in",)),
                                               [big["w_in"]])
        big["w_in"] = parts[0]
    small = {
        "g_pre_mix": dg_pre_mix, "b_a_up": dp["bau"], "g_gla_norm": dp["ggla"], "conv_b": dp["cb"],
        "w_q_ml": _block_stored(dp["wq"]), "w_k_ml": _block_stored(dp["wk"]), "w_v_ml": _block_stored(dp["wv"]),
        "w_if": dp["wif"][:, 0:8].T,
        "b_if": dp["bif"][:, 0:8], "ml_skip": dp["skip"], "g_ml_norm": dp["gml"], "g_post_mix": dg_post_mix,
        "g_pre_mlp": dg_pre_mlp, "g_post_mlp": dg_post_mlp, "w_a_up": dp["wau"][0:16], "conv_w": dp["cw"],
        "loss": loss[:, 0:1],
    }
    return dx, big, small


_SMALL_REPL = ("g_pre_mix", "b_a_up", "g_gla_norm", "conv_b", "w_q_ml", "w_k_ml", "w_v_ml", "b_if", "ml_skip",
               "g_ml_norm", "g_post_mix", "g_pre_mlp", "g_post_mlp")
_SMALL_SHARDED = ("w_a_up", "conv_w", "w_if")
_SMALL_ORDER = _SMALL_REPL + _SMALL_SHARDED + ("loss",)
_WEIGHTS = ("g_pre_mix", "w_in", "w_a_up", "b_a_up", "g_gla_norm", "conv_w", "conv_b", "w_q_ml", "w_k_ml", "w_v_ml",
            "w_if", "b_if", "ml_skip", "g_ml_norm", "w_pa", "w_pb", "w_o", "g_post_mix", "g_pre_mlp", "w_up", "w_down",
            "g_post_mlp")


_BLOCK_WEIGHTS = ("w_q_ml", "w_k_ml", "w_v_ml")


def _stored(name, a):
    if name in _BLOCK_WEIGHTS:
        return jnp.transpose(a, (0, 2, 3, 1)).reshape(16, 128)
    if name == "w_if":
        return jnp.transpose(a, (0, 2, 1)).reshape(8, 384)
    return a


def _unstored(name, a):
    if name in _BLOCK_WEIGHTS:
        return jnp.transpose(a.reshape(1, 4, 4, 128), (0, 3, 1, 2))
    if name == "w_if":
        return jnp.transpose(a.reshape(1, 8, 384), (0, 2, 1))
    return a


def _as_shard(name, a):
    return jnp.transpose(a, (2, 0, 1)).reshape(IN_SHARD, D_MODEL // 128, 128) if name == "w_in" else a[0]


def _in_shard_bf16(w_in):
    return jnp.transpose(w_in.astype(BF16), (2, 0, 1)).reshape(IN_SHARD, D_MODEL)


def _from_shard(name, a):
    return jnp.transpose(a, (1, 2, 0)).reshape(1, D_MODEL, IN_SHARD) if name == "w_in" else a[None]


def kernel(x, g_pre_mix, w_in, w_a_up, b_a_up, g_gla_norm, conv_w, conv_b, w_q_ml, w_k_ml, w_v_ml, w_if, b_if, ml_skip, g_ml_norm, w_pa, w_pb, w_o, g_post_mix, g_pre_mlp, w_up, w_down, g_post_mlp, loss_target, m_g_pre_mix, m_w_in, m_w_a_up, m_b_a_up, m_g_gla_norm, m_conv_w, m_conv_b, m_w_q_ml, m_w_k_ml, m_w_v_ml, m_w_if, m_b_if, m_ml_skip, m_g_ml_norm, m_w_pa, m_w_pb, m_w_o, m_g_post_mix, m_g_pre_mlp, m_w_up, m_w_down, m_g_post_mlp, v_g_pre_mix, v_w_in, v_w_a_up, v_b_a_up, v_g_gla_norm, v_conv_w, v_conv_b, v_w_q_ml, v_w_k_ml, v_w_v_ml, v_w_if, v_b_if, v_ml_skip, v_g_ml_norm, v_w_pa, v_w_pb, v_w_o, v_g_post_mix, v_g_pre_mlp, v_w_up, v_w_down, v_g_post_mlp):
    args = dict(locals())
    wts = {n: _as_shard(n, args[n]) for n in _WEIGHTS}
    mom = {n: _as_shard(n, args["m_" + n]) for n in _WEIGHTS}
    var = {n: _as_shard(n, args["v_" + n]) for n in _WEIGHTS}
    chip = 2 * lax.axis_index("x") + lax.axis_index("y")

    first = ("w_in",) + _SMALL_SHARDED
    gathered = dict(zip(first, _run_alone(_Gather(("w_in",), [wts[n] for n in _SMALL_SHARDED]),
                                          [_in_shard_bf16(w_in)] + [wts[n] for n in _SMALL_SHARDED],
                                          "gather_first")))
    sp = {n: wts[n] for n in _SMALL_REPL}
    sp["w_a_up"] = _cols(gathered["w_a_up"])
    sp["conv_w"] = _cols(gathered["conv_w"])
    sp["w_if"] = gathered["w_if"].reshape(1536, 8)

    dx, big, small = _local_step(x[0], loss_target[0], _full_weights({"w_in": gathered["w_in"]}), sp,
                                 late_shards=[wts[n] for n in _LATE])

    small_shapes = [small[n].shape for n in _SMALL_ORDER]
    packed = _pack([small[n] for n in _SMALL_ORDER])
    sums, small_sum = _sum_swap(_BIG, [big[n] for n in _BIG], packed)

    grads, delta, new_m, new_v = {}, {}, {}, {}
    for n, g in zip(_BIG, sums):
        adamw = _adamw_rows if n == "w_in" else _adamw_big
        g, d, nm, nv = adamw(g, wts[n], mom[n], var[n], "adamw_" + n)
        grads[n], delta[n], new_m[n], new_v[n] = (_from_shard(n, a) for a in (g, d, nm, nv))
    summed = dict(zip(_SMALL_ORDER, _unpack(small_sum, small_shapes)))
    loss = summed["loss"].reshape(())
    summed["w_a_up"] = lax.dynamic_slice_in_dim(summed["w_a_up"], chip * 64, 64, axis=1)
    summed["conv_w"] = lax.dynamic_slice_in_dim(summed["conv_w"], chip * 128, 128, axis=1)
    summed["w_if"] = lax.dynamic_slice_in_dim(summed["w_if"], chip * 384, 384, axis=1)
    small_names = _SMALL_REPL + _SMALL_SHARDED
    came_stored = _BLOCK_WEIGHTS + ("w_if",)
    g_stored = [summed[n] if n in came_stored else _stored(n, summed[n].reshape(args[n].shape)) for n in small_names]
    upd = _adamw_small([_stored(n, args[n]) for n in small_names], g_stored,
                       [_stored(n, args["m_" + n]) for n in small_names], [_stored(n, args["v_" + n]) for n in small_names])
    for dst, arrs in zip((grads, delta, new_m, new_v), (g_stored,) + tuple(upd)):
        dst.update({n: _unstored(n, a) for n, a in zip(small_names, arrs)})

    outs = [loss, dx[None]]
    for group in (grads, delta, new_m, new_v):
        outs += [group[n] for n in _WEIGHTS]
    return tuple(outs)
```

```python
import functools

import jax
import jax.numpy as jnp
from jax import lax
from jax.experimental import pallas as pl
from jax.experimental.pallas import tpu as pltpu

F32 = jnp.float32
BF16 = jnp.bfloat16

SEQ = 2048
D_MODEL = 1024
CHUNK = 64
N_CHUNK = SEQ // CHUNK
HEADS = 4
GLA_DK = 64
GLA_DV = 128
ML_DH = 128
D_FF = 4096
EPS = 1e-6
N_CHIP = 4
N_DEV = 8
TOK_TILE = 256
N_TOK_TILE = SEQ // TOK_TILE
SWEEP = 2
assert CHUNK == 64
N_SWEEP = N_CHUNK // SWEEP

PM_W = 2688
PM_XM = 1536
PM_OP = 2048
PM_AL = 2560
GAB_W = 2048
D_IN = 4624
IN_SHARD = D_IN // N_CHIP
IN_ALOW = 1536
IN_XM = 1552
IN_GATES = 2576

ADAM_LR = 0.001
ADAM_B1 = 0.9
ADAM_B2 = 0.999
ADAM_EPS = 1e-08
ADAM_WD = 0.01
ADAM_STEP = 10

VMEM_LIMIT = 56 * 1024 * 1024


def _params(sem=None):
    return pltpu.CompilerParams(dimension_semantics=sem, vmem_limit_bytes=VMEM_LIMIT)


def _dot(a, b, ca, cb):
    return lax.dot_general(a.astype(BF16), b.astype(BF16), (((ca,), (cb,)), ((), ())), preferred_element_type=F32)


def _pmm_nn(a, b):
    return _dot(a, b, 1, 0)


def _pmm_nt(a, b):
    return _dot(a, b, 1, 1)


def _pmm_tn(a, b):
    return _dot(a, b, 0, 0)


def _pcmm(c, x):
    return lax.dot_general(c, x, (((1,), (0,)), ((), ())), precision=lax.Precision.HIGHEST, preferred_element_type=F32)


@jax.custom_vjp
def _mm_nn(a, b):
    return _dot(a, b, 1, 0)


@jax.custom_vjp
def _mm_nt(a, b):
    return _dot(a, b, 1, 1)


@jax.custom_vjp
def _mm_tn(a, b):
    return _dot(a, b, 0, 0)


_mm_nn.defvjp(lambda a, b: (_dot(a, b, 1, 0), (a, b)), lambda r, g: (_mm_nt(g, r[1]), _mm_tn(r[0], g)))
_mm_nt.defvjp(lambda a, b: (_dot(a, b, 1, 1), (a, b)), lambda r, g: (_mm_nn(g, r[1]), _mm_tn(g, r[0])))
_mm_tn.defvjp(lambda a, b: (_dot(a, b, 0, 0), (a, b)), lambda r, g: (_mm_nt(r[1], g), _mm_nn(r[0], g)))


@jax.custom_vjp
def _cmm(c, x):
    return _pcmm(c, x)


_cmm.defvjp(
    lambda c, x: (_pcmm(c, x), c),
    lambda c, g: (jnp.zeros_like(c), lax.dot_general(c, g, (((0,), (0,)), ((), ())), precision=lax.Precision.HIGHEST,
                                                      preferred_element_type=F32)),
)

_PLAIN_OPS = (_pmm_nn, _pmm_nt, _pmm_tn, _pcmm)
_VJP_OPS = (_mm_nn, _mm_nt, _mm_tn, _cmm)


def _sigmoid(x):
    return 0.5 * (jnp.tanh(0.5 * x) + 1.0)


def _log_sigmoid(x):
    return jnp.minimum(x, 0.0) - jnp.log(1.0 + jnp.exp(-jnp.abs(x)))


def _mean(x):
    return jnp.mean(x, axis=-1, keepdims=True)


def _nt(a, b):
    return lax.dot_general(a, b, (((1,), (1,)), ((), ())), preferred_element_type=F32)


def _tn(a, b):
    return lax.dot_general(a, b, (((0,), (0,)), ((), ())), preferred_element_type=F32)


def _mixer_chunk(ops, p, st, pm, xprev8):
    mm_nn, mm_nt, mm_tn, cmm = ops
    n_rows = pm.shape[0]
    n_ch = n_rows // CHUNK
    row = lax.broadcasted_iota(jnp.int32, (n_rows, n_rows), 0)
    col = lax.broadcasted_iota(jnp.int32, (n_rows, n_rows), 1)
    tri = jnp.logical_and((row >> 6) == (col >> 6), row >= col).astype(F32)
    causal = tri[0:CHUNK, 0:CHUNK] > 0.0
    q = pm[:, 0:256]
    k = pm[:, 256:512]
    v = pm[:, 512:1024]
    g = pm[:, 1024:1536]
    xm = pm[:, PM_XM:PM_XM + 512]
    opre = pm[:, PM_OP:PM_OP + 512]
    alow = pm[:, PM_AL:PM_AL + 128]
    hs = range(HEADS)
    cs = range(n_ch)
    pairs = [(i, h) for i in cs for h in hs]
    rs = [slice(i * CHUNK, (i + 1) * CHUNK) for i in cs]
    last = [slice((i + 1) * CHUNK - 1, (i + 1) * CHUNK) for i in cs]
    s6 = [slice(h * GLA_DK, (h + 1) * GLA_DK) for h in hs]
    s12 = [slice(h * 128, (h + 1) * 128) for h in hs]

    xx = jnp.concatenate([xprev8, xm], axis=0)
    pre = p["cb"]
    for j in range(4):
        pre = pre + p["cw"][j:j + 1, :] * xx[5 + j:5 + j + n_rows, :]
    xc = pre * _sigmoid(pre)
    qm = [mm_nn(xc[:, s12[h]], p["wq"][h]) for h in hs]
    km = [mm_nn(xc[:, s12[h]], p["wk"][h]) for h in hs]
    vm = [mm_nn(xm[:, s12[h]], p["wv"][h]) for h in hs]
    qcat = jnp.concatenate(qm, axis=1)
    kcat = jnp.concatenate(km, axis=1)
    vcat = jnp.concatenate(vm, axis=1)
    gates = (mm_nn(qcat, p["wif"][0:512]) + mm_nn(kcat, p["wif"][512:1024]) + mm_nn(vcat, p["wif"][1024:1536])
             + p["bif"])
    lf = _log_sigmoid(gates)
    fc = cmm(tri, lf)
    gates_t = gates.T
    fc_t = fc.T

    la = _log_sigmoid(mm_nn(alow, p["wau"]) + p["bau"]) * (1.0 / 16.0)
    cum = cmm(tri, la)
    cum_last = [cum[last[i], :] for i in cs]
    to_end = jnp.concatenate([cum_last[i] - cum[rs[i], :] for i in cs], axis=0)
    e_pos = jnp.exp(cum)
    e_neg = jnp.exp(-cum)
    qs = q * (GLA_DK ** -0.5)
    qp = qs * e_pos
    qn = qs * e_neg
    kp = k * e_pos
    kn = k * e_neg
    kl = k * jnp.exp(to_end)
    dec = [jnp.exp(cum_last[i]) for i in cs]
    ks = [km[h] * (ML_DH ** -0.5) for h in hs]
    li_c = {(i, h): gates[rs[i], h:h + 1] for i, h in pairs}
    fc_c = {(i, h): fc[rs[i], 4 + h:5 + h] for i, h in pairs}
    f_last = {(i, h): fc[last[i], 4 + h:5 + h] for i, h in pairs}

    a_fwd = {(i, h): mm_nt(qp[rs[i], s6[h]], kn[rs[i], s6[h]]) for i, h in pairs}
    a_bwd = {(i, h): mm_nt(qn[rs[i], s6[h]], kp[rs[i], s6[h]]) for i, h in pairs}
    s_chunk = {(i, h): mm_tn(v[rs[i], s12[h]], kl[rs[i], s6[h]]) for i, h in pairs}
    qk = {(i, h): mm_nt(qm[h][rs[i]], ks[h][rs[i]]) for i, h in pairs}
    a = {ih: f_last[ih] - fc_c[ih] + li_c[ih] for ih in pairs}
    m_loc = {ih: jnp.max(a[ih], axis=0, keepdims=True) for ih in pairs}
    kw = {(i, h): ks[h][rs[i]] * jnp.exp(a[(i, h)] - m_loc[(i, h)]) for i, h in pairs}
    c_chunk = {(i, h): mm_tn(kw[(i, h)], vm[h][rs[i]]) for i, h in pairs}
    mem = {(0, h): st["S"][h] for h in hs}
    c_in = {(0, h): st["C"][h] for h in hs}
    n_in = {(0, h): st["n"][h] for h in hs}
    m_in = {(0, h): st["m"][h][:, 0:1] for h in hs}
    for i, h in pairs:
        mem[(i + 1, h)] = mem[(i, h)] * dec[i][:, s6[h]] + s_chunk[(i, h)]
        m_nx = jnp.maximum(f_last[(i, h)] + m_in[(i, h)], m_loc[(i, h)])
        sp = jnp.exp(f_last[(i, h)] + m_in[(i, h)] - m_nx)
        sl = jnp.exp(m_loc[(i, h)] - m_nx)
        c_in[(i + 1, h)] = sp * c_in[(i, h)] + sl * c_chunk[(i, h)]
        n_in[(i + 1, h)] = sp * n_in[(i, h)] + sl * jnp.sum(kw[(i, h)], axis=0, keepdims=True)
        m_in[(i + 1, h)] = m_nx
    s_new = [mem[(n_ch, h)] for h in hs]
    o_inter = {(i, h): mm_nt(qp[rs[i], s6[h]], mem[(i, h)]) for i, h in pairs}
    q_c = {(i, h): mm_nn(qm[h][rs[i]], c_in[(i, h)]) for i, h in pairs}
    scores = {ih: jnp.where(causal, a_fwd[ih], a_bwd[ih]) for ih in pairs}
    log_d = {(i, h): gates_t[h:h + 1, rs[i]] - jnp.abs(fc_c[(i, h)] - fc_t[4 + h:5 + h, rs[i]]) for i, h in pairs}
    g_int = {ih: fc_c[ih] + m_in[ih] for ih in pairs}
    m_t = {ih: jnp.maximum(g_int[ih], jnp.max(log_d[ih], axis=1, keepdims=True)) for ih in pairs}
    s = {ih: qk[ih] * jnp.exp(log_d[ih] - m_t[ih]) for ih in pairs}
    scl = {ih: jnp.exp(g_int[ih] - m_t[ih]) for ih in pairs}
    o = {(i, h): mm_nn(scores[(i, h)], v[rs[i], s12[h]]) + o_inter[(i, h)] for i, h in pairs}
    num = {(i, h): mm_nn(s[(i, h)], vm[h][rs[i]]) + scl[(i, h)] * q_c[(i, h)] for i, h in pairs}
    o = {ih: o[ih] * lax.rsqrt(_mean(o[ih] * o[ih]) + EPS) * p["ggla"] for ih in pairs}
    gate = g * _sigmoid(g)
    out_a = {(i, h): o[(i, h)] * gate[rs[i], s12[h]] for i, h in pairs}
    den = {(i, h): jnp.sum(s[(i, h)], axis=1, keepdims=True)
           + scl[(i, h)] * jnp.sum(qm[h][rs[i]] * n_in[(i, h)], axis=1, keepdims=True) for i, h in pairs}
    den = {ih: jnp.maximum(jnp.abs(den[ih]), jnp.exp(-m_t[ih])) for ih in pairs}
    open_gate = _sigmoid(opre)
    hc = {(i, h): num[(i, h)] / den[(i, h)] * open_gate[rs[i], s12[h]] for i, h in pairs}
    d0 = {ih: hc[ih] - _mean(hc[ih]) for ih in pairs}
    y = {ih: d0[ih] * lax.rsqrt(_mean(d0[ih] * d0[ih]) + EPS) for ih in pairs}
    skipped = p["skip"] * xc
    out_b = {(i, h): y[(i, h)] * p["gml"][:, s12[h]] + skipped[rs[i], s12[h]] for i, h in pairs}
    ab = jnp.concatenate([jnp.concatenate([out_a[(i, h)] for h in hs] + [out_b[(i, h)] for h in hs], axis=1) for i in cs],
                         axis=0)
    new = {"S": s_new, "C": [c_in[(n_ch, h)] for h in hs], "n": [n_in[(n_ch, h)] for h in hs],
           "m": [jnp.broadcast_to(m_in[(n_ch, h)], (1, ML_DH)) for h in hs]}
    return ab, new


_P_NAMES = ("wau", "bau", "ggla", "cw", "cb", "wq", "wk", "wv", "wif", "bif", "skip", "gml")
_P_SHAPES = {
    "wau": (128, 256), "bau": (1, 256), "ggla": (1, 128), "cw": (4, 512), "cb": (1, 512),
    "wq": (512, 128), "wk": (512, 128), "wv": (512, 128),
    "wif": (1536, 128), "bif": (1, 128), "skip": (1, 512), "gml": (1, 512),
}
_P_BLOCKDIAG = ("wq", "wk", "wv")
_S_NAMES = ("S", "C", "n", "m")
_S_SHAPES = {"S": (HEADS, GLA_DV, GLA_DK), "C": (HEADS, ML_DH, ML_DH), "n": (HEADS, 1, ML_DH), "m": (HEADS, 1, ML_DH)}


def _per_head(ref):
    return [ref[h] for h in range(HEADS)]


def _block_mask():
    r = lax.broadcasted_iota(jnp.int32, (128, 128), 0)
    c = lax.broadcasted_iota(jnp.int32, (128, 128), 1)
    same_block = (r >> 2) == (c >> 2)
    spread = jnp.logical_and(r < 4, (c & 3) == r)
    return same_block.astype(F32), spread.astype(F32)


def _expand_blockdiag(w_ref, dense_ref):
    same_block, spread = _block_mask()
    for h in range(HEADS):
        tiled = _pmm_nn(w_ref[h * 128:(h + 1) * 128, :], spread)
        dense_ref[h] = tiled * same_block


def _collect_blockdiag(ddense_ref, dw_ref):
    same_block, spread = _block_mask()
    for h in range(HEADS):
        dw_ref[h * 128:(h + 1) * 128, :] = lax.dot_general(
            ddense_ref[h] * same_block, spread, (((1,), (1,)), ((), ())), precision=lax.Precision.HIGHEST,
            preferred_element_type=F32)


def _const_spec(shape):
    zeros = (0,) * len(shape)
    return pl.BlockSpec(shape, lambda i: zeros)


def _split(refs, *counts):
    out, at = [], 0
    for c in counts:
        out.append(refs[at:at + c])
        at += c
    assert at == len(refs)
    return out


def _ride(rider, phases, cond, ins, outs, sems):
    if rider is None or not any(hasattr(rider, phase) for phase in phases):
        return
    lands, (send_sems, recv_sems, flush_sems) = sems[:-3], sems[-3:]

    @pl.when(cond)
    def _():
        for phase in phases:
            if phase == "last" and hasattr(rider, "late"):
                rider.late(ins, lands, send_sems, recv_sems)
                rider.flush("late", lands, outs, flush_sems)
            getattr(rider, phase)(ins, lands, send_sems, recv_sems)
            if hasattr(rider, "flush"):
                rider.flush(phase, lands, outs, flush_sems)
        if "last" in phases and not hasattr(rider, "flush"):
            flush = [pltpu.make_async_copy(lands[k], outs[k], flush_sems.at[k]) for k in range(len(outs))]
            for cp in flush:
                cp.start()
            for cp in flush:
                cp.wait()


def _middle_step(rider, n_steps):
    return min(n_steps - 2, int(getattr(rider, "middle_at", 1.0) * n_steps))


def _rider_specs(rider, rider_ins):
    if rider is None:
        return [], [], [], []
    scratch = [pltpu.VMEM(s.shape, s.dtype) for s in list(rider.out_shape) + list(getattr(rider, "work_shape", ()))]
    scratch += [pltpu.SemaphoreType.DMA((rider.n_sems,)), pltpu.SemaphoreType.DMA((rider.n_sems,)),
                pltpu.SemaphoreType.DMA((getattr(rider, "n_flush", len(rider.out_shape)),))]
    in_space = getattr(rider, "in_space", VMEM_WHOLE)
    in_specs = list(in_space) if isinstance(in_space, (list, tuple)) else [in_space] * len(rider_ins)
    return in_specs, [ANY] * len(rider.out_shape), list(rider.out_shape), scratch


def _merge_tile(ab, gab_ref, x_ref, wpa_ref, wpb_ref, wo_ref, g_ref, x1_ref, mix_ref, mg_ref):
    a = ab[:, 0:512]
    b = ab[:, 512:1024]
    for j in range(N_CHIP):
        blk = slice(j * 256, (j + 1) * 256)
        ya = jnp.dot(a, wpa_ref[j], preferred_element_type=F32)
        yb = jnp.dot(b, wpb_ref[j], preferred_element_type=F32)
        sa = _sigmoid(gab_ref[:, j * 256:(j + 1) * 256])
        sb = _sigmoid(gab_ref[:, 1024 + j * 256:1024 + (j + 1) * 256])
        mg_ref[:, blk] = (sa * ya + sb * yb).astype(BF16)
    mix = jnp.dot(mg_ref[...], wo_ref[...], preferred_element_type=F32)
    mix_ref[...] = mix
    mn, _ = _rms_fwd(mix)
    x1_ref[...] = x_ref[...] + mn * g_ref[...]


def _mixer_fwd(pm, p, gab, x, w_pa4, w_pb4, w_o, g_post, rider=None, rider_ins=()):
    n_p = len(_P_NAMES)
    r_in, r_out_specs, r_out_shape, r_sems = _rider_specs(rider, rider_ins)

    def body(*refs):
        ((pm_ref, xprev_ref), p_list, merge_in, ride_in, (ab_ref,), merge_out, so_refs, ride_out, sc_refs, dense_list,
         sems) = _split(refs, 2, n_p, 6, len(r_in), 1, 3, 4, len(r_out_specs), 4, 3, len(r_sems))
        p_refs = dict(zip(_P_NAMES, p_list))
        dense = dict(zip(_P_BLOCKDIAG, dense_list))
        n = pl.program_id(0)
        _ride(rider, ("first",), n == 0, ride_in, ride_out, sems)

        @pl.when(n == 0)
        def _():
            for r in sc_refs:
                r[...] = jnp.zeros_like(r)
            for nm in _P_BLOCKDIAG:
                _expand_blockdiag(p_refs[nm], dense[nm])

        st = {name: _per_head(r) for name, r in zip(_S_NAMES, sc_refs)}
        pv = {nm: (_per_head(dense[nm]) if nm in _P_BLOCKDIAG else p_refs[nm][...]) for nm in _P_NAMES}
        for name, r in zip(_S_NAMES, so_refs):
            for h in range(HEADS):
                r[0, h] = st[name][h]
        xprev8 = jnp.where(n > 0, xprev_ref[CHUNK - 8:CHUNK, :], 0.0)
        ab, st = _mixer_chunk(_PLAIN_OPS, pv, st, pm_ref[...], xprev8)
        ab = ab.astype(BF16)
        ab_ref[...] = ab
        for name, r in zip(_S_NAMES, sc_refs):
            for h in range(HEADS):
                r[h] = st[name][h]
        _merge_tile(ab, *merge_in, *merge_out)
        _ride(rider, ("middle",), n == _middle_step(rider, N_SWEEP), ride_in, ride_out, sems)
        _ride(rider, ("last",), n == N_SWEEP - 1, ride_in, ride_out, sems)

    rows = lambda width: pl.BlockSpec((SWEEP * CHUNK, width), lambda i: (i, 0))
    in_specs = [rows(PM_W), pl.BlockSpec((CHUNK, 512), lambda i: (jnp.maximum(SWEEP * i - 1, 0), PM_XM // 512))]
    in_specs += [_const_spec(_P_SHAPES[nm]) for nm in _P_NAMES]
    in_specs += [rows(GAB_W), rows(D_MODEL), _once((N_CHIP, 512, 256)), _once((N_CHIP, 512, 256)), _once((D_MODEL, D_MODEL)),
                 _once((1, D_MODEL))] + r_in
    out_specs = [rows(1024), rows(D_MODEL), rows(D_MODEL), rows(D_MODEL)]
    out_shape = [jax.ShapeDtypeStruct((SEQ, 1024), BF16), jax.ShapeDtypeStruct((SEQ, D_MODEL), F32),
                 jax.ShapeDtypeStruct((SEQ, D_MODEL), F32), jax.ShapeDtypeStruct((SEQ, D_MODEL), BF16)]
    for nm in _S_NAMES:
        shp = _S_SHAPES[nm]
        out_specs.append(pl.BlockSpec((1,) + shp, lambda i: (i, 0, 0, 0)))
        out_shape.append(jax.ShapeDtypeStruct((N_SWEEP,) + shp, F32))
    return pl.pallas_call(
        body, grid=(N_SWEEP,), in_specs=in_specs, out_specs=out_specs + r_out_specs, out_shape=out_shape + r_out_shape,
        scratch_shapes=[pltpu.VMEM(_S_SHAPES[nm], F32) for nm in _S_NAMES]
        + [pltpu.VMEM((HEADS, 128, 128), F32) for _ in _P_BLOCKDIAG] + r_sems,
        compiler_params=_params(("arbitrary",)), name="mixer_fwd",
    )(pm, pm, *[p[nm] for nm in _P_NAMES], gab, x, w_pa4, w_pb4, w_o, g_post, *rider_ins)


def _mixer_bwd(pm, dab, states, p, rider=None, rider_ins=()):
    n_p = len(_P_NAMES)
    r_in, r_out_specs, r_out_shape, r_sems = _rider_specs(rider, rider_ins)

    def body(*refs):
        ((pm_ref, xprev_ref, dab_ref), si_refs, p_list, ride_in, (dpm_ref,), dp_list, ride_out, ds_refs, (carry_ref,),
         dense_list, ddense_list, sems) = _split(refs, 3, 4, n_p, len(r_in), 1, n_p, len(r_out_specs), 4, 1, 3, 3, len(r_sems))
        p_refs = dict(zip(_P_NAMES, p_list))
        dp_refs = dict(zip(_P_NAMES, dp_list))
        dense = dict(zip(_P_BLOCKDIAG, dense_list))
        ddense = dict(zip(_P_BLOCKDIAG, ddense_list))
        i = pl.program_id(0)
        blk = N_SWEEP - 1 - i
        _ride(rider, ("first",), i == 0, ride_in, ride_out, sems)

        @pl.when(i == 0)
        def _():
            for r in ds_refs:
                r[...] = jnp.zeros_like(r)
            for nm in _P_NAMES:
                if nm in _P_BLOCKDIAG:
                    ddense[nm][...] = jnp.zeros_like(ddense[nm])
                    _expand_blockdiag(p_refs[nm], dense[nm])
                else:
                    dp_refs[nm][...] = jnp.zeros_like(dp_refs[nm])
            carry_ref[...] = jnp.zeros_like(carry_ref)

        pv = {nm: (_per_head(dense[nm]) if nm in _P_BLOCKDIAG else p_refs[nm][...]) for nm in _P_NAMES}
        dst = {name: _per_head(r) for name, r in zip(_S_NAMES, ds_refs)}
        st = {name: [r[0, h] for h in range(HEADS)] for name, r in zip(_S_NAMES, si_refs)}
        xprev8 = jnp.where(blk > 0, xprev_ref[CHUNK - 8:CHUNK, :], 0.0)
        _, vjp = jax.vjp(functools.partial(_mixer_chunk, _VJP_OPS), pv, st, pm_ref[...], xprev8)
        dp_sum, dst, dpm, dxprev8 = vjp((dab_ref[...], dst))
        reach = jnp.concatenate([jnp.zeros((SWEEP * CHUNK - 8, 512), F32), carry_ref[...]], axis=0)
        dpm_ref[:, 0:PM_XM] = dpm[:, 0:PM_XM].astype(BF16)
        dpm_ref[:, PM_XM:PM_XM + 512] = (dpm[:, PM_XM:PM_XM + 512] + reach).astype(BF16)
        dpm_ref[:, PM_XM + 512:PM_W] = dpm[:, PM_XM + 512:PM_W].astype(BF16)
        carry_ref[...] = dxprev8
        for name, r in zip(_S_NAMES, ds_refs):
            for h in range(HEADS):
                r[h] = dst[name][h]
        for nm in _P_NAMES:
            if nm in _P_BLOCKDIAG:
                for h in range(HEADS):
                    ddense[nm][h] += dp_sum[nm][h]
            else:
                dp_refs[nm][...] += dp_sum[nm]

        @pl.when(i == N_SWEEP - 1)
        def _():
            for nm in _P_BLOCKDIAG:
                _collect_blockdiag(ddense[nm], dp_refs[nm])

        _ride(rider, ("early",), i == 1, ride_in, ride_out, sems)
        _ride(rider, ("middle",), i == _middle_step(rider, N_SWEEP), ride_in, ride_out, sems)
        _ride(rider, ("last",), i == N_SWEEP - 1, ride_in, ride_out, sems)

    rev = lambda i: (N_SWEEP - 1 - i, 0)
    in_specs = [pl.BlockSpec((SWEEP * CHUNK, PM_W), rev),
                pl.BlockSpec((CHUNK, 512), lambda i: (jnp.maximum(SWEEP * (N_SWEEP - 1 - i) - 1, 0), PM_XM // 512)),
                pl.BlockSpec((SWEEP * CHUNK, 1024), rev)]
    for nm in _S_NAMES:
        in_specs.append(pl.BlockSpec((1,) + _S_SHAPES[nm], lambda i: (N_SWEEP - 1 - i, 0, 0, 0)))
    in_specs += [_const_spec(_P_SHAPES[nm]) for nm in _P_NAMES] + r_in
    out_specs = [pl.BlockSpec((SWEEP * CHUNK, PM_W), rev)] + [_const_spec(_P_SHAPES[nm]) for nm in _P_NAMES]
    out_shape = [jax.ShapeDtypeStruct((SEQ, PM_W), BF16)] + [jax.ShapeDtypeStruct(_P_SHAPES[nm], F32) for nm in _P_NAMES]
    res = pl.pallas_call(
        body, grid=(N_SWEEP,), in_specs=in_specs, out_specs=out_specs + r_out_specs, out_shape=out_shape + r_out_shape,
        scratch_shapes=[pltpu.VMEM(_S_SHAPES[nm], F32) for nm in _S_NAMES] + [pltpu.VMEM((8, 512), F32)]
        + [pltpu.VMEM((HEADS, 128, 128), F32) for _ in range(2 * len(_P_BLOCKDIAG))] + r_sems,
        compiler_params=_params(("arbitrary",)), name="mixer_bwd",
    )(pm, pm, dab, *states, *[p[nm] for nm in _P_NAMES], *rider_ins)
    return res[0], dict(zip(_P_NAMES, res[1:1 + n_p])), res[1 + n_p:]


def _tok(width):
    return pl.BlockSpec((TOK_TILE, width), lambda i: (i, 0))


def _once(shape):
    zeros = (0,) * len(shape)
    return pl.BlockSpec(shape, lambda i: zeros, pipeline_mode=pl.Buffered(1))


def _rms_fwd(x):
    r = lax.rsqrt(_mean(x * x) + EPS)
    return x * r, r


def _rms_bwd(dy, xn, r, g):
    gd = dy * g
    return r * (gd - xn * _mean(xn * gd))


def _tiled_call(body, in_specs, out_specs, out_shape, args, name, rider=None, rider_ins=(), scratch=()):
    r_in, r_out_specs, r_out_shape, r_scratch = _rider_specs(rider, rider_ins)
    n_in, n_out = len(in_specs), len(out_specs)

    def hosted(*refs):
        ins, ride_in, outs, ride_out, own, r_scr = _split(refs, n_in, len(r_in), n_out, len(r_out_specs), len(scratch),
                                                          len(r_scratch))
        i = pl.program_id(0)
        _ride(rider, ("first",), i == 0, ride_in, ride_out, r_scr)
        body(*ins, *outs, *own)
        _ride(rider, ("early",), i == 1, ride_in, ride_out, r_scr)
        _ride(rider, ("middle",), i == _middle_step(rider, N_TOK_TILE), ride_in, ride_out, r_scr)
        _ride(rider, ("last",), i == N_TOK_TILE - 1, ride_in, ride_out, r_scr)

    res = pl.pallas_call(
        hosted, grid=(N_TOK_TILE,), in_specs=list(in_specs) + r_in, out_specs=list(out_specs) + r_out_specs,
        out_shape=list(out_shape) + r_out_shape, scratch_shapes=list(scratch) + r_scratch,
        compiler_params=_params(("arbitrary",)), name=name,
    )(*args, *rider_ins)
    return res[:n_out], res[n_out:]


def _join_rows(w4_ref, wt_ref):
    @pl.when(pl.program_id(0) == 0)
    def _():
        for j in range(N_CHIP):
            wt_ref[j * IN_SHARD:(j + 1) * IN_SHARD, :] = w4_ref[j]


def _joined_scratch():
    return [pltpu.VMEM((D_IN, D_MODEL), BF16)]


def _in_proj(x, g_pre, w4_in, rider=None, rider_ins=()):
    def body(x_ref, g_ref, w4_ref, pm_ref, gab_ref, h_ref, wt_ref):
        _join_rows(w4_ref, wt_ref)
        xn, _ = _rms_fwd(x_ref[...])
        h = (xn * g_ref[...]).astype(BF16)
        h_ref[...] = h
        pm_ref[:, 0:PM_XM] = _nt(h, wt_ref[0:IN_ALOW, :])
        pm_ref[:, PM_XM:PM_AL] = _nt(h, wt_ref[IN_XM:IN_GATES, :])
        pm_ref[:, PM_AL:PM_W] = _nt(h, wt_ref[IN_ALOW:IN_ALOW + 128, :])
        gab_ref[...] = _nt(h, wt_ref[IN_GATES:D_IN, :])

    return _tiled_call(
        body, [_tok(D_MODEL), _once((1, D_MODEL)), _once((N_CHIP, IN_SHARD, D_MODEL))],
        [_tok(PM_W), _tok(GAB_W), _tok(D_MODEL)],
        [jax.ShapeDtypeStruct((SEQ, PM_W), F32), jax.ShapeDtypeStruct((SEQ, GAB_W), F32),
         jax.ShapeDtypeStruct((SEQ, D_MODEL), BF16)], (x, g_pre, w4_in), "in_proj", rider, rider_ins, _joined_scratch())


def _mlp(x1, target, g_pre, g_post, w_up4, w_down_a4, w_down_b4):
    def body(x1_ref, t_ref, gpre_ref, gpost_ref, wup_ref, wda_ref, wdb_ref,
             dx1_ref, u_ref, dd_ref, h2_ref, dpre_ref, dgpost_ref, dgpre_ref, loss_ref):
        @pl.when(pl.program_id(0) == 0)
        def _():
            dgpost_ref[...] = jnp.zeros_like(dgpost_ref)
            dgpre_ref[...] = jnp.zeros_like(dgpre_ref)
            loss_ref[...] = jnp.zeros_like(loss_ref)

        x1 = x1_ref[...]
        gpre = gpre_ref[...]
        gpost = gpost_ref[...]
        xn2, r2 = _rms_fwd(x1)
        h2 = (xn2 * gpre).astype(BF16)
        h2_ref[...] = h2
        rl = []
        d = jnp.zeros((TOK_TILE, D_MODEL), F32)
        for j in range(N_CHIP):
            blk = slice(j * 1024, (j + 1) * 1024)
            r = jnp.maximum(jnp.dot(h2, wup_ref[j], preferred_element_type=F32), 0.0)
            rl.append(r)
            u = (r * r).astype(BF16)
            u_ref[:, blk] = u
            d = d + jnp.dot(u[:, 0:512], wda_ref[j], preferred_element_type=F32)
            d = d + jnp.dot(u[:, 512:1024], wdb_ref[j], preferred_element_type=F32)
        dn, r3 = _rms_fwd(d)
        diff = x1 + dn * gpost - t_ref[...]
        loss_ref[...] += jnp.sum(diff * diff, keepdims=True) * (0.5 / D_MODEL)
        dy = diff * (1.0 / D_MODEL)
        dgpost_ref[...] += jnp.sum(dy * dn, axis=0, keepdims=True)
        dd = _rms_bwd(dy, dn, r3, gpost).astype(BF16)
        dd_ref[...] = dd
        dh2 = jnp.zeros((TOK_TILE, D_MODEL), F32)
        for j in range(N_CHIP):
            blk = slice(j * 1024, (j + 1) * 1024)
            du = jnp.concatenate([_nt(dd, wda_ref[j]), _nt(dd, wdb_ref[j])], axis=1)
            dpre = (du * (2.0 * rl[j])).astype(BF16)
            dpre_ref[:, blk] = dpre
            dh2 = dh2 + _nt(dpre, wup_ref[j])
        dgpre_ref[...] += jnp.sum(dh2 * xn2, axis=0, keepdims=True)
        dx1_ref[...] = dy + _rms_bwd(dh2, xn2, r2, gpre)

    acc = pl.BlockSpec((1, D_MODEL), lambda i: (0, 0))
    return pl.pallas_call(
        body, grid=(N_TOK_TILE,),
        in_specs=[_tok(D_MODEL), _tok(D_MODEL), _once((1, D_MODEL)), _once((1, D_MODEL)),
                  _once((N_CHIP, D_MODEL, 1024)), _once((N_CHIP, 512, D_MODEL)), _once((N_CHIP, 512, D_MODEL))],
        out_specs=[_tok(D_MODEL), _tok(D_FF), _tok(D_MODEL), _tok(D_MODEL), _tok(D_FF), acc, acc,
                   pl.BlockSpec((1, 128), lambda i: (0, 0))],
        out_shape=[jax.ShapeDtypeStruct((SEQ, D_MODEL), F32), jax.ShapeDtypeStruct((SEQ, D_FF), BF16),
                   jax.ShapeDtypeStruct((SEQ, D_MODEL), BF16), jax.ShapeDtypeStruct((SEQ, D_MODEL), BF16),
                   jax.ShapeDtypeStruct((SEQ, D_FF), BF16), jax.ShapeDtypeStruct((1, D_MODEL), F32),
                   jax.ShapeDtypeStruct((1, D_MODEL), F32), jax.ShapeDtypeStruct((1, 128), F32)],
        compiler_params=_params(("arbitrary",)), name="mlp_fwd_bwd",
    )(x1, target, g_pre, g_post, w_up4, w_down_a4, w_down_b4)


def _merge_bwd(dx1, mix, ab, gab, merged, w_pa4, w_pb4, w_o, g_post):
    def body(dx1_ref, mix_ref, ab_ref, gab_ref, mg_ref, wpa_ref, wpb_ref, wo_ref, g_ref,
             dgab_ref, dab_ref, dg_ref, dwpa_ref, dwpb_ref, dwo_ref, acc_pa, acc_pb, acc_o):
        @pl.when(pl.program_id(0) == 0)
        def _():
            dg_ref[...] = jnp.zeros_like(dg_ref)
            acc_pa[...] = jnp.zeros_like(acc_pa)
            acc_pb[...] = jnp.zeros_like(acc_pb)
            acc_o[...] = jnp.zeros_like(acc_o)

        dx1 = dx1_ref[...]
        mn, r = _rms_fwd(mix_ref[...])
        dg_ref[...] += jnp.sum(dx1 * mn, axis=0, keepdims=True)
        dmix = _rms_bwd(dx1, mn, r, g_ref[...]).astype(BF16)
        acc_o[...] += _tn(mg_ref[...], dmix)
        dmerged = _nt(dmix, wo_ref[...])
        a = ab_ref[:, 0:512]
        b = ab_ref[:, 512:1024]
        da = jnp.zeros((TOK_TILE, 512), F32)
        db = jnp.zeros((TOK_TILE, 512), F32)
        dyas, dybs = [], []
        for j in range(N_CHIP):
            blk = slice(j * 256, (j + 1) * 256)
            blk_b = slice(1024 + j * 256, 1024 + (j + 1) * 256)
            dm = dmerged[:, blk]
            ya = jnp.dot(a, wpa_ref[j], preferred_element_type=F32)
            yb = jnp.dot(b, wpb_ref[j], preferred_element_type=F32)
            sa = _sigmoid(gab_ref[:, blk])
            sb = _sigmoid(gab_ref[:, blk_b])
            dya = (dm * sa).astype(BF16)
            dyb = (dm * sb).astype(BF16)
            dyas.append(dya)
            dybs.append(dyb)
            dgab_ref[:, blk] = (dm * ya * sa * (1.0 - sa)).astype(BF16)
            dgab_ref[:, blk_b] = (dm * yb * sb * (1.0 - sb)).astype(BF16)
            da = da + _nt(dya, wpa_ref[j])
            db = db + _nt(dyb, wpb_ref[j])
        dab_ref[:, 0:512] = da
        dab_ref[:, 512:1024] = db
        acc_pa[...] += _tn(a, jnp.concatenate(dyas, axis=1))
        acc_pb[...] += _tn(b, jnp.concatenate(dybs, axis=1))

        @pl.when(pl.program_id(0) == N_TOK_TILE - 1)
        def _():
            dwo_ref[...] = acc_o[...].astype(BF16)
            for j in range(N_CHIP):
                dwpa_ref[j] = acc_pa[:, j * 256:(j + 1) * 256].astype(BF16)
                dwpb_ref[j] = acc_pb[:, j * 256:(j + 1) * 256].astype(BF16)

    whole = lambda shape: pl.BlockSpec(shape, lambda i: (0,) * len(shape))
    return pl.pallas_call(
        body, grid=(N_TOK_TILE,),
        in_specs=[_tok(D_MODEL), _tok(D_MODEL), _tok(1024), _tok(GAB_W), _tok(D_MODEL), _once((N_CHIP, 512, 256)),
                  _once((N_CHIP, 512, 256)), _once((D_MODEL, D_MODEL)), _once((1, D_MODEL))],
        out_specs=[_tok(GAB_W), _tok(1024), whole((1, D_MODEL)), whole((N_CHIP, 512, 256)), whole((N_CHIP, 512, 256)),
                   whole((D_MODEL, D_MODEL))],
        out_shape=[jax.ShapeDtypeStruct((SEQ, GAB_W), BF16), jax.ShapeDtypeStruct((SEQ, 1024), F32),
                   jax.ShapeDtypeStruct((1, D_MODEL), F32), jax.ShapeDtypeStruct((N_CHIP, 512, 256), BF16),
                   jax.ShapeDtypeStruct((N_CHIP, 512, 256), BF16), jax.ShapeDtypeStruct((D_MODEL, D_MODEL), BF16)],
        scratch_shapes=[pltpu.VMEM((512, D_MODEL), F32), pltpu.VMEM((512, D_MODEL), F32),
                        pltpu.VMEM((D_MODEL, D_MODEL), F32)],
        compiler_params=_params(("arbitrary",)), name="merge_bwd",
    )(dx1, mix, ab, gab, merged, w_pa4, w_pb4, w_o, g_post)


def _in_proj_bwd(dpm, dgab, x, dx1, g_pre, w4_in, rider=None, rider_ins=()):
    def body(dpm_ref, dgab_ref, x_ref, dx1_ref, g_ref, w4_ref, dx_ref, dg_ref, wt_ref):
        _join_rows(w4_ref, wt_ref)

        @pl.when(pl.program_id(0) == 0)
        def _():
            dg_ref[...] = jnp.zeros_like(dg_ref)

        dh = jnp.dot(dpm_ref[:, 0:PM_XM], wt_ref[0:IN_ALOW, :], preferred_element_type=F32)
        dh = dh + jnp.dot(dpm_ref[:, PM_XM:PM_AL], wt_ref[IN_XM:IN_GATES, :], preferred_element_type=F32)
        dh = dh + jnp.dot(dpm_ref[:, PM_AL:PM_W], wt_ref[IN_ALOW:IN_ALOW + 128, :], preferred_element_type=F32)
        dh = dh + jnp.dot(dgab_ref[...], wt_ref[IN_GATES:D_IN, :], preferred_element_type=F32)
        xn, r = _rms_fwd(x_ref[...])
        dg_ref[...] += jnp.sum(dh * xn, axis=0, keepdims=True)
        dx_ref[...] = dx1_ref[...] + _rms_bwd(dh, xn, r, g_ref[...])

    return _tiled_call(
        body, [_tok(PM_W), _tok(GAB_W), _tok(D_MODEL), _tok(D_MODEL), _once((1, D_MODEL)),
               _once((N_CHIP, IN_SHARD, D_MODEL))],
        [_tok(D_MODEL), pl.BlockSpec((1, D_MODEL), lambda i: (0, 0))],
        [jax.ShapeDtypeStruct((SEQ, D_MODEL), F32), jax.ShapeDtypeStruct((1, D_MODEL), F32)],
        (dpm, dgab, x, dx1, g_pre, w4_in), "in_proj_bwd", rider, rider_ins, _joined_scratch())


def _dw_in(dpm, dgab, h):
    n_pm = PM_AL // 512
    n_blk = n_pm + GAB_W // 512

    def place(o_ref, rows, lo, hi):
        for j in range(N_CHIP):
            a, b = max(lo, j * IN_SHARD), min(hi, (j + 1) * IN_SHARD)
            if a < b:
                o_ref[j, a - j * IN_SHARD:b - j * IN_SHARD, :] = rows(a - lo, b - lo)

    def body(dpm_ref, dgab_ref, dal_ref, h_ref, o_ref, blk_ref):
        i = pl.program_id(0)

        @pl.when(i < n_pm)
        def _():
            blk_ref[...] = _tn(dpm_ref[...], h_ref[...]).astype(BF16)

        @pl.when(i >= n_pm)
        def _():
            blk_ref[...] = _tn(dgab_ref[...], h_ref[...]).astype(BF16)

        for k in range(n_blk):
            off = k * 512 + (IN_XM - IN_ALOW) * (k >= IN_ALOW // 512)

            @pl.when(i == k)
            def _():
                place(o_ref, lambda a, b: blk_ref[a:b, :], off, off + 512)

        @pl.when(i == 0)
        def _():
            a_low = _tn(dal_ref[...], h_ref[...])[0:IN_XM - IN_ALOW].astype(BF16)
            place(o_ref, lambda a, b: a_low[a:b], IN_ALOW, IN_XM)

    return pl.pallas_call(
        body, grid=(n_blk,),
        in_specs=[pl.BlockSpec((SEQ, 512), lambda i: (0, jnp.minimum(i, n_pm - 1))),
                  pl.BlockSpec((SEQ, 512), lambda i: (0, jnp.maximum(i - n_pm, 0))),
                  pl.BlockSpec((SEQ, 128), lambda i: (0, PM_AL // 128)),
                  _once((SEQ, D_MODEL))],
        out_specs=pl.BlockSpec((N_CHIP, IN_SHARD, D_MODEL), lambda i: (0, 0, 0)),
        out_shape=jax.ShapeDtypeStruct((N_CHIP, IN_SHARD, D_MODEL), BF16),
        scratch_shapes=[pltpu.VMEM((512, D_MODEL), BF16)],
        compiler_params=_params(("arbitrary",)), name="dw_in",
    )(dpm, dgab, dpm, h)


def _tn_matmul(a, b, name, shards=1, tm=1024, rider=None, rider_ins=()):
    m, n = a.shape[1], b.shape[1]
    tm = min(tm, m)
    tn = n // shards if shards > 1 else min(n, 1024)
    steps_i, steps_j = m // tm, n // tn
    r_in, r_out_specs, r_out_shape, r_scratch = _rider_specs(rider, rider_ins)

    def body(*refs):
        (a_ref, b_ref), ride_in, (o_ref,), ride_out, scratch = _split(refs, 2, len(r_in), 1, len(r_out_specs), len(r_scratch))
        step = pl.program_id(0) * steps_j + pl.program_id(1)
        _ride(rider, ("first",), step == 0, ride_in, ride_out, scratch)
        o_ref[...] = _tn(a_ref[...], b_ref[...]).astype(BF16)
        _ride(rider, ("middle", "last"), step == steps_i * steps_j - 1, ride_in, ride_out, scratch)

    if shards > 1:
        out_spec = pl.BlockSpec((None, tm, tn), lambda i, j: (j, i, 0))
        out_shape = jax.ShapeDtypeStruct((shards, m, tn), BF16)
    else:
        out_spec = pl.BlockSpec((tm, tn), lambda i, j: (i, j))
        out_shape = jax.ShapeDtypeStruct((m, n), BF16)
    res = pl.pallas_call(
        body, grid=(steps_i, steps_j),
        in_specs=[pl.BlockSpec((SEQ, tm), lambda i, j: (0, i)), pl.BlockSpec((SEQ, tn), lambda i, j: (0, j))] + r_in,
        out_specs=[out_spec] + r_out_specs, out_shape=[out_shape] + r_out_shape, scratch_shapes=r_scratch,
        compiler_params=_params(("arbitrary", "arbitrary")), name=name,
    )(a, b, *rider_ins)
    return res[0] if rider is None else (res[0], res[1:])


MESH = pl.DeviceIdType.MESH
ANY = pl.BlockSpec(memory_space=pl.ANY)
VMEM_WHOLE = pl.BlockSpec(memory_space=pltpu.VMEM)

_BIG = ("w_in", "w_pa", "w_pb", "w_o", "w_up", "w_down")
_BIG_SHARD = {"w_in": (IN_SHARD, D_MODEL), "w_pa": (512, 256), "w_pb": (512, 256), "w_o": (256, D_MODEL),
              "w_up": (D_MODEL, 1024), "w_down": (1024, D_MODEL),
              "w_down_a": (512, D_MODEL), "w_down_b": (512, D_MODEL)}
_BIG_SPLIT = {"w_in": 1, "w_pa": 0, "w_pb": 0, "w_o": 0, "w_up": 0, "w_down": 0, "w_down_a": 0, "w_down_b": 0}


def _half(ref, e, name, lead=0, part=None):
    axis = _BIG_SPLIT[name]
    size = _BIG_SHARD[name][axis] // 2
    start = e * size
    if part is not None:
        size //= 2
        start = start + part * size
    start = pl.multiple_of(start, 128 if axis == 1 else 16)
    idx = [pl.ds(0, ref.shape[a]) for a in range(lead)]
    idx += [pl.ds(start, size), pl.ds(0, _BIG_SHARD[name][1])] if axis == 0 else [pl.ds(0, _BIG_SHARD[name][0]), pl.ds(start, size)]
    return ref.at[tuple(idx)]


def _half_shape(name):
    r, c = _BIG_SHARD[name]
    return (r // 2, c) if _BIG_SPLIT[name] == 0 else (r, c // 2)


def _remote(src, dst, send_sems, recv_sems, k, to):
    return pltpu.make_async_remote_copy(src_ref=src, dst_ref=dst, send_sem=send_sems.at[k], recv_sem=recv_sems.at[k],
                                        device_id=to, device_id_type=MESH)


def _mesh_place():
    x, y, c = lax.axis_index("x"), lax.axis_index("y"), lax.axis_index("c")
    return x, y, c, [(1 - x, y), (x, 1 - y), (1 - x, 1 - y)]


class _Gather:
    def __init__(self, names, small=(), middle_at=0.5):
        self.middle_at = middle_at
        self.names = tuple(names)
        self.nb = len(self.names)
        self.n = self.nb + len(small)
        self.n_sems = 8 * self.nb + 3 * len(small)
        self.n_flush = 6 * self.nb + len(small)
        self.out_shape = [jax.ShapeDtypeStruct((N_CHIP,) + _BIG_SHARD[nm], BF16) for nm in self.names]
        self.out_shape += [jax.ShapeDtypeStruct((N_CHIP,) + s.shape, s.dtype) for s in small]
        self.in_space = [self._in_spec(nm) for nm in self.names] + [VMEM_WHOLE] * len(small)

    @staticmethod
    def _in_spec(name):
        if name in ("w_down_a", "w_down_b"):
            half = 0 if name == "w_down_a" else 1
            return pl.BlockSpec(_BIG_SHARD[name], lambda *_: (half, 0), pipeline_mode=pl.Buffered(1))
        return VMEM_WHOLE

    def _copies(self, ins, outs, ss, rs, k):
        x, y, c, _ = _mesh_place()
        name = self.names[k]
        me, xn, yn, dg = 2 * x + y, 2 * (1 - x) + y, 2 * x + (1 - y), 2 * (1 - x) + (1 - y)
        to_x, to_y, sibling = (1 - x, y, c), (x, 1 - y, c), (x, y, 1 - c)

        def region(slot, e, part=None):
            return _half(outs[k].at[slot], e, name, part=part)

        def copy(pair, src, dst, to):
            return _remote(src, dst, ss, rs, 8 * k + pair, to)

        mine = region(me, c)
        sent = [copy(0, mine, mine, to_x), copy(1, mine, mine, to_y),
                copy(2, region(xn, c, 0), region(xn, c, 0), to_y), copy(3, region(yn, c, 1), region(yn, c, 1), to_x),
                copy(4, region(xn, c), region(xn, c), sibling), copy(5, region(yn, c), region(yn, c), sibling),
                copy(6, region(dg, c, 0), region(dg, c, 0), sibling), copy(7, region(dg, c, 1), region(dg, c, 1), sibling)]
        landing = [region(xn, c), region(yn, c), region(dg, c, 0), region(dg, c, 1),
                   region(xn, 1 - c), region(yn, 1 - c), region(dg, 1 - c, 0), region(dg, 1 - c, 1)]
        received = [copy(pair, dst, dst, sibling) for pair, dst in enumerate(landing)]
        return sent, received

    def _small(self, ins, outs, ss, rs, k, j, peer, slot, c):
        return _remote(ins[k], outs[k].at[slot], ss, rs, 8 * self.nb + 3 * (k - self.nb) + j, (*peer, c))

    def flush(self, phase, lands, outs, fs):
        x, y, c, _ = _mesh_place()
        me, xn, yn, dg = 2 * x + y, 2 * (1 - x) + y, 2 * x + (1 - y), 2 * (1 - x) + (1 - y)

        def pieces(k):
            name = self.names[k]
            spots = [lambda r: r.at[me], lambda r: _half(r.at[xn], c, name), lambda r: _half(r.at[yn], c, name),
                     lambda r: _half(r.at[xn], 1 - c, name), lambda r: _half(r.at[yn], 1 - c, name), lambda r: r.at[dg]]
            return [pltpu.make_async_copy(spot(lands[k]), spot(outs[k]), fs.at[6 * k + t]) for t, spot in enumerate(spots)]

        ready = {"first": (0,), "middle": (1, 2), "late": (3, 4), "last": (5,)}[phase]
        for k in range(self.nb):
            cps = pieces(k)
            for t in ready:
                cps[t].start()
        if phase == "last":
            small = [pltpu.make_async_copy(lands[k], outs[k], fs.at[6 * self.nb + k - self.nb]) for k in range(self.nb, self.n)]
            for cp in small:
                cp.start()
            for k in range(self.nb):
                for cp in pieces(k):
                    cp.wait()
            for cp in small:
                cp.wait()

    def first(self, ins, outs, ss, rs):
        x, y, c, peers = _mesh_place()
        me = 2 * x + y
        for k in range(self.nb):
            outs[k][me] = ins[k][...].astype(BF16)
            sent, _ = self._copies(ins, outs, ss, rs, k)
            sent[0].start()
            sent[1].start()
        for k in range(self.nb, self.n):
            for j, peer in enumerate(peers):
                self._small(ins, outs, ss, rs, k, j, peer, me, c).start()
            outs[k][me] = ins[k][...]

    def middle(self, ins, outs, ss, rs):
        for k in range(self.nb):
            sent, received = self._copies(ins, outs, ss, rs, k)
            for pair in (0, 1):
                received[pair].wait_recv()
                sent[2 + pair].start()
                sent[4 + pair].start()

    def late(self, ins, outs, ss, rs):
        for k in range(self.nb):
            _, received = self._copies(ins, outs, ss, rs, k)
            for pair in (4, 5):
                received[pair].wait_recv()

    def last(self, ins, outs, ss, rs):
        x, y, c, peers = _mesh_place()
        for k in range(self.nb):
            sent, received = self._copies(ins, outs, ss, rs, k)
            for pair in (2, 3):
                received[pair].wait_recv()
                sent[4 + pair].start()
        for k in range(self.nb):
            sent, received = self._copies(ins, outs, ss, rs, k)
            for pair in (6, 7):
                received[pair].wait_recv()
            for cp in sent:
                cp.wait_send()
        for k in range(self.nb, self.n):
            for j, (px, py) in enumerate(peers):
                self._small(ins, outs, ss, rs, k, j, (px, py), 2 * px + py, c).wait_recv()
                self._small(ins, outs, ss, rs, k, j, (px, py), 2 * x + y, c).wait_send()


def _run_alone(rider, ins, name):
    r_in, r_out_specs, r_out_shape, r_scratch = _rider_specs(rider, ins)

    def body(*refs):
        ride_in, ride_out, scratch = _split(refs, len(r_in), len(r_out_specs), len(r_scratch))
        _ride(rider, ("first", "middle", "last"), pl.program_id(0) == 0, ride_in, ride_out, scratch)

    return pl.pallas_call(
        body, grid=(1,), in_specs=r_in, out_specs=r_out_specs, out_shape=r_out_shape, scratch_shapes=r_scratch,
        compiler_params=_params(("arbitrary",)), name=name,
    )(*ins)


class _Presum:
    in_space = ANY

    def __init__(self, names, base=0):
        self.names = tuple(names)
        self.n = len(self.names)
        self.base = base
        self.n_sems = 3 * self.n
        self.out_shape = [jax.ShapeDtypeStruct((N_CHIP,) + _half_shape(nm), BF16) for nm in self.names]
        self.work_shape = self.out_shape + self.out_shape

    def _stage(self, ins, bufs, ss, k, e, which):
        n = self.n
        return pltpu.make_async_copy(_half(ins[k], e, self.names[k], lead=1), bufs[which * n + k],
                                     ss.at[self.base + which * n + k])

    def _give(self, bufs, ss, rs, k, sibling):
        return _remote(bufs[self.n + k], bufs[k], ss, rs, self.base + k, sibling)

    def first(self, ins, bufs, ss, rs):
        x, y, c, _ = _mesh_place()
        for k in range(self.n):
            self._stage(ins, bufs, ss, k, 1 - c, 1).start()
        for k in range(self.n):
            self._stage(ins, bufs, ss, k, c, 2).start()
        for k in range(self.n):
            self._stage(ins, bufs, ss, k, 1 - c, 1).wait()
            self._give(bufs, ss, rs, k, (x, y, 1 - c)).start()

    def middle(self, ins, bufs, ss, rs):
        pass

    def last(self, ins, bufs, ss, rs):
        x, y, c, _ = _mesh_place()
        for k in range(self.n):
            self._give(bufs, ss, rs, k, (x, y, 1 - c)).wait_recv()
            self._stage(ins, bufs, ss, k, c, 2).wait()

            @pl.loop(0, N_CHIP)
            def _(j):
                bufs[k][j] = (bufs[k][j].astype(F32) + bufs[2 * self.n + k][j].astype(F32)).astype(BF16)
        for k in range(self.n):
            self._give(bufs, ss, rs, k, (x, y, 1 - c)).wait_send()


class _ReduceRelay:
    middle_at = 0.75

    def __init__(self, names, base=0):
        self.names = tuple(names)
        self.n = len(self.names)
        self.base = base
        self.n_sems = 6 * self.n
        self.out_shape = [jax.ShapeDtypeStruct((N_CHIP,) + _half_shape(nm), BF16) for nm in self.names]
        quarter = [jax.ShapeDtypeStruct(self._part_shape(nm), BF16) for nm in self.names]
        self.work_shape = quarter + quarter

    @staticmethod
    def _part_shape(name):
        r, c = _half_shape(name)
        return (r // 2, c) if _BIG_SPLIT[name] == 0 else (r, c // 2)

    def _part(self, ref, name, p):
        r, c = self._part_shape(name)
        return ref.at[pl.ds(p * r, r), pl.ds(0, c)] if _BIG_SPLIT[name] == 0 else ref.at[pl.ds(0, r), pl.ds(p * c, c)]

    def _copies(self, ins, bufs, ss, rs, k):
        x, y, c, _ = _mesh_place()
        name, n = self.names[k], self.n
        me, xn, yn, dg = 2 * x + y, 2 * (1 - x) + y, 2 * x + (1 - y), 2 * (1 - x) + (1 - y)
        to_x, to_y = (1 - x, y, c), (x, 1 - y, c)
        mine = lambda slot, p: self._part(ins[k].at[slot], name, p)
        slot = lambda s, p: self._part(bufs[k].at[s], name, p)
        from_x, from_y = bufs[n + k], bufs[2 * n + k]

        def copy(pair, src, dst, to):
            return _remote(src, dst, ss, rs, self.base + 6 * k + pair, to)

        sent = [copy(0, mine(dg, 0), from_x, to_x), copy(1, mine(dg, 1), from_y, to_y),
                copy(2, mine(xn, 0), slot(me, 0), to_x), copy(3, mine(yn, 1), slot(me, 1), to_y),
                copy(4, from_y, slot(me, 1), to_x), copy(5, from_x, slot(me, 0), to_y)]
        landing = [from_x, from_y, slot(xn, 0), slot(yn, 1), slot(xn, 1), slot(yn, 0)]
        received = [copy(pair, dst, dst, to_x) for pair, dst in enumerate(landing)]
        return sent, received

    def first(self, ins, bufs, ss, rs):
        x, y, c, _ = _mesh_place()
        me, dg = 2 * x + y, 2 * (1 - x) + (1 - y)
        for k in range(self.n):
            sent, _ = self._copies(ins, bufs, ss, rs, k)
            for pair in range(4):
                sent[pair].start()
        for k in range(self.n):
            bufs[k][me] = ins[k][me]
            bufs[k][dg] = jnp.zeros(_half_shape(self.names[k]), BF16)

    def middle(self, ins, bufs, ss, rs):
        x, y, c, _ = _mesh_place()
        xn, yn = 2 * (1 - x) + y, 2 * x + (1 - y)
        for k in range(self.n):
            sent, received = self._copies(ins, bufs, ss, rs, k)
            name, n = self.names[k], self.n
            for pair, buf, own in ((0, bufs[n + k], self._part(ins[k].at[yn], name, 0)),
                                   (1, bufs[2 * n + k], self._part(ins[k].at[xn], name, 1))):
                received[pair].wait_recv()
                buf[...] = (buf[...].astype(F32) + own[...].astype(F32)).astype(BF16)
            sent[5].start()
            sent[4].start()

    def last(self, ins, bufs, ss, rs):
        for k in range(self.n):
            sent, received = self._copies(ins, bufs, ss, rs, k)
            for pair in range(2, 6):
                received[pair].wait_recv()
            for cp in sent:
                cp.wait_send()


class _PresumThenRelay:
    in_space = ANY
    middle_at = _ReduceRelay.middle_at

    def __init__(self, names):
        self.relay = _ReduceRelay(names)
        self.pre = _Presum(names, base=self.relay.n_sems)
        self.n_sems = self.relay.n_sems + self.pre.n_sems
        self.out_shape = self.relay.out_shape
        self.work_shape = list(self.relay.work_shape) + list(self.pre.out_shape) + list(self.pre.work_shape)
        self.n_relay = len(self.relay.out_shape) + len(self.relay.work_shape)

    def first(self, ins, bufs, ss, rs):
        self.pre.first(ins, bufs[self.n_relay:], ss, rs)

    def early(self, ins, bufs, ss, rs):
        self.pre.last(ins, bufs[self.n_relay:], ss, rs)
        self.relay.first(bufs[self.n_relay:], bufs[:self.n_relay], ss, rs)

    def middle(self, ins, bufs, ss, rs):
        self.relay.middle(bufs[self.n_relay:], bufs[:self.n_relay], ss, rs)

    def last(self, ins, bufs, ss, rs):
        self.relay.last(bufs[self.n_relay:], bufs[:self.n_relay], ss, rs)


class _SendPartials:
    def __init__(self, names, small_shape=None):
        self.n = len(names)
        self.small = small_shape is not None
        self.n_sems = 3 * self.n + 7
        self.out_shape = [jax.ShapeDtypeStruct((N_CHIP,) + _half_shape(nm), BF16) for nm in names]
        if self.small:
            self.out_shape.append(jax.ShapeDtypeStruct((N_DEV,) + small_shape, F32))

    def _piece(self, ins, outs, ss, rs, k, j, peer, src_slot, dst_slot, c):
        return _remote(ins[k].at[src_slot], outs[k].at[dst_slot], ss, rs, 3 * k + j, (*peer, c))

    def _small(self, ins, outs, ss, rs, r, other, slot):
        return _remote(ins[self.n], outs[self.n].at[slot], ss, rs, 3 * self.n + r, other)

    @staticmethod
    def _others(x, y, c):
        return [(x, y, 1 - c), (1 - x, y, c), (1 - x, y, 1 - c), (x, 1 - y, c), (x, 1 - y, 1 - c),
                (1 - x, 1 - y, c), (1 - x, 1 - y, 1 - c)]

    def first(self, ins, outs, ss, rs, only=None):
        x, y, c, peers = _mesh_place()
        me = 2 * x + y
        which = range(self.n) if only is None else only
        for k in which:
            for j, (px, py) in enumerate(peers):
                self._piece(ins, outs, ss, rs, k, j, (px, py), 2 * px + py, me, c).start()
        if self.small:
            for r, other in enumerate(self._others(x, y, c)):
                self._small(ins, outs, ss, rs, r, other, 4 * x + 2 * y + c).start()
            outs[self.n][4 * x + 2 * y + c] = ins[self.n][...]
        for k in which:
            outs[k][me] = ins[k][me]

    def middle(self, ins, outs, ss, rs):
        pass

    def last(self, ins, outs, ss, rs):
        x, y, c, peers = _mesh_place()
        me = 2 * x + y
        for k in range(self.n):
            for j, (px, py) in enumerate(peers):
                self._piece(ins, outs, ss, rs, k, j, (px, py), me, 2 * px + py, c).wait_recv()
                self._piece(ins, outs, ss, rs, k, j, (px, py), 2 * px + py, me, c).wait_send()
        if self.small:
            for r, (px, py, pc) in enumerate(self._others(x, y, c)):
                self._small(ins, outs, ss, rs, r, (px, py, pc), 4 * px + 2 * py + pc).wait_recv()
                self._small(ins, outs, ss, rs, r, (px, py, pc), 4 * x + 2 * y + c).wait_send()


class _PresumThenSend:
    def __init__(self, names):
        self.send = _SendPartials(names)
        self.pre = _Presum(names[-1:], base=self.send.n_sems)
        self.n = self.send.n
        self.n_sems = self.send.n_sems + self.pre.n_sems
        self.out_shape = self.send.out_shape
        self.work_shape = list(self.pre.out_shape) + list(self.pre.work_shape)
        self.in_space = [VMEM_WHOLE] * (self.n - 1) + [ANY]

    def _partials(self, ins, bufs):
        return list(ins[:self.n - 1]) + [bufs[self.n]]

    def first(self, ins, bufs, ss, rs):
        self.pre.first(ins[self.n - 1:], bufs[self.n:], ss, rs)
        self.send.first(ins, bufs[:self.n], ss, rs, only=range(self.n - 1))

    def early(self, ins, bufs, ss, rs):
        self.pre.last(ins[self.n - 1:], bufs[self.n:], ss, rs)
        self.send.first(self._partials(ins, bufs), bufs[:self.n], ss, rs, only=(self.n - 1,))

    def middle(self, ins, bufs, ss, rs):
        pass

    def last(self, ins, bufs, ss, rs):
        self.send.last(self._partials(ins, bufs), bufs[:self.n], ss, rs)


def _sum_swap(names, parts, small):
    n = len(parts)
    everyone = _SendPartials((), small.shape)

    def body(*refs):
        (p_hbm, (small_ref,), o_hbm, (osmall_ref,), p_refs, o_refs, (all_ref,),
         (send_sems, recv_sems, ss_small, rs_small, load_sems, leave_sems)) = _split(refs, n, 1, n, 1, n, n, 1, 6)
        x, y, c = lax.axis_index("x"), lax.axis_index("y"), lax.axis_index("c")
        loads = [pltpu.make_async_copy(p_hbm[k], p_refs[k], load_sems.at[k]) for k in range(n)]
        for cp in loads:
            cp.start()
        everyone.first([small_ref], [all_ref], ss_small, rs_small)

        def mine(k):
            part = _half(o_refs[k], c, names[k])
            return _remote(part, part, send_sems, recv_sems, k, (x, y, 1 - c))

        def leave(k, whose):
            e = c if whose == 0 else 1 - c
            return pltpu.make_async_copy(_half(o_refs[k], e, names[k]), _half(o_hbm[k], e, names[k]),
                                         leave_sems.at[2 * k + whose])

        for k in range(n):
            loads[k].wait()
            for e in range(2):
                @pl.when(c == e)
                def _():
                    g = p_refs[k][0].astype(F32)
                    for s in range(1, N_CHIP):
                        g = g + p_refs[k][s].astype(F32)
                    r, cols = _half_shape(names[k])
                    if _BIG_SPLIT[names[k]] == 0:
                        o_refs[k][e * r:(e + 1) * r, :] = g
                    else:
                        o_refs[k][:, e * cols:(e + 1) * cols] = g
            mine(k).start()
            leave(k, 0).start()
        for k in range(n):
            theirs = _half(o_refs[k], 1 - c, names[k])
            _remote(theirs, theirs, send_sems, recv_sems, k, (x, y, 1 - c)).wait_recv()
            leave(k, 1).start()
        everyone.last([small_ref], [all_ref], ss_small, rs_small)
        g = all_ref[0]
        for d in range(1, N_DEV):
            g = g + all_ref[d]
        osmall_ref[...] = g
        for k in range(n):
            mine(k).wait_send()
            leave(k, 0).wait()
            leave(k, 1).wait()

    shards = [jax.ShapeDtypeStruct(_BIG_SHARD[nm], F32) for nm in names]
    res = pl.pallas_call(
        body, in_specs=[ANY] * n + [VMEM_WHOLE], out_specs=[ANY] * n + [VMEM_WHOLE],
        out_shape=shards + [jax.ShapeDtypeStruct(small.shape, F32)],
        scratch_shapes=[pltpu.VMEM(q.shape, q.dtype) for q in parts] + [pltpu.VMEM(s.shape, s.dtype) for s in shards]
        + [pltpu.VMEM((N_DEV,) + small.shape, F32), pltpu.SemaphoreType.DMA((n,)), pltpu.SemaphoreType.DMA((n,)),
           pltpu.SemaphoreType.DMA((everyone.n_sems,)), pltpu.SemaphoreType.DMA((everyone.n_sems,)),
           pltpu.SemaphoreType.DMA((n,)), pltpu.SemaphoreType.DMA((2 * n,))],
        compiler_params=_params(), name="sum_swap",
    )(*parts, small)
    return res[:n], res[n]


def _tile(rows, cols, itemsize, budget):
    t = cols if rows % 16 else rows
    other = rows if rows % 16 else cols
    step = 256 if rows % 16 else 32
    while t % step == 0 and t * other * itemsize > budget:
        t //= 2
    return (rows, t) if rows % 16 else (t, cols)


def _adamw_math(w, g, m, v):
    m = ADAM_B1 * m + (1.0 - ADAM_B1) * g
    v = ADAM_B2 * v + (1.0 - ADAM_B2) * (g * g)
    m_hat = m / (1.0 - ADAM_B1 ** ADAM_STEP)
    v_hat = v / (1.0 - ADAM_B2 ** ADAM_STEP)
    delta = -ADAM_LR * (m_hat / (jnp.sqrt(v_hat) + ADAM_EPS) + ADAM_WD * w)
    return delta, m, v


def _adamw_big(g, w, m, v, name):
    r, c = w.shape
    tr, tc = _tile(r, c, 4, 2 * 1024 * 1024)

    def body(g_ref, w_ref, m_ref, v_ref, g_out_ref, d_ref, nm_ref, nv_ref):
        g = g_ref[...]
        g_out_ref[...] = g
        d_ref[...], nm_ref[...], nv_ref[...] = _adamw_math(w_ref[...], g, m_ref[...], v_ref[...])

    blk = pl.BlockSpec((tr, tc), lambda i, l: (i, l))
    return pl.pallas_call(
        body, grid=(r // tr, c // tc), in_specs=[blk, blk, blk, blk],
        out_specs=[blk] * 4, out_shape=[jax.ShapeDtypeStruct((r, c), F32)] * 4,
        compiler_params=_params(("arbitrary", "arbitrary")), name=name,
    )(g, w, m, v)


def _adamw_rows(g, w, m, v, name):
    r, k, lanes = w.shape
    tr = 296

    def body(g_ref, w_ref, m_ref, v_ref, g3_ref, d_ref, nm_ref, nv_ref):
        g = g_ref[...].reshape(tr, k, lanes)
        g3_ref[...] = g
        d_ref[...], nm_ref[...], nv_ref[...] = _adamw_math(w_ref[...], g, m_ref[...], v_ref[...])

    rows = pl.BlockSpec((tr, k, lanes), lambda i: (i, 0, 0))
    return pl.pallas_call(
        body, grid=(pl.cdiv(r, tr),), in_specs=[pl.BlockSpec((tr, k * lanes), lambda i: (i, 0)), rows, rows, rows],
        out_specs=[rows] * 4, out_shape=[jax.ShapeDtypeStruct((r, k, lanes), F32)] * 4,
        compiler_params=_params(("arbitrary",)), name=name,
    )(g, w, m, v)


def _adamw_small(ws, gs, ms, vs):
    n = len(ws)

    def body(*refs):
        w_refs, g_refs, m_refs, v_refs, d_refs, nm_refs, nv_refs = _split(refs, *([n] * 7))
        for k in range(n):
            d_refs[k][...], nm_refs[k][...], nv_refs[k][...] = _adamw_math(w_refs[k][...], g_refs[k][...], m_refs[k][...],
                                                                             v_refs[k][...])

    shapes = [jax.ShapeDtypeStruct(w.shape, F32) for w in ws]
    res = pl.pallas_call(body, out_shape=shapes * 3, name="adamw_small")(*ws, *gs, *ms, *vs)
    return res[:n], res[n:2 * n], res[2 * n:]


def _pack(arrs):
    flat = jnp.concatenate([a.reshape(-1) for a in arrs])
    rows = -(-flat.shape[0] // 1024) * 8
    return jnp.pad(flat, (0, rows * 128 - flat.shape[0])).reshape(rows, 128)


def _unpack(buf, shapes):
    flat = buf.reshape(-1)
    out, off = [], 0
    for s in shapes:
        size = 1
        for d in s:
            size *= d
        out.append(flat[off:off + size].reshape(s))
        off += size
    return out


def _block_rows(w):
    return jnp.pad(w.reshape(512, 4), ((0, 0), (0, 124)))


def _block_stored(dw):
    return jnp.transpose(dw[:, 0:4].reshape(128, 4, 4), (1, 2, 0)).reshape(16, 128)


def _cols(a4):
    return jnp.transpose(a4, (1, 0, 2)).reshape(a4.shape[1], -1)


_LATE = ("w_pa", "w_pb", "w_o", "w_up", "w_down")
_RIDE_IN_PROJ = ("w_pa", "w_pb", "w_o")
_RIDE_MIXER = ("w_up", "w_down_a", "w_down_b")


def _full_weights(gathered):
    joined = {"w_o": (D_MODEL, D_MODEL)}
    return {n: (a.reshape(joined[n]) if n in joined else a) for n, a in gathered.items()}


def _local_step(x, target, w, sp, late_shards=None):
    sp = {n: (a.reshape(1, -1) if a.ndim == 1 else a) for n, a in sp.items()}
    wau = jnp.pad(sp["w_a_up"], ((0, 112), (0, 0)))
    wif = jnp.pad(sp["w_if"], ((0, 0), (0, 120)))
    bif = jnp.pad(sp["b_if"], ((0, 0), (0, 120)))
    p = {"wau": wau, "bau": sp["b_a_up"], "ggla": sp["g_gla_norm"], "cw": sp["conv_w"], "cb": sp["conv_b"],
         "wq": _block_rows(sp["w_q_ml"]), "wk": _block_rows(sp["w_k_ml"]), "wv": _block_rows(sp["w_v_ml"]),
         "wif": wif, "bif": bif, "skip": sp["ml_skip"], "gml": sp["g_ml_norm"]}

    if late_shards is None:
        (pm, gab, h), _ = _in_proj(x, sp["g_pre_mix"], w["w_in"])
        ab, x1, mix, merged, *states = _mixer_fwd(pm, p, gab, x, w["w_pa"], w["w_pb"], w["w_o"], sp["g_post_mix"])
    else:
        shard = dict(zip(_LATE, late_shards))
        shard["w_down_a"] = shard["w_down_b"] = shard["w_down"]
        (pm, gab, h), got = _in_proj(x, sp["g_pre_mix"], w["w_in"], _Gather(_RIDE_IN_PROJ, middle_at=0.45),
                                     [shard[n] for n in _RIDE_IN_PROJ])
        w = dict(w, **_full_weights(dict(zip(_RIDE_IN_PROJ, got))))
        ab, x1, mix, merged, *rest = _mixer_fwd(pm, p, gab, x, w["w_pa"], w["w_pb"], w["w_o"], sp["g_post_mix"],
                                                _Gather(_RIDE_MIXER, middle_at=0.75), [shard[n] for n in _RIDE_MIXER])
        states = rest[:4]
        w.update(_full_weights(dict(zip(_RIDE_MIXER, rest[4:]))))
    dx1, u, dd, h2, dpre, dg_post_mlp, dg_pre_mlp, loss = _mlp(x1, target, sp["g_pre_mlp"], sp["g_post_mlp"],
                                                                w["w_up"], w["w_down_a"], w["w_down_b"])
    dgab, dab, dg_post_mix, dw_pa, dw_pb, dw_o = _merge_bwd(dx1, mix, ab, gab, merged, w["w_pa"], w["w_pb"], w["w_o"],
                                                            sp["g_post_mix"])
    big = {"w_pa": dw_pa, "w_pb": dw_pb, "w_o": dw_o, "w_up": _tn_matmul(h2, dpre, "dw_up", shards=N_CHIP)}
    if late_shards is None:
        big["w_down"] = _tn_matmul(u, dd, "dw_down")
        dpm, dp, _ = _mixer_bwd(pm, dab, states, p)
    else:
        pieces = lambda n: big[n].reshape((N_CHIP,) + _BIG_SHARD[n])
        big["w_down"], partial = _tn_matmul(u, dd, "dw_down", rider=_Presum(_LATE[:4]),
                                            rider_ins=[pieces(n) for n in _LATE[:4]])
        dpm, dp, parts = _mixer_bwd(pm, dab, states, p, _PresumThenSend(_LATE), list(partial) + [pieces("w_down")])
        big = dict(zip(_LATE, parts))
    big["w_in"] = _dw_in(dpm, dgab, h)
    if late_shards is None:
        (dx, dg_pre_mix), _ = _in_proj_bwd(dpm, dgab, x, dx1, sp["g_pre_mix"], w["w_in"])
    else:
        (dx, dg_pre_mix), parts = _in_proj_bwd(dpm, dgab, x, dx1, sp["g_pre_mix"], w["w_in"], _PresumThenRelay(("w_in",)),
                                               [big["w_in"]])
        big["w_in"] = parts[0]
    small = {
        "g_pre_mix": dg_pre_mix, "b_a_up": dp["bau"], "g_gla_norm": dp["ggla"], "conv_b": dp["cb"],
        "w_q_ml": _block_stored(dp["wq"]), "w_k_ml": _block_stored(dp["wk"]), "w_v_ml": _block_stored(dp["wv"]),
        "w_if": dp["wif"][:, 0:8].T,
        "b_if": dp["bif"][:, 0:8], "ml_skip": dp["skip"], "g_ml_norm": dp["gml"], "g_post_mix": dg_post_mix,
        "g_pre_mlp": dg_pre_mlp, "g_post_mlp": dg_post_mlp, "w_a_up": dp["wau"][0:16], "conv_w": dp["cw"],
        "loss": loss[:, 0:1],
    }
    return dx, big, small


_SMALL_REPL = ("g_pre_mix", "b_a_up", "g_gla_norm", "conv_b", "w_q_ml", "w_k_ml", "w_v_ml", "b_if", "ml_skip",
               "g_ml_norm", "g_post_mix", "g_pre_mlp", "g_post_mlp")
_SMALL_SHARDED = ("w_a_up", "conv_w", "w_if")
_SMALL_ORDER = _SMALL_REPL + _SMALL_SHARDED + ("loss",)
_WEIGHTS = ("g_pre_mix", "w_in", "w_a_up", "b_a_up", "g_gla_norm", "conv_w", "conv_b", "w_q_ml", "w_k_ml", "w_v_ml",
            "w_if", "b_if", "ml_skip", "g_ml_norm", "w_pa", "w_pb", "w_o", "g_post_mix", "g_pre_mlp", "w_up", "w_down",
            "g_post_mlp")


_BLOCK_WEIGHTS = ("w_q_ml", "w_k_ml", "w_v_ml")


def _stored(name, a):
    if name in _BLOCK_WEIGHTS:
        return jnp.transpose(a, (0, 2, 3, 1)).reshape(16, 128)
    if name == "w_if":
        return jnp.transpose(a, (0, 2, 1)).reshape(8, 384)
    return a


def _unstored(name, a):
    if name in _BLOCK_WEIGHTS:
        return jnp.transpose(a.reshape(1, 4, 4, 128), (0, 3, 1, 2))
    if name == "w_if":
        return jnp.transpose(a.reshape(1, 8, 384), (0, 2, 1))
    return a


def _as_shard(name, a):
    return jnp.transpose(a, (2, 0, 1)).reshape(IN_SHARD, D_MODEL // 128, 128) if name == "w_in" else a[0]


def _in_shard_bf16(w_in):
    return jnp.transpose(w_in.astype(BF16), (2, 0, 1)).reshape(IN_SHARD, D_MODEL)


def _from_shard(name, a):
    return jnp.transpose(a, (1, 2, 0)).reshape(1, D_MODEL, IN_SHARD) if name == "w_in" else a[None]


def kernel(x, g_pre_mix, w_in, w_a_up, b_a_up, g_gla_norm, conv_w, conv_b, w_q_ml, w_k_ml, w_v_ml, w_if, b_if, ml_skip, g_ml_norm, w_pa, w_pb, w_o, g_post_mix, g_pre_mlp, w_up, w_down, g_post_mlp, loss_target, m_g_pre_mix, m_w_in, m_w_a_up, m_b_a_up, m_g_gla_norm, m_conv_w, m_conv_b, m_w_q_ml, m_w_k_ml, m_w_v_ml, m_w_if, m_b_if, m_ml_skip, m_g_ml_norm, m_w_pa, m_w_pb, m_w_o, m_g_post_mix, m_g_pre_mlp, m_w_up, m_w_down, m_g_post_mlp, v_g_pre_mix, v_w_in, v_w_a_up, v_b_a_up, v_g_gla_norm, v_conv_w, v_conv_b, v_w_q_ml, v_w_k_ml, v_w_v_ml, v_w_if, v_b_if, v_ml_skip, v_g_ml_norm, v_w_pa, v_w_pb, v_w_o, v_g_post_mix, v_g_pre_mlp, v_w_up, v_w_down, v_g_post_mlp):
    args = dict(locals())
    wts = {n: _as_shard(n, args[n]) for n in _WEIGHTS}
    mom = {n: _as_shard(n, args["m_" + n]) for n in _WEIGHTS}
    var = {n: _as_shard(n, args["v_" + n]) for n in _WEIGHTS}
    chip = 2 * lax.axis_index("x") + lax.axis_index("y")

    first = ("w_in",) + _SMALL_SHARDED
    gathered = dict(zip(first, _run_alone(_Gather(("w_in",), [wts[n] for n in _SMALL_SHARDED]),
                                          [_in_shard_bf16(w_in)] + [wts[n] for n in _SMALL_SHARDED],
                                          "gather_first")))
    sp = {n: wts[n] for n in _SMALL_REPL}
    sp["w_a_up"] = _cols(gathered["w_a_up"])
    sp["conv_w"] = _cols(gathered["conv_w"])
    sp["w_if"] = gathered["w_if"].reshape(1536, 8)

    dx, big, small = _local_step(x[0], loss_target[0], _full_weights({"w_in": gathered["w_in"]}), sp,
                                 late_shards=[wts[n] for n in _LATE])

    small_shapes = [small[n].shape for n in _SMALL_ORDER]
    packed = _pack([small[n] for n in _SMALL_ORDER])
    sums, small_sum = _sum_swap(_BIG, [big[n] for n in _BIG], packed)

    grads, delta, new_m, new_v = {}, {}, {}, {}
    for n, g in zip(_BIG, sums):
        adamw = _adamw_rows if n == "w_in" else _adamw_big
        g, d, nm, nv = adamw(g, wts[n], mom[n], var[n], "adamw_" + n)
        grads[n], delta[n], new_m[n], new_v[n] = (_from_shard(n, a) for a in (g, d, nm, nv))
    summed = dict(zip(_SMALL_ORDER, _unpack(small_sum, small_shapes)))
    loss = summed["loss"].reshape(())
    summed["w_a_up"] = lax.dynamic_slice_in_dim(summed["w_a_up"], chip * 64, 64, axis=1)
    summed["conv_w"] = lax.dynamic_slice_in_dim(summed["conv_w"], chip * 128, 128, axis=1)
    summed["w_if"] = lax.dynamic_slice_in_dim(summed["w_if"], chip * 384, 384, axis=1)
    small_names = _SMALL_REPL + _SMALL_SHARDED
    came_stored = _BLOCK_WEIGHTS + ("w_if",)
    g_stored = [summed[n] if n in came_stored else _stored(n, summed[n].reshape(args[n].shape)) for n in small_names]
    upd = _adamw_small([_stored(n, args[n]) for n in small_names], g_stored,
                       [_stored(n, args["m_" + n]) for n in small_names], [_stored(n, args["v_" + n]) for n in small_names])
    for dst, arrs in zip((grads, delta, new_m, new_v), (g_stored,) + tuple(upd)):
        dst.update({n: _unstored(n, a) for n, a in zip(small_names, arrs)})

    outs = [loss, dx[None]]
    for group in (grads, delta, new_m, new_v):
        outs += [group[n] for n in _WEIGHTS]
    return tuple(outs)
```

```python
import functools

import jax
import jax.numpy as jnp
from jax import lax
from jax.experimental import pallas as pl
from jax.experimental.pallas import tpu as pltpu

F32 = jnp.float32
BF16 = jnp.bfloat16

SEQ = 2048
D_MODEL = 1024
CHUNK = 64
N_CHUNK = SEQ // CHUNK
HEADS = 4
GLA_DK = 64
GLA_DV = 128
ML_DH = 128
D_FF = 4096
EPS = 1e-6
N_CHIP = 4
N_DEV = 8
TOK_TILE = 256
N_TOK_TILE = SEQ // TOK_TILE
SWEEP = 2
assert CHUNK == 64
N_SWEEP = N_CHUNK // SWEEP

PM_W = 2688
PM_XM = 1536
PM_OP = 2048
PM_AL = 2560
GAB_W = 2048
D_IN = 4624
IN_SHARD = D_IN // N_CHIP
IN_ALOW = 1536
IN_XM = 1552
IN_GATES = 2576

ADAM_LR = 0.001
ADAM_B1 = 0.9
ADAM_B2 = 0.999
ADAM_EPS = 1e-08
ADAM_WD = 0.01
ADAM_STEP = 10

VMEM_LIMIT = 56 * 1024 * 1024


def _params(sem=None):
    return pltpu.CompilerParams(dimension_semantics=sem, vmem_limit_bytes=VMEM_LIMIT)


def _dot(a, b, ca, cb):
    return lax.dot_general(a.astype(BF16), b.astype(BF16), (((ca,), (cb,)), ((), ())), preferred_element_type=F32)


def _pmm_nn(a, b):
    return _dot(a, b, 1, 0)


def _pmm_nt(a, b):
    return _dot(a, b, 1, 1)


def _pmm_tn(a, b):
    return _dot(a, b, 0, 0)


def _pcmm(c, x):
    return lax.dot_general(c, x, (((1,), (0,)), ((), ())), precision=lax.Precision.HIGHEST, preferred_element_type=F32)


@jax.custom_vjp
def _mm_nn(a, b):
    return _dot(a, b, 1, 0)


@jax.custom_vjp
def _mm_nt(a, b):
    return _dot(a, b, 1, 1)


@jax.custom_vjp
def _mm_tn(a, b):
    return _dot(a, b, 0, 0)


_mm_nn.defvjp(lambda a, b: (_dot(a, b, 1, 0), (a, b)), lambda r, g: (_mm_nt(g, r[1]), _mm_tn(r[0], g)))
_mm_nt.defvjp(lambda a, b: (_dot(a, b, 1, 1), (a, b)), lambda r, g: (_mm_nn(g, r[1]), _mm_tn(g, r[0])))
_mm_tn.defvjp(lambda a, b: (_dot(a, b, 0, 0), (a, b)), lambda r, g: (_mm_nt(r[1], g), _mm_nn(r[0], g)))


@jax.custom_vjp
def _cmm(c, x):
    return _pcmm(c, x)


_cmm.defvjp(
    lambda c, x: (_pcmm(c, x), c),
    lambda c, g: (jnp.zeros_like(c), lax.dot_general(c, g, (((0,), (0,)), ((), ())), precision=lax.Precision.HIGHEST,
                                                      preferred_element_type=F32)),
)

_PLAIN_OPS = (_pmm_nn, _pmm_nt, _pmm_tn, _pcmm)
_VJP_OPS = (_mm_nn, _mm_nt, _mm_tn, _cmm)


def _sigmoid(x):
    return 0.5 * (jnp.tanh(0.5 * x) + 1.0)


def _log_sigmoid(x):
    return jnp.minimum(x, 0.0) - jnp.log(1.0 + jnp.exp(-jnp.abs(x)))


def _mean(x):
    return jnp.mean(x, axis=-1, keepdims=True)


def _nt(a, b):
    return lax.dot_general(a, b, (((1,), (1,)), ((), ())), preferred_element_type=F32)


def _tn(a, b):
    return lax.dot_general(a, b, (((0,), (0,)), ((), ())), preferred_element_type=F32)


def _mixer_chunk(ops, p, st, pm, xprev8):
    mm_nn, mm_nt, mm_tn, cmm = ops
    n_rows = pm.shape[0]
    n_ch = n_rows // CHUNK
    row = lax.broadcasted_iota(jnp.int32, (n_rows, n_rows), 0)
    col = lax.broadcasted_iota(jnp.int32, (n_rows, n_rows), 1)
    tri = jnp.logical_and((row >> 6) == (col >> 6), row >= col).astype(F32)
    causal = tri[0:CHUNK, 0:CHUNK] > 0.0
    q = pm[:, 0:256]
    k = pm[:, 256:512]
    v = pm[:, 512:1024]
    g = pm[:, 1024:1536]
    xm = pm[:, PM_XM:PM_XM + 512]
    opre = pm[:, PM_OP:PM_OP + 512]
    alow = pm[:, PM_AL:PM_AL + 128]
    hs = range(HEADS)
    cs = range(n_ch)
    pairs = [(i, h) for i in cs for h in hs]
    rs = [slice(i * CHUNK, (i + 1) * CHUNK) for i in cs]
    last = [slice((i + 1) * CHUNK - 1, (i + 1) * CHUNK) for i in cs]
    s6 = [slice(h * GLA_DK, (h + 1) * GLA_DK) for h in hs]
    s12 = [slice(h * 128, (h + 1) * 128) for h in hs]

    xx = jnp.concatenate([xprev8, xm], axis=0)
    pre = p["cb"]
    for j in range(4):
        pre = pre + p["cw"][j:j + 1, :] * xx[5 + j:5 + j + n_rows, :]
    xc = pre * _sigmoid(pre)
    qm = [mm_nn(xc[:, s12[h]], p["wq"][h]) for h in hs]
    km = [mm_nn(xc[:, s12[h]], p["wk"][h]) for h in hs]
    vm = [mm_nn(xm[:, s12[h]], p["wv"][h]) for h in hs]
    qcat = jnp.concatenate(qm, axis=1)
    kcat = jnp.concatenate(km, axis=1)
    vcat = jnp.concatenate(vm, axis=1)
    gates = (mm_nn(qcat, p["wif"][0:512]) + mm_nn(kcat, p["wif"][512:1024]) + mm_nn(vcat, p["wif"][1024:1536])
             + p["bif"])
    lf = _log_sigmoid(gates)
    fc = cmm(tri, lf)
    gates_t = gates.T
    fc_t = fc.T

    la = _log_sigmoid(mm_nn(alow, p["wau"]) + p["bau"]) * (1.0 / 16.0)
    cum = cmm(tri, la)
    cum_last = [cum[last[i], :] for i in cs]
    to_end = jnp.concatenate([cum_last[i] - cum[rs[i], :] for i in cs], axis=0)
    e_pos = jnp.exp(cum)
    e_neg = jnp.exp(-cum)
    qs = q * (GLA_DK ** -0.5)
    qp = qs * e_pos
    qn = qs * e_neg
    kp = k * e_pos
    kn = k * e_neg
    kl = k * jnp.exp(to_end)
    dec = [jnp.exp(cum_last[i]) for i in cs]
    ks = [km[h] * (ML_DH ** -0.5) for h in hs]
    li_c = {(i, h): gates[rs[i], h:h + 1] for i, h in pairs}
    fc_c = {(i, h): fc[rs[i], 4 + h:5 + h] for i, h in pairs}
    f_last = {(i, h): fc[last[i], 4 + h:5 + h] for i, h in pairs}

    a_fwd = {(i, h): mm_nt(qp[rs[i], s6[h]], kn[rs[i], s6[h]]) for i, h in pairs}
    a_bwd = {(i, h): mm_nt(qn[rs[i], s6[h]], kp[rs[i], s6[h]]) for i, h in pairs}
    s_chunk = {(i, h): mm_tn(v[rs[i], s12[h]], kl[rs[i], s6[h]]) for i, h in pairs}
    qk = {(i, h): mm_nt(qm[h][rs[i]], ks[h][rs[i]]) for i, h in pairs}
    a = {ih: f_last[ih] - fc_c[ih] + li_c[ih] for ih in pairs}
    m_loc = {ih: jnp.max(a[ih], axis=0, keepdims=True) for ih in pairs}
    kw = {(i, h): ks[h][rs[i]] * jnp.exp(a[(i, h)] - m_loc[(i, h)]) for i, h in pairs}
    c_chunk = {(i, h): mm_tn(kw[(i, h)], vm[h][rs[i]]) for i, h in pairs}
    mem = {(0, h): st["S"][h] for h in hs}
    c_in = {(0, h): st["C"][h] for h in hs}
    n_in = {(0, h): st["n"][h] for h in hs}
    m_in = {(0, h): st["m"][h][:, 0:1] for h in hs}
    for i, h in pairs:
        mem[(i + 1, h)] = mem[(i, h)] * dec[i][:, s6[h]] + s_chunk[(i, h)]
        m_nx = jnp.maximum(f_last[(i, h)] + m_in[(i, h)], m_loc[(i, h)])
        sp = jnp.exp(f_last[(i, h)] + m_in[(i, h)] - m_nx)
        sl = jnp.exp(m_loc[(i, h)] - m_nx)
        c_in[(i + 1, h)] = sp * c_in[(i, h)] + sl * c_chunk[(i, h)]
        n_in[(i + 1, h)] = sp * n_in[(i, h)] + sl * jnp.sum(kw[(i, h)], axis=0, keepdims=True)
        m_in[(i + 1, h)] = m_nx
    s_new = [mem[(n_ch, h)] for h in hs]
    o_inter = {(i, h): mm_nt(qp[rs[i], s6[h]], mem[(i, h)]) for i, h in pairs}
    q_c = {(i, h): mm_nn(qm[h][rs[i]], c_in[(i, h)]) for i, h in pairs}
    scores = {ih: jnp.where(causal, a_fwd[ih], a_bwd[ih]) for ih in pairs}
    log_d = {(i, h): gates_t[h:h + 1, rs[i]] - jnp.abs(fc_c[(i, h)] - fc_t[4 + h:5 + h, rs[i]]) for i, h in pairs}
    g_int = {ih: fc_c[ih] + m_in[ih] for ih in pairs}
    m_t = {ih: jnp.maximum(g_int[ih], jnp.max(log_d[ih], axis=1, keepdims=True)) for ih in pairs}
    s = {ih: qk[ih] * jnp.exp(log_d[ih] - m_t[ih]) for ih in pairs}
    scl = {ih: jnp.exp(g_int[ih] - m_t[ih]) for ih in pairs}
    o = {(i, h): mm_nn(scores[(i, h)], v[rs[i], s12[h]]) + o_inter[(i, h)] for i, h in pairs}
    num = {(i, h): mm_nn(s[(i, h)], vm[h][rs[i]]) + scl[(i, h)] * q_c[(i, h)] for i, h in pairs}
    o = {ih: o[ih] * lax.rsqrt(_mean(o[ih] * o[ih]) + EPS) * p["ggla"] for ih in pairs}
    gate = g * _sigmoid(g)
    out_a = {(i, h): o[(i, h)] * gate[rs[i], s12[h]] for i, h in pairs}
    den = {(i, h): jnp.sum(s[(i, h)], axis=1, keepdims=True)
           + scl[(i, h)] * jnp.sum(qm[h][rs[i]] * n_in[(i, h)], axis=1, keepdims=True) for i, h in pairs}
    den = {ih: jnp.maximum(jnp.abs(den[ih]), jnp.exp(-m_t[ih])) for ih in pairs}
    open_gate = _sigmoid(opre)
    hc = {(i, h): num[(i, h)] / den[(i, h)] * open_gate[rs[i], s12[h]] for i, h in pairs}
    d0 = {ih: hc[ih] - _mean(hc[ih]) for ih in pairs}
    y = {ih: d0[ih] * lax.rsqrt(_mean(d0[ih] * d0[ih]) + EPS) for ih in pairs}
    skipped = p["skip"] * xc
    out_b = {(i, h): y[(i, h)] * p["gml"][:, s12[h]] + skipped[rs[i], s12[h]] for i, h in pairs}
    ab = jnp.concatenate([jnp.concatenate([out_a[(i, h)] for h in hs] + [out_b[(i, h)] for h in hs], axis=1) for i in cs],
                         axis=0)
    new = {"S": s_new, "C": [c_in[(n_ch, h)] for h in hs], "n": [n_in[(n_ch, h)] for h in hs],
           "m": [jnp.broadcast_to(m_in[(n_ch, h)], (1, ML_DH)) for h in hs]}
    return ab, new


_P_NAMES = ("wau", "bau", "ggla", "cw", "cb", "wq", "wk", "wv", "wif", "bif", "skip", "gml")
_P_SHAPES = {
    "wau": (128, 256), "bau": (1, 256), "ggla": (1, 128), "cw": (4, 512), "cb": (1, 512),
    "wq": (512, 128), "wk": (512, 128), "wv": (512, 128),
    "wif": (1536, 128), "bif": (1, 128), "skip": (1, 512), "gml": (1, 512),
}
_P_BLOCKDIAG = ("wq", "wk", "wv")
_S_NAMES = ("S", "C", "n", "m")
_S_SHAPES = {"S": (HEADS, GLA_DV, GLA_DK), "C": (HEADS, ML_DH, ML_DH), "n": (HEADS, 1, ML_DH), "m": (HEADS, 1, ML_DH)}


def _per_head(ref):
    return [ref[h] for h in range(HEADS)]


def _block_mask():
    r = lax.broadcasted_iota(jnp.int32, (128, 128), 0)
    c = lax.broadcasted_iota(jnp.int32, (128, 128), 1)
    same_block = (r >> 2) == (c >> 2)
    spread = jnp.logical_and(r < 4, (c & 3) == r)
    return same_block.astype(F32), spread.astype(F32)


def _expand_blockdiag(w_ref, dense_ref):
    same_block, spread = _block_mask()
    for h in range(HEADS):
        tiled = _pmm_nn(w_ref[h * 128:(h + 1) * 128, :], spread)
        dense_ref[h] = tiled * same_block


def _collect_blockdiag(ddense_ref, dw_ref):
    same_block, spread = _block_mask()
    for h in range(HEADS):
        dw_ref[h * 128:(h + 1) * 128, :] = lax.dot_general(
            ddense_ref[h] * same_block, spread, (((1,), (1,)), ((), ())), precision=lax.Precision.HIGHEST,
            preferred_element_type=F32)


def _const_spec(shape):
    zeros = (0,) * len(shape)
    return pl.BlockSpec(shape, lambda i: zeros)


def _split(refs, *counts):
    out, at = [], 0
    for c in counts:
        out.append(refs[at:at + c])
        at += c
    assert at == len(refs)
    return out


def _ride(rider, phases, cond, ins, outs, sems):
    if rider is None or not any(hasattr(rider, phase) for phase in phases):
        return
    lands, (send_sems, recv_sems, flush_sems) = sems[:-3], sems[-3:]

    @pl.when(cond)
    def _():
        for phase in phases:
            if phase == "last" and hasattr(rider, "late"):
                rider.late(ins, lands, send_sems, recv_sems)
                rider.flush("late", lands, outs, flush_sems)
            getattr(rider, phase)(ins, lands, send_sems, recv_sems)
            if hasattr(rider, "flush"):
                rider.flush(phase, lands, outs, flush_sems)
        if "last" in phases and not hasattr(rider, "flush"):
            flush = [pltpu.make_async_copy(lands[k], outs[k], flush_sems.at[k]) for k in range(len(outs))]
            for cp in flush:
                cp.start()
            for cp in flush:
                cp.wait()


def _middle_step(rider, n_steps):
    return min(n_steps - 2, int(getattr(rider, "middle_at", 1.0) * n_steps))


def _rider_specs(rider, rider_ins):
    if rider is None:
        return [], [], [], []
    scratch = [pltpu.VMEM(s.shape, s.dtype) for s in list(rider.out_shape) + list(getattr(rider, "work_shape", ()))]
    scratch += [pltpu.SemaphoreType.DMA((rider.n_sems,)), pltpu.SemaphoreType.DMA((rider.n_sems,)),
                pltpu.SemaphoreType.DMA((getattr(rider, "n_flush", len(rider.out_shape)),))]
    in_space = getattr(rider, "in_space", VMEM_WHOLE)
    in_specs = list(in_space) if isinstance(in_space, (list, tuple)) else [in_space] * len(rider_ins)
    return in_specs, [ANY] * len(rider.out_shape), list(rider.out_shape), scratch


def _merge_tile(ab, gab_ref, x_ref, wpa_ref, wpb_ref, wo_ref, g_ref, x1_ref, mix_ref, mg_ref):
    a = ab[:, 0:512]
    b = ab[:, 512:1024]
    for j in range(N_CHIP):
        blk = slice(j * 256, (j + 1) * 256)
        ya = jnp.dot(a, wpa_ref[j], preferred_element_type=F32)
        yb = jnp.dot(b, wpb_ref[j], preferred_element_type=F32)
        sa = _sigmoid(gab_ref[:, j * 256:(j + 1) * 256])
        sb = _sigmoid(gab_ref[:, 1024 + j * 256:1024 + (j + 1) * 256])
        mg_ref[:, blk] = (sa * ya + sb * yb).astype(BF16)
    mix = jnp.dot(mg_ref[...], wo_ref[...], preferred_element_type=F32)
    mix_ref[...] = mix
    mn, _ = _rms_fwd(mix)
    x1_ref[...] = x_ref[...] + mn * g_ref[...]


def _mixer_fwd(pm, p, gab, x, w_pa4, w_pb4, w_o, g_post, rider=None, rider_ins=()):
    n_p = len(_P_NAMES)
    r_in, r_out_specs, r_out_shape, r_sems = _rider_specs(rider, rider_ins)

    def body(*refs):
        ((pm_ref, xprev_ref), p_list, merge_in, ride_in, (ab_ref,), merge_out, so_refs, ride_out, sc_refs, dense_list,
         sems) = _split(refs, 2, n_p, 6, len(r_in), 1, 3, 4, len(r_out_specs), 4, 3, len(r_sems))
        p_refs = dict(zip(_P_NAMES, p_list))
        dense = dict(zip(_P_BLOCKDIAG, dense_list))
        n = pl.program_id(0)
        _ride(rider, ("first",), n == 0, ride_in, ride_out, sems)

        @pl.when(n == 0)
        def _():
            for r in sc_refs:
                r[...] = jnp.zeros_like(r)
            for nm in _P_BLOCKDIAG:
                _expand_blockdiag(p_refs[nm], dense[nm])

        st = {name: _per_head(r) for name, r in zip(_S_NAMES, sc_refs)}
        pv = {nm: (_per_head(dense[nm]) if nm in _P_BLOCKDIAG else p_refs[nm][...]) for nm in _P_NAMES}
        for name, r in zip(_S_NAMES, so_refs):
            for h in range(HEADS):
                r[0, h] = st[name][h]
        xprev8 = jnp.where(n > 0, xprev_ref[CHUNK - 8:CHUNK, :], 0.0)
        ab, st = _mixer_chunk(_PLAIN_OPS, pv, st, pm_ref[...], xprev8)
        ab = ab.astype(BF16)
        ab_ref[...] = ab
        for name, r in zip(_S_NAMES, sc_refs):
            for h in range(HEADS):
                r[h] = st[name][h]
        _merge_tile(ab, *merge_in, *merge_out)
        _ride(rider, ("middle",), n == _middle_step(rider, N_SWEEP), ride_in, ride_out, sems)
        _ride(rider, ("last",), n == N_SWEEP - 1, ride_in, ride_out, sems)

    rows = lambda width: pl.BlockSpec((SWEEP * CHUNK, width), lambda i: (i, 0))
    in_specs = [rows(PM_W), pl.BlockSpec((CHUNK, 512), lambda i: (jnp.maximum(SWEEP * i - 1, 0), PM_XM // 512))]
    in_specs += [_const_spec(_P_SHAPES[nm]) for nm in _P_NAMES]
    in_specs += [rows(GAB_W), rows(D_MODEL), _once((N_CHIP, 512, 256)), _once((N_CHIP, 512, 256)), _once((D_MODEL, D_MODEL)),
                 _once((1, D_MODEL))] + r_in
    out_specs = [rows(1024), rows(D_MODEL), rows(D_MODEL), rows(D_MODEL)]
    out_shape = [jax.ShapeDtypeStruct((SEQ, 1024), BF16), jax.ShapeDtypeStruct((SEQ, D_MODEL), F32),
                 jax.ShapeDtypeStruct((SEQ, D_MODEL), F32), jax.ShapeDtypeStruct((SEQ, D_MODEL), BF16)]
    for nm in _S_NAMES:
        shp = _S_SHAPES[nm]
        out_specs.append(pl.BlockSpec((1,) + shp, lambda i: (i, 0, 0, 0)))
        out_shape.append(jax.ShapeDtypeStruct((N_SWEEP,) + shp, F32))
    return pl.pallas_call(
        body, grid=(N_SWEEP,), in_specs=in_specs, out_specs=out_specs + r_out_specs, out_shape=out_shape + r_out_shape,
        scratch_shapes=[pltpu.VMEM(_S_SHAPES[nm], F32) for nm in _S_NAMES]
        + [pltpu.VMEM((HEADS, 128, 128), F32) for _ in _P_BLOCKDIAG] + r_sems,
        compiler_params=_params(("arbitrary",)), name="mixer_fwd",
    )(pm, pm, *[p[nm] for nm in _P_NAMES], gab, x, w_pa4, w_pb4, w_o, g_post, *rider_ins)


def _mixer_bwd(pm, dab, states, p, rider=None, rider_ins=()):
    n_p = len(_P_NAMES)
    r_in, r_out_specs, r_out_shape, r_sems = _rider_specs(rider, rider_ins)

    def body(*refs):
        ((pm_ref, xprev_ref, dab_ref), si_refs, p_list, ride_in, (dpm_ref,), dp_list, ride_out, ds_refs, (carry_ref,),
         dense_list, ddense_list, sems) = _split(refs, 3, 4, n_p, len(r_in), 1, n_p, len(r_out_specs), 4, 1, 3, 3, len(r_sems))
        p_refs = dict(zip(_P_NAMES, p_list))
        dp_refs = dict(zip(_P_NAMES, dp_list))
        dense = dict(zip(_P_BLOCKDIAG, dense_list))
        ddense = dict(zip(_P_BLOCKDIAG, ddense_list))
        i = pl.program_id(0)
        blk = N_SWEEP - 1 - i
        _ride(rider, ("first",), i == 0, ride_in, ride_out, sems)

        @pl.when(i == 0)
        def _():
            for r in ds_refs:
                r[...] = jnp.zeros_like(r)
            for nm in _P_NAMES:
                if nm in _P_BLOCKDIAG:
                    ddense[nm][...] = jnp.zeros_like(ddense[nm])
                    _expand_blockdiag(p_refs[nm], dense[nm])
                else:
                    dp_refs[nm][...] = jnp.zeros_like(dp_refs[nm])
            carry_ref[...] = jnp.zeros_like(carry_ref)

        pv = {nm: (_per_head(dense[nm]) if nm in _P_BLOCKDIAG else p_refs[nm][...]) for nm in _P_NAMES}
        dst = {name: _per_head(r) for name, r in zip(_S_NAMES, ds_refs)}
        st = {name: [r[0, h] for h in range(HEADS)] for name, r in zip(_S_NAMES, si_refs)}
        xprev8 = jnp.where(blk > 0, xprev_ref[CHUNK - 8:CHUNK, :], 0.0)
        _, vjp = jax.vjp(functools.partial(_mixer_chunk, _VJP_OPS), pv, st, pm_ref[...], xprev8)
        dp_sum, dst, dpm, dxprev8 = vjp((dab_ref[...], dst))
        reach = jnp.concatenate([jnp.zeros((SWEEP * CHUNK - 8, 512), F32), carry_ref[...]], axis=0)
        dpm_ref[:, 0:PM_XM] = dpm[:, 0:PM_XM].astype(BF16)
        dpm_ref[:, PM_XM:PM_XM + 512] = (dpm[:, PM_XM:PM_XM + 512] + reach).astype(BF16)
        dpm_ref[:, PM_XM + 512:PM_W] = dpm[:, PM_XM + 512:PM_W].astype(BF16)
        carry_ref[...] = dxprev8
        for name, r in zip(_S_NAMES, ds_refs):
            for h in range(HEADS):
                r[h] = dst[name][h]
        for nm in _P_NAMES:
            if nm in _P_BLOCKDIAG:
                for h in range(HEADS):
                    ddense[nm][h] += dp_sum[nm][h]
            else:
                dp_refs[nm][...] += dp_sum[nm]

        @pl.when(i == N_SWEEP - 1)
        def _():
            for nm in _P_BLOCKDIAG:
                _collect_blockdiag(ddense[nm], dp_refs[nm])

        _ride(rider, ("early",), i == 1, ride_in, ride_out, sems)
        _ride(rider, ("middle",), i == _middle_step(rider, N_SWEEP), ride_in, ride_out, sems)
        _ride(rider, ("last",), i == N_SWEEP - 1, ride_in, ride_out, sems)

    rev = lambda i: (N_SWEEP - 1 - i, 0)
    in_specs = [pl.BlockSpec((SWEEP * CHUNK, PM_W), rev),
                pl.BlockSpec((CHUNK, 512), lambda i: (jnp.maximum(SWEEP * (N_SWEEP - 1 - i) - 1, 0), PM_XM // 512)),
                pl.BlockSpec((SWEEP * CHUNK, 1024), rev)]
    for nm in _S_NAMES:
        in_specs.append(pl.BlockSpec((1,) + _S_SHAPES[nm], lambda i: (N_SWEEP - 1 - i, 0, 0, 0)))
    in_specs += [_const_spec(_P_SHAPES[nm]) for nm in _P_NAMES] + r_in
    out_specs = [pl.BlockSpec((SWEEP * CHUNK, PM_W), rev)] + [_const_spec(_P_SHAPES[nm]) for nm in _P_NAMES]
    out_shape = [jax.ShapeDtypeStruct((SEQ, PM_W), BF16)] + [jax.ShapeDtypeStruct(_P_SHAPES[nm], F32) for nm in _P_NAMES]
    res = pl.pallas_call(
        body, grid=(N_SWEEP,), in_specs=in_specs, out_specs=out_specs + r_out_specs, out_shape=out_shape + r_out_shape,
        scratch_shapes=[pltpu.VMEM(_S_SHAPES[nm], F32) for nm in _S_NAMES] + [pltpu.VMEM((8, 512), F32)]
        + [pltpu.VMEM((HEADS, 128, 128), F32) for _ in range(2 * len(_P_BLOCKDIAG))] + r_sems,
        compiler_params=_params(("arbitrary",)), name="mixer_bwd",
    )(pm, pm, dab, *states, *[p[nm] for nm in _P_NAMES], *rider_ins)
    return res[0], dict(zip(_P_NAMES, res[1:1 + n_p])), res[1 + n_p:]


def _tok(width):
    return pl.BlockSpec((TOK_TILE, width), lambda i: (i, 0))


def _once(shape):
    zeros = (0,) * len(shape)
    return pl.BlockSpec(shape, lambda i: zeros, pipeline_mode=pl.Buffered(1))


def _rms_fwd(x):
    r = lax.rsqrt(_mean(x * x) + EPS)
    return x * r, r


def _rms_bwd(dy, xn, r, g):
    gd = dy * g
    return r * (gd - xn * _mean(xn * gd))


def _tiled_call(body, in_specs, out_specs, out_shape, args, name, rider=None, rider_ins=(), scratch=()):
    r_in, r_out_specs, r_out_shape, r_scratch = _rider_specs(rider, rider_ins)
    n_in, n_out = len(in_specs), len(out_specs)

    def hosted(*refs):
        ins, ride_in, outs, ride_out, own, r_scr = _split(refs, n_in, len(r_in), n_out, len(r_out_specs), len(scratch),
                                                          len(r_scratch))
        i = pl.program_id(0)
        _ride(rider, ("first",), i == 0, ride_in, ride_out, r_scr)
        body(*ins, *outs, *own)
        _ride(rider, ("early",), i == 1, ride_in, ride_out, r_scr)
        _ride(rider, ("middle",), i == _middle_step(rider, N_TOK_TILE), ride_in, ride_out, r_scr)
        _ride(rider, ("last",), i == N_TOK_TILE - 1, ride_in, ride_out, r_scr)

    res = pl.pallas_call(
        hosted, grid=(N_TOK_TILE,), in_specs=list(in_specs) + r_in, out_specs=list(out_specs) + r_out_specs,
        out_shape=list(out_shape) + r_out_shape, scratch_shapes=list(scratch) + r_scratch,
        compiler_params=_params(("arbitrary",)), name=name,
    )(*args, *rider_ins)
    return res[:n_out], res[n_out:]


def _join_rows(w4_ref, wt_ref):
    @pl.when(pl.program_id(0) == 0)
    def _():
        for j in range(N_CHIP):
            wt_ref[j * IN_SHARD:(j + 1) * IN_SHARD, :] = w4_ref[j]


def _joined_scratch():
    return [pltpu.VMEM((D_IN, D_MODEL), BF16)]


def _in_proj(x, g_pre, w4_in, rider=None, rider_ins=()):
    def body(x_ref, g_ref, w4_ref, pm_ref, gab_ref, h_ref, wt_ref):
        _join_rows(w4_ref, wt_ref)
        xn, _ = _rms_fwd(x_ref[...])
        h = (xn * g_ref[...]).astype(BF16)
        h_ref[...] = h
        pm_ref[:, 0:PM_XM] = _nt(h, wt_ref[0:IN_ALOW, :])
        pm_ref[:, PM_XM:PM_AL] = _nt(h, wt_ref[IN_XM:IN_GATES, :])
        pm_ref[:, PM_AL:PM_W] = _nt(h, wt_ref[IN_ALOW:IN_ALOW + 128, :])
        gab_ref[...] = _nt(h, wt_ref[IN_GATES:D_IN, :])

    return _tiled_call(
        body, [_tok(D_MODEL), _once((1, D_MODEL)), _once((N_CHIP, IN_SHARD, D_MODEL))],
        [_tok(PM_W), _tok(GAB_W), _tok(D_MODEL)],
        [jax.ShapeDtypeStruct((SEQ, PM_W), F32), jax.ShapeDtypeStruct((SEQ, GAB_W), F32),
         jax.ShapeDtypeStruct((SEQ, D_MODEL), BF16)], (x, g_pre, w4_in), "in_proj", rider, rider_ins, _joined_scratch())


def _mlp(x1, target, g_pre, g_post, w_up4, w_down_a4, w_down_b4):
    def body(x1_ref, t_ref, gpre_ref, gpost_ref, wup_ref, wda_ref, wdb_ref,
             dx1_ref, u_ref, dd_ref, h2_ref, dpre_ref, dgpost_ref, dgpre_ref, loss_ref):
        @pl.when(pl.program_id(0) == 0)
        def _():
            dgpost_ref[...] = jnp.zeros_like(dgpost_ref)
            dgpre_ref[...] = jnp.zeros_like(dgpre_ref)
            loss_ref[...] = jnp.zeros_like(loss_ref)

        x1 = x1_ref[...]
        gpre = gpre_ref[...]
        gpost = gpost_ref[...]
        xn2, r2 = _rms_fwd(x1)
        h2 = (xn2 * gpre).astype(BF16)
        h2_ref[...] = h2
        rl = []
        d = jnp.zeros((TOK_TILE, D_MODEL), F32)
        for j in range(N_CHIP):
            blk = slice(j * 1024, (j + 1) * 1024)
            r = jnp.maximum(jnp.dot(h2, wup_ref[j], preferred_element_type=F32), 0.0)
            rl.append(r)
            u = (r * r).astype(BF16)
            u_ref[:, blk] = u
            d = d + jnp.dot(u[:, 0:512], wda_ref[j], preferred_element_type=F32)
            d = d + jnp.dot(u[:, 512:1024], wdb_ref[j], preferred_element_type=F32)
        dn, r3 = _rms_fwd(d)
        diff = x1 + dn * gpost - t_ref[...]
        loss_ref[...] += jnp.sum(diff * diff, keepdims=True) * (0.5 / D_MODEL)
        dy = diff * (1.0 / D_MODEL)
        dgpost_ref[...] += jnp.sum(dy * dn, axis=0, keepdims=True)
        dd = _rms_bwd(dy, dn, r3, gpost).astype(BF16)
        dd_ref[...] = dd
        dh2 = jnp.zeros((TOK_TILE, D_MODEL), F32)
        for j in range(N_CHIP):
            blk = slice(j * 1024, (j + 1) * 1024)
            du = jnp.concatenate([_nt(dd, wda_ref[j]), _nt(dd, wdb_ref[j])], axis=1)
            dpre = (du * (2.0 * rl[j])).astype(BF16)
            dpre_ref[:, blk] = dpre
            dh2 = dh2 + _nt(dpre, wup_ref[j])
        dgpre_ref[...] += jnp.sum(dh2 * xn2, axis=0, keepdims=True)
        dx1_ref[...] = dy + _rms_bwd(dh2, xn2, r2, gpre)

    acc = pl.BlockSpec((1, D_MODEL), lambda i: (0, 0))
    return pl.pallas_call(
        body, grid=(N_TOK_TILE,),
        in_specs=[_tok(D_MODEL), _tok(D_MODEL), _once((1, D_MODEL)), _once((1, D_MODEL)),
                  _once((N_CHIP, D_MODEL, 1024)), _once((N_CHIP, 512, D_MODEL)), _once((N_CHIP, 512, D_MODEL))],
        out_specs=[_tok(D_MODEL), _tok(D_FF), _tok(D_MODEL), _tok(D_MODEL), _tok(D_FF), acc, acc,
                   pl.BlockSpec((1, 128), lambda i: (0, 0))],
        out_shape=[jax.ShapeDtypeStruct((SEQ, D_MODEL), F32), jax.ShapeDtypeStruct((SEQ, D_FF), BF16),
                   jax.ShapeDtypeStruct((SEQ, D_MODEL), BF16), jax.ShapeDtypeStruct((SEQ, D_MODEL), BF16),
                   jax.ShapeDtypeStruct((SEQ, D_FF), BF16), jax.ShapeDtypeStruct((1, D_MODEL), F32),
                   jax.ShapeDtypeStruct((1, D_MODEL), F32), jax.ShapeDtypeStruct((1, 128), F32)],
        compiler_params=_params(("arbitrary",)), name="mlp_fwd_bwd",
    )(x1, target, g_pre, g_post, w_up4, w_down_a4, w_down_b4)


def _merge_bwd(dx1, mix, ab, gab, merged, w_pa4, w_pb4, w_o, g_post):
    def body(dx1_ref, mix_ref, ab_ref, gab_ref, mg_ref, wpa_ref, wpb_ref, wo_ref, g_ref,
             dgab_ref, dab_ref, dg_ref, dwpa_ref, dwpb_ref, dwo_ref, acc_pa, acc_pb, acc_o):
        @pl.when(pl.program_id(0) == 0)
        def _():
            dg_ref[...] = jnp.zeros_like(dg_ref)
            acc_pa[...] = jnp.zeros_like(acc_pa)
            acc_pb[...] = jnp.zeros_like(acc_pb)
            acc_o[...] = jnp.zeros_like(acc_o)

        dx1 = dx1_ref[...]
        mn, r = _rms_fwd(mix_ref[...])
        dg_ref[...] += jnp.sum(dx1 * mn, axis=0, keepdims=True)
        dmix = _rms_bwd(dx1, mn, r, g_ref[...]).astype(BF16)
        acc_o[...] += _tn(mg_ref[...], dmix)
        dmerged = _nt(dmix, wo_ref[...])
        a = ab_ref[:, 0:512]
        b = ab_ref[:, 512:1024]
        da = jnp.zeros((TOK_TILE, 512), F32)
        db = jnp.zeros((TOK_TILE, 512), F32)
        dyas, dybs = [], []
        for j in range(N_CHIP):
            blk = slice(j * 256, (j + 1) * 256)
            blk_b = slice(1024 + j * 256, 1024 + (j + 1) * 256)
            dm = dmerged[:, blk]
            ya = jnp.dot(a, wpa_ref[j], preferred_element_type=F32)
            yb = jnp.dot(b, wpb_ref[j], preferred_element_type=F32)
            sa = _sigmoid(gab_ref[:, blk])
            sb = _sigmoid(gab_ref[:, blk_b])
            dya = (dm * sa).astype(BF16)
            dyb = (dm * sb).astype(BF16)
            dyas.append(dya)
            dybs.append(dyb)
            dgab_ref[:, blk] = (dm * ya * sa * (1.0 - sa)).astype(BF16)
            dgab_ref[:, blk_b] = (dm * yb * sb * (1.0 - sb)).astype(BF16)
            da = da + _nt(dya, wpa_ref[j])
            db = db + _nt(dyb, wpb_ref[j])
        dab_ref[:, 0:512] = da
        dab_ref[:, 512:1024] = db
        acc_pa[...] += _tn(a, jnp.concatenate(dyas, axis=1))
        acc_pb[...] += _tn(b, jnp.concatenate(dybs, axis=1))

        @pl.when(pl.program_id(0) == N_TOK_TILE - 1)
        def _():
            dwo_ref[...] = acc_o[...].astype(BF16)
            for j in range(N_CHIP):
                dwpa_ref[j] = acc_pa[:, j * 256:(j + 1) * 256].astype(BF16)
                dwpb_ref[j] = acc_pb[:, j * 256:(j + 1) * 256].astype(BF16)

    whole = lambda shape: pl.BlockSpec(shape, lambda i: (0,) * len(shape))
    return pl.pallas_call(
        body, grid=(N_TOK_TILE,),
        in_specs=[_tok(D_MODEL), _tok(D_MODEL), _tok(1024), _tok(GAB_W), _tok(D_MODEL), _once((N_CHIP, 512, 256)),
                  _once((N_CHIP, 512, 256)), _once((D_MODEL, D_MODEL)), _once((1, D_MODEL))],
        out_specs=[_tok(GAB_W), _tok(1024), whole((1, D_MODEL)), whole((N_CHIP, 512, 256)), whole((N_CHIP, 512, 256)),
                   whole((D_MODEL, D_MODEL))],
        out_shape=[jax.ShapeDtypeStruct((SEQ, GAB_W), BF16), jax.ShapeDtypeStruct((SEQ, 1024), F32),
                   jax.ShapeDtypeStruct((1, D_MODEL), F32), jax.ShapeDtypeStruct((N_CHIP, 512, 256), BF16),
                   jax.ShapeDtypeStruct((N_CHIP, 512, 256), BF16), jax.ShapeDtypeStruct((D_MODEL, D_MODEL), BF16)],
        scratch_shapes=[pltpu.VMEM((512, D_MODEL), F32), pltpu.VMEM((512, D_MODEL), F32),
                        pltpu.VMEM((D_MODEL, D_MODEL), F32)],
        compiler_params=_params(("arbitrary",)), name="merge_bwd",
    )(dx1, mix, ab, gab, merged, w_pa4, w_pb4, w_o, g_post)


def _in_proj_bwd(dpm, dgab, x, dx1, g_pre, w4_in, rider=None, rider_ins=()):
    def body(dpm_ref, dgab_ref, x_ref, dx1_ref, g_ref, w4_ref, dx_ref, dg_ref, wt_ref):
        _join_rows(w4_ref, wt_ref)

        @pl.when(pl.program_id(0) == 0)
        def _():
            dg_ref[...] = jnp.zeros_like(dg_ref)

        dh = jnp.dot(dpm_ref[:, 0:PM_XM], wt_ref[0:IN_ALOW, :], preferred_element_type=F32)
        dh = dh + jnp.dot(dpm_ref[:, PM_XM:PM_AL], wt_ref[IN_XM:IN_GATES, :], preferred_element_type=F32)
        dh = dh + jnp.dot(dpm_ref[:, PM_AL:PM_W], wt_ref[IN_ALOW:IN_ALOW + 128, :], preferred_element_type=F32)
        dh = dh + jnp.dot(dgab_ref[...], wt_ref[IN_GATES:D_IN, :], preferred_element_type=F32)
        xn, r = _rms_fwd(x_ref[...])
        dg_ref[...] += jnp.sum(dh * xn, axis=0, keepdims=True)
        dx_ref[...] = dx1_ref[...] + _rms_bwd(dh, xn, r, g_ref[...])

    return _tiled_call(
        body, [_tok(PM_W), _tok(GAB_W), _tok(D_MODEL), _tok(D_MODEL), _once((1, D_MODEL)),
               _once((N_CHIP, IN_SHARD, D_MODEL))],
        [_tok(D_MODEL), pl.BlockSpec((1, D_MODEL), lambda i: (0, 0))],
        [jax.ShapeDtypeStruct((SEQ, D_MODEL), F32), jax.ShapeDtypeStruct((1, D_MODEL), F32)],
        (dpm, dgab, x, dx1, g_pre, w4_in), "in_proj_bwd", rider, rider_ins, _joined_scratch())


def _dw_in(dpm, dgab, h):
    n_pm = PM_AL // 512
    n_blk = n_pm + GAB_W // 512

    def place(o_ref, rows, lo, hi):
        for j in range(N_CHIP):
            a, b = max(lo, j * IN_SHARD), min(hi, (j + 1) * IN_SHARD)
            if a < b:
                o_ref[j, a - j * IN_SHARD:b - j * IN_SHARD, :] = rows(a - lo, b - lo)

    def body(dpm_ref, dgab_ref, dal_ref, h_ref, o_ref, blk_ref):
        i = pl.program_id(0)

        @pl.when(i < n_pm)
        def _():
            blk_ref[...] = _tn(dpm_ref[...], h_ref[...]).astype(BF16)

        @pl.when(i >= n_pm)
        def _():
            blk_ref[...] = _tn(dgab_ref[...], h_ref[...]).astype(BF16)

        for k in range(n_blk):
            off = k * 512 + (IN_XM - IN_ALOW) * (k >= IN_ALOW // 512)

            @pl.when(i == k)
            def _():
                place(o_ref, lambda a, b: blk_ref[a:b, :], off, off + 512)

        @pl.when(i == 0)
        def _():
            a_low = _tn(dal_ref[...], h_ref[...])[0:IN_XM - IN_ALOW].astype(BF16)
            place(o_ref, lambda a, b: a_low[a:b], IN_ALOW, IN_XM)

    return pl.pallas_call(
        body, grid=(n_blk,),
        in_specs=[pl.BlockSpec((SEQ, 512), lambda i: (0, jnp.minimum(i, n_pm - 1))),
                  pl.BlockSpec((SEQ, 512), lambda i: (0, jnp.maximum(i - n_pm, 0))),
                  pl.BlockSpec((SEQ, 128), lambda i: (0, PM_AL // 128)),
                  _once((SEQ, D_MODEL))],
        out_specs=pl.BlockSpec((N_CHIP, IN_SHARD, D_MODEL), lambda i: (0, 0, 0)),
        out_shape=jax.ShapeDtypeStruct((N_CHIP, IN_SHARD, D_MODEL), BF16),
        scratch_shapes=[pltpu.VMEM((512, D_MODEL), BF16)],
        compiler_params=_params(("arbitrary",)), name="dw_in",
    )(dpm, dgab, dpm, h)


def _tn_matmul(a, b, name, shards=1, tm=1024, rider=None, rider_ins=()):
    m, n = a.shape[1], b.shape[1]
    tm = min(tm, m)
    tn = n // shards if shards > 1 else min(n, 1024)
    steps_i, steps_j = m // tm, n // tn
    r_in, r_out_specs, r_out_shape, r_scratch = _rider_specs(rider, rider_ins)

    def body(*refs):
        (a_ref, b_ref), ride_in, (o_ref,), ride_out, scratch = _split(refs, 2, len(r_in), 1, len(r_out_specs), len(r_scratch))
        step = pl.program_id(0) * steps_j + pl.program_id(1)
        _ride(rider, ("first",), step == 0, ride_in, ride_out, scratch)
        o_ref[...] = _tn(a_ref[...], b_ref[...]).astype(BF16)
        _ride(rider, ("middle", "last"), step == steps_i * steps_j - 1, ride_in, ride_out, scratch)

    if shards > 1:
        out_spec = pl.BlockSpec((None, tm, tn), lambda i, j: (j, i, 0))
        out_shape = jax.ShapeDtypeStruct((shards, m, tn), BF16)
    else:
        out_spec = pl.BlockSpec((tm, tn), lambda i, j: (i, j))
        out_shape = jax.ShapeDtypeStruct((m, n), BF16)
    res = pl.pallas_call(
        body, grid=(steps_i, steps_j),
        in_specs=[pl.BlockSpec((SEQ, tm), lambda i, j: (0, i)), pl.BlockSpec((SEQ, tn), lambda i, j: (0, j))] + r_in,
        out_specs=[out_spec] + r_out_specs, out_shape=[out_shape] + r_out_shape, scratch_shapes=r_scratch,
        compiler_params=_params(("arbitrary", "arbitrary")), name=name,
    )(a, b, *rider_ins)
    return res[0] if rider is None else (res[0], res[1:])


MESH = pl.DeviceIdType.MESH
ANY = pl.BlockSpec(memory_space=pl.ANY)
VMEM_WHOLE = pl.BlockSpec(memory_space=pltpu.VMEM)

_BIG = ("w_in", "w_pa", "w_pb", "w_o", "w_up", "w_down")
_BIG_SHARD = {"w_in": (IN_SHARD, D_MODEL), "w_pa": (512, 256), "w_pb": (512, 256), "w_o": (256, D_MODEL),
              "w_up": (D_MODEL, 1024), "w_down": (1024, D_MODEL),
              "w_down_a": (512, D_MODEL), "w_down_b": (512, D_MODEL)}
_BIG_SPLIT = {"w_in": 1, "w_pa": 0, "w_pb": 0, "w_o": 0, "w_up": 0, "w_down": 0, "w_down_a": 0, "w_down_b": 0}


def _half(ref, e, name, lead=0, part=None):
    axis = _BIG_SPLIT[name]
    size = _BIG_SHARD[name][axis] // 2
    start = e * size
    if part is not None:
        size //= 2
        start = start + part * size
    start = pl.multiple_of(start, 128 if axis == 1 else 16)
    idx = [pl.ds(0, ref.shape[a]) for a in range(lead)]
    idx += [pl.ds(start, size), pl.ds(0, _BIG_SHARD[name][1])] if axis == 0 else [pl.ds(0, _BIG_SHARD[name][0]), pl.ds(start, size)]
    return ref.at[tuple(idx)]


def _half_shape(name):
    r, c = _BIG_SHARD[name]
    return (r // 2, c) if _BIG_SPLIT[name] == 0 else (r, c // 2)


def _remote(src, dst, send_sems, recv_sems, k, to):
    return pltpu.make_async_remote_copy(src_ref=src, dst_ref=dst, send_sem=send_sems.at[k], recv_sem=recv_sems.at[k],
                                        device_id=to, device_id_type=MESH)


def _mesh_place():
    x, y, c = lax.axis_index("x"), lax.axis_index("y"), lax.axis_index("c")
    return x, y, c, [(1 - x, y), (x, 1 - y), (1 - x, 1 - y)]


class _Gather:
    def __init__(self, names, small=(), middle_at=0.5):
        self.middle_at = middle_at
        self.names = tuple(names)
        self.nb = len(self.names)
        self.n = self.nb + len(small)
        self.n_sems = 8 * self.nb + 3 * len(small)
        self.n_flush = 6 * self.nb + len(small)
        self.out_shape = [jax.ShapeDtypeStruct((N_CHIP,) + _BIG_SHARD[nm], BF16) for nm in self.names]
        self.out_shape += [jax.ShapeDtypeStruct((N_CHIP,) + s.shape, s.dtype) for s in small]
        self.in_space = [self._in_spec(nm) for nm in self.names] + [VMEM_WHOLE] * len(small)

    @staticmethod
    def _in_spec(name):
        if name in ("w_down_a", "w_down_b"):
            half = 0 if name == "w_down_a" else 1
            return pl.BlockSpec(_BIG_SHARD[name], lambda *_: (half, 0), pipeline_mode=pl.Buffered(1))
        return VMEM_WHOLE

    def _copies(self, ins, outs, ss, rs, k):
        x, y, c, _ = _mesh_place()
        name = self.names[k]
        me, xn, yn, dg = 2 * x + y, 2 * (1 - x) + y, 2 * x + (1 - y), 2 * (1 - x) + (1 - y)
        to_x, to_y, sibling = (1 - x, y, c), (x, 1 - y, c), (x, y, 1 - c)

        def region(slot, e, part=None):
            return _half(outs[k].at[slot], e, name, part=part)

        def copy(pair, src, dst, to):
            return _remote(src, dst, ss, rs, 8 * k + pair, to)

        mine = region(me, c)
        sent = [copy(0, mine, mine, to_x), copy(1, mine, mine, to_y),
                copy(2, region(xn, c, 0), region(xn, c, 0), to_y), copy(3, region(yn, c, 1), region(yn, c, 1), to_x),
                copy(4, region(xn, c), region(xn, c), sibling), copy(5, region(yn, c), region(yn, c), sibling),
                copy(6, region(dg, c, 0), region(dg, c, 0), sibling), copy(7, region(dg, c, 1), region(dg, c, 1), sibling)]
        landing = [region(xn, c), region(yn, c), region(dg, c, 0), region(dg, c, 1),
                   region(xn, 1 - c), region(yn, 1 - c), region(dg, 1 - c, 0), region(dg, 1 - c, 1)]
        received = [copy(pair, dst, dst, sibling) for pair, dst in enumerate(landing)]
        return sent, received

    def _small(self, ins, outs, ss, rs, k, j, peer, slot, c):
        return _remote(ins[k], outs[k].at[slot], ss, rs, 8 * self.nb + 3 * (k - self.nb) + j, (*peer, c))

    def flush(self, phase, lands, outs, fs):
        x, y, c, _ = _mesh_place()
        me, xn, yn, dg = 2 * x + y, 2 * (1 - x) + y, 2 * x + (1 - y), 2 * (1 - x) + (1 - y)

        def pieces(k):
            name = self.names[k]
            spots = [lambda r: r.at[me], lambda r: _half(r.at[xn], c, name), lambda r: _half(r.at[yn], c, name),
                     lambda r: _half(r.at[xn], 1 - c, name), lambda r: _half(r.at[yn], 1 - c, name), lambda r: r.at[dg]]
            return [pltpu.make_async_copy(spot(lands[k]), spot(outs[k]), fs.at[6 * k + t]) for t, spot in enumerate(spots)]

        ready = {"first": (0,), "middle": (1, 2), "late": (3, 4), "last": (5,)}[phase]
        for k in range(self.nb):
            cps = pieces(k)
            for t in ready:
                cps[t].start()
        if phase == "last":
            small = [pltpu.make_async_copy(lands[k], outs[k], fs.at[6 * self.nb + k - self.nb]) for k in range(self.nb, self.n)]
            for cp in small:
                cp.start()
            for k in range(self.nb):
                for cp in pieces(k):
                    cp.wait()
            for cp in small:
                cp.wait()

    def first(self, ins, outs, ss, rs):
        x, y, c, peers = _mesh_place()
        me = 2 * x + y
        for k in range(self.nb):
            outs[k][me] = ins[k][...].astype(BF16)
            sent, _ = self._copies(ins, outs, ss, rs, k)
            sent[0].start()
            sent[1].start()
        for k in range(self.nb, self.n):
            for j, peer in enumerate(peers):
                self._small(ins, outs, ss, rs, k, j, peer, me, c).start()
            outs[k][me] = ins[k][...]

    def middle(self, ins, outs, ss, rs):
        for k in range(self.nb):
            sent, received = self._copies(ins, outs, ss, rs, k)
            for pair in (0, 1):
                received[pair].wait_recv()
                sent[2 + pair].start()
                sent[4 + pair].start()

    def late(self, ins, outs, ss, rs):
        for k in range(self.nb):
            _, received = self._copies(ins, outs, ss, rs, k)
            for pair in (4, 5):
                received[pair].wait_recv()

    def last(self, ins, outs, ss, rs):
        x, y, c, peers = _mesh_place()
        for k in range(self.nb):
            sent, received = self._copies(ins, outs, ss, rs, k)
            for pair in (2, 3):
                received[pair].wait_recv()
                sent[4 + pair].start()
        for k in range(self.nb):
            sent, received = self._copies(ins, outs, ss, rs, k)
            for pair in (6, 7):
                received[pair].wait_recv()
            for cp in sent:
                cp.wait_send()
        for k in range(self.nb, self.n):
            for j, (px, py) in enumerate(peers):
                self._small(ins, outs, ss, rs, k, j, (px, py), 2 * px + py, c).wait_recv()
                self._small(ins, outs, ss, rs, k, j, (px, py), 2 * x + y, c).wait_send()


def _run_alone(rider, ins, name):
    r_in, r_out_specs, r_out_shape, r_scratch = _rider_specs(rider, ins)

    def body(*refs):
        ride_in, ride_out, scratch = _split(refs, len(r_in), len(r_out_specs), len(r_scratch))
        _ride(rider, ("first", "middle", "last"), pl.program_id(0) == 0, ride_in, ride_out, scratch)

    return pl.pallas_call(
        body, grid=(1,), in_specs=r_in, out_specs=r_out_specs, out_shape=r_out_shape, scratch_shapes=r_scratch,
        compiler_params=_params(("arbitrary",)), name=name,
    )(*ins)


class _Presum:
    in_space = ANY

    def __init__(self, names, base=0):
        self.names = tuple(names)
        self.n = len(self.names)
        self.base = base
        self.n_sems = 3 * self.n
        self.out_shape = [jax.ShapeDtypeStruct((N_CHIP,) + _half_shape(nm), BF16) for nm in self.names]
        self.work_shape = self.out_shape + self.out_shape

    def _stage(self, ins, bufs, ss, k, e, which):
        n = self.n
        return pltpu.make_async_copy(_half(ins[k], e, self.names[k], lead=1), bufs[which * n + k],
                                     ss.at[self.base + which * n + k])

    def _give(self, bufs, ss, rs, k, sibling):
        return _remote(bufs[self.n + k], bufs[k], ss, rs, self.base + k, sibling)

    def first(self, ins, bufs, ss, rs):
        x, y, c, _ = _mesh_place()
        for k in range(self.n):
            self._stage(ins, bufs, ss, k, 1 - c, 1).start()
        for k in range(self.n):
            self._stage(ins, bufs, ss, k, c, 2).start()
        for k in range(self.n):
            self._stage(ins, bufs, ss, k, 1 - c, 1).wait()
            self._give(bufs, ss, rs, k, (x, y, 1 - c)).start()

    def middle(self, ins, bufs, ss, rs):
        pass

    def last(self, ins, bufs, ss, rs):
        x, y, c, _ = _mesh_place()
        for k in range(self.n):
            self._give(bufs, ss, rs, k, (x, y, 1 - c)).wait_recv()
            self._stage(ins, bufs, ss, k, c, 2).wait()

            @pl.loop(0, N_CHIP)
            def _(j):
                bufs[k][j] = (bufs[k][j].astype(F32) + bufs[2 * self.n + k][j].astype(F32)).astype(BF16)
        for k in range(self.n):
            self._give(bufs, ss, rs, k, (x, y, 1 - c)).wait_send()


class _ReduceRelay:
    middle_at = 0.75

    def __init__(self, names, base=0):
        self.names = tuple(names)
        self.n = len(self.names)
        self.base = base
        self.n_sems = 6 * self.n
        self.out_shape = [jax.ShapeDtypeStruct((N_CHIP,) + _half_shape(nm), BF16) for nm in self.names]
        quarter = [jax.ShapeDtypeStruct(self._part_shape(nm), BF16) for nm in self.names]
        self.work_shape = quarter + quarter

    @staticmethod
    def _part_shape(name):
        r, c = _half_shape(name)
        return (r // 2, c) if _BIG_SPLIT[name] == 0 else (r, c // 2)

    def _part(self, ref, name, p):
        r, c = self._part_shape(name)
        return ref.at[pl.ds(p * r, r), pl.ds(0, c)] if _BIG_SPLIT[name] == 0 else ref.at[pl.ds(0, r), pl.ds(p * c, c)]

    def _copies(self, ins, bufs, ss, rs, k):
        x, y, c, _ = _mesh_place()
        name, n = self.names[k], self.n
        me, xn, yn, dg = 2 * x + y, 2 * (1 - x) + y, 2 * x + (1 - y), 2 * (1 - x) + (1 - y)
        to_x, to_y = (1 - x, y, c), (x, 1 - y, c)
        mine = lambda slot, p: self._part(ins[k].at[slot], name, p)
        slot = lambda s, p: self._part(bufs[k].at[s], name, p)
        from_x, from_y = bufs[n + k], bufs[2 * n + k]

        def copy(pair, src, dst, to):
            return _remote(src, dst, ss, rs, self.base + 6 * k + pair, to)

        sent = [copy(0, mine(dg, 0), from_x, to_x), copy(1, mine(dg, 1), from_y, to_y),
                copy(2, mine(xn, 0), slot(me, 0), to_x), copy(3, mine(yn, 1), slot(me, 1), to_y),
                copy(4, from_y, slot(me, 1), to_x), copy(5, from_x, slot(me, 0), to_y)]
        landing = [from_x, from_y, slot(xn, 0), slot(yn, 1), slot(xn, 1), slot(yn, 0)]
        received = [copy(pair, dst, dst, to_x) for pair, dst in enumerate(landing)]
        return sent, received

    def first(self, ins, bufs, ss, rs):
        x, y, c, _ = _mesh_place()
        me, dg = 2 * x + y, 2 * (1 - x) + (1 - y)
        for k in range(self.n):
            sent, _ = self._copies(ins, bufs, ss, rs, k)
            for pair in range(4):
                sent[pair].start()
        for k in range(self.n):
            bufs[k][me] = ins[k][me]
            bufs[k][dg] = jnp.zeros(_half_shape(self.names[k]), BF16)

    def middle(self, ins, bufs, ss, rs):
        x, y, c, _ = _mesh_place()
        xn, yn = 2 * (1 - x) + y, 2 * x + (1 - y)
        for k in range(self.n):
            sent, received = self._copies(ins, bufs, ss, rs, k)
            name, n = self.names[k], self.n
            for pair, buf, own in ((0, bufs[n + k], self._part(ins[k].at[yn], name, 0)),
                                   (1, bufs[2 * n + k], self._part(ins[k].at[xn], name, 1))):
                received[pair].wait_recv()
                buf[...] = (buf[...].astype(F32) + own[...].astype(F32)).astype(BF16)
            sent[5].start()
            sent[4].start()

    def last(self, ins, bufs, ss, rs):
        for k in range(self.n):
            sent, received = self._copies(ins, bufs, ss, rs, k)
            for pair in range(2, 6):
                received[pair].wait_recv()
            for cp in sent:
                cp.wait_send()


class _PresumThenRelay:
    in_space = ANY
    middle_at = _ReduceRelay.middle_at

    def __init__(self, names):
        self.relay = _ReduceRelay(names)
        self.pre = _Presum(names, base=self.relay.n_sems)
        self.n_sems = self.relay.n_sems + self.pre.n_sems
        self.out_shape = self.relay.out_shape
        self.work_shape = list(self.relay.work_shape) + list(self.pre.out_shape) + list(self.pre.work_shape)
        self.n_relay = len(self.relay.out_shape) + len(self.relay.work_shape)

    def first(self, ins, bufs, ss, rs):
        self.pre.first(ins, bufs[self.n_relay:], ss, rs)

    def early(self, ins, bufs, ss, rs):
        self.pre.last(ins, bufs[self.n_relay:], ss, rs)
        self.relay.first(bufs[self.n_relay:], bufs[:self.n_relay], ss, rs)

    def middle(self, ins, bufs, ss, rs):
        self.relay.middle(bufs[self.n_relay:], bufs[:self.n_relay], ss, rs)

    def last(self, ins, bufs, ss, rs):
        self.relay.last(bufs[self.n_relay:], bufs[:self.n_relay], ss, rs)


class _SendPartials:
    def __init__(self, names, small_shape=None):
        self.n = len(names)
        self.small = small_shape is not None
        self.n_sems = 3 * self.n + 7
        self.out_shape = [jax.ShapeDtypeStruct((N_CHIP,) + _half_shape(nm), BF16) for nm in names]
        if self.small:
            self.out_shape.append(jax.ShapeDtypeStruct((N_DEV,) + small_shape, F32))

    def _piece(self, ins, outs, ss, rs, k, j, peer, src_slot, dst_slot, c):
        return _remote(ins[k].at[src_slot], outs[k].at[dst_slot], ss, rs, 3 * k + j, (*peer, c))

    def _small(self, ins, outs, ss, rs, r, other, slot):
        return _remote(ins[self.n], outs[self.n].at[slot], ss, rs, 3 * self.n + r, other)

    @staticmethod
    def _others(x, y, c):
        return [(x, y, 1 - c), (1 - x, y, c), (1 - x, y, 1 - c), (x, 1 - y, c), (x, 1 - y, 1 - c),
                (1 - x, 1 - y, c), (1 - x, 1 - y, 1 - c)]

    def first(self, ins, outs, ss, rs, only=None):
        x, y, c, peers = _mesh_place()
        me = 2 * x + y
        which = range(self.n) if only is None else only
        for k in which:
            for j, (px, py) in enumerate(peers):
                self._piece(ins, outs, ss, rs, k, j, (px, py), 2 * px + py, me, c).start()
        if self.small:
            for r, other in enumerate(self._others(x, y, c)):
                self._small(ins, outs, ss, rs, r, other, 4 * x + 2 * y + c).start()
            outs[self.n][4 * x + 2 * y + c] = ins[self.n][...]
        for k in which:
            outs[k][me] = ins[k][me]

    def middle(self, ins, outs, ss, rs):
        pass

    def last(self, ins, outs, ss, rs):
        x, y, c, peers = _mesh_place()
        me = 2 * x + y
        for k in range(self.n):
            for j, (px, py) in enumerate(peers):
                self._piece(ins, outs, ss, rs, k, j, (px, py), me, 2 * px + py, c).wait_recv()
                self._piece(ins, outs, ss, rs, k, j, (px, py), 2 * px + py, me, c).wait_send()
        if self.small:
            for r, (px, py, pc) in enumerate(self._others(x, y, c)):
                self._small(ins, outs, ss, rs, r, (px, py, pc), 4 * px + 2 * py + pc).wait_recv()
                self._small(ins, outs, ss, rs, r, (px, py, pc), 4 * x + 2 * y + c).wait_send()


class _PresumThenSend:
    def __init__(self, names):
        self.send = _SendPartials(names)
        self.pre = _Presum(names[-1:], base=self.send.n_sems)
        self.n = self.send.n
        self.n_sems = self.send.n_sems + self.pre.n_sems
        self.out_shape = self.send.out_shape
        self.work_shape = list(self.pre.out_shape) + list(self.pre.work_shape)
        self.in_space = [VMEM_WHOLE] * (self.n - 1) + [ANY]

    def _partials(self, ins, bufs):
        return list(ins[:self.n - 1]) + [bufs[self.n]]

    def first(self, ins, bufs, ss, rs):
        self.pre.first(ins[self.n - 1:], bufs[self.n:], ss, rs)
        self.send.first(ins, bufs[:self.n], ss, rs, only=range(self.n - 1))

    def early(self, ins, bufs, ss, rs):
        self.pre.last(ins[self.n - 1:], bufs[self.n:], ss, rs)
        self.send.first(self._partials(ins, bufs), bufs[:self.n], ss, rs, only=(self.n - 1,))

    def middle(self, ins, bufs, ss, rs):
        pass

    def last(self, ins, bufs, ss, rs):
        self.send.last(self._partials(ins, bufs), bufs[:self.n], ss, rs)


def _sum_swap(names, parts, small):
    n = len(parts)
    everyone = _SendPartials((), small.shape)

    def body(*refs):
        (p_hbm, (small_ref,), o_hbm, (osmall_ref,), p_refs, o_refs, (all_ref,),
         (send_sems, recv_sems, ss_small, rs_small, load_sems, leave_sems)) = _split(refs, n, 1, n, 1, n, n, 1, 6)
        x, y, c = lax.axis_index("x"), lax.axis_index("y"), lax.axis_index("c")
        loads = [pltpu.make_async_copy(p_hbm[k], p_refs[k], load_sems.at[k]) for k in range(n)]
        for cp in loads:
            cp.start()
        everyone.first([small_ref], [all_ref], ss_small, rs_small)

        def mine(k):
            part = _half(o_refs[k], c, names[k])
            return _remote(part, part, send_sems, recv_sems, k, (x, y, 1 - c))

        def leave(k, whose):
            e = c if whose == 0 else 1 - c
            return pltpu.make_async_copy(_half(o_refs[k], e, names[k]), _half(o_hbm[k], e, names[k]),
                                         leave_sems.at[2 * k + whose])

        for k in range(n):
            loads[k].wait()
            for e in range(2):
                @pl.when(c == e)
                def _():
                    g = p_refs[k][0].astype(F32)
                    for s in range(1, N_CHIP):
                        g = g + p_refs[k][s].astype(F32)
                    r, cols = _half_shape(names[k])
                    if _BIG_SPLIT[names[k]] == 0:
                        o_refs[k][e * r:(e + 1) * r, :] = g
                    else:
                        o_refs[k][:, e * cols:(e + 1) * cols] = g
            mine(k).start()
            leave(k, 0).start()
        for k in range(n):
            theirs = _half(o_refs[k], 1 - c, names[k])
            _remote(theirs, theirs, send_sems, recv_sems, k, (x, y, 1 - c)).wait_recv()
            leave(k, 1).start()
        everyone.last([small_ref], [all_ref], ss_small, rs_small)
        g = all_ref[0]
        for d in range(1, N_DEV):
            g = g + all_ref[d]
        osmall_ref[...] = g
        for k in range(n):
            mine(k).wait_send()
            leave(k, 0).wait()
            leave(k, 1).wait()

    shards = [jax.ShapeDtypeStruct(_BIG_SHARD[nm], F32) for nm in names]
    res = pl.pallas_call(
        body, in_specs=[ANY] * n + [VMEM_WHOLE], out_specs=[ANY] * n + [VMEM_WHOLE],
        out_shape=shards + [jax.ShapeDtypeStruct(small.shape, F32)],
        scratch_shapes=[pltpu.VMEM(q.shape, q.dtype) for q in parts] + [pltpu.VMEM(s.shape, s.dtype) for s in shards]
        + [pltpu.VMEM((N_DEV,) + small.shape, F32), pltpu.SemaphoreType.DMA((n,)), pltpu.SemaphoreType.DMA((n,)),
           pltpu.SemaphoreType.DMA((everyone.n_sems,)), pltpu.SemaphoreType.DMA((everyone.n_sems,)),
           pltpu.SemaphoreType.DMA((n,)), pltpu.SemaphoreType.DMA((2 * n,))],
        compiler_params=_params(), name="sum_swap",
    )(*parts, small)
    return res[:n], res[n]


def _adamw_math(w, g, m, v):
    m = ADAM_B1 * m + (1.0 - ADAM_B1) * g
    v = ADAM_B2 * v + (1.0 - ADAM_B2) * (g * g)
    m_hat = m / (1.0 - ADAM_B1 ** ADAM_STEP)
    v_hat = v / (1.0 - ADAM_B2 ** ADAM_STEP)
    delta = -ADAM_LR * (m_hat / (jnp.sqrt(v_hat) + ADAM_EPS) + ADAM_WD * w)
    return delta, m, v


ADAMW_STEPS = 8


def _adamw_big(gs, ws, ms, vs, name):
    n = len(ws)

    def body(*refs):
        for k in range(n):
            g_ref, w_ref, m_ref, v_ref = refs[4 * k:4 * k + 4]
            g_out_ref, d_ref, nm_ref, nv_ref = refs[4 * (n + k):4 * (n + k) + 4]
            g = g_ref[...]
            g_out_ref[...] = g
            d_ref[...], nm_ref[...], nv_ref[...] = _adamw_math(w_ref[...], g, m_ref[...], v_ref[...])

    specs, shapes, args = [], [], []
    for g, w, m, v in zip(gs, ws, ms, vs):
        r, c = w.shape
        assert r % (8 * ADAMW_STEPS) == 0, w.shape
        specs += [pl.BlockSpec((r // ADAMW_STEPS, c), lambda i: (i, 0))] * 4
        shapes += [jax.ShapeDtypeStruct((r, c), F32)] * 4
        args += [g, w, m, v]
    res = pl.pallas_call(
        body, grid=(ADAMW_STEPS,), in_specs=specs, out_specs=specs, out_shape=shapes,
        compiler_params=_params(("arbitrary",)), name=name,
    )(*args)
    return [tuple(res[4 * k:4 * k + 4]) for k in range(n)]


def _adamw_rows(g, w, m, v, name):
    r, k, lanes = w.shape
    tr = 296

    def body(g_ref, w_ref, m_ref, v_ref, g3_ref, d_ref, nm_ref, nv_ref):
        g = g_ref[...].reshape(tr, k, lanes)
        g3_ref[...] = g
        d_ref[...], nm_ref[...], nv_ref[...] = _adamw_math(w_ref[...], g, m_ref[...], v_ref[...])

    rows = pl.BlockSpec((tr, k, lanes), lambda i: (i, 0, 0))
    return pl.pallas_call(
        body, grid=(pl.cdiv(r, tr),), in_specs=[pl.BlockSpec((tr, k * lanes), lambda i: (i, 0)), rows, rows, rows],
        out_specs=[rows] * 4, out_shape=[jax.ShapeDtypeStruct((r, k, lanes), F32)] * 4,
        compiler_params=_params(("arbitrary",)), name=name,
    )(g, w, m, v)


def _adamw_small(ws, gs, ms, vs):
    n = len(ws)

    def body(*refs):
        w_refs, g_refs, m_refs, v_refs, d_refs, nm_refs, nv_refs = _split(refs, *([n] * 7))
        for k in range(n):
            d_refs[k][...], nm_refs[k][...], nv_refs[k][...] = _adamw_math(w_refs[k][...], g_refs[k][...], m_refs[k][...],
                                                                             v_refs[k][...])

    shapes = [jax.ShapeDtypeStruct(w.shape, F32) for w in ws]
    res = pl.pallas_call(body, out_shape=shapes * 3, name="adamw_small")(*ws, *gs, *ms, *vs)
    return res[:n], res[n:2 * n], res[2 * n:]


def _pack(arrs):
    flat = jnp.concatenate([a.reshape(-1) for a in arrs])
    rows = -(-flat.shape[0] // 1024) * 8
    return jnp.pad(flat, (0, rows * 128 - flat.shape[0])).reshape(rows, 128)


def _unpack(buf, shapes):
    flat = buf.reshape(-1)
    out, off = [], 0
    for s in shapes:
        size = 1
        for d in s:
            size *= d
        out.append(flat[off:off + size].reshape(s))
        off += size
    return out


def _block_rows(w):
    return jnp.pad(w.reshape(512, 4), ((0, 0), (0, 124)))


def _block_stored(dw):
    return jnp.transpose(dw[:, 0:4].reshape(128, 4, 4), (1, 2, 0)).reshape(16, 128)


def _cols(a4):
    return jnp.transpose(a4, (1, 0, 2)).reshape(a4.shape[1], -1)


_LATE = ("w_pa", "w_pb", "w_o", "w_up", "w_down")
_RIDE_IN_PROJ = ("w_pa", "w_pb", "w_o", "w_down_b")
_RIDE_MIXER = ("w_up", "w_down_a")


def _full_weights(gathered):
    joined = {"w_o": (D_MODEL, D_MODEL)}
    return {n: (a.reshape(joined[n]) if n in joined else a) for n, a in gathered.items()}


def _local_step(x, target, w, sp, late_shards=None):
    sp = {n: (a.reshape(1, -1) if a.ndim == 1 else a) for n, a in sp.items()}
    wau = jnp.pad(sp["w_a_up"], ((0, 112), (0, 0)))
    wif = jnp.pad(sp["w_if"], ((0, 0), (0, 120)))
    bif = jnp.pad(sp["b_if"], ((0, 0), (0, 120)))
    p = {"wau": wau, "bau": sp["b_a_up"], "ggla": sp["g_gla_norm"], "cw": sp["conv_w"], "cb": sp["conv_b"],
         "wq": _block_rows(sp["w_q_ml"]), "wk": _block_rows(sp["w_k_ml"]), "wv": _block_rows(sp["w_v_ml"]),
         "wif": wif, "bif": bif, "skip": sp["ml_skip"], "gml": sp["g_ml_norm"]}

    if late_shards is None:
        (pm, gab, h), _ = _in_proj(x, sp["g_pre_mix"], w["w_in"])
        ab, x1, mix, merged, *states = _mixer_fwd(pm, p, gab, x, w["w_pa"], w["w_pb"], w["w_o"], sp["g_post_mix"])
    else:
        shard = dict(zip(_LATE, late_shards))
        shard["w_down_a"] = shard["w_down_b"] = shard["w_down"]
        (pm, gab, h), got = _in_proj(x, sp["g_pre_mix"], w["w_in"], _Gather(_RIDE_IN_PROJ, middle_at=0.7),
                                     [shard[n] for n in _RIDE_IN_PROJ])
        w = dict(w, **_full_weights(dict(zip(_RIDE_IN_PROJ, got))))
        ab, x1, mix, merged, *rest = _mixer_fwd(pm, p, gab, x, w["w_pa"], w["w_pb"], w["w_o"], sp["g_post_mix"],
                                                _Gather(_RIDE_MIXER, middle_at=0.62), [shard[n] for n in _RIDE_MIXER])
        states = rest[:4]
        w.update(_full_weights(dict(zip(_RIDE_MIXER, rest[4:]))))
    dx1, u, dd, h2, dpre, dg_post_mlp, dg_pre_mlp, loss = _mlp(x1, target, sp["g_pre_mlp"], sp["g_post_mlp"],
                                                                w["w_up"], w["w_down_a"], w["w_down_b"])
    dgab, dab, dg_post_mix, dw_pa, dw_pb, dw_o = _merge_bwd(dx1, mix, ab, gab, merged, w["w_pa"], w["w_pb"], w["w_o"],
                                                            sp["g_post_mix"])
    big = {"w_pa": dw_pa, "w_pb": dw_pb, "w_o": dw_o, "w_up": _tn_matmul(h2, dpre, "dw_up", shards=N_CHIP)}
    if late_shards is None:
        big["w_down"] = _tn_matmul(u, dd, "dw_down")
        dpm, dp, _ = _mixer_bwd(pm, dab, states, p)
    else:
        pieces = lambda n: big[n].reshape((N_CHIP,) + _BIG_SHARD[n])
        big["w_down"], partial = _tn_matmul(u, dd, "dw_down", rider=_Presum(_LATE[:4]),
                                            rider_ins=[pieces(n) for n in _LATE[:4]])
        dpm, dp, parts = _mixer_bwd(pm, dab, states, p, _PresumThenSend(_LATE), list(partial) + [pieces("w_down")])
        big = dict(zip(_LATE, parts))
    big["w_in"] = _dw_in(dpm, dgab, h)
    if late_shards is None:
        (dx, dg_pre_mix), _ = _in_proj_bwd(dpm, dgab, x, dx1, sp["g_pre_mix"], w["w_in"])
    else:
        (dx, dg_pre_mix), parts = _in_proj_bwd(dpm, dgab, x, dx1, sp["g_pre_mix"], w["w_in"], _PresumThenRelay(("w_in",)),
                                               [big["w_in"]])
        big["w_in"] = parts[0]
    small = {
        "g_pre_mix": dg_pre_mix, "b_a_up": dp["bau"], "g_gla_norm": dp["ggla"], "conv_b": dp["cb"],
        "w_q_ml": _block_stored(dp["wq"]), "w_k_ml": _block_stored(dp["wk"]), "w_v_ml": _block_stored(dp["wv"]),
        "w_if": dp["wif"][:, 0:8].T,
        "b_if": dp["bif"][:, 0:8], "ml_skip": dp["skip"], "g_ml_norm": dp["gml"], "g_post_mix": dg_post_mix,
        "g_pre_mlp": dg_pre_mlp, "g_post_mlp": dg_post_mlp, "w_a_up": dp["wau"][0:16], "conv_w": dp["cw"],
        "loss": loss[:, 0:1],
    }
    return dx, big, small


_SMALL_REPL = ("g_pre_mix", "b_a_up", "g_gla_norm", "conv_b", "w_q_ml", "w_k_ml", "w_v_ml", "b_if", "ml_skip",
               "g_ml_norm", "g_post_mix", "g_pre_mlp", "g_post_mlp")
_SMALL_SHARDED = ("w_a_up", "conv_w", "w_if")
_SMALL_ORDER = _SMALL_REPL + _SMALL_SHARDED + ("loss",)
_WEIGHTS = ("g_pre_mix", "w_in", "w_a_up", "b_a_up", "g_gla_norm", "conv_w", "conv_b", "w_q_ml", "w_k_ml", "w_v_ml",
            "w_if", "b_if", "ml_skip", "g_ml_norm", "w_pa", "w_pb", "w_o", "g_post_mix", "g_pre_mlp", "w_up", "w_down",
            "g_post_mlp")


_BLOCK_WEIGHTS = ("w_q_ml", "w_k_ml", "w_v_ml")


def _stored(name, a):
    if name in _BLOCK_WEIGHTS:
        return jnp.transpose(a, (0, 2, 3, 1)).reshape(16, 128)
    if name == "w_if":
        return jnp.transpose(a, (0, 2, 1)).reshape(8, 384)
    return a


def _unstored(name, a):
    if name in _BLOCK_WEIGHTS:
        return jnp.transpose(a.reshape(1, 4, 4, 128), (0, 3, 1, 2))
    if name == "w_if":
        return jnp.transpose(a.reshape(1, 8, 384), (0, 2, 1))
    return a


def _as_shard(name, a):
    return jnp.transpose(a, (2, 0, 1)).reshape(IN_SHARD, D_MODEL // 128, 128) if name == "w_in" else a[0]


def _in_shard_bf16(w_in):
    return jnp.transpose(w_in.astype(BF16), (2, 0, 1)).reshape(IN_SHARD, D_MODEL)


def _from_shard(name, a):
    return jnp.transpose(a, (1, 2, 0)).reshape(1, D_MODEL, IN_SHARD) if name == "w_in" else a[None]


def kernel(x, g_pre_mix, w_in, w_a_up, b_a_up, g_gla_norm, conv_w, conv_b, w_q_ml, w_k_ml, w_v_ml, w_if, b_if, ml_skip, g_ml_norm, w_pa, w_pb, w_o, g_post_mix, g_pre_mlp, w_up, w_down, g_post_mlp, loss_target, m_g_pre_mix, m_w_in, m_w_a_up, m_b_a_up, m_g_gla_norm, m_conv_w, m_conv_b, m_w_q_ml, m_w_k_ml, m_w_v_ml, m_w_if, m_b_if, m_ml_skip, m_g_ml_norm, m_w_pa, m_w_pb, m_w_o, m_g_post_mix, m_g_pre_mlp, m_w_up, m_w_down, m_g_post_mlp, v_g_pre_mix, v_w_in, v_w_a_up, v_b_a_up, v_g_gla_norm, v_conv_w, v_conv_b, v_w_q_ml, v_w_k_ml, v_w_v_ml, v_w_if, v_b_if, v_ml_skip, v_g_ml_norm, v_w_pa, v_w_pb, v_w_o, v_g_post_mix, v_g_pre_mlp, v_w_up, v_w_down, v_g_post_mlp):
    args = dict(locals())
    wts = {n: _as_shard(n, args[n]) for n in _WEIGHTS}
    mom = {n: _as_shard(n, args["m_" + n]) for n in _WEIGHTS}
    var = {n: _as_shard(n, args["v_" + n]) for n in _WEIGHTS}
    chip = 2 * lax.axis_index("x") + lax.axis_index("y")

    first = ("w_in",) + _SMALL_SHARDED
    gathered = dict(zip(first, _run_alone(_Gather(("w_in",), [wts[n] for n in _SMALL_SHARDED]),
                                          [_in_shard_bf16(w_in)] + [wts[n] for n in _SMALL_SHARDED],
                                          "gather_first")))
    sp = {n: wts[n] for n in _SMALL_REPL}
    sp["w_a_up"] = _cols(gathered["w_a_up"])
    sp["conv_w"] = _cols(gathered["conv_w"])
    sp["w_if"] = gathered["w_if"].reshape(1536, 8)

    dx, big, small = _local_step(x[0], loss_target[0], _full_weights({"w_in": gathered["w_in"]}), sp,
                                 late_shards=[wts[n] for n in _LATE])

    small_shapes = [small[n].shape for n in _SMALL_ORDER]
    packed = _pack([small[n] for n in _SMALL_ORDER])
    sums, small_sum = _sum_swap(_BIG, [big[n] for n in _BIG], packed)

    grads, delta, new_m, new_v = {}, {}, {}, {}
    g_sum = dict(zip(_BIG, sums))
    updated = {"w_in": _adamw_rows(g_sum["w_in"], wts["w_in"], mom["w_in"], var["w_in"], "adamw_w_in")}
    updated.update(zip(_LATE, _adamw_big([g_sum[n] for n in _LATE], [wts[n] for n in _LATE], [mom[n] for n in _LATE],
                                         [var[n] for n in _LATE], "adamw_late")))
    for n in _BIG:
        grads[n], delta[n], new_m[n], new_v[n] = (_from_shard(n, a) for a in updated[n])
    summed = dict(zip(_SMALL_ORDER, _unpack(small_sum, small_shapes)))
    loss = summed["loss"].reshape(())
    summed["w_a_up"] = lax.dynamic_slice_in_dim(summed["w_a_up"], chip * 64, 64, axis=1)
    summed["conv_w"] = lax.dynamic_slice_in_dim(summed["conv_w"], chip * 128, 128, axis=1)
    summed["w_if"] = lax.dynamic_slice_in_dim(summed["w_if"], chip * 384, 384, axis=1)
    small_names = _SMALL_REPL + _SMALL_SHARDED
    came_stored = _BLOCK_WEIGHTS + ("w_if",)
    g_stored = [summed[n] if n in came_stored else _stored(n, summed[n].reshape(args[n].shape)) for n in small_names]
    upd = _adamw_small([_stored(n, args[n]) for n in small_names], g_stored,
                       [_stored(n, args["m_" + n]) for n in small_names], [_stored(n, args["v_" + n]) for n in small_names])
    for dst, arrs in zip((grads, delta, new_m, new_v), (g_stored,) + tuple(upd)):
        dst.update({n: _unstored(n, a) for n, a in zip(small_names, arrs)})

    outs = [loss, dx[None]]
    for group in (grads, delta, new_m, new_v):
        outs += [group[n] for n in _WEIGHTS]
    return tuple(outs)
```

```python
import functools

import jax
import jax.numpy as jnp
from jax import lax
from jax.experimental import pallas as pl
from jax.experimental.pallas import tpu as pltpu

F32 = jnp.float32
BF16 = jnp.bfloat16

SEQ = 2048
D_MODEL = 1024
CHUNK = 64
N_CHUNK = SEQ // CHUNK
HEADS = 4
GLA_DK = 64
GLA_DV = 128
ML_DH = 128
D_FF = 4096
EPS = 1e-6
N_CHIP = 4
N_DEV = 8
TOK_TILE = 256
N_TOK_TILE = SEQ // TOK_TILE
SWEEP = 2
assert CHUNK == 64
N_SWEEP = N_CHUNK // SWEEP

PM_W = 2688
PM_XM = 1536
PM_OP = 2048
PM_AL = 2560
GAB_W = 2048
D_IN = 4624
IN_SHARD = D_IN // N_CHIP
IN_ALOW = 1536
IN_XM = 1552
IN_GATES = 2576

ADAM_LR = 0.001
ADAM_B1 = 0.9
ADAM_B2 = 0.999
ADAM_EPS = 1e-08
ADAM_WD = 0.01
ADAM_STEP = 10

VMEM_LIMIT = 56 * 1024 * 1024


def _params(sem=None):
    return pltpu.CompilerParams(dimension_semantics=sem, vmem_limit_bytes=VMEM_LIMIT)


def _dot(a, b, ca, cb):
    return lax.dot_general(a.astype(BF16), b.astype(BF16), (((ca,), (cb,)), ((), ())), preferred_element_type=F32)


def _pmm_nn(a, b):
    return _dot(a, b, 1, 0)


def _pmm_nt(a, b):
    return _dot(a, b, 1, 1)


def _pmm_tn(a, b):
    return _dot(a, b, 0, 0)


def _pcmm(c, x):
    return lax.dot_general(c, x, (((1,), (0,)), ((), ())), precision=lax.Precision.HIGHEST, preferred_element_type=F32)


@jax.custom_vjp
def _mm_nn(a, b):
    return _dot(a, b, 1, 0)


@jax.custom_vjp
def _mm_nt(a, b):
    return _dot(a, b, 1, 1)


@jax.custom_vjp
def _mm_tn(a, b):
    return _dot(a, b, 0, 0)


_mm_nn.defvjp(lambda a, b: (_dot(a, b, 1, 0), (a, b)), lambda r, g: (_mm_nt(g, r[1]), _mm_tn(r[0], g)))
_mm_nt.defvjp(lambda a, b: (_dot(a, b, 1, 1), (a, b)), lambda r, g: (_mm_nn(g, r[1]), _mm_tn(g, r[0])))
_mm_tn.defvjp(lambda a, b: (_dot(a, b, 0, 0), (a, b)), lambda r, g: (_mm_nt(r[1], g), _mm_nn(r[0], g)))


@jax.custom_vjp
def _cmm(c, x):
    return _pcmm(c, x)


_cmm.defvjp(
    lambda c, x: (_pcmm(c, x), c),
    lambda c, g: (jnp.zeros_like(c), lax.dot_general(c, g, (((0,), (0,)), ((), ())), precision=lax.Precision.HIGHEST,
                                                      preferred_element_type=F32)),
)

_PLAIN_OPS = (_pmm_nn, _pmm_nt, _pmm_tn, _pcmm)
_VJP_OPS = (_mm_nn, _mm_nt, _mm_tn, _cmm)


def _sigmoid(x):
    return 0.5 * (jnp.tanh(0.5 * x) + 1.0)


def _log_sigmoid(x):
    return jnp.minimum(x, 0.0) - jnp.log(1.0 + jnp.exp(-jnp.abs(x)))


def _mean(x):
    return jnp.mean(x, axis=-1, keepdims=True)


def _nt(a, b):
    return lax.dot_general(a, b, (((1,), (1,)), ((), ())), preferred_element_type=F32)


def _tn(a, b):
    return lax.dot_general(a, b, (((0,), (0,)), ((), ())), preferred_element_type=F32)


def _mixer_chunk(ops, p, st, pm, xprev8):
    mm_nn, mm_nt, mm_tn, cmm = ops
    n_rows = pm.shape[0]
    n_ch = n_rows // CHUNK
    row = lax.broadcasted_iota(jnp.int32, (n_rows, n_rows), 0)
    col = lax.broadcasted_iota(jnp.int32, (n_rows, n_rows), 1)
    tri = jnp.logical_and((row >> 6) == (col >> 6), row >= col).astype(F32)
    causal = tri[0:CHUNK, 0:CHUNK] > 0.0
    q = pm[:, 0:256]
    k = pm[:, 256:512]
    v = pm[:, 512:1024]
    g = pm[:, 1024:1536]
    xm = pm[:, PM_XM:PM_XM + 512]
    opre = pm[:, PM_OP:PM_OP + 512]
    alow = pm[:, PM_AL:PM_AL + 128]
    hs = range(HEADS)
    cs = range(n_ch)
    pairs = [(i, h) for i in cs for h in hs]
    rs = [slice(i * CHUNK, (i + 1) * CHUNK) for i in cs]
    last = [slice((i + 1) * CHUNK - 1, (i + 1) * CHUNK) for i in cs]
    s6 = [slice(h * GLA_DK, (h + 1) * GLA_DK) for h in hs]
    s12 = [slice(h * 128, (h + 1) * 128) for h in hs]

    xx = jnp.concatenate([xprev8, xm], axis=0)
    pre = p["cb"]
    for j in range(4):
        pre = pre + p["cw"][j:j + 1, :] * xx[5 + j:5 + j + n_rows, :]
    xc = pre * _sigmoid(pre)
    qm = [mm_nn(xc[:, s12[h]], p["wq"][h]) for h in hs]
    km = [mm_nn(xc[:, s12[h]], p["wk"][h]) for h in hs]
    vm = [mm_nn(xm[:, s12[h]], p["wv"][h]) for h in hs]
    qcat = jnp.concatenate(qm, axis=1)
    kcat = jnp.concatenate(km, axis=1)
    vcat = jnp.concatenate(vm, axis=1)
    gates = (mm_nn(qcat, p["wif"][0:512]) + mm_nn(kcat, p["wif"][512:1024]) + mm_nn(vcat, p["wif"][1024:1536])
             + p["bif"])
    lf = _log_sigmoid(gates)
    fc = cmm(tri, lf)
    gates_t = gates.T
    fc_t = fc.T

    la = _log_sigmoid(mm_nn(alow, p["wau"]) + p["bau"]) * (1.0 / 16.0)
    cum = cmm(tri, la)
    cum_last = [cum[last[i], :] for i in cs]
    to_end = jnp.concatenate([cum_last[i] - cum[rs[i], :] for i in cs], axis=0)
    e_pos = jnp.exp(cum)
    e_neg = jnp.exp(-cum)
    qs = q * (GLA_DK ** -0.5)
    qp = qs * e_pos
    qn = qs * e_neg
    kp = k * e_pos
    kn = k * e_neg
    kl = k * jnp.exp(to_end)
    dec = [jnp.exp(cum_last[i]) for i in cs]
    ks = [km[h] * (ML_DH ** -0.5) for h in hs]
    li_c = {(i, h): gates[rs[i], h:h + 1] for i, h in pairs}
    fc_c = {(i, h): fc[rs[i], 4 + h:5 + h] for i, h in pairs}
    f_last = {(i, h): fc[last[i], 4 + h:5 + h] for i, h in pairs}

    a_fwd = {(i, h): mm_nt(qp[rs[i], s6[h]], kn[rs[i], s6[h]]) for i, h in pairs}
    a_bwd = {(i, h): mm_nt(qn[rs[i], s6[h]], kp[rs[i], s6[h]]) for i, h in pairs}
    s_chunk = {(i, h): mm_tn(v[rs[i], s12[h]], kl[rs[i], s6[h]]) for i, h in pairs}
    qk = {(i, h): mm_nt(qm[h][rs[i]], ks[h][rs[i]]) for i, h in pairs}
    a = {ih: f_last[ih] - fc_c[ih] + li_c[ih] for ih in pairs}
    m_loc = {ih: jnp.max(a[ih], axis=0, keepdims=True) for ih in pairs}
    kw = {(i, h): ks[h][rs[i]] * jnp.exp(a[(i, h)] - m_loc[(i, h)]) for i, h in pairs}
    c_chunk = {(i, h): mm_tn(kw[(i, h)], vm[h][rs[i]]) for i, h in pairs}
    mem = {(0, h): st["S"][h] for h in hs}
    c_in = {(0, h): st["C"][h] for h in hs}
    n_in = {(0, h): st["n"][h] for h in hs}
    m_in = {(0, h): st["m"][h][:, 0:1] for h in hs}
    for i, h in pairs:
        mem[(i + 1, h)] = mem[(i, h)] * dec[i][:, s6[h]] + s_chunk[(i, h)]
        m_nx = jnp.maximum(f_last[(i, h)] + m_in[(i, h)], m_loc[(i, h)])
        sp = jnp.exp(f_last[(i, h)] + m_in[(i, h)] - m_nx)
        sl = jnp.exp(m_loc[(i, h)] - m_nx)
        c_in[(i + 1, h)] = sp * c_in[(i, h)] + sl * c_chunk[(i, h)]
        n_in[(i + 1, h)] = sp * n_in[(i, h)] + sl * jnp.sum(kw[(i, h)], axis=0, keepdims=True)
        m_in[(i + 1, h)] = m_nx
    s_new = [mem[(n_ch, h)] for h in hs]
    o_inter = {(i, h): mm_nt(qp[rs[i], s6[h]], mem[(i, h)]) for i, h in pairs}
    q_c = {(i, h): mm_nn(qm[h][rs[i]], c_in[(i, h)]) for i, h in pairs}
    scores = {ih: jnp.where(causal, a_fwd[ih], a_bwd[ih]) for ih in pairs}
    log_d = {(i, h): gates_t[h:h + 1, rs[i]] - jnp.abs(fc_c[(i, h)] - fc_t[4 + h:5 + h, rs[i]]) for i, h in pairs}
    g_int = {ih: fc_c[ih] + m_in[ih] for ih in pairs}
    m_t = {ih: jnp.maximum(g_int[ih], jnp.max(log_d[ih], axis=1, keepdims=True)) for ih in pairs}
    s = {ih: qk[ih] * jnp.exp(log_d[ih] - m_t[ih]) for ih in pairs}
    scl = {ih: jnp.exp(g_int[ih] - m_t[ih]) for ih in pairs}
    o = {(i, h): mm_nn(scores[(i, h)], v[rs[i], s12[h]]) + o_inter[(i, h)] for i, h in pairs}
    num = {(i, h): mm_nn(s[(i, h)], vm[h][rs[i]]) + scl[(i, h)] * q_c[(i, h)] for i, h in pairs}
    o = {ih: o[ih] * lax.rsqrt(_mean(o[ih] * o[ih]) + EPS) * p["ggla"] for ih in pairs}
    gate = g * _sigmoid(g)
    out_a = {(i, h): o[(i, h)] * gate[rs[i], s12[h]] for i, h in pairs}
    den = {(i, h): jnp.sum(s[(i, h)], axis=1, keepdims=True)
           + scl[(i, h)] * jnp.sum(qm[h][rs[i]] * n_in[(i, h)], axis=1, keepdims=True) for i, h in pairs}
    den = {ih: jnp.maximum(jnp.abs(den[ih]), jnp.exp(-m_t[ih])) for ih in pairs}
    open_gate = _sigmoid(opre)
    hc = {(i, h): num[(i, h)] / den[(i, h)] * open_gate[rs[i], s12[h]] for i, h in pairs}
    d0 = {ih: hc[ih] - _mean(hc[ih]) for ih in pairs}
    y = {ih: d0[ih] * lax.rsqrt(_mean(d0[ih] * d0[ih]) + EPS) for ih in pairs}
    skipped = p["skip"] * xc
    out_b = {(i, h): y[(i, h)] * p["gml"][:, s12[h]] + skipped[rs[i], s12[h]] for i, h in pairs}
    ab = jnp.concatenate([jnp.concatenate([out_a[(i, h)] for h in hs] + [out_b[(i, h)] for h in hs], axis=1) for i in cs],
                         axis=0)
    new = {"S": s_new, "C": [c_in[(n_ch, h)] for h in hs], "n": [n_in[(n_ch, h)] for h in hs],
           "m": [jnp.broadcast_to(m_in[(n_ch, h)], (1, ML_DH)) for h in hs]}
    return ab, new


_P_NAMES = ("wau", "bau", "ggla", "cw", "cb", "wq", "wk", "wv", "wif", "bif", "skip", "gml")
_P_SHAPES = {
    "wau": (128, 256), "bau": (1, 256), "ggla": (1, 128), "cw": (4, 512), "cb": (1, 512),
    "wq": (512, 128), "wk": (512, 128), "wv": (512, 128),
    "wif": (1536, 128), "bif": (1, 128), "skip": (1, 512), "gml": (1, 512),
}
_P_BLOCKDIAG = ("wq", "wk", "wv")
_S_NAMES = ("S", "C", "n", "m")
_S_SHAPES = {"S": (HEADS, GLA_DV, GLA_DK), "C": (HEADS, ML_DH, ML_DH), "n": (HEADS, 1, ML_DH), "m": (HEADS, 1, ML_DH)}


def _per_head(ref):
    return [ref[h] for h in range(HEADS)]


def _block_mask():
    r = lax.broadcasted_iota(jnp.int32, (128, 128), 0)
    c = lax.broadcasted_iota(jnp.int32, (128, 128), 1)
    same_block = (r >> 2) == (c >> 2)
    spread = jnp.logical_and(r < 4, (c & 3) == r)
    return same_block.astype(F32), spread.astype(F32)


def _expand_blockdiag(w_ref, dense_ref):
    same_block, spread = _block_mask()
    for h in range(HEADS):
        tiled = _pmm_nn(w_ref[h * 128:(h + 1) * 128, :], spread)
        dense_ref[h] = tiled * same_block


def _collect_blockdiag(ddense_ref, dw_ref):
    same_block, spread = _block_mask()
    for h in range(HEADS):
        dw_ref[h * 128:(h + 1) * 128, :] = lax.dot_general(
            ddense_ref[h] * same_block, spread, (((1,), (1,)), ((), ())), precision=lax.Precision.HIGHEST,
            preferred_element_type=F32)


def _const_spec(shape):
    zeros = (0,) * len(shape)
    return pl.BlockSpec(shape, lambda i: zeros)


def _split(refs, *counts):
    out, at = [], 0
    for c in counts:
        out.append(refs[at:at + c])
        at += c
    assert at == len(refs)
    return out


def _ride(rider, phases, cond, ins, outs, sems):
    if rider is None or not any(hasattr(rider, phase) for phase in phases):
        return
    lands, (send_sems, recv_sems, flush_sems) = sems[:-3], sems[-3:]

    @pl.when(cond)
    def _():
        for phase in phases:
            if phase == "last" and hasattr(rider, "late"):
                rider.late(ins, lands, send_sems, recv_sems)
                rider.flush("late", lands, outs, flush_sems)
            getattr(rider, phase)(ins, lands, send_sems, recv_sems)
            if hasattr(rider, "flush"):
                rider.flush(phase, lands, outs, flush_sems)
        if "last" in phases and not hasattr(rider, "flush"):
            flush = [pltpu.make_async_copy(lands[k], outs[k], flush_sems.at[k]) for k in range(len(outs))]
            for cp in flush:
                cp.start()
            for cp in flush:
                cp.wait()


def _middle_step(rider, n_steps):
    return min(n_steps - 2, int(getattr(rider, "middle_at", 1.0) * n_steps))


def _rider_specs(rider, rider_ins):
    if rider is None:
        return [], [], [], []
    scratch = [pltpu.VMEM(s.shape, s.dtype) for s in list(rider.out_shape) + list(getattr(rider, "work_shape", ()))]
    scratch += [pltpu.SemaphoreType.DMA((rider.n_sems,)), pltpu.SemaphoreType.DMA((rider.n_sems,)),
                pltpu.SemaphoreType.DMA((getattr(rider, "n_flush", len(rider.out_shape)),))]
    in_space = getattr(rider, "in_space", VMEM_WHOLE)
    in_specs = list(in_space) if isinstance(in_space, (list, tuple)) else [in_space] * len(rider_ins)
    return in_specs, [ANY] * len(rider.out_shape), list(rider.out_shape), scratch


def _merge_tile(ab, gab_ref, x_ref, wpa_ref, wpb_ref, wo_ref, g_ref, x1_ref, mix_ref, mg_ref):
    a = ab[:, 0:512]
    b = ab[:, 512:1024]
    for j in range(N_CHIP):
        blk = slice(j * 256, (j + 1) * 256)
        ya = jnp.dot(a, wpa_ref[j], preferred_element_type=F32)
        yb = jnp.dot(b, wpb_ref[j], preferred_element_type=F32)
        sa = _sigmoid(gab_ref[:, j * 256:(j + 1) * 256])
        sb = _sigmoid(gab_ref[:, 1024 + j * 256:1024 + (j + 1) * 256])
        mg_ref[:, blk] = (sa * ya + sb * yb).astype(BF16)
    mix = jnp.dot(mg_ref[...], wo_ref[...], preferred_element_type=F32)
    mix_ref[...] = mix
    mn, _ = _rms_fwd(mix)
    x1_ref[...] = x_ref[...] + mn * g_ref[...]


def _mixer_fwd(pm, p, gab, x, w_pa4, w_pb4, w_o, g_post, rider=None, rider_ins=()):
    n_p = len(_P_NAMES)
    r_in, r_out_specs, r_out_shape, r_sems = _rider_specs(rider, rider_ins)

    def body(*refs):
        ((pm_ref, xprev_ref), p_list, merge_in, ride_in, (ab_ref,), merge_out, so_refs, ride_out, sc_refs, dense_list,
         sems) = _split(refs, 2, n_p, 6, len(r_in), 1, 3, 4, len(r_out_specs), 4, 3, len(r_sems))
        p_refs = dict(zip(_P_NAMES, p_list))
        dense = dict(zip(_P_BLOCKDIAG, dense_list))
        n = pl.program_id(0)
        _ride(rider, ("first",), n == 0, ride_in, ride_out, sems)

        @pl.when(n == 0)
        def _():
            for r in sc_refs:
                r[...] = jnp.zeros_like(r)
            for nm in _P_BLOCKDIAG:
                _expand_blockdiag(p_refs[nm], dense[nm])

        st = {name: _per_head(r) for name, r in zip(_S_NAMES, sc_refs)}
        pv = {nm: (_per_head(dense[nm]) if nm in _P_BLOCKDIAG else p_refs[nm][...]) for nm in _P_NAMES}
        for name, r in zip(_S_NAMES, so_refs):
            for h in range(HEADS):
                r[0, h] = st[name][h]
        xprev8 = jnp.where(n > 0, xprev_ref[CHUNK - 8:CHUNK, :], 0.0)
        ab, st = _mixer_chunk(_PLAIN_OPS, pv, st, pm_ref[...], xprev8)
        ab = ab.astype(BF16)
        ab_ref[...] = ab
        for name, r in zip(_S_NAMES, sc_refs):
            for h in range(HEADS):
                r[h] = st[name][h]
        _merge_tile(ab, *merge_in, *merge_out)
        _ride(rider, ("middle",), n == _middle_step(rider, N_SWEEP), ride_in, ride_out, sems)
        _ride(rider, ("last",), n == N_SWEEP - 1, ride_in, ride_out, sems)

    rows = lambda width: pl.BlockSpec((SWEEP * CHUNK, width), lambda i: (i, 0))
    in_specs = [rows(PM_W), pl.BlockSpec((CHUNK, 512), lambda i: (jnp.maximum(SWEEP * i - 1, 0), PM_XM // 512))]
    in_specs += [_const_spec(_P_SHAPES[nm]) for nm in _P_NAMES]
    in_specs += [rows(GAB_W), rows(D_MODEL), _once((N_CHIP, 512, 256)), _once((N_CHIP, 512, 256)), _once((D_MODEL, D_MODEL)),
                 _once((1, D_MODEL))] + r_in
    out_specs = [rows(1024), rows(D_MODEL), rows(D_MODEL), rows(D_MODEL)]
    out_shape = [jax.ShapeDtypeStruct((SEQ, 1024), BF16), jax.ShapeDtypeStruct((SEQ, D_MODEL), F32),
                 jax.ShapeDtypeStruct((SEQ, D_MODEL), F32), jax.ShapeDtypeStruct((SEQ, D_MODEL), BF16)]
    for nm in _S_NAMES:
        shp = _S_SHAPES[nm]
        out_specs.append(pl.BlockSpec((1,) + shp, lambda i: (i, 0, 0, 0)))
        out_shape.append(jax.ShapeDtypeStruct((N_SWEEP,) + shp, F32))
    return pl.pallas_call(
        body, grid=(N_SWEEP,), in_specs=in_specs, out_specs=out_specs + r_out_specs, out_shape=out_shape + r_out_shape,
        scratch_shapes=[pltpu.VMEM(_S_SHAPES[nm], F32) for nm in _S_NAMES]
        + [pltpu.VMEM((HEADS, 128, 128), F32) for _ in _P_BLOCKDIAG] + r_sems,
        compiler_params=_params(("arbitrary",)), name="mixer_fwd",
    )(pm, pm, *[p[nm] for nm in _P_NAMES], gab, x, w_pa4, w_pb4, w_o, g_post, *rider_ins)


def _mixer_bwd(pm, dab, states, p, rider=None, rider_ins=()):
    n_p = len(_P_NAMES)
    r_in, r_out_specs, r_out_shape, r_sems = _rider_specs(rider, rider_ins)

    def body(*refs):
        ((pm_ref, xprev_ref, dab_ref), si_refs, p_list, ride_in, (dpm_ref,), dp_list, ride_out, ds_refs, (carry_ref,),
         dense_list, ddense_list, sems) = _split(refs, 3, 4, n_p, len(r_in), 1, n_p, len(r_out_specs), 4, 1, 3, 3, len(r_sems))
        p_refs = dict(zip(_P_NAMES, p_list))
        dp_refs = dict(zip(_P_NAMES, dp_list))
        dense = dict(zip(_P_BLOCKDIAG, dense_list))
        ddense = dict(zip(_P_BLOCKDIAG, ddense_list))
        i = pl.program_id(0)
        blk = N_SWEEP - 1 - i
        _ride(rider, ("first",), i == 0, ride_in, ride_out, sems)

        @pl.when(i == 0)
        def _():
            for r in ds_refs:
                r[...] = jnp.zeros_like(r)
            for nm in _P_NAMES:
                if nm in _P_BLOCKDIAG:
                    ddense[nm][...] = jnp.zeros_like(ddense[nm])
                    _expand_blockdiag(p_refs[nm], dense[nm])
                else:
                    dp_refs[nm][...] = jnp.zeros_like(dp_refs[nm])
            carry_ref[...] = jnp.zeros_like(carry_ref)

        pv = {nm: (_per_head(dense[nm]) if nm in _P_BLOCKDIAG else p_refs[nm][...]) for nm in _P_NAMES}
        dst = {name: _per_head(r) for name, r in zip(_S_NAMES, ds_refs)}
        st = {name: [r[0, h] for h in range(HEADS)] for name, r in zip(_S_NAMES, si_refs)}
        xprev8 = jnp.where(blk > 0, xprev_ref[CHUNK - 8:CHUNK, :], 0.0)
        _, vjp = jax.vjp(functools.partial(_mixer_chunk, _VJP_OPS), pv, st, pm_ref[...], xprev8)
        dp_sum, dst, dpm, dxprev8 = vjp((dab_ref[...], dst))
        reach = jnp.concatenate([jnp.zeros((SWEEP * CHUNK - 8, 512), F32), carry_ref[...]], axis=0)
        dpm_ref[:, 0:PM_XM] = dpm[:, 0:PM_XM].astype(BF16)
        dpm_ref[:, PM_XM:PM_XM + 512] = (dpm[:, PM_XM:PM_XM + 512] + reach).astype(BF16)
        dpm_ref[:, PM_XM + 512:PM_W] = dpm[:, PM_XM + 512:PM_W].astype(BF16)
        carry_ref[...] = dxprev8
        for name, r in zip(_S_NAMES, ds_refs):
            for h in range(HEADS):
                r[h] = dst[name][h]
        for nm in _P_NAMES:
            if nm in _P_BLOCKDIAG:
                for h in range(HEADS):
                    ddense[nm][h] += dp_sum[nm][h]
            else:
                dp_refs[nm][...] += dp_sum[nm]

        @pl.when(i == N_SWEEP - 1)
        def _():
            for nm in _P_BLOCKDIAG:
                _collect_blockdiag(ddense[nm], dp_refs[nm])

        _ride(rider, ("early",), i == 1, ride_in, ride_out, sems)
        _ride(rider, ("middle",), i == _middle_step(rider, N_SWEEP), ride_in, ride_out, sems)
        _ride(rider, ("last",), i == N_SWEEP - 1, ride_in, ride_out, sems)

    rev = lambda i: (N_SWEEP - 1 - i, 0)
    in_specs = [pl.BlockSpec((SWEEP * CHUNK, PM_W), rev),
                pl.BlockSpec((CHUNK, 512), lambda i: (jnp.maximum(SWEEP * (N_SWEEP - 1 - i) - 1, 0), PM_XM // 512)),
                pl.BlockSpec((SWEEP * CHUNK, 1024), rev)]
    for nm in _S_NAMES:
        in_specs.append(pl.BlockSpec((1,) + _S_SHAPES[nm], lambda i: (N_SWEEP - 1 - i, 0, 0, 0)))
    in_specs += [_const_spec(_P_SHAPES[nm]) for nm in _P_NAMES] + r_in
    out_specs = [pl.BlockSpec((SWEEP * CHUNK, PM_W), rev)] + [_const_spec(_P_SHAPES[nm]) for nm in _P_NAMES]
    out_shape = [jax.ShapeDtypeStruct((SEQ, PM_W), BF16)] + [jax.ShapeDtypeStruct(_P_SHAPES[nm], F32) for nm in _P_NAMES]
    res = pl.pallas_call(
        body, grid=(N_SWEEP,), in_specs=in_specs, out_specs=out_specs + r_out_specs, out_shape=out_shape + r_out_shape,
        scratch_shapes=[pltpu.VMEM(_S_SHAPES[nm], F32) for nm in _S_NAMES] + [pltpu.VMEM((8, 512), F32)]
        + [pltpu.VMEM((HEADS, 128, 128), F32) for _ in range(2 * len(_P_BLOCKDIAG))] + r_sems,
        compiler_params=_params(("arbitrary",)), name="mixer_bwd",
    )(pm, pm, dab, *states, *[p[nm] for nm in _P_NAMES], *rider_ins)
    return res[0], dict(zip(_P_NAMES, res[1:1 + n_p])), res[1 + n_p:]


def _tok(width):
    return pl.BlockSpec((TOK_TILE, width), lambda i: (i, 0))


def _once(shape):
    zeros = (0,) * len(shape)
    return pl.BlockSpec(shape, lambda i: zeros, pipeline_mode=pl.Buffered(1))


def _rms_fwd(x):
    r = lax.rsqrt(_mean(x * x) + EPS)
    return x * r, r


def _rms_bwd(dy, xn, r, g):
    gd = dy * g
    return r * (gd - xn * _mean(xn * gd))


def _tiled_call(body, in_specs, out_specs, out_shape, args, name, rider=None, rider_ins=(), scratch=()):
    r_in, r_out_specs, r_out_shape, r_scratch = _rider_specs(rider, rider_ins)
    n_in, n_out = len(in_specs), len(out_specs)

    def hosted(*refs):
        ins, ride_in, outs, ride_out, own, r_scr = _split(refs, n_in, len(r_in), n_out, len(r_out_specs), len(scratch),
                                                          len(r_scratch))
        i = pl.program_id(0)
        _ride(rider, ("first",), i == 0, ride_in, ride_out, r_scr)
        body(*ins, *outs, *own)
        _ride(rider, ("early",), i == 1, ride_in, ride_out, r_scr)
        _ride(rider, ("middle",), i == _middle_step(rider, N_TOK_TILE), ride_in, ride_out, r_scr)
        _ride(rider, ("last",), i == N_TOK_TILE - 1, ride_in, ride_out, r_scr)

    res = pl.pallas_call(
        hosted, grid=(N_TOK_TILE,), in_specs=list(in_specs) + r_in, out_specs=list(out_specs) + r_out_specs,
        out_shape=list(out_shape) + r_out_shape, scratch_shapes=list(scratch) + r_scratch,
        compiler_params=_params(("arbitrary",)), name=name,
    )(*args, *rider_ins)
    return res[:n_out], res[n_out:]


def _join_rows(w4_ref, wt_ref):
    @pl.when(pl.program_id(0) == 0)
    def _():
        for j in range(N_CHIP):
            wt_ref[j * IN_SHARD:(j + 1) * IN_SHARD, :] = w4_ref[j]


def _joined_scratch():
    return [pltpu.VMEM((D_IN, D_MODEL), BF16)]


def _in_proj(x, g_pre, w4_in, rider=None, rider_ins=()):
    def body(x_ref, g_ref, w4_ref, pm_ref, gab_ref, h_ref, wt_ref):
        _join_rows(w4_ref, wt_ref)
        xn, _ = _rms_fwd(x_ref[...])
        h = (xn * g_ref[...]).astype(BF16)
        h_ref[...] = h
        pm_ref[:, 0:PM_XM] = _nt(h, wt_ref[0:IN_ALOW, :])
        pm_ref[:, PM_XM:PM_AL] = _nt(h, wt_ref[IN_XM:IN_GATES, :])
        pm_ref[:, PM_AL:PM_W] = _nt(h, wt_ref[IN_ALOW:IN_ALOW + 128, :])
        gab_ref[...] = _nt(h, wt_ref[IN_GATES:D_IN, :])

    return _tiled_call(
        body, [_tok(D_MODEL), _once((1, D_MODEL)), _once((N_CHIP, IN_SHARD, D_MODEL))],
        [_tok(PM_W), _tok(GAB_W), _tok(D_MODEL)],
        [jax.ShapeDtypeStruct((SEQ, PM_W), F32), jax.ShapeDtypeStruct((SEQ, GAB_W), F32),
         jax.ShapeDtypeStruct((SEQ, D_MODEL), BF16)], (x, g_pre, w4_in), "in_proj", rider, rider_ins, _joined_scratch())


def _mlp(x1, target, g_pre, g_post, w_up4, w_down_a4, w_down_b4):
    def body(x1_ref, t_ref, gpre_ref, gpost_ref, wup_ref, wda_ref, wdb_ref,
             dx1_ref, u_ref, dd_ref, h2_ref, dpre_ref, dgpost_ref, dgpre_ref, loss_ref):
        @pl.when(pl.program_id(0) == 0)
        def _():
            dgpost_ref[...] = jnp.zeros_like(dgpost_ref)
            dgpre_ref[...] = jnp.zeros_like(dgpre_ref)
            loss_ref[...] = jnp.zeros_like(loss_ref)

        x1 = x1_ref[...]
        gpre = gpre_ref[...]
        gpost = gpost_ref[...]
        xn2, r2 = _rms_fwd(x1)
        h2 = (xn2 * gpre).astype(BF16)
        h2_ref[...] = h2
        rl = []
        d = jnp.zeros((TOK_TILE, D_MODEL), F32)
        for j in range(N_CHIP):
            blk = slice(j * 1024, (j + 1) * 1024)
            r = jnp.maximum(jnp.dot(h2, wup_ref[j], preferred_element_type=F32), 0.0)
            rl.append(r)
            u = (r * r).astype(BF16)
            u_ref[:, blk] = u
            d = d + jnp.dot(u[:, 0:512], wda_ref[j], preferred_element_type=F32)
            d = d + jnp.dot(u[:, 512:1024], wdb_ref[j], preferred_element_type=F32)
        dn, r3 = _rms_fwd(d)
        diff = x1 + dn * gpost - t_ref[...]
        loss_ref[...] += jnp.sum(diff * diff, keepdims=True) * (0.5 / D_MODEL)
        dy = diff * (1.0 / D_MODEL)
        dgpost_ref[...] += jnp.sum(dy * dn, axis=0, keepdims=True)
        dd = _rms_bwd(dy, dn, r3, gpost).astype(BF16)
        dd_ref[...] = dd
        dh2 = jnp.zeros((TOK_TILE, D_MODEL), F32)
        for j in range(N_CHIP):
            blk = slice(j * 1024, (j + 1) * 1024)
            du = jnp.concatenate([_nt(dd, wda_ref[j]), _nt(dd, wdb_ref[j])], axis=1)
            dpre = (du * (2.0 * rl[j])).astype(BF16)
            dpre_ref[:, blk] = dpre
            dh2 = dh2 + _nt(dpre, wup_ref[j])
        dgpre_ref[...] += jnp.sum(dh2 * xn2, axis=0, keepdims=True)
        dx1_ref[...] = dy + _rms_bwd(dh2, xn2, r2, gpre)

    acc = pl.BlockSpec((1, D_MODEL), lambda i: (0, 0))
    return pl.pallas_call(
        body, grid=(N_TOK_TILE,),
        in_specs=[_tok(D_MODEL), _tok(D_MODEL), _once((1, D_MODEL)), _once((1, D_MODEL)),
                  _once((N_CHIP, D_MODEL, 1024)), _once((N_CHIP, 512, D_MODEL)), _once((N_CHIP, 512, D_MODEL))],
        out_specs=[_tok(D_MODEL), _tok(D_FF), _tok(D_MODEL), _tok(D_MODEL), _tok(D_FF), acc, acc,
                   pl.BlockSpec((1, 128), lambda i: (0, 0))],
        out_shape=[jax.ShapeDtypeStruct((SEQ, D_MODEL), F32), jax.ShapeDtypeStruct((SEQ, D_FF), BF16),
                   jax.ShapeDtypeStruct((SEQ, D_MODEL), BF16), jax.ShapeDtypeStruct((SEQ, D_MODEL), BF16),
                   jax.ShapeDtypeStruct((SEQ, D_FF), BF16), jax.ShapeDtypeStruct((1, D_MODEL), F32),
                   jax.ShapeDtypeStruct((1, D_MODEL), F32), jax.ShapeDtypeStruct((1, 128), F32)],
        compiler_params=_params(("arbitrary",)), name="mlp_fwd_bwd",
    )(x1, target, g_pre, g_post, w_up4, w_down_a4, w_down_b4)


def _merge_bwd(dx1, mix, ab, gab, merged, w_pa4, w_pb4, w_o, g_post):
    def body(dx1_ref, mix_ref, ab_ref, gab_ref, mg_ref, wpa_ref, wpb_ref, wo_ref, g_ref,
             dgab_ref, dab_ref, dg_ref, dwpa_ref, dwpb_ref, dwo_ref, acc_pa, acc_pb, acc_o):
        @pl.when(pl.program_id(0) == 0)
        def _():
            dg_ref[...] = jnp.zeros_like(dg_ref)
            acc_pa[...] = jnp.zeros_like(acc_pa)
            acc_pb[...] = jnp.zeros_like(acc_pb)
            acc_o[...] = jnp.zeros_like(acc_o)

        dx1 = dx1_ref[...]
        mn, r = _rms_fwd(mix_ref[...])
        dg_ref[...] += jnp.sum(dx1 * mn, axis=0, keepdims=True)
        dmix = _rms_bwd(dx1, mn, r, g_ref[...]).astype(BF16)
        acc_o[...] += _tn(mg_ref[...], dmix)
        dmerged = _nt(dmix, wo_ref[...])
        a = ab_ref[:, 0:512]
        b = ab_ref[:, 512:1024]
        da = jnp.zeros((TOK_TILE, 512), F32)
        db = jnp.zeros((TOK_TILE, 512), F32)
        dyas, dybs = [], []
        for j in range(N_CHIP):
            blk = slice(j * 256, (j + 1) * 256)
            blk_b = slice(1024 + j * 256, 1024 + (j + 1) * 256)
            dm = dmerged[:, blk]
            ya = jnp.dot(a, wpa_ref[j], preferred_element_type=F32)
            yb = jnp.dot(b, wpb_ref[j], preferred_element_type=F32)
            sa = _sigmoid(gab_ref[:, blk])
            sb = _sigmoid(gab_ref[:, blk_b])
            dya = (dm * sa).astype(BF16)
            dyb = (dm * sb).astype(BF16)
            dyas.append(dya)
            dybs.append(dyb)
            dgab_ref[:, blk] = (dm * ya * sa * (1.0 - sa)).astype(BF16)
            dgab_ref[:, blk_b] = (dm * yb * sb * (1.0 - sb)).astype(BF16)
            da = da + _nt(dya, wpa_ref[j])
            db = db + _nt(dyb, wpb_ref[j])
        dab_ref[:, 0:512] = da
        dab_ref[:, 512:1024] = db
        acc_pa[...] += _tn(a, jnp.concatenate(dyas, axis=1))
        acc_pb[...] += _tn(b, jnp.concatenate(dybs, axis=1))

        @pl.when(pl.program_id(0) == N_TOK_TILE - 1)
        def _():
            dwo_ref[...] = acc_o[...].astype(BF16)
            for j in range(N_CHIP):
                dwpa_ref[j] = acc_pa[:, j * 256:(j + 1) * 256].astype(BF16)
                dwpb_ref[j] = acc_pb[:, j * 256:(j + 1) * 256].astype(BF16)

    whole = lambda shape: pl.BlockSpec(shape, lambda i: (0,) * len(shape))
    return pl.pallas_call(
        body, grid=(N_TOK_TILE,),
        in_specs=[_tok(D_MODEL), _tok(D_MODEL), _tok(1024), _tok(GAB_W), _tok(D_MODEL), _once((N_CHIP, 512, 256)),
                  _once((N_CHIP, 512, 256)), _once((D_MODEL, D_MODEL)), _once((1, D_MODEL))],
        out_specs=[_tok(GAB_W), _tok(1024), whole((1, D_MODEL)), whole((N_CHIP, 512, 256)), whole((N_CHIP, 512, 256)),
                   whole((D_MODEL, D_MODEL))],
        out_shape=[jax.ShapeDtypeStruct((SEQ, GAB_W), BF16), jax.ShapeDtypeStruct((SEQ, 1024), F32),
                   jax.ShapeDtypeStruct((1, D_MODEL), F32), jax.ShapeDtypeStruct((N_CHIP, 512, 256), BF16),
                   jax.ShapeDtypeStruct((N_CHIP, 512, 256), BF16), jax.ShapeDtypeStruct((D_MODEL, D_MODEL), BF16)],
        scratch_shapes=[pltpu.VMEM((512, D_MODEL), F32), pltpu.VMEM((512, D_MODEL), F32),
                        pltpu.VMEM((D_MODEL, D_MODEL), F32)],
        compiler_params=_params(("arbitrary",)), name="merge_bwd",
    )(dx1, mix, ab, gab, merged, w_pa4, w_pb4, w_o, g_post)


def _in_proj_bwd(dpm, dgab, x, dx1, g_pre, w4_in, rider=None, rider_ins=()):
    def body(dpm_ref, dgab_ref, x_ref, dx1_ref, g_ref, w4_ref, dx_ref, dg_ref, wt_ref):
        _join_rows(w4_ref, wt_ref)

        @pl.when(pl.program_id(0) == 0)
        def _():
            dg_ref[...] = jnp.zeros_like(dg_ref)

        dh = jnp.dot(dpm_ref[:, 0:PM_XM], wt_ref[0:IN_ALOW, :], preferred_element_type=F32)
        dh = dh + jnp.dot(dpm_ref[:, PM_XM:PM_AL], wt_ref[IN_XM:IN_GATES, :], preferred_element_type=F32)
        dh = dh + jnp.dot(dpm_ref[:, PM_AL:PM_W], wt_ref[IN_ALOW:IN_ALOW + 128, :], preferred_element_type=F32)
        dh = dh + jnp.dot(dgab_ref[...], wt_ref[IN_GATES:D_IN, :], preferred_element_type=F32)
        xn, r = _rms_fwd(x_ref[...])
        dg_ref[...] += jnp.sum(dh * xn, axis=0, keepdims=True)
        dx_ref[...] = dx1_ref[...] + _rms_bwd(dh, xn, r, g_ref[...])

    return _tiled_call(
        body, [_tok(PM_W), _tok(GAB_W), _tok(D_MODEL), _tok(D_MODEL), _once((1, D_MODEL)),
               _once((N_CHIP, IN_SHARD, D_MODEL))],
        [_tok(D_MODEL), pl.BlockSpec((1, D_MODEL), lambda i: (0, 0))],
        [jax.ShapeDtypeStruct((SEQ, D_MODEL), F32), jax.ShapeDtypeStruct((1, D_MODEL), F32)],
        (dpm, dgab, x, dx1, g_pre, w4_in), "in_proj_bwd", rider, rider_ins, _joined_scratch())


def _dw_in(dpm, dgab, h):
    n_pm = PM_AL // 512
    n_blk = n_pm + GAB_W // 512

    def place(o_ref, rows, lo, hi):
        for j in range(N_CHIP):
            a, b = max(lo, j * IN_SHARD), min(hi, (j + 1) * IN_SHARD)
            if a < b:
                o_ref[j, a - j * IN_SHARD:b - j * IN_SHARD, :] = rows(a - lo, b - lo)

    def body(dpm_ref, dgab_ref, dal_ref, h_ref, o_ref, blk_ref):
        i = pl.program_id(0)

        @pl.when(i < n_pm)
        def _():
            blk_ref[...] = _tn(dpm_ref[...], h_ref[...]).astype(BF16)

        @pl.when(i >= n_pm)
        def _():
            blk_ref[...] = _tn(dgab_ref[...], h_ref[...]).astype(BF16)

        for k in range(n_blk):
            off = k * 512 + (IN_XM - IN_ALOW) * (k >= IN_ALOW // 512)

            @pl.when(i == k)
            def _():
                place(o_ref, lambda a, b: blk_ref[a:b, :], off, off + 512)

        @pl.when(i == 0)
        def _():
            a_low = _tn(dal_ref[...], h_ref[...])[0:IN_XM - IN_ALOW].astype(BF16)
            place(o_ref, lambda a, b: a_low[a:b], IN_ALOW, IN_XM)

    return pl.pallas_call(
        body, grid=(n_blk,),
        in_specs=[pl.BlockSpec((SEQ, 512), lambda i: (0, jnp.minimum(i, n_pm - 1))),
                  pl.BlockSpec((SEQ, 512), lambda i: (0, jnp.maximum(i - n_pm, 0))),
                  pl.BlockSpec((SEQ, 128), lambda i: (0, PM_AL // 128)),
                  _once((SEQ, D_MODEL))],
        out_specs=pl.BlockSpec((N_CHIP, IN_SHARD, D_MODEL), lambda i: (0, 0, 0)),
        out_shape=jax.ShapeDtypeStruct((N_CHIP, IN_SHARD, D_MODEL), BF16),
        scratch_shapes=[pltpu.VMEM((512, D_MODEL), BF16)],
        compiler_params=_params(("arbitrary",)), name="dw_in",
    )(dpm, dgab, dpm, h)


def _tn_matmul(a, b, name, shards=1, tm=1024, rider=None, rider_ins=()):
    m, n = a.shape[1], b.shape[1]
    tm = min(tm, m)
    tn = n // shards if shards > 1 else min(n, 1024)
    steps_i, steps_j = m // tm, n // tn
    r_in, r_out_specs, r_out_shape, r_scratch = _rider_specs(rider, rider_ins)

    def body(*refs):
        (a_ref, b_ref), ride_in, (o_ref,), ride_out, scratch = _split(refs, 2, len(r_in), 1, len(r_out_specs), len(r_scratch))
        step = pl.program_id(0) * steps_j + pl.program_id(1)
        _ride(rider, ("first",), step == 0, ride_in, ride_out, scratch)
        o_ref[...] = _tn(a_ref[...], b_ref[...]).astype(BF16)
        _ride(rider, ("middle", "last"), step == steps_i * steps_j - 1, ride_in, ride_out, scratch)

    if shards > 1:
        out_spec = pl.BlockSpec((None, tm, tn), lambda i, j: (j, i, 0))
        out_shape = jax.ShapeDtypeStruct((shards, m, tn), BF16)
    else:
        out_spec = pl.BlockSpec((tm, tn), lambda i, j: (i, j))
        out_shape = jax.ShapeDtypeStruct((m, n), BF16)
    res = pl.pallas_call(
        body, grid=(steps_i, steps_j),
        in_specs=[pl.BlockSpec((SEQ, tm), lambda i, j: (0, i)), pl.BlockSpec((SEQ, tn), lambda i, j: (0, j))] + r_in,
        out_specs=[out_spec] + r_out_specs, out_shape=[out_shape] + r_out_shape, scratch_shapes=r_scratch,
        compiler_params=_params(("arbitrary", "arbitrary")), name=name,
    )(a, b, *rider_ins)
    return res[0] if rider is None else (res[0], res[1:])


MESH = pl.DeviceIdType.MESH
ANY = pl.BlockSpec(memory_space=pl.ANY)
VMEM_WHOLE = pl.BlockSpec(memory_space=pltpu.VMEM)

_BIG = ("w_in", "w_pa", "w_pb", "w_o", "w_up", "w_down")
_BIG_SHARD = {"w_in": (IN_SHARD, D_MODEL), "w_pa": (512, 256), "w_pb": (512, 256), "w_o": (256, D_MODEL),
              "w_up": (D_MODEL, 1024), "w_down": (1024, D_MODEL),
              "w_down_a": (512, D_MODEL), "w_down_b": (512, D_MODEL)}
_BIG_SPLIT = {"w_in": 1, "w_pa": 0, "w_pb": 0, "w_o": 0, "w_up": 0, "w_down": 0, "w_down_a": 0, "w_down_b": 0}


def _half(ref, e, name, lead=0, part=None):
    axis = _BIG_SPLIT[name]
    size = _BIG_SHARD[name][axis] // 2
    start = e * size
    if part is not None:
        size //= 2
        start = start + part * size
    start = pl.multiple_of(start, 128 if axis == 1 else 16)
    idx = [pl.ds(0, ref.shape[a]) for a in range(lead)]
    idx += [pl.ds(start, size), pl.ds(0, _BIG_SHARD[name][1])] if axis == 0 else [pl.ds(0, _BIG_SHARD[name][0]), pl.ds(start, size)]
    return ref.at[tuple(idx)]


def _half_shape(name):
    r, c = _BIG_SHARD[name]
    return (r // 2, c) if _BIG_SPLIT[name] == 0 else (r, c // 2)


def _remote(src, dst, send_sems, recv_sems, k, to):
    return pltpu.make_async_remote_copy(src_ref=src, dst_ref=dst, send_sem=send_sems.at[k], recv_sem=recv_sems.at[k],
                                        device_id=to, device_id_type=MESH)


def _mesh_place():
    x, y, c = lax.axis_index("x"), lax.axis_index("y"), lax.axis_index("c")
    return x, y, c, [(1 - x, y), (x, 1 - y), (1 - x, 1 - y)]


class _Gather:
    def __init__(self, names, small=(), middle_at=0.5):
        self.middle_at = middle_at
        self.names = tuple(names)
        self.nb = len(self.names)
        self.n = self.nb + len(small)
        self.n_sems = 8 * self.nb + 3 * len(small)
        self.n_flush = 6 * self.nb + len(small)
        self.out_shape = [jax.ShapeDtypeStruct((N_CHIP,) + _BIG_SHARD[nm], BF16) for nm in self.names]
        self.out_shape += [jax.ShapeDtypeStruct((N_CHIP,) + s.shape, s.dtype) for s in small]
        self.in_space = [self._in_spec(nm) for nm in self.names] + [VMEM_WHOLE] * len(small)

    @staticmethod
    def _in_spec(name):
        if name in ("w_down_a", "w_down_b"):
            half = 0 if name == "w_down_a" else 1
            return pl.BlockSpec(_BIG_SHARD[name], lambda *_: (half, 0), pipeline_mode=pl.Buffered(1))
        return VMEM_WHOLE

    def _copies(self, ins, outs, ss, rs, k):
        x, y, c, _ = _mesh_place()
        name = self.names[k]
        me, xn, yn, dg = 2 * x + y, 2 * (1 - x) + y, 2 * x + (1 - y), 2 * (1 - x) + (1 - y)
        to_x, to_y, sibling = (1 - x, y, c), (x, 1 - y, c), (x, y, 1 - c)

        def region(slot, e, part=None):
            return _half(outs[k].at[slot], e, name, part=part)

        def copy(pair, src, dst, to):
            return _remote(src, dst, ss, rs, 8 * k + pair, to)

        mine = region(me, c)
        sent = [copy(0, mine, mine, to_x), copy(1, mine, mine, to_y),
                copy(2, region(xn, c, 0), region(xn, c, 0), to_y), copy(3, region(yn, c, 1), region(yn, c, 1), to_x),
                copy(4, region(xn, c), region(xn, c), sibling), copy(5, region(yn, c), region(yn, c), sibling),
                copy(6, region(dg, c, 0), region(dg, c, 0), sibling), copy(7, region(dg, c, 1), region(dg, c, 1), sibling)]
        landing = [region(xn, c), region(yn, c), region(dg, c, 0), region(dg, c, 1),
                   region(xn, 1 - c), region(yn, 1 - c), region(dg, 1 - c, 0), region(dg, 1 - c, 1)]
        received = [copy(pair, dst, dst, sibling) for pair, dst in enumerate(landing)]
        return sent, received

    def _small(self, ins, outs, ss, rs, k, j, peer, slot, c):
        return _remote(ins[k], outs[k].at[slot], ss, rs, 8 * self.nb + 3 * (k - self.nb) + j, (*peer, c))

    def flush(self, phase, lands, outs, fs):
        x, y, c, _ = _mesh_place()
        me, xn, yn, dg = 2 * x + y, 2 * (1 - x) + y, 2 * x + (1 - y), 2 * (1 - x) + (1 - y)

        def pieces(k):
            name = self.names[k]
            spots = [lambda r: r.at[me], lambda r: _half(r.at[xn], c, name), lambda r: _half(r.at[yn], c, name),
                     lambda r: _half(r.at[xn], 1 - c, name), lambda r: _half(r.at[yn], 1 - c, name), lambda r: r.at[dg]]
            return [pltpu.make_async_copy(spot(lands[k]), spot(outs[k]), fs.at[6 * k + t]) for t, spot in enumerate(spots)]

        ready = {"first": (0,), "middle": (1, 2), "late": (3, 4), "last": (5,)}[phase]
        for k in range(self.nb):
            cps = pieces(k)
            for t in ready:
                cps[t].start()
        if phase == "last":
            small = [pltpu.make_async_copy(lands[k], outs[k], fs.at[6 * self.nb + k - self.nb]) for k in range(self.nb, self.n)]
            for cp in small:
                cp.start()
            for k in range(self.nb):
                for cp in pieces(k):
                    cp.wait()
            for cp in small:
                cp.wait()

    def first(self, ins, outs, ss, rs):
        x, y, c, peers = _mesh_place()
        me = 2 * x + y
        for k in range(self.nb):
            outs[k][me] = ins[k][...].astype(BF16)
            sent, _ = self._copies(ins, outs, ss, rs, k)
            sent[0].start()
            sent[1].start()
        for k in range(self.nb, self.n):
            for j, peer in enumerate(peers):
                self._small(ins, outs, ss, rs, k, j, peer, me, c).start()
            outs[k][me] = ins[k][...]

    def middle(self, ins, outs, ss, rs):
        for k in range(self.nb):
            sent, received = self._copies(ins, outs, ss, rs, k)
            for pair in (0, 1):
                received[pair].wait_recv()
                sent[2 + pair].start()
                sent[4 + pair].start()

    def late(self, ins, outs, ss, rs):
        for k in range(self.nb):
            _, received = self._copies(ins, outs, ss, rs, k)
            for pair in (4, 5):
                received[pair].wait_recv()

    def last(self, ins, outs, ss, rs):
        x, y, c, peers = _mesh_place()
        for k in range(self.nb):
            sent, received = self._copies(ins, outs, ss, rs, k)
            for pair in (2, 3):
                received[pair].wait_recv()
                sent[4 + pair].start()
        for k in range(self.nb):
            sent, received = self._copies(ins, outs, ss, rs, k)
            for pair in (6, 7):
                received[pair].wait_recv()
            for cp in sent:
                cp.wait_send()
        for k in range(self.nb, self.n):
            for j, (px, py) in enumerate(peers):
                self._small(ins, outs, ss, rs, k, j, (px, py), 2 * px + py, c).wait_recv()
                self._small(ins, outs, ss, rs, k, j, (px, py), 2 * x + y, c).wait_send()


def _run_alone(rider, ins, name):
    r_in, r_out_specs, r_out_shape, r_scratch = _rider_specs(rider, ins)

    def body(*refs):
        ride_in, ride_out, scratch = _split(refs, len(r_in), len(r_out_specs), len(r_scratch))
        _ride(rider, ("first", "middle", "last"), pl.program_id(0) == 0, ride_in, ride_out, scratch)

    return pl.pallas_call(
        body, grid=(1,), in_specs=r_in, out_specs=r_out_specs, out_shape=r_out_shape, scratch_shapes=r_scratch,
        compiler_params=_params(("arbitrary",)), name=name,
    )(*ins)


class _Presum:
    in_space = ANY

    def __init__(self, names, base=0):
        self.names = tuple(names)
        self.n = len(self.names)
        self.base = base
        self.n_sems = 3 * self.n
        self.out_shape = [jax.ShapeDtypeStruct((N_CHIP,) + _half_shape(nm), BF16) for nm in self.names]
        self.work_shape = self.out_shape + self.out_shape

    def _stage(self, ins, bufs, ss, k, e, which):
        n = self.n
        return pltpu.make_async_copy(_half(ins[k], e, self.names[k], lead=1), bufs[which * n + k],
                                     ss.at[self.base + which * n + k])

    def _give(self, bufs, ss, rs, k, sibling):
        return _remote(bufs[self.n + k], bufs[k], ss, rs, self.base + k, sibling)

    def first(self, ins, bufs, ss, rs):
        x, y, c, _ = _mesh_place()
        for k in range(self.n):
            self._stage(ins, bufs, ss, k, 1 - c, 1).start()
        for k in range(self.n):
            self._stage(ins, bufs, ss, k, c, 2).start()
        for k in range(self.n):
            self._stage(ins, bufs, ss, k, 1 - c, 1).wait()
            self._give(bufs, ss, rs, k, (x, y, 1 - c)).start()

    def middle(self, ins, bufs, ss, rs):
        pass

    def last(self, ins, bufs, ss, rs):
        x, y, c, _ = _mesh_place()
        for k in range(self.n):
            self._give(bufs, ss, rs, k, (x, y, 1 - c)).wait_recv()
            self._stage(ins, bufs, ss, k, c, 2).wait()

            @pl.loop(0, N_CHIP)
            def _(j):
                bufs[k][j] = (bufs[k][j].astype(F32) + bufs[2 * self.n + k][j].astype(F32)).astype(BF16)
        for k in range(self.n):
            self._give(bufs, ss, rs, k, (x, y, 1 - c)).wait_send()


class _ReduceRelay:
    middle_at = 0.75

    def __init__(self, names, base=0):
        self.names = tuple(names)
        self.n = len(self.names)
        self.base = base
        self.n_sems = 6 * self.n
        self.out_shape = [jax.ShapeDtypeStruct((N_CHIP,) + _half_shape(nm), BF16) for nm in self.names]
        quarter = [jax.ShapeDtypeStruct(self._part_shape(nm), BF16) for nm in self.names]
        self.work_shape = quarter + quarter

    @staticmethod
    def _part_shape(name):
        r, c = _half_shape(name)
        return (r // 2, c) if _BIG_SPLIT[name] == 0 else (r, c // 2)

    def _part(self, ref, name, p):
        r, c = self._part_shape(name)
        return ref.at[pl.ds(p * r, r), pl.ds(0, c)] if _BIG_SPLIT[name] == 0 else ref.at[pl.ds(0, r), pl.ds(p * c, c)]

    def _copies(self, ins, bufs, ss, rs, k):
        x, y, c, _ = _mesh_place()
        name, n = self.names[k], self.n
        me, xn, yn, dg = 2 * x + y, 2 * (1 - x) + y, 2 * x + (1 - y), 2 * (1 - x) + (1 - y)
        to_x, to_y = (1 - x, y, c), (x, 1 - y, c)
        mine = lambda slot, p: self._part(ins[k].at[slot], name, p)
        slot = lambda s, p: self._part(bufs[k].at[s], name, p)
        from_x, from_y = bufs[n + k], bufs[2 * n + k]

        def copy(pair, src, dst, to):
            return _remote(src, dst, ss, rs, self.base + 6 * k + pair, to)

        sent = [copy(0, mine(dg, 0), from_x, to_x), copy(1, mine(dg, 1), from_y, to_y),
                copy(2, mine(xn, 0), slot(me, 0), to_x), copy(3, mine(yn, 1), slot(me, 1), to_y),
                copy(4, from_y, slot(me, 1), to_x), copy(5, from_x, slot(me, 0), to_y)]
        landing = [from_x, from_y, slot(xn, 0), slot(yn, 1), slot(xn, 1), slot(yn, 0)]
        received = [copy(pair, dst, dst, to_x) for pair, dst in enumerate(landing)]
        return sent, received

    def first(self, ins, bufs, ss, rs):
        x, y, c, _ = _mesh_place()
        me, dg = 2 * x + y, 2 * (1 - x) + (1 - y)
        for k in range(self.n):
            sent, _ = self._copies(ins, bufs, ss, rs, k)
            for pair in range(4):
                sent[pair].start()
        for k in range(self.n):
            bufs[k][me] = ins[k][me]
            bufs[k][dg] = jnp.zeros(_half_shape(self.names[k]), BF16)

    def middle(self, ins, bufs, ss, rs):
        x, y, c, _ = _mesh_place()
        xn, yn = 2 * (1 - x) + y, 2 * x + (1 - y)
        for k in range(self.n):
            sent, received = self._copies(ins, bufs, ss, rs, k)
            name, n = self.names[k], self.n
            for pair, buf, own in ((0, bufs[n + k], self._part(ins[k].at[yn], name, 0)),
                                   (1, bufs[2 * n + k], self._part(ins[k].at[xn], name, 1))):
                received[pair].wait_recv()
                buf[...] = (buf[...].astype(F32) + own[...].astype(F32)).astype(BF16)
            sent[5].start()
            sent[4].start()

    def last(self, ins, bufs, ss, rs):
        for k in range(self.n):
            sent, received = self._copies(ins, bufs, ss, rs, k)
            for pair in range(2, 6):
                received[pair].wait_recv()
            for cp in sent:
                cp.wait_send()


class _PresumThenRelay:
    in_space = ANY
    middle_at = _ReduceRelay.middle_at

    def __init__(self, names):
        self.relay = _ReduceRelay(names)
        self.pre = _Presum(names, base=self.relay.n_sems)
        self.n_sems = self.relay.n_sems + self.pre.n_sems
        self.out_shape = self.relay.out_shape
        self.work_shape = list(self.relay.work_shape) + list(self.pre.out_shape) + list(self.pre.work_shape)
        self.n_relay = len(self.relay.out_shape) + len(self.relay.work_shape)

    def first(self, ins, bufs, ss, rs):
        self.pre.first(ins, bufs[self.n_relay:], ss, rs)

    def early(self, ins, bufs, ss, rs):
        self.pre.last(ins, bufs[self.n_relay:], ss, rs)
        self.relay.first(bufs[self.n_relay:], bufs[:self.n_relay], ss, rs)

    def middle(self, ins, bufs, ss, rs):
        self.relay.middle(bufs[self.n_relay:], bufs[:self.n_relay], ss, rs)

    def last(self, ins, bufs, ss, rs):
        self.relay.last(bufs[self.n_relay:], bufs[:self.n_relay], ss, rs)


class _SendPartials:
    def __init__(self, names, small_shape=None):
        self.n = len(names)
        self.small = small_shape is not None
        self.n_sems = 3 * self.n + 7
        self.out_shape = [jax.ShapeDtypeStruct((N_CHIP,) + _half_shape(nm), BF16) for nm in names]
        if self.small:
            self.out_shape.append(jax.ShapeDtypeStruct((N_DEV,) + small_shape, F32))

    def _piece(self, ins, outs, ss, rs, k, j, peer, src_slot, dst_slot, c):
        return _remote(ins[k].at[src_slot], outs[k].at[dst_slot], ss, rs, 3 * k + j, (*peer, c))

    def _small(self, ins, outs, ss, rs, r, other, slot):
        return _remote(ins[self.n], outs[self.n].at[slot], ss, rs, 3 * self.n + r, other)

    @staticmethod
    def _others(x, y, c):
        return [(x, y, 1 - c), (1 - x, y, c), (1 - x, y, 1 - c), (x, 1 - y, c), (x, 1 - y, 1 - c),
                (1 - x, 1 - y, c), (1 - x, 1 - y, 1 - c)]

    def first(self, ins, outs, ss, rs, only=None):
        x, y, c, peers = _mesh_place()
        me = 2 * x + y
        which = range(self.n) if only is None else only
        for k in which:
            for j, (px, py) in enumerate(peers):
                self._piece(ins, outs, ss, rs, k, j, (px, py), 2 * px + py, me, c).start()
        if self.small:
            for r, other in enumerate(self._others(x, y, c)):
                self._small(ins, outs, ss, rs, r, other, 4 * x + 2 * y + c).start()
            outs[self.n][4 * x + 2 * y + c] = ins[self.n][...]
        for k in which:
            outs[k][me] = ins[k][me]

    def middle(self, ins, outs, ss, rs):
        pass

    def last(self, ins, outs, ss, rs):
        x, y, c, peers = _mesh_place()
        me = 2 * x + y
        for k in range(self.n):
            for j, (px, py) in enumerate(peers):
                self._piece(ins, outs, ss, rs, k, j, (px, py), me, 2 * px + py, c).wait_recv()
                self._piece(ins, outs, ss, rs, k, j, (px, py), 2 * px + py, me, c).wait_send()
        if self.small:
            for r, (px, py, pc) in enumerate(self._others(x, y, c)):
                self._small(ins, outs, ss, rs, r, (px, py, pc), 4 * px + 2 * py + pc).wait_recv()
                self._small(ins, outs, ss, rs, r, (px, py, pc), 4 * x + 2 * y + c).wait_send()


class _PresumThenSend:
    def __init__(self, names):
        self.send = _SendPartials(names)
        self.pre = _Presum(names[-1:], base=self.send.n_sems)
        self.n = self.send.n
        self.n_sems = self.send.n_sems + self.pre.n_sems
        self.out_shape = self.send.out_shape
        self.work_shape = list(self.pre.out_shape) + list(self.pre.work_shape)
        self.in_space = [VMEM_WHOLE] * (self.n - 1) + [ANY]

    def _partials(self, ins, bufs):
        return list(ins[:self.n - 1]) + [bufs[self.n]]

    def first(self, ins, bufs, ss, rs):
        self.pre.first(ins[self.n - 1:], bufs[self.n:], ss, rs)
        self.send.first(ins, bufs[:self.n], ss, rs, only=range(self.n - 1))

    def early(self, ins, bufs, ss, rs):
        self.pre.last(ins[self.n - 1:], bufs[self.n:], ss, rs)
        self.send.first(self._partials(ins, bufs), bufs[:self.n], ss, rs, only=(self.n - 1,))

    def middle(self, ins, bufs, ss, rs):
        pass

    def last(self, ins, bufs, ss, rs):
        self.send.last(self._partials(ins, bufs), bufs[:self.n], ss, rs)


def _sum_swap(names, parts, small):
    n = len(parts)
    everyone = _SendPartials((), small.shape)

    def body(*refs):
        (p_hbm, (small_ref,), o_hbm, (osmall_ref,), p_refs, o_refs, (all_ref,),
         (send_sems, recv_sems, ss_small, rs_small, load_sems, leave_sems)) = _split(refs, n, 1, n, 1, n, n, 1, 6)
        x, y, c = lax.axis_index("x"), lax.axis_index("y"), lax.axis_index("c")
        loads = [pltpu.make_async_copy(p_hbm[k], p_refs[k], load_sems.at[k]) for k in range(n)]
        for cp in loads:
            cp.start()
        everyone.first([small_ref], [all_ref], ss_small, rs_small)

        def mine(k):
            part = _half(o_refs[k], c, names[k])
            return _remote(part, part, send_sems, recv_sems, k, (x, y, 1 - c))

        def leave(k, whose):
            e = c if whose == 0 else 1 - c
            return pltpu.make_async_copy(_half(o_refs[k], e, names[k]), _half(o_hbm[k], e, names[k]),
                                         leave_sems.at[2 * k + whose])

        for k in range(n):
            loads[k].wait()
            for e in range(2):
                @pl.when(c == e)
                def _():
                    g = p_refs[k][0].astype(F32)
                    for s in range(1, N_CHIP):
                        g = g + p_refs[k][s].astype(F32)
                    r, cols = _half_shape(names[k])
                    if _BIG_SPLIT[names[k]] == 0:
                        o_refs[k][e * r:(e + 1) * r, :] = g
                    else:
                        o_refs[k][:, e * cols:(e + 1) * cols] = g
            mine(k).start()
            leave(k, 0).start()
        for k in range(n):
            theirs = _half(o_refs[k], 1 - c, names[k])
            _remote(theirs, theirs, send_sems, recv_sems, k, (x, y, 1 - c)).wait_recv()
            leave(k, 1).start()
        everyone.last([small_ref], [all_ref], ss_small, rs_small)
        g = all_ref[0]
        for d in range(1, N_DEV):
            g = g + all_ref[d]
        osmall_ref[...] = g
        for k in range(n):
            mine(k).wait_send()
            leave(k, 0).wait()
            leave(k, 1).wait()

    shards = [jax.ShapeDtypeStruct(_BIG_SHARD[nm], F32) for nm in names]
    res = pl.pallas_call(
        body, in_specs=[ANY] * n + [VMEM_WHOLE], out_specs=[ANY] * n + [VMEM_WHOLE],
        out_shape=shards + [jax.ShapeDtypeStruct(small.shape, F32)],
        scratch_shapes=[pltpu.VMEM(q.shape, q.dtype) for q in parts] + [pltpu.VMEM(s.shape, s.dtype) for s in shards]
        + [pltpu.VMEM((N_DEV,) + small.shape, F32), pltpu.SemaphoreType.DMA((n,)), pltpu.SemaphoreType.DMA((n,)),
           pltpu.SemaphoreType.DMA((everyone.n_sems,)), pltpu.SemaphoreType.DMA((everyone.n_sems,)),
           pltpu.SemaphoreType.DMA((n,)), pltpu.SemaphoreType.DMA((2 * n,))],
        compiler_params=_params(), name="sum_swap",
    )(*parts, small)
    return res[:n], res[n]


def _adamw_math(w, g, m, v):
    m = ADAM_B1 * m + (1.0 - ADAM_B1) * g
    v = ADAM_B2 * v + (1.0 - ADAM_B2) * (g * g)
    m_hat = m / (1.0 - ADAM_B1 ** ADAM_STEP)
    v_hat = v / (1.0 - ADAM_B2 ** ADAM_STEP)
    delta = -ADAM_LR * (m_hat / (jnp.sqrt(v_hat) + ADAM_EPS) + ADAM_WD * w)
    return delta, m, v


ADAMW_STEPS = 8


def _adamw_big(gs, ws, ms, vs, name):
    n = len(ws)

    def body(*refs):
        for k in range(n):
            g_ref, w_ref, m_ref, v_ref = refs[4 * k:4 * k + 4]
            g_out_ref, d_ref, nm_ref, nv_ref = refs[4 * (n + k):4 * (n + k) + 4]
            g = g_ref[...]
            g_out_ref[...] = g
            d_ref[...], nm_ref[...], nv_ref[...] = _adamw_math(w_ref[...], g, m_ref[...], v_ref[...])

    specs, shapes, args = [], [], []
    for g, w, m, v in zip(gs, ws, ms, vs):
        r, c = w.shape
        assert r % (8 * ADAMW_STEPS) == 0, w.shape
        specs += [pl.BlockSpec((r // ADAMW_STEPS, c), lambda i: (i, 0))] * 4
        shapes += [jax.ShapeDtypeStruct((r, c), F32)] * 4
        args += [g, w, m, v]
    res = pl.pallas_call(
        body, grid=(ADAMW_STEPS,), in_specs=specs, out_specs=specs, out_shape=shapes,
        compiler_params=_params(("arbitrary",)), name=name,
    )(*args)
    return [tuple(res[4 * k:4 * k + 4]) for k in range(n)]


def _adamw_rows(g, w, m, v, name):
    r, k, lanes = w.shape
    tr = 296

    def body(g_ref, w_ref, m_ref, v_ref, g3_ref, d_ref, nm_ref, nv_ref):
        g = g_ref[...].reshape(tr, k, lanes)
        g3_ref[...] = g
        d_ref[...], nm_ref[...], nv_ref[...] = _adamw_math(w_ref[...], g, m_ref[...], v_ref[...])

    rows = pl.BlockSpec((tr, k, lanes), lambda i: (i, 0, 0))
    return pl.pallas_call(
        body, grid=(pl.cdiv(r, tr),), in_specs=[pl.BlockSpec((tr, k * lanes), lambda i: (i, 0)), rows, rows, rows],
        out_specs=[rows] * 4, out_shape=[jax.ShapeDtypeStruct((r, k, lanes), F32)] * 4,
        compiler_params=_params(("arbitrary",)), name=name,
    )(g, w, m, v)


def _adamw_small(ws, gs, ms, vs):
    n = len(ws)

    def body(*refs):
        w_refs, g_refs, m_refs, v_refs, d_refs, nm_refs, nv_refs = _split(refs, *([n] * 7))
        for k in range(n):
            d_refs[k][...], nm_refs[k][...], nv_refs[k][...] = _adamw_math(w_refs[k][...], g_refs[k][...], m_refs[k][...],
                                                                             v_refs[k][...])

    shapes = [jax.ShapeDtypeStruct(w.shape, F32) for w in ws]
    res = pl.pallas_call(body, out_shape=shapes * 3, name="adamw_small")(*ws, *gs, *ms, *vs)
    return res[:n], res[n:2 * n], res[2 * n:]


def _pack(arrs):
    flat = jnp.concatenate([a.reshape(-1) for a in arrs])
    rows = -(-flat.shape[0] // 1024) * 8
    return jnp.pad(flat, (0, rows * 128 - flat.shape[0])).reshape(rows, 128)


def _unpack(buf, shapes):
    flat = buf.reshape(-1)
    out, off = [], 0
    for s in shapes:
        size = 1
        for d in s:
            size *= d
        out.append(flat[off:off + size].reshape(s))
        off += size
    return out


def _block_rows(w):
    return jnp.pad(w.reshape(512, 4), ((0, 0), (0, 124)))


def _block_stored(dw):
    return jnp.transpose(dw[:, 0:4].reshape(128, 4, 4), (1, 2, 0)).reshape(16, 128)


def _cols(a4):
    return jnp.transpose(a4, (1, 0, 2)).reshape(a4.shape[1], -1)


_LATE = ("w_pa", "w_pb", "w_o", "w_up", "w_down")
_RIDE_IN_PROJ = ("w_pa", "w_pb", "w_o")
_RIDE_MIXER = ("w_up", "w_down_a", "w_down_b")


def _full_weights(gathered):
    joined = {"w_o": (D_MODEL, D_MODEL)}
    return {n: (a.reshape(joined[n]) if n in joined else a) for n, a in gathered.items()}


def _local_step(x, target, w, sp, late_shards=None):
    sp = {n: (a.reshape(1, -1) if a.ndim == 1 else a) for n, a in sp.items()}
    wau = jnp.pad(sp["w_a_up"], ((0, 112), (0, 0)))
    wif = jnp.pad(sp["w_if"], ((0, 0), (0, 120)))
    bif = jnp.pad(sp["b_if"], ((0, 0), (0, 120)))
    p = {"wau": wau, "bau": sp["b_a_up"], "ggla": sp["g_gla_norm"], "cw": sp["conv_w"], "cb": sp["conv_b"],
         "wq": _block_rows(sp["w_q_ml"]), "wk": _block_rows(sp["w_k_ml"]), "wv": _block_rows(sp["w_v_ml"]),
         "wif": wif, "bif": bif, "skip": sp["ml_skip"], "gml": sp["g_ml_norm"]}

    if late_shards is None:
        (pm, gab, h), _ = _in_proj(x, sp["g_pre_mix"], w["w_in"])
        ab, x1, mix, merged, *states = _mixer_fwd(pm, p, gab, x, w["w_pa"], w["w_pb"], w["w_o"], sp["g_post_mix"])
    else:
        shard = dict(zip(_LATE, late_shards))
        shard["w_down_a"] = shard["w_down_b"] = shard["w_down"]
        (pm, gab, h), got = _in_proj(x, sp["g_pre_mix"], w["w_in"], _Gather(_RIDE_IN_PROJ, middle_at=0.45),
                                     [shard[n] for n in _RIDE_IN_PROJ])
        w = dict(w, **_full_weights(dict(zip(_RIDE_IN_PROJ, got))))
        ab, x1, mix, merged, *rest = _mixer_fwd(pm, p, gab, x, w["w_pa"], w["w_pb"], w["w_o"], sp["g_post_mix"],
                                                _Gather(_RIDE_MIXER, middle_at=0.62), [shard[n] for n in _RIDE_MIXER])
        states = rest[:4]
        w.update(_full_weights(dict(zip(_RIDE_MIXER, rest[4:]))))
    dx1, u, dd, h2, dpre, dg_post_mlp, dg_pre_mlp, loss = _mlp(x1, target, sp["g_pre_mlp"], sp["g_post_mlp"],
                                                                w["w_up"], w["w_down_a"], w["w_down_b"])
    dgab, dab, dg_post_mix, dw_pa, dw_pb, dw_o = _merge_bwd(dx1, mix, ab, gab, merged, w["w_pa"], w["w_pb"], w["w_o"],
                                                            sp["g_post_mix"])
    big = {"w_pa": dw_pa, "w_pb": dw_pb, "w_o": dw_o, "w_up": _tn_matmul(h2, dpre, "dw_up", shards=N_CHIP)}
    if late_shards is None:
        big["w_down"] = _tn_matmul(u, dd, "dw_down")
        dpm, dp, _ = _mixer_bwd(pm, dab, states, p)
    else:
        pieces = lambda n: big[n].reshape((N_CHIP,) + _BIG_SHARD[n])
        big["w_down"], partial = _tn_matmul(u, dd, "dw_down", rider=_Presum(_LATE[:4]),
                                            rider_ins=[pieces(n) for n in _LATE[:4]])
        dpm, dp, parts = _mixer_bwd(pm, dab, states, p, _PresumThenSend(_LATE), list(partial) + [pieces("w_down")])
        big = dict(zip(_LATE, parts))
    big["w_in"] = _dw_in(dpm, dgab, h)
    if late_shards is None:
        (dx, dg_pre_mix), _ = _in_proj_bwd(dpm, dgab, x, dx1, sp["g_pre_mix"], w["w_in"])
    else:
        (dx, dg_pre_mix), parts = _in_proj_bwd(dpm, dgab, x, dx1, sp["g_pre_mix"], w["w_in"], _PresumThenRelay(("w_in",)),
                                               [big["w_in"]])
        big["w_in"] = parts[0]
    small = {
        "g_pre_mix": dg_pre_mix, "b_a_up": dp["bau"], "g_gla_norm": dp["ggla"], "conv_b": dp["cb"],
        "w_q_ml": _block_stored(dp["wq"]), "w_k_ml": _block_stored(dp["wk"]), "w_v_ml": _block_stored(dp["wv"]),
        "w_if": dp["wif"][:, 0:8].T,
        "b_if": dp["bif"][:, 0:8], "ml_skip": dp["skip"], "g_ml_norm": dp["gml"], "g_post_mix": dg_post_mix,
        "g_pre_mlp": dg_pre_mlp, "g_post_mlp": dg_post_mlp, "w_a_up": dp["wau"][0:16], "conv_w": dp["cw"],
        "loss": loss[:, 0:1],
    }
    return dx, big, small


_SMALL_REPL = ("g_pre_mix", "b_a_up", "g_gla_norm", "conv_b", "w_q_ml", "w_k_ml", "w_v_ml", "b_if", "ml_skip",
               "g_ml_norm", "g_post_mix", "g_pre_mlp", "g_post_mlp")
_SMALL_SHARDED = ("w_a_up", "conv_w", "w_if")
_SMALL_ORDER = _SMALL_REPL + _SMALL_SHARDED + ("loss",)
_WEIGHTS = ("g_pre_mix", "w_in", "w_a_up", "b_a_up", "g_gla_norm", "conv_w", "conv_b", "w_q_ml", "w_k_ml", "w_v_ml",
            "w_if", "b_if", "ml_skip", "g_ml_norm", "w_pa", "w_pb", "w_o", "g_post_mix", "g_pre_mlp", "w_up", "w_down",
            "g_post_mlp")


_BLOCK_WEIGHTS = ("w_q_ml", "w_k_ml", "w_v_ml")


def _stored(name, a):
    if name in _BLOCK_WEIGHTS:
        return jnp.transpose(a, (0, 2, 3, 1)).reshape(16, 128)
    if name == "w_if":
        return jnp.transpose(a, (0, 2, 1)).reshape(8, 384)
    return a


def _unstored(name, a):
    if name in _BLOCK_WEIGHTS:
        return jnp.transpose(a.reshape(1, 4, 4, 128), (0, 3, 1, 2))
    if name == "w_if":
        return jnp.transpose(a.reshape(1, 8, 384), (0, 2, 1))
    return a


def _as_shard(name, a):
    return jnp.transpose(a, (2, 0, 1)).reshape(IN_SHARD, D_MODEL // 128, 128) if name == "w_in" else a[0]


def _in_shard_bf16(w_in):
    return jnp.transpose(w_in.astype(BF16), (2, 0, 1)).reshape(IN_SHARD, D_MODEL)


def _from_shard(name, a):
    return jnp.transpose(a, (1, 2, 0)).reshape(1, D_MODEL, IN_SHARD) if name == "w_in" else a[None]


def kernel(x, g_pre_mix, w_in, w_a_up, b_a_up, g_gla_norm, conv_w, conv_b, w_q_ml, w_k_ml, w_v_ml, w_if, b_if, ml_skip, g_ml_norm, w_pa, w_pb, w_o, g_post_mix, g_pre_mlp, w_up, w_down, g_post_mlp, loss_target, m_g_pre_mix, m_w_in, m_w_a_up, m_b_a_up, m_g_gla_norm, m_conv_w, m_conv_b, m_w_q_ml, m_w_k_ml, m_w_v_ml, m_w_if, m_b_if, m_ml_skip, m_g_ml_norm, m_w_pa, m_w_pb, m_w_o, m_g_post_mix, m_g_pre_mlp, m_w_up, m_w_down, m_g_post_mlp, v_g_pre_mix, v_w_in, v_w_a_up, v_b_a_up, v_g_gla_norm, v_conv_w, v_conv_b, v_w_q_ml, v_w_k_ml, v_w_v_ml, v_w_if, v_b_if, v_ml_skip, v_g_ml_norm, v_w_pa, v_w_pb, v_w_o, v_g_post_mix, v_g_pre_mlp, v_w_up, v_w_down, v_g_post_mlp):
    args = dict(locals())
    wts = {n: _as_shard(n, args[n]) for n in _WEIGHTS}
    mom = {n: _as_shard(n, args["m_" + n]) for n in _WEIGHTS}
    var = {n: _as_shard(n, args["v_" + n]) for n in _WEIGHTS}
    chip = 2 * lax.axis_index("x") + lax.axis_index("y")

    first = ("w_in",) + _SMALL_SHARDED
    gathered = dict(zip(first, _run_alone(_Gather(("w_in",), [wts[n] for n in _SMALL_SHARDED]),
                                          [_in_shard_bf16(w_in)] + [wts[n] for n in _SMALL_SHARDED],
                                          "gather_first")))
    sp = {n: wts[n] for n in _SMALL_REPL}
    sp["w_a_up"] = _cols(gathered["w_a_up"])
    sp["conv_w"] = _cols(gathered["conv_w"])
    sp["w_if"] = gathered["w_if"].reshape(1536, 8)

    dx, big, small = _local_step(x[0], loss_target[0], _full_weights({"w_in": gathered["w_in"]}), sp,
                                 late_shards=[wts[n] for n in _LATE])

    small_shapes = [small[n].shape for n in _SMALL_ORDER]
    packed = _pack([small[n] for n in _SMALL_ORDER])
    sums, small_sum = _sum_swap(_BIG, [big[n] for n in _BIG], packed)

    grads, delta, new_m, new_v = {}, {}, {}, {}
    g_sum = dict(zip(_BIG, sums))
    updated = {"w_in": _adamw_rows(g_sum["w_in"], wts["w_in"], mom["w_in"], var["w_in"], "adamw_w_in")}
    updated.update(zip(_LATE, _adamw_big([g_sum[n] for n in _LATE], [wts[n] for n in _LATE], [mom[n] for n in _LATE],
                                         [var[n] for n in _LATE], "adamw_late")))
    for n in _BIG:
        grads[n], delta[n], new_m[n], new_v[n] = (_from_shard(n, a) for a in updated[n])
    summed = dict(zip(_SMALL_ORDER, _unpack(small_sum, small_shapes)))
    loss = summed["loss"].reshape(())
    summed["w_a_up"] = lax.dynamic_slice_in_dim(summed["w_a_up"], chip * 64, 64, axis=1)
    summed["conv_w"] = lax.dynamic_slice_in_dim(summed["conv_w"], chip * 128, 128, axis=1)
    summed["w_if"] = lax.dynamic_slice_in_dim(summed["w_if"], chip * 384, 384, axis=1)
    small_names = _SMALL_REPL + _SMALL_SHARDED
    came_stored = _BLOCK_WEIGHTS + ("w_if",)
    g_stored = [summed[n] if n in came_stored else _stored(n, summed[n].reshape(args[n].shape)) for n in small_names]
    upd = _adamw_small([_stored(n, args[n]) for n in small_names], g_stored,
                       [_stored(n, args["m_" + n]) for n in small_names], [_stored(n, args["v_" + n]) for n in small_names])
    for dst, arrs in zip((grads, delta, new_m, new_v), (g_stored,) + tuple(upd)):
        dst.update({n: _unstored(n, a) for n, a in zip(small_names, arrs)})

    outs = [loss, dx[None]]
    for group in (grads, delta, new_m, new_v):
        outs += [group[n] for n in _WEIGHTS]
    return tuple(outs)
```

```python
import functools

import jax
import jax.numpy as jnp
from jax import lax
from jax.experimental import pallas as pl
from jax.experimental.pallas import tpu as pltpu

F32 = jnp.float32
BF16 = jnp.bfloat16

SEQ = 2048
D_MODEL = 1024
CHUNK = 64
N_CHUNK = SEQ // CHUNK
HEADS = 4
GLA_DK = 64
GLA_DV = 128
ML_DH = 128
D_FF = 4096
EPS = 1e-6
N_CHIP = 4
N_DEV = 8
TOK_TILE = 256
N_TOK_TILE = SEQ // TOK_TILE
SWEEP = 2
assert CHUNK == 64
N_SWEEP = N_CHUNK // SWEEP

PM_W = 2688
PM_XM = 1536
PM_OP = 2048
PM_AL = 2560
GAB_W = 2048
D_IN = 4624
IN_SHARD = D_IN // N_CHIP
IN_ALOW = 1536
IN_XM = 1552
IN_GATES = 2576

ADAM_LR = 0.001
ADAM_B1 = 0.9
ADAM_B2 = 0.999
ADAM_EPS = 1e-08
ADAM_WD = 0.01
ADAM_STEP = 10

VMEM_LIMIT = 56 * 1024 * 1024


def _params(sem=None):
    return pltpu.CompilerParams(dimension_semantics=sem, vmem_limit_bytes=VMEM_LIMIT)


def _dot(a, b, ca, cb):
    return lax.dot_general(a.astype(BF16), b.astype(BF16), (((ca,), (cb,)), ((), ())), preferred_element_type=F32)


def _pmm_nn(a, b):
    return _dot(a, b, 1, 0)


def _pmm_nt(a, b):
    return _dot(a, b, 1, 1)


def _pmm_tn(a, b):
    return _dot(a, b, 0, 0)


def _pcmm(c, x):
    return lax.dot_general(c, x, (((1,), (0,)), ((), ())), precision=lax.Precision.HIGHEST, preferred_element_type=F32)


@jax.custom_vjp
def _mm_nn(a, b):
    return _dot(a, b, 1, 0)


@jax.custom_vjp
def _mm_nt(a, b):
    return _dot(a, b, 1, 1)


@jax.custom_vjp
def _mm_tn(a, b):
    return _dot(a, b, 0, 0)


_mm_nn.defvjp(lambda a, b: (_dot(a, b, 1, 0), (a, b)), lambda r, g: (_mm_nt(g, r[1]), _mm_tn(r[0], g)))
_mm_nt.defvjp(lambda a, b: (_dot(a, b, 1, 1), (a, b)), lambda r, g: (_mm_nn(g, r[1]), _mm_tn(g, r[0])))
_mm_tn.defvjp(lambda a, b: (_dot(a, b, 0, 0), (a, b)), lambda r, g: (_mm_nt(r[1], g), _mm_nn(r[0], g)))


@jax.custom_vjp
def _cmm(c, x):
    return _pcmm(c, x)


_cmm.defvjp(
    lambda c, x: (_pcmm(c, x), c),
    lambda c, g: (jnp.zeros_like(c), lax.dot_general(c, g, (((0,), (0,)), ((), ())), precision=lax.Precision.HIGHEST,
                                                      preferred_element_type=F32)),
)

_PLAIN_OPS = (_pmm_nn, _pmm_nt, _pmm_tn, _pcmm)
_VJP_OPS = (_mm_nn, _mm_nt, _mm_tn, _cmm)


def _sigmoid(x):
    return 0.5 * (jnp.tanh(0.5 * x) + 1.0)


def _log_sigmoid(x):
    return jnp.minimum(x, 0.0) - jnp.log(1.0 + jnp.exp(-jnp.abs(x)))


def _mean(x):
    return jnp.mean(x, axis=-1, keepdims=True)


def _nt(a, b):
    return lax.dot_general(a, b, (((1,), (1,)), ((), ())), preferred_element_type=F32)


def _tn(a, b):
    return lax.dot_general(a, b, (((0,), (0,)), ((), ())), preferred_element_type=F32)


def _mixer_chunk(ops, p, st, pm, xprev8):
    mm_nn, mm_nt, mm_tn, cmm = ops
    n_rows = pm.shape[0]
    n_ch = n_rows // CHUNK
    row = lax.broadcasted_iota(jnp.int32, (n_rows, n_rows), 0)
    col = lax.broadcasted_iota(jnp.int32, (n_rows, n_rows), 1)
    tri = jnp.logical_and((row >> 6) == (col >> 6), row >= col).astype(F32)
    causal = tri[0:CHUNK, 0:CHUNK] > 0.0
    q = pm[:, 0:256]
    k = pm[:, 256:512]
    v = pm[:, 512:1024]
    g = pm[:, 1024:1536]
    xm = pm[:, PM_XM:PM_XM + 512]
    opre = pm[:, PM_OP:PM_OP + 512]
    alow = pm[:, PM_AL:PM_AL + 128]
    hs = range(HEADS)
    cs = range(n_ch)
    pairs = [(i, h) for i in cs for h in hs]
    rs = [slice(i * CHUNK, (i + 1) * CHUNK) for i in cs]
    last = [slice((i + 1) * CHUNK - 1, (i + 1) * CHUNK) for i in cs]
    s6 = [slice(h * GLA_DK, (h + 1) * GLA_DK) for h in hs]
    s12 = [slice(h * 128, (h + 1) * 128) for h in hs]

    xx = jnp.concatenate([xprev8, xm], axis=0)
    pre = p["cb"]
    for j in range(4):
        pre = pre + p["cw"][j:j + 1, :] * xx[5 + j:5 + j + n_rows, :]
    xc = pre * _sigmoid(pre)
    qm = [mm_nn(xc[:, s12[h]], p["wq"][h]) for h in hs]
    km = [mm_nn(xc[:, s12[h]], p["wk"][h]) for h in hs]
    vm = [mm_nn(xm[:, s12[h]], p["wv"][h]) for h in hs]
    qcat = jnp.concatenate(qm, axis=1)
    kcat = jnp.concatenate(km, axis=1)
    vcat = jnp.concatenate(vm, axis=1)
    gates = (mm_nn(qcat, p["wif"][0:512]) + mm_nn(kcat, p["wif"][512:1024]) + mm_nn(vcat, p["wif"][1024:1536])
             + p["bif"])
    lf = _log_sigmoid(gates)
    fc = cmm(tri, lf)
    gates_t = gates.T
    fc_t = fc.T

    la = _log_sigmoid(mm_nn(alow, p["wau"]) + p["bau"]) * (1.0 / 16.0)
    cum = cmm(tri, la)
    cum_last = [cum[last[i], :] for i in cs]
    to_end = jnp.concatenate([cum_last[i] - cum[rs[i], :] for i in cs], axis=0)
    e_pos = jnp.exp(cum)
    e_neg = jnp.exp(-cum)
    qs = q * (GLA_DK ** -0.5)
    qp = qs * e_pos
    qn = qs * e_neg
    kp = k * e_pos
    kn = k * e_neg
    kl = k * jnp.exp(to_end)
    dec = [jnp.exp(cum_last[i]) for i in cs]
    ks = [km[h] * (ML_DH ** -0.5) for h in hs]
    li_c = {(i, h): gates[rs[i], h:h + 1] for i, h in pairs}
    fc_c = {(i, h): fc[rs[i], 4 + h:5 + h] for i, h in pairs}
    f_last = {(i, h): fc[last[i], 4 + h:5 + h] for i, h in pairs}

    a_fwd = {(i, h): mm_nt(qp[rs[i], s6[h]], kn[rs[i], s6[h]]) for i, h in pairs}
    a_bwd = {(i, h): mm_nt(qn[rs[i], s6[h]], kp[rs[i], s6[h]]) for i, h in pairs}
    s_chunk = {(i, h): mm_tn(v[rs[i], s12[h]], kl[rs[i], s6[h]]) for i, h in pairs}
    qk = {(i, h): mm_nt(qm[h][rs[i]], ks[h][rs[i]]) for i, h in pairs}
    a = {ih: f_last[ih] - fc_c[ih] + li_c[ih] for ih in pairs}
    m_loc = {ih: jnp.max(a[ih], axis=0, keepdims=True) for ih in pairs}
    kw = {(i, h): ks[h][rs[i]] * jnp.exp(a[(i, h)] - m_loc[(i, h)]) for i, h in pairs}
    c_chunk = {(i, h): mm_tn(kw[(i, h)], vm[h][rs[i]]) for i, h in pairs}
    mem = {(0, h): st["S"][h] for h in hs}
    c_in = {(0, h): st["C"][h] for h in hs}
    n_in = {(0, h): st["n"][h] for h in hs}
    m_in = {(0, h): st["m"][h][:, 0:1] for h in hs}
    for i, h in pairs:
        mem[(i + 1, h)] = mem[(i, h)] * dec[i][:, s6[h]] + s_chunk[(i, h)]
        m_nx = jnp.maximum(f_last[(i, h)] + m_in[(i, h)], m_loc[(i, h)])
        sp = jnp.exp(f_last[(i, h)] + m_in[(i, h)] - m_nx)
        sl = jnp.exp(m_loc[(i, h)] - m_nx)
        c_in[(i + 1, h)] = sp * c_in[(i, h)] + sl * c_chunk[(i, h)]
        n_in[(i + 1, h)] = sp * n_in[(i, h)] + sl * jnp.sum(kw[(i, h)], axis=0, keepdims=True)
        m_in[(i + 1, h)] = m_nx
    s_new = [mem[(n_ch, h)] for h in hs]
    o_inter = {(i, h): mm_nt(qp[rs[i], s6[h]], mem[(i, h)]) for i, h in pairs}
    q_c = {(i, h): mm_nn(qm[h][rs[i]], c_in[(i, h)]) for i, h in pairs}
    scores = {ih: jnp.where(causal, a_fwd[ih], a_bwd[ih]) for ih in pairs}
    log_d = {(i, h): gates_t[h:h + 1, rs[i]] - jnp.abs(fc_c[(i, h)] - fc_t[4 + h:5 + h, rs[i]]) for i, h in pairs}
    g_int = {ih: fc_c[ih] + m_in[ih] for ih in pairs}
    m_t = {ih: jnp.maximum(g_int[ih], jnp.max(log_d[ih], axis=1, keepdims=True)) for ih in pairs}
    s = {ih: qk[ih] * jnp.exp(log_d[ih] - m_t[ih]) for ih in pairs}
    scl = {ih: jnp.exp(g_int[ih] - m_t[ih]) for ih in pairs}
    o = {(i, h): mm_nn(scores[(i, h)], v[rs[i], s12[h]]) + o_inter[(i, h)] for i, h in pairs}
    num = {(i, h): mm_nn(s[(i, h)], vm[h][rs[i]]) + scl[(i, h)] * q_c[(i, h)] for i, h in pairs}
    o = {ih: o[ih] * lax.rsqrt(_mean(o[ih] * o[ih]) + EPS) * p["ggla"] for ih in pairs}
    gate = g * _sigmoid(g)
    out_a = {(i, h): o[(i, h)] * gate[rs[i], s12[h]] for i, h in pairs}
    den = {(i, h): jnp.sum(s[(i, h)], axis=1, keepdims=True)
           + scl[(i, h)] * jnp.sum(qm[h][rs[i]] * n_in[(i, h)], axis=1, keepdims=True) for i, h in pairs}
    den = {ih: jnp.maximum(jnp.abs(den[ih]), jnp.exp(-m_t[ih])) for ih in pairs}
    open_gate = _sigmoid(opre)
    hc = {(i, h): num[(i, h)] / den[(i, h)] * open_gate[rs[i], s12[h]] for i, h in pairs}
    d0 = {ih: hc[ih] - _mean(hc[ih]) for ih in pairs}
    y = {ih: d0[ih] * lax.rsqrt(_mean(d0[ih] * d0[ih]) + EPS) for ih in pairs}
    skipped = p["skip"] * xc
    out_b = {(i, h): y[(i, h)] * p["gml"][:, s12[h]] + skipped[rs[i], s12[h]] for i, h in pairs}
    ab = jnp.concatenate([jnp.concatenate([out_a[(i, h)] for h in hs] + [out_b[(i, h)] for h in hs], axis=1) for i in cs],
                         axis=0)
    new = {"S": s_new, "C": [c_in[(n_ch, h)] for h in hs], "n": [n_in[(n_ch, h)] for h in hs],
           "m": [jnp.broadcast_to(m_in[(n_ch, h)], (1, ML_DH)) for h in hs]}
    return ab, new


_P_NAMES = ("wau", "bau", "ggla", "cw", "cb", "wq", "wk", "wv", "wif", "bif", "skip", "gml")
_P_SHAPES = {
    "wau": (128, 256), "bau": (1, 256), "ggla": (1, 128), "cw": (4, 512), "cb": (1, 512),
    "wq": (512, 128), "wk": (512, 128), "wv": (512, 128),
    "wif": (1536, 128), "bif": (1, 128), "skip": (1, 512), "gml": (1, 512),
}
_P_BLOCKDIAG = ("wq", "wk", "wv")
_S_NAMES = ("S", "C", "n", "m")
_S_SHAPES = {"S": (HEADS, GLA_DV, GLA_DK), "C": (HEADS, ML_DH, ML_DH), "n": (HEADS, 1, ML_DH), "m": (HEADS, 1, ML_DH)}


def _per_head(ref):
    return [ref[h] for h in range(HEADS)]


def _block_mask():
    r = lax.broadcasted_iota(jnp.int32, (128, 128), 0)
    c = lax.broadcasted_iota(jnp.int32, (128, 128), 1)
    same_block = (r >> 2) == (c >> 2)
    spread = jnp.logical_and(r < 4, (c & 3) == r)
    return same_block.astype(F32), spread.astype(F32)


def _expand_blockdiag(w_ref, dense_ref):
    same_block, spread = _block_mask()
    for h in range(HEADS):
        tiled = _pmm_nn(w_ref[h * 128:(h + 1) * 128, :], spread)
        dense_ref[h] = tiled * same_block


def _collect_blockdiag(ddense_ref, dw_ref):
    same_block, spread = _block_mask()
    for h in range(HEADS):
        dw_ref[h * 128:(h + 1) * 128, :] = lax.dot_general(
            ddense_ref[h] * same_block, spread, (((1,), (1,)), ((), ())), precision=lax.Precision.HIGHEST,
            preferred_element_type=F32)


def _const_spec(shape):
    zeros = (0,) * len(shape)
    return pl.BlockSpec(shape, lambda i: zeros)


def _split(refs, *counts):
    out, at = [], 0
    for c in counts:
        out.append(refs[at:at + c])
        at += c
    assert at == len(refs)
    return out


def _ride(rider, phases, cond, ins, outs, sems):
    if rider is None or not any(hasattr(rider, phase) for phase in phases):
        return
    lands, (send_sems, recv_sems, flush_sems) = sems[:-3], sems[-3:]

    @pl.when(cond)
    def _():
        for phase in phases:
            if phase == "last" and hasattr(rider, "late"):
                rider.late(ins, lands, send_sems, recv_sems)
                rider.flush("late", lands, outs, flush_sems)
            getattr(rider, phase)(ins, lands, send_sems, recv_sems)
            if hasattr(rider, "flush"):
                rider.flush(phase, lands, outs, flush_sems)
        if "last" in phases and not hasattr(rider, "flush"):
            flush = [pltpu.make_async_copy(lands[k], outs[k], flush_sems.at[k]) for k in range(len(outs))]
            for cp in flush:
                cp.start()
            for cp in flush:
                cp.wait()


def _middle_step(rider, n_steps):
    return min(n_steps - 2, int(getattr(rider, "middle_at", 1.0) * n_steps))


def _rider_specs(rider, rider_ins):
    if rider is None:
        return [], [], [], []
    scratch = [pltpu.VMEM(s.shape, s.dtype) for s in list(rider.out_shape) + list(getattr(rider, "work_shape", ()))]
    scratch += [pltpu.SemaphoreType.DMA((rider.n_sems,)), pltpu.SemaphoreType.DMA((rider.n_sems,)),
                pltpu.SemaphoreType.DMA((getattr(rider, "n_flush", len(rider.out_shape)),))]
    in_space = getattr(rider, "in_space", VMEM_WHOLE)
    in_specs = list(in_space) if isinstance(in_space, (list, tuple)) else [in_space] * len(rider_ins)
    return in_specs, [ANY] * len(rider.out_shape), list(rider.out_shape), scratch


def _merge_tile(ab, gab_ref, x_ref, wpa_ref, wpb_ref, wo_ref, g_ref, x1_ref, mix_ref, mg_ref):
    a = ab[:, 0:512]
    b = ab[:, 512:1024]
    for j in range(N_CHIP):
        blk = slice(j * 256, (j + 1) * 256)
        ya = jnp.dot(a, wpa_ref[j], preferred_element_type=F32)
        yb = jnp.dot(b, wpb_ref[j], preferred_element_type=F32)
        sa = _sigmoid(gab_ref[:, j * 256:(j + 1) * 256])
        sb = _sigmoid(gab_ref[:, 1024 + j * 256:1024 + (j + 1) * 256])
        mg_ref[:, blk] = (sa * ya + sb * yb).astype(BF16)
    mix = jnp.dot(mg_ref[...], wo_ref[...], preferred_element_type=F32)
    mix_ref[...] = mix
    mn, _ = _rms_fwd(mix)
    x1_ref[...] = x_ref[...] + mn * g_ref[...]


def _mixer_fwd(pm, p, gab, x, w_pa4, w_pb4, w_o, g_post, rider=None, rider_ins=()):
    n_p = len(_P_NAMES)
    r_in, r_out_specs, r_out_shape, r_sems = _rider_specs(rider, rider_ins)

    def body(*refs):
        ((pm_ref, xprev_ref), p_list, merge_in, ride_in, (ab_ref,), merge_out, so_refs, ride_out, sc_refs, dense_list,
         sems) = _split(refs, 2, n_p, 6, len(r_in), 1, 3, 4, len(r_out_specs), 4, 3, len(r_sems))
        p_refs = dict(zip(_P_NAMES, p_list))
        dense = dict(zip(_P_BLOCKDIAG, dense_list))
        n = pl.program_id(0)
        _ride(rider, ("first",), n == 0, ride_in, ride_out, sems)

        @pl.when(n == 0)
        def _():
            for r in sc_refs:
                r[...] = jnp.zeros_like(r)
            for nm in _P_BLOCKDIAG:
                _expand_blockdiag(p_refs[nm], dense[nm])

        st = {name: _per_head(r) for name, r in zip(_S_NAMES, sc_refs)}
        pv = {nm: (_per_head(dense[nm]) if nm in _P_BLOCKDIAG else p_refs[nm][...]) for nm in _P_NAMES}
        for name, r in zip(_S_NAMES, so_refs):
            for h in range(HEADS):
                r[0, h] = st[name][h]
        xprev8 = jnp.where(n > 0, xprev_ref[CHUNK - 8:CHUNK, :], 0.0)
        ab, st = _mixer_chunk(_PLAIN_OPS, pv, st, pm_ref[...], xprev8)
        ab = ab.astype(BF16)
        ab_ref[...] = ab
        for name, r in zip(_S_NAMES, sc_refs):
            for h in range(HEADS):
                r[h] = st[name][h]
        _merge_tile(ab, *merge_in, *merge_out)
        _ride(rider, ("middle",), n == _middle_step(rider, N_SWEEP), ride_in, ride_out, sems)
        _ride(rider, ("last",), n == N_SWEEP - 1, ride_in, ride_out, sems)

    rows = lambda width: pl.BlockSpec((SWEEP * CHUNK, width), lambda i: (i, 0))
    in_specs = [rows(PM_W), pl.BlockSpec((CHUNK, 512), lambda i: (jnp.maximum(SWEEP * i - 1, 0), PM_XM // 512))]
    in_specs += [_const_spec(_P_SHAPES[nm]) for nm in _P_NAMES]
    in_specs += [rows(GAB_W), rows(D_MODEL), _once((N_CHIP, 512, 256)), _once((N_CHIP, 512, 256)), _once((D_MODEL, D_MODEL)),
                 _once((1, D_MODEL))] + r_in
    out_specs = [rows(1024), rows(D_MODEL), rows(D_MODEL), rows(D_MODEL)]
    out_shape = [jax.ShapeDtypeStruct((SEQ, 1024), BF16), jax.ShapeDtypeStruct((SEQ, D_MODEL), F32),
                 jax.ShapeDtypeStruct((SEQ, D_MODEL), F32), jax.ShapeDtypeStruct((SEQ, D_MODEL), BF16)]
    for nm in _S_NAMES:
        shp = _S_SHAPES[nm]
        out_specs.append(pl.BlockSpec((1,) + shp, lambda i: (i, 0, 0, 0)))
        out_shape.append(jax.ShapeDtypeStruct((N_SWEEP,) + shp, F32))
    return pl.pallas_call(
        body, grid=(N_SWEEP,), in_specs=in_specs, out_specs=out_specs + r_out_specs, out_shape=out_shape + r_out_shape,
        scratch_shapes=[pltpu.VMEM(_S_SHAPES[nm], F32) for nm in _S_NAMES]
        + [pltpu.VMEM((HEADS, 128, 128), F32) for _ in _P_BLOCKDIAG] + r_sems,
        compiler_params=_params(("arbitrary",)), name="mixer_fwd",
    )(pm, pm, *[p[nm] for nm in _P_NAMES], gab, x, w_pa4, w_pb4, w_o, g_post, *rider_ins)


def _mixer_bwd(pm, dab, states, p, rider=None, rider_ins=()):
    n_p = len(_P_NAMES)
    r_in, r_out_specs, r_out_shape, r_sems = _rider_specs(rider, rider_ins)

    def body(*refs):
        ((pm_ref, xprev_ref, dab_ref), si_refs, p_list, ride_in, (dpm_ref,), dp_list, ride_out, ds_refs, (carry_ref,),
         dense_list, ddense_list, sems) = _split(refs, 3, 4, n_p, len(r_in), 1, n_p, len(r_out_specs), 4, 1, 3, 3, len(r_sems))
        p_refs = dict(zip(_P_NAMES, p_list))
        dp_refs = dict(zip(_P_NAMES, dp_list))
        dense = dict(zip(_P_BLOCKDIAG, dense_list))
        ddense = dict(zip(_P_BLOCKDIAG, ddense_list))
        i = pl.program_id(0)
        blk = N_SWEEP - 1 - i
        _ride(rider, ("first",), i == 0, ride_in, ride_out, sems)

        @pl.when(i == 0)
        def _():
            for r in ds_refs:
                r[...] = jnp.zeros_like(r)
            for nm in _P_NAMES:
                if nm in _P_BLOCKDIAG:
                    ddense[nm][...] = jnp.zeros_like(ddense[nm])
                    _expand_blockdiag(p_refs[nm], dense[nm])
                else:
                    dp_refs[nm][...] = jnp.zeros_like(dp_refs[nm])
            carry_ref[...] = jnp.zeros_like(carry_ref)

        pv = {nm: (_per_head(dense[nm]) if nm in _P_BLOCKDIAG else p_refs[nm][...]) for nm in _P_NAMES}
        dst = {name: _per_head(r) for name, r in zip(_S_NAMES, ds_refs)}
        st = {name: [r[0, h] for h in range(HEADS)] for name, r in zip(_S_NAMES, si_refs)}
        xprev8 = jnp.where(blk > 0, xprev_ref[CHUNK - 8:CHUNK, :], 0.0)
        _, vjp = jax.vjp(functools.partial(_mixer_chunk, _VJP_OPS), pv, st, pm_ref[...], xprev8)
        dp_sum, dst, dpm, dxprev8 = vjp((dab_ref[...], dst))
        reach = jnp.concatenate([jnp.zeros((SWEEP * CHUNK - 8, 512), F32), carry_ref[...]], axis=0)
        dpm_ref[:, 0:PM_XM] = dpm[:, 0:PM_XM].astype(BF16)
        dpm_ref[:, PM_XM:PM_XM + 512] = (dpm[:, PM_XM:PM_XM + 512] + reach).astype(BF16)
        dpm_ref[:, PM_XM + 512:PM_W] = dpm[:, PM_XM + 512:PM_W].astype(BF16)
        carry_ref[...] = dxprev8
        for name, r in zip(_S_NAMES, ds_refs):
            for h in range(HEADS):
                r[h] = dst[name][h]
        for nm in _P_NAMES:
            if nm in _P_BLOCKDIAG:
                for h in range(HEADS):
                    ddense[nm][h] += dp_sum[nm][h]
            else:
                dp_refs[nm][...] += dp_sum[nm]

        @pl.when(i == N_SWEEP - 1)
        def _():
            for nm in _P_BLOCKDIAG:
                _collect_blockdiag(ddense[nm], dp_refs[nm])

        _ride(rider, ("early",), i == 1, ride_in, ride_out, sems)
        _ride(rider, ("middle",), i == _middle_step(rider, N_SWEEP), ride_in, ride_out, sems)
        _ride(rider, ("last",), i == N_SWEEP - 1, ride_in, ride_out, sems)

    rev = lambda i: (N_SWEEP - 1 - i, 0)
    in_specs = [pl.BlockSpec((SWEEP * CHUNK, PM_W), rev),
                pl.BlockSpec((CHUNK, 512), lambda i: (jnp.maximum(SWEEP * (N_SWEEP - 1 - i) - 1, 0), PM_XM // 512)),
                pl.BlockSpec((SWEEP * CHUNK, 1024), rev)]
    for nm in _S_NAMES:
        in_specs.append(pl.BlockSpec((1,) + _S_SHAPES[nm], lambda i: (N_SWEEP - 1 - i, 0, 0, 0)))
    in_specs += [_const_spec(_P_SHAPES[nm]) for nm in _P_NAMES] + r_in
    out_specs = [pl.BlockSpec((SWEEP * CHUNK, PM_W), rev)] + [_const_spec(_P_SHAPES[nm]) for nm in _P_NAMES]
    out_shape = [jax.ShapeDtypeStruct((SEQ, PM_W), BF16)] + [jax.ShapeDtypeStruct(_P_SHAPES[nm], F32) for nm in _P_NAMES]
    res = pl.pallas_call(
        body, grid=(N_SWEEP,), in_specs=in_specs, out_specs=out_specs + r_out_specs, out_shape=out_shape + r_out_shape,
        scratch_shapes=[pltpu.VMEM(_S_SHAPES[nm], F32) for nm in _S_NAMES] + [pltpu.VMEM((8, 512), F32)]
        + [pltpu.VMEM((HEADS, 128, 128), F32) for _ in range(2 * len(_P_BLOCKDIAG))] + r_sems,
        compiler_params=_params(("arbitrary",)), name="mixer_bwd",
    )(pm, pm, dab, *states, *[p[nm] for nm in _P_NAMES], *rider_ins)
    return res[0], dict(zip(_P_NAMES, res[1:1 + n_p])), res[1 + n_p:]


def _tok(width):
    return pl.BlockSpec((TOK_TILE, width), lambda i: (i, 0))


def _once(shape):
    zeros = (0,) * len(shape)
    return pl.BlockSpec(shape, lambda i: zeros, pipeline_mode=pl.Buffered(1))


def _rms_fwd(x):
    r = lax.rsqrt(_mean(x * x) + EPS)
    return x * r, r


def _rms_bwd(dy, xn, r, g):
    gd = dy * g
    return r * (gd - xn * _mean(xn * gd))


def _tiled_call(body, in_specs, out_specs, out_shape, args, name, rider=None, rider_ins=(), scratch=()):
    r_in, r_out_specs, r_out_shape, r_scratch = _rider_specs(rider, rider_ins)
    n_in, n_out = len(in_specs), len(out_specs)

    def hosted(*refs):
        ins, ride_in, outs, ride_out, own, r_scr = _split(refs, n_in, len(r_in), n_out, len(r_out_specs), len(scratch),
                                                          len(r_scratch))
        i = pl.program_id(0)
        _ride(rider, ("first",), i == 0, ride_in, ride_out, r_scr)
        body(*ins, *outs, *own)
        _ride(rider, ("early",), i == 1, ride_in, ride_out, r_scr)
        _ride(rider, ("middle",), i == _middle_step(rider, N_TOK_TILE), ride_in, ride_out, r_scr)
        _ride(rider, ("last",), i == N_TOK_TILE - 1, ride_in, ride_out, r_scr)

    res = pl.pallas_call(
        hosted, grid=(N_TOK_TILE,), in_specs=list(in_specs) + r_in, out_specs=list(out_specs) + r_out_specs,
        out_shape=list(out_shape) + r_out_shape, scratch_shapes=list(scratch) + r_scratch,
        compiler_params=_params(("arbitrary",)), name=name,
    )(*args, *rider_ins)
    return res[:n_out], res[n_out:]


def _join_rows(w4_ref, wt_ref):
    @pl.when(pl.program_id(0) == 0)
    def _():
        for j in range(N_CHIP):
            wt_ref[j * IN_SHARD:(j + 1) * IN_SHARD, :] = w4_ref[j]


def _joined_scratch():
    return [pltpu.VMEM((D_IN, D_MODEL), BF16)]


def _in_proj(x, g_pre, w4_in, rider=None, rider_ins=()):
    def body(x_ref, g_ref, w4_ref, pm_ref, gab_ref, h_ref, wt_ref):
        _join_rows(w4_ref, wt_ref)
        xn, _ = _rms_fwd(x_ref[...])
        h = (xn * g_ref[...]).astype(BF16)
        h_ref[...] = h
        pm_ref[:, 0:PM_XM] = _nt(h, wt_ref[0:IN_ALOW, :])
        pm_ref[:, PM_XM:PM_AL] = _nt(h, wt_ref[IN_XM:IN_GATES, :])
        pm_ref[:, PM_AL:PM_W] = _nt(h, wt_ref[IN_ALOW:IN_ALOW + 128, :])
        gab_ref[...] = _nt(h, wt_ref[IN_GATES:D_IN, :])

    return _tiled_call(
        body, [_tok(D_MODEL), _once((1, D_MODEL)), _once((N_CHIP, IN_SHARD, D_MODEL))],
        [_tok(PM_W), _tok(GAB_W), _tok(D_MODEL)],
        [jax.ShapeDtypeStruct((SEQ, PM_W), F32), jax.ShapeDtypeStruct((SEQ, GAB_W), F32),
         jax.ShapeDtypeStruct((SEQ, D_MODEL), BF16)], (x, g_pre, w4_in), "in_proj", rider, rider_ins, _joined_scratch())


def _mlp(x1, target, g_pre, g_post, w_up4, w_down_a4, w_down_b4):
    def body(x1_ref, t_ref, gpre_ref, gpost_ref, wup_ref, wda_ref, wdb_ref,
             dx1_ref, u_ref, dd_ref, h2_ref, dpre_ref, dgpost_ref, dgpre_ref, loss_ref):
        @pl.when(pl.program_id(0) == 0)
        def _():
            dgpost_ref[...] = jnp.zeros_like(dgpost_ref)
            dgpre_ref[...] = jnp.zeros_like(dgpre_ref)
            loss_ref[...] = jnp.zeros_like(loss_ref)

        x1 = x1_ref[...]
        gpre = gpre_ref[...]
        gpost = gpost_ref[...]
        xn2, r2 = _rms_fwd(x1)
        h2 = (xn2 * gpre).astype(BF16)
        h2_ref[...] = h2
        rl = []
        d = jnp.zeros((TOK_TILE, D_MODEL), F32)
        for j in range(N_CHIP):
            blk = slice(j * 1024, (j + 1) * 1024)
            r = jnp.maximum(jnp.dot(h2, wup_ref[j], preferred_element_type=F32), 0.0)
            rl.append(r)
            u = (r * r).astype(BF16)
            u_ref[:, blk] = u
            d = d + jnp.dot(u[:, 0:512], wda_ref[j], preferred_element_type=F32)
            d = d + jnp.dot(u[:, 512:1024], wdb_ref[j], preferred_element_type=F32)
        dn, r3 = _rms_fwd(d)
        diff = x1 + dn * gpost - t_ref[...]
        loss_ref[...] += jnp.sum(diff * diff, keepdims=True) * (0.5 / D_MODEL)
        dy = diff * (1.0 / D_MODEL)
        dgpost_ref[...] += jnp.sum(dy * dn, axis=0, keepdims=True)
        dd = _rms_bwd(dy, dn, r3, gpost).astype(BF16)
        dd_ref[...] = dd
        dh2 = jnp.zeros((TOK_TILE, D_MODEL), F32)
        for j in range(N_CHIP):
            blk = slice(j * 1024, (j + 1) * 1024)
            du = jnp.concatenate([_nt(dd, wda_ref[j]), _nt(dd, wdb_ref[j])], axis=1)
            dpre = (du * (2.0 * rl[j])).astype(BF16)
            dpre_ref[:, blk] = dpre
            dh2 = dh2 + _nt(dpre, wup_ref[j])
        dgpre_ref[...] += jnp.sum(dh2 * xn2, axis=0, keepdims=True)
        dx1_ref[...] = dy + _rms_bwd(dh2, xn2, r2, gpre)

    acc = pl.BlockSpec((1, D_MODEL), lambda i: (0, 0))
    return pl.pallas_call(
        body, grid=(N_TOK_TILE,),
        in_specs=[_tok(D_MODEL), _tok(D_MODEL), _once((1, D_MODEL)), _once((1, D_MODEL)),
                  _once((N_CHIP, D_MODEL, 1024)), _once((N_CHIP, 512, D_MODEL)), _once((N_CHIP, 512, D_MODEL))],
        out_specs=[_tok(D_MODEL), _tok(D_FF), _tok(D_MODEL), _tok(D_MODEL), _tok(D_FF), acc, acc,
                   pl.BlockSpec((1, 128), lambda i: (0, 0))],
        out_shape=[jax.ShapeDtypeStruct((SEQ, D_MODEL), F32), jax.ShapeDtypeStruct((SEQ, D_FF), BF16),
                   jax.ShapeDtypeStruct((SEQ, D_MODEL), BF16), jax.ShapeDtypeStruct((SEQ, D_MODEL), BF16),
                   jax.ShapeDtypeStruct((SEQ, D_FF), BF16), jax.ShapeDtypeStruct((1, D_MODEL), F32),
                   jax.ShapeDtypeStruct((1, D_MODEL), F32), jax.ShapeDtypeStruct((1, 128), F32)],
        compiler_params=_params(("arbitrary",)), name="mlp_fwd_bwd",
    )(x1, target, g_pre, g_post, w_up4, w_down_a4, w_down_b4)


def _merge_bwd(dx1, mix, ab, gab, merged, w_pa4, w_pb4, w_o, g_post):
    def body(dx1_ref, mix_ref, ab_ref, gab_ref, mg_ref, wpa_ref, wpb_ref, wo_ref, g_ref,
             dgab_ref, dab_ref, dg_ref, dwpa_ref, dwpb_ref, dwo_ref, acc_pa, acc_pb, acc_o):
        @pl.when(pl.program_id(0) == 0)
        def _():
            dg_ref[...] = jnp.zeros_like(dg_ref)
            acc_pa[...] = jnp.zeros_like(acc_pa)
            acc_pb[...] = jnp.zeros_like(acc_pb)
            acc_o[...] = jnp.zeros_like(acc_o)

        dx1 = dx1_ref[...]
        mn, r = _rms_fwd(mix_ref[...])
        dg_ref[...] += jnp.sum(dx1 * mn, axis=0, keepdims=True)
        dmix = _rms_bwd(dx1, mn, r, g_ref[...]).astype(BF16)
        acc_o[...] += _tn(mg_ref[...], dmix)
        dmerged = _nt(dmix, wo_ref[...])
        a = ab_ref[:, 0:512]
        b = ab_ref[:, 512:1024]
        da = jnp.zeros((TOK_TILE, 512), F32)
        db = jnp.zeros((TOK_TILE, 512), F32)
        dyas, dybs = [], []
        for j in range(N_CHIP):
            blk = slice(j * 256, (j + 1) * 256)
            blk_b = slice(1024 + j * 256, 1024 + (j + 1) * 256)
            dm = dmerged[:, blk]
            ya = jnp.dot(a, wpa_ref[j], preferred_element_type=F32)
            yb = jnp.dot(b, wpb_ref[j], preferred_element_type=F32)
            sa = _sigmoid(gab_ref[:, blk])
            sb = _sigmoid(gab_ref[:, blk_b])
            dya = (dm * sa).astype(BF16)
            dyb = (dm * sb).astype(BF16)
            dyas.append(dya)
            dybs.append(dyb)
            dgab_ref[:, blk] = (dm * ya * sa * (1.0 - sa)).astype(BF16)
            dgab_ref[:, blk_b] = (dm * yb * sb * (1.0 - sb)).astype(BF16)
            da = da + _nt(dya, wpa_ref[j])
            db = db + _nt(dyb, wpb_ref[j])
        dab_ref[:, 0:512] = da
        dab_ref[:, 512:1024] = db
        acc_pa[...] += _tn(a, jnp.concatenate(dyas, axis=1))
        acc_pb[...] += _tn(b, jnp.concatenate(dybs, axis=1))

        @pl.when(pl.program_id(0) == N_TOK_TILE - 1)
        def _():
            dwo_ref[...] = acc_o[...].astype(BF16)
            for j in range(N_CHIP):
                dwpa_ref[j] = acc_pa[:, j * 256:(j + 1) * 256].astype(BF16)
                dwpb_ref[j] = acc_pb[:, j * 256:(j + 1) * 256].astype(BF16)

    whole = lambda shape: pl.BlockSpec(shape, lambda i: (0,) * len(shape))
    return pl.pallas_call(
        body, grid=(N_TOK_TILE,),
        in_specs=[_tok(D_MODEL), _tok(D_MODEL), _tok(1024), _tok(GAB_W), _tok(D_MODEL), _once((N_CHIP, 512, 256)),
                  _once((N_CHIP, 512, 256)), _once((D_MODEL, D_MODEL)), _once((1, D_MODEL))],
        out_specs=[_tok(GAB_W), _tok(1024), whole((1, D_MODEL)), whole((N_CHIP, 512, 256)), whole((N_CHIP, 512, 256)),
                   whole((D_MODEL, D_MODEL))],
        out_shape=[jax.ShapeDtypeStruct((SEQ, GAB_W), BF16), jax.ShapeDtypeStruct((SEQ, 1024), F32),
                   jax.ShapeDtypeStruct((1, D_MODEL), F32), jax.ShapeDtypeStruct((N_CHIP, 512, 256), BF16),
                   jax.ShapeDtypeStruct((N_CHIP, 512, 256), BF16), jax.ShapeDtypeStruct((D_MODEL, D_MODEL), BF16)],
        scratch_shapes=[pltpu.VMEM((512, D_MODEL), F32), pltpu.VMEM((512, D_MODEL), F32),
                        pltpu.VMEM((D_MODEL, D_MODEL), F32)],
        compiler_params=_params(("arbitrary",)), name="merge_bwd",
    )(dx1, mix, ab, gab, merged, w_pa4, w_pb4, w_o, g_post)


def _in_proj_bwd(dpm, dgab, x, dx1, g_pre, w4_in, rider=None, rider_ins=()):
    def body(dpm_ref, dgab_ref, x_ref, dx1_ref, g_ref, w4_ref, dx_ref, dg_ref, wt_ref):
        _join_rows(w4_ref, wt_ref)

        @pl.when(pl.program_id(0) == 0)
        def _():
            dg_ref[...] = jnp.zeros_like(dg_ref)

        dh = jnp.dot(dpm_ref[:, 0:PM_XM], wt_ref[0:IN_ALOW, :], preferred_element_type=F32)
        dh = dh + jnp.dot(dpm_ref[:, PM_XM:PM_AL], wt_ref[IN_XM:IN_GATES, :], preferred_element_type=F32)
        dh = dh + jnp.dot(dpm_ref[:, PM_AL:PM_W], wt_ref[IN_ALOW:IN_ALOW + 128, :], preferred_element_type=F32)
        dh = dh + jnp.dot(dgab_ref[...], wt_ref[IN_GATES:D_IN, :], preferred_element_type=F32)
        xn, r = _rms_fwd(x_ref[...])
        dg_ref[...] += jnp.sum(dh * xn, axis=0, keepdims=True)
        dx_ref[...] = dx1_ref[...] + _rms_bwd(dh, xn, r, g_ref[...])

    return _tiled_call(
        body, [_tok(PM_W), _tok(GAB_W), _tok(D_MODEL), _tok(D_MODEL), _once((1, D_MODEL)),
               _once((N_CHIP, IN_SHARD, D_MODEL))],
        [_tok(D_MODEL), pl.BlockSpec((1, D_MODEL), lambda i: (0, 0))],
        [jax.ShapeDtypeStruct((SEQ, D_MODEL), F32), jax.ShapeDtypeStruct((1, D_MODEL), F32)],
        (dpm, dgab, x, dx1, g_pre, w4_in), "in_proj_bwd", rider, rider_ins, _joined_scratch())


def _dw_in(dpm, dgab, h):
    n_pm = PM_AL // 512
    n_blk = n_pm + GAB_W // 512

    def place(o_ref, rows, lo, hi):
        for j in range(N_CHIP):
            a, b = max(lo, j * IN_SHARD), min(hi, (j + 1) * IN_SHARD)
            if a < b:
                o_ref[j, a - j * IN_SHARD:b - j * IN_SHARD, :] = rows(a - lo, b - lo)

    def body(dpm_ref, dgab_ref, dal_ref, h_ref, o_ref, blk_ref):
        i = pl.program_id(0)

        @pl.when(i < n_pm)
        def _():
            blk_ref[...] = _tn(dpm_ref[...], h_ref[...]).astype(BF16)

        @pl.when(i >= n_pm)
        def _():
            blk_ref[...] = _tn(dgab_ref[...], h_ref[...]).astype(BF16)

        for k in range(n_blk):
            off = k * 512 + (IN_XM - IN_ALOW) * (k >= IN_ALOW // 512)

            @pl.when(i == k)
            def _():
                place(o_ref, lambda a, b: blk_ref[a:b, :], off, off + 512)

        @pl.when(i == 0)
        def _():
            a_low = _tn(dal_ref[...], h_ref[...])[0:IN_XM - IN_ALOW].astype(BF16)
            place(o_ref, lambda a, b: a_low[a:b], IN_ALOW, IN_XM)

    return pl.pallas_call(
        body, grid=(n_blk,),
        in_specs=[pl.BlockSpec((SEQ, 512), lambda i: (0, jnp.minimum(i, n_pm - 1))),
                  pl.BlockSpec((SEQ, 512), lambda i: (0, jnp.maximum(i - n_pm, 0))),
                  pl.BlockSpec((SEQ, 128), lambda i: (0, PM_AL // 128)),
                  _once((SEQ, D_MODEL))],
        out_specs=pl.BlockSpec((N_CHIP, IN_SHARD, D_MODEL), lambda i: (0, 0, 0)),
        out_shape=jax.ShapeDtypeStruct((N_CHIP, IN_SHARD, D_MODEL), BF16),
        scratch_shapes=[pltpu.VMEM((512, D_MODEL), BF16)],
        compiler_params=_params(("arbitrary",)), name="dw_in",
    )(dpm, dgab, dpm, h)


def _tn_matmul(a, b, name, shards=1, tm=1024, rider=None, rider_ins=()):
    m, n = a.shape[1], b.shape[1]
    tm = min(tm, m)
    tn = n // shards if shards > 1 else min(n, 1024)
    steps_i, steps_j = m // tm, n // tn
    r_in, r_out_specs, r_out_shape, r_scratch = _rider_specs(rider, rider_ins)

    def body(*refs):
        (a_ref, b_ref), ride_in, (o_ref,), ride_out, scratch = _split(refs, 2, len(r_in), 1, len(r_out_specs), len(r_scratch))
        step = pl.program_id(0) * steps_j + pl.program_id(1)
        _ride(rider, ("first",), step == 0, ride_in, ride_out, scratch)
        o_ref[...] = _tn(a_ref[...], b_ref[...]).astype(BF16)
        _ride(rider, ("middle", "last"), step == steps_i * steps_j - 1, ride_in, ride_out, scratch)

    if shards > 1:
        out_spec = pl.BlockSpec((None, tm, tn), lambda i, j: (j, i, 0))
        out_shape = jax.ShapeDtypeStruct((shards, m, tn), BF16)
    else:
        out_spec = pl.BlockSpec((tm, tn), lambda i, j: (i, j))
        out_shape = jax.ShapeDtypeStruct((m, n), BF16)
    res = pl.pallas_call(
        body, grid=(steps_i, steps_j),
        in_specs=[pl.BlockSpec((SEQ, tm), lambda i, j: (0, i)), pl.BlockSpec((SEQ, tn), lambda i, j: (0, j))] + r_in,
        out_specs=[out_spec] + r_out_specs, out_shape=[out_shape] + r_out_shape, scratch_shapes=r_scratch,
        compiler_params=_params(("arbitrary", "arbitrary")), name=name,
    )(a, b, *rider_ins)
    return res[0] if rider is None else (res[0], res[1:])


MESH = pl.DeviceIdType.MESH
ANY = pl.BlockSpec(memory_space=pl.ANY)
VMEM_WHOLE = pl.BlockSpec(memory_space=pltpu.VMEM)

_BIG = ("w_in", "w_pa", "w_pb", "w_o", "w_up", "w_down")
_BIG_SHARD = {"w_in": (IN_SHARD, D_MODEL), "w_pa": (512, 256), "w_pb": (512, 256), "w_o": (256, D_MODEL),
              "w_up": (D_MODEL, 1024), "w_down": (1024, D_MODEL),
              "w_down_a": (512, D_MODEL), "w_down_b": (512, D_MODEL)}
_BIG_SPLIT = {"w_in": 1, "w_pa": 0, "w_pb": 0, "w_o": 0, "w_up": 0, "w_down": 0, "w_down_a": 0, "w_down_b": 0}


def _half(ref, e, name, lead=0, part=None):
    axis = _BIG_SPLIT[name]
    size = _BIG_SHARD[name][axis] // 2
    start = e * size
    if part is not None:
        size //= 2
        start = start + part * size
    start = pl.multiple_of(start, 128 if axis == 1 else 16)
    idx = [pl.ds(0, ref.shape[a]) for a in range(lead)]
    idx += [pl.ds(start, size), pl.ds(0, _BIG_SHARD[name][1])] if axis == 0 else [pl.ds(0, _BIG_SHARD[name][0]), pl.ds(start, size)]
    return ref.at[tuple(idx)]


def _half_shape(name):
    r, c = _BIG_SHARD[name]
    return (r // 2, c) if _BIG_SPLIT[name] == 0 else (r, c // 2)


def _remote(src, dst, send_sems, recv_sems, k, to):
    return pltpu.make_async_remote_copy(src_ref=src, dst_ref=dst, send_sem=send_sems.at[k], recv_sem=recv_sems.at[k],
                                        device_id=to, device_id_type=MESH)


def _mesh_place():
    x, y, c = lax.axis_index("x"), lax.axis_index("y"), lax.axis_index("c")
    return x, y, c, [(1 - x, y), (x, 1 - y), (1 - x, 1 - y)]


class _Gather:
    def __init__(self, names, small=(), middle_at=0.5):
        self.middle_at = middle_at
        self.names = tuple(names)
        self.nb = len(self.names)
        self.n = self.nb + len(small)
        self.n_sems = 8 * self.nb + 3 * len(small)
        self.n_flush = 6 * self.nb + len(small)
        self.out_shape = [jax.ShapeDtypeStruct((N_CHIP,) + _BIG_SHARD[nm], BF16) for nm in self.names]
        self.out_shape += [jax.ShapeDtypeStruct((N_CHIP,) + s.shape, s.dtype) for s in small]
        self.in_space = [self._in_spec(nm) for nm in self.names] + [VMEM_WHOLE] * len(small)

    @staticmethod
    def _in_spec(name):
        if name in ("w_down_a", "w_down_b"):
            half = 0 if name == "w_down_a" else 1
            return pl.BlockSpec(_BIG_SHARD[name], lambda *_: (half, 0), pipeline_mode=pl.Buffered(1))
        return VMEM_WHOLE

    def _copies(self, ins, outs, ss, rs, k):
        x, y, c, _ = _mesh_place()
        name = self.names[k]
        me, xn, yn, dg = 2 * x + y, 2 * (1 - x) + y, 2 * x + (1 - y), 2 * (1 - x) + (1 - y)
        to_x, to_y, sibling = (1 - x, y, c), (x, 1 - y, c), (x, y, 1 - c)

        def region(slot, e, part=None):
            return _half(outs[k].at[slot], e, name, part=part)

        def copy(pair, src, dst, to):
            return _remote(src, dst, ss, rs, 8 * k + pair, to)

        mine = region(me, c)
        sent = [copy(0, mine, mine, to_x), copy(1, mine, mine, to_y),
                copy(2, region(xn, c, 0), region(xn, c, 0), to_y), copy(3, region(yn, c, 1), region(yn, c, 1), to_x),
                copy(4, region(xn, c), region(xn, c), sibling), copy(5, region(yn, c), region(yn, c), sibling),
                copy(6, region(dg, c, 0), region(dg, c, 0), sibling), copy(7, region(dg, c, 1), region(dg, c, 1), sibling)]
        landing = [region(xn, c), region(yn, c), region(dg, c, 0), region(dg, c, 1),
                   region(xn, 1 - c), region(yn, 1 - c), region(dg, 1 - c, 0), region(dg, 1 - c, 1)]
        received = [copy(pair, dst, dst, sibling) for pair, dst in enumerate(landing)]
        return sent, received

    def _small(self, ins, outs, ss, rs, k, j, peer, slot, c):
        return _remote(ins[k], outs[k].at[slot], ss, rs, 8 * self.nb + 3 * (k - self.nb) + j, (*peer, c))

    def flush(self, phase, lands, outs, fs):
        x, y, c, _ = _mesh_place()
        me, xn, yn, dg = 2 * x + y, 2 * (1 - x) + y, 2 * x + (1 - y), 2 * (1 - x) + (1 - y)

        def pieces(k):
            name = self.names[k]
            spots = [lambda r: r.at[me], lambda r: _half(r.at[xn], c, name), lambda r: _half(r.at[yn], c, name),
                     lambda r: _half(r.at[xn], 1 - c, name), lambda r: _half(r.at[yn], 1 - c, name), lambda r: r.at[dg]]
            return [pltpu.make_async_copy(spot(lands[k]), spot(outs[k]), fs.at[6 * k + t]) for t, spot in enumerate(spots)]

        ready = {"first": (0,), "middle": (1, 2), "late": (3, 4), "last": (5,)}[phase]
        for k in range(self.nb):
            cps = pieces(k)
            for t in ready:
                cps[t].start()
        if phase == "last":
            small = [pltpu.make_async_copy(lands[k], outs[k], fs.at[6 * self.nb + k - self.nb]) for k in range(self.nb, self.n)]
            for cp in small:
                cp.start()
            for k in range(self.nb):
                for cp in pieces(k):
                    cp.wait()
            for cp in small:
                cp.wait()

    def first(self, ins, outs, ss, rs):
        x, y, c, peers = _mesh_place()
        me = 2 * x + y
        for k in range(self.nb):
            outs[k][me] = ins[k][...].astype(BF16)
            sent, _ = self._copies(ins, outs, ss, rs, k)
            sent[0].start()
            sent[1].start()
        for k in range(self.nb, self.n):
            for j, peer in enumerate(peers):
                self._small(ins, outs, ss, rs, k, j, peer, me, c).start()
            outs[k][me] = ins[k][...]

    def middle(self, ins, outs, ss, rs):
        for k in range(self.nb):
            sent, received = self._copies(ins, outs, ss, rs, k)
            for pair in (0, 1):
                received[pair].wait_recv()
                sent[2 + pair].start()
                sent[4 + pair].start()

    def late(self, ins, outs, ss, rs):
        for k in range(self.nb):
            _, received = self._copies(ins, outs, ss, rs, k)
            for pair in (4, 5):
                received[pair].wait_recv()

    def last(self, ins, outs, ss, rs):
        x, y, c, peers = _mesh_place()
        for k in range(self.nb):
            sent, received = self._copies(ins, outs, ss, rs, k)
            for pair in (2, 3):
                received[pair].wait_recv()
                sent[4 + pair].start()
        for k in range(self.nb):
            sent, received = self._copies(ins, outs, ss, rs, k)
            for pair in (6, 7):
                received[pair].wait_recv()
            for cp in sent:
                cp.wait_send()
        for k in range(self.nb, self.n):
            for j, (px, py) in enumerate(peers):
                self._small(ins, outs, ss, rs, k, j, (px, py), 2 * px + py, c).wait_recv()
                self._small(ins, outs, ss, rs, k, j, (px, py), 2 * x + y, c).wait_send()


def _run_alone(rider, ins, name):
    r_in, r_out_specs, r_out_shape, r_scratch = _rider_specs(rider, ins)

    def body(*refs):
        ride_in, ride_out, scratch = _split(refs, len(r_in), len(r_out_specs), len(r_scratch))
        _ride(rider, ("first", "middle", "last"), pl.program_id(0) == 0, ride_in, ride_out, scratch)

    return pl.pallas_call(
        body, grid=(1,), in_specs=r_in, out_specs=r_out_specs, out_shape=r_out_shape, scratch_shapes=r_scratch,
        compiler_params=_params(("arbitrary",)), name=name,
    )(*ins)


class _Presum:
    in_space = ANY

    def __init__(self, names, base=0):
        self.names = tuple(names)
        self.n = len(self.names)
        self.base = base
        self.n_sems = 3 * self.n
        self.out_shape = [jax.ShapeDtypeStruct((N_CHIP,) + _half_shape(nm), BF16) for nm in self.names]
        self.work_shape = self.out_shape + self.out_shape

    def _stage(self, ins, bufs, ss, k, e, which):
        n = self.n
        return pltpu.make_async_copy(_half(ins[k], e, self.names[k], lead=1), bufs[which * n + k],
                                     ss.at[self.base + which * n + k])

    def _give(self, bufs, ss, rs, k, sibling):
        return _remote(bufs[self.n + k], bufs[k], ss, rs, self.base + k, sibling)

    def first(self, ins, bufs, ss, rs):
        x, y, c, _ = _mesh_place()
        for k in range(self.n):
            self._stage(ins, bufs, ss, k, 1 - c, 1).start()
        for k in range(self.n):
            self._stage(ins, bufs, ss, k, c, 2).start()
        for k in range(self.n):
            self._stage(ins, bufs, ss, k, 1 - c, 1).wait()
            self._give(bufs, ss, rs, k, (x, y, 1 - c)).start()

    def middle(self, ins, bufs, ss, rs):
        pass

    def last(self, ins, bufs, ss, rs):
        x, y, c, _ = _mesh_place()
        for k in range(self.n):
            self._give(bufs, ss, rs, k, (x, y, 1 - c)).wait_recv()
            self._stage(ins, bufs, ss, k, c, 2).wait()

            @pl.loop(0, N_CHIP)
            def _(j):
                bufs[k][j] = (bufs[k][j].astype(F32) + bufs[2 * self.n + k][j].astype(F32)).astype(BF16)
        for k in range(self.n):
            self._give(bufs, ss, rs, k, (x, y, 1 - c)).wait_send()


class _ReduceRelay:
    middle_at = 0.75

    def __init__(self, names, base=0):
        self.names = tuple(names)
        self.n = len(self.names)
        self.base = base
        self.n_sems = 6 * self.n
        self.out_shape = [jax.ShapeDtypeStruct((N_CHIP,) + _half_shape(nm), BF16) for nm in self.names]
        quarter = [jax.ShapeDtypeStruct(self._part_shape(nm), BF16) for nm in self.names]
        self.work_shape = quarter + quarter

    @staticmethod
    def _part_shape(name):
        r, c = _half_shape(name)
        return (r // 2, c) if _BIG_SPLIT[name] == 0 else (r, c // 2)

    def _part(self, ref, name, p):
        r, c = self._part_shape(name)
        return ref.at[pl.ds(p * r, r), pl.ds(0, c)] if _BIG_SPLIT[name] == 0 else ref.at[pl.ds(0, r), pl.ds(p * c, c)]

    def _copies(self, ins, bufs, ss, rs, k):
        x, y, c, _ = _mesh_place()
        name, n = self.names[k], self.n
        me, xn, yn, dg = 2 * x + y, 2 * (1 - x) + y, 2 * x + (1 - y), 2 * (1 - x) + (1 - y)
        to_x, to_y = (1 - x, y, c), (x, 1 - y, c)
        mine = lambda slot, p: self._part(ins[k].at[slot], name, p)
        slot = lambda s, p: self._part(bufs[k].at[s], name, p)
        from_x, from_y = bufs[n + k], bufs[2 * n + k]

        def copy(pair, src, dst, to):
            return _remote(src, dst, ss, rs, self.base + 6 * k + pair, to)

        sent = [copy(0, mine(dg, 0), from_x, to_x), copy(1, mine(dg, 1), from_y, to_y),
                copy(2, mine(xn, 0), slot(me, 0), to_x), copy(3, mine(yn, 1), slot(me, 1), to_y),
                copy(4, from_y, slot(me, 1), to_x), copy(5, from_x, slot(me, 0), to_y)]
        landing = [from_x, from_y, slot(xn, 0), slot(yn, 1), slot(xn, 1), slot(yn, 0)]
        received = [copy(pair, dst, dst, to_x) for pair, dst in enumerate(landing)]
        return sent, received

    def first(self, ins, bufs, ss, rs):
        x, y, c, _ = _mesh_place()
        me, dg = 2 * x + y, 2 * (1 - x) + (1 - y)
        for k in range(self.n):
            sent, _ = self._copies(ins, bufs, ss, rs, k)
            for pair in range(4):
                sent[pair].start()
        for k in range(self.n):
            bufs[k][me] = ins[k][me]
            bufs[k][dg] = jnp.zeros(_half_shape(self.names[k]), BF16)

    def middle(self, ins, bufs, ss, rs):
        x, y, c, _ = _mesh_place()
        xn, yn = 2 * (1 - x) + y, 2 * x + (1 - y)
        for k in range(self.n):
            sent, received = self._copies(ins, bufs, ss, rs, k)
            name, n = self.names[k], self.n
            for pair, buf, own in ((0, bufs[n + k], self._part(ins[k].at[yn], name, 0)),
                                   (1, bufs[2 * n + k], self._part(ins[k].at[xn], name, 1))):
                received[pair].wait_recv()
                buf[...] = (buf[...].astype(F32) + own[...].astype(F32)).astype(BF16)
            sent[5].start()
            sent[4].start()

    def last(self, ins, bufs, ss, rs):
        for k in range(self.n):
            sent, received = self._copies(ins, bufs, ss, rs, k)
            for pair in range(2, 6):
                received[pair].wait_recv()
            for cp in sent:
                cp.wait_send()


class _PresumThenRelay:
    in_space = ANY
    middle_at = _ReduceRelay.middle_at

    def __init__(self, names):
        self.relay = _ReduceRelay(names)
        self.pre = _Presum(names, base=self.relay.n_sems)
        self.n_sems = self.relay.n_sems + self.pre.n_sems
        self.out_shape = self.relay.out_shape
        self.work_shape = list(self.relay.work_shape) + list(self.pre.out_shape) + list(self.pre.work_shape)
        self.n_relay = len(self.relay.out_shape) + len(self.relay.work_shape)

    def first(self, ins, bufs, ss, rs):
        self.pre.first(ins, bufs[self.n_relay:], ss, rs)

    def early(self, ins, bufs, ss, rs):
        self.pre.last(ins, bufs[self.n_relay:], ss, rs)
        self.relay.first(bufs[self.n_relay:], bufs[:self.n_relay], ss, rs)

    def middle(self, ins, bufs, ss, rs):
        self.relay.middle(bufs[self.n_relay:], bufs[:self.n_relay], ss, rs)

    def last(self, ins, bufs, ss, rs):
        self.relay.last(bufs[self.n_relay:], bufs[:self.n_relay], ss, rs)


class _SendPartials:
    def __init__(self, names, small_shape=None):
        self.n = len(names)
        self.small = small_shape is not None
        self.n_sems = 3 * self.n + 7
        self.out_shape = [jax.ShapeDtypeStruct((N_CHIP,) + _half_shape(nm), BF16) for nm in names]
        if self.small:
            self.out_shape.append(jax.ShapeDtypeStruct((N_DEV,) + small_shape, F32))

    def _piece(self, ins, outs, ss, rs, k, j, peer, src_slot, dst_slot, c):
        return _remote(ins[k].at[src_slot], outs[k].at[dst_slot], ss, rs, 3 * k + j, (*peer, c))

    def _small(self, ins, outs, ss, rs, r, other, slot):
        return _remote(ins[self.n], outs[self.n].at[slot], ss, rs, 3 * self.n + r, other)

    @staticmethod
    def _others(x, y, c):
        return [(x, y, 1 - c), (1 - x, y, c), (1 - x, y, 1 - c), (x, 1 - y, c), (x, 1 - y, 1 - c),
                (1 - x, 1 - y, c), (1 - x, 1 - y, 1 - c)]

    def first(self, ins, outs, ss, rs, only=None):
        x, y, c, peers = _mesh_place()
        me = 2 * x + y
        which = range(self.n) if only is None else only
        for k in which:
            for j, (px, py) in enumerate(peers):
                self._piece(ins, outs, ss, rs, k, j, (px, py), 2 * px + py, me, c).start()
        if self.small:
            for r, other in enumerate(self._others(x, y, c)):
                self._small(ins, outs, ss, rs, r, other, 4 * x + 2 * y + c).start()
            outs[self.n][4 * x + 2 * y + c] = ins[self.n][...]
        for k in which:
            outs[k][me] = ins[k][me]

    def middle(self, ins, outs, ss, rs):
        pass

    def last(self, ins, outs, ss, rs):
        x, y, c, peers = _mesh_place()
        me = 2 * x + y
        for k in range(self.n):
            for j, (px, py) in enumerate(peers):
                self._piece(ins, outs, ss, rs, k, j, (px, py), me, 2 * px + py, c).wait_recv()
                self._piece(ins, outs, ss, rs, k, j, (px, py), 2 * px + py, me, c).wait_send()
        if self.small:
            for r, (px, py, pc) in enumerate(self._others(x, y, c)):
                self._small(ins, outs, ss, rs, r, (px, py, pc), 4 * px + 2 * py + pc).wait_recv()
                self._small(ins, outs, ss, rs, r, (px, py, pc), 4 * x + 2 * y + c).wait_send()


class _PresumThenSend:
    def __init__(self, names):
        self.send = _SendPartials(names)
        self.pre = _Presum(names[-1:], base=self.send.n_sems)
        self.n = self.send.n
        self.n_sems = self.send.n_sems + self.pre.n_sems
        self.out_shape = self.send.out_shape
        self.work_shape = list(self.pre.out_shape) + list(self.pre.work_shape)
        self.in_space = [VMEM_WHOLE] * (self.n - 1) + [ANY]

    def _partials(self, ins, bufs):
        return list(ins[:self.n - 1]) + [bufs[self.n]]

    def first(self, ins, bufs, ss, rs):
        self.pre.first(ins[self.n - 1:], bufs[self.n:], ss, rs)
        self.send.first(ins, bufs[:self.n], ss, rs, only=range(self.n - 1))

    def early(self, ins, bufs, ss, rs):
        self.pre.last(ins[self.n - 1:], bufs[self.n:], ss, rs)
        self.send.first(self._partials(ins, bufs), bufs[:self.n], ss, rs, only=(self.n - 1,))

    def middle(self, ins, bufs, ss, rs):
        pass

    def last(self, ins, bufs, ss, rs):
        self.send.last(self._partials(ins, bufs), bufs[:self.n], ss, rs)


def _sum_swap(names, parts, small):
    n = len(parts)
    everyone = _SendPartials((), small.shape)

    def body(*refs):
        (p_hbm, (small_ref,), o_hbm, (osmall_ref,), p_refs, o_refs, (all_ref,),
         (send_sems, recv_sems, ss_small, rs_small, load_sems, leave_sems)) = _split(refs, n, 1, n, 1, n, n, 1, 6)
        x, y, c = lax.axis_index("x"), lax.axis_index("y"), lax.axis_index("c")
        loads = [pltpu.make_async_copy(p_hbm[k], p_refs[k], load_sems.at[k]) for k in range(n)]
        for cp in loads:
            cp.start()
        everyone.first([small_ref], [all_ref], ss_small, rs_small)

        def mine(k):
            part = _half(o_refs[k], c, names[k])
            return _remote(part, part, send_sems, recv_sems, k, (x, y, 1 - c))

        def leave(k, whose):
            e = c if whose == 0 else 1 - c
            return pltpu.make_async_copy(_half(o_refs[k], e, names[k]), _half(o_hbm[k], e, names[k]),
                                         leave_sems.at[2 * k + whose])

        for k in range(n):
            loads[k].wait()
            for e in range(2):
                @pl.when(c == e)
                def _():
                    g = p_refs[k][0].astype(F32)
                    for s in range(1, N_CHIP):
                        g = g + p_refs[k][s].astype(F32)
                    r, cols = _half_shape(names[k])
                    if _BIG_SPLIT[names[k]] == 0:
                        o_refs[k][e * r:(e + 1) * r, :] = g
                    else:
                        o_refs[k][:, e * cols:(e + 1) * cols] = g
            mine(k).start()
            leave(k, 0).start()
        for k in range(n):
            theirs = _half(o_refs[k], 1 - c, names[k])
            _remote(theirs, theirs, send_sems, recv_sems, k, (x, y, 1 - c)).wait_recv()
            leave(k, 1).start()
        everyone.last([small_ref], [all_ref], ss_small, rs_small)
        g = all_ref[0]
        for d in range(1, N_DEV):
            g = g + all_ref[d]
        osmall_ref[...] = g
        for k in range(n):
            mine(k).wait_send()
            leave(k, 0).wait()
            leave(k, 1).wait()

    shards = [jax.ShapeDtypeStruct(_BIG_SHARD[nm], F32) for nm in names]
    res = pl.pallas_call(
        body, in_specs=[ANY] * n + [VMEM_WHOLE], out_specs=[ANY] * n + [VMEM_WHOLE],
        out_shape=shards + [jax.ShapeDtypeStruct(small.shape, F32)],
        scratch_shapes=[pltpu.VMEM(q.shape, q.dtype) for q in parts] + [pltpu.VMEM(s.shape, s.dtype) for s in shards]
        + [pltpu.VMEM((N_DEV,) + small.shape, F32), pltpu.SemaphoreType.DMA((n,)), pltpu.SemaphoreType.DMA((n,)),
           pltpu.SemaphoreType.DMA((everyone.n_sems,)), pltpu.SemaphoreType.DMA((everyone.n_sems,)),
           pltpu.SemaphoreType.DMA((n,)), pltpu.SemaphoreType.DMA((2 * n,))],
        compiler_params=_params(), name="sum_swap",
    )(*parts, small)
    return res[:n], res[n]


def _adamw_math(w, g, m, v):
    m = ADAM_B1 * m + (1.0 - ADAM_B1) * g
    v = ADAM_B2 * v + (1.0 - ADAM_B2) * (g * g)
    m_hat = m / (1.0 - ADAM_B1 ** ADAM_STEP)
    v_hat = v / (1.0 - ADAM_B2 ** ADAM_STEP)
    delta = -ADAM_LR * (m_hat / (jnp.sqrt(v_hat) + ADAM_EPS) + ADAM_WD * w)
    return delta, m, v


ADAMW_STEPS = 8


def _adamw_big(gs, ws, ms, vs, name):
    n = len(ws)

    def body(*refs):
        for k in range(n):
            g_ref, w_ref, m_ref, v_ref = refs[4 * k:4 * k + 4]
            g_out_ref, d_ref, nm_ref, nv_ref = refs[4 * (n + k):4 * (n + k) + 4]
            g = g_ref[...].reshape(w_ref.shape)
            g_out_ref[...] = g
            d_ref[...], nm_ref[...], nv_ref[...] = _adamw_math(w_ref[...], g, m_ref[...], v_ref[...])

    in_specs, out_specs, shapes, args = [], [], [], []
    for g, w, m, v in zip(gs, ws, ms, vs):
        tr = -(-w.shape[0] // (8 * ADAMW_STEPS)) * 8
        zeros = (0,) * (w.ndim - 1)
        blk = pl.BlockSpec((tr,) + w.shape[1:], lambda i, zeros=zeros: (i,) + zeros)
        in_specs += [pl.BlockSpec((tr,) + g.shape[1:], lambda i: (i, 0)), blk, blk, blk]
        out_specs += [blk] * 4
        shapes += [jax.ShapeDtypeStruct(w.shape, F32)] * 4
        args += [g, w, m, v]
    res = pl.pallas_call(
        body, grid=(ADAMW_STEPS,), in_specs=in_specs, out_specs=out_specs, out_shape=shapes,
        compiler_params=_params(("arbitrary",)), name=name,
    )(*args)
    return [tuple(res[4 * k:4 * k + 4]) for k in range(n)]


def _adamw_small(ws, gs, ms, vs):
    n = len(ws)

    def body(*refs):
        w_refs, g_refs, m_refs, v_refs, d_refs, nm_refs, nv_refs = _split(refs, *([n] * 7))
        for k in range(n):
            d_refs[k][...], nm_refs[k][...], nv_refs[k][...] = _adamw_math(w_refs[k][...], g_refs[k][...], m_refs[k][...],
                                                                             v_refs[k][...])

    shapes = [jax.ShapeDtypeStruct(w.shape, F32) for w in ws]
    res = pl.pallas_call(body, out_shape=shapes * 3, name="adamw_small")(*ws, *gs, *ms, *vs)
    return res[:n], res[n:2 * n], res[2 * n:]


def _pack(arrs):
    flat = jnp.concatenate([a.reshape(-1) for a in arrs])
    rows = -(-flat.shape[0] // 1024) * 8
    return jnp.pad(flat, (0, rows * 128 - flat.shape[0])).reshape(rows, 128)


def _unpack(buf, shapes):
    flat = buf.reshape(-1)
    out, off = [], 0
    for s in shapes:
        size = 1
        for d in s:
            size *= d
        out.append(flat[off:off + size].reshape(s))
        off += size
    return out


def _block_rows(w):
    return jnp.pad(w.reshape(512, 4), ((0, 0), (0, 124)))


def _block_stored(dw):
    return jnp.transpose(dw[:, 0:4].reshape(128, 4, 4), (1, 2, 0)).reshape(16, 128)


def _cols(a4):
    return jnp.transpose(a4, (1, 0, 2)).reshape(a4.shape[1], -1)


_LATE = ("w_pa", "w_pb", "w_o", "w_up", "w_down")
_RIDE_IN_PROJ = ("w_pa", "w_pb", "w_o", "w_down_b")
_RIDE_MIXER = ("w_up", "w_down_a")


def _full_weights(gathered):
    joined = {"w_o": (D_MODEL, D_MODEL)}
    return {n: (a.reshape(joined[n]) if n in joined else a) for n, a in gathered.items()}


def _local_step(x, target, w, sp, late_shards=None):
    sp = {n: (a.reshape(1, -1) if a.ndim == 1 else a) for n, a in sp.items()}
    wau = jnp.pad(sp["w_a_up"], ((0, 112), (0, 0)))
    wif = jnp.pad(sp["w_if"], ((0, 0), (0, 120)))
    bif = jnp.pad(sp["b_if"], ((0, 0), (0, 120)))
    p = {"wau": wau, "bau": sp["b_a_up"], "ggla": sp["g_gla_norm"], "cw": sp["conv_w"], "cb": sp["conv_b"],
         "wq": _block_rows(sp["w_q_ml"]), "wk": _block_rows(sp["w_k_ml"]), "wv": _block_rows(sp["w_v_ml"]),
         "wif": wif, "bif": bif, "skip": sp["ml_skip"], "gml": sp["g_ml_norm"]}

    if late_shards is None:
        (pm, gab, h), _ = _in_proj(x, sp["g_pre_mix"], w["w_in"])
        ab, x1, mix, merged, *states = _mixer_fwd(pm, p, gab, x, w["w_pa"], w["w_pb"], w["w_o"], sp["g_post_mix"])
    else:
        shard = dict(zip(_LATE, late_shards))
        shard["w_down_a"] = shard["w_down_b"] = shard["w_down"]
        (pm, gab, h), got = _in_proj(x, sp["g_pre_mix"], w["w_in"], _Gather(_RIDE_IN_PROJ, middle_at=0.7),
                                     [shard[n] for n in _RIDE_IN_PROJ])
        w = dict(w, **_full_weights(dict(zip(_RIDE_IN_PROJ, got))))
        ab, x1, mix, merged, *rest = _mixer_fwd(pm, p, gab, x, w["w_pa"], w["w_pb"], w["w_o"], sp["g_post_mix"],
                                                _Gather(_RIDE_MIXER, middle_at=0.62), [shard[n] for n in _RIDE_MIXER])
        states = rest[:4]
        w.update(_full_weights(dict(zip(_RIDE_MIXER, rest[4:]))))
    dx1, u, dd, h2, dpre, dg_post_mlp, dg_pre_mlp, loss = _mlp(x1, target, sp["g_pre_mlp"], sp["g_post_mlp"],
                                                                w["w_up"], w["w_down_a"], w["w_down_b"])
    dgab, dab, dg_post_mix, dw_pa, dw_pb, dw_o = _merge_bwd(dx1, mix, ab, gab, merged, w["w_pa"], w["w_pb"], w["w_o"],
                                                            sp["g_post_mix"])
    big = {"w_pa": dw_pa, "w_pb": dw_pb, "w_o": dw_o, "w_up": _tn_matmul(h2, dpre, "dw_up", shards=N_CHIP)}
    if late_shards is None:
        big["w_down"] = _tn_matmul(u, dd, "dw_down")
        dpm, dp, _ = _mixer_bwd(pm, dab, states, p)
    else:
        pieces = lambda n: big[n].reshape((N_CHIP,) + _BIG_SHARD[n])
        big["w_down"], partial = _tn_matmul(u, dd, "dw_down", rider=_Presum(_LATE[:4]),
                                            rider_ins=[pieces(n) for n in _LATE[:4]])
        dpm, dp, parts = _mixer_bwd(pm, dab, states, p, _PresumThenSend(_LATE), list(partial) + [pieces("w_down")])
        big = dict(zip(_LATE, parts))
    big["w_in"] = _dw_in(dpm, dgab, h)
    if late_shards is None:
        (dx, dg_pre_mix), _ = _in_proj_bwd(dpm, dgab, x, dx1, sp["g_pre_mix"], w["w_in"])
    else:
        (dx, dg_pre_mix), parts = _in_proj_bwd(dpm, dgab, x, dx1, sp["g_pre_mix"], w["w_in"], _PresumThenRelay(("w_in",)),
                                               [big["w_in"]])
        big["w_in"] = parts[0]
    small = {
        "g_pre_mix": dg_pre_mix, "b_a_up": dp["bau"], "g_gla_norm": dp["ggla"], "conv_b": dp["cb"],
        "w_q_ml": _block_stored(dp["wq"]), "w_k_ml": _block_stored(dp["wk"]), "w_v_ml": _block_stored(dp["wv"]),
        "w_if": dp["wif"][:, 0:8].T,
        "b_if": dp["bif"][:, 0:8], "ml_skip": dp["skip"], "g_ml_norm": dp["gml"], "g_post_mix": dg_post_mix,
        "g_pre_mlp": dg_pre_mlp, "g_post_mlp": dg_post_mlp, "w_a_up": dp["wau"][0:16], "conv_w": dp["cw"],
        "loss": loss[:, 0:1],
    }
    return dx, big, small


_SMALL_REPL = ("g_pre_mix", "b_a_up", "g_gla_norm", "conv_b", "w_q_ml", "w_k_ml", "w_v_ml", "b_if", "ml_skip",
               "g_ml_norm", "g_post_mix", "g_pre_mlp", "g_post_mlp")
_SMALL_SHARDED = ("w_a_up", "conv_w", "w_if")
_SMALL_ORDER = _SMALL_REPL + _SMALL_SHARDED + ("loss",)
_WEIGHTS = ("g_pre_mix", "w_in", "w_a_up", "b_a_up", "g_gla_norm", "conv_w", "conv_b", "w_q_ml", "w_k_ml", "w_v_ml",
            "w_if", "b_if", "ml_skip", "g_ml_norm", "w_pa", "w_pb", "w_o", "g_post_mix", "g_pre_mlp", "w_up", "w_down",
            "g_post_mlp")


_BLOCK_WEIGHTS = ("w_q_ml", "w_k_ml", "w_v_ml")


def _stored(name, a):
    if name in _BLOCK_WEIGHTS:
        return jnp.transpose(a, (0, 2, 3, 1)).reshape(16, 128)
    if name == "w_if":
        return jnp.transpose(a, (0, 2, 1)).reshape(8, 384)
    return a


def _unstored(name, a):
    if name in _BLOCK_WEIGHTS:
        return jnp.transpose(a.reshape(1, 4, 4, 128), (0, 3, 1, 2))
    if name == "w_if":
        return jnp.transpose(a.reshape(1, 8, 384), (0, 2, 1))
    return a


def _as_shard(name, a):
    return jnp.transpose(a, (2, 0, 1)).reshape(IN_SHARD, D_MODEL // 128, 128) if name == "w_in" else a[0]


def _in_shard_bf16(w_in):
    return jnp.transpose(w_in.astype(BF16), (2, 0, 1)).reshape(IN_SHARD, D_MODEL)


def _from_shard(name, a):
    return jnp.transpose(a, (1, 2, 0)).reshape(1, D_MODEL, IN_SHARD) if name == "w_in" else a[None]


def kernel(x, g_pre_mix, w_in, w_a_up, b_a_up, g_gla_norm, conv_w, conv_b, w_q_ml, w_k_ml, w_v_ml, w_if, b_if, ml_skip, g_ml_norm, w_pa, w_pb, w_o, g_post_mix, g_pre_mlp, w_up, w_down, g_post_mlp, loss_target, m_g_pre_mix, m_w_in, m_w_a_up, m_b_a_up, m_g_gla_norm, m_conv_w, m_conv_b, m_w_q_ml, m_w_k_ml, m_w_v_ml, m_w_if, m_b_if, m_ml_skip, m_g_ml_norm, m_w_pa, m_w_pb, m_w_o, m_g_post_mix, m_g_pre_mlp, m_w_up, m_w_down, m_g_post_mlp, v_g_pre_mix, v_w_in, v_w_a_up, v_b_a_up, v_g_gla_norm, v_conv_w, v_conv_b, v_w_q_ml, v_w_k_ml, v_w_v_ml, v_w_if, v_b_if, v_ml_skip, v_g_ml_norm, v_w_pa, v_w_pb, v_w_o, v_g_post_mix, v_g_pre_mlp, v_w_up, v_w_down, v_g_post_mlp):
    args = dict(locals())
    wts = {n: _as_shard(n, args[n]) for n in _WEIGHTS}
    mom = {n: _as_shard(n, args["m_" + n]) for n in _WEIGHTS}
    var = {n: _as_shard(n, args["v_" + n]) for n in _WEIGHTS}
    chip = 2 * lax.axis_index("x") + lax.axis_index("y")

    first = ("w_in",) + _SMALL_SHARDED
    gathered = dict(zip(first, _run_alone(_Gather(("w_in",), [wts[n] for n in _SMALL_SHARDED]),
                                          [_in_shard_bf16(w_in)] + [wts[n] for n in _SMALL_SHARDED],
                                          "gather_first")))
    sp = {n: wts[n] for n in _SMALL_REPL}
    sp["w_a_up"] = _cols(gathered["w_a_up"])
    sp["conv_w"] = _cols(gathered["conv_w"])
    sp["w_if"] = gathered["w_if"].reshape(1536, 8)

    dx, big, small = _local_step(x[0], loss_target[0], _full_weights({"w_in": gathered["w_in"]}), sp,
                                 late_shards=[wts[n] for n in _LATE])

    small_shapes = [small[n].shape for n in _SMALL_ORDER]
    packed = _pack([small[n] for n in _SMALL_ORDER])
    sums, small_sum = _sum_swap(_BIG, [big[n] for n in _BIG], packed)

    grads, delta, new_m, new_v = {}, {}, {}, {}
    updated = dict(zip(_BIG, _adamw_big(sums, [wts[n] for n in _BIG], [mom[n] for n in _BIG], [var[n] for n in _BIG],
                                        "adamw_big")))
    for n in _BIG:
        grads[n], delta[n], new_m[n], new_v[n] = (_from_shard(n, a) for a in updated[n])
    summed = dict(zip(_SMALL_ORDER, _unpack(small_sum, small_shapes)))
    loss = summed["loss"].reshape(())
    summed["w_a_up"] = lax.dynamic_slice_in_dim(summed["w_a_up"], chip * 64, 64, axis=1)
    summed["conv_w"] = lax.dynamic_slice_in_dim(summed["conv_w"], chip * 128, 128, axis=1)
    summed["w_if"] = lax.dynamic_slice_in_dim(summed["w_if"], chip * 384, 384, axis=1)
    small_names = _SMALL_REPL + _SMALL_SHARDED
    came_stored = _BLOCK_WEIGHTS + ("w_if",)
    g_stored = [summed[n] if n in came_stored else _stored(n, summed[n].reshape(args[n].shape)) for n in small_names]
    upd = _adamw_small([_stored(n, args[n]) for n in small_names], g_stored,
                       [_stored(n, args["m_" + n]) for n in small_names], [_stored(n, args["v_" + n]) for n in small_names])
    for dst, arrs in zip((grads, delta, new_m, new_v), (g_stored,) + tuple(upd)):
        dst.update({n: _unstored(n, a) for n, a in zip(small_names, arrs)})

    outs = [loss, dx[None]]
    for group in (grads, delta, new_m, new_v):
        outs += [group[n] for n in _WEIGHTS]
    return tuple(outs)
```

```python
import functools

import jax
import jax.numpy as jnp
from jax import lax
from jax.experimental import pallas as pl
from jax.experimental.pallas import tpu as pltpu

F32 = jnp.float32
BF16 = jnp.bfloat16

SEQ = 2048
D_MODEL = 1024
CHUNK = 64
N_CHUNK = SEQ // CHUNK
HEADS = 4
GLA_DK = 64
GLA_DV = 128
ML_DH = 128
D_FF = 4096
EPS = 1e-6
N_CHIP = 4
N_DEV = 8
TOK_TILE = 256
N_TOK_TILE = SEQ // TOK_TILE
SWEEP = 2
assert CHUNK == 64
N_SWEEP = N_CHUNK // SWEEP

PM_W = 2688
PM_XM = 1536
PM_OP = 2048
PM_AL = 2560
GAB_W = 2048
D_IN = 4624
IN_SHARD = D_IN // N_CHIP
IN_ALOW = 1536
IN_XM = 1552
IN_GATES = 2576

ADAM_LR = 0.001
ADAM_B1 = 0.9
ADAM_B2 = 0.999
ADAM_EPS = 1e-08
ADAM_WD = 0.01
ADAM_STEP = 10

VMEM_LIMIT = 56 * 1024 * 1024


def _params(sem=None):
    return pltpu.CompilerParams(dimension_semantics=sem, vmem_limit_bytes=VMEM_LIMIT)


def _dot(a, b, ca, cb):
    return lax.dot_general(a.astype(BF16), b.astype(BF16), (((ca,), (cb,)), ((), ())), preferred_element_type=F32)


def _pmm_nn(a, b):
    return _dot(a, b, 1, 0)


def _pmm_nt(a, b):
    return _dot(a, b, 1, 1)


def _pmm_tn(a, b):
    return _dot(a, b, 0, 0)


def _pcmm(c, x):
    return lax.dot_general(c, x, (((1,), (0,)), ((), ())), precision=lax.Precision.HIGHEST, preferred_element_type=F32)


@jax.custom_vjp
def _mm_nn(a, b):
    return _dot(a, b, 1, 0)


@jax.custom_vjp
def _mm_nt(a, b):
    return _dot(a, b, 1, 1)


@jax.custom_vjp
def _mm_tn(a, b):
    return _dot(a, b, 0, 0)


_mm_nn.defvjp(lambda a, b: (_dot(a, b, 1, 0), (a, b)), lambda r, g: (_mm_nt(g, r[1]), _mm_tn(r[0], g)))
_mm_nt.defvjp(lambda a, b: (_dot(a, b, 1, 1), (a, b)), lambda r, g: (_mm_nn(g, r[1]), _mm_tn(g, r[0])))
_mm_tn.defvjp(lambda a, b: (_dot(a, b, 0, 0), (a, b)), lambda r, g: (_mm_nt(r[1], g), _mm_nn(r[0], g)))


@jax.custom_vjp
def _cmm(c, x):
    return _pcmm(c, x)


_cmm.defvjp(
    lambda c, x: (_pcmm(c, x), c),
    lambda c, g: (jnp.zeros_like(c), lax.dot_general(c, g, (((0,), (0,)), ((), ())), precision=lax.Precision.HIGHEST,
                                                      preferred_element_type=F32)),
)

_PLAIN_OPS = (_pmm_nn, _pmm_nt, _pmm_tn, _pcmm)
_VJP_OPS = (_mm_nn, _mm_nt, _mm_tn, _cmm)


def _sigmoid(x):
    return 0.5 * (jnp.tanh(0.5 * x) + 1.0)


def _log_sigmoid(x):
    return jnp.minimum(x, 0.0) - jnp.log(1.0 + jnp.exp(-jnp.abs(x)))


def _mean(x):
    return jnp.mean(x, axis=-1, keepdims=True)


def _nt(a, b):
    return lax.dot_general(a, b, (((1,), (1,)), ((), ())), preferred_element_type=F32)


def _tn(a, b):
    return lax.dot_general(a, b, (((0,), (0,)), ((), ())), preferred_element_type=F32)


def _mixer_chunk(ops, p, st, pm, xprev8):
    mm_nn, mm_nt, mm_tn, cmm = ops
    n_rows = pm.shape[0]
    n_ch = n_rows // CHUNK
    row = lax.broadcasted_iota(jnp.int32, (n_rows, n_rows), 0)
    col = lax.broadcasted_iota(jnp.int32, (n_rows, n_rows), 1)
    tri = jnp.logical_and((row >> 6) == (col >> 6), row >= col).astype(F32)
    causal = tri[0:CHUNK, 0:CHUNK] > 0.0
    q = pm[:, 0:256]
    k = pm[:, 256:512]
    v = pm[:, 512:1024]
    g = pm[:, 1024:1536]
    xm = pm[:, PM_XM:PM_XM + 512]
    opre = pm[:, PM_OP:PM_OP + 512]
    alow = pm[:, PM_AL:PM_AL + 128]
    hs = range(HEADS)
    cs = range(n_ch)
    pairs = [(i, h) for i in cs for h in hs]
    rs = [slice(i * CHUNK, (i + 1) * CHUNK) for i in cs]
    last = [slice((i + 1) * CHUNK - 1, (i + 1) * CHUNK) for i in cs]
    s6 = [slice(h * GLA_DK, (h + 1) * GLA_DK) for h in hs]
    s12 = [slice(h * 128, (h + 1) * 128) for h in hs]

    xx = jnp.concatenate([xprev8, xm], axis=0)
    pre = p["cb"]
    for j in range(4):
        pre = pre + p["cw"][j:j + 1, :] * xx[5 + j:5 + j + n_rows, :]
    xc = pre * _sigmoid(pre)
    qm = [mm_nn(xc[:, s12[h]], p["wq"][h]) for h in hs]
    km = [mm_nn(xc[:, s12[h]], p["wk"][h]) for h in hs]
    vm = [mm_nn(xm[:, s12[h]], p["wv"][h]) for h in hs]
    qcat = jnp.concatenate(qm, axis=1)
    kcat = jnp.concatenate(km, axis=1)
    vcat = jnp.concatenate(vm, axis=1)
    gates = (mm_nn(qcat, p["wif"][0:512]) + mm_nn(kcat, p["wif"][512:1024]) + mm_nn(vcat, p["wif"][1024:1536])
             + p["bif"])
    lf = _log_sigmoid(gates)
    fc = cmm(tri, lf)
    gates_t = gates.T
    fc_t = fc.T

    la = _log_sigmoid(mm_nn(alow, p["wau"]) + p["bau"]) * (1.0 / 16.0)
    cum = cmm(tri, la)
    cum_last = [cum[last[i], :] for i in cs]
    to_end = jnp.concatenate([cum_last[i] - cum[rs[i], :] for i in cs], axis=0)
    e_pos = jnp.exp(cum)
    e_neg = jnp.exp(-cum)
    qs = q * (GLA_DK ** -0.5)
    qp = qs * e_pos
    qn = qs * e_neg
    kp = k * e_pos
    kn = k * e_neg
    kl = k * jnp.exp(to_end)
    dec = [jnp.exp(cum_last[i]) for i in cs]
    ks = [km[h] * (ML_DH ** -0.5) for h in hs]
    li_c = {(i, h): gates[rs[i], h:h + 1] for i, h in pairs}
    fc_c = {(i, h): fc[rs[i], 4 + h:5 + h] for i, h in pairs}
    f_last = {(i, h): fc[last[i], 4 + h:5 + h] for i, h in pairs}

    a_fwd = {(i, h): mm_nt(qp[rs[i], s6[h]], kn[rs[i], s6[h]]) for i, h in pairs}
    a_bwd = {(i, h): mm_nt(qn[rs[i], s6[h]], kp[rs[i], s6[h]]) for i, h in pairs}
    s_chunk = {(i, h): mm_tn(v[rs[i], s12[h]], kl[rs[i], s6[h]]) for i, h in pairs}
    qk = {(i, h): mm_nt(qm[h][rs[i]], ks[h][rs[i]]) for i, h in pairs}
    a = {ih: f_last[ih] - fc_c[ih] + li_c[ih] for ih in pairs}
    m_loc = {ih: jnp.max(a[ih], axis=0, keepdims=True) for ih in pairs}
    kw = {(i, h): ks[h][rs[i]] * jnp.exp(a[(i, h)] - m_loc[(i, h)]) for i, h in pairs}
    c_chunk = {(i, h): mm_tn(kw[(i, h)], vm[h][rs[i]]) for i, h in pairs}
    mem = {(0, h): st["S"][h] for h in hs}
    c_in = {(0, h): st["C"][h] for h in hs}
    n_in = {(0, h): st["n"][h] for h in hs}
    m_in = {(0, h): st["m"][h][:, 0:1] for h in hs}
    for i, h in pairs:
        mem[(i + 1, h)] = mem[(i, h)] * dec[i][:, s6[h]] + s_chunk[(i, h)]
        m_nx = jnp.maximum(f_last[(i, h)] + m_in[(i, h)], m_loc[(i, h)])
        sp = jnp.exp(f_last[(i, h)] + m_in[(i, h)] - m_nx)
        sl = jnp.exp(m_loc[(i, h)] - m_nx)
        c_in[(i + 1, h)] = sp * c_in[(i, h)] + sl * c_chunk[(i, h)]
        n_in[(i + 1, h)] = sp * n_in[(i, h)] + sl * jnp.sum(kw[(i, h)], axis=0, keepdims=True)
        m_in[(i + 1, h)] = m_nx
    s_new = [mem[(n_ch, h)] for h in hs]
    o_inter = {(i, h): mm_nt(qp[rs[i], s6[h]], mem[(i, h)]) for i, h in pairs}
    q_c = {(i, h): mm_nn(qm[h][rs[i]], c_in[(i, h)]) for i, h in pairs}
    scores = {ih: jnp.where(causal, a_fwd[ih], a_bwd[ih]) for ih in pairs}
    log_d = {(i, h): gates_t[h:h + 1, rs[i]] - jnp.abs(fc_c[(i, h)] - fc_t[4 + h:5 + h, rs[i]]) for i, h in pairs}
    g_int = {ih: fc_c[ih] + m_in[ih] for ih in pairs}
    m_t = {ih: jnp.maximum(g_int[ih], jnp.max(log_d[ih], axis=1, keepdims=True)) for ih in pairs}
    s = {ih: qk[ih] * jnp.exp(log_d[ih] - m_t[ih]) for ih in pairs}
    scl = {ih: jnp.exp(g_int[ih] - m_t[ih]) for ih in pairs}
    o = {(i, h): mm_nn(scores[(i, h)], v[rs[i], s12[h]]) + o_inter[(i, h)] for i, h in pairs}
    num = {(i, h): mm_nn(s[(i, h)], vm[h][rs[i]]) + scl[(i, h)] * q_c[(i, h)] for i, h in pairs}
    o = {ih: o[ih] * lax.rsqrt(_mean(o[ih] * o[ih]) + EPS) * p["ggla"] for ih in pairs}
    gate = g * _sigmoid(g)
    out_a = {(i, h): o[(i, h)] * gate[rs[i], s12[h]] for i, h in pairs}
    den = {(i, h): jnp.sum(s[(i, h)], axis=1, keepdims=True)
           + scl[(i, h)] * jnp.sum(qm[h][rs[i]] * n_in[(i, h)], axis=1, keepdims=True) for i, h in pairs}
    den = {ih: jnp.maximum(jnp.abs(den[ih]), jnp.exp(-m_t[ih])) for ih in pairs}
    open_gate = _sigmoid(opre)
    hc = {(i, h): num[(i, h)] / den[(i, h)] * open_gate[rs[i], s12[h]] for i, h in pairs}
    d0 = {ih: hc[ih] - _mean(hc[ih]) for ih in pairs}
    y = {ih: d0[ih] * lax.rsqrt(_mean(d0[ih] * d0[ih]) + EPS) for ih in pairs}
    skipped = p["skip"] * xc
    out_b = {(i, h): y[(i, h)] * p["gml"][:, s12[h]] + skipped[rs[i], s12[h]] for i, h in pairs}
    ab = jnp.concatenate([jnp.concatenate([out_a[(i, h)] for h in hs] + [out_b[(i, h)] for h in hs], axis=1) for i in cs],
                         axis=0)
    new = {"S": s_new, "C": [c_in[(n_ch, h)] for h in hs], "n": [n_in[(n_ch, h)] for h in hs],
           "m": [jnp.broadcast_to(m_in[(n_ch, h)], (1, ML_DH)) for h in hs]}
    return ab, new


_P_NAMES = ("wau", "bau", "ggla", "cw", "cb", "wq", "wk", "wv", "wif", "bif", "skip", "gml")
_P_SHAPES = {
    "wau": (128, 256), "bau": (1, 256), "ggla": (1, 128), "cw": (4, 512), "cb": (1, 512),
    "wq": (512, 128), "wk": (512, 128), "wv": (512, 128),
    "wif": (1536, 128), "bif": (1, 128), "skip": (1, 512), "gml": (1, 512),
}
_P_BLOCKDIAG = ("wq", "wk", "wv")
_S_NAMES = ("S", "C", "n", "m")
_S_SHAPES = {"S": (HEADS, GLA_DV, GLA_DK), "C": (HEADS, ML_DH, ML_DH), "n": (HEADS, 1, ML_DH), "m": (HEADS, 1, ML_DH)}


def _per_head(ref):
    return [ref[h] for h in range(HEADS)]


def _block_mask():
    r = lax.broadcasted_iota(jnp.int32, (128, 128), 0)
    c = lax.broadcasted_iota(jnp.int32, (128, 128), 1)
    same_block = (r >> 2) == (c >> 2)
    spread = jnp.logical_and(r < 4, (c & 3) == r)
    return same_block.astype(F32), spread.astype(F32)


def _expand_blockdiag(w_ref, dense_ref):
    same_block, spread = _block_mask()
    for h in range(HEADS):
        tiled = _pmm_nn(w_ref[h * 128:(h + 1) * 128, :], spread)
        dense_ref[h] = tiled * same_block


def _collect_blockdiag(ddense_ref, dw_ref):
    same_block, spread = _block_mask()
    for h in range(HEADS):
        dw_ref[h * 128:(h + 1) * 128, :] = lax.dot_general(
            ddense_ref[h] * same_block, spread, (((1,), (1,)), ((), ())), precision=lax.Precision.HIGHEST,
            preferred_element_type=F32)


def _const_spec(shape):
    zeros = (0,) * len(shape)
    return pl.BlockSpec(shape, lambda i: zeros)


def _split(refs, *counts):
    out, at = [], 0
    for c in counts:
        out.append(refs[at:at + c])
        at += c
    assert at == len(refs)
    return out


def _landed(lands, outs, flush_sems):
    return [pltpu.make_async_copy(lands[k], outs[k], flush_sems.at[k]) for k in range(len(outs))]


def _ride_done(rider, cond, outs, sems):
    if rider is None or hasattr(rider, "flush"):
        return

    @pl.when(cond)
    def _():
        for cp in _landed(sems[:-3], outs, sems[-1]):
            cp.wait()


def _ride(rider, phases, cond, ins, outs, sems, flush_later=False):
    if rider is None or not any(hasattr(rider, phase) for phase in phases):
        return
    lands, (send_sems, recv_sems, flush_sems) = sems[:-3], sems[-3:]

    @pl.when(cond)
    def _():
        for phase in phases:
            if phase == "last" and hasattr(rider, "late"):
                rider.late(ins, lands, send_sems, recv_sems)
                rider.flush("late", lands, outs, flush_sems)
            getattr(rider, phase)(ins, lands, send_sems, recv_sems)
            if hasattr(rider, "flush"):
                rider.flush(phase, lands, outs, flush_sems)
        if "last" in phases and not hasattr(rider, "flush"):
            flush = _landed(lands, outs, flush_sems)
            for cp in flush:
                cp.start()
            if not flush_later:
                for cp in flush:
                    cp.wait()


def _middle_step(rider, n_steps):
    return min(n_steps - 2, int(getattr(rider, "middle_at", 1.0) * n_steps))


def _rider_specs(rider, rider_ins):
    if rider is None:
        return [], [], [], []
    scratch = [pltpu.VMEM(s.shape, s.dtype) for s in list(rider.out_shape) + list(getattr(rider, "work_shape", ()))]
    scratch += [pltpu.SemaphoreType.DMA((rider.n_sems,)), pltpu.SemaphoreType.DMA((rider.n_sems,)),
                pltpu.SemaphoreType.DMA((getattr(rider, "n_flush", len(rider.out_shape)),))]
    in_space = getattr(rider, "in_space", VMEM_WHOLE)
    in_specs = list(in_space) if isinstance(in_space, (list, tuple)) else [in_space] * len(rider_ins)
    return in_specs, [ANY] * len(rider.out_shape), list(rider.out_shape), scratch


def _merge_tile(ab, gab_ref, x_ref, wpa_ref, wpb_ref, wo_ref, g_ref, x1_ref, mix_ref, mg_ref):
    a = ab[:, 0:512]
    b = ab[:, 512:1024]
    for j in range(N_CHIP):
        blk = slice(j * 256, (j + 1) * 256)
        ya = jnp.dot(a, wpa_ref[j], preferred_element_type=F32)
        yb = jnp.dot(b, wpb_ref[j], preferred_element_type=F32)
        sa = _sigmoid(gab_ref[:, j * 256:(j + 1) * 256])
        sb = _sigmoid(gab_ref[:, 1024 + j * 256:1024 + (j + 1) * 256])
        mg_ref[:, blk] = (sa * ya + sb * yb).astype(BF16)
    mix = jnp.dot(mg_ref[...], wo_ref[...], preferred_element_type=F32)
    mix_ref[...] = mix
    mn, _ = _rms_fwd(mix)
    x1_ref[...] = x_ref[...] + mn * g_ref[...]


def _mixer_fwd(pm, p, gab, x, w_pa4, w_pb4, w_o, g_post, rider=None, rider_ins=()):
    n_p = len(_P_NAMES)
    r_in, r_out_specs, r_out_shape, r_sems = _rider_specs(rider, rider_ins)

    def body(*refs):
        ((pm_ref, xprev_ref), p_list, merge_in, ride_in, (ab_ref,), merge_out, so_refs, ride_out, sc_refs, dense_list,
         sems) = _split(refs, 2, n_p, 6, len(r_in), 1, 3, 4, len(r_out_specs), 4, 3, len(r_sems))
        p_refs = dict(zip(_P_NAMES, p_list))
        dense = dict(zip(_P_BLOCKDIAG, dense_list))
        n = pl.program_id(0)
        _ride(rider, ("first",), n == 0, ride_in, ride_out, sems)

        @pl.when(n == 0)
        def _():
            for r in sc_refs:
                r[...] = jnp.zeros_like(r)
            for nm in _P_BLOCKDIAG:
                _expand_blockdiag(p_refs[nm], dense[nm])

        st = {name: _per_head(r) for name, r in zip(_S_NAMES, sc_refs)}
        pv = {nm: (_per_head(dense[nm]) if nm in _P_BLOCKDIAG else p_refs[nm][...]) for nm in _P_NAMES}
        for name, r in zip(_S_NAMES, so_refs):
            for h in range(HEADS):
                r[0, h] = st[name][h]
        xprev8 = jnp.where(n > 0, xprev_ref[CHUNK - 8:CHUNK, :], 0.0)
        ab, st = _mixer_chunk(_PLAIN_OPS, pv, st, pm_ref[...], xprev8)
        ab = ab.astype(BF16)
        ab_ref[...] = ab
        for name, r in zip(_S_NAMES, sc_refs):
            for h in range(HEADS):
                r[h] = st[name][h]
        _merge_tile(ab, *merge_in, *merge_out)
        _ride(rider, ("middle",), n == _middle_step(rider, N_SWEEP), ride_in, ride_out, sems)
        _ride(rider, ("last",), n == N_SWEEP - 1, ride_in, ride_out, sems)

    rows = lambda width: pl.BlockSpec((SWEEP * CHUNK, width), lambda i: (i, 0))
    in_specs = [rows(PM_W), pl.BlockSpec((CHUNK, 512), lambda i: (jnp.maximum(SWEEP * i - 1, 0), PM_XM // 512))]
    in_specs += [_const_spec(_P_SHAPES[nm]) for nm in _P_NAMES]
    in_specs += [rows(GAB_W), rows(D_MODEL), _once((N_CHIP, 512, 256)), _once((N_CHIP, 512, 256)), _once((D_MODEL, D_MODEL)),
                 _once((1, D_MODEL))] + r_in
    out_specs = [rows(1024), rows(D_MODEL), rows(D_MODEL), rows(D_MODEL)]
    out_shape = [jax.ShapeDtypeStruct((SEQ, 1024), BF16), jax.ShapeDtypeStruct((SEQ, D_MODEL), F32),
                 jax.ShapeDtypeStruct((SEQ, D_MODEL), F32), jax.ShapeDtypeStruct((SEQ, D_MODEL), BF16)]
    for nm in _S_NAMES:
        shp = _S_SHAPES[nm]
        out_specs.append(pl.BlockSpec((1,) + shp, lambda i: (i, 0, 0, 0)))
        out_shape.append(jax.ShapeDtypeStruct((N_SWEEP,) + shp, F32))
    return pl.pallas_call(
        body, grid=(N_SWEEP,), in_specs=in_specs, out_specs=out_specs + r_out_specs, out_shape=out_shape + r_out_shape,
        scratch_shapes=[pltpu.VMEM(_S_SHAPES[nm], F32) for nm in _S_NAMES]
        + [pltpu.VMEM((HEADS, 128, 128), F32) for _ in _P_BLOCKDIAG] + r_sems,
        compiler_params=_params(("arbitrary",)), name="mixer_fwd",
    )(pm, pm, *[p[nm] for nm in _P_NAMES], gab, x, w_pa4, w_pb4, w_o, g_post, *rider_ins)


def _mixer_bwd(pm, dab, states, p, rider=None, rider_ins=()):
    n_p = len(_P_NAMES)
    r_in, r_out_specs, r_out_shape, r_sems = _rider_specs(rider, rider_ins)

    def body(*refs):
        ((pm_ref, xprev_ref, dab_ref), si_refs, p_list, ride_in, (dpm_ref,), dp_list, ride_out, ds_refs, (carry_ref,),
         dense_list, ddense_list, sems) = _split(refs, 3, 4, n_p, len(r_in), 1, n_p, len(r_out_specs), 4, 1, 3, 3, len(r_sems))
        p_refs = dict(zip(_P_NAMES, p_list))
        dp_refs = dict(zip(_P_NAMES, dp_list))
        dense = dict(zip(_P_BLOCKDIAG, dense_list))
        ddense = dict(zip(_P_BLOCKDIAG, ddense_list))
        i = pl.program_id(0)
        blk = N_SWEEP - 1 - i
        _ride(rider, ("first",), i == 0, ride_in, ride_out, sems)
        _ride(rider, ("last",), i == N_SWEEP - 1, ride_in, ride_out, sems, flush_later=True)

        @pl.when(i == 0)
        def _():
            for r in ds_refs:
                r[...] = jnp.zeros_like(r)
            for nm in _P_NAMES:
                if nm in _P_BLOCKDIAG:
                    ddense[nm][...] = jnp.zeros_like(ddense[nm])
                    _expand_blockdiag(p_refs[nm], dense[nm])
                else:
                    dp_refs[nm][...] = jnp.zeros_like(dp_refs[nm])
            carry_ref[...] = jnp.zeros_like(carry_ref)

        pv = {nm: (_per_head(dense[nm]) if nm in _P_BLOCKDIAG else p_refs[nm][...]) for nm in _P_NAMES}
        dst = {name: _per_head(r) for name, r in zip(_S_NAMES, ds_refs)}
        st = {name: [r[0, h] for h in range(HEADS)] for name, r in zip(_S_NAMES, si_refs)}
        xprev8 = jnp.where(blk > 0, xprev_ref[CHUNK - 8:CHUNK, :], 0.0)
        _, vjp = jax.vjp(functools.partial(_mixer_chunk, _VJP_OPS), pv, st, pm_ref[...], xprev8)
        dp_sum, dst, dpm, dxprev8 = vjp((dab_ref[...], dst))
        reach = jnp.concatenate([jnp.zeros((SWEEP * CHUNK - 8, 512), F32), carry_ref[...]], axis=0)
        dpm_ref[:, 0:PM_XM] = dpm[:, 0:PM_XM].astype(BF16)
        dpm_ref[:, PM_XM:PM_XM + 512] = (dpm[:, PM_XM:PM_XM + 512] + reach).astype(BF16)
        dpm_ref[:, PM_XM + 512:PM_W] = dpm[:, PM_XM + 512:PM_W].astype(BF16)
        carry_ref[...] = dxprev8
        for name, r in zip(_S_NAMES, ds_refs):
            for h in range(HEADS):
                r[h] = dst[name][h]
        for nm in _P_NAMES:
            if nm in _P_BLOCKDIAG:
                for h in range(HEADS):
                    ddense[nm][h] += dp_sum[nm][h]
            else:
                dp_refs[nm][...] += dp_sum[nm]

        @pl.when(i == N_SWEEP - 1)
        def _():
            for nm in _P_BLOCKDIAG:
                _collect_blockdiag(ddense[nm], dp_refs[nm])

        _ride(rider, ("early",), i == 1, ride_in, ride_out, sems)
        _ride(rider, ("middle",), i == _middle_step(rider, N_SWEEP), ride_in, ride_out, sems)
        _ride_done(rider, i == N_SWEEP - 1, ride_out, sems)

    rev = lambda i: (N_SWEEP - 1 - i, 0)
    in_specs = [pl.BlockSpec((SWEEP * CHUNK, PM_W), rev),
                pl.BlockSpec((CHUNK, 512), lambda i: (jnp.maximum(SWEEP * (N_SWEEP - 1 - i) - 1, 0), PM_XM // 512)),
                pl.BlockSpec((SWEEP * CHUNK, 1024), rev)]
    for nm in _S_NAMES:
        in_specs.append(pl.BlockSpec((1,) + _S_SHAPES[nm], lambda i: (N_SWEEP - 1 - i, 0, 0, 0)))
    in_specs += [_const_spec(_P_SHAPES[nm]) for nm in _P_NAMES] + r_in
    out_specs = [pl.BlockSpec((SWEEP * CHUNK, PM_W), rev)] + [_const_spec(_P_SHAPES[nm]) for nm in _P_NAMES]
    out_shape = [jax.ShapeDtypeStruct((SEQ, PM_W), BF16)] + [jax.ShapeDtypeStruct(_P_SHAPES[nm], F32) for nm in _P_NAMES]
    res = pl.pallas_call(
        body, grid=(N_SWEEP,), in_specs=in_specs, out_specs=out_specs + r_out_specs, out_shape=out_shape + r_out_shape,
        scratch_shapes=[pltpu.VMEM(_S_SHAPES[nm], F32) for nm in _S_NAMES] + [pltpu.VMEM((8, 512), F32)]
        + [pltpu.VMEM((HEADS, 128, 128), F32) for _ in range(2 * len(_P_BLOCKDIAG))] + r_sems,
        compiler_params=_params(("arbitrary",)), name="mixer_bwd",
    )(pm, pm, dab, *states, *[p[nm] for nm in _P_NAMES], *rider_ins)
    return res[0], dict(zip(_P_NAMES, res[1:1 + n_p])), res[1 + n_p:]


def _tok(width):
    return pl.BlockSpec((TOK_TILE, width), lambda i: (i, 0))


def _once(shape):
    zeros = (0,) * len(shape)
    return pl.BlockSpec(shape, lambda i: zeros, pipeline_mode=pl.Buffered(1))


def _rms_fwd(x):
    r = lax.rsqrt(_mean(x * x) + EPS)
    return x * r, r


def _rms_bwd(dy, xn, r, g):
    gd = dy * g
    return r * (gd - xn * _mean(xn * gd))


def _tiled_call(body, in_specs, out_specs, out_shape, args, name, rider=None, rider_ins=(), scratch=()):
    r_in, r_out_specs, r_out_shape, r_scratch = _rider_specs(rider, rider_ins)
    n_in, n_out = len(in_specs), len(out_specs)

    def hosted(*refs):
        ins, ride_in, outs, ride_out, own, r_scr = _split(refs, n_in, len(r_in), n_out, len(r_out_specs), len(scratch),
                                                          len(r_scratch))
        i = pl.program_id(0)
        _ride(rider, ("first",), i == 0, ride_in, ride_out, r_scr)
        body(*ins, *outs, *own)
        _ride(rider, ("early",), i == 1, ride_in, ride_out, r_scr)
        _ride(rider, ("middle",), i == _middle_step(rider, N_TOK_TILE), ride_in, ride_out, r_scr)
        _ride(rider, ("last",), i == N_TOK_TILE - 1, ride_in, ride_out, r_scr)

    res = pl.pallas_call(
        hosted, grid=(N_TOK_TILE,), in_specs=list(in_specs) + r_in, out_specs=list(out_specs) + r_out_specs,
        out_shape=list(out_shape) + r_out_shape, scratch_shapes=list(scratch) + r_scratch,
        compiler_params=_params(("arbitrary",)), name=name,
    )(*args, *rider_ins)
    return res[:n_out], res[n_out:]


def _join_rows(w4_ref, wt_ref):
    @pl.when(pl.program_id(0) == 0)
    def _():
        for j in range(N_CHIP):
            wt_ref[j * IN_SHARD:(j + 1) * IN_SHARD, :] = w4_ref[j]


def _joined_scratch():
    return [pltpu.VMEM((D_IN, D_MODEL), BF16)]


def _in_proj(x, g_pre, w4_in, rider=None, rider_ins=()):
    def body(x_ref, g_ref, w4_ref, pm_ref, gab_ref, h_ref, wt_ref):
        _join_rows(w4_ref, wt_ref)
        xn, _ = _rms_fwd(x_ref[...])
        h = (xn * g_ref[...]).astype(BF16)
        h_ref[...] = h
        pm_ref[:, 0:PM_XM] = _nt(h, wt_ref[0:IN_ALOW, :])
        pm_ref[:, PM_XM:PM_AL] = _nt(h, wt_ref[IN_XM:IN_GATES, :])
        pm_ref[:, PM_AL:PM_W] = _nt(h, wt_ref[IN_ALOW:IN_ALOW + 128, :])
        gab_ref[...] = _nt(h, wt_ref[IN_GATES:D_IN, :])

    return _tiled_call(
        body, [_tok(D_MODEL), _once((1, D_MODEL)), _once((N_CHIP, IN_SHARD, D_MODEL))],
        [_tok(PM_W), _tok(GAB_W), _tok(D_MODEL)],
        [jax.ShapeDtypeStruct((SEQ, PM_W), F32), jax.ShapeDtypeStruct((SEQ, GAB_W), F32),
         jax.ShapeDtypeStruct((SEQ, D_MODEL), BF16)], (x, g_pre, w4_in), "in_proj", rider, rider_ins, _joined_scratch())


def _mlp(x1, target, g_pre, g_post, w_up4, w_down_a4, w_down_b4):
    def body(x1_ref, t_ref, gpre_ref, gpost_ref, wup_ref, wda_ref, wdb_ref,
             dx1_ref, u_ref, dd_ref, h2_ref, dpre_ref, dgpost_ref, dgpre_ref, loss_ref):
        @pl.when(pl.program_id(0) == 0)
        def _():
            dgpost_ref[...] = jnp.zeros_like(dgpost_ref)
            dgpre_ref[...] = jnp.zeros_like(dgpre_ref)
            loss_ref[...] = jnp.zeros_like(loss_ref)

        x1 = x1_ref[...]
        gpre = gpre_ref[...]
        gpost = gpost_ref[...]
        xn2, r2 = _rms_fwd(x1)
        h2 = (xn2 * gpre).astype(BF16)
        h2_ref[...] = h2
        rl = []
        d = jnp.zeros((TOK_TILE, D_MODEL), F32)
        for j in range(N_CHIP):
            blk = slice(j * 1024, (j + 1) * 1024)
            r = jnp.maximum(jnp.dot(h2, wup_ref[j], preferred_element_type=F32), 0.0)
            rl.append(r)
            u = (r * r).astype(BF16)
            u_ref[:, blk] = u
            d = d + jnp.dot(u[:, 0:512], wda_ref[j], preferred_element_type=F32)
            d = d + jnp.dot(u[:, 512:1024], wdb_ref[j], preferred_element_type=F32)
        dn, r3 = _rms_fwd(d)
        diff = x1 + dn * gpost - t_ref[...]
        loss_ref[...] += jnp.sum(diff * diff, keepdims=True) * (0.5 / D_MODEL)
        dy = diff * (1.0 / D_MODEL)
        dgpost_ref[...] += jnp.sum(dy * dn, axis=0, keepdims=True)
        dd = _rms_bwd(dy, dn, r3, gpost).astype(BF16)
        dd_ref[...] = dd
        dh2 = jnp.zeros((TOK_TILE, D_MODEL), F32)
        for j in range(N_CHIP):
            blk = slice(j * 1024, (j + 1) * 1024)
            du = jnp.concatenate([_nt(dd, wda_ref[j]), _nt(dd, wdb_ref[j])], axis=1)
            dpre = (du * (2.0 * rl[j])).astype(BF16)
            dpre_ref[:, blk] = dpre
            dh2 = dh2 + _nt(dpre, wup_ref[j])
        dgpre_ref[...] += jnp.sum(dh2 * xn2, axis=0, keepdims=True)
        dx1_ref[...] = dy + _rms_bwd(dh2, xn2, r2, gpre)

    acc = pl.BlockSpec((1, D_MODEL), lambda i: (0, 0))
    return pl.pallas_call(
        body, grid=(N_TOK_TILE,),
        in_specs=[_tok(D_MODEL), _tok(D_MODEL), _once((1, D_MODEL)), _once((1, D_MODEL)),
                  _once((N_CHIP, D_MODEL, 1024)), _once((N_CHIP, 512, D_MODEL)), _once((N_CHIP, 512, D_MODEL))],
        out_specs=[_tok(D_MODEL), _tok(D_FF), _tok(D_MODEL), _tok(D_MODEL), _tok(D_FF), acc, acc,
                   pl.BlockSpec((1, 128), lambda i: (0, 0))],
        out_shape=[jax.ShapeDtypeStruct((SEQ, D_MODEL), F32), jax.ShapeDtypeStruct((SEQ, D_FF), BF16),
                   jax.ShapeDtypeStruct((SEQ, D_MODEL), BF16), jax.ShapeDtypeStruct((SEQ, D_MODEL), BF16),
                   jax.ShapeDtypeStruct((SEQ, D_FF), BF16), jax.ShapeDtypeStruct((1, D_MODEL), F32),
                   jax.ShapeDtypeStruct((1, D_MODEL), F32), jax.ShapeDtypeStruct((1, 128), F32)],
        compiler_params=_params(("arbitrary",)), name="mlp_fwd_bwd",
    )(x1, target, g_pre, g_post, w_up4, w_down_a4, w_down_b4)


def _merge_bwd(dx1, mix, ab, gab, merged, w_pa4, w_pb4, w_o, g_post):
    def body(dx1_ref, mix_ref, ab_ref, gab_ref, mg_ref, wpa_ref, wpb_ref, wo_ref, g_ref,
             dgab_ref, dab_ref, dg_ref, dwpa_ref, dwpb_ref, dwo_ref, acc_pa, acc_pb, acc_o):
        @pl.when(pl.program_id(0) == 0)
        def _():
            dg_ref[...] = jnp.zeros_like(dg_ref)
            acc_pa[...] = jnp.zeros_like(acc_pa)
            acc_pb[...] = jnp.zeros_like(acc_pb)
            acc_o[...] = jnp.zeros_like(acc_o)

        dx1 = dx1_ref[...]
        mn, r = _rms_fwd(mix_ref[...])
        dg_ref[...] += jnp.sum(dx1 * mn, axis=0, keepdims=True)
        dmix = _rms_bwd(dx1, mn, r, g_ref[...]).astype(BF16)
        acc_o[...] += _tn(mg_ref[...], dmix)
        dmerged = _nt(dmix, wo_ref[...])
        a = ab_ref[:, 0:512]
        b = ab_ref[:, 512:1024]
        da = jnp.zeros((TOK_TILE, 512), F32)
        db = jnp.zeros((TOK_TILE, 512), F32)
        dyas, dybs = [], []
        for j in range(N_CHIP):
            blk = slice(j * 256, (j + 1) * 256)
            blk_b = slice(1024 + j * 256, 1024 + (j + 1) * 256)
            dm = dmerged[:, blk]
            ya = jnp.dot(a, wpa_ref[j], preferred_element_type=F32)
            yb = jnp.dot(b, wpb_ref[j], preferred_element_type=F32)
            sa = _sigmoid(gab_ref[:, blk])
            sb = _sigmoid(gab_ref[:, blk_b])
            dya = (dm * sa).astype(BF16)
            dyb = (dm * sb).astype(BF16)
            dyas.append(dya)
            dybs.append(dyb)
            dgab_ref[:, blk] = (dm * ya * sa * (1.0 - sa)).astype(BF16)
            dgab_ref[:, blk_b] = (dm * yb * sb * (1.0 - sb)).astype(BF16)
            da = da + _nt(dya, wpa_ref[j])
            db = db + _nt(dyb, wpb_ref[j])
        dab_ref[:, 0:512] = da
        dab_ref[:, 512:1024] = db
        acc_pa[...] += _tn(a, jnp.concatenate(dyas, axis=1))
        acc_pb[...] += _tn(b, jnp.concatenate(dybs, axis=1))

        @pl.when(pl.program_id(0) == N_TOK_TILE - 1)
        def _():
            dwo_ref[...] = acc_o[...].astype(BF16)
            for j in range(N_CHIP):
                dwpa_ref[j] = acc_pa[:, j * 256:(j + 1) * 256].astype(BF16)
                dwpb_ref[j] = acc_pb[:, j * 256:(j + 1) * 256].astype(BF16)

    whole = lambda shape: pl.BlockSpec(shape, lambda i: (0,) * len(shape))
    return pl.pallas_call(
        body, grid=(N_TOK_TILE,),
        in_specs=[_tok(D_MODEL), _tok(D_MODEL), _tok(1024), _tok(GAB_W), _tok(D_MODEL), _once((N_CHIP, 512, 256)),
                  _once((N_CHIP, 512, 256)), _once((D_MODEL, D_MODEL)), _once((1, D_MODEL))],
        out_specs=[_tok(GAB_W), _tok(1024), whole((1, D_MODEL)), whole((N_CHIP, 512, 256)), whole((N_CHIP, 512, 256)),
                   whole((D_MODEL, D_MODEL))],
        out_shape=[jax.ShapeDtypeStruct((SEQ, GAB_W), BF16), jax.ShapeDtypeStruct((SEQ, 1024), F32),
                   jax.ShapeDtypeStruct((1, D_MODEL), F32), jax.ShapeDtypeStruct((N_CHIP, 512, 256), BF16),
                   jax.ShapeDtypeStruct((N_CHIP, 512, 256), BF16), jax.ShapeDtypeStruct((D_MODEL, D_MODEL), BF16)],
        scratch_shapes=[pltpu.VMEM((512, D_MODEL), F32), pltpu.VMEM((512, D_MODEL), F32),
                        pltpu.VMEM((D_MODEL, D_MODEL), F32)],
        compiler_params=_params(("arbitrary",)), name="merge_bwd",
    )(dx1, mix, ab, gab, merged, w_pa4, w_pb4, w_o, g_post)


def _in_proj_bwd(dpm, dgab, x, dx1, g_pre, w4_in, rider=None, rider_ins=()):
    def body(dpm_ref, dgab_ref, x_ref, dx1_ref, g_ref, w4_ref, dx_ref, dg_ref, wt_ref):
        _join_rows(w4_ref, wt_ref)

        @pl.when(pl.program_id(0) == 0)
        def _():
            dg_ref[...] = jnp.zeros_like(dg_ref)

        dh = jnp.dot(dpm_ref[:, 0:PM_XM], wt_ref[0:IN_ALOW, :], preferred_element_type=F32)
        dh = dh + jnp.dot(dpm_ref[:, PM_XM:PM_AL], wt_ref[IN_XM:IN_GATES, :], preferred_element_type=F32)
        dh = dh + jnp.dot(dpm_ref[:, PM_AL:PM_W], wt_ref[IN_ALOW:IN_ALOW + 128, :], preferred_element_type=F32)
        dh = dh + jnp.dot(dgab_ref[...], wt_ref[IN_GATES:D_IN, :], preferred_element_type=F32)
        xn, r = _rms_fwd(x_ref[...])
        dg_ref[...] += jnp.sum(dh * xn, axis=0, keepdims=True)
        dx_ref[...] = dx1_ref[...] + _rms_bwd(dh, xn, r, g_ref[...])

    return _tiled_call(
        body, [_tok(PM_W), _tok(GAB_W), _tok(D_MODEL), _tok(D_MODEL), _once((1, D_MODEL)),
               _once((N_CHIP, IN_SHARD, D_MODEL))],
        [_tok(D_MODEL), pl.BlockSpec((1, D_MODEL), lambda i: (0, 0))],
        [jax.ShapeDtypeStruct((SEQ, D_MODEL), F32), jax.ShapeDtypeStruct((1, D_MODEL), F32)],
        (dpm, dgab, x, dx1, g_pre, w4_in), "in_proj_bwd", rider, rider_ins, _joined_scratch())


def _dw_in(dpm, dgab, h):
    n_pm = PM_AL // 512
    n_blk = n_pm + GAB_W // 512

    def place(o_ref, rows, lo, hi):
        for j in range(N_CHIP):
            a, b = max(lo, j * IN_SHARD), min(hi, (j + 1) * IN_SHARD)
            if a < b:
                o_ref[j, a - j * IN_SHARD:b - j * IN_SHARD, :] = rows(a - lo, b - lo)

    def body(dpm_ref, dgab_ref, dal_ref, h_ref, o_ref, blk_ref):
        i = pl.program_id(0)

        @pl.when(i < n_pm)
        def _():
            blk_ref[...] = _tn(dpm_ref[...], h_ref[...]).astype(BF16)

        @pl.when(i >= n_pm)
        def _():
            blk_ref[...] = _tn(dgab_ref[...], h_ref[...]).astype(BF16)

        for k in range(n_blk):
            off = k * 512 + (IN_XM - IN_ALOW) * (k >= IN_ALOW // 512)

            @pl.when(i == k)
            def _():
                place(o_ref, lambda a, b: blk_ref[a:b, :], off, off + 512)

        @pl.when(i == 0)
        def _():
            a_low = _tn(dal_ref[...], h_ref[...])[0:IN_XM - IN_ALOW].astype(BF16)
            place(o_ref, lambda a, b: a_low[a:b], IN_ALOW, IN_XM)

    return pl.pallas_call(
        body, grid=(n_blk,),
        in_specs=[pl.BlockSpec((SEQ, 512), lambda i: (0, jnp.minimum(i, n_pm - 1))),
                  pl.BlockSpec((SEQ, 512), lambda i: (0, jnp.maximum(i - n_pm, 0))),
                  pl.BlockSpec((SEQ, 128), lambda i: (0, PM_AL // 128)),
                  _once((SEQ, D_MODEL))],
        out_specs=pl.BlockSpec((N_CHIP, IN_SHARD, D_MODEL), lambda i: (0, 0, 0)),
        out_shape=jax.ShapeDtypeStruct((N_CHIP, IN_SHARD, D_MODEL), BF16),
        scratch_shapes=[pltpu.VMEM((512, D_MODEL), BF16)],
        compiler_params=_params(("arbitrary",)), name="dw_in",
    )(dpm, dgab, dpm, h)


def _tn_matmul(a, b, name, shards=1, tm=1024, rider=None, rider_ins=()):
    m, n = a.shape[1], b.shape[1]
    tm = min(tm, m)
    tn = n // shards if shards > 1 else min(n, 1024)
    steps_i, steps_j = m // tm, n // tn
    r_in, r_out_specs, r_out_shape, r_scratch = _rider_specs(rider, rider_ins)

    def body(*refs):
        (a_ref, b_ref), ride_in, (o_ref,), ride_out, scratch = _split(refs, 2, len(r_in), 1, len(r_out_specs), len(r_scratch))
        step = pl.program_id(0) * steps_j + pl.program_id(1)
        last = step == steps_i * steps_j - 1
        _ride(rider, ("first",), step == 0, ride_in, ride_out, scratch)
        _ride(rider, ("middle", "last"), last, ride_in, ride_out, scratch, flush_later=True)
        o_ref[...] = _tn(a_ref[...], b_ref[...]).astype(BF16)
        _ride_done(rider, last, ride_out, scratch)

    if shards > 1:
        out_spec = pl.BlockSpec((None, tm, tn), lambda i, j: (j, i, 0))
        out_shape = jax.ShapeDtypeStruct((shards, m, tn), BF16)
    else:
        out_spec = pl.BlockSpec((tm, tn), lambda i, j: (i, j))
        out_shape = jax.ShapeDtypeStruct((m, n), BF16)
    res = pl.pallas_call(
        body, grid=(steps_i, steps_j),
        in_specs=[pl.BlockSpec((SEQ, tm), lambda i, j: (0, i)), pl.BlockSpec((SEQ, tn), lambda i, j: (0, j))] + r_in,
        out_specs=[out_spec] + r_out_specs, out_shape=[out_shape] + r_out_shape, scratch_shapes=r_scratch,
        compiler_params=_params(("arbitrary", "arbitrary")), name=name,
    )(a, b, *rider_ins)
    return res[0] if rider is None else (res[0], res[1:])


MESH = pl.DeviceIdType.MESH
ANY = pl.BlockSpec(memory_space=pl.ANY)
VMEM_WHOLE = pl.BlockSpec(memory_space=pltpu.VMEM)

_BIG = ("w_in", "w_pa", "w_pb", "w_o", "w_up", "w_down")
_BIG_SHARD = {"w_in": (IN_SHARD, D_MODEL), "w_pa": (512, 256), "w_pb": (512, 256), "w_o": (256, D_MODEL),
              "w_up": (D_MODEL, 1024), "w_down": (1024, D_MODEL),
              "w_down_a": (512, D_MODEL), "w_down_b": (512, D_MODEL)}
_BIG_SPLIT = {"w_in": 1, "w_pa": 0, "w_pb": 0, "w_o": 0, "w_up": 0, "w_down": 0, "w_down_a": 0, "w_down_b": 0}


def _half(ref, e, name, lead=0, part=None):
    axis = _BIG_SPLIT[name]
    size = _BIG_SHARD[name][axis] // 2
    start = e * size
    if part is not None:
        size //= 2
        start = start + part * size
    start = pl.multiple_of(start, 128 if axis == 1 else 16)
    idx = [pl.ds(0, ref.shape[a]) for a in range(lead)]
    idx += [pl.ds(start, size), pl.ds(0, _BIG_SHARD[name][1])] if axis == 0 else [pl.ds(0, _BIG_SHARD[name][0]), pl.ds(start, size)]
    return ref.at[tuple(idx)]


def _half_shape(name):
    r, c = _BIG_SHARD[name]
    return (r // 2, c) if _BIG_SPLIT[name] == 0 else (r, c // 2)


def _remote(src, dst, send_sems, recv_sems, k, to):
    return pltpu.make_async_remote_copy(src_ref=src, dst_ref=dst, send_sem=send_sems.at[k], recv_sem=recv_sems.at[k],
                                        device_id=to, device_id_type=MESH)


def _mesh_place():
    x, y, c = lax.axis_index("x"), lax.axis_index("y"), lax.axis_index("c")
    return x, y, c, [(1 - x, y), (x, 1 - y), (1 - x, 1 - y)]


class _Gather:
    def __init__(self, names, small=(), middle_at=0.5):
        self.middle_at = middle_at
        self.names = tuple(names)
        self.nb = len(self.names)
        self.n = self.nb + len(small)
        self.n_sems = 8 * self.nb + 3 * len(small)
        self.n_flush = 6 * self.nb + len(small)
        self.out_shape = [jax.ShapeDtypeStruct((N_CHIP,) + _BIG_SHARD[nm], BF16) for nm in self.names]
        self.out_shape += [jax.ShapeDtypeStruct((N_CHIP,) + s.shape, s.dtype) for s in small]
        self.in_space = [self._in_spec(nm) for nm in self.names] + [VMEM_WHOLE] * len(small)

    @staticmethod
    def _in_spec(name):
        if name in ("w_down_a", "w_down_b"):
            half = 0 if name == "w_down_a" else 1
            return pl.BlockSpec(_BIG_SHARD[name], lambda *_: (half, 0), pipeline_mode=pl.Buffered(1))
        return VMEM_WHOLE

    def _copies(self, ins, outs, ss, rs, k):
        x, y, c, _ = _mesh_place()
        name = self.names[k]
        me, xn, yn, dg = 2 * x + y, 2 * (1 - x) + y, 2 * x + (1 - y), 2 * (1 - x) + (1 - y)
        to_x, to_y, sibling = (1 - x, y, c), (x, 1 - y, c), (x, y, 1 - c)

        def region(slot, e, part=None):
            return _half(outs[k].at[slot], e, name, part=part)

        def copy(pair, src, dst, to):
            return _remote(src, dst, ss, rs, 8 * k + pair, to)

        mine = region(me, c)
        sent = [copy(0, mine, mine, to_x), copy(1, mine, mine, to_y),
                copy(2, region(xn, c, 0), region(xn, c, 0), to_y), copy(3, region(yn, c, 1), region(yn, c, 1), to_x),
                copy(4, region(xn, c), region(xn, c), sibling), copy(5, region(yn, c), region(yn, c), sibling),
                copy(6, region(dg, c, 0), region(dg, c, 0), sibling), copy(7, region(dg, c, 1), region(dg, c, 1), sibling)]
        landing = [region(xn, c), region(yn, c), region(dg, c, 0), region(dg, c, 1),
                   region(xn, 1 - c), region(yn, 1 - c), region(dg, 1 - c, 0), region(dg, 1 - c, 1)]
        received = [copy(pair, dst, dst, sibling) for pair, dst in enumerate(landing)]
        return sent, received

    def _small(self, ins, outs, ss, rs, k, j, peer, slot, c):
        return _remote(ins[k], outs[k].at[slot], ss, rs, 8 * self.nb + 3 * (k - self.nb) + j, (*peer, c))

    def flush(self, phase, lands, outs, fs):
        x, y, c, _ = _mesh_place()
        me, xn, yn, dg = 2 * x + y, 2 * (1 - x) + y, 2 * x + (1 - y), 2 * (1 - x) + (1 - y)

        def pieces(k):
            name = self.names[k]
            spots = [lambda r: r.at[me], lambda r: _half(r.at[xn], c, name), lambda r: _half(r.at[yn], c, name),
                     lambda r: _half(r.at[xn], 1 - c, name), lambda r: _half(r.at[yn], 1 - c, name), lambda r: r.at[dg]]
            return [pltpu.make_async_copy(spot(lands[k]), spot(outs[k]), fs.at[6 * k + t]) for t, spot in enumerate(spots)]

        ready = {"first": (0,), "middle": (1, 2), "late": (3, 4), "last": (5,)}[phase]
        for k in range(self.nb):
            cps = pieces(k)
            for t in ready:
                cps[t].start()
        if phase == "last":
            small = [pltpu.make_async_copy(lands[k], outs[k], fs.at[6 * self.nb + k - self.nb]) for k in range(self.nb, self.n)]
            for cp in small:
                cp.start()
            for k in range(self.nb):
                for cp in pieces(k):
                    cp.wait()
            for cp in small:
                cp.wait()

    def first(self, ins, outs, ss, rs):
        x, y, c, peers = _mesh_place()
        me = 2 * x + y
        for k in range(self.nb):
            outs[k][me] = ins[k][...].astype(BF16)
            sent, _ = self._copies(ins, outs, ss, rs, k)
            sent[0].start()
            sent[1].start()
        for k in range(self.nb, self.n):
            for j, peer in enumerate(peers):
                self._small(ins, outs, ss, rs, k, j, peer, me, c).start()
            outs[k][me] = ins[k][...]

    def middle(self, ins, outs, ss, rs):
        for k in range(self.nb):
            sent, received = self._copies(ins, outs, ss, rs, k)
            for pair in (0, 1):
                received[pair].wait_recv()
                sent[2 + pair].start()
                sent[4 + pair].start()

    def late(self, ins, outs, ss, rs):
        for k in range(self.nb):
            _, received = self._copies(ins, outs, ss, rs, k)
            for pair in (4, 5):
                received[pair].wait_recv()

    def last(self, ins, outs, ss, rs):
        x, y, c, peers = _mesh_place()
        for k in range(self.nb):
            sent, received = self._copies(ins, outs, ss, rs, k)
            for pair in (2, 3):
                received[pair].wait_recv()
                sent[4 + pair].start()
        for k in range(self.nb):
            sent, received = self._copies(ins, outs, ss, rs, k)
            for pair in (6, 7):
                received[pair].wait_recv()
            for cp in sent:
                cp.wait_send()
        for k in range(self.nb, self.n):
            for j, (px, py) in enumerate(peers):
                self._small(ins, outs, ss, rs, k, j, (px, py), 2 * px + py, c).wait_recv()
                self._small(ins, outs, ss, rs, k, j, (px, py), 2 * x + y, c).wait_send()


def _run_alone(rider, ins, name):
    r_in, r_out_specs, r_out_shape, r_scratch = _rider_specs(rider, ins)

    def body(*refs):
        ride_in, ride_out, scratch = _split(refs, len(r_in), len(r_out_specs), len(r_scratch))
        _ride(rider, ("first", "middle", "last"), pl.program_id(0) == 0, ride_in, ride_out, scratch)

    return pl.pallas_call(
        body, grid=(1,), in_specs=r_in, out_specs=r_out_specs, out_shape=r_out_shape, scratch_shapes=r_scratch,
        compiler_params=_params(("arbitrary",)), name=name,
    )(*ins)


class _Presum:
    in_space = ANY

    def __init__(self, names, base=0):
        self.names = tuple(names)
        self.n = len(self.names)
        self.base = base
        self.n_sems = 3 * self.n
        self.out_shape = [jax.ShapeDtypeStruct((N_CHIP,) + _half_shape(nm), BF16) for nm in self.names]
        self.work_shape = self.out_shape + self.out_shape

    def _stage(self, ins, bufs, ss, k, e, which):
        n = self.n
        return pltpu.make_async_copy(_half(ins[k], e, self.names[k], lead=1), bufs[which * n + k],
                                     ss.at[self.base + which * n + k])

    def _give(self, bufs, ss, rs, k, sibling):
        return _remote(bufs[self.n + k], bufs[k], ss, rs, self.base + k, sibling)

    def first(self, ins, bufs, ss, rs):
        x, y, c, _ = _mesh_place()
        for k in range(self.n):
            self._stage(ins, bufs, ss, k, 1 - c, 1).start()
        for k in range(self.n):
            self._stage(ins, bufs, ss, k, c, 2).start()
        for k in range(self.n):
            self._stage(ins, bufs, ss, k, 1 - c, 1).wait()
            self._give(bufs, ss, rs, k, (x, y, 1 - c)).start()

    def middle(self, ins, bufs, ss, rs):
        pass

    def last(self, ins, bufs, ss, rs):
        x, y, c, _ = _mesh_place()
        for k in range(self.n):
            self._give(bufs, ss, rs, k, (x, y, 1 - c)).wait_recv()
            self._stage(ins, bufs, ss, k, c, 2).wait()

            @pl.loop(0, N_CHIP)
            def _(j):
                bufs[k][j] = (bufs[k][j].astype(F32) + bufs[2 * self.n + k][j].astype(F32)).astype(BF16)
        for k in range(self.n):
            self._give(bufs, ss, rs, k, (x, y, 1 - c)).wait_send()


class _ReduceRelay:
    middle_at = 0.75

    def __init__(self, names, base=0):
        self.names = tuple(names)
        self.n = len(self.names)
        self.base = base
        self.n_sems = 6 * self.n
        self.out_shape = [jax.ShapeDtypeStruct((N_CHIP,) + _half_shape(nm), BF16) for nm in self.names]
        quarter = [jax.ShapeDtypeStruct(self._part_shape(nm), BF16) for nm in self.names]
        self.work_shape = quarter + quarter

    @staticmethod
    def _part_shape(name):
        r, c = _half_shape(name)
        return (r // 2, c) if _BIG_SPLIT[name] == 0 else (r, c // 2)

    def _part(self, ref, name, p):
        r, c = self._part_shape(name)
        return ref.at[pl.ds(p * r, r), pl.ds(0, c)] if _BIG_SPLIT[name] == 0 else ref.at[pl.ds(0, r), pl.ds(p * c, c)]

    def _copies(self, ins, bufs, ss, rs, k):
        x, y, c, _ = _mesh_place()
        name, n = self.names[k], self.n
        me, xn, yn, dg = 2 * x + y, 2 * (1 - x) + y, 2 * x + (1 - y), 2 * (1 - x) + (1 - y)
        to_x, to_y = (1 - x, y, c), (x, 1 - y, c)
        mine = lambda slot, p: self._part(ins[k].at[slot], name, p)
        slot = lambda s, p: self._part(bufs[k].at[s], name, p)
        from_x, from_y = bufs[n + k], bufs[2 * n + k]

        def copy(pair, src, dst, to):
            return _remote(src, dst, ss, rs, self.base + 6 * k + pair, to)

        sent = [copy(0, mine(dg, 0), from_x, to_x), copy(1, mine(dg, 1), from_y, to_y),
                copy(2, mine(xn, 0), slot(me, 0), to_x), copy(3, mine(yn, 1), slot(me, 1), to_y),
                copy(4, from_y, slot(me, 1), to_x), copy(5, from_x, slot(me, 0), to_y)]
        landing = [from_x, from_y, slot(xn, 0), slot(yn, 1), slot(xn, 1), slot(yn, 0)]
        received = [copy(pair, dst, dst, to_x) for pair, dst in enumerate(landing)]
        return sent, received

    def first(self, ins, bufs, ss, rs):
        x, y, c, _ = _mesh_place()
        me, dg = 2 * x + y, 2 * (1 - x) + (1 - y)
        for k in range(self.n):
            sent, _ = self._copies(ins, bufs, ss, rs, k)
            for pair in range(4):
                sent[pair].start()
        for k in range(self.n):
            bufs[k][me] = ins[k][me]
            bufs[k][dg] = jnp.zeros(_half_shape(self.names[k]), BF16)

    def middle(self, ins, bufs, ss, rs):
        x, y, c, _ = _mesh_place()
        xn, yn = 2 * (1 - x) + y, 2 * x + (1 - y)
        for k in range(self.n):
            sent, received = self._copies(ins, bufs, ss, rs, k)
            name, n = self.names[k], self.n
            for pair, buf, own in ((0, bufs[n + k], self._part(ins[k].at[yn], name, 0)),
                                   (1, bufs[2 * n + k], self._part(ins[k].at[xn], name, 1))):
                received[pair].wait_recv()
                buf[...] = (buf[...].astype(F32) + own[...].astype(F32)).astype(BF16)
            sent[5].start()
            sent[4].start()

    def last(self, ins, bufs, ss, rs):
        for k in range(self.n):
            sent, received = self._copies(ins, bufs, ss, rs, k)
            for pair in range(2, 6):
                received[pair].wait_recv()
            for cp in sent:
                cp.wait_send()


class _PresumThenRelay:
    in_space = ANY
    middle_at = _ReduceRelay.middle_at

    def __init__(self, names):
        self.relay = _ReduceRelay(names)
        self.pre = _Presum(names, base=self.relay.n_sems)
        self.n_sems = self.relay.n_sems + self.pre.n_sems
        self.out_shape = self.relay.out_shape
        self.work_shape = list(self.relay.work_shape) + list(self.pre.out_shape) + list(self.pre.work_shape)
        self.n_relay = len(self.relay.out_shape) + len(self.relay.work_shape)

    def first(self, ins, bufs, ss, rs):
        self.pre.first(ins, bufs[self.n_relay:], ss, rs)

    def early(self, ins, bufs, ss, rs):
        self.pre.last(ins, bufs[self.n_relay:], ss, rs)
        self.relay.first(bufs[self.n_relay:], bufs[:self.n_relay], ss, rs)

    def middle(self, ins, bufs, ss, rs):
        self.relay.middle(bufs[self.n_relay:], bufs[:self.n_relay], ss, rs)

    def last(self, ins, bufs, ss, rs):
        self.relay.last(bufs[self.n_relay:], bufs[:self.n_relay], ss, rs)


class _SendPartials:
    def __init__(self, names, small_shape=None):
        self.n = len(names)
        self.small = small_shape is not None
        self.n_sems = 3 * self.n + 7
        self.out_shape = [jax.ShapeDtypeStruct((N_CHIP,) + _half_shape(nm), BF16) for nm in names]
        if self.small:
            self.out_shape.append(jax.ShapeDtypeStruct((N_DEV,) + small_shape, F32))

    def _piece(self, ins, outs, ss, rs, k, j, peer, src_slot, dst_slot, c):
        return _remote(ins[k].at[src_slot], outs[k].at[dst_slot], ss, rs, 3 * k + j, (*peer, c))

    def _small(self, ins, outs, ss, rs, r, other, slot):
        return _remote(ins[self.n], outs[self.n].at[slot], ss, rs, 3 * self.n + r, other)

    @staticmethod
    def _others(x, y, c):
        return [(x, y, 1 - c), (1 - x, y, c), (1 - x, y, 1 - c), (x, 1 - y, c), (x, 1 - y, 1 - c),
                (1 - x, 1 - y, c), (1 - x, 1 - y, 1 - c)]

    def first(self, ins, outs, ss, rs, only=None):
        x, y, c, peers = _mesh_place()
        me = 2 * x + y
        which = range(self.n) if only is None else only
        for k in which:
            for j, (px, py) in enumerate(peers):
                self._piece(ins, outs, ss, rs, k, j, (px, py), 2 * px + py, me, c).start()
        if self.small:
            for r, other in enumerate(self._others(x, y, c)):
                self._small(ins, outs, ss, rs, r, other, 4 * x + 2 * y + c).start()
            outs[self.n][4 * x + 2 * y + c] = ins[self.n][...]
        for k in which:
            outs[k][me] = ins[k][me]

    def middle(self, ins, outs, ss, rs):
        pass

    def last(self, ins, outs, ss, rs):
        x, y, c, peers = _mesh_place()
        me = 2 * x + y
        for k in range(self.n):
            for j, (px, py) in enumerate(peers):
                self._piece(ins, outs, ss, rs, k, j, (px, py), me, 2 * px + py, c).wait_recv()
                self._piece(ins, outs, ss, rs, k, j, (px, py), 2 * px + py, me, c).wait_send()
        if self.small:
            for r, (px, py, pc) in enumerate(self._others(x, y, c)):
                self._small(ins, outs, ss, rs, r, (px, py, pc), 4 * px + 2 * py + pc).wait_recv()
                self._small(ins, outs, ss, rs, r, (px, py, pc), 4 * x + 2 * y + c).wait_send()


class _PresumThenSend:
    def __init__(self, names):
        self.send = _SendPartials(names)
        self.pre = _Presum(names[-1:], base=self.send.n_sems)
        self.n = self.send.n
        self.n_sems = self.send.n_sems + self.pre.n_sems
        self.out_shape = self.send.out_shape
        self.work_shape = list(self.pre.out_shape) + list(self.pre.work_shape)
        self.in_space = [VMEM_WHOLE] * (self.n - 1) + [ANY]

    def _partials(self, ins, bufs):
        return list(ins[:self.n - 1]) + [bufs[self.n]]

    def first(self, ins, bufs, ss, rs):
        self.pre.first(ins[self.n - 1:], bufs[self.n:], ss, rs)
        self.send.first(ins, bufs[:self.n], ss, rs, only=range(self.n - 1))

    def early(self, ins, bufs, ss, rs):
        self.pre.last(ins[self.n - 1:], bufs[self.n:], ss, rs)
        self.send.first(self._partials(ins, bufs), bufs[:self.n], ss, rs, only=(self.n - 1,))

    def middle(self, ins, bufs, ss, rs):
        pass

    def last(self, ins, bufs, ss, rs):
        self.send.last(self._partials(ins, bufs), bufs[:self.n], ss, rs)


def _sum_swap(names, parts, small):
    n = len(parts)
    everyone = _SendPartials((), small.shape)

    def body(*refs):
        (p_hbm, (small_ref,), o_hbm, (osmall_ref,), p_refs, o_refs, (all_ref,),
         (send_sems, recv_sems, ss_small, rs_small, load_sems, leave_sems)) = _split(refs, n, 1, n, 1, n, n, 1, 6)
        x, y, c = lax.axis_index("x"), lax.axis_index("y"), lax.axis_index("c")
        loads = [pltpu.make_async_copy(p_hbm[k], p_refs[k], load_sems.at[k]) for k in range(n)]
        for cp in loads:
            cp.start()
        everyone.first([small_ref], [all_ref], ss_small, rs_small)

        def mine(k):
            part = _half(o_refs[k], c, names[k])
            return _remote(part, part, send_sems, recv_sems, k, (x, y, 1 - c))

        def leave(k, whose):
            e = c if whose == 0 else 1 - c
            return pltpu.make_async_copy(_half(o_refs[k], e, names[k]), _half(o_hbm[k], e, names[k]),
                                         leave_sems.at[2 * k + whose])

        for k in range(n):
            loads[k].wait()
            for e in range(2):
                @pl.when(c == e)
                def _():
                    g = p_refs[k][0].astype(F32)
                    for s in range(1, N_CHIP):
                        g = g + p_refs[k][s].astype(F32)
                    r, cols = _half_shape(names[k])
                    if _BIG_SPLIT[names[k]] == 0:
                        o_refs[k][e * r:(e + 1) * r, :] = g
                    else:
                        o_refs[k][:, e * cols:(e + 1) * cols] = g
            mine(k).start()
            leave(k, 0).start()
        for k in range(n):
            theirs = _half(o_refs[k], 1 - c, names[k])
            _remote(theirs, theirs, send_sems, recv_sems, k, (x, y, 1 - c)).wait_recv()
            leave(k, 1).start()
        everyone.last([small_ref], [all_ref], ss_small, rs_small)
        g = all_ref[0]
        for d in range(1, N_DEV):
            g = g + all_ref[d]
        osmall_ref[...] = g
        for k in range(n):
            mine(k).wait_send()
            leave(k, 0).wait()
            leave(k, 1).wait()

    shards = [jax.ShapeDtypeStruct(_BIG_SHARD[nm], F32) for nm in names]
    res = pl.pallas_call(
        body, in_specs=[ANY] * n + [VMEM_WHOLE], out_specs=[ANY] * n + [VMEM_WHOLE],
        out_shape=shards + [jax.ShapeDtypeStruct(small.shape, F32)],
        scratch_shapes=[pltpu.VMEM(q.shape, q.dtype) for q in parts] + [pltpu.VMEM(s.shape, s.dtype) for s in shards]
        + [pltpu.VMEM((N_DEV,) + small.shape, F32), pltpu.SemaphoreType.DMA((n,)), pltpu.SemaphoreType.DMA((n,)),
           pltpu.SemaphoreType.DMA((everyone.n_sems,)), pltpu.SemaphoreType.DMA((everyone.n_sems,)),
           pltpu.SemaphoreType.DMA((n,)), pltpu.SemaphoreType.DMA((2 * n,))],
        compiler_params=_params(), name="sum_swap",
    )(*parts, small)
    return res[:n], res[n]


def _adamw_math(w, g, m, v):
    m = ADAM_B1 * m + (1.0 - ADAM_B1) * g
    v = ADAM_B2 * v + (1.0 - ADAM_B2) * (g * g)
    m_hat = m / (1.0 - ADAM_B1 ** ADAM_STEP)
    v_hat = v / (1.0 - ADAM_B2 ** ADAM_STEP)
    delta = -ADAM_LR * (m_hat / (jnp.sqrt(v_hat) + ADAM_EPS) + ADAM_WD * w)
    return delta, m, v


ADAMW_STEPS = 8


def _adamw_big(gs, ws, ms, vs, name):
    n = len(ws)

    def body(*refs):
        for k in range(n):
            g_ref, w_ref, m_ref, v_ref = refs[4 * k:4 * k + 4]
            g_out_ref, d_ref, nm_ref, nv_ref = refs[4 * (n + k):4 * (n + k) + 4]
            g = g_ref[...].reshape(w_ref.shape)
            g_out_ref[...] = g
            d_ref[...], nm_ref[...], nv_ref[...] = _adamw_math(w_ref[...], g, m_ref[...], v_ref[...])

    in_specs, out_specs, shapes, args = [], [], [], []
    for g, w, m, v in zip(gs, ws, ms, vs):
        tr = -(-w.shape[0] // (8 * ADAMW_STEPS)) * 8
        zeros = (0,) * (w.ndim - 1)
        blk = pl.BlockSpec((tr,) + w.shape[1:], lambda i, zeros=zeros: (i,) + zeros)
        in_specs += [pl.BlockSpec((tr,) + g.shape[1:], lambda i: (i, 0)), blk, blk, blk]
        out_specs += [blk] * 4
        shapes += [jax.ShapeDtypeStruct(w.shape, F32)] * 4
        args += [g, w, m, v]
    res = pl.pallas_call(
        body, grid=(ADAMW_STEPS,), in_specs=in_specs, out_specs=out_specs, out_shape=shapes,
        compiler_params=_params(("arbitrary",)), name=name,
    )(*args)
    return [tuple(res[4 * k:4 * k + 4]) for k in range(n)]


def _adamw_small(ws, gs, ms, vs):
    n = len(ws)

    def body(*refs):
        w_refs, g_refs, m_refs, v_refs, d_refs, nm_refs, nv_refs = _split(refs, *([n] * 7))
        for k in range(n):
            d_refs[k][...], nm_refs[k][...], nv_refs[k][...] = _adamw_math(w_refs[k][...], g_refs[k][...], m_refs[k][...],
                                                                             v_refs[k][...])

    shapes = [jax.ShapeDtypeStruct(w.shape, F32) for w in ws]
    res = pl.pallas_call(body, out_shape=shapes * 3, name="adamw_small")(*ws, *gs, *ms, *vs)
    return res[:n], res[n:2 * n], res[2 * n:]


def _pack(arrs):
    flat = jnp.concatenate([a.reshape(-1) for a in arrs])
    rows = -(-flat.shape[0] // 1024) * 8
    return jnp.pad(flat, (0, rows * 128 - flat.shape[0])).reshape(rows, 128)


def _unpack(buf, shapes):
    flat = buf.reshape(-1)
    out, off = [], 0
    for s in shapes:
        size = 1
        for d in s:
            size *= d
        out.append(flat[off:off + size].reshape(s))
        off += size
    return out


def _block_rows(w):
    return jnp.pad(w.reshape(512, 4), ((0, 0), (0, 124)))


def _block_stored(dw):
    return jnp.transpose(dw[:, 0:4].reshape(128, 4, 4), (1, 2, 0)).reshape(16, 128)


def _cols(a4):
    return jnp.transpose(a4, (1, 0, 2)).reshape(a4.shape[1], -1)


_LATE = ("w_pa", "w_pb", "w_o", "w_up", "w_down")
_RIDE_IN_PROJ = ("w_pa", "w_pb", "w_o", "w_down_b")
_RIDE_MIXER = ("w_up", "w_down_a")


def _full_weights(gathered):
    joined = {"w_o": (D_MODEL, D_MODEL)}
    return {n: (a.reshape(joined[n]) if n in joined else a) for n, a in gathered.items()}


def _local_step(x, target, w, sp, late_shards=None):
    sp = {n: (a.reshape(1, -1) if a.ndim == 1 else a) for n, a in sp.items()}
    wau = jnp.pad(sp["w_a_up"], ((0, 112), (0, 0)))
    wif = jnp.pad(sp["w_if"], ((0, 0), (0, 120)))
    bif = jnp.pad(sp["b_if"], ((0, 0), (0, 120)))
    p = {"wau": wau, "bau": sp["b_a_up"], "ggla": sp["g_gla_norm"], "cw": sp["conv_w"], "cb": sp["conv_b"],
         "wq": _block_rows(sp["w_q_ml"]), "wk": _block_rows(sp["w_k_ml"]), "wv": _block_rows(sp["w_v_ml"]),
         "wif": wif, "bif": bif, "skip": sp["ml_skip"], "gml": sp["g_ml_norm"]}

    if late_shards is None:
        (pm, gab, h), _ = _in_proj(x, sp["g_pre_mix"], w["w_in"])
        ab, x1, mix, merged, *states = _mixer_fwd(pm, p, gab, x, w["w_pa"], w["w_pb"], w["w_o"], sp["g_post_mix"])
    else:
        shard = dict(zip(_LATE, late_shards))
        shard["w_down_a"] = shard["w_down_b"] = shard["w_down"]
        (pm, gab, h), got = _in_proj(x, sp["g_pre_mix"], w["w_in"], _Gather(_RIDE_IN_PROJ, middle_at=0.7),
                                     [shard[n] for n in _RIDE_IN_PROJ])
        w = dict(w, **_full_weights(dict(zip(_RIDE_IN_PROJ, got))))
        ab, x1, mix, merged, *rest = _mixer_fwd(pm, p, gab, x, w["w_pa"], w["w_pb"], w["w_o"], sp["g_post_mix"],
                                                _Gather(_RIDE_MIXER, middle_at=0.62), [shard[n] for n in _RIDE_MIXER])
        states = rest[:4]
        w.update(_full_weights(dict(zip(_RIDE_MIXER, rest[4:]))))
    dx1, u, dd, h2, dpre, dg_post_mlp, dg_pre_mlp, loss = _mlp(x1, target, sp["g_pre_mlp"], sp["g_post_mlp"],
                                                                w["w_up"], w["w_down_a"], w["w_down_b"])
    dgab, dab, dg_post_mix, dw_pa, dw_pb, dw_o = _merge_bwd(dx1, mix, ab, gab, merged, w["w_pa"], w["w_pb"], w["w_o"],
                                                            sp["g_post_mix"])
    big = {"w_pa": dw_pa, "w_pb": dw_pb, "w_o": dw_o, "w_up": _tn_matmul(h2, dpre, "dw_up", shards=N_CHIP)}
    if late_shards is None:
        big["w_down"] = _tn_matmul(u, dd, "dw_down")
        dpm, dp, _ = _mixer_bwd(pm, dab, states, p)
    else:
        pieces = lambda n: big[n].reshape((N_CHIP,) + _BIG_SHARD[n])
        big["w_down"], partial = _tn_matmul(u, dd, "dw_down", rider=_Presum(_LATE[:4]),
                                            rider_ins=[pieces(n) for n in _LATE[:4]])
        dpm, dp, parts = _mixer_bwd(pm, dab, states, p, _PresumThenSend(_LATE), list(partial) + [pieces("w_down")])
        big = dict(zip(_LATE, parts))
    big["w_in"] = _dw_in(dpm, dgab, h)
    if late_shards is None:
        (dx, dg_pre_mix), _ = _in_proj_bwd(dpm, dgab, x, dx1, sp["g_pre_mix"], w["w_in"])
    else:
        (dx, dg_pre_mix), parts = _in_proj_bwd(dpm, dgab, x, dx1, sp["g_pre_mix"], w["w_in"], _PresumThenRelay(("w_in",)),
                                               [big["w_in"]])
        big["w_in"] = parts[0]
    small = {
        "g_pre_mix": dg_pre_mix, "b_a_up": dp["bau"], "g_gla_norm": dp["ggla"], "conv_b": dp["cb"],
        "w_q_ml": _block_stored(dp["wq"]), "w_k_ml": _block_stored(dp["wk"]), "w_v_ml": _block_stored(dp["wv"]),
        "w_if": dp["wif"][:, 0:8].T,
        "b_if": dp["bif"][:, 0:8], "ml_skip": dp["skip"], "g_ml_norm": dp["gml"], "g_post_mix": dg_post_mix,
        "g_pre_mlp": dg_pre_mlp, "g_post_mlp": dg_post_mlp, "w_a_up": dp["wau"][0:16], "conv_w": dp["cw"],
        "loss": loss[:, 0:1],
    }
    return dx, big, small


_SMALL_REPL = ("g_pre_mix", "b_a_up", "g_gla_norm", "conv_b", "w_q_ml", "w_k_ml", "w_v_ml", "b_if", "ml_skip",
               "g_ml_norm", "g_post_mix", "g_pre_mlp", "g_post_mlp")
_SMALL_SHARDED = ("w_a_up", "conv_w", "w_if")
_SMALL_ORDER = _SMALL_REPL + _SMALL_SHARDED + ("loss",)
_WEIGHTS = ("g_pre_mix", "w_in", "w_a_up", "b_a_up", "g_gla_norm", "conv_w", "conv_b", "w_q_ml", "w_k_ml", "w_v_ml",
            "w_if", "b_if", "ml_skip", "g_ml_norm", "w_pa", "w_pb", "w_o", "g_post_mix", "g_pre_mlp", "w_up", "w_down",
            "g_post_mlp")


_BLOCK_WEIGHTS = ("w_q_ml", "w_k_ml", "w_v_ml")


def _stored(name, a):
    if name in _BLOCK_WEIGHTS:
        return jnp.transpose(a, (0, 2, 3, 1)).reshape(16, 128)
    if name == "w_if":
        return jnp.transpose(a, (0, 2, 1)).reshape(8, 384)
    return a


def _unstored(name, a):
    if name in _BLOCK_WEIGHTS:
        return jnp.transpose(a.reshape(1, 4, 4, 128), (0, 3, 1, 2))
    if name == "w_if":
        return jnp.transpose(a.reshape(1, 8, 384), (0, 2, 1))
    return a


def _as_shard(name, a):
    return jnp.transpose(a, (2, 0, 1)).reshape(IN_SHARD, D_MODEL // 128, 128) if name == "w_in" else a[0]


def _in_shard_bf16(w_in):
    return jnp.transpose(w_in.astype(BF16), (2, 0, 1)).reshape(IN_SHARD, D_MODEL)


def _from_shard(name, a):
    return jnp.transpose(a, (1, 2, 0)).reshape(1, D_MODEL, IN_SHARD) if name == "w_in" else a[None]


def kernel(x, g_pre_mix, w_in, w_a_up, b_a_up, g_gla_norm, conv_w, conv_b, w_q_ml, w_k_ml, w_v_ml, w_if, b_if, ml_skip, g_ml_norm, w_pa, w_pb, w_o, g_post_mix, g_pre_mlp, w_up, w_down, g_post_mlp, loss_target, m_g_pre_mix, m_w_in, m_w_a_up, m_b_a_up, m_g_gla_norm, m_conv_w, m_conv_b, m_w_q_ml, m_w_k_ml, m_w_v_ml, m_w_if, m_b_if, m_ml_skip, m_g_ml_norm, m_w_pa, m_w_pb, m_w_o, m_g_post_mix, m_g_pre_mlp, m_w_up, m_w_down, m_g_post_mlp, v_g_pre_mix, v_w_in, v_w_a_up, v_b_a_up, v_g_gla_norm, v_conv_w, v_conv_b, v_w_q_ml, v_w_k_ml, v_w_v_ml, v_w_if, v_b_if, v_ml_skip, v_g_ml_norm, v_w_pa, v_w_pb, v_w_o, v_g_post_mix, v_g_pre_mlp, v_w_up, v_w_down, v_g_post_mlp):
    args = dict(locals())
    wts = {n: _as_shard(n, args[n]) for n in _WEIGHTS}
    mom = {n: _as_shard(n, args["m_" + n]) for n in _WEIGHTS}
    var = {n: _as_shard(n, args["v_" + n]) for n in _WEIGHTS}
    chip = 2 * lax.axis_index("x") + lax.axis_index("y")

    first = ("w_in",) + _SMALL_SHARDED
    gathered = dict(zip(first, _run_alone(_Gather(("w_in",), [wts[n] for n in _SMALL_SHARDED]),
                                          [_in_shard_bf16(w_in)] + [wts[n] for n in _SMALL_SHARDED],
                                          "gather_first")))
    sp = {n: wts[n] for n in _SMALL_REPL}
    sp["w_a_up"] = _cols(gathered["w_a_up"])
    sp["conv_w"] = _cols(gathered["conv_w"])
    sp["w_if"] = gathered["w_if"].reshape(1536, 8)

    dx, big, small = _local_step(x[0], loss_target[0], _full_weights({"w_in": gathered["w_in"]}), sp,
                                 late_shards=[wts[n] for n in _LATE])

    small_shapes = [small[n].shape for n in _SMALL_ORDER]
    packed = _pack([small[n] for n in _SMALL_ORDER])
    sums, small_sum = _sum_swap(_BIG, [big[n] for n in _BIG], packed)

    grads, delta, new_m, new_v = {}, {}, {}, {}
    updated = dict(zip(_BIG, _adamw_big(sums, [wts[n] for n in _BIG], [mom[n] for n in _BIG], [var[n] for n in _BIG],
                                        "adamw_big")))
    for n in _BIG:
        grads[n], delta[n], new_m[n], new_v[n] = (_from_shard(n, a) for a in updated[n])
    summed = dict(zip(_SMALL_ORDER, _unpack(small_sum, small_shapes)))
    loss = summed["loss"].reshape(())
    summed["w_a_up"] = lax.dynamic_slice_in_dim(summed["w_a_up"], chip * 64, 64, axis=1)
    summed["conv_w"] = lax.dynamic_slice_in_dim(summed["conv_w"], chip * 128, 128, axis=1)
    summed["w_if"] = lax.dynamic_slice_in_dim(summed["w_if"], chip * 384, 384, axis=1)
    small_names = _SMALL_REPL + _SMALL_SHARDED
    came_stored = _BLOCK_WEIGHTS + ("w_if",)
    g_stored = [summed[n] if n in came_stored else _stored(n, summed[n].reshape(args[n].shape)) for n in small_names]
    upd = _adamw_small([_stored(n, args[n]) for n in small_names], g_stored,
                       [_stored(n, args["m_" + n]) for n in small_names], [_stored(n, args["v_" + n]) for n in small_names])
    for dst, arrs in zip((grads, delta, new_m, new_v), (g_stored,) + tuple(upd)):
        dst.update({n: _unstored(n, a) for n, a in zip(small_names, arrs)})

    outs = [loss, dx[None]]
    for group in (grads, delta, new_m, new_v):
        outs += [group[n] for n in _WEIGHTS]
    return tuple(outs)
```

```python
import functools

import jax
import jax.numpy as jnp
from jax import lax
from jax.experimental import pallas as pl
from jax.experimental.pallas import tpu as pltpu

F32 = jnp.float32
BF16 = jnp.bfloat16

SEQ = 2048
D_MODEL = 1024
CHUNK = 64
N_CHUNK = SEQ // CHUNK
HEADS = 4
GLA_DK = 64
GLA_DV = 128
ML_DH = 128
D_FF = 4096
EPS = 1e-6
N_CHIP = 4
N_DEV = 8
TOK_TILE = 256
N_TOK_TILE = SEQ // TOK_TILE
SWEEP = 2
assert CHUNK == 64
N_SWEEP = N_CHUNK // SWEEP

PM_W = 2688
PM_XM = 1536
PM_OP = 2048
PM_AL = 2560
GAB_W = 2048
D_IN = 4624
IN_SHARD = D_IN // N_CHIP
IN_ALOW = 1536
IN_XM = 1552
IN_GATES = 2576

ADAM_LR = 0.001
ADAM_B1 = 0.9
ADAM_B2 = 0.999
ADAM_EPS = 1e-08
ADAM_WD = 0.01
ADAM_STEP = 10

VMEM_LIMIT = 56 * 1024 * 1024


def _params(sem=None):
    return pltpu.CompilerParams(dimension_semantics=sem, vmem_limit_bytes=VMEM_LIMIT)


def _dot(a, b, ca, cb):
    return lax.dot_general(a.astype(BF16), b.astype(BF16), (((ca,), (cb,)), ((), ())), preferred_element_type=F32)


def _pmm_nn(a, b):
    return _dot(a, b, 1, 0)


def _pmm_nt(a, b):
    return _dot(a, b, 1, 1)


def _pmm_tn(a, b):
    return _dot(a, b, 0, 0)


def _pcmm(c, x):
    return lax.dot_general(c, x, (((1,), (0,)), ((), ())), precision=lax.Precision.HIGHEST, preferred_element_type=F32)


@jax.custom_vjp
def _mm_nn(a, b):
    return _dot(a, b, 1, 0)


@jax.custom_vjp
def _mm_nt(a, b):
    return _dot(a, b, 1, 1)


@jax.custom_vjp
def _mm_tn(a, b):
    return _dot(a, b, 0, 0)


_mm_nn.defvjp(lambda a, b: (_dot(a, b, 1, 0), (a, b)), lambda r, g: (_mm_nt(g, r[1]), _mm_tn(r[0], g)))
_mm_nt.defvjp(lambda a, b: (_dot(a, b, 1, 1), (a, b)), lambda r, g: (_mm_nn(g, r[1]), _mm_tn(g, r[0])))
_mm_tn.defvjp(lambda a, b: (_dot(a, b, 0, 0), (a, b)), lambda r, g: (_mm_nt(r[1], g), _mm_nn(r[0], g)))


@jax.custom_vjp
def _cmm(c, x):
    return _pcmm(c, x)


_cmm.defvjp(
    lambda c, x: (_pcmm(c, x), c),
    lambda c, g: (jnp.zeros_like(c), lax.dot_general(c, g, (((0,), (0,)), ((), ())), precision=lax.Precision.HIGHEST,
                                                      preferred_element_type=F32)),
)

_PLAIN_OPS = (_pmm_nn, _pmm_nt, _pmm_tn, _pcmm)
_VJP_OPS = (_mm_nn, _mm_nt, _mm_tn, _cmm)


def _sigmoid(x):
    return 0.5 * (jnp.tanh(0.5 * x) + 1.0)


def _log_sigmoid(x):
    return jnp.minimum(x, 0.0) - jnp.log(1.0 + jnp.exp(-jnp.abs(x)))


def _mean(x):
    return jnp.mean(x, axis=-1, keepdims=True)


def _nt(a, b):
    return lax.dot_general(a, b, (((1,), (1,)), ((), ())), preferred_element_type=F32)


def _tn(a, b):
    return lax.dot_general(a, b, (((0,), (0,)), ((), ())), preferred_element_type=F32)


def _mixer_chunk(ops, p, st, pm, xprev8):
    mm_nn, mm_nt, mm_tn, cmm = ops
    n_rows = pm.shape[0]
    n_ch = n_rows // CHUNK
    row = lax.broadcasted_iota(jnp.int32, (n_rows, n_rows), 0)
    col = lax.broadcasted_iota(jnp.int32, (n_rows, n_rows), 1)
    tri = jnp.logical_and((row >> 6) == (col >> 6), row >= col).astype(F32)
    causal = tri[0:CHUNK, 0:CHUNK] > 0.0
    q = pm[:, 0:256]
    k = pm[:, 256:512]
    v = pm[:, 512:1024]
    g = pm[:, 1024:1536]
    xm = pm[:, PM_XM:PM_XM + 512]
    opre = pm[:, PM_OP:PM_OP + 512]
    alow = pm[:, PM_AL:PM_AL + 128]
    hs = range(HEADS)
    cs = range(n_ch)
    pairs = [(i, h) for i in cs for h in hs]
    rs = [slice(i * CHUNK, (i + 1) * CHUNK) for i in cs]
    last = [slice((i + 1) * CHUNK - 1, (i + 1) * CHUNK) for i in cs]
    s6 = [slice(h * GLA_DK, (h + 1) * GLA_DK) for h in hs]
    s12 = [slice(h * 128, (h + 1) * 128) for h in hs]

    xx = jnp.concatenate([xprev8, xm], axis=0)
    pre = p["cb"]
    for j in range(4):
        pre = pre + p["cw"][j:j + 1, :] * xx[5 + j:5 + j + n_rows, :]
    xc = pre * _sigmoid(pre)
    qm = [mm_nn(xc[:, s12[h]], p["wq"][h]) for h in hs]
    km = [mm_nn(xc[:, s12[h]], p["wk"][h]) for h in hs]
    vm = [mm_nn(xm[:, s12[h]], p["wv"][h]) for h in hs]
    qcat = jnp.concatenate(qm, axis=1)
    kcat = jnp.concatenate(km, axis=1)
    vcat = jnp.concatenate(vm, axis=1)
    gates = (mm_nn(qcat, p["wif"][0:512]) + mm_nn(kcat, p["wif"][512:1024]) + mm_nn(vcat, p["wif"][1024:1536])
             + p["bif"])
    lf = _log_sigmoid(gates)
    fc = cmm(tri, lf)
    gates_t = gates.T
    fc_t = fc.T

    la = _log_sigmoid(mm_nn(alow, p["wau"]) + p["bau"]) * (1.0 / 16.0)
    cum = cmm(tri, la)
    cum_last = [cum[last[i], :] for i in cs]
    to_end = jnp.concatenate([cum_last[i] - cum[rs[i], :] for i in cs], axis=0)
    e_pos = jnp.exp(cum)
    e_neg = jnp.exp(-cum)
    qs = q * (GLA_DK ** -0.5)
    qp = qs * e_pos
    qn = qs * e_neg
    kp = k * e_pos
    kn = k * e_neg
    kl = k * jnp.exp(to_end)
    dec = [jnp.exp(cum_last[i]) for i in cs]
    ks = [km[h] * (ML_DH ** -0.5) for h in hs]
    li_c = {(i, h): gates[rs[i], h:h + 1] for i, h in pairs}
    fc_c = {(i, h): fc[rs[i], 4 + h:5 + h] for i, h in pairs}
    f_last = {(i, h): fc[last[i], 4 + h:5 + h] for i, h in pairs}

    a_fwd = {(i, h): mm_nt(qp[rs[i], s6[h]], kn[rs[i], s6[h]]) for i, h in pairs}
    a_bwd = {(i, h): mm_nt(qn[rs[i], s6[h]], kp[rs[i], s6[h]]) for i, h in pairs}
    s_chunk = {(i, h): mm_tn(v[rs[i], s12[h]], kl[rs[i], s6[h]]) for i, h in pairs}
    qk = {(i, h): mm_nt(qm[h][rs[i]], ks[h][rs[i]]) for i, h in pairs}
    a = {ih: f_last[ih] - fc_c[ih] + li_c[ih] for ih in pairs}
    m_loc = {ih: jnp.max(a[ih], axis=0, keepdims=True) for ih in pairs}
    kw = {(i, h): ks[h][rs[i]] * jnp.exp(a[(i, h)] - m_loc[(i, h)]) for i, h in pairs}
    c_chunk = {(i, h): mm_tn(kw[(i, h)], vm[h][rs[i]]) for i, h in pairs}
    mem = {(0, h): st["S"][h] for h in hs}
    c_in = {(0, h): st["C"][h] for h in hs}
    n_in = {(0, h): st["n"][h] for h in hs}
    m_in = {(0, h): st["m"][h][:, 0:1] for h in hs}
    for i, h in pairs:
        mem[(i + 1, h)] = mem[(i, h)] * dec[i][:, s6[h]] + s_chunk[(i, h)]
        m_nx = jnp.maximum(f_last[(i, h)] + m_in[(i, h)], m_loc[(i, h)])
        sp = jnp.exp(f_last[(i, h)] + m_in[(i, h)] - m_nx)
        sl = jnp.exp(m_loc[(i, h)] - m_nx)
        c_in[(i + 1, h)] = sp * c_in[(i, h)] + sl * c_chunk[(i, h)]
        n_in[(i + 1, h)] = sp * n_in[(i, h)] + sl * jnp.sum(kw[(i, h)], axis=0, keepdims=True)
        m_in[(i + 1, h)] = m_nx
    s_new = [mem[(n_ch, h)] for h in hs]
    o_inter = {(i, h): mm_nt(qp[rs[i], s6[h]], mem[(i, h)]) for i, h in pairs}
    q_c = {(i, h): mm_nn(qm[h][rs[i]], c_in[(i, h)]) for i, h in pairs}
    scores = {ih: jnp.where(causal, a_fwd[ih], a_bwd[ih]) for ih in pairs}
    log_d = {(i, h): gates_t[h:h + 1, rs[i]] - jnp.abs(fc_c[(i, h)] - fc_t[4 + h:5 + h, rs[i]]) for i, h in pairs}
    g_int = {ih: fc_c[ih] + m_in[ih] for ih in pairs}
    m_t = {ih: jnp.maximum(g_int[ih], jnp.max(log_d[ih], axis=1, keepdims=True)) for ih in pairs}
    s = {ih: qk[ih] * jnp.exp(log_d[ih] - m_t[ih]) for ih in pairs}
    scl = {ih: jnp.exp(g_int[ih] - m_t[ih]) for ih in pairs}
    o = {(i, h): mm_nn(scores[(i, h)], v[rs[i], s12[h]]) + o_inter[(i, h)] for i, h in pairs}
    num = {(i, h): mm_nn(s[(i, h)], vm[h][rs[i]]) + scl[(i, h)] * q_c[(i, h)] for i, h in pairs}
    o = {ih: o[ih] * lax.rsqrt(_mean(o[ih] * o[ih]) + EPS) * p["ggla"] for ih in pairs}
    gate = g * _sigmoid(g)
    out_a = {(i, h): o[(i, h)] * gate[rs[i], s12[h]] for i, h in pairs}
    den = {(i, h): jnp.sum(s[(i, h)], axis=1, keepdims=True)
           + scl[(i, h)] * jnp.sum(qm[h][rs[i]] * n_in[(i, h)], axis=1, keepdims=True) for i, h in pairs}
    den = {ih: jnp.maximum(jnp.abs(den[ih]), jnp.exp(-m_t[ih])) for ih in pairs}
    open_gate = _sigmoid(opre)
    hc = {(i, h): num[(i, h)] / den[(i, h)] * open_gate[rs[i], s12[h]] for i, h in pairs}
    d0 = {ih: hc[ih] - _mean(hc[ih]) for ih in pairs}
    y = {ih: d0[ih] * lax.rsqrt(_mean(d0[ih] * d0[ih]) + EPS) for ih in pairs}
    skipped = p["skip"] * xc
    out_b = {(i, h): y[(i, h)] * p["gml"][:, s12[h]] + skipped[rs[i], s12[h]] for i, h in pairs}
    ab = jnp.concatenate([jnp.concatenate([out_a[(i, h)] for h in hs] + [out_b[(i, h)] for h in hs], axis=1) for i in cs],
                         axis=0)
    new = {"S": s_new, "C": [c_in[(n_ch, h)] for h in hs], "n": [n_in[(n_ch, h)] for h in hs],
           "m": [jnp.broadcast_to(m_in[(n_ch, h)], (1, ML_DH)) for h in hs]}
    return ab, new


_P_NAMES = ("wau", "bau", "ggla", "cw", "cb", "wq", "wk", "wv", "wif", "bif", "skip", "gml")
_P_SHAPES = {
    "wau": (128, 256), "bau": (1, 256), "ggla": (1, 128), "cw": (4, 512), "cb": (1, 512),
    "wq": (512, 128), "wk": (512, 128), "wv": (512, 128),
    "wif": (1536, 128), "bif": (1, 128), "skip": (1, 512), "gml": (1, 512),
}
_P_BLOCKDIAG = ("wq", "wk", "wv")
_S_NAMES = ("S", "C", "n", "m")
_S_SHAPES = {"S": (HEADS, GLA_DV, GLA_DK), "C": (HEADS, ML_DH, ML_DH), "n": (HEADS, 1, ML_DH), "m": (HEADS, 1, ML_DH)}


def _per_head(ref):
    return [ref[h] for h in range(HEADS)]


def _block_mask():
    r = lax.broadcasted_iota(jnp.int32, (128, 128), 0)
    c = lax.broadcasted_iota(jnp.int32, (128, 128), 1)
    same_block = (r >> 2) == (c >> 2)
    spread = jnp.logical_and(r < 4, (c & 3) == r)
    return same_block.astype(F32), spread.astype(F32)


def _expand_blockdiag(w_ref, dense_ref):
    same_block, spread = _block_mask()
    for h in range(HEADS):
        tiled = _pmm_nn(w_ref[h * 128:(h + 1) * 128, :], spread)
        dense_ref[h] = tiled * same_block


def _collect_blockdiag(ddense_ref, dw_ref):
    same_block, spread = _block_mask()
    for h in range(HEADS):
        dw_ref[h * 128:(h + 1) * 128, :] = lax.dot_general(
            ddense_ref[h] * same_block, spread, (((1,), (1,)), ((), ())), precision=lax.Precision.HIGHEST,
            preferred_element_type=F32)


def _const_spec(shape):
    zeros = (0,) * len(shape)
    return pl.BlockSpec(shape, lambda i: zeros)


def _split(refs, *counts):
    out, at = [], 0
    for c in counts:
        out.append(refs[at:at + c])
        at += c
    assert at == len(refs)
    return out


def _landed(lands, outs, flush_sems):
    return [pltpu.make_async_copy(lands[k], outs[k], flush_sems.at[k]) for k in range(len(outs))]


def _ride_done(rider, cond, outs, sems):
    if rider is None:
        return

    @pl.when(cond)
    def _():
        if hasattr(rider, "flush"):
            rider.flush_done(sems[:-3], outs, sems[-1])
        else:
            for cp in _landed(sems[:-3], outs, sems[-1]):
                cp.wait()


def _ride(rider, phases, cond, ins, outs, sems, flush_later=False):
    if rider is None or not any(hasattr(rider, phase) for phase in phases):
        return
    lands, (send_sems, recv_sems, flush_sems) = sems[:-3], sems[-3:]

    @pl.when(cond)
    def _():
        for phase in phases:
            if phase == "last" and hasattr(rider, "late"):
                rider.late(ins, lands, send_sems, recv_sems)
                rider.flush("late", lands, outs, flush_sems)
            getattr(rider, phase)(ins, lands, send_sems, recv_sems)
            if hasattr(rider, "flush"):
                rider.flush(phase, lands, outs, flush_sems, wait=not flush_later)
        if "last" in phases and not hasattr(rider, "flush"):
            flush = _landed(lands, outs, flush_sems)
            for cp in flush:
                cp.start()
            if not flush_later:
                for cp in flush:
                    cp.wait()


def _middle_step(rider, n_steps):
    return min(n_steps - 2, int(getattr(rider, "middle_at", 1.0) * n_steps))


def _rider_specs(rider, rider_ins):
    if rider is None:
        return [], [], [], []
    scratch = [pltpu.VMEM(s.shape, s.dtype) for s in list(rider.out_shape) + list(getattr(rider, "work_shape", ()))]
    scratch += [pltpu.SemaphoreType.DMA((rider.n_sems,)), pltpu.SemaphoreType.DMA((rider.n_sems,)),
                pltpu.SemaphoreType.DMA((getattr(rider, "n_flush", len(rider.out_shape)),))]
    in_space = getattr(rider, "in_space", VMEM_WHOLE)
    in_specs = list(in_space) if isinstance(in_space, (list, tuple)) else [in_space] * len(rider_ins)
    return in_specs, [ANY] * len(rider.out_shape), list(rider.out_shape), scratch


def _merge_tile(ab, gab_ref, x_ref, wpa_ref, wpb_ref, wo_ref, g_ref, x1_ref, mix_ref, mg_ref):
    a = ab[:, 0:512]
    b = ab[:, 512:1024]
    for j in range(N_CHIP):
        blk = slice(j * 256, (j + 1) * 256)
        ya = jnp.dot(a, wpa_ref[j], preferred_element_type=F32)
        yb = jnp.dot(b, wpb_ref[j], preferred_element_type=F32)
        sa = _sigmoid(gab_ref[:, j * 256:(j + 1) * 256])
        sb = _sigmoid(gab_ref[:, 1024 + j * 256:1024 + (j + 1) * 256])
        mg_ref[:, blk] = (sa * ya + sb * yb).astype(BF16)
    mix = jnp.dot(mg_ref[...], wo_ref[...], preferred_element_type=F32)
    mix_ref[...] = mix
    mn, _ = _rms_fwd(mix)
    x1_ref[...] = x_ref[...] + mn * g_ref[...]


def _mixer_fwd(pm, p, gab, x, w_pa4, w_pb4, w_o, g_post, rider=None, rider_ins=()):
    n_p = len(_P_NAMES)
    r_in, r_out_specs, r_out_shape, r_sems = _rider_specs(rider, rider_ins)

    def body(*refs):
        ((pm_ref, xprev_ref), p_list, merge_in, ride_in, (ab_ref,), merge_out, so_refs, ride_out, sc_refs, dense_list,
         sems) = _split(refs, 2, n_p, 6, len(r_in), 1, 3, 4, len(r_out_specs), 4, 3, len(r_sems))
        p_refs = dict(zip(_P_NAMES, p_list))
        dense = dict(zip(_P_BLOCKDIAG, dense_list))
        n = pl.program_id(0)
        _ride(rider, ("first",), n == 0, ride_in, ride_out, sems)
        _ride(rider, ("last",), n == N_SWEEP - 1, ride_in, ride_out, sems, flush_later=True)

        @pl.when(n == 0)
        def _():
            for r in sc_refs:
                r[...] = jnp.zeros_like(r)
            for nm in _P_BLOCKDIAG:
                _expand_blockdiag(p_refs[nm], dense[nm])

        st = {name: _per_head(r) for name, r in zip(_S_NAMES, sc_refs)}
        pv = {nm: (_per_head(dense[nm]) if nm in _P_BLOCKDIAG else p_refs[nm][...]) for nm in _P_NAMES}
        for name, r in zip(_S_NAMES, so_refs):
            for h in range(HEADS):
                r[0, h] = st[name][h]
        xprev8 = jnp.where(n > 0, xprev_ref[CHUNK - 8:CHUNK, :], 0.0)
        ab, st = _mixer_chunk(_PLAIN_OPS, pv, st, pm_ref[...], xprev8)
        ab = ab.astype(BF16)
        ab_ref[...] = ab
        for name, r in zip(_S_NAMES, sc_refs):
            for h in range(HEADS):
                r[h] = st[name][h]
        _merge_tile(ab, *merge_in, *merge_out)
        _ride(rider, ("middle",), n == _middle_step(rider, N_SWEEP), ride_in, ride_out, sems)
        _ride_done(rider, n == N_SWEEP - 1, ride_out, sems)

    rows = lambda width: pl.BlockSpec((SWEEP * CHUNK, width), lambda i: (i, 0))
    in_specs = [rows(PM_W), pl.BlockSpec((CHUNK, 512), lambda i: (jnp.maximum(SWEEP * i - 1, 0), PM_XM // 512))]
    in_specs += [_const_spec(_P_SHAPES[nm]) for nm in _P_NAMES]
    in_specs += [rows(GAB_W), rows(D_MODEL), _once((N_CHIP, 512, 256)), _once((N_CHIP, 512, 256)), _once((D_MODEL, D_MODEL)),
                 _once((1, D_MODEL))] + r_in
    out_specs = [rows(1024), rows(D_MODEL), rows(D_MODEL), rows(D_MODEL)]
    out_shape = [jax.ShapeDtypeStruct((SEQ, 1024), BF16), jax.ShapeDtypeStruct((SEQ, D_MODEL), F32),
                 jax.ShapeDtypeStruct((SEQ, D_MODEL), F32), jax.ShapeDtypeStruct((SEQ, D_MODEL), BF16)]
    for nm in _S_NAMES:
        shp = _S_SHAPES[nm]
        out_specs.append(pl.BlockSpec((1,) + shp, lambda i: (i, 0, 0, 0)))
        out_shape.append(jax.ShapeDtypeStruct((N_SWEEP,) + shp, F32))
    return pl.pallas_call(
        body, grid=(N_SWEEP,), in_specs=in_specs, out_specs=out_specs + r_out_specs, out_shape=out_shape + r_out_shape,
        scratch_shapes=[pltpu.VMEM(_S_SHAPES[nm], F32) for nm in _S_NAMES]
        + [pltpu.VMEM((HEADS, 128, 128), F32) for _ in _P_BLOCKDIAG] + r_sems,
        compiler_params=_params(("arbitrary",)), name="mixer_fwd",
    )(pm, pm, *[p[nm] for nm in _P_NAMES], gab, x, w_pa4, w_pb4, w_o, g_post, *rider_ins)


def _mixer_bwd(pm, dab, states, p, rider=None, rider_ins=()):
    n_p = len(_P_NAMES)
    r_in, r_out_specs, r_out_shape, r_sems = _rider_specs(rider, rider_ins)

    def body(*refs):
        ((pm_ref, xprev_ref, dab_ref), si_refs, p_list, ride_in, (dpm_ref,), dp_list, ride_out, ds_refs, (carry_ref,),
         dense_list, ddense_list, sems) = _split(refs, 3, 4, n_p, len(r_in), 1, n_p, len(r_out_specs), 4, 1, 3, 3, len(r_sems))
        p_refs = dict(zip(_P_NAMES, p_list))
        dp_refs = dict(zip(_P_NAMES, dp_list))
        dense = dict(zip(_P_BLOCKDIAG, dense_list))
        ddense = dict(zip(_P_BLOCKDIAG, ddense_list))
        i = pl.program_id(0)
        blk = N_SWEEP - 1 - i
        _ride(rider, ("first",), i == 0, ride_in, ride_out, sems)
        _ride(rider, ("last",), i == N_SWEEP - 1, ride_in, ride_out, sems, flush_later=True)

        @pl.when(i == 0)
        def _():
            for r in ds_refs:
                r[...] = jnp.zeros_like(r)
            for nm in _P_NAMES:
                if nm in _P_BLOCKDIAG:
                    ddense[nm][...] = jnp.zeros_like(ddense[nm])
                    _expand_blockdiag(p_refs[nm], dense[nm])
                else:
                    dp_refs[nm][...] = jnp.zeros_like(dp_refs[nm])
            carry_ref[...] = jnp.zeros_like(carry_ref)

        pv = {nm: (_per_head(dense[nm]) if nm in _P_BLOCKDIAG else p_refs[nm][...]) for nm in _P_NAMES}
        dst = {name: _per_head(r) for name, r in zip(_S_NAMES, ds_refs)}
        st = {name: [r[0, h] for h in range(HEADS)] for name, r in zip(_S_NAMES, si_refs)}
        xprev8 = jnp.where(blk > 0, xprev_ref[CHUNK - 8:CHUNK, :], 0.0)
        _, vjp = jax.vjp(functools.partial(_mixer_chunk, _VJP_OPS), pv, st, pm_ref[...], xprev8)
        dp_sum, dst, dpm, dxprev8 = vjp((dab_ref[...], dst))
        reach = jnp.concatenate([jnp.zeros((SWEEP * CHUNK - 8, 512), F32), carry_ref[...]], axis=0)
        dpm_ref[:, 0:PM_XM] = dpm[:, 0:PM_XM].astype(BF16)
        dpm_ref[:, PM_XM:PM_XM + 512] = (dpm[:, PM_XM:PM_XM + 512] + reach).astype(BF16)
        dpm_ref[:, PM_XM + 512:PM_W] = dpm[:, PM_XM + 512:PM_W].astype(BF16)
        carry_ref[...] = dxprev8
        for name, r in zip(_S_NAMES, ds_refs):
            for h in range(HEADS):
                r[h] = dst[name][h]
        for nm in _P_NAMES:
            if nm in _P_BLOCKDIAG:
                for h in range(HEADS):
                    ddense[nm][h] += dp_sum[nm][h]
            else:
                dp_refs[nm][...] += dp_sum[nm]

        @pl.when(i == N_SWEEP - 1)
        def _():
            for nm in _P_BLOCKDIAG:
                _collect_blockdiag(ddense[nm], dp_refs[nm])

        _ride(rider, ("early",), i == 1, ride_in, ride_out, sems)
        _ride(rider, ("middle",), i == _middle_step(rider, N_SWEEP), ride_in, ride_out, sems)
        _ride_done(rider, i == N_SWEEP - 1, ride_out, sems)

    rev = lambda i: (N_SWEEP - 1 - i, 0)
    in_specs = [pl.BlockSpec((SWEEP * CHUNK, PM_W), rev),
                pl.BlockSpec((CHUNK, 512), lambda i: (jnp.maximum(SWEEP * (N_SWEEP - 1 - i) - 1, 0), PM_XM // 512)),
                pl.BlockSpec((SWEEP * CHUNK, 1024), rev)]
    for nm in _S_NAMES:
        in_specs.append(pl.BlockSpec((1,) + _S_SHAPES[nm], lambda i: (N_SWEEP - 1 - i, 0, 0, 0)))
    in_specs += [_const_spec(_P_SHAPES[nm]) for nm in _P_NAMES] + r_in
    out_specs = [pl.BlockSpec((SWEEP * CHUNK, PM_W), rev)] + [_const_spec(_P_SHAPES[nm]) for nm in _P_NAMES]
    out_shape = [jax.ShapeDtypeStruct((SEQ, PM_W), BF16)] + [jax.ShapeDtypeStruct(_P_SHAPES[nm], F32) for nm in _P_NAMES]
    res = pl.pallas_call(
        body, grid=(N_SWEEP,), in_specs=in_specs, out_specs=out_specs + r_out_specs, out_shape=out_shape + r_out_shape,
        scratch_shapes=[pltpu.VMEM(_S_SHAPES[nm], F32) for nm in _S_NAMES] + [pltpu.VMEM((8, 512), F32)]
        + [pltpu.VMEM((HEADS, 128, 128), F32) for _ in range(2 * len(_P_BLOCKDIAG))] + r_sems,
        compiler_params=_params(("arbitrary",)), name="mixer_bwd",
    )(pm, pm, dab, *states, *[p[nm] for nm in _P_NAMES], *rider_ins)
    return res[0], dict(zip(_P_NAMES, res[1:1 + n_p])), res[1 + n_p:]


def _tok(width):
    return pl.BlockSpec((TOK_TILE, width), lambda i: (i, 0))


def _once(shape):
    zeros = (0,) * len(shape)
    return pl.BlockSpec(shape, lambda i: zeros, pipeline_mode=pl.Buffered(1))


def _rms_fwd(x):
    r = lax.rsqrt(_mean(x * x) + EPS)
    return x * r, r


def _rms_bwd(dy, xn, r, g):
    gd = dy * g
    return r * (gd - xn * _mean(xn * gd))


def _tiled_call(body, in_specs, out_specs, out_shape, args, name, rider=None, rider_ins=(), scratch=()):
    r_in, r_out_specs, r_out_shape, r_scratch = _rider_specs(rider, rider_ins)
    n_in, n_out = len(in_specs), len(out_specs)

    def hosted(*refs):
        ins, ride_in, outs, ride_out, own, r_scr = _split(refs, n_in, len(r_in), n_out, len(r_out_specs), len(scratch),
                                                          len(r_scratch))
        i = pl.program_id(0)
        _ride(rider, ("first",), i == 0, ride_in, ride_out, r_scr)
        body(*ins, *outs, *own)
        _ride(rider, ("early",), i == 1, ride_in, ride_out, r_scr)
        _ride(rider, ("middle",), i == _middle_step(rider, N_TOK_TILE), ride_in, ride_out, r_scr)
        _ride(rider, ("last",), i == N_TOK_TILE - 1, ride_in, ride_out, r_scr)

    res = pl.pallas_call(
        hosted, grid=(N_TOK_TILE,), in_specs=list(in_specs) + r_in, out_specs=list(out_specs) + r_out_specs,
        out_shape=list(out_shape) + r_out_shape, scratch_shapes=list(scratch) + r_scratch,
        compiler_params=_params(("arbitrary",)), name=name,
    )(*args, *rider_ins)
    return res[:n_out], res[n_out:]


def _join_rows(w4_ref, wt_ref):
    @pl.when(pl.program_id(0) == 0)
    def _():
        for j in range(N_CHIP):
            wt_ref[j * IN_SHARD:(j + 1) * IN_SHARD, :] = w4_ref[j]


def _joined_scratch():
    return [pltpu.VMEM((D_IN, D_MODEL), BF16)]


def _in_proj(x, g_pre, w4_in, rider=None, rider_ins=()):
    def body(x_ref, g_ref, w4_ref, pm_ref, gab_ref, h_ref, wt_ref):
        _join_rows(w4_ref, wt_ref)
        xn, _ = _rms_fwd(x_ref[...])
        h = (xn * g_ref[...]).astype(BF16)
        h_ref[...] = h
        pm_ref[:, 0:PM_XM] = _nt(h, wt_ref[0:IN_ALOW, :])
        pm_ref[:, PM_XM:PM_AL] = _nt(h, wt_ref[IN_XM:IN_GATES, :])
        pm_ref[:, PM_AL:PM_W] = _nt(h, wt_ref[IN_ALOW:IN_ALOW + 128, :])
        gab_ref[...] = _nt(h, wt_ref[IN_GATES:D_IN, :])

    return _tiled_call(
        body, [_tok(D_MODEL), _once((1, D_MODEL)), _once((N_CHIP, IN_SHARD, D_MODEL))],
        [_tok(PM_W), _tok(GAB_W), _tok(D_MODEL)],
        [jax.ShapeDtypeStruct((SEQ, PM_W), F32), jax.ShapeDtypeStruct((SEQ, GAB_W), F32),
         jax.ShapeDtypeStruct((SEQ, D_MODEL), BF16)], (x, g_pre, w4_in), "in_proj", rider, rider_ins, _joined_scratch())


def _mlp(x1, target, g_pre, g_post, w_up4, w_down_a4, w_down_b4):
    def body(x1_ref, t_ref, gpre_ref, gpost_ref, wup_ref, wda_ref, wdb_ref,
             dx1_ref, u_ref, dd_ref, h2_ref, dpre_ref, dgpost_ref, dgpre_ref, loss_ref):
        @pl.when(pl.program_id(0) == 0)
        def _():
            dgpost_ref[...] = jnp.zeros_like(dgpost_ref)
            dgpre_ref[...] = jnp.zeros_like(dgpre_ref)
            loss_ref[...] = jnp.zeros_like(loss_ref)

        x1 = x1_ref[...]
        gpre = gpre_ref[...]
        gpost = gpost_ref[...]
        xn2, r2 = _rms_fwd(x1)
        h2 = (xn2 * gpre).astype(BF16)
        h2_ref[...] = h2
        rl = []
        d = jnp.zeros((TOK_TILE, D_MODEL), F32)
        for j in range(N_CHIP):
            blk = slice(j * 1024, (j + 1) * 1024)
            r = jnp.maximum(jnp.dot(h2, wup_ref[j], preferred_element_type=F32), 0.0)
            rl.append(r)
            u = (r * r).astype(BF16)
            u_ref[:, blk] = u
            d = d + jnp.dot(u[:, 0:512], wda_ref[j], preferred_element_type=F32)
            d = d + jnp.dot(u[:, 512:1024], wdb_ref[j], preferred_element_type=F32)
        dn, r3 = _rms_fwd(d)
        diff = x1 + dn * gpost - t_ref[...]
        loss_ref[...] += jnp.sum(diff * diff, keepdims=True) * (0.5 / D_MODEL)
        dy = diff * (1.0 / D_MODEL)
        dgpost_ref[...] += jnp.sum(dy * dn, axis=0, keepdims=True)
        dd = _rms_bwd(dy, dn, r3, gpost).astype(BF16)
        dd_ref[...] = dd
        dh2 = jnp.zeros((TOK_TILE, D_MODEL), F32)
        for j in range(N_CHIP):
            blk = slice(j * 1024, (j + 1) * 1024)
            du = jnp.concatenate([_nt(dd, wda_ref[j]), _nt(dd, wdb_ref[j])], axis=1)
            dpre = (du * (2.0 * rl[j])).astype(BF16)
            dpre_ref[:, blk] = dpre
            dh2 = dh2 + _nt(dpre, wup_ref[j])
        dgpre_ref[...] += jnp.sum(dh2 * xn2, axis=0, keepdims=True)
        dx1_ref[...] = dy + _rms_bwd(dh2, xn2, r2, gpre)

    acc = pl.BlockSpec((1, D_MODEL), lambda i: (0, 0))
    return pl.pallas_call(
        body, grid=(N_TOK_TILE,),
        in_specs=[_tok(D_MODEL), _tok(D_MODEL), _once((1, D_MODEL)), _once((1, D_MODEL)),
                  _once((N_CHIP, D_MODEL, 1024)), _once((N_CHIP, 512, D_MODEL)), _once((N_CHIP, 512, D_MODEL))],
        out_specs=[_tok(D_MODEL), _tok(D_FF), _tok(D_MODEL), _tok(D_MODEL), _tok(D_FF), acc, acc,
                   pl.BlockSpec((1, 128), lambda i: (0, 0))],
        out_shape=[jax.ShapeDtypeStruct((SEQ, D_MODEL), F32), jax.ShapeDtypeStruct((SEQ, D_FF), BF16),
                   jax.ShapeDtypeStruct((SEQ, D_MODEL), BF16), jax.ShapeDtypeStruct((SEQ, D_MODEL), BF16),
                   jax.ShapeDtypeStruct((SEQ, D_FF), BF16), jax.ShapeDtypeStruct((1, D_MODEL), F32),
                   jax.ShapeDtypeStruct((1, D_MODEL), F32), jax.ShapeDtypeStruct((1, 128), F32)],
        compiler_params=_params(("arbitrary",)), name="mlp_fwd_bwd",
    )(x1, target, g_pre, g_post, w_up4, w_down_a4, w_down_b4)


def _merge_bwd(dx1, mix, ab, gab, merged, w_pa4, w_pb4, w_o, g_post):
    def body(dx1_ref, mix_ref, ab_ref, gab_ref, mg_ref, wpa_ref, wpb_ref, wo_ref, g_ref,
             dgab_ref, dab_ref, dg_ref, dwpa_ref, dwpb_ref, dwo_ref, acc_pa, acc_pb, acc_o):
        @pl.when(pl.program_id(0) == 0)
        def _():
            dg_ref[...] = jnp.zeros_like(dg_ref)
            acc_pa[...] = jnp.zeros_like(acc_pa)
            acc_pb[...] = jnp.zeros_like(acc_pb)
            acc_o[...] = jnp.zeros_like(acc_o)

        dx1 = dx1_ref[...]
        mn, r = _rms_fwd(mix_ref[...])
        dg_ref[...] += jnp.sum(dx1 * mn, axis=0, keepdims=True)
        dmix = _rms_bwd(dx1, mn, r, g_ref[...]).astype(BF16)
        acc_o[...] += _tn(mg_ref[...], dmix)
        dmerged = _nt(dmix, wo_ref[...])
        a = ab_ref[:, 0:512]
        b = ab_ref[:, 512:1024]
        da = jnp.zeros((TOK_TILE, 512), F32)
        db = jnp.zeros((TOK_TILE, 512), F32)
        dyas, dybs = [], []
        for j in range(N_CHIP):
            blk = slice(j * 256, (j + 1) * 256)
            blk_b = slice(1024 + j * 256, 1024 + (j + 1) * 256)
            dm = dmerged[:, blk]
            ya = jnp.dot(a, wpa_ref[j], preferred_element_type=F32)
            yb = jnp.dot(b, wpb_ref[j], preferred_element_type=F32)
            sa = _sigmoid(gab_ref[:, blk])
            sb = _sigmoid(gab_ref[:, blk_b])
            dya = (dm * sa).astype(BF16)
            dyb = (dm * sb).astype(BF16)
            dyas.append(dya)
            dybs.append(dyb)
            dgab_ref[:, blk] = (dm * ya * sa * (1.0 - sa)).astype(BF16)
            dgab_ref[:, blk_b] = (dm * yb * sb * (1.0 - sb)).astype(BF16)
            da = da + _nt(dya, wpa_ref[j])
            db = db + _nt(dyb, wpb_ref[j])
        dab_ref[:, 0:512] = da
        dab_ref[:, 512:1024] = db
        acc_pa[...] += _tn(a, jnp.concatenate(dyas, axis=1))
        acc_pb[...] += _tn(b, jnp.concatenate(dybs, axis=1))

        @pl.when(pl.program_id(0) == N_TOK_TILE - 1)
        def _():
            dwo_ref[...] = acc_o[...].astype(BF16)
            for j in range(N_CHIP):
                dwpa_ref[j] = acc_pa[:, j * 256:(j + 1) * 256].astype(BF16)
                dwpb_ref[j] = acc_pb[:, j * 256:(j + 1) * 256].astype(BF16)

    whole = lambda shape: pl.BlockSpec(shape, lambda i: (0,) * len(shape))
    return pl.pallas_call(
        body, grid=(N_TOK_TILE,),
        in_specs=[_tok(D_MODEL), _tok(D_MODEL), _tok(1024), _tok(GAB_W), _tok(D_MODEL), _once((N_CHIP, 512, 256)),
                  _once((N_CHIP, 512, 256)), _once((D_MODEL, D_MODEL)), _once((1, D_MODEL))],
        out_specs=[_tok(GAB_W), _tok(1024), whole((1, D_MODEL)), whole((N_CHIP, 512, 256)), whole((N_CHIP, 512, 256)),
                   whole((D_MODEL, D_MODEL))],
        out_shape=[jax.ShapeDtypeStruct((SEQ, GAB_W), BF16), jax.ShapeDtypeStruct((SEQ, 1024), F32),
                   jax.ShapeDtypeStruct((1, D_MODEL), F32), jax.ShapeDtypeStruct((N_CHIP, 512, 256), BF16),
                   jax.ShapeDtypeStruct((N_CHIP, 512, 256), BF16), jax.ShapeDtypeStruct((D_MODEL, D_MODEL), BF16)],
        scratch_shapes=[pltpu.VMEM((512, D_MODEL), F32), pltpu.VMEM((512, D_MODEL), F32),
                        pltpu.VMEM((D_MODEL, D_MODEL), F32)],
        compiler_params=_params(("arbitrary",)), name="merge_bwd",
    )(dx1, mix, ab, gab, merged, w_pa4, w_pb4, w_o, g_post)


def _in_proj_bwd(dpm, dgab, x, dx1, g_pre, w4_in, rider=None, rider_ins=()):
    def body(dpm_ref, dgab_ref, x_ref, dx1_ref, g_ref, w4_ref, dx_ref, dg_ref, wt_ref):
        _join_rows(w4_ref, wt_ref)

        @pl.when(pl.program_id(0) == 0)
        def _():
            dg_ref[...] = jnp.zeros_like(dg_ref)

        dh = jnp.dot(dpm_ref[:, 0:PM_XM], wt_ref[0:IN_ALOW, :], preferred_element_type=F32)
        dh = dh + jnp.dot(dpm_ref[:, PM_XM:PM_AL], wt_ref[IN_XM:IN_GATES, :], preferred_element_type=F32)
        dh = dh + jnp.dot(dpm_ref[:, PM_AL:PM_W], wt_ref[IN_ALOW:IN_ALOW + 128, :], preferred_element_type=F32)
        dh = dh + jnp.dot(dgab_ref[...], wt_ref[IN_GATES:D_IN, :], preferred_element_type=F32)
        xn, r = _rms_fwd(x_ref[...])
        dg_ref[...] += jnp.sum(dh * xn, axis=0, keepdims=True)
        dx_ref[...] = dx1_ref[...] + _rms_bwd(dh, xn, r, g_ref[...])

    return _tiled_call(
        body, [_tok(PM_W), _tok(GAB_W), _tok(D_MODEL), _tok(D_MODEL), _once((1, D_MODEL)),
               _once((N_CHIP, IN_SHARD, D_MODEL))],
        [_tok(D_MODEL), pl.BlockSpec((1, D_MODEL), lambda i: (0, 0))],
        [jax.ShapeDtypeStruct((SEQ, D_MODEL), F32), jax.ShapeDtypeStruct((1, D_MODEL), F32)],
        (dpm, dgab, x, dx1, g_pre, w4_in), "in_proj_bwd", rider, rider_ins, _joined_scratch())


def _dw_in(dpm, dgab, h):
    n_pm = PM_AL // 512
    n_blk = n_pm + GAB_W // 512

    def place(o_ref, rows, lo, hi):
        for j in range(N_CHIP):
            a, b = max(lo, j * IN_SHARD), min(hi, (j + 1) * IN_SHARD)
            if a < b:
                o_ref[j, a - j * IN_SHARD:b - j * IN_SHARD, :] = rows(a - lo, b - lo)

    def body(dpm_ref, dgab_ref, dal_ref, h_ref, o_ref, blk_ref):
        i = pl.program_id(0)

        @pl.when(i < n_pm)
        def _():
            blk_ref[...] = _tn(dpm_ref[...], h_ref[...]).astype(BF16)

        @pl.when(i >= n_pm)
        def _():
            blk_ref[...] = _tn(dgab_ref[...], h_ref[...]).astype(BF16)

        for k in range(n_blk):
            off = k * 512 + (IN_XM - IN_ALOW) * (k >= IN_ALOW // 512)

            @pl.when(i == k)
            def _():
                place(o_ref, lambda a, b: blk_ref[a:b, :], off, off + 512)

        @pl.when(i == 0)
        def _():
            a_low = _tn(dal_ref[...], h_ref[...])[0:IN_XM - IN_ALOW].astype(BF16)
            place(o_ref, lambda a, b: a_low[a:b], IN_ALOW, IN_XM)

    return pl.pallas_call(
        body, grid=(n_blk,),
        in_specs=[pl.BlockSpec((SEQ, 512), lambda i: (0, jnp.minimum(i, n_pm - 1))),
                  pl.BlockSpec((SEQ, 512), lambda i: (0, jnp.maximum(i - n_pm, 0))),
                  pl.BlockSpec((SEQ, 128), lambda i: (0, PM_AL // 128)),
                  _once((SEQ, D_MODEL))],
        out_specs=pl.BlockSpec((N_CHIP, IN_SHARD, D_MODEL), lambda i: (0, 0, 0)),
        out_shape=jax.ShapeDtypeStruct((N_CHIP, IN_SHARD, D_MODEL), BF16),
        scratch_shapes=[pltpu.VMEM((512, D_MODEL), BF16)],
        compiler_params=_params(("arbitrary",)), name="dw_in",
    )(dpm, dgab, dpm, h)


def _tn_matmul(a, b, name, shards=1, tm=1024, rider=None, rider_ins=()):
    m, n = a.shape[1], b.shape[1]
    tm = min(tm, m)
    tn = n // shards if shards > 1 else min(n, 1024)
    steps_i, steps_j = m // tm, n // tn
    r_in, r_out_specs, r_out_shape, r_scratch = _rider_specs(rider, rider_ins)

    def body(*refs):
        (a_ref, b_ref), ride_in, (o_ref,), ride_out, scratch = _split(refs, 2, len(r_in), 1, len(r_out_specs), len(r_scratch))
        step = pl.program_id(0) * steps_j + pl.program_id(1)
        last = step == steps_i * steps_j - 1
        _ride(rider, ("first",), step == 0, ride_in, ride_out, scratch)
        _ride(rider, ("middle", "last"), last, ride_in, ride_out, scratch, flush_later=True)
        o_ref[...] = _tn(a_ref[...], b_ref[...]).astype(BF16)
        _ride_done(rider, last, ride_out, scratch)

    if shards > 1:
        out_spec = pl.BlockSpec((None, tm, tn), lambda i, j: (j, i, 0))
        out_shape = jax.ShapeDtypeStruct((shards, m, tn), BF16)
    else:
        out_spec = pl.BlockSpec((tm, tn), lambda i, j: (i, j))
        out_shape = jax.ShapeDtypeStruct((m, n), BF16)
    res = pl.pallas_call(
        body, grid=(steps_i, steps_j),
        in_specs=[pl.BlockSpec((SEQ, tm), lambda i, j: (0, i)), pl.BlockSpec((SEQ, tn), lambda i, j: (0, j))] + r_in,
        out_specs=[out_spec] + r_out_specs, out_shape=[out_shape] + r_out_shape, scratch_shapes=r_scratch,
        compiler_params=_params(("arbitrary", "arbitrary")), name=name,
    )(a, b, *rider_ins)
    return res[0] if rider is None else (res[0], res[1:])


MESH = pl.DeviceIdType.MESH
ANY = pl.BlockSpec(memory_space=pl.ANY)
VMEM_WHOLE = pl.BlockSpec(memory_space=pltpu.VMEM)

_BIG = ("w_in", "w_pa", "w_pb", "w_o", "w_up", "w_down")
_BIG_SHARD = {"w_in": (IN_SHARD, D_MODEL), "w_pa": (512, 256), "w_pb": (512, 256), "w_o": (256, D_MODEL),
              "w_up": (D_MODEL, 1024), "w_down": (1024, D_MODEL),
              "w_down_a": (512, D_MODEL), "w_down_b": (512, D_MODEL)}
_BIG_SPLIT = {"w_in": 1, "w_pa": 0, "w_pb": 0, "w_o": 0, "w_up": 0, "w_down": 0, "w_down_a": 0, "w_down_b": 0}


def _half(ref, e, name, lead=0, part=None):
    axis = _BIG_SPLIT[name]
    size = _BIG_SHARD[name][axis] // 2
    start = e * size
    if part is not None:
        size //= 2
        start = start + part * size
    start = pl.multiple_of(start, 128 if axis == 1 else 16)
    idx = [pl.ds(0, ref.shape[a]) for a in range(lead)]
    idx += [pl.ds(start, size), pl.ds(0, _BIG_SHARD[name][1])] if axis == 0 else [pl.ds(0, _BIG_SHARD[name][0]), pl.ds(start, size)]
    return ref.at[tuple(idx)]


def _half_shape(name):
    r, c = _BIG_SHARD[name]
    return (r // 2, c) if _BIG_SPLIT[name] == 0 else (r, c // 2)


def _remote(src, dst, send_sems, recv_sems, k, to):
    return pltpu.make_async_remote_copy(src_ref=src, dst_ref=dst, send_sem=send_sems.at[k], recv_sem=recv_sems.at[k],
                                        device_id=to, device_id_type=MESH)


def _mesh_place():
    x, y, c = lax.axis_index("x"), lax.axis_index("y"), lax.axis_index("c")
    return x, y, c, [(1 - x, y), (x, 1 - y), (1 - x, 1 - y)]


class _Gather:
    def __init__(self, names, small=(), middle_at=0.5):
        self.middle_at = middle_at
        self.names = tuple(names)
        self.nb = len(self.names)
        self.n = self.nb + len(small)
        self.n_sems = 8 * self.nb + 3 * len(small)
        self.n_flush = 6 * self.nb + len(small)
        self.out_shape = [jax.ShapeDtypeStruct((N_CHIP,) + _BIG_SHARD[nm], BF16) for nm in self.names]
        self.out_shape += [jax.ShapeDtypeStruct((N_CHIP,) + s.shape, s.dtype) for s in small]
        self.in_space = [self._in_spec(nm) for nm in self.names] + [VMEM_WHOLE] * len(small)

    @staticmethod
    def _in_spec(name):
        if name in ("w_down_a", "w_down_b"):
            half = 0 if name == "w_down_a" else 1
            return pl.BlockSpec(_BIG_SHARD[name], lambda *_: (half, 0), pipeline_mode=pl.Buffered(1))
        return VMEM_WHOLE

    def _copies(self, ins, outs, ss, rs, k):
        x, y, c, _ = _mesh_place()
        name = self.names[k]
        me, xn, yn, dg = 2 * x + y, 2 * (1 - x) + y, 2 * x + (1 - y), 2 * (1 - x) + (1 - y)
        to_x, to_y, sibling = (1 - x, y, c), (x, 1 - y, c), (x, y, 1 - c)

        def region(slot, e, part=None):
            return _half(outs[k].at[slot], e, name, part=part)

        def copy(pair, src, dst, to):
            return _remote(src, dst, ss, rs, 8 * k + pair, to)

        mine = region(me, c)
        sent = [copy(0, mine, mine, to_x), copy(1, mine, mine, to_y),
                copy(2, region(xn, c, 0), region(xn, c, 0), to_y), copy(3, region(yn, c, 1), region(yn, c, 1), to_x),
                copy(4, region(xn, c), region(xn, c), sibling), copy(5, region(yn, c), region(yn, c), sibling),
                copy(6, region(dg, c, 0), region(dg, c, 0), sibling), copy(7, region(dg, c, 1), region(dg, c, 1), sibling)]
        landing = [region(xn, c), region(yn, c), region(dg, c, 0), region(dg, c, 1),
                   region(xn, 1 - c), region(yn, 1 - c), region(dg, 1 - c, 0), region(dg, 1 - c, 1)]
        received = [copy(pair, dst, dst, sibling) for pair, dst in enumerate(landing)]
        return sent, received

    def _small(self, ins, outs, ss, rs, k, j, peer, slot, c):
        return _remote(ins[k], outs[k].at[slot], ss, rs, 8 * self.nb + 3 * (k - self.nb) + j, (*peer, c))

    def _leaving(self, lands, outs, fs):
        x, y, c, _ = _mesh_place()
        me, xn, yn, dg = 2 * x + y, 2 * (1 - x) + y, 2 * x + (1 - y), 2 * (1 - x) + (1 - y)

        def pieces(k):
            name = self.names[k]
            spots = [lambda r: r.at[me], lambda r: _half(r.at[xn], c, name), lambda r: _half(r.at[yn], c, name),
                     lambda r: _half(r.at[xn], 1 - c, name), lambda r: _half(r.at[yn], 1 - c, name), lambda r: r.at[dg]]
            return [pltpu.make_async_copy(spot(lands[k]), spot(outs[k]), fs.at[6 * k + t]) for t, spot in enumerate(spots)]

        small = [pltpu.make_async_copy(lands[k], outs[k], fs.at[6 * self.nb + k - self.nb]) for k in range(self.nb, self.n)]
        return [pieces(k) for k in range(self.nb)], small

    def flush(self, phase, lands, outs, fs, wait=True):
        big, small = self._leaving(lands, outs, fs)
        ready = {"first": (0,), "middle": (1, 2), "late": (3, 4), "last": (5,)}[phase]
        for cps in big:
            for t in ready:
                cps[t].start()
        if phase == "last":
            for cp in small:
                cp.start()
            if wait:
                self.flush_done(lands, outs, fs)

    def flush_done(self, lands, outs, fs):
        big, small = self._leaving(lands, outs, fs)
        for cp in [cp for cps in big for cp in cps] + small:
            cp.wait()

    def first(self, ins, outs, ss, rs):
        x, y, c, peers = _mesh_place()
        me = 2 * x + y
        for k in range(self.nb):
            outs[k][me] = ins[k][...].astype(BF16)
            sent, _ = self._copies(ins, outs, ss, rs, k)
            sent[0].start()
            sent[1].start()
        for k in range(self.nb, self.n):
            for j, peer in enumerate(peers):
                self._small(ins, outs, ss, rs, k, j, peer, me, c).start()
            outs[k][me] = ins[k][...]

    def middle(self, ins, outs, ss, rs):
        for k in range(self.nb):
            sent, received = self._copies(ins, outs, ss, rs, k)
            for pair in (0, 1):
                received[pair].wait_recv()
                sent[2 + pair].start()
                sent[4 + pair].start()

    def late(self, ins, outs, ss, rs):
        for k in range(self.nb):
            _, received = self._copies(ins, outs, ss, rs, k)
            for pair in (4, 5):
                received[pair].wait_recv()

    def last(self, ins, outs, ss, rs):
        x, y, c, peers = _mesh_place()
        for k in range(self.nb):
            sent, received = self._copies(ins, outs, ss, rs, k)
            for pair in (2, 3):
                received[pair].wait_recv()
                sent[4 + pair].start()
        for k in range(self.nb):
            sent, received = self._copies(ins, outs, ss, rs, k)
            for pair in (6, 7):
                received[pair].wait_recv()
            for cp in sent:
                cp.wait_send()
        for k in range(self.nb, self.n):
            for j, (px, py) in enumerate(peers):
                self._small(ins, outs, ss, rs, k, j, (px, py), 2 * px + py, c).wait_recv()
                self._small(ins, outs, ss, rs, k, j, (px, py), 2 * x + y, c).wait_send()


def _run_alone(rider, ins, name):
    r_in, r_out_specs, r_out_shape, r_scratch = _rider_specs(rider, ins)

    def body(*refs):
        ride_in, ride_out, scratch = _split(refs, len(r_in), len(r_out_specs), len(r_scratch))
        _ride(rider, ("first", "middle", "last"), pl.program_id(0) == 0, ride_in, ride_out, scratch)

    return pl.pallas_call(
        body, grid=(1,), in_specs=r_in, out_specs=r_out_specs, out_shape=r_out_shape, scratch_shapes=r_scratch,
        compiler_params=_params(("arbitrary",)), name=name,
    )(*ins)


class _Presum:
    in_space = ANY

    def __init__(self, names, base=0):
        self.names = tuple(names)
        self.n = len(self.names)
        self.base = base
        self.n_sems = 3 * self.n
        self.out_shape = [jax.ShapeDtypeStruct((N_CHIP,) + _half_shape(nm), BF16) for nm in self.names]
        self.work_shape = self.out_shape + self.out_shape

    def _stage(self, ins, bufs, ss, k, e, which):
        n = self.n
        return pltpu.make_async_copy(_half(ins[k], e, self.names[k], lead=1), bufs[which * n + k],
                                     ss.at[self.base + which * n + k])

    def _give(self, bufs, ss, rs, k, sibling):
        return _remote(bufs[self.n + k], bufs[k], ss, rs, self.base + k, sibling)

    def first(self, ins, bufs, ss, rs):
        x, y, c, _ = _mesh_place()
        for k in range(self.n):
            self._stage(ins, bufs, ss, k, 1 - c, 1).start()
        for k in range(self.n):
            self._stage(ins, bufs, ss, k, c, 2).start()
        for k in range(self.n):
            self._stage(ins, bufs, ss, k, 1 - c, 1).wait()
            self._give(bufs, ss, rs, k, (x, y, 1 - c)).start()

    def middle(self, ins, bufs, ss, rs):
        pass

    def last(self, ins, bufs, ss, rs):
        x, y, c, _ = _mesh_place()
        for k in range(self.n):
            self._give(bufs, ss, rs, k, (x, y, 1 - c)).wait_recv()
            self._stage(ins, bufs, ss, k, c, 2).wait()

            @pl.loop(0, N_CHIP)
            def _(j):
                bufs[k][j] = (bufs[k][j].astype(F32) + bufs[2 * self.n + k][j].astype(F32)).astype(BF16)
        for k in range(self.n):
            self._give(bufs, ss, rs, k, (x, y, 1 - c)).wait_send()


class _ReduceRelay:
    middle_at = 0.75

    def __init__(self, names, base=0):
        self.names = tuple(names)
        self.n = len(self.names)
        self.base = base
        self.n_sems = 6 * self.n
        self.out_shape = [jax.ShapeDtypeStruct((N_CHIP,) + _half_shape(nm), BF16) for nm in self.names]
        quarter = [jax.ShapeDtypeStruct(self._part_shape(nm), BF16) for nm in self.names]
        self.work_shape = quarter + quarter

    @staticmethod
    def _part_shape(name):
        r, c = _half_shape(name)
        return (r // 2, c) if _BIG_SPLIT[name] == 0 else (r, c // 2)

    def _part(self, ref, name, p):
        r, c = self._part_shape(name)
        return ref.at[pl.ds(p * r, r), pl.ds(0, c)] if _BIG_SPLIT[name] == 0 else ref.at[pl.ds(0, r), pl.ds(p * c, c)]

    def _copies(self, ins, bufs, ss, rs, k):
        x, y, c, _ = _mesh_place()
        name, n = self.names[k], self.n
        me, xn, yn, dg = 2 * x + y, 2 * (1 - x) + y, 2 * x + (1 - y), 2 * (1 - x) + (1 - y)
        to_x, to_y = (1 - x, y, c), (x, 1 - y, c)
        mine = lambda slot, p: self._part(ins[k].at[slot], name, p)
        slot = lambda s, p: self._part(bufs[k].at[s], name, p)
        from_x, from_y = bufs[n + k], bufs[2 * n + k]

        def copy(pair, src, dst, to):
            return _remote(src, dst, ss, rs, self.base + 6 * k + pair, to)

        sent = [copy(0, mine(dg, 0), from_x, to_x), copy(1, mine(dg, 1), from_y, to_y),
                copy(2, mine(xn, 0), slot(me, 0), to_x), copy(3, mine(yn, 1), slot(me, 1), to_y),
                copy(4, from_y, slot(me, 1), to_x), copy(5, from_x, slot(me, 0), to_y)]
        landing = [from_x, from_y, slot(xn, 0), slot(yn, 1), slot(xn, 1), slot(yn, 0)]
        received = [copy(pair, dst, dst, to_x) for pair, dst in enumerate(landing)]
        return sent, received

    def first(self, ins, bufs, ss, rs):
        x, y, c, _ = _mesh_place()
        me, dg = 2 * x + y, 2 * (1 - x) + (1 - y)
        for k in range(self.n):
            sent, _ = self._copies(ins, bufs, ss, rs, k)
            for pair in range(4):
                sent[pair].start()
        for k in range(self.n):
            bufs[k][me] = ins[k][me]
            bufs[k][dg] = jnp.zeros(_half_shape(self.names[k]), BF16)

    def middle(self, ins, bufs, ss, rs):
        x, y, c, _ = _mesh_place()
        xn, yn = 2 * (1 - x) + y, 2 * x + (1 - y)
        for k in range(self.n):
            sent, received = self._copies(ins, bufs, ss, rs, k)
            name, n = self.names[k], self.n
            for pair, buf, own in ((0, bufs[n + k], self._part(ins[k].at[yn], name, 0)),
                                   (1, bufs[2 * n + k], self._part(ins[k].at[xn], name, 1))):
                received[pair].wait_recv()
                buf[...] = (buf[...].astype(F32) + own[...].astype(F32)).astype(BF16)
            sent[5].start()
            sent[4].start()

    def last(self, ins, bufs, ss, rs):
        for k in range(self.n):
            sent, received = self._copies(ins, bufs, ss, rs, k)
            for pair in range(2, 6):
                received[pair].wait_recv()
            for cp in sent:
                cp.wait_send()


class _PresumThenRelay:
    in_space = ANY
    middle_at = _ReduceRelay.middle_at

    def __init__(self, names):
        self.relay = _ReduceRelay(names)
        self.pre = _Presum(names, base=self.relay.n_sems)
        self.n_sems = self.relay.n_sems + self.pre.n_sems
        self.out_shape = self.relay.out_shape
        self.work_shape = list(self.relay.work_shape) + list(self.pre.out_shape) + list(self.pre.work_shape)
        self.n_relay = len(self.relay.out_shape) + len(self.relay.work_shape)

    def first(self, ins, bufs, ss, rs):
        self.pre.first(ins, bufs[self.n_relay:], ss, rs)

    def early(self, ins, bufs, ss, rs):
        self.pre.last(ins, bufs[self.n_relay:], ss, rs)
        self.relay.first(bufs[self.n_relay:], bufs[:self.n_relay], ss, rs)

    def middle(self, ins, bufs, ss, rs):
        self.relay.middle(bufs[self.n_relay:], bufs[:self.n_relay], ss, rs)

    def last(self, ins, bufs, ss, rs):
        self.relay.last(bufs[self.n_relay:], bufs[:self.n_relay], ss, rs)


class _SendPartials:
    def __init__(self, names, small_shape=None):
        self.n = len(names)
        self.small = small_shape is not None
        self.n_sems = 3 * self.n + 7
        self.out_shape = [jax.ShapeDtypeStruct((N_CHIP,) + _half_shape(nm), BF16) for nm in names]
        if self.small:
            self.out_shape.append(jax.ShapeDtypeStruct((N_DEV,) + small_shape, F32))

    def _piece(self, ins, outs, ss, rs, k, j, peer, src_slot, dst_slot, c):
        return _remote(ins[k].at[src_slot], outs[k].at[dst_slot], ss, rs, 3 * k + j, (*peer, c))

    def _small(self, ins, outs, ss, rs, r, other, slot):
        return _remote(ins[self.n], outs[self.n].at[slot], ss, rs, 3 * self.n + r, other)

    @staticmethod
    def _others(x, y, c):
        return [(x, y, 1 - c), (1 - x, y, c), (1 - x, y, 1 - c), (x, 1 - y, c), (x, 1 - y, 1 - c),
                (1 - x, 1 - y, c), (1 - x, 1 - y, 1 - c)]

    def first(self, ins, outs, ss, rs, only=None):
        x, y, c, peers = _mesh_place()
        me = 2 * x + y
        which = range(self.n) if only is None else only
        for k in which:
            for j, (px, py) in enumerate(peers):
                self._piece(ins, outs, ss, rs, k, j, (px, py), 2 * px + py, me, c).start()
        if self.small:
            for r, other in enumerate(self._others(x, y, c)):
                self._small(ins, outs, ss, rs, r, other, 4 * x + 2 * y + c).start()
            outs[self.n][4 * x + 2 * y + c] = ins[self.n][...]
        for k in which:
            outs[k][me] = ins[k][me]

    def middle(self, ins, outs, ss, rs):
        pass

    def last(self, ins, outs, ss, rs):
        x, y, c, peers = _mesh_place()
        me = 2 * x + y
        for k in range(self.n):
            for j, (px, py) in enumerate(peers):
                self._piece(ins, outs, ss, rs, k, j, (px, py), me, 2 * px + py, c).wait_recv()
                self._piece(ins, outs, ss, rs, k, j, (px, py), 2 * px + py, me, c).wait_send()
        if self.small:
            for r, (px, py, pc) in enumerate(self._others(x, y, c)):
                self._small(ins, outs, ss, rs, r, (px, py, pc), 4 * px + 2 * py + pc).wait_recv()
                self._small(ins, outs, ss, rs, r, (px, py, pc), 4 * x + 2 * y + c).wait_send()


class _PresumThenSend:
    def __init__(self, names):
        self.send = _SendPartials(names)
        self.pre = _Presum(names[-1:], base=self.send.n_sems)
        self.n = self.send.n
        self.n_sems = self.send.n_sems + self.pre.n_sems
        self.out_shape = self.send.out_shape
        self.work_shape = list(self.pre.out_shape) + list(self.pre.work_shape)
        self.in_space = [VMEM_WHOLE] * (self.n - 1) + [ANY]

    def _partials(self, ins, bufs):
        return list(ins[:self.n - 1]) + [bufs[self.n]]

    def first(self, ins, bufs, ss, rs):
        self.pre.first(ins[self.n - 1:], bufs[self.n:], ss, rs)
        self.send.first(ins, bufs[:self.n], ss, rs, only=range(self.n - 1))

    def early(self, ins, bufs, ss, rs):
        self.pre.last(ins[self.n - 1:], bufs[self.n:], ss, rs)
        self.send.first(self._partials(ins, bufs), bufs[:self.n], ss, rs, only=(self.n - 1,))

    def middle(self, ins, bufs, ss, rs):
        pass

    def last(self, ins, bufs, ss, rs):
        self.send.last(self._partials(ins, bufs), bufs[:self.n], ss, rs)


def _sum_swap(names, parts, small):
    n = len(parts)
    everyone = _SendPartials((), small.shape)

    def body(*refs):
        (p_hbm, (small_ref,), o_hbm, (osmall_ref,), p_refs, o_refs, (all_ref,),
         (send_sems, recv_sems, ss_small, rs_small, load_sems, leave_sems)) = _split(refs, n, 1, n, 1, n, n, 1, 6)
        x, y, c = lax.axis_index("x"), lax.axis_index("y"), lax.axis_index("c")
        loads = [pltpu.make_async_copy(p_hbm[k], p_refs[k], load_sems.at[k]) for k in range(n)]
        for cp in loads:
            cp.start()
        everyone.first([small_ref], [all_ref], ss_small, rs_small)

        def mine(k):
            part = _half(o_refs[k], c, names[k])
            return _remote(part, part, send_sems, recv_sems, k, (x, y, 1 - c))

        def leave(k, whose):
            e = c if whose == 0 else 1 - c
            return pltpu.make_async_copy(_half(o_refs[k], e, names[k]), _half(o_hbm[k], e, names[k]),
                                         leave_sems.at[2 * k + whose])

        for k in range(n):
            loads[k].wait()
            for e in range(2):
                @pl.when(c == e)
                def _():
                    g = p_refs[k][0].astype(F32)
                    for s in range(1, N_CHIP):
                        g = g + p_refs[k][s].astype(F32)
                    r, cols = _half_shape(names[k])
                    if _BIG_SPLIT[names[k]] == 0:
                        o_refs[k][e * r:(e + 1) * r, :] = g
                    else:
                        o_refs[k][:, e * cols:(e + 1) * cols] = g
            mine(k).start()
            leave(k, 0).start()
        for k in range(n):
            theirs = _half(o_refs[k], 1 - c, names[k])
            _remote(theirs, theirs, send_sems, recv_sems, k, (x, y, 1 - c)).wait_recv()
            leave(k, 1).start()
        everyone.last([small_ref], [all_ref], ss_small, rs_small)
        g = all_ref[0]
        for d in range(1, N_DEV):
            g = g + all_ref[d]
        osmall_ref[...] = g
        for k in range(n):
            mine(k).wait_send()
            leave(k, 0).wait()
            leave(k, 1).wait()

    shards = [jax.ShapeDtypeStruct(_BIG_SHARD[nm], F32) for nm in names]
    res = pl.pallas_call(
        body, in_specs=[ANY] * n + [VMEM_WHOLE], out_specs=[ANY] * n + [VMEM_WHOLE],
        out_shape=shards + [jax.ShapeDtypeStruct(small.shape, F32)],
        scratch_shapes=[pltpu.VMEM(q.shape, q.dtype) for q in parts] + [pltpu.VMEM(s.shape, s.dtype) for s in shards]
        + [pltpu.VMEM((N_DEV,) + small.shape, F32), pltpu.SemaphoreType.DMA((n,)), pltpu.SemaphoreType.DMA((n,)),
           pltpu.SemaphoreType.DMA((everyone.n_sems,)), pltpu.SemaphoreType.DMA((everyone.n_sems,)),
           pltpu.SemaphoreType.DMA((n,)), pltpu.SemaphoreType.DMA((2 * n,))],
        compiler_params=_params(), name="sum_swap",
    )(*parts, small)
    return res[:n], res[n]


def _adamw_math(w, g, m, v):
    m = ADAM_B1 * m + (1.0 - ADAM_B1) * g
    v = ADAM_B2 * v + (1.0 - ADAM_B2) * (g * g)
    m_hat = m / (1.0 - ADAM_B1 ** ADAM_STEP)
    v_hat = v / (1.0 - ADAM_B2 ** ADAM_STEP)
    delta = -ADAM_LR * (m_hat / (jnp.sqrt(v_hat) + ADAM_EPS) + ADAM_WD * w)
    return delta, m, v


ADAMW_STEPS = 8


def _adamw_big(gs, ws, ms, vs, name):
    n = len(ws)

    def body(*refs):
        for k in range(n):
            g_ref, w_ref, m_ref, v_ref = refs[4 * k:4 * k + 4]
            g_out_ref, d_ref, nm_ref, nv_ref = refs[4 * (n + k):4 * (n + k) + 4]
            g = g_ref[...].reshape(w_ref.shape)
            g_out_ref[...] = g
            d_ref[...], nm_ref[...], nv_ref[...] = _adamw_math(w_ref[...], g, m_ref[...], v_ref[...])

    in_specs, out_specs, shapes, args = [], [], [], []
    for g, w, m, v in zip(gs, ws, ms, vs):
        tr = -(-w.shape[0] // (8 * ADAMW_STEPS)) * 8
        zeros = (0,) * (w.ndim - 1)
        blk = pl.BlockSpec((tr,) + w.shape[1:], lambda i, zeros=zeros: (i,) + zeros)
        in_specs += [pl.BlockSpec((tr,) + g.shape[1:], lambda i: (i, 0)), blk, blk, blk]
        out_specs += [blk] * 4
        shapes += [jax.ShapeDtypeStruct(w.shape, F32)] * 4
        args += [g, w, m, v]
    res = pl.pallas_call(
        body, grid=(ADAMW_STEPS,), in_specs=in_specs, out_specs=out_specs, out_shape=shapes,
        compiler_params=_params(("arbitrary",)), name=name,
    )(*args)
    return [tuple(res[4 * k:4 * k + 4]) for k in range(n)]


def _adamw_small(ws, gs, ms, vs):
    n = len(ws)

    def body(*refs):
        w_refs, g_refs, m_refs, v_refs, d_refs, nm_refs, nv_refs = _split(refs, *([n] * 7))
        for k in range(n):
            d_refs[k][...], nm_refs[k][...], nv_refs[k][...] = _adamw_math(w_refs[k][...], g_refs[k][...], m_refs[k][...],
                                                                             v_refs[k][...])

    shapes = [jax.ShapeDtypeStruct(w.shape, F32) for w in ws]
    res = pl.pallas_call(body, out_shape=shapes * 3, name="adamw_small")(*ws, *gs, *ms, *vs)
    return res[:n], res[n:2 * n], res[2 * n:]


def _pack(arrs):
    flat = jnp.concatenate([a.reshape(-1) for a in arrs])
    rows = -(-flat.shape[0] // 1024) * 8
    return jnp.pad(flat, (0, rows * 128 - flat.shape[0])).reshape(rows, 128)


def _unpack(buf, shapes):
    flat = buf.reshape(-1)
    out, off = [], 0
    for s in shapes:
        size = 1
        for d in s:
            size *= d
        out.append(flat[off:off + size].reshape(s))
        off += size
    return out


def _block_rows(w):
    return jnp.pad(w.reshape(512, 4), ((0, 0), (0, 124)))


def _block_stored(dw):
    return jnp.transpose(dw[:, 0:4].reshape(128, 4, 4), (1, 2, 0)).reshape(16, 128)


def _cols(a4):
    return jnp.transpose(a4, (1, 0, 2)).reshape(a4.shape[1], -1)


_LATE = ("w_pa", "w_pb", "w_o", "w_up", "w_down")
_RIDE_IN_PROJ = ("w_pa", "w_pb", "w_o", "w_down_b")
_RIDE_MIXER = ("w_up", "w_down_a")


def _full_weights(gathered):
    joined = {"w_o": (D_MODEL, D_MODEL)}
    return {n: (a.reshape(joined[n]) if n in joined else a) for n, a in gathered.items()}


def _local_step(x, target, w, sp, late_shards=None):
    sp = {n: (a.reshape(1, -1) if a.ndim == 1 else a) for n, a in sp.items()}
    wau = jnp.pad(sp["w_a_up"], ((0, 112), (0, 0)))
    wif = jnp.pad(sp["w_if"], ((0, 0), (0, 120)))
    bif = jnp.pad(sp["b_if"], ((0, 0), (0, 120)))
    p = {"wau": wau, "bau": sp["b_a_up"], "ggla": sp["g_gla_norm"], "cw": sp["conv_w"], "cb": sp["conv_b"],
         "wq": _block_rows(sp["w_q_ml"]), "wk": _block_rows(sp["w_k_ml"]), "wv": _block_rows(sp["w_v_ml"]),
         "wif": wif, "bif": bif, "skip": sp["ml_skip"], "gml": sp["g_ml_norm"]}

    if late_shards is None:
        (pm, gab, h), _ = _in_proj(x, sp["g_pre_mix"], w["w_in"])
        ab, x1, mix, merged, *states = _mixer_fwd(pm, p, gab, x, w["w_pa"], w["w_pb"], w["w_o"], sp["g_post_mix"])
    else:
        shard = dict(zip(_LATE, late_shards))
        shard["w_down_a"] = shard["w_down_b"] = shard["w_down"]
        (pm, gab, h), got = _in_proj(x, sp["g_pre_mix"], w["w_in"], _Gather(_RIDE_IN_PROJ, middle_at=0.7),
                                     [shard[n] for n in _RIDE_IN_PROJ])
        w = dict(w, **_full_weights(dict(zip(_RIDE_IN_PROJ, got))))
        ab, x1, mix, merged, *rest = _mixer_fwd(pm, p, gab, x, w["w_pa"], w["w_pb"], w["w_o"], sp["g_post_mix"],
                                                _Gather(_RIDE_MIXER, middle_at=0.62), [shard[n] for n in _RIDE_MIXER])
        states = rest[:4]
        w.update(_full_weights(dict(zip(_RIDE_MIXER, rest[4:]))))
    dx1, u, dd, h2, dpre, dg_post_mlp, dg_pre_mlp, loss = _mlp(x1, target, sp["g_pre_mlp"], sp["g_post_mlp"],
                                                                w["w_up"], w["w_down_a"], w["w_down_b"])
    dgab, dab, dg_post_mix, dw_pa, dw_pb, dw_o = _merge_bwd(dx1, mix, ab, gab, merged, w["w_pa"], w["w_pb"], w["w_o"],
                                                            sp["g_post_mix"])
    big = {"w_pa": dw_pa, "w_pb": dw_pb, "w_o": dw_o, "w_up": _tn_matmul(h2, dpre, "dw_up", shards=N_CHIP)}
    if late_shards is None:
        big["w_down"] = _tn_matmul(u, dd, "dw_down")
        dpm, dp, _ = _mixer_bwd(pm, dab, states, p)
    else:
        pieces = lambda n: big[n].reshape((N_CHIP,) + _BIG_SHARD[n])
        big["w_down"], partial = _tn_matmul(u, dd, "dw_down", rider=_Presum(_LATE[:4]),
                                            rider_ins=[pieces(n) for n in _LATE[:4]])
        dpm, dp, parts = _mixer_bwd(pm, dab, states, p, _PresumThenSend(_LATE), list(partial) + [pieces("w_down")])
        big = dict(zip(_LATE, parts))
    big["w_in"] = _dw_in(dpm, dgab, h)
    if late_shards is None:
        (dx, dg_pre_mix), _ = _in_proj_bwd(dpm, dgab, x, dx1, sp["g_pre_mix"], w["w_in"])
    else:
        (dx, dg_pre_mix), parts = _in_proj_bwd(dpm, dgab, x, dx1, sp["g_pre_mix"], w["w_in"], _PresumThenRelay(("w_in",)),
                                               [big["w_in"]])
        big["w_in"] = parts[0]
    small = {
        "g_pre_mix": dg_pre_mix, "b_a_up": dp["bau"], "g_gla_norm": dp["ggla"], "conv_b": dp["cb"],
        "w_q_ml": _block_stored(dp["wq"]), "w_k_ml": _block_stored(dp["wk"]), "w_v_ml": _block_stored(dp["wv"]),
        "w_if": dp["wif"][:, 0:8].T,
        "b_if": dp["bif"][:, 0:8], "ml_skip": dp["skip"], "g_ml_norm": dp["gml"], "g_post_mix": dg_post_mix,
        "g_pre_mlp": dg_pre_mlp, "g_post_mlp": dg_post_mlp, "w_a_up": dp["wau"][0:16], "conv_w": dp["cw"],
        "loss": loss[:, 0:1],
    }
    return dx, big, small


_SMALL_REPL = ("g_pre_mix", "b_a_up", "g_gla_norm", "conv_b", "w_q_ml", "w_k_ml", "w_v_ml", "b_if", "ml_skip",
               "g_ml_norm", "g_post_mix", "g_pre_mlp", "g_post_mlp")
_SMALL_SHARDED = ("w_a_up", "conv_w", "w_if")
_SMALL_ORDER = _SMALL_REPL + _SMALL_SHARDED + ("loss",)
_WEIGHTS = ("g_pre_mix", "w_in", "w_a_up", "b_a_up", "g_gla_norm", "conv_w", "conv_b", "w_q_ml", "w_k_ml", "w_v_ml",
            "w_if", "b_if", "ml_skip", "g_ml_norm", "w_pa", "w_pb", "w_o", "g_post_mix", "g_pre_mlp", "w_up", "w_down",
            "g_post_mlp")


_BLOCK_WEIGHTS = ("w_q_ml", "w_k_ml", "w_v_ml")


def _stored(name, a):
    if name in _BLOCK_WEIGHTS:
        return jnp.transpose(a, (0, 2, 3, 1)).reshape(16, 128)
    if name == "w_if":
        return jnp.transpose(a, (0, 2, 1)).reshape(8, 384)
    return a


def _unstored(name, a):
    if name in _BLOCK_WEIGHTS:
        return jnp.transpose(a.reshape(1, 4, 4, 128), (0, 3, 1, 2))
    if name == "w_if":
        return jnp.transpose(a.reshape(1, 8, 384), (0, 2, 1))
    return a


def _as_shard(name, a):
    return jnp.transpose(a, (2, 0, 1)).reshape(IN_SHARD, D_MODEL // 128, 128) if name == "w_in" else a[0]


def _in_shard_bf16(w_in):
    return jnp.transpose(w_in.astype(BF16), (2, 0, 1)).reshape(IN_SHARD, D_MODEL)


def _from_shard(name, a):
    return jnp.transpose(a, (1, 2, 0)).reshape(1, D_MODEL, IN_SHARD) if name == "w_in" else a[None]


def kernel(x, g_pre_mix, w_in, w_a_up, b_a_up, g_gla_norm, conv_w, conv_b, w_q_ml, w_k_ml, w_v_ml, w_if, b_if, ml_skip, g_ml_norm, w_pa, w_pb, w_o, g_post_mix, g_pre_mlp, w_up, w_down, g_post_mlp, loss_target, m_g_pre_mix, m_w_in, m_w_a_up, m_b_a_up, m_g_gla_norm, m_conv_w, m_conv_b, m_w_q_ml, m_w_k_ml, m_w_v_ml, m_w_if, m_b_if, m_ml_skip, m_g_ml_norm, m_w_pa, m_w_pb, m_w_o, m_g_post_mix, m_g_pre_mlp, m_w_up, m_w_down, m_g_post_mlp, v_g_pre_mix, v_w_in, v_w_a_up, v_b_a_up, v_g_gla_norm, v_conv_w, v_conv_b, v_w_q_ml, v_w_k_ml, v_w_v_ml, v_w_if, v_b_if, v_ml_skip, v_g_ml_norm, v_w_pa, v_w_pb, v_w_o, v_g_post_mix, v_g_pre_mlp, v_w_up, v_w_down, v_g_post_mlp):
    args = dict(locals())
    wts = {n: _as_shard(n, args[n]) for n in _WEIGHTS}
    mom = {n: _as_shard(n, args["m_" + n]) for n in _WEIGHTS}
    var = {n: _as_shard(n, args["v_" + n]) for n in _WEIGHTS}
    chip = 2 * lax.axis_index("x") + lax.axis_index("y")

    first = ("w_in",) + _SMALL_SHARDED
    gathered = dict(zip(first, _run_alone(_Gather(("w_in",), [wts[n] for n in _SMALL_SHARDED]),
                                          [_in_shard_bf16(w_in)] + [wts[n] for n in _SMALL_SHARDED],
                                          "gather_first")))
    sp = {n: wts[n] for n in _SMALL_REPL}
    sp["w_a_up"] = _cols(gathered["w_a_up"])
    sp["conv_w"] = _cols(gathered["conv_w"])
    sp["w_if"] = gathered["w_if"].reshape(1536, 8)

    dx, big, small = _local_step(x[0], loss_target[0], _full_weights({"w_in": gathered["w_in"]}), sp,
                                 late_shards=[wts[n] for n in _LATE])

    small_shapes = [small[n].shape for n in _SMALL_ORDER]
    packed = _pack([small[n] for n in _SMALL_ORDER])
    sums, small_sum = _sum_swap(_BIG, [big[n] for n in _BIG], packed)

    grads, delta, new_m, new_v = {}, {}, {}, {}
    updated = dict(zip(_BIG, _adamw_big(sums, [wts[n] for n in _BIG], [mom[n] for n in _BIG], [var[n] for n in _BIG],
                                        "adamw_big")))
    for n in _BIG:
        grads[n], delta[n], new_m[n], new_v[n] = (_from_shard(n, a) for a in updated[n])
    summed = dict(zip(_SMALL_ORDER, _unpack(small_sum, small_shapes)))
    loss = summed["loss"].reshape(())
    summed["w_a_up"] = lax.dynamic_slice_in_dim(summed["w_a_up"], chip * 64, 64, axis=1)
    summed["conv_w"] = lax.dynamic_slice_in_dim(summed["conv_w"], chip * 128, 128, axis=1)
    summed["w_if"] = lax.dynamic_slice_in_dim(summed["w_if"], chip * 384, 384, axis=1)
    small_names = _SMALL_REPL + _SMALL_SHARDED
    came_stored = _BLOCK_WEIGHTS + ("w_if",)
    g_stored = [summed[n] if n in came_stored else _stored(n, summed[n].reshape(args[n].shape)) for n in small_names]
    upd = _adamw_small([_stored(n, args[n]) for n in small_names], g_stored,
                       [_stored(n, args["m_" + n]) for n in small_names], [_stored(n, args["v_" + n]) for n in small_names])
    for dst, arrs in zip((grads, delta, new_m, new_v), (g_stored,) + tuple(upd)):
        dst.update({n: _unstored(n, a) for n, a in zip(small_names, arrs)})

    outs = [loss, dx[None]]
    for group in (grads, delta, new_m, new_v):
        outs += [group[n] for n in _WEIGHTS]
    return tuple(outs)
```

```python
import functools

import jax
import jax.numpy as jnp
from jax import lax
from jax.experimental import pallas as pl
from jax.experimental.pallas import tpu as pltpu

F32 = jnp.float32
BF16 = jnp.bfloat16

SEQ = 2048
D_MODEL = 1024
CHUNK = 64
N_CHUNK = SEQ // CHUNK
HEADS = 4
GLA_DK = 64
GLA_DV = 128
ML_DH = 128
D_FF = 4096
EPS = 1e-6
N_CHIP = 4
N_DEV = 8
TOK_TILE = 256
N_TOK_TILE = SEQ // TOK_TILE
SWEEP = 2
assert CHUNK == 64
N_SWEEP = N_CHUNK // SWEEP

PM_W = 2688
PM_XM = 1536
PM_OP = 2048
PM_AL = 2560
GAB_W = 2048
D_IN = 4624
IN_SHARD = D_IN // N_CHIP
IN_ALOW = 1536
IN_XM = 1552
IN_GATES = 2576

ADAM_LR = 0.001
ADAM_B1 = 0.9
ADAM_B2 = 0.999
ADAM_EPS = 1e-08
ADAM_WD = 0.01
ADAM_STEP = 10

VMEM_LIMIT = 56 * 1024 * 1024


def _params(sem=None):
    return pltpu.CompilerParams(dimension_semantics=sem, vmem_limit_bytes=VMEM_LIMIT)


def _dot(a, b, ca, cb):
    return lax.dot_general(a.astype(BF16), b.astype(BF16), (((ca,), (cb,)), ((), ())), preferred_element_type=F32)


def _pmm_nn(a, b):
    return _dot(a, b, 1, 0)


def _pmm_nt(a, b):
    return _dot(a, b, 1, 1)


def _pmm_tn(a, b):
    return _dot(a, b, 0, 0)


def _pcmm(c, x):
    return lax.dot_general(c, x, (((1,), (0,)), ((), ())), precision=lax.Precision.HIGHEST, preferred_element_type=F32)


@jax.custom_vjp
def _mm_nn(a, b):
    return _dot(a, b, 1, 0)


@jax.custom_vjp
def _mm_nt(a, b):
    return _dot(a, b, 1, 1)


@jax.custom_vjp
def _mm_tn(a, b):
    return _dot(a, b, 0, 0)


_mm_nn.defvjp(lambda a, b: (_dot(a, b, 1, 0), (a, b)), lambda r, g: (_mm_nt(g, r[1]), _mm_tn(r[0], g)))
_mm_nt.defvjp(lambda a, b: (_dot(a, b, 1, 1), (a, b)), lambda r, g: (_mm_nn(g, r[1]), _mm_tn(g, r[0])))
_mm_tn.defvjp(lambda a, b: (_dot(a, b, 0, 0), (a, b)), lambda r, g: (_mm_nt(r[1], g), _mm_nn(r[0], g)))


@jax.custom_vjp
def _cmm(c, x):
    return _pcmm(c, x)


_cmm.defvjp(
    lambda c, x: (_pcmm(c, x), c),
    lambda c, g: (jnp.zeros_like(c), lax.dot_general(c, g, (((0,), (0,)), ((), ())), precision=lax.Precision.HIGHEST,
                                                      preferred_element_type=F32)),
)

_PLAIN_OPS = (_pmm_nn, _pmm_nt, _pmm_tn, _pcmm)
_VJP_OPS = (_mm_nn, _mm_nt, _mm_tn, _cmm)


def _sigmoid(x):
    return 0.5 * (jnp.tanh(0.5 * x) + 1.0)


def _log_sigmoid(x):
    return jnp.minimum(x, 0.0) - jnp.log(1.0 + jnp.exp(-jnp.abs(x)))


def _mean(x):
    return jnp.mean(x, axis=-1, keepdims=True)


def _nt(a, b):
    return lax.dot_general(a, b, (((1,), (1,)), ((), ())), preferred_element_type=F32)


def _tn(a, b):
    return lax.dot_general(a, b, (((0,), (0,)), ((), ())), preferred_element_type=F32)


def _mixer_chunk(ops, p, st, pm, xprev8):
    mm_nn, mm_nt, mm_tn, cmm = ops
    n_rows = pm.shape[0]
    n_ch = n_rows // CHUNK
    row = lax.broadcasted_iota(jnp.int32, (n_rows, n_rows), 0)
    col = lax.broadcasted_iota(jnp.int32, (n_rows, n_rows), 1)
    tri = jnp.logical_and((row >> 6) == (col >> 6), row >= col).astype(F32)
    causal = tri[0:CHUNK, 0:CHUNK] > 0.0
    q = pm[:, 0:256]
    k = pm[:, 256:512]
    v = pm[:, 512:1024]
    g = pm[:, 1024:1536]
    xm = pm[:, PM_XM:PM_XM + 512]
    opre = pm[:, PM_OP:PM_OP + 512]
    alow = pm[:, PM_AL:PM_AL + 128]
    hs = range(HEADS)
    cs = range(n_ch)
    pairs = [(i, h) for i in cs for h in hs]
    rs = [slice(i * CHUNK, (i + 1) * CHUNK) for i in cs]
    last = [slice((i + 1) * CHUNK - 1, (i + 1) * CHUNK) for i in cs]
    s6 = [slice(h * GLA_DK, (h + 1) * GLA_DK) for h in hs]
    s12 = [slice(h * 128, (h + 1) * 128) for h in hs]

    xx = jnp.concatenate([xprev8, xm], axis=0)
    pre = p["cb"]
    for j in range(4):
        pre = pre + p["cw"][j:j + 1, :] * xx[5 + j:5 + j + n_rows, :]
    xc = pre * _sigmoid(pre)
    qm = [mm_nn(xc[:, s12[h]], p["wq"][h]) for h in hs]
    km = [mm_nn(xc[:, s12[h]], p["wk"][h]) for h in hs]
    vm = [mm_nn(xm[:, s12[h]], p["wv"][h]) for h in hs]
    qcat = jnp.concatenate(qm, axis=1)
    kcat = jnp.concatenate(km, axis=1)
    vcat = jnp.concatenate(vm, axis=1)
    gates = (mm_nn(qcat, p["wif"][0:512]) + mm_nn(kcat, p["wif"][512:1024]) + mm_nn(vcat, p["wif"][1024:1536])
             + p["bif"])
    lf = _log_sigmoid(gates)
    fc = cmm(tri, lf)
    gates_t = gates.T
    fc_t = fc.T

    la = _log_sigmoid(mm_nn(alow, p["wau"]) + p["bau"]) * (1.0 / 16.0)
    cum = cmm(tri, la)
    cum_last = [cum[last[i], :] for i in cs]
    to_end = jnp.concatenate([cum_last[i] - cum[rs[i], :] for i in cs], axis=0)
    e_pos = jnp.exp(cum)
    e_neg = jnp.exp(-cum)
    qs = q * (GLA_DK ** -0.5)
    qp = qs * e_pos
    qn = qs * e_neg
    kp = k * e_pos
    kn = k * e_neg
    kl = k * jnp.exp(to_end)
    dec = [jnp.exp(cum_last[i]) for i in cs]
    ks = [km[h] * (ML_DH ** -0.5) for h in hs]
    li_c = {(i, h): gates[rs[i], h:h + 1] for i, h in pairs}
    fc_c = {(i, h): fc[rs[i], 4 + h:5 + h] for i, h in pairs}
    f_last = {(i, h): fc[last[i], 4 + h:5 + h] for i, h in pairs}

    a_fwd = {(i, h): mm_nt(qp[rs[i], s6[h]], kn[rs[i], s6[h]]) for i, h in pairs}
    a_bwd = {(i, h): mm_nt(qn[rs[i], s6[h]], kp[rs[i], s6[h]]) for i, h in pairs}
    s_chunk = {(i, h): mm_tn(v[rs[i], s12[h]], kl[rs[i], s6[h]]) for i, h in pairs}
    qk = {(i, h): mm_nt(qm[h][rs[i]], ks[h][rs[i]]) for i, h in pairs}
    a = {ih: f_last[ih] - fc_c[ih] + li_c[ih] for ih in pairs}
    m_loc = {ih: jnp.max(a[ih], axis=0, keepdims=True) for ih in pairs}
    kw = {(i, h): ks[h][rs[i]] * jnp.exp(a[(i, h)] - m_loc[(i, h)]) for i, h in pairs}
    c_chunk = {(i, h): mm_tn(kw[(i, h)], vm[h][rs[i]]) for i, h in pairs}
    mem = {(0, h): st["S"][h] for h in hs}
    c_in = {(0, h): st["C"][h] for h in hs}
    n_in = {(0, h): st["n"][h] for h in hs}
    m_in = {(0, h): st["m"][h][:, 0:1] for h in hs}
    for i, h in pairs:
        mem[(i + 1, h)] = mem[(i, h)] * dec[i][:, s6[h]] + s_chunk[(i, h)]
        m_nx = jnp.maximum(f_last[(i, h)] + m_in[(i, h)], m_loc[(i, h)])
        sp = jnp.exp(f_last[(i, h)] + m_in[(i, h)] - m_nx)
        sl = jnp.exp(m_loc[(i, h)] - m_nx)
        c_in[(i + 1, h)] = sp * c_in[(i, h)] + sl * c_chunk[(i, h)]
        n_in[(i + 1, h)] = sp * n_in[(i, h)] + sl * jnp.sum(kw[(i, h)], axis=0, keepdims=True)
        m_in[(i + 1, h)] = m_nx
    s_new = [mem[(n_ch, h)] for h in hs]
    o_inter = {(i, h): mm_nt(qp[rs[i], s6[h]], mem[(i, h)]) for i, h in pairs}
    q_c = {(i, h): mm_nn(qm[h][rs[i]], c_in[(i, h)]) for i, h in pairs}
    scores = {ih: jnp.where(causal, a_fwd[ih], a_bwd[ih]) for ih in pairs}
    log_d = {(i, h): gates_t[h:h + 1, rs[i]] - jnp.abs(fc_c[(i, h)] - fc_t[4 + h:5 + h, rs[i]]) for i, h in pairs}
    g_int = {ih: fc_c[ih] + m_in[ih] for ih in pairs}
    m_t = {ih: jnp.maximum(g_int[ih], jnp.max(log_d[ih], axis=1, keepdims=True)) for ih in pairs}
    s = {ih: qk[ih] * jnp.exp(log_d[ih] - m_t[ih]) for ih in pairs}
    scl = {ih: jnp.exp(g_int[ih] - m_t[ih]) for ih in pairs}
    o = {(i, h): mm_nn(scores[(i, h)], v[rs[i], s12[h]]) + o_inter[(i, h)] for i, h in pairs}
    num = {(i, h): mm_nn(s[(i, h)], vm[h][rs[i]]) + scl[(i, h)] * q_c[(i, h)] for i, h in pairs}
    o = {ih: o[ih] * lax.rsqrt(_mean(o[ih] * o[ih]) + EPS) * p["ggla"] for ih in pairs}
    gate = g * _sigmoid(g)
    out_a = {(i, h): o[(i, h)] * gate[rs[i], s12[h]] for i, h in pairs}
    den = {(i, h): jnp.sum(s[(i, h)], axis=1, keepdims=True)
           + scl[(i, h)] * jnp.sum(qm[h][rs[i]] * n_in[(i, h)], axis=1, keepdims=True) for i, h in pairs}
    den = {ih: jnp.maximum(jnp.abs(den[ih]), jnp.exp(-m_t[ih])) for ih in pairs}
    open_gate = _sigmoid(opre)
    hc = {(i, h): num[(i, h)] / den[(i, h)] * open_gate[rs[i], s12[h]] for i, h in pairs}
    d0 = {ih: hc[ih] - _mean(hc[ih]) for ih in pairs}
    y = {ih: d0[ih] * lax.rsqrt(_mean(d0[ih] * d0[ih]) + EPS) for ih in pairs}
    skipped = p["skip"] * xc
    out_b = {(i, h): y[(i, h)] * p["gml"][:, s12[h]] + skipped[rs[i], s12[h]] for i, h in pairs}
    ab = jnp.concatenate([jnp.concatenate([out_a[(i, h)] for h in hs] + [out_b[(i, h)] for h in hs], axis=1) for i in cs],
                         axis=0)
    new = {"S": s_new, "C": [c_in[(n_ch, h)] for h in hs], "n": [n_in[(n_ch, h)] for h in hs],
           "m": [jnp.broadcast_to(m_in[(n_ch, h)], (1, ML_DH)) for h in hs]}
    return ab, new


_P_NAMES = ("wau", "bau", "ggla", "cw", "cb", "wq", "wk", "wv", "wif", "bif", "skip", "gml")
_P_SHAPES = {
    "wau": (128, 256), "bau": (1, 256), "ggla": (1, 128), "cw": (4, 512), "cb": (1, 512),
    "wq": (512, 128), "wk": (512, 128), "wv": (512, 128),
    "wif": (1536, 128), "bif": (1, 128), "skip": (1, 512), "gml": (1, 512),
}
_P_BLOCKDIAG = ("wq", "wk", "wv")
_S_NAMES = ("S", "C", "n", "m")
_S_SHAPES = {"S": (HEADS, GLA_DV, GLA_DK), "C": (HEADS, ML_DH, ML_DH), "n": (HEADS, 1, ML_DH), "m": (HEADS, 1, ML_DH)}


def _per_head(ref):
    return [ref[h] for h in range(HEADS)]


def _block_mask():
    r = lax.broadcasted_iota(jnp.int32, (128, 128), 0)
    c = lax.broadcasted_iota(jnp.int32, (128, 128), 1)
    same_block = (r >> 2) == (c >> 2)
    spread = jnp.logical_and(r < 4, (c & 3) == r)
    return same_block.astype(F32), spread.astype(F32)


def _expand_blockdiag(w_ref, dense_ref):
    same_block, spread = _block_mask()
    for h in range(HEADS):
        tiled = _pmm_nn(w_ref[h * 128:(h + 1) * 128, :], spread)
        dense_ref[h] = tiled * same_block


def _collect_blockdiag(ddense_ref, dw_ref):
    same_block, spread = _block_mask()
    for h in range(HEADS):
        dw_ref[h * 128:(h + 1) * 128, :] = lax.dot_general(
            ddense_ref[h] * same_block, spread, (((1,), (1,)), ((), ())), precision=lax.Precision.HIGHEST,
            preferred_element_type=F32)


def _const_spec(shape):
    zeros = (0,) * len(shape)
    return pl.BlockSpec(shape, lambda i: zeros)


def _split(refs, *counts):
    out, at = [], 0
    for c in counts:
        out.append(refs[at:at + c])
        at += c
    assert at == len(refs)
    return out


def _landed(lands, outs, flush_sems):
    return [pltpu.make_async_copy(lands[k], outs[k], flush_sems.at[k]) for k in range(len(outs))]


def _ride_done(rider, cond, outs, sems):
    if rider is None:
        return

    @pl.when(cond)
    def _():
        if hasattr(rider, "flush"):
            rider.flush_done(sems[:-3], outs, sems[-1])
        else:
            for cp in _landed(sems[:-3], outs, sems[-1]):
                cp.wait()


def _ride(rider, phases, cond, ins, outs, sems, flush_later=False):
    if rider is None or not any(hasattr(rider, phase) for phase in phases):
        return
    lands, (send_sems, recv_sems, flush_sems) = sems[:-3], sems[-3:]

    @pl.when(cond)
    def _():
        for phase in phases:
            if phase == "last" and hasattr(rider, "late"):
                rider.late(ins, lands, send_sems, recv_sems)
                rider.flush("late", lands, outs, flush_sems)
            getattr(rider, phase)(ins, lands, send_sems, recv_sems)
            if hasattr(rider, "flush"):
                rider.flush(phase, lands, outs, flush_sems, wait=not flush_later)
        if "last" in phases and not hasattr(rider, "flush"):
            flush = _landed(lands, outs, flush_sems)
            for cp in flush:
                cp.start()
            if not flush_later:
                for cp in flush:
                    cp.wait()


def _middle_step(rider, n_steps):
    return min(n_steps - 2, int(getattr(rider, "middle_at", 1.0) * n_steps))


def _rider_specs(rider, rider_ins):
    if rider is None:
        return [], [], [], []
    scratch = [pltpu.VMEM(s.shape, s.dtype) for s in list(rider.out_shape) + list(getattr(rider, "work_shape", ()))]
    scratch += [pltpu.SemaphoreType.DMA((rider.n_sems,)), pltpu.SemaphoreType.DMA((rider.n_sems,)),
                pltpu.SemaphoreType.DMA((getattr(rider, "n_flush", len(rider.out_shape)),))]
    in_space = getattr(rider, "in_space", VMEM_WHOLE)
    in_specs = list(in_space) if isinstance(in_space, (list, tuple)) else [in_space] * len(rider_ins)
    return in_specs, [ANY] * len(rider.out_shape), list(rider.out_shape), scratch


def _merge_tile(ab, gab_ref, x_ref, wpa_ref, wpb_ref, wo_ref, g_ref, x1_ref, mix_ref, mg_ref):
    a = ab[:, 0:512]
    b = ab[:, 512:1024]
    for j in range(N_CHIP):
        blk = slice(j * 256, (j + 1) * 256)
        ya = jnp.dot(a, wpa_ref[j], preferred_element_type=F32)
        yb = jnp.dot(b, wpb_ref[j], preferred_element_type=F32)
        sa = _sigmoid(gab_ref[:, j * 256:(j + 1) * 256])
        sb = _sigmoid(gab_ref[:, 1024 + j * 256:1024 + (j + 1) * 256])
        mg_ref[:, blk] = (sa * ya + sb * yb).astype(BF16)
    mix = jnp.dot(mg_ref[...], wo_ref[...], preferred_element_type=F32)
    mix_ref[...] = mix
    mn, _ = _rms_fwd(mix)
    x1_ref[...] = x_ref[...] + mn * g_ref[...]


def _mixer_fwd(pm, p, gab, x, w_pa4, w_pb4, w_o, g_post, rider=None, rider_ins=()):
    n_p = len(_P_NAMES)
    r_in, r_out_specs, r_out_shape, r_sems = _rider_specs(rider, rider_ins)

    def body(*refs):
        ((pm_ref, xprev_ref), p_list, merge_in, ride_in, (ab_ref,), merge_out, so_refs, ride_out, sc_refs, dense_list,
         sems) = _split(refs, 2, n_p, 6, len(r_in), 1, 3, 4, len(r_out_specs), 4, 3, len(r_sems))
        p_refs = dict(zip(_P_NAMES, p_list))
        dense = dict(zip(_P_BLOCKDIAG, dense_list))
        n = pl.program_id(0)
        _ride(rider, ("first",), n == 0, ride_in, ride_out, sems)
        _ride(rider, ("last",), n == N_SWEEP - 1, ride_in, ride_out, sems, flush_later=True)

        @pl.when(n == 0)
        def _():
            for r in sc_refs:
                r[...] = jnp.zeros_like(r)
            for nm in _P_BLOCKDIAG:
                _expand_blockdiag(p_refs[nm], dense[nm])

        st = {name: _per_head(r) for name, r in zip(_S_NAMES, sc_refs)}
        pv = {nm: (_per_head(dense[nm]) if nm in _P_BLOCKDIAG else p_refs[nm][...]) for nm in _P_NAMES}
        for name, r in zip(_S_NAMES, so_refs):
            for h in range(HEADS):
                r[0, h] = st[name][h]
        xprev8 = jnp.where(n > 0, xprev_ref[CHUNK - 8:CHUNK, :], 0.0)
        ab, st = _mixer_chunk(_PLAIN_OPS, pv, st, pm_ref[...], xprev8)
        ab = ab.astype(BF16)
        ab_ref[...] = ab
        for name, r in zip(_S_NAMES, sc_refs):
            for h in range(HEADS):
                r[h] = st[name][h]
        _merge_tile(ab, *merge_in, *merge_out)
        _ride(rider, ("middle",), n == _middle_step(rider, N_SWEEP), ride_in, ride_out, sems)
        _ride_done(rider, n == N_SWEEP - 1, ride_out, sems)

    rows = lambda width: pl.BlockSpec((SWEEP * CHUNK, width), lambda i: (i, 0))
    in_specs = [rows(PM_W), pl.BlockSpec((CHUNK, 512), lambda i: (jnp.maximum(SWEEP * i - 1, 0), PM_XM // 512))]
    in_specs += [_const_spec(_P_SHAPES[nm]) for nm in _P_NAMES]
    in_specs += [rows(GAB_W), rows(D_MODEL), _once((N_CHIP, 512, 256)), _once((N_CHIP, 512, 256)), _once((D_MODEL, D_MODEL)),
                 _once((1, D_MODEL))] + r_in
    out_specs = [rows(1024), rows(D_MODEL), rows(D_MODEL), rows(D_MODEL)]
    out_shape = [jax.ShapeDtypeStruct((SEQ, 1024), BF16), jax.ShapeDtypeStruct((SEQ, D_MODEL), F32),
                 jax.ShapeDtypeStruct((SEQ, D_MODEL), F32), jax.ShapeDtypeStruct((SEQ, D_MODEL), BF16)]
    for nm in _S_NAMES:
        shp = _S_SHAPES[nm]
        out_specs.append(pl.BlockSpec((1,) + shp, lambda i: (i, 0, 0, 0)))
        out_shape.append(jax.ShapeDtypeStruct((N_SWEEP,) + shp, F32))
    return pl.pallas_call(
        body, grid=(N_SWEEP,), in_specs=in_specs, out_specs=out_specs + r_out_specs, out_shape=out_shape + r_out_shape,
        scratch_shapes=[pltpu.VMEM(_S_SHAPES[nm], F32) for nm in _S_NAMES]
        + [pltpu.VMEM((HEADS, 128, 128), F32) for _ in _P_BLOCKDIAG] + r_sems,
        compiler_params=_params(("arbitrary",)), name="mixer_fwd",
    )(pm, pm, *[p[nm] for nm in _P_NAMES], gab, x, w_pa4, w_pb4, w_o, g_post, *rider_ins)


def _mixer_bwd(pm, dab, states, p, rider=None, rider_ins=()):
    n_p = len(_P_NAMES)
    r_in, r_out_specs, r_out_shape, r_sems = _rider_specs(rider, rider_ins)

    def body(*refs):
        ((pm_ref, xprev_ref, dab_ref), si_refs, p_list, ride_in, (dpm_ref,), dp_list, ride_out, ds_refs, (carry_ref,),
         dense_list, ddense_list, sems) = _split(refs, 3, 4, n_p, len(r_in), 1, n_p, len(r_out_specs), 4, 1, 3, 3, len(r_sems))
        p_refs = dict(zip(_P_NAMES, p_list))
        dp_refs = dict(zip(_P_NAMES, dp_list))
        dense = dict(zip(_P_BLOCKDIAG, dense_list))
        ddense = dict(zip(_P_BLOCKDIAG, ddense_list))
        i = pl.program_id(0)
        blk = N_SWEEP - 1 - i
        _ride(rider, ("first",), i == 0, ride_in, ride_out, sems)
        _ride(rider, ("last",), i == N_SWEEP - 1, ride_in, ride_out, sems, flush_later=True)

        @pl.when(i == 0)
        def _():
            for r in ds_refs:
                r[...] = jnp.zeros_like(r)
            for nm in _P_NAMES:
                if nm in _P_BLOCKDIAG:
                    ddense[nm][...] = jnp.zeros_like(ddense[nm])
                    _expand_blockdiag(p_refs[nm], dense[nm])
                else:
                    dp_refs[nm][...] = jnp.zeros_like(dp_refs[nm])
            carry_ref[...] = jnp.zeros_like(carry_ref)

        pv = {nm: (_per_head(dense[nm]) if nm in _P_BLOCKDIAG else p_refs[nm][...]) for nm in _P_NAMES}
        dst = {name: _per_head(r) for name, r in zip(_S_NAMES, ds_refs)}
        st = {name: [r[0, h] for h in range(HEADS)] for name, r in zip(_S_NAMES, si_refs)}
        xprev8 = jnp.where(blk > 0, xprev_ref[CHUNK - 8:CHUNK, :], 0.0)
        _, vjp = jax.vjp(functools.partial(_mixer_chunk, _VJP_OPS), pv, st, pm_ref[...], xprev8)
        dp_sum, dst, dpm, dxprev8 = vjp((dab_ref[...], dst))
        reach = jnp.concatenate([jnp.zeros((SWEEP * CHUNK - 8, 512), F32), carry_ref[...]], axis=0)
        dpm_ref[:, 0:PM_XM] = dpm[:, 0:PM_XM].astype(BF16)
        dpm_ref[:, PM_XM:PM_XM + 512] = (dpm[:, PM_XM:PM_XM + 512] + reach).astype(BF16)
        dpm_ref[:, PM_XM + 512:PM_W] = dpm[:, PM_XM + 512:PM_W].astype(BF16)
        carry_ref[...] = dxprev8
        for name, r in zip(_S_NAMES, ds_refs):
            for h in range(HEADS):
                r[h] = dst[name][h]
        for nm in _P_NAMES:
            if nm in _P_BLOCKDIAG:
                for h in range(HEADS):
                    ddense[nm][h] += dp_sum[nm][h]
            else:
                dp_refs[nm][...] += dp_sum[nm]

        @pl.when(i == N_SWEEP - 1)
        def _():
            for nm in _P_BLOCKDIAG:
                _collect_blockdiag(ddense[nm], dp_refs[nm])

        _ride(rider, ("early",), i == 1, ride_in, ride_out, sems)
        _ride(rider, ("middle",), i == _middle_step(rider, N_SWEEP), ride_in, ride_out, sems)
        _ride_done(rider, i == N_SWEEP - 1, ride_out, sems)

    rev = lambda i: (N_SWEEP - 1 - i, 0)
    in_specs = [pl.BlockSpec((SWEEP * CHUNK, PM_W), rev),
                pl.BlockSpec((CHUNK, 512), lambda i: (jnp.maximum(SWEEP * (N_SWEEP - 1 - i) - 1, 0), PM_XM // 512)),
                pl.BlockSpec((SWEEP * CHUNK, 1024), rev)]
    for nm in _S_NAMES:
        in_specs.append(pl.BlockSpec((1,) + _S_SHAPES[nm], lambda i: (N_SWEEP - 1 - i, 0, 0, 0)))
    in_specs += [_const_spec(_P_SHAPES[nm]) for nm in _P_NAMES] + r_in
    out_specs = [pl.BlockSpec((SWEEP * CHUNK, PM_W), rev)] + [_const_spec(_P_SHAPES[nm]) for nm in _P_NAMES]
    out_shape = [jax.ShapeDtypeStruct((SEQ, PM_W), BF16)] + [jax.ShapeDtypeStruct(_P_SHAPES[nm], F32) for nm in _P_NAMES]
    res = pl.pallas_call(
        body, grid=(N_SWEEP,), in_specs=in_specs, out_specs=out_specs + r_out_specs, out_shape=out_shape + r_out_shape,
        scratch_shapes=[pltpu.VMEM(_S_SHAPES[nm], F32) for nm in _S_NAMES] + [pltpu.VMEM((8, 512), F32)]
        + [pltpu.VMEM((HEADS, 128, 128), F32) for _ in range(2 * len(_P_BLOCKDIAG))] + r_sems,
        compiler_params=_params(("arbitrary",)), name="mixer_bwd",
    )(pm, pm, dab, *states, *[p[nm] for nm in _P_NAMES], *rider_ins)
    return res[0], dict(zip(_P_NAMES, res[1:1 + n_p])), res[1 + n_p:]


def _tok(width):
    return pl.BlockSpec((TOK_TILE, width), lambda i: (i, 0))


def _once(shape):
    zeros = (0,) * len(shape)
    return pl.BlockSpec(shape, lambda i: zeros, pipeline_mode=pl.Buffered(1))


def _rms_fwd(x):
    r = lax.rsqrt(_mean(x * x) + EPS)
    return x * r, r


def _rms_bwd(dy, xn, r, g):
    gd = dy * g
    return r * (gd - xn * _mean(xn * gd))


def _tiled_call(body, in_specs, out_specs, out_shape, args, name, rider=None, rider_ins=(), scratch=()):
    r_in, r_out_specs, r_out_shape, r_scratch = _rider_specs(rider, rider_ins)
    n_in, n_out = len(in_specs), len(out_specs)

    def hosted(*refs):
        ins, ride_in, outs, ride_out, own, r_scr = _split(refs, n_in, len(r_in), n_out, len(r_out_specs), len(scratch),
                                                          len(r_scratch))
        i = pl.program_id(0)
        _ride(rider, ("first",), i == 0, ride_in, ride_out, r_scr)
        body(*ins, *outs, *own)
        _ride(rider, ("early",), i == 1, ride_in, ride_out, r_scr)
        _ride(rider, ("middle",), i == _middle_step(rider, N_TOK_TILE), ride_in, ride_out, r_scr)
        _ride(rider, ("last",), i == N_TOK_TILE - 1, ride_in, ride_out, r_scr)

    res = pl.pallas_call(
        hosted, grid=(N_TOK_TILE,), in_specs=list(in_specs) + r_in, out_specs=list(out_specs) + r_out_specs,
        out_shape=list(out_shape) + r_out_shape, scratch_shapes=list(scratch) + r_scratch,
        compiler_params=_params(("arbitrary",)), name=name,
    )(*args, *rider_ins)
    return res[:n_out], res[n_out:]


def _join_rows(w4_ref, wt_ref):
    @pl.when(pl.program_id(0) == 0)
    def _():
        for j in range(N_CHIP):
            wt_ref[j * IN_SHARD:(j + 1) * IN_SHARD, :] = w4_ref[j]


def _joined_scratch():
    return [pltpu.VMEM((D_IN, D_MODEL), BF16)]


def _in_proj(x, g_pre, w4_in, rider=None, rider_ins=()):
    def body(x_ref, g_ref, w4_ref, pm_ref, gab_ref, h_ref, wt_ref):
        _join_rows(w4_ref, wt_ref)
        xn, _ = _rms_fwd(x_ref[...])
        h = (xn * g_ref[...]).astype(BF16)
        h_ref[...] = h
        pm_ref[:, 0:PM_XM] = _nt(h, wt_ref[0:IN_ALOW, :])
        pm_ref[:, PM_XM:PM_AL] = _nt(h, wt_ref[IN_XM:IN_GATES, :])
        pm_ref[:, PM_AL:PM_W] = _nt(h, wt_ref[IN_ALOW:IN_ALOW + 128, :])
        gab_ref[...] = _nt(h, wt_ref[IN_GATES:D_IN, :])

    return _tiled_call(
        body, [_tok(D_MODEL), _once((1, D_MODEL)), _once((N_CHIP, IN_SHARD, D_MODEL))],
        [_tok(PM_W), _tok(GAB_W), _tok(D_MODEL)],
        [jax.ShapeDtypeStruct((SEQ, PM_W), F32), jax.ShapeDtypeStruct((SEQ, GAB_W), F32),
         jax.ShapeDtypeStruct((SEQ, D_MODEL), BF16)], (x, g_pre, w4_in), "in_proj", rider, rider_ins, _joined_scratch())


def _mlp(x1, target, g_pre, g_post, w_up4, w_down_a4, w_down_b4):
    def body(x1_ref, t_ref, gpre_ref, gpost_ref, wup_ref, wda_ref, wdb_ref,
             dx1_ref, u_ref, dd_ref, h2_ref, dpre_ref, dgpost_ref, dgpre_ref, loss_ref):
        @pl.when(pl.program_id(0) == 0)
        def _():
            dgpost_ref[...] = jnp.zeros_like(dgpost_ref)
            dgpre_ref[...] = jnp.zeros_like(dgpre_ref)
            loss_ref[...] = jnp.zeros_like(loss_ref)

        x1 = x1_ref[...]
        gpre = gpre_ref[...]
        gpost = gpost_ref[...]
        xn2, r2 = _rms_fwd(x1)
        h2 = (xn2 * gpre).astype(BF16)
        h2_ref[...] = h2
        rl = []
        d = jnp.zeros((TOK_TILE, D_MODEL), F32)
        for j in range(N_CHIP):
            for half, wd_ref in enumerate((wda_ref, wdb_ref)):
                cols = slice(half * 512, (half + 1) * 512)
                r = jnp.maximum(jnp.dot(h2, wup_ref[j, :, cols], preferred_element_type=F32), 0.0)
                rl.append(r)
                u = (r * r).astype(BF16)
                u_ref[:, j * 1024 + half * 512:j * 1024 + (half + 1) * 512] = u
                d = d + jnp.dot(u, wd_ref[j], preferred_element_type=F32)
        dn, r3 = _rms_fwd(d)
        diff = x1 + dn * gpost - t_ref[...]
        loss_ref[...] += jnp.sum(diff * diff, keepdims=True) * (0.5 / D_MODEL)
        dy = diff * (1.0 / D_MODEL)
        dgpost_ref[...] += jnp.sum(dy * dn, axis=0, keepdims=True)
        dd = _rms_bwd(dy, dn, r3, gpost).astype(BF16)
        dd_ref[...] = dd
        dh2 = jnp.zeros((TOK_TILE, D_MODEL), F32)
        for j in range(N_CHIP):
            for half, wd_ref in enumerate((wda_ref, wdb_ref)):
                cols = slice(half * 512, (half + 1) * 512)
                dpre = (_nt(dd, wd_ref[j]) * (2.0 * rl[2 * j + half])).astype(BF16)
                dpre_ref[:, j * 1024 + half * 512:j * 1024 + (half + 1) * 512] = dpre
                dh2 = dh2 + _nt(dpre, wup_ref[j, :, cols])
        dgpre_ref[...] += jnp.sum(dh2 * xn2, axis=0, keepdims=True)
        dx1_ref[...] = dy + _rms_bwd(dh2, xn2, r2, gpre)

    acc = pl.BlockSpec((1, D_MODEL), lambda i: (0, 0))
    return pl.pallas_call(
        body, grid=(N_TOK_TILE,),
        in_specs=[_tok(D_MODEL), _tok(D_MODEL), _once((1, D_MODEL)), _once((1, D_MODEL)),
                  _once((N_CHIP, D_MODEL, 1024)), _once((N_CHIP, 512, D_MODEL)), _once((N_CHIP, 512, D_MODEL))],
        out_specs=[_tok(D_MODEL), _tok(D_FF), _tok(D_MODEL), _tok(D_MODEL), _tok(D_FF), acc, acc,
                   pl.BlockSpec((1, 128), lambda i: (0, 0))],
        out_shape=[jax.ShapeDtypeStruct((SEQ, D_MODEL), F32), jax.ShapeDtypeStruct((SEQ, D_FF), BF16),
                   jax.ShapeDtypeStruct((SEQ, D_MODEL), BF16), jax.ShapeDtypeStruct((SEQ, D_MODEL), BF16),
                   jax.ShapeDtypeStruct((SEQ, D_FF), BF16), jax.ShapeDtypeStruct((1, D_MODEL), F32),
                   jax.ShapeDtypeStruct((1, D_MODEL), F32), jax.ShapeDtypeStruct((1, 128), F32)],
        compiler_params=_params(("arbitrary",)), name="mlp_fwd_bwd",
    )(x1, target, g_pre, g_post, w_up4, w_down_a4, w_down_b4)


def _merge_bwd(dx1, mix, ab, gab, merged, w_pa4, w_pb4, w_o, g_post):
    def body(dx1_ref, mix_ref, ab_ref, gab_ref, mg_ref, wpa_ref, wpb_ref, wo_ref, g_ref,
             dgab_ref, dab_ref, dg_ref, dwpa_ref, dwpb_ref, dwo_ref, acc_pa, acc_pb, acc_o):
        @pl.when(pl.program_id(0) == 0)
        def _():
            dg_ref[...] = jnp.zeros_like(dg_ref)
            acc_pa[...] = jnp.zeros_like(acc_pa)
            acc_pb[...] = jnp.zeros_like(acc_pb)
            acc_o[...] = jnp.zeros_like(acc_o)

        dx1 = dx1_ref[...]
        mn, r = _rms_fwd(mix_ref[...])
        dg_ref[...] += jnp.sum(dx1 * mn, axis=0, keepdims=True)
        dmix = _rms_bwd(dx1, mn, r, g_ref[...]).astype(BF16)
        acc_o[...] += _tn(mg_ref[...], dmix)
        dmerged = _nt(dmix, wo_ref[...])
        a = ab_ref[:, 0:512]
        b = ab_ref[:, 512:1024]
        da = jnp.zeros((TOK_TILE, 512), F32)
        db = jnp.zeros((TOK_TILE, 512), F32)
        dyas, dybs = [], []
        for j in range(N_CHIP):
            blk = slice(j * 256, (j + 1) * 256)
            blk_b = slice(1024 + j * 256, 1024 + (j + 1) * 256)
            dm = dmerged[:, blk]
            ya = jnp.dot(a, wpa_ref[j], preferred_element_type=F32)
            yb = jnp.dot(b, wpb_ref[j], preferred_element_type=F32)
            sa = _sigmoid(gab_ref[:, blk])
            sb = _sigmoid(gab_ref[:, blk_b])
            dya = (dm * sa).astype(BF16)
            dyb = (dm * sb).astype(BF16)
            dyas.append(dya)
            dybs.append(dyb)
            dgab_ref[:, blk] = (dm * ya * sa * (1.0 - sa)).astype(BF16)
            dgab_ref[:, blk_b] = (dm * yb * sb * (1.0 - sb)).astype(BF16)
            da = da + _nt(dya, wpa_ref[j])
            db = db + _nt(dyb, wpb_ref[j])
        dab_ref[:, 0:512] = da
        dab_ref[:, 512:1024] = db
        acc_pa[...] += _tn(a, jnp.concatenate(dyas, axis=1))
        acc_pb[...] += _tn(b, jnp.concatenate(dybs, axis=1))

        @pl.when(pl.program_id(0) == N_TOK_TILE - 1)
        def _():
            dwo_ref[...] = acc_o[...].astype(BF16)
            for j in range(N_CHIP):
                dwpa_ref[j] = acc_pa[:, j * 256:(j + 1) * 256].astype(BF16)
                dwpb_ref[j] = acc_pb[:, j * 256:(j + 1) * 256].astype(BF16)

    whole = lambda shape: pl.BlockSpec(shape, lambda i: (0,) * len(shape))
    return pl.pallas_call(
        body, grid=(N_TOK_TILE,),
        in_specs=[_tok(D_MODEL), _tok(D_MODEL), _tok(1024), _tok(GAB_W), _tok(D_MODEL), _once((N_CHIP, 512, 256)),
                  _once((N_CHIP, 512, 256)), _once((D_MODEL, D_MODEL)), _once((1, D_MODEL))],
        out_specs=[_tok(GAB_W), _tok(1024), whole((1, D_MODEL)), whole((N_CHIP, 512, 256)), whole((N_CHIP, 512, 256)),
                   whole((D_MODEL, D_MODEL))],
        out_shape=[jax.ShapeDtypeStruct((SEQ, GAB_W), BF16), jax.ShapeDtypeStruct((SEQ, 1024), F32),
                   jax.ShapeDtypeStruct((1, D_MODEL), F32), jax.ShapeDtypeStruct((N_CHIP, 512, 256), BF16),
                   jax.ShapeDtypeStruct((N_CHIP, 512, 256), BF16), jax.ShapeDtypeStruct((D_MODEL, D_MODEL), BF16)],
        scratch_shapes=[pltpu.VMEM((512, D_MODEL), F32), pltpu.VMEM((512, D_MODEL), F32),
                        pltpu.VMEM((D_MODEL, D_MODEL), F32)],
        compiler_params=_params(("arbitrary",)), name="merge_bwd",
    )(dx1, mix, ab, gab, merged, w_pa4, w_pb4, w_o, g_post)


def _in_proj_bwd(dpm, dgab, x, dx1, g_pre, w4_in, rider=None, rider_ins=()):
    def body(dpm_ref, dgab_ref, x_ref, dx1_ref, g_ref, w4_ref, dx_ref, dg_ref, wt_ref):
        _join_rows(w4_ref, wt_ref)

        @pl.when(pl.program_id(0) == 0)
        def _():
            dg_ref[...] = jnp.zeros_like(dg_ref)

        dh = jnp.dot(dpm_ref[:, 0:PM_XM], wt_ref[0:IN_ALOW, :], preferred_element_type=F32)
        dh = dh + jnp.dot(dpm_ref[:, PM_XM:PM_AL], wt_ref[IN_XM:IN_GATES, :], preferred_element_type=F32)
        dh = dh + jnp.dot(dpm_ref[:, PM_AL:PM_W], wt_ref[IN_ALOW:IN_ALOW + 128, :], preferred_element_type=F32)
        dh = dh + jnp.dot(dgab_ref[...], wt_ref[IN_GATES:D_IN, :], preferred_element_type=F32)
        xn, r = _rms_fwd(x_ref[...])
        dg_ref[...] += jnp.sum(dh * xn, axis=0, keepdims=True)
        dx_ref[...] = dx1_ref[...] + _rms_bwd(dh, xn, r, g_ref[...])

    return _tiled_call(
        body, [_tok(PM_W), _tok(GAB_W), _tok(D_MODEL), _tok(D_MODEL), _once((1, D_MODEL)),
               _once((N_CHIP, IN_SHARD, D_MODEL))],
        [_tok(D_MODEL), pl.BlockSpec((1, D_MODEL), lambda i: (0, 0))],
        [jax.ShapeDtypeStruct((SEQ, D_MODEL), F32), jax.ShapeDtypeStruct((1, D_MODEL), F32)],
        (dpm, dgab, x, dx1, g_pre, w4_in), "in_proj_bwd", rider, rider_ins, _joined_scratch())


def _dw_in(dpm, dgab, h):
    n_pm = PM_AL // 512
    n_blk = n_pm + GAB_W // 512

    def place(o_ref, rows, lo, hi):
        for j in range(N_CHIP):
            a, b = max(lo, j * IN_SHARD), min(hi, (j + 1) * IN_SHARD)
            if a < b:
                o_ref[j, a - j * IN_SHARD:b - j * IN_SHARD, :] = rows(a - lo, b - lo)

    def body(dpm_ref, dgab_ref, dal_ref, h_ref, o_ref, blk_ref):
        i = pl.program_id(0)

        @pl.when(i < n_pm)
        def _():
            blk_ref[...] = _tn(dpm_ref[...], h_ref[...]).astype(BF16)

        @pl.when(i >= n_pm)
        def _():
            blk_ref[...] = _tn(dgab_ref[...], h_ref[...]).astype(BF16)

        for k in range(n_blk):
            off = k * 512 + (IN_XM - IN_ALOW) * (k >= IN_ALOW // 512)

            @pl.when(i == k)
            def _():
                place(o_ref, lambda a, b: blk_ref[a:b, :], off, off + 512)

        @pl.when(i == 0)
        def _():
            a_low = _tn(dal_ref[...], h_ref[...])[0:IN_XM - IN_ALOW].astype(BF16)
            place(o_ref, lambda a, b: a_low[a:b], IN_ALOW, IN_XM)

    return pl.pallas_call(
        body, grid=(n_blk,),
        in_specs=[pl.BlockSpec((SEQ, 512), lambda i: (0, jnp.minimum(i, n_pm - 1))),
                  pl.BlockSpec((SEQ, 512), lambda i: (0, jnp.maximum(i - n_pm, 0))),
                  pl.BlockSpec((SEQ, 128), lambda i: (0, PM_AL // 128)),
                  _once((SEQ, D_MODEL))],
        out_specs=pl.BlockSpec((N_CHIP, IN_SHARD, D_MODEL), lambda i: (0, 0, 0)),
        out_shape=jax.ShapeDtypeStruct((N_CHIP, IN_SHARD, D_MODEL), BF16),
        scratch_shapes=[pltpu.VMEM((512, D_MODEL), BF16)],
        compiler_params=_params(("arbitrary",)), name="dw_in",
    )(dpm, dgab, dpm, h)


def _tn_matmul(a, b, name, shards=1, tm=1024, rider=None, rider_ins=()):
    m, n = a.shape[1], b.shape[1]
    tm = min(tm, m)
    tn = n // shards if shards > 1 else min(n, 1024)
    steps_i, steps_j = m // tm, n // tn
    r_in, r_out_specs, r_out_shape, r_scratch = _rider_specs(rider, rider_ins)

    def body(*refs):
        (a_ref, b_ref), ride_in, (o_ref,), ride_out, scratch = _split(refs, 2, len(r_in), 1, len(r_out_specs), len(r_scratch))
        step = pl.program_id(0) * steps_j + pl.program_id(1)
        last = step == steps_i * steps_j - 1
        _ride(rider, ("first",), step == 0, ride_in, ride_out, scratch)
        _ride(rider, ("middle", "last"), last, ride_in, ride_out, scratch, flush_later=True)
        o_ref[...] = _tn(a_ref[...], b_ref[...]).astype(BF16)
        _ride_done(rider, last, ride_out, scratch)

    if shards > 1:
        out_spec = pl.BlockSpec((None, tm, tn), lambda i, j: (j, i, 0))
        out_shape = jax.ShapeDtypeStruct((shards, m, tn), BF16)
    else:
        out_spec = pl.BlockSpec((tm, tn), lambda i, j: (i, j))
        out_shape = jax.ShapeDtypeStruct((m, n), BF16)
    res = pl.pallas_call(
        body, grid=(steps_i, steps_j),
        in_specs=[pl.BlockSpec((SEQ, tm), lambda i, j: (0, i)), pl.BlockSpec((SEQ, tn), lambda i, j: (0, j))] + r_in,
        out_specs=[out_spec] + r_out_specs, out_shape=[out_shape] + r_out_shape, scratch_shapes=r_scratch,
        compiler_params=_params(("arbitrary", "arbitrary")), name=name,
    )(a, b, *rider_ins)
    return res[0] if rider is None else (res[0], res[1:])


MESH = pl.DeviceIdType.MESH
ANY = pl.BlockSpec(memory_space=pl.ANY)
VMEM_WHOLE = pl.BlockSpec(memory_space=pltpu.VMEM)

_BIG = ("w_in", "w_pa", "w_pb", "w_o", "w_up", "w_down")
_BIG_SHARD = {"w_in": (IN_SHARD, D_MODEL), "w_pa": (512, 256), "w_pb": (512, 256), "w_o": (256, D_MODEL),
              "w_up": (D_MODEL, 1024), "w_down": (1024, D_MODEL),
              "w_down_a": (512, D_MODEL), "w_down_b": (512, D_MODEL)}
_BIG_SPLIT = {"w_in": 1, "w_pa": 0, "w_pb": 0, "w_o": 0, "w_up": 0, "w_down": 0, "w_down_a": 0, "w_down_b": 0}


def _half(ref, e, name, lead=0, part=None):
    axis = _BIG_SPLIT[name]
    size = _BIG_SHARD[name][axis] // 2
    start = e * size
    if part is not None:
        size //= 2
        start = start + part * size
    start = pl.multiple_of(start, 128 if axis == 1 else 16)
    idx = [pl.ds(0, ref.shape[a]) for a in range(lead)]
    idx += [pl.ds(start, size), pl.ds(0, _BIG_SHARD[name][1])] if axis == 0 else [pl.ds(0, _BIG_SHARD[name][0]), pl.ds(start, size)]
    return ref.at[tuple(idx)]


def _half_shape(name):
    r, c = _BIG_SHARD[name]
    return (r // 2, c) if _BIG_SPLIT[name] == 0 else (r, c // 2)


def _remote(src, dst, send_sems, recv_sems, k, to):
    return pltpu.make_async_remote_copy(src_ref=src, dst_ref=dst, send_sem=send_sems.at[k], recv_sem=recv_sems.at[k],
                                        device_id=to, device_id_type=MESH)


def _mesh_place():
    x, y, c = lax.axis_index("x"), lax.axis_index("y"), lax.axis_index("c")
    return x, y, c, [(1 - x, y), (x, 1 - y), (1 - x, 1 - y)]


class _Gather:
    def __init__(self, names, small=(), middle_at=0.5):
        self.middle_at = middle_at
        self.names = tuple(names)
        self.nb = len(self.names)
        self.n = self.nb + len(small)
        self.n_sems = 8 * self.nb + 3 * len(small)
        self.n_flush = 6 * self.nb + len(small)
        self.out_shape = [jax.ShapeDtypeStruct((N_CHIP,) + _BIG_SHARD[nm], BF16) for nm in self.names]
        self.out_shape += [jax.ShapeDtypeStruct((N_CHIP,) + s.shape, s.dtype) for s in small]
        self.in_space = [self._in_spec(nm) for nm in self.names] + [VMEM_WHOLE] * len(small)

    @staticmethod
    def _in_spec(name):
        if name in ("w_down_a", "w_down_b"):
            half = 0 if name == "w_down_a" else 1
            return pl.BlockSpec(_BIG_SHARD[name], lambda *_: (half, 0), pipeline_mode=pl.Buffered(1))
        return VMEM_WHOLE

    def _copies(self, ins, outs, ss, rs, k):
        x, y, c, _ = _mesh_place()
        name = self.names[k]
        me, xn, yn, dg = 2 * x + y, 2 * (1 - x) + y, 2 * x + (1 - y), 2 * (1 - x) + (1 - y)
        to_x, to_y, sibling = (1 - x, y, c), (x, 1 - y, c), (x, y, 1 - c)

        def region(slot, e, part=None):
            return _half(outs[k].at[slot], e, name, part=part)

        def copy(pair, src, dst, to):
            return _remote(src, dst, ss, rs, 8 * k + pair, to)

        mine = region(me, c)
        sent = [copy(0, mine, mine, to_x), copy(1, mine, mine, to_y),
                copy(2, region(xn, c, 0), region(xn, c, 0), to_y), copy(3, region(yn, c, 1), region(yn, c, 1), to_x),
                copy(4, region(xn, c), region(xn, c), sibling), copy(5, region(yn, c), region(yn, c), sibling),
                copy(6, region(dg, c, 0), region(dg, c, 0), sibling), copy(7, region(dg, c, 1), region(dg, c, 1), sibling)]
        landing = [region(xn, c), region(yn, c), region(dg, c, 0), region(dg, c, 1),
                   region(xn, 1 - c), region(yn, 1 - c), region(dg, 1 - c, 0), region(dg, 1 - c, 1)]
        received = [copy(pair, dst, dst, sibling) for pair, dst in enumerate(landing)]
        return sent, received

    def _small(self, ins, outs, ss, rs, k, j, peer, slot, c):
        return _remote(ins[k], outs[k].at[slot], ss, rs, 8 * self.nb + 3 * (k - self.nb) + j, (*peer, c))

    def _leaving(self, lands, outs, fs):
        x, y, c, _ = _mesh_place()
        me, xn, yn, dg = 2 * x + y, 2 * (1 - x) + y, 2 * x + (1 - y), 2 * (1 - x) + (1 - y)

        def pieces(k):
            name = self.names[k]
            spots = [lambda r: r.at[me], lambda r: _half(r.at[xn], c, name), lambda r: _half(r.at[yn], c, name),
                     lambda r: _half(r.at[xn], 1 - c, name), lambda r: _half(r.at[yn], 1 - c, name), lambda r: r.at[dg]]
            return [pltpu.make_async_copy(spot(lands[k]), spot(outs[k]), fs.at[6 * k + t]) for t, spot in enumerate(spots)]

        small = [pltpu.make_async_copy(lands[k], outs[k], fs.at[6 * self.nb + k - self.nb]) for k in range(self.nb, self.n)]
        return [pieces(k) for k in range(self.nb)], small

    def flush(self, phase, lands, outs, fs, wait=True):
        big, small = self._leaving(lands, outs, fs)
        ready = {"first": (0,), "middle": (1, 2), "late": (3, 4), "last": (5,)}[phase]
        for cps in big:
            for t in ready:
                cps[t].start()
        if phase == "last":
            for cp in small:
                cp.start()
            if wait:
                self.flush_done(lands, outs, fs)

    def flush_done(self, lands, outs, fs):
        big, small = self._leaving(lands, outs, fs)
        for cp in [cp for cps in big for cp in cps] + small:
            cp.wait()

    def first(self, ins, outs, ss, rs):
        x, y, c, peers = _mesh_place()
        me = 2 * x + y
        for k in range(self.nb):
            outs[k][me] = ins[k][...].astype(BF16)
            sent, _ = self._copies(ins, outs, ss, rs, k)
            sent[0].start()
            sent[1].start()
        for k in range(self.nb, self.n):
            for j, peer in enumerate(peers):
                self._small(ins, outs, ss, rs, k, j, peer, me, c).start()
            outs[k][me] = ins[k][...]

    def middle(self, ins, outs, ss, rs):
        for k in range(self.nb):
            sent, received = self._copies(ins, outs, ss, rs, k)
            for pair in (0, 1):
                received[pair].wait_recv()
                sent[2 + pair].start()
                sent[4 + pair].start()

    def late(self, ins, outs, ss, rs):
        for k in range(self.nb):
            _, received = self._copies(ins, outs, ss, rs, k)
            for pair in (4, 5):
                received[pair].wait_recv()

    def last(self, ins, outs, ss, rs):
        x, y, c, peers = _mesh_place()
        for k in range(self.nb):
            sent, received = self._copies(ins, outs, ss, rs, k)
            for pair in (2, 3):
                received[pair].wait_recv()
                sent[4 + pair].start()
        for k in range(self.nb):
            sent, received = self._copies(ins, outs, ss, rs, k)
            for pair in (6, 7):
                received[pair].wait_recv()
            for cp in sent:
                cp.wait_send()
        for k in range(self.nb, self.n):
            for j, (px, py) in enumerate(peers):
                self._small(ins, outs, ss, rs, k, j, (px, py), 2 * px + py, c).wait_recv()
                self._small(ins, outs, ss, rs, k, j, (px, py), 2 * x + y, c).wait_send()


def _run_alone(rider, ins, name):
    r_in, r_out_specs, r_out_shape, r_scratch = _rider_specs(rider, ins)

    def body(*refs):
        ride_in, ride_out, scratch = _split(refs, len(r_in), len(r_out_specs), len(r_scratch))
        _ride(rider, ("first", "middle", "last"), pl.program_id(0) == 0, ride_in, ride_out, scratch)

    return pl.pallas_call(
        body, grid=(1,), in_specs=r_in, out_specs=r_out_specs, out_shape=r_out_shape, scratch_shapes=r_scratch,
        compiler_params=_params(("arbitrary",)), name=name,
    )(*ins)


class _Presum:
    in_space = ANY

    def __init__(self, names, base=0):
        self.names = tuple(names)
        self.n = len(self.names)
        self.base = base
        self.n_sems = 3 * self.n
        self.out_shape = [jax.ShapeDtypeStruct((N_CHIP,) + _half_shape(nm), BF16) for nm in self.names]
        self.work_shape = self.out_shape + self.out_shape

    def _stage(self, ins, bufs, ss, k, e, which):
        n = self.n
        return pltpu.make_async_copy(_half(ins[k], e, self.names[k], lead=1), bufs[which * n + k],
                                     ss.at[self.base + which * n + k])

    def _give(self, bufs, ss, rs, k, sibling):
        return _remote(bufs[self.n + k], bufs[k], ss, rs, self.base + k, sibling)

    def first(self, ins, bufs, ss, rs):
        x, y, c, _ = _mesh_place()
        for k in range(self.n):
            self._stage(ins, bufs, ss, k, 1 - c, 1).start()
        for k in range(self.n):
            self._stage(ins, bufs, ss, k, c, 2).start()
        for k in range(self.n):
            self._stage(ins, bufs, ss, k, 1 - c, 1).wait()
            self._give(bufs, ss, rs, k, (x, y, 1 - c)).start()

    def middle(self, ins, bufs, ss, rs):
        pass

    def last(self, ins, bufs, ss, rs):
        x, y, c, _ = _mesh_place()
        for k in range(self.n):
            self._give(bufs, ss, rs, k, (x, y, 1 - c)).wait_recv()
            self._stage(ins, bufs, ss, k, c, 2).wait()

            @pl.loop(0, N_CHIP)
            def _(j):
                bufs[k][j] = (bufs[k][j].astype(F32) + bufs[2 * self.n + k][j].astype(F32)).astype(BF16)
        for k in range(self.n):
            self._give(bufs, ss, rs, k, (x, y, 1 - c)).wait_send()


class _ReduceRelay:
    middle_at = 0.75

    def __init__(self, names, base=0):
        self.names = tuple(names)
        self.n = len(self.names)
        self.base = base
        self.n_sems = 6 * self.n
        self.out_shape = [jax.ShapeDtypeStruct((N_CHIP,) + _half_shape(nm), BF16) for nm in self.names]
        quarter = [jax.ShapeDtypeStruct(self._part_shape(nm), BF16) for nm in self.names]
        self.work_shape = quarter + quarter

    @staticmethod
    def _part_shape(name):
        r, c = _half_shape(name)
        return (r // 2, c) if _BIG_SPLIT[name] == 0 else (r, c // 2)

    def _part(self, ref, name, p):
        r, c = self._part_shape(name)
        return ref.at[pl.ds(p * r, r), pl.ds(0, c)] if _BIG_SPLIT[name] == 0 else ref.at[pl.ds(0, r), pl.ds(p * c, c)]

    def _copies(self, ins, bufs, ss, rs, k):
        x, y, c, _ = _mesh_place()
        name, n = self.names[k], self.n
        me, xn, yn, dg = 2 * x + y, 2 * (1 - x) + y, 2 * x + (1 - y), 2 * (1 - x) + (1 - y)
        to_x, to_y = (1 - x, y, c), (x, 1 - y, c)
        mine = lambda slot, p: self._part(ins[k].at[slot], name, p)
        slot = lambda s, p: self._part(bufs[k].at[s], name, p)
        from_x, from_y = bufs[n + k], bufs[2 * n + k]

        def copy(pair, src, dst, to):
            return _remote(src, dst, ss, rs, self.base + 6 * k + pair, to)

        sent = [copy(0, mine(dg, 0), from_x, to_x), copy(1, mine(dg, 1), from_y, to_y),
                copy(2, mine(xn, 0), slot(me, 0), to_x), copy(3, mine(yn, 1), slot(me, 1), to_y),
                copy(4, from_y, slot(me, 1), to_x), copy(5, from_x, slot(me, 0), to_y)]
        landing = [from_x, from_y, slot(xn, 0), slot(yn, 1), slot(xn, 1), slot(yn, 0)]
        received = [copy(pair, dst, dst, to_x) for pair, dst in enumerate(landing)]
        return sent, received

    def first(self, ins, bufs, ss, rs):
        x, y, c, _ = _mesh_place()
        me, dg = 2 * x + y, 2 * (1 - x) + (1 - y)
        for k in range(self.n):
            sent, _ = self._copies(ins, bufs, ss, rs, k)
            for pair in range(4):
                sent[pair].start()
        for k in range(self.n):
            bufs[k][me] = ins[k][me]
            bufs[k][dg] = jnp.zeros(_half_shape(self.names[k]), BF16)

    def middle(self, ins, bufs, ss, rs):
        x, y, c, _ = _mesh_place()
        xn, yn = 2 * (1 - x) + y, 2 * x + (1 - y)
        for k in range(self.n):
            sent, received = self._copies(ins, bufs, ss, rs, k)
            name, n = self.names[k], self.n
            for pair, buf, own in ((0, bufs[n + k], self._part(ins[k].at[yn], name, 0)),
                                   (1, bufs[2 * n + k], self._part(ins[k].at[xn], name, 1))):
                received[pair].wait_recv()
                buf[...] = (buf[...].astype(F32) + own[...].astype(F32)).astype(BF16)
            sent[5].start()
            sent[4].start()

    def last(self, ins, bufs, ss, rs):
        for k in range(self.n):
            sent, received = self._copies(ins, bufs, ss, rs, k)
            for pair in range(2, 6):
                received[pair].wait_recv()
            for cp in sent:
                cp.wait_send()


class _PresumThenRelay:
    in_space = ANY
    middle_at = _ReduceRelay.middle_at

    def __init__(self, names):
        self.relay = _ReduceRelay(names)
        self.pre = _Presum(names, base=self.relay.n_sems)
        self.n_sems = self.relay.n_sems + self.pre.n_sems
        self.out_shape = self.relay.out_shape
        self.work_shape = list(self.relay.work_shape) + list(self.pre.out_shape) + list(self.pre.work_shape)
        self.n_relay = len(self.relay.out_shape) + len(self.relay.work_shape)

    def first(self, ins, bufs, ss, rs):
        self.pre.first(ins, bufs[self.n_relay:], ss, rs)

    def early(self, ins, bufs, ss, rs):
        self.pre.last(ins, bufs[self.n_relay:], ss, rs)
        self.relay.first(bufs[self.n_relay:], bufs[:self.n_relay], ss, rs)

    def middle(self, ins, bufs, ss, rs):
        self.relay.middle(bufs[self.n_relay:], bufs[:self.n_relay], ss, rs)

    def last(self, ins, bufs, ss, rs):
        self.relay.last(bufs[self.n_relay:], bufs[:self.n_relay], ss, rs)


class _SendPartials:
    def __init__(self, names, small_shape=None):
        self.n = len(names)
        self.small = small_shape is not None
        self.n_sems = 3 * self.n + 7
        self.out_shape = [jax.ShapeDtypeStruct((N_CHIP,) + _half_shape(nm), BF16) for nm in names]
        if self.small:
            self.out_shape.append(jax.ShapeDtypeStruct((N_DEV,) + small_shape, F32))

    def _piece(self, ins, outs, ss, rs, k, j, peer, src_slot, dst_slot, c):
        return _remote(ins[k].at[src_slot], outs[k].at[dst_slot], ss, rs, 3 * k + j, (*peer, c))

    def _small(self, ins, outs, ss, rs, r, other, slot):
        return _remote(ins[self.n], outs[self.n].at[slot], ss, rs, 3 * self.n + r, other)

    @staticmethod
    def _others(x, y, c):
        return [(x, y, 1 - c), (1 - x, y, c), (1 - x, y, 1 - c), (x, 1 - y, c), (x, 1 - y, 1 - c),
                (1 - x, 1 - y, c), (1 - x, 1 - y, 1 - c)]

    def first(self, ins, outs, ss, rs, only=None):
        x, y, c, peers = _mesh_place()
        me = 2 * x + y
        which = range(self.n) if only is None else only
        for k in which:
            for j, (px, py) in enumerate(peers):
                self._piece(ins, outs, ss, rs, k, j, (px, py), 2 * px + py, me, c).start()
        if self.small:
            for r, other in enumerate(self._others(x, y, c)):
                self._small(ins, outs, ss, rs, r, other, 4 * x + 2 * y + c).start()
            outs[self.n][4 * x + 2 * y + c] = ins[self.n][...]
        for k in which:
            outs[k][me] = ins[k][me]

    def middle(self, ins, outs, ss, rs):
        pass

    def last(self, ins, outs, ss, rs):
        x, y, c, peers = _mesh_place()
        me = 2 * x + y
        for k in range(self.n):
            for j, (px, py) in enumerate(peers):
                self._piece(ins, outs, ss, rs, k, j, (px, py), me, 2 * px + py, c).wait_recv()
                self._piece(ins, outs, ss, rs, k, j, (px, py), 2 * px + py, me, c).wait_send()
        if self.small:
            for r, (px, py, pc) in enumerate(self._others(x, y, c)):
                self._small(ins, outs, ss, rs, r, (px, py, pc), 4 * px + 2 * py + pc).wait_recv()
                self._small(ins, outs, ss, rs, r, (px, py, pc), 4 * x + 2 * y + c).wait_send()


class _PresumThenSend:
    def __init__(self, names):
        self.send = _SendPartials(names)
        self.pre = _Presum(names[-1:], base=self.send.n_sems)
        self.n = self.send.n
        self.n_sems = self.send.n_sems + self.pre.n_sems
        self.out_shape = self.send.out_shape
        self.work_shape = list(self.pre.out_shape) + list(self.pre.work_shape)
        self.in_space = [VMEM_WHOLE] * (self.n - 1) + [ANY]

    def _partials(self, ins, bufs):
        return list(ins[:self.n - 1]) + [bufs[self.n]]

    def first(self, ins, bufs, ss, rs):
        self.pre.first(ins[self.n - 1:], bufs[self.n:], ss, rs)
        self.send.first(ins, bufs[:self.n], ss, rs, only=range(self.n - 1))

    def early(self, ins, bufs, ss, rs):
        self.pre.last(ins[self.n - 1:], bufs[self.n:], ss, rs)
        self.send.first(self._partials(ins, bufs), bufs[:self.n], ss, rs, only=(self.n - 1,))

    def middle(self, ins, bufs, ss, rs):
        pass

    def last(self, ins, bufs, ss, rs):
        self.send.last(self._partials(ins, bufs), bufs[:self.n], ss, rs)


def _sum_swap(names, parts, small):
    n = len(parts)
    everyone = _SendPartials((), small.shape)

    def body(*refs):
        (p_hbm, (small_ref,), o_hbm, (osmall_ref,), p_refs, o_refs, (all_ref,),
         (send_sems, recv_sems, ss_small, rs_small, load_sems, leave_sems)) = _split(refs, n, 1, n, 1, n, n, 1, 6)
        x, y, c = lax.axis_index("x"), lax.axis_index("y"), lax.axis_index("c")
        loads = [pltpu.make_async_copy(p_hbm[k], p_refs[k], load_sems.at[k]) for k in range(n)]
        for cp in loads:
            cp.start()
        everyone.first([small_ref], [all_ref], ss_small, rs_small)

        def mine(k):
            part = _half(o_refs[k], c, names[k])
            return _remote(part, part, send_sems, recv_sems, k, (x, y, 1 - c))

        def leave(k, whose):
            e = c if whose == 0 else 1 - c
            return pltpu.make_async_copy(_half(o_refs[k], e, names[k]), _half(o_hbm[k], e, names[k]),
                                         leave_sems.at[2 * k + whose])

        for k in range(n):
            loads[k].wait()
            for e in range(2):
                @pl.when(c == e)
                def _():
                    g = p_refs[k][0].astype(F32)
                    for s in range(1, N_CHIP):
                        g = g + p_refs[k][s].astype(F32)
                    r, cols = _half_shape(names[k])
                    if _BIG_SPLIT[names[k]] == 0:
                        o_refs[k][e * r:(e + 1) * r, :] = g
                    else:
                        o_refs[k][:, e * cols:(e + 1) * cols] = g
            mine(k).start()
            leave(k, 0).start()
        for k in range(n):
            theirs = _half(o_refs[k], 1 - c, names[k])
            _remote(theirs, theirs, send_sems, recv_sems, k, (x, y, 1 - c)).wait_recv()
            leave(k, 1).start()
        everyone.last([small_ref], [all_ref], ss_small, rs_small)
        g = all_ref[0]
        for d in range(1, N_DEV):
            g = g + all_ref[d]
        osmall_ref[...] = g
        for k in range(n):
            mine(k).wait_send()
            leave(k, 0).wait()
            leave(k, 1).wait()

    shards = [jax.ShapeDtypeStruct(_BIG_SHARD[nm], F32) for nm in names]
    res = pl.pallas_call(
        body, in_specs=[ANY] * n + [VMEM_WHOLE], out_specs=[ANY] * n + [VMEM_WHOLE],
        out_shape=shards + [jax.ShapeDtypeStruct(small.shape, F32)],
        scratch_shapes=[pltpu.VMEM(q.shape, q.dtype) for q in parts] + [pltpu.VMEM(s.shape, s.dtype) for s in shards]
        + [pltpu.VMEM((N_DEV,) + small.shape, F32), pltpu.SemaphoreType.DMA((n,)), pltpu.SemaphoreType.DMA((n,)),
           pltpu.SemaphoreType.DMA((everyone.n_sems,)), pltpu.SemaphoreType.DMA((everyone.n_sems,)),
           pltpu.SemaphoreType.DMA((n,)), pltpu.SemaphoreType.DMA((2 * n,))],
        compiler_params=_params(), name="sum_swap",
    )(*parts, small)
    return res[:n], res[n]


def _adamw_math(w, g, m, v):
    m = ADAM_B1 * m + (1.0 - ADAM_B1) * g
    v = ADAM_B2 * v + (1.0 - ADAM_B2) * (g * g)
    m_hat = m / (1.0 - ADAM_B1 ** ADAM_STEP)
    v_hat = v / (1.0 - ADAM_B2 ** ADAM_STEP)
    delta = -ADAM_LR * (m_hat / (jnp.sqrt(v_hat) + ADAM_EPS) + ADAM_WD * w)
    return delta, m, v


ADAMW_STEPS = 8


def _adamw_big(gs, ws, ms, vs, name):
    n = len(ws)

    def body(*refs):
        for k in range(n):
            g_ref, w_ref, m_ref, v_ref = refs[4 * k:4 * k + 4]
            g_out_ref, d_ref, nm_ref, nv_ref = refs[4 * (n + k):4 * (n + k) + 4]
            g = g_ref[...].reshape(w_ref.shape)
            g_out_ref[...] = g
            d_ref[...], nm_ref[...], nv_ref[...] = _adamw_math(w_ref[...], g, m_ref[...], v_ref[...])

    in_specs, out_specs, shapes, args = [], [], [], []
    for g, w, m, v in zip(gs, ws, ms, vs):
        tr = -(-w.shape[0] // (8 * ADAMW_STEPS)) * 8
        zeros = (0,) * (w.ndim - 1)
        blk = pl.BlockSpec((tr,) + w.shape[1:], lambda i, zeros=zeros: (i,) + zeros)
        in_specs += [pl.BlockSpec((tr,) + g.shape[1:], lambda i: (i, 0)), blk, blk, blk]
        out_specs += [blk] * 4
        shapes += [jax.ShapeDtypeStruct(w.shape, F32)] * 4
        args += [g, w, m, v]
    res = pl.pallas_call(
        body, grid=(ADAMW_STEPS,), in_specs=in_specs, out_specs=out_specs, out_shape=shapes,
        compiler_params=_params(("arbitrary",)), name=name,
    )(*args)
    return [tuple(res[4 * k:4 * k + 4]) for k in range(n)]


def _adamw_small(ws, gs, ms, vs):
    n = len(ws)

    def body(*refs):
        w_refs, g_refs, m_refs, v_refs, d_refs, nm_refs, nv_refs = _split(refs, *([n] * 7))
        for k in range(n):
            d_refs[k][...], nm_refs[k][...], nv_refs[k][...] = _adamw_math(w_refs[k][...], g_refs[k][...], m_refs[k][...],
                                                                             v_refs[k][...])

    shapes = [jax.ShapeDtypeStruct(w.shape, F32) for w in ws]
    res = pl.pallas_call(body, out_shape=shapes * 3, name="adamw_small")(*ws, *gs, *ms, *vs)
    return res[:n], res[n:2 * n], res[2 * n:]


def _pack(arrs):
    flat = jnp.concatenate([a.reshape(-1) for a in arrs])
    rows = -(-flat.shape[0] // 1024) * 8
    return jnp.pad(flat, (0, rows * 128 - flat.shape[0])).reshape(rows, 128)


def _unpack(buf, shapes):
    flat = buf.reshape(-1)
    out, off = [], 0
    for s in shapes:
        size = 1
        for d in s:
            size *= d
        out.append(flat[off:off + size].reshape(s))
        off += size
    return out


def _block_rows(w):
    return jnp.pad(w.reshape(512, 4), ((0, 0), (0, 124)))


def _block_stored(dw):
    return jnp.transpose(dw[:, 0:4].reshape(128, 4, 4), (1, 2, 0)).reshape(16, 128)


def _cols(a4):
    return jnp.transpose(a4, (1, 0, 2)).reshape(a4.shape[1], -1)


_LATE = ("w_pa", "w_pb", "w_o", "w_up", "w_down")
_RIDE_IN_PROJ = ("w_pa", "w_pb", "w_o", "w_down_b")
_RIDE_MIXER = ("w_up", "w_down_a")


def _full_weights(gathered):
    joined = {"w_o": (D_MODEL, D_MODEL)}
    return {n: (a.reshape(joined[n]) if n in joined else a) for n, a in gathered.items()}


def _local_step(x, target, w, sp, late_shards=None):
    sp = {n: (a.reshape(1, -1) if a.ndim == 1 else a) for n, a in sp.items()}
    wau = jnp.pad(sp["w_a_up"], ((0, 112), (0, 0)))
    wif = jnp.pad(sp["w_if"], ((0, 0), (0, 120)))
    bif = jnp.pad(sp["b_if"], ((0, 0), (0, 120)))
    p = {"wau": wau, "bau": sp["b_a_up"], "ggla": sp["g_gla_norm"], "cw": sp["conv_w"], "cb": sp["conv_b"],
         "wq": _block_rows(sp["w_q_ml"]), "wk": _block_rows(sp["w_k_ml"]), "wv": _block_rows(sp["w_v_ml"]),
         "wif": wif, "bif": bif, "skip": sp["ml_skip"], "gml": sp["g_ml_norm"]}

    if late_shards is None:
        (pm, gab, h), _ = _in_proj(x, sp["g_pre_mix"], w["w_in"])
        ab, x1, mix, merged, *states = _mixer_fwd(pm, p, gab, x, w["w_pa"], w["w_pb"], w["w_o"], sp["g_post_mix"])
    else:
        shard = dict(zip(_LATE, late_shards))
        shard["w_down_a"] = shard["w_down_b"] = shard["w_down"]
        (pm, gab, h), got = _in_proj(x, sp["g_pre_mix"], w["w_in"], _Gather(_RIDE_IN_PROJ, middle_at=0.7),
                                     [shard[n] for n in _RIDE_IN_PROJ])
        w = dict(w, **_full_weights(dict(zip(_RIDE_IN_PROJ, got))))
        ab, x1, mix, merged, *rest = _mixer_fwd(pm, p, gab, x, w["w_pa"], w["w_pb"], w["w_o"], sp["g_post_mix"],
                                                _Gather(_RIDE_MIXER, middle_at=0.62), [shard[n] for n in _RIDE_MIXER])
        states = rest[:4]
        w.update(_full_weights(dict(zip(_RIDE_MIXER, rest[4:]))))
    dx1, u, dd, h2, dpre, dg_post_mlp, dg_pre_mlp, loss = _mlp(x1, target, sp["g_pre_mlp"], sp["g_post_mlp"],
                                                                w["w_up"], w["w_down_a"], w["w_down_b"])
    dgab, dab, dg_post_mix, dw_pa, dw_pb, dw_o = _merge_bwd(dx1, mix, ab, gab, merged, w["w_pa"], w["w_pb"], w["w_o"],
                                                            sp["g_post_mix"])
    big = {"w_pa": dw_pa, "w_pb": dw_pb, "w_o": dw_o, "w_up": _tn_matmul(h2, dpre, "dw_up", shards=N_CHIP)}
    if late_shards is None:
        big["w_down"] = _tn_matmul(u, dd, "dw_down")
        dpm, dp, _ = _mixer_bwd(pm, dab, states, p)
    else:
        pieces = lambda n: big[n].reshape((N_CHIP,) + _BIG_SHARD[n])
        big["w_down"], partial = _tn_matmul(u, dd, "dw_down", rider=_Presum(_LATE[:4]),
                                            rider_ins=[pieces(n) for n in _LATE[:4]])
        dpm, dp, parts = _mixer_bwd(pm, dab, states, p, _PresumThenSend(_LATE), list(partial) + [pieces("w_down")])
        big = dict(zip(_LATE, parts))
    big["w_in"] = _dw_in(dpm, dgab, h)
    if late_shards is None:
        (dx, dg_pre_mix), _ = _in_proj_bwd(dpm, dgab, x, dx1, sp["g_pre_mix"], w["w_in"])
    else:
        (dx, dg_pre_mix), parts = _in_proj_bwd(dpm, dgab, x, dx1, sp["g_pre_mix"], w["w_in"], _PresumThenRelay(("w_in",)),
                                               [big["w_in"]])
        big["w_in"] = parts[0]
    small = {
        "g_pre_mix": dg_pre_mix, "b_a_up": dp["bau"], "g_gla_norm": dp["ggla"], "conv_b": dp["cb"],
        "w_q_ml": _block_stored(dp["wq"]), "w_k_ml": _block_stored(dp["wk"]), "w_v_ml": _block_stored(dp["wv"]),
        "w_if": dp["wif"][:, 0:8].T,
        "b_if": dp["bif"][:, 0:8], "ml_skip": dp["skip"], "g_ml_norm": dp["gml"], "g_post_mix": dg_post_mix,
        "g_pre_mlp": dg_pre_mlp, "g_post_mlp": dg_post_mlp, "w_a_up": dp["wau"][0:16], "conv_w": dp["cw"],
        "loss": loss[:, 0:1],
    }
    return dx, big, small


_SMALL_REPL = ("g_pre_mix", "b_a_up", "g_gla_norm", "conv_b", "w_q_ml", "w_k_ml", "w_v_ml", "b_if", "ml_skip",
               "g_ml_norm", "g_post_mix", "g_pre_mlp", "g_post_mlp")
_SMALL_SHARDED = ("w_a_up", "conv_w", "w_if")
_SMALL_ORDER = _SMALL_REPL + _SMALL_SHARDED + ("loss",)
_WEIGHTS = ("g_pre_mix", "w_in", "w_a_up", "b_a_up", "g_gla_norm", "conv_w", "conv_b", "w_q_ml", "w_k_ml", "w_v_ml",
            "w_if", "b_if", "ml_skip", "g_ml_norm", "w_pa", "w_pb", "w_o", "g_post_mix", "g_pre_mlp", "w_up", "w_down",
            "g_post_mlp")


_BLOCK_WEIGHTS = ("w_q_ml", "w_k_ml", "w_v_ml")


def _stored(name, a):
    if name in _BLOCK_WEIGHTS:
        return jnp.transpose(a, (0, 2, 3, 1)).reshape(16, 128)
    if name == "w_if":
        return jnp.transpose(a, (0, 2, 1)).reshape(8, 384)
    return a


def _unstored(name, a):
    if name in _BLOCK_WEIGHTS:
        return jnp.transpose(a.reshape(1, 4, 4, 128), (0, 3, 1, 2))
    if name == "w_if":
        return jnp.transpose(a.reshape(1, 8, 384), (0, 2, 1))
    return a


def _as_shard(name, a):
    return jnp.transpose(a, (2, 0, 1)).reshape(IN_SHARD, D_MODEL // 128, 128) if name == "w_in" else a[0]


def _in_shard_bf16(w_in):
    return jnp.transpose(w_in.astype(BF16), (2, 0, 1)).reshape(IN_SHARD, D_MODEL)


def _from_shard(name, a):
    return jnp.transpose(a, (1, 2, 0)).reshape(1, D_MODEL, IN_SHARD) if name == "w_in" else a[None]


def kernel(x, g_pre_mix, w_in, w_a_up, b_a_up, g_gla_norm, conv_w, conv_b, w_q_ml, w_k_ml, w_v_ml, w_if, b_if, ml_skip, g_ml_norm, w_pa, w_pb, w_o, g_post_mix, g_pre_mlp, w_up, w_down, g_post_mlp, loss_target, m_g_pre_mix, m_w_in, m_w_a_up, m_b_a_up, m_g_gla_norm, m_conv_w, m_conv_b, m_w_q_ml, m_w_k_ml, m_w_v_ml, m_w_if, m_b_if, m_ml_skip, m_g_ml_norm, m_w_pa, m_w_pb, m_w_o, m_g_post_mix, m_g_pre_mlp, m_w_up, m_w_down, m_g_post_mlp, v_g_pre_mix, v_w_in, v_w_a_up, v_b_a_up, v_g_gla_norm, v_conv_w, v_conv_b, v_w_q_ml, v_w_k_ml, v_w_v_ml, v_w_if, v_b_if, v_ml_skip, v_g_ml_norm, v_w_pa, v_w_pb, v_w_o, v_g_post_mix, v_g_pre_mlp, v_w_up, v_w_down, v_g_post_mlp):
    args = dict(locals())
    wts = {n: _as_shard(n, args[n]) for n in _WEIGHTS}
    mom = {n: _as_shard(n, args["m_" + n]) for n in _WEIGHTS}
    var = {n: _as_shard(n, args["v_" + n]) for n in _WEIGHTS}
    chip = 2 * lax.axis_index("x") + lax.axis_index("y")

    first = ("w_in",) + _SMALL_SHARDED
    gathered = dict(zip(first, _run_alone(_Gather(("w_in",), [wts[n] for n in _SMALL_SHARDED]),
                                          [_in_shard_bf16(w_in)] + [wts[n] for n in _SMALL_SHARDED],
                                          "gather_first")))
    sp = {n: wts[n] for n in _SMALL_REPL}
    sp["w_a_up"] = _cols(gathered["w_a_up"])
    sp["conv_w"] = _cols(gathered["conv_w"])
    sp["w_if"] = gathered["w_if"].reshape(1536, 8)

    dx, big, small = _local_step(x[0], loss_target[0], _full_weights({"w_in": gathered["w_in"]}), sp,
                                 late_shards=[wts[n] for n in _LATE])

    small_shapes = [small[n].shape for n in _SMALL_ORDER]
    packed = _pack([small[n] for n in _SMALL_ORDER])
    sums, small_sum = _sum_swap(_BIG, [big[n] for n in _BIG], packed)

    grads, delta, new_m, new_v = {}, {}, {}, {}
    updated = dict(zip(_BIG, _adamw_big(sums, [wts[n] for n in _BIG], [mom[n] for n in _BIG], [var[n] for n in _BIG],
                                        "adamw_big")))
    for n in _BIG:
        grads[n], delta[n], new_m[n], new_v[n] = (_from_shard(n, a) for a in updated[n])
    summed = dict(zip(_SMALL_ORDER, _unpack(small_sum, small_shapes)))
    loss = summed["loss"].reshape(())
    summed["w_a_up"] = lax.dynamic_slice_in_dim(summed["w_a_up"], chip * 64, 64, axis=1)
    summed["conv_w"] = lax.dynamic_slice_in_dim(summed["conv_w"], chip * 128, 128, axis=1)
    summed["w_if"] = lax.dynamic_slice_in_dim(summed["w_if"], chip * 384, 384, axis=1)
    small_names = _SMALL_REPL + _SMALL_SHARDED
    came_stored = _BLOCK_WEIGHTS + ("w_if",)
    g_stored = [summed[n] if n in came_stored else _stored(n, summed[n].reshape(args[n].shape)) for n in small_names]
    upd = _adamw_small([_stored(n, args[n]) for n in small_names], g_stored,
                       [_stored(n, args["m_" + n]) for n in small_names], [_stored(n, args["v_" + n]) for n in small_names])
    for dst, arrs in zip((grads, delta, new_m, new_v), (g_stored,) + tuple(upd)):
        dst.update({n: _unstored(n, a) for n, a in zip(small_names, arrs)})

    outs = [loss, dx[None]]
    for group in (grads, delta, new_m, new_v):
        outs += [group[n] for n in _WEIGHTS]
    return tuple(outs)
```

```python
import functools

import jax
import jax.numpy as jnp
from jax import lax
from jax.experimental import pallas as pl
from jax.experimental.pallas import tpu as pltpu

F32 = jnp.float32
BF16 = jnp.bfloat16

SEQ = 2048
D_MODEL = 1024
CHUNK = 64
N_CHUNK = SEQ // CHUNK
HEADS = 4
GLA_DK = 64
GLA_DV = 128
ML_DH = 128
D_FF = 4096
EPS = 1e-6
N_CHIP = 4
N_DEV = 8
TOK_TILE = 256
N_TOK_TILE = SEQ // TOK_TILE
SWEEP = 2
assert CHUNK == 64
N_SWEEP = N_CHUNK // SWEEP

PM_W = 2688
PM_XM = 1536
PM_OP = 2048
PM_AL = 2560
GAB_W = 2048
D_IN = 4624
IN_SHARD = D_IN // N_CHIP
IN_ALOW = 1536
IN_XM = 1552
IN_GATES = 2576

ADAM_LR = 0.001
ADAM_B1 = 0.9
ADAM_B2 = 0.999
ADAM_EPS = 1e-08
ADAM_WD = 0.01
ADAM_STEP = 10

VMEM_LIMIT = 56 * 1024 * 1024


def _params(sem=None):
    return pltpu.CompilerParams(dimension_semantics=sem, vmem_limit_bytes=VMEM_LIMIT)


def _dot(a, b, ca, cb):
    return lax.dot_general(a.astype(BF16), b.astype(BF16), (((ca,), (cb,)), ((), ())), preferred_element_type=F32)


def _pmm_nn(a, b):
    return _dot(a, b, 1, 0)


def _pmm_nt(a, b):
    return _dot(a, b, 1, 1)


def _pmm_tn(a, b):
    return _dot(a, b, 0, 0)


def _pcmm(c, x):
    return lax.dot_general(c, x, (((1,), (0,)), ((), ())), precision=lax.Precision.HIGHEST, preferred_element_type=F32)


@jax.custom_vjp
def _mm_nn(a, b):
    return _dot(a, b, 1, 0)


@jax.custom_vjp
def _mm_nt(a, b):
    return _dot(a, b, 1, 1)


@jax.custom_vjp
def _mm_tn(a, b):
    return _dot(a, b, 0, 0)


_mm_nn.defvjp(lambda a, b: (_dot(a, b, 1, 0), (a, b)), lambda r, g: (_mm_nt(g, r[1]), _mm_tn(r[0], g)))
_mm_nt.defvjp(lambda a, b: (_dot(a, b, 1, 1), (a, b)), lambda r, g: (_mm_nn(g, r[1]), _mm_tn(g, r[0])))
_mm_tn.defvjp(lambda a, b: (_dot(a, b, 0, 0), (a, b)), lambda r, g: (_mm_nt(r[1], g), _mm_nn(r[0], g)))


@jax.custom_vjp
def _cmm(c, x):
    return _pcmm(c, x)


_cmm.defvjp(
    lambda c, x: (_pcmm(c, x), c),
    lambda c, g: (jnp.zeros_like(c), lax.dot_general(c, g, (((0,), (0,)), ((), ())), precision=lax.Precision.HIGHEST,
                                                      preferred_element_type=F32)),
)

_PLAIN_OPS = (_pmm_nn, _pmm_nt, _pmm_tn, _pcmm)
_VJP_OPS = (_mm_nn, _mm_nt, _mm_tn, _cmm)


def _sigmoid(x):
    return 0.5 * (jnp.tanh(0.5 * x) + 1.0)


def _log_sigmoid(x):
    return jnp.minimum(x, 0.0) - jnp.log(1.0 + jnp.exp(-jnp.abs(x)))


def _mean(x):
    return jnp.mean(x, axis=-1, keepdims=True)


def _nt(a, b):
    return lax.dot_general(a, b, (((1,), (1,)), ((), ())), preferred_element_type=F32)


def _tn(a, b):
    return lax.dot_general(a, b, (((0,), (0,)), ((), ())), preferred_element_type=F32)


def _mixer_chunk(ops, p, st, pm, xprev8):
    mm_nn, mm_nt, mm_tn, cmm = ops
    n_rows = pm.shape[0]
    n_ch = n_rows // CHUNK
    row = lax.broadcasted_iota(jnp.int32, (n_rows, n_rows), 0)
    col = lax.broadcasted_iota(jnp.int32, (n_rows, n_rows), 1)
    tri = jnp.logical_and((row >> 6) == (col >> 6), row >= col).astype(F32)
    causal = tri[0:CHUNK, 0:CHUNK] > 0.0
    q = pm[:, 0:256]
    k = pm[:, 256:512]
    v = pm[:, 512:1024]
    g = pm[:, 1024:1536]
    xm = pm[:, PM_XM:PM_XM + 512]
    opre = pm[:, PM_OP:PM_OP + 512]
    alow = pm[:, PM_AL:PM_AL + 128]
    hs = range(HEADS)
    cs = range(n_ch)
    pairs = [(i, h) for i in cs for h in hs]
    rs = [slice(i * CHUNK, (i + 1) * CHUNK) for i in cs]
    last = [slice((i + 1) * CHUNK - 1, (i + 1) * CHUNK) for i in cs]
    s6 = [slice(h * GLA_DK, (h + 1) * GLA_DK) for h in hs]
    s12 = [slice(h * 128, (h + 1) * 128) for h in hs]

    xx = jnp.concatenate([xprev8, xm], axis=0)
    pre = p["cb"]
    for j in range(4):
        pre = pre + p["cw"][j:j + 1, :] * xx[5 + j:5 + j + n_rows, :]
    xc = pre * _sigmoid(pre)
    qm = [mm_nn(xc[:, s12[h]], p["wq"][h]) for h in hs]
    km = [mm_nn(xc[:, s12[h]], p["wk"][h]) for h in hs]
    vm = [mm_nn(xm[:, s12[h]], p["wv"][h]) for h in hs]
    qcat = jnp.concatenate(qm, axis=1)
    kcat = jnp.concatenate(km, axis=1)
    vcat = jnp.concatenate(vm, axis=1)
    gates = (mm_nn(qcat, p["wif"][0:512]) + mm_nn(kcat, p["wif"][512:1024]) + mm_nn(vcat, p["wif"][1024:1536])
             + p["bif"])
    lf = _log_sigmoid(gates)
    fc = cmm(tri, lf)
    gates_t = gates.T
    fc_t = fc.T

    la = _log_sigmoid(mm_nn(alow, p["wau"]) + p["bau"]) * (1.0 / 16.0)
    cum = cmm(tri, la)
    cum_last = [cum[last[i], :] for i in cs]
    to_end = jnp.concatenate([cum_last[i] - cum[rs[i], :] for i in cs], axis=0)
    e_pos = jnp.exp(cum)
    e_neg = jnp.exp(-cum)
    qs = q * (GLA_DK ** -0.5)
    qp = qs * e_pos
    qn = qs * e_neg
    kp = k * e_pos
    kn = k * e_neg
    kl = k * jnp.exp(to_end)
    dec = [jnp.exp(cum_last[i]) for i in cs]
    ks = [km[h] * (ML_DH ** -0.5) for h in hs]
    li_c = {(i, h): gates[rs[i], h:h + 1] for i, h in pairs}
    fc_c = {(i, h): fc[rs[i], 4 + h:5 + h] for i, h in pairs}
    f_last = {(i, h): fc[last[i], 4 + h:5 + h] for i, h in pairs}

    a_fwd = {(i, h): mm_nt(qp[rs[i], s6[h]], kn[rs[i], s6[h]]) for i, h in pairs}
    a_bwd = {(i, h): mm_nt(qn[rs[i], s6[h]], kp[rs[i], s6[h]]) for i, h in pairs}
    s_chunk = {(i, h): mm_tn(v[rs[i], s12[h]], kl[rs[i], s6[h]]) for i, h in pairs}
    qk = {(i, h): mm_nt(qm[h][rs[i]], ks[h][rs[i]]) for i, h in pairs}
    a = {ih: f_last[ih] - fc_c[ih] + li_c[ih] for ih in pairs}
    m_loc = {ih: jnp.max(a[ih], axis=0, keepdims=True) for ih in pairs}
    kw = {(i, h): ks[h][rs[i]] * jnp.exp(a[(i, h)] - m_loc[(i, h)]) for i, h in pairs}
    c_chunk = {(i, h): mm_tn(kw[(i, h)], vm[h][rs[i]]) for i, h in pairs}
    mem = {(0, h): st["S"][h] for h in hs}
    c_in = {(0, h): st["C"][h] for h in hs}
    n_in = {(0, h): st["n"][h] for h in hs}
    m_in = {(0, h): st["m"][h][:, 0:1] for h in hs}
    for i, h in pairs:
        mem[(i + 1, h)] = mem[(i, h)] * dec[i][:, s6[h]] + s_chunk[(i, h)]
        m_nx = jnp.maximum(f_last[(i, h)] + m_in[(i, h)], m_loc[(i, h)])
        sp = jnp.exp(f_last[(i, h)] + m_in[(i, h)] - m_nx)
        sl = jnp.exp(m_loc[(i, h)] - m_nx)
        c_in[(i + 1, h)] = sp * c_in[(i, h)] + sl * c_chunk[(i, h)]
        n_in[(i + 1, h)] = sp * n_in[(i, h)] + sl * jnp.sum(kw[(i, h)], axis=0, keepdims=True)
        m_in[(i + 1, h)] = m_nx
    s_new = [mem[(n_ch, h)] for h in hs]
    o_inter = {(i, h): mm_nt(qp[rs[i], s6[h]], mem[(i, h)]) for i, h in pairs}
    q_c = {(i, h): mm_nn(qm[h][rs[i]], c_in[(i, h)]) for i, h in pairs}
    scores = {ih: jnp.where(causal, a_fwd[ih], a_bwd[ih]) for ih in pairs}
    log_d = {(i, h): gates_t[h:h + 1, rs[i]] - jnp.abs(fc_c[(i, h)] - fc_t[4 + h:5 + h, rs[i]]) for i, h in pairs}
    g_int = {ih: fc_c[ih] + m_in[ih] for ih in pairs}
    m_t = {ih: jnp.maximum(g_int[ih], jnp.max(log_d[ih], axis=1, keepdims=True)) for ih in pairs}
    s = {ih: qk[ih] * jnp.exp(log_d[ih] - m_t[ih]) for ih in pairs}
    scl = {ih: jnp.exp(g_int[ih] - m_t[ih]) for ih in pairs}
    o = {(i, h): mm_nn(scores[(i, h)], v[rs[i], s12[h]]) + o_inter[(i, h)] for i, h in pairs}
    num = {(i, h): mm_nn(s[(i, h)], vm[h][rs[i]]) + scl[(i, h)] * q_c[(i, h)] for i, h in pairs}
    o = {ih: o[ih] * lax.rsqrt(_mean(o[ih] * o[ih]) + EPS) * p["ggla"] for ih in pairs}
    gate = g * _sigmoid(g)
    out_a = {(i, h): o[(i, h)] * gate[rs[i], s12[h]] for i, h in pairs}
    den = {(i, h): jnp.sum(s[(i, h)], axis=1, keepdims=True)
           + scl[(i, h)] * jnp.sum(qm[h][rs[i]] * n_in[(i, h)], axis=1, keepdims=True) for i, h in pairs}
    den = {ih: jnp.maximum(jnp.abs(den[ih]), jnp.exp(-m_t[ih])) for ih in pairs}
    open_gate = _sigmoid(opre)
    hc = {(i, h): num[(i, h)] / den[(i, h)] * open_gate[rs[i], s12[h]] for i, h in pairs}
    d0 = {ih: hc[ih] - _mean(hc[ih]) for ih in pairs}
    y = {ih: d0[ih] * lax.rsqrt(_mean(d0[ih] * d0[ih]) + EPS) for ih in pairs}
    skipped = p["skip"] * xc
    out_b = {(i, h): y[(i, h)] * p["gml"][:, s12[h]] + skipped[rs[i], s12[h]] for i, h in pairs}
    ab = jnp.concatenate([jnp.concatenate([out_a[(i, h)] for h in hs] + [out_b[(i, h)] for h in hs], axis=1) for i in cs],
                         axis=0)
    new = {"S": s_new, "C": [c_in[(n_ch, h)] for h in hs], "n": [n_in[(n_ch, h)] for h in hs],
           "m": [jnp.broadcast_to(m_in[(n_ch, h)], (1, ML_DH)) for h in hs]}
    return ab, new


_P_NAMES = ("wau", "bau", "ggla", "cw", "cb", "wq", "wk", "wv", "wif", "bif", "skip", "gml")
_P_SHAPES = {
    "wau": (128, 256), "bau": (1, 256), "ggla": (1, 128), "cw": (4, 512), "cb": (1, 512),
    "wq": (512, 128), "wk": (512, 128), "wv": (512, 128),
    "wif": (1536, 128), "bif": (1, 128), "skip": (1, 512), "gml": (1, 512),
}
_P_BLOCKDIAG = ("wq", "wk", "wv")
_S_NAMES = ("S", "C", "n", "m")
_S_SHAPES = {"S": (HEADS, GLA_DV, GLA_DK), "C": (HEADS, ML_DH, ML_DH), "n": (HEADS, 1, ML_DH), "m": (HEADS, 1, ML_DH)}


def _per_head(ref):
    return [ref[h] for h in range(HEADS)]


def _block_mask():
    r = lax.broadcasted_iota(jnp.int32, (128, 128), 0)
    c = lax.broadcasted_iota(jnp.int32, (128, 128), 1)
    same_block = (r >> 2) == (c >> 2)
    spread = jnp.logical_and(r < 4, (c & 3) == r)
    return same_block.astype(F32), spread.astype(F32)


def _expand_blockdiag(w_ref, dense_ref):
    same_block, spread = _block_mask()
    for h in range(HEADS):
        tiled = _pmm_nn(w_ref[h * 128:(h + 1) * 128, :], spread)
        dense_ref[h] = tiled * same_block


def _collect_blockdiag(ddense_ref, dw_ref):
    same_block, spread = _block_mask()
    for h in range(HEADS):
        dw_ref[h * 128:(h + 1) * 128, :] = lax.dot_general(
            ddense_ref[h] * same_block, spread, (((1,), (1,)), ((), ())), precision=lax.Precision.HIGHEST,
            preferred_element_type=F32)


def _const_spec(shape):
    zeros = (0,) * len(shape)
    return pl.BlockSpec(shape, lambda i: zeros)


def _split(refs, *counts):
    out, at = [], 0
    for c in counts:
        out.append(refs[at:at + c])
        at += c
    assert at == len(refs)
    return out


def _landed(lands, outs, flush_sems):
    return [pltpu.make_async_copy(lands[k], outs[k], flush_sems.at[k]) for k in range(len(outs))]


def _ride_done(rider, cond, outs, sems):
    if rider is None:
        return

    @pl.when(cond)
    def _():
        if hasattr(rider, "flush"):
            rider.flush_done(sems[:-3], outs, sems[-1])
        else:
            for cp in _landed(sems[:-3], outs, sems[-1]):
                cp.wait()


def _ride(rider, phases, cond, ins, outs, sems, flush_later=False):
    if rider is None or not any(hasattr(rider, phase) for phase in phases):
        return
    lands, (send_sems, recv_sems, flush_sems) = sems[:-3], sems[-3:]

    @pl.when(cond)
    def _():
        for phase in phases:
            if phase == "last" and hasattr(rider, "late"):
                rider.late(ins, lands, send_sems, recv_sems)
                rider.flush("late", lands, outs, flush_sems)
            getattr(rider, phase)(ins, lands, send_sems, recv_sems)
            if hasattr(rider, "flush"):
                rider.flush(phase, lands, outs, flush_sems, wait=not flush_later)
        if "last" in phases and not hasattr(rider, "flush"):
            flush = _landed(lands, outs, flush_sems)
            for cp in flush:
                cp.start()
            if not flush_later:
                for cp in flush:
                    cp.wait()


def _middle_step(rider, n_steps):
    return min(n_steps - 2, int(getattr(rider, "middle_at", 1.0) * n_steps))


def _rider_specs(rider, rider_ins):
    if rider is None:
        return [], [], [], []
    scratch = [pltpu.VMEM(s.shape, s.dtype) for s in list(rider.out_shape) + list(getattr(rider, "work_shape", ()))]
    scratch += [pltpu.SemaphoreType.DMA((rider.n_sems,)), pltpu.SemaphoreType.DMA((rider.n_sems,)),
                pltpu.SemaphoreType.DMA((getattr(rider, "n_flush", len(rider.out_shape)),))]
    in_space = getattr(rider, "in_space", VMEM_WHOLE)
    in_specs = list(in_space) if isinstance(in_space, (list, tuple)) else [in_space] * len(rider_ins)
    return in_specs, [ANY] * len(rider.out_shape), list(rider.out_shape), scratch


def _merge_tile(ab, gab_ref, x_ref, wpa_ref, wpb_ref, wo_ref, g_ref, x1_ref, mix_ref, mg_ref):
    a = ab[:, 0:512]
    b = ab[:, 512:1024]
    for j in range(N_CHIP):
        blk = slice(j * 256, (j + 1) * 256)
        ya = jnp.dot(a, wpa_ref[j], preferred_element_type=F32)
        yb = jnp.dot(b, wpb_ref[j], preferred_element_type=F32)
        sa = _sigmoid(gab_ref[:, j * 256:(j + 1) * 256])
        sb = _sigmoid(gab_ref[:, 1024 + j * 256:1024 + (j + 1) * 256])
        mg_ref[:, blk] = (sa * ya + sb * yb).astype(BF16)
    mix = jnp.dot(mg_ref[...], wo_ref[...], preferred_element_type=F32)
    mix_ref[...] = mix
    mn, _ = _rms_fwd(mix)
    x1_ref[...] = x_ref[...] + mn * g_ref[...]


def _mixer_fwd(pm, p, gab, x, w_pa4, w_pb4, w_o, g_post, rider=None, rider_ins=()):
    n_p = len(_P_NAMES)
    r_in, r_out_specs, r_out_shape, r_sems = _rider_specs(rider, rider_ins)

    def body(*refs):
        ((pm_ref, xprev_ref), p_list, merge_in, ride_in, (ab_ref,), merge_out, so_refs, ride_out, sc_refs, dense_list,
         sems) = _split(refs, 2, n_p, 6, len(r_in), 1, 3, 4, len(r_out_specs), 4, 3, len(r_sems))
        p_refs = dict(zip(_P_NAMES, p_list))
        dense = dict(zip(_P_BLOCKDIAG, dense_list))
        n = pl.program_id(0)
        _ride(rider, ("first",), n == 0, ride_in, ride_out, sems)
        _ride(rider, ("last",), n == N_SWEEP - 1, ride_in, ride_out, sems, flush_later=True)

        @pl.when(n == 0)
        def _():
            for r in sc_refs:
                r[...] = jnp.zeros_like(r)
            for nm in _P_BLOCKDIAG:
                _expand_blockdiag(p_refs[nm], dense[nm])

        st = {name: _per_head(r) for name, r in zip(_S_NAMES, sc_refs)}
        pv = {nm: (_per_head(dense[nm]) if nm in _P_BLOCKDIAG else p_refs[nm][...]) for nm in _P_NAMES}
        for name, r in zip(_S_NAMES, so_refs):
            for h in range(HEADS):
                r[0, h] = st[name][h]
        xprev8 = jnp.where(n > 0, xprev_ref[CHUNK - 8:CHUNK, :], 0.0)
        ab, st = _mixer_chunk(_PLAIN_OPS, pv, st, pm_ref[...], xprev8)
        ab = ab.astype(BF16)
        ab_ref[...] = ab
        for name, r in zip(_S_NAMES, sc_refs):
            for h in range(HEADS):
                r[h] = st[name][h]
        _merge_tile(ab, *merge_in, *merge_out)
        _ride(rider, ("middle",), n == _middle_step(rider, N_SWEEP), ride_in, ride_out, sems)
        _ride_done(rider, n == N_SWEEP - 1, ride_out, sems)

    rows = lambda width: pl.BlockSpec((SWEEP * CHUNK, width), lambda i: (i, 0))
    in_specs = [rows(PM_W), pl.BlockSpec((CHUNK, 512), lambda i: (jnp.maximum(SWEEP * i - 1, 0), PM_XM // 512))]
    in_specs += [_const_spec(_P_SHAPES[nm]) for nm in _P_NAMES]
    in_specs += [rows(GAB_W), rows(D_MODEL), _once((N_CHIP, 512, 256)), _once((N_CHIP, 512, 256)), _once((D_MODEL, D_MODEL)),
                 _once((1, D_MODEL))] + r_in
    out_specs = [rows(1024), rows(D_MODEL), rows(D_MODEL), rows(D_MODEL)]
    out_shape = [jax.ShapeDtypeStruct((SEQ, 1024), BF16), jax.ShapeDtypeStruct((SEQ, D_MODEL), F32),
                 jax.ShapeDtypeStruct((SEQ, D_MODEL), F32), jax.ShapeDtypeStruct((SEQ, D_MODEL), BF16)]
    for nm in _S_NAMES:
        shp = _S_SHAPES[nm]
        out_specs.append(pl.BlockSpec((1,) + shp, lambda i: (i, 0, 0, 0)))
        out_shape.append(jax.ShapeDtypeStruct((N_SWEEP,) + shp, F32))
    return pl.pallas_call(
        body, grid=(N_SWEEP,), in_specs=in_specs, out_specs=out_specs + r_out_specs, out_shape=out_shape + r_out_shape,
        scratch_shapes=[pltpu.VMEM(_S_SHAPES[nm], F32) for nm in _S_NAMES]
        + [pltpu.VMEM((HEADS, 128, 128), F32) for _ in _P_BLOCKDIAG] + r_sems,
        compiler_params=_params(("arbitrary",)), name="mixer_fwd",
    )(pm, pm, *[p[nm] for nm in _P_NAMES], gab, x, w_pa4, w_pb4, w_o, g_post, *rider_ins)


def _mixer_bwd(pm, dab, states, p, rider=None, rider_ins=()):
    n_p = len(_P_NAMES)
    r_in, r_out_specs, r_out_shape, r_sems = _rider_specs(rider, rider_ins)

    def body(*refs):
        ((pm_ref, xprev_ref, dab_ref), si_refs, p_list, ride_in, (dpm_ref,), dp_list, ride_out, ds_refs, (carry_ref,),
         dense_list, ddense_list, sems) = _split(refs, 3, 4, n_p, len(r_in), 1, n_p, len(r_out_specs), 4, 1, 3, 3, len(r_sems))
        p_refs = dict(zip(_P_NAMES, p_list))
        dp_refs = dict(zip(_P_NAMES, dp_list))
        dense = dict(zip(_P_BLOCKDIAG, dense_list))
        ddense = dict(zip(_P_BLOCKDIAG, ddense_list))
        i = pl.program_id(0)
        blk = N_SWEEP - 1 - i
        _ride(rider, ("first",), i == 0, ride_in, ride_out, sems)
        _ride(rider, ("last",), i == N_SWEEP - 1, ride_in, ride_out, sems, flush_later=True)

        @pl.when(i == 0)
        def _():
            for r in ds_refs:
                r[...] = jnp.zeros_like(r)
            for nm in _P_NAMES:
                if nm in _P_BLOCKDIAG:
                    ddense[nm][...] = jnp.zeros_like(ddense[nm])
                    _expand_blockdiag(p_refs[nm], dense[nm])
                else:
                    dp_refs[nm][...] = jnp.zeros_like(dp_refs[nm])
            carry_ref[...] = jnp.zeros_like(carry_ref)

        pv = {nm: (_per_head(dense[nm]) if nm in _P_BLOCKDIAG else p_refs[nm][...]) for nm in _P_NAMES}
        dst = {name: _per_head(r) for name, r in zip(_S_NAMES, ds_refs)}
        st = {name: [r[0, h] for h in range(HEADS)] for name, r in zip(_S_NAMES, si_refs)}
        xprev8 = jnp.where(blk > 0, xprev_ref[CHUNK - 8:CHUNK, :], 0.0)
        _, vjp = jax.vjp(functools.partial(_mixer_chunk, _VJP_OPS), pv, st, pm_ref[...], xprev8)
        dp_sum, dst, dpm, dxprev8 = vjp((dab_ref[...], dst))
        reach = jnp.concatenate([jnp.zeros((SWEEP * CHUNK - 8, 512), F32), carry_ref[...]], axis=0)
        dpm_ref[:, 0:PM_XM] = dpm[:, 0:PM_XM].astype(BF16)
        dpm_ref[:, PM_XM:PM_XM + 512] = (dpm[:, PM_XM:PM_XM + 512] + reach).astype(BF16)
        dpm_ref[:, PM_XM + 512:PM_W] = dpm[:, PM_XM + 512:PM_W].astype(BF16)
        carry_ref[...] = dxprev8
        for name, r in zip(_S_NAMES, ds_refs):
            for h in range(HEADS):
                r[h] = dst[name][h]
        for nm in _P_NAMES:
            if nm in _P_BLOCKDIAG:
                for h in range(HEADS):
                    ddense[nm][h] += dp_sum[nm][h]
            else:
                dp_refs[nm][...] += dp_sum[nm]

        @pl.when(i == N_SWEEP - 1)
        def _():
            for nm in _P_BLOCKDIAG:
                _collect_blockdiag(ddense[nm], dp_refs[nm])

        _ride(rider, ("early",), i == 1, ride_in, ride_out, sems)
        _ride(rider, ("middle",), i == _middle_step(rider, N_SWEEP), ride_in, ride_out, sems)
        _ride_done(rider, i == N_SWEEP - 1, ride_out, sems)

    rev = lambda i: (N_SWEEP - 1 - i, 0)
    in_specs = [pl.BlockSpec((SWEEP * CHUNK, PM_W), rev),
                pl.BlockSpec((CHUNK, 512), lambda i: (jnp.maximum(SWEEP * (N_SWEEP - 1 - i) - 1, 0), PM_XM // 512)),
                pl.BlockSpec((SWEEP * CHUNK, 1024), rev)]
    for nm in _S_NAMES:
        in_specs.append(pl.BlockSpec((1,) + _S_SHAPES[nm], lambda i: (N_SWEEP - 1 - i, 0, 0, 0)))
    in_specs += [_const_spec(_P_SHAPES[nm]) for nm in _P_NAMES] + r_in
    out_specs = [pl.BlockSpec((SWEEP * CHUNK, PM_W), rev)] + [_const_spec(_P_SHAPES[nm]) for nm in _P_NAMES]
    out_shape = [jax.ShapeDtypeStruct((SEQ, PM_W), BF16)] + [jax.ShapeDtypeStruct(_P_SHAPES[nm], F32) for nm in _P_NAMES]
    res = pl.pallas_call(
        body, grid=(N_SWEEP,), in_specs=in_specs, out_specs=out_specs + r_out_specs, out_shape=out_shape + r_out_shape,
        scratch_shapes=[pltpu.VMEM(_S_SHAPES[nm], F32) for nm in _S_NAMES] + [pltpu.VMEM((8, 512), F32)]
        + [pltpu.VMEM((HEADS, 128, 128), F32) for _ in range(2 * len(_P_BLOCKDIAG))] + r_sems,
        compiler_params=_params(("arbitrary",)), name="mixer_bwd",
    )(pm, pm, dab, *states, *[p[nm] for nm in _P_NAMES], *rider_ins)
    return res[0], dict(zip(_P_NAMES, res[1:1 + n_p])), res[1 + n_p:]


def _tok(width):
    return pl.BlockSpec((TOK_TILE, width), lambda i: (i, 0))


def _once(shape):
    zeros = (0,) * len(shape)
    return pl.BlockSpec(shape, lambda i: zeros, pipeline_mode=pl.Buffered(1))


def _rms_fwd(x):
    r = lax.rsqrt(_mean(x * x) + EPS)
    return x * r, r


def _rms_bwd(dy, xn, r, g):
    gd = dy * g
    return r * (gd - xn * _mean(xn * gd))


def _tiled_call(body, in_specs, out_specs, out_shape, args, name, rider=None, rider_ins=(), scratch=()):
    r_in, r_out_specs, r_out_shape, r_scratch = _rider_specs(rider, rider_ins)
    n_in, n_out = len(in_specs), len(out_specs)

    def hosted(*refs):
        ins, ride_in, outs, ride_out, own, r_scr = _split(refs, n_in, len(r_in), n_out, len(r_out_specs), len(scratch),
                                                          len(r_scratch))
        i = pl.program_id(0)
        _ride(rider, ("first",), i == 0, ride_in, ride_out, r_scr)
        body(*ins, *outs, *own)
        _ride(rider, ("early",), i == 1, ride_in, ride_out, r_scr)
        _ride(rider, ("middle",), i == _middle_step(rider, N_TOK_TILE), ride_in, ride_out, r_scr)
        _ride(rider, ("last",), i == N_TOK_TILE - 1, ride_in, ride_out, r_scr)

    res = pl.pallas_call(
        hosted, grid=(N_TOK_TILE,), in_specs=list(in_specs) + r_in, out_specs=list(out_specs) + r_out_specs,
        out_shape=list(out_shape) + r_out_shape, scratch_shapes=list(scratch) + r_scratch,
        compiler_params=_params(("arbitrary",)), name=name,
    )(*args, *rider_ins)
    return res[:n_out], res[n_out:]


def _join_rows(w4_ref, wt_ref):
    @pl.when(pl.program_id(0) == 0)
    def _():
        for j in range(N_CHIP):
            wt_ref[j * IN_SHARD:(j + 1) * IN_SHARD, :] = w4_ref[j]


def _joined_scratch():
    return [pltpu.VMEM((D_IN, D_MODEL), BF16)]


def _in_proj(x, g_pre, w4_in, rider=None, rider_ins=()):
    def body(x_ref, g_ref, w4_ref, pm_ref, gab_ref, h_ref, wt_ref):
        _join_rows(w4_ref, wt_ref)
        xn, _ = _rms_fwd(x_ref[...])
        h = (xn * g_ref[...]).astype(BF16)
        h_ref[...] = h
        pm_ref[:, 0:PM_XM] = _nt(h, wt_ref[0:IN_ALOW, :])
        pm_ref[:, PM_XM:PM_AL] = _nt(h, wt_ref[IN_XM:IN_GATES, :])
        pm_ref[:, PM_AL:PM_W] = _nt(h, wt_ref[IN_ALOW:IN_ALOW + 128, :])
        gab_ref[...] = _nt(h, wt_ref[IN_GATES:D_IN, :])

    return _tiled_call(
        body, [_tok(D_MODEL), _once((1, D_MODEL)), _once((N_CHIP, IN_SHARD, D_MODEL))],
        [_tok(PM_W), _tok(GAB_W), _tok(D_MODEL)],
        [jax.ShapeDtypeStruct((SEQ, PM_W), F32), jax.ShapeDtypeStruct((SEQ, GAB_W), F32),
         jax.ShapeDtypeStruct((SEQ, D_MODEL), BF16)], (x, g_pre, w4_in), "in_proj", rider, rider_ins, _joined_scratch())


def _mlp(x1, target, g_pre, g_post, w_up4, w_down_a4, w_down_b4):
    def body(x1_ref, t_ref, gpre_ref, gpost_ref, wup_ref, wda_ref, wdb_ref,
             dx1_ref, u_ref, dd_ref, h2_ref, dpre_ref, dgpost_ref, dgpre_ref, loss_ref):
        @pl.when(pl.program_id(0) == 0)
        def _():
            dgpost_ref[...] = jnp.zeros_like(dgpost_ref)
            dgpre_ref[...] = jnp.zeros_like(dgpre_ref)
            loss_ref[...] = jnp.zeros_like(loss_ref)

        x1 = x1_ref[...]
        gpre = gpre_ref[...]
        gpost = gpost_ref[...]
        xn2, r2 = _rms_fwd(x1)
        h2 = (xn2 * gpre).astype(BF16)
        h2_ref[...] = h2
        rl = []
        d = jnp.zeros((TOK_TILE, D_MODEL), F32)
        for j in range(N_CHIP):
            blk = slice(j * 1024, (j + 1) * 1024)
            r = jnp.maximum(jnp.dot(h2, wup_ref[j], preferred_element_type=F32), 0.0)
            rl.append(r)
            u = (r * r).astype(BF16)
            u_ref[:, blk] = u
            d = d + jnp.dot(u[:, 0:512], wda_ref[j], preferred_element_type=F32)
            d = d + jnp.dot(u[:, 512:1024], wdb_ref[j], preferred_element_type=F32)
        dn, r3 = _rms_fwd(d)
        diff = x1 + dn * gpost - t_ref[...]
        loss_ref[...] += jnp.sum(diff * diff, keepdims=True) * (0.5 / D_MODEL)
        dy = diff * (1.0 / D_MODEL)
        dgpost_ref[...] += jnp.sum(dy * dn, axis=0, keepdims=True)
        dd = _rms_bwd(dy, dn, r3, gpost).astype(BF16)
        dd_ref[...] = dd
        dh2 = jnp.zeros((TOK_TILE, D_MODEL), F32)
        for j in range(N_CHIP):
            blk = slice(j * 1024, (j + 1) * 1024)
            du = jnp.concatenate([_nt(dd, wda_ref[j]), _nt(dd, wdb_ref[j])], axis=1)
            dpre = (du * (2.0 * rl[j])).astype(BF16)
            dpre_ref[:, blk] = dpre
            dh2 = dh2 + _nt(dpre, wup_ref[j])
        dgpre_ref[...] += jnp.sum(dh2 * xn2, axis=0, keepdims=True)
        dx1_ref[...] = dy + _rms_bwd(dh2, xn2, r2, gpre)

    acc = pl.BlockSpec((1, D_MODEL), lambda i: (0, 0))
    return pl.pallas_call(
        body, grid=(N_TOK_TILE,),
        in_specs=[_tok(D_MODEL), _tok(D_MODEL), _once((1, D_MODEL)), _once((1, D_MODEL)),
                  _once((N_CHIP, D_MODEL, 1024)), _once((N_CHIP, 512, D_MODEL)), _once((N_CHIP, 512, D_MODEL))],
        out_specs=[_tok(D_MODEL), _tok(D_FF), _tok(D_MODEL), _tok(D_MODEL), _tok(D_FF), acc, acc,
                   pl.BlockSpec((1, 128), lambda i: (0, 0))],
        out_shape=[jax.ShapeDtypeStruct((SEQ, D_MODEL), F32), jax.ShapeDtypeStruct((SEQ, D_FF), BF16),
                   jax.ShapeDtypeStruct((SEQ, D_MODEL), BF16), jax.ShapeDtypeStruct((SEQ, D_MODEL), BF16),
                   jax.ShapeDtypeStruct((SEQ, D_FF), BF16), jax.ShapeDtypeStruct((1, D_MODEL), F32),
                   jax.ShapeDtypeStruct((1, D_MODEL), F32), jax.ShapeDtypeStruct((1, 128), F32)],
        compiler_params=_params(("arbitrary",)), name="mlp_fwd_bwd",
    )(x1, target, g_pre, g_post, w_up4, w_down_a4, w_down_b4)


def _merge_bwd(dx1, mix, ab, gab, merged, w_pa4, w_pb4, w_o, g_post):
    def body(dx1_ref, mix_ref, ab_ref, gab_ref, mg_ref, wpa_ref, wpb_ref, wo_ref, g_ref,
             dgab_ref, dab_ref, dg_ref, dwpa_ref, dwpb_ref, dwo_ref, acc_pa, acc_pb, acc_o):
        @pl.when(pl.program_id(0) == 0)
        def _():
            dg_ref[...] = jnp.zeros_like(dg_ref)
            acc_pa[...] = jnp.zeros_like(acc_pa)
            acc_pb[...] = jnp.zeros_like(acc_pb)
            acc_o[...] = jnp.zeros_like(acc_o)

        dx1 = dx1_ref[...]
        mn, r = _rms_fwd(mix_ref[...])
        dg_ref[...] += jnp.sum(dx1 * mn, axis=0, keepdims=True)
        dmix = _rms_bwd(dx1, mn, r, g_ref[...]).astype(BF16)
        acc_o[...] += _tn(mg_ref[...], dmix)
        dmerged = _nt(dmix, wo_ref[...])
        a = ab_ref[:, 0:512]
        b = ab_ref[:, 512:1024]
        da = jnp.zeros((TOK_TILE, 512), F32)
        db = jnp.zeros((TOK_TILE, 512), F32)
        dyas, dybs = [], []
        for j in range(N_CHIP):
            blk = slice(j * 256, (j + 1) * 256)
            blk_b = slice(1024 + j * 256, 1024 + (j + 1) * 256)
            dm = dmerged[:, blk]
            ya = jnp.dot(a, wpa_ref[j], preferred_element_type=F32)
            yb = jnp.dot(b, wpb_ref[j], preferred_element_type=F32)
            sa = _sigmoid(gab_ref[:, blk])
            sb = _sigmoid(gab_ref[:, blk_b])
            dya = (dm * sa).astype(BF16)
            dyb = (dm * sb).astype(BF16)
            dyas.append(dya)
            dybs.append(dyb)
            dgab_ref[:, blk] = (dm * ya * sa * (1.0 - sa)).astype(BF16)
            dgab_ref[:, blk_b] = (dm * yb * sb * (1.0 - sb)).astype(BF16)
            da = da + _nt(dya, wpa_ref[j])
            db = db + _nt(dyb, wpb_ref[j])
        dab_ref[:, 0:512] = da
        dab_ref[:, 512:1024] = db
        acc_pa[...] += _tn(a, jnp.concatenate(dyas, axis=1))
        acc_pb[...] += _tn(b, jnp.concatenate(dybs, axis=1))

        @pl.when(pl.program_id(0) == N_TOK_TILE - 1)
        def _():
            dwo_ref[...] = acc_o[...].astype(BF16)
            for j in range(N_CHIP):
                dwpa_ref[j] = acc_pa[:, j * 256:(j + 1) * 256].astype(BF16)
                dwpb_ref[j] = acc_pb[:, j * 256:(j + 1) * 256].astype(BF16)

    whole = lambda shape: pl.BlockSpec(shape, lambda i: (0,) * len(shape))
    return pl.pallas_call(
        body, grid=(N_TOK_TILE,),
        in_specs=[_tok(D_MODEL), _tok(D_MODEL), _tok(1024), _tok(GAB_W), _tok(D_MODEL), _once((N_CHIP, 512, 256)),
                  _once((N_CHIP, 512, 256)), _once((D_MODEL, D_MODEL)), _once((1, D_MODEL))],
        out_specs=[_tok(GAB_W), _tok(1024), whole((1, D_MODEL)), whole((N_CHIP, 512, 256)), whole((N_CHIP, 512, 256)),
                   whole((D_MODEL, D_MODEL))],
        out_shape=[jax.ShapeDtypeStruct((SEQ, GAB_W), BF16), jax.ShapeDtypeStruct((SEQ, 1024), F32),
                   jax.ShapeDtypeStruct((1, D_MODEL), F32), jax.ShapeDtypeStruct((N_CHIP, 512, 256), BF16),
                   jax.ShapeDtypeStruct((N_CHIP, 512, 256), BF16), jax.ShapeDtypeStruct((D_MODEL, D_MODEL), BF16)],
        scratch_shapes=[pltpu.VMEM((512, D_MODEL), F32), pltpu.VMEM((512, D_MODEL), F32),
                        pltpu.VMEM((D_MODEL, D_MODEL), F32)],
        compiler_params=_params(("arbitrary",)), name="merge_bwd",
    )(dx1, mix, ab, gab, merged, w_pa4, w_pb4, w_o, g_post)


def _in_proj_bwd(dpm, dgab, x, dx1, g_pre, w4_in, rider=None, rider_ins=()):
    def body(dpm_ref, dgab_ref, x_ref, dx1_ref, g_ref, w4_ref, dx_ref, dg_ref, wt_ref):
        _join_rows(w4_ref, wt_ref)

        @pl.when(pl.program_id(0) == 0)
        def _():
            dg_ref[...] = jnp.zeros_like(dg_ref)

        dh = jnp.dot(dpm_ref[:, 0:PM_XM], wt_ref[0:IN_ALOW, :], preferred_element_type=F32)
        dh = dh + jnp.dot(dpm_ref[:, PM_XM:PM_AL], wt_ref[IN_XM:IN_GATES, :], preferred_element_type=F32)
        dh = dh + jnp.dot(dpm_ref[:, PM_AL:PM_W], wt_ref[IN_ALOW:IN_ALOW + 128, :], preferred_element_type=F32)
        dh = dh + jnp.dot(dgab_ref[...], wt_ref[IN_GATES:D_IN, :], preferred_element_type=F32)
        xn, r = _rms_fwd(x_ref[...])
        dg_ref[...] += jnp.sum(dh * xn, axis=0, keepdims=True)
        dx_ref[...] = dx1_ref[...] + _rms_bwd(dh, xn, r, g_ref[...])

    return _tiled_call(
        body, [_tok(PM_W), _tok(GAB_W), _tok(D_MODEL), _tok(D_MODEL), _once((1, D_MODEL)),
               _once((N_CHIP, IN_SHARD, D_MODEL))],
        [_tok(D_MODEL), pl.BlockSpec((1, D_MODEL), lambda i: (0, 0))],
        [jax.ShapeDtypeStruct((SEQ, D_MODEL), F32), jax.ShapeDtypeStruct((1, D_MODEL), F32)],
        (dpm, dgab, x, dx1, g_pre, w4_in), "in_proj_bwd", rider, rider_ins, _joined_scratch())


def _dw_in(dpm, dgab, h):
    n_pm = PM_AL // 512
    n_blk = n_pm + GAB_W // 512

    def place(o_ref, rows, lo, hi):
        for j in range(N_CHIP):
            a, b = max(lo, j * IN_SHARD), min(hi, (j + 1) * IN_SHARD)
            if a < b:
                o_ref[j, a - j * IN_SHARD:b - j * IN_SHARD, :] = rows(a - lo, b - lo)

    def body(dpm_ref, dgab_ref, dal_ref, h_ref, o_ref, blk_ref):
        i = pl.program_id(0)

        @pl.when(i < n_pm)
        def _():
            blk_ref[...] = _tn(dpm_ref[...], h_ref[...]).astype(BF16)

        @pl.when(i >= n_pm)
        def _():
            blk_ref[...] = _tn(dgab_ref[...], h_ref[...]).astype(BF16)

        for k in range(n_blk):
            off = k * 512 + (IN_XM - IN_ALOW) * (k >= IN_ALOW // 512)

            @pl.when(i == k)
            def _():
                place(o_ref, lambda a, b: blk_ref[a:b, :], off, off + 512)

        @pl.when(i == 0)
        def _():
            a_low = _tn(dal_ref[...], h_ref[...])[0:IN_XM - IN_ALOW].astype(BF16)
            place(o_ref, lambda a, b: a_low[a:b], IN_ALOW, IN_XM)

    return pl.pallas_call(
        body, grid=(n_blk,),
        in_specs=[pl.BlockSpec((SEQ, 512), lambda i: (0, jnp.minimum(i, n_pm - 1))),
                  pl.BlockSpec((SEQ, 512), lambda i: (0, jnp.maximum(i - n_pm, 0))),
                  pl.BlockSpec((SEQ, 128), lambda i: (0, PM_AL // 128)),
                  _once((SEQ, D_MODEL))],
        out_specs=pl.BlockSpec((N_CHIP, IN_SHARD, D_MODEL), lambda i: (0, 0, 0)),
        out_shape=jax.ShapeDtypeStruct((N_CHIP, IN_SHARD, D_MODEL), BF16),
        scratch_shapes=[pltpu.VMEM((512, D_MODEL), BF16)],
        compiler_params=_params(("arbitrary",)), name="dw_in",
    )(dpm, dgab, dpm, h)


def _tn_matmul(a, b, name, shards=1, tm=1024, rider=None, rider_ins=()):
    m, n = a.shape[1], b.shape[1]
    tm = min(tm, m)
    tn = n // shards if shards > 1 else min(n, 1024)
    steps_i, steps_j = m // tm, n // tn
    r_in, r_out_specs, r_out_shape, r_scratch = _rider_specs(rider, rider_ins)

    def body(*refs):
        (a_ref, b_ref), ride_in, (o_ref,), ride_out, scratch = _split(refs, 2, len(r_in), 1, len(r_out_specs), len(r_scratch))
        step = pl.program_id(0) * steps_j + pl.program_id(1)
        last = step == steps_i * steps_j - 1
        _ride(rider, ("first",), step == 0, ride_in, ride_out, scratch)
        _ride(rider, ("middle", "last"), last, ride_in, ride_out, scratch, flush_later=True)
        o_ref[...] = _tn(a_ref[...], b_ref[...]).astype(BF16)
        _ride_done(rider, last, ride_out, scratch)

    if shards > 1:
        out_spec = pl.BlockSpec((None, tm, tn), lambda i, j: (j, i, 0))
        out_shape = jax.ShapeDtypeStruct((shards, m, tn), BF16)
    else:
        out_spec = pl.BlockSpec((tm, tn), lambda i, j: (i, j))
        out_shape = jax.ShapeDtypeStruct((m, n), BF16)
    res = pl.pallas_call(
        body, grid=(steps_i, steps_j),
        in_specs=[pl.BlockSpec((SEQ, tm), lambda i, j: (0, i)), pl.BlockSpec((SEQ, tn), lambda i, j: (0, j))] + r_in,
        out_specs=[out_spec] + r_out_specs, out_shape=[out_shape] + r_out_shape, scratch_shapes=r_scratch,
        compiler_params=_params(("arbitrary", "arbitrary")), name=name,
    )(a, b, *rider_ins)
    return res[0] if rider is None else (res[0], res[1:])


MESH = pl.DeviceIdType.MESH
ANY = pl.BlockSpec(memory_space=pl.ANY)
VMEM_WHOLE = pl.BlockSpec(memory_space=pltpu.VMEM)

_BIG = ("w_in", "w_pa", "w_pb", "w_o", "w_up", "w_down")
_BIG_SHARD = {"w_in": (IN_SHARD, D_MODEL), "w_pa": (512, 256), "w_pb": (512, 256), "w_o": (256, D_MODEL),
              "w_up": (D_MODEL, 1024), "w_down": (1024, D_MODEL),
              "w_down_a": (512, D_MODEL), "w_down_b": (512, D_MODEL)}
_BIG_SPLIT = {"w_in": 1, "w_pa": 0, "w_pb": 0, "w_o": 0, "w_up": 0, "w_down": 0, "w_down_a": 0, "w_down_b": 0}


def _half(ref, e, name, lead=0, part=None):
    axis = _BIG_SPLIT[name]
    size = _BIG_SHARD[name][axis] // 2
    start = e * size
    if part is not None:
        size //= 2
        start = start + part * size
    start = pl.multiple_of(start, 128 if axis == 1 else 16)
    idx = [pl.ds(0, ref.shape[a]) for a in range(lead)]
    idx += [pl.ds(start, size), pl.ds(0, _BIG_SHARD[name][1])] if axis == 0 else [pl.ds(0, _BIG_SHARD[name][0]), pl.ds(start, size)]
    return ref.at[tuple(idx)]


def _half_shape(name):
    r, c = _BIG_SHARD[name]
    return (r // 2, c) if _BIG_SPLIT[name] == 0 else (r, c // 2)


def _remote(src, dst, send_sems, recv_sems, k, to):
    return pltpu.make_async_remote_copy(src_ref=src, dst_ref=dst, send_sem=send_sems.at[k], recv_sem=recv_sems.at[k],
                                        device_id=to, device_id_type=MESH)


def _mesh_place():
    x, y, c = lax.axis_index("x"), lax.axis_index("y"), lax.axis_index("c")
    return x, y, c, [(1 - x, y), (x, 1 - y), (1 - x, 1 - y)]


class _Gather:
    def __init__(self, names, small=(), middle_at=0.5):
        self.middle_at = middle_at
        self.names = tuple(names)
        self.nb = len(self.names)
        self.n = self.nb + len(small)
        self.n_sems = 8 * self.nb + 3 * len(small)
        self.n_flush = 6 * self.nb + len(small)
        self.out_shape = [jax.ShapeDtypeStruct((N_CHIP,) + _BIG_SHARD[nm], BF16) for nm in self.names]
        self.out_shape += [jax.ShapeDtypeStruct((N_CHIP,) + s.shape, s.dtype) for s in small]
        self.in_space = [self._in_spec(nm) for nm in self.names] + [VMEM_WHOLE] * len(small)

    @staticmethod
    def _in_spec(name):
        if name in ("w_down_a", "w_down_b"):
            half = 0 if name == "w_down_a" else 1
            return pl.BlockSpec(_BIG_SHARD[name], lambda *_: (half, 0), pipeline_mode=pl.Buffered(1))
        return VMEM_WHOLE

    def _copies(self, ins, outs, ss, rs, k):
        x, y, c, _ = _mesh_place()
        name = self.names[k]
        me, xn, yn, dg = 2 * x + y, 2 * (1 - x) + y, 2 * x + (1 - y), 2 * (1 - x) + (1 - y)
        to_x, to_y, sibling = (1 - x, y, c), (x, 1 - y, c), (x, y, 1 - c)

        def region(slot, e, part=None):
            return _half(outs[k].at[slot], e, name, part=part)

        def copy(pair, src, dst, to):
            return _remote(src, dst, ss, rs, 8 * k + pair, to)

        mine = _half(ins[k], c, name) if ins[k].dtype == BF16 else region(me, c)
        sent = [copy(0, mine, region(me, c), to_x), copy(1, mine, region(me, c), to_y),
                copy(2, region(xn, c, 0), region(xn, c, 0), to_y), copy(3, region(yn, c, 1), region(yn, c, 1), to_x),
                copy(4, region(xn, c), region(xn, c), sibling), copy(5, region(yn, c), region(yn, c), sibling),
                copy(6, region(dg, c, 0), region(dg, c, 0), sibling), copy(7, region(dg, c, 1), region(dg, c, 1), sibling)]
        landing = [region(xn, c), region(yn, c), region(dg, c, 0), region(dg, c, 1),
                   region(xn, 1 - c), region(yn, 1 - c), region(dg, 1 - c, 0), region(dg, 1 - c, 1)]
        received = [copy(pair, dst, dst, sibling) for pair, dst in enumerate(landing)]
        return sent, received

    def _small(self, ins, outs, ss, rs, k, j, peer, slot, c):
        return _remote(ins[k], outs[k].at[slot], ss, rs, 8 * self.nb + 3 * (k - self.nb) + j, (*peer, c))

    def _leaving(self, lands, outs, fs):
        x, y, c, _ = _mesh_place()
        me, xn, yn, dg = 2 * x + y, 2 * (1 - x) + y, 2 * x + (1 - y), 2 * (1 - x) + (1 - y)

        def pieces(k):
            name = self.names[k]
            spots = [lambda r: r.at[me], lambda r: _half(r.at[xn], c, name), lambda r: _half(r.at[yn], c, name),
                     lambda r: _half(r.at[xn], 1 - c, name), lambda r: _half(r.at[yn], 1 - c, name), lambda r: r.at[dg]]
            return [pltpu.make_async_copy(spot(lands[k]), spot(outs[k]), fs.at[6 * k + t]) for t, spot in enumerate(spots)]

        small = [pltpu.make_async_copy(lands[k], outs[k], fs.at[6 * self.nb + k - self.nb]) for k in range(self.nb, self.n)]
        return [pieces(k) for k in range(self.nb)], small

    def flush(self, phase, lands, outs, fs, wait=True):
        big, small = self._leaving(lands, outs, fs)
        ready = {"first": (0,), "middle": (1, 2), "late": (3, 4), "last": (5,)}[phase]
        for cps in big:
            for t in ready:
                cps[t].start()
        if phase == "last":
            for cp in small:
                cp.start()
            if wait:
                self.flush_done(lands, outs, fs)

    def flush_done(self, lands, outs, fs):
        big, small = self._leaving(lands, outs, fs)
        for cp in [cp for cps in big for cp in cps] + small:
            cp.wait()

    def first(self, ins, outs, ss, rs):
        x, y, c, peers = _mesh_place()
        me = 2 * x + y
        for k in range(self.nb):
            ready = ins[k].dtype == BF16
            sent, _ = self._copies(ins, outs, ss, rs, k)
            if ready:
                sent[0].start()
                sent[1].start()
            outs[k][me] = ins[k][...].astype(BF16)
            if not ready:
                sent[0].start()
                sent[1].start()
        for k in range(self.nb, self.n):
            for j, peer in enumerate(peers):
                self._small(ins, outs, ss, rs, k, j, peer, me, c).start()
            outs[k][me] = ins[k][...]

    def middle(self, ins, outs, ss, rs):
        for k in range(self.nb):
            sent, received = self._copies(ins, outs, ss, rs, k)
            for pair in (0, 1):
                received[pair].wait_recv()
                sent[2 + pair].start()
                sent[4 + pair].start()

    def late(self, ins, outs, ss, rs):
        for k in range(self.nb):
            _, received = self._copies(ins, outs, ss, rs, k)
            for pair in (4, 5):
                received[pair].wait_recv()

    def last(self, ins, outs, ss, rs):
        x, y, c, peers = _mesh_place()
        for k in range(self.nb):
            sent, received = self._copies(ins, outs, ss, rs, k)
            for pair in (2, 3):
                received[pair].wait_recv()
                sent[4 + pair].start()
        for k in range(self.nb):
            sent, received = self._copies(ins, outs, ss, rs, k)
            for pair in (6, 7):
                received[pair].wait_recv()
            for cp in sent:
                cp.wait_send()
        for k in range(self.nb, self.n):
            for j, (px, py) in enumerate(peers):
                self._small(ins, outs, ss, rs, k, j, (px, py), 2 * px + py, c).wait_recv()
                self._small(ins, outs, ss, rs, k, j, (px, py), 2 * x + y, c).wait_send()


def _run_alone(rider, ins, name):
    r_in, r_out_specs, r_out_shape, r_scratch = _rider_specs(rider, ins)

    def body(*refs):
        ride_in, ride_out, scratch = _split(refs, len(r_in), len(r_out_specs), len(r_scratch))
        _ride(rider, ("first", "middle", "last"), pl.program_id(0) == 0, ride_in, ride_out, scratch)

    return pl.pallas_call(
        body, grid=(1,), in_specs=r_in, out_specs=r_out_specs, out_shape=r_out_shape, scratch_shapes=r_scratch,
        compiler_params=_params(("arbitrary",)), name=name,
    )(*ins)


class _Presum:
    in_space = ANY

    def __init__(self, names, base=0):
        self.names = tuple(names)
        self.n = len(self.names)
        self.base = base
        self.n_sems = 3 * self.n
        self.out_shape = [jax.ShapeDtypeStruct((N_CHIP,) + _half_shape(nm), BF16) for nm in self.names]
        self.work_shape = self.out_shape + self.out_shape

    def _stage(self, ins, bufs, ss, k, e, which):
        n = self.n
        return pltpu.make_async_copy(_half(ins[k], e, self.names[k], lead=1), bufs[which * n + k],
                                     ss.at[self.base + which * n + k])

    def _give(self, bufs, ss, rs, k, sibling):
        return _remote(bufs[self.n + k], bufs[k], ss, rs, self.base + k, sibling)

    def first(self, ins, bufs, ss, rs):
        x, y, c, _ = _mesh_place()
        for k in range(self.n):
            self._stage(ins, bufs, ss, k, 1 - c, 1).start()
        for k in range(self.n):
            self._stage(ins, bufs, ss, k, c, 2).start()
        for k in range(self.n):
            self._stage(ins, bufs, ss, k, 1 - c, 1).wait()
            self._give(bufs, ss, rs, k, (x, y, 1 - c)).start()

    def middle(self, ins, bufs, ss, rs):
        pass

    def last(self, ins, bufs, ss, rs):
        x, y, c, _ = _mesh_place()
        for k in range(self.n):
            self._give(bufs, ss, rs, k, (x, y, 1 - c)).wait_recv()
            self._stage(ins, bufs, ss, k, c, 2).wait()

            @pl.loop(0, N_CHIP)
            def _(j):
                bufs[k][j] = (bufs[k][j].astype(F32) + bufs[2 * self.n + k][j].astype(F32)).astype(BF16)
        for k in range(self.n):
            self._give(bufs, ss, rs, k, (x, y, 1 - c)).wait_send()


class _ReduceRelay:
    middle_at = 0.75

    def __init__(self, names, base=0):
        self.names = tuple(names)
        self.n = len(self.names)
        self.base = base
        self.n_sems = 6 * self.n
        self.out_shape = [jax.ShapeDtypeStruct((N_CHIP,) + _half_shape(nm), BF16) for nm in self.names]
        quarter = [jax.ShapeDtypeStruct(self._part_shape(nm), BF16) for nm in self.names]
        self.work_shape = quarter + quarter

    @staticmethod
    def _part_shape(name):
        r, c = _half_shape(name)
        return (r // 2, c) if _BIG_SPLIT[name] == 0 else (r, c // 2)

    def _part(self, ref, name, p):
        r, c = self._part_shape(name)
        return ref.at[pl.ds(p * r, r), pl.ds(0, c)] if _BIG_SPLIT[name] == 0 else ref.at[pl.ds(0, r), pl.ds(p * c, c)]

    def _copies(self, ins, bufs, ss, rs, k):
        x, y, c, _ = _mesh_place()
        name, n = self.names[k], self.n
        me, xn, yn, dg = 2 * x + y, 2 * (1 - x) + y, 2 * x + (1 - y), 2 * (1 - x) + (1 - y)
        to_x, to_y = (1 - x, y, c), (x, 1 - y, c)
        mine = lambda slot, p: self._part(ins[k].at[slot], name, p)
        slot = lambda s, p: self._part(bufs[k].at[s], name, p)
        from_x, from_y = bufs[n + k], bufs[2 * n + k]

        def copy(pair, src, dst, to):
            return _remote(src, dst, ss, rs, self.base + 6 * k + pair, to)

        sent = [copy(0, mine(dg, 0), from_x, to_x), copy(1, mine(dg, 1), from_y, to_y),
                copy(2, mine(xn, 0), slot(me, 0), to_x), copy(3, mine(yn, 1), slot(me, 1), to_y),
                copy(4, from_y, slot(me, 1), to_x), copy(5, from_x, slot(me, 0), to_y)]
        landing = [from_x, from_y, slot(xn, 0), slot(yn, 1), slot(xn, 1), slot(yn, 0)]
        received = [copy(pair, dst, dst, to_x) for pair, dst in enumerate(landing)]
        return sent, received

    def first(self, ins, bufs, ss, rs):
        x, y, c, _ = _mesh_place()
        me, dg = 2 * x + y, 2 * (1 - x) + (1 - y)
        for k in range(self.n):
            sent, _ = self._copies(ins, bufs, ss, rs, k)
            for pair in range(4):
                sent[pair].start()
        for k in range(self.n):
            bufs[k][me] = ins[k][me]
            bufs[k][dg] = jnp.zeros(_half_shape(self.names[k]), BF16)

    def middle(self, ins, bufs, ss, rs):
        x, y, c, _ = _mesh_place()
        xn, yn = 2 * (1 - x) + y, 2 * x + (1 - y)
        for k in range(self.n):
            sent, received = self._copies(ins, bufs, ss, rs, k)
            name, n = self.names[k], self.n
            for pair, buf, own in ((0, bufs[n + k], self._part(ins[k].at[yn], name, 0)),
                                   (1, bufs[2 * n + k], self._part(ins[k].at[xn], name, 1))):
                received[pair].wait_recv()
                buf[...] = (buf[...].astype(F32) + own[...].astype(F32)).astype(BF16)
            sent[5].start()
            sent[4].start()

    def last(self, ins, bufs, ss, rs):
        for k in range(self.n):
            sent, received = self._copies(ins, bufs, ss, rs, k)
            for pair in range(2, 6):
                received[pair].wait_recv()
            for cp in sent:
                cp.wait_send()


class _PresumThenRelay:
    in_space = ANY
    middle_at = _ReduceRelay.middle_at

    def __init__(self, names):
        self.relay = _ReduceRelay(names)
        self.pre = _Presum(names, base=self.relay.n_sems)
        self.n_sems = self.relay.n_sems + self.pre.n_sems
        self.out_shape = self.relay.out_shape
        self.work_shape = list(self.relay.work_shape) + list(self.pre.out_shape) + list(self.pre.work_shape)
        self.n_relay = len(self.relay.out_shape) + len(self.relay.work_shape)

    def first(self, ins, bufs, ss, rs):
        self.pre.first(ins, bufs[self.n_relay:], ss, rs)

    def early(self, ins, bufs, ss, rs):
        self.pre.last(ins, bufs[self.n_relay:], ss, rs)
        self.relay.first(bufs[self.n_relay:], bufs[:self.n_relay], ss, rs)

    def middle(self, ins, bufs, ss, rs):
        self.relay.middle(bufs[self.n_relay:], bufs[:self.n_relay], ss, rs)

    def last(self, ins, bufs, ss, rs):
        self.relay.last(bufs[self.n_relay:], bufs[:self.n_relay], ss, rs)


class _SendPartials:
    def __init__(self, names, small_shape=None):
        self.n = len(names)
        self.small = small_shape is not None
        self.n_sems = 3 * self.n + 7
        self.out_shape = [jax.ShapeDtypeStruct((N_CHIP,) + _half_shape(nm), BF16) for nm in names]
        if self.small:
            self.out_shape.append(jax.ShapeDtypeStruct((N_DEV,) + small_shape, F32))

    def _piece(self, ins, outs, ss, rs, k, j, peer, src_slot, dst_slot, c):
        return _remote(ins[k].at[src_slot], outs[k].at[dst_slot], ss, rs, 3 * k + j, (*peer, c))

    def _small(self, ins, outs, ss, rs, r, other, slot):
        return _remote(ins[self.n], outs[self.n].at[slot], ss, rs, 3 * self.n + r, other)

    @staticmethod
    def _others(x, y, c):
        return [(x, y, 1 - c), (1 - x, y, c), (1 - x, y, 1 - c), (x, 1 - y, c), (x, 1 - y, 1 - c),
                (1 - x, 1 - y, c), (1 - x, 1 - y, 1 - c)]

    def first(self, ins, outs, ss, rs, only=None):
        x, y, c, peers = _mesh_place()
        me = 2 * x + y
        which = range(self.n) if only is None else only
        for k in which:
            for j, (px, py) in enumerate(peers):
                self._piece(ins, outs, ss, rs, k, j, (px, py), 2 * px + py, me, c).start()
        if self.small:
            for r, other in enumerate(self._others(x, y, c)):
                self._small(ins, outs, ss, rs, r, other, 4 * x + 2 * y + c).start()
            outs[self.n][4 * x + 2 * y + c] = ins[self.n][...]
        for k in which:
            outs[k][me] = ins[k][me]

    def middle(self, ins, outs, ss, rs):
        pass

    def last(self, ins, outs, ss, rs):
        x, y, c, peers = _mesh_place()
        me = 2 * x + y
        for k in range(self.n):
            for j, (px, py) in enumerate(peers):
                self._piece(ins, outs, ss, rs, k, j, (px, py), me, 2 * px + py, c).wait_recv()
                self._piece(ins, outs, ss, rs, k, j, (px, py), 2 * px + py, me, c).wait_send()
        if self.small:
            for r, (px, py, pc) in enumerate(self._others(x, y, c)):
                self._small(ins, outs, ss, rs, r, (px, py, pc), 4 * px + 2 * py + pc).wait_recv()
                self._small(ins, outs, ss, rs, r, (px, py, pc), 4 * x + 2 * y + c).wait_send()


class _PresumThenSend:
    def __init__(self, names):
        self.send = _SendPartials(names)
        self.pre = _Presum(names[-1:], base=self.send.n_sems)
        self.n = self.send.n
        self.n_sems = self.send.n_sems + self.pre.n_sems
        self.out_shape = self.send.out_shape
        self.work_shape = list(self.pre.out_shape) + list(self.pre.work_shape)
        self.in_space = [VMEM_WHOLE] * (self.n - 1) + [ANY]

    def _partials(self, ins, bufs):
        return list(ins[:self.n - 1]) + [bufs[self.n]]

    def first(self, ins, bufs, ss, rs):
        self.pre.first(ins[self.n - 1:], bufs[self.n:], ss, rs)
        self.send.first(ins, bufs[:self.n], ss, rs, only=range(self.n - 1))

    def early(self, ins, bufs, ss, rs):
        self.pre.last(ins[self.n - 1:], bufs[self.n:], ss, rs)
        self.send.first(self._partials(ins, bufs), bufs[:self.n], ss, rs, only=(self.n - 1,))

    def middle(self, ins, bufs, ss, rs):
        pass

    def last(self, ins, bufs, ss, rs):
        self.send.last(self._partials(ins, bufs), bufs[:self.n], ss, rs)


def _sum_swap(names, parts, small):
    n = len(parts)
    everyone = _SendPartials((), small.shape)

    def body(*refs):
        (p_hbm, (small_ref,), o_hbm, (osmall_ref,), p_refs, o_refs, (all_ref,),
         (send_sems, recv_sems, ss_small, rs_small, load_sems, leave_sems)) = _split(refs, n, 1, n, 1, n, n, 1, 6)
        x, y, c = lax.axis_index("x"), lax.axis_index("y"), lax.axis_index("c")
        loads = [pltpu.make_async_copy(p_hbm[k], p_refs[k], load_sems.at[k]) for k in range(n)]
        for cp in loads:
            cp.start()
        everyone.first([small_ref], [all_ref], ss_small, rs_small)

        def mine(k):
            part = _half(o_refs[k], c, names[k])
            return _remote(part, part, send_sems, recv_sems, k, (x, y, 1 - c))

        def leave(k, whose):
            e = c if whose == 0 else 1 - c
            return pltpu.make_async_copy(_half(o_refs[k], e, names[k]), _half(o_hbm[k], e, names[k]),
                                         leave_sems.at[2 * k + whose])

        for k in range(n):
            loads[k].wait()
            for e in range(2):
                @pl.when(c == e)
                def _():
                    g = p_refs[k][0].astype(F32)
                    for s in range(1, N_CHIP):
                        g = g + p_refs[k][s].astype(F32)
                    r, cols = _half_shape(names[k])
                    if _BIG_SPLIT[names[k]] == 0:
                        o_refs[k][e * r:(e + 1) * r, :] = g
                    else:
                        o_refs[k][:, e * cols:(e + 1) * cols] = g
            mine(k).start()
            leave(k, 0).start()
        for k in range(n):
            theirs = _half(o_refs[k], 1 - c, names[k])
            _remote(theirs, theirs, send_sems, recv_sems, k, (x, y, 1 - c)).wait_recv()
            leave(k, 1).start()
        everyone.last([small_ref], [all_ref], ss_small, rs_small)
        g = all_ref[0]
        for d in range(1, N_DEV):
            g = g + all_ref[d]
        osmall_ref[...] = g
        for k in range(n):
            mine(k).wait_send()
            leave(k, 0).wait()
            leave(k, 1).wait()

    shards = [jax.ShapeDtypeStruct(_BIG_SHARD[nm], F32) for nm in names]
    res = pl.pallas_call(
        body, in_specs=[ANY] * n + [VMEM_WHOLE], out_specs=[ANY] * n + [VMEM_WHOLE],
        out_shape=shards + [jax.ShapeDtypeStruct(small.shape, F32)],
        scratch_shapes=[pltpu.VMEM(q.shape, q.dtype) for q in parts] + [pltpu.VMEM(s.shape, s.dtype) for s in shards]
        + [pltpu.VMEM((N_DEV,) + small.shape, F32), pltpu.SemaphoreType.DMA((n,)), pltpu.SemaphoreType.DMA((n,)),
           pltpu.SemaphoreType.DMA((everyone.n_sems,)), pltpu.SemaphoreType.DMA((everyone.n_sems,)),
           pltpu.SemaphoreType.DMA((n,)), pltpu.SemaphoreType.DMA((2 * n,))],
        compiler_params=_params(), name="sum_swap",
    )(*parts, small)
    return res[:n], res[n]


def _adamw_math(w, g, m, v):
    m = ADAM_B1 * m + (1.0 - ADAM_B1) * g
    v = ADAM_B2 * v + (1.0 - ADAM_B2) * (g * g)
    m_hat = m / (1.0 - ADAM_B1 ** ADAM_STEP)
    v_hat = v / (1.0 - ADAM_B2 ** ADAM_STEP)
    delta = -ADAM_LR * (m_hat / (jnp.sqrt(v_hat) + ADAM_EPS) + ADAM_WD * w)
    return delta, m, v


ADAMW_STEPS = 8


def _adamw_big(gs, ws, ms, vs, name):
    n = len(ws)

    def body(*refs):
        for k in range(n):
            g_ref, w_ref, m_ref, v_ref = refs[4 * k:4 * k + 4]
            g_out_ref, d_ref, nm_ref, nv_ref = refs[4 * (n + k):4 * (n + k) + 4]
            g = g_ref[...].reshape(w_ref.shape)
            g_out_ref[...] = g
            d_ref[...], nm_ref[...], nv_ref[...] = _adamw_math(w_ref[...], g, m_ref[...], v_ref[...])

    in_specs, out_specs, shapes, args = [], [], [], []
    for g, w, m, v in zip(gs, ws, ms, vs):
        tr = -(-w.shape[0] // (8 * ADAMW_STEPS)) * 8
        zeros = (0,) * (w.ndim - 1)
        blk = pl.BlockSpec((tr,) + w.shape[1:], lambda i, zeros=zeros: (i,) + zeros)
        in_specs += [pl.BlockSpec((tr,) + g.shape[1:], lambda i: (i, 0)), blk, blk, blk]
        out_specs += [blk] * 4
        shapes += [jax.ShapeDtypeStruct(w.shape, F32)] * 4
        args += [g, w, m, v]
    res = pl.pallas_call(
        body, grid=(ADAMW_STEPS,), in_specs=in_specs, out_specs=out_specs, out_shape=shapes,
        compiler_params=_params(("arbitrary",)), name=name,
    )(*args)
    return [tuple(res[4 * k:4 * k + 4]) for k in range(n)]


def _adamw_small(ws, gs, ms, vs):
    n = len(ws)

    def body(*refs):
        w_refs, g_refs, m_refs, v_refs, d_refs, nm_refs, nv_refs = _split(refs, *([n] * 7))
        for k in range(n):
            d_refs[k][...], nm_refs[k][...], nv_refs[k][...] = _adamw_math(w_refs[k][...], g_refs[k][...], m_refs[k][...],
                                                                             v_refs[k][...])

    shapes = [jax.ShapeDtypeStruct(w.shape, F32) for w in ws]
    res = pl.pallas_call(body, out_shape=shapes * 3, name="adamw_small")(*ws, *gs, *ms, *vs)
    return res[:n], res[n:2 * n], res[2 * n:]


def _pack(arrs):
    flat = jnp.concatenate([a.reshape(-1) for a in arrs])
    rows = -(-flat.shape[0] // 1024) * 8
    return jnp.pad(flat, (0, rows * 128 - flat.shape[0])).reshape(rows, 128)


def _unpack(buf, shapes):
    flat = buf.reshape(-1)
    out, off = [], 0
    for s in shapes:
        size = 1
        for d in s:
            size *= d
        out.append(flat[off:off + size].reshape(s))
        off += size
    return out


def _block_rows(w):
    return jnp.pad(w.reshape(512, 4), ((0, 0), (0, 124)))


def _block_stored(dw):
    return jnp.transpose(dw[:, 0:4].reshape(128, 4, 4), (1, 2, 0)).reshape(16, 128)


def _cols(a4):
    return jnp.transpose(a4, (1, 0, 2)).reshape(a4.shape[1], -1)


_LATE = ("w_pa", "w_pb", "w_o", "w_up", "w_down")
_RIDE_IN_PROJ = ("w_pa", "w_pb", "w_o", "w_down_b")
_RIDE_MIXER = ("w_up", "w_down_a")


def _full_weights(gathered):
    joined = {"w_o": (D_MODEL, D_MODEL)}
    return {n: (a.reshape(joined[n]) if n in joined else a) for n, a in gathered.items()}


def _local_step(x, target, w, sp, late_shards=None):
    sp = {n: (a.reshape(1, -1) if a.ndim == 1 else a) for n, a in sp.items()}
    wau = jnp.pad(sp["w_a_up"], ((0, 112), (0, 0)))
    wif = jnp.pad(sp["w_if"], ((0, 0), (0, 120)))
    bif = jnp.pad(sp["b_if"], ((0, 0), (0, 120)))
    p = {"wau": wau, "bau": sp["b_a_up"], "ggla": sp["g_gla_norm"], "cw": sp["conv_w"], "cb": sp["conv_b"],
         "wq": _block_rows(sp["w_q_ml"]), "wk": _block_rows(sp["w_k_ml"]), "wv": _block_rows(sp["w_v_ml"]),
         "wif": wif, "bif": bif, "skip": sp["ml_skip"], "gml": sp["g_ml_norm"]}

    if late_shards is None:
        (pm, gab, h), _ = _in_proj(x, sp["g_pre_mix"], w["w_in"])
        ab, x1, mix, merged, *states = _mixer_fwd(pm, p, gab, x, w["w_pa"], w["w_pb"], w["w_o"], sp["g_post_mix"])
    else:
        shard = dict(zip(_LATE, late_shards))
        shard["w_down_a"] = shard["w_down_b"] = shard["w_down"]
        (pm, gab, h), got = _in_proj(x, sp["g_pre_mix"], w["w_in"], _Gather(_RIDE_IN_PROJ, middle_at=0.7),
                                     [shard[n] for n in _RIDE_IN_PROJ])
        w = dict(w, **_full_weights(dict(zip(_RIDE_IN_PROJ, got))))
        ab, x1, mix, merged, *rest = _mixer_fwd(pm, p, gab, x, w["w_pa"], w["w_pb"], w["w_o"], sp["g_post_mix"],
                                                _Gather(_RIDE_MIXER, middle_at=0.62), [shard[n] for n in _RIDE_MIXER])
        states = rest[:4]
        w.update(_full_weights(dict(zip(_RIDE_MIXER, rest[4:]))))
    dx1, u, dd, h2, dpre, dg_post_mlp, dg_pre_mlp, loss = _mlp(x1, target, sp["g_pre_mlp"], sp["g_post_mlp"],
                                                                w["w_up"], w["w_down_a"], w["w_down_b"])
    dgab, dab, dg_post_mix, dw_pa, dw_pb, dw_o = _merge_bwd(dx1, mix, ab, gab, merged, w["w_pa"], w["w_pb"], w["w_o"],
                                                            sp["g_post_mix"])
    big = {"w_pa": dw_pa, "w_pb": dw_pb, "w_o": dw_o, "w_up": _tn_matmul(h2, dpre, "dw_up", shards=N_CHIP)}
    if late_shards is None:
        big["w_down"] = _tn_matmul(u, dd, "dw_down")
        dpm, dp, _ = _mixer_bwd(pm, dab, states, p)
    else:
        pieces = lambda n: big[n].reshape((N_CHIP,) + _BIG_SHARD[n])
        big["w_down"], partial = _tn_matmul(u, dd, "dw_down", rider=_Presum(_LATE[:4]),
                                            rider_ins=[pieces(n) for n in _LATE[:4]])
        dpm, dp, parts = _mixer_bwd(pm, dab, states, p, _PresumThenSend(_LATE), list(partial) + [pieces("w_down")])
        big = dict(zip(_LATE, parts))
    big["w_in"] = _dw_in(dpm, dgab, h)
    if late_shards is None:
        (dx, dg_pre_mix), _ = _in_proj_bwd(dpm, dgab, x, dx1, sp["g_pre_mix"], w["w_in"])
    else:
        (dx, dg_pre_mix), parts = _in_proj_bwd(dpm, dgab, x, dx1, sp["g_pre_mix"], w["w_in"], _PresumThenRelay(("w_in",)),
                                               [big["w_in"]])
        big["w_in"] = parts[0]
    small = {
        "g_pre_mix": dg_pre_mix, "b_a_up": dp["bau"], "g_gla_norm": dp["ggla"], "conv_b": dp["cb"],
        "w_q_ml": _block_stored(dp["wq"]), "w_k_ml": _block_stored(dp["wk"]), "w_v_ml": _block_stored(dp["wv"]),
        "w_if": dp["wif"][:, 0:8].T,
        "b_if": dp["bif"][:, 0:8], "ml_skip": dp["skip"], "g_ml_norm": dp["gml"], "g_post_mix": dg_post_mix,
        "g_pre_mlp": dg_pre_mlp, "g_post_mlp": dg_post_mlp, "w_a_up": dp["wau"][0:16], "conv_w": dp["cw"],
        "loss": loss[:, 0:1],
    }
    return dx, big, small


_SMALL_REPL = ("g_pre_mix", "b_a_up", "g_gla_norm", "conv_b", "w_q_ml", "w_k_ml", "w_v_ml", "b_if", "ml_skip",
               "g_ml_norm", "g_post_mix", "g_pre_mlp", "g_post_mlp")
_SMALL_SHARDED = ("w_a_up", "conv_w", "w_if")
_SMALL_ORDER = _SMALL_REPL + _SMALL_SHARDED + ("loss",)
_WEIGHTS = ("g_pre_mix", "w_in", "w_a_up", "b_a_up", "g_gla_norm", "conv_w", "conv_b", "w_q_ml", "w_k_ml", "w_v_ml",
            "w_if", "b_if", "ml_skip", "g_ml_norm", "w_pa", "w_pb", "w_o", "g_post_mix", "g_pre_mlp", "w_up", "w_down",
            "g_post_mlp")


_BLOCK_WEIGHTS = ("w_q_ml", "w_k_ml", "w_v_ml")


def _stored(name, a):
    if name in _BLOCK_WEIGHTS:
        return jnp.transpose(a, (0, 2, 3, 1)).reshape(16, 128)
    if name == "w_if":
        return jnp.transpose(a, (0, 2, 1)).reshape(8, 384)
    return a


def _unstored(name, a):
    if name in _BLOCK_WEIGHTS:
        return jnp.transpose(a.reshape(1, 4, 4, 128), (0, 3, 1, 2))
    if name == "w_if":
        return jnp.transpose(a.reshape(1, 8, 384), (0, 2, 1))
    return a


def _as_shard(name, a):
    return jnp.transpose(a, (2, 0, 1)).reshape(IN_SHARD, D_MODEL // 128, 128) if name == "w_in" else a[0]


def _in_shard_bf16(w_in):
    return jnp.transpose(w_in.astype(BF16), (2, 0, 1)).reshape(IN_SHARD, D_MODEL)


def _from_shard(name, a):
    return jnp.transpose(a, (1, 2, 0)).reshape(1, D_MODEL, IN_SHARD) if name == "w_in" else a[None]


def kernel(x, g_pre_mix, w_in, w_a_up, b_a_up, g_gla_norm, conv_w, conv_b, w_q_ml, w_k_ml, w_v_ml, w_if, b_if, ml_skip, g_ml_norm, w_pa, w_pb, w_o, g_post_mix, g_pre_mlp, w_up, w_down, g_post_mlp, loss_target, m_g_pre_mix, m_w_in, m_w_a_up, m_b_a_up, m_g_gla_norm, m_conv_w, m_conv_b, m_w_q_ml, m_w_k_ml, m_w_v_ml, m_w_if, m_b_if, m_ml_skip, m_g_ml_norm, m_w_pa, m_w_pb, m_w_o, m_g_post_mix, m_g_pre_mlp, m_w_up, m_w_down, m_g_post_mlp, v_g_pre_mix, v_w_in, v_w_a_up, v_b_a_up, v_g_gla_norm, v_conv_w, v_conv_b, v_w_q_ml, v_w_k_ml, v_w_v_ml, v_w_if, v_b_if, v_ml_skip, v_g_ml_norm, v_w_pa, v_w_pb, v_w_o, v_g_post_mix, v_g_pre_mlp, v_w_up, v_w_down, v_g_post_mlp):
    args = dict(locals())
    wts = {n: _as_shard(n, args[n]) for n in _WEIGHTS}
    mom = {n: _as_shard(n, args["m_" + n]) for n in _WEIGHTS}
    var = {n: _as_shard(n, args["v_" + n]) for n in _WEIGHTS}
    chip = 2 * lax.axis_index("x") + lax.axis_index("y")

    first = ("w_in",) + _SMALL_SHARDED
    gathered = dict(zip(first, _run_alone(_Gather(("w_in",), [wts[n] for n in _SMALL_SHARDED]),
                                          [_in_shard_bf16(w_in)] + [wts[n] for n in _SMALL_SHARDED],
                                          "gather_first")))
    sp = {n: wts[n] for n in _SMALL_REPL}
    sp["w_a_up"] = _cols(gathered["w_a_up"])
    sp["conv_w"] = _cols(gathered["conv_w"])
    sp["w_if"] = gathered["w_if"].reshape(1536, 8)

    dx, big, small = _local_step(x[0], loss_target[0], _full_weights({"w_in": gathered["w_in"]}), sp,
                                 late_shards=[wts[n] for n in _LATE])

    small_shapes = [small[n].shape for n in _SMALL_ORDER]
    packed = _pack([small[n] for n in _SMALL_ORDER])
    sums, small_sum = _sum_swap(_BIG, [big[n] for n in _BIG], packed)

    grads, delta, new_m, new_v = {}, {}, {}, {}
    updated = dict(zip(_BIG, _adamw_big(sums, [wts[n] for n in _BIG], [mom[n] for n in _BIG], [var[n] for n in _BIG],
                                        "adamw_big")))
    for n in _BIG:
        grads[n], delta[n], new_m[n], new_v[n] = (_from_shard(n, a) for a in updated[n])
    summed = dict(zip(_SMALL_ORDER, _unpack(small_sum, small_shapes)))
    loss = summed["loss"].reshape(())
    summed["w_a_up"] = lax.dynamic_slice_in_dim(summed["w_a_up"], chip * 64, 64, axis=1)
    summed["conv_w"] = lax.dynamic_slice_in_dim(summed["conv_w"], chip * 128, 128, axis=1)
    summed["w_if"] = lax.dynamic_slice_in_dim(summed["w_if"], chip * 384, 384, axis=1)
    small_names = _SMALL_REPL + _SMALL_SHARDED
    came_stored = _BLOCK_WEIGHTS + ("w_if",)
    g_stored = [summed[n] if n in came_stored else _stored(n, summed[n].reshape(args[n].shape)) for n in small_names]
    upd = _adamw_small([_stored(n, args[n]) for n in small_names], g_stored,
                       [_stored(n, args["m_" + n]) for n in small_names], [_stored(n, args["v_" + n]) for n in small_names])
    for dst, arrs in zip((grads, delta, new_m, new_v), (g_stored,) + tuple(upd)):
        dst.update({n: _unstored(n, a) for n, a in zip(small_names, arrs)})

    outs = [loss, dx[None]]
    for group in (grads, delta, new_m, new_v):
        outs += [group[n] for n in _WEIGHTS]
    return tuple(outs)
```
